```python
import jax, jax.numpy as jnp
from jax import lax
import numpy as np

D_MODEL = 1024
BATCH = 8
SEQ = 4096
DEPTH = 1

CHUNK = 64
Q_BLOCK = 128
LN_EPS = 1e-5
RMS_EPS = 1e-6

MLA_HEADS = 8
MLA_NOPE = 64
MLA_ROPE = 32
MLA_V = 64
MLA_QK = MLA_NOPE + MLA_ROPE
MLA_Q_RANK = 256
MLA_KV_RANK = 128
MLA_WIDTH = MLA_HEADS * MLA_V
ROPE_THETA = 10000.0

RWKV_HEADS = 8
RWKV_HEAD = 64
RWKV_WIDTH = RWKV_HEADS * RWKV_HEAD
DECAY_LORA = 64
ICLR_LORA = 64
RWKV_SHIFT_WIDTH = 3 * RWKV_WIDTH + DECAY_LORA + ICLR_LORA
GN_EPS = 64e-5

ALPHA = (2.0 * DEPTH) ** 0.25
BETA = (8.0 * DEPTH) ** -0.25

IN_SPLITS = (MLA_Q_RANK, MLA_KV_RANK, MLA_ROPE, MLA_WIDTH,
             RWKV_SHIFT_WIDTH, RWKV_WIDTH,
             D_MODEL, D_MODEL)
IN_WIDTH = sum(IN_SPLITS)

kernel_name = 'hybrid_mla_rwkv7_gated_deepnorm_block'


def _split(t, sizes):
    return jnp.split(t, [int(i) for i in np.cumsum(sizes)[:-1]], axis=-1)


def layer_norm(x):
    x = x.astype(jnp.float32)
    xc = x - jnp.mean(x, -1, keepdims=True)
    return xc * lax.rsqrt(jnp.mean(xc * xc, -1, keepdims=True) + LN_EPS)


def rms_norm(x, g):
    x32 = x.astype(jnp.float32)
    y = x32 * lax.rsqrt(jnp.mean(x32 * x32, -1, keepdims=True) + RMS_EPS)
    return (y * g).astype(x.dtype)


def rope_tables(positions):
    inv = ROPE_THETA ** (-jnp.arange(0, MLA_ROPE, 2, dtype=jnp.float32) / MLA_ROPE)
    ang = positions.astype(jnp.float32)[..., None] * inv
    return jnp.cos(ang)[:, :, None, :], jnp.sin(ang)[:, :, None, :]


def apply_rope(t, cos, sin):
    t1, t2 = jnp.split(t.astype(jnp.float32), 2, axis=-1)
    return jnp.concatenate([t1 * cos - t2 * sin, t1 * sin + t2 * cos], -1).astype(t.dtype)


def token_shift(u, mu):
    u_prev = jnp.pad(u, ((0, 0), (1, 0), (0, 0)))[:, :-1]
    return u + (u_prev - u) * mu


def chunk_causal_attention(q, k, v):
    B, S, H, Dk = q.shape
    nb = S // Q_BLOCK
    qb = q.reshape(B, nb, Q_BLOCK, H, Dk).transpose(1, 0, 2, 3, 4)
    key_chunk = jnp.arange(S) // CHUNK
    scale = Dk ** -0.5

    def block(args):
        qi, bi = args
        s = jnp.einsum('bqhd,bkhd->bhqk', qi, k).astype(jnp.float32) * scale
        q_chunk = (bi * Q_BLOCK + jnp.arange(Q_BLOCK)) // CHUNK
        mask = key_chunk[None, :] <= q_chunk[:, None]
        p = jax.nn.softmax(jnp.where(mask, s, -jnp.inf), axis=-1).astype(v.dtype)
        return jnp.einsum('bhqk,bkhd->bqhd', p, v)

    out = lax.map(block, (qb, jnp.arange(nb)))
    return out.transpose(1, 0, 2, 3, 4).reshape(B, S, H, v.shape[-1])


def wkv7(r, w, k, v, a, b):
    B, S, H, N = r.shape

    def step(state, inp):
        r_t, w_t, k_t, v_t, a_t, b_t = inp
        sa = jnp.einsum('bhij,bhj->bhi', state, a_t)
        state = (state * w_t[:, :, None, :] + sa[..., None] * b_t[:, :, None, :]
                 + v_t[..., None] * k_t[:, :, None, :])
        return state, jnp.einsum('bhij,bhj->bhi', state, r_t)

    xs = tuple(jnp.moveaxis(t.astype(jnp.float32), 1, 0) for t in (r, w, k, v, a, b))
    _, y = lax.scan(step, jnp.zeros((B, H, N, N), jnp.float32), xs)
    return jnp.moveaxis(y, 0, 1)


def mla_branch(q_c, kv_c, k_rope, cos, sin, q_norm_g, w_uq, kv_norm_g, w_ukv):
    B, S, _ = q_c.shape
    q = (rms_norm(q_c, q_norm_g) @ w_uq).reshape(B, S, MLA_HEADS, MLA_QK)
    kv = (rms_norm(kv_c, kv_norm_g) @ w_ukv).reshape(B, S, MLA_HEADS, MLA_NOPE + MLA_V)
    q_nope, q_pe = jnp.split(q, [MLA_NOPE], axis=-1)
    k_nope, v = jnp.split(kv, [MLA_NOPE], axis=-1)
    q_pe = apply_rope(q_pe, cos, sin)
    k_pe = apply_rope(k_rope[:, :, None, :], cos, sin)
    q = jnp.concatenate([q_nope, q_pe], -1)
    k = jnp.concatenate([k_nope, jnp.broadcast_to(k_pe, (B, S, MLA_HEADS, MLA_ROPE))], -1)
    return chunk_causal_attention(q, k, v).reshape(B, S, MLA_WIDTH)


def rwkv7_branch(u, w0, w_decay_up, a0, w_iclr_up, k_k, k_a, r_k, gn_g, gn_b):
    B, S, _ = u.shape
    r, k, v, wd, ad = _split(u, (RWKV_WIDTH, RWKV_WIDTH, RWKV_WIDTH, DECAY_LORA, ICLR_LORA))
    w_log = -jax.nn.softplus(-(w0 + jnp.tanh(wd) @ w_decay_up)) - 0.5
    decay = jnp.exp(-jnp.exp(w_log.astype(jnp.float32)))
    a = jax.nn.sigmoid(a0 + ad @ w_iclr_up)
    hs = lambda t: t.reshape(B, S, RWKV_HEADS, RWKV_HEAD)
    kk = hs(k * k_k).astype(jnp.float32)
    kk = kk / jnp.maximum(jnp.sqrt(jnp.sum(kk * kk, -1, keepdims=True)), 1e-12)
    k = k * (1 + (a - 1) * k_a)
    r_h, k_h, v_h, a_h = hs(r), hs(k), hs(v), hs(a)
    y = wkv7(r_h, hs(decay), k_h, v_h, -kk, kk * a_h)
    y = y - jnp.mean(y, -1, keepdims=True)
    y = y * lax.rsqrt(jnp.mean(y * y, -1, keepdims=True) + GN_EPS)
    y = y.reshape(B, S, RWKV_WIDTH) * gn_g + gn_b
    bonus = jnp.sum((r_h * k_h * r_k).astype(jnp.float32), -1, keepdims=True) * v_h
    return (y + bonus.reshape(B, S, RWKV_WIDTH)).astype(u.dtype)


def hybrid_layer(x, c, cos, sin, w_ada, b_ada, w_in, q_norm_g, w_uq, kv_norm_g, w_ukv,
                 mu_rwkv, w0, w_decay_up, a0, w_iclr_up, k_k, k_a, r_k, gn_g, gn_b,
                 w_proj_a, w_proj_b, w_out, post_g, post_b):
    dt = x.dtype
    shift, scale, gate = jnp.split(jax.nn.silu(c) @ w_ada + b_ada, 3, axis=-1)
    h = (layer_norm(x) * (1 + scale[:, None]) + shift[:, None]).astype(dt)
    proj = h @ w_in
    q_c, kv_c, k_rope, gpath_a, rwkv_in, gpath_b, merge_a, merge_b = _split(proj, IN_SPLITS)
    y_a = mla_branch(q_c, kv_c, k_rope, cos, sin, q_norm_g, w_uq, kv_norm_g, w_ukv)
    y_b = rwkv7_branch(token_shift(rwkv_in, mu_rwkv), w0, w_decay_up, a0, w_iclr_up,
                       k_k, k_a, r_k, gn_g, gn_b)
    y_a = (y_a * jax.nn.silu(gpath_a)) @ w_proj_a
    y_b = (y_b * jax.nn.silu(gpath_b)) @ w_proj_b
    merged = jax.nn.sigmoid(merge_a) * y_a + jax.nn.sigmoid(merge_b) * y_b
    sub = merged @ w_out
    out = layer_norm(ALPHA * x + (1 + gate[:, None]) * sub) * post_g + post_b
    return out.astype(dt)


def _fwd_setup_inputs(seed: int = 0) -> dict:
    key = jax.random.key(seed)
    k = jax.random.split(key, 32)

    def nrm(i, shape, fan_in, gain=1.0):
        return jax.random.normal(k[i], (DEPTH,) + shape, jnp.float32) * (gain * fan_in ** -0.5)

    def near(i, shape, center, spread=0.02):
        return center + spread * jax.random.normal(k[i], (DEPTH,) + shape, jnp.float32)

    x = jax.random.normal(k[0], (BATCH, SEQ, D_MODEL), jnp.float32)
    c = jax.random.normal(k[1], (BATCH, D_MODEL), jnp.float32)
    positions = (jax.random.randint(k[2], (BATCH, 1), 0, 8192, dtype=jnp.int32)
                 + jnp.arange(SEQ, dtype=jnp.int32)[None, :])
    decay_base = -6.0 + 5.0 * jnp.linspace(0.0, 1.0, RWKV_WIDTH, dtype=jnp.float32) ** 0.9
    return {
        'x': x,
        'c': c,
        'positions': positions,
        'w_ada': nrm(3, (D_MODEL, 3 * D_MODEL), D_MODEL, 0.2),
        'b_ada': near(4, (3 * D_MODEL,), 0.0),
        'w_in': nrm(5, (D_MODEL, IN_WIDTH), D_MODEL),
        'q_norm_g': near(6, (MLA_Q_RANK,), 1.0),
        'w_uq': nrm(7, (MLA_Q_RANK, MLA_HEADS * MLA_QK), MLA_Q_RANK),
        'kv_norm_g': near(8, (MLA_KV_RANK,), 1.0),
        'w_ukv': nrm(9, (MLA_KV_RANK, MLA_HEADS * (MLA_NOPE + MLA_V)), MLA_KV_RANK),
        'mu_rwkv': jax.random.uniform(k[10], (DEPTH, RWKV_SHIFT_WIDTH), jnp.float32),
        'w0': decay_base + near(11, (RWKV_WIDTH,), 0.0, 0.1),
        'w_decay_up': nrm(12, (DECAY_LORA, RWKV_WIDTH), DECAY_LORA),
        'a0': near(13, (RWKV_WIDTH,), 0.0, 0.1),
        'w_iclr_up': nrm(14, (ICLR_LORA, RWKV_WIDTH), ICLR_LORA),
        'k_k': near(15, (RWKV_WIDTH,), 0.85),
        'k_a': near(16, (RWKV_WIDTH,), 1.0),
        'r_k': near(17, (RWKV_HEADS, RWKV_HEAD), 0.0, 0.1),
        'gn_g': near(18, (RWKV_WIDTH,), 1.0),
        'gn_b': near(19, (RWKV_WIDTH,), 0.0),
        'w_proj_a': nrm(20, (MLA_WIDTH, D_MODEL), MLA_WIDTH, BETA),
        'w_proj_b': nrm(21, (RWKV_WIDTH, D_MODEL), RWKV_WIDTH, BETA),
        'w_out': nrm(22, (D_MODEL, D_MODEL), D_MODEL, BETA),
        'post_g': near(23, (D_MODEL,), 1.0),
        'post_b': near(24, (D_MODEL,), 0.0),
    }


def _fwd_reference(x, c, positions, w_ada, b_ada, w_in, q_norm_g, w_uq, kv_norm_g, w_ukv,
              mu_rwkv, w0, w_decay_up, a0, w_iclr_up, k_k, k_a, r_k, gn_g, gn_b,
              w_proj_a, w_proj_b, w_out, post_g, post_b):
    cos, sin = rope_tables(positions)
    for l in range(DEPTH):
        x = hybrid_layer(x, c, cos, sin, w_ada[l], b_ada[l], w_in[l], q_norm_g[l], w_uq[l],
                         kv_norm_g[l], w_ukv[l], mu_rwkv[l], w0[l], w_decay_up[l], a0[l],
                         w_iclr_up[l], k_k[l], k_a[l], r_k[l], gn_g[l], gn_b[l],
                         w_proj_a[l], w_proj_b[l], w_out[l], post_g[l], post_b[l])
    return x


import jax as _jax
import jax.numpy as _jnp

TWIN_FORMAT = 'train_step'
FWD_PARAMS = ['x', 'c', 'positions', 'w_ada', 'b_ada', 'w_in', 'q_norm_g', 'w_uq', 'kv_norm_g', 'w_ukv', 'mu_rwkv', 'w0', 'w_decay_up', 'a0', 'w_iclr_up', 'k_k', 'k_a', 'r_k', 'gn_g', 'gn_b', 'w_proj_a', 'w_proj_b', 'w_out', 'post_g', 'post_b']
TWIN_WEIGHTS = ['w_ada', 'b_ada', 'w_in', 'q_norm_g', 'w_uq', 'kv_norm_g', 'w_ukv', 'mu_rwkv', 'w0', 'w_decay_up', 'a0', 'w_iclr_up', 'k_k', 'k_a', 'r_k', 'gn_g', 'gn_b', 'w_proj_a', 'w_proj_b', 'w_out', 'post_g', 'post_b']
TWIN_DIFF_INPUT = 'x'
TWIN_INPUTS = ['x', 'c', 'positions', 'w_ada', 'b_ada', 'w_in', 'q_norm_g', 'w_uq', 'kv_norm_g', 'w_ukv', 'mu_rwkv', 'w0', 'w_decay_up', 'a0', 'w_iclr_up', 'k_k', 'k_a', 'r_k', 'gn_g', 'gn_b', 'w_proj_a', 'w_proj_b', 'w_out', 'post_g', 'post_b', 'loss_target', 'm_w_ada', 'm_b_ada', 'm_w_in', 'm_q_norm_g', 'm_w_uq', 'm_kv_norm_g', 'm_w_ukv', 'm_mu_rwkv', 'm_w0', 'm_w_decay_up', 'm_a0', 'm_w_iclr_up', 'm_k_k', 'm_k_a', 'm_r_k', 'm_gn_g', 'm_gn_b', 'm_w_proj_a', 'm_w_proj_b', 'm_w_out', 'm_post_g', 'm_post_b', 'v_w_ada', 'v_b_ada', 'v_w_in', 'v_q_norm_g', 'v_w_uq', 'v_kv_norm_g', 'v_w_ukv', 'v_mu_rwkv', 'v_w0', 'v_w_decay_up', 'v_a0', 'v_w_iclr_up', 'v_k_k', 'v_k_a', 'v_r_k', 'v_gn_g', 'v_gn_b', 'v_w_proj_a', 'v_w_proj_b', 'v_w_out', 'v_post_g', 'v_post_b']
TWIN_OUTPUTS = ['loss', 'grad_x', 'grad_w_ada', 'grad_b_ada', 'grad_w_in', 'grad_q_norm_g', 'grad_w_uq', 'grad_kv_norm_g', 'grad_w_ukv', 'grad_mu_rwkv', 'grad_w0', 'grad_w_decay_up', 'grad_a0', 'grad_w_iclr_up', 'grad_k_k', 'grad_k_a', 'grad_r_k', 'grad_gn_g', 'grad_gn_b', 'grad_w_proj_a', 'grad_w_proj_b', 'grad_w_out', 'grad_post_g', 'grad_post_b', 'delta_w_ada', 'delta_b_ada', 'delta_w_in', 'delta_q_norm_g', 'delta_w_uq', 'delta_kv_norm_g', 'delta_w_ukv', 'delta_mu_rwkv', 'delta_w0', 'delta_w_decay_up', 'delta_a0', 'delta_w_iclr_up', 'delta_k_k', 'delta_k_a', 'delta_r_k', 'delta_gn_g', 'delta_gn_b', 'delta_w_proj_a', 'delta_w_proj_b', 'delta_w_out', 'delta_post_g', 'delta_post_b', 'new_m_w_ada', 'new_m_b_ada', 'new_m_w_in', 'new_m_q_norm_g', 'new_m_w_uq', 'new_m_kv_norm_g', 'new_m_w_ukv', 'new_m_mu_rwkv', 'new_m_w0', 'new_m_w_decay_up', 'new_m_a0', 'new_m_w_iclr_up', 'new_m_k_k', 'new_m_k_a', 'new_m_r_k', 'new_m_gn_g', 'new_m_gn_b', 'new_m_w_proj_a', 'new_m_w_proj_b', 'new_m_w_out', 'new_m_post_g', 'new_m_post_b', 'new_v_w_ada', 'new_v_b_ada', 'new_v_w_in', 'new_v_q_norm_g', 'new_v_w_uq', 'new_v_kv_norm_g', 'new_v_w_ukv', 'new_v_mu_rwkv', 'new_v_w0', 'new_v_w_decay_up', 'new_v_a0', 'new_v_w_iclr_up', 'new_v_k_k', 'new_v_k_a', 'new_v_r_k', 'new_v_gn_g', 'new_v_gn_b', 'new_v_w_proj_a', 'new_v_w_proj_b', 'new_v_w_out', 'new_v_post_g', 'new_v_post_b']
TWIN_LEAF_KINDS = {'loss': 'loss', 'grad_x': 'grad_x', 'grad_w_ada': 'grad_w', 'grad_b_ada': 'grad_w', 'grad_w_in': 'grad_w', 'grad_q_norm_g': 'grad_w', 'grad_w_uq': 'grad_w', 'grad_kv_norm_g': 'grad_w', 'grad_w_ukv': 'grad_w', 'grad_mu_rwkv': 'grad_w', 'grad_w0': 'grad_w', 'grad_w_decay_up': 'grad_w', 'grad_a0': 'grad_w', 'grad_w_iclr_up': 'grad_w', 'grad_k_k': 'grad_w', 'grad_k_a': 'grad_w', 'grad_r_k': 'grad_w', 'grad_gn_g': 'grad_w', 'grad_gn_b': 'grad_w', 'grad_w_proj_a': 'grad_w', 'grad_w_proj_b': 'grad_w', 'grad_w_out': 'grad_w', 'grad_post_g': 'grad_w', 'grad_post_b': 'grad_w', 'delta_w_ada': 'delta_w', 'delta_b_ada': 'delta_w', 'delta_w_in': 'delta_w', 'delta_q_norm_g': 'delta_w', 'delta_w_uq': 'delta_w', 'delta_kv_norm_g': 'delta_w', 'delta_w_ukv': 'delta_w', 'delta_mu_rwkv': 'delta_w', 'delta_w0': 'delta_w', 'delta_w_decay_up': 'delta_w', 'delta_a0': 'delta_w', 'delta_w_iclr_up': 'delta_w', 'delta_k_k': 'delta_w', 'delta_k_a': 'delta_w', 'delta_r_k': 'delta_w', 'delta_gn_g': 'delta_w', 'delta_gn_b': 'delta_w', 'delta_w_proj_a': 'delta_w', 'delta_w_proj_b': 'delta_w', 'delta_w_out': 'delta_w', 'delta_post_g': 'delta_w', 'delta_post_b': 'delta_w', 'new_m_w_ada': 'new_m', 'new_m_b_ada': 'new_m', 'new_m_w_in': 'new_m', 'new_m_q_norm_g': 'new_m', 'new_m_w_uq': 'new_m', 'new_m_kv_norm_g': 'new_m', 'new_m_w_ukv': 'new_m', 'new_m_mu_rwkv': 'new_m', 'new_m_w0': 'new_m', 'new_m_w_decay_up': 'new_m', 'new_m_a0': 'new_m', 'new_m_w_iclr_up': 'new_m', 'new_m_k_k': 'new_m', 'new_m_k_a': 'new_m', 'new_m_r_k': 'new_m', 'new_m_gn_g': 'new_m', 'new_m_gn_b': 'new_m', 'new_m_w_proj_a': 'new_m', 'new_m_w_proj_b': 'new_m', 'new_m_w_out': 'new_m', 'new_m_post_g': 'new_m', 'new_m_post_b': 'new_m', 'new_v_w_ada': 'new_v', 'new_v_b_ada': 'new_v', 'new_v_w_in': 'new_v', 'new_v_q_norm_g': 'new_v', 'new_v_w_uq': 'new_v', 'new_v_kv_norm_g': 'new_v', 'new_v_w_ukv': 'new_v', 'new_v_mu_rwkv': 'new_v', 'new_v_w0': 'new_v', 'new_v_w_decay_up': 'new_v', 'new_v_a0': 'new_v', 'new_v_w_iclr_up': 'new_v', 'new_v_k_k': 'new_v', 'new_v_k_a': 'new_v', 'new_v_r_k': 'new_v', 'new_v_gn_g': 'new_v', 'new_v_gn_b': 'new_v', 'new_v_w_proj_a': 'new_v', 'new_v_w_proj_b': 'new_v', 'new_v_w_out': 'new_v', 'new_v_post_g': 'new_v', 'new_v_post_b': 'new_v'}


def _forward(args):
    return _fwd_reference(*[args[k] for k in FWD_PARAMS])


def _output_shape():
    out = _jax.eval_shape(lambda: _forward(_fwd_setup_inputs(0)))
    return out.shape, out.dtype

N_MICROBATCH = 1
ADAM_LR = 0.001
ADAM_B1 = 0.9
ADAM_B2 = 0.999
ADAM_EPS = 1e-08
ADAM_WD = 0.01
ADAM_STEP = 10
PER_EXAMPLE_BATCH_AXIS = {'x': 0, 'c': 0, 'positions': 0, 'loss_target': 0}
SHARED_INPUTS = []
_WEIGHT_DTYPES = {'w_ada': _jnp.float32, 'b_ada': _jnp.float32, 'w_in': _jnp.float32, 'q_norm_g': _jnp.float32, 'w_uq': _jnp.float32, 'kv_norm_g': _jnp.float32, 'w_ukv': _jnp.float32, 'mu_rwkv': _jnp.float32, 'w0': _jnp.float32, 'w_decay_up': _jnp.float32, 'a0': _jnp.float32, 'w_iclr_up': _jnp.float32, 'k_k': _jnp.float32, 'k_a': _jnp.float32, 'r_k': _jnp.float32, 'gn_g': _jnp.float32, 'gn_b': _jnp.float32, 'w_proj_a': _jnp.float32, 'w_proj_b': _jnp.float32, 'w_out': _jnp.float32, 'post_g': _jnp.float32, 'post_b': _jnp.float32}
MOMENT_SCALE = {'w_ada': 2.390567e-02, 'b_ada': 4.087651e-02, 'w_in': 1.772786e-02, 'q_norm_g': 5.249208e-03, 'w_uq': 3.156483e-03, 'kv_norm_g': 1.243998e-02, 'w_ukv': 4.200936e-03, 'mu_rwkv': 4.415131e-02, 'w0': 1.075199e-02, 'w_decay_up': 2.308710e-03, 'a0': 1.066162e-02, 'w_iclr_up': 9.162571e-03, 'k_k': 2.064347e-02, 'k_a': 2.952903e-02, 'r_k': 5.956229e-02, 'gn_g': 2.607188e-02, 'gn_b': 2.832800e-02, 'w_proj_a': 5.848641e-03, 'w_proj_b': 3.018656e-02, 'w_out': 3.067423e-02, 'post_g': 3.194954e+01, 'post_b': 4.892174e-01}


def _to_microbatches(a, axis):
    t = _jnp.moveaxis(a, axis, 0)
    t = t.reshape((N_MICROBATCH, t.shape[0] // N_MICROBATCH) + t.shape[1:])
    return _jnp.moveaxis(t, 1, axis + 1)


def setup_inputs(seed: int = 0) -> dict:
    inp = _fwd_setup_inputs(seed)
    key = _jax.random.fold_in(_jax.random.key(seed), 7919)
    shape, _ = _output_shape()
    out = dict(inp)
    out["loss_target"] = _jax.random.normal(_jax.random.fold_in(key, 0), shape, _jnp.float32)
    for i, name in enumerate(TWIN_WEIGHTS):
        w = inp[name].astype(_jnp.float32)
        if MOMENT_SCALE is None:
            s = _jnp.sqrt(_jnp.mean(_jnp.square(w)) + 1e-30)
        else:
            s = MOMENT_SCALE[name]
        km, kv = _jax.random.split(_jax.random.fold_in(key, i + 1))
        out[name] = w
        out["m_" + name] = s * _jax.random.normal(km, w.shape, _jnp.float32)
        out["v_" + name] = (s * s) * _jax.random.uniform(kv, w.shape, _jnp.float32, 0.5, 1.5)
    if N_MICROBATCH > 1:
        for name, axis in PER_EXAMPLE_BATCH_AXIS.items():
            out[name] = _to_microbatches(out[name], axis)
    return {'x': out['x'], 'c': out['c'], 'positions': out['positions'], 'w_ada': out['w_ada'], 'b_ada': out['b_ada'], 'w_in': out['w_in'], 'q_norm_g': out['q_norm_g'], 'w_uq': out['w_uq'], 'kv_norm_g': out['kv_norm_g'], 'w_ukv': out['w_ukv'], 'mu_rwkv': out['mu_rwkv'], 'w0': out['w0'], 'w_decay_up': out['w_decay_up'], 'a0': out['a0'], 'w_iclr_up': out['w_iclr_up'], 'k_k': out['k_k'], 'k_a': out['k_a'], 'r_k': out['r_k'], 'gn_g': out['gn_g'], 'gn_b': out['gn_b'], 'w_proj_a': out['w_proj_a'], 'w_proj_b': out['w_proj_b'], 'w_out': out['w_out'], 'post_g': out['post_g'], 'post_b': out['post_b'], 'loss_target': out['loss_target'], 'm_w_ada': out['m_w_ada'], 'm_b_ada': out['m_b_ada'], 'm_w_in': out['m_w_in'], 'm_q_norm_g': out['m_q_norm_g'], 'm_w_uq': out['m_w_uq'], 'm_kv_norm_g': out['m_kv_norm_g'], 'm_w_ukv': out['m_w_ukv'], 'm_mu_rwkv': out['m_mu_rwkv'], 'm_w0': out['m_w0'], 'm_w_decay_up': out['m_w_decay_up'], 'm_a0': out['m_a0'], 'm_w_iclr_up': out['m_w_iclr_up'], 'm_k_k': out['m_k_k'], 'm_k_a': out['m_k_a'], 'm_r_k': out['m_r_k'], 'm_gn_g': out['m_gn_g'], 'm_gn_b': out['m_gn_b'], 'm_w_proj_a': out['m_w_proj_a'], 'm_w_proj_b': out['m_w_proj_b'], 'm_w_out': out['m_w_out'], 'm_post_g': out['m_post_g'], 'm_post_b': out['m_post_b'], 'v_w_ada': out['v_w_ada'], 'v_b_ada': out['v_b_ada'], 'v_w_in': out['v_w_in'], 'v_q_norm_g': out['v_q_norm_g'], 'v_w_uq': out['v_w_uq'], 'v_kv_norm_g': out['v_kv_norm_g'], 'v_w_ukv': out['v_w_ukv'], 'v_mu_rwkv': out['v_mu_rwkv'], 'v_w0': out['v_w0'], 'v_w_decay_up': out['v_w_decay_up'], 'v_a0': out['v_a0'], 'v_w_iclr_up': out['v_w_iclr_up'], 'v_k_k': out['v_k_k'], 'v_k_a': out['v_k_a'], 'v_r_k': out['v_r_k'], 'v_gn_g': out['v_gn_g'], 'v_gn_b': out['v_gn_b'], 'v_w_proj_a': out['v_w_proj_a'], 'v_w_proj_b': out['v_w_proj_b'], 'v_w_out': out['v_w_out'], 'v_post_g': out['v_post_g'], 'v_post_b': out['v_post_b']}


def _loss(weights, diff, rest, loss_target):
    with _jax.named_scope("forward"):
        args = {**rest, TWIN_DIFF_INPUT: diff, **{k: w.astype(_WEIGHT_DTYPES[k]) for k, w in weights.items()}}
        y = _forward(args)
    with _jax.named_scope("loss_head"):
        err = _jnp.square(y.astype(_jnp.float32) - loss_target)
        return 0.5 * _jnp.sum(_jnp.mean(err, axis=-1)) if err.ndim else 0.5 * err


def _adamw(w, g, m, v):
    m = ADAM_B1 * m + (1.0 - ADAM_B1) * g
    v = ADAM_B2 * v + (1.0 - ADAM_B2) * _jnp.square(g)
    m_hat = m / (1.0 - ADAM_B1 ** ADAM_STEP)
    v_hat = v / (1.0 - ADAM_B2 ** ADAM_STEP)
    delta = -ADAM_LR * (m_hat / (_jnp.sqrt(v_hat) + ADAM_EPS) + ADAM_WD * w)
    return delta, m, v


def reference(x, c, positions, w_ada, b_ada, w_in, q_norm_g, w_uq, kv_norm_g, w_ukv, mu_rwkv, w0, w_decay_up, a0, w_iclr_up, k_k, k_a, r_k, gn_g, gn_b, w_proj_a, w_proj_b, w_out, post_g, post_b, loss_target, m_w_ada, m_b_ada, m_w_in, m_q_norm_g, m_w_uq, m_kv_norm_g, m_w_ukv, m_mu_rwkv, m_w0, m_w_decay_up, m_a0, m_w_iclr_up, m_k_k, m_k_a, m_r_k, m_gn_g, m_gn_b, m_w_proj_a, m_w_proj_b, m_w_out, m_post_g, m_post_b, v_w_ada, v_b_ada, v_w_in, v_q_norm_g, v_w_uq, v_kv_norm_g, v_w_ukv, v_mu_rwkv, v_w0, v_w_decay_up, v_a0, v_w_iclr_up, v_k_k, v_k_a, v_r_k, v_gn_g, v_gn_b, v_w_proj_a, v_w_proj_b, v_w_out, v_post_g, v_post_b):
    given = dict(x=x, c=c, positions=positions, w_ada=w_ada, b_ada=b_ada, w_in=w_in, q_norm_g=q_norm_g, w_uq=w_uq, kv_norm_g=kv_norm_g, w_ukv=w_ukv, mu_rwkv=mu_rwkv, w0=w0, w_decay_up=w_decay_up, a0=a0, w_iclr_up=w_iclr_up, k_k=k_k, k_a=k_a, r_k=r_k, gn_g=gn_g, gn_b=gn_b, w_proj_a=w_proj_a, w_proj_b=w_proj_b, w_out=w_out, post_g=post_g, post_b=post_b, loss_target=loss_target, m_w_ada=m_w_ada, m_b_ada=m_b_ada, m_w_in=m_w_in, m_q_norm_g=m_q_norm_g, m_w_uq=m_w_uq, m_kv_norm_g=m_kv_norm_g, m_w_ukv=m_w_ukv, m_mu_rwkv=m_mu_rwkv, m_w0=m_w0, m_w_decay_up=m_w_decay_up, m_a0=m_a0, m_w_iclr_up=m_w_iclr_up, m_k_k=m_k_k, m_k_a=m_k_a, m_r_k=m_r_k, m_gn_g=m_gn_g, m_gn_b=m_gn_b, m_w_proj_a=m_w_proj_a, m_w_proj_b=m_w_proj_b, m_w_out=m_w_out, m_post_g=m_post_g, m_post_b=m_post_b, v_w_ada=v_w_ada, v_b_ada=v_b_ada, v_w_in=v_w_in, v_q_norm_g=v_q_norm_g, v_w_uq=v_w_uq, v_kv_norm_g=v_kv_norm_g, v_w_ukv=v_w_ukv, v_mu_rwkv=v_mu_rwkv, v_w0=v_w0, v_w_decay_up=v_w_decay_up, v_a0=v_a0, v_w_iclr_up=v_w_iclr_up, v_k_k=v_k_k, v_k_a=v_k_a, v_r_k=v_r_k, v_gn_g=v_gn_g, v_gn_b=v_gn_b, v_w_proj_a=v_w_proj_a, v_w_proj_b=v_w_proj_b, v_w_out=v_w_out, v_post_g=v_post_g, v_post_b=v_post_b)
    weights = {n: given[n] for n in TWIN_WEIGHTS}
    shared = {n: given[n] for n in SHARED_INPUTS}
    per_example = {n: given[n] for n in ['x', 'c', 'positions']}
    grad_fn = _jax.value_and_grad(_loss, argnums=(0, 1))

    def one_microbatch(ex, loss_target):
        ex = dict(ex)
        diff = ex.pop(TWIN_DIFF_INPUT)
        return grad_fn(weights, diff, {**shared, **ex}, loss_target)

    if N_MICROBATCH == 1:
        loss, (grad_w, grad_x) = one_microbatch(per_example, given["loss_target"])
    else:
        def body(carry, xs):
            loss_sum, grad_sum = carry
            l_k, (gw_k, gx_k) = one_microbatch(xs[0], xs[1])
            with _jax.named_scope("update"):
                return (loss_sum + l_k, _jax.tree.map(_jnp.add, grad_sum, gw_k)), gx_k

        init = (_jnp.zeros((), _jnp.float32), _jax.tree.map(_jnp.zeros_like, weights))
        (loss, grad_w), grad_x = _jax.lax.scan(body, init, (per_example, given["loss_target"]))
    with _jax.named_scope("update"):
        delta_w, new_m, new_v = {}, {}, {}
        for n in TWIN_WEIGHTS:
            delta_w[n], new_m[n], new_v[n] = _adamw(weights[n], grad_w[n], given["m_" + n], given["v_" + n])
    return (loss, grad_x, *[grad_w[n] for n in TWIN_WEIGHTS], *[delta_w[n] for n in TWIN_WEIGHTS],
            *[new_m[n] for n in TWIN_WEIGHTS], *[new_v[n] for n in TWIN_WEIGHTS])
```

```python
import functools
import math

import jax
import jax.numpy as jnp
from jax import lax
from jax.experimental import pallas as pl
from jax.experimental.pallas import tpu as pltpu

F32 = jnp.float32
BF16 = jnp.bfloat16
HIGHEST = lax.Precision.HIGHEST
MESH_IDS = pl.DeviceIdType.MESH

N_DEV = 8
D_MODEL = 1024
LN_EPS = 1e-5
RMS_EPS = 1e-6
GN_EPS = 64e-5
HEADS = 8
Q_RANK = 256
KV_RANK = 128
ROPE = 32
NOPE = 64
QK_DIM = NOPE + ROPE
WIDTH = 512
HEAD = 64
LORA = 64
CHUNK = 64
DEPTH = 1
ALPHA = (2.0 * DEPTH) ** 0.25
ROPE_THETA = 10000.0
ATTN_SCALE = QK_DIM ** -0.5
DECAY_SCALE = math.exp(-0.5)

ADAM_LR = 0.001
ADAM_B1 = 0.9
ADAM_B2 = 0.999
ADAM_EPS = 1e-08
ADAM_WD = 0.01
ADAM_STEP = 10

LANE = 128
PAIR = 2 * HEAD
ROW_TILE = 256
ATTN_TILE = 256
WKV_CHUNKS_PER_STEP = 4
VMEM_LIMIT = 56 * 1024 * 1024

P_MA, P_MB, P_R, P_K, P_V, P_GPA, P_GPB, P_QC, P_KVC, P_KR, P_KRR, P_LORA = (
    0, 1024, 2048, 2560, 3072, 3584, 4096, 4608, 4864, 4992, 5120, 5248)
P_WIDTH = 5376
DW_BLOCK = 768

N_QC, N_KVC, N_KROPE, N_GPA, N_RWKV, N_GPB, N_MA, N_MB = 0, 256, 384, 416, 928, 2592, 3104, 4128
IN_WIDTH = 5152

SHARDED = (("w_in", 1024, 644), ("w_uq", 256, 96), ("w_ukv", 128, 128), ("w_decay_up", 64, 64),
           ("w_iclr_up", 64, 64), ("w_proj_a", 512, 128), ("w_proj_b", 512, 128), ("w_out", 128, 1024))
SHARD_ELEMS = sum(r * c for _, r, c in SHARDED)
SHARD_ROWS = SHARD_ELEMS // LANE
GATHER_ROWS = SHARD_ROWS + 2 * D_MODEL // LANE
SMALL = (("b_ada", 3072), ("q_norm_g", 256), ("kv_norm_g", 128), ("mu_rwkv", 1664), ("w0", 512), ("a0", 512),
         ("k_k", 512), ("k_a", 512), ("r_k", 512), ("gn_g", 512), ("gn_b", 512), ("post_g", 1024), ("post_b", 1024))
SMALL_ELEMS = sum(n for _, n in SMALL)
SMALL_ROWS = SMALL_ELEMS // LANE


def mm(a, b):
    return jnp.dot(a.astype(BF16), b.astype(BF16), preferred_element_type=F32)


def mm_nt(a, b):
    return lax.dot_general(a.astype(BF16), b.astype(BF16), (((1,), (1,)), ((), ())), preferred_element_type=F32)


def mm_tn(a, b):
    return lax.dot_general(a.astype(BF16), b.astype(BF16), (((0,), (0,)), ((), ())), preferred_element_type=F32)


def hdot(a, b):
    return jnp.dot(a, b, precision=HIGHEST, preferred_element_type=F32)


def hdot_nt(a, b):
    return lax.dot_general(a, b, (((1,), (1,)), ((), ())), precision=HIGHEST, preferred_element_type=F32)


def hdot_tn(a, b):
    return lax.dot_general(a, b, (((0,), (0,)), ((), ())), precision=HIGHEST, preferred_element_type=F32)


def sigmoid(x):
    return 1.0 / (1.0 + jnp.exp(-x))


def colsum(x):
    return jnp.sum(x, axis=0, keepdims=True)


def rowmean(x):
    return jnp.mean(x, axis=-1, keepdims=True)


def layer_norm_stats(x):
    xc = x - rowmean(x)
    rstd = lax.rsqrt(rowmean(xc * xc) + LN_EPS)
    return xc * rstd, rstd


def layer_norm_bwd(dy, xhat, rstd):
    return rstd * (dy - rowmean(dy) - xhat * rowmean(dy * xhat))


def head_sum(x, bd):
    return jnp.concatenate([hdot(x[:, p * LANE:(p + 1) * LANE], bd) for p in range(x.shape[1] // LANE)], axis=1)


def tile_lanes(t, n):
    return jnp.concatenate([t] * n, axis=1)


def row_iota(shape):
    return lax.broadcasted_iota(jnp.int32, shape, 0)


def lane_iota(shape):
    return lax.broadcasted_iota(jnp.int32, shape, 1)


def shift_rows_down(x, row0):
    rolled = pltpu.roll(x, 1, axis=0)
    return jnp.where(row_iota(x.shape) == 0, row0, rolled)


def shift_rows_up(x, row_last):
    rolled = pltpu.roll(x, x.shape[0] - 1, axis=0)
    return jnp.where(row_iota(x.shape) == x.shape[0] - 1, row_last, rolled)


def row_call(name, fn, n_rows, row_in, const_in, row_out, acc_out=(), halo_in=(), carry=(), reverse=False):
    ts = ROW_TILE
    n_tiles = n_rows // ts
    n_in = len(row_in) + len(halo_in) + len(const_in)
    n_ro, n_ao = len(row_out), len(acc_out)

    def tile_of(g):
        return (n_tiles - 1 - g) if reverse else g

    def body(*refs):
        ins = refs[:n_in]
        ro = refs[n_in:n_in + n_ro]
        ao = refs[n_in + n_ro:n_in + n_ro + n_ao]
        cr = refs[n_in + n_ro + n_ao:]
        g = pl.program_id(0)
        step0 = g == 0
        tile0 = tile_of(g) == 0
        for r in cr:
            @pl.when(step0)
            def _(r=r):
                r[...] = jnp.zeros_like(r)
        vals = [r[...] for r in ins]
        outs = fn(step0, tile0, *vals, *[c[0:1, :] for c in cr])
        for r, v in zip(ro, outs[:n_ro]):
            r[...] = v.astype(r.dtype)
        for r, v in zip(ao, outs[n_ro:n_ro + n_ao]):
            @pl.when(step0)
            def _(r=r, v=v):
                r[...] = v.astype(r.dtype)

            @pl.when(jnp.logical_not(step0))
            def _(r=r, v=v):
                r[...] += v.astype(r.dtype)
        for r, v in zip(cr, outs[n_ro + n_ao:]):
            r[0:1, :] = v

    in_specs = [pl.BlockSpec((ts, w), functools.partial(lambda g, cb: (tile_of(g), cb), cb=cb)) for _, w, cb in row_in]
    in_specs += [pl.BlockSpec((8, w), functools.partial(
        lambda g, cb: (jnp.maximum(tile_of(g) * (ts // 8) - 1, 0), cb), cb=cb)) for _, w, cb in halo_in]
    in_specs += [pl.BlockSpec(memory_space=pltpu.VMEM) for _ in const_in]
    out_specs = [pl.BlockSpec((ts, w), lambda g: (tile_of(g), 0)) for w, _ in row_out]
    out_specs += [pl.BlockSpec(s, lambda g: (0, 0)) for s, _ in acc_out]
    out_shape = [jax.ShapeDtypeStruct((n_rows, w), d) for w, d in row_out]
    out_shape += [jax.ShapeDtypeStruct(s, d) for s, d in acc_out]
    return pl.pallas_call(
        body, name=name, grid=(n_tiles,), in_specs=in_specs, out_specs=out_specs, out_shape=out_shape,
        scratch_shapes=[pltpu.VMEM((8, w), F32) for w in carry],
        compiler_params=pltpu.CompilerParams(dimension_semantics=("arbitrary",), vmem_limit_bytes=VMEM_LIMIT),
    )(*[a for a, _, _ in row_in], *[a for a, _, _ in halo_in], *const_in)


def my_position():
    return lax.axis_index("x"), lax.axis_index("y"), lax.axis_index("c")


def flip(pos, k):
    x, y, c = pos
    dx, dy, dc = (k >> 2) & 1, (k >> 1) & 1, k & 1
    return (1 - x if dx else x, 1 - y if dy else y, 1 - c if dc else c)


def flat_index(pos):
    return 4 * pos[0] + 2 * pos[1] + pos[2]


def gather_shards(shard):
    rows = shard.shape[0]

    def body(x_ref, out_ref, send_sems, recv_sems, local_sem):
        x, y, c = my_position()
        me, sibling = (x, y, c), (x, y, 1 - c)
        chips = [(1 - x, y), (x, 1 - y), (1 - x, 1 - y)]

        def slot(pos):
            return out_ref.at[flat_index(pos)]

        def copy(k, block, to, src=None):
            return pltpu.make_async_remote_copy(
                src_ref=slot(block) if src is None else src, dst_ref=slot(block),
                send_sem=send_sems.at[k], recv_sem=recv_sems.at[k], device_id=to, device_id_type=MESH_IDS)

        mine = pltpu.make_async_copy(x_ref, slot(me), local_sem)
        mine.start()
        first = [copy(0, me, sibling, src=x_ref)]
        first += [copy(1 + j, me, (*chip, c), src=x_ref) for j, chip in enumerate(chips)]
        for cp in first:
            cp.start()
        passed = [copy(4 + j, (*chip, c), sibling) for j, chip in enumerate(chips)]
        for j, chip in enumerate(chips):
            copy(1 + j, (*chip, c), me).wait_recv()
            passed[j].start()
        copy(0, sibling, me).wait_recv()
        for j, chip in enumerate(chips):
            copy(4 + j, (*chip, 1 - c), me).wait_recv()
        for cp in first + passed:
            cp.wait_send()
        mine.wait()

    return pl.pallas_call(
        body, name="gather_shards",
        out_shape=jax.ShapeDtypeStruct((N_DEV, rows, LANE), shard.dtype),
        in_specs=[pl.BlockSpec(memory_space=pl.ANY)], out_specs=pl.BlockSpec(memory_space=pl.ANY),
        scratch_shapes=[pltpu.SemaphoreType.DMA((7,)), pltpu.SemaphoreType.DMA((7,)), pltpu.SemaphoreType.DMA],
    )(shard)


def ada_modulation(c_all, w_ada_loc, b_ada_blocks):
    cols = w_ada_loc.shape[1]

    def body(c_ref, w_ref, b_ref, out_ref, send_sems, recv_sems):
        me = my_position()
        mi = flat_index(me)
        cv = c_ref[...]
        res = hdot(cv * sigmoid(cv), w_ref[...]) + b_ref[pl.ds(mi, 1), :]
        out_ref[mi] = res
        sends = []
        for k in range(1, N_DEV):
            cp = pltpu.make_async_remote_copy(
                src_ref=out_ref.at[mi], dst_ref=out_ref.at[mi], send_sem=send_sems.at[k - 1],
                recv_sem=recv_sems.at[k - 1], device_id=flip(me, k), device_id_type=MESH_IDS)
            cp.start()
            sends.append(cp)
        for k in range(1, N_DEV):
            pi = flat_index(flip(me, k))
            pltpu.make_async_remote_copy(
                src_ref=out_ref.at[pi], dst_ref=out_ref.at[pi], send_sem=send_sems.at[k - 1],
                recv_sem=recv_sems.at[k - 1], device_id=flip(me, k), device_id_type=MESH_IDS).wait_recv()
        for cp in sends:
            cp.wait_send()

    return pl.pallas_call(
        body, name="ada_modulation",
        out_shape=jax.ShapeDtypeStruct((N_DEV, N_DEV, cols), F32),
        in_specs=[pl.BlockSpec(memory_space=pltpu.VMEM)] * 3, out_specs=pl.BlockSpec(memory_space=pltpu.VMEM),
        scratch_shapes=[pltpu.SemaphoreType.DMA((7,)), pltpu.SemaphoreType.DMA((7,))],
    )(c_all, w_ada_loc, b_ada_blocks)


def exchange_grads(blocks, small):
    def body(g_ref, s_ref, rg_ref, rs_ref, send_sems, recv_sems, local_sems):
        me = my_position()
        mi = flat_index(me)
        loc_g = pltpu.make_async_copy(g_ref.at[mi], rg_ref.at[mi], local_sems.at[0])
        loc_s = pltpu.make_async_copy(s_ref, rs_ref.at[mi], local_sems.at[1])
        loc_g.start()
        loc_s.start()
        sends = []
        for k in range(1, N_DEV):
            peer = flip(me, k)
            pi = flat_index(peer)
            cg = pltpu.make_async_remote_copy(
                src_ref=g_ref.at[pi], dst_ref=rg_ref.at[mi], send_sem=send_sems.at[2 * (k - 1)],
                recv_sem=recv_sems.at[2 * (k - 1)], device_id=peer, device_id_type=MESH_IDS)
            cs = pltpu.make_async_remote_copy(
                src_ref=s_ref, dst_ref=rs_ref.at[mi], send_sem=send_sems.at[2 * (k - 1) + 1],
                recv_sem=recv_sems.at[2 * (k - 1) + 1], device_id=peer, device_id_type=MESH_IDS)
            cg.start()
            cs.start()
            sends += [cg, cs]
        for k in range(1, N_DEV):
            peer = flip(me, k)
            pi = flat_index(peer)
            pltpu.make_async_remote_copy(
                src_ref=g_ref.at[pi], dst_ref=rg_ref.at[pi], send_sem=send_sems.at[2 * (k - 1)],
                recv_sem=recv_sems.at[2 * (k - 1)], device_id=peer, device_id_type=MESH_IDS).wait_recv()
            pltpu.make_async_remote_copy(
                src_ref=s_ref, dst_ref=rs_ref.at[pi], send_sem=send_sems.at[2 * (k - 1) + 1],
                recv_sem=recv_sems.at[2 * (k - 1) + 1], device_id=peer, device_id_type=MESH_IDS).wait_recv()
        for cp in sends:
            cp.wait_send()
        loc_g.wait()
        loc_s.wait()

    return pl.pallas_call(
        body, name="exchange_grads",
        out_shape=(jax.ShapeDtypeStruct(blocks.shape, blocks.dtype),
                   jax.ShapeDtypeStruct((N_DEV,) + small.shape, small.dtype)),
        in_specs=[pl.BlockSpec(memory_space=pl.ANY)] * 2, out_specs=[pl.BlockSpec(memory_space=pl.ANY)] * 2,
        scratch_shapes=[pltpu.SemaphoreType.DMA((14,)), pltpu.SemaphoreType.DMA((14,)), pltpu.SemaphoreType.DMA((2,))],
    )(blocks, small)


def fwd_in_tile(step0, tile0, x, mod, w_in_p):
    xhat, _ = layer_norm_stats(x)
    h = xhat * (1.0 + mod[1:2]) + mod[0:1]
    return (mm(h, w_in_p),)


def rms_norm_fwd(x, g):
    r = lax.rsqrt(rowmean(x * x) + RMS_EPS)
    xh = x * r
    return xh * g, xh, r


def key_rope_mask(shape):
    return (lane_iota(shape) >= NOPE).astype(F32)


def mla_prep_tile(step0, tile0, q_c, kv_c, kr, krr, cos, sin, gq, gkv, wq, wqr, wkn, wv):
    qn, _, _ = rms_norm_fwd(q_c, gq)
    kvn, _, _ = rms_norm_fwd(kv_c, gkv)
    q = mm(qn, wq) * tile_lanes(cos, HEADS) + mm(qn, wqr) * tile_lanes(sin, HEADS)
    kpe = kr * (cos * key_rope_mask(cos.shape)) + krr * sin
    k = mm(kvn, wkn) + tile_lanes(kpe, HEADS)
    v = mm(kvn, wv)
    return q, k, v


def rwkv_prep_core(tile0, r0, k0, v0, l0, hr, hk, hv, hl, mu_r, mu_k, mu_v, mu_l, w0, a0, k_k, k_a,
                   w_dec, w_iclr, tril, same, bd):
    def shifted(x, halo, mu):
        row0 = jnp.where(tile0, 0.0, halo[7:8, :])
        prev = shift_rows_down(x, row0)
        return x + (prev - x) * mu, prev

    ur, pr = shifted(r0, hr, mu_r)
    uk, pk = shifted(k0, hk, mu_k)
    uv, pv = shifted(v0, hv, mu_v)
    ul, plo = shifted(l0, hl, mu_l)
    th = jnp.tanh(ul)
    sg = sigmoid(w0 + mm(th, w_dec))
    lw = -DECAY_SCALE * sg
    a_ic = sigmoid(a0 + mm(ul, w_iclr))
    kkraw = uk * k_k
    nrm_raw = jnp.sqrt(head_sum(kkraw * kkraw, bd))
    nrm = jnp.maximum(nrm_raw, 1e-12)
    kk = kkraw / nrm
    k2 = uk * (1.0 + (a_ic - 1.0) * k_a)
    lc = hdot(tril, lw)
    lcl = hdot(same, lw)
    return dict(ur=ur, uk=uk, uv=uv, ul=ul, pr=pr, pk=pk, pv=pv, pl=plo, th=th, sg=sg, lw=lw, a_ic=a_ic,
                kkraw=kkraw, nrm_raw=nrm_raw, nrm=nrm, kk=kk, k2=k2, lc=lc, lcl=lcl)


def rwkv_prep_tile(step0, tile0, r0, k0, v0, l0, hr, hk, hv, hl, *consts):
    f = rwkv_prep_core(tile0, r0, k0, v0, l0, hr, hk, hv, hl, *consts)
    lc, lw = f["lc"], f["lw"]
    e_neg = jnp.exp(-lc)
    rt = f["ur"] * jnp.exp(lc)
    at = -f["kk"] * jnp.exp(lc - lw)
    bt = f["kk"] * f["a_ic"] * e_neg
    kt = f["k2"] * e_neg
    return rt, at, bt, kt, jnp.exp(f["lcl"]), f["uv"], f["ur"], f["k2"]


def wkv_masks():
    lane = lane_iota((1, PAIR))
    m_lo = (lane < HEAD).astype(F32)
    ri = row_iota((CHUNK, CHUNK))
    ci = lane_iota((CHUNK, CHUNK))
    r2 = row_iota((PAIR, PAIR))
    c2 = lane_iota((PAIR, PAIR))
    bd = ((r2 < HEAD) == (c2 < HEAD)).astype(F32)
    eye2 = (r2 == c2).astype(F32)
    return (m_lo, 1.0 - m_lo), ri > ci, ri >= ci, (ri == ci).astype(F32), bd, eye2


def unit_lower_inverse(a, eye):
    t = eye + a
    p = a
    for _ in range(5):
        p = hdot(p, p)
        t = t + hdot(t, p)
    return t


def wkv_chunk_fwd(at, bt, kt, rt, v, cl, m0, masks):
    ms, strict, incl, eye, bd, eye2 = masks
    z = hdot(at, m0)
    y = hdot(rt, m0)
    tinv, aak, prb, prk = [], [], [], []
    for m in ms:
        atm, rtm = at * m, rt * m
        aab = jnp.where(strict, hdot_nt(atm, bt), 0.0)
        aak.append(jnp.where(strict, hdot_nt(atm, kt), 0.0))
        prb.append(jnp.where(incl, hdot_nt(rtm, bt), 0.0))
        prk.append(jnp.where(incl, hdot_nt(rtm, kt), 0.0))
        tinv.append(unit_lower_inverse(aab, eye))
        z = z + hdot(aak[-1], v * m)
    u = hdot(tinv[0], z * ms[0]) + hdot(tinv[1], z * ms[1])
    for i, m in enumerate(ms):
        y = y + hdot(prb[i], u * m) + hdot(prk[i], v * m)
    m1 = hdot(eye2 * cl, m0) + bd * (hdot_tn(bt * cl, u) + hdot_tn(kt * cl, v))
    return y, m1, (z, u, tinv, aak, prb, prk)


def wkv_chunk_bwd(at, bt, kt, rt, v, cl, m0, dy, dm1, masks):
    ms, strict, incl, eye, bd, eye2 = masks
    _, _, (z, u, tinv, aak, prb, prk) = wkv_chunk_fwd(at, bt, kt, rt, v, cl, m0, masks)
    dm1 = dm1 * bd
    bc, kc = bt * cl, kt * cl
    du = hdot(bc, dm1)
    dv = hdot(kc, dm1)
    for i, m in enumerate(ms):
        dym = dy * m
        du = du + hdot_tn(prb[i], dym)
        dv = dv + hdot_tn(prk[i], dym)
    dz = hdot_tn(tinv[0], du * ms[0]) + hdot_tn(tinv[1], du * ms[1])
    for i, m in enumerate(ms):
        dv = dv + hdot_tn(aak[i], dz * m)
    dm0 = bd * (hdot_tn(rt, dy) + hdot_tn(at, dz)) + hdot(eye2 * cl, dm1)
    drt = hdot_nt(dy, m0)
    dat = hdot_nt(dz, m0)
    udm = hdot_nt(u, dm1)
    vdm = hdot_nt(v, dm1)
    dbt = udm * cl
    dkt = vdm * cl
    for m in ms:
        dzm, dym, atm, rtm = dz * m, dy * m, at * m, rt * m
        daab = jnp.where(strict, hdot_nt(dzm, u), 0.0)
        daak = jnp.where(strict, hdot_nt(dzm, v), 0.0)
        dprb = jnp.where(incl, hdot_nt(dym, u), 0.0)
        dprk = jnp.where(incl, hdot_nt(dym, v), 0.0)
        drt = drt + (hdot(dprb, bt) + hdot(dprk, kt)) * m
        dat = dat + (hdot(daab, bt) + hdot(daak, kt)) * m
        dbt = dbt + hdot_tn(dprb, rtm) + hdot_tn(daab, atm)
        dkt = dkt + hdot_tn(dprk, rtm) + hdot_tn(daak, atm)
    ones = jnp.ones((8, PAIR), F32)
    dlcl = hdot_nt(ones, dm1 * m0)[0:1, :] * cl + colsum(bc * udm + kc * vdm)
    g = drt * rt - dbt * bt - dkt * kt + dat * at
    upper = (lane_iota((CHUNK, CHUNK)) >= row_iota((CHUNK, CHUNK))).astype(F32)
    dlw = hdot(upper, g) - dat * at + dlcl
    return dat, dbt, dkt, drt, dv, dlw, dm0


def wkv_forward(at, bt, kt, rt, v, clf):
    n_rows = at.shape[0]
    cps = WKV_CHUNKS_PER_STEP
    rb = cps * CHUNK
    n_steps = n_rows // rb

    def body(a_ref, b_ref, k_ref, r_ref, v_ref, c_ref, y_ref, m0_ref, m_scr):
        @pl.when(pl.program_id(1) == 0)
        def _():
            m_scr[...] = jnp.zeros_like(m_scr)

        masks = wkv_masks()
        m = m_scr[...]
        for cc in range(cps):
            sl = slice(cc * CHUNK, (cc + 1) * CHUNK)
            m0_ref[0, cc] = m
            y, m, _ = wkv_chunk_fwd(a_ref[sl, :], b_ref[sl, :], k_ref[sl, :], r_ref[sl, :], v_ref[sl, :],
                                    c_ref[cc * CHUNK:cc * CHUNK + 1, :], m, masks)
            y_ref[sl, :] = y
        m_scr[...] = m

    blk = pl.BlockSpec((rb, PAIR), lambda p, s: (s, p))
    return pl.pallas_call(
        body, name="wkv_forward", grid=(WIDTH // PAIR, n_steps),
        in_specs=[blk] * 6,
        out_specs=[blk, pl.BlockSpec((1, cps, PAIR, PAIR), lambda p, s: (p, s, 0, 0))],
        out_shape=[jax.ShapeDtypeStruct((n_rows, WIDTH), F32),
                   jax.ShapeDtypeStruct((WIDTH // PAIR, n_rows // CHUNK, PAIR, PAIR), F32)],
        scratch_shapes=[pltpu.VMEM((PAIR, PAIR), F32)],
        compiler_params=pltpu.CompilerParams(dimension_semantics=("arbitrary", "arbitrary"),
                                             vmem_limit_bytes=VMEM_LIMIT),
    )(at, bt, kt, rt, v, clf)


def wkv_backward(at, bt, kt, rt, v, clf, m0s, dy):
    n_rows = at.shape[0]
    cps = WKV_CHUNKS_PER_STEP
    rb = cps * CHUNK
    n_steps = n_rows // rb

    def body(a_ref, b_ref, k_ref, r_ref, v_ref, c_ref, m0_ref, dy_ref,
             da_ref, db_ref, dk_ref, dr_ref, dv_ref, dlw_ref, dm_scr):
        @pl.when(pl.program_id(1) == 0)
        def _():
            dm_scr[...] = jnp.zeros_like(dm_scr)

        masks = wkv_masks()
        dm = dm_scr[...]
        for cc in reversed(range(cps)):
            sl = slice(cc * CHUNK, (cc + 1) * CHUNK)
            dat, dbt, dkt, drt, dv, dlw, dm = wkv_chunk_bwd(
                a_ref[sl, :], b_ref[sl, :], k_ref[sl, :], r_ref[sl, :], v_ref[sl, :],
                c_ref[cc * CHUNK:cc * CHUNK + 1, :], m0_ref[0, cc], dy_ref[sl, :], dm, masks)
            da_ref[sl, :] = dat
            db_ref[sl, :] = dbt
            dk_ref[sl, :] = dkt
            dr_ref[sl, :] = drt
            dv_ref[sl, :] = dv
            dlw_ref[sl, :] = dlw
        dm_scr[...] = dm

    blk = pl.BlockSpec((rb, PAIR), lambda p, s: (n_steps - 1 - s, p))
    return pl.pallas_call(
        body, name="wkv_backward", grid=(WIDTH // PAIR, n_steps),
        in_specs=[blk] * 6 + [pl.BlockSpec((1, cps, PAIR, PAIR), lambda p, s: (p, n_steps - 1 - s, 0, 0)), blk],
        out_specs=[blk] * 6,
        out_shape=[jax.ShapeDtypeStruct((n_rows, WIDTH), F32)] * 6,
        scratch_shapes=[pltpu.VMEM((PAIR, PAIR), F32)],
        compiler_params=pltpu.CompilerParams(dimension_semantics=("arbitrary", "arbitrary"),
                                             vmem_limit_bytes=VMEM_LIMIT),
    )(at, bt, kt, rt, v, clf, m0s, dy)


def visible(q_row0, k_row0, shape):
    qc = (q_row0 + row_iota(shape)) // CHUNK
    kc = (k_row0 + lane_iota(shape)) // CHUNK
    return kc <= qc


def attention_forward(q, k, v):
    n_rows = q.shape[0]
    tq = tk = ATTN_TILE
    n_q = n_rows // tq

    def body(q_ref, k_ref, v_ref, o_ref, lse_ref):
        i = pl.program_id(1)
        lane = lane_iota((tq, LANE))
        out = jnp.zeros((tq, LANE), F32)
        lse_out = jnp.zeros((tq, LANE), F32)
        for hh in range(2):
            cols = slice(hh * LANE, (hh + 1) * LANE)
            qh = q_ref[:, cols]

            def step(j, carry, cols=cols, qh=qh):
                m, l, acc = carry
                rows = pl.ds(pl.multiple_of(j * tk, tk), tk)
                s = mm_nt(qh, k_ref[rows, cols]) * ATTN_SCALE
                s = jnp.where(visible(i * tq, j * tk, s.shape), s, -jnp.inf)
                m_new = jnp.maximum(m, jnp.max(s, axis=-1, keepdims=True))
                p = jnp.exp(s - m_new)
                alpha = jnp.exp(m - m_new)
                l = alpha * l + jnp.sum(p, axis=-1, keepdims=True)
                acc = alpha * acc + mm(p, v_ref[rows, cols])
                return m_new, l, acc

            init = (jnp.full((tq, 1), -jnp.inf, F32), jnp.zeros((tq, 1), F32), jnp.zeros((tq, LANE), F32))
            m, l, acc = lax.fori_loop(0, i + 1, step, init)
            out = out + acc / l
            lse_out = jnp.where((lane >= HEAD) == (hh == 1), m + jnp.log(l), lse_out)
        o_ref[...] = out
        lse_ref[...] = lse_out

    return pl.pallas_call(
        body, name="attention_forward", grid=(HEADS // 2, n_q),
        in_specs=[pl.BlockSpec((tq, 2 * LANE), lambda p, i: (i, p)),
                  pl.BlockSpec((n_rows, 2 * LANE), lambda p, i: (0, p)),
                  pl.BlockSpec((n_rows, 2 * LANE), lambda p, i: (0, p))],
        out_specs=[pl.BlockSpec((tq, LANE), lambda p, i: (i, p))] * 2,
        out_shape=[jax.ShapeDtypeStruct((n_rows, WIDTH), F32)] * 2,
        compiler_params=pltpu.CompilerParams(dimension_semantics=("arbitrary", "arbitrary"),
                                             vmem_limit_bytes=VMEM_LIMIT),
    )(q, k, v)


def attention_backward(q, k, v, o, do, lse):
    n_rows = q.shape[0]
    tq = tk = ATTN_TILE
    n_q = n_rows // tq

    def body(q_ref, k_ref, v_ref, o_ref, do_ref, lse_ref, dq_ref, dk_ref, dv_ref):
        j = pl.program_id(1)

        @pl.when(j == 0)
        def _():
            dq_ref[...] = jnp.zeros_like(dq_ref)

        lane = lane_iota((tq, LANE))
        for hh in range(2):
            cols = slice(hh * LANE, (hh + 1) * LANE)
            kh = k_ref[:, cols]
            vh = v_ref[:, cols]
            head_lanes = ((lane >= HEAD) == (hh == 1)).astype(F32)

            def step(i, carry, cols=cols, kh=kh, vh=vh, head_lanes=head_lanes, hh=hh):
                dk, dv = carry
                rows = pl.ds(pl.multiple_of(i * tq, tq), tq)
                qh = q_ref[rows, cols]
                dout = do_ref[rows, :]
                delta = jnp.sum(dout * o_ref[rows, :] * head_lanes, axis=-1, keepdims=True)
                lse_h = jnp.sum(jnp.where(lane == hh * HEAD, lse_ref[rows, :], 0.0), axis=-1, keepdims=True)
                s = mm_nt(qh, kh) * ATTN_SCALE
                p = jnp.where(visible(i * tq, j * tk, s.shape), jnp.exp(s - lse_h), 0.0)
                dv = dv + mm_tn(p, dout)
                dp = mm_nt(dout, vh)
                ds = p * (dp - delta) * ATTN_SCALE
                dq_ref[rows, cols] += mm(ds, kh)
                dk = dk + mm_tn(ds, qh)
                return dk, dv

            dk, dv = lax.fori_loop(j, n_q, step, (jnp.zeros((tk, LANE), F32), jnp.zeros((tk, LANE), F32)))
            dk_ref[:, cols] = dk
            dv_ref[:, cols] = dv

    full = lambda w: pl.BlockSpec((n_rows, w), lambda p, j: (0, p))
    blk = pl.BlockSpec((tk, 2 * LANE), lambda p, j: (j, p))
    return pl.pallas_call(
        body, name="attention_backward", grid=(HEADS // 2, n_q),
        in_specs=[full(2 * LANE), blk, blk, full(LANE), full(LANE), full(LANE)],
        out_specs=[full(2 * LANE), blk, blk],
        out_shape=[jax.ShapeDtypeStruct((n_rows, HEADS * LANE), F32)] * 3,
        compiler_params=pltpu.CompilerParams(dimension_semantics=("arbitrary", "arbitrary"),
                                             vmem_limit_bytes=VMEM_LIMIT),
    )(q, k, v, o, do, lse)


def tail_tile(step0, tile0, x, tgt, ma, mb, gpa, gpb, ya, y, ur, k2, uv,
              mod, wpa, wpb, wout, gn_g, gn_b, r_k, post_g, post_b, bd):
    gate = mod[2:3]
    inv = 1.0 / HEAD
    yc = y - head_sum(y, bd) * inv
    rs = lax.rsqrt(head_sum(yc * yc, bd) * inv + GN_EPS)
    yn = yc * rs
    yb = yn * gn_g + gn_b + head_sum(ur * k2 * r_k, bd) * uv
    sga, sgb = sigmoid(gpa), sigmoid(gpb)
    sila, silb = gpa * sga, gpb * sgb
    ga, gb = ya * sila, yb * silb
    pa, pb = mm(ga, wpa), mm(gb, wpb)
    sa, sb = sigmoid(ma), sigmoid(mb)
    merged = sa * pa + sb * pb
    sub = mm(merged, wout)
    z = ALPHA * x + (1.0 + gate) * sub
    zhat, rstd = layer_norm_stats(z)
    err = zhat * post_g + post_b - tgt
    loss = 0.5 * jnp.sum(rowmean(err * err), axis=0, keepdims=True) + jnp.zeros((1, LANE), F32)
    dout = err * (1.0 / D_MODEL)
    dpost_g = colsum(dout * zhat)
    dpost_b = colsum(dout)
    dz = layer_norm_bwd(dout * post_g, zhat, rstd)
    dgate = colsum(dz * sub)
    dsub = dz * (1.0 + gate)
    dwout = mm_tn(merged, dsub)
    dmerged = mm_nt(dsub, wout)
    dpa, dpb = dmerged * sa, dmerged * sb
    dma = dmerged * pa * sa * (1.0 - sa)
    dmb = dmerged * pb * sb * (1.0 - sb)
    dwpa = mm_tn(ga, dpa)
    dwpb = mm_tn(gb, dpb)
    dga = mm_nt(dpa, wpa)
    dgb = mm_nt(dpb, wpb)
    dya = dga * sila
    dgpa = dga * ya * (sga * (1.0 + gpa * (1.0 - sga)))
    dyb = dgb * silb
    dgpb = dgb * yb * (sgb * (1.0 + gpb * (1.0 - sgb)))
    dgn_g = colsum(dyb * yn)
    dgn_b = colsum(dyb)
    dyn = dyb * gn_g
    dy = rs * (dyn - head_sum(dyn, bd) * inv - yn * head_sum(dyn * yn, bd) * inv)
    return (dz, dma, dmb, dgpa, dgpb, dya, dy, dyb,
            loss, dwout, dwpa, dwpb, dgn_g, dgn_b, dpost_g, dpost_b, dgate)


def mla_prep_bwd_tile(step0, tile0, q_c, kv_c, cos, sin, dq, dk, dv, gq, gkv, wq, wqr, wkn, wv):
    qn, qh, rq = rms_norm_fwd(q_c, gq)
    kvn, kvh, rkv = rms_norm_fwd(kv_c, gkv)
    dqc = dq * tile_lanes(cos, HEADS)
    dqs = dq * tile_lanes(sin, HEADS)
    dqn = mm_nt(dqc, wq) + mm_nt(dqs, wqr)
    dkvn = mm_nt(dk, wkn) + mm_nt(dv, wv)
    dkpe = dk[:, 0:LANE]
    for h in range(1, HEADS):
        dkpe = dkpe + dk[:, h * LANE:(h + 1) * LANE]
    dkr = dkpe * (cos * key_rope_mask(cos.shape))
    dkrr = dkpe * sin

    def rms_bwd(dyv, xh, r, g):
        dyg = dyv * g
        return r * (dyg - xh * rowmean(dyg * xh)), colsum(dyv * xh)

    dq_c, dgq = rms_bwd(dqn, qh, rq, gq)
    dkv_c, dgkv = rms_bwd(dkvn, kvh, rkv, gkv)
    return (dq_c, dkv_c, dkr, dkrr,
            mm_tn(qn, dqc), mm_tn(qn, dqs), mm_tn(kvn, dk), mm_tn(kvn, dv), dgq, dgkv)


def rwkv_prep_bwd_tile(step0, tile0, r0, k0, v0, l0, drt, dat, dbt, dkt, dvv, dlw, dyb, hr, hk, hv, hl,
                       mu_r, mu_k, mu_v, mu_l, w0, a0, k_k, k_a, w_dec, w_iclr, tril, same, bd, r_k,
                       cr, ck, cv, cl_):
    f = rwkv_prep_core(tile0, r0, k0, v0, l0, hr, hk, hv, hl, mu_r, mu_k, mu_v, mu_l, w0, a0, k_k, k_a,
                       w_dec, w_iclr, tril, same, bd)
    ur, uk, uv, ul, kk, k2, a_ic, sg, th = (f[n] for n in ("ur", "uk", "uv", "ul", "kk", "k2", "a_ic", "sg", "th"))
    lc, lw = f["lc"], f["lw"]
    e_neg = jnp.exp(-lc)
    dur = drt * jnp.exp(lc)
    da = dat * jnp.exp(lc - lw)
    db = dbt * e_neg
    dk2 = dkt * e_neg
    s = head_sum(ur * k2 * r_k, bd)
    duv = dvv + dyb * s
    ds = head_sum(dyb * uv, bd)
    dur = dur + ds * k2 * r_k
    dk2 = dk2 + ds * ur * r_k
    dr_k = colsum(ds * ur * k2)
    dkk = db * a_ic - da
    da_ic = db * kk + dk2 * uk * k_a
    duk = dk2 * (1.0 + (a_ic - 1.0) * k_a)
    dk_a = colsum(dk2 * uk * (a_ic - 1.0))
    dkkraw = jnp.where(f["nrm_raw"] > 1e-12, (dkk - kk * head_sum(dkk * kk, bd)) / f["nrm"], dkk * 1e12)
    duk = duk + dkkraw * k_k
    dk_k = colsum(dkkraw * uk)
    dai = da_ic * a_ic * (1.0 - a_ic)
    dd = dlw * (-DECAY_SCALE) * sg * (1.0 - sg)
    dul = mm_nt(dai, w_iclr) + mm_nt(dd, w_dec) * (1.0 - th * th)

    def unshift(du, x, prev, mu, carry_row):
        nxt = shift_rows_up(du, carry_row)
        return du * (1.0 - mu) + nxt * mu, colsum(du * (prev - x)), du[0:1, :]

    dr0, dmu_r, ncr = unshift(dur, r0, f["pr"], mu_r, cr)
    dk0, dmu_k, nck = unshift(duk, k0, f["pk"], mu_k, ck)
    dv0, dmu_v, ncv = unshift(duv, v0, f["pv"], mu_v, cv)
    dl0, dmu_l, ncl = unshift(dul, l0, f["pl"], mu_l, cl_)
    return (dr0, dk0, dv0, dl0,
            dmu_r, dmu_k, dmu_v, dmu_l, colsum(dd), colsum(dai), dk_k, dk_a, dr_k, mm_tn(th, dd), mm_tn(ul, dai),
            ncr, nck, ncv, ncl)


def in_bwd_tile(step0, tile0, x, dz, dma, dmb, dr0, dk0, dv0, dgpa, dgpb, dq_c, dkv_c, dkr, dkrr, dl0, mod, w_in_p):
    dproj = jnp.concatenate([dma, dmb, dr0, dk0, dv0, dgpa, dgpb, dq_c, dkv_c, dkr, dkrr, dl0], axis=1).astype(BF16)
    dh = mm_nt(dproj, w_in_p)
    xhat, rstd = layer_norm_stats(x)
    dx = layer_norm_bwd(dh * (1.0 + mod[1:2]), xhat, rstd) + ALPHA * dz
    return dx, dproj, colsum(dh), colsum(dh * xhat)


def in_weight_grad(x, mod, dproj):
    n_rows = x.shape[0]
    ts = ROW_TILE

    def body(x_ref, mod_ref, dp_ref, dw_ref):
        xhat, _ = layer_norm_stats(x_ref[...])
        h = xhat * (1.0 + mod_ref[1:2, :]) + mod_ref[0:1, :]
        contrib = mm_tn(h, dp_ref[...])

        @pl.when(pl.program_id(1) == 0)
        def _():
            dw_ref[...] = contrib

        @pl.when(pl.program_id(1) != 0)
        def _():
            dw_ref[...] += contrib

    return pl.pallas_call(
        body, name="in_weight_grad", grid=(P_WIDTH // DW_BLOCK, n_rows // ts),
        in_specs=[pl.BlockSpec((ts, D_MODEL), lambda j, i: (i, 0)), pl.BlockSpec(memory_space=pltpu.VMEM),
                  pl.BlockSpec((ts, DW_BLOCK), lambda j, i: (i, j))],
        out_specs=pl.BlockSpec((D_MODEL, DW_BLOCK), lambda j, i: (0, j)),
        out_shape=jax.ShapeDtypeStruct((D_MODEL, P_WIDTH), F32),
        compiler_params=pltpu.CompilerParams(dimension_semantics=("arbitrary", "arbitrary"),
                                             vmem_limit_bytes=VMEM_LIMIT),
    )(x, mod, dproj)


def ada_weight_grad(c_all, dmod_cols):
    def body(c_ref, d_ref, o_ref):
        cv = c_ref[...]
        o_ref[...] = hdot_tn(cv * sigmoid(cv), d_ref[...])

    return pl.pallas_call(
        body, name="ada_weight_grad",
        out_shape=jax.ShapeDtypeStruct((c_all.shape[1], dmod_cols.shape[1]), F32),
    )(c_all, dmod_cols)


def adamw(parts, w, m, v, name):
    k, rows, _ = parts.shape
    rb = rows
    for cand in (512, 384, 256, 192, 128, 96, 64, 48, 32, 16):
        if rows % cand == 0:
            rb = cand
            break

    def body(p_ref, w_ref, m_ref, v_ref, g_ref, d_ref, nm_ref, nv_ref):
        g = p_ref[0].astype(F32)
        for i in range(1, k):
            g = g + p_ref[i].astype(F32)
        nm = ADAM_B1 * m_ref[...] + (1.0 - ADAM_B1) * g
        nv = ADAM_B2 * v_ref[...] + (1.0 - ADAM_B2) * (g * g)
        m_hat = nm / (1.0 - ADAM_B1 ** ADAM_STEP)
        v_hat = nv / (1.0 - ADAM_B2 ** ADAM_STEP)
        g_ref[...] = g
        d_ref[...] = -ADAM_LR * (m_hat / (jnp.sqrt(v_hat) + ADAM_EPS) + ADAM_WD * w_ref[...])
        nm_ref[...] = nm
        nv_ref[...] = nv

    blk = pl.BlockSpec((rb, LANE), lambda i: (i, 0))
    return pl.pallas_call(
        body, name=name, grid=(rows // rb,),
        in_specs=[pl.BlockSpec((k, rb, LANE), lambda i: (0, i, 0)), blk, blk, blk],
        out_specs=[blk] * 4, out_shape=[jax.ShapeDtypeStruct((rows, LANE), F32)] * 4,
        compiler_params=pltpu.CompilerParams(dimension_semantics=("arbitrary",), vmem_limit_bytes=VMEM_LIMIT),
    )(parts, w, m, v)


def rot_cols(w):
    return jnp.concatenate([-w[:, ROPE // 2:], w[:, :ROPE // 2]], axis=1)


def unrot_cols(dw):
    return jnp.concatenate([dw[:, ROPE // 2:], -dw[:, :ROPE // 2]], axis=1)


def columns_from_shards(g, rows, cols):
    return g.reshape(N_DEV, rows, cols).transpose(1, 0, 2).reshape(rows, N_DEV * cols)


def shards_from_columns(w, rows, cols):
    return w.reshape(rows, N_DEV, cols).transpose(1, 0, 2).reshape(N_DEV, rows * cols)


def permute_w_in(w):
    z = lambda n: jnp.zeros((D_MODEL, n), w.dtype)
    krope = w[:, N_KROPE:N_KROPE + ROPE]
    rw = N_RWKV
    return jnp.concatenate([
        w[:, N_MA:N_MA + 1024], w[:, N_MB:N_MB + 1024],
        w[:, rw:rw + 512], w[:, rw + 512:rw + 1024], w[:, rw + 1024:rw + 1536],
        w[:, N_GPA:N_GPA + 512], w[:, N_GPB:N_GPB + 512],
        w[:, N_QC:N_QC + 256], w[:, N_KVC:N_KVC + 128],
        z(NOPE), krope, z(LANE - QK_DIM), z(NOPE), rot_cols(krope), z(LANE - QK_DIM),
        w[:, rw + 1536:rw + 1664]], axis=1)


def unpermute_w_in_grad(d):
    rw = P_R
    krope = d[:, P_KR + NOPE:P_KR + QK_DIM] + unrot_cols(d[:, P_KRR + NOPE:P_KRR + QK_DIM])
    return jnp.concatenate([
        d[:, P_QC:P_QC + 256], d[:, P_KVC:P_KVC + 128], krope, d[:, P_GPA:P_GPA + 512],
        d[:, rw:rw + 1536], d[:, P_LORA:P_LORA + 128], d[:, P_GPB:P_GPB + 512],
        d[:, P_MA:P_MA + 1024], d[:, P_MB:P_MB + 1024]], axis=1)


def pad_heads_q(w_uq):
    w = w_uq.reshape(Q_RANK, HEADS, QK_DIM)
    zpad = jnp.zeros((Q_RANK, HEADS, LANE - QK_DIM), w.dtype)
    wq = jnp.concatenate([w, zpad], axis=2).reshape(Q_RANK, HEADS * LANE)
    pe = w[:, :, NOPE:]
    rot = jnp.concatenate([-pe[:, :, ROPE // 2:], pe[:, :, :ROPE // 2]], axis=2)
    wqr = jnp.concatenate([jnp.zeros((Q_RANK, HEADS, NOPE), w.dtype), rot, zpad], axis=2).reshape(Q_RANK, HEADS * LANE)
    return wq, wqr


def unpad_heads_q_grad(dwq, dwqr):
    a = dwq.reshape(Q_RANK, HEADS, LANE)
    r = dwqr.reshape(Q_RANK, HEADS, LANE)[:, :, NOPE:QK_DIM]
    pe = a[:, :, NOPE:QK_DIM] + jnp.concatenate([r[:, :, ROPE // 2:], -r[:, :, :ROPE // 2]], axis=2)
    return jnp.concatenate([a[:, :, :NOPE], pe], axis=2).reshape(Q_RANK, HEADS * QK_DIM)


def pad_heads_kv(w_ukv):
    w = w_ukv.reshape(KV_RANK, HEADS, 2 * HEAD)
    z = jnp.zeros((KV_RANK, HEADS, HEAD), w.dtype)
    wkn = jnp.concatenate([w[:, :, :NOPE], z], axis=2).reshape(KV_RANK, HEADS * LANE)
    val = w[:, :, NOPE:]
    odd = (jnp.arange(HEADS) % 2 == 1)[None, :, None]
    wv = jnp.concatenate([jnp.where(odd, 0, val), jnp.where(odd, val, 0)], axis=2).reshape(KV_RANK, HEADS * LANE)
    return wkn, wv


def unpad_heads_kv_grad(dwkn, dwv):
    a = dwkn.reshape(KV_RANK, HEADS, LANE)[:, :, :NOPE]
    b = dwv.reshape(KV_RANK, HEADS, LANE)
    odd = (jnp.arange(HEADS) % 2 == 1)[None, :, None]
    val = jnp.where(odd, b[:, :, HEAD:], b[:, :, :HEAD])
    return jnp.concatenate([a, val], axis=2).reshape(KV_RANK, HEADS * 2 * HEAD)


def kernel(x, c, positions, w_ada, b_ada, w_in, q_norm_g, w_uq, kv_norm_g, w_ukv, mu_rwkv, w0, w_decay_up, a0, w_iclr_up, k_k, k_a, r_k, gn_g, gn_b, w_proj_a, w_proj_b, w_out, post_g, post_b, loss_target, m_w_ada, m_b_ada, m_w_in, m_q_norm_g, m_w_uq, m_kv_norm_g, m_w_ukv, m_mu_rwkv, m_w0, m_w_decay_up, m_a0, m_w_iclr_up, m_k_k, m_k_a, m_r_k, m_gn_g, m_gn_b, m_w_proj_a, m_w_proj_b, m_w_out, m_post_g, m_post_b, v_w_ada, v_b_ada, v_w_in, v_q_norm_g, v_w_uq, v_kv_norm_g, v_w_ukv, v_mu_rwkv, v_w0, v_w_decay_up, v_a0, v_w_iclr_up, v_k_k, v_k_a, v_r_k, v_gn_g, v_gn_b, v_w_proj_a, v_w_proj_b, v_w_out, v_post_g, v_post_b):
    weights = dict(w_ada=w_ada, b_ada=b_ada, w_in=w_in, q_norm_g=q_norm_g, w_uq=w_uq, kv_norm_g=kv_norm_g,
                   w_ukv=w_ukv, mu_rwkv=mu_rwkv, w0=w0, w_decay_up=w_decay_up, a0=a0, w_iclr_up=w_iclr_up,
                   k_k=k_k, k_a=k_a, r_k=r_k, gn_g=gn_g, gn_b=gn_b, w_proj_a=w_proj_a, w_proj_b=w_proj_b,
                   w_out=w_out, post_g=post_g, post_b=post_b)
    mom1 = dict(w_ada=m_w_ada, b_ada=m_b_ada, w_in=m_w_in, q_norm_g=m_q_norm_g, w_uq=m_w_uq, kv_norm_g=m_kv_norm_g,
                w_ukv=m_w_ukv, mu_rwkv=m_mu_rwkv, w0=m_w0, w_decay_up=m_w_decay_up, a0=m_a0, w_iclr_up=m_w_iclr_up,
                k_k=m_k_k, k_a=m_k_a, r_k=m_r_k, gn_g=m_gn_g, gn_b=m_gn_b, w_proj_a=m_w_proj_a, w_proj_b=m_w_proj_b,
                w_out=m_w_out, post_g=m_post_g, post_b=m_post_b)
    mom2 = dict(w_ada=v_w_ada, b_ada=v_b_ada, w_in=v_w_in, q_norm_g=v_q_norm_g, w_uq=v_w_uq, kv_norm_g=v_kv_norm_g,
                w_ukv=v_w_ukv, mu_rwkv=v_mu_rwkv, w0=v_w0, w_decay_up=v_w_decay_up, a0=v_a0, w_iclr_up=v_w_iclr_up,
                k_k=v_k_k, k_a=v_k_a, r_k=v_r_k, gn_g=v_gn_g, gn_b=v_gn_b, w_proj_a=v_w_proj_a, w_proj_b=v_w_proj_b,
                w_out=v_w_out, post_g=v_post_g, post_b=v_post_b)
    names = list(weights)
    n_rows = x.shape[1]
    me = 4 * lax.axis_index("x") + 2 * lax.axis_index("y") + lax.axis_index("c")
    xr = x[0]
    tgt = loss_target[0]
    row = lambda a: a.reshape(1, -1)

    def flat_shards(tree):
        return jnp.concatenate([tree[n].reshape(-1) for n, _, _ in SHARDED])

    c_pairs = lax.bitcast_convert_type(c[0], BF16).reshape(-1)
    send = jnp.concatenate([flat_shards(weights).astype(BF16), c_pairs]).reshape(GATHER_ROWS, LANE)
    gathered = gather_shards(send).reshape(N_DEV, GATHER_ROWS * LANE)
    c_all = lax.bitcast_convert_type(gathered[:, SHARD_ELEMS:].reshape(N_DEV, D_MODEL, 2), F32)
    full = {}
    off = 0
    for n, r, cdim in SHARDED:
        part = gathered[:, off:off + r * cdim]
        off += r * cdim
        full[n] = part.reshape(N_DEV * r, cdim) if n == "w_out" else columns_from_shards(part, r, cdim)
    w_in_p = permute_w_in(full["w_in"])
    wq, wqr = pad_heads_q(full["w_uq"])
    wkn, wv = pad_heads_kv(full["w_ukv"])
    zl = jnp.zeros((LORA, WIDTH), BF16)
    w_dec = jnp.concatenate([full["w_decay_up"], zl], axis=0)
    w_iclr = jnp.concatenate([zl, full["w_iclr_up"]], axis=0)
    wpa, wpb, wout = full["w_proj_a"], full["w_proj_b"], full["w_out"]

    mod_all = ada_modulation(c_all, w_ada[0], b_ada.reshape(N_DEV, -1))
    mod = lax.dynamic_index_in_dim(mod_all, me, axis=1, keepdims=False).reshape(3, D_MODEL)

    (proj,) = row_call("fwd_in", fwd_in_tile, n_rows, [(xr, D_MODEL, 0)], [mod, w_in_p], [(P_WIDTH, F32)])
    pcol = lambda off_, w: (proj, w, off_ // w)

    inv_freq = ROPE_THETA ** (-jnp.arange(0, ROPE, 2, dtype=F32) / ROPE)
    ang = positions[0].astype(F32)[:, None] * inv_freq
    ones_n, zeros_n, zeros_p = jnp.ones((n_rows, NOPE), F32), jnp.zeros((n_rows, NOPE), F32), jnp.zeros((n_rows, LANE - QK_DIM), F32)
    cos_t = jnp.concatenate([ones_n, jnp.cos(ang), jnp.cos(ang), zeros_p], axis=1)
    sin_t = jnp.concatenate([zeros_n, jnp.sin(ang), jnp.sin(ang), zeros_p], axis=1)

    gq, gkv = q_norm_g, kv_norm_g
    mla_consts = [gq, gkv, wq, wqr, wkn, wv]
    q, k, v = row_call(
        "mla_prep", mla_prep_tile, n_rows,
        [pcol(P_QC, 256), pcol(P_KVC, 128), pcol(P_KR, 128), pcol(P_KRR, 128), (cos_t, LANE, 0), (sin_t, LANE, 0)],
        mla_consts, [(HEADS * LANE, BF16)] * 3)
    ya, lse = attention_forward(q, k, v)

    t_idx = jnp.arange(ROW_TILE)
    same_chunk = (t_idx[:, None] // CHUNK) == (t_idx[None, :] // CHUNK)
    same = same_chunk.astype(F32)
    tril = (same_chunk & (t_idx[:, None] >= t_idx[None, :])).astype(F32)
    l_idx = jnp.arange(LANE)
    bd = ((l_idx[:, None] // HEAD) == (l_idx[None, :] // HEAD)).astype(F32)
    mu = mu_rwkv
    mu_r, mu_k, mu_v, mu_l = mu[:, 0:512], mu[:, 512:1024], mu[:, 1024:1536], mu[:, 1536:1664]
    rk_row = row(r_k)
    rwkv_consts = [mu_r, mu_k, mu_v, mu_l, w0, a0, k_k, k_a, w_dec, w_iclr, tril, same, bd]
    rwkv_rows = [pcol(P_R, 512), pcol(P_K, 512), pcol(P_V, 512), pcol(P_LORA, 128)]
    rt, at, bt, kt, clf, uv, ur, k2 = row_call(
        "rwkv_prep", rwkv_prep_tile, n_rows, rwkv_rows, rwkv_consts, [(WIDTH, F32)] * 8, halo_in=rwkv_rows)
    y, m0s = wkv_forward(at, bt, kt, rt, uv, clf)

    tail = row_call(
        "tail", tail_tile, n_rows,
        [(xr, D_MODEL, 0), (tgt, D_MODEL, 0), pcol(P_MA, 1024), pcol(P_MB, 1024), pcol(P_GPA, 512), pcol(P_GPB, 512),
         (ya, WIDTH, 0), (y, WIDTH, 0), (ur, WIDTH, 0), (k2, WIDTH, 0), (uv, WIDTH, 0)],
        [mod, wpa, wpb, wout, gn_g, gn_b, rk_row, post_g, post_b, bd],
        [(D_MODEL, F32), (1024, F32), (1024, F32), (512, F32), (512, F32), (WIDTH, F32), (WIDTH, F32), (WIDTH, F32)],
        acc_out=[((1, LANE), F32), ((D_MODEL, D_MODEL), F32), ((WIDTH, D_MODEL), F32), ((WIDTH, D_MODEL), F32),
                 ((1, WIDTH), F32), ((1, WIDTH), F32), ((1, D_MODEL), F32), ((1, D_MODEL), F32), ((1, D_MODEL), F32)])
    (dz, dma, dmb, dgpa, dgpb, dya, dy, dyb,
     loss_row, g_wout, g_wpa, g_wpb, g_gn_g, g_gn_b, g_post_g, g_post_b, dgate) = tail
    loss = lax.psum(loss_row[0, 0], ("x", "y", "c"))

    dq, dk, dv = attention_backward(q, k, v, ya, dya, lse)
    dq_c, dkv_c, dkr, dkrr, g_wq, g_wqr, g_wkn, g_wv, g_gq, g_gkv = row_call(
        "mla_prep_bwd", mla_prep_bwd_tile, n_rows,
        [pcol(P_QC, 256), pcol(P_KVC, 128), (cos_t, LANE, 0), (sin_t, LANE, 0),
         (dq, HEADS * LANE, 0), (dk, HEADS * LANE, 0), (dv, HEADS * LANE, 0)],
        mla_consts, [(256, F32), (128, F32), (128, F32), (128, F32)],
        acc_out=[((Q_RANK, HEADS * LANE), F32)] * 2 + [((KV_RANK, HEADS * LANE), F32)] * 2
        + [((1, Q_RANK), F32), ((1, KV_RANK), F32)])

    dat, dbt, dkt, drt, dvv, dlw = wkv_backward(at, bt, kt, rt, uv, clf, m0s, dy)
    (dr0, dk0, dv0, dl0, g_mu_r, g_mu_k, g_mu_v, g_mu_l, g_w0, g_a0, g_k_k, g_k_a, g_r_k, g_wdec, g_wiclr) = row_call(
        "rwkv_prep_bwd", rwkv_prep_bwd_tile, n_rows,
        rwkv_rows + [(drt, WIDTH, 0), (dat, WIDTH, 0), (dbt, WIDTH, 0), (dkt, WIDTH, 0), (dvv, WIDTH, 0),
                     (dlw, WIDTH, 0), (dyb, WIDTH, 0)],
        rwkv_consts + [rk_row], [(512, F32), (512, F32), (512, F32), (128, F32)],
        acc_out=[((1, 512), F32)] * 3 + [((1, 128), F32)] + [((1, 512), F32)] * 5 + [((LANE, WIDTH), F32)] * 2,
        halo_in=rwkv_rows, carry=[512, 512, 512, 128], reverse=True)

    dx, dproj, dshift, dscale = row_call(
        "in_bwd", in_bwd_tile, n_rows,
        [(xr, D_MODEL, 0), (dz, D_MODEL, 0), (dma, 1024, 0), (dmb, 1024, 0), (dr0, 512, 0), (dk0, 512, 0), (dv0, 512, 0),
         (dgpa, 512, 0), (dgpb, 512, 0), (dq_c, 256, 0), (dkv_c, 128, 0), (dkr, 128, 0), (dkrr, 128, 0), (dl0, 128, 0)],
        [mod, w_in_p], [(D_MODEL, F32), (P_WIDTH, BF16)], acc_out=[((1, D_MODEL), F32)] * 2)
    g_w_in_p = in_weight_grad(xr, mod, dproj)

    grads_full = {
        "w_in": unpermute_w_in_grad(g_w_in_p), "w_uq": unpad_heads_q_grad(g_wq, g_wqr),
        "w_ukv": unpad_heads_kv_grad(g_wkn, g_wv), "w_decay_up": g_wdec[:LORA], "w_iclr_up": g_wiclr[LORA:],
        "w_proj_a": g_wpa, "w_proj_b": g_wpb, "w_out": g_wout}
    blocks = jnp.concatenate(
        [grads_full[n].reshape(N_DEV, r * cdim) if n == "w_out" else shards_from_columns(grads_full[n], r, cdim)
         for n, r, cdim in SHARDED], axis=1).astype(BF16).reshape(N_DEV, SHARD_ROWS, LANE)
    dmod = jnp.concatenate([dshift, dscale, dgate], axis=1)
    small = jnp.concatenate([dmod, g_gq, g_gkv, g_mu_r, g_mu_k, g_mu_v, g_mu_l, g_w0, g_a0, g_k_k, g_k_a, g_r_k,
                             g_gn_g, g_gn_b, g_post_g, g_post_b], axis=1).reshape(SMALL_ROWS, LANE)
    got_blocks, got_small = exchange_grads(blocks, small)

    ada_cols = w_ada.shape[2]
    dmod_all = got_small.reshape(N_DEV, SMALL_ELEMS)[:, :3 * D_MODEL]
    g_ada = ada_weight_grad(c_all, lax.dynamic_slice_in_dim(dmod_all, me * ada_cols, ada_cols, axis=1))

    def flat_rows(tree, entries):
        return jnp.concatenate([tree[n].reshape(-1) for n, *_ in entries]).reshape(-1, LANE)

    res_ada = adamw(g_ada.reshape(1, -1, LANE), w_ada.reshape(-1, LANE), m_w_ada.reshape(-1, LANE),
                    v_w_ada.reshape(-1, LANE), "adamw_ada")
    res_shard = adamw(got_blocks, flat_rows(weights, SHARDED), flat_rows(mom1, SHARDED), flat_rows(mom2, SHARDED),
                      "adamw_sharded")
    res_small = adamw(got_small, flat_rows(weights, SMALL), flat_rows(mom1, SMALL), flat_rows(mom2, SMALL),
                      "adamw_small")

    outs = [dict() for _ in range(4)]
    for kind in range(4):
        outs[kind]["w_ada"] = res_ada[kind].reshape(w_ada.shape)
        flat = res_shard[kind].reshape(-1)
        off = 0
        for n, r, cdim in SHARDED:
            outs[kind][n] = flat[off:off + r * cdim].reshape(weights[n].shape)
            off += r * cdim
        flat = res_small[kind].reshape(-1)
        off = 0
        for n, size in SMALL:
            outs[kind][n] = flat[off:off + size].reshape(weights[n].shape)
            off += size
    return (loss, dx[None], *[outs[0][n] for n in names], *[outs[1][n] for n in names],
            *[outs[2][n] for n in names], *[outs[3][n] for n in names])
```

```python
import functools
import math

import jax
import jax.numpy as jnp
from jax import lax
from jax.experimental import pallas as pl
from jax.experimental.pallas import tpu as pltpu

F32 = jnp.float32
BF16 = jnp.bfloat16
HIGHEST = lax.Precision.HIGHEST
MESH_IDS = pl.DeviceIdType.MESH

N_DEV = 8
D_MODEL = 1024
LN_EPS = 1e-5
RMS_EPS = 1e-6
GN_EPS = 64e-5
HEADS = 8
Q_RANK = 256
KV_RANK = 128
ROPE = 32
NOPE = 64
QK_DIM = NOPE + ROPE
WIDTH = 512
HEAD = 64
LORA = 64
CHUNK = 64
DEPTH = 1
ALPHA = (2.0 * DEPTH) ** 0.25
ROPE_THETA = 10000.0
ATTN_SCALE = QK_DIM ** -0.5
DECAY_SCALE = math.exp(-0.5)

ADAM_LR = 0.001
ADAM_B1 = 0.9
ADAM_B2 = 0.999
ADAM_EPS = 1e-08
ADAM_WD = 0.01
ADAM_STEP = 10

LANE = 128
PAIR = 2 * HEAD
ROW_TILE = 256
ATTN_TILE = 256
WKV_CHUNKS_PER_STEP = 4
VMEM_LIMIT = 56 * 1024 * 1024

P_MA, P_MB, P_R, P_K, P_V, P_GPA, P_GPB, P_QC, P_KVC, P_KR, P_KRR, P_LORA = (
    0, 1024, 2048, 2560, 3072, 3584, 4096, 4608, 4864, 4992, 5120, 5248)
P_WIDTH = 5376
DW_BLOCK = 768

N_QC, N_KVC, N_KROPE, N_GPA, N_RWKV, N_GPB, N_MA, N_MB = 0, 256, 384, 416, 928, 2592, 3104, 4128
IN_WIDTH = 5152

SHARDED = (("w_in", 1024, 644), ("w_uq", 256, 96), ("w_ukv", 128, 128), ("w_decay_up", 64, 64),
           ("w_iclr_up", 64, 64), ("w_proj_a", 512, 128), ("w_proj_b", 512, 128), ("w_out", 128, 1024))
SHARD_ELEMS = sum(r * c for _, r, c in SHARDED)
SHARD_ROWS = SHARD_ELEMS // LANE
GATHER_ROWS = SHARD_ROWS + 2 * D_MODEL // LANE
SMALL = (("b_ada", 3072), ("q_norm_g", 256), ("kv_norm_g", 128), ("mu_rwkv", 1664), ("w0", 512), ("a0", 512),
         ("k_k", 512), ("k_a", 512), ("r_k", 512), ("gn_g", 512), ("gn_b", 512), ("post_g", 1024), ("post_b", 1024))
SMALL_ELEMS = sum(n for _, n in SMALL)
SMALL_ROWS = SMALL_ELEMS // LANE


def mm(a, b):
    return jnp.dot(a.astype(BF16), b.astype(BF16), preferred_element_type=F32)


def mm_nt(a, b):
    return lax.dot_general(a.astype(BF16), b.astype(BF16), (((1,), (1,)), ((), ())), preferred_element_type=F32)


def mm_tn(a, b):
    return lax.dot_general(a.astype(BF16), b.astype(BF16), (((0,), (0,)), ((), ())), preferred_element_type=F32)


def hdot(a, b):
    return jnp.dot(a, b, precision=HIGHEST, preferred_element_type=F32)


def hdot_nt(a, b):
    return lax.dot_general(a, b, (((1,), (1,)), ((), ())), precision=HIGHEST, preferred_element_type=F32)


def hdot_tn(a, b):
    return lax.dot_general(a, b, (((0,), (0,)), ((), ())), precision=HIGHEST, preferred_element_type=F32)


def sigmoid(x):
    return 1.0 / (1.0 + jnp.exp(-x))


def colsum(x):
    return jnp.sum(x, axis=0, keepdims=True)


def rowmean(x):
    return jnp.mean(x, axis=-1, keepdims=True)


def layer_norm_stats(x):
    xc = x - rowmean(x)
    rstd = lax.rsqrt(rowmean(xc * xc) + LN_EPS)
    return xc * rstd, rstd


def layer_norm_bwd(dy, xhat, rstd):
    return rstd * (dy - rowmean(dy) - xhat * rowmean(dy * xhat))


def head_sum(x, bd):
    return jnp.concatenate([hdot(x[:, p * LANE:(p + 1) * LANE], bd) for p in range(x.shape[1] // LANE)], axis=1)


def tile_lanes(t, n):
    return jnp.concatenate([t] * n, axis=1)


def row_iota(shape):
    return lax.broadcasted_iota(jnp.int32, shape, 0)


def lane_iota(shape):
    return lax.broadcasted_iota(jnp.int32, shape, 1)


def shift_rows_down(x, row0):
    rolled = pltpu.roll(x, 1, axis=0)
    return jnp.where(row_iota(x.shape) == 0, row0, rolled)


def shift_rows_up(x, row_last):
    rolled = pltpu.roll(x, x.shape[0] - 1, axis=0)
    return jnp.where(row_iota(x.shape) == x.shape[0] - 1, row_last, rolled)


def row_call(name, fn, n_rows, row_in, const_in, row_out, acc_out=(), halo_in=(), carry=(), reverse=False):
    ts = ROW_TILE
    n_tiles = n_rows // ts
    n_in = len(row_in) + len(halo_in) + len(const_in)
    n_ro, n_ao = len(row_out), len(acc_out)

    def tile_of(g):
        return (n_tiles - 1 - g) if reverse else g

    def body(*refs):
        ins = refs[:n_in]
        ro = refs[n_in:n_in + n_ro]
        ao = refs[n_in + n_ro:n_in + n_ro + n_ao]
        cr = refs[n_in + n_ro + n_ao:]
        g = pl.program_id(0)
        step0 = g == 0
        tile0 = tile_of(g) == 0
        for r in cr:
            @pl.when(step0)
            def _(r=r):
                r[...] = jnp.zeros_like(r)
        vals = [r[...] for r in ins]
        outs = fn(step0, tile0, *vals, *[c[0:1, :] for c in cr])
        for r, v in zip(ro, outs[:n_ro]):
            r[...] = v.astype(r.dtype)
        for r, v in zip(ao, outs[n_ro:n_ro + n_ao]):
            @pl.when(step0)
            def _(r=r, v=v):
                r[...] = v.astype(r.dtype)

            @pl.when(jnp.logical_not(step0))
            def _(r=r, v=v):
                r[...] += v.astype(r.dtype)
        for r, v in zip(cr, outs[n_ro + n_ao:]):
            r[0:1, :] = v

    in_specs = [pl.BlockSpec((ts, w), functools.partial(lambda g, cb: (tile_of(g), cb), cb=cb)) for _, w, cb in row_in]
    in_specs += [pl.BlockSpec((8, w), functools.partial(
        lambda g, cb: (jnp.maximum(tile_of(g) * (ts // 8) - 1, 0), cb), cb=cb)) for _, w, cb in halo_in]
    in_specs += [pl.BlockSpec(memory_space=pltpu.VMEM) for _ in const_in]
    out_specs = [pl.BlockSpec((ts, w), lambda g: (tile_of(g), 0)) for w, _ in row_out]
    out_specs += [pl.BlockSpec(s, lambda g: (0, 0)) for s, _ in acc_out]
    out_shape = [jax.ShapeDtypeStruct((n_rows, w), d) for w, d in row_out]
    out_shape += [jax.ShapeDtypeStruct(s, d) for s, d in acc_out]
    return pl.pallas_call(
        body, name=name, grid=(n_tiles,), in_specs=in_specs, out_specs=out_specs, out_shape=out_shape,
        scratch_shapes=[pltpu.VMEM((8, w), F32) for w in carry],
        compiler_params=pltpu.CompilerParams(dimension_semantics=("arbitrary",), vmem_limit_bytes=VMEM_LIMIT),
    )(*[a for a, _, _ in row_in], *[a for a, _, _ in halo_in], *const_in)


def my_position():
    return lax.axis_index("x"), lax.axis_index("y"), lax.axis_index("c")


def flip(pos, k):
    x, y, c = pos
    dx, dy, dc = (k >> 2) & 1, (k >> 1) & 1, k & 1
    return (1 - x if dx else x, 1 - y if dy else y, 1 - c if dc else c)


def flat_index(pos):
    return 4 * pos[0] + 2 * pos[1] + pos[2]


def gather_shards(shards):
    n = len(shards)

    def body(*refs):
        x_refs, out_refs = refs[:n], refs[n:2 * n]
        send_sems, recv_sems, local_sems = refs[2 * n:]
        x, y, c = my_position()
        me, sibling = (x, y, c), (x, y, 1 - c)
        chips = [(1 - x, y), (x, 1 - y), (1 - x, 1 - y)]

        def copy(a, k, block, to, from_input=False):
            slot = out_refs[a].at[flat_index(block)]
            return pltpu.make_async_remote_copy(
                src_ref=x_refs[a] if from_input else slot, dst_ref=slot,
                send_sem=send_sems.at[7 * a + k], recv_sem=recv_sems.at[7 * a + k],
                device_id=to, device_id_type=MESH_IDS)

        mine = [pltpu.make_async_copy(x_refs[a], out_refs[a].at[flat_index(me)], local_sems.at[a]) for a in range(n)]
        for cp in mine:
            cp.start()
        first = []
        for a in range(n):
            first.append(copy(a, 0, me, sibling, from_input=True))
            first += [copy(a, 1 + j, me, (*chip, c), from_input=True) for j, chip in enumerate(chips)]
        for cp in first:
            cp.start()
        passed = []
        for j, chip in enumerate(chips):
            for a in range(n):
                copy(a, 1 + j, (*chip, c), me).wait_recv()
                cp = copy(a, 4 + j, (*chip, c), sibling)
                cp.start()
                passed.append(cp)
        for a in range(n):
            copy(a, 0, sibling, me).wait_recv()
            for j, chip in enumerate(chips):
                copy(a, 4 + j, (*chip, 1 - c), me).wait_recv()
        for cp in first + passed:
            cp.wait_send()
        for cp in mine:
            cp.wait()

    return pl.pallas_call(
        body, name="gather_shards",
        out_shape=[jax.ShapeDtypeStruct((N_DEV,) + s.shape, s.dtype) for s in shards],
        in_specs=[pl.BlockSpec(memory_space=pl.ANY)] * n, out_specs=[pl.BlockSpec(memory_space=pl.ANY)] * n,
        scratch_shapes=[pltpu.SemaphoreType.DMA((7 * n,)), pltpu.SemaphoreType.DMA((7 * n,)),
                        pltpu.SemaphoreType.DMA((n,))],
    )(*shards)


def ada_modulation(c_all, w_ada_loc, b_ada_blocks):
    cols = w_ada_loc.shape[1]

    def body(c_ref, w_ref, b_ref, out_ref, send_sems, recv_sems):
        me = my_position()
        mi = flat_index(me)
        cv = c_ref[...]
        res = hdot(cv * sigmoid(cv), w_ref[...]) + b_ref[pl.ds(mi, 1), :]
        out_ref[mi] = res
        sends = []
        for k in range(1, N_DEV):
            cp = pltpu.make_async_remote_copy(
                src_ref=out_ref.at[mi], dst_ref=out_ref.at[mi], send_sem=send_sems.at[k - 1],
                recv_sem=recv_sems.at[k - 1], device_id=flip(me, k), device_id_type=MESH_IDS)
            cp.start()
            sends.append(cp)
        for k in range(1, N_DEV):
            pi = flat_index(flip(me, k))
            pltpu.make_async_remote_copy(
                src_ref=out_ref.at[pi], dst_ref=out_ref.at[pi], send_sem=send_sems.at[k - 1],
                recv_sem=recv_sems.at[k - 1], device_id=flip(me, k), device_id_type=MESH_IDS).wait_recv()
        for cp in sends:
            cp.wait_send()

    return pl.pallas_call(
        body, name="ada_modulation",
        out_shape=jax.ShapeDtypeStruct((N_DEV, N_DEV, cols), F32),
        in_specs=[pl.BlockSpec(memory_space=pltpu.VMEM)] * 3, out_specs=pl.BlockSpec(memory_space=pltpu.VMEM),
        scratch_shapes=[pltpu.SemaphoreType.DMA((7,)), pltpu.SemaphoreType.DMA((7,))],
    )(c_all, w_ada_loc, b_ada_blocks)


def exchange_grads(blocks, small):
    n = len(blocks)

    def body(*refs):
        g_refs, s_ref = refs[:n], refs[n]
        rg_refs, rs_ref = refs[n + 1:2 * n + 1], refs[2 * n + 1]
        send_sems, recv_sems, local_sems = refs[2 * n + 2:]
        me = my_position()
        mi = flat_index(me)

        def copies(k, src_index, dst_index):
            peer = flip(me, k)
            out = [pltpu.make_async_remote_copy(
                src_ref=g_refs[a].at[src_index], dst_ref=rg_refs[a].at[dst_index],
                send_sem=send_sems.at[(n + 1) * (k - 1) + a], recv_sem=recv_sems.at[(n + 1) * (k - 1) + a],
                device_id=peer, device_id_type=MESH_IDS) for a in range(n)]
            out.append(pltpu.make_async_remote_copy(
                src_ref=s_ref, dst_ref=rs_ref.at[dst_index],
                send_sem=send_sems.at[(n + 1) * (k - 1) + n], recv_sem=recv_sems.at[(n + 1) * (k - 1) + n],
                device_id=peer, device_id_type=MESH_IDS))
            return out

        local = [pltpu.make_async_copy(g_refs[a].at[mi], rg_refs[a].at[mi], local_sems.at[a]) for a in range(n)]
        local.append(pltpu.make_async_copy(s_ref, rs_ref.at[mi], local_sems.at[n]))
        for cp in local:
            cp.start()
        sends = []
        for k in range(1, N_DEV):
            sends += copies(k, flat_index(flip(me, k)), mi)
        for cp in sends:
            cp.start()
        for k in range(1, N_DEV):
            pi = flat_index(flip(me, k))
            for cp in copies(k, pi, pi):
                cp.wait_recv()
        for cp in sends:
            cp.wait_send()
        for cp in local:
            cp.wait()

    n_sem = 7 * (n + 1)
    return pl.pallas_call(
        body, name="exchange_grads",
        out_shape=[jax.ShapeDtypeStruct(b.shape, b.dtype) for b in blocks]
        + [jax.ShapeDtypeStruct((N_DEV,) + small.shape, small.dtype)],
        in_specs=[pl.BlockSpec(memory_space=pl.ANY)] * (n + 1), out_specs=[pl.BlockSpec(memory_space=pl.ANY)] * (n + 1),
        scratch_shapes=[pltpu.SemaphoreType.DMA((n_sem,)), pltpu.SemaphoreType.DMA((n_sem,)),
                        pltpu.SemaphoreType.DMA((n + 1,))],
    )(*blocks, small)


def fwd_in_tile(step0, tile0, x, mod, w_in_p):
    xhat, _ = layer_norm_stats(x)
    h = xhat * (1.0 + mod[1:2]) + mod[0:1]
    return (mm(h, w_in_p),)


def rms_norm_fwd(x, g):
    r = lax.rsqrt(rowmean(x * x) + RMS_EPS)
    xh = x * r
    return xh * g, xh, r


def key_rope_mask(shape):
    return (lane_iota(shape) >= NOPE).astype(F32)


def mla_prep_tile(step0, tile0, q_c, kv_c, kr, krr, cos, sin, gq, gkv, wq, wqr, wkn, wv):
    qn, _, _ = rms_norm_fwd(q_c, gq)
    kvn, _, _ = rms_norm_fwd(kv_c, gkv)
    q = mm(qn, wq) * tile_lanes(cos, HEADS) + mm(qn, wqr) * tile_lanes(sin, HEADS)
    kpe = kr * (cos * key_rope_mask(cos.shape)) + krr * sin
    k = mm(kvn, wkn) + tile_lanes(kpe, HEADS)
    v = mm(kvn, wv)
    return q, k, v


def rwkv_prep_core(tile0, r0, k0, v0, l0, hr, hk, hv, hl, mu_r, mu_k, mu_v, mu_l, w0, a0, k_k, k_a,
                   w_dec, w_iclr, tril, same, bd):
    def shifted(x, halo, mu):
        row0 = jnp.where(tile0, 0.0, halo[7:8, :])
        prev = shift_rows_down(x, row0)
        return x + (prev - x) * mu, prev

    ur, pr = shifted(r0, hr, mu_r)
    uk, pk = shifted(k0, hk, mu_k)
    uv, pv = shifted(v0, hv, mu_v)
    ul, plo = shifted(l0, hl, mu_l)
    th = jnp.tanh(ul)
    sg = sigmoid(w0 + mm(th, w_dec))
    lw = -DECAY_SCALE * sg
    a_ic = sigmoid(a0 + mm(ul, w_iclr))
    kkraw = uk * k_k
    nrm_raw = jnp.sqrt(head_sum(kkraw * kkraw, bd))
    nrm = jnp.maximum(nrm_raw, 1e-12)
    kk = kkraw / nrm
    k2 = uk * (1.0 + (a_ic - 1.0) * k_a)
    lc = hdot(tril, lw)
    lcl = hdot(same, lw)
    return dict(ur=ur, uk=uk, uv=uv, ul=ul, pr=pr, pk=pk, pv=pv, pl=plo, th=th, sg=sg, lw=lw, a_ic=a_ic,
                kkraw=kkraw, nrm_raw=nrm_raw, nrm=nrm, kk=kk, k2=k2, lc=lc, lcl=lcl)


def rwkv_prep_tile(step0, tile0, r0, k0, v0, l0, hr, hk, hv, hl, *consts):
    f = rwkv_prep_core(tile0, r0, k0, v0, l0, hr, hk, hv, hl, *consts)
    lc, lw = f["lc"], f["lw"]
    e_neg = jnp.exp(-lc)
    rt = f["ur"] * jnp.exp(lc)
    at = -f["kk"] * jnp.exp(lc - lw)
    bt = f["kk"] * f["a_ic"] * e_neg
    kt = f["k2"] * e_neg
    return rt, at, bt, kt, jnp.exp(f["lcl"]), f["uv"], f["ur"], f["k2"]


def wkv_masks():
    lane = lane_iota((1, PAIR))
    m_lo = (lane < HEAD).astype(F32)
    ri = row_iota((CHUNK, CHUNK))
    ci = lane_iota((CHUNK, CHUNK))
    r2 = row_iota((PAIR, PAIR))
    c2 = lane_iota((PAIR, PAIR))
    bd = ((r2 < HEAD) == (c2 < HEAD)).astype(F32)
    eye2 = (r2 == c2).astype(F32)
    return (m_lo, 1.0 - m_lo), ri > ci, ri >= ci, (ri == ci).astype(F32), bd, eye2


def unit_lower_inverse(a, eye):
    t = eye + a
    p = a
    for _ in range(5):
        p = mm(p, p)
        t = t + mm(t, p)
    return t


def wkv_chunk_pre(at, bt, kt, rt, v, cl, masks):
    ms, strict, incl, eye, bd, eye2 = masks
    tinv, aak, prb, prk = [], [], [], []
    w = jnp.zeros_like(v)
    for m in ms:
        atm, rtm = at * m, rt * m
        aab = jnp.where(strict, mm_nt(atm, bt), 0.0)
        aak.append(jnp.where(strict, mm_nt(atm, kt), 0.0))
        prb.append(jnp.where(incl, mm_nt(rtm, bt), 0.0))
        prk.append(jnp.where(incl, mm_nt(rtm, kt), 0.0))
        tinv.append(unit_lower_inverse(aab, eye))
        w = w + mm(aak[-1], v * m)
    ah = mm(tinv[0], at * ms[0]) + mm(tinv[1], at * ms[1])
    wh = mm(tinv[0], w * ms[0]) + mm(tinv[1], w * ms[1])
    rh = rt
    yh = jnp.zeros_like(v)
    for i, m in enumerate(ms):
        rh = rh + mm(prb[i], ah * m)
        yh = yh + mm(prb[i], wh * m) + mm(prk[i], v * m)
    bc, kc = bt * cl, kt * cl
    g = eye2 * cl + bd * mm_tn(bc, ah)
    h = bd * (mm_tn(bc, wh) + mm_tn(kc, v))
    return g, h, rh, yh, (tinv, aak, prb, prk, ah, wh)


def wkv_chunk_grad(at, bt, kt, rt, v, cl, m0, dy, dm1, masks):
    ms, strict, incl, eye, bd, eye2 = masks
    _, _, _, _, (tinv, aak, prb, prk, ah, wh) = wkv_chunk_pre(at, bt, kt, rt, v, cl, masks)
    u = mm(ah, m0) + wh
    dm1 = dm1 * bd
    bc, kc = bt * cl, kt * cl
    du = mm(bc, dm1)
    dv = mm(kc, dm1)
    for i, m in enumerate(ms):
        dym = dy * m
        du = du + mm_tn(prb[i], dym)
        dv = dv + mm_tn(prk[i], dym)
    dz = mm_tn(tinv[0], du * ms[0]) + mm_tn(tinv[1], du * ms[1])
    for i, m in enumerate(ms):
        dv = dv + mm_tn(aak[i], dz * m)
    drt = mm_nt(dy, m0)
    dat = mm_nt(dz, m0)
    udm = mm_nt(u, dm1)
    vdm = mm_nt(v, dm1)
    dbt = udm * cl
    dkt = vdm * cl
    for m in ms:
        dzm, dym, atm, rtm = dz * m, dy * m, at * m, rt * m
        daab = jnp.where(strict, mm_nt(dzm, u), 0.0)
        daak = jnp.where(strict, mm_nt(dzm, v), 0.0)
        dprb = jnp.where(incl, mm_nt(dym, u), 0.0)
        dprk = jnp.where(incl, mm_nt(dym, v), 0.0)
        drt = drt + (mm(dprb, bt) + mm(dprk, kt)) * m
        dat = dat + (mm(daab, bt) + mm(daak, kt)) * m
        dbt = dbt + mm_tn(dprb, rtm) + mm_tn(daab, atm)
        dkt = dkt + mm_tn(dprk, rtm) + mm_tn(daak, atm)
    ones = jnp.ones((8, PAIR), F32)
    dlcl = hdot_nt(ones, dm1 * m0)[0:1, :] * cl + colsum(bc * udm + kc * vdm)
    g = drt * rt - dbt * bt - dkt * kt + dat * at
    upper = (lane_iota((CHUNK, CHUNK)) >= row_iota((CHUNK, CHUNK))).astype(F32)
    dlw = hdot(upper, g) - dat * at + dlcl
    return dat, dbt, dkt, drt, dv, dlw


def wkv_forward(at, bt, kt, rt, v, clf):
    n_rows = at.shape[0]
    cps = WKV_CHUNKS_PER_STEP
    rb = cps * CHUNK
    n_steps = n_rows // rb

    def body(a_ref, b_ref, k_ref, r_ref, v_ref, c_ref, y_ref, m0_ref, g_ref, rh_ref, m_scr):
        @pl.when(pl.program_id(1) == 0)
        def _():
            m_scr[...] = jnp.zeros_like(m_scr)

        masks = wkv_masks()
        pre = []
        for cc in range(cps):
            sl = slice(cc * CHUNK, (cc + 1) * CHUNK)
            pre.append(wkv_chunk_pre(a_ref[sl, :], b_ref[sl, :], k_ref[sl, :], r_ref[sl, :], v_ref[sl, :],
                                     c_ref[cc * CHUNK:cc * CHUNK + 1, :], masks)[:4])
        m = m_scr[...]
        for cc, (g, h, rh, yh) in enumerate(pre):
            sl = slice(cc * CHUNK, (cc + 1) * CHUNK)
            m0_ref[0, cc] = m
            g_ref[0, cc] = g
            rh_ref[sl, :] = rh
            y_ref[sl, :] = hdot(rh, m) + yh
            m = hdot(g, m) + h
        m_scr[...] = m

    blk = pl.BlockSpec((rb, PAIR), lambda p, s: (s, p))
    state_blk = pl.BlockSpec((1, cps, PAIR, PAIR), lambda p, s: (p, s, 0, 0))
    state_shape = jax.ShapeDtypeStruct((WIDTH // PAIR, n_rows // CHUNK, PAIR, PAIR), F32)
    return pl.pallas_call(
        body, name="wkv_forward", grid=(WIDTH // PAIR, n_steps),
        in_specs=[blk] * 6,
        out_specs=[blk, state_blk, state_blk, blk],
        out_shape=[jax.ShapeDtypeStruct((n_rows, WIDTH), F32), state_shape, state_shape,
                   jax.ShapeDtypeStruct((n_rows, WIDTH), F32)],
        scratch_shapes=[pltpu.VMEM((PAIR, PAIR), F32)],
        compiler_params=pltpu.CompilerParams(dimension_semantics=("arbitrary", "arbitrary"),
                                             vmem_limit_bytes=VMEM_LIMIT),
    )(at, bt, kt, rt, v, clf)


def wkv_backward(at, bt, kt, rt, v, clf, m0s, gs, rh, dy):
    n_rows = at.shape[0]
    cps = WKV_CHUNKS_PER_STEP
    rb = cps * CHUNK
    n_steps = n_rows // rb

    def body(a_ref, b_ref, k_ref, r_ref, v_ref, c_ref, m0_ref, g_ref, rh_ref, dy_ref,
             da_ref, db_ref, dk_ref, dr_ref, dv_ref, dlw_ref, dm_scr):
        @pl.when(pl.program_id(1) == 0)
        def _():
            dm_scr[...] = jnp.zeros_like(dm_scr)

        masks = wkv_masks()
        bd = masks[4]
        dm = dm_scr[...]
        dm1 = [None] * cps
        for cc in reversed(range(cps)):
            sl = slice(cc * CHUNK, (cc + 1) * CHUNK)
            dm1[cc] = dm
            dm = bd * (hdot_tn(g_ref[0, cc], dm) + hdot_tn(rh_ref[sl, :], dy_ref[sl, :]))
        dm_scr[...] = dm
        for cc in range(cps):
            sl = slice(cc * CHUNK, (cc + 1) * CHUNK)
            dat, dbt, dkt, drt, dv, dlw = wkv_chunk_grad(
                a_ref[sl, :], b_ref[sl, :], k_ref[sl, :], r_ref[sl, :], v_ref[sl, :],
                c_ref[cc * CHUNK:cc * CHUNK + 1, :], m0_ref[0, cc], dy_ref[sl, :], dm1[cc], masks)
            da_ref[sl, :] = dat
            db_ref[sl, :] = dbt
            dk_ref[sl, :] = dkt
            dr_ref[sl, :] = drt
            dv_ref[sl, :] = dv
            dlw_ref[sl, :] = dlw

    blk = pl.BlockSpec((rb, PAIR), lambda p, s: (n_steps - 1 - s, p))
    state_blk = pl.BlockSpec((1, cps, PAIR, PAIR), lambda p, s: (p, n_steps - 1 - s, 0, 0))
    return pl.pallas_call(
        body, name="wkv_backward", grid=(WIDTH // PAIR, n_steps),
        in_specs=[blk] * 6 + [state_blk, state_blk, blk, blk],
        out_specs=[blk] * 6,
        out_shape=[jax.ShapeDtypeStruct((n_rows, WIDTH), F32)] * 6,
        scratch_shapes=[pltpu.VMEM((PAIR, PAIR), F32)],
        compiler_params=pltpu.CompilerParams(dimension_semantics=("arbitrary", "arbitrary"),
                                             vmem_limit_bytes=VMEM_LIMIT),
    )(at, bt, kt, rt, v, clf, m0s, gs, rh, dy)


def visible(q_row0, k_row0, shape):
    qc = (q_row0 + row_iota(shape)) // CHUNK
    kc = (k_row0 + lane_iota(shape)) // CHUNK
    return kc <= qc


def attention_forward(q, k, v):
    n_rows = q.shape[0]
    tq = tk = ATTN_TILE
    n_q = n_rows // tq

    def body(q_ref, k_ref, v_ref, o_ref, lse_ref):
        i = pl.program_id(1)
        lane = lane_iota((tq, LANE))
        out = jnp.zeros((tq, LANE), F32)
        lse_out = jnp.zeros((tq, LANE), F32)
        for hh in range(2):
            cols = slice(hh * LANE, (hh + 1) * LANE)
            qh = q_ref[:, cols]

            def step(j, carry, cols=cols, qh=qh):
                m, l, acc = carry
                rows = pl.ds(pl.multiple_of(j * tk, tk), tk)
                s = mm_nt(qh, k_ref[rows, cols]) * ATTN_SCALE
                s = jnp.where(visible(i * tq, j * tk, s.shape), s, -jnp.inf)
                m_new = jnp.maximum(m, jnp.max(s, axis=-1, keepdims=True))
                p = jnp.exp(s - m_new)
                alpha = jnp.exp(m - m_new)
                l = alpha * l + jnp.sum(p, axis=-1, keepdims=True)
                acc = alpha * acc + mm(p, v_ref[rows, cols])
                return m_new, l, acc

            init = (jnp.full((tq, 1), -jnp.inf, F32), jnp.zeros((tq, 1), F32), jnp.zeros((tq, LANE), F32))
            m, l, acc = lax.fori_loop(0, i + 1, step, init)
            out = out + acc / l
            lse_out = jnp.where((lane >= HEAD) == (hh == 1), m + jnp.log(l), lse_out)
        o_ref[...] = out
        lse_ref[...] = lse_out

    return pl.pallas_call(
        body, name="attention_forward", grid=(HEADS // 2, n_q),
        in_specs=[pl.BlockSpec((tq, 2 * LANE), lambda p, i: (i, p)),
                  pl.BlockSpec((n_rows, 2 * LANE), lambda p, i: (0, p)),
                  pl.BlockSpec((n_rows, 2 * LANE), lambda p, i: (0, p))],
        out_specs=[pl.BlockSpec((tq, LANE), lambda p, i: (i, p))] * 2,
        out_shape=[jax.ShapeDtypeStruct((n_rows, WIDTH), F32)] * 2,
        compiler_params=pltpu.CompilerParams(dimension_semantics=("arbitrary", "arbitrary"),
                                             vmem_limit_bytes=VMEM_LIMIT),
    )(q, k, v)


def attention_backward(q, k, v, o, do, lse):
    n_rows = q.shape[0]
    tq = tk = ATTN_TILE
    n_q = n_rows // tq

    def body(q_ref, k_ref, v_ref, o_ref, do_ref, lse_ref, dq_ref, dk_ref, dv_ref):
        j = pl.program_id(1)

        @pl.when(j == 0)
        def _():
            dq_ref[...] = jnp.zeros_like(dq_ref)

        lane = lane_iota((tq, LANE))
        for hh in range(2):
            cols = slice(hh * LANE, (hh + 1) * LANE)
            kh = k_ref[:, cols]
            vh = v_ref[:, cols]
            head_lanes = ((lane >= HEAD) == (hh == 1)).astype(F32)

            def step(i, carry, cols=cols, kh=kh, vh=vh, head_lanes=head_lanes, hh=hh):
                dk, dv = carry
                rows = pl.ds(pl.multiple_of(i * tq, tq), tq)
                qh = q_ref[rows, cols]
                dout = do_ref[rows, :]
                delta = jnp.sum(dout * o_ref[rows, :] * head_lanes, axis=-1, keepdims=True)
                lse_h = jnp.sum(jnp.where(lane == hh * HEAD, lse_ref[rows, :], 0.0), axis=-1, keepdims=True)
                s = mm_nt(qh, kh) * ATTN_SCALE
                p = jnp.where(visible(i * tq, j * tk, s.shape), jnp.exp(s - lse_h), 0.0)
                dv = dv + mm_tn(p, dout)
                dp = mm_nt(dout, vh)
                ds = p * (dp - delta) * ATTN_SCALE
                dq_ref[rows, cols] += mm(ds, kh)
                dk = dk + mm_tn(ds, qh)
                return dk, dv

            dk, dv = lax.fori_loop(j, n_q, step, (jnp.zeros((tk, LANE), F32), jnp.zeros((tk, LANE), F32)))
            dk_ref[:, cols] = dk
            dv_ref[:, cols] = dv

    full = lambda w: pl.BlockSpec((n_rows, w), lambda p, j: (0, p))
    blk = pl.BlockSpec((tk, 2 * LANE), lambda p, j: (j, p))
    return pl.pallas_call(
        body, name="attention_backward", grid=(HEADS // 2, n_q),
        in_specs=[full(2 * LANE), blk, blk, full(LANE), full(LANE), full(LANE)],
        out_specs=[full(2 * LANE), blk, blk],
        out_shape=[jax.ShapeDtypeStruct((n_rows, HEADS * LANE), F32)] * 3,
        compiler_params=pltpu.CompilerParams(dimension_semantics=("arbitrary", "arbitrary"),
                                             vmem_limit_bytes=VMEM_LIMIT),
    )(q, k, v, o, do, lse)


def tail_tile(step0, tile0, x, tgt, ma, mb, gpa, gpb, ya, y, ur, k2, uv,
              mod, wpa, wpb, wout, gn_g, gn_b, r_k, post_g, post_b, bd):
    gate = mod[2:3]
    inv = 1.0 / HEAD
    yc = y - head_sum(y, bd) * inv
    rs = lax.rsqrt(head_sum(yc * yc, bd) * inv + GN_EPS)
    yn = yc * rs
    yb = yn * gn_g + gn_b + head_sum(ur * k2 * r_k, bd) * uv
    sga, sgb = sigmoid(gpa), sigmoid(gpb)
    sila, silb = gpa * sga, gpb * sgb
    ga, gb = ya * sila, yb * silb
    pa, pb = mm(ga, wpa), mm(gb, wpb)
    sa, sb = sigmoid(ma), sigmoid(mb)
    merged = sa * pa + sb * pb
    sub = mm(merged, wout)
    z = ALPHA * x + (1.0 + gate) * sub
    zhat, rstd = layer_norm_stats(z)
    err = zhat * post_g + post_b - tgt
    loss = 0.5 * jnp.sum(rowmean(err * err), axis=0, keepdims=True) + jnp.zeros((1, LANE), F32)
    dout = err * (1.0 / D_MODEL)
    dpost_g = colsum(dout * zhat)
    dpost_b = colsum(dout)
    dz = layer_norm_bwd(dout * post_g, zhat, rstd)
    dgate = colsum(dz * sub)
    dsub = dz * (1.0 + gate)
    dwout = mm_tn(merged, dsub)
    dmerged = mm_nt(dsub, wout)
    dpa, dpb = dmerged * sa, dmerged * sb
    dma = dmerged * pa * sa * (1.0 - sa)
    dmb = dmerged * pb * sb * (1.0 - sb)
    dwpa = mm_tn(ga, dpa)
    dwpb = mm_tn(gb, dpb)
    dga = mm_nt(dpa, wpa)
    dgb = mm_nt(dpb, wpb)
    dya = dga * sila
    dgpa = dga * ya * (sga * (1.0 + gpa * (1.0 - sga)))
    dyb = dgb * silb
    dgpb = dgb * yb * (sgb * (1.0 + gpb * (1.0 - sgb)))
    dgn_g = colsum(dyb * yn)
    dgn_b = colsum(dyb)
    dyn = dyb * gn_g
    dy = rs * (dyn - head_sum(dyn, bd) * inv - yn * head_sum(dyn * yn, bd) * inv)
    return (dz, dma, dmb, dgpa, dgpb, dya, dy, dyb,
            loss, dwout, dwpa, dwpb, dgn_g, dgn_b, dpost_g, dpost_b, dgate)


def mla_prep_bwd_tile(step0, tile0, q_c, kv_c, cos, sin, dq, dk, dv, gq, gkv, wq, wqr, wkn, wv):
    qn, qh, rq = rms_norm_fwd(q_c, gq)
    kvn, kvh, rkv = rms_norm_fwd(kv_c, gkv)
    dqc = dq * tile_lanes(cos, HEADS)
    dqs = dq * tile_lanes(sin, HEADS)
    dqn = mm_nt(dqc, wq) + mm_nt(dqs, wqr)
    dkvn = mm_nt(dk, wkn) + mm_nt(dv, wv)
    dkpe = dk[:, 0:LANE]
    for h in range(1, HEADS):
        dkpe = dkpe + dk[:, h * LANE:(h + 1) * LANE]
    dkr = dkpe * (cos * key_rope_mask(cos.shape))
    dkrr = dkpe * sin

    def rms_bwd(dyv, xh, r, g):
        dyg = dyv * g
        return r * (dyg - xh * rowmean(dyg * xh)), colsum(dyv * xh)

    dq_c, dgq = rms_bwd(dqn, qh, rq, gq)
    dkv_c, dgkv = rms_bwd(dkvn, kvh, rkv, gkv)
    return (dq_c, dkv_c, dkr, dkrr,
            mm_tn(qn, dqc), mm_tn(qn, dqs), mm_tn(kvn, dk), mm_tn(kvn, dv), dgq, dgkv)


def rwkv_prep_bwd_tile(step0, tile0, r0, k0, v0, l0, drt, dat, dbt, dkt, dvv, dlw, dyb, hr, hk, hv, hl,
                       mu_r, mu_k, mu_v, mu_l, w0, a0, k_k, k_a, w_dec, w_iclr, tril, same, bd, r_k,
                       cr, ck, cv, cl_):
    f = rwkv_prep_core(tile0, r0, k0, v0, l0, hr, hk, hv, hl, mu_r, mu_k, mu_v, mu_l, w0, a0, k_k, k_a,
                       w_dec, w_iclr, tril, same, bd)
    ur, uk, uv, ul, kk, k2, a_ic, sg, th = (f[n] for n in ("ur", "uk", "uv", "ul", "kk", "k2", "a_ic", "sg", "th"))
    lc, lw = f["lc"], f["lw"]
    e_neg = jnp.exp(-lc)
    dur = drt * jnp.exp(lc)
    da = dat * jnp.exp(lc - lw)
    db = dbt * e_neg
    dk2 = dkt * e_neg
    s = head_sum(ur * k2 * r_k, bd)
    duv = dvv + dyb * s
    ds = head_sum(dyb * uv, bd)
    dur = dur + ds * k2 * r_k
    dk2 = dk2 + ds * ur * r_k
    dr_k = colsum(ds * ur * k2)
    dkk = db * a_ic - da
    da_ic = db * kk + dk2 * uk * k_a
    duk = dk2 * (1.0 + (a_ic - 1.0) * k_a)
    dk_a = colsum(dk2 * uk * (a_ic - 1.0))
    dkkraw = jnp.where(f["nrm_raw"] > 1e-12, (dkk - kk * head_sum(dkk * kk, bd)) / f["nrm"], dkk * 1e12)
    duk = duk + dkkraw * k_k
    dk_k = colsum(dkkraw * uk)
    dai = da_ic * a_ic * (1.0 - a_ic)
    dd = dlw * (-DECAY_SCALE) * sg * (1.0 - sg)
    dul = mm_nt(dai, w_iclr) + mm_nt(dd, w_dec) * (1.0 - th * th)

    def unshift(du, x, prev, mu, carry_row):
        nxt = shift_rows_up(du, carry_row)
        return du * (1.0 - mu) + nxt * mu, colsum(du * (prev - x)), du[0:1, :]

    dr0, dmu_r, ncr = unshift(dur, r0, f["pr"], mu_r, cr)
    dk0, dmu_k, nck = unshift(duk, k0, f["pk"], mu_k, ck)
    dv0, dmu_v, ncv = unshift(duv, v0, f["pv"], mu_v, cv)
    dl0, dmu_l, ncl = unshift(dul, l0, f["pl"], mu_l, cl_)
    return (dr0, dk0, dv0, dl0,
            dmu_r, dmu_k, dmu_v, dmu_l, colsum(dd), colsum(dai), dk_k, dk_a, dr_k, mm_tn(th, dd), mm_tn(ul, dai),
            ncr, nck, ncv, ncl)


def in_bwd_tile(step0, tile0, x, dz, dma, dmb, dr0, dk0, dv0, dgpa, dgpb, dq_c, dkv_c, dkr, dkrr, dl0, mod, w_in_p):
    dproj = jnp.concatenate([dma, dmb, dr0, dk0, dv0, dgpa, dgpb, dq_c, dkv_c, dkr, dkrr, dl0], axis=1).astype(BF16)
    dh = mm_nt(dproj, w_in_p)
    xhat, rstd = layer_norm_stats(x)
    dx = layer_norm_bwd(dh * (1.0 + mod[1:2]), xhat, rstd) + ALPHA * dz
    return dx, dproj, colsum(dh), colsum(dh * xhat)


def in_weight_grad(x, mod, dproj):
    n_rows = x.shape[0]
    ts = ROW_TILE

    def body(x_ref, mod_ref, dp_ref, dw_ref):
        xhat, _ = layer_norm_stats(x_ref[...])
        h = xhat * (1.0 + mod_ref[1:2, :]) + mod_ref[0:1, :]
        contrib = mm_tn(h, dp_ref[...])

        @pl.when(pl.program_id(1) == 0)
        def _():
            dw_ref[...] = contrib

        @pl.when(pl.program_id(1) != 0)
        def _():
            dw_ref[...] += contrib

    return pl.pallas_call(
        body, name="in_weight_grad", grid=(P_WIDTH // DW_BLOCK, n_rows // ts),
        in_specs=[pl.BlockSpec((ts, D_MODEL), lambda j, i: (i, 0)), pl.BlockSpec(memory_space=pltpu.VMEM),
                  pl.BlockSpec((ts, DW_BLOCK), lambda j, i: (i, j))],
        out_specs=pl.BlockSpec((D_MODEL, DW_BLOCK), lambda j, i: (0, j)),
        out_shape=jax.ShapeDtypeStruct((D_MODEL, P_WIDTH), F32),
        compiler_params=pltpu.CompilerParams(dimension_semantics=("arbitrary", "arbitrary"),
                                             vmem_limit_bytes=VMEM_LIMIT),
    )(x, mod, dproj)


def ada_weight_grad(c_all, dmod_cols):
    def body(c_ref, d_ref, o_ref):
        cv = c_ref[...]
        o_ref[...] = hdot_tn(cv * sigmoid(cv), d_ref[...])

    return pl.pallas_call(
        body, name="ada_weight_grad",
        out_shape=jax.ShapeDtypeStruct((c_all.shape[1], dmod_cols.shape[1]), F32),
    )(c_all, dmod_cols)


def adamw(parts, w, m, v, name):
    k, rows, cols = parts.shape
    rb = 128 if rows % 128 == 0 else rows

    def body(p_ref, w_ref, m_ref, v_ref, g_ref, d_ref, nm_ref, nv_ref):
        g = p_ref[0].astype(F32)
        for i in range(1, k):
            g = g + p_ref[i].astype(F32)
        nm = ADAM_B1 * m_ref[...] + (1.0 - ADAM_B1) * g
        nv = ADAM_B2 * v_ref[...] + (1.0 - ADAM_B2) * (g * g)
        m_hat = nm / (1.0 - ADAM_B1 ** ADAM_STEP)
        v_hat = nv / (1.0 - ADAM_B2 ** ADAM_STEP)
        g_ref[...] = g
        d_ref[...] = -ADAM_LR * (m_hat / (jnp.sqrt(v_hat) + ADAM_EPS) + ADAM_WD * w_ref[...])
        nm_ref[...] = nm
        nv_ref[...] = nv

    blk = pl.BlockSpec((rb, cols), lambda i: (i, 0))
    return pl.pallas_call(
        body, name=name, grid=(rows // rb,),
        in_specs=[pl.BlockSpec((k, rb, cols), lambda i: (0, i, 0)), blk, blk, blk],
        out_specs=[blk] * 4, out_shape=[jax.ShapeDtypeStruct((rows, cols), F32)] * 4,
        compiler_params=pltpu.CompilerParams(dimension_semantics=("arbitrary",), vmem_limit_bytes=VMEM_LIMIT),
    )(parts, w, m, v)


def rot_cols(w):
    return jnp.concatenate([-w[:, ROPE // 2:], w[:, :ROPE // 2]], axis=1)


def unrot_cols(dw):
    return jnp.concatenate([dw[:, ROPE // 2:], -dw[:, :ROPE // 2]], axis=1)


def columns_from_shards(g, rows, cols):
    return g.reshape(N_DEV, rows, cols).transpose(1, 0, 2).reshape(rows, N_DEV * cols)


def shards_from_columns(w, rows, cols):
    return w.reshape(rows, N_DEV, cols).transpose(1, 0, 2).reshape(N_DEV, rows * cols)


def permute_w_in(w):
    z = lambda n: jnp.zeros((D_MODEL, n), w.dtype)
    krope = w[:, N_KROPE:N_KROPE + ROPE]
    rw = N_RWKV
    return jnp.concatenate([
        w[:, N_MA:N_MA + 1024], w[:, N_MB:N_MB + 1024],
        w[:, rw:rw + 512], w[:, rw + 512:rw + 1024], w[:, rw + 1024:rw + 1536],
        w[:, N_GPA:N_GPA + 512], w[:, N_GPB:N_GPB + 512],
        w[:, N_QC:N_QC + 256], w[:, N_KVC:N_KVC + 128],
        z(NOPE), krope, z(LANE - QK_DIM), z(NOPE), rot_cols(krope), z(LANE - QK_DIM),
        w[:, rw + 1536:rw + 1664]], axis=1)


def unpermute_w_in_grad(d):
    rw = P_R
    krope = d[:, P_KR + NOPE:P_KR + QK_DIM] + unrot_cols(d[:, P_KRR + NOPE:P_KRR + QK_DIM])
    return jnp.concatenate([
        d[:, P_QC:P_QC + 256], d[:, P_KVC:P_KVC + 128], krope, d[:, P_GPA:P_GPA + 512],
        d[:, rw:rw + 1536], d[:, P_LORA:P_LORA + 128], d[:, P_GPB:P_GPB + 512],
        d[:, P_MA:P_MA + 1024], d[:, P_MB:P_MB + 1024]], axis=1)


def pad_heads_q(w_uq):
    w = w_uq.reshape(Q_RANK, HEADS, QK_DIM)
    zpad = jnp.zeros((Q_RANK, HEADS, LANE - QK_DIM), w.dtype)
    wq = jnp.concatenate([w, zpad], axis=2).reshape(Q_RANK, HEADS * LANE)
    pe = w[:, :, NOPE:]
    rot = jnp.concatenate([-pe[:, :, ROPE // 2:], pe[:, :, :ROPE // 2]], axis=2)
    wqr = jnp.concatenate([jnp.zeros((Q_RANK, HEADS, NOPE), w.dtype), rot, zpad], axis=2).reshape(Q_RANK, HEADS * LANE)
    return wq, wqr


def unpad_heads_q_grad(dwq, dwqr):
    a = dwq.reshape(Q_RANK, HEADS, LANE)
    r = dwqr.reshape(Q_RANK, HEADS, LANE)[:, :, NOPE:QK_DIM]
    pe = a[:, :, NOPE:QK_DIM] + jnp.concatenate([r[:, :, ROPE // 2:], -r[:, :, :ROPE // 2]], axis=2)
    return jnp.concatenate([a[:, :, :NOPE], pe], axis=2).reshape(Q_RANK, HEADS * QK_DIM)


def pad_heads_kv(w_ukv):
    w = w_ukv.reshape(KV_RANK, HEADS, 2 * HEAD)
    z = jnp.zeros((KV_RANK, HEADS, HEAD), w.dtype)
    wkn = jnp.concatenate([w[:, :, :NOPE], z], axis=2).reshape(KV_RANK, HEADS * LANE)
    val = w[:, :, NOPE:]
    odd = (jnp.arange(HEADS) % 2 == 1)[None, :, None]
    wv = jnp.concatenate([jnp.where(odd, 0, val), jnp.where(odd, val, 0)], axis=2).reshape(KV_RANK, HEADS * LANE)
    return wkn, wv


def unpad_heads_kv_grad(dwkn, dwv):
    a = dwkn.reshape(KV_RANK, HEADS, LANE)[:, :, :NOPE]
    b = dwv.reshape(KV_RANK, HEADS, LANE)
    odd = (jnp.arange(HEADS) % 2 == 1)[None, :, None]
    val = jnp.where(odd, b[:, :, HEAD:], b[:, :, :HEAD])
    return jnp.concatenate([a, val], axis=2).reshape(KV_RANK, HEADS * 2 * HEAD)


def kernel(x, c, positions, w_ada, b_ada, w_in, q_norm_g, w_uq, kv_norm_g, w_ukv, mu_rwkv, w0, w_decay_up, a0, w_iclr_up, k_k, k_a, r_k, gn_g, gn_b, w_proj_a, w_proj_b, w_out, post_g, post_b, loss_target, m_w_ada, m_b_ada, m_w_in, m_q_norm_g, m_w_uq, m_kv_norm_g, m_w_ukv, m_mu_rwkv, m_w0, m_w_decay_up, m_a0, m_w_iclr_up, m_k_k, m_k_a, m_r_k, m_gn_g, m_gn_b, m_w_proj_a, m_w_proj_b, m_w_out, m_post_g, m_post_b, v_w_ada, v_b_ada, v_w_in, v_q_norm_g, v_w_uq, v_kv_norm_g, v_w_ukv, v_mu_rwkv, v_w0, v_w_decay_up, v_a0, v_w_iclr_up, v_k_k, v_k_a, v_r_k, v_gn_g, v_gn_b, v_w_proj_a, v_w_proj_b, v_w_out, v_post_g, v_post_b):
    weights = dict(w_ada=w_ada, b_ada=b_ada, w_in=w_in, q_norm_g=q_norm_g, w_uq=w_uq, kv_norm_g=kv_norm_g,
                   w_ukv=w_ukv, mu_rwkv=mu_rwkv, w0=w0, w_decay_up=w_decay_up, a0=a0, w_iclr_up=w_iclr_up,
                   k_k=k_k, k_a=k_a, r_k=r_k, gn_g=gn_g, gn_b=gn_b, w_proj_a=w_proj_a, w_proj_b=w_proj_b,
                   w_out=w_out, post_g=post_g, post_b=post_b)
    mom1 = dict(w_ada=m_w_ada, b_ada=m_b_ada, w_in=m_w_in, q_norm_g=m_q_norm_g, w_uq=m_w_uq, kv_norm_g=m_kv_norm_g,
                w_ukv=m_w_ukv, mu_rwkv=m_mu_rwkv, w0=m_w0, w_decay_up=m_w_decay_up, a0=m_a0, w_iclr_up=m_w_iclr_up,
                k_k=m_k_k, k_a=m_k_a, r_k=m_r_k, gn_g=m_gn_g, gn_b=m_gn_b, w_proj_a=m_w_proj_a, w_proj_b=m_w_proj_b,
                w_out=m_w_out, post_g=m_post_g, post_b=m_post_b)
    mom2 = dict(w_ada=v_w_ada, b_ada=v_b_ada, w_in=v_w_in, q_norm_g=v_q_norm_g, w_uq=v_w_uq, kv_norm_g=v_kv_norm_g,
                w_ukv=v_w_ukv, mu_rwkv=v_mu_rwkv, w0=v_w0, w_decay_up=v_w_decay_up, a0=v_a0, w_iclr_up=v_w_iclr_up,
                k_k=v_k_k, k_a=v_k_a, r_k=v_r_k, gn_g=v_gn_g, gn_b=v_gn_b, w_proj_a=v_w_proj_a, w_proj_b=v_w_proj_b,
                w_out=v_w_out, post_g=v_post_g, post_b=v_post_b)
    names = list(weights)
    n_rows = x.shape[1]
    me = 4 * lax.axis_index("x") + 2 * lax.axis_index("y") + lax.axis_index("c")
    xr = x[0]
    tgt = loss_target[0]
    row = lambda a: a.reshape(1, -1)

    gathered = gather_shards([weights[n][0].astype(BF16) for n, _, _ in SHARDED] + [c])
    c_all = gathered[-1].reshape(N_DEV, D_MODEL)
    full = {}
    for (n, r, cdim), part in zip(SHARDED, gathered):
        full[n] = part.reshape(N_DEV * r, cdim) if n == "w_out" else columns_from_shards(part, r, cdim)
    w_in_p = permute_w_in(full["w_in"])
    wq, wqr = pad_heads_q(full["w_uq"])
    wkn, wv = pad_heads_kv(full["w_ukv"])
    zl = jnp.zeros((LORA, WIDTH), BF16)
    w_dec = jnp.concatenate([full["w_decay_up"], zl], axis=0)
    w_iclr = jnp.concatenate([zl, full["w_iclr_up"]], axis=0)
    wpa, wpb, wout = full["w_proj_a"], full["w_proj_b"], full["w_out"]

    mod_all = ada_modulation(c_all, w_ada[0], b_ada.reshape(N_DEV, -1))
    mod = lax.dynamic_index_in_dim(mod_all, me, axis=1, keepdims=False).reshape(3, D_MODEL)

    (proj,) = row_call("fwd_in", fwd_in_tile, n_rows, [(xr, D_MODEL, 0)], [mod, w_in_p], [(P_WIDTH, F32)])
    pcol = lambda off_, w: (proj, w, off_ // w)

    inv_freq = ROPE_THETA ** (-jnp.arange(0, ROPE, 2, dtype=F32) / ROPE)
    ang = positions[0].astype(F32)[:, None] * inv_freq
    ones_n, zeros_n, zeros_p = jnp.ones((n_rows, NOPE), F32), jnp.zeros((n_rows, NOPE), F32), jnp.zeros((n_rows, LANE - QK_DIM), F32)
    cos_t = jnp.concatenate([ones_n, jnp.cos(ang), jnp.cos(ang), zeros_p], axis=1)
    sin_t = jnp.concatenate([zeros_n, jnp.sin(ang), jnp.sin(ang), zeros_p], axis=1)

    gq, gkv = q_norm_g, kv_norm_g
    mla_consts = [gq, gkv, wq, wqr, wkn, wv]
    q, k, v = row_call(
        "mla_prep", mla_prep_tile, n_rows,
        [pcol(P_QC, 256), pcol(P_KVC, 128), pcol(P_KR, 128), pcol(P_KRR, 128), (cos_t, LANE, 0), (sin_t, LANE, 0)],
        mla_consts, [(HEADS * LANE, BF16)] * 3)
    ya, lse = attention_forward(q, k, v)

    t_idx = jnp.arange(ROW_TILE)
    same_chunk = (t_idx[:, None] // CHUNK) == (t_idx[None, :] // CHUNK)
    same = same_chunk.astype(F32)
    tril = (same_chunk & (t_idx[:, None] >= t_idx[None, :])).astype(F32)
    l_idx = jnp.arange(LANE)
    bd = ((l_idx[:, None] // HEAD) == (l_idx[None, :] // HEAD)).astype(F32)
    mu = mu_rwkv
    mu_r, mu_k, mu_v, mu_l = mu[:, 0:512], mu[:, 512:1024], mu[:, 1024:1536], mu[:, 1536:1664]
    rk_row = row(r_k)
    rwkv_consts = [mu_r, mu_k, mu_v, mu_l, w0, a0, k_k, k_a, w_dec, w_iclr, tril, same, bd]
    rwkv_rows = [pcol(P_R, 512), pcol(P_K, 512), pcol(P_V, 512), pcol(P_LORA, 128)]
    rt, at, bt, kt, clf, uv, ur, k2 = row_call(
        "rwkv_prep", rwkv_prep_tile, n_rows, rwkv_rows, rwkv_consts, [(WIDTH, F32)] * 8, halo_in=rwkv_rows)
    y, m0s, state_maps, out_maps = wkv_forward(at, bt, kt, rt, uv, clf)

    tail = row_call(
        "tail", tail_tile, n_rows,
        [(xr, D_MODEL, 0), (tgt, D_MODEL, 0), pcol(P_MA, 1024), pcol(P_MB, 1024), pcol(P_GPA, 512), pcol(P_GPB, 512),
         (ya, WIDTH, 0), (y, WIDTH, 0), (ur, WIDTH, 0), (k2, WIDTH, 0), (uv, WIDTH, 0)],
        [mod, wpa, wpb, wout, gn_g, gn_b, rk_row, post_g, post_b, bd],
        [(D_MODEL, F32), (1024, F32), (1024, F32), (512, F32), (512, F32), (WIDTH, F32), (WIDTH, F32), (WIDTH, F32)],
        acc_out=[((1, LANE), F32), ((D_MODEL, D_MODEL), F32), ((WIDTH, D_MODEL), F32), ((WIDTH, D_MODEL), F32),
                 ((1, WIDTH), F32), ((1, WIDTH), F32), ((1, D_MODEL), F32), ((1, D_MODEL), F32), ((1, D_MODEL), F32)])
    (dz, dma, dmb, dgpa, dgpb, dya, dy, dyb,
     loss_row, g_wout, g_wpa, g_wpb, g_gn_g, g_gn_b, g_post_g, g_post_b, dgate) = tail
    loss = lax.psum(loss_row[0, 0], ("x", "y", "c"))

    dq, dk, dv = attention_backward(q, k, v, ya, dya, lse)
    dq_c, dkv_c, dkr, dkrr, g_wq, g_wqr, g_wkn, g_wv, g_gq, g_gkv = row_call(
        "mla_prep_bwd", mla_prep_bwd_tile, n_rows,
        [pcol(P_QC, 256), pcol(P_KVC, 128), (cos_t, LANE, 0), (sin_t, LANE, 0),
         (dq, HEADS * LANE, 0), (dk, HEADS * LANE, 0), (dv, HEADS * LANE, 0)],
        mla_consts, [(256, F32), (128, F32), (128, F32), (128, F32)],
        acc_out=[((Q_RANK, HEADS * LANE), F32)] * 2 + [((KV_RANK, HEADS * LANE), F32)] * 2
        + [((1, Q_RANK), F32), ((1, KV_RANK), F32)])

    dat, dbt, dkt, drt, dvv, dlw = wkv_backward(at, bt, kt, rt, uv, clf, m0s, state_maps, out_maps, dy)
    (dr0, dk0, dv0, dl0, g_mu_r, g_mu_k, g_mu_v, g_mu_l, g_w0, g_a0, g_k_k, g_k_a, g_r_k, g_wdec, g_wiclr) = row_call(
        "rwkv_prep_bwd", rwkv_prep_bwd_tile, n_rows,
        rwkv_rows + [(drt, WIDTH, 0), (dat, WIDTH, 0), (dbt, WIDTH, 0), (dkt, WIDTH, 0), (dvv, WIDTH, 0),
                     (dlw, WIDTH, 0), (dyb, WIDTH, 0)],
        rwkv_consts + [rk_row], [(512, F32), (512, F32), (512, F32), (128, F32)],
        acc_out=[((1, 512), F32)] * 3 + [((1, 128), F32)] + [((1, 512), F32)] * 5 + [((LANE, WIDTH), F32)] * 2,
        halo_in=rwkv_rows, carry=[512, 512, 512, 128], reverse=True)

    dx, dproj, dshift, dscale = row_call(
        "in_bwd", in_bwd_tile, n_rows,
        [(xr, D_MODEL, 0), (dz, D_MODEL, 0), (dma, 1024, 0), (dmb, 1024, 0), (dr0, 512, 0), (dk0, 512, 0), (dv0, 512, 0),
         (dgpa, 512, 0), (dgpb, 512, 0), (dq_c, 256, 0), (dkv_c, 128, 0), (dkr, 128, 0), (dkrr, 128, 0), (dl0, 128, 0)],
        [mod, w_in_p], [(D_MODEL, F32), (P_WIDTH, BF16)], acc_out=[((1, D_MODEL), F32)] * 2)
    g_w_in_p = in_weight_grad(xr, mod, dproj)

    grads_full = {
        "w_in": unpermute_w_in_grad(g_w_in_p), "w_uq": unpad_heads_q_grad(g_wq, g_wqr),
        "w_ukv": unpad_heads_kv_grad(g_wkn, g_wv), "w_decay_up": g_wdec[:LORA], "w_iclr_up": g_wiclr[LORA:],
        "w_proj_a": g_wpa, "w_proj_b": g_wpb, "w_out": g_wout}
    blocks = [(grads_full[n].reshape(N_DEV, r, cdim) if n == "w_out"
               else grads_full[n].reshape(r, N_DEV, cdim).transpose(1, 0, 2)).astype(BF16) for n, r, cdim in SHARDED]
    dmod = jnp.concatenate([dshift, dscale, dgate], axis=1)
    small = jnp.concatenate([dmod, g_gq, g_gkv, g_mu_r, g_mu_k, g_mu_v, g_mu_l, g_w0, g_a0, g_k_k, g_k_a, g_r_k,
                             g_gn_g, g_gn_b, g_post_g, g_post_b], axis=1)
    *got_blocks, got_small = exchange_grads(blocks, small)

    ada_cols = w_ada.shape[2]
    dmod_all = got_small.reshape(N_DEV, SMALL_ELEMS)[:, :3 * D_MODEL]
    g_ada = ada_weight_grad(c_all, lax.dynamic_slice_in_dim(dmod_all, me * ada_cols, ada_cols, axis=1))

    def small_row(tree):
        return jnp.concatenate([tree[n].reshape(1, -1) for n, _ in SMALL], axis=1)

    outs = [dict() for _ in range(4)]
    res = adamw(g_ada[None], w_ada[0], m_w_ada[0], v_w_ada[0], "adamw_w_ada")
    for kind in range(4):
        outs[kind]["w_ada"] = res[kind][None]
    for (n, r, cdim), got in zip(SHARDED, got_blocks):
        res = adamw(got, weights[n][0], mom1[n][0], mom2[n][0], "adamw_" + n)
        for kind in range(4):
            outs[kind][n] = res[kind][None]
    res = adamw(got_small, small_row(weights), small_row(mom1), small_row(mom2), "adamw_small")
    for kind in range(4):
        off = 0
        for n, size in SMALL:
            outs[kind][n] = res[kind][:, off:off + size].reshape(weights[n].shape)
            off += size
    return (loss, dx[None], *[outs[0][n] for n in names], *[outs[1][n] for n in names],
            *[outs[2][n] for n in names], *[outs[3][n] for n in names])
```

```python
import functools
import math

import jax
import jax.numpy as jnp
from jax import lax
from jax.experimental import pallas as pl
from jax.experimental.pallas import tpu as pltpu

F32 = jnp.float32
BF16 = jnp.bfloat16
HIGHEST = lax.Precision.HIGHEST
MESH_IDS = pl.DeviceIdType.MESH

N_DEV = 8
D_MODEL = 1024
LN_EPS = 1e-5
RMS_EPS = 1e-6
GN_EPS = 64e-5
HEADS = 8
Q_RANK = 256
KV_RANK = 128
ROPE = 32
NOPE = 64
QK_DIM = NOPE + ROPE
WIDTH = 512
HEAD = 64
LORA = 64
CHUNK = 64
DEPTH = 1
ALPHA = (2.0 * DEPTH) ** 0.25
ROPE_THETA = 10000.0
ATTN_SCALE = QK_DIM ** -0.5
DECAY_SCALE = math.exp(-0.5)

ADAM_LR = 0.001
ADAM_B1 = 0.9
ADAM_B2 = 0.999
ADAM_EPS = 1e-08
ADAM_WD = 0.01
ADAM_STEP = 10

LANE = 128
PAIR = 2 * HEAD
ROW_TILE = 256
ATTN_TILE = 256
WKV_CHUNKS_PER_STEP = 8
VMEM_LIMIT = 56 * 1024 * 1024

P_MA, P_MB, P_R, P_K, P_V, P_GPA, P_GPB, P_QC, P_KVC, P_KR, P_KRR, P_LORA = (
    0, 1024, 2048, 2560, 3072, 3584, 4096, 4608, 4864, 4992, 5120, 5248)
P_WIDTH = 5376
DW_BLOCK = 768

N_QC, N_KVC, N_KROPE, N_GPA, N_RWKV, N_GPB, N_MA, N_MB = 0, 256, 384, 416, 928, 2592, 3104, 4128
IN_WIDTH = 5152

SHARDED = (("w_in", 1024, 644), ("w_uq", 256, 96), ("w_ukv", 128, 128), ("w_decay_up", 64, 64),
           ("w_iclr_up", 64, 64), ("w_proj_a", 512, 128), ("w_proj_b", 512, 128), ("w_out", 128, 1024))
SHARD_ELEMS = sum(r * c for _, r, c in SHARDED)
SHARD_ROWS = SHARD_ELEMS // LANE
GATHER_ROWS = SHARD_ROWS + 2 * D_MODEL // LANE
SMALL = (("b_ada", 3072), ("q_norm_g", 256), ("kv_norm_g", 128), ("mu_rwkv", 1664), ("w0", 512), ("a0", 512),
         ("k_k", 512), ("k_a", 512), ("r_k", 512), ("gn_g", 512), ("gn_b", 512), ("post_g", 1024), ("post_b", 1024))
SMALL_ELEMS = sum(n for _, n in SMALL)
SMALL_ROWS = SMALL_ELEMS // LANE


def mm(a, b):
    return jnp.dot(a.astype(BF16), b.astype(BF16), preferred_element_type=F32)


def mm_nt(a, b):
    return lax.dot_general(a.astype(BF16), b.astype(BF16), (((1,), (1,)), ((), ())), preferred_element_type=F32)


def mm_tn(a, b):
    return lax.dot_general(a.astype(BF16), b.astype(BF16), (((0,), (0,)), ((), ())), preferred_element_type=F32)


def hdot(a, b):
    return jnp.dot(a, b, precision=HIGHEST, preferred_element_type=F32)


def hdot_nt(a, b):
    return lax.dot_general(a, b, (((1,), (1,)), ((), ())), precision=HIGHEST, preferred_element_type=F32)


def hdot_tn(a, b):
    return lax.dot_general(a, b, (((0,), (0,)), ((), ())), precision=HIGHEST, preferred_element_type=F32)


def sigmoid(x):
    return 1.0 / (1.0 + jnp.exp(-x))


def colsum(x):
    return jnp.sum(x, axis=0, keepdims=True)


def rowmean(x):
    return jnp.mean(x, axis=-1, keepdims=True)


def layer_norm_stats(x):
    xc = x - rowmean(x)
    rstd = lax.rsqrt(rowmean(xc * xc) + LN_EPS)
    return xc * rstd, rstd


def layer_norm_bwd(dy, xhat, rstd):
    return rstd * (dy - rowmean(dy) - xhat * rowmean(dy * xhat))


def head_sum(x, bd):
    return jnp.concatenate([hdot(x[:, p * LANE:(p + 1) * LANE], bd) for p in range(x.shape[1] // LANE)], axis=1)


def tile_lanes(t, n):
    return jnp.concatenate([t] * n, axis=1)


def row_iota(shape):
    return lax.broadcasted_iota(jnp.int32, shape, 0)


def lane_iota(shape):
    return lax.broadcasted_iota(jnp.int32, shape, 1)


def shift_rows_down(x, row0):
    rolled = pltpu.roll(x, 1, axis=0)
    return jnp.where(row_iota(x.shape) == 0, row0, rolled)


def shift_rows_up(x, row_last):
    rolled = pltpu.roll(x, x.shape[0] - 1, axis=0)
    return jnp.where(row_iota(x.shape) == x.shape[0] - 1, row_last, rolled)


def row_call(name, fn, n_rows, row_in, const_in, row_out, acc_out=(), halo_in=(), carry=(), reverse=False):
    ts = ROW_TILE
    n_tiles = n_rows // ts
    n_in = len(row_in) + len(halo_in) + len(const_in)
    n_ro, n_ao = len(row_out), len(acc_out)

    def tile_of(g):
        return (n_tiles - 1 - g) if reverse else g

    def body(*refs):
        ins = refs[:n_in]
        ro = refs[n_in:n_in + n_ro]
        ao = refs[n_in + n_ro:n_in + n_ro + n_ao]
        cr = refs[n_in + n_ro + n_ao:]
        g = pl.program_id(0)
        step0 = g == 0
        tile0 = tile_of(g) == 0
        for r in cr:
            @pl.when(step0)
            def _(r=r):
                r[...] = jnp.zeros_like(r)
        vals = [r[...] for r in ins]
        outs = fn(step0, tile0, *vals, *[c[0:1, :] for c in cr])
        for r, v in zip(ro, outs[:n_ro]):
            r[...] = v.astype(r.dtype)
        for r, v in zip(ao, outs[n_ro:n_ro + n_ao]):
            @pl.when(step0)
            def _(r=r, v=v):
                r[...] = v.astype(r.dtype)

            @pl.when(jnp.logical_not(step0))
            def _(r=r, v=v):
                r[...] += v.astype(r.dtype)
        for r, v in zip(cr, outs[n_ro + n_ao:]):
            r[0:1, :] = v

    in_specs = [pl.BlockSpec((ts, w), functools.partial(lambda g, cb: (tile_of(g), cb), cb=cb)) for _, w, cb in row_in]
    in_specs += [pl.BlockSpec((8, w), functools.partial(
        lambda g, cb: (jnp.maximum(tile_of(g) * (ts // 8) - 1, 0), cb), cb=cb)) for _, w, cb in halo_in]
    in_specs += [pl.BlockSpec(memory_space=pltpu.VMEM) for _ in const_in]
    out_specs = [pl.BlockSpec((ts, w), lambda g: (tile_of(g), 0)) for w, _ in row_out]
    out_specs += [pl.BlockSpec(s, lambda g: (0, 0)) for s, _ in acc_out]
    out_shape = [jax.ShapeDtypeStruct((n_rows, w), d) for w, d in row_out]
    out_shape += [jax.ShapeDtypeStruct(s, d) for s, d in acc_out]
    return pl.pallas_call(
        body, name=name, grid=(n_tiles,), in_specs=in_specs, out_specs=out_specs, out_shape=out_shape,
        scratch_shapes=[pltpu.VMEM((8, w), F32) for w in carry],
        compiler_params=pltpu.CompilerParams(dimension_semantics=("arbitrary",), vmem_limit_bytes=VMEM_LIMIT),
    )(*[a for a, _, _ in row_in], *[a for a, _, _ in halo_in], *const_in)


def my_position():
    return lax.axis_index("x"), lax.axis_index("y"), lax.axis_index("c")


def flip(pos, k):
    x, y, c = pos
    dx, dy, dc = (k >> 2) & 1, (k >> 1) & 1, k & 1
    return (1 - x if dx else x, 1 - y if dy else y, 1 - c if dc else c)


def flat_index(pos):
    return 4 * pos[0] + 2 * pos[1] + pos[2]


def gather_shards(shards):
    n = len(shards)

    def body(*refs):
        x_refs, out_refs = refs[:n], refs[n:2 * n]
        send_sems, recv_sems, local_sems = refs[2 * n:]
        x, y, c = my_position()
        me, sibling = (x, y, c), (x, y, 1 - c)
        chips = [(1 - x, y), (x, 1 - y), (1 - x, 1 - y)]

        def copy(a, k, block, to, from_input=False):
            slot = out_refs[a].at[flat_index(block)]
            return pltpu.make_async_remote_copy(
                src_ref=x_refs[a] if from_input else slot, dst_ref=slot,
                send_sem=send_sems.at[7 * a + k], recv_sem=recv_sems.at[7 * a + k],
                device_id=to, device_id_type=MESH_IDS)

        mine = [pltpu.make_async_copy(x_refs[a], out_refs[a].at[flat_index(me)], local_sems.at[a]) for a in range(n)]
        for cp in mine:
            cp.start()
        first = []
        for a in range(n):
            first.append(copy(a, 0, me, sibling, from_input=True))
            first += [copy(a, 1 + j, me, (*chip, c), from_input=True) for j, chip in enumerate(chips)]
        for cp in first:
            cp.start()
        passed = []
        for j, chip in enumerate(chips):
            for a in range(n):
                copy(a, 1 + j, (*chip, c), me).wait_recv()
                cp = copy(a, 4 + j, (*chip, c), sibling)
                cp.start()
                passed.append(cp)
        for a in range(n):
            copy(a, 0, sibling, me).wait_recv()
            for j, chip in enumerate(chips):
                copy(a, 4 + j, (*chip, 1 - c), me).wait_recv()
        for cp in first + passed:
            cp.wait_send()
        for cp in mine:
            cp.wait()

    return pl.pallas_call(
        body, name="gather_shards",
        out_shape=[jax.ShapeDtypeStruct((N_DEV,) + s.shape, s.dtype) for s in shards],
        in_specs=[pl.BlockSpec(memory_space=pl.ANY)] * n, out_specs=[pl.BlockSpec(memory_space=pl.ANY)] * n,
        scratch_shapes=[pltpu.SemaphoreType.DMA((7 * n,)), pltpu.SemaphoreType.DMA((7 * n,)),
                        pltpu.SemaphoreType.DMA((n,))],
    )(*shards)


def ada_modulation(c_all, w_ada_loc, b_ada_blocks):
    cols = w_ada_loc.shape[1]

    def body(c_ref, w_ref, b_ref, out_ref, send_sems, recv_sems):
        me = my_position()
        mi = flat_index(me)
        cv = c_ref[...]
        res = hdot(cv * sigmoid(cv), w_ref[...]) + b_ref[pl.ds(mi, 1), :]
        out_ref[mi] = res
        sends = []
        for k in range(1, N_DEV):
            cp = pltpu.make_async_remote_copy(
                src_ref=out_ref.at[mi], dst_ref=out_ref.at[mi], send_sem=send_sems.at[k - 1],
                recv_sem=recv_sems.at[k - 1], device_id=flip(me, k), device_id_type=MESH_IDS)
            cp.start()
            sends.append(cp)
        for k in range(1, N_DEV):
            pi = flat_index(flip(me, k))
            pltpu.make_async_remote_copy(
                src_ref=out_ref.at[pi], dst_ref=out_ref.at[pi], send_sem=send_sems.at[k - 1],
                recv_sem=recv_sems.at[k - 1], device_id=flip(me, k), device_id_type=MESH_IDS).wait_recv()
        for cp in sends:
            cp.wait_send()

    return pl.pallas_call(
        body, name="ada_modulation",
        out_shape=jax.ShapeDtypeStruct((N_DEV, N_DEV, cols), F32),
        in_specs=[pl.BlockSpec(memory_space=pltpu.VMEM)] * 3, out_specs=pl.BlockSpec(memory_space=pltpu.VMEM),
        scratch_shapes=[pltpu.SemaphoreType.DMA((7,)), pltpu.SemaphoreType.DMA((7,))],
    )(c_all, w_ada_loc, b_ada_blocks)


def exchange_grads(blocks, small):
    n = len(blocks)

    def body(*refs):
        g_refs, s_ref = refs[:n], refs[n]
        rg_refs, rs_ref = refs[n + 1:2 * n + 1], refs[2 * n + 1]
        send_sems, recv_sems, local_sems = refs[2 * n + 2:]
        me = my_position()
        mi = flat_index(me)

        def copies(k, src_index, dst_index):
            peer = flip(me, k)
            out = [pltpu.make_async_remote_copy(
                src_ref=g_refs[a].at[src_index], dst_ref=rg_refs[a].at[dst_index],
                send_sem=send_sems.at[(n + 1) * (k - 1) + a], recv_sem=recv_sems.at[(n + 1) * (k - 1) + a],
                device_id=peer, device_id_type=MESH_IDS) for a in range(n)]
            out.append(pltpu.make_async_remote_copy(
                src_ref=s_ref, dst_ref=rs_ref.at[dst_index],
                send_sem=send_sems.at[(n + 1) * (k - 1) + n], recv_sem=recv_sems.at[(n + 1) * (k - 1) + n],
                device_id=peer, device_id_type=MESH_IDS))
            return out

        local = [pltpu.make_async_copy(g_refs[a].at[mi], rg_refs[a].at[mi], local_sems.at[a]) for a in range(n)]
        local.append(pltpu.make_async_copy(s_ref, rs_ref.at[mi], local_sems.at[n]))
        for cp in local:
            cp.start()
        sends = []
        for k in range(1, N_DEV):
            sends += copies(k, flat_index(flip(me, k)), mi)
        for cp in sends:
            cp.start()
        for k in range(1, N_DEV):
            pi = flat_index(flip(me, k))
            for cp in copies(k, pi, pi):
                cp.wait_recv()
        for cp in sends:
            cp.wait_send()
        for cp in local:
            cp.wait()

    n_sem = 7 * (n + 1)
    return pl.pallas_call(
        body, name="exchange_grads",
        out_shape=[jax.ShapeDtypeStruct(b.shape, b.dtype) for b in blocks]
        + [jax.ShapeDtypeStruct((N_DEV,) + small.shape, small.dtype)],
        in_specs=[pl.BlockSpec(memory_space=pl.ANY)] * (n + 1), out_specs=[pl.BlockSpec(memory_space=pl.ANY)] * (n + 1),
        scratch_shapes=[pltpu.SemaphoreType.DMA((n_sem,)), pltpu.SemaphoreType.DMA((n_sem,)),
                        pltpu.SemaphoreType.DMA((n + 1,))],
    )(*blocks, small)


def fwd_in_tile(step0, tile0, x, mod, w_in_p):
    xhat, _ = layer_norm_stats(x)
    h = xhat * (1.0 + mod[1:2]) + mod[0:1]
    return (mm(h, w_in_p),)


def rms_norm_fwd(x, g):
    r = lax.rsqrt(rowmean(x * x) + RMS_EPS)
    xh = x * r
    return xh * g, xh, r


def key_rope_mask(shape):
    return (lane_iota(shape) >= NOPE).astype(F32)


def mla_prep_tile(step0, tile0, q_c, kv_c, kr, krr, cos, sin, gq, gkv, wq, wqr, wkn, wv):
    qn, _, _ = rms_norm_fwd(q_c, gq)
    kvn, _, _ = rms_norm_fwd(kv_c, gkv)
    q = mm(qn, wq) * tile_lanes(cos, HEADS) + mm(qn, wqr) * tile_lanes(sin, HEADS)
    kpe = kr * (cos * key_rope_mask(cos.shape)) + krr * sin
    k = mm(kvn, wkn) + tile_lanes(kpe, HEADS)
    v = mm(kvn, wv)
    return q, k, v


def rwkv_prep_core(tile0, r0, k0, v0, l0, hr, hk, hv, hl, mu_r, mu_k, mu_v, mu_l, w0, a0, k_k, k_a,
                   w_dec, w_iclr, tril, same, bd):
    def shifted(x, halo, mu):
        row0 = jnp.where(tile0, 0.0, halo[7:8, :])
        prev = shift_rows_down(x, row0)
        return x + (prev - x) * mu, prev

    ur, pr = shifted(r0, hr, mu_r)
    uk, pk = shifted(k0, hk, mu_k)
    uv, pv = shifted(v0, hv, mu_v)
    ul, plo = shifted(l0, hl, mu_l)
    th = jnp.tanh(ul)
    sg = sigmoid(w0 + mm(th, w_dec))
    lw = -DECAY_SCALE * sg
    a_ic = sigmoid(a0 + mm(ul, w_iclr))
    kkraw = uk * k_k
    nrm_raw = jnp.sqrt(head_sum(kkraw * kkraw, bd))
    nrm = jnp.maximum(nrm_raw, 1e-12)
    kk = kkraw / nrm
    k2 = uk * (1.0 + (a_ic - 1.0) * k_a)
    lc = hdot(tril, lw)
    lcl = hdot(same, lw)
    return dict(ur=ur, uk=uk, uv=uv, ul=ul, pr=pr, pk=pk, pv=pv, pl=plo, th=th, sg=sg, lw=lw, a_ic=a_ic,
                kkraw=kkraw, nrm_raw=nrm_raw, nrm=nrm, kk=kk, k2=k2, lc=lc, lcl=lcl)


def rwkv_prep_tile(step0, tile0, r0, k0, v0, l0, hr, hk, hv, hl, *consts):
    f = rwkv_prep_core(tile0, r0, k0, v0, l0, hr, hk, hv, hl, *consts)
    lc, lw = f["lc"], f["lw"]
    e_neg = jnp.exp(-lc)
    rt = f["ur"] * jnp.exp(lc)
    at = -f["kk"] * jnp.exp(lc - lw)
    bt = f["kk"] * f["a_ic"] * e_neg
    kt = f["k2"] * e_neg
    return rt, at, bt, kt, jnp.exp(f["lcl"]), f["uv"], f["ur"], f["k2"]


def wkv_masks():
    lane = lane_iota((1, PAIR))
    m_lo = (lane < HEAD).astype(F32)
    ri = row_iota((CHUNK, CHUNK))
    ci = lane_iota((CHUNK, CHUNK))
    r2 = row_iota((PAIR, PAIR))
    c2 = lane_iota((PAIR, PAIR))
    bd = ((r2 < HEAD) == (c2 < HEAD)).astype(F32)
    eye2 = (r2 == c2).astype(F32)
    return (m_lo, 1.0 - m_lo), ri > ci, ri >= ci, (ri == ci).astype(F32), bd, eye2


def wkv_chunks_pre(chunks, masks):
    ms, strict, incl, eye, bd, eye2 = masks
    items = [(c, m) for c in range(len(chunks)) for m in ms]
    at, bt, kt, rt, v, cl = (list(t) for t in zip(*chunks))
    atm = [at[c] * m for c, m in items]
    rtm = [rt[c] * m for c, m in items]
    aab = [jnp.where(strict, mm_nt(x, bt[c]), 0.0) for x, (c, _) in zip(atm, items)]
    aak = [jnp.where(strict, mm_nt(x, kt[c]), 0.0) for x, (c, _) in zip(atm, items)]
    prb = [jnp.where(incl, mm_nt(x, bt[c]), 0.0) for x, (c, _) in zip(rtm, items)]
    prk = [jnp.where(incl, mm_nt(x, kt[c]), 0.0) for x, (c, _) in zip(rtm, items)]
    tinv = [eye + a for a in aab]
    power = aab
    for _ in range(5):
        power = [mm(p, p) for p in power]
        tinv = [t + mm(t, p) for t, p in zip(tinv, power)]

    def by_chunk(parts):
        return [parts[2 * c] + parts[2 * c + 1] for c in range(len(chunks))]

    w = by_chunk([mm(a, v[c] * m) for a, (c, m) in zip(aak, items)])
    ah = by_chunk([mm(t, x) for t, x in zip(tinv, atm)])
    wh = by_chunk([mm(t, w[c] * m) for t, (c, m) in zip(tinv, items)])
    rh = [r + d for r, d in zip(rt, by_chunk([mm(p, ah[c] * m) for p, (c, m) in zip(prb, items)]))]
    yh = by_chunk([mm(p, wh[c] * m) + mm(q, v[c] * m) for p, q, (c, m) in zip(prb, prk, items)])
    bc = [b * c_ for b, c_ in zip(bt, cl)]
    kc = [k * c_ for k, c_ in zip(kt, cl)]
    g = [eye2 * c_ + bd * mm_tn(b, a) for c_, b, a in zip(cl, bc, ah)]
    h = [bd * (mm_tn(b, w_) + mm_tn(k, v_)) for b, w_, k, v_ in zip(bc, wh, kc, v)]
    return g, h, rh, yh, (items, atm, rtm, tinv, aak, prb, prk, ah, wh, bc, kc)


def wkv_chunks_grad(chunks, m0, dy, dm1, masks):
    ms, strict, incl, eye, bd, eye2 = masks
    n = len(chunks)
    _, _, _, _, (items, atm, rtm, tinv, aak, prb, prk, ah, wh, bc, kc) = wkv_chunks_pre(chunks, masks)
    at, bt, kt, rt, v, cl = (list(t) for t in zip(*chunks))

    def by_chunk(parts):
        return [parts[2 * c] + parts[2 * c + 1] for c in range(n)]

    u = [mm(a, m) + w for a, m, w in zip(ah, m0, wh)]
    dm1 = [d * bd for d in dm1]
    dym = [dy[c] * m for c, m in items]
    du = [mm(b, d) + e for b, d, e in zip(bc, dm1, by_chunk([mm_tn(p, x) for p, x in zip(prb, dym)]))]
    dv = [mm(k, d) + e for k, d, e in zip(kc, dm1, by_chunk([mm_tn(p, x) for p, x in zip(prk, dym)]))]
    dz = by_chunk([mm_tn(t, du[c] * m) for t, (c, m) in zip(tinv, items)])
    dzm = [dz[c] * m for c, m in items]
    dv = [a + b for a, b in zip(dv, by_chunk([mm_tn(a_, x) for a_, x in zip(aak, dzm)]))]
    drt = [mm_nt(d, m) for d, m in zip(dy, m0)]
    dat = [mm_nt(d, m) for d, m in zip(dz, m0)]
    udm = [mm_nt(x, d) for x, d in zip(u, dm1)]
    vdm = [mm_nt(x, d) for x, d in zip(v, dm1)]
    daab = [jnp.where(strict, mm_nt(x, u[c]), 0.0) for x, (c, _) in zip(dzm, items)]
    daak = [jnp.where(strict, mm_nt(x, v[c]), 0.0) for x, (c, _) in zip(dzm, items)]
    dprb = [jnp.where(incl, mm_nt(x, u[c]), 0.0) for x, (c, _) in zip(dym, items)]
    dprk = [jnp.where(incl, mm_nt(x, v[c]), 0.0) for x, (c, _) in zip(dym, items)]
    drt2 = by_chunk([(mm(p, bt[c]) + mm(q, kt[c])) * m for p, q, (c, m) in zip(dprb, dprk, items)])
    dat2 = by_chunk([(mm(p, bt[c]) + mm(q, kt[c])) * m for p, q, (c, m) in zip(daab, daak, items)])
    dbt2 = by_chunk([mm_tn(p, r) + mm_tn(a_, x) for p, r, a_, x in zip(dprb, rtm, daab, atm)])
    dkt2 = by_chunk([mm_tn(p, r) + mm_tn(a_, x) for p, r, a_, x in zip(dprk, rtm, daak, atm)])
    ones = jnp.ones((8, PAIR), F32)
    upper = (lane_iota((CHUNK, CHUNK)) >= row_iota((CHUNK, CHUNK))).astype(F32)
    out = []
    for c in range(n):
        drt_c = drt[c] + drt2[c]
        dat_c = dat[c] + dat2[c]
        dbt_c = udm[c] * cl[c] + dbt2[c]
        dkt_c = vdm[c] * cl[c] + dkt2[c]
        dlcl = hdot_nt(ones, dm1[c] * m0[c])[0:1, :] * cl[c] + colsum(bc[c] * udm[c] + kc[c] * vdm[c])
        g = drt_c * rt[c] - dbt_c * bt[c] - dkt_c * kt[c] + dat_c * at[c]
        dlw = hdot(upper, g) - dat_c * at[c] + dlcl
        out.append((dat_c, dbt_c, dkt_c, drt_c, dv[c], dlw))
    return out


def wkv_forward(at, bt, kt, rt, v, clf):
    n_rows = at.shape[0]
    cps = WKV_CHUNKS_PER_STEP
    rb = cps * CHUNK
    n_steps = n_rows // rb

    def body(a_ref, b_ref, k_ref, r_ref, v_ref, c_ref, y_ref, m0_ref, g_ref, rh_ref, m_scr):
        @pl.when(pl.program_id(1) == 0)
        def _():
            m_scr[...] = jnp.zeros_like(m_scr)

        masks = wkv_masks()
        chunks = []
        for cc in range(cps):
            sl = slice(cc * CHUNK, (cc + 1) * CHUNK)
            chunks.append((a_ref[sl, :], b_ref[sl, :], k_ref[sl, :], r_ref[sl, :], v_ref[sl, :],
                           c_ref[cc * CHUNK:cc * CHUNK + 1, :]))
        pre = wkv_chunks_pre(chunks, masks)[:4]
        m = m_scr[...]
        for cc, (g, h, rh, yh) in enumerate(zip(*pre)):
            sl = slice(cc * CHUNK, (cc + 1) * CHUNK)
            m0_ref[0, cc] = m
            g_ref[0, cc] = g
            rh_ref[sl, :] = rh
            y_ref[sl, :] = hdot(rh, m) + yh
            m = hdot(g, m) + h
        m_scr[...] = m

    blk = pl.BlockSpec((rb, PAIR), lambda p, s: (s, p))
    state_blk = pl.BlockSpec((1, cps, PAIR, PAIR), lambda p, s: (p, s, 0, 0))
    state_shape = jax.ShapeDtypeStruct((WIDTH // PAIR, n_rows // CHUNK, PAIR, PAIR), F32)
    return pl.pallas_call(
        body, name="wkv_forward", grid=(WIDTH // PAIR, n_steps),
        in_specs=[blk] * 6,
        out_specs=[blk, state_blk, state_blk, blk],
        out_shape=[jax.ShapeDtypeStruct((n_rows, WIDTH), F32), state_shape, state_shape,
                   jax.ShapeDtypeStruct((n_rows, WIDTH), F32)],
        scratch_shapes=[pltpu.VMEM((PAIR, PAIR), F32)],
        compiler_params=pltpu.CompilerParams(dimension_semantics=("arbitrary", "arbitrary"),
                                             vmem_limit_bytes=VMEM_LIMIT),
    )(at, bt, kt, rt, v, clf)


def wkv_backward(at, bt, kt, rt, v, clf, m0s, gs, rh, dy):
    n_rows = at.shape[0]
    cps = WKV_CHUNKS_PER_STEP
    rb = cps * CHUNK
    n_steps = n_rows // rb

    def body(a_ref, b_ref, k_ref, r_ref, v_ref, c_ref, m0_ref, g_ref, rh_ref, dy_ref,
             da_ref, db_ref, dk_ref, dr_ref, dv_ref, dlw_ref, dm_scr):
        @pl.when(pl.program_id(1) == 0)
        def _():
            dm_scr[...] = jnp.zeros_like(dm_scr)

        masks = wkv_masks()
        bd = masks[4]
        dm = dm_scr[...]
        dm1 = [None] * cps
        for cc in reversed(range(cps)):
            sl = slice(cc * CHUNK, (cc + 1) * CHUNK)
            dm1[cc] = dm
            dm = bd * (hdot_tn(g_ref[0, cc], dm) + hdot_tn(rh_ref[sl, :], dy_ref[sl, :]))
        dm_scr[...] = dm
        chunks, m0, dys = [], [], []
        for cc in range(cps):
            sl = slice(cc * CHUNK, (cc + 1) * CHUNK)
            chunks.append((a_ref[sl, :], b_ref[sl, :], k_ref[sl, :], r_ref[sl, :], v_ref[sl, :],
                           c_ref[cc * CHUNK:cc * CHUNK + 1, :]))
            m0.append(m0_ref[0, cc])
            dys.append(dy_ref[sl, :])
        grads = wkv_chunks_grad(chunks, m0, dys, dm1, masks)
        for cc, (dat, dbt, dkt, drt, dv, dlw) in enumerate(grads):
            sl = slice(cc * CHUNK, (cc + 1) * CHUNK)
            da_ref[sl, :] = dat
            db_ref[sl, :] = dbt
            dk_ref[sl, :] = dkt
            dr_ref[sl, :] = drt
            dv_ref[sl, :] = dv
            dlw_ref[sl, :] = dlw

    blk = pl.BlockSpec((rb, PAIR), lambda p, s: (n_steps - 1 - s, p))
    state_blk = pl.BlockSpec((1, cps, PAIR, PAIR), lambda p, s: (p, n_steps - 1 - s, 0, 0))
    return pl.pallas_call(
        body, name="wkv_backward", grid=(WIDTH // PAIR, n_steps),
        in_specs=[blk] * 6 + [state_blk, state_blk, blk, blk],
        out_specs=[blk] * 6,
        out_shape=[jax.ShapeDtypeStruct((n_rows, WIDTH), F32)] * 6,
        scratch_shapes=[pltpu.VMEM((PAIR, PAIR), F32)],
        compiler_params=pltpu.CompilerParams(dimension_semantics=("arbitrary", "arbitrary"),
                                             vmem_limit_bytes=VMEM_LIMIT),
    )(at, bt, kt, rt, v, clf, m0s, gs, rh, dy)


def visible(q_row0, k_row0, shape):
    qc = (q_row0 + row_iota(shape)) // CHUNK
    kc = (k_row0 + lane_iota(shape)) // CHUNK
    return kc <= qc


def attention_forward(q, k, v):
    n_rows = q.shape[0]
    tq = tk = ATTN_TILE
    n_q = n_rows // tq

    def body(q_ref, k_ref, v_ref, o_ref, lse_ref):
        i = pl.program_id(1)
        lane = lane_iota((tq, LANE))
        heads = [slice(0, LANE), slice(LANE, 2 * LANE)]
        qs = [q_ref[:, cols] for cols in heads]

        def step(j, carry, masked):
            rows = pl.ds(pl.multiple_of(j * tk, tk), tk)
            ss = [mm_nt(qh, k_ref[rows, cols]) * ATTN_SCALE for qh, cols in zip(qs, heads)]
            if masked:
                vis = visible(i * tq, j * tk, ss[0].shape)
                ss = [jnp.where(vis, s, -jnp.inf) for s in ss]
            ps, stats = [], []
            for s, (m, l, _) in zip(ss, carry):
                m_new = jnp.maximum(m, jnp.max(s, axis=-1, keepdims=True))
                p = jnp.exp(s - m_new)
                alpha = jnp.exp(m - m_new)
                ps.append(p)
                stats.append((m_new, alpha, alpha * l + jnp.sum(p, axis=-1, keepdims=True)))
            pvs = [mm(p, v_ref[rows, cols]) for p, cols in zip(ps, heads)]
            return tuple((m_new, l, alpha * acc + pv)
                         for (m_new, alpha, l), (_, _, acc), pv in zip(stats, carry, pvs))

        init = tuple((jnp.full((tq, 1), -jnp.inf, F32), jnp.zeros((tq, 1), F32), jnp.zeros((tq, LANE), F32))
                     for _ in heads)
        carry = lax.fori_loop(0, i, functools.partial(step, masked=False), init)
        (m0, l0, acc0), (m1, l1, acc1) = step(i, carry, masked=True)
        o_ref[...] = acc0 / l0 + acc1 / l1
        lse_ref[...] = jnp.where(lane >= HEAD, m1 + jnp.log(l1), m0 + jnp.log(l0))

    return pl.pallas_call(
        body, name="attention_forward", grid=(HEADS // 2, n_q),
        in_specs=[pl.BlockSpec((tq, 2 * LANE), lambda p, i: (i, p)),
                  pl.BlockSpec((n_rows, 2 * LANE), lambda p, i: (0, p)),
                  pl.BlockSpec((n_rows, 2 * LANE), lambda p, i: (0, p))],
        out_specs=[pl.BlockSpec((tq, LANE), lambda p, i: (i, p))] * 2,
        out_shape=[jax.ShapeDtypeStruct((n_rows, WIDTH), F32)] * 2,
        compiler_params=pltpu.CompilerParams(dimension_semantics=("arbitrary", "arbitrary"),
                                             vmem_limit_bytes=VMEM_LIMIT),
    )(q, k, v)


def attention_backward(q, k, v, o, do, lse):
    n_rows = q.shape[0]
    tq = tk = ATTN_TILE
    n_q = n_rows // tq

    def body(q_ref, k_ref, v_ref, o_ref, do_ref, lse_ref, dq_ref, dk_ref, dv_ref):
        j = pl.program_id(1)

        @pl.when(j == 0)
        def _():
            dq_ref[...] = jnp.zeros_like(dq_ref)

        lane = lane_iota((tq, LANE))
        heads = [slice(0, LANE), slice(LANE, 2 * LANE)]
        ks = [k_ref[:, cols] for cols in heads]
        vs = [v_ref[:, cols] for cols in heads]
        head_lanes = [(lane < HEAD).astype(F32), (lane >= HEAD).astype(F32)]

        def step(i, carry, masked):
            rows = pl.ds(pl.multiple_of(i * tq, tq), tq)
            qs = [q_ref[rows, cols] for cols in heads]
            dout = do_ref[rows, :]
            dout_o = dout * o_ref[rows, :]
            lse_t = lse_ref[rows, :]
            ss = [mm_nt(qh, kh) * ATTN_SCALE for qh, kh in zip(qs, ks)]
            dps = [mm_nt(dout, vh) for vh in vs]
            ps, dss = [], []
            for hh in range(2):
                delta = jnp.sum(dout_o * head_lanes[hh], axis=-1, keepdims=True)
                lse_h = jnp.sum(jnp.where(lane == hh * HEAD, lse_t, 0.0), axis=-1, keepdims=True)
                p = jnp.exp(ss[hh] - lse_h)
                if masked:
                    p = jnp.where(visible(i * tq, j * tk, p.shape), p, 0.0)
                ps.append(p)
                dss.append(p * (dps[hh] - delta) * ATTN_SCALE)
            dvs = [mm_tn(p, dout) for p in ps]
            dqs = [mm(ds, kh) for ds, kh in zip(dss, ks)]
            dks = [mm_tn(ds, qh) for ds, qh in zip(dss, qs)]
            for cols, dq in zip(heads, dqs):
                dq_ref[rows, cols] += dq
            return tuple((dk + a, dv + b) for (dk, dv), a, b in zip(carry, dks, dvs))

        init = tuple((jnp.zeros((tk, LANE), F32), jnp.zeros((tk, LANE), F32)) for _ in heads)
        carry = step(j, init, masked=True)
        carry = lax.fori_loop(j + 1, n_q, functools.partial(step, masked=False), carry)
        for cols, (dk, dv) in zip(heads, carry):
            dk_ref[:, cols] = dk
            dv_ref[:, cols] = dv

    full = lambda w: pl.BlockSpec((n_rows, w), lambda p, j: (0, p))
    blk = pl.BlockSpec((tk, 2 * LANE), lambda p, j: (j, p))
    return pl.pallas_call(
        body, name="attention_backward", grid=(HEADS // 2, n_q),
        in_specs=[full(2 * LANE), blk, blk, full(LANE), full(LANE), full(LANE)],
        out_specs=[full(2 * LANE), blk, blk],
        out_shape=[jax.ShapeDtypeStruct((n_rows, HEADS * LANE), F32)] * 3,
        compiler_params=pltpu.CompilerParams(dimension_semantics=("arbitrary", "arbitrary"),
                                             vmem_limit_bytes=VMEM_LIMIT),
    )(q, k, v, o, do, lse)


def tail_tile(step0, tile0, x, tgt, ma, mb, gpa, gpb, ya, y, ur, k2, uv,
              mod, wpa, wpb, wout, gn_g, gn_b, r_k, post_g, post_b, bd):
    gate = mod[2:3]
    inv = 1.0 / HEAD
    yc = y - head_sum(y, bd) * inv
    rs = lax.rsqrt(head_sum(yc * yc, bd) * inv + GN_EPS)
    yn = yc * rs
    yb = yn * gn_g + gn_b + head_sum(ur * k2 * r_k, bd) * uv
    sga, sgb = sigmoid(gpa), sigmoid(gpb)
    sila, silb = gpa * sga, gpb * sgb
    ga, gb = ya * sila, yb * silb
    pa, pb = mm(ga, wpa), mm(gb, wpb)
    sa, sb = sigmoid(ma), sigmoid(mb)
    merged = sa * pa + sb * pb
    sub = mm(merged, wout)
    z = ALPHA * x + (1.0 + gate) * sub
    zhat, rstd = layer_norm_stats(z)
    err = zhat * post_g + post_b - tgt
    loss = 0.5 * jnp.sum(rowmean(err * err), axis=0, keepdims=True) + jnp.zeros((1, LANE), F32)
    dout = err * (1.0 / D_MODEL)
    dpost_g = colsum(dout * zhat)
    dpost_b = colsum(dout)
    dz = layer_norm_bwd(dout * post_g, zhat, rstd)
    dgate = colsum(dz * sub)
    dsub = dz * (1.0 + gate)
    dwout = mm_tn(merged, dsub)
    dmerged = mm_nt(dsub, wout)
    dpa, dpb = dmerged * sa, dmerged * sb
    dma = dmerged * pa * sa * (1.0 - sa)
    dmb = dmerged * pb * sb * (1.0 - sb)
    dwpa = mm_tn(ga, dpa)
    dwpb = mm_tn(gb, dpb)
    dga = mm_nt(dpa, wpa)
    dgb = mm_nt(dpb, wpb)
    dya = dga * sila
    dgpa = dga * ya * (sga * (1.0 + gpa * (1.0 - sga)))
    dyb = dgb * silb
    dgpb = dgb * yb * (sgb * (1.0 + gpb * (1.0 - sgb)))
    dgn_g = colsum(dyb * yn)
    dgn_b = colsum(dyb)
    dyn = dyb * gn_g
    dy = rs * (dyn - head_sum(dyn, bd) * inv - yn * head_sum(dyn * yn, bd) * inv)
    return (dz, dma, dmb, dgpa, dgpb, dya, dy, dyb,
            loss, dwout, dwpa, dwpb, dgn_g, dgn_b, dpost_g, dpost_b, dgate)


def mla_prep_bwd_tile(step0, tile0, q_c, kv_c, cos, sin, dq, dk, dv, gq, gkv, wq, wqr, wkn, wv):
    qn, qh, rq = rms_norm_fwd(q_c, gq)
    kvn, kvh, rkv = rms_norm_fwd(kv_c, gkv)
    dqc = dq * tile_lanes(cos, HEADS)
    dqs = dq * tile_lanes(sin, HEADS)
    dqn = mm_nt(dqc, wq) + mm_nt(dqs, wqr)
    dkvn = mm_nt(dk, wkn) + mm_nt(dv, wv)
    dkpe = dk[:, 0:LANE]
    for h in range(1, HEADS):
        dkpe = dkpe + dk[:, h * LANE:(h + 1) * LANE]
    dkr = dkpe * (cos * key_rope_mask(cos.shape))
    dkrr = dkpe * sin

    def rms_bwd(dyv, xh, r, g):
        dyg = dyv * g
        return r * (dyg - xh * rowmean(dyg * xh)), colsum(dyv * xh)

    dq_c, dgq = rms_bwd(dqn, qh, rq, gq)
    dkv_c, dgkv = rms_bwd(dkvn, kvh, rkv, gkv)
    return (dq_c, dkv_c, dkr, dkrr,
            mm_tn(qn, dqc), mm_tn(qn, dqs), mm_tn(kvn, dk), mm_tn(kvn, dv), dgq, dgkv)


def rwkv_prep_bwd_tile(step0, tile0, r0, k0, v0, l0, drt, dat, dbt, dkt, dvv, dlw, dyb, hr, hk, hv, hl,
                       mu_r, mu_k, mu_v, mu_l, w0, a0, k_k, k_a, w_dec, w_iclr, tril, same, bd, r_k,
                       cr, ck, cv, cl_):
    f = rwkv_prep_core(tile0, r0, k0, v0, l0, hr, hk, hv, hl, mu_r, mu_k, mu_v, mu_l, w0, a0, k_k, k_a,
                       w_dec, w_iclr, tril, same, bd)
    ur, uk, uv, ul, kk, k2, a_ic, sg, th = (f[n] for n in ("ur", "uk", "uv", "ul", "kk", "k2", "a_ic", "sg", "th"))
    lc, lw = f["lc"], f["lw"]
    e_neg = jnp.exp(-lc)
    dur = drt * jnp.exp(lc)
    da = dat * jnp.exp(lc - lw)
    db = dbt * e_neg
    dk2 = dkt * e_neg
    s = head_sum(ur * k2 * r_k, bd)
    duv = dvv + dyb * s
    ds = head_sum(dyb * uv, bd)
    dur = dur + ds * k2 * r_k
    dk2 = dk2 + ds * ur * r_k
    dr_k = colsum(ds * ur * k2)
    dkk = db * a_ic - da
    da_ic = db * kk + dk2 * uk * k_a
    duk = dk2 * (1.0 + (a_ic - 1.0) * k_a)
    dk_a = colsum(dk2 * uk * (a_ic - 1.0))
    dkkraw = jnp.where(f["nrm_raw"] > 1e-12, (dkk - kk * head_sum(dkk * kk, bd)) / f["nrm"], dkk * 1e12)
    duk = duk + dkkraw * k_k
    dk_k = colsum(dkkraw * uk)
    dai = da_ic * a_ic * (1.0 - a_ic)
    dd = dlw * (-DECAY_SCALE) * sg * (1.0 - sg)
    dul = mm_nt(dai, w_iclr) + mm_nt(dd, w_dec) * (1.0 - th * th)

    def unshift(du, x, prev, mu, carry_row):
        nxt = shift_rows_up(du, carry_row)
        return du * (1.0 - mu) + nxt * mu, colsum(du * (prev - x)), du[0:1, :]

    dr0, dmu_r, ncr = unshift(dur, r0, f["pr"], mu_r, cr)
    dk0, dmu_k, nck = unshift(duk, k0, f["pk"], mu_k, ck)
    dv0, dmu_v, ncv = unshift(duv, v0, f["pv"], mu_v, cv)
    dl0, dmu_l, ncl = unshift(dul, l0, f["pl"], mu_l, cl_)
    return (dr0, dk0, dv0, dl0,
            dmu_r, dmu_k, dmu_v, dmu_l, colsum(dd), colsum(dai), dk_k, dk_a, dr_k, mm_tn(th, dd), mm_tn(ul, dai),
            ncr, nck, ncv, ncl)


def in_bwd_tile(step0, tile0, x, dz, dma, dmb, dr0, dk0, dv0, dgpa, dgpb, dq_c, dkv_c, dkr, dkrr, dl0, mod, w_in_p):
    dproj = jnp.concatenate([dma, dmb, dr0, dk0, dv0, dgpa, dgpb, dq_c, dkv_c, dkr, dkrr, dl0], axis=1).astype(BF16)
    dh = mm_nt(dproj, w_in_p)
    xhat, rstd = layer_norm_stats(x)
    dx = layer_norm_bwd(dh * (1.0 + mod[1:2]), xhat, rstd) + ALPHA * dz
    return dx, dproj, colsum(dh), colsum(dh * xhat)


def in_weight_grad(x, mod, dproj):
    n_rows = x.shape[0]
    ts = ROW_TILE

    def body(x_ref, mod_ref, dp_ref, dw_ref):
        xhat, _ = layer_norm_stats(x_ref[...])
        h = xhat * (1.0 + mod_ref[1:2, :]) + mod_ref[0:1, :]
        contrib = mm_tn(h, dp_ref[...])

        @pl.when(pl.program_id(1) == 0)
        def _():
            dw_ref[...] = contrib

        @pl.when(pl.program_id(1) != 0)
        def _():
            dw_ref[...] += contrib

    return pl.pallas_call(
        body, name="in_weight_grad", grid=(P_WIDTH // DW_BLOCK, n_rows // ts),
        in_specs=[pl.BlockSpec((ts, D_MODEL), lambda j, i: (i, 0)), pl.BlockSpec(memory_space=pltpu.VMEM),
                  pl.BlockSpec((ts, DW_BLOCK), lambda j, i: (i, j))],
        out_specs=pl.BlockSpec((D_MODEL, DW_BLOCK), lambda j, i: (0, j)),
        out_shape=jax.ShapeDtypeStruct((D_MODEL, P_WIDTH), F32),
        compiler_params=pltpu.CompilerParams(dimension_semantics=("arbitrary", "arbitrary"),
                                             vmem_limit_bytes=VMEM_LIMIT),
    )(x, mod, dproj)


def ada_weight_grad(c_all, dmod_cols):
    def body(c_ref, d_ref, o_ref):
        cv = c_ref[...]
        o_ref[...] = hdot_tn(cv * sigmoid(cv), d_ref[...])

    return pl.pallas_call(
        body, name="ada_weight_grad",
        out_shape=jax.ShapeDtypeStruct((c_all.shape[1], dmod_cols.shape[1]), F32),
    )(c_all, dmod_cols)


def adamw(parts, w, m, v, name):
    k, rows, cols = parts.shape
    rb = 128 if rows % 128 == 0 else rows

    def body(p_ref, w_ref, m_ref, v_ref, g_ref, d_ref, nm_ref, nv_ref):
        g = p_ref[0].astype(F32)
        for i in range(1, k):
            g = g + p_ref[i].astype(F32)
        nm = ADAM_B1 * m_ref[...] + (1.0 - ADAM_B1) * g
        nv = ADAM_B2 * v_ref[...] + (1.0 - ADAM_B2) * (g * g)
        m_hat = nm / (1.0 - ADAM_B1 ** ADAM_STEP)
        v_hat = nv / (1.0 - ADAM_B2 ** ADAM_STEP)
        g_ref[...] = g
        d_ref[...] = -ADAM_LR * (m_hat / (jnp.sqrt(v_hat) + ADAM_EPS) + ADAM_WD * w_ref[...])
        nm_ref[...] = nm
        nv_ref[...] = nv

    blk = pl.BlockSpec((rb, cols), lambda i: (i, 0))
    return pl.pallas_call(
        body, name=name, grid=(rows // rb,),
        in_specs=[pl.BlockSpec((k, rb, cols), lambda i: (0, i, 0)), blk, blk, blk],
        out_specs=[blk] * 4, out_shape=[jax.ShapeDtypeStruct((rows, cols), F32)] * 4,
        compiler_params=pltpu.CompilerParams(dimension_semantics=("arbitrary",), vmem_limit_bytes=VMEM_LIMIT),
    )(parts, w, m, v)


def rot_cols(w):
    return jnp.concatenate([-w[:, ROPE // 2:], w[:, :ROPE // 2]], axis=1)


def unrot_cols(dw):
    return jnp.concatenate([dw[:, ROPE // 2:], -dw[:, :ROPE // 2]], axis=1)


def columns_from_shards(g, rows, cols):
    return g.reshape(N_DEV, rows, cols).transpose(1, 0, 2).reshape(rows, N_DEV * cols)


def shards_from_columns(w, rows, cols):
    return w.reshape(rows, N_DEV, cols).transpose(1, 0, 2).reshape(N_DEV, rows * cols)


def permute_w_in(w):
    z = lambda n: jnp.zeros((D_MODEL, n), w.dtype)
    krope = w[:, N_KROPE:N_KROPE + ROPE]
    rw = N_RWKV
    return jnp.concatenate([
        w[:, N_MA:N_MA + 1024], w[:, N_MB:N_MB + 1024],
        w[:, rw:rw + 512], w[:, rw + 512:rw + 1024], w[:, rw + 1024:rw + 1536],
        w[:, N_GPA:N_GPA + 512], w[:, N_GPB:N_GPB + 512],
        w[:, N_QC:N_QC + 256], w[:, N_KVC:N_KVC + 128],
        z(NOPE), krope, z(LANE - QK_DIM), z(NOPE), rot_cols(krope), z(LANE - QK_DIM),
        w[:, rw + 1536:rw + 1664]], axis=1)


def unpermute_w_in_grad(d):
    rw = P_R
    krope = d[:, P_KR + NOPE:P_KR + QK_DIM] + unrot_cols(d[:, P_KRR + NOPE:P_KRR + QK_DIM])
    return jnp.concatenate([
        d[:, P_QC:P_QC + 256], d[:, P_KVC:P_KVC + 128], krope, d[:, P_GPA:P_GPA + 512],
        d[:, rw:rw + 1536], d[:, P_LORA:P_LORA + 128], d[:, P_GPB:P_GPB + 512],
        d[:, P_MA:P_MA + 1024], d[:, P_MB:P_MB + 1024]], axis=1)


def pad_heads_q(w_uq):
    w = w_uq.reshape(Q_RANK, HEADS, QK_DIM)
    zpad = jnp.zeros((Q_RANK, HEADS, LANE - QK_DIM), w.dtype)
    wq = jnp.concatenate([w, zpad], axis=2).reshape(Q_RANK, HEADS * LANE)
    pe = w[:, :, NOPE:]
    rot = jnp.concatenate([-pe[:, :, ROPE // 2:], pe[:, :, :ROPE // 2]], axis=2)
    wqr = jnp.concatenate([jnp.zeros((Q_RANK, HEADS, NOPE), w.dtype), rot, zpad], axis=2).reshape(Q_RANK, HEADS * LANE)
    return wq, wqr


def unpad_heads_q_grad(dwq, dwqr):
    a = dwq.reshape(Q_RANK, HEADS, LANE)
    r = dwqr.reshape(Q_RANK, HEADS, LANE)[:, :, NOPE:QK_DIM]
    pe = a[:, :, NOPE:QK_DIM] + jnp.concatenate([r[:, :, ROPE // 2:], -r[:, :, :ROPE // 2]], axis=2)
    return jnp.concatenate([a[:, :, :NOPE], pe], axis=2).reshape(Q_RANK, HEADS * QK_DIM)


def pad_heads_kv(w_ukv):
    w = w_ukv.reshape(KV_RANK, HEADS, 2 * HEAD)
    z = jnp.zeros((KV_RANK, HEADS, HEAD), w.dtype)
    wkn = jnp.concatenate([w[:, :, :NOPE], z], axis=2).reshape(KV_RANK, HEADS * LANE)
    val = w[:, :, NOPE:]
    odd = (jnp.arange(HEADS) % 2 == 1)[None, :, None]
    wv = jnp.concatenate([jnp.where(odd, 0, val), jnp.where(odd, val, 0)], axis=2).reshape(KV_RANK, HEADS * LANE)
    return wkn, wv


def unpad_heads_kv_grad(dwkn, dwv):
    a = dwkn.reshape(KV_RANK, HEADS, LANE)[:, :, :NOPE]
    b = dwv.reshape(KV_RANK, HEADS, LANE)
    odd = (jnp.arange(HEADS) % 2 == 1)[None, :, None]
    val = jnp.where(odd, b[:, :, HEAD:], b[:, :, :HEAD])
    return jnp.concatenate([a, val], axis=2).reshape(KV_RANK, HEADS * 2 * HEAD)


def kernel(x, c, positions, w_ada, b_ada, w_in, q_norm_g, w_uq, kv_norm_g, w_ukv, mu_rwkv, w0, w_decay_up, a0, w_iclr_up, k_k, k_a, r_k, gn_g, gn_b, w_proj_a, w_proj_b, w_out, post_g, post_b, loss_target, m_w_ada, m_b_ada, m_w_in, m_q_norm_g, m_w_uq, m_kv_norm_g, m_w_ukv, m_mu_rwkv, m_w0, m_w_decay_up, m_a0, m_w_iclr_up, m_k_k, m_k_a, m_r_k, m_gn_g, m_gn_b, m_w_proj_a, m_w_proj_b, m_w_out, m_post_g, m_post_b, v_w_ada, v_b_ada, v_w_in, v_q_norm_g, v_w_uq, v_kv_norm_g, v_w_ukv, v_mu_rwkv, v_w0, v_w_decay_up, v_a0, v_w_iclr_up, v_k_k, v_k_a, v_r_k, v_gn_g, v_gn_b, v_w_proj_a, v_w_proj_b, v_w_out, v_post_g, v_post_b):
    weights = dict(w_ada=w_ada, b_ada=b_ada, w_in=w_in, q_norm_g=q_norm_g, w_uq=w_uq, kv_norm_g=kv_norm_g,
                   w_ukv=w_ukv, mu_rwkv=mu_rwkv, w0=w0, w_decay_up=w_decay_up, a0=a0, w_iclr_up=w_iclr_up,
                   k_k=k_k, k_a=k_a, r_k=r_k, gn_g=gn_g, gn_b=gn_b, w_proj_a=w_proj_a, w_proj_b=w_proj_b,
                   w_out=w_out, post_g=post_g, post_b=post_b)
    mom1 = dict(w_ada=m_w_ada, b_ada=m_b_ada, w_in=m_w_in, q_norm_g=m_q_norm_g, w_uq=m_w_uq, kv_norm_g=m_kv_norm_g,
                w_ukv=m_w_ukv, mu_rwkv=m_mu_rwkv, w0=m_w0, w_decay_up=m_w_decay_up, a0=m_a0, w_iclr_up=m_w_iclr_up,
                k_k=m_k_k, k_a=m_k_a, r_k=m_r_k, gn_g=m_gn_g, gn_b=m_gn_b, w_proj_a=m_w_proj_a, w_proj_b=m_w_proj_b,
                w_out=m_w_out, post_g=m_post_g, post_b=m_post_b)
    mom2 = dict(w_ada=v_w_ada, b_ada=v_b_ada, w_in=v_w_in, q_norm_g=v_q_norm_g, w_uq=v_w_uq, kv_norm_g=v_kv_norm_g,
                w_ukv=v_w_ukv, mu_rwkv=v_mu_rwkv, w0=v_w0, w_decay_up=v_w_decay_up, a0=v_a0, w_iclr_up=v_w_iclr_up,
                k_k=v_k_k, k_a=v_k_a, r_k=v_r_k, gn_g=v_gn_g, gn_b=v_gn_b, w_proj_a=v_w_proj_a, w_proj_b=v_w_proj_b,
                w_out=v_w_out, post_g=v_post_g, post_b=v_post_b)
    names = list(weights)
    n_rows = x.shape[1]
    me = 4 * lax.axis_index("x") + 2 * lax.axis_index("y") + lax.axis_index("c")
    xr = x[0]
    tgt = loss_target[0]
    row = lambda a: a.reshape(1, -1)

    gathered = gather_shards([weights[n][0].astype(BF16) for n, _, _ in SHARDED] + [c])
    c_all = gathered[-1].reshape(N_DEV, D_MODEL)
    full = {}
    for (n, r, cdim), part in zip(SHARDED, gathered):
        full[n] = part.reshape(N_DEV * r, cdim) if n == "w_out" else columns_from_shards(part, r, cdim)
    w_in_p = permute_w_in(full["w_in"])
    wq, wqr = pad_heads_q(full["w_uq"])
    wkn, wv = pad_heads_kv(full["w_ukv"])
    zl = jnp.zeros((LORA, WIDTH), BF16)
    w_dec = jnp.concatenate([full["w_decay_up"], zl], axis=0)
    w_iclr = jnp.concatenate([zl, full["w_iclr_up"]], axis=0)
    wpa, wpb, wout = full["w_proj_a"], full["w_proj_b"], full["w_out"]

    mod_all = ada_modulation(c_all, w_ada[0], b_ada.reshape(N_DEV, -1))
    mod = lax.dynamic_index_in_dim(mod_all, me, axis=1, keepdims=False).reshape(3, D_MODEL)

    (proj,) = row_call("fwd_in", fwd_in_tile, n_rows, [(xr, D_MODEL, 0)], [mod, w_in_p], [(P_WIDTH, F32)])
    pcol = lambda off_, w: (proj, w, off_ // w)

    inv_freq = ROPE_THETA ** (-jnp.arange(0, ROPE, 2, dtype=F32) / ROPE)
    ang = positions[0].astype(F32)[:, None] * inv_freq
    ones_n, zeros_n, zeros_p = jnp.ones((n_rows, NOPE), F32), jnp.zeros((n_rows, NOPE), F32), jnp.zeros((n_rows, LANE - QK_DIM), F32)
    cos_t = jnp.concatenate([ones_n, jnp.cos(ang), jnp.cos(ang), zeros_p], axis=1)
    sin_t = jnp.concatenate([zeros_n, jnp.sin(ang), jnp.sin(ang), zeros_p], axis=1)

    gq, gkv = q_norm_g, kv_norm_g
    mla_consts = [gq, gkv, wq, wqr, wkn, wv]
    q, k, v = row_call(
        "mla_prep", mla_prep_tile, n_rows,
        [pcol(P_QC, 256), pcol(P_KVC, 128), pcol(P_KR, 128), pcol(P_KRR, 128), (cos_t, LANE, 0), (sin_t, LANE, 0)],
        mla_consts, [(HEADS * LANE, BF16)] * 3)
    ya, lse = attention_forward(q, k, v)

    t_idx = jnp.arange(ROW_TILE)
    same_chunk = (t_idx[:, None] // CHUNK) == (t_idx[None, :] // CHUNK)
    same = same_chunk.astype(F32)
    tril = (same_chunk & (t_idx[:, None] >= t_idx[None, :])).astype(F32)
    l_idx = jnp.arange(LANE)
    bd = ((l_idx[:, None] // HEAD) == (l_idx[None, :] // HEAD)).astype(F32)
    mu = mu_rwkv
    mu_r, mu_k, mu_v, mu_l = mu[:, 0:512], mu[:, 512:1024], mu[:, 1024:1536], mu[:, 1536:1664]
    rk_row = row(r_k)
    rwkv_consts = [mu_r, mu_k, mu_v, mu_l, w0, a0, k_k, k_a, w_dec, w_iclr, tril, same, bd]
    rwkv_rows = [pcol(P_R, 512), pcol(P_K, 512), pcol(P_V, 512), pcol(P_LORA, 128)]
    rt, at, bt, kt, clf, uv, ur, k2 = row_call(
        "rwkv_prep", rwkv_prep_tile, n_rows, rwkv_rows, rwkv_consts, [(WIDTH, F32)] * 8, halo_in=rwkv_rows)
    y, m0s, state_maps, out_maps = wkv_forward(at, bt, kt, rt, uv, clf)

    tail = row_call(
        "tail", tail_tile, n_rows,
        [(xr, D_MODEL, 0), (tgt, D_MODEL, 0), pcol(P_MA, 1024), pcol(P_MB, 1024), pcol(P_GPA, 512), pcol(P_GPB, 512),
         (ya, WIDTH, 0), (y, WIDTH, 0), (ur, WIDTH, 0), (k2, WIDTH, 0), (uv, WIDTH, 0)],
        [mod, wpa, wpb, wout, gn_g, gn_b, rk_row, post_g, post_b, bd],
        [(D_MODEL, F32), (1024, F32), (1024, F32), (512, F32), (512, F32), (WIDTH, F32), (WIDTH, F32), (WIDTH, F32)],
        acc_out=[((1, LANE), F32), ((D_MODEL, D_MODEL), F32), ((WIDTH, D_MODEL), F32), ((WIDTH, D_MODEL), F32),
                 ((1, WIDTH), F32), ((1, WIDTH), F32), ((1, D_MODEL), F32), ((1, D_MODEL), F32), ((1, D_MODEL), F32)])
    (dz, dma, dmb, dgpa, dgpb, dya, dy, dyb,
     loss_row, g_wout, g_wpa, g_wpb, g_gn_g, g_gn_b, g_post_g, g_post_b, dgate) = tail
    loss = lax.psum(loss_row[0, 0], ("x", "y", "c"))

    dq, dk, dv = attention_backward(q, k, v, ya, dya, lse)
    dq_c, dkv_c, dkr, dkrr, g_wq, g_wqr, g_wkn, g_wv, g_gq, g_gkv = row_call(
        "mla_prep_bwd", mla_prep_bwd_tile, n_rows,
        [pcol(P_QC, 256), pcol(P_KVC, 128), (cos_t, LANE, 0), (sin_t, LANE, 0),
         (dq, HEADS * LANE, 0), (dk, HEADS * LANE, 0), (dv, HEADS * LANE, 0)],
        mla_consts, [(256, F32), (128, F32), (128, F32), (128, F32)],
        acc_out=[((Q_RANK, HEADS * LANE), F32)] * 2 + [((KV_RANK, HEADS * LANE), F32)] * 2
        + [((1, Q_RANK), F32), ((1, KV_RANK), F32)])

    dat, dbt, dkt, drt, dvv, dlw = wkv_backward(at, bt, kt, rt, uv, clf, m0s, state_maps, out_maps, dy)
    (dr0, dk0, dv0, dl0, g_mu_r, g_mu_k, g_mu_v, g_mu_l, g_w0, g_a0, g_k_k, g_k_a, g_r_k, g_wdec, g_wiclr) = row_call(
        "rwkv_prep_bwd", rwkv_prep_bwd_tile, n_rows,
        rwkv_rows + [(drt, WIDTH, 0), (dat, WIDTH, 0), (dbt, WIDTH, 0), (dkt, WIDTH, 0), (dvv, WIDTH, 0),
                     (dlw, WIDTH, 0), (dyb, WIDTH, 0)],
        rwkv_consts + [rk_row], [(512, F32), (512, F32), (512, F32), (128, F32)],
        acc_out=[((1, 512), F32)] * 3 + [((1, 128), F32)] + [((1, 512), F32)] * 5 + [((LANE, WIDTH), F32)] * 2,
        halo_in=rwkv_rows, carry=[512, 512, 512, 128], reverse=True)

    dx, dproj, dshift, dscale = row_call(
        "in_bwd", in_bwd_tile, n_rows,
        [(xr, D_MODEL, 0), (dz, D_MODEL, 0), (dma, 1024, 0), (dmb, 1024, 0), (dr0, 512, 0), (dk0, 512, 0), (dv0, 512, 0),
         (dgpa, 512, 0), (dgpb, 512, 0), (dq_c, 256, 0), (dkv_c, 128, 0), (dkr, 128, 0), (dkrr, 128, 0), (dl0, 128, 0)],
        [mod, w_in_p], [(D_MODEL, F32), (P_WIDTH, BF16)], acc_out=[((1, D_MODEL), F32)] * 2)
    g_w_in_p = in_weight_grad(xr, mod, dproj)

    grads_full = {
        "w_in": unpermute_w_in_grad(g_w_in_p), "w_uq": unpad_heads_q_grad(g_wq, g_wqr),
        "w_ukv": unpad_heads_kv_grad(g_wkn, g_wv), "w_decay_up": g_wdec[:LORA], "w_iclr_up": g_wiclr[LORA:],
        "w_proj_a": g_wpa, "w_proj_b": g_wpb, "w_out": g_wout}
    blocks = [(grads_full[n].reshape(N_DEV, r, cdim) if n == "w_out"
               else grads_full[n].reshape(r, N_DEV, cdim).transpose(1, 0, 2)).astype(BF16) for n, r, cdim in SHARDED]
    dmod = jnp.concatenate([dshift, dscale, dgate], axis=1)
    small = jnp.concatenate([dmod, g_gq, g_gkv, g_mu_r, g_mu_k, g_mu_v, g_mu_l, g_w0, g_a0, g_k_k, g_k_a, g_r_k,
                             g_gn_g, g_gn_b, g_post_g, g_post_b], axis=1)
    *got_blocks, got_small = exchange_grads(blocks, small)

    ada_cols = w_ada.shape[2]
    dmod_all = got_small.reshape(N_DEV, SMALL_ELEMS)[:, :3 * D_MODEL]
    g_ada = ada_weight_grad(c_all, lax.dynamic_slice_in_dim(dmod_all, me * ada_cols, ada_cols, axis=1))

    def small_row(tree):
        return jnp.concatenate([tree[n].reshape(1, -1) for n, _ in SMALL], axis=1)

    outs = [dict() for _ in range(4)]
    res = adamw(g_ada[None], w_ada[0], m_w_ada[0], v_w_ada[0], "adamw_w_ada")
    for kind in range(4):
        outs[kind]["w_ada"] = res[kind][None]
    for (n, r, cdim), got in zip(SHARDED, got_blocks):
        res = adamw(got, weights[n][0], mom1[n][0], mom2[n][0], "adamw_" + n)
        for kind in range(4):
            outs[kind][n] = res[kind][None]
    res = adamw(got_small, small_row(weights), small_row(mom1), small_row(mom2), "adamw_small")
    for kind in range(4):
        off = 0
        for n, size in SMALL:
            outs[kind][n] = res[kind][:, off:off + size].reshape(weights[n].shape)
            off += size
    return (loss, dx[None], *[outs[0][n] for n in names], *[outs[1][n] for n in names],
            *[outs[2][n] for n in names], *[outs[3][n] for n in names])
```

```python
import functools
import math

import jax
import jax.numpy as jnp
from jax import lax
from jax.experimental import pallas as pl
from jax.experimental.pallas import tpu as pltpu

F32 = jnp.float32
BF16 = jnp.bfloat16
HIGHEST = lax.Precision.HIGHEST
MESH_IDS = pl.DeviceIdType.MESH

N_DEV = 8
D_MODEL = 1024
LN_EPS = 1e-5
RMS_EPS = 1e-6
GN_EPS = 64e-5
HEADS = 8
Q_RANK = 256
KV_RANK = 128
ROPE = 32
NOPE = 64
QK_DIM = NOPE + ROPE
WIDTH = 512
HEAD = 64
LORA = 64
CHUNK = 64
DEPTH = 1
ALPHA = (2.0 * DEPTH) ** 0.25
ROPE_THETA = 10000.0
ATTN_SCALE = QK_DIM ** -0.5
DECAY_SCALE = math.exp(-0.5)

ADAM_LR = 0.001
ADAM_B1 = 0.9
ADAM_B2 = 0.999
ADAM_EPS = 1e-08
ADAM_WD = 0.01
ADAM_STEP = 10

LANE = 128
PAIR = 2 * HEAD
ROW_TILE = 256
ATTN_TILE = 256
WKV_CHUNKS_PER_STEP = 8
VMEM_LIMIT = 56 * 1024 * 1024

P_MA, P_MB, P_R, P_K, P_V, P_GPA, P_GPB, P_QC, P_KVC, P_KR, P_KRR, P_LORA = (
    0, 1024, 2048, 2560, 3072, 3584, 4096, 4608, 4864, 4992, 5120, 5248)
P_WIDTH = 5376
DW_BLOCK = 768

N_QC, N_KVC, N_KROPE, N_GPA, N_RWKV, N_GPB, N_MA, N_MB = 0, 256, 384, 416, 928, 2592, 3104, 4128
IN_WIDTH = 5152

SHARDED = (("w_in", 1024, 644), ("w_uq", 256, 96), ("w_ukv", 128, 128), ("w_decay_up", 64, 64),
           ("w_iclr_up", 64, 64), ("w_proj_a", 512, 128), ("w_proj_b", 512, 128), ("w_out", 128, 1024))
SHARD_ELEMS = sum(r * c for _, r, c in SHARDED)
SHARD_ROWS = SHARD_ELEMS // LANE
GATHER_ROWS = SHARD_ROWS + 2 * D_MODEL // LANE
SMALL = (("b_ada", 3072), ("q_norm_g", 256), ("kv_norm_g", 128), ("mu_rwkv", 1664), ("w0", 512), ("a0", 512),
         ("k_k", 512), ("k_a", 512), ("r_k", 512), ("gn_g", 512), ("gn_b", 512), ("post_g", 1024), ("post_b", 1024))
SMALL_ELEMS = sum(n for _, n in SMALL)
SMALL_ROWS = SMALL_ELEMS // LANE


def mm(a, b):
    return jnp.dot(a.astype(BF16), b.astype(BF16), preferred_element_type=F32)


def mm_nt(a, b):
    return lax.dot_general(a.astype(BF16), b.astype(BF16), (((1,), (1,)), ((), ())), preferred_element_type=F32)


def mm_tn(a, b):
    return lax.dot_general(a.astype(BF16), b.astype(BF16), (((0,), (0,)), ((), ())), preferred_element_type=F32)


def hdot(a, b):
    return jnp.dot(a, b, precision=HIGHEST, preferred_element_type=F32)


def hdot_nt(a, b):
    return lax.dot_general(a, b, (((1,), (1,)), ((), ())), precision=HIGHEST, preferred_element_type=F32)


def hdot_tn(a, b):
    return lax.dot_general(a, b, (((0,), (0,)), ((), ())), precision=HIGHEST, preferred_element_type=F32)


def sigmoid(x):
    return 1.0 / (1.0 + jnp.exp(-x))


def colsum(x):
    return jnp.sum(x, axis=0, keepdims=True)


def rowmean(x):
    return jnp.mean(x, axis=-1, keepdims=True)


def layer_norm_stats(x):
    xc = x - rowmean(x)
    rstd = lax.rsqrt(rowmean(xc * xc) + LN_EPS)
    return xc * rstd, rstd


def layer_norm_bwd(dy, xhat, rstd):
    return rstd * (dy - rowmean(dy) - xhat * rowmean(dy * xhat))


def head_sum(x, bd):
    return jnp.concatenate([hdot(x[:, p * LANE:(p + 1) * LANE], bd) for p in range(x.shape[1] // LANE)], axis=1)


def tile_lanes(t, n):
    return jnp.concatenate([t] * n, axis=1)


def row_iota(shape):
    return lax.broadcasted_iota(jnp.int32, shape, 0)


def lane_iota(shape):
    return lax.broadcasted_iota(jnp.int32, shape, 1)


def shift_rows_down(x, row0):
    rolled = pltpu.roll(x, 1, axis=0)
    return jnp.where(row_iota(x.shape) == 0, row0, rolled)


def shift_rows_up(x, row_last):
    rolled = pltpu.roll(x, x.shape[0] - 1, axis=0)
    return jnp.where(row_iota(x.shape) == x.shape[0] - 1, row_last, rolled)


def row_call(name, fn, n_rows, row_in, const_in, row_out, acc_out=(), halo_in=(), carry=(), reverse=False):
    ts = ROW_TILE
    n_tiles = n_rows // ts
    n_in = len(row_in) + len(halo_in) + len(const_in)
    n_ro, n_ao = len(row_out), len(acc_out)

    def tile_of(g):
        return (n_tiles - 1 - g) if reverse else g

    def body(*refs):
        ins = refs[:n_in]
        ro = refs[n_in:n_in + n_ro]
        ao = refs[n_in + n_ro:n_in + n_ro + n_ao]
        cr = refs[n_in + n_ro + n_ao:]
        g = pl.program_id(0)
        step0 = g == 0
        tile0 = tile_of(g) == 0
        for r in cr:
            @pl.when(step0)
            def _(r=r):
                r[...] = jnp.zeros_like(r)
        vals = [r[...] for r in ins]
        outs = fn(step0, tile0, *vals, *[c[0:1, :] for c in cr])
        for r, v in zip(ro, outs[:n_ro]):
            r[...] = v.astype(r.dtype)
        for r, v in zip(ao, outs[n_ro:n_ro + n_ao]):
            @pl.when(step0)
            def _(r=r, v=v):
                r[...] = v.astype(r.dtype)

            @pl.when(jnp.logical_not(step0))
            def _(r=r, v=v):
                r[...] += v.astype(r.dtype)
        for r, v in zip(cr, outs[n_ro + n_ao:]):
            r[0:1, :] = v

    in_specs = [pl.BlockSpec((ts, w), functools.partial(lambda g, cb: (tile_of(g), cb), cb=cb)) for _, w, cb in row_in]
    in_specs += [pl.BlockSpec((8, w), functools.partial(
        lambda g, cb: (jnp.maximum(tile_of(g) * (ts // 8) - 1, 0), cb), cb=cb)) for _, w, cb in halo_in]
    in_specs += [pl.BlockSpec(memory_space=pltpu.VMEM) for _ in const_in]
    out_specs = [pl.BlockSpec((ts, w), lambda g: (tile_of(g), 0)) for w, _ in row_out]
    out_specs += [pl.BlockSpec(s, lambda g: (0, 0)) for s, _ in acc_out]
    out_shape = [jax.ShapeDtypeStruct((n_rows, w), d) for w, d in row_out]
    out_shape += [jax.ShapeDtypeStruct(s, d) for s, d in acc_out]
    return pl.pallas_call(
        body, name=name, grid=(n_tiles,), in_specs=in_specs, out_specs=out_specs, out_shape=out_shape,
        scratch_shapes=[pltpu.VMEM((8, w), F32) for w in carry],
        compiler_params=pltpu.CompilerParams(dimension_semantics=("arbitrary",), vmem_limit_bytes=VMEM_LIMIT),
    )(*[a for a, _, _ in row_in], *[a for a, _, _ in halo_in], *const_in)


def my_position():
    return lax.axis_index("x"), lax.axis_index("y"), lax.axis_index("c")


def flip(pos, k):
    x, y, c = pos
    dx, dy, dc = (k >> 2) & 1, (k >> 1) & 1, k & 1
    return (1 - x if dx else x, 1 - y if dy else y, 1 - c if dc else c)


def flat_index(pos):
    return 4 * pos[0] + 2 * pos[1] + pos[2]


def gather_shards(shards):
    n = len(shards)

    def body(*refs):
        x_refs, out_refs = refs[:n], refs[n:2 * n]
        send_sems, recv_sems, local_sems = refs[2 * n:]
        x, y, c = my_position()
        me, sibling = (x, y, c), (x, y, 1 - c)
        chips = [(1 - x, y), (x, 1 - y), (1 - x, 1 - y)]

        def copy(a, k, block, to, from_input=False):
            slot = out_refs[a].at[flat_index(block)]
            return pltpu.make_async_remote_copy(
                src_ref=x_refs[a] if from_input else slot, dst_ref=slot,
                send_sem=send_sems.at[7 * a + k], recv_sem=recv_sems.at[7 * a + k],
                device_id=to, device_id_type=MESH_IDS)

        mine = [pltpu.make_async_copy(x_refs[a], out_refs[a].at[flat_index(me)], local_sems.at[a]) for a in range(n)]
        for cp in mine:
            cp.start()
        first = []
        for a in range(n):
            first.append(copy(a, 0, me, sibling, from_input=True))
            first += [copy(a, 1 + j, me, (*chip, c), from_input=True) for j, chip in enumerate(chips)]
        for cp in first:
            cp.start()
        passed = []
        for j, chip in enumerate(chips):
            for a in range(n):
                copy(a, 1 + j, (*chip, c), me).wait_recv()
                cp = copy(a, 4 + j, (*chip, c), sibling)
                cp.start()
                passed.append(cp)
        for a in range(n):
            copy(a, 0, sibling, me).wait_recv()
            for j, chip in enumerate(chips):
                copy(a, 4 + j, (*chip, 1 - c), me).wait_recv()
        for cp in first + passed:
            cp.wait_send()
        for cp in mine:
            cp.wait()

    return pl.pallas_call(
        body, name="gather_shards",
        out_shape=[jax.ShapeDtypeStruct((N_DEV,) + s.shape, s.dtype) for s in shards],
        in_specs=[pl.BlockSpec(memory_space=pl.ANY)] * n, out_specs=[pl.BlockSpec(memory_space=pl.ANY)] * n,
        scratch_shapes=[pltpu.SemaphoreType.DMA((7 * n,)), pltpu.SemaphoreType.DMA((7 * n,)),
                        pltpu.SemaphoreType.DMA((n,))],
    )(*shards)


def ada_modulation(c_all, w_ada_loc, b_ada_blocks):
    cols = w_ada_loc.shape[1]

    def body(c_ref, w_ref, b_ref, out_ref, send_sems, recv_sems):
        me = my_position()
        mi = flat_index(me)
        cv = c_ref[...]
        res = hdot(cv * sigmoid(cv), w_ref[...]) + b_ref[pl.ds(mi, 1), :]
        out_ref[mi] = res
        sends = []
        for k in range(1, N_DEV):
            cp = pltpu.make_async_remote_copy(
                src_ref=out_ref.at[mi], dst_ref=out_ref.at[mi], send_sem=send_sems.at[k - 1],
                recv_sem=recv_sems.at[k - 1], device_id=flip(me, k), device_id_type=MESH_IDS)
            cp.start()
            sends.append(cp)
        for k in range(1, N_DEV):
            pi = flat_index(flip(me, k))
            pltpu.make_async_remote_copy(
                src_ref=out_ref.at[pi], dst_ref=out_ref.at[pi], send_sem=send_sems.at[k - 1],
                recv_sem=recv_sems.at[k - 1], device_id=flip(me, k), device_id_type=MESH_IDS).wait_recv()
        for cp in sends:
            cp.wait_send()

    return pl.pallas_call(
        body, name="ada_modulation",
        out_shape=jax.ShapeDtypeStruct((N_DEV, N_DEV, cols), F32),
        in_specs=[pl.BlockSpec(memory_space=pltpu.VMEM)] * 3, out_specs=pl.BlockSpec(memory_space=pltpu.VMEM),
        scratch_shapes=[pltpu.SemaphoreType.DMA((7,)), pltpu.SemaphoreType.DMA((7,))],
    )(c_all, w_ada_loc, b_ada_blocks)


def fwd_in_tile(step0, tile0, x, mod, w_in_p):
    xhat, _ = layer_norm_stats(x)
    h = xhat * (1.0 + mod[1:2]) + mod[0:1]
    return (mm(h, w_in_p),)


def rms_norm_fwd(x, g):
    r = lax.rsqrt(rowmean(x * x) + RMS_EPS)
    xh = x * r
    return xh * g, xh, r


def key_rope_mask(shape):
    return (lane_iota(shape) >= NOPE).astype(F32)


def mla_prep_tile(step0, tile0, q_c, kv_c, kr, krr, cos, sin, gq, gkv, wq, wqr, wkn, wv):
    qn, _, _ = rms_norm_fwd(q_c, gq)
    kvn, _, _ = rms_norm_fwd(kv_c, gkv)
    q = mm(qn, wq) * tile_lanes(cos, HEADS) + mm(qn, wqr) * tile_lanes(sin, HEADS)
    kpe = kr * (cos * key_rope_mask(cos.shape)) + krr * sin
    k = mm(kvn, wkn) + tile_lanes(kpe, HEADS)
    v = mm(kvn, wv)
    return q, k, v


def rwkv_prep_core(tile0, r0, k0, v0, l0, hr, hk, hv, hl, mu_r, mu_k, mu_v, mu_l, w0, a0, k_k, k_a,
                   w_dec, w_iclr, tril, same, bd):
    def shifted(x, halo, mu):
        row0 = jnp.where(tile0, 0.0, halo[7:8, :])
        prev = shift_rows_down(x, row0)
        return x + (prev - x) * mu, prev

    ur, pr = shifted(r0, hr, mu_r)
    uk, pk = shifted(k0, hk, mu_k)
    uv, pv = shifted(v0, hv, mu_v)
    ul, plo = shifted(l0, hl, mu_l)
    th = jnp.tanh(ul)
    sg = sigmoid(w0 + mm(th, w_dec))
    lw = -DECAY_SCALE * sg
    a_ic = sigmoid(a0 + mm(ul, w_iclr))
    kkraw = uk * k_k
    nrm_raw = jnp.sqrt(head_sum(kkraw * kkraw, bd))
    nrm = jnp.maximum(nrm_raw, 1e-12)
    kk = kkraw / nrm
    k2 = uk * (1.0 + (a_ic - 1.0) * k_a)
    lc = hdot(tril, lw)
    lcl = hdot(same, lw)
    return dict(ur=ur, uk=uk, uv=uv, ul=ul, pr=pr, pk=pk, pv=pv, pl=plo, th=th, sg=sg, lw=lw, a_ic=a_ic,
                kkraw=kkraw, nrm_raw=nrm_raw, nrm=nrm, kk=kk, k2=k2, lc=lc, lcl=lcl)


def rwkv_prep_tile(step0, tile0, r0, k0, v0, l0, hr, hk, hv, hl, *consts):
    f = rwkv_prep_core(tile0, r0, k0, v0, l0, hr, hk, hv, hl, *consts)
    lc, lw = f["lc"], f["lw"]
    e_neg = jnp.exp(-lc)
    rt = f["ur"] * jnp.exp(lc)
    at = -f["kk"] * jnp.exp(lc - lw)
    bt = f["kk"] * f["a_ic"] * e_neg
    kt = f["k2"] * e_neg
    return rt, at, bt, kt, jnp.exp(f["lcl"]), f["uv"], f["ur"], f["k2"]


def wkv_masks():
    lane = lane_iota((1, PAIR))
    m_lo = (lane < HEAD).astype(F32)
    ri = row_iota((CHUNK, CHUNK))
    ci = lane_iota((CHUNK, CHUNK))
    r2 = row_iota((PAIR, PAIR))
    c2 = lane_iota((PAIR, PAIR))
    bd = ((r2 < HEAD) == (c2 < HEAD)).astype(F32)
    eye2 = (r2 == c2).astype(F32)
    return (m_lo, 1.0 - m_lo), ri > ci, ri >= ci, (ri == ci).astype(F32), bd, eye2


def wkv_chunks_pre(chunks, masks):
    ms, strict, incl, eye, bd, eye2 = masks
    items = [(c, m) for c in range(len(chunks)) for m in ms]
    at, bt, kt, rt, v, cl = (list(t) for t in zip(*chunks))
    atm = [at[c] * m for c, m in items]
    rtm = [rt[c] * m for c, m in items]
    aab = [jnp.where(strict, mm_nt(x, bt[c]), 0.0) for x, (c, _) in zip(atm, items)]
    aak = [jnp.where(strict, mm_nt(x, kt[c]), 0.0) for x, (c, _) in zip(atm, items)]
    prb = [jnp.where(incl, mm_nt(x, bt[c]), 0.0) for x, (c, _) in zip(rtm, items)]
    prk = [jnp.where(incl, mm_nt(x, kt[c]), 0.0) for x, (c, _) in zip(rtm, items)]
    tinv = [eye + a for a in aab]
    power = aab
    for _ in range(5):
        power = [mm(p, p) for p in power]
        tinv = [t + mm(t, p) for t, p in zip(tinv, power)]

    def by_chunk(parts):
        return [parts[2 * c] + parts[2 * c + 1] for c in range(len(chunks))]

    w = by_chunk([mm(a, v[c] * m) for a, (c, m) in zip(aak, items)])
    ah = by_chunk([mm(t, x) for t, x in zip(tinv, atm)])
    wh = by_chunk([mm(t, w[c] * m) for t, (c, m) in zip(tinv, items)])
    rh = [r + d for r, d in zip(rt, by_chunk([mm(p, ah[c] * m) for p, (c, m) in zip(prb, items)]))]
    yh = by_chunk([mm(p, wh[c] * m) + mm(q, v[c] * m) for p, q, (c, m) in zip(prb, prk, items)])
    bc = [b * c_ for b, c_ in zip(bt, cl)]
    kc = [k * c_ for k, c_ in zip(kt, cl)]
    g = [eye2 * c_ + bd * mm_tn(b, a) for c_, b, a in zip(cl, bc, ah)]
    h = [bd * (mm_tn(b, w_) + mm_tn(k, v_)) for b, w_, k, v_ in zip(bc, wh, kc, v)]
    return g, h, rh, yh, (items, atm, rtm, tinv, aak, prb, prk, ah, wh, bc, kc)


def wkv_chunks_grad(chunks, m0, dy, dm1, masks):
    ms, strict, incl, eye, bd, eye2 = masks
    n = len(chunks)
    _, _, _, _, (items, atm, rtm, tinv, aak, prb, prk, ah, wh, bc, kc) = wkv_chunks_pre(chunks, masks)
    at, bt, kt, rt, v, cl = (list(t) for t in zip(*chunks))

    def by_chunk(parts):
        return [parts[2 * c] + parts[2 * c + 1] for c in range(n)]

    u = [mm(a, m) + w for a, m, w in zip(ah, m0, wh)]
    dm1 = [d * bd for d in dm1]
    dym = [dy[c] * m for c, m in items]
    du = [mm(b, d) + e for b, d, e in zip(bc, dm1, by_chunk([mm_tn(p, x) for p, x in zip(prb, dym)]))]
    dv = [mm(k, d) + e for k, d, e in zip(kc, dm1, by_chunk([mm_tn(p, x) for p, x in zip(prk, dym)]))]
    dz = by_chunk([mm_tn(t, du[c] * m) for t, (c, m) in zip(tinv, items)])
    dzm = [dz[c] * m for c, m in items]
    dv = [a + b for a, b in zip(dv, by_chunk([mm_tn(a_, x) for a_, x in zip(aak, dzm)]))]
    drt = [mm_nt(d, m) for d, m in zip(dy, m0)]
    dat = [mm_nt(d, m) for d, m in zip(dz, m0)]
    udm = [mm_nt(x, d) for x, d in zip(u, dm1)]
    vdm = [mm_nt(x, d) for x, d in zip(v, dm1)]
    daab = [jnp.where(strict, mm_nt(x, u[c]), 0.0) for x, (c, _) in zip(dzm, items)]
    daak = [jnp.where(strict, mm_nt(x, v[c]), 0.0) for x, (c, _) in zip(dzm, items)]
    dprb = [jnp.where(incl, mm_nt(x, u[c]), 0.0) for x, (c, _) in zip(dym, items)]
    dprk = [jnp.where(incl, mm_nt(x, v[c]), 0.0) for x, (c, _) in zip(dym, items)]
    drt2 = by_chunk([(mm(p, bt[c]) + mm(q, kt[c])) * m for p, q, (c, m) in zip(dprb, dprk, items)])
    dat2 = by_chunk([(mm(p, bt[c]) + mm(q, kt[c])) * m for p, q, (c, m) in zip(daab, daak, items)])
    dbt2 = by_chunk([mm_tn(p, r) + mm_tn(a_, x) for p, r, a_, x in zip(dprb, rtm, daab, atm)])
    dkt2 = by_chunk([mm_tn(p, r) + mm_tn(a_, x) for p, r, a_, x in zip(dprk, rtm, daak, atm)])
    ones = jnp.ones((8, PAIR), F32)
    upper = (lane_iota((CHUNK, CHUNK)) >= row_iota((CHUNK, CHUNK))).astype(F32)
    out = []
    for c in range(n):
        drt_c = drt[c] + drt2[c]
        dat_c = dat[c] + dat2[c]
        dbt_c = udm[c] * cl[c] + dbt2[c]
        dkt_c = vdm[c] * cl[c] + dkt2[c]
        dlcl = hdot_nt(ones, dm1[c] * m0[c])[0:1, :] * cl[c] + colsum(bc[c] * udm[c] + kc[c] * vdm[c])
        g = drt_c * rt[c] - dbt_c * bt[c] - dkt_c * kt[c] + dat_c * at[c]
        dlw = hdot(upper, g) - dat_c * at[c] + dlcl
        out.append((dat_c, dbt_c, dkt_c, drt_c, dv[c], dlw))
    return out


def wkv_forward(at, bt, kt, rt, v, clf):
    n_rows = at.shape[0]
    cps = WKV_CHUNKS_PER_STEP
    rb = cps * CHUNK
    n_steps = n_rows // rb

    def body(a_ref, b_ref, k_ref, r_ref, v_ref, c_ref, y_ref, m0_ref, g_ref, rh_ref, m_scr):
        @pl.when(pl.program_id(1) == 0)
        def _():
            m_scr[...] = jnp.zeros_like(m_scr)

        masks = wkv_masks()
        chunks = []
        for cc in range(cps):
            sl = slice(cc * CHUNK, (cc + 1) * CHUNK)
            chunks.append((a_ref[sl, :], b_ref[sl, :], k_ref[sl, :], r_ref[sl, :], v_ref[sl, :],
                           c_ref[cc * CHUNK:cc * CHUNK + 1, :]))
        pre = wkv_chunks_pre(chunks, masks)[:4]
        m = m_scr[...]
        for cc, (g, h, rh, yh) in enumerate(zip(*pre)):
            sl = slice(cc * CHUNK, (cc + 1) * CHUNK)
            m0_ref[0, cc] = m
            g_ref[0, cc] = g
            rh_ref[sl, :] = rh
            y_ref[sl, :] = hdot(rh, m) + yh
            m = hdot(g, m) + h
        m_scr[...] = m

    blk = pl.BlockSpec((rb, PAIR), lambda p, s: (s, p))
    state_blk = pl.BlockSpec((1, cps, PAIR, PAIR), lambda p, s: (p, s, 0, 0))
    state_shape = jax.ShapeDtypeStruct((WIDTH // PAIR, n_rows // CHUNK, PAIR, PAIR), F32)
    return pl.pallas_call(
        body, name="wkv_forward", grid=(WIDTH // PAIR, n_steps),
        in_specs=[blk] * 6,
        out_specs=[blk, state_blk, state_blk, blk],
        out_shape=[jax.ShapeDtypeStruct((n_rows, WIDTH), F32), state_shape, state_shape,
                   jax.ShapeDtypeStruct((n_rows, WIDTH), F32)],
        scratch_shapes=[pltpu.VMEM((PAIR, PAIR), F32)],
        compiler_params=pltpu.CompilerParams(dimension_semantics=("arbitrary", "arbitrary"),
                                             vmem_limit_bytes=VMEM_LIMIT),
    )(at, bt, kt, rt, v, clf)


def wkv_backward(at, bt, kt, rt, v, clf, m0s, gs, rh, dy):
    n_rows = at.shape[0]
    cps = WKV_CHUNKS_PER_STEP
    rb = cps * CHUNK
    n_steps = n_rows // rb

    def body(a_ref, b_ref, k_ref, r_ref, v_ref, c_ref, m0_ref, g_ref, rh_ref, dy_ref,
             da_ref, db_ref, dk_ref, dr_ref, dv_ref, dlw_ref, dm_scr):
        @pl.when(pl.program_id(1) == 0)
        def _():
            dm_scr[...] = jnp.zeros_like(dm_scr)

        masks = wkv_masks()
        bd = masks[4]
        dm = dm_scr[...]
        dm1 = [None] * cps
        for cc in reversed(range(cps)):
            sl = slice(cc * CHUNK, (cc + 1) * CHUNK)
            dm1[cc] = dm
            dm = bd * (hdot_tn(g_ref[0, cc], dm) + hdot_tn(rh_ref[sl, :], dy_ref[sl, :]))
        dm_scr[...] = dm
        chunks, m0, dys = [], [], []
        for cc in range(cps):
            sl = slice(cc * CHUNK, (cc + 1) * CHUNK)
            chunks.append((a_ref[sl, :], b_ref[sl, :], k_ref[sl, :], r_ref[sl, :], v_ref[sl, :],
                           c_ref[cc * CHUNK:cc * CHUNK + 1, :]))
            m0.append(m0_ref[0, cc])
            dys.append(dy_ref[sl, :])
        grads = wkv_chunks_grad(chunks, m0, dys, dm1, masks)
        for cc, (dat, dbt, dkt, drt, dv, dlw) in enumerate(grads):
            sl = slice(cc * CHUNK, (cc + 1) * CHUNK)
            da_ref[sl, :] = dat
            db_ref[sl, :] = dbt
            dk_ref[sl, :] = dkt
            dr_ref[sl, :] = drt
            dv_ref[sl, :] = dv
            dlw_ref[sl, :] = dlw

    blk = pl.BlockSpec((rb, PAIR), lambda p, s: (n_steps - 1 - s, p))
    state_blk = pl.BlockSpec((1, cps, PAIR, PAIR), lambda p, s: (p, n_steps - 1 - s, 0, 0))
    return pl.pallas_call(
        body, name="wkv_backward", grid=(WIDTH // PAIR, n_steps),
        in_specs=[blk] * 6 + [state_blk, state_blk, blk, blk],
        out_specs=[blk] * 6,
        out_shape=[jax.ShapeDtypeStruct((n_rows, WIDTH), F32)] * 6,
        scratch_shapes=[pltpu.VMEM((PAIR, PAIR), F32)],
        compiler_params=pltpu.CompilerParams(dimension_semantics=("arbitrary", "arbitrary"),
                                             vmem_limit_bytes=VMEM_LIMIT),
    )(at, bt, kt, rt, v, clf, m0s, gs, rh, dy)


def visible(q_row0, k_row0, shape):
    qc = (q_row0 + row_iota(shape)) // CHUNK
    kc = (k_row0 + lane_iota(shape)) // CHUNK
    return kc <= qc


def attention_forward(q, k, v):
    n_rows = q.shape[0]
    tq = tk = ATTN_TILE
    n_q = n_rows // tq

    def body(q_ref, k_ref, v_ref, o_ref, lse_ref):
        i = pl.program_id(1)
        lane = lane_iota((tq, LANE))
        heads = [slice(0, LANE), slice(LANE, 2 * LANE)]
        qs = [q_ref[:, cols] for cols in heads]

        def step(j, carry, masked):
            rows = pl.ds(pl.multiple_of(j * tk, tk), tk)
            ss = [mm_nt(qh, k_ref[rows, cols]) * ATTN_SCALE for qh, cols in zip(qs, heads)]
            if masked:
                vis = visible(i * tq, j * tk, ss[0].shape)
                ss = [jnp.where(vis, s, -jnp.inf) for s in ss]
            ps, stats = [], []
            for s, (m, l, _) in zip(ss, carry):
                m_new = jnp.maximum(m, jnp.max(s, axis=-1, keepdims=True))
                p = jnp.exp(s - m_new)
                alpha = jnp.exp(m - m_new)
                ps.append(p)
                stats.append((m_new, alpha, alpha * l + jnp.sum(p, axis=-1, keepdims=True)))
            pvs = [mm(p, v_ref[rows, cols]) for p, cols in zip(ps, heads)]
            return tuple((m_new, l, alpha * acc + pv)
                         for (m_new, alpha, l), (_, _, acc), pv in zip(stats, carry, pvs))

        init = tuple((jnp.full((tq, 1), -jnp.inf, F32), jnp.zeros((tq, 1), F32), jnp.zeros((tq, LANE), F32))
                     for _ in heads)
        carry = lax.fori_loop(0, i, functools.partial(step, masked=False), init)
        (m0, l0, acc0), (m1, l1, acc1) = step(i, carry, masked=True)
        o_ref[...] = acc0 / l0 + acc1 / l1
        lse_ref[...] = jnp.where(lane >= HEAD, m1 + jnp.log(l1), m0 + jnp.log(l0))

    return pl.pallas_call(
        body, name="attention_forward", grid=(HEADS // 2, n_q),
        in_specs=[pl.BlockSpec((tq, 2 * LANE), lambda p, i: (i, p)),
                  pl.BlockSpec((n_rows, 2 * LANE), lambda p, i: (0, p)),
                  pl.BlockSpec((n_rows, 2 * LANE), lambda p, i: (0, p))],
        out_specs=[pl.BlockSpec((tq, LANE), lambda p, i: (i, p))] * 2,
        out_shape=[jax.ShapeDtypeStruct((n_rows, WIDTH), F32)] * 2,
        compiler_params=pltpu.CompilerParams(dimension_semantics=("arbitrary", "arbitrary"),
                                             vmem_limit_bytes=VMEM_LIMIT),
    )(q, k, v)


def attention_backward(q, k, v, o, do, lse):
    n_rows = q.shape[0]
    tq = tk = ATTN_TILE
    n_q = n_rows // tq

    def body(q_ref, k_ref, v_ref, o_ref, do_ref, lse_ref, dq_ref, dk_ref, dv_ref):
        j = pl.program_id(1)

        @pl.when(j == 0)
        def _():
            dq_ref[...] = jnp.zeros_like(dq_ref)

        lane = lane_iota((tq, LANE))
        heads = [slice(0, LANE), slice(LANE, 2 * LANE)]
        ks = [k_ref[:, cols] for cols in heads]
        vs = [v_ref[:, cols] for cols in heads]
        head_lanes = [(lane < HEAD).astype(F32), (lane >= HEAD).astype(F32)]

        def step(i, carry, masked):
            rows = pl.ds(pl.multiple_of(i * tq, tq), tq)
            qs = [q_ref[rows, cols] for cols in heads]
            dout = do_ref[rows, :]
            dout_o = dout * o_ref[rows, :]
            lse_t = lse_ref[rows, :]
            ss = [mm_nt(qh, kh) * ATTN_SCALE for qh, kh in zip(qs, ks)]
            dps = [mm_nt(dout, vh) for vh in vs]
            ps, dss = [], []
            for hh in range(2):
                delta = jnp.sum(dout_o * head_lanes[hh], axis=-1, keepdims=True)
                lse_h = jnp.sum(jnp.where(lane == hh * HEAD, lse_t, 0.0), axis=-1, keepdims=True)
                p = jnp.exp(ss[hh] - lse_h)
                if masked:
                    p = jnp.where(visible(i * tq, j * tk, p.shape), p, 0.0)
                ps.append(p)
                dss.append(p * (dps[hh] - delta) * ATTN_SCALE)
            dvs = [mm_tn(p, dout) for p in ps]
            dqs = [mm(ds, kh) for ds, kh in zip(dss, ks)]
            dks = [mm_tn(ds, qh) for ds, qh in zip(dss, qs)]
            for cols, dq in zip(heads, dqs):
                dq_ref[rows, cols] += dq
            return tuple((dk + a, dv + b) for (dk, dv), a, b in zip(carry, dks, dvs))

        init = tuple((jnp.zeros((tk, LANE), F32), jnp.zeros((tk, LANE), F32)) for _ in heads)
        carry = step(j, init, masked=True)
        carry = lax.fori_loop(j + 1, n_q, functools.partial(step, masked=False), carry)
        for cols, (dk, dv) in zip(heads, carry):
            dk_ref[:, cols] = dk
            dv_ref[:, cols] = dv

    full = lambda w: pl.BlockSpec((n_rows, w), lambda p, j: (0, p))
    blk = pl.BlockSpec((tk, 2 * LANE), lambda p, j: (j, p))
    return pl.pallas_call(
        body, name="attention_backward", grid=(HEADS // 2, n_q),
        in_specs=[full(2 * LANE), blk, blk, full(LANE), full(LANE), full(LANE)],
        out_specs=[full(2 * LANE), blk, blk],
        out_shape=[jax.ShapeDtypeStruct((n_rows, HEADS * LANE), F32)] * 3,
        compiler_params=pltpu.CompilerParams(dimension_semantics=("arbitrary", "arbitrary"),
                                             vmem_limit_bytes=VMEM_LIMIT),
    )(q, k, v, o, do, lse)


def tail_tile(step0, tile0, x, tgt, ma, mb, gpa, gpb, ya, y, ur, k2, uv,
              mod, wpa, wpb, wout, gn_g, gn_b, r_k, post_g, post_b, bd):
    gate = mod[2:3]
    inv = 1.0 / HEAD
    yc = y - head_sum(y, bd) * inv
    rs = lax.rsqrt(head_sum(yc * yc, bd) * inv + GN_EPS)
    yn = yc * rs
    yb = yn * gn_g + gn_b + head_sum(ur * k2 * r_k, bd) * uv
    sga, sgb = sigmoid(gpa), sigmoid(gpb)
    sila, silb = gpa * sga, gpb * sgb
    ga, gb = ya * sila, yb * silb
    pa, pb = mm(ga, wpa), mm(gb, wpb)
    sa, sb = sigmoid(ma), sigmoid(mb)
    merged = sa * pa + sb * pb
    sub = mm(merged, wout)
    z = ALPHA * x + (1.0 + gate) * sub
    zhat, rstd = layer_norm_stats(z)
    err = zhat * post_g + post_b - tgt
    loss = 0.5 * jnp.sum(rowmean(err * err), axis=0, keepdims=True) + jnp.zeros((1, LANE), F32)
    dout = err * (1.0 / D_MODEL)
    dpost_g = colsum(dout * zhat)
    dpost_b = colsum(dout)
    dz = layer_norm_bwd(dout * post_g, zhat, rstd)
    dgate = colsum(dz * sub)
    dsub = dz * (1.0 + gate)
    dwout = mm_tn(merged, dsub)
    dmerged = mm_nt(dsub, wout)
    dpa, dpb = dmerged * sa, dmerged * sb
    dma = dmerged * pa * sa * (1.0 - sa)
    dmb = dmerged * pb * sb * (1.0 - sb)
    dwpa = mm_tn(ga, dpa)
    dwpb = mm_tn(gb, dpb)
    dga = mm_nt(dpa, wpa)
    dgb = mm_nt(dpb, wpb)
    dya = dga * sila
    dgpa = dga * ya * (sga * (1.0 + gpa * (1.0 - sga)))
    dyb = dgb * silb
    dgpb = dgb * yb * (sgb * (1.0 + gpb * (1.0 - sgb)))
    dgn_g = colsum(dyb * yn)
    dgn_b = colsum(dyb)
    dyn = dyb * gn_g
    dy = rs * (dyn - head_sum(dyn, bd) * inv - yn * head_sum(dyn * yn, bd) * inv)
    return (dz, dma, dmb, dgpa, dgpb, dya, dy, dyb,
            loss, dwout, dwpa, dwpb, dgn_g, dgn_b, dpost_g, dpost_b, dgate)


def mla_prep_bwd_tile(step0, tile0, q_c, kv_c, cos, sin, dq, dk, dv, gq, gkv, wq, wqr, wkn, wv):
    qn, qh, rq = rms_norm_fwd(q_c, gq)
    kvn, kvh, rkv = rms_norm_fwd(kv_c, gkv)
    dqc = dq * tile_lanes(cos, HEADS)
    dqs = dq * tile_lanes(sin, HEADS)
    dqn = mm_nt(dqc, wq) + mm_nt(dqs, wqr)
    dkvn = mm_nt(dk, wkn) + mm_nt(dv, wv)
    dkpe = dk[:, 0:LANE]
    for h in range(1, HEADS):
        dkpe = dkpe + dk[:, h * LANE:(h + 1) * LANE]
    dkr = dkpe * (cos * key_rope_mask(cos.shape))
    dkrr = dkpe * sin

    def rms_bwd(dyv, xh, r, g):
        dyg = dyv * g
        return r * (dyg - xh * rowmean(dyg * xh)), colsum(dyv * xh)

    dq_c, dgq = rms_bwd(dqn, qh, rq, gq)
    dkv_c, dgkv = rms_bwd(dkvn, kvh, rkv, gkv)
    return (dq_c, dkv_c, dkr, dkrr,
            mm_tn(qn, dqc), mm_tn(qn, dqs), mm_tn(kvn, dk), mm_tn(kvn, dv), dgq, dgkv)


def rwkv_prep_bwd_tile(step0, tile0, r0, k0, v0, l0, drt, dat, dbt, dkt, dvv, dlw, dyb, hr, hk, hv, hl,
                       mu_r, mu_k, mu_v, mu_l, w0, a0, k_k, k_a, w_dec, w_iclr, tril, same, bd, r_k,
                       cr, ck, cv, cl_):
    f = rwkv_prep_core(tile0, r0, k0, v0, l0, hr, hk, hv, hl, mu_r, mu_k, mu_v, mu_l, w0, a0, k_k, k_a,
                       w_dec, w_iclr, tril, same, bd)
    ur, uk, uv, ul, kk, k2, a_ic, sg, th = (f[n] for n in ("ur", "uk", "uv", "ul", "kk", "k2", "a_ic", "sg", "th"))
    lc, lw = f["lc"], f["lw"]
    e_neg = jnp.exp(-lc)
    dur = drt * jnp.exp(lc)
    da = dat * jnp.exp(lc - lw)
    db = dbt * e_neg
    dk2 = dkt * e_neg
    s = head_sum(ur * k2 * r_k, bd)
    duv = dvv + dyb * s
    ds = head_sum(dyb * uv, bd)
    dur = dur + ds * k2 * r_k
    dk2 = dk2 + ds * ur * r_k
    dr_k = colsum(ds * ur * k2)
    dkk = db * a_ic - da
    da_ic = db * kk + dk2 * uk * k_a
    duk = dk2 * (1.0 + (a_ic - 1.0) * k_a)
    dk_a = colsum(dk2 * uk * (a_ic - 1.0))
    dkkraw = jnp.where(f["nrm_raw"] > 1e-12, (dkk - kk * head_sum(dkk * kk, bd)) / f["nrm"], dkk * 1e12)
    duk = duk + dkkraw * k_k
    dk_k = colsum(dkkraw * uk)
    dai = da_ic * a_ic * (1.0 - a_ic)
    dd = dlw * (-DECAY_SCALE) * sg * (1.0 - sg)
    dul = mm_nt(dai, w_iclr) + mm_nt(dd, w_dec) * (1.0 - th * th)

    def unshift(du, x, prev, mu, carry_row):
        nxt = shift_rows_up(du, carry_row)
        return du * (1.0 - mu) + nxt * mu, colsum(du * (prev - x)), du[0:1, :]

    dr0, dmu_r, ncr = unshift(dur, r0, f["pr"], mu_r, cr)
    dk0, dmu_k, nck = unshift(duk, k0, f["pk"], mu_k, ck)
    dv0, dmu_v, ncv = unshift(duv, v0, f["pv"], mu_v, cv)
    dl0, dmu_l, ncl = unshift(dul, l0, f["pl"], mu_l, cl_)
    return (dr0, dk0, dv0, dl0,
            dmu_r, dmu_k, dmu_v, dmu_l, colsum(dd), colsum(dai), dk_k, dk_a, dr_k, mm_tn(th, dd), mm_tn(ul, dai),
            ncr, nck, ncv, ncl)


def in_backward(x, dz, pieces, mod, w_in_p, unrot):
    n_rows = x.shape[0]
    ts = ROW_TILE
    n_p = len(pieces)
    shard_cols = IN_WIDTH // N_DEV

    def body(*refs):
        x_ref, dz_ref = refs[:2]
        p_refs = refs[2:2 + n_p]
        mod_ref, w_ref, unrot_ref = refs[2 + n_p:5 + n_p]
        dx_ref, ht_ref, blocks_ref, dshift_ref, dscale_ref = refs[5 + n_p:]
        step0 = pl.program_id(0) == 0
        dma, dmb, dr0, dk0, dv0, dgpa, dgpb, dq_c, dkv_c, dkr, dkrr, dl0 = (r[...] for r in p_refs)
        dproj = jnp.concatenate([dma, dmb, dr0, dk0, dv0, dgpa, dgpb, dq_c, dkv_c, dkr, dkrr, dl0], axis=1)
        dh = mm_nt(dproj, w_ref[...])
        xhat, rstd = layer_norm_stats(x_ref[...])
        scale1 = 1.0 + mod_ref[1:2, :]
        dx_ref[...] = layer_norm_bwd(dh * scale1, xhat, rstd) + ALPHA * dz_ref[...]
        ht_ref[...] = jnp.transpose(xhat * scale1 + mod_ref[0:1, :]).astype(BF16)
        dkrope = (dkr + mm(dkrr, unrot_ref[...]))[:, NOPE:QK_DIM]
        natural = jnp.concatenate([dq_c, dkv_c, dkrope, dgpa, dr0, dk0, dv0, dl0, dgpb, dma, dmb], axis=1)
        for j in range(N_DEV):
            blocks_ref[j] = natural[:, j * shard_cols:(j + 1) * shard_cols].astype(BF16)
        for ref, val in ((dshift_ref, colsum(dh)), (dscale_ref, colsum(dh * xhat))):
            @pl.when(step0)
            def _(ref=ref, val=val):
                ref[...] = val

            @pl.when(jnp.logical_not(step0))
            def _(ref=ref, val=val):
                ref[...] += val

    row = lambda w: pl.BlockSpec((ts, w), lambda i: (i, 0))
    const = pl.BlockSpec(memory_space=pltpu.VMEM)
    vec = pl.BlockSpec((1, D_MODEL), lambda i: (0, 0))
    return pl.pallas_call(
        body, name="in_backward", grid=(n_rows // ts,),
        in_specs=[row(D_MODEL), row(D_MODEL)] + [row(p.shape[1]) for p in pieces] + [const] * 3,
        out_specs=[row(D_MODEL), pl.BlockSpec((D_MODEL, ts), lambda i: (0, i)),
                   pl.BlockSpec((N_DEV, ts, shard_cols), lambda i: (0, i, 0)), vec, vec],
        out_shape=[jax.ShapeDtypeStruct((n_rows, D_MODEL), F32), jax.ShapeDtypeStruct((D_MODEL, n_rows), BF16),
                   jax.ShapeDtypeStruct((N_DEV, n_rows, shard_cols), BF16),
                   jax.ShapeDtypeStruct((1, D_MODEL), F32), jax.ShapeDtypeStruct((1, D_MODEL), F32)],
        compiler_params=pltpu.CompilerParams(dimension_semantics=("arbitrary",), vmem_limit_bytes=VMEM_LIMIT),
    )(x, dz, *pieces, mod, w_in_p, unrot)


def in_weight_grad_exchange(h_t, dp_blocks, others, small, order):
    n = len(others)
    n_rows = h_t.shape[1]
    ts = 2 * ROW_TILE
    n_i = n_rows // ts
    shard_cols = dp_blocks.shape[2]
    last = N_DEV - 1

    def body(order_ref, h_ref, dp_ref, *rest):
        g_refs, s_ref = rest[:n], rest[n]
        rwin_ref, rg_refs, rs_ref = rest[n + 1], rest[n + 2:2 * n + 2], rest[2 * n + 2]
        acc, sendbuf, win_send, win_recv, o_send, o_recv, local_sems = rest[2 * n + 3:]
        b, i = pl.program_id(0), pl.program_id(1)
        me = my_position()
        mi = flat_index(me)

        def other_copies(k, src_index, dst_index):
            peer = flip(me, k)
            out = [pltpu.make_async_remote_copy(
                src_ref=g_refs[a].at[src_index], dst_ref=rg_refs[a].at[dst_index],
                send_sem=o_send.at[(n + 1) * (k - 1) + a], recv_sem=o_recv.at[(n + 1) * (k - 1) + a],
                device_id=peer, device_id_type=MESH_IDS) for a in range(n)]
            out.append(pltpu.make_async_remote_copy(
                src_ref=s_ref, dst_ref=rs_ref.at[dst_index],
                send_sem=o_send.at[(n + 1) * (k - 1) + n], recv_sem=o_recv.at[(n + 1) * (k - 1) + n],
                device_id=peer, device_id_type=MESH_IDS))
            return out

        def local_copies():
            out = [pltpu.make_async_copy(g_refs[a].at[mi], rg_refs[a].at[mi], local_sems.at[a]) for a in range(n)]
            out.append(pltpu.make_async_copy(s_ref, rs_ref.at[mi], local_sems.at[n]))
            return out

        def block_copy(step, src_index, dst_index):
            k = last - step
            peer = (me[0] ^ ((k >> 2) & 1), me[1] ^ ((k >> 1) & 1), me[2] ^ (k & 1))
            return pltpu.make_async_remote_copy(
                src_ref=sendbuf.at[src_index], dst_ref=rwin_ref.at[dst_index],
                send_sem=win_send.at[step], recv_sem=win_recv.at[step], device_id=peer, device_id_type=MESH_IDS)

        own_block = pltpu.make_async_copy(sendbuf.at[last], rwin_ref.at[mi], local_sems.at[n + 1])

        @pl.when(jnp.logical_and(b == 0, i == 0))
        def _():
            for cp in local_copies():
                cp.start()
            for k in range(1, N_DEV):
                for cp in other_copies(k, flat_index(flip(me, k)), mi):
                    cp.start()

        contrib = jnp.dot(h_ref[...], dp_ref[...], preferred_element_type=F32)

        @pl.when(i == 0)
        def _():
            acc[...] = contrib

        @pl.when(i > 0)
        def _():
            acc[...] += contrib

        @pl.when(i == n_i - 1)
        def _():
            sendbuf[b] = acc[...].astype(BF16)

            @pl.when(b < last)
            def _():
                block_copy(b, b, mi).start()

            @pl.when(b == last)
            def _():
                own_block.start()

        @pl.when(jnp.logical_and(b == last, i == n_i - 1))
        def _():
            for step in range(last):
                pi = flat_index(flip(me, last - step))
                block_copy(step, step, pi).wait_recv()
            for k in range(1, N_DEV):
                pi = flat_index(flip(me, k))
                for cp in other_copies(k, pi, pi):
                    cp.wait_recv()
            for step in range(last):
                block_copy(step, step, mi).wait_send()
            for k in range(1, N_DEV):
                for cp in other_copies(k, flat_index(flip(me, k)), mi):
                    cp.wait_send()
            for cp in local_copies():
                cp.wait()
            own_block.wait()

    hbm = pl.BlockSpec(memory_space=pl.ANY)
    n_sem = 7 * (n + 1)
    grid_spec = pltpu.PrefetchScalarGridSpec(
        num_scalar_prefetch=1, grid=(N_DEV, n_i),
        in_specs=[pl.BlockSpec((D_MODEL, ts), lambda b, i, order: (0, i)),
                  pl.BlockSpec((None, ts, shard_cols), lambda b, i, order: (order[b], i, 0))] + [hbm] * (n + 1),
        out_specs=[hbm] * (n + 2),
        scratch_shapes=[pltpu.VMEM((D_MODEL, shard_cols), F32), pltpu.VMEM((N_DEV, D_MODEL, shard_cols), BF16),
                        pltpu.SemaphoreType.DMA((last,)), pltpu.SemaphoreType.DMA((last,)),
                        pltpu.SemaphoreType.DMA((n_sem,)), pltpu.SemaphoreType.DMA((n_sem,)),
                        pltpu.SemaphoreType.DMA((n + 2,))])
    return pl.pallas_call(
        body, name="in_weight_grad_exchange", grid_spec=grid_spec,
        out_shape=[jax.ShapeDtypeStruct((N_DEV, D_MODEL, shard_cols), BF16)]
        + [jax.ShapeDtypeStruct(o.shape, o.dtype) for o in others]
        + [jax.ShapeDtypeStruct((N_DEV,) + small.shape, small.dtype)],
        compiler_params=pltpu.CompilerParams(dimension_semantics=("arbitrary", "arbitrary"),
                                             vmem_limit_bytes=VMEM_LIMIT),
    )(order, h_t, dp_blocks, *others, small)


def ada_weight_grad(c_all, dmod_cols):
    def body(c_ref, d_ref, o_ref):
        cv = c_ref[...]
        o_ref[...] = hdot_tn(cv * sigmoid(cv), d_ref[...])

    return pl.pallas_call(
        body, name="ada_weight_grad",
        out_shape=jax.ShapeDtypeStruct((c_all.shape[1], dmod_cols.shape[1]), F32),
    )(c_all, dmod_cols)


def adamw(parts, w, m, v, name):
    k, rows, cols = parts.shape
    rb = 128 if rows % 128 == 0 else rows

    def body(p_ref, w_ref, m_ref, v_ref, g_ref, d_ref, nm_ref, nv_ref):
        g = p_ref[0].astype(F32)
        for i in range(1, k):
            g = g + p_ref[i].astype(F32)
        nm = ADAM_B1 * m_ref[...] + (1.0 - ADAM_B1) * g
        nv = ADAM_B2 * v_ref[...] + (1.0 - ADAM_B2) * (g * g)
        m_hat = nm / (1.0 - ADAM_B1 ** ADAM_STEP)
        v_hat = nv / (1.0 - ADAM_B2 ** ADAM_STEP)
        g_ref[...] = g
        d_ref[...] = -ADAM_LR * (m_hat / (jnp.sqrt(v_hat) + ADAM_EPS) + ADAM_WD * w_ref[...])
        nm_ref[...] = nm
        nv_ref[...] = nv

    blk = pl.BlockSpec((rb, cols), lambda i: (i, 0))
    return pl.pallas_call(
        body, name=name, grid=(rows // rb,),
        in_specs=[pl.BlockSpec((k, rb, cols), lambda i: (0, i, 0)), blk, blk, blk],
        out_specs=[blk] * 4, out_shape=[jax.ShapeDtypeStruct((rows, cols), F32)] * 4,
        compiler_params=pltpu.CompilerParams(dimension_semantics=("arbitrary",), vmem_limit_bytes=VMEM_LIMIT),
    )(parts, w, m, v)


def rot_cols(w):
    return jnp.concatenate([-w[:, ROPE // 2:], w[:, :ROPE // 2]], axis=1)


def unrot_cols(dw):
    return jnp.concatenate([dw[:, ROPE // 2:], -dw[:, :ROPE // 2]], axis=1)


def columns_from_shards(g, rows, cols):
    return g.reshape(N_DEV, rows, cols).transpose(1, 0, 2).reshape(rows, N_DEV * cols)


def shards_from_columns(w, rows, cols):
    return w.reshape(rows, N_DEV, cols).transpose(1, 0, 2).reshape(N_DEV, rows * cols)


def permute_w_in(w):
    z = lambda n: jnp.zeros((D_MODEL, n), w.dtype)
    krope = w[:, N_KROPE:N_KROPE + ROPE]
    rw = N_RWKV
    return jnp.concatenate([
        w[:, N_MA:N_MA + 1024], w[:, N_MB:N_MB + 1024],
        w[:, rw:rw + 512], w[:, rw + 512:rw + 1024], w[:, rw + 1024:rw + 1536],
        w[:, N_GPA:N_GPA + 512], w[:, N_GPB:N_GPB + 512],
        w[:, N_QC:N_QC + 256], w[:, N_KVC:N_KVC + 128],
        z(NOPE), krope, z(LANE - QK_DIM), z(NOPE), rot_cols(krope), z(LANE - QK_DIM),
        w[:, rw + 1536:rw + 1664]], axis=1)


def unpermute_w_in_grad(d):
    rw = P_R
    krope = d[:, P_KR + NOPE:P_KR + QK_DIM] + unrot_cols(d[:, P_KRR + NOPE:P_KRR + QK_DIM])
    return jnp.concatenate([
        d[:, P_QC:P_QC + 256], d[:, P_KVC:P_KVC + 128], krope, d[:, P_GPA:P_GPA + 512],
        d[:, rw:rw + 1536], d[:, P_LORA:P_LORA + 128], d[:, P_GPB:P_GPB + 512],
        d[:, P_MA:P_MA + 1024], d[:, P_MB:P_MB + 1024]], axis=1)


def pad_heads_q(w_uq):
    w = w_uq.reshape(Q_RANK, HEADS, QK_DIM)
    zpad = jnp.zeros((Q_RANK, HEADS, LANE - QK_DIM), w.dtype)
    wq = jnp.concatenate([w, zpad], axis=2).reshape(Q_RANK, HEADS * LANE)
    pe = w[:, :, NOPE:]
    rot = jnp.concatenate([-pe[:, :, ROPE // 2:], pe[:, :, :ROPE // 2]], axis=2)
    wqr = jnp.concatenate([jnp.zeros((Q_RANK, HEADS, NOPE), w.dtype), rot, zpad], axis=2).reshape(Q_RANK, HEADS * LANE)
    return wq, wqr


def unpad_heads_q_grad(dwq, dwqr):
    a = dwq.reshape(Q_RANK, HEADS, LANE)
    r = dwqr.reshape(Q_RANK, HEADS, LANE)[:, :, NOPE:QK_DIM]
    pe = a[:, :, NOPE:QK_DIM] + jnp.concatenate([r[:, :, ROPE // 2:], -r[:, :, :ROPE // 2]], axis=2)
    return jnp.concatenate([a[:, :, :NOPE], pe], axis=2).reshape(Q_RANK, HEADS * QK_DIM)


def pad_heads_kv(w_ukv):
    w = w_ukv.reshape(KV_RANK, HEADS, 2 * HEAD)
    z = jnp.zeros((KV_RANK, HEADS, HEAD), w.dtype)
    wkn = jnp.concatenate([w[:, :, :NOPE], z], axis=2).reshape(KV_RANK, HEADS * LANE)
    val = w[:, :, NOPE:]
    odd = (jnp.arange(HEADS) % 2 == 1)[None, :, None]
    wv = jnp.concatenate([jnp.where(odd, 0, val), jnp.where(odd, val, 0)], axis=2).reshape(KV_RANK, HEADS * LANE)
    return wkn, wv


def unpad_heads_kv_grad(dwkn, dwv):
    a = dwkn.reshape(KV_RANK, HEADS, LANE)[:, :, :NOPE]
    b = dwv.reshape(KV_RANK, HEADS, LANE)
    odd = (jnp.arange(HEADS) % 2 == 1)[None, :, None]
    val = jnp.where(odd, b[:, :, HEAD:], b[:, :, :HEAD])
    return jnp.concatenate([a, val], axis=2).reshape(KV_RANK, HEADS * 2 * HEAD)


def kernel(x, c, positions, w_ada, b_ada, w_in, q_norm_g, w_uq, kv_norm_g, w_ukv, mu_rwkv, w0, w_decay_up, a0, w_iclr_up, k_k, k_a, r_k, gn_g, gn_b, w_proj_a, w_proj_b, w_out, post_g, post_b, loss_target, m_w_ada, m_b_ada, m_w_in, m_q_norm_g, m_w_uq, m_kv_norm_g, m_w_ukv, m_mu_rwkv, m_w0, m_w_decay_up, m_a0, m_w_iclr_up, m_k_k, m_k_a, m_r_k, m_gn_g, m_gn_b, m_w_proj_a, m_w_proj_b, m_w_out, m_post_g, m_post_b, v_w_ada, v_b_ada, v_w_in, v_q_norm_g, v_w_uq, v_kv_norm_g, v_w_ukv, v_mu_rwkv, v_w0, v_w_decay_up, v_a0, v_w_iclr_up, v_k_k, v_k_a, v_r_k, v_gn_g, v_gn_b, v_w_proj_a, v_w_proj_b, v_w_out, v_post_g, v_post_b):
    weights = dict(w_ada=w_ada, b_ada=b_ada, w_in=w_in, q_norm_g=q_norm_g, w_uq=w_uq, kv_norm_g=kv_norm_g,
                   w_ukv=w_ukv, mu_rwkv=mu_rwkv, w0=w0, w_decay_up=w_decay_up, a0=a0, w_iclr_up=w_iclr_up,
                   k_k=k_k, k_a=k_a, r_k=r_k, gn_g=gn_g, gn_b=gn_b, w_proj_a=w_proj_a, w_proj_b=w_proj_b,
                   w_out=w_out, post_g=post_g, post_b=post_b)
    mom1 = dict(w_ada=m_w_ada, b_ada=m_b_ada, w_in=m_w_in, q_norm_g=m_q_norm_g, w_uq=m_w_uq, kv_norm_g=m_kv_norm_g,
                w_ukv=m_w_ukv, mu_rwkv=m_mu_rwkv, w0=m_w0, w_decay_up=m_w_decay_up, a0=m_a0, w_iclr_up=m_w_iclr_up,
                k_k=m_k_k, k_a=m_k_a, r_k=m_r_k, gn_g=m_gn_g, gn_b=m_gn_b, w_proj_a=m_w_proj_a, w_proj_b=m_w_proj_b,
                w_out=m_w_out, post_g=m_post_g, post_b=m_post_b)
    mom2 = dict(w_ada=v_w_ada, b_ada=v_b_ada, w_in=v_w_in, q_norm_g=v_q_norm_g, w_uq=v_w_uq, kv_norm_g=v_kv_norm_g,
                w_ukv=v_w_ukv, mu_rwkv=v_mu_rwkv, w0=v_w0, w_decay_up=v_w_decay_up, a0=v_a0, w_iclr_up=v_w_iclr_up,
                k_k=v_k_k, k_a=v_k_a, r_k=v_r_k, gn_g=v_gn_g, gn_b=v_gn_b, w_proj_a=v_w_proj_a, w_proj_b=v_w_proj_b,
                w_out=v_w_out, post_g=v_post_g, post_b=v_post_b)
    names = list(weights)
    n_rows = x.shape[1]
    me = 4 * lax.axis_index("x") + 2 * lax.axis_index("y") + lax.axis_index("c")
    xr = x[0]
    tgt = loss_target[0]
    row = lambda a: a.reshape(1, -1)

    gathered = gather_shards([weights[n][0].astype(BF16) for n, _, _ in SHARDED] + [c])
    c_all = gathered[-1].reshape(N_DEV, D_MODEL)
    full = {}
    for (n, r, cdim), part in zip(SHARDED, gathered):
        full[n] = part.reshape(N_DEV * r, cdim) if n == "w_out" else columns_from_shards(part, r, cdim)
    w_in_p = permute_w_in(full["w_in"])
    wq, wqr = pad_heads_q(full["w_uq"])
    wkn, wv = pad_heads_kv(full["w_ukv"])
    zl = jnp.zeros((LORA, WIDTH), BF16)
    w_dec = jnp.concatenate([full["w_decay_up"], zl], axis=0)
    w_iclr = jnp.concatenate([zl, full["w_iclr_up"]], axis=0)
    wpa, wpb, wout = full["w_proj_a"], full["w_proj_b"], full["w_out"]

    mod_all = ada_modulation(c_all, w_ada[0], b_ada.reshape(N_DEV, -1))
    mod = lax.dynamic_index_in_dim(mod_all, me, axis=1, keepdims=False).reshape(3, D_MODEL)

    (proj,) = row_call("fwd_in", fwd_in_tile, n_rows, [(xr, D_MODEL, 0)], [mod, w_in_p], [(P_WIDTH, F32)])
    pcol = lambda off_, w: (proj, w, off_ // w)

    inv_freq = ROPE_THETA ** (-jnp.arange(0, ROPE, 2, dtype=F32) / ROPE)
    ang = positions[0].astype(F32)[:, None] * inv_freq
    ones_n, zeros_n, zeros_p = jnp.ones((n_rows, NOPE), F32), jnp.zeros((n_rows, NOPE), F32), jnp.zeros((n_rows, LANE - QK_DIM), F32)
    cos_t = jnp.concatenate([ones_n, jnp.cos(ang), jnp.cos(ang), zeros_p], axis=1)
    sin_t = jnp.concatenate([zeros_n, jnp.sin(ang), jnp.sin(ang), zeros_p], axis=1)

    gq, gkv = q_norm_g, kv_norm_g
    mla_consts = [gq, gkv, wq, wqr, wkn, wv]
    q, k, v = row_call(
        "mla_prep", mla_prep_tile, n_rows,
        [pcol(P_QC, 256), pcol(P_KVC, 128), pcol(P_KR, 128), pcol(P_KRR, 128), (cos_t, LANE, 0), (sin_t, LANE, 0)],
        mla_consts, [(HEADS * LANE, BF16)] * 3)
    ya, lse = attention_forward(q, k, v)

    t_idx = jnp.arange(ROW_TILE)
    same_chunk = (t_idx[:, None] // CHUNK) == (t_idx[None, :] // CHUNK)
    same = same_chunk.astype(F32)
    tril = (same_chunk & (t_idx[:, None] >= t_idx[None, :])).astype(F32)
    l_idx = jnp.arange(LANE)
    bd = ((l_idx[:, None] // HEAD) == (l_idx[None, :] // HEAD)).astype(F32)
    mu = mu_rwkv
    mu_r, mu_k, mu_v, mu_l = mu[:, 0:512], mu[:, 512:1024], mu[:, 1024:1536], mu[:, 1536:1664]
    rk_row = row(r_k)
    rwkv_consts = [mu_r, mu_k, mu_v, mu_l, w0, a0, k_k, k_a, w_dec, w_iclr, tril, same, bd]
    rwkv_rows = [pcol(P_R, 512), pcol(P_K, 512), pcol(P_V, 512), pcol(P_LORA, 128)]
    rt, at, bt, kt, clf, uv, ur, k2 = row_call(
        "rwkv_prep", rwkv_prep_tile, n_rows, rwkv_rows, rwkv_consts, [(WIDTH, F32)] * 8, halo_in=rwkv_rows)
    y, m0s, state_maps, out_maps = wkv_forward(at, bt, kt, rt, uv, clf)

    tail = row_call(
        "tail", tail_tile, n_rows,
        [(xr, D_MODEL, 0), (tgt, D_MODEL, 0), pcol(P_MA, 1024), pcol(P_MB, 1024), pcol(P_GPA, 512), pcol(P_GPB, 512),
         (ya, WIDTH, 0), (y, WIDTH, 0), (ur, WIDTH, 0), (k2, WIDTH, 0), (uv, WIDTH, 0)],
        [mod, wpa, wpb, wout, gn_g, gn_b, rk_row, post_g, post_b, bd],
        [(D_MODEL, F32), (1024, F32), (1024, F32), (512, F32), (512, F32), (WIDTH, F32), (WIDTH, F32), (WIDTH, F32)],
        acc_out=[((1, LANE), F32), ((D_MODEL, D_MODEL), F32), ((WIDTH, D_MODEL), F32), ((WIDTH, D_MODEL), F32),
                 ((1, WIDTH), F32), ((1, WIDTH), F32), ((1, D_MODEL), F32), ((1, D_MODEL), F32), ((1, D_MODEL), F32)])
    (dz, dma, dmb, dgpa, dgpb, dya, dy, dyb,
     loss_row, g_wout, g_wpa, g_wpb, g_gn_g, g_gn_b, g_post_g, g_post_b, dgate) = tail

    dq, dk, dv = attention_backward(q, k, v, ya, dya, lse)
    dq_c, dkv_c, dkr, dkrr, g_wq, g_wqr, g_wkn, g_wv, g_gq, g_gkv = row_call(
        "mla_prep_bwd", mla_prep_bwd_tile, n_rows,
        [pcol(P_QC, 256), pcol(P_KVC, 128), (cos_t, LANE, 0), (sin_t, LANE, 0),
         (dq, HEADS * LANE, 0), (dk, HEADS * LANE, 0), (dv, HEADS * LANE, 0)],
        mla_consts, [(256, F32), (128, F32), (128, F32), (128, F32)],
        acc_out=[((Q_RANK, HEADS * LANE), F32)] * 2 + [((KV_RANK, HEADS * LANE), F32)] * 2
        + [((1, Q_RANK), F32), ((1, KV_RANK), F32)])

    dat, dbt, dkt, drt, dvv, dlw = wkv_backward(at, bt, kt, rt, uv, clf, m0s, state_maps, out_maps, dy)
    (dr0, dk0, dv0, dl0, g_mu_r, g_mu_k, g_mu_v, g_mu_l, g_w0, g_a0, g_k_k, g_k_a, g_r_k, g_wdec, g_wiclr) = row_call(
        "rwkv_prep_bwd", rwkv_prep_bwd_tile, n_rows,
        rwkv_rows + [(drt, WIDTH, 0), (dat, WIDTH, 0), (dbt, WIDTH, 0), (dkt, WIDTH, 0), (dvv, WIDTH, 0),
                     (dlw, WIDTH, 0), (dyb, WIDTH, 0)],
        rwkv_consts + [rk_row], [(512, F32), (512, F32), (512, F32), (128, F32)],
        acc_out=[((1, 512), F32)] * 3 + [((1, 128), F32)] + [((1, 512), F32)] * 5 + [((LANE, WIDTH), F32)] * 2,
        halo_in=rwkv_rows, carry=[512, 512, 512, 128], reverse=True)

    li = jnp.arange(LANE)
    src, dst = li[:, None], li[None, :]
    half = ROPE // 2
    unrot = (jnp.where((dst >= NOPE) & (dst < NOPE + half) & (src == dst + half), 1.0, 0.0)
             - jnp.where((dst >= NOPE + half) & (dst < QK_DIM) & (src == dst - half), 1.0, 0.0)).astype(BF16)
    dx, h_t, dproj_blocks, dshift, dscale = in_backward(
        xr, dz, [dma, dmb, dr0, dk0, dv0, dgpa, dgpb, dq_c, dkv_c, dkr, dkrr, dl0], mod, w_in_p, unrot)

    grads_full = {
        "w_uq": unpad_heads_q_grad(g_wq, g_wqr), "w_ukv": unpad_heads_kv_grad(g_wkn, g_wv),
        "w_decay_up": g_wdec[:LORA], "w_iclr_up": g_wiclr[LORA:],
        "w_proj_a": g_wpa, "w_proj_b": g_wpb, "w_out": g_wout}
    blocks = [(grads_full[n].reshape(N_DEV, r, cdim) if n == "w_out"
               else grads_full[n].reshape(r, N_DEV, cdim).transpose(1, 0, 2)).astype(BF16) for n, r, cdim in SHARDED[1:]]
    dmod = jnp.concatenate([dshift, dscale, dgate], axis=1)
    small = jnp.concatenate([dmod, g_gq, g_gkv, g_mu_r, g_mu_k, g_mu_v, g_mu_l, g_w0, g_a0, g_k_k, g_k_a, g_r_k,
                             g_gn_g, g_gn_b, g_post_g, g_post_b, loss_row], axis=1)
    order = jnp.bitwise_xor(me, N_DEV - 1 - jnp.arange(N_DEV, dtype=jnp.int32)).astype(jnp.int32)
    *got_blocks, got_small = in_weight_grad_exchange(h_t, dproj_blocks, blocks, small, order)
    loss = jnp.sum(got_small[:, 0, SMALL_ELEMS])

    ada_cols = w_ada.shape[2]
    dmod_all = got_small[:, 0, :3 * D_MODEL]
    got_small = got_small[:, :, :SMALL_ELEMS]
    g_ada = ada_weight_grad(c_all, lax.dynamic_slice_in_dim(dmod_all, me * ada_cols, ada_cols, axis=1))

    def small_row(tree):
        return jnp.concatenate([tree[n].reshape(1, -1) for n, _ in SMALL], axis=1)

    outs = [dict() for _ in range(4)]
    res = adamw(g_ada[None], w_ada[0], m_w_ada[0], v_w_ada[0], "adamw_w_ada")
    for kind in range(4):
        outs[kind]["w_ada"] = res[kind][None]
    for (n, r, cdim), got in zip(SHARDED, got_blocks):
        res = adamw(got, weights[n][0], mom1[n][0], mom2[n][0], "adamw_" + n)
        for kind in range(4):
            outs[kind][n] = res[kind][None]
    res = adamw(got_small, small_row(weights), small_row(mom1), small_row(mom2), "adamw_small")
    for kind in range(4):
        off = 0
        for n, size in SMALL:
            outs[kind][n] = res[kind][:, off:off + size].reshape(weights[n].shape)
            off += size
    return (loss, dx[None], *[outs[0][n] for n in names], *[outs[1][n] for n in names],
            *[outs[2][n] for n in names], *[outs[3][n] for n in names])
```

```python
import functools
import math

import jax
import jax.numpy as jnp
from jax import lax
from jax.experimental import pallas as pl
from jax.experimental.pallas import tpu as pltpu

F32 = jnp.float32
BF16 = jnp.bfloat16
HIGHEST = lax.Precision.HIGHEST
MESH_IDS = pl.DeviceIdType.MESH

N_DEV = 8
D_MODEL = 1024
LN_EPS = 1e-5
RMS_EPS = 1e-6
GN_EPS = 64e-5
HEADS = 8
Q_RANK = 256
KV_RANK = 128
ROPE = 32
NOPE = 64
QK_DIM = NOPE + ROPE
WIDTH = 512
HEAD = 64
LORA = 64
CHUNK = 64
DEPTH = 1
ALPHA = (2.0 * DEPTH) ** 0.25
ROPE_THETA = 10000.0
ATTN_SCALE = QK_DIM ** -0.5
DECAY_SCALE = math.exp(-0.5)

ADAM_LR = 0.001
ADAM_B1 = 0.9
ADAM_B2 = 0.999
ADAM_EPS = 1e-08
ADAM_WD = 0.01
ADAM_STEP = 10

LANE = 128
PAIR = 2 * HEAD
ROW_TILE = 256
ATTN_TILE = 256
WKV_CHUNKS_PER_STEP = 8
VMEM_LIMIT = 56 * 1024 * 1024

P_MA, P_MB, P_R, P_K, P_V, P_GPA, P_GPB, P_QC, P_KVC, P_KR, P_KRR, P_LORA = (
    0, 1024, 2048, 2560, 3072, 3584, 4096, 4608, 4864, 4992, 5120, 5248)
P_WIDTH = 5376
DW_BLOCK = 768

N_QC, N_KVC, N_KROPE, N_GPA, N_RWKV, N_GPB, N_MA, N_MB = 0, 256, 384, 416, 928, 2592, 3104, 4128
IN_WIDTH = 5152

SHARDED = (("w_in", 1024, 644), ("w_uq", 256, 96), ("w_ukv", 128, 128), ("w_decay_up", 64, 64),
           ("w_iclr_up", 64, 64), ("w_proj_a", 512, 128), ("w_proj_b", 512, 128), ("w_out", 128, 1024))
SHARD_ELEMS = sum(r * c for _, r, c in SHARDED)
SHARD_ROWS = SHARD_ELEMS // LANE
GATHER_ROWS = SHARD_ROWS + 2 * D_MODEL // LANE
SMALL = (("b_ada", 3072), ("q_norm_g", 256), ("kv_norm_g", 128), ("mu_rwkv", 1664), ("w0", 512), ("a0", 512),
         ("k_k", 512), ("k_a", 512), ("r_k", 512), ("gn_g", 512), ("gn_b", 512), ("post_g", 1024), ("post_b", 1024))
SMALL_ELEMS = sum(n for _, n in SMALL)
SMALL_ROWS = SMALL_ELEMS // LANE


def mm(a, b):
    return jnp.dot(a.astype(BF16), b.astype(BF16), preferred_element_type=F32)


def mm_nt(a, b):
    return lax.dot_general(a.astype(BF16), b.astype(BF16), (((1,), (1,)), ((), ())), preferred_element_type=F32)


def mm_tn(a, b):
    return lax.dot_general(a.astype(BF16), b.astype(BF16), (((0,), (0,)), ((), ())), preferred_element_type=F32)


def hdot(a, b):
    return jnp.dot(a, b, precision=HIGHEST, preferred_element_type=F32)


def hdot_nt(a, b):
    return lax.dot_general(a, b, (((1,), (1,)), ((), ())), precision=HIGHEST, preferred_element_type=F32)


def hdot_tn(a, b):
    return lax.dot_general(a, b, (((0,), (0,)), ((), ())), precision=HIGHEST, preferred_element_type=F32)


def sigmoid(x):
    return 1.0 / (1.0 + jnp.exp(-x))


def colsum(x):
    return jnp.sum(x, axis=0, keepdims=True)


def rowmean(x):
    return jnp.mean(x, axis=-1, keepdims=True)


def layer_norm_stats(x):
    xc = x - rowmean(x)
    rstd = lax.rsqrt(rowmean(xc * xc) + LN_EPS)
    return xc * rstd, rstd


def layer_norm_bwd(dy, xhat, rstd):
    return rstd * (dy - rowmean(dy) - xhat * rowmean(dy * xhat))


def head_sum(x, bd):
    return jnp.concatenate([hdot(x[:, p * LANE:(p + 1) * LANE], bd) for p in range(x.shape[1] // LANE)], axis=1)


def tile_lanes(t, n):
    return jnp.concatenate([t] * n, axis=1)


def row_iota(shape):
    return lax.broadcasted_iota(jnp.int32, shape, 0)


def lane_iota(shape):
    return lax.broadcasted_iota(jnp.int32, shape, 1)


def shift_rows_down(x, row0):
    rolled = pltpu.roll(x, 1, axis=0)
    return jnp.where(row_iota(x.shape) == 0, row0, rolled)


def shift_rows_up(x, row_last):
    rolled = pltpu.roll(x, x.shape[0] - 1, axis=0)
    return jnp.where(row_iota(x.shape) == x.shape[0] - 1, row_last, rolled)


def row_call(name, fn, n_rows, row_in, const_in, row_out, acc_out=(), halo_in=(), carry=(), reverse=False):
    ts = ROW_TILE
    n_tiles = n_rows // ts
    n_in = len(row_in) + len(halo_in) + len(const_in)
    n_ro, n_ao = len(row_out), len(acc_out)

    def tile_of(g):
        return (n_tiles - 1 - g) if reverse else g

    def body(*refs):
        ins = refs[:n_in]
        ro = refs[n_in:n_in + n_ro]
        ao = refs[n_in + n_ro:n_in + n_ro + n_ao]
        cr = refs[n_in + n_ro + n_ao:]
        g = pl.program_id(0)
        step0 = g == 0
        tile0 = tile_of(g) == 0
        for r in cr:
            @pl.when(step0)
            def _(r=r):
                r[...] = jnp.zeros_like(r)
        vals = [r[...] for r in ins]
        outs = fn(step0, tile0, *vals, *[c[0:1, :] for c in cr])
        for r, v in zip(ro, outs[:n_ro]):
            r[...] = v.astype(r.dtype)
        for r, v in zip(ao, outs[n_ro:n_ro + n_ao]):
            @pl.when(step0)
            def _(r=r, v=v):
                r[...] = v.astype(r.dtype)

            @pl.when(jnp.logical_not(step0))
            def _(r=r, v=v):
                r[...] += v.astype(r.dtype)
        for r, v in zip(cr, outs[n_ro + n_ao:]):
            r[0:1, :] = v

    in_specs = [pl.BlockSpec((ts, w), functools.partial(lambda g, cb: (tile_of(g), cb), cb=cb)) for _, w, cb in row_in]
    in_specs += [pl.BlockSpec((8, w), functools.partial(
        lambda g, cb: (jnp.maximum(tile_of(g) * (ts // 8) - 1, 0), cb), cb=cb)) for _, w, cb in halo_in]
    in_specs += [pl.BlockSpec(memory_space=pltpu.VMEM) for _ in const_in]
    out_specs = [pl.BlockSpec((ts, w), lambda g: (tile_of(g), 0)) for w, _ in row_out]
    out_specs += [pl.BlockSpec(s, lambda g: (0, 0)) for s, _ in acc_out]
    out_shape = [jax.ShapeDtypeStruct((n_rows, w), d) for w, d in row_out]
    out_shape += [jax.ShapeDtypeStruct(s, d) for s, d in acc_out]
    return pl.pallas_call(
        body, name=name, grid=(n_tiles,), in_specs=in_specs, out_specs=out_specs, out_shape=out_shape,
        scratch_shapes=[pltpu.VMEM((8, w), F32) for w in carry],
        compiler_params=pltpu.CompilerParams(dimension_semantics=("arbitrary",), vmem_limit_bytes=VMEM_LIMIT),
    )(*[a for a, _, _ in row_in], *[a for a, _, _ in halo_in], *const_in)


def my_position():
    return lax.axis_index("x"), lax.axis_index("y"), lax.axis_index("c")


def flip(pos, k):
    x, y, c = pos
    dx, dy, dc = (k >> 2) & 1, (k >> 1) & 1, k & 1
    return (1 - x if dx else x, 1 - y if dy else y, 1 - c if dc else c)


def flat_index(pos):
    return 4 * pos[0] + 2 * pos[1] + pos[2]


def gather_shards(shards):
    n = len(shards)

    def body(*refs):
        x_refs, out_refs = refs[:n], refs[n:2 * n]
        send_sems, recv_sems, local_sems = refs[2 * n:]
        x, y, c = my_position()
        me, sibling = (x, y, c), (x, y, 1 - c)
        chips = [(1 - x, y), (x, 1 - y), (1 - x, 1 - y)]

        def copy(a, k, block, to, from_input=False):
            slot = out_refs[a].at[flat_index(block)]
            return pltpu.make_async_remote_copy(
                src_ref=x_refs[a] if from_input else slot, dst_ref=slot,
                send_sem=send_sems.at[7 * a + k], recv_sem=recv_sems.at[7 * a + k],
                device_id=to, device_id_type=MESH_IDS)

        mine = [pltpu.make_async_copy(x_refs[a], out_refs[a].at[flat_index(me)], local_sems.at[a]) for a in range(n)]
        for cp in mine:
            cp.start()
        first = []
        for a in range(n):
            first.append(copy(a, 0, me, sibling, from_input=True))
            first += [copy(a, 1 + j, me, (*chip, c), from_input=True) for j, chip in enumerate(chips)]
        for cp in first:
            cp.start()
        passed = []
        for j, chip in enumerate(chips):
            for a in range(n):
                copy(a, 1 + j, (*chip, c), me).wait_recv()
                cp = copy(a, 4 + j, (*chip, c), sibling)
                cp.start()
                passed.append(cp)
        for a in range(n):
            copy(a, 0, sibling, me).wait_recv()
            for j, chip in enumerate(chips):
                copy(a, 4 + j, (*chip, 1 - c), me).wait_recv()
        for cp in first + passed:
            cp.wait_send()
        for cp in mine:
            cp.wait()

    return pl.pallas_call(
        body, name="gather_shards",
        out_shape=[jax.ShapeDtypeStruct((N_DEV,) + s.shape, s.dtype) for s in shards],
        in_specs=[pl.BlockSpec(memory_space=pl.ANY)] * n, out_specs=[pl.BlockSpec(memory_space=pl.ANY)] * n,
        scratch_shapes=[pltpu.SemaphoreType.DMA((7 * n,)), pltpu.SemaphoreType.DMA((7 * n,)),
                        pltpu.SemaphoreType.DMA((n,))],
    )(*shards)


def ada_modulation(c_all, w_ada_loc, b_ada_blocks):
    cols = w_ada_loc.shape[1]

    def body(c_ref, w_ref, b_ref, out_ref, send_sems, recv_sems):
        me = my_position()
        mi = flat_index(me)
        cv = c_ref[...]
        res = hdot(cv * sigmoid(cv), w_ref[...]) + b_ref[pl.ds(mi, 1), :]
        out_ref[mi] = res
        sends = []
        for k in range(1, N_DEV):
            cp = pltpu.make_async_remote_copy(
                src_ref=out_ref.at[mi], dst_ref=out_ref.at[mi], send_sem=send_sems.at[k - 1],
                recv_sem=recv_sems.at[k - 1], device_id=flip(me, k), device_id_type=MESH_IDS)
            cp.start()
            sends.append(cp)
        for k in range(1, N_DEV):
            pi = flat_index(flip(me, k))
            pltpu.make_async_remote_copy(
                src_ref=out_ref.at[pi], dst_ref=out_ref.at[pi], send_sem=send_sems.at[k - 1],
                recv_sem=recv_sems.at[k - 1], device_id=flip(me, k), device_id_type=MESH_IDS).wait_recv()
        for cp in sends:
            cp.wait_send()

    return pl.pallas_call(
        body, name="ada_modulation",
        out_shape=jax.ShapeDtypeStruct((N_DEV, N_DEV, cols), F32),
        in_specs=[pl.BlockSpec(memory_space=pltpu.VMEM)] * 3, out_specs=pl.BlockSpec(memory_space=pltpu.VMEM),
        scratch_shapes=[pltpu.SemaphoreType.DMA((7,)), pltpu.SemaphoreType.DMA((7,))],
    )(c_all, w_ada_loc, b_ada_blocks)


def fwd_in_tile(step0, tile0, x, mod, w_in_p):
    xhat, _ = layer_norm_stats(x)
    h = xhat * (1.0 + mod[1:2]) + mod[0:1]
    return (mm(h, w_in_p),)


def rms_norm_fwd(x, g):
    r = lax.rsqrt(rowmean(x * x) + RMS_EPS)
    xh = x * r
    return xh * g, xh, r


def key_rope_mask(shape):
    return (lane_iota(shape) >= NOPE).astype(F32)


def mla_prep_tile(step0, tile0, q_c, kv_c, kr, krr, cos, sin, gq, gkv, wq, wqr, wkn, wv):
    qn, _, _ = rms_norm_fwd(q_c, gq)
    kvn, _, _ = rms_norm_fwd(kv_c, gkv)
    q = mm(qn, wq) * tile_lanes(cos, HEADS) + mm(qn, wqr) * tile_lanes(sin, HEADS)
    kpe = kr * (cos * key_rope_mask(cos.shape)) + krr * sin
    k = mm(kvn, wkn) + tile_lanes(kpe, HEADS)
    v = mm(kvn, wv)
    return q, k, v


def rwkv_prep_core(tile0, r0, k0, v0, l0, hr, hk, hv, hl, mu_r, mu_k, mu_v, mu_l, w0, a0, k_k, k_a,
                   w_dec, w_iclr, tril, same, bd):
    def shifted(x, halo, mu):
        row0 = jnp.where(tile0, 0.0, halo[7:8, :])
        prev = shift_rows_down(x, row0)
        return x + (prev - x) * mu, prev

    ur, pr = shifted(r0, hr, mu_r)
    uk, pk = shifted(k0, hk, mu_k)
    uv, pv = shifted(v0, hv, mu_v)
    ul, plo = shifted(l0, hl, mu_l)
    th = jnp.tanh(ul)
    sg = sigmoid(w0 + mm(th, w_dec))
    lw = -DECAY_SCALE * sg
    a_ic = sigmoid(a0 + mm(ul, w_iclr))
    kkraw = uk * k_k
    nrm_raw = jnp.sqrt(head_sum(kkraw * kkraw, bd))
    nrm = jnp.maximum(nrm_raw, 1e-12)
    kk = kkraw / nrm
    k2 = uk * (1.0 + (a_ic - 1.0) * k_a)
    lc = hdot(tril, lw)
    lcl = hdot(same, lw)
    return dict(ur=ur, uk=uk, uv=uv, ul=ul, pr=pr, pk=pk, pv=pv, pl=plo, th=th, sg=sg, lw=lw, a_ic=a_ic,
                kkraw=kkraw, nrm_raw=nrm_raw, nrm=nrm, kk=kk, k2=k2, lc=lc, lcl=lcl)


def rwkv_prep_tile(step0, tile0, r0, k0, v0, l0, hr, hk, hv, hl, *consts):
    f = rwkv_prep_core(tile0, r0, k0, v0, l0, hr, hk, hv, hl, *consts)
    lc, lw = f["lc"], f["lw"]
    e_neg = jnp.exp(-lc)
    rt = f["ur"] * jnp.exp(lc)
    at = -f["kk"] * jnp.exp(lc - lw)
    bt = f["kk"] * f["a_ic"] * e_neg
    kt = f["k2"] * e_neg
    return rt, at, bt, kt, jnp.exp(f["lcl"]), f["uv"], f["ur"], f["k2"]


def wkv_masks():
    lane = lane_iota((1, PAIR))
    m_lo = (lane < HEAD).astype(F32)
    ri = row_iota((CHUNK, CHUNK))
    ci = lane_iota((CHUNK, CHUNK))
    r2 = row_iota((PAIR, PAIR))
    c2 = lane_iota((PAIR, PAIR))
    bd = ((r2 < HEAD) == (c2 < HEAD)).astype(F32)
    eye2 = (r2 == c2).astype(F32)
    return (m_lo, 1.0 - m_lo), ri > ci, ri >= ci, (ri == ci).astype(F32), bd, eye2


def wkv_chunks_pre(chunks, masks):
    ms, strict, incl, eye, bd, eye2 = masks
    items = [(c, m) for c in range(len(chunks)) for m in ms]
    at, bt, kt, rt, v, cl = (list(t) for t in zip(*chunks))
    atm = [at[c] * m for c, m in items]
    rtm = [rt[c] * m for c, m in items]
    aab = [jnp.where(strict, mm_nt(x, bt[c]), 0.0) for x, (c, _) in zip(atm, items)]
    aak = [jnp.where(strict, mm_nt(x, kt[c]), 0.0) for x, (c, _) in zip(atm, items)]
    prb = [jnp.where(incl, mm_nt(x, bt[c]), 0.0) for x, (c, _) in zip(rtm, items)]
    prk = [jnp.where(incl, mm_nt(x, kt[c]), 0.0) for x, (c, _) in zip(rtm, items)]
    tinv = [eye + a for a in aab]
    power = aab
    for _ in range(5):
        power = [mm(p, p) for p in power]
        tinv = [t + mm(t, p) for t, p in zip(tinv, power)]

    def by_chunk(parts):
        return [parts[2 * c] + parts[2 * c + 1] for c in range(len(chunks))]

    w = by_chunk([mm(a, v[c] * m) for a, (c, m) in zip(aak, items)])
    ah = by_chunk([mm(t, x) for t, x in zip(tinv, atm)])
    wh = by_chunk([mm(t, w[c] * m) for t, (c, m) in zip(tinv, items)])
    rh = [r + d for r, d in zip(rt, by_chunk([mm(p, ah[c] * m) for p, (c, m) in zip(prb, items)]))]
    yh = by_chunk([mm(p, wh[c] * m) + mm(q, v[c] * m) for p, q, (c, m) in zip(prb, prk, items)])
    bc = [b * c_ for b, c_ in zip(bt, cl)]
    kc = [k * c_ for k, c_ in zip(kt, cl)]
    g = [eye2 * c_ + bd * mm_tn(b, a) for c_, b, a in zip(cl, bc, ah)]
    h = [bd * (mm_tn(b, w_) + mm_tn(k, v_)) for b, w_, k, v_ in zip(bc, wh, kc, v)]
    return g, h, rh, yh, (items, atm, rtm, tinv, aak, prb, prk, ah, wh, bc, kc)


def wkv_chunks_grad(chunks, m0, dy, dm1, masks):
    ms, strict, incl, eye, bd, eye2 = masks
    n = len(chunks)
    _, _, _, _, (items, atm, rtm, tinv, aak, prb, prk, ah, wh, bc, kc) = wkv_chunks_pre(chunks, masks)
    at, bt, kt, rt, v, cl = (list(t) for t in zip(*chunks))

    def by_chunk(parts):
        return [parts[2 * c] + parts[2 * c + 1] for c in range(n)]

    u = [mm(a, m) + w for a, m, w in zip(ah, m0, wh)]
    dm1 = [d * bd for d in dm1]
    dym = [dy[c] * m for c, m in items]
    du = [mm(b, d) + e for b, d, e in zip(bc, dm1, by_chunk([mm_tn(p, x) for p, x in zip(prb, dym)]))]
    dv = [mm(k, d) + e for k, d, e in zip(kc, dm1, by_chunk([mm_tn(p, x) for p, x in zip(prk, dym)]))]
    dz = by_chunk([mm_tn(t, du[c] * m) for t, (c, m) in zip(tinv, items)])
    dzm = [dz[c] * m for c, m in items]
    dv = [a + b for a, b in zip(dv, by_chunk([mm_tn(a_, x) for a_, x in zip(aak, dzm)]))]
    drt = [mm_nt(d, m) for d, m in zip(dy, m0)]
    dat = [mm_nt(d, m) for d, m in zip(dz, m0)]
    udm = [mm_nt(x, d) for x, d in zip(u, dm1)]
    vdm = [mm_nt(x, d) for x, d in zip(v, dm1)]
    daab = [jnp.where(strict, mm_nt(x, u[c]), 0.0) for x, (c, _) in zip(dzm, items)]
    daak = [jnp.where(strict, mm_nt(x, v[c]), 0.0) for x, (c, _) in zip(dzm, items)]
    dprb = [jnp.where(incl, mm_nt(x, u[c]), 0.0) for x, (c, _) in zip(dym, items)]
    dprk = [jnp.where(incl, mm_nt(x, v[c]), 0.0) for x, (c, _) in zip(dym, items)]
    drt2 = by_chunk([(mm(p, bt[c]) + mm(q, kt[c])) * m for p, q, (c, m) in zip(dprb, dprk, items)])
    dat2 = by_chunk([(mm(p, bt[c]) + mm(q, kt[c])) * m for p, q, (c, m) in zip(daab, daak, items)])
    dbt2 = by_chunk([mm_tn(p, r) + mm_tn(a_, x) for p, r, a_, x in zip(dprb, rtm, daab, atm)])
    dkt2 = by_chunk([mm_tn(p, r) + mm_tn(a_, x) for p, r, a_, x in zip(dprk, rtm, daak, atm)])
    ones = jnp.ones((8, PAIR), F32)
    upper = (lane_iota((CHUNK, CHUNK)) >= row_iota((CHUNK, CHUNK))).astype(F32)
    out = []
    for c in range(n):
        drt_c = drt[c] + drt2[c]
        dat_c = dat[c] + dat2[c]
        dbt_c = udm[c] * cl[c] + dbt2[c]
        dkt_c = vdm[c] * cl[c] + dkt2[c]
        dlcl = hdot_nt(ones, dm1[c] * m0[c])[0:1, :] * cl[c] + colsum(bc[c] * udm[c] + kc[c] * vdm[c])
        g = drt_c * rt[c] - dbt_c * bt[c] - dkt_c * kt[c] + dat_c * at[c]
        dlw = hdot(upper, g) - dat_c * at[c] + dlcl
        out.append((dat_c, dbt_c, dkt_c, drt_c, dv[c], dlw))
    return out


def wkv_forward(at, bt, kt, rt, v, clf):
    n_rows = at.shape[0]
    cps = WKV_CHUNKS_PER_STEP
    rb = cps * CHUNK
    n_steps = n_rows // rb

    def body(a_ref, b_ref, k_ref, r_ref, v_ref, c_ref, y_ref, m0_ref, g_ref, rh_ref, m_scr):
        @pl.when(pl.program_id(1) == 0)
        def _():
            m_scr[...] = jnp.zeros_like(m_scr)

        masks = wkv_masks()
        chunks = []
        for cc in range(cps):
            sl = slice(cc * CHUNK, (cc + 1) * CHUNK)
            chunks.append((a_ref[sl, :], b_ref[sl, :], k_ref[sl, :], r_ref[sl, :], v_ref[sl, :],
                           c_ref[cc * CHUNK:cc * CHUNK + 1, :]))
        pre = wkv_chunks_pre(chunks, masks)[:4]
        m = m_scr[...]
        for cc, (g, h, rh, yh) in enumerate(zip(*pre)):
            sl = slice(cc * CHUNK, (cc + 1) * CHUNK)
            m0_ref[0, cc] = m
            g_ref[0, cc] = g
            rh_ref[sl, :] = rh
            y_ref[sl, :] = hdot(rh, m) + yh
            m = hdot(g, m) + h
        m_scr[...] = m

    blk = pl.BlockSpec((rb, PAIR), lambda p, s: (s, p))
    state_blk = pl.BlockSpec((1, cps, PAIR, PAIR), lambda p, s: (p, s, 0, 0))
    state_shape = jax.ShapeDtypeStruct((WIDTH // PAIR, n_rows // CHUNK, PAIR, PAIR), F32)
    return pl.pallas_call(
        body, name="wkv_forward", grid=(WIDTH // PAIR, n_steps),
        in_specs=[blk] * 6,
        out_specs=[blk, state_blk, state_blk, blk],
        out_shape=[jax.ShapeDtypeStruct((n_rows, WIDTH), F32), state_shape, state_shape,
                   jax.ShapeDtypeStruct((n_rows, WIDTH), F32)],
        scratch_shapes=[pltpu.VMEM((PAIR, PAIR), F32)],
        compiler_params=pltpu.CompilerParams(dimension_semantics=("arbitrary", "arbitrary"),
                                             vmem_limit_bytes=VMEM_LIMIT),
    )(at, bt, kt, rt, v, clf)


def wkv_backward(at, bt, kt, rt, v, clf, m0s, gs, rh, dy):
    n_rows = at.shape[0]
    cps = WKV_CHUNKS_PER_STEP
    rb = cps * CHUNK
    n_steps = n_rows // rb

    def body(a_ref, b_ref, k_ref, r_ref, v_ref, c_ref, m0_ref, g_ref, rh_ref, dy_ref,
             da_ref, db_ref, dk_ref, dr_ref, dv_ref, dlw_ref, dm_scr):
        @pl.when(pl.program_id(1) == 0)
        def _():
            dm_scr[...] = jnp.zeros_like(dm_scr)

        masks = wkv_masks()
        bd = masks[4]
        dm = dm_scr[...]
        dm1 = [None] * cps
        for cc in reversed(range(cps)):
            sl = slice(cc * CHUNK, (cc + 1) * CHUNK)
            dm1[cc] = dm
            dm = bd * (hdot_tn(g_ref[0, cc], dm) + hdot_tn(rh_ref[sl, :], dy_ref[sl, :]))
        dm_scr[...] = dm
        chunks, m0, dys = [], [], []
        for cc in range(cps):
            sl = slice(cc * CHUNK, (cc + 1) * CHUNK)
            chunks.append((a_ref[sl, :], b_ref[sl, :], k_ref[sl, :], r_ref[sl, :], v_ref[sl, :],
                           c_ref[cc * CHUNK:cc * CHUNK + 1, :]))
            m0.append(m0_ref[0, cc])
            dys.append(dy_ref[sl, :])
        grads = wkv_chunks_grad(chunks, m0, dys, dm1, masks)
        for cc, (dat, dbt, dkt, drt, dv, dlw) in enumerate(grads):
            sl = slice(cc * CHUNK, (cc + 1) * CHUNK)
            da_ref[sl, :] = dat
            db_ref[sl, :] = dbt
            dk_ref[sl, :] = dkt
            dr_ref[sl, :] = drt
            dv_ref[sl, :] = dv
            dlw_ref[sl, :] = dlw

    blk = pl.BlockSpec((rb, PAIR), lambda p, s: (n_steps - 1 - s, p))
    state_blk = pl.BlockSpec((1, cps, PAIR, PAIR), lambda p, s: (p, n_steps - 1 - s, 0, 0))
    return pl.pallas_call(
        body, name="wkv_backward", grid=(WIDTH // PAIR, n_steps),
        in_specs=[blk] * 6 + [state_blk, state_blk, blk, blk],
        out_specs=[blk] * 6,
        out_shape=[jax.ShapeDtypeStruct((n_rows, WIDTH), F32)] * 6,
        scratch_shapes=[pltpu.VMEM((PAIR, PAIR), F32)],
        compiler_params=pltpu.CompilerParams(dimension_semantics=("arbitrary", "arbitrary"),
                                             vmem_limit_bytes=VMEM_LIMIT),
    )(at, bt, kt, rt, v, clf, m0s, gs, rh, dy)


def visible(q_row0, k_row0, shape):
    qc = (q_row0 + row_iota(shape)) // CHUNK
    kc = (k_row0 + lane_iota(shape)) // CHUNK
    return kc <= qc


def attention_forward(q, k, v):
    n_rows = q.shape[0]
    tq = tk = ATTN_TILE
    n_q = n_rows // tq

    def body(q_ref, k_ref, v_ref, o_ref, lse_ref):
        i = pl.program_id(1)
        lane = lane_iota((tq, LANE))
        heads = [slice(0, LANE), slice(LANE, 2 * LANE)]
        qs = [q_ref[:, cols] for cols in heads]

        def step(j, carry, masked):
            rows = pl.ds(pl.multiple_of(j * tk, tk), tk)
            ss = [mm_nt(qh, k_ref[rows, cols]) * ATTN_SCALE for qh, cols in zip(qs, heads)]
            if masked:
                vis = visible(i * tq, j * tk, ss[0].shape)
                ss = [jnp.where(vis, s, -jnp.inf) for s in ss]
            ps, stats = [], []
            for s, (m, l, _) in zip(ss, carry):
                m_new = jnp.maximum(m, jnp.max(s, axis=-1, keepdims=True))
                p = jnp.exp(s - m_new)
                alpha = jnp.exp(m - m_new)
                ps.append(p)
                stats.append((m_new, alpha, alpha * l + jnp.sum(p, axis=-1, keepdims=True)))
            pvs = [mm(p, v_ref[rows, cols]) for p, cols in zip(ps, heads)]
            return tuple((m_new, l, alpha * acc + pv)
                         for (m_new, alpha, l), (_, _, acc), pv in zip(stats, carry, pvs))

        init = tuple((jnp.full((tq, 1), -jnp.inf, F32), jnp.zeros((tq, 1), F32), jnp.zeros((tq, LANE), F32))
                     for _ in heads)
        carry = lax.fori_loop(0, i, functools.partial(step, masked=False), init)
        (m0, l0, acc0), (m1, l1, acc1) = step(i, carry, masked=True)
        o_ref[...] = acc0 / l0 + acc1 / l1
        lse_ref[...] = jnp.where(lane >= HEAD, m1 + jnp.log(l1), m0 + jnp.log(l0))

    return pl.pallas_call(
        body, name="attention_forward", grid=(HEADS // 2, n_q),
        in_specs=[pl.BlockSpec((tq, 2 * LANE), lambda p, i: (i, p)),
                  pl.BlockSpec((n_rows, 2 * LANE), lambda p, i: (0, p)),
                  pl.BlockSpec((n_rows, 2 * LANE), lambda p, i: (0, p))],
        out_specs=[pl.BlockSpec((tq, LANE), lambda p, i: (i, p))] * 2,
        out_shape=[jax.ShapeDtypeStruct((n_rows, WIDTH), F32)] * 2,
        compiler_params=pltpu.CompilerParams(dimension_semantics=("arbitrary", "arbitrary"),
                                             vmem_limit_bytes=VMEM_LIMIT),
    )(q, k, v)


def attention_backward(q, k, v, o, do, lse):
    n_rows = q.shape[0]
    tq = tk = ATTN_TILE
    n_q = n_rows // tq

    def body(q_ref, k_ref, v_ref, o_ref, do_ref, lse_ref, dq_ref, dk_ref, dv_ref):
        j = pl.program_id(1)

        @pl.when(j == 0)
        def _():
            dq_ref[...] = jnp.zeros_like(dq_ref)

        lane = lane_iota((tq, LANE))
        heads = [slice(0, LANE), slice(LANE, 2 * LANE)]
        ks = [k_ref[:, cols] for cols in heads]
        vs = [v_ref[:, cols] for cols in heads]
        head_lanes = [(lane < HEAD).astype(F32), (lane >= HEAD).astype(F32)]

        def step(i, carry, masked):
            rows = pl.ds(pl.multiple_of(i * tq, tq), tq)
            qs = [q_ref[rows, cols] for cols in heads]
            dout = do_ref[rows, :]
            dout_o = dout * o_ref[rows, :]
            lse_t = lse_ref[rows, :]
            ss = [mm_nt(qh, kh) * ATTN_SCALE for qh, kh in zip(qs, ks)]
            dps = [mm_nt(dout, vh) for vh in vs]
            ps, dss = [], []
            for hh in range(2):
                delta = jnp.sum(dout_o * head_lanes[hh], axis=-1, keepdims=True)
                lse_h = jnp.sum(jnp.where(lane == hh * HEAD, lse_t, 0.0), axis=-1, keepdims=True)
                p = jnp.exp(ss[hh] - lse_h)
                if masked:
                    p = jnp.where(visible(i * tq, j * tk, p.shape), p, 0.0)
                ps.append(p)
                dss.append(p * (dps[hh] - delta) * ATTN_SCALE)
            dvs = [mm_tn(p, dout) for p in ps]
            dqs = [mm(ds, kh) for ds, kh in zip(dss, ks)]
            dks = [mm_tn(ds, qh) for ds, qh in zip(dss, qs)]
            for cols, dq in zip(heads, dqs):
                dq_ref[rows, cols] += dq
            return tuple((dk + a, dv + b) for (dk, dv), a, b in zip(carry, dks, dvs))

        init = tuple((jnp.zeros((tk, LANE), F32), jnp.zeros((tk, LANE), F32)) for _ in heads)
        carry = step(j, init, masked=True)
        carry = lax.fori_loop(j + 1, n_q, functools.partial(step, masked=False), carry)
        for cols, (dk, dv) in zip(heads, carry):
            dk_ref[:, cols] = dk
            dv_ref[:, cols] = dv

    full = lambda w: pl.BlockSpec((n_rows, w), lambda p, j: (0, p))
    blk = pl.BlockSpec((tk, 2 * LANE), lambda p, j: (j, p))
    return pl.pallas_call(
        body, name="attention_backward", grid=(HEADS // 2, n_q),
        in_specs=[full(2 * LANE), blk, blk, full(LANE), full(LANE), full(LANE)],
        out_specs=[full(2 * LANE), blk, blk],
        out_shape=[jax.ShapeDtypeStruct((n_rows, HEADS * LANE), F32)] * 3,
        compiler_params=pltpu.CompilerParams(dimension_semantics=("arbitrary", "arbitrary"),
                                             vmem_limit_bytes=VMEM_LIMIT),
    )(q, k, v, o, do, lse)


def tail_tile(step0, tile0, x, tgt, ma, mb, gpa, gpb, ya, y, ur, k2, uv,
              mod, wpa, wpb, wout, gn_g, gn_b, r_k, post_g, post_b, bd):
    gate = mod[2:3]
    inv = 1.0 / HEAD
    yc = y - head_sum(y, bd) * inv
    rs = lax.rsqrt(head_sum(yc * yc, bd) * inv + GN_EPS)
    yn = yc * rs
    yb = yn * gn_g + gn_b + head_sum(ur * k2 * r_k, bd) * uv
    sga, sgb = sigmoid(gpa), sigmoid(gpb)
    sila, silb = gpa * sga, gpb * sgb
    ga, gb = ya * sila, yb * silb
    pa, pb = mm(ga, wpa), mm(gb, wpb)
    sa, sb = sigmoid(ma), sigmoid(mb)
    merged = sa * pa + sb * pb
    sub = mm(merged, wout)
    z = ALPHA * x + (1.0 + gate) * sub
    zhat, rstd = layer_norm_stats(z)
    err = zhat * post_g + post_b - tgt
    loss = 0.5 * jnp.sum(rowmean(err * err), axis=0, keepdims=True) + jnp.zeros((1, LANE), F32)
    dout = err * (1.0 / D_MODEL)
    dpost_g = colsum(dout * zhat)
    dpost_b = colsum(dout)
    dz = layer_norm_bwd(dout * post_g, zhat, rstd)
    dgate = colsum(dz * sub)
    dsub = dz * (1.0 + gate)
    dwout = mm_tn(merged, dsub)
    dmerged = mm_nt(dsub, wout)
    dpa, dpb = dmerged * sa, dmerged * sb
    dma = dmerged * pa * sa * (1.0 - sa)
    dmb = dmerged * pb * sb * (1.0 - sb)
    dwpa = mm_tn(ga, dpa)
    dwpb = mm_tn(gb, dpb)
    dga = mm_nt(dpa, wpa)
    dgb = mm_nt(dpb, wpb)
    dya = dga * sila
    dgpa = dga * ya * (sga * (1.0 + gpa * (1.0 - sga)))
    dyb = dgb * silb
    dgpb = dgb * yb * (sgb * (1.0 + gpb * (1.0 - sgb)))
    dgn_g = colsum(dyb * yn)
    dgn_b = colsum(dyb)
    dyn = dyb * gn_g
    dy = rs * (dyn - head_sum(dyn, bd) * inv - yn * head_sum(dyn * yn, bd) * inv)
    return (dz, dma, dmb, dgpa, dgpb, dya, dy, dyb,
            loss, dwout, dwpa, dwpb, dgn_g, dgn_b, dpost_g, dpost_b, dgate)


def mla_prep_bwd_tile(step0, tile0, q_c, kv_c, cos, sin, dq, dk, dv, gq, gkv, wq, wqr, wkn, wv):
    qn, qh, rq = rms_norm_fwd(q_c, gq)
    kvn, kvh, rkv = rms_norm_fwd(kv_c, gkv)
    dqc = dq * tile_lanes(cos, HEADS)
    dqs = dq * tile_lanes(sin, HEADS)
    dqn = mm_nt(dqc, wq) + mm_nt(dqs, wqr)
    dkvn = mm_nt(dk, wkn) + mm_nt(dv, wv)
    dkpe = dk[:, 0:LANE]
    for h in range(1, HEADS):
        dkpe = dkpe + dk[:, h * LANE:(h + 1) * LANE]
    dkr = dkpe * (cos * key_rope_mask(cos.shape))
    dkrr = dkpe * sin

    def rms_bwd(dyv, xh, r, g):
        dyg = dyv * g
        return r * (dyg - xh * rowmean(dyg * xh)), colsum(dyv * xh)

    dq_c, dgq = rms_bwd(dqn, qh, rq, gq)
    dkv_c, dgkv = rms_bwd(dkvn, kvh, rkv, gkv)
    return (dq_c, dkv_c, dkr, dkrr,
            mm_tn(qn, dqc), mm_tn(qn, dqs), mm_tn(kvn, dk), mm_tn(kvn, dv), dgq, dgkv)


def rwkv_prep_bwd_tile(step0, tile0, r0, k0, v0, l0, drt, dat, dbt, dkt, dvv, dlw, dyb, hr, hk, hv, hl,
                       mu_r, mu_k, mu_v, mu_l, w0, a0, k_k, k_a, w_dec, w_iclr, tril, same, bd, r_k,
                       cr, ck, cv, cl_):
    f = rwkv_prep_core(tile0, r0, k0, v0, l0, hr, hk, hv, hl, mu_r, mu_k, mu_v, mu_l, w0, a0, k_k, k_a,
                       w_dec, w_iclr, tril, same, bd)
    ur, uk, uv, ul, kk, k2, a_ic, sg, th = (f[n] for n in ("ur", "uk", "uv", "ul", "kk", "k2", "a_ic", "sg", "th"))
    lc, lw = f["lc"], f["lw"]
    e_neg = jnp.exp(-lc)
    dur = drt * jnp.exp(lc)
    da = dat * jnp.exp(lc - lw)
    db = dbt * e_neg
    dk2 = dkt * e_neg
    s = head_sum(ur * k2 * r_k, bd)
    duv = dvv + dyb * s
    ds = head_sum(dyb * uv, bd)
    dur = dur + ds * k2 * r_k
    dk2 = dk2 + ds * ur * r_k
    dr_k = colsum(ds * ur * k2)
    dkk = db * a_ic - da
    da_ic = db * kk + dk2 * uk * k_a
    duk = dk2 * (1.0 + (a_ic - 1.0) * k_a)
    dk_a = colsum(dk2 * uk * (a_ic - 1.0))
    dkkraw = jnp.where(f["nrm_raw"] > 1e-12, (dkk - kk * head_sum(dkk * kk, bd)) / f["nrm"], dkk * 1e12)
    duk = duk + dkkraw * k_k
    dk_k = colsum(dkkraw * uk)
    dai = da_ic * a_ic * (1.0 - a_ic)
    dd = dlw * (-DECAY_SCALE) * sg * (1.0 - sg)
    dul = mm_nt(dai, w_iclr) + mm_nt(dd, w_dec) * (1.0 - th * th)

    def unshift(du, x, prev, mu, carry_row):
        nxt = shift_rows_up(du, carry_row)
        return du * (1.0 - mu) + nxt * mu, colsum(du * (prev - x)), du[0:1, :]

    dr0, dmu_r, ncr = unshift(dur, r0, f["pr"], mu_r, cr)
    dk0, dmu_k, nck = unshift(duk, k0, f["pk"], mu_k, ck)
    dv0, dmu_v, ncv = unshift(duv, v0, f["pv"], mu_v, cv)
    dl0, dmu_l, ncl = unshift(dul, l0, f["pl"], mu_l, cl_)
    return (dr0, dk0, dv0, dl0,
            dmu_r, dmu_k, dmu_v, dmu_l, colsum(dd), colsum(dai), dk_k, dk_a, dr_k, mm_tn(th, dd), mm_tn(ul, dai),
            ncr, nck, ncv, ncl)


def in_backward(x, dz, pieces, mod, w_in_p, unrot):
    n_rows = x.shape[0]
    ts = ROW_TILE
    n_p = len(pieces)
    shard_cols = IN_WIDTH // N_DEV

    def body(*refs):
        x_ref, dz_ref = refs[:2]
        p_refs = refs[2:2 + n_p]
        mod_ref, w_ref, unrot_ref = refs[2 + n_p:5 + n_p]
        dx_ref, ht_ref, blocks_ref, dshift_ref, dscale_ref = refs[5 + n_p:]
        step0 = pl.program_id(0) == 0
        dma, dmb, dr0, dk0, dv0, dgpa, dgpb, dq_c, dkv_c, dkr, dkrr, dl0 = (r[...] for r in p_refs)
        dproj = jnp.concatenate([dma, dmb, dr0, dk0, dv0, dgpa, dgpb, dq_c, dkv_c, dkr, dkrr, dl0], axis=1)
        dh = mm_nt(dproj, w_ref[...])
        xhat, rstd = layer_norm_stats(x_ref[...])
        scale1 = 1.0 + mod_ref[1:2, :]
        dx_ref[...] = layer_norm_bwd(dh * scale1, xhat, rstd) + ALPHA * dz_ref[...]
        ht_ref[...] = jnp.transpose(xhat * scale1 + mod_ref[0:1, :]).astype(BF16)
        dkrope = (dkr + mm(dkrr, unrot_ref[...]))[:, NOPE:QK_DIM]
        natural = jnp.concatenate([dq_c, dkv_c, dkrope, dgpa, dr0, dk0, dv0, dl0, dgpb, dma, dmb], axis=1)
        for j in range(N_DEV):
            blocks_ref[j] = natural[:, j * shard_cols:(j + 1) * shard_cols].astype(BF16)
        for ref, val in ((dshift_ref, colsum(dh)), (dscale_ref, colsum(dh * xhat))):
            @pl.when(step0)
            def _(ref=ref, val=val):
                ref[...] = val

            @pl.when(jnp.logical_not(step0))
            def _(ref=ref, val=val):
                ref[...] += val

    row = lambda w: pl.BlockSpec((ts, w), lambda i: (i, 0))
    const = pl.BlockSpec(memory_space=pltpu.VMEM)
    vec = pl.BlockSpec((1, D_MODEL), lambda i: (0, 0))
    return pl.pallas_call(
        body, name="in_backward", grid=(n_rows // ts,),
        in_specs=[row(D_MODEL), row(D_MODEL)] + [row(p.shape[1]) for p in pieces] + [const] * 3,
        out_specs=[row(D_MODEL), pl.BlockSpec((D_MODEL, ts), lambda i: (0, i)),
                   pl.BlockSpec((N_DEV, ts, shard_cols), lambda i: (0, i, 0)), vec, vec],
        out_shape=[jax.ShapeDtypeStruct((n_rows, D_MODEL), F32), jax.ShapeDtypeStruct((D_MODEL, n_rows), BF16),
                   jax.ShapeDtypeStruct((N_DEV, n_rows, shard_cols), BF16),
                   jax.ShapeDtypeStruct((1, D_MODEL), F32), jax.ShapeDtypeStruct((1, D_MODEL), F32)],
        compiler_params=pltpu.CompilerParams(dimension_semantics=("arbitrary",), vmem_limit_bytes=VMEM_LIMIT),
    )(x, dz, *pieces, mod, w_in_p, unrot)


def in_weight_grad_exchange(h_t, dp_blocks, others, small, order):
    n = len(others)
    n_rows = h_t.shape[1]
    ts = 2 * ROW_TILE
    n_i = n_rows // ts
    shard_cols = dp_blocks.shape[2]
    n_chips = N_DEV // 2
    last = N_DEV - 1

    def body(order_ref, h_ref, dp_ref, *rest):
        g_refs, s_ref = rest[:n], rest[n]
        rwin_ref, rg_refs, rs_ref = rest[n + 1], rest[n + 2:2 * n + 2], rest[2 * n + 2]
        (acc, sendbuf, sib_buf, sib_send, sib_recv, win_send, win_recv,
         o_send, o_recv, local_sems) = rest[2 * n + 3:]
        b, i = pl.program_id(0), pl.program_id(1)
        me = my_position()
        mi = flat_index(me)
        sibling = (me[0], me[1], 1 - me[2])

        def other_copies(k, src_index, dst_index):
            peer = flip(me, k)
            out = [pltpu.make_async_remote_copy(
                src_ref=g_refs[a].at[src_index], dst_ref=rg_refs[a].at[dst_index],
                send_sem=o_send.at[(n + 1) * (k - 1) + a], recv_sem=o_recv.at[(n + 1) * (k - 1) + a],
                device_id=peer, device_id_type=MESH_IDS) for a in range(n)]
            out.append(pltpu.make_async_remote_copy(
                src_ref=s_ref, dst_ref=rs_ref.at[dst_index],
                send_sem=o_send.at[(n + 1) * (k - 1) + n], recv_sem=o_recv.at[(n + 1) * (k - 1) + n],
                device_id=peer, device_id_type=MESH_IDS))
            return out

        def local_copies():
            out = [pltpu.make_async_copy(g_refs[a].at[mi], rg_refs[a].at[mi], local_sems.at[a]) for a in range(n)]
            out.append(pltpu.make_async_copy(s_ref, rs_ref.at[mi], local_sems.at[n]))
            return out

        def to_sibling(t):
            return pltpu.make_async_remote_copy(
                src_ref=sendbuf.at[t], dst_ref=sib_buf.at[t], send_sem=sib_send.at[t], recv_sem=sib_recv.at[t],
                device_id=sibling, device_id_type=MESH_IDS)

        def to_owner(t):
            flip_x = (t < 2) * 1
            flip_y = 1 - (t & 1)
            owner = (me[0] ^ flip_x, me[1] ^ flip_y, me[2])
            return pltpu.make_async_remote_copy(
                src_ref=sendbuf.at[n_chips + t], dst_ref=rwin_ref.at[t], send_sem=win_send.at[t],
                recv_sem=win_recv.at[t], device_id=owner, device_id_type=MESH_IDS)

        own_block = pltpu.make_async_copy(sendbuf.at[last], rwin_ref.at[n_chips - 1], local_sems.at[n + 1])

        @pl.when(jnp.logical_and(b == 0, i == 0))
        def _():
            for cp in local_copies():
                cp.start()
            for k in range(1, N_DEV):
                for cp in other_copies(k, flat_index(flip(me, k)), mi):
                    cp.start()

        contrib = jnp.dot(h_ref[...], dp_ref[...], preferred_element_type=F32)

        @pl.when(i == 0)
        def _():
            acc[...] = contrib

        @pl.when(i > 0)
        def _():
            acc[...] += contrib

        @pl.when(jnp.logical_and(i == n_i - 1, b < n_chips))
        def _():
            sendbuf[b] = acc[...].astype(BF16)
            to_sibling(b).start()

        @pl.when(jnp.logical_and(i == n_i - 1, b >= n_chips))
        def _():
            t = b - n_chips
            to_sibling(t).wait_recv()
            sendbuf[b] = (acc[...] + sib_buf[t].astype(F32)).astype(BF16)

            @pl.when(b < last)
            def _():
                to_owner(t).start()

            @pl.when(b == last)
            def _():
                own_block.start()

        @pl.when(jnp.logical_and(b == last, i == n_i - 1))
        def _():
            for t in range(n_chips - 1):
                to_owner(t).wait_recv()
            for k in range(1, N_DEV):
                pi = flat_index(flip(me, k))
                for cp in other_copies(k, pi, pi):
                    cp.wait_recv()
            for t in range(n_chips):
                to_sibling(t).wait_send()
            for t in range(n_chips - 1):
                to_owner(t).wait_send()
            for k in range(1, N_DEV):
                for cp in other_copies(k, flat_index(flip(me, k)), mi):
                    cp.wait_send()
            for cp in local_copies():
                cp.wait()
            own_block.wait()

    hbm = pl.BlockSpec(memory_space=pl.ANY)
    n_sem = 7 * (n + 1)
    grid_spec = pltpu.PrefetchScalarGridSpec(
        num_scalar_prefetch=1, grid=(N_DEV, n_i),
        in_specs=[pl.BlockSpec((D_MODEL, ts), lambda b, i, order: (0, i)),
                  pl.BlockSpec((None, ts, shard_cols), lambda b, i, order: (order[b], i, 0))] + [hbm] * (n + 1),
        out_specs=[hbm] * (n + 2),
        scratch_shapes=[pltpu.VMEM((D_MODEL, shard_cols), F32), pltpu.VMEM((N_DEV, D_MODEL, shard_cols), BF16),
                        pltpu.VMEM((n_chips, D_MODEL, shard_cols), BF16),
                        pltpu.SemaphoreType.DMA((n_chips,)), pltpu.SemaphoreType.DMA((n_chips,)),
                        pltpu.SemaphoreType.DMA((n_chips - 1,)), pltpu.SemaphoreType.DMA((n_chips - 1,)),
                        pltpu.SemaphoreType.DMA((n_sem,)), pltpu.SemaphoreType.DMA((n_sem,)),
                        pltpu.SemaphoreType.DMA((n + 2,))])
    return pl.pallas_call(
        body, name="in_weight_grad_exchange", grid_spec=grid_spec,
        out_shape=[jax.ShapeDtypeStruct((n_chips, D_MODEL, shard_cols), BF16)]
        + [jax.ShapeDtypeStruct(o.shape, o.dtype) for o in others]
        + [jax.ShapeDtypeStruct((N_DEV,) + small.shape, small.dtype)],
        compiler_params=pltpu.CompilerParams(dimension_semantics=("arbitrary", "arbitrary"),
                                             vmem_limit_bytes=VMEM_LIMIT),
    )(order, h_t, dp_blocks, *others, small)


def ada_weight_grad(c_all, dmod_cols):
    def body(c_ref, d_ref, o_ref):
        cv = c_ref[...]
        o_ref[...] = hdot_tn(cv * sigmoid(cv), d_ref[...])

    return pl.pallas_call(
        body, name="ada_weight_grad",
        out_shape=jax.ShapeDtypeStruct((c_all.shape[1], dmod_cols.shape[1]), F32),
    )(c_all, dmod_cols)


def adamw(parts, w, m, v, name):
    k, rows, cols = parts.shape
    rb = 128 if rows % 128 == 0 else rows

    def body(p_ref, w_ref, m_ref, v_ref, g_ref, d_ref, nm_ref, nv_ref):
        g = p_ref[0].astype(F32)
        for i in range(1, k):
            g = g + p_ref[i].astype(F32)
        nm = ADAM_B1 * m_ref[...] + (1.0 - ADAM_B1) * g
        nv = ADAM_B2 * v_ref[...] + (1.0 - ADAM_B2) * (g * g)
        m_hat = nm / (1.0 - ADAM_B1 ** ADAM_STEP)
        v_hat = nv / (1.0 - ADAM_B2 ** ADAM_STEP)
        g_ref[...] = g
        d_ref[...] = -ADAM_LR * (m_hat / (jnp.sqrt(v_hat) + ADAM_EPS) + ADAM_WD * w_ref[...])
        nm_ref[...] = nm
        nv_ref[...] = nv

    blk = pl.BlockSpec((rb, cols), lambda i: (i, 0))
    return pl.pallas_call(
        body, name=name, grid=(rows // rb,),
        in_specs=[pl.BlockSpec((k, rb, cols), lambda i: (0, i, 0)), blk, blk, blk],
        out_specs=[blk] * 4, out_shape=[jax.ShapeDtypeStruct((rows, cols), F32)] * 4,
        compiler_params=pltpu.CompilerParams(dimension_semantics=("arbitrary",), vmem_limit_bytes=VMEM_LIMIT),
    )(parts, w, m, v)


def rot_cols(w):
    return jnp.concatenate([-w[:, ROPE // 2:], w[:, :ROPE // 2]], axis=1)


def unrot_cols(dw):
    return jnp.concatenate([dw[:, ROPE // 2:], -dw[:, :ROPE // 2]], axis=1)


def columns_from_shards(g, rows, cols):
    return g.reshape(N_DEV, rows, cols).transpose(1, 0, 2).reshape(rows, N_DEV * cols)


def shards_from_columns(w, rows, cols):
    return w.reshape(rows, N_DEV, cols).transpose(1, 0, 2).reshape(N_DEV, rows * cols)


def permute_w_in(w):
    z = lambda n: jnp.zeros((D_MODEL, n), w.dtype)
    krope = w[:, N_KROPE:N_KROPE + ROPE]
    rw = N_RWKV
    return jnp.concatenate([
        w[:, N_MA:N_MA + 1024], w[:, N_MB:N_MB + 1024],
        w[:, rw:rw + 512], w[:, rw + 512:rw + 1024], w[:, rw + 1024:rw + 1536],
        w[:, N_GPA:N_GPA + 512], w[:, N_GPB:N_GPB + 512],
        w[:, N_QC:N_QC + 256], w[:, N_KVC:N_KVC + 128],
        z(NOPE), krope, z(LANE - QK_DIM), z(NOPE), rot_cols(krope), z(LANE - QK_DIM),
        w[:, rw + 1536:rw + 1664]], axis=1)


def unpermute_w_in_grad(d):
    rw = P_R
    krope = d[:, P_KR + NOPE:P_KR + QK_DIM] + unrot_cols(d[:, P_KRR + NOPE:P_KRR + QK_DIM])
    return jnp.concatenate([
        d[:, P_QC:P_QC + 256], d[:, P_KVC:P_KVC + 128], krope, d[:, P_GPA:P_GPA + 512],
        d[:, rw:rw + 1536], d[:, P_LORA:P_LORA + 128], d[:, P_GPB:P_GPB + 512],
        d[:, P_MA:P_MA + 1024], d[:, P_MB:P_MB + 1024]], axis=1)


def pad_heads_q(w_uq):
    w = w_uq.reshape(Q_RANK, HEADS, QK_DIM)
    zpad = jnp.zeros((Q_RANK, HEADS, LANE - QK_DIM), w.dtype)
    wq = jnp.concatenate([w, zpad], axis=2).reshape(Q_RANK, HEADS * LANE)
    pe = w[:, :, NOPE:]
    rot = jnp.concatenate([-pe[:, :, ROPE // 2:], pe[:, :, :ROPE // 2]], axis=2)
    wqr = jnp.concatenate([jnp.zeros((Q_RANK, HEADS, NOPE), w.dtype), rot, zpad], axis=2).reshape(Q_RANK, HEADS * LANE)
    return wq, wqr


def unpad_heads_q_grad(dwq, dwqr):
    a = dwq.reshape(Q_RANK, HEADS, LANE)
    r = dwqr.reshape(Q_RANK, HEADS, LANE)[:, :, NOPE:QK_DIM]
    pe = a[:, :, NOPE:QK_DIM] + jnp.concatenate([r[:, :, ROPE // 2:], -r[:, :, :ROPE // 2]], axis=2)
    return jnp.concatenate([a[:, :, :NOPE], pe], axis=2).reshape(Q_RANK, HEADS * QK_DIM)


def pad_heads_kv(w_ukv):
    w = w_ukv.reshape(KV_RANK, HEADS, 2 * HEAD)
    z = jnp.zeros((KV_RANK, HEADS, HEAD), w.dtype)
    wkn = jnp.concatenate([w[:, :, :NOPE], z], axis=2).reshape(KV_RANK, HEADS * LANE)
    val = w[:, :, NOPE:]
    odd = (jnp.arange(HEADS) % 2 == 1)[None, :, None]
    wv = jnp.concatenate([jnp.where(odd, 0, val), jnp.where(odd, val, 0)], axis=2).reshape(KV_RANK, HEADS * LANE)
    return wkn, wv


def unpad_heads_kv_grad(dwkn, dwv):
    a = dwkn.reshape(KV_RANK, HEADS, LANE)[:, :, :NOPE]
    b = dwv.reshape(KV_RANK, HEADS, LANE)
    odd = (jnp.arange(HEADS) % 2 == 1)[None, :, None]
    val = jnp.where(odd, b[:, :, HEAD:], b[:, :, :HEAD])
    return jnp.concatenate([a, val], axis=2).reshape(KV_RANK, HEADS * 2 * HEAD)


def kernel(x, c, positions, w_ada, b_ada, w_in, q_norm_g, w_uq, kv_norm_g, w_ukv, mu_rwkv, w0, w_decay_up, a0, w_iclr_up, k_k, k_a, r_k, gn_g, gn_b, w_proj_a, w_proj_b, w_out, post_g, post_b, loss_target, m_w_ada, m_b_ada, m_w_in, m_q_norm_g, m_w_uq, m_kv_norm_g, m_w_ukv, m_mu_rwkv, m_w0, m_w_decay_up, m_a0, m_w_iclr_up, m_k_k, m_k_a, m_r_k, m_gn_g, m_gn_b, m_w_proj_a, m_w_proj_b, m_w_out, m_post_g, m_post_b, v_w_ada, v_b_ada, v_w_in, v_q_norm_g, v_w_uq, v_kv_norm_g, v_w_ukv, v_mu_rwkv, v_w0, v_w_decay_up, v_a0, v_w_iclr_up, v_k_k, v_k_a, v_r_k, v_gn_g, v_gn_b, v_w_proj_a, v_w_proj_b, v_w_out, v_post_g, v_post_b):
    weights = dict(w_ada=w_ada, b_ada=b_ada, w_in=w_in, q_norm_g=q_norm_g, w_uq=w_uq, kv_norm_g=kv_norm_g,
                   w_ukv=w_ukv, mu_rwkv=mu_rwkv, w0=w0, w_decay_up=w_decay_up, a0=a0, w_iclr_up=w_iclr_up,
                   k_k=k_k, k_a=k_a, r_k=r_k, gn_g=gn_g, gn_b=gn_b, w_proj_a=w_proj_a, w_proj_b=w_proj_b,
                   w_out=w_out, post_g=post_g, post_b=post_b)
    mom1 = dict(w_ada=m_w_ada, b_ada=m_b_ada, w_in=m_w_in, q_norm_g=m_q_norm_g, w_uq=m_w_uq, kv_norm_g=m_kv_norm_g,
                w_ukv=m_w_ukv, mu_rwkv=m_mu_rwkv, w0=m_w0, w_decay_up=m_w_decay_up, a0=m_a0, w_iclr_up=m_w_iclr_up,
                k_k=m_k_k, k_a=m_k_a, r_k=m_r_k, gn_g=m_gn_g, gn_b=m_gn_b, w_proj_a=m_w_proj_a, w_proj_b=m_w_proj_b,
                w_out=m_w_out, post_g=m_post_g, post_b=m_post_b)
    mom2 = dict(w_ada=v_w_ada, b_ada=v_b_ada, w_in=v_w_in, q_norm_g=v_q_norm_g, w_uq=v_w_uq, kv_norm_g=v_kv_norm_g,
                w_ukv=v_w_ukv, mu_rwkv=v_mu_rwkv, w0=v_w0, w_decay_up=v_w_decay_up, a0=v_a0, w_iclr_up=v_w_iclr_up,
                k_k=v_k_k, k_a=v_k_a, r_k=v_r_k, gn_g=v_gn_g, gn_b=v_gn_b, w_proj_a=v_w_proj_a, w_proj_b=v_w_proj_b,
                w_out=v_w_out, post_g=v_post_g, post_b=v_post_b)
    names = list(weights)
    n_rows = x.shape[1]
    me = 4 * lax.axis_index("x") + 2 * lax.axis_index("y") + lax.axis_index("c")
    xr = x[0]
    tgt = loss_target[0]
    row = lambda a: a.reshape(1, -1)

    gathered = gather_shards([weights[n][0].astype(BF16) for n, _, _ in SHARDED] + [c])
    c_all = gathered[-1].reshape(N_DEV, D_MODEL)
    full = {}
    for (n, r, cdim), part in zip(SHARDED, gathered):
        full[n] = part.reshape(N_DEV * r, cdim) if n == "w_out" else columns_from_shards(part, r, cdim)
    w_in_p = permute_w_in(full["w_in"])
    wq, wqr = pad_heads_q(full["w_uq"])
    wkn, wv = pad_heads_kv(full["w_ukv"])
    zl = jnp.zeros((LORA, WIDTH), BF16)
    w_dec = jnp.concatenate([full["w_decay_up"], zl], axis=0)
    w_iclr = jnp.concatenate([zl, full["w_iclr_up"]], axis=0)
    wpa, wpb, wout = full["w_proj_a"], full["w_proj_b"], full["w_out"]

    mod_all = ada_modulation(c_all, w_ada[0], b_ada.reshape(N_DEV, -1))
    mod = lax.dynamic_index_in_dim(mod_all, me, axis=1, keepdims=False).reshape(3, D_MODEL)

    (proj,) = row_call("fwd_in", fwd_in_tile, n_rows, [(xr, D_MODEL, 0)], [mod, w_in_p], [(P_WIDTH, F32)])
    pcol = lambda off_, w: (proj, w, off_ // w)

    inv_freq = ROPE_THETA ** (-jnp.arange(0, ROPE, 2, dtype=F32) / ROPE)
    ang = positions[0].astype(F32)[:, None] * inv_freq
    ones_n, zeros_n, zeros_p = jnp.ones((n_rows, NOPE), F32), jnp.zeros((n_rows, NOPE), F32), jnp.zeros((n_rows, LANE - QK_DIM), F32)
    cos_t = jnp.concatenate([ones_n, jnp.cos(ang), jnp.cos(ang), zeros_p], axis=1)
    sin_t = jnp.concatenate([zeros_n, jnp.sin(ang), jnp.sin(ang), zeros_p], axis=1)

    gq, gkv = q_norm_g, kv_norm_g
    mla_consts = [gq, gkv, wq, wqr, wkn, wv]
    q, k, v = row_call(
        "mla_prep", mla_prep_tile, n_rows,
        [pcol(P_QC, 256), pcol(P_KVC, 128), pcol(P_KR, 128), pcol(P_KRR, 128), (cos_t, LANE, 0), (sin_t, LANE, 0)],
        mla_consts, [(HEADS * LANE, BF16)] * 3)
    ya, lse = attention_forward(q, k, v)

    t_idx = jnp.arange(ROW_TILE)
    same_chunk = (t_idx[:, None] // CHUNK) == (t_idx[None, :] // CHUNK)
    same = same_chunk.astype(F32)
    tril = (same_chunk & (t_idx[:, None] >= t_idx[None, :])).astype(F32)
    l_idx = jnp.arange(LANE)
    bd = ((l_idx[:, None] // HEAD) == (l_idx[None, :] // HEAD)).astype(F32)
    mu = mu_rwkv
    mu_r, mu_k, mu_v, mu_l = mu[:, 0:512], mu[:, 512:1024], mu[:, 1024:1536], mu[:, 1536:1664]
    rk_row = row(r_k)
    rwkv_consts = [mu_r, mu_k, mu_v, mu_l, w0, a0, k_k, k_a, w_dec, w_iclr, tril, same, bd]
    rwkv_rows = [pcol(P_R, 512), pcol(P_K, 512), pcol(P_V, 512), pcol(P_LORA, 128)]
    rt, at, bt, kt, clf, uv, ur, k2 = row_call(
        "rwkv_prep", rwkv_prep_tile, n_rows, rwkv_rows, rwkv_consts, [(WIDTH, F32)] * 8, halo_in=rwkv_rows)
    y, m0s, state_maps, out_maps = wkv_forward(at, bt, kt, rt, uv, clf)

    tail = row_call(
        "tail", tail_tile, n_rows,
        [(xr, D_MODEL, 0), (tgt, D_MODEL, 0), pcol(P_MA, 1024), pcol(P_MB, 1024), pcol(P_GPA, 512), pcol(P_GPB, 512),
         (ya, WIDTH, 0), (y, WIDTH, 0), (ur, WIDTH, 0), (k2, WIDTH, 0), (uv, WIDTH, 0)],
        [mod, wpa, wpb, wout, gn_g, gn_b, rk_row, post_g, post_b, bd],
        [(D_MODEL, F32), (1024, F32), (1024, F32), (512, F32), (512, F32), (WIDTH, F32), (WIDTH, F32), (WIDTH, F32)],
        acc_out=[((1, LANE), F32), ((D_MODEL, D_MODEL), F32), ((WIDTH, D_MODEL), F32), ((WIDTH, D_MODEL), F32),
                 ((1, WIDTH), F32), ((1, WIDTH), F32), ((1, D_MODEL), F32), ((1, D_MODEL), F32), ((1, D_MODEL), F32)])
    (dz, dma, dmb, dgpa, dgpb, dya, dy, dyb,
     loss_row, g_wout, g_wpa, g_wpb, g_gn_g, g_gn_b, g_post_g, g_post_b, dgate) = tail

    dq, dk, dv = attention_backward(q, k, v, ya, dya, lse)
    dq_c, dkv_c, dkr, dkrr, g_wq, g_wqr, g_wkn, g_wv, g_gq, g_gkv = row_call(
        "mla_prep_bwd", mla_prep_bwd_tile, n_rows,
        [pcol(P_QC, 256), pcol(P_KVC, 128), (cos_t, LANE, 0), (sin_t, LANE, 0),
         (dq, HEADS * LANE, 0), (dk, HEADS * LANE, 0), (dv, HEADS * LANE, 0)],
        mla_consts, [(256, F32), (128, F32), (128, F32), (128, F32)],
        acc_out=[((Q_RANK, HEADS * LANE), F32)] * 2 + [((KV_RANK, HEADS * LANE), F32)] * 2
        + [((1, Q_RANK), F32), ((1, KV_RANK), F32)])

    dat, dbt, dkt, drt, dvv, dlw = wkv_backward(at, bt, kt, rt, uv, clf, m0s, state_maps, out_maps, dy)
    (dr0, dk0, dv0, dl0, g_mu_r, g_mu_k, g_mu_v, g_mu_l, g_w0, g_a0, g_k_k, g_k_a, g_r_k, g_wdec, g_wiclr) = row_call(
        "rwkv_prep_bwd", rwkv_prep_bwd_tile, n_rows,
        rwkv_rows + [(drt, WIDTH, 0), (dat, WIDTH, 0), (dbt, WIDTH, 0), (dkt, WIDTH, 0), (dvv, WIDTH, 0),
                     (dlw, WIDTH, 0), (dyb, WIDTH, 0)],
        rwkv_consts + [rk_row], [(512, F32), (512, F32), (512, F32), (128, F32)],
        acc_out=[((1, 512), F32)] * 3 + [((1, 128), F32)] + [((1, 512), F32)] * 5 + [((LANE, WIDTH), F32)] * 2,
        halo_in=rwkv_rows, carry=[512, 512, 512, 128], reverse=True)

    li = jnp.arange(LANE)
    src, dst = li[:, None], li[None, :]
    half = ROPE // 2
    unrot = (jnp.where((dst >= NOPE) & (dst < NOPE + half) & (src == dst + half), 1.0, 0.0)
             - jnp.where((dst >= NOPE + half) & (dst < QK_DIM) & (src == dst - half), 1.0, 0.0)).astype(BF16)
    dx, h_t, dproj_blocks, dshift, dscale = in_backward(
        xr, dz, [dma, dmb, dr0, dk0, dv0, dgpa, dgpb, dq_c, dkv_c, dkr, dkrr, dl0], mod, w_in_p, unrot)

    grads_full = {
        "w_uq": unpad_heads_q_grad(g_wq, g_wqr), "w_ukv": unpad_heads_kv_grad(g_wkn, g_wv),
        "w_decay_up": g_wdec[:LORA], "w_iclr_up": g_wiclr[LORA:],
        "w_proj_a": g_wpa, "w_proj_b": g_wpb, "w_out": g_wout}
    blocks = [(grads_full[n].reshape(N_DEV, r, cdim) if n == "w_out"
               else grads_full[n].reshape(r, N_DEV, cdim).transpose(1, 0, 2)).astype(BF16) for n, r, cdim in SHARDED[1:]]
    dmod = jnp.concatenate([dshift, dscale, dgate], axis=1)
    small = jnp.concatenate([dmod, g_gq, g_gkv, g_mu_r, g_mu_k, g_mu_v, g_mu_l, g_w0, g_a0, g_k_k, g_k_a, g_r_k,
                             g_gn_g, g_gn_b, g_post_g, g_post_b, loss_row], axis=1)
    my_x, my_y, my_c = lax.axis_index("x"), lax.axis_index("y"), lax.axis_index("c")
    chip_order = [4 * (my_x ^ fx) + 2 * (my_y ^ fy) for fx, fy in ((1, 1), (1, 0), (0, 1), (0, 0))]
    order = jnp.stack([ch + (1 - my_c) for ch in chip_order] + [ch + my_c for ch in chip_order]).astype(jnp.int32)
    *got_blocks, got_small = in_weight_grad_exchange(h_t, dproj_blocks, blocks, small, order)
    loss = jnp.sum(got_small[:, 0, SMALL_ELEMS])

    ada_cols = w_ada.shape[2]
    dmod_all = got_small[:, 0, :3 * D_MODEL]
    got_small = got_small[:, :, :SMALL_ELEMS]
    g_ada = ada_weight_grad(c_all, lax.dynamic_slice_in_dim(dmod_all, me * ada_cols, ada_cols, axis=1))

    def small_row(tree):
        return jnp.concatenate([tree[n].reshape(1, -1) for n, _ in SMALL], axis=1)

    outs = [dict() for _ in range(4)]
    res = adamw(g_ada[None], w_ada[0], m_w_ada[0], v_w_ada[0], "adamw_w_ada")
    for kind in range(4):
        outs[kind]["w_ada"] = res[kind][None]
    for (n, r, cdim), got in zip(SHARDED, got_blocks):
        res = adamw(got, weights[n][0], mom1[n][0], mom2[n][0], "adamw_" + n)
        for kind in range(4):
            outs[kind][n] = res[kind][None]
    res = adamw(got_small, small_row(weights), small_row(mom1), small_row(mom2), "adamw_small")
    for kind in range(4):
        off = 0
        for n, size in SMALL:
            outs[kind][n] = res[kind][:, off:off + size].reshape(weights[n].shape)
            off += size
    return (loss, dx[None], *[outs[0][n] for n in names], *[outs[1][n] for n in names],
            *[outs[2][n] for n in names], *[outs[3][n] for n in names])
```

```python
import functools
import math

import jax
import jax.numpy as jnp
from jax import lax
from jax.experimental import pallas as pl
from jax.experimental.pallas import tpu as pltpu

F32 = jnp.float32
BF16 = jnp.bfloat16
HIGHEST = lax.Precision.HIGHEST
MESH_IDS = pl.DeviceIdType.MESH

N_DEV = 8
D_MODEL = 1024
LN_EPS = 1e-5
RMS_EPS = 1e-6
GN_EPS = 64e-5
HEADS = 8
Q_RANK = 256
KV_RANK = 128
ROPE = 32
NOPE = 64
QK_DIM = NOPE + ROPE
WIDTH = 512
HEAD = 64
LORA = 64
CHUNK = 64
DEPTH = 1
ALPHA = (2.0 * DEPTH) ** 0.25
ROPE_THETA = 10000.0
ATTN_SCALE = QK_DIM ** -0.5
DECAY_SCALE = math.exp(-0.5)

ADAM_LR = 0.001
ADAM_B1 = 0.9
ADAM_B2 = 0.999
ADAM_EPS = 1e-08
ADAM_WD = 0.01
ADAM_STEP = 10

LANE = 128
PAIR = 2 * HEAD
ROW_TILE = 256
ATTN_FWD_TILES = (512, 1024)
ATTN_BWD_TILES = (512, 512)
LOG2_E = math.log2(math.e)
Q_PRESCALE = ATTN_SCALE * LOG2_E
WKV_CHUNKS_PER_STEP = 8
VMEM_LIMIT = 56 * 1024 * 1024

P_MA, P_MB, P_R, P_K, P_V, P_GPA, P_GPB, P_QC, P_KVC, P_KR, P_KRR, P_LORA = (
    0, 1024, 2048, 2560, 3072, 3584, 4096, 4608, 4864, 4992, 5120, 5248)
P_WIDTH = 5376
DW_BLOCK = 768

N_QC, N_KVC, N_KROPE, N_GPA, N_RWKV, N_GPB, N_MA, N_MB = 0, 256, 384, 416, 928, 2592, 3104, 4128
IN_WIDTH = 5152

SHARDED = (("w_in", 1024, 644), ("w_uq", 256, 96), ("w_ukv", 128, 128), ("w_decay_up", 64, 64),
           ("w_iclr_up", 64, 64), ("w_proj_a", 512, 128), ("w_proj_b", 512, 128), ("w_out", 128, 1024))
SHARD_ELEMS = sum(r * c for _, r, c in SHARDED)
SHARD_ROWS = SHARD_ELEMS // LANE
GATHER_ROWS = SHARD_ROWS + 2 * D_MODEL // LANE
SMALL = (("b_ada", 3072), ("q_norm_g", 256), ("kv_norm_g", 128), ("mu_rwkv", 1664), ("w0", 512), ("a0", 512),
         ("k_k", 512), ("k_a", 512), ("r_k", 512), ("gn_g", 512), ("gn_b", 512), ("post_g", 1024), ("post_b", 1024))
SMALL_ELEMS = sum(n for _, n in SMALL)
SMALL_ROWS = SMALL_ELEMS // LANE


def mm(a, b):
    return jnp.dot(a.astype(BF16), b.astype(BF16), preferred_element_type=F32)


def mm_nt(a, b):
    return lax.dot_general(a.astype(BF16), b.astype(BF16), (((1,), (1,)), ((), ())), preferred_element_type=F32)


def mm_tn(a, b):
    return lax.dot_general(a.astype(BF16), b.astype(BF16), (((0,), (0,)), ((), ())), preferred_element_type=F32)


def hdot(a, b):
    return jnp.dot(a, b, precision=HIGHEST, preferred_element_type=F32)


def hdot_nt(a, b):
    return lax.dot_general(a, b, (((1,), (1,)), ((), ())), precision=HIGHEST, preferred_element_type=F32)


def hdot_tn(a, b):
    return lax.dot_general(a, b, (((0,), (0,)), ((), ())), precision=HIGHEST, preferred_element_type=F32)


def sigmoid(x):
    return 1.0 / (1.0 + jnp.exp(-x))


def colsum(x):
    return jnp.sum(x, axis=0, keepdims=True)


def rowmean(x):
    return jnp.mean(x, axis=-1, keepdims=True)


def layer_norm_stats(x):
    xc = x - rowmean(x)
    rstd = lax.rsqrt(rowmean(xc * xc) + LN_EPS)
    return xc * rstd, rstd


def layer_norm_bwd(dy, xhat, rstd):
    return rstd * (dy - rowmean(dy) - xhat * rowmean(dy * xhat))


def head_sum(x, bd):
    return jnp.concatenate([hdot(x[:, p * LANE:(p + 1) * LANE], bd) for p in range(x.shape[1] // LANE)], axis=1)


def tile_lanes(t, n):
    return jnp.concatenate([t] * n, axis=1)


def row_iota(shape):
    return lax.broadcasted_iota(jnp.int32, shape, 0)


def lane_iota(shape):
    return lax.broadcasted_iota(jnp.int32, shape, 1)


def shift_rows_down(x, row0):
    rolled = pltpu.roll(x, 1, axis=0)
    return jnp.where(row_iota(x.shape) == 0, row0, rolled)


def shift_rows_up(x, row_last):
    rolled = pltpu.roll(x, x.shape[0] - 1, axis=0)
    return jnp.where(row_iota(x.shape) == x.shape[0] - 1, row_last, rolled)


def row_call(name, fn, n_rows, row_in, const_in, row_out, acc_out=(), halo_in=(), carry=(), reverse=False):
    ts = ROW_TILE
    n_tiles = n_rows // ts
    n_in = len(row_in) + len(halo_in) + len(const_in)
    n_ro, n_ao = len(row_out), len(acc_out)

    def tile_of(g):
        return (n_tiles - 1 - g) if reverse else g

    def body(*refs):
        ins = refs[:n_in]
        ro = refs[n_in:n_in + n_ro]
        ao = refs[n_in + n_ro:n_in + n_ro + n_ao]
        cr = refs[n_in + n_ro + n_ao:]
        g = pl.program_id(0)
        step0 = g == 0
        tile0 = tile_of(g) == 0
        for r in cr:
            @pl.when(step0)
            def _(r=r):
                r[...] = jnp.zeros_like(r)
        vals = [r[...] for r in ins]
        outs = fn(step0, tile0, *vals, *[c[0:1, :] for c in cr])
        for r, v in zip(ro, outs[:n_ro]):
            r[...] = v.astype(r.dtype)
        for r, v in zip(ao, outs[n_ro:n_ro + n_ao]):
            @pl.when(step0)
            def _(r=r, v=v):
                r[...] = v.astype(r.dtype)

            @pl.when(jnp.logical_not(step0))
            def _(r=r, v=v):
                r[...] += v.astype(r.dtype)
        for r, v in zip(cr, outs[n_ro + n_ao:]):
            r[0:1, :] = v

    in_specs = [pl.BlockSpec((ts, w), functools.partial(lambda g, cb: (tile_of(g), cb), cb=cb)) for _, w, cb in row_in]
    in_specs += [pl.BlockSpec((8, w), functools.partial(
        lambda g, cb: (jnp.maximum(tile_of(g) * (ts // 8) - 1, 0), cb), cb=cb)) for _, w, cb in halo_in]
    in_specs += [pl.BlockSpec(memory_space=pltpu.VMEM) for _ in const_in]
    out_specs = [pl.BlockSpec((ts, w), lambda g: (tile_of(g), 0)) for w, _ in row_out]
    out_specs += [pl.BlockSpec(s, lambda g: (0, 0)) for s, _ in acc_out]
    out_shape = [jax.ShapeDtypeStruct((n_rows, w), d) for w, d in row_out]
    out_shape += [jax.ShapeDtypeStruct(s, d) for s, d in acc_out]
    return pl.pallas_call(
        body, name=name, grid=(n_tiles,), in_specs=in_specs, out_specs=out_specs, out_shape=out_shape,
        scratch_shapes=[pltpu.VMEM((8, w), F32) for w in carry],
        compiler_params=pltpu.CompilerParams(dimension_semantics=("arbitrary",), vmem_limit_bytes=VMEM_LIMIT),
    )(*[a for a, _, _ in row_in], *[a for a, _, _ in halo_in], *const_in)


def my_position():
    return lax.axis_index("x"), lax.axis_index("y"), lax.axis_index("c")


def flip(pos, k):
    x, y, c = pos
    dx, dy, dc = (k >> 2) & 1, (k >> 1) & 1, k & 1
    return (1 - x if dx else x, 1 - y if dy else y, 1 - c if dc else c)


def flat_index(pos):
    return 4 * pos[0] + 2 * pos[1] + pos[2]


def gather_shards(shards):
    n = len(shards)

    def body(*refs):
        x_refs, out_refs = refs[:n], refs[n:2 * n]
        send_sems, recv_sems, local_sems = refs[2 * n:]
        x, y, c = my_position()
        me, sibling = (x, y, c), (x, y, 1 - c)
        chips = [(1 - x, y), (x, 1 - y), (1 - x, 1 - y)]

        def copy(a, k, block, to, from_input=False):
            slot = out_refs[a].at[flat_index(block)]
            return pltpu.make_async_remote_copy(
                src_ref=x_refs[a] if from_input else slot, dst_ref=slot,
                send_sem=send_sems.at[7 * a + k], recv_sem=recv_sems.at[7 * a + k],
                device_id=to, device_id_type=MESH_IDS)

        mine = [pltpu.make_async_copy(x_refs[a], out_refs[a].at[flat_index(me)], local_sems.at[a]) for a in range(n)]
        for cp in mine:
            cp.start()
        first = []
        for a in range(n):
            first.append(copy(a, 0, me, sibling, from_input=True))
            first += [copy(a, 1 + j, me, (*chip, c), from_input=True) for j, chip in enumerate(chips)]
        for cp in first:
            cp.start()
        passed = []
        for j, chip in enumerate(chips):
            for a in range(n):
                copy(a, 1 + j, (*chip, c), me).wait_recv()
                cp = copy(a, 4 + j, (*chip, c), sibling)
                cp.start()
                passed.append(cp)
        for a in range(n):
            copy(a, 0, sibling, me).wait_recv()
            for j, chip in enumerate(chips):
                copy(a, 4 + j, (*chip, 1 - c), me).wait_recv()
        for cp in first + passed:
            cp.wait_send()
        for cp in mine:
            cp.wait()

    return pl.pallas_call(
        body, name="gather_shards",
        out_shape=[jax.ShapeDtypeStruct((N_DEV,) + s.shape, s.dtype) for s in shards],
        in_specs=[pl.BlockSpec(memory_space=pl.ANY)] * n, out_specs=[pl.BlockSpec(memory_space=pl.ANY)] * n,
        scratch_shapes=[pltpu.SemaphoreType.DMA((7 * n,)), pltpu.SemaphoreType.DMA((7 * n,)),
                        pltpu.SemaphoreType.DMA((n,))],
    )(*shards)


def ada_modulation(c_all, w_ada_loc, b_ada_blocks):
    cols = w_ada_loc.shape[1]

    def body(c_ref, w_ref, b_ref, out_ref, send_sems, recv_sems):
        me = my_position()
        mi = flat_index(me)
        cv = c_ref[...]
        res = hdot(cv * sigmoid(cv), w_ref[...]) + b_ref[pl.ds(mi, 1), :]
        out_ref[mi] = res
        sends = []
        for k in range(1, N_DEV):
            cp = pltpu.make_async_remote_copy(
                src_ref=out_ref.at[mi], dst_ref=out_ref.at[mi], send_sem=send_sems.at[k - 1],
                recv_sem=recv_sems.at[k - 1], device_id=flip(me, k), device_id_type=MESH_IDS)
            cp.start()
            sends.append(cp)
        for k in range(1, N_DEV):
            pi = flat_index(flip(me, k))
            pltpu.make_async_remote_copy(
                src_ref=out_ref.at[pi], dst_ref=out_ref.at[pi], send_sem=send_sems.at[k - 1],
                recv_sem=recv_sems.at[k - 1], device_id=flip(me, k), device_id_type=MESH_IDS).wait_recv()
        for cp in sends:
            cp.wait_send()

    return pl.pallas_call(
        body, name="ada_modulation",
        out_shape=jax.ShapeDtypeStruct((N_DEV, N_DEV, cols), F32),
        in_specs=[pl.BlockSpec(memory_space=pltpu.VMEM)] * 3, out_specs=pl.BlockSpec(memory_space=pltpu.VMEM),
        scratch_shapes=[pltpu.SemaphoreType.DMA((7,)), pltpu.SemaphoreType.DMA((7,))],
    )(c_all, w_ada_loc, b_ada_blocks)


def fwd_in_tile(step0, tile0, x, mod, w_in_p):
    xhat, _ = layer_norm_stats(x)
    h = xhat * (1.0 + mod[1:2]) + mod[0:1]
    return (mm(h, w_in_p),)


def rms_norm_fwd(x, g):
    r = lax.rsqrt(rowmean(x * x) + RMS_EPS)
    xh = x * r
    return xh * g, xh, r


def key_rope_mask(shape):
    return (lane_iota(shape) >= NOPE).astype(F32)


def mla_prep_tile(step0, tile0, q_c, kv_c, kr, krr, cos, sin, gq, gkv, wq, wqr, wkn, wv):
    qn, _, _ = rms_norm_fwd(q_c, gq)
    kvn, _, _ = rms_norm_fwd(kv_c, gkv)
    q = (mm(qn, wq) * tile_lanes(cos, HEADS) + mm(qn, wqr) * tile_lanes(sin, HEADS)) * Q_PRESCALE
    kpe = kr * (cos * key_rope_mask(cos.shape)) + krr * sin
    k = mm(kvn, wkn) + tile_lanes(kpe, HEADS)
    v = mm(kvn, wv)
    return q, k, v


def rwkv_prep_core(tile0, r0, k0, v0, l0, hr, hk, hv, hl, mu_r, mu_k, mu_v, mu_l, w0, a0, k_k, k_a,
                   w_dec, w_iclr, tril, same, bd):
    def shifted(x, halo, mu):
        row0 = jnp.where(tile0, 0.0, halo[7:8, :])
        prev = shift_rows_down(x, row0)
        return x + (prev - x) * mu, prev

    ur, pr = shifted(r0, hr, mu_r)
    uk, pk = shifted(k0, hk, mu_k)
    uv, pv = shifted(v0, hv, mu_v)
    ul, plo = shifted(l0, hl, mu_l)
    th = jnp.tanh(ul)
    sg = sigmoid(w0 + mm(th, w_dec))
    lw = -DECAY_SCALE * sg
    a_ic = sigmoid(a0 + mm(ul, w_iclr))
    kkraw = uk * k_k
    nrm_raw = jnp.sqrt(head_sum(kkraw * kkraw, bd))
    nrm = jnp.maximum(nrm_raw, 1e-12)
    kk = kkraw / nrm
    k2 = uk * (1.0 + (a_ic - 1.0) * k_a)
    lc = hdot(tril, lw)
    lcl = hdot(same, lw)
    return dict(ur=ur, uk=uk, uv=uv, ul=ul, pr=pr, pk=pk, pv=pv, pl=plo, th=th, sg=sg, lw=lw, a_ic=a_ic,
                kkraw=kkraw, nrm_raw=nrm_raw, nrm=nrm, kk=kk, k2=k2, lc=lc, lcl=lcl)


def rwkv_prep_tile(step0, tile0, r0, k0, v0, l0, hr, hk, hv, hl, *consts):
    f = rwkv_prep_core(tile0, r0, k0, v0, l0, hr, hk, hv, hl, *consts)
    lc, lw = f["lc"], f["lw"]
    e_neg = jnp.exp(-lc)
    rt = f["ur"] * jnp.exp(lc)
    at = -f["kk"] * jnp.exp(lc - lw)
    bt = f["kk"] * f["a_ic"] * e_neg
    kt = f["k2"] * e_neg
    return rt, at, bt, kt, jnp.exp(f["lcl"]), f["uv"], f["ur"], f["k2"]


def wkv_masks():
    lane = lane_iota((1, PAIR))
    m_lo = (lane < HEAD).astype(F32)
    ri = row_iota((CHUNK, CHUNK))
    ci = lane_iota((CHUNK, CHUNK))
    r2 = row_iota((PAIR, PAIR))
    c2 = lane_iota((PAIR, PAIR))
    bd = ((r2 < HEAD) == (c2 < HEAD)).astype(F32)
    eye2 = (r2 == c2).astype(F32)
    return (m_lo, 1.0 - m_lo), ri > ci, ri >= ci, (ri == ci).astype(F32), bd, eye2


def wkv_chunks_pre(chunks, masks):
    ms, strict, incl, eye, bd, eye2 = masks
    items = [(c, m) for c in range(len(chunks)) for m in ms]
    at, bt, kt, rt, v, cl = (list(t) for t in zip(*chunks))
    atm = [at[c] * m for c, m in items]
    rtm = [rt[c] * m for c, m in items]
    aab = [jnp.where(strict, mm_nt(x, bt[c]), 0.0) for x, (c, _) in zip(atm, items)]
    aak = [jnp.where(strict, mm_nt(x, kt[c]), 0.0) for x, (c, _) in zip(atm, items)]
    prb = [jnp.where(incl, mm_nt(x, bt[c]), 0.0) for x, (c, _) in zip(rtm, items)]
    prk = [jnp.where(incl, mm_nt(x, kt[c]), 0.0) for x, (c, _) in zip(rtm, items)]
    tinv = [eye + a for a in aab]
    power = aab
    for _ in range(5):
        power = [mm(p, p) for p in power]
        tinv = [t + mm(t, p) for t, p in zip(tinv, power)]

    def by_chunk(parts):
        return [parts[2 * c] + parts[2 * c + 1] for c in range(len(chunks))]

    w = by_chunk([mm(a, v[c] * m) for a, (c, m) in zip(aak, items)])
    ah = by_chunk([mm(t, x) for t, x in zip(tinv, atm)])
    wh = by_chunk([mm(t, w[c] * m) for t, (c, m) in zip(tinv, items)])
    rh = [r + d for r, d in zip(rt, by_chunk([mm(p, ah[c] * m) for p, (c, m) in zip(prb, items)]))]
    yh = by_chunk([mm(p, wh[c] * m) + mm(q, v[c] * m) for p, q, (c, m) in zip(prb, prk, items)])
    bc = [b * c_ for b, c_ in zip(bt, cl)]
    kc = [k * c_ for k, c_ in zip(kt, cl)]
    g = [eye2 * c_ + bd * mm_tn(b, a) for c_, b, a in zip(cl, bc, ah)]
    h = [bd * (mm_tn(b, w_) + mm_tn(k, v_)) for b, w_, k, v_ in zip(bc, wh, kc, v)]
    return g, h, rh, yh, (items, atm, rtm, tinv, aak, prb, prk, ah, wh, bc, kc)


def wkv_chunks_grad(chunks, m0, dy, dm1, masks):
    ms, strict, incl, eye, bd, eye2 = masks
    n = len(chunks)
    _, _, _, _, (items, atm, rtm, tinv, aak, prb, prk, ah, wh, bc, kc) = wkv_chunks_pre(chunks, masks)
    at, bt, kt, rt, v, cl = (list(t) for t in zip(*chunks))

    def by_chunk(parts):
        return [parts[2 * c] + parts[2 * c + 1] for c in range(n)]

    u = [mm(a, m) + w for a, m, w in zip(ah, m0, wh)]
    dm1 = [d * bd for d in dm1]
    dym = [dy[c] * m for c, m in items]
    du = [mm(b, d) + e for b, d, e in zip(bc, dm1, by_chunk([mm_tn(p, x) for p, x in zip(prb, dym)]))]
    dv = [mm(k, d) + e for k, d, e in zip(kc, dm1, by_chunk([mm_tn(p, x) for p, x in zip(prk, dym)]))]
    dz = by_chunk([mm_tn(t, du[c] * m) for t, (c, m) in zip(tinv, items)])
    dzm = [dz[c] * m for c, m in items]
    dv = [a + b for a, b in zip(dv, by_chunk([mm_tn(a_, x) for a_, x in zip(aak, dzm)]))]
    drt = [mm_nt(d, m) for d, m in zip(dy, m0)]
    dat = [mm_nt(d, m) for d, m in zip(dz, m0)]
    udm = [mm_nt(x, d) for x, d in zip(u, dm1)]
    vdm = [mm_nt(x, d) for x, d in zip(v, dm1)]
    daab = [jnp.where(strict, mm_nt(x, u[c]), 0.0) for x, (c, _) in zip(dzm, items)]
    daak = [jnp.where(strict, mm_nt(x, v[c]), 0.0) for x, (c, _) in zip(dzm, items)]
    dprb = [jnp.where(incl, mm_nt(x, u[c]), 0.0) for x, (c, _) in zip(dym, items)]
    dprk = [jnp.where(incl, mm_nt(x, v[c]), 0.0) for x, (c, _) in zip(dym, items)]
    drt2 = by_chunk([(mm(p, bt[c]) + mm(q, kt[c])) * m for p, q, (c, m) in zip(dprb, dprk, items)])
    dat2 = by_chunk([(mm(p, bt[c]) + mm(q, kt[c])) * m for p, q, (c, m) in zip(daab, daak, items)])
    dbt2 = by_chunk([mm_tn(p, r) + mm_tn(a_, x) for p, r, a_, x in zip(dprb, rtm, daab, atm)])
    dkt2 = by_chunk([mm_tn(p, r) + mm_tn(a_, x) for p, r, a_, x in zip(dprk, rtm, daak, atm)])
    ones = jnp.ones((8, PAIR), F32)
    upper = (lane_iota((CHUNK, CHUNK)) >= row_iota((CHUNK, CHUNK))).astype(F32)
    out = []
    for c in range(n):
        drt_c = drt[c] + drt2[c]
        dat_c = dat[c] + dat2[c]
        dbt_c = udm[c] * cl[c] + dbt2[c]
        dkt_c = vdm[c] * cl[c] + dkt2[c]
        dlcl = hdot_nt(ones, dm1[c] * m0[c])[0:1, :] * cl[c] + colsum(bc[c] * udm[c] + kc[c] * vdm[c])
        g = drt_c * rt[c] - dbt_c * bt[c] - dkt_c * kt[c] + dat_c * at[c]
        dlw = hdot(upper, g) - dat_c * at[c] + dlcl
        out.append((dat_c, dbt_c, dkt_c, drt_c, dv[c], dlw))
    return out


def wkv_forward(at, bt, kt, rt, v, clf):
    n_rows = at.shape[0]
    cps = WKV_CHUNKS_PER_STEP
    rb = cps * CHUNK
    n_steps = n_rows // rb

    def body(a_ref, b_ref, k_ref, r_ref, v_ref, c_ref, y_ref, m0_ref, g_ref, rh_ref, m_scr):
        @pl.when(pl.program_id(1) == 0)
        def _():
            m_scr[...] = jnp.zeros_like(m_scr)

        masks = wkv_masks()
        chunks = []
        for cc in range(cps):
            sl = slice(cc * CHUNK, (cc + 1) * CHUNK)
            chunks.append((a_ref[sl, :], b_ref[sl, :], k_ref[sl, :], r_ref[sl, :], v_ref[sl, :],
                           c_ref[cc * CHUNK:cc * CHUNK + 1, :]))
        pre = wkv_chunks_pre(chunks, masks)[:4]
        m = m_scr[...]
        for cc, (g, h, rh, yh) in enumerate(zip(*pre)):
            sl = slice(cc * CHUNK, (cc + 1) * CHUNK)
            m0_ref[0, cc] = m
            g_ref[0, cc] = g
            rh_ref[sl, :] = rh
            y_ref[sl, :] = hdot(rh, m) + yh
            m = hdot(g, m) + h
        m_scr[...] = m

    blk = pl.BlockSpec((rb, PAIR), lambda p, s: (s, p))
    state_blk = pl.BlockSpec((1, cps, PAIR, PAIR), lambda p, s: (p, s, 0, 0))
    state_shape = jax.ShapeDtypeStruct((WIDTH // PAIR, n_rows // CHUNK, PAIR, PAIR), F32)
    return pl.pallas_call(
        body, name="wkv_forward", grid=(WIDTH // PAIR, n_steps),
        in_specs=[blk] * 6,
        out_specs=[blk, state_blk, state_blk, blk],
        out_shape=[jax.ShapeDtypeStruct((n_rows, WIDTH), F32), state_shape, state_shape,
                   jax.ShapeDtypeStruct((n_rows, WIDTH), F32)],
        scratch_shapes=[pltpu.VMEM((PAIR, PAIR), F32)],
        compiler_params=pltpu.CompilerParams(dimension_semantics=("arbitrary", "arbitrary"),
                                             vmem_limit_bytes=VMEM_LIMIT),
    )(at, bt, kt, rt, v, clf)


def wkv_backward(at, bt, kt, rt, v, clf, m0s, gs, rh, dy):
    n_rows = at.shape[0]
    cps = WKV_CHUNKS_PER_STEP
    rb = cps * CHUNK
    n_steps = n_rows // rb

    def body(a_ref, b_ref, k_ref, r_ref, v_ref, c_ref, m0_ref, g_ref, rh_ref, dy_ref,
             da_ref, db_ref, dk_ref, dr_ref, dv_ref, dlw_ref, dm_scr):
        @pl.when(pl.program_id(1) == 0)
        def _():
            dm_scr[...] = jnp.zeros_like(dm_scr)

        masks = wkv_masks()
        bd = masks[4]
        dm = dm_scr[...]
        dm1 = [None] * cps
        for cc in reversed(range(cps)):
            sl = slice(cc * CHUNK, (cc + 1) * CHUNK)
            dm1[cc] = dm
            dm = bd * (hdot_tn(g_ref[0, cc], dm) + hdot_tn(rh_ref[sl, :], dy_ref[sl, :]))
        dm_scr[...] = dm
        chunks, m0, dys = [], [], []
        for cc in range(cps):
            sl = slice(cc * CHUNK, (cc + 1) * CHUNK)
            chunks.append((a_ref[sl, :], b_ref[sl, :], k_ref[sl, :], r_ref[sl, :], v_ref[sl, :],
                           c_ref[cc * CHUNK:cc * CHUNK + 1, :]))
            m0.append(m0_ref[0, cc])
            dys.append(dy_ref[sl, :])
        grads = wkv_chunks_grad(chunks, m0, dys, dm1, masks)
        for cc, (dat, dbt, dkt, drt, dv, dlw) in enumerate(grads):
            sl = slice(cc * CHUNK, (cc + 1) * CHUNK)
            da_ref[sl, :] = dat
            db_ref[sl, :] = dbt
            dk_ref[sl, :] = dkt
            dr_ref[sl, :] = drt
            dv_ref[sl, :] = dv
            dlw_ref[sl, :] = dlw

    blk = pl.BlockSpec((rb, PAIR), lambda p, s: (n_steps - 1 - s, p))
    state_blk = pl.BlockSpec((1, cps, PAIR, PAIR), lambda p, s: (p, n_steps - 1 - s, 0, 0))
    return pl.pallas_call(
        body, name="wkv_backward", grid=(WIDTH // PAIR, n_steps),
        in_specs=[blk] * 6 + [state_blk, state_blk, blk, blk],
        out_specs=[blk] * 6,
        out_shape=[jax.ShapeDtypeStruct((n_rows, WIDTH), F32)] * 6,
        scratch_shapes=[pltpu.VMEM((PAIR, PAIR), F32)],
        compiler_params=pltpu.CompilerParams(dimension_semantics=("arbitrary", "arbitrary"),
                                             vmem_limit_bytes=VMEM_LIMIT),
    )(at, bt, kt, rt, v, clf, m0s, gs, rh, dy)


def visible(q_row0, k_row0, shape):
    qc = (q_row0 + row_iota(shape)) // CHUNK
    kc = (k_row0 + lane_iota(shape)) // CHUNK
    return kc <= qc


def attention_forward(q, k, v):
    n_rows = q.shape[0]
    tq, tk = ATTN_FWD_TILES
    n_q = n_rows // tq
    n_masked = max(1, tq // tk)

    def body(q_ref, k_ref, v_ref, o_ref, lse_ref):
        i = pl.program_id(1)
        lane = lane_iota((tq, LANE))
        heads = [slice(0, LANE), slice(LANE, 2 * LANE)]
        qs = [q_ref[:, cols] for cols in heads]

        def step(j, carry, masked):
            rows = pl.ds(pl.multiple_of(j * tk, tk), tk)
            ss = [mm_nt(qh, k_ref[rows, cols]) for qh, cols in zip(qs, heads)]
            if masked:
                vis = visible(i * tq, j * tk, ss[0].shape)
                ss = [jnp.where(vis, s, -jnp.inf) for s in ss]
            ps, stats = [], []
            for s, (m, l, _) in zip(ss, carry):
                m_new = jnp.maximum(m, jnp.max(s, axis=-1, keepdims=True))
                p = jnp.exp2(s - m_new)
                alpha = jnp.exp2(m - m_new)
                ps.append(p)
                stats.append((m_new, alpha, alpha * l + jnp.sum(p, axis=-1, keepdims=True)))
            pvs = [mm(p, v_ref[rows, cols]) for p, cols in zip(ps, heads)]
            return tuple((m_new, l, alpha * acc + pv)
                         for (m_new, alpha, l), (_, _, acc), pv in zip(stats, carry, pvs))

        carry = tuple((jnp.full((tq, 1), -jnp.inf, F32), jnp.zeros((tq, 1), F32), jnp.zeros((tq, LANE), F32))
                      for _ in heads)
        n_full = (i * tq) // tk
        carry = lax.fori_loop(0, n_full, functools.partial(step, masked=False), carry)
        for extra in range(n_masked):
            carry = step(n_full + extra, carry, masked=True)
        (m0, l0, acc0), (m1, l1, acc1) = carry
        o_ref[...] = acc0 / l0 + acc1 / l1
        lse_ref[...] = jnp.where(lane >= HEAD, m1 + jnp.log2(l1), m0 + jnp.log2(l0))

    return pl.pallas_call(
        body, name="attention_forward", grid=(HEADS // 2, n_q),
        in_specs=[pl.BlockSpec((tq, 2 * LANE), lambda p, i: (i, p)),
                  pl.BlockSpec((n_rows, 2 * LANE), lambda p, i: (0, p)),
                  pl.BlockSpec((n_rows, 2 * LANE), lambda p, i: (0, p))],
        out_specs=[pl.BlockSpec((tq, LANE), lambda p, i: (i, p))] * 2,
        out_shape=[jax.ShapeDtypeStruct((n_rows, WIDTH), F32)] * 2,
        compiler_params=pltpu.CompilerParams(dimension_semantics=("arbitrary", "arbitrary"),
                                             vmem_limit_bytes=VMEM_LIMIT),
    )(q, k, v)


def attention_backward(q, k, v, o, do, lse):
    n_rows = q.shape[0]
    tq, tk = ATTN_BWD_TILES
    n_q = n_rows // tq
    n_masked = max(1, tk // tq)

    def body(q_ref, k_ref, v_ref, o_ref, do_ref, lse_ref, dq_ref, dk_ref, dv_ref):
        j = pl.program_id(1)

        @pl.when(j == 0)
        def _():
            dq_ref[...] = jnp.zeros_like(dq_ref)

        lane = lane_iota((tq, LANE))
        heads = [slice(0, LANE), slice(LANE, 2 * LANE)]
        ks = [k_ref[:, cols] for cols in heads]
        vs = [v_ref[:, cols] for cols in heads]
        head_lanes = [(lane < HEAD).astype(F32), (lane >= HEAD).astype(F32)]

        def step(i, carry, masked):
            rows = pl.ds(pl.multiple_of(i * tq, tq), tq)
            qs = [q_ref[rows, cols] for cols in heads]
            dout = do_ref[rows, :]
            dout_o = dout * o_ref[rows, :]
            lse_t = lse_ref[rows, :]
            ss = [mm_nt(qh, kh) for qh, kh in zip(qs, ks)]
            dps = [mm_nt(dout, vh) for vh in vs]
            ps, dss = [], []
            for hh in range(2):
                delta = jnp.sum(dout_o * head_lanes[hh], axis=-1, keepdims=True)
                lse_h = jnp.sum(jnp.where(lane == hh * HEAD, lse_t, 0.0), axis=-1, keepdims=True)
                p = jnp.exp2(ss[hh] - lse_h)
                if masked:
                    p = jnp.where(visible(i * tq, j * tk, p.shape), p, 0.0)
                ps.append(p)
                dss.append(p * (dps[hh] - delta))
            dvs = [mm_tn(p, dout) for p in ps]
            dqs = [mm(ds, kh) for ds, kh in zip(dss, ks)]
            dks = [mm_tn(ds, qh) for ds, qh in zip(dss, qs)]
            for cols, dq in zip(heads, dqs):
                dq_ref[rows, cols] += dq * ATTN_SCALE
            return tuple((dk + a, dv + b) for (dk, dv), a, b in zip(carry, dks, dvs))

        carry = tuple((jnp.zeros((tk, LANE), F32), jnp.zeros((tk, LANE), F32)) for _ in heads)
        i_first = (j * tk) // tq
        for extra in range(n_masked):
            carry = step(i_first + extra, carry, masked=True)
        carry = lax.fori_loop(i_first + n_masked, n_q, functools.partial(step, masked=False), carry)
        for cols, (dk, dv) in zip(heads, carry):
            dk_ref[:, cols] = dk * (1.0 / LOG2_E)
            dv_ref[:, cols] = dv

    full = lambda w: pl.BlockSpec((n_rows, w), lambda p, j: (0, p))
    blk = pl.BlockSpec((tk, 2 * LANE), lambda p, j: (j, p))
    return pl.pallas_call(
        body, name="attention_backward", grid=(HEADS // 2, n_rows // tk),
        in_specs=[full(2 * LANE), blk, blk, full(LANE), full(LANE), full(LANE)],
        out_specs=[full(2 * LANE), blk, blk],
        out_shape=[jax.ShapeDtypeStruct((n_rows, HEADS * LANE), F32)] * 3,
        compiler_params=pltpu.CompilerParams(dimension_semantics=("arbitrary", "arbitrary"),
                                             vmem_limit_bytes=VMEM_LIMIT),
    )(q, k, v, o, do, lse)


def tail_tile(step0, tile0, x, tgt, ma, mb, gpa, gpb, ya, y, ur, k2, uv,
              mod, wpa, wpb, wout, gn_g, gn_b, r_k, post_g, post_b, bd):
    gate = mod[2:3]
    inv = 1.0 / HEAD
    yc = y - head_sum(y, bd) * inv
    rs = lax.rsqrt(head_sum(yc * yc, bd) * inv + GN_EPS)
    yn = yc * rs
    yb = yn * gn_g + gn_b + head_sum(ur * k2 * r_k, bd) * uv
    sga, sgb = sigmoid(gpa), sigmoid(gpb)
    sila, silb = gpa * sga, gpb * sgb
    ga, gb = ya * sila, yb * silb
    pa, pb = mm(ga, wpa), mm(gb, wpb)
    sa, sb = sigmoid(ma), sigmoid(mb)
    merged = sa * pa + sb * pb
    sub = mm(merged, wout)
    z = ALPHA * x + (1.0 + gate) * sub
    zhat, rstd = layer_norm_stats(z)
    err = zhat * post_g + post_b - tgt
    loss = 0.5 * jnp.sum(rowmean(err * err), axis=0, keepdims=True) + jnp.zeros((1, LANE), F32)
    dout = err * (1.0 / D_MODEL)
    dpost_g = colsum(dout * zhat)
    dpost_b = colsum(dout)
    dz = layer_norm_bwd(dout * post_g, zhat, rstd)
    dgate = colsum(dz * sub)
    dsub = dz * (1.0 + gate)
    dwout = mm_tn(merged, dsub)
    dmerged = mm_nt(dsub, wout)
    dpa, dpb = dmerged * sa, dmerged * sb
    dma = dmerged * pa * sa * (1.0 - sa)
    dmb = dmerged * pb * sb * (1.0 - sb)
    dwpa = mm_tn(ga, dpa)
    dwpb = mm_tn(gb, dpb)
    dga = mm_nt(dpa, wpa)
    dgb = mm_nt(dpb, wpb)
    dya = dga * sila
    dgpa = dga * ya * (sga * (1.0 + gpa * (1.0 - sga)))
    dyb = dgb * silb
    dgpb = dgb * yb * (sgb * (1.0 + gpb * (1.0 - sgb)))
    dgn_g = colsum(dyb * yn)
    dgn_b = colsum(dyb)
    dyn = dyb * gn_g
    dy = rs * (dyn - head_sum(dyn, bd) * inv - yn * head_sum(dyn * yn, bd) * inv)
    return (dz, dma, dmb, dgpa, dgpb, dya, dy, dyb,
            loss, dwout, dwpa, dwpb, dgn_g, dgn_b, dpost_g, dpost_b, dgate)


def mla_prep_bwd_tile(step0, tile0, q_c, kv_c, cos, sin, dq, dk, dv, gq, gkv, wq, wqr, wkn, wv):
    qn, qh, rq = rms_norm_fwd(q_c, gq)
    kvn, kvh, rkv = rms_norm_fwd(kv_c, gkv)
    dqc = dq * tile_lanes(cos, HEADS)
    dqs = dq * tile_lanes(sin, HEADS)
    dqn = mm_nt(dqc, wq) + mm_nt(dqs, wqr)
    dkvn = mm_nt(dk, wkn) + mm_nt(dv, wv)
    dkpe = dk[:, 0:LANE]
    for h in range(1, HEADS):
        dkpe = dkpe + dk[:, h * LANE:(h + 1) * LANE]
    dkr = dkpe * (cos * key_rope_mask(cos.shape))
    dkrr = dkpe * sin

    def rms_bwd(dyv, xh, r, g):
        dyg = dyv * g
        return r * (dyg - xh * rowmean(dyg * xh)), colsum(dyv * xh)

    dq_c, dgq = rms_bwd(dqn, qh, rq, gq)
    dkv_c, dgkv = rms_bwd(dkvn, kvh, rkv, gkv)
    return (dq_c, dkv_c, dkr, dkrr,
            mm_tn(qn, dqc), mm_tn(qn, dqs), mm_tn(kvn, dk), mm_tn(kvn, dv), dgq, dgkv)


def rwkv_prep_bwd_tile(step0, tile0, r0, k0, v0, l0, drt, dat, dbt, dkt, dvv, dlw, dyb, hr, hk, hv, hl,
                       mu_r, mu_k, mu_v, mu_l, w0, a0, k_k, k_a, w_dec, w_iclr, tril, same, bd, r_k,
                       cr, ck, cv, cl_):
    f = rwkv_prep_core(tile0, r0, k0, v0, l0, hr, hk, hv, hl, mu_r, mu_k, mu_v, mu_l, w0, a0, k_k, k_a,
                       w_dec, w_iclr, tril, same, bd)
    ur, uk, uv, ul, kk, k2, a_ic, sg, th = (f[n] for n in ("ur", "uk", "uv", "ul", "kk", "k2", "a_ic", "sg", "th"))
    lc, lw = f["lc"], f["lw"]
    e_neg = jnp.exp(-lc)
    dur = drt * jnp.exp(lc)
    da = dat * jnp.exp(lc - lw)
    db = dbt * e_neg
    dk2 = dkt * e_neg
    s = head_sum(ur * k2 * r_k, bd)
    duv = dvv + dyb * s
    ds = head_sum(dyb * uv, bd)
    dur = dur + ds * k2 * r_k
    dk2 = dk2 + ds * ur * r_k
    dr_k = colsum(ds * ur * k2)
    dkk = db * a_ic - da
    da_ic = db * kk + dk2 * uk * k_a
    duk = dk2 * (1.0 + (a_ic - 1.0) * k_a)
    dk_a = colsum(dk2 * uk * (a_ic - 1.0))
    dkkraw = jnp.where(f["nrm_raw"] > 1e-12, (dkk - kk * head_sum(dkk * kk, bd)) / f["nrm"], dkk * 1e12)
    duk = duk + dkkraw * k_k
    dk_k = colsum(dkkraw * uk)
    dai = da_ic * a_ic * (1.0 - a_ic)
    dd = dlw * (-DECAY_SCALE) * sg * (1.0 - sg)
    dul = mm_nt(dai, w_iclr) + mm_nt(dd, w_dec) * (1.0 - th * th)

    def unshift(du, x, prev, mu, carry_row):
        nxt = shift_rows_up(du, carry_row)
        return du * (1.0 - mu) + nxt * mu, colsum(du * (prev - x)), du[0:1, :]

    dr0, dmu_r, ncr = unshift(dur, r0, f["pr"], mu_r, cr)
    dk0, dmu_k, nck = unshift(duk, k0, f["pk"], mu_k, ck)
    dv0, dmu_v, ncv = unshift(duv, v0, f["pv"], mu_v, cv)
    dl0, dmu_l, ncl = unshift(dul, l0, f["pl"], mu_l, cl_)
    return (dr0, dk0, dv0, dl0,
            dmu_r, dmu_k, dmu_v, dmu_l, colsum(dd), colsum(dai), dk_k, dk_a, dr_k, mm_tn(th, dd), mm_tn(ul, dai),
            ncr, nck, ncv, ncl)


def in_backward(x, dz, pieces, mod, w_in_p, unrot):
    n_rows = x.shape[0]
    ts = ROW_TILE
    n_p = len(pieces)
    shard_cols = IN_WIDTH // N_DEV

    def body(*refs):
        x_ref, dz_ref = refs[:2]
        p_refs = refs[2:2 + n_p]
        mod_ref, w_ref, unrot_ref = refs[2 + n_p:5 + n_p]
        dx_ref, ht_ref, blocks_ref, dshift_ref, dscale_ref = refs[5 + n_p:]
        step0 = pl.program_id(0) == 0
        dma, dmb, dr0, dk0, dv0, dgpa, dgpb, dq_c, dkv_c, dkr, dkrr, dl0 = (r[...] for r in p_refs)
        dproj = jnp.concatenate([dma, dmb, dr0, dk0, dv0, dgpa, dgpb, dq_c, dkv_c, dkr, dkrr, dl0], axis=1)
        dh = mm_nt(dproj, w_ref[...])
        xhat, rstd = layer_norm_stats(x_ref[...])
        scale1 = 1.0 + mod_ref[1:2, :]
        dx_ref[...] = layer_norm_bwd(dh * scale1, xhat, rstd) + ALPHA * dz_ref[...]
        ht_ref[...] = jnp.transpose(xhat * scale1 + mod_ref[0:1, :]).astype(BF16)
        dkrope = (dkr + mm(dkrr, unrot_ref[...]))[:, NOPE:QK_DIM]
        natural = jnp.concatenate([dq_c, dkv_c, dkrope, dgpa, dr0, dk0, dv0, dl0, dgpb, dma, dmb], axis=1)
        for j in range(N_DEV):
            blocks_ref[j] = natural[:, j * shard_cols:(j + 1) * shard_cols].astype(BF16)
        for ref, val in ((dshift_ref, colsum(dh)), (dscale_ref, colsum(dh * xhat))):
            @pl.when(step0)
            def _(ref=ref, val=val):
                ref[...] = val

            @pl.when(jnp.logical_not(step0))
            def _(ref=ref, val=val):
                ref[...] += val

    row = lambda w: pl.BlockSpec((ts, w), lambda i: (i, 0))
    const = pl.BlockSpec(memory_space=pltpu.VMEM)
    vec = pl.BlockSpec((1, D_MODEL), lambda i: (0, 0))
    return pl.pallas_call(
        body, name="in_backward", grid=(n_rows // ts,),
        in_specs=[row(D_MODEL), row(D_MODEL)] + [row(p.shape[1]) for p in pieces] + [const] * 3,
        out_specs=[row(D_MODEL), pl.BlockSpec((D_MODEL, ts), lambda i: (0, i)),
                   pl.BlockSpec((N_DEV, ts, shard_cols), lambda i: (0, i, 0)), vec, vec],
        out_shape=[jax.ShapeDtypeStruct((n_rows, D_MODEL), F32), jax.ShapeDtypeStruct((D_MODEL, n_rows), BF16),
                   jax.ShapeDtypeStruct((N_DEV, n_rows, shard_cols), BF16),
                   jax.ShapeDtypeStruct((1, D_MODEL), F32), jax.ShapeDtypeStruct((1, D_MODEL), F32)],
        compiler_params=pltpu.CompilerParams(dimension_semantics=("arbitrary",), vmem_limit_bytes=VMEM_LIMIT),
    )(x, dz, *pieces, mod, w_in_p, unrot)


def in_weight_grad_exchange(h_t, dp_blocks, others, small, order):
    n = len(others)
    n_rows = h_t.shape[1]
    ts = 2 * ROW_TILE
    n_i = n_rows // ts
    shard_cols = dp_blocks.shape[2]
    n_chips = N_DEV // 2
    last = N_DEV - 1

    def body(order_ref, h_ref, dp_ref, *rest):
        g_refs, s_ref = rest[:n], rest[n]
        rwin_ref, rg_refs, rs_ref = rest[n + 1], rest[n + 2:2 * n + 2], rest[2 * n + 2]
        (acc, sendbuf, sib_buf, sib_send, sib_recv, win_send, win_recv,
         o_send, o_recv, local_sems) = rest[2 * n + 3:]
        b, i = pl.program_id(0), pl.program_id(1)
        me = my_position()
        mi = flat_index(me)
        sibling = (me[0], me[1], 1 - me[2])

        def other_copies(k, src_index, dst_index):
            peer = flip(me, k)
            out = [pltpu.make_async_remote_copy(
                src_ref=g_refs[a].at[src_index], dst_ref=rg_refs[a].at[dst_index],
                send_sem=o_send.at[(n + 1) * (k - 1) + a], recv_sem=o_recv.at[(n + 1) * (k - 1) + a],
                device_id=peer, device_id_type=MESH_IDS) for a in range(n)]
            out.append(pltpu.make_async_remote_copy(
                src_ref=s_ref, dst_ref=rs_ref.at[dst_index],
                send_sem=o_send.at[(n + 1) * (k - 1) + n], recv_sem=o_recv.at[(n + 1) * (k - 1) + n],
                device_id=peer, device_id_type=MESH_IDS))
            return out

        def local_copies():
            out = [pltpu.make_async_copy(g_refs[a].at[mi], rg_refs[a].at[mi], local_sems.at[a]) for a in range(n)]
            out.append(pltpu.make_async_copy(s_ref, rs_ref.at[mi], local_sems.at[n]))
            return out

        def to_sibling(t):
            return pltpu.make_async_remote_copy(
                src_ref=sendbuf.at[t], dst_ref=sib_buf.at[t], send_sem=sib_send.at[t], recv_sem=sib_recv.at[t],
                device_id=sibling, device_id_type=MESH_IDS)

        def to_owner(t):
            flip_x = (t < 2) * 1
            flip_y = 1 - (t & 1)
            owner = (me[0] ^ flip_x, me[1] ^ flip_y, me[2])
            return pltpu.make_async_remote_copy(
                src_ref=sendbuf.at[n_chips + t], dst_ref=rwin_ref.at[t], send_sem=win_send.at[t],
                recv_sem=win_recv.at[t], device_id=owner, device_id_type=MESH_IDS)

        own_block = pltpu.make_async_copy(sendbuf.at[last], rwin_ref.at[n_chips - 1], local_sems.at[n + 1])

        @pl.when(jnp.logical_and(b == 0, i == 0))
        def _():
            for cp in local_copies():
                cp.start()
            for k in range(1, N_DEV):
                for cp in other_copies(k, flat_index(flip(me, k)), mi):
                    cp.start()

        contrib = jnp.dot(h_ref[...], dp_ref[...], preferred_element_type=F32)

        @pl.when(i == 0)
        def _():
            acc[...] = contrib

        @pl.when(i > 0)
        def _():
            acc[...] += contrib

        @pl.when(jnp.logical_and(i == n_i - 1, b < n_chips))
        def _():
            sendbuf[b] = acc[...].astype(BF16)
            to_sibling(b).start()

        @pl.when(jnp.logical_and(i == n_i - 1, b >= n_chips))
        def _():
            t = b - n_chips
            to_sibling(t).wait_recv()
            sendbuf[b] = (acc[...] + sib_buf[t].astype(F32)).astype(BF16)

            @pl.when(b < last)
            def _():
                to_owner(t).start()

            @pl.when(b == last)
            def _():
                own_block.start()

        @pl.when(jnp.logical_and(b == last, i == n_i - 1))
        def _():
            for t in range(n_chips - 1):
                to_owner(t).wait_recv()
            for k in range(1, N_DEV):
                pi = flat_index(flip(me, k))
                for cp in other_copies(k, pi, pi):
                    cp.wait_recv()
            for t in range(n_chips):
                to_sibling(t).wait_send()
            for t in range(n_chips - 1):
                to_owner(t).wait_send()
            for k in range(1, N_DEV):
                for cp in other_copies(k, flat_index(flip(me, k)), mi):
                    cp.wait_send()
            for cp in local_copies():
                cp.wait()
            own_block.wait()

    hbm = pl.BlockSpec(memory_space=pl.ANY)
    n_sem = 7 * (n + 1)
    grid_spec = pltpu.PrefetchScalarGridSpec(
        num_scalar_prefetch=1, grid=(N_DEV, n_i),
        in_specs=[pl.BlockSpec((D_MODEL, ts), lambda b, i, order: (0, i)),
                  pl.BlockSpec((None, ts, shard_cols), lambda b, i, order: (order[b], i, 0))] + [hbm] * (n + 1),
        out_specs=[hbm] * (n + 2),
        scratch_shapes=[pltpu.VMEM((D_MODEL, shard_cols), F32), pltpu.VMEM((N_DEV, D_MODEL, shard_cols), BF16),
                        pltpu.VMEM((n_chips, D_MODEL, shard_cols), BF16),
                        pltpu.SemaphoreType.DMA((n_chips,)), pltpu.SemaphoreType.DMA((n_chips,)),
                        pltpu.SemaphoreType.DMA((n_chips - 1,)), pltpu.SemaphoreType.DMA((n_chips - 1,)),
                        pltpu.SemaphoreType.DMA((n_sem,)), pltpu.SemaphoreType.DMA((n_sem,)),
                        pltpu.SemaphoreType.DMA((n + 2,))])
    return pl.pallas_call(
        body, name="in_weight_grad_exchange", grid_spec=grid_spec,
        out_shape=[jax.ShapeDtypeStruct((n_chips, D_MODEL, shard_cols), BF16)]
        + [jax.ShapeDtypeStruct(o.shape, o.dtype) for o in others]
        + [jax.ShapeDtypeStruct((N_DEV,) + small.shape, small.dtype)],
        compiler_params=pltpu.CompilerParams(dimension_semantics=("arbitrary", "arbitrary"),
                                             vmem_limit_bytes=VMEM_LIMIT),
    )(order, h_t, dp_blocks, *others, small)


def ada_weight_grad(c_all, dmod_cols):
    def body(c_ref, d_ref, o_ref):
        cv = c_ref[...]
        o_ref[...] = hdot_tn(cv * sigmoid(cv), d_ref[...])

    return pl.pallas_call(
        body, name="ada_weight_grad",
        out_shape=jax.ShapeDtypeStruct((c_all.shape[1], dmod_cols.shape[1]), F32),
    )(c_all, dmod_cols)


def adamw(parts, w, m, v, name):
    k, rows, cols = parts.shape
    rb = 128 if rows % 128 == 0 else rows

    def body(p_ref, w_ref, m_ref, v_ref, g_ref, d_ref, nm_ref, nv_ref):
        g = p_ref[0].astype(F32)
        for i in range(1, k):
            g = g + p_ref[i].astype(F32)
        nm = ADAM_B1 * m_ref[...] + (1.0 - ADAM_B1) * g
        nv = ADAM_B2 * v_ref[...] + (1.0 - ADAM_B2) * (g * g)
        m_hat = nm / (1.0 - ADAM_B1 ** ADAM_STEP)
        v_hat = nv / (1.0 - ADAM_B2 ** ADAM_STEP)
        g_ref[...] = g
        d_ref[...] = -ADAM_LR * (m_hat / (jnp.sqrt(v_hat) + ADAM_EPS) + ADAM_WD * w_ref[...])
        nm_ref[...] = nm
        nv_ref[...] = nv

    blk = pl.BlockSpec((rb, cols), lambda i: (i, 0))
    return pl.pallas_call(
        body, name=name, grid=(rows // rb,),
        in_specs=[pl.BlockSpec((k, rb, cols), lambda i: (0, i, 0)), blk, blk, blk],
        out_specs=[blk] * 4, out_shape=[jax.ShapeDtypeStruct((rows, cols), F32)] * 4,
        compiler_params=pltpu.CompilerParams(dimension_semantics=("arbitrary",), vmem_limit_bytes=VMEM_LIMIT),
    )(parts, w, m, v)


def rot_cols(w):
    return jnp.concatenate([-w[:, ROPE // 2:], w[:, :ROPE // 2]], axis=1)


def unrot_cols(dw):
    return jnp.concatenate([dw[:, ROPE // 2:], -dw[:, :ROPE // 2]], axis=1)


def columns_from_shards(g, rows, cols):
    return g.reshape(N_DEV, rows, cols).transpose(1, 0, 2).reshape(rows, N_DEV * cols)


def shards_from_columns(w, rows, cols):
    return w.reshape(rows, N_DEV, cols).transpose(1, 0, 2).reshape(N_DEV, rows * cols)


def permute_w_in(w):
    z = lambda n: jnp.zeros((D_MODEL, n), w.dtype)
    krope = w[:, N_KROPE:N_KROPE + ROPE]
    rw = N_RWKV
    return jnp.concatenate([
        w[:, N_MA:N_MA + 1024], w[:, N_MB:N_MB + 1024],
        w[:, rw:rw + 512], w[:, rw + 512:rw + 1024], w[:, rw + 1024:rw + 1536],
        w[:, N_GPA:N_GPA + 512], w[:, N_GPB:N_GPB + 512],
        w[:, N_QC:N_QC + 256], w[:, N_KVC:N_KVC + 128],
        z(NOPE), krope, z(LANE - QK_DIM), z(NOPE), rot_cols(krope), z(LANE - QK_DIM),
        w[:, rw + 1536:rw + 1664]], axis=1)


def unpermute_w_in_grad(d):
    rw = P_R
    krope = d[:, P_KR + NOPE:P_KR + QK_DIM] + unrot_cols(d[:, P_KRR + NOPE:P_KRR + QK_DIM])
    return jnp.concatenate([
        d[:, P_QC:P_QC + 256], d[:, P_KVC:P_KVC + 128], krope, d[:, P_GPA:P_GPA + 512],
        d[:, rw:rw + 1536], d[:, P_LORA:P_LORA + 128], d[:, P_GPB:P_GPB + 512],
        d[:, P_MA:P_MA + 1024], d[:, P_MB:P_MB + 1024]], axis=1)


def pad_heads_q(w_uq):
    w = w_uq.reshape(Q_RANK, HEADS, QK_DIM)
    zpad = jnp.zeros((Q_RANK, HEADS, LANE - QK_DIM), w.dtype)
    wq = jnp.concatenate([w, zpad], axis=2).reshape(Q_RANK, HEADS * LANE)
    pe = w[:, :, NOPE:]
    rot = jnp.concatenate([-pe[:, :, ROPE // 2:], pe[:, :, :ROPE // 2]], axis=2)
    wqr = jnp.concatenate([jnp.zeros((Q_RANK, HEADS, NOPE), w.dtype), rot, zpad], axis=2).reshape(Q_RANK, HEADS * LANE)
    return wq, wqr


def unpad_heads_q_grad(dwq, dwqr):
    a = dwq.reshape(Q_RANK, HEADS, LANE)
    r = dwqr.reshape(Q_RANK, HEADS, LANE)[:, :, NOPE:QK_DIM]
    pe = a[:, :, NOPE:QK_DIM] + jnp.concatenate([r[:, :, ROPE // 2:], -r[:, :, :ROPE // 2]], axis=2)
    return jnp.concatenate([a[:, :, :NOPE], pe], axis=2).reshape(Q_RANK, HEADS * QK_DIM)


def pad_heads_kv(w_ukv):
    w = w_ukv.reshape(KV_RANK, HEADS, 2 * HEAD)
    z = jnp.zeros((KV_RANK, HEADS, HEAD), w.dtype)
    wkn = jnp.concatenate([w[:, :, :NOPE], z], axis=2).reshape(KV_RANK, HEADS * LANE)
    val = w[:, :, NOPE:]
    odd = (jnp.arange(HEADS) % 2 == 1)[None, :, None]
    wv = jnp.concatenate([jnp.where(odd, 0, val), jnp.where(odd, val, 0)], axis=2).reshape(KV_RANK, HEADS * LANE)
    return wkn, wv


def unpad_heads_kv_grad(dwkn, dwv):
    a = dwkn.reshape(KV_RANK, HEADS, LANE)[:, :, :NOPE]
    b = dwv.reshape(KV_RANK, HEADS, LANE)
    odd = (jnp.arange(HEADS) % 2 == 1)[None, :, None]
    val = jnp.where(odd, b[:, :, HEAD:], b[:, :, :HEAD])
    return jnp.concatenate([a, val], axis=2).reshape(KV_RANK, HEADS * 2 * HEAD)


def kernel(x, c, positions, w_ada, b_ada, w_in, q_norm_g, w_uq, kv_norm_g, w_ukv, mu_rwkv, w0, w_decay_up, a0, w_iclr_up, k_k, k_a, r_k, gn_g, gn_b, w_proj_a, w_proj_b, w_out, post_g, post_b, loss_target, m_w_ada, m_b_ada, m_w_in, m_q_norm_g, m_w_uq, m_kv_norm_g, m_w_ukv, m_mu_rwkv, m_w0, m_w_decay_up, m_a0, m_w_iclr_up, m_k_k, m_k_a, m_r_k, m_gn_g, m_gn_b, m_w_proj_a, m_w_proj_b, m_w_out, m_post_g, m_post_b, v_w_ada, v_b_ada, v_w_in, v_q_norm_g, v_w_uq, v_kv_norm_g, v_w_ukv, v_mu_rwkv, v_w0, v_w_decay_up, v_a0, v_w_iclr_up, v_k_k, v_k_a, v_r_k, v_gn_g, v_gn_b, v_w_proj_a, v_w_proj_b, v_w_out, v_post_g, v_post_b):
    weights = dict(w_ada=w_ada, b_ada=b_ada, w_in=w_in, q_norm_g=q_norm_g, w_uq=w_uq, kv_norm_g=kv_norm_g,
                   w_ukv=w_ukv, mu_rwkv=mu_rwkv, w0=w0, w_decay_up=w_decay_up, a0=a0, w_iclr_up=w_iclr_up,
                   k_k=k_k, k_a=k_a, r_k=r_k, gn_g=gn_g, gn_b=gn_b, w_proj_a=w_proj_a, w_proj_b=w_proj_b,
                   w_out=w_out, post_g=post_g, post_b=post_b)
    mom1 = dict(w_ada=m_w_ada, b_ada=m_b_ada, w_in=m_w_in, q_norm_g=m_q_norm_g, w_uq=m_w_uq, kv_norm_g=m_kv_norm_g,
                w_ukv=m_w_ukv, mu_rwkv=m_mu_rwkv, w0=m_w0, w_decay_up=m_w_decay_up, a0=m_a0, w_iclr_up=m_w_iclr_up,
                k_k=m_k_k, k_a=m_k_a, r_k=m_r_k, gn_g=m_gn_g, gn_b=m_gn_b, w_proj_a=m_w_proj_a, w_proj_b=m_w_proj_b,
                w_out=m_w_out, post_g=m_post_g, post_b=m_post_b)
    mom2 = dict(w_ada=v_w_ada, b_ada=v_b_ada, w_in=v_w_in, q_norm_g=v_q_norm_g, w_uq=v_w_uq, kv_norm_g=v_kv_norm_g,
                w_ukv=v_w_ukv, mu_rwkv=v_mu_rwkv, w0=v_w0, w_decay_up=v_w_decay_up, a0=v_a0, w_iclr_up=v_w_iclr_up,
                k_k=v_k_k, k_a=v_k_a, r_k=v_r_k, gn_g=v_gn_g, gn_b=v_gn_b, w_proj_a=v_w_proj_a, w_proj_b=v_w_proj_b,
                w_out=v_w_out, post_g=v_post_g, post_b=v_post_b)
    names = list(weights)
    n_rows = x.shape[1]
    me = 4 * lax.axis_index("x") + 2 * lax.axis_index("y") + lax.axis_index("c")
    xr = x[0]
    tgt = loss_target[0]
    row = lambda a: a.reshape(1, -1)

    gathered = gather_shards([weights[n][0].astype(BF16) for n, _, _ in SHARDED] + [c])
    c_all = gathered[-1].reshape(N_DEV, D_MODEL)
    full = {}
    for (n, r, cdim), part in zip(SHARDED, gathered):
        full[n] = part.reshape(N_DEV * r, cdim) if n == "w_out" else columns_from_shards(part, r, cdim)
    w_in_p = permute_w_in(full["w_in"])
    wq, wqr = pad_heads_q(full["w_uq"])
    wkn, wv = pad_heads_kv(full["w_ukv"])
    zl = jnp.zeros((LORA, WIDTH), BF16)
    w_dec = jnp.concatenate([full["w_decay_up"], zl], axis=0)
    w_iclr = jnp.concatenate([zl, full["w_iclr_up"]], axis=0)
    wpa, wpb, wout = full["w_proj_a"], full["w_proj_b"], full["w_out"]

    mod_all = ada_modulation(c_all, w_ada[0], b_ada.reshape(N_DEV, -1))
    mod = lax.dynamic_index_in_dim(mod_all, me, axis=1, keepdims=False).reshape(3, D_MODEL)

    (proj,) = row_call("fwd_in", fwd_in_tile, n_rows, [(xr, D_MODEL, 0)], [mod, w_in_p], [(P_WIDTH, F32)])
    pcol = lambda off_, w: (proj, w, off_ // w)

    inv_freq = ROPE_THETA ** (-jnp.arange(0, ROPE, 2, dtype=F32) / ROPE)
    ang = positions[0].astype(F32)[:, None] * inv_freq
    ones_n, zeros_n, zeros_p = jnp.ones((n_rows, NOPE), F32), jnp.zeros((n_rows, NOPE), F32), jnp.zeros((n_rows, LANE - QK_DIM), F32)
    cos_t = jnp.concatenate([ones_n, jnp.cos(ang), jnp.cos(ang), zeros_p], axis=1)
    sin_t = jnp.concatenate([zeros_n, jnp.sin(ang), jnp.sin(ang), zeros_p], axis=1)

    gq, gkv = q_norm_g, kv_norm_g
    mla_consts = [gq, gkv, wq, wqr, wkn, wv]
    q, k, v = row_call(
        "mla_prep", mla_prep_tile, n_rows,
        [pcol(P_QC, 256), pcol(P_KVC, 128), pcol(P_KR, 128), pcol(P_KRR, 128), (cos_t, LANE, 0), (sin_t, LANE, 0)],
        mla_consts, [(HEADS * LANE, BF16)] * 3)
    ya, lse = attention_forward(q, k, v)

    t_idx = jnp.arange(ROW_TILE)
    same_chunk = (t_idx[:, None] // CHUNK) == (t_idx[None, :] // CHUNK)
    same = same_chunk.astype(F32)
    tril = (same_chunk & (t_idx[:, None] >= t_idx[None, :])).astype(F32)
    l_idx = jnp.arange(LANE)
    bd = ((l_idx[:, None] // HEAD) == (l_idx[None, :] // HEAD)).astype(F32)
    mu = mu_rwkv
    mu_r, mu_k, mu_v, mu_l = mu[:, 0:512], mu[:, 512:1024], mu[:, 1024:1536], mu[:, 1536:1664]
    rk_row = row(r_k)
    rwkv_consts = [mu_r, mu_k, mu_v, mu_l, w0, a0, k_k, k_a, w_dec, w_iclr, tril, same, bd]
    rwkv_rows = [pcol(P_R, 512), pcol(P_K, 512), pcol(P_V, 512), pcol(P_LORA, 128)]
    rt, at, bt, kt, clf, uv, ur, k2 = row_call(
        "rwkv_prep", rwkv_prep_tile, n_rows, rwkv_rows, rwkv_consts, [(WIDTH, F32)] * 8, halo_in=rwkv_rows)
    y, m0s, state_maps, out_maps = wkv_forward(at, bt, kt, rt, uv, clf)

    tail = row_call(
        "tail", tail_tile, n_rows,
        [(xr, D_MODEL, 0), (tgt, D_MODEL, 0), pcol(P_MA, 1024), pcol(P_MB, 1024), pcol(P_GPA, 512), pcol(P_GPB, 512),
         (ya, WIDTH, 0), (y, WIDTH, 0), (ur, WIDTH, 0), (k2, WIDTH, 0), (uv, WIDTH, 0)],
        [mod, wpa, wpb, wout, gn_g, gn_b, rk_row, post_g, post_b, bd],
        [(D_MODEL, F32), (1024, F32), (1024, F32), (512, F32), (512, F32), (WIDTH, F32), (WIDTH, F32), (WIDTH, F32)],
        acc_out=[((1, LANE), F32), ((D_MODEL, D_MODEL), F32), ((WIDTH, D_MODEL), F32), ((WIDTH, D_MODEL), F32),
                 ((1, WIDTH), F32), ((1, WIDTH), F32), ((1, D_MODEL), F32), ((1, D_MODEL), F32), ((1, D_MODEL), F32)])
    (dz, dma, dmb, dgpa, dgpb, dya, dy, dyb,
     loss_row, g_wout, g_wpa, g_wpb, g_gn_g, g_gn_b, g_post_g, g_post_b, dgate) = tail

    dq, dk, dv = attention_backward(q, k, v, ya, dya, lse)
    dq_c, dkv_c, dkr, dkrr, g_wq, g_wqr, g_wkn, g_wv, g_gq, g_gkv = row_call(
        "mla_prep_bwd", mla_prep_bwd_tile, n_rows,
        [pcol(P_QC, 256), pcol(P_KVC, 128), (cos_t, LANE, 0), (sin_t, LANE, 0),
         (dq, HEADS * LANE, 0), (dk, HEADS * LANE, 0), (dv, HEADS * LANE, 0)],
        mla_consts, [(256, F32), (128, F32), (128, F32), (128, F32)],
        acc_out=[((Q_RANK, HEADS * LANE), F32)] * 2 + [((KV_RANK, HEADS * LANE), F32)] * 2
        + [((1, Q_RANK), F32), ((1, KV_RANK), F32)])

    dat, dbt, dkt, drt, dvv, dlw = wkv_backward(at, bt, kt, rt, uv, clf, m0s, state_maps, out_maps, dy)
    (dr0, dk0, dv0, dl0, g_mu_r, g_mu_k, g_mu_v, g_mu_l, g_w0, g_a0, g_k_k, g_k_a, g_r_k, g_wdec, g_wiclr) = row_call(
        "rwkv_prep_bwd", rwkv_prep_bwd_tile, n_rows,
        rwkv_rows + [(drt, WIDTH, 0), (dat, WIDTH, 0), (dbt, WIDTH, 0), (dkt, WIDTH, 0), (dvv, WIDTH, 0),
                     (dlw, WIDTH, 0), (dyb, WIDTH, 0)],
        rwkv_consts + [rk_row], [(512, F32), (512, F32), (512, F32), (128, F32)],
        acc_out=[((1, 512), F32)] * 3 + [((1, 128), F32)] + [((1, 512), F32)] * 5 + [((LANE, WIDTH), F32)] * 2,
        halo_in=rwkv_rows, carry=[512, 512, 512, 128], reverse=True)

    li = jnp.arange(LANE)
    src, dst = li[:, None], li[None, :]
    half = ROPE // 2
    unrot = (jnp.where((dst >= NOPE) & (dst < NOPE + half) & (src == dst + half), 1.0, 0.0)
             - jnp.where((dst >= NOPE + half) & (dst < QK_DIM) & (src == dst - half), 1.0, 0.0)).astype(BF16)
    dx, h_t, dproj_blocks, dshift, dscale = in_backward(
        xr, dz, [dma, dmb, dr0, dk0, dv0, dgpa, dgpb, dq_c, dkv_c, dkr, dkrr, dl0], mod, w_in_p, unrot)

    grads_full = {
        "w_uq": unpad_heads_q_grad(g_wq, g_wqr), "w_ukv": unpad_heads_kv_grad(g_wkn, g_wv),
        "w_decay_up": g_wdec[:LORA], "w_iclr_up": g_wiclr[LORA:],
        "w_proj_a": g_wpa, "w_proj_b": g_wpb, "w_out": g_wout}
    blocks = [(grads_full[n].reshape(N_DEV, r, cdim) if n == "w_out"
               else grads_full[n].reshape(r, N_DEV, cdim).transpose(1, 0, 2)).astype(BF16) for n, r, cdim in SHARDED[1:]]
    dmod = jnp.concatenate([dshift, dscale, dgate], axis=1)
    small = jnp.concatenate([dmod, g_gq, g_gkv, g_mu_r, g_mu_k, g_mu_v, g_mu_l, g_w0, g_a0, g_k_k, g_k_a, g_r_k,
                             g_gn_g, g_gn_b, g_post_g, g_post_b, loss_row], axis=1)
    my_x, my_y, my_c = lax.axis_index("x"), lax.axis_index("y"), lax.axis_index("c")
    chip_order = [4 * (my_x ^ fx) + 2 * (my_y ^ fy) for fx, fy in ((1, 1), (1, 0), (0, 1), (0, 0))]
    order = jnp.stack([ch + (1 - my_c) for ch in chip_order] + [ch + my_c for ch in chip_order]).astype(jnp.int32)
    *got_blocks, got_small = in_weight_grad_exchange(h_t, dproj_blocks, blocks, small, order)
    loss = jnp.sum(got_small[:, 0, SMALL_ELEMS])

    ada_cols = w_ada.shape[2]
    dmod_all = got_small[:, 0, :3 * D_MODEL]
    got_small = got_small[:, :, :SMALL_ELEMS]
    g_ada = ada_weight_grad(c_all, lax.dynamic_slice_in_dim(dmod_all, me * ada_cols, ada_cols, axis=1))

    def small_row(tree):
        return jnp.concatenate([tree[n].reshape(1, -1) for n, _ in SMALL], axis=1)

    outs = [dict() for _ in range(4)]
    res = adamw(g_ada[None], w_ada[0], m_w_ada[0], v_w_ada[0], "adamw_w_ada")
    for kind in range(4):
        outs[kind]["w_ada"] = res[kind][None]
    for (n, r, cdim), got in zip(SHARDED, got_blocks):
        res = adamw(got, weights[n][0], mom1[n][0], mom2[n][0], "adamw_" + n)
        for kind in range(4):
            outs[kind][n] = res[kind][None]
    res = adamw(got_small, small_row(weights), small_row(mom1), small_row(mom2), "adamw_small")
    for kind in range(4):
        off = 0
        for n, size in SMALL:
            outs[kind][n] = res[kind][:, off:off + size].reshape(weights[n].shape)
            off += size
    return (loss, dx[None], *[outs[0][n] for n in names], *[outs[1][n] for n in names],
            *[outs[2][n] for n in names], *[outs[3][n] for n in names])
```

```python
import functools
import math

import jax
import jax.numpy as jnp
from jax import lax
from jax.experimental import pallas as pl
from jax.experimental.pallas import tpu as pltpu

F32 = jnp.float32
BF16 = jnp.bfloat16
HIGHEST = lax.Precision.HIGHEST
MESH_IDS = pl.DeviceIdType.MESH

N_DEV = 8
D_MODEL = 1024
LN_EPS = 1e-5
RMS_EPS = 1e-6
GN_EPS = 64e-5
HEADS = 8
Q_RANK = 256
KV_RANK = 128
ROPE = 32
NOPE = 64
QK_DIM = NOPE + ROPE
WIDTH = 512
HEAD = 64
LORA = 64
CHUNK = 64
DEPTH = 1
ALPHA = (2.0 * DEPTH) ** 0.25
ROPE_THETA = 10000.0
ATTN_SCALE = QK_DIM ** -0.5
DECAY_SCALE = math.exp(-0.5)

ADAM_LR = 0.001
ADAM_B1 = 0.9
ADAM_B2 = 0.999
ADAM_EPS = 1e-08
ADAM_WD = 0.01
ADAM_STEP = 10

LANE = 128
PAIR = 2 * HEAD
ROW_TILE = 256
ATTN_FWD_TILES = (512, 1024)
ATTN_BWD_TILES = (512, 512)
LOG2_E = math.log2(math.e)
Q_PRESCALE = ATTN_SCALE * LOG2_E
WKV_CHUNKS_PER_STEP = 8
VMEM_LIMIT = 56 * 1024 * 1024

P_MA, P_MB, P_R, P_K, P_V, P_GPA, P_GPB, P_QC, P_KVC, P_KR, P_KRR, P_LORA = (
    0, 1024, 2048, 2560, 3072, 3584, 4096, 4608, 4864, 4992, 5120, 5248)
P_WIDTH = 5376
DW_BLOCK = 768

N_QC, N_KVC, N_KROPE, N_GPA, N_RWKV, N_GPB, N_MA, N_MB = 0, 256, 384, 416, 928, 2592, 3104, 4128
IN_WIDTH = 5152

SHARDED = (("w_in", 1024, 644), ("w_uq", 256, 96), ("w_ukv", 128, 128), ("w_decay_up", 64, 64),
           ("w_iclr_up", 64, 64), ("w_proj_a", 512, 128), ("w_proj_b", 512, 128), ("w_out", 128, 1024))
SHARD_ELEMS = sum(r * c for _, r, c in SHARDED)
SHARD_ROWS = SHARD_ELEMS // LANE
GATHER_ROWS = SHARD_ROWS + 2 * D_MODEL // LANE
SMALL = (("b_ada", 3072), ("q_norm_g", 256), ("kv_norm_g", 128), ("mu_rwkv", 1664), ("w0", 512), ("a0", 512),
         ("k_k", 512), ("k_a", 512), ("r_k", 512), ("gn_g", 512), ("gn_b", 512), ("post_g", 1024), ("post_b", 1024))
SMALL_ELEMS = sum(n for _, n in SMALL)
SMALL_ROWS = SMALL_ELEMS // LANE


def mm(a, b):
    return jnp.dot(a.astype(BF16), b.astype(BF16), preferred_element_type=F32)


def mm_nt(a, b):
    return lax.dot_general(a.astype(BF16), b.astype(BF16), (((1,), (1,)), ((), ())), preferred_element_type=F32)


def mm_tn(a, b):
    return lax.dot_general(a.astype(BF16), b.astype(BF16), (((0,), (0,)), ((), ())), preferred_element_type=F32)


def hdot(a, b):
    return jnp.dot(a, b, precision=HIGHEST, preferred_element_type=F32)


def hdot_nt(a, b):
    return lax.dot_general(a, b, (((1,), (1,)), ((), ())), precision=HIGHEST, preferred_element_type=F32)


def hdot_tn(a, b):
    return lax.dot_general(a, b, (((0,), (0,)), ((), ())), precision=HIGHEST, preferred_element_type=F32)


def sigmoid(x):
    return 1.0 / (1.0 + jnp.exp(-x))


def colsum(x):
    return jnp.sum(x, axis=0, keepdims=True)


def rowmean(x):
    return jnp.mean(x, axis=-1, keepdims=True)


def layer_norm_stats(x):
    xc = x - rowmean(x)
    rstd = lax.rsqrt(rowmean(xc * xc) + LN_EPS)
    return xc * rstd, rstd


def layer_norm_bwd(dy, xhat, rstd):
    return rstd * (dy - rowmean(dy) - xhat * rowmean(dy * xhat))


def bf16_pieces(x, n):
    pieces = []
    for _ in range(n):
        p = x.astype(BF16)
        pieces.append(p)
        x = x - p.astype(F32)
    return pieces


def ones_dot(ones, x, n_pieces):
    ones = ones.astype(BF16)
    return sum(jnp.dot(ones, p, preferred_element_type=F32) for p in bf16_pieces(x, n_pieces))


def head_sum(x, bd):
    bd = bd.astype(BF16)
    out = []
    for p in range(x.shape[1] // LANE):
        hi, lo = bf16_pieces(x[:, p * LANE:(p + 1) * LANE], 2)
        out.append(jnp.dot(hi, bd, preferred_element_type=F32) + jnp.dot(lo, bd, preferred_element_type=F32))
    return jnp.concatenate(out, axis=1)


def tile_lanes(t, n):
    return jnp.concatenate([t] * n, axis=1)


def row_iota(shape):
    return lax.broadcasted_iota(jnp.int32, shape, 0)


def lane_iota(shape):
    return lax.broadcasted_iota(jnp.int32, shape, 1)


def shift_rows_down(x, row0):
    rolled = pltpu.roll(x, 1, axis=0)
    return jnp.where(row_iota(x.shape) == 0, row0, rolled)


def shift_rows_up(x, row_last):
    rolled = pltpu.roll(x, x.shape[0] - 1, axis=0)
    return jnp.where(row_iota(x.shape) == x.shape[0] - 1, row_last, rolled)


def row_call(name, fn, n_rows, row_in, const_in, row_out, acc_out=(), halo_in=(), carry=(), reverse=False):
    ts = ROW_TILE
    n_tiles = n_rows // ts
    n_in = len(row_in) + len(halo_in) + len(const_in)
    n_ro, n_ao = len(row_out), len(acc_out)

    def tile_of(g):
        return (n_tiles - 1 - g) if reverse else g

    def body(*refs):
        ins = refs[:n_in]
        ro = refs[n_in:n_in + n_ro]
        ao = refs[n_in + n_ro:n_in + n_ro + n_ao]
        cr = refs[n_in + n_ro + n_ao:]
        g = pl.program_id(0)
        step0 = g == 0
        tile0 = tile_of(g) == 0
        for r in cr:
            @pl.when(step0)
            def _(r=r):
                r[...] = jnp.zeros_like(r)
        vals = [r[...] for r in ins]
        outs = fn(step0, tile0, *vals, *[c[0:1, :] for c in cr])
        for r, v in zip(ro, outs[:n_ro]):
            r[...] = v.astype(r.dtype)
        for r, v in zip(ao, outs[n_ro:n_ro + n_ao]):
            @pl.when(step0)
            def _(r=r, v=v):
                r[...] = v.astype(r.dtype)

            @pl.when(jnp.logical_not(step0))
            def _(r=r, v=v):
                r[...] += v.astype(r.dtype)
        for r, v in zip(cr, outs[n_ro + n_ao:]):
            r[0:1, :] = v

    in_specs = [pl.BlockSpec((ts, w), functools.partial(lambda g, cb: (tile_of(g), cb), cb=cb)) for _, w, cb in row_in]
    in_specs += [pl.BlockSpec((8, w), functools.partial(
        lambda g, cb: (jnp.maximum(tile_of(g) * (ts // 8) - 1, 0), cb), cb=cb)) for _, w, cb in halo_in]
    in_specs += [pl.BlockSpec(memory_space=pltpu.VMEM) for _ in const_in]
    out_specs = [pl.BlockSpec((ts, w), lambda g: (tile_of(g), 0)) for w, _ in row_out]
    out_specs += [pl.BlockSpec(s, lambda g: (0, 0)) for s, _ in acc_out]
    out_shape = [jax.ShapeDtypeStruct((n_rows, w), d) for w, d in row_out]
    out_shape += [jax.ShapeDtypeStruct(s, d) for s, d in acc_out]
    return pl.pallas_call(
        body, name=name, grid=(n_tiles,), in_specs=in_specs, out_specs=out_specs, out_shape=out_shape,
        scratch_shapes=[pltpu.VMEM((8, w), F32) for w in carry],
        compiler_params=pltpu.CompilerParams(dimension_semantics=("arbitrary",), vmem_limit_bytes=VMEM_LIMIT),
    )(*[a for a, _, _ in row_in], *[a for a, _, _ in halo_in], *const_in)


def my_position():
    return lax.axis_index("x"), lax.axis_index("y"), lax.axis_index("c")


def flip(pos, k):
    x, y, c = pos
    dx, dy, dc = (k >> 2) & 1, (k >> 1) & 1, k & 1
    return (1 - x if dx else x, 1 - y if dy else y, 1 - c if dc else c)


def flat_index(pos):
    return 4 * pos[0] + 2 * pos[1] + pos[2]


def gather_shards(shards):
    n = len(shards)

    def body(*refs):
        x_refs, out_refs = refs[:n], refs[n:2 * n]
        send_sems, recv_sems, local_sems = refs[2 * n:]
        x, y, c = my_position()
        me, sibling = (x, y, c), (x, y, 1 - c)
        chips = [(1 - x, y), (x, 1 - y), (1 - x, 1 - y)]

        def copy(a, k, block, to, from_input=False):
            slot = out_refs[a].at[flat_index(block)]
            return pltpu.make_async_remote_copy(
                src_ref=x_refs[a] if from_input else slot, dst_ref=slot,
                send_sem=send_sems.at[7 * a + k], recv_sem=recv_sems.at[7 * a + k],
                device_id=to, device_id_type=MESH_IDS)

        mine = [pltpu.make_async_copy(x_refs[a], out_refs[a].at[flat_index(me)], local_sems.at[a]) for a in range(n)]
        for cp in mine:
            cp.start()
        first = []
        for a in range(n):
            first.append(copy(a, 0, me, sibling, from_input=True))
            first += [copy(a, 1 + j, me, (*chip, c), from_input=True) for j, chip in enumerate(chips)]
        for cp in first:
            cp.start()
        passed = []
        for j, chip in enumerate(chips):
            for a in range(n):
                copy(a, 1 + j, (*chip, c), me).wait_recv()
                cp = copy(a, 4 + j, (*chip, c), sibling)
                cp.start()
                passed.append(cp)
        for a in range(n):
            copy(a, 0, sibling, me).wait_recv()
            for j, chip in enumerate(chips):
                copy(a, 4 + j, (*chip, 1 - c), me).wait_recv()
        for cp in first + passed:
            cp.wait_send()
        for cp in mine:
            cp.wait()

    return pl.pallas_call(
        body, name="gather_shards",
        out_shape=[jax.ShapeDtypeStruct((N_DEV,) + s.shape, s.dtype) for s in shards],
        in_specs=[pl.BlockSpec(memory_space=pl.ANY)] * n, out_specs=[pl.BlockSpec(memory_space=pl.ANY)] * n,
        scratch_shapes=[pltpu.SemaphoreType.DMA((7 * n,)), pltpu.SemaphoreType.DMA((7 * n,)),
                        pltpu.SemaphoreType.DMA((n,))],
    )(*shards)


def ada_modulation(c_all, w_ada_loc, b_ada_blocks):
    cols = w_ada_loc.shape[1]

    def body(c_ref, w_ref, b_ref, out_ref, send_sems, recv_sems):
        me = my_position()
        mi = flat_index(me)
        cv = c_ref[...]
        res = hdot(cv * sigmoid(cv), w_ref[...]) + b_ref[pl.ds(mi, 1), :]
        out_ref[mi] = res
        sends = []
        for k in range(1, N_DEV):
            cp = pltpu.make_async_remote_copy(
                src_ref=out_ref.at[mi], dst_ref=out_ref.at[mi], send_sem=send_sems.at[k - 1],
                recv_sem=recv_sems.at[k - 1], device_id=flip(me, k), device_id_type=MESH_IDS)
            cp.start()
            sends.append(cp)
        for k in range(1, N_DEV):
            pi = flat_index(flip(me, k))
            pltpu.make_async_remote_copy(
                src_ref=out_ref.at[pi], dst_ref=out_ref.at[pi], send_sem=send_sems.at[k - 1],
                recv_sem=recv_sems.at[k - 1], device_id=flip(me, k), device_id_type=MESH_IDS).wait_recv()
        for cp in sends:
            cp.wait_send()

    return pl.pallas_call(
        body, name="ada_modulation",
        out_shape=jax.ShapeDtypeStruct((N_DEV, N_DEV, cols), F32),
        in_specs=[pl.BlockSpec(memory_space=pltpu.VMEM)] * 3, out_specs=pl.BlockSpec(memory_space=pltpu.VMEM),
        scratch_shapes=[pltpu.SemaphoreType.DMA((7,)), pltpu.SemaphoreType.DMA((7,))],
    )(c_all, w_ada_loc, b_ada_blocks)


def fwd_in_tile(step0, tile0, x, mod, w_in_p):
    xhat, _ = layer_norm_stats(x)
    h = xhat * (1.0 + mod[1:2]) + mod[0:1]
    return (mm(h, w_in_p),)


def rms_norm_fwd(x, g):
    r = lax.rsqrt(rowmean(x * x) + RMS_EPS)
    xh = x * r
    return xh * g, xh, r


def key_rope_mask(shape):
    return (lane_iota(shape) >= NOPE).astype(F32)


def mla_prep_tile(step0, tile0, q_c, kv_c, kr, krr, cos, sin, gq, gkv, wq, wqr, wkn, wv):
    qn, _, _ = rms_norm_fwd(q_c, gq)
    kvn, _, _ = rms_norm_fwd(kv_c, gkv)
    q = (mm(qn, wq) * tile_lanes(cos, HEADS) + mm(qn, wqr) * tile_lanes(sin, HEADS)) * Q_PRESCALE
    kpe = kr * (cos * key_rope_mask(cos.shape)) + krr * sin
    k = mm(kvn, wkn) + tile_lanes(kpe, HEADS)
    v = mm(kvn, wv)
    return q, k, v


def rwkv_prep_core(tile0, r0, k0, v0, l0, hr, hk, hv, hl, mu_r, mu_k, mu_v, mu_l, w0, a0, k_k, k_a,
                   w_dec, w_iclr, tril, same, bd):
    def shifted(x, halo, mu):
        row0 = jnp.where(tile0, 0.0, halo[7:8, :])
        prev = shift_rows_down(x, row0)
        return x + (prev - x) * mu, prev

    ur, pr = shifted(r0, hr, mu_r)
    uk, pk = shifted(k0, hk, mu_k)
    uv, pv = shifted(v0, hv, mu_v)
    ul, plo = shifted(l0, hl, mu_l)
    th = jnp.tanh(ul)
    sg = sigmoid(w0 + mm(th, w_dec))
    lw = -DECAY_SCALE * sg
    a_ic = sigmoid(a0 + mm(ul, w_iclr))
    kkraw = uk * k_k
    nrm_raw = jnp.sqrt(head_sum(kkraw * kkraw, bd))
    nrm = jnp.maximum(nrm_raw, 1e-12)
    kk = kkraw / nrm
    k2 = uk * (1.0 + (a_ic - 1.0) * k_a)
    lc = ones_dot(tril, lw, 3)
    lcl = ones_dot(same, lw, 3)
    return dict(ur=ur, uk=uk, uv=uv, ul=ul, pr=pr, pk=pk, pv=pv, pl=plo, th=th, sg=sg, lw=lw, a_ic=a_ic,
                kkraw=kkraw, nrm_raw=nrm_raw, nrm=nrm, kk=kk, k2=k2, lc=lc, lcl=lcl)


def rwkv_prep_tile(step0, tile0, r0, k0, v0, l0, hr, hk, hv, hl, *consts):
    f = rwkv_prep_core(tile0, r0, k0, v0, l0, hr, hk, hv, hl, *consts)
    lc, lw = f["lc"], f["lw"]
    e_neg = jnp.exp(-lc)
    rt = f["ur"] * jnp.exp(lc)
    at = -f["kk"] * jnp.exp(lc - lw)
    bt = f["kk"] * f["a_ic"] * e_neg
    kt = f["k2"] * e_neg
    return rt, at, bt, kt, jnp.exp(f["lcl"]), f["uv"], f["ur"], f["k2"]


def wkv_masks():
    lane = lane_iota((1, PAIR))
    m_lo = (lane < HEAD).astype(F32)
    ri = row_iota((CHUNK, CHUNK))
    ci = lane_iota((CHUNK, CHUNK))
    r2 = row_iota((PAIR, PAIR))
    c2 = lane_iota((PAIR, PAIR))
    bd = ((r2 < HEAD) == (c2 < HEAD)).astype(F32)
    eye2 = (r2 == c2).astype(F32)
    return (m_lo, 1.0 - m_lo), ri > ci, ri >= ci, (ri == ci).astype(F32), bd, eye2


def wkv_chunks_pre(chunks, masks):
    ms, strict, incl, eye, bd, eye2 = masks
    items = [(c, m) for c in range(len(chunks)) for m in ms]
    at, bt, kt, rt, v, cl = (list(t) for t in zip(*chunks))
    atm = [at[c] * m for c, m in items]
    rtm = [rt[c] * m for c, m in items]
    aab = [jnp.where(strict, mm_nt(x, bt[c]), 0.0) for x, (c, _) in zip(atm, items)]
    aak = [jnp.where(strict, mm_nt(x, kt[c]), 0.0) for x, (c, _) in zip(atm, items)]
    prb = [jnp.where(incl, mm_nt(x, bt[c]), 0.0) for x, (c, _) in zip(rtm, items)]
    prk = [jnp.where(incl, mm_nt(x, kt[c]), 0.0) for x, (c, _) in zip(rtm, items)]
    tinv = [eye + a for a in aab]
    power = aab
    for _ in range(5):
        power = [mm(p, p) for p in power]
        tinv = [t + mm(t, p) for t, p in zip(tinv, power)]

    def by_chunk(parts):
        return [parts[2 * c] + parts[2 * c + 1] for c in range(len(chunks))]

    w = by_chunk([mm(a, v[c] * m) for a, (c, m) in zip(aak, items)])
    ah = by_chunk([mm(t, x) for t, x in zip(tinv, atm)])
    wh = by_chunk([mm(t, w[c] * m) for t, (c, m) in zip(tinv, items)])
    rh = [r + d for r, d in zip(rt, by_chunk([mm(p, ah[c] * m) for p, (c, m) in zip(prb, items)]))]
    yh = by_chunk([mm(p, wh[c] * m) + mm(q, v[c] * m) for p, q, (c, m) in zip(prb, prk, items)])
    bc = [b * c_ for b, c_ in zip(bt, cl)]
    kc = [k * c_ for k, c_ in zip(kt, cl)]
    g = [eye2 * c_ + bd * mm_tn(b, a) for c_, b, a in zip(cl, bc, ah)]
    h = [bd * (mm_tn(b, w_) + mm_tn(k, v_)) for b, w_, k, v_ in zip(bc, wh, kc, v)]
    side = lambda parts: [jnp.concatenate([parts[2 * c], parts[2 * c + 1]], axis=1).astype(BF16)
                          for c in range(len(chunks))]
    saved = (side(tinv), side(aak), side(prb), side(prk), [a.astype(BF16) for a in ah], wh)
    return g, h, rh, yh, saved


def wkv_chunks_grad(chunks, saved, m0, dy, dm1, masks):
    ms, strict, incl, eye, bd, eye2 = masks
    n = len(chunks)
    at, bt, kt, rt, v, cl = (list(t) for t in zip(*chunks))
    items = [(c, m) for c in range(n) for m in ms]
    atm = [at[c] * m for c, m in items]
    rtm = [rt[c] * m for c, m in items]
    halves = lambda pairs: [x for pr in pairs for x in (pr[:, :CHUNK], pr[:, CHUNK:])]
    tinv, aak, prb, prk = (halves(s) for s in zip(*[(a, b, c_, d) for a, b, c_, d, _, _ in saved]))
    ah = [s[4] for s in saved]
    wh = [s[5] for s in saved]
    bc = [b * c_ for b, c_ in zip(bt, cl)]
    kc = [k * c_ for k, c_ in zip(kt, cl)]

    def by_chunk(parts):
        return [parts[2 * c] + parts[2 * c + 1] for c in range(n)]

    u = [mm(a, m) + w for a, m, w in zip(ah, m0, wh)]
    dm1 = [d * bd for d in dm1]
    dym = [dy[c] * m for c, m in items]
    du = [mm(b, d) + e for b, d, e in zip(bc, dm1, by_chunk([mm_tn(p, x) for p, x in zip(prb, dym)]))]
    dv = [mm(k, d) + e for k, d, e in zip(kc, dm1, by_chunk([mm_tn(p, x) for p, x in zip(prk, dym)]))]
    dz = by_chunk([mm_tn(t, du[c] * m) for t, (c, m) in zip(tinv, items)])
    dzm = [dz[c] * m for c, m in items]
    dv = [a + b for a, b in zip(dv, by_chunk([mm_tn(a_, x) for a_, x in zip(aak, dzm)]))]
    drt = [mm_nt(d, m) for d, m in zip(dy, m0)]
    dat = [mm_nt(d, m) for d, m in zip(dz, m0)]
    udm = [mm_nt(x, d) for x, d in zip(u, dm1)]
    vdm = [mm_nt(x, d) for x, d in zip(v, dm1)]
    daab = [jnp.where(strict, mm_nt(x, u[c]), 0.0) for x, (c, _) in zip(dzm, items)]
    daak = [jnp.where(strict, mm_nt(x, v[c]), 0.0) for x, (c, _) in zip(dzm, items)]
    dprb = [jnp.where(incl, mm_nt(x, u[c]), 0.0) for x, (c, _) in zip(dym, items)]
    dprk = [jnp.where(incl, mm_nt(x, v[c]), 0.0) for x, (c, _) in zip(dym, items)]
    drt2 = by_chunk([(mm(p, bt[c]) + mm(q, kt[c])) * m for p, q, (c, m) in zip(dprb, dprk, items)])
    dat2 = by_chunk([(mm(p, bt[c]) + mm(q, kt[c])) * m for p, q, (c, m) in zip(daab, daak, items)])
    dbt2 = by_chunk([mm_tn(p, r) + mm_tn(a_, x) for p, r, a_, x in zip(dprb, rtm, daab, atm)])
    dkt2 = by_chunk([mm_tn(p, r) + mm_tn(a_, x) for p, r, a_, x in zip(dprk, rtm, daak, atm)])
    ones = jnp.ones((8, PAIR), F32)
    upper = (lane_iota((CHUNK, CHUNK)) >= row_iota((CHUNK, CHUNK))).astype(F32)
    out = []
    for c in range(n):
        drt_c = drt[c] + drt2[c]
        dat_c = dat[c] + dat2[c]
        dbt_c = udm[c] * cl[c] + dbt2[c]
        dkt_c = vdm[c] * cl[c] + dkt2[c]
        dlcl = hdot_nt(ones, dm1[c] * m0[c])[0:1, :] * cl[c] + colsum(bc[c] * udm[c] + kc[c] * vdm[c])
        g = drt_c * rt[c] - dbt_c * bt[c] - dkt_c * kt[c] + dat_c * at[c]
        dlw = hdot(upper, g) - dat_c * at[c] + dlcl
        out.append((dat_c, dbt_c, dkt_c, drt_c, dv[c], dlw))
    return out


def wkv_forward(at, bt, kt, rt, v, clf):
    n_rows = at.shape[0]
    cps = WKV_CHUNKS_PER_STEP
    rb = cps * CHUNK
    n_steps = n_rows // rb

    def body(a_ref, b_ref, k_ref, r_ref, v_ref, c_ref, y_ref, m0_ref, g_ref, rh_ref, *rest):
        saved_refs, m_scr = rest[:6], rest[6]

        @pl.when(pl.program_id(1) == 0)
        def _():
            m_scr[...] = jnp.zeros_like(m_scr)

        masks = wkv_masks()
        chunks = []
        for cc in range(cps):
            sl = slice(cc * CHUNK, (cc + 1) * CHUNK)
            chunks.append((a_ref[sl, :], b_ref[sl, :], k_ref[sl, :], r_ref[sl, :], v_ref[sl, :],
                           c_ref[cc * CHUNK:cc * CHUNK + 1, :]))
        gs, hs, rhs, yhs, saved = wkv_chunks_pre(chunks, masks)
        for ref, per_chunk in zip(saved_refs, saved):
            for cc, val in enumerate(per_chunk):
                ref[cc * CHUNK:(cc + 1) * CHUNK, :] = val
        m = m_scr[...]
        for cc, (g, h, rh, yh) in enumerate(zip(gs, hs, rhs, yhs)):
            sl = slice(cc * CHUNK, (cc + 1) * CHUNK)
            m0_ref[0, cc] = m
            g_ref[0, cc] = g
            rh_ref[sl, :] = rh
            y_ref[sl, :] = hdot(rh, m) + yh
            m = hdot(g, m) + h
        m_scr[...] = m

    blk = pl.BlockSpec((rb, PAIR), lambda p, s: (s, p))
    state_blk = pl.BlockSpec((1, cps, PAIR, PAIR), lambda p, s: (p, s, 0, 0))
    state_shape = jax.ShapeDtypeStruct((WIDTH // PAIR, n_rows // CHUNK, PAIR, PAIR), F32)
    rows_f32 = jax.ShapeDtypeStruct((n_rows, WIDTH), F32)
    rows_bf16 = jax.ShapeDtypeStruct((n_rows, WIDTH), BF16)
    return pl.pallas_call(
        body, name="wkv_forward", grid=(WIDTH // PAIR, n_steps),
        in_specs=[blk] * 6,
        out_specs=[blk, state_blk, state_blk, blk] + [blk] * 6,
        out_shape=[rows_f32, state_shape, state_shape, rows_f32] + [rows_bf16] * 5 + [rows_f32],
        scratch_shapes=[pltpu.VMEM((PAIR, PAIR), F32)],
        compiler_params=pltpu.CompilerParams(dimension_semantics=("arbitrary", "arbitrary"),
                                             vmem_limit_bytes=VMEM_LIMIT),
    )(at, bt, kt, rt, v, clf)


def wkv_backward(at, bt, kt, rt, v, clf, m0s, gs, rh, saved, dy):
    n_rows = at.shape[0]
    cps = WKV_CHUNKS_PER_STEP
    rb = cps * CHUNK
    n_steps = n_rows // rb

    def body(a_ref, b_ref, k_ref, r_ref, v_ref, c_ref, m0_ref, g_ref, rh_ref, *rest):
        saved_refs, dy_ref = rest[:6], rest[6]
        da_ref, db_ref, dk_ref, dr_ref, dv_ref, dlw_ref, dm_scr = rest[7:]

        @pl.when(pl.program_id(1) == 0)
        def _():
            dm_scr[...] = jnp.zeros_like(dm_scr)

        masks = wkv_masks()
        bd = masks[4]
        dm = dm_scr[...]
        dm1 = [None] * cps
        for cc in reversed(range(cps)):
            sl = slice(cc * CHUNK, (cc + 1) * CHUNK)
            dm1[cc] = dm
            dm = bd * (hdot_tn(g_ref[0, cc], dm) + hdot_tn(rh_ref[sl, :], dy_ref[sl, :]))
        dm_scr[...] = dm
        chunks, kept, m0, dys = [], [], [], []
        for cc in range(cps):
            sl = slice(cc * CHUNK, (cc + 1) * CHUNK)
            chunks.append((a_ref[sl, :], b_ref[sl, :], k_ref[sl, :], r_ref[sl, :], v_ref[sl, :],
                           c_ref[cc * CHUNK:cc * CHUNK + 1, :]))
            kept.append(tuple(ref[sl, :] for ref in saved_refs))
            m0.append(m0_ref[0, cc])
            dys.append(dy_ref[sl, :])
        grads = wkv_chunks_grad(chunks, kept, m0, dys, dm1, masks)
        for cc, (dat, dbt, dkt, drt, dv, dlw) in enumerate(grads):
            sl = slice(cc * CHUNK, (cc + 1) * CHUNK)
            da_ref[sl, :] = dat
            db_ref[sl, :] = dbt
            dk_ref[sl, :] = dkt
            dr_ref[sl, :] = drt
            dv_ref[sl, :] = dv
            dlw_ref[sl, :] = dlw

    blk = pl.BlockSpec((rb, PAIR), lambda p, s: (n_steps - 1 - s, p))
    state_blk = pl.BlockSpec((1, cps, PAIR, PAIR), lambda p, s: (p, n_steps - 1 - s, 0, 0))
    return pl.pallas_call(
        body, name="wkv_backward", grid=(WIDTH // PAIR, n_steps),
        in_specs=[blk] * 6 + [state_blk, state_blk, blk] + [blk] * 6 + [blk],
        out_specs=[blk] * 6,
        out_shape=[jax.ShapeDtypeStruct((n_rows, WIDTH), F32)] * 6,
        scratch_shapes=[pltpu.VMEM((PAIR, PAIR), F32)],
        compiler_params=pltpu.CompilerParams(dimension_semantics=("arbitrary", "arbitrary"),
                                             vmem_limit_bytes=VMEM_LIMIT),
    )(at, bt, kt, rt, v, clf, m0s, gs, rh, *saved, dy)


def visible(q_row0, k_row0, shape):
    qc = (q_row0 + row_iota(shape)) // CHUNK
    kc = (k_row0 + lane_iota(shape)) // CHUNK
    return kc <= qc


def attention_forward(q, k, v):
    n_rows = q.shape[0]
    tq, tk = ATTN_FWD_TILES
    n_q = n_rows // tq
    n_masked = max(1, tq // tk)

    def body(q_ref, k_ref, v_ref, o_ref, lse_ref):
        i = pl.program_id(1)
        lane = lane_iota((tq, LANE))
        heads = [slice(0, LANE), slice(LANE, 2 * LANE)]
        qs = [q_ref[:, cols] for cols in heads]

        def step(j, carry, masked):
            rows = pl.ds(pl.multiple_of(j * tk, tk), tk)
            ss = [mm_nt(qh, k_ref[rows, cols]) for qh, cols in zip(qs, heads)]
            if masked:
                vis = visible(i * tq, j * tk, ss[0].shape)
                ss = [jnp.where(vis, s, -jnp.inf) for s in ss]
            ps, stats = [], []
            for s, (m, l, _) in zip(ss, carry):
                m_new = jnp.maximum(m, jnp.max(s, axis=-1, keepdims=True))
                p = jnp.exp2(s - m_new)
                alpha = jnp.exp2(m - m_new)
                ps.append(p)
                stats.append((m_new, alpha, alpha * l + jnp.sum(p, axis=-1, keepdims=True)))
            pvs = [mm(p, v_ref[rows, cols]) for p, cols in zip(ps, heads)]
            return tuple((m_new, l, alpha * acc + pv)
                         for (m_new, alpha, l), (_, _, acc), pv in zip(stats, carry, pvs))

        carry = tuple((jnp.full((tq, 1), -jnp.inf, F32), jnp.zeros((tq, 1), F32), jnp.zeros((tq, LANE), F32))
                      for _ in heads)
        n_full = (i * tq) // tk
        carry = lax.fori_loop(0, n_full, functools.partial(step, masked=False), carry)
        for extra in range(n_masked):
            carry = step(n_full + extra, carry, masked=True)
        (m0, l0, acc0), (m1, l1, acc1) = carry
        o_ref[...] = acc0 / l0 + acc1 / l1
        lse_ref[...] = jnp.where(lane >= HEAD, m1 + jnp.log2(l1), m0 + jnp.log2(l0))

    return pl.pallas_call(
        body, name="attention_forward", grid=(HEADS // 2, n_q),
        in_specs=[pl.BlockSpec((tq, 2 * LANE), lambda p, i: (i, p)),
                  pl.BlockSpec((n_rows, 2 * LANE), lambda p, i: (0, p)),
                  pl.BlockSpec((n_rows, 2 * LANE), lambda p, i: (0, p))],
        out_specs=[pl.BlockSpec((tq, LANE), lambda p, i: (i, p))] * 2,
        out_shape=[jax.ShapeDtypeStruct((n_rows, WIDTH), F32)] * 2,
        compiler_params=pltpu.CompilerParams(dimension_semantics=("arbitrary", "arbitrary"),
                                             vmem_limit_bytes=VMEM_LIMIT),
    )(q, k, v)


def attention_backward(q, k, v, o, do, lse):
    n_rows = q.shape[0]
    tq, tk = ATTN_BWD_TILES
    n_q = n_rows // tq
    n_masked = max(1, tk // tq)

    def body(q_ref, k_ref, v_ref, o_ref, do_ref, lse_ref, dq_ref, dk_ref, dv_ref):
        j = pl.program_id(1)

        @pl.when(j == 0)
        def _():
            dq_ref[...] = jnp.zeros_like(dq_ref)

        lane = lane_iota((tq, LANE))
        heads = [slice(0, LANE), slice(LANE, 2 * LANE)]
        ks = [k_ref[:, cols] for cols in heads]
        vs = [v_ref[:, cols] for cols in heads]
        head_lanes = [(lane < HEAD).astype(F32), (lane >= HEAD).astype(F32)]

        def step(i, carry, masked):
            rows = pl.ds(pl.multiple_of(i * tq, tq), tq)
            qs = [q_ref[rows, cols] for cols in heads]
            dout = do_ref[rows, :]
            dout_o = dout * o_ref[rows, :]
            lse_t = lse_ref[rows, :]
            ss = [mm_nt(qh, kh) for qh, kh in zip(qs, ks)]
            dps = [mm_nt(dout, vh) for vh in vs]
            ps, dss = [], []
            for hh in range(2):
                delta = jnp.sum(dout_o * head_lanes[hh], axis=-1, keepdims=True)
                lse_h = jnp.sum(jnp.where(lane == hh * HEAD, lse_t, 0.0), axis=-1, keepdims=True)
                p = jnp.exp2(ss[hh] - lse_h)
                if masked:
                    p = jnp.where(visible(i * tq, j * tk, p.shape), p, 0.0)
                ps.append(p)
                dss.append(p * (dps[hh] - delta))
            dvs = [mm_tn(p, dout) for p in ps]
            dqs = [mm(ds, kh) for ds, kh in zip(dss, ks)]
            dks = [mm_tn(ds, qh) for ds, qh in zip(dss, qs)]
            for cols, dq in zip(heads, dqs):
                dq_ref[rows, cols] += dq * ATTN_SCALE
            return tuple((dk + a, dv + b) for (dk, dv), a, b in zip(carry, dks, dvs))

        carry = tuple((jnp.zeros((tk, LANE), F32), jnp.zeros((tk, LANE), F32)) for _ in heads)
        i_first = (j * tk) // tq
        for extra in range(n_masked):
            carry = step(i_first + extra, carry, masked=True)
        carry = lax.fori_loop(i_first + n_masked, n_q, functools.partial(step, masked=False), carry)
        for cols, (dk, dv) in zip(heads, carry):
            dk_ref[:, cols] = dk * (1.0 / LOG2_E)
            dv_ref[:, cols] = dv

    full = lambda w: pl.BlockSpec((n_rows, w), lambda p, j: (0, p))
    blk = pl.BlockSpec((tk, 2 * LANE), lambda p, j: (j, p))
    return pl.pallas_call(
        body, name="attention_backward", grid=(HEADS // 2, n_rows // tk),
        in_specs=[full(2 * LANE), blk, blk, full(LANE), full(LANE), full(LANE)],
        out_specs=[full(2 * LANE), blk, blk],
        out_shape=[jax.ShapeDtypeStruct((n_rows, HEADS * LANE), F32)] * 3,
        compiler_params=pltpu.CompilerParams(dimension_semantics=("arbitrary", "arbitrary"),
                                             vmem_limit_bytes=VMEM_LIMIT),
    )(q, k, v, o, do, lse)


def tail_tile(step0, tile0, x, tgt, ma, mb, gpa, gpb, ya, y, ur, k2, uv,
              mod, wpa, wpb, wout, gn_g, gn_b, r_k, post_g, post_b, bd):
    gate = mod[2:3]
    inv = 1.0 / HEAD
    yc = y - head_sum(y, bd) * inv
    rs = lax.rsqrt(head_sum(yc * yc, bd) * inv + GN_EPS)
    yn = yc * rs
    yb = yn * gn_g + gn_b + head_sum(ur * k2 * r_k, bd) * uv
    sga, sgb = sigmoid(gpa), sigmoid(gpb)
    sila, silb = gpa * sga, gpb * sgb
    ga, gb = ya * sila, yb * silb
    pa, pb = mm(ga, wpa), mm(gb, wpb)
    sa, sb = sigmoid(ma), sigmoid(mb)
    merged = sa * pa + sb * pb
    sub = mm(merged, wout)
    z = ALPHA * x + (1.0 + gate) * sub
    zhat, rstd = layer_norm_stats(z)
    err = zhat * post_g + post_b - tgt
    loss = 0.5 * jnp.sum(rowmean(err * err), axis=0, keepdims=True) + jnp.zeros((1, LANE), F32)
    dout = err * (1.0 / D_MODEL)
    dpost_g = colsum(dout * zhat)
    dpost_b = colsum(dout)
    dz = layer_norm_bwd(dout * post_g, zhat, rstd)
    dgate = colsum(dz * sub)
    dsub = dz * (1.0 + gate)
    dwout = mm_tn(merged, dsub)
    dmerged = mm_nt(dsub, wout)
    dpa, dpb = dmerged * sa, dmerged * sb
    dma = dmerged * pa * sa * (1.0 - sa)
    dmb = dmerged * pb * sb * (1.0 - sb)
    dwpa = mm_tn(ga, dpa)
    dwpb = mm_tn(gb, dpb)
    dga = mm_nt(dpa, wpa)
    dgb = mm_nt(dpb, wpb)
    dya = dga * sila
    dgpa = dga * ya * (sga * (1.0 + gpa * (1.0 - sga)))
    dyb = dgb * silb
    dgpb = dgb * yb * (sgb * (1.0 + gpb * (1.0 - sgb)))
    dgn_g = colsum(dyb * yn)
    dgn_b = colsum(dyb)
    dyn = dyb * gn_g
    dy = rs * (dyn - head_sum(dyn, bd) * inv - yn * head_sum(dyn * yn, bd) * inv)
    return (dz, dma, dmb, dgpa, dgpb, dya, dy, dyb,
            loss, dwout, dwpa, dwpb, dgn_g, dgn_b, dpost_g, dpost_b, dgate)


def mla_prep_bwd_tile(step0, tile0, q_c, kv_c, cos, sin, dq, dk, dv, gq, gkv, wq, wqr, wkn, wv):
    qn, qh, rq = rms_norm_fwd(q_c, gq)
    kvn, kvh, rkv = rms_norm_fwd(kv_c, gkv)
    dqc = dq * tile_lanes(cos, HEADS)
    dqs = dq * tile_lanes(sin, HEADS)
    dqn = mm_nt(dqc, wq) + mm_nt(dqs, wqr)
    dkvn = mm_nt(dk, wkn) + mm_nt(dv, wv)
    dkpe = dk[:, 0:LANE]
    for h in range(1, HEADS):
        dkpe = dkpe + dk[:, h * LANE:(h + 1) * LANE]
    dkr = dkpe * (cos * key_rope_mask(cos.shape))
    dkrr = dkpe * sin

    def rms_bwd(dyv, xh, r, g):
        dyg = dyv * g
        return r * (dyg - xh * rowmean(dyg * xh)), colsum(dyv * xh)

    dq_c, dgq = rms_bwd(dqn, qh, rq, gq)
    dkv_c, dgkv = rms_bwd(dkvn, kvh, rkv, gkv)
    return (dq_c, dkv_c, dkr, dkrr,
            mm_tn(qn, dqc), mm_tn(qn, dqs), mm_tn(kvn, dk), mm_tn(kvn, dv), dgq, dgkv)


def rwkv_prep_bwd_tile(step0, tile0, r0, k0, v0, l0, drt, dat, dbt, dkt, dvv, dlw, dyb, hr, hk, hv, hl,
                       mu_r, mu_k, mu_v, mu_l, w0, a0, k_k, k_a, w_dec, w_iclr, tril, same, bd, r_k,
                       cr, ck, cv, cl_):
    f = rwkv_prep_core(tile0, r0, k0, v0, l0, hr, hk, hv, hl, mu_r, mu_k, mu_v, mu_l, w0, a0, k_k, k_a,
                       w_dec, w_iclr, tril, same, bd)
    ur, uk, uv, ul, kk, k2, a_ic, sg, th = (f[n] for n in ("ur", "uk", "uv", "ul", "kk", "k2", "a_ic", "sg", "th"))
    lc, lw = f["lc"], f["lw"]
    e_neg = jnp.exp(-lc)
    dur = drt * jnp.exp(lc)
    da = dat * jnp.exp(lc - lw)
    db = dbt * e_neg
    dk2 = dkt * e_neg
    s = head_sum(ur * k2 * r_k, bd)
    duv = dvv + dyb * s
    ds = head_sum(dyb * uv, bd)
    dur = dur + ds * k2 * r_k
    dk2 = dk2 + ds * ur * r_k
    dr_k = colsum(ds * ur * k2)
    dkk = db * a_ic - da
    da_ic = db * kk + dk2 * uk * k_a
    duk = dk2 * (1.0 + (a_ic - 1.0) * k_a)
    dk_a = colsum(dk2 * uk * (a_ic - 1.0))
    dkkraw = jnp.where(f["nrm_raw"] > 1e-12, (dkk - kk * head_sum(dkk * kk, bd)) / f["nrm"], dkk * 1e12)
    duk = duk + dkkraw * k_k
    dk_k = colsum(dkkraw * uk)
    dai = da_ic * a_ic * (1.0 - a_ic)
    dd = dlw * (-DECAY_SCALE) * sg * (1.0 - sg)
    dul = mm_nt(dai, w_iclr) + mm_nt(dd, w_dec) * (1.0 - th * th)

    def unshift(du, x, prev, mu, carry_row):
        nxt = shift_rows_up(du, carry_row)
        return du * (1.0 - mu) + nxt * mu, colsum(du * (prev - x)), du[0:1, :]

    dr0, dmu_r, ncr = unshift(dur, r0, f["pr"], mu_r, cr)
    dk0, dmu_k, nck = unshift(duk, k0, f["pk"], mu_k, ck)
    dv0, dmu_v, ncv = unshift(duv, v0, f["pv"], mu_v, cv)
    dl0, dmu_l, ncl = unshift(dul, l0, f["pl"], mu_l, cl_)
    return (dr0, dk0, dv0, dl0,
            dmu_r, dmu_k, dmu_v, dmu_l, colsum(dd), colsum(dai), dk_k, dk_a, dr_k, mm_tn(th, dd), mm_tn(ul, dai),
            ncr, nck, ncv, ncl)


def in_backward(x, dz, pieces, mod, w_in_p, unrot):
    n_rows = x.shape[0]
    ts = ROW_TILE
    n_p = len(pieces)
    shard_cols = IN_WIDTH // N_DEV

    def body(*refs):
        x_ref, dz_ref = refs[:2]
        p_refs = refs[2:2 + n_p]
        mod_ref, w_ref, unrot_ref = refs[2 + n_p:5 + n_p]
        dx_ref, ht_ref, blocks_ref, dshift_ref, dscale_ref = refs[5 + n_p:]
        step0 = pl.program_id(0) == 0
        dma, dmb, dr0, dk0, dv0, dgpa, dgpb, dq_c, dkv_c, dkr, dkrr, dl0 = (r[...] for r in p_refs)
        dproj = jnp.concatenate([dma, dmb, dr0, dk0, dv0, dgpa, dgpb, dq_c, dkv_c, dkr, dkrr, dl0], axis=1)
        dh = mm_nt(dproj, w_ref[...])
        xhat, rstd = layer_norm_stats(x_ref[...])
        scale1 = 1.0 + mod_ref[1:2, :]
        dx_ref[...] = layer_norm_bwd(dh * scale1, xhat, rstd) + ALPHA * dz_ref[...]
        ht_ref[...] = jnp.transpose(xhat * scale1 + mod_ref[0:1, :]).astype(BF16)
        dkrope = (dkr + mm(dkrr, unrot_ref[...]))[:, NOPE:QK_DIM]
        natural = jnp.concatenate([dq_c, dkv_c, dkrope, dgpa, dr0, dk0, dv0, dl0, dgpb, dma, dmb], axis=1)
        for j in range(N_DEV):
            blocks_ref[j] = natural[:, j * shard_cols:(j + 1) * shard_cols].astype(BF16)
        for ref, val in ((dshift_ref, colsum(dh)), (dscale_ref, colsum(dh * xhat))):
            @pl.when(step0)
            def _(ref=ref, val=val):
                ref[...] = val

            @pl.when(jnp.logical_not(step0))
            def _(ref=ref, val=val):
                ref[...] += val

    row = lambda w: pl.BlockSpec((ts, w), lambda i: (i, 0))
    const = pl.BlockSpec(memory_space=pltpu.VMEM)
    vec = pl.BlockSpec((1, D_MODEL), lambda i: (0, 0))
    return pl.pallas_call(
        body, name="in_backward", grid=(n_rows // ts,),
        in_specs=[row(D_MODEL), row(D_MODEL)] + [row(p.shape[1]) for p in pieces] + [const] * 3,
        out_specs=[row(D_MODEL), pl.BlockSpec((D_MODEL, ts), lambda i: (0, i)),
                   pl.BlockSpec((N_DEV, ts, shard_cols), lambda i: (0, i, 0)), vec, vec],
        out_shape=[jax.ShapeDtypeStruct((n_rows, D_MODEL), F32), jax.ShapeDtypeStruct((D_MODEL, n_rows), BF16),
                   jax.ShapeDtypeStruct((N_DEV, n_rows, shard_cols), BF16),
                   jax.ShapeDtypeStruct((1, D_MODEL), F32), jax.ShapeDtypeStruct((1, D_MODEL), F32)],
        compiler_params=pltpu.CompilerParams(dimension_semantics=("arbitrary",), vmem_limit_bytes=VMEM_LIMIT),
    )(x, dz, *pieces, mod, w_in_p, unrot)


def in_weight_grad_exchange(h_t, dp_blocks, others, small, order):
    n = len(others)
    n_rows = h_t.shape[1]
    ts = 2 * ROW_TILE
    n_i = n_rows // ts
    shard_cols = dp_blocks.shape[2]
    n_chips = N_DEV // 2
    last = N_DEV - 1

    def body(order_ref, h_ref, dp_ref, *rest):
        g_refs, s_ref = rest[:n], rest[n]
        rwin_ref, rg_refs, rs_ref = rest[n + 1], rest[n + 2:2 * n + 2], rest[2 * n + 2]
        (acc, sendbuf, sib_buf, sib_send, sib_recv, win_send, win_recv,
         o_send, o_recv, local_sems) = rest[2 * n + 3:]
        b, i = pl.program_id(0), pl.program_id(1)
        me = my_position()
        mi = flat_index(me)
        sibling = (me[0], me[1], 1 - me[2])

        def other_copies(k, src_index, dst_index):
            peer = flip(me, k)
            out = [pltpu.make_async_remote_copy(
                src_ref=g_refs[a].at[src_index], dst_ref=rg_refs[a].at[dst_index],
                send_sem=o_send.at[(n + 1) * (k - 1) + a], recv_sem=o_recv.at[(n + 1) * (k - 1) + a],
                device_id=peer, device_id_type=MESH_IDS) for a in range(n)]
            out.append(pltpu.make_async_remote_copy(
                src_ref=s_ref, dst_ref=rs_ref.at[dst_index],
                send_sem=o_send.at[(n + 1) * (k - 1) + n], recv_sem=o_recv.at[(n + 1) * (k - 1) + n],
                device_id=peer, device_id_type=MESH_IDS))
            return out

        def local_copies():
            out = [pltpu.make_async_copy(g_refs[a].at[mi], rg_refs[a].at[mi], local_sems.at[a]) for a in range(n)]
            out.append(pltpu.make_async_copy(s_ref, rs_ref.at[mi], local_sems.at[n]))
            return out

        def to_sibling(t):
            return pltpu.make_async_remote_copy(
                src_ref=sendbuf.at[t], dst_ref=sib_buf.at[t], send_sem=sib_send.at[t], recv_sem=sib_recv.at[t],
                device_id=sibling, device_id_type=MESH_IDS)

        def to_owner(t):
            flip_x = (t < 2) * 1
            flip_y = 1 - (t & 1)
            owner = (me[0] ^ flip_x, me[1] ^ flip_y, me[2])
            return pltpu.make_async_remote_copy(
                src_ref=sendbuf.at[n_chips + t], dst_ref=rwin_ref.at[t], send_sem=win_send.at[t],
                recv_sem=win_recv.at[t], device_id=owner, device_id_type=MESH_IDS)

        own_block = pltpu.make_async_copy(sendbuf.at[last], rwin_ref.at[n_chips - 1], local_sems.at[n + 1])

        @pl.when(jnp.logical_and(b == 0, i == 0))
        def _():
            for cp in local_copies():
                cp.start()
            for k in range(1, N_DEV):
                for cp in other_copies(k, flat_index(flip(me, k)), mi):
                    cp.start()

        contrib = jnp.dot(h_ref[...], dp_ref[...], preferred_element_type=F32)

        @pl.when(i == 0)
        def _():
            acc[...] = contrib

        @pl.when(i > 0)
        def _():
            acc[...] += contrib

        @pl.when(jnp.logical_and(i == n_i - 1, b < n_chips))
        def _():
            sendbuf[b] = acc[...].astype(BF16)
            to_sibling(b).start()

        @pl.when(jnp.logical_and(i == n_i - 1, b >= n_chips))
        def _():
            t = b - n_chips
            to_sibling(t).wait_recv()
            sendbuf[b] = (acc[...] + sib_buf[t].astype(F32)).astype(BF16)

            @pl.when(b < last)
            def _():
                to_owner(t).start()

            @pl.when(b == last)
            def _():
                own_block.start()

        @pl.when(jnp.logical_and(b == last, i == n_i - 1))
        def _():
            for t in range(n_chips - 1):
                to_owner(t).wait_recv()
            for k in range(1, N_DEV):
                pi = flat_index(flip(me, k))
                for cp in other_copies(k, pi, pi):
                    cp.wait_recv()
            for t in range(n_chips):
                to_sibling(t).wait_send()
            for t in range(n_chips - 1):
                to_owner(t).wait_send()
            for k in range(1, N_DEV):
                for cp in other_copies(k, flat_index(flip(me, k)), mi):
                    cp.wait_send()
            for cp in local_copies():
                cp.wait()
            own_block.wait()

    hbm = pl.BlockSpec(memory_space=pl.ANY)
    n_sem = 7 * (n + 1)
    grid_spec = pltpu.PrefetchScalarGridSpec(
        num_scalar_prefetch=1, grid=(N_DEV, n_i),
        in_specs=[pl.BlockSpec((D_MODEL, ts), lambda b, i, order: (0, i)),
                  pl.BlockSpec((None, ts, shard_cols), lambda b, i, order: (order[b], i, 0))] + [hbm] * (n + 1),
        out_specs=[hbm] * (n + 2),
        scratch_shapes=[pltpu.VMEM((D_MODEL, shard_cols), F32), pltpu.VMEM((N_DEV, D_MODEL, shard_cols), BF16),
                        pltpu.VMEM((n_chips, D_MODEL, shard_cols), BF16),
                        pltpu.SemaphoreType.DMA((n_chips,)), pltpu.SemaphoreType.DMA((n_chips,)),
                        pltpu.SemaphoreType.DMA((n_chips - 1,)), pltpu.SemaphoreType.DMA((n_chips - 1,)),
                        pltpu.SemaphoreType.DMA((n_sem,)), pltpu.SemaphoreType.DMA((n_sem,)),
                        pltpu.SemaphoreType.DMA((n + 2,))])
    return pl.pallas_call(
        body, name="in_weight_grad_exchange", grid_spec=grid_spec,
        out_shape=[jax.ShapeDtypeStruct((n_chips, D_MODEL, shard_cols), BF16)]
        + [jax.ShapeDtypeStruct(o.shape, o.dtype) for o in others]
        + [jax.ShapeDtypeStruct((N_DEV,) + small.shape, small.dtype)],
        compiler_params=pltpu.CompilerParams(dimension_semantics=("arbitrary", "arbitrary"),
                                             vmem_limit_bytes=VMEM_LIMIT),
    )(order, h_t, dp_blocks, *others, small)


def ada_weight_grad(c_all, dmod_cols):
    def body(c_ref, d_ref, o_ref):
        cv = c_ref[...]
        o_ref[...] = hdot_tn(cv * sigmoid(cv), d_ref[...])

    return pl.pallas_call(
        body, name="ada_weight_grad",
        out_shape=jax.ShapeDtypeStruct((c_all.shape[1], dmod_cols.shape[1]), F32),
    )(c_all, dmod_cols)


def adamw(parts, w, m, v, name):
    k, rows, cols = parts.shape
    rb = 128 if rows % 128 == 0 else rows

    def body(p_ref, w_ref, m_ref, v_ref, g_ref, d_ref, nm_ref, nv_ref):
        g = p_ref[0].astype(F32)
        for i in range(1, k):
            g = g + p_ref[i].astype(F32)
        nm = ADAM_B1 * m_ref[...] + (1.0 - ADAM_B1) * g
        nv = ADAM_B2 * v_ref[...] + (1.0 - ADAM_B2) * (g * g)
        m_hat = nm / (1.0 - ADAM_B1 ** ADAM_STEP)
        v_hat = nv / (1.0 - ADAM_B2 ** ADAM_STEP)
        g_ref[...] = g
        d_ref[...] = -ADAM_LR * (m_hat / (jnp.sqrt(v_hat) + ADAM_EPS) + ADAM_WD * w_ref[...])
        nm_ref[...] = nm
        nv_ref[...] = nv

    blk = pl.BlockSpec((rb, cols), lambda i: (i, 0))
    return pl.pallas_call(
        body, name=name, grid=(rows // rb,),
        in_specs=[pl.BlockSpec((k, rb, cols), lambda i: (0, i, 0)), blk, blk, blk],
        out_specs=[blk] * 4, out_shape=[jax.ShapeDtypeStruct((rows, cols), F32)] * 4,
        compiler_params=pltpu.CompilerParams(dimension_semantics=("arbitrary",), vmem_limit_bytes=VMEM_LIMIT),
    )(parts, w, m, v)


def rot_cols(w):
    return jnp.concatenate([-w[:, ROPE // 2:], w[:, :ROPE // 2]], axis=1)


def unrot_cols(dw):
    return jnp.concatenate([dw[:, ROPE // 2:], -dw[:, :ROPE // 2]], axis=1)


def columns_from_shards(g, rows, cols):
    return g.reshape(N_DEV, rows, cols).transpose(1, 0, 2).reshape(rows, N_DEV * cols)


def shards_from_columns(w, rows, cols):
    return w.reshape(rows, N_DEV, cols).transpose(1, 0, 2).reshape(N_DEV, rows * cols)


def permute_w_in(w):
    z = lambda n: jnp.zeros((D_MODEL, n), w.dtype)
    krope = w[:, N_KROPE:N_KROPE + ROPE]
    rw = N_RWKV
    return jnp.concatenate([
        w[:, N_MA:N_MA + 1024], w[:, N_MB:N_MB + 1024],
        w[:, rw:rw + 512], w[:, rw + 512:rw + 1024], w[:, rw + 1024:rw + 1536],
        w[:, N_GPA:N_GPA + 512], w[:, N_GPB:N_GPB + 512],
        w[:, N_QC:N_QC + 256], w[:, N_KVC:N_KVC + 128],
        z(NOPE), krope, z(LANE - QK_DIM), z(NOPE), rot_cols(krope), z(LANE - QK_DIM),
        w[:, rw + 1536:rw + 1664]], axis=1)


def unpermute_w_in_grad(d):
    rw = P_R
    krope = d[:, P_KR + NOPE:P_KR + QK_DIM] + unrot_cols(d[:, P_KRR + NOPE:P_KRR + QK_DIM])
    return jnp.concatenate([
        d[:, P_QC:P_QC + 256], d[:, P_KVC:P_KVC + 128], krope, d[:, P_GPA:P_GPA + 512],
        d[:, rw:rw + 1536], d[:, P_LORA:P_LORA + 128], d[:, P_GPB:P_GPB + 512],
        d[:, P_MA:P_MA + 1024], d[:, P_MB:P_MB + 1024]], axis=1)


def pad_heads_q(w_uq):
    w = w_uq.reshape(Q_RANK, HEADS, QK_DIM)
    zpad = jnp.zeros((Q_RANK, HEADS, LANE - QK_DIM), w.dtype)
    wq = jnp.concatenate([w, zpad], axis=2).reshape(Q_RANK, HEADS * LANE)
    pe = w[:, :, NOPE:]
    rot = jnp.concatenate([-pe[:, :, ROPE // 2:], pe[:, :, :ROPE // 2]], axis=2)
    wqr = jnp.concatenate([jnp.zeros((Q_RANK, HEADS, NOPE), w.dtype), rot, zpad], axis=2).reshape(Q_RANK, HEADS * LANE)
    return wq, wqr


def unpad_heads_q_grad(dwq, dwqr):
    a = dwq.reshape(Q_RANK, HEADS, LANE)
    r = dwqr.reshape(Q_RANK, HEADS, LANE)[:, :, NOPE:QK_DIM]
    pe = a[:, :, NOPE:QK_DIM] + jnp.concatenate([r[:, :, ROPE // 2:], -r[:, :, :ROPE // 2]], axis=2)
    return jnp.concatenate([a[:, :, :NOPE], pe], axis=2).reshape(Q_RANK, HEADS * QK_DIM)


def pad_heads_kv(w_ukv):
    w = w_ukv.reshape(KV_RANK, HEADS, 2 * HEAD)
    z = jnp.zeros((KV_RANK, HEADS, HEAD), w.dtype)
    wkn = jnp.concatenate([w[:, :, :NOPE], z], axis=2).reshape(KV_RANK, HEADS * LANE)
    val = w[:, :, NOPE:]
    odd = (jnp.arange(HEADS) % 2 == 1)[None, :, None]
    wv = jnp.concatenate([jnp.where(odd, 0, val), jnp.where(odd, val, 0)], axis=2).reshape(KV_RANK, HEADS * LANE)
    return wkn, wv


def unpad_heads_kv_grad(dwkn, dwv):
    a = dwkn.reshape(KV_RANK, HEADS, LANE)[:, :, :NOPE]
    b = dwv.reshape(KV_RANK, HEADS, LANE)
    odd = (jnp.arange(HEADS) % 2 == 1)[None, :, None]
    val = jnp.where(odd, b[:, :, HEAD:], b[:, :, :HEAD])
    return jnp.concatenate([a, val], axis=2).reshape(KV_RANK, HEADS * 2 * HEAD)


def kernel(x, c, positions, w_ada, b_ada, w_in, q_norm_g, w_uq, kv_norm_g, w_ukv, mu_rwkv, w0, w_decay_up, a0, w_iclr_up, k_k, k_a, r_k, gn_g, gn_b, w_proj_a, w_proj_b, w_out, post_g, post_b, loss_target, m_w_ada, m_b_ada, m_w_in, m_q_norm_g, m_w_uq, m_kv_norm_g, m_w_ukv, m_mu_rwkv, m_w0, m_w_decay_up, m_a0, m_w_iclr_up, m_k_k, m_k_a, m_r_k, m_gn_g, m_gn_b, m_w_proj_a, m_w_proj_b, m_w_out, m_post_g, m_post_b, v_w_ada, v_b_ada, v_w_in, v_q_norm_g, v_w_uq, v_kv_norm_g, v_w_ukv, v_mu_rwkv, v_w0, v_w_decay_up, v_a0, v_w_iclr_up, v_k_k, v_k_a, v_r_k, v_gn_g, v_gn_b, v_w_proj_a, v_w_proj_b, v_w_out, v_post_g, v_post_b):
    weights = dict(w_ada=w_ada, b_ada=b_ada, w_in=w_in, q_norm_g=q_norm_g, w_uq=w_uq, kv_norm_g=kv_norm_g,
                   w_ukv=w_ukv, mu_rwkv=mu_rwkv, w0=w0, w_decay_up=w_decay_up, a0=a0, w_iclr_up=w_iclr_up,
                   k_k=k_k, k_a=k_a, r_k=r_k, gn_g=gn_g, gn_b=gn_b, w_proj_a=w_proj_a, w_proj_b=w_proj_b,
                   w_out=w_out, post_g=post_g, post_b=post_b)
    mom1 = dict(w_ada=m_w_ada, b_ada=m_b_ada, w_in=m_w_in, q_norm_g=m_q_norm_g, w_uq=m_w_uq, kv_norm_g=m_kv_norm_g,
                w_ukv=m_w_ukv, mu_rwkv=m_mu_rwkv, w0=m_w0, w_decay_up=m_w_decay_up, a0=m_a0, w_iclr_up=m_w_iclr_up,
                k_k=m_k_k, k_a=m_k_a, r_k=m_r_k, gn_g=m_gn_g, gn_b=m_gn_b, w_proj_a=m_w_proj_a, w_proj_b=m_w_proj_b,
                w_out=m_w_out, post_g=m_post_g, post_b=m_post_b)
    mom2 = dict(w_ada=v_w_ada, b_ada=v_b_ada, w_in=v_w_in, q_norm_g=v_q_norm_g, w_uq=v_w_uq, kv_norm_g=v_kv_norm_g,
                w_ukv=v_w_ukv, mu_rwkv=v_mu_rwkv, w0=v_w0, w_decay_up=v_w_decay_up, a0=v_a0, w_iclr_up=v_w_iclr_up,
                k_k=v_k_k, k_a=v_k_a, r_k=v_r_k, gn_g=v_gn_g, gn_b=v_gn_b, w_proj_a=v_w_proj_a, w_proj_b=v_w_proj_b,
                w_out=v_w_out, post_g=v_post_g, post_b=v_post_b)
    names = list(weights)
    n_rows = x.shape[1]
    me = 4 * lax.axis_index("x") + 2 * lax.axis_index("y") + lax.axis_index("c")
    xr = x[0]
    tgt = loss_target[0]
    row = lambda a: a.reshape(1, -1)

    gathered = gather_shards([weights[n][0].astype(BF16) for n, _, _ in SHARDED] + [c])
    c_all = gathered[-1].reshape(N_DEV, D_MODEL)
    full = {}
    for (n, r, cdim), part in zip(SHARDED, gathered):
        full[n] = part.reshape(N_DEV * r, cdim) if n == "w_out" else columns_from_shards(part, r, cdim)
    w_in_p = permute_w_in(full["w_in"])
    wq, wqr = pad_heads_q(full["w_uq"])
    wkn, wv = pad_heads_kv(full["w_ukv"])
    zl = jnp.zeros((LORA, WIDTH), BF16)
    w_dec = jnp.concatenate([full["w_decay_up"], zl], axis=0)
    w_iclr = jnp.concatenate([zl, full["w_iclr_up"]], axis=0)
    wpa, wpb, wout = full["w_proj_a"], full["w_proj_b"], full["w_out"]

    mod_all = ada_modulation(c_all, w_ada[0], b_ada.reshape(N_DEV, -1))
    mod = lax.dynamic_index_in_dim(mod_all, me, axis=1, keepdims=False).reshape(3, D_MODEL)

    (proj,) = row_call("fwd_in", fwd_in_tile, n_rows, [(xr, D_MODEL, 0)], [mod, w_in_p], [(P_WIDTH, F32)])
    pcol = lambda off_, w: (proj, w, off_ // w)

    inv_freq = ROPE_THETA ** (-jnp.arange(0, ROPE, 2, dtype=F32) / ROPE)
    ang = positions[0].astype(F32)[:, None] * inv_freq
    ones_n, zeros_n, zeros_p = jnp.ones((n_rows, NOPE), F32), jnp.zeros((n_rows, NOPE), F32), jnp.zeros((n_rows, LANE - QK_DIM), F32)
    cos_t = jnp.concatenate([ones_n, jnp.cos(ang), jnp.cos(ang), zeros_p], axis=1)
    sin_t = jnp.concatenate([zeros_n, jnp.sin(ang), jnp.sin(ang), zeros_p], axis=1)

    gq, gkv = q_norm_g, kv_norm_g
    mla_consts = [gq, gkv, wq, wqr, wkn, wv]
    q, k, v = row_call(
        "mla_prep", mla_prep_tile, n_rows,
        [pcol(P_QC, 256), pcol(P_KVC, 128), pcol(P_KR, 128), pcol(P_KRR, 128), (cos_t, LANE, 0), (sin_t, LANE, 0)],
        mla_consts, [(HEADS * LANE, BF16)] * 3)
    ya, lse = attention_forward(q, k, v)

    t_idx = jnp.arange(ROW_TILE)
    same_chunk = (t_idx[:, None] // CHUNK) == (t_idx[None, :] // CHUNK)
    same = same_chunk.astype(F32)
    tril = (same_chunk & (t_idx[:, None] >= t_idx[None, :])).astype(F32)
    l_idx = jnp.arange(LANE)
    bd = ((l_idx[:, None] // HEAD) == (l_idx[None, :] // HEAD)).astype(F32)
    mu = mu_rwkv
    mu_r, mu_k, mu_v, mu_l = mu[:, 0:512], mu[:, 512:1024], mu[:, 1024:1536], mu[:, 1536:1664]
    rk_row = row(r_k)
    rwkv_consts = [mu_r, mu_k, mu_v, mu_l, w0, a0, k_k, k_a, w_dec, w_iclr, tril, same, bd]
    rwkv_rows = [pcol(P_R, 512), pcol(P_K, 512), pcol(P_V, 512), pcol(P_LORA, 128)]
    rt, at, bt, kt, clf, uv, ur, k2 = row_call(
        "rwkv_prep", rwkv_prep_tile, n_rows, rwkv_rows, rwkv_consts, [(WIDTH, F32)] * 8, halo_in=rwkv_rows)
    y, m0s, state_maps, out_maps, *wkv_saved = wkv_forward(at, bt, kt, rt, uv, clf)

    tail = row_call(
        "tail", tail_tile, n_rows,
        [(xr, D_MODEL, 0), (tgt, D_MODEL, 0), pcol(P_MA, 1024), pcol(P_MB, 1024), pcol(P_GPA, 512), pcol(P_GPB, 512),
         (ya, WIDTH, 0), (y, WIDTH, 0), (ur, WIDTH, 0), (k2, WIDTH, 0), (uv, WIDTH, 0)],
        [mod, wpa, wpb, wout, gn_g, gn_b, rk_row, post_g, post_b, bd],
        [(D_MODEL, F32), (1024, F32), (1024, F32), (512, F32), (512, F32), (WIDTH, F32), (WIDTH, F32), (WIDTH, F32)],
        acc_out=[((1, LANE), F32), ((D_MODEL, D_MODEL), F32), ((WIDTH, D_MODEL), F32), ((WIDTH, D_MODEL), F32),
                 ((1, WIDTH), F32), ((1, WIDTH), F32), ((1, D_MODEL), F32), ((1, D_MODEL), F32), ((1, D_MODEL), F32)])
    (dz, dma, dmb, dgpa, dgpb, dya, dy, dyb,
     loss_row, g_wout, g_wpa, g_wpb, g_gn_g, g_gn_b, g_post_g, g_post_b, dgate) = tail

    dq, dk, dv = attention_backward(q, k, v, ya, dya, lse)
    dq_c, dkv_c, dkr, dkrr, g_wq, g_wqr, g_wkn, g_wv, g_gq, g_gkv = row_call(
        "mla_prep_bwd", mla_prep_bwd_tile, n_rows,
        [pcol(P_QC, 256), pcol(P_KVC, 128), (cos_t, LANE, 0), (sin_t, LANE, 0),
         (dq, HEADS * LANE, 0), (dk, HEADS * LANE, 0), (dv, HEADS * LANE, 0)],
        mla_consts, [(256, F32), (128, F32), (128, F32), (128, F32)],
        acc_out=[((Q_RANK, HEADS * LANE), F32)] * 2 + [((KV_RANK, HEADS * LANE), F32)] * 2
        + [((1, Q_RANK), F32), ((1, KV_RANK), F32)])

    dat, dbt, dkt, drt, dvv, dlw = wkv_backward(at, bt, kt, rt, uv, clf, m0s, state_maps, out_maps, wkv_saved, dy)
    (dr0, dk0, dv0, dl0, g_mu_r, g_mu_k, g_mu_v, g_mu_l, g_w0, g_a0, g_k_k, g_k_a, g_r_k, g_wdec, g_wiclr) = row_call(
        "rwkv_prep_bwd", rwkv_prep_bwd_tile, n_rows,
        rwkv_rows + [(drt, WIDTH, 0), (dat, WIDTH, 0), (dbt, WIDTH, 0), (dkt, WIDTH, 0), (dvv, WIDTH, 0),
                     (dlw, WIDTH, 0), (dyb, WIDTH, 0)],
        rwkv_consts + [rk_row], [(512, F32), (512, F32), (512, F32), (128, F32)],
        acc_out=[((1, 512), F32)] * 3 + [((1, 128), F32)] + [((1, 512), F32)] * 5 + [((LANE, WIDTH), F32)] * 2,
        halo_in=rwkv_rows, carry=[512, 512, 512, 128], reverse=True)

    li = jnp.arange(LANE)
    src, dst = li[:, None], li[None, :]
    half = ROPE // 2
    unrot = (jnp.where((dst >= NOPE) & (dst < NOPE + half) & (src == dst + half), 1.0, 0.0)
             - jnp.where((dst >= NOPE + half) & (dst < QK_DIM) & (src == dst - half), 1.0, 0.0)).astype(BF16)
    dx, h_t, dproj_blocks, dshift, dscale = in_backward(
        xr, dz, [dma, dmb, dr0, dk0, dv0, dgpa, dgpb, dq_c, dkv_c, dkr, dkrr, dl0], mod, w_in_p, unrot)

    grads_full = {
        "w_uq": unpad_heads_q_grad(g_wq, g_wqr), "w_ukv": unpad_heads_kv_grad(g_wkn, g_wv),
        "w_decay_up": g_wdec[:LORA], "w_iclr_up": g_wiclr[LORA:],
        "w_proj_a": g_wpa, "w_proj_b": g_wpb, "w_out": g_wout}
    blocks = [(grads_full[n].reshape(N_DEV, r, cdim) if n == "w_out"
               else grads_full[n].reshape(r, N_DEV, cdim).transpose(1, 0, 2)).astype(BF16) for n, r, cdim in SHARDED[1:]]
    dmod = jnp.concatenate([dshift, dscale, dgate], axis=1)
    small = jnp.concatenate([dmod, g_gq, g_gkv, g_mu_r, g_mu_k, g_mu_v, g_mu_l, g_w0, g_a0, g_k_k, g_k_a, g_r_k,
                             g_gn_g, g_gn_b, g_post_g, g_post_b, loss_row], axis=1)
    my_x, my_y, my_c = lax.axis_index("x"), lax.axis_index("y"), lax.axis_index("c")
    chip_order = [4 * (my_x ^ fx) + 2 * (my_y ^ fy) for fx, fy in ((1, 1), (1, 0), (0, 1), (0, 0))]
    order = jnp.stack([ch + (1 - my_c) for ch in chip_order] + [ch + my_c for ch in chip_order]).astype(jnp.int32)
    *got_blocks, got_small = in_weight_grad_exchange(h_t, dproj_blocks, blocks, small, order)
    loss = jnp.sum(got_small[:, 0, SMALL_ELEMS])

    ada_cols = w_ada.shape[2]
    dmod_all = got_small[:, 0, :3 * D_MODEL]
    got_small = got_small[:, :, :SMALL_ELEMS]
    g_ada = ada_weight_grad(c_all, lax.dynamic_slice_in_dim(dmod_all, me * ada_cols, ada_cols, axis=1))

    def small_row(tree):
        return jnp.concatenate([tree[n].reshape(1, -1) for n, _ in SMALL], axis=1)

    outs = [dict() for _ in range(4)]
    res = adamw(g_ada[None], w_ada[0], m_w_ada[0], v_w_ada[0], "adamw_w_ada")
    for kind in range(4):
        outs[kind]["w_ada"] = res[kind][None]
    for (n, r, cdim), got in zip(SHARDED, got_blocks):
        res = adamw(got, weights[n][0], mom1[n][0], mom2[n][0], "adamw_" + n)
        for kind in range(4):
            outs[kind][n] = res[kind][None]
    res = adamw(got_small, small_row(weights), small_row(mom1), small_row(mom2), "adamw_small")
    for kind in range(4):
        off = 0
        for n, size in SMALL:
            outs[kind][n] = res[kind][:, off:off + size].reshape(weights[n].shape)
            off += size
    return (loss, dx[None], *[outs[0][n] for n in names], *[outs[1][n] for n in names],
            *[outs[2][n] for n in names], *[outs[3][n] for n in names])
```

```python
import functools
import math

import jax
import jax.numpy as jnp
from jax import lax
from jax.experimental import pallas as pl
from jax.experimental.pallas import tpu as pltpu

F32 = jnp.float32
BF16 = jnp.bfloat16
HIGHEST = lax.Precision.HIGHEST
MESH_IDS = pl.DeviceIdType.MESH

N_DEV = 8
D_MODEL = 1024
LN_EPS = 1e-5
RMS_EPS = 1e-6
GN_EPS = 64e-5
HEADS = 8
Q_RANK = 256
KV_RANK = 128
ROPE = 32
NOPE = 64
QK_DIM = NOPE + ROPE
WIDTH = 512
HEAD = 64
LORA = 64
CHUNK = 64
DEPTH = 1
ALPHA = (2.0 * DEPTH) ** 0.25
ROPE_THETA = 10000.0
ATTN_SCALE = QK_DIM ** -0.5
DECAY_SCALE = math.exp(-0.5)

ADAM_LR = 0.001
ADAM_B1 = 0.9
ADAM_B2 = 0.999
ADAM_EPS = 1e-08
ADAM_WD = 0.01
ADAM_STEP = 10

LANE = 128
PAIR = 2 * HEAD
ROW_TILE = 256
ATTN_FWD_TILES = (512, 1024)
ATTN_BWD_TILES = (512, 512)
LOG2_E = math.log2(math.e)
Q_PRESCALE = ATTN_SCALE * LOG2_E
WKV_CHUNKS_PER_STEP = 8
VMEM_LIMIT = 56 * 1024 * 1024

P_MA, P_MB, P_R, P_K, P_V, P_GPA, P_GPB, P_QC, P_KVC, P_KR, P_KRR, P_LORA = (
    0, 1024, 2048, 2560, 3072, 3584, 4096, 4608, 4864, 4992, 5120, 5248)
P_WIDTH = 5376
DW_BLOCK = 768

N_QC, N_KVC, N_KROPE, N_GPA, N_RWKV, N_GPB, N_MA, N_MB = 0, 256, 384, 416, 928, 2592, 3104, 4128
IN_WIDTH = 5152

SHARDED = (("w_in", 1024, 644), ("w_uq", 256, 96), ("w_ukv", 128, 128), ("w_decay_up", 64, 64),
           ("w_iclr_up", 64, 64), ("w_proj_a", 512, 128), ("w_proj_b", 512, 128), ("w_out", 128, 1024))
SHARD_ELEMS = sum(r * c for _, r, c in SHARDED)
SHARD_ROWS = SHARD_ELEMS // LANE
GATHER_ROWS = SHARD_ROWS + 2 * D_MODEL // LANE
SMALL = (("b_ada", 3072), ("q_norm_g", 256), ("kv_norm_g", 128), ("mu_rwkv", 1664), ("w0", 512), ("a0", 512),
         ("k_k", 512), ("k_a", 512), ("r_k", 512), ("gn_g", 512), ("gn_b", 512), ("post_g", 1024), ("post_b", 1024))
SMALL_ELEMS = sum(n for _, n in SMALL)
SMALL_ROWS = SMALL_ELEMS // LANE


def mm(a, b):
    return jnp.dot(a.astype(BF16), b.astype(BF16), preferred_element_type=F32)


def mm_nt(a, b):
    return lax.dot_general(a.astype(BF16), b.astype(BF16), (((1,), (1,)), ((), ())), preferred_element_type=F32)


def mm_tn(a, b):
    return lax.dot_general(a.astype(BF16), b.astype(BF16), (((0,), (0,)), ((), ())), preferred_element_type=F32)


def hdot(a, b):
    return jnp.dot(a, b, precision=HIGHEST, preferred_element_type=F32)


def hdot_nt(a, b):
    return lax.dot_general(a, b, (((1,), (1,)), ((), ())), precision=HIGHEST, preferred_element_type=F32)


def hdot_tn(a, b):
    return lax.dot_general(a, b, (((0,), (0,)), ((), ())), precision=HIGHEST, preferred_element_type=F32)


def sigmoid(x):
    return 1.0 / (1.0 + jnp.exp(-x))


def colsum(x):
    return jnp.sum(x, axis=0, keepdims=True)


def rowmean(x):
    return jnp.mean(x, axis=-1, keepdims=True)


def layer_norm_stats(x):
    xc = x - rowmean(x)
    rstd = lax.rsqrt(rowmean(xc * xc) + LN_EPS)
    return xc * rstd, rstd


def layer_norm_bwd(dy, xhat, rstd):
    return rstd * (dy - rowmean(dy) - xhat * rowmean(dy * xhat))


def bf16_pieces(x, n):
    pieces = []
    for _ in range(n):
        p = x.astype(BF16)
        pieces.append(p)
        x = x - p.astype(F32)
    return pieces


def ones_dot(ones, x, n_pieces):
    ones = ones.astype(BF16)
    return sum(jnp.dot(ones, p, preferred_element_type=F32) for p in bf16_pieces(x, n_pieces))


def head_sum(x, bd):
    bd = bd.astype(BF16)
    out = []
    for p in range(x.shape[1] // LANE):
        hi, lo = bf16_pieces(x[:, p * LANE:(p + 1) * LANE], 2)
        out.append(jnp.dot(hi, bd, preferred_element_type=F32) + jnp.dot(lo, bd, preferred_element_type=F32))
    return jnp.concatenate(out, axis=1)


def tile_lanes(t, n):
    return jnp.concatenate([t] * n, axis=1)


def row_iota(shape):
    return lax.broadcasted_iota(jnp.int32, shape, 0)


def lane_iota(shape):
    return lax.broadcasted_iota(jnp.int32, shape, 1)


def shift_rows_down(x, row0):
    rolled = pltpu.roll(x, 1, axis=0)
    return jnp.where(row_iota(x.shape) == 0, row0, rolled)


def shift_rows_up(x, row_last):
    rolled = pltpu.roll(x, x.shape[0] - 1, axis=0)
    return jnp.where(row_iota(x.shape) == x.shape[0] - 1, row_last, rolled)


def row_call(name, fn, n_rows, row_in, const_in, row_out, acc_out=(), halo_in=(), carry=(), reverse=False):
    ts = ROW_TILE
    n_tiles = n_rows // ts
    n_in = len(row_in) + len(halo_in) + len(const_in)
    n_ro, n_ao = len(row_out), len(acc_out)

    def tile_of(g):
        return (n_tiles - 1 - g) if reverse else g

    def body(*refs):
        ins = refs[:n_in]
        ro = refs[n_in:n_in + n_ro]
        ao = refs[n_in + n_ro:n_in + n_ro + n_ao]
        cr = refs[n_in + n_ro + n_ao:]
        g = pl.program_id(0)
        step0 = g == 0
        tile0 = tile_of(g) == 0
        for r in cr:
            @pl.when(step0)
            def _(r=r):
                r[...] = jnp.zeros_like(r)
        vals = [r[...] for r in ins]
        outs = fn(step0, tile0, *vals, *[c[0:1, :] for c in cr])
        for r, v in zip(ro, outs[:n_ro]):
            r[...] = v.astype(r.dtype)
        for r, v in zip(ao, outs[n_ro:n_ro + n_ao]):
            @pl.when(step0)
            def _(r=r, v=v):
                r[...] = v.astype(r.dtype)

            @pl.when(jnp.logical_not(step0))
            def _(r=r, v=v):
                r[...] += v.astype(r.dtype)
        for r, v in zip(cr, outs[n_ro + n_ao:]):
            r[0:1, :] = v

    in_specs = [pl.BlockSpec((ts, w), functools.partial(lambda g, cb: (tile_of(g), cb), cb=cb)) for _, w, cb in row_in]
    in_specs += [pl.BlockSpec((8, w), functools.partial(
        lambda g, cb: (jnp.maximum(tile_of(g) * (ts // 8) - 1, 0), cb), cb=cb)) for _, w, cb in halo_in]
    in_specs += [pl.BlockSpec(memory_space=pltpu.VMEM) for _ in const_in]
    out_specs = [pl.BlockSpec((ts, w), lambda g: (tile_of(g), 0)) for w, _ in row_out]
    out_specs += [pl.BlockSpec(s, lambda g: (0, 0)) for s, _ in acc_out]
    out_shape = [jax.ShapeDtypeStruct((n_rows, w), d) for w, d in row_out]
    out_shape += [jax.ShapeDtypeStruct(s, d) for s, d in acc_out]
    return pl.pallas_call(
        body, name=name, grid=(n_tiles,), in_specs=in_specs, out_specs=out_specs, out_shape=out_shape,
        scratch_shapes=[pltpu.VMEM((8, w), F32) for w in carry],
        compiler_params=pltpu.CompilerParams(dimension_semantics=("arbitrary",), vmem_limit_bytes=VMEM_LIMIT),
    )(*[a for a, _, _ in row_in], *[a for a, _, _ in halo_in], *const_in)


def my_position():
    return lax.axis_index("x"), lax.axis_index("y"), lax.axis_index("c")


def flip(pos, k):
    x, y, c = pos
    dx, dy, dc = (k >> 2) & 1, (k >> 1) & 1, k & 1
    return (1 - x if dx else x, 1 - y if dy else y, 1 - c if dc else c)


def flat_index(pos):
    return 4 * pos[0] + 2 * pos[1] + pos[2]


def gather_shards(shards):
    n = len(shards)

    def body(*refs):
        x_refs, out_refs = refs[:n], refs[n:2 * n]
        send_sems, recv_sems, local_sems = refs[2 * n:]
        x, y, c = my_position()
        me, sibling = (x, y, c), (x, y, 1 - c)
        chips = [(1 - x, y), (x, 1 - y), (1 - x, 1 - y)]

        def copy(a, k, block, to, from_input=False):
            slot = out_refs[a].at[flat_index(block)]
            return pltpu.make_async_remote_copy(
                src_ref=x_refs[a] if from_input else slot, dst_ref=slot,
                send_sem=send_sems.at[7 * a + k], recv_sem=recv_sems.at[7 * a + k],
                device_id=to, device_id_type=MESH_IDS)

        mine = [pltpu.make_async_copy(x_refs[a], out_refs[a].at[flat_index(me)], local_sems.at[a]) for a in range(n)]
        for cp in mine:
            cp.start()
        first = []
        for a in range(n):
            first.append(copy(a, 0, me, sibling, from_input=True))
            first += [copy(a, 1 + j, me, (*chip, c), from_input=True) for j, chip in enumerate(chips)]
        for cp in first:
            cp.start()
        passed = []
        for j, chip in enumerate(chips):
            for a in range(n):
                copy(a, 1 + j, (*chip, c), me).wait_recv()
                cp = copy(a, 4 + j, (*chip, c), sibling)
                cp.start()
                passed.append(cp)
        for a in range(n):
            copy(a, 0, sibling, me).wait_recv()
            for j, chip in enumerate(chips):
                copy(a, 4 + j, (*chip, 1 - c), me).wait_recv()
        for cp in first + passed:
            cp.wait_send()
        for cp in mine:
            cp.wait()

    return pl.pallas_call(
        body, name="gather_shards",
        out_shape=[jax.ShapeDtypeStruct((N_DEV,) + s.shape, s.dtype) for s in shards],
        in_specs=[pl.BlockSpec(memory_space=pl.ANY)] * n, out_specs=[pl.BlockSpec(memory_space=pl.ANY)] * n,
        scratch_shapes=[pltpu.SemaphoreType.DMA((7 * n,)), pltpu.SemaphoreType.DMA((7 * n,)),
                        pltpu.SemaphoreType.DMA((n,))],
    )(*shards)


def ada_modulation(c_all, w_ada_loc, b_ada_blocks):
    cols = w_ada_loc.shape[1]

    def body(c_ref, w_ref, b_ref, out_ref, send_sems, recv_sems):
        me = my_position()
        mi = flat_index(me)
        cv = c_ref[...]
        res = hdot(cv * sigmoid(cv), w_ref[...]) + b_ref[pl.ds(mi, 1), :]
        out_ref[mi] = res
        sends = []
        for k in range(1, N_DEV):
            cp = pltpu.make_async_remote_copy(
                src_ref=out_ref.at[mi], dst_ref=out_ref.at[mi], send_sem=send_sems.at[k - 1],
                recv_sem=recv_sems.at[k - 1], device_id=flip(me, k), device_id_type=MESH_IDS)
            cp.start()
            sends.append(cp)
        for k in range(1, N_DEV):
            pi = flat_index(flip(me, k))
            pltpu.make_async_remote_copy(
                src_ref=out_ref.at[pi], dst_ref=out_ref.at[pi], send_sem=send_sems.at[k - 1],
                recv_sem=recv_sems.at[k - 1], device_id=flip(me, k), device_id_type=MESH_IDS).wait_recv()
        for cp in sends:
            cp.wait_send()

    return pl.pallas_call(
        body, name="ada_modulation",
        out_shape=jax.ShapeDtypeStruct((N_DEV, N_DEV, cols), F32),
        in_specs=[pl.BlockSpec(memory_space=pltpu.VMEM)] * 3, out_specs=pl.BlockSpec(memory_space=pltpu.VMEM),
        scratch_shapes=[pltpu.SemaphoreType.DMA((7,)), pltpu.SemaphoreType.DMA((7,))],
    )(c_all, w_ada_loc, b_ada_blocks)


def fwd_in_tile(step0, tile0, x, mod, w_in_p):
    xhat, _ = layer_norm_stats(x)
    h = xhat * (1.0 + mod[1:2]) + mod[0:1]
    return (mm(h, w_in_p),)


def rms_norm_fwd(x, g):
    r = lax.rsqrt(rowmean(x * x) + RMS_EPS)
    xh = x * r
    return xh * g, xh, r


def key_rope_mask(shape):
    return (lane_iota(shape) >= NOPE).astype(F32)


def mla_prep_tile(step0, tile0, q_c, kv_c, kr, krr, cos, sin, gq, gkv, wq, wqr, wkn, wv):
    qn, _, _ = rms_norm_fwd(q_c, gq)
    kvn, _, _ = rms_norm_fwd(kv_c, gkv)
    q = (mm(qn, wq) * tile_lanes(cos, HEADS) + mm(qn, wqr) * tile_lanes(sin, HEADS)) * Q_PRESCALE
    kpe = kr * (cos * key_rope_mask(cos.shape)) + krr * sin
    k = mm(kvn, wkn) + tile_lanes(kpe, HEADS)
    v = mm(kvn, wv)
    return q, k, v


def rwkv_prep_core(tile0, r0, k0, v0, l0, hr, hk, hv, hl, mu_r, mu_k, mu_v, mu_l, w0, a0, k_k, k_a,
                   w_dec, w_iclr, tril, same, bd):
    def shifted(x, halo, mu):
        row0 = jnp.where(tile0, 0.0, halo[7:8, :])
        prev = shift_rows_down(x, row0)
        return x + (prev - x) * mu, prev

    ur, pr = shifted(r0, hr, mu_r)
    uk, pk = shifted(k0, hk, mu_k)
    uv, pv = shifted(v0, hv, mu_v)
    ul, plo = shifted(l0, hl, mu_l)
    th = jnp.tanh(ul)
    sg = sigmoid(w0 + mm(th, w_dec))
    lw = -DECAY_SCALE * sg
    a_ic = sigmoid(a0 + mm(ul, w_iclr))
    kkraw = uk * k_k
    nrm_raw = jnp.sqrt(head_sum(kkraw * kkraw, bd))
    nrm = jnp.maximum(nrm_raw, 1e-12)
    kk = kkraw / nrm
    k2 = uk * (1.0 + (a_ic - 1.0) * k_a)
    lc = ones_dot(tril, lw, 3)
    lcl = ones_dot(same, lw, 3)
    return dict(ur=ur, uk=uk, uv=uv, ul=ul, pr=pr, pk=pk, pv=pv, pl=plo, th=th, sg=sg, lw=lw, a_ic=a_ic,
                kkraw=kkraw, nrm_raw=nrm_raw, nrm=nrm, kk=kk, k2=k2, lc=lc, lcl=lcl)


def rwkv_prep_tile(step0, tile0, r0, k0, v0, l0, hr, hk, hv, hl, *consts):
    f = rwkv_prep_core(tile0, r0, k0, v0, l0, hr, hk, hv, hl, *consts)
    lc, lw = f["lc"], f["lw"]
    e_neg = jnp.exp(-lc)
    rt = f["ur"] * jnp.exp(lc)
    at = -f["kk"] * jnp.exp(lc - lw)
    bt = f["kk"] * f["a_ic"] * e_neg
    kt = f["k2"] * e_neg
    return rt, at, bt, kt, jnp.exp(f["lcl"]), f["uv"], f["ur"], f["k2"]


def wkv_masks():
    lane = lane_iota((1, PAIR))
    m_lo = (lane < HEAD).astype(F32)
    ri = row_iota((CHUNK, CHUNK))
    ci = lane_iota((CHUNK, CHUNK))
    r2 = row_iota((PAIR, PAIR))
    c2 = lane_iota((PAIR, PAIR))
    bd = ((r2 < HEAD) == (c2 < HEAD)).astype(F32)
    eye2 = (r2 == c2).astype(F32)
    return (m_lo, 1.0 - m_lo), ri > ci, ri >= ci, (ri == ci).astype(F32), bd, eye2


def wkv_chunks_pre(chunks, masks):
    ms, strict, incl, eye, bd, eye2 = masks
    items = [(c, m) for c in range(len(chunks)) for m in ms]
    at, bt, kt, rt, v, cl = (list(t) for t in zip(*chunks))
    atm = [at[c] * m for c, m in items]
    rtm = [rt[c] * m for c, m in items]
    aab = [jnp.where(strict, mm_nt(x, bt[c]), 0.0) for x, (c, _) in zip(atm, items)]
    aak = [jnp.where(strict, mm_nt(x, kt[c]), 0.0) for x, (c, _) in zip(atm, items)]
    prb = [jnp.where(incl, mm_nt(x, bt[c]), 0.0) for x, (c, _) in zip(rtm, items)]
    prk = [jnp.where(incl, mm_nt(x, kt[c]), 0.0) for x, (c, _) in zip(rtm, items)]
    tinv = [eye + a for a in aab]
    power = aab
    for _ in range(5):
        power = [mm(p, p) for p in power]
        tinv = [t + mm(t, p) for t, p in zip(tinv, power)]

    def by_chunk(parts):
        return [parts[2 * c] + parts[2 * c + 1] for c in range(len(chunks))]

    w = by_chunk([mm(a, v[c] * m) for a, (c, m) in zip(aak, items)])
    ah = by_chunk([mm(t, x) for t, x in zip(tinv, atm)])
    wh = by_chunk([mm(t, w[c] * m) for t, (c, m) in zip(tinv, items)])
    rh = [r + d for r, d in zip(rt, by_chunk([mm(p, ah[c] * m) for p, (c, m) in zip(prb, items)]))]
    yh = by_chunk([mm(p, wh[c] * m) + mm(q, v[c] * m) for p, q, (c, m) in zip(prb, prk, items)])
    bc = [b * c_ for b, c_ in zip(bt, cl)]
    kc = [k * c_ for k, c_ in zip(kt, cl)]
    g = [eye2 * c_ + bd * mm_tn(b, a) for c_, b, a in zip(cl, bc, ah)]
    h = [bd * (mm_tn(b, w_) + mm_tn(k, v_)) for b, w_, k, v_ in zip(bc, wh, kc, v)]
    side = lambda parts: [jnp.concatenate([parts[2 * c], parts[2 * c + 1]], axis=1).astype(BF16)
                          for c in range(len(chunks))]
    saved = (side(tinv), side(aak), side(prb), side(prk), [a.astype(BF16) for a in ah], wh)
    return g, h, rh, yh, saved


def wkv_chunks_grad(chunks, saved, m0, dy, dm1, masks):
    ms, strict, incl, eye, bd, eye2 = masks
    n = len(chunks)
    at, bt, kt, rt, v, cl = (list(t) for t in zip(*chunks))
    items = [(c, m) for c in range(n) for m in ms]
    atm = [at[c] * m for c, m in items]
    rtm = [rt[c] * m for c, m in items]
    halves = lambda pairs: [x for pr in pairs for x in (pr[:, :CHUNK], pr[:, CHUNK:])]
    tinv, aak, prb, prk = (halves(s) for s in zip(*[(a, b, c_, d) for a, b, c_, d, _, _ in saved]))
    ah = [s[4] for s in saved]
    wh = [s[5] for s in saved]
    bc = [b * c_ for b, c_ in zip(bt, cl)]
    kc = [k * c_ for k, c_ in zip(kt, cl)]

    def by_chunk(parts):
        return [parts[2 * c] + parts[2 * c + 1] for c in range(n)]

    u = [mm(a, m) + w for a, m, w in zip(ah, m0, wh)]
    dm1 = [d * bd for d in dm1]
    dym = [dy[c] * m for c, m in items]
    du = [mm(b, d) + e for b, d, e in zip(bc, dm1, by_chunk([mm_tn(p, x) for p, x in zip(prb, dym)]))]
    dv = [mm(k, d) + e for k, d, e in zip(kc, dm1, by_chunk([mm_tn(p, x) for p, x in zip(prk, dym)]))]
    dz = by_chunk([mm_tn(t, du[c] * m) for t, (c, m) in zip(tinv, items)])
    dzm = [dz[c] * m for c, m in items]
    dv = [a + b for a, b in zip(dv, by_chunk([mm_tn(a_, x) for a_, x in zip(aak, dzm)]))]
    drt = [mm_nt(d, m) for d, m in zip(dy, m0)]
    dat = [mm_nt(d, m) for d, m in zip(dz, m0)]
    udm = [mm_nt(x, d) for x, d in zip(u, dm1)]
    vdm = [mm_nt(x, d) for x, d in zip(v, dm1)]
    daab = [jnp.where(strict, mm_nt(x, u[c]), 0.0) for x, (c, _) in zip(dzm, items)]
    daak = [jnp.where(strict, mm_nt(x, v[c]), 0.0) for x, (c, _) in zip(dzm, items)]
    dprb = [jnp.where(incl, mm_nt(x, u[c]), 0.0) for x, (c, _) in zip(dym, items)]
    dprk = [jnp.where(incl, mm_nt(x, v[c]), 0.0) for x, (c, _) in zip(dym, items)]
    drt2 = by_chunk([(mm(p, bt[c]) + mm(q, kt[c])) * m for p, q, (c, m) in zip(dprb, dprk, items)])
    dat2 = by_chunk([(mm(p, bt[c]) + mm(q, kt[c])) * m for p, q, (c, m) in zip(daab, daak, items)])
    dbt2 = by_chunk([mm_tn(p, r) + mm_tn(a_, x) for p, r, a_, x in zip(dprb, rtm, daab, atm)])
    dkt2 = by_chunk([mm_tn(p, r) + mm_tn(a_, x) for p, r, a_, x in zip(dprk, rtm, daak, atm)])
    ones = jnp.ones((8, PAIR), F32)
    upper = (lane_iota((CHUNK, CHUNK)) >= row_iota((CHUNK, CHUNK))).astype(F32)
    out = []
    for c in range(n):
        drt_c = drt[c] + drt2[c]
        dat_c = dat[c] + dat2[c]
        dbt_c = udm[c] * cl[c] + dbt2[c]
        dkt_c = vdm[c] * cl[c] + dkt2[c]
        dlcl = hdot_nt(ones, dm1[c] * m0[c])[0:1, :] * cl[c] + colsum(bc[c] * udm[c] + kc[c] * vdm[c])
        g = drt_c * rt[c] - dbt_c * bt[c] - dkt_c * kt[c] + dat_c * at[c]
        dlw = hdot(upper, g) - dat_c * at[c] + dlcl
        out.append((dat_c, dbt_c, dkt_c, drt_c, dv[c], dlw))
    return out


def wkv_forward(at, bt, kt, rt, v, clf):
    n_rows = at.shape[0]
    cps = WKV_CHUNKS_PER_STEP
    rb = cps * CHUNK
    n_steps = n_rows // rb

    def body(a_ref, b_ref, k_ref, r_ref, v_ref, c_ref, y_ref, m0_ref, g_ref, rh_ref, *rest):
        saved_refs, m_scr = rest[:6], rest[6]

        @pl.when(pl.program_id(1) == 0)
        def _():
            m_scr[...] = jnp.zeros_like(m_scr)

        masks = wkv_masks()
        chunks = []
        for cc in range(cps):
            sl = slice(cc * CHUNK, (cc + 1) * CHUNK)
            chunks.append((a_ref[sl, :], b_ref[sl, :], k_ref[sl, :], r_ref[sl, :], v_ref[sl, :],
                           c_ref[cc * CHUNK:cc * CHUNK + 1, :]))
        gs, hs, rhs, yhs, saved = wkv_chunks_pre(chunks, masks)
        for ref, per_chunk in zip(saved_refs, saved):
            for cc, val in enumerate(per_chunk):
                ref[cc * CHUNK:(cc + 1) * CHUNK, :] = val
        m = m_scr[...]
        for cc, (g, h, rh, yh) in enumerate(zip(gs, hs, rhs, yhs)):
            sl = slice(cc * CHUNK, (cc + 1) * CHUNK)
            m0_ref[0, cc] = m
            g_ref[0, cc] = g
            rh_ref[sl, :] = rh
            y_ref[sl, :] = hdot(rh, m) + yh
            m = hdot(g, m) + h
        m_scr[...] = m

    blk = pl.BlockSpec((rb, PAIR), lambda p, s: (s, p))
    state_blk = pl.BlockSpec((1, cps, PAIR, PAIR), lambda p, s: (p, s, 0, 0))
    state_shape = jax.ShapeDtypeStruct((WIDTH // PAIR, n_rows // CHUNK, PAIR, PAIR), F32)
    rows_f32 = jax.ShapeDtypeStruct((n_rows, WIDTH), F32)
    rows_bf16 = jax.ShapeDtypeStruct((n_rows, WIDTH), BF16)
    return pl.pallas_call(
        body, name="wkv_forward", grid=(WIDTH // PAIR, n_steps),
        in_specs=[blk] * 6,
        out_specs=[blk, state_blk, state_blk, blk] + [blk] * 6,
        out_shape=[rows_f32, state_shape, state_shape, rows_f32] + [rows_bf16] * 5 + [rows_f32],
        scratch_shapes=[pltpu.VMEM((PAIR, PAIR), F32)],
        compiler_params=pltpu.CompilerParams(dimension_semantics=("arbitrary", "arbitrary"),
                                             vmem_limit_bytes=VMEM_LIMIT),
    )(at, bt, kt, rt, v, clf)


def wkv_backward(at, bt, kt, rt, v, clf, m0s, gs, rh, saved, dy):
    n_rows = at.shape[0]
    cps = WKV_CHUNKS_PER_STEP
    rb = cps * CHUNK
    n_steps = n_rows // rb

    def body(a_ref, b_ref, k_ref, r_ref, v_ref, c_ref, m0_ref, g_ref, rh_ref, *rest):
        saved_refs, dy_ref = rest[:6], rest[6]
        da_ref, db_ref, dk_ref, dr_ref, dv_ref, dlw_ref, dm_scr = rest[7:]

        @pl.when(pl.program_id(1) == 0)
        def _():
            dm_scr[...] = jnp.zeros_like(dm_scr)

        masks = wkv_masks()
        bd = masks[4]
        dm = dm_scr[...]
        dm1 = [None] * cps
        for cc in reversed(range(cps)):
            sl = slice(cc * CHUNK, (cc + 1) * CHUNK)
            dm1[cc] = dm
            dm = bd * (hdot_tn(g_ref[0, cc], dm) + hdot_tn(rh_ref[sl, :], dy_ref[sl, :]))
        dm_scr[...] = dm
        chunks, kept, m0, dys = [], [], [], []
        for cc in range(cps):
            sl = slice(cc * CHUNK, (cc + 1) * CHUNK)
            chunks.append((a_ref[sl, :], b_ref[sl, :], k_ref[sl, :], r_ref[sl, :], v_ref[sl, :],
                           c_ref[cc * CHUNK:cc * CHUNK + 1, :]))
            kept.append(tuple(ref[sl, :] for ref in saved_refs))
            m0.append(m0_ref[0, cc])
            dys.append(dy_ref[sl, :])
        grads = wkv_chunks_grad(chunks, kept, m0, dys, dm1, masks)
        for cc, (dat, dbt, dkt, drt, dv, dlw) in enumerate(grads):
            sl = slice(cc * CHUNK, (cc + 1) * CHUNK)
            da_ref[sl, :] = dat
            db_ref[sl, :] = dbt
            dk_ref[sl, :] = dkt
            dr_ref[sl, :] = drt
            dv_ref[sl, :] = dv
            dlw_ref[sl, :] = dlw

    blk = pl.BlockSpec((rb, PAIR), lambda p, s: (n_steps - 1 - s, p))
    state_blk = pl.BlockSpec((1, cps, PAIR, PAIR), lambda p, s: (p, n_steps - 1 - s, 0, 0))
    return pl.pallas_call(
        body, name="wkv_backward", grid=(WIDTH // PAIR, n_steps),
        in_specs=[blk] * 6 + [state_blk, state_blk, blk] + [blk] * 6 + [blk],
        out_specs=[blk] * 6,
        out_shape=[jax.ShapeDtypeStruct((n_rows, WIDTH), F32)] * 6,
        scratch_shapes=[pltpu.VMEM((PAIR, PAIR), F32)],
        compiler_params=pltpu.CompilerParams(dimension_semantics=("arbitrary", "arbitrary"),
                                             vmem_limit_bytes=VMEM_LIMIT),
    )(at, bt, kt, rt, v, clf, m0s, gs, rh, *saved, dy)


def visible(q_row0, k_row0, shape):
    qc = (q_row0 + row_iota(shape)) // CHUNK
    kc = (k_row0 + lane_iota(shape)) // CHUNK
    return kc <= qc


def attention_forward(q, k, v):
    n_rows = q.shape[0]
    tq, tk = ATTN_FWD_TILES
    n_q = n_rows // tq
    n_masked = max(1, tq // tk)

    def body(q_ref, k_ref, v_ref, o_ref, lse_ref):
        i = pl.program_id(1)
        lane = lane_iota((tq, LANE))
        heads = [slice(0, LANE), slice(LANE, 2 * LANE)]
        qs = [q_ref[:, cols] for cols in heads]

        def step(j, carry, masked):
            rows = pl.ds(pl.multiple_of(j * tk, tk), tk)
            ss = [mm_nt(qh, k_ref[rows, cols]) for qh, cols in zip(qs, heads)]
            if masked:
                vis = visible(i * tq, j * tk, ss[0].shape)
                ss = [jnp.where(vis, s, -jnp.inf) for s in ss]
            ps, stats = [], []
            for s, (m, l, _) in zip(ss, carry):
                m_new = jnp.maximum(m, jnp.max(s, axis=-1, keepdims=True))
                p = jnp.exp2(s - m_new)
                alpha = jnp.exp2(m - m_new)
                ps.append(p)
                stats.append((m_new, alpha, alpha * l + jnp.sum(p, axis=-1, keepdims=True)))
            pvs = [mm(p, v_ref[rows, cols]) for p, cols in zip(ps, heads)]
            return tuple((m_new, l, alpha * acc + pv)
                         for (m_new, alpha, l), (_, _, acc), pv in zip(stats, carry, pvs))

        carry = tuple((jnp.full((tq, 1), -jnp.inf, F32), jnp.zeros((tq, 1), F32), jnp.zeros((tq, LANE), F32))
                      for _ in heads)
        n_full = (i * tq) // tk
        carry = lax.fori_loop(0, n_full, functools.partial(step, masked=False), carry)
        for extra in range(n_masked):
            carry = step(n_full + extra, carry, masked=True)
        (m0, l0, acc0), (m1, l1, acc1) = carry
        o_ref[...] = acc0 / l0 + acc1 / l1
        lse_ref[...] = jnp.where(lane >= HEAD, m1 + jnp.log2(l1), m0 + jnp.log2(l0))

    return pl.pallas_call(
        body, name="attention_forward", grid=(HEADS // 2, n_q),
        in_specs=[pl.BlockSpec((tq, 2 * LANE), lambda p, i: (i, p)),
                  pl.BlockSpec((n_rows, 2 * LANE), lambda p, i: (0, p)),
                  pl.BlockSpec((n_rows, 2 * LANE), lambda p, i: (0, p))],
        out_specs=[pl.BlockSpec((tq, LANE), lambda p, i: (i, p))] * 2,
        out_shape=[jax.ShapeDtypeStruct((n_rows, WIDTH), F32)] * 2,
        compiler_params=pltpu.CompilerParams(dimension_semantics=("arbitrary", "arbitrary"),
                                             vmem_limit_bytes=VMEM_LIMIT),
    )(q, k, v)


def attention_backward(q, k, v, o, do, lse):
    n_rows = q.shape[0]
    tq, tk = ATTN_BWD_TILES
    n_q = n_rows // tq
    n_masked = max(1, tk // tq)

    def body(q_ref, k_ref, v_ref, o_ref, do_ref, lse_ref, dq_ref, dk_ref, dv_ref):
        j = pl.program_id(1)

        @pl.when(j == 0)
        def _():
            dq_ref[...] = jnp.zeros_like(dq_ref)

        lane = lane_iota((tq, LANE))
        heads = [slice(0, LANE), slice(LANE, 2 * LANE)]
        ks = [k_ref[:, cols] for cols in heads]
        vs = [v_ref[:, cols] for cols in heads]
        head_lanes = [(lane < HEAD).astype(F32), (lane >= HEAD).astype(F32)]

        def step(i, carry, masked):
            rows = pl.ds(pl.multiple_of(i * tq, tq), tq)
            qs = [q_ref[rows, cols] for cols in heads]
            dout = do_ref[rows, :]
            dout_o = dout * o_ref[rows, :]
            lse_t = lse_ref[rows, :]
            ss = [mm_nt(qh, kh) for qh, kh in zip(qs, ks)]
            dps = [mm_nt(dout, vh) for vh in vs]
            ps, dss = [], []
            for hh in range(2):
                delta = jnp.sum(dout_o * head_lanes[hh], axis=-1, keepdims=True)
                lse_h = jnp.sum(jnp.where(lane == hh * HEAD, lse_t, 0.0), axis=-1, keepdims=True)
                p = jnp.exp2(ss[hh] - lse_h)
                if masked:
                    p = jnp.where(visible(i * tq, j * tk, p.shape), p, 0.0)
                ps.append(p)
                dss.append(p * (dps[hh] - delta))
            dvs = [mm_tn(p, dout) for p in ps]
            dqs = [mm(ds, kh) for ds, kh in zip(dss, ks)]
            dks = [mm_tn(ds, qh) for ds, qh in zip(dss, qs)]
            for cols, dq in zip(heads, dqs):
                dq_ref[rows, cols] += dq * ATTN_SCALE
            return tuple((dk + a, dv + b) for (dk, dv), a, b in zip(carry, dks, dvs))

        carry = tuple((jnp.zeros((tk, LANE), F32), jnp.zeros((tk, LANE), F32)) for _ in heads)
        i_first = (j * tk) // tq
        for extra in range(n_masked):
            carry = step(i_first + extra, carry, masked=True)
        carry = lax.fori_loop(i_first + n_masked, n_q, functools.partial(step, masked=False), carry)
        for cols, (dk, dv) in zip(heads, carry):
            dk_ref[:, cols] = dk * (1.0 / LOG2_E)
            dv_ref[:, cols] = dv

    full = lambda w: pl.BlockSpec((n_rows, w), lambda p, j: (0, p))
    blk = pl.BlockSpec((tk, 2 * LANE), lambda p, j: (j, p))
    return pl.pallas_call(
        body, name="attention_backward", grid=(HEADS // 2, n_rows // tk),
        in_specs=[full(2 * LANE), blk, blk, full(LANE), full(LANE), full(LANE)],
        out_specs=[full(2 * LANE), blk, blk],
        out_shape=[jax.ShapeDtypeStruct((n_rows, HEADS * LANE), F32)] * 3,
        compiler_params=pltpu.CompilerParams(dimension_semantics=("arbitrary", "arbitrary"),
                                             vmem_limit_bytes=VMEM_LIMIT),
    )(q, k, v, o, do, lse)


def tail_tile(step0, tile0, x, tgt, ma, mb, gpa, gpb, ya, y, ur, k2, uv,
              mod, wpa, wpb, wout, gn_g, gn_b, r_k, post_g, post_b, bd):
    gate = mod[2:3]
    inv = 1.0 / HEAD
    yc = y - head_sum(y, bd) * inv
    rs = lax.rsqrt(head_sum(yc * yc, bd) * inv + GN_EPS)
    yn = yc * rs
    yb = yn * gn_g + gn_b + head_sum(ur * k2 * r_k, bd) * uv
    sga, sgb = sigmoid(gpa), sigmoid(gpb)
    sila, silb = gpa * sga, gpb * sgb
    ga, gb = ya * sila, yb * silb
    pa, pb = mm(ga, wpa), mm(gb, wpb)
    sa, sb = sigmoid(ma), sigmoid(mb)
    merged = sa * pa + sb * pb
    sub = mm(merged, wout)
    z = ALPHA * x + (1.0 + gate) * sub
    zhat, rstd = layer_norm_stats(z)
    err = zhat * post_g + post_b - tgt
    loss = 0.5 * jnp.sum(rowmean(err * err), axis=0, keepdims=True) + jnp.zeros((1, LANE), F32)
    dout = err * (1.0 / D_MODEL)
    dpost_g = colsum(dout * zhat)
    dpost_b = colsum(dout)
    dz = layer_norm_bwd(dout * post_g, zhat, rstd)
    dgate = colsum(dz * sub)
    dsub = dz * (1.0 + gate)
    dwout = mm_tn(merged, dsub)
    dmerged = mm_nt(dsub, wout)
    dpa, dpb = dmerged * sa, dmerged * sb
    dma = dmerged * pa * sa * (1.0 - sa)
    dmb = dmerged * pb * sb * (1.0 - sb)
    dwpa = mm_tn(ga, dpa)
    dwpb = mm_tn(gb, dpb)
    dga = mm_nt(dpa, wpa)
    dgb = mm_nt(dpb, wpb)
    dya = dga * sila
    dgpa = dga * ya * (sga * (1.0 + gpa * (1.0 - sga)))
    dyb = dgb * silb
    dgpb = dgb * yb * (sgb * (1.0 + gpb * (1.0 - sgb)))
    dgn_g = colsum(dyb * yn)
    dgn_b = colsum(dyb)
    dyn = dyb * gn_g
    dy = rs * (dyn - head_sum(dyn, bd) * inv - yn * head_sum(dyn * yn, bd) * inv)
    return (dz, dma, dmb, dgpa, dgpb, dya, dy, dyb,
            loss, dwout, dwpa, dwpb, dgn_g, dgn_b, dpost_g, dpost_b, dgate)


def mla_prep_bwd_tile(step0, tile0, q_c, kv_c, cos, sin, dq, dk, dv, gq, gkv, wq, wqr, wkn, wv):
    qn, qh, rq = rms_norm_fwd(q_c, gq)
    kvn, kvh, rkv = rms_norm_fwd(kv_c, gkv)
    dqc = dq * tile_lanes(cos, HEADS)
    dqs = dq * tile_lanes(sin, HEADS)
    dqn = mm_nt(dqc, wq) + mm_nt(dqs, wqr)
    dkvn = mm_nt(dk, wkn) + mm_nt(dv, wv)
    dkpe = dk[:, 0:LANE]
    for h in range(1, HEADS):
        dkpe = dkpe + dk[:, h * LANE:(h + 1) * LANE]
    dkr = dkpe * (cos * key_rope_mask(cos.shape))
    dkrr = dkpe * sin

    def rms_bwd(dyv, xh, r, g):
        dyg = dyv * g
        return r * (dyg - xh * rowmean(dyg * xh)), colsum(dyv * xh)

    dq_c, dgq = rms_bwd(dqn, qh, rq, gq)
    dkv_c, dgkv = rms_bwd(dkvn, kvh, rkv, gkv)
    return (dq_c, dkv_c, dkr, dkrr,
            mm_tn(qn, dqc), mm_tn(qn, dqs), mm_tn(kvn, dk), mm_tn(kvn, dv), dgq, dgkv)


def rwkv_prep_bwd_tile(step0, tile0, r0, k0, v0, l0, drt, dat, dbt, dkt, dvv, dlw, dyb, hr, hk, hv, hl,
                       mu_r, mu_k, mu_v, mu_l, w0, a0, k_k, k_a, w_dec, w_iclr, tril, same, bd, r_k,
                       cr, ck, cv, cl_):
    f = rwkv_prep_core(tile0, r0, k0, v0, l0, hr, hk, hv, hl, mu_r, mu_k, mu_v, mu_l, w0, a0, k_k, k_a,
                       w_dec, w_iclr, tril, same, bd)
    ur, uk, uv, ul, kk, k2, a_ic, sg, th = (f[n] for n in ("ur", "uk", "uv", "ul", "kk", "k2", "a_ic", "sg", "th"))
    lc, lw = f["lc"], f["lw"]
    e_neg = jnp.exp(-lc)
    dur = drt * jnp.exp(lc)
    da = dat * jnp.exp(lc - lw)
    db = dbt * e_neg
    dk2 = dkt * e_neg
    s = head_sum(ur * k2 * r_k, bd)
    duv = dvv + dyb * s
    ds = head_sum(dyb * uv, bd)
    dur = dur + ds * k2 * r_k
    dk2 = dk2 + ds * ur * r_k
    dr_k = colsum(ds * ur * k2)
    dkk = db * a_ic - da
    da_ic = db * kk + dk2 * uk * k_a
    duk = dk2 * (1.0 + (a_ic - 1.0) * k_a)
    dk_a = colsum(dk2 * uk * (a_ic - 1.0))
    dkkraw = jnp.where(f["nrm_raw"] > 1e-12, (dkk - kk * head_sum(dkk * kk, bd)) / f["nrm"], dkk * 1e12)
    duk = duk + dkkraw * k_k
    dk_k = colsum(dkkraw * uk)
    dai = da_ic * a_ic * (1.0 - a_ic)
    dd = dlw * (-DECAY_SCALE) * sg * (1.0 - sg)
    dul = mm_nt(dai, w_iclr) + mm_nt(dd, w_dec) * (1.0 - th * th)

    def unshift(du, x, prev, mu, carry_row):
        nxt = shift_rows_up(du, carry_row)
        return du * (1.0 - mu) + nxt * mu, colsum(du * (prev - x)), du[0:1, :]

    dr0, dmu_r, ncr = unshift(dur, r0, f["pr"], mu_r, cr)
    dk0, dmu_k, nck = unshift(duk, k0, f["pk"], mu_k, ck)
    dv0, dmu_v, ncv = unshift(duv, v0, f["pv"], mu_v, cv)
    dl0, dmu_l, ncl = unshift(dul, l0, f["pl"], mu_l, cl_)
    return (dr0, dk0, dv0, dl0,
            dmu_r, dmu_k, dmu_v, dmu_l, colsum(dd), colsum(dai), dk_k, dk_a, dr_k, mm_tn(th, dd), mm_tn(ul, dai),
            ncr, nck, ncv, ncl)


def in_backward(x, dz, pieces, mod, w_in_p, unrot):
    n_rows = x.shape[0]
    ts = ROW_TILE
    n_p = len(pieces)
    shard_cols = IN_WIDTH // N_DEV

    def body(*refs):
        x_ref, dz_ref = refs[:2]
        p_refs = refs[2:2 + n_p]
        mod_ref, w_ref, unrot_ref = refs[2 + n_p:5 + n_p]
        dx_ref, ht_ref, blocks_ref, dshift_ref, dscale_ref = refs[5 + n_p:]
        step0 = pl.program_id(0) == 0
        dma, dmb, dr0, dk0, dv0, dgpa, dgpb, dq_c, dkv_c, dkr, dkrr, dl0 = (r[...] for r in p_refs)
        dproj = jnp.concatenate([dma, dmb, dr0, dk0, dv0, dgpa, dgpb, dq_c, dkv_c, dkr, dkrr, dl0], axis=1)
        dh = mm_nt(dproj, w_ref[...])
        xhat, rstd = layer_norm_stats(x_ref[...])
        scale1 = 1.0 + mod_ref[1:2, :]
        dx_ref[...] = layer_norm_bwd(dh * scale1, xhat, rstd) + ALPHA * dz_ref[...]
        ht_ref[...] = jnp.transpose(xhat * scale1 + mod_ref[0:1, :]).astype(BF16)
        dkrope = (dkr.astype(F32) + mm(dkrr, unrot_ref[...]))[:, NOPE:QK_DIM]
        natural = jnp.concatenate(
            [dq_c.astype(F32), dkv_c.astype(F32), dkrope]
            + [p.astype(F32) for p in (dgpa, dr0, dk0, dv0, dl0, dgpb, dma, dmb)], axis=1)
        for j in range(N_DEV):
            blocks_ref[j] = natural[:, j * shard_cols:(j + 1) * shard_cols].astype(BF16)
        for ref, val in ((dshift_ref, colsum(dh)), (dscale_ref, colsum(dh * xhat))):
            @pl.when(step0)
            def _(ref=ref, val=val):
                ref[...] = val

            @pl.when(jnp.logical_not(step0))
            def _(ref=ref, val=val):
                ref[...] += val

    row = lambda w: pl.BlockSpec((ts, w), lambda i: (i, 0))
    const = pl.BlockSpec(memory_space=pltpu.VMEM)
    vec = pl.BlockSpec((1, D_MODEL), lambda i: (0, 0))
    return pl.pallas_call(
        body, name="in_backward", grid=(n_rows // ts,),
        in_specs=[row(D_MODEL), row(D_MODEL)] + [row(p.shape[1]) for p in pieces] + [const] * 3,
        out_specs=[row(D_MODEL), pl.BlockSpec((D_MODEL, ts), lambda i: (0, i)),
                   pl.BlockSpec((N_DEV, ts, shard_cols), lambda i: (0, i, 0)), vec, vec],
        out_shape=[jax.ShapeDtypeStruct((n_rows, D_MODEL), F32), jax.ShapeDtypeStruct((D_MODEL, n_rows), BF16),
                   jax.ShapeDtypeStruct((N_DEV, n_rows, shard_cols), BF16),
                   jax.ShapeDtypeStruct((1, D_MODEL), F32), jax.ShapeDtypeStruct((1, D_MODEL), F32)],
        compiler_params=pltpu.CompilerParams(dimension_semantics=("arbitrary",), vmem_limit_bytes=VMEM_LIMIT),
    )(x, dz, *pieces, mod, w_in_p, unrot)


def in_weight_grad_exchange(h_t, dp_blocks, others, small, order):
    n = len(others)
    n_rows = h_t.shape[1]
    ts = 2 * ROW_TILE
    n_i = n_rows // ts
    shard_cols = dp_blocks.shape[2]
    n_chips = N_DEV // 2
    last = N_DEV - 1

    def body(order_ref, h_ref, dp_ref, *rest):
        g_refs, s_ref = rest[:n], rest[n]
        rwin_ref, rg_refs, rs_ref = rest[n + 1], rest[n + 2:2 * n + 2], rest[2 * n + 2]
        (acc, sendbuf, sib_buf, sib_send, sib_recv, win_send, win_recv,
         o_send, o_recv, local_sems) = rest[2 * n + 3:]
        b, i = pl.program_id(0), pl.program_id(1)
        me = my_position()
        mi = flat_index(me)
        sibling = (me[0], me[1], 1 - me[2])

        def other_copies(k, src_index, dst_index):
            peer = flip(me, k)
            out = [pltpu.make_async_remote_copy(
                src_ref=g_refs[a].at[src_index], dst_ref=rg_refs[a].at[dst_index],
                send_sem=o_send.at[(n + 1) * (k - 1) + a], recv_sem=o_recv.at[(n + 1) * (k - 1) + a],
                device_id=peer, device_id_type=MESH_IDS) for a in range(n)]
            out.append(pltpu.make_async_remote_copy(
                src_ref=s_ref, dst_ref=rs_ref.at[dst_index],
                send_sem=o_send.at[(n + 1) * (k - 1) + n], recv_sem=o_recv.at[(n + 1) * (k - 1) + n],
                device_id=peer, device_id_type=MESH_IDS))
            return out

        def local_copies():
            out = [pltpu.make_async_copy(g_refs[a].at[mi], rg_refs[a].at[mi], local_sems.at[a]) for a in range(n)]
            out.append(pltpu.make_async_copy(s_ref, rs_ref.at[mi], local_sems.at[n]))
            return out

        def to_sibling(t):
            return pltpu.make_async_remote_copy(
                src_ref=sendbuf.at[t], dst_ref=sib_buf.at[t], send_sem=sib_send.at[t], recv_sem=sib_recv.at[t],
                device_id=sibling, device_id_type=MESH_IDS)

        def to_owner(t):
            flip_x = (t < 2) * 1
            flip_y = 1 - (t & 1)
            owner = (me[0] ^ flip_x, me[1] ^ flip_y, me[2])
            return pltpu.make_async_remote_copy(
                src_ref=sendbuf.at[n_chips + t], dst_ref=rwin_ref.at[t], send_sem=win_send.at[t],
                recv_sem=win_recv.at[t], device_id=owner, device_id_type=MESH_IDS)

        own_block = pltpu.make_async_copy(sendbuf.at[last], rwin_ref.at[n_chips - 1], local_sems.at[n + 1])

        @pl.when(jnp.logical_and(b == 0, i == 0))
        def _():
            for cp in local_copies():
                cp.start()
            for k in range(1, N_DEV):
                for cp in other_copies(k, flat_index(flip(me, k)), mi):
                    cp.start()

        contrib = jnp.dot(h_ref[...], dp_ref[...], preferred_element_type=F32)

        @pl.when(i == 0)
        def _():
            acc[...] = contrib

        @pl.when(i > 0)
        def _():
            acc[...] += contrib

        @pl.when(jnp.logical_and(i == n_i - 1, b < n_chips))
        def _():
            sendbuf[b] = acc[...].astype(BF16)
            to_sibling(b).start()

        @pl.when(jnp.logical_and(i == n_i - 1, b >= n_chips))
        def _():
            t = b - n_chips
            to_sibling(t).wait_recv()
            sendbuf[b] = (acc[...] + sib_buf[t].astype(F32)).astype(BF16)

            @pl.when(b < last)
            def _():
                to_owner(t).start()

            @pl.when(b == last)
            def _():
                own_block.start()

        @pl.when(jnp.logical_and(b == last, i == n_i - 1))
        def _():
            for t in range(n_chips - 1):
                to_owner(t).wait_recv()
            for k in range(1, N_DEV):
                pi = flat_index(flip(me, k))
                for cp in other_copies(k, pi, pi):
                    cp.wait_recv()
            for t in range(n_chips):
                to_sibling(t).wait_send()
            for t in range(n_chips - 1):
                to_owner(t).wait_send()
            for k in range(1, N_DEV):
                for cp in other_copies(k, flat_index(flip(me, k)), mi):
                    cp.wait_send()
            for cp in local_copies():
                cp.wait()
            own_block.wait()

    hbm = pl.BlockSpec(memory_space=pl.ANY)
    n_sem = 7 * (n + 1)
    grid_spec = pltpu.PrefetchScalarGridSpec(
        num_scalar_prefetch=1, grid=(N_DEV, n_i),
        in_specs=[pl.BlockSpec((D_MODEL, ts), lambda b, i, order: (0, i)),
                  pl.BlockSpec((None, ts, shard_cols), lambda b, i, order: (order[b], i, 0))] + [hbm] * (n + 1),
        out_specs=[hbm] * (n + 2),
        scratch_shapes=[pltpu.VMEM((D_MODEL, shard_cols), F32), pltpu.VMEM((N_DEV, D_MODEL, shard_cols), BF16),
                        pltpu.VMEM((n_chips, D_MODEL, shard_cols), BF16),
                        pltpu.SemaphoreType.DMA((n_chips,)), pltpu.SemaphoreType.DMA((n_chips,)),
                        pltpu.SemaphoreType.DMA((n_chips - 1,)), pltpu.SemaphoreType.DMA((n_chips - 1,)),
                        pltpu.SemaphoreType.DMA((n_sem,)), pltpu.SemaphoreType.DMA((n_sem,)),
                        pltpu.SemaphoreType.DMA((n + 2,))])
    return pl.pallas_call(
        body, name="in_weight_grad_exchange", grid_spec=grid_spec,
        out_shape=[jax.ShapeDtypeStruct((n_chips, D_MODEL, shard_cols), BF16)]
        + [jax.ShapeDtypeStruct(o.shape, o.dtype) for o in others]
        + [jax.ShapeDtypeStruct((N_DEV,) + small.shape, small.dtype)],
        compiler_params=pltpu.CompilerParams(dimension_semantics=("arbitrary", "arbitrary"),
                                             vmem_limit_bytes=VMEM_LIMIT),
    )(order, h_t, dp_blocks, *others, small)


def ada_weight_grad(c_all, dmod_cols):
    def body(c_ref, d_ref, o_ref):
        cv = c_ref[...]
        o_ref[...] = hdot_tn(cv * sigmoid(cv), d_ref[...])

    return pl.pallas_call(
        body, name="ada_weight_grad",
        out_shape=jax.ShapeDtypeStruct((c_all.shape[1], dmod_cols.shape[1]), F32),
    )(c_all, dmod_cols)


def adamw_update(g, w, m, v):
    nm = ADAM_B1 * m + (1.0 - ADAM_B1) * g
    nv = ADAM_B2 * v + (1.0 - ADAM_B2) * (g * g)
    m_hat = nm / (1.0 - ADAM_B1 ** ADAM_STEP)
    v_hat = nv / (1.0 - ADAM_B2 ** ADAM_STEP)
    return -ADAM_LR * (m_hat / (jnp.sqrt(v_hat) + ADAM_EPS) + ADAM_WD * w), nm, nv


def adamw(parts, w, m, v, name):
    k, rows, cols = parts.shape
    rb = 128 if rows % 128 == 0 else rows

    def body(p_ref, w_ref, m_ref, v_ref, g_ref, d_ref, nm_ref, nv_ref):
        g = p_ref[0].astype(F32)
        for i in range(1, k):
            g = g + p_ref[i].astype(F32)
        g_ref[0] = g
        d_ref[0], nm_ref[0], nv_ref[0] = adamw_update(g, w_ref[0], m_ref[0], v_ref[0])

    blk = pl.BlockSpec((1, rb, cols), lambda i: (0, i, 0))
    return pl.pallas_call(
        body, name=name, grid=(rows // rb,),
        in_specs=[pl.BlockSpec((k, rb, cols), lambda i: (0, i, 0)), blk, blk, blk],
        out_specs=[blk] * 4, out_shape=[jax.ShapeDtypeStruct((1, rows, cols), F32)] * 4,
        compiler_params=pltpu.CompilerParams(dimension_semantics=("arbitrary",), vmem_limit_bytes=VMEM_LIMIT),
    )(parts, w, m, v)


def adamw_small(parts, ws, ms, vs):
    k = parts.shape[0]
    n = len(ws)
    sizes = [w.shape[1] for w in ws]

    def body(p_ref, *refs):
        ins, outs = refs[:3 * n], refs[3 * n:]
        g_all = p_ref[0]
        for i in range(1, k):
            g_all = g_all + p_ref[i]
        off = 0
        for a, size in enumerate(sizes):
            g = g_all[:, off:off + size]
            off += size
            d, nm, nv = adamw_update(g, ins[a][...], ins[n + a][...], ins[2 * n + a][...])
            for kind, val in enumerate((g, d, nm, nv)):
                outs[kind * n + a][...] = val

    return pl.pallas_call(
        body, name="adamw_small",
        out_shape=[jax.ShapeDtypeStruct((1, size), F32) for _ in range(4) for size in sizes],
    )(parts, *ws, *ms, *vs)


def rot_cols(w):
    return jnp.concatenate([-w[:, ROPE // 2:], w[:, :ROPE // 2]], axis=1)


def unrot_cols(dw):
    return jnp.concatenate([dw[:, ROPE // 2:], -dw[:, :ROPE // 2]], axis=1)


def columns_from_shards(g, rows, cols):
    return g.reshape(N_DEV, rows, cols).transpose(1, 0, 2).reshape(rows, N_DEV * cols)


def shards_from_columns(w, rows, cols):
    return w.reshape(rows, N_DEV, cols).transpose(1, 0, 2).reshape(N_DEV, rows * cols)


def permute_w_in(w):
    z = lambda n: jnp.zeros((D_MODEL, n), w.dtype)
    krope = w[:, N_KROPE:N_KROPE + ROPE]
    rw = N_RWKV
    return jnp.concatenate([
        w[:, N_MA:N_MA + 1024], w[:, N_MB:N_MB + 1024],
        w[:, rw:rw + 512], w[:, rw + 512:rw + 1024], w[:, rw + 1024:rw + 1536],
        w[:, N_GPA:N_GPA + 512], w[:, N_GPB:N_GPB + 512],
        w[:, N_QC:N_QC + 256], w[:, N_KVC:N_KVC + 128],
        z(NOPE), krope, z(LANE - QK_DIM), z(NOPE), rot_cols(krope), z(LANE - QK_DIM),
        w[:, rw + 1536:rw + 1664]], axis=1)


def unpermute_w_in_grad(d):
    rw = P_R
    krope = d[:, P_KR + NOPE:P_KR + QK_DIM] + unrot_cols(d[:, P_KRR + NOPE:P_KRR + QK_DIM])
    return jnp.concatenate([
        d[:, P_QC:P_QC + 256], d[:, P_KVC:P_KVC + 128], krope, d[:, P_GPA:P_GPA + 512],
        d[:, rw:rw + 1536], d[:, P_LORA:P_LORA + 128], d[:, P_GPB:P_GPB + 512],
        d[:, P_MA:P_MA + 1024], d[:, P_MB:P_MB + 1024]], axis=1)


def pad_heads_q(w_uq):
    w = w_uq.reshape(Q_RANK, HEADS, QK_DIM)
    zpad = jnp.zeros((Q_RANK, HEADS, LANE - QK_DIM), w.dtype)
    wq = jnp.concatenate([w, zpad], axis=2).reshape(Q_RANK, HEADS * LANE)
    pe = w[:, :, NOPE:]
    rot = jnp.concatenate([-pe[:, :, ROPE // 2:], pe[:, :, :ROPE // 2]], axis=2)
    wqr = jnp.concatenate([jnp.zeros((Q_RANK, HEADS, NOPE), w.dtype), rot, zpad], axis=2).reshape(Q_RANK, HEADS * LANE)
    return wq, wqr


def unpad_heads_q_grad(dwq, dwqr):
    a = dwq.reshape(Q_RANK, HEADS, LANE)
    r = dwqr.reshape(Q_RANK, HEADS, LANE)[:, :, NOPE:QK_DIM]
    pe = a[:, :, NOPE:QK_DIM] + jnp.concatenate([r[:, :, ROPE // 2:], -r[:, :, :ROPE // 2]], axis=2)
    return jnp.concatenate([a[:, :, :NOPE], pe], axis=2).reshape(Q_RANK, HEADS * QK_DIM)


def pad_heads_kv(w_ukv):
    w = w_ukv.reshape(KV_RANK, HEADS, 2 * HEAD)
    z = jnp.zeros((KV_RANK, HEADS, HEAD), w.dtype)
    wkn = jnp.concatenate([w[:, :, :NOPE], z], axis=2).reshape(KV_RANK, HEADS * LANE)
    val = w[:, :, NOPE:]
    odd = (jnp.arange(HEADS) % 2 == 1)[None, :, None]
    wv = jnp.concatenate([jnp.where(odd, 0, val), jnp.where(odd, val, 0)], axis=2).reshape(KV_RANK, HEADS * LANE)
    return wkn, wv


def unpad_heads_kv_grad(dwkn, dwv):
    a = dwkn.reshape(KV_RANK, HEADS, LANE)[:, :, :NOPE]
    b = dwv.reshape(KV_RANK, HEADS, LANE)
    odd = (jnp.arange(HEADS) % 2 == 1)[None, :, None]
    val = jnp.where(odd, b[:, :, HEAD:], b[:, :, :HEAD])
    return jnp.concatenate([a, val], axis=2).reshape(KV_RANK, HEADS * 2 * HEAD)


def kernel(x, c, positions, w_ada, b_ada, w_in, q_norm_g, w_uq, kv_norm_g, w_ukv, mu_rwkv, w0, w_decay_up, a0, w_iclr_up, k_k, k_a, r_k, gn_g, gn_b, w_proj_a, w_proj_b, w_out, post_g, post_b, loss_target, m_w_ada, m_b_ada, m_w_in, m_q_norm_g, m_w_uq, m_kv_norm_g, m_w_ukv, m_mu_rwkv, m_w0, m_w_decay_up, m_a0, m_w_iclr_up, m_k_k, m_k_a, m_r_k, m_gn_g, m_gn_b, m_w_proj_a, m_w_proj_b, m_w_out, m_post_g, m_post_b, v_w_ada, v_b_ada, v_w_in, v_q_norm_g, v_w_uq, v_kv_norm_g, v_w_ukv, v_mu_rwkv, v_w0, v_w_decay_up, v_a0, v_w_iclr_up, v_k_k, v_k_a, v_r_k, v_gn_g, v_gn_b, v_w_proj_a, v_w_proj_b, v_w_out, v_post_g, v_post_b):
    weights = dict(w_ada=w_ada, b_ada=b_ada, w_in=w_in, q_norm_g=q_norm_g, w_uq=w_uq, kv_norm_g=kv_norm_g,
                   w_ukv=w_ukv, mu_rwkv=mu_rwkv, w0=w0, w_decay_up=w_decay_up, a0=a0, w_iclr_up=w_iclr_up,
                   k_k=k_k, k_a=k_a, r_k=r_k, gn_g=gn_g, gn_b=gn_b, w_proj_a=w_proj_a, w_proj_b=w_proj_b,
                   w_out=w_out, post_g=post_g, post_b=post_b)
    mom1 = dict(w_ada=m_w_ada, b_ada=m_b_ada, w_in=m_w_in, q_norm_g=m_q_norm_g, w_uq=m_w_uq, kv_norm_g=m_kv_norm_g,
                w_ukv=m_w_ukv, mu_rwkv=m_mu_rwkv, w0=m_w0, w_decay_up=m_w_decay_up, a0=m_a0, w_iclr_up=m_w_iclr_up,
                k_k=m_k_k, k_a=m_k_a, r_k=m_r_k, gn_g=m_gn_g, gn_b=m_gn_b, w_proj_a=m_w_proj_a, w_proj_b=m_w_proj_b,
                w_out=m_w_out, post_g=m_post_g, post_b=m_post_b)
    mom2 = dict(w_ada=v_w_ada, b_ada=v_b_ada, w_in=v_w_in, q_norm_g=v_q_norm_g, w_uq=v_w_uq, kv_norm_g=v_kv_norm_g,
                w_ukv=v_w_ukv, mu_rwkv=v_mu_rwkv, w0=v_w0, w_decay_up=v_w_decay_up, a0=v_a0, w_iclr_up=v_w_iclr_up,
                k_k=v_k_k, k_a=v_k_a, r_k=v_r_k, gn_g=v_gn_g, gn_b=v_gn_b, w_proj_a=v_w_proj_a, w_proj_b=v_w_proj_b,
                w_out=v_w_out, post_g=v_post_g, post_b=v_post_b)
    names = list(weights)
    n_rows = x.shape[1]
    me = 4 * lax.axis_index("x") + 2 * lax.axis_index("y") + lax.axis_index("c")
    xr = x[0]
    tgt = loss_target[0]
    row = lambda a: a.reshape(1, -1)

    gathered = gather_shards([weights[n][0].astype(BF16) for n, _, _ in SHARDED] + [c])
    c_all = gathered[-1].reshape(N_DEV, D_MODEL)
    full = {}
    for (n, r, cdim), part in zip(SHARDED, gathered):
        full[n] = part.reshape(N_DEV * r, cdim) if n == "w_out" else columns_from_shards(part, r, cdim)
    w_in_p = permute_w_in(full["w_in"])
    wq, wqr = pad_heads_q(full["w_uq"])
    wkn, wv = pad_heads_kv(full["w_ukv"])
    zl = jnp.zeros((LORA, WIDTH), BF16)
    w_dec = jnp.concatenate([full["w_decay_up"], zl], axis=0)
    w_iclr = jnp.concatenate([zl, full["w_iclr_up"]], axis=0)
    wpa, wpb, wout = full["w_proj_a"], full["w_proj_b"], full["w_out"]

    mod_all = ada_modulation(c_all, w_ada[0], b_ada.reshape(N_DEV, -1))
    mod = lax.dynamic_index_in_dim(mod_all, me, axis=1, keepdims=False).reshape(3, D_MODEL)

    (proj,) = row_call("fwd_in", fwd_in_tile, n_rows, [(xr, D_MODEL, 0)], [mod, w_in_p], [(P_WIDTH, F32)])
    pcol = lambda off_, w: (proj, w, off_ // w)

    inv_freq = ROPE_THETA ** (-jnp.arange(0, ROPE, 2, dtype=F32) / ROPE)
    ang = positions[0].astype(F32)[:, None] * inv_freq
    ones_n, zeros_n, zeros_p = jnp.ones((n_rows, NOPE), F32), jnp.zeros((n_rows, NOPE), F32), jnp.zeros((n_rows, LANE - QK_DIM), F32)
    cos_t = jnp.concatenate([ones_n, jnp.cos(ang), jnp.cos(ang), zeros_p], axis=1)
    sin_t = jnp.concatenate([zeros_n, jnp.sin(ang), jnp.sin(ang), zeros_p], axis=1)

    gq, gkv = q_norm_g, kv_norm_g
    mla_consts = [gq, gkv, wq, wqr, wkn, wv]
    q, k, v = row_call(
        "mla_prep", mla_prep_tile, n_rows,
        [pcol(P_QC, 256), pcol(P_KVC, 128), pcol(P_KR, 128), pcol(P_KRR, 128), (cos_t, LANE, 0), (sin_t, LANE, 0)],
        mla_consts, [(HEADS * LANE, BF16)] * 3)
    ya, lse = attention_forward(q, k, v)

    t_idx = jnp.arange(ROW_TILE)
    same_chunk = (t_idx[:, None] // CHUNK) == (t_idx[None, :] // CHUNK)
    same = same_chunk.astype(F32)
    tril = (same_chunk & (t_idx[:, None] >= t_idx[None, :])).astype(F32)
    l_idx = jnp.arange(LANE)
    bd = ((l_idx[:, None] // HEAD) == (l_idx[None, :] // HEAD)).astype(F32)
    mu = mu_rwkv
    mu_r, mu_k, mu_v, mu_l = mu[:, 0:512], mu[:, 512:1024], mu[:, 1024:1536], mu[:, 1536:1664]
    rk_row = row(r_k)
    rwkv_consts = [mu_r, mu_k, mu_v, mu_l, w0, a0, k_k, k_a, w_dec, w_iclr, tril, same, bd]
    rwkv_rows = [pcol(P_R, 512), pcol(P_K, 512), pcol(P_V, 512), pcol(P_LORA, 128)]
    rt, at, bt, kt, clf, uv, ur, k2 = row_call(
        "rwkv_prep", rwkv_prep_tile, n_rows, rwkv_rows, rwkv_consts, [(WIDTH, F32)] * 8, halo_in=rwkv_rows)
    y, m0s, state_maps, out_maps, *wkv_saved = wkv_forward(at, bt, kt, rt, uv, clf)

    tail = row_call(
        "tail", tail_tile, n_rows,
        [(xr, D_MODEL, 0), (tgt, D_MODEL, 0), pcol(P_MA, 1024), pcol(P_MB, 1024), pcol(P_GPA, 512), pcol(P_GPB, 512),
         (ya, WIDTH, 0), (y, WIDTH, 0), (ur, WIDTH, 0), (k2, WIDTH, 0), (uv, WIDTH, 0)],
        [mod, wpa, wpb, wout, gn_g, gn_b, rk_row, post_g, post_b, bd],
        [(D_MODEL, F32), (1024, BF16), (1024, BF16), (512, BF16), (512, BF16), (WIDTH, F32), (WIDTH, F32), (WIDTH, F32)],
        acc_out=[((1, LANE), F32), ((D_MODEL, D_MODEL), F32), ((WIDTH, D_MODEL), F32), ((WIDTH, D_MODEL), F32),
                 ((1, WIDTH), F32), ((1, WIDTH), F32), ((1, D_MODEL), F32), ((1, D_MODEL), F32), ((1, D_MODEL), F32)])
    (dz, dma, dmb, dgpa, dgpb, dya, dy, dyb,
     loss_row, g_wout, g_wpa, g_wpb, g_gn_g, g_gn_b, g_post_g, g_post_b, dgate) = tail

    dq, dk, dv = attention_backward(q, k, v, ya, dya, lse)
    dq_c, dkv_c, dkr, dkrr, g_wq, g_wqr, g_wkn, g_wv, g_gq, g_gkv = row_call(
        "mla_prep_bwd", mla_prep_bwd_tile, n_rows,
        [pcol(P_QC, 256), pcol(P_KVC, 128), (cos_t, LANE, 0), (sin_t, LANE, 0),
         (dq, HEADS * LANE, 0), (dk, HEADS * LANE, 0), (dv, HEADS * LANE, 0)],
        mla_consts, [(256, BF16), (128, BF16), (128, BF16), (128, BF16)],
        acc_out=[((Q_RANK, HEADS * LANE), F32)] * 2 + [((KV_RANK, HEADS * LANE), F32)] * 2
        + [((1, Q_RANK), F32), ((1, KV_RANK), F32)])

    dat, dbt, dkt, drt, dvv, dlw = wkv_backward(at, bt, kt, rt, uv, clf, m0s, state_maps, out_maps, wkv_saved, dy)
    (dr0, dk0, dv0, dl0, g_mu_r, g_mu_k, g_mu_v, g_mu_l, g_w0, g_a0, g_k_k, g_k_a, g_r_k, g_wdec, g_wiclr) = row_call(
        "rwkv_prep_bwd", rwkv_prep_bwd_tile, n_rows,
        rwkv_rows + [(drt, WIDTH, 0), (dat, WIDTH, 0), (dbt, WIDTH, 0), (dkt, WIDTH, 0), (dvv, WIDTH, 0),
                     (dlw, WIDTH, 0), (dyb, WIDTH, 0)],
        rwkv_consts + [rk_row], [(512, BF16), (512, BF16), (512, BF16), (128, BF16)],
        acc_out=[((1, 512), F32)] * 3 + [((1, 128), F32)] + [((1, 512), F32)] * 5 + [((LANE, WIDTH), F32)] * 2,
        halo_in=rwkv_rows, carry=[512, 512, 512, 128], reverse=True)

    li = jnp.arange(LANE)
    src, dst = li[:, None], li[None, :]
    half = ROPE // 2
    unrot = (jnp.where((dst >= NOPE) & (dst < NOPE + half) & (src == dst + half), 1.0, 0.0)
             - jnp.where((dst >= NOPE + half) & (dst < QK_DIM) & (src == dst - half), 1.0, 0.0)).astype(BF16)
    dx, h_t, dproj_blocks, dshift, dscale = in_backward(
        xr, dz, [dma, dmb, dr0, dk0, dv0, dgpa, dgpb, dq_c, dkv_c, dkr, dkrr, dl0], mod, w_in_p, unrot)

    grads_full = {
        "w_uq": unpad_heads_q_grad(g_wq, g_wqr), "w_ukv": unpad_heads_kv_grad(g_wkn, g_wv),
        "w_decay_up": g_wdec[:LORA], "w_iclr_up": g_wiclr[LORA:],
        "w_proj_a": g_wpa, "w_proj_b": g_wpb, "w_out": g_wout}
    blocks = [(grads_full[n].reshape(N_DEV, r, cdim) if n == "w_out"
               else grads_full[n].reshape(r, N_DEV, cdim).transpose(1, 0, 2)).astype(BF16) for n, r, cdim in SHARDED[1:]]
    dmod = jnp.concatenate([dshift, dscale, dgate], axis=1)
    small = jnp.concatenate([dmod, g_gq, g_gkv, g_mu_r, g_mu_k, g_mu_v, g_mu_l, g_w0, g_a0, g_k_k, g_k_a, g_r_k,
                             g_gn_g, g_gn_b, g_post_g, g_post_b, loss_row], axis=1)
    my_x, my_y, my_c = lax.axis_index("x"), lax.axis_index("y"), lax.axis_index("c")
    chip_order = [4 * (my_x ^ fx) + 2 * (my_y ^ fy) for fx, fy in ((1, 1), (1, 0), (0, 1), (0, 0))]
    order = jnp.stack([ch + (1 - my_c) for ch in chip_order] + [ch + my_c for ch in chip_order]).astype(jnp.int32)
    *got_blocks, got_small = in_weight_grad_exchange(h_t, dproj_blocks, blocks, small, order)
    loss = jnp.sum(got_small[:, 0, SMALL_ELEMS])

    ada_cols = w_ada.shape[2]
    dmod_all = got_small[:, 0, :3 * D_MODEL]
    g_ada = ada_weight_grad(c_all, lax.dynamic_slice_in_dim(dmod_all, me * ada_cols, ada_cols, axis=1))

    outs = [dict() for _ in range(4)]
    res = adamw(g_ada[None], w_ada, m_w_ada, v_w_ada, "adamw_w_ada")
    for kind in range(4):
        outs[kind]["w_ada"] = res[kind]
    for (n, r, cdim), got in zip(SHARDED, got_blocks):
        res = adamw(got, weights[n], mom1[n], mom2[n], "adamw_" + n)
        for kind in range(4):
            outs[kind][n] = res[kind]
    rows_of = lambda tree: [tree[n].reshape(1, -1) for n, _ in SMALL]
    res = adamw_small(got_small, rows_of(weights), rows_of(mom1), rows_of(mom2))
    for kind in range(4):
        for a, (n, _) in enumerate(SMALL):
            outs[kind][n] = res[kind * len(SMALL) + a].reshape(weights[n].shape)
    return (loss, dx[None], *[outs[0][n] for n in names], *[outs[1][n] for n in names],
            *[outs[2][n] for n in names], *[outs[3][n] for n in names])
```

```python
import functools
import math

import jax
import jax.numpy as jnp
from jax import lax
from jax.experimental import pallas as pl
from jax.experimental.pallas import tpu as pltpu

F32 = jnp.float32
BF16 = jnp.bfloat16
HIGHEST = lax.Precision.HIGHEST
MESH_IDS = pl.DeviceIdType.MESH

N_DEV = 8
D_MODEL = 1024
LN_EPS = 1e-5
RMS_EPS = 1e-6
GN_EPS = 64e-5
HEADS = 8
Q_RANK = 256
KV_RANK = 128
ROPE = 32
NOPE = 64
QK_DIM = NOPE + ROPE
WIDTH = 512
HEAD = 64
LORA = 64
CHUNK = 64
DEPTH = 1
ALPHA = (2.0 * DEPTH) ** 0.25
ROPE_THETA = 10000.0
ATTN_SCALE = QK_DIM ** -0.5
DECAY_SCALE = math.exp(-0.5)

ADAM_LR = 0.001
ADAM_B1 = 0.9
ADAM_B2 = 0.999
ADAM_EPS = 1e-08
ADAM_WD = 0.01
ADAM_STEP = 10

LANE = 128
PAIR = 2 * HEAD
ROW_TILE = 256
ATTN_FWD_TILES = (512, 1024)
ATTN_BWD_TILES = (512, 512)
LOG2_E = math.log2(math.e)
Q_PRESCALE = ATTN_SCALE * LOG2_E
WKV_CHUNKS_PER_STEP = 8
VMEM_LIMIT = 56 * 1024 * 1024

P_MA, P_MB, P_R, P_K, P_V, P_GPA, P_GPB, P_QC, P_KVC, P_KR, P_KRR, P_LORA = (
    0, 1024, 2048, 2560, 3072, 3584, 4096, 4608, 4864, 4992, 5120, 5248)
P_WIDTH = 5376
DW_BLOCK = 768

N_QC, N_KVC, N_KROPE, N_GPA, N_RWKV, N_GPB, N_MA, N_MB = 0, 256, 384, 416, 928, 2592, 3104, 4128
IN_WIDTH = 5152

SHARDED = (("w_in", 1024, 644), ("w_uq", 256, 96), ("w_ukv", 128, 128), ("w_decay_up", 64, 64),
           ("w_iclr_up", 64, 64), ("w_proj_a", 512, 128), ("w_proj_b", 512, 128), ("w_out", 128, 1024))
SHARD_ELEMS = sum(r * c for _, r, c in SHARDED)
SHARD_ROWS = SHARD_ELEMS // LANE
GATHER_ROWS = SHARD_ROWS + 2 * D_MODEL // LANE
SMALL = (("b_ada", 3072), ("q_norm_g", 256), ("kv_norm_g", 128), ("mu_rwkv", 1664), ("w0", 512), ("a0", 512),
         ("k_k", 512), ("k_a", 512), ("r_k", 512), ("gn_g", 512), ("gn_b", 512), ("post_g", 1024), ("post_b", 1024))
SMALL_ELEMS = sum(n for _, n in SMALL)
SMALL_ROWS = SMALL_ELEMS // LANE


def mm(a, b):
    return jnp.dot(a.astype(BF16), b.astype(BF16), preferred_element_type=F32)


def mm_nt(a, b):
    return lax.dot_general(a.astype(BF16), b.astype(BF16), (((1,), (1,)), ((), ())), preferred_element_type=F32)


def mm_tn(a, b):
    return lax.dot_general(a.astype(BF16), b.astype(BF16), (((0,), (0,)), ((), ())), preferred_element_type=F32)


def hdot(a, b):
    return jnp.dot(a, b, precision=HIGHEST, preferred_element_type=F32)


def hdot_nt(a, b):
    return lax.dot_general(a, b, (((1,), (1,)), ((), ())), precision=HIGHEST, preferred_element_type=F32)


def hdot_tn(a, b):
    return lax.dot_general(a, b, (((0,), (0,)), ((), ())), precision=HIGHEST, preferred_element_type=F32)


def sigmoid(x):
    return 1.0 / (1.0 + jnp.exp(-x))


def colsum(x):
    return jnp.sum(x, axis=0, keepdims=True)


def rowmean(x):
    return jnp.mean(x, axis=-1, keepdims=True)


def layer_norm_stats(x):
    xc = x - rowmean(x)
    rstd = lax.rsqrt(rowmean(xc * xc) + LN_EPS)
    return xc * rstd, rstd


def layer_norm_bwd(dy, xhat, rstd):
    return rstd * (dy - rowmean(dy) - xhat * rowmean(dy * xhat))


def bf16_pieces(x, n):
    pieces = []
    for _ in range(n):
        p = x.astype(BF16)
        pieces.append(p)
        x = x - p.astype(F32)
    return pieces


def ones_dot(ones, x, n_pieces):
    ones = ones.astype(BF16)
    return sum(jnp.dot(ones, p, preferred_element_type=F32) for p in bf16_pieces(x, n_pieces))


def head_sum(x, bd):
    bd = bd.astype(BF16)
    out = []
    for p in range(x.shape[1] // LANE):
        hi, lo = bf16_pieces(x[:, p * LANE:(p + 1) * LANE], 2)
        out.append(jnp.dot(hi, bd, preferred_element_type=F32) + jnp.dot(lo, bd, preferred_element_type=F32))
    return jnp.concatenate(out, axis=1)


def tile_lanes(t, n):
    return jnp.concatenate([t] * n, axis=1)


def row_iota(shape):
    return lax.broadcasted_iota(jnp.int32, shape, 0)


def lane_iota(shape):
    return lax.broadcasted_iota(jnp.int32, shape, 1)


def shift_rows_down(x, row0):
    rolled = pltpu.roll(x, 1, axis=0)
    return jnp.where(row_iota(x.shape) == 0, row0, rolled)


def shift_rows_up(x, row_last):
    rolled = pltpu.roll(x, x.shape[0] - 1, axis=0)
    return jnp.where(row_iota(x.shape) == x.shape[0] - 1, row_last, rolled)


def row_call(name, fn, n_rows, row_in, const_in, row_out, acc_out=(), halo_in=(), carry=(), reverse=False):
    ts = ROW_TILE
    n_tiles = n_rows // ts
    n_in = len(row_in) + len(halo_in) + len(const_in)
    n_ro, n_ao = len(row_out), len(acc_out)

    def tile_of(g):
        return (n_tiles - 1 - g) if reverse else g

    def body(*refs):
        ins = refs[:n_in]
        ro = refs[n_in:n_in + n_ro]
        ao = refs[n_in + n_ro:n_in + n_ro + n_ao]
        cr = refs[n_in + n_ro + n_ao:]
        g = pl.program_id(0)
        step0 = g == 0
        tile0 = tile_of(g) == 0
        for r in cr:
            @pl.when(step0)
            def _(r=r):
                r[...] = jnp.zeros_like(r)
        vals = [r[...] for r in ins]
        outs = fn(step0, tile0, *vals, *[c[0:1, :] for c in cr])
        for r, v in zip(ro, outs[:n_ro]):
            r[...] = v.astype(r.dtype)
        for r, v in zip(ao, outs[n_ro:n_ro + n_ao]):
            @pl.when(step0)
            def _(r=r, v=v):
                r[...] = v.astype(r.dtype)

            @pl.when(jnp.logical_not(step0))
            def _(r=r, v=v):
                r[...] += v.astype(r.dtype)
        for r, v in zip(cr, outs[n_ro + n_ao:]):
            r[0:1, :] = v

    in_specs = [pl.BlockSpec((ts, w), functools.partial(lambda g, cb: (tile_of(g), cb), cb=cb)) for _, w, cb in row_in]
    in_specs += [pl.BlockSpec((8, w), functools.partial(
        lambda g, cb: (jnp.maximum(tile_of(g) * (ts // 8) - 1, 0), cb), cb=cb)) for _, w, cb in halo_in]
    in_specs += [pl.BlockSpec(memory_space=pltpu.VMEM) for _ in const_in]
    out_specs = [pl.BlockSpec((ts, w), lambda g: (tile_of(g), 0)) for w, _ in row_out]
    out_specs += [pl.BlockSpec(s, lambda g: (0, 0)) for s, _ in acc_out]
    out_shape = [jax.ShapeDtypeStruct((n_rows, w), d) for w, d in row_out]
    out_shape += [jax.ShapeDtypeStruct(s, d) for s, d in acc_out]
    return pl.pallas_call(
        body, name=name, grid=(n_tiles,), in_specs=in_specs, out_specs=out_specs, out_shape=out_shape,
        scratch_shapes=[pltpu.VMEM((8, w), F32) for w in carry],
        compiler_params=pltpu.CompilerParams(dimension_semantics=("arbitrary",), vmem_limit_bytes=VMEM_LIMIT),
    )(*[a for a, _, _ in row_in], *[a for a, _, _ in halo_in], *const_in)


def my_position():
    return lax.axis_index("x"), lax.axis_index("y"), lax.axis_index("c")


def flip(pos, k):
    x, y, c = pos
    dx, dy, dc = (k >> 2) & 1, (k >> 1) & 1, k & 1
    return (1 - x if dx else x, 1 - y if dy else y, 1 - c if dc else c)


def flat_index(pos):
    return 4 * pos[0] + 2 * pos[1] + pos[2]


def gather_shards(shards):
    n = len(shards)

    def body(*refs):
        x_refs, out_refs = refs[:n], refs[n:2 * n]
        send_sems, recv_sems, local_sems = refs[2 * n:]
        x, y, c = my_position()
        me, sibling = (x, y, c), (x, y, 1 - c)
        chips = [(1 - x, y), (x, 1 - y), (1 - x, 1 - y)]

        def copy(a, k, block, to, from_input=False):
            slot = out_refs[a].at[flat_index(block)]
            return pltpu.make_async_remote_copy(
                src_ref=x_refs[a] if from_input else slot, dst_ref=slot,
                send_sem=send_sems.at[7 * a + k], recv_sem=recv_sems.at[7 * a + k],
                device_id=to, device_id_type=MESH_IDS)

        mine = [pltpu.make_async_copy(x_refs[a], out_refs[a].at[flat_index(me)], local_sems.at[a]) for a in range(n)]
        for cp in mine:
            cp.start()
        first = []
        for a in range(n):
            first.append(copy(a, 0, me, sibling, from_input=True))
            first += [copy(a, 1 + j, me, (*chip, c), from_input=True) for j, chip in enumerate(chips)]
        for cp in first:
            cp.start()
        passed = []
        for j, chip in enumerate(chips):
            for a in range(n):
                copy(a, 1 + j, (*chip, c), me).wait_recv()
                cp = copy(a, 4 + j, (*chip, c), sibling)
                cp.start()
                passed.append(cp)
        for a in range(n):
            copy(a, 0, sibling, me).wait_recv()
            for j, chip in enumerate(chips):
                copy(a, 4 + j, (*chip, 1 - c), me).wait_recv()
        for cp in first + passed:
            cp.wait_send()
        for cp in mine:
            cp.wait()

    return pl.pallas_call(
        body, name="gather_shards",
        out_shape=[jax.ShapeDtypeStruct((N_DEV,) + s.shape, s.dtype) for s in shards],
        in_specs=[pl.BlockSpec(memory_space=pl.ANY)] * n, out_specs=[pl.BlockSpec(memory_space=pl.ANY)] * n,
        scratch_shapes=[pltpu.SemaphoreType.DMA((7 * n,)), pltpu.SemaphoreType.DMA((7 * n,)),
                        pltpu.SemaphoreType.DMA((n,))],
    )(*shards)


def ada_modulation(c_all, w_ada_loc, b_ada_blocks):
    cols = w_ada_loc.shape[1]

    def body(c_ref, w_ref, b_ref, out_ref, send_sems, recv_sems):
        me = my_position()
        mi = flat_index(me)
        cv = c_ref[...]
        res = hdot(cv * sigmoid(cv), w_ref[...]) + b_ref[pl.ds(mi, 1), :]
        out_ref[mi] = res
        sends = []
        for k in range(1, N_DEV):
            cp = pltpu.make_async_remote_copy(
                src_ref=out_ref.at[mi], dst_ref=out_ref.at[mi], send_sem=send_sems.at[k - 1],
                recv_sem=recv_sems.at[k - 1], device_id=flip(me, k), device_id_type=MESH_IDS)
            cp.start()
            sends.append(cp)
        for k in range(1, N_DEV):
            pi = flat_index(flip(me, k))
            pltpu.make_async_remote_copy(
                src_ref=out_ref.at[pi], dst_ref=out_ref.at[pi], send_sem=send_sems.at[k - 1],
                recv_sem=recv_sems.at[k - 1], device_id=flip(me, k), device_id_type=MESH_IDS).wait_recv()
        for cp in sends:
            cp.wait_send()

    return pl.pallas_call(
        body, name="ada_modulation",
        out_shape=jax.ShapeDtypeStruct((N_DEV, N_DEV, cols), F32),
        in_specs=[pl.BlockSpec(memory_space=pltpu.VMEM)] * 3, out_specs=pl.BlockSpec(memory_space=pltpu.VMEM),
        scratch_shapes=[pltpu.SemaphoreType.DMA((7,)), pltpu.SemaphoreType.DMA((7,))],
    )(c_all, w_ada_loc, b_ada_blocks)


def fwd_in_tile(step0, tile0, x, mod, w_in_p):
    xhat, _ = layer_norm_stats(x)
    h = xhat * (1.0 + mod[1:2]) + mod[0:1]
    return (mm(h, w_in_p),)


def fwd_in_gather(x, mod, w_in_p, shards):
    n = len(shards)
    n_rows = x.shape[0]
    ts = ROW_TILE
    n_tiles = n_rows // ts

    def body(x_ref, mod_ref, w_ref, *rest):
        s_refs = rest[:n]
        proj_ref, out_refs = rest[n], rest[n + 1:2 * n + 1]
        send_sems, recv_sems, local_sems = rest[2 * n + 1:]
        g = pl.program_id(0)
        me = my_position()
        mi = flat_index(me)

        def copies(k, slot):
            return [pltpu.make_async_remote_copy(
                src_ref=s_refs[a], dst_ref=out_refs[a].at[slot], send_sem=send_sems.at[7 * a + k - 1],
                recv_sem=recv_sems.at[7 * a + k - 1], device_id=flip(me, k), device_id_type=MESH_IDS)
                for a in range(n)]

        local = [pltpu.make_async_copy(s_refs[a], out_refs[a].at[mi], local_sems.at[a]) for a in range(n)]

        @pl.when(g == 0)
        def _():
            for cp in local:
                cp.start()
            for k in range(1, N_DEV):
                for cp in copies(k, mi):
                    cp.start()

        proj_ref[...] = fwd_in_tile(None, None, x_ref[...], mod_ref[...], w_ref[...])[0]

        @pl.when(g == n_tiles - 1)
        def _():
            for k in range(1, N_DEV):
                for cp in copies(k, flat_index(flip(me, k))):
                    cp.wait_recv()
            for k in range(1, N_DEV):
                for cp in copies(k, mi):
                    cp.wait_send()
            for cp in local:
                cp.wait()

    hbm = pl.BlockSpec(memory_space=pl.ANY)
    const = pl.BlockSpec(memory_space=pltpu.VMEM)
    return pl.pallas_call(
        body, name="fwd_in_gather", grid=(n_tiles,),
        in_specs=[pl.BlockSpec((ts, D_MODEL), lambda g: (g, 0)), const, const] + [hbm] * n,
        out_specs=[pl.BlockSpec((ts, P_WIDTH), lambda g: (g, 0))] + [hbm] * n,
        out_shape=[jax.ShapeDtypeStruct((n_rows, P_WIDTH), F32)]
        + [jax.ShapeDtypeStruct((N_DEV,) + s.shape, s.dtype) for s in shards],
        scratch_shapes=[pltpu.SemaphoreType.DMA((7 * n,)), pltpu.SemaphoreType.DMA((7 * n,)),
                        pltpu.SemaphoreType.DMA((n,))],
        compiler_params=pltpu.CompilerParams(dimension_semantics=("arbitrary",), vmem_limit_bytes=VMEM_LIMIT),
    )(x, mod, w_in_p, *shards)


def rms_norm_fwd(x, g):
    r = lax.rsqrt(rowmean(x * x) + RMS_EPS)
    xh = x * r
    return xh * g, xh, r


def key_rope_mask(shape):
    return (lane_iota(shape) >= NOPE).astype(F32)


def mla_prep_tile(step0, tile0, q_c, kv_c, kr, krr, cos, sin, gq, gkv, wq, wqr, wkn, wv):
    qn, _, _ = rms_norm_fwd(q_c, gq)
    kvn, _, _ = rms_norm_fwd(kv_c, gkv)
    q = (mm(qn, wq) * tile_lanes(cos, HEADS) + mm(qn, wqr) * tile_lanes(sin, HEADS)) * Q_PRESCALE
    kpe = kr * (cos * key_rope_mask(cos.shape)) + krr * sin
    k = mm(kvn, wkn) + tile_lanes(kpe, HEADS)
    v = mm(kvn, wv)
    return q, k, v


def rwkv_prep_core(tile0, r0, k0, v0, l0, hr, hk, hv, hl, mu_r, mu_k, mu_v, mu_l, w0, a0, k_k, k_a,
                   w_dec, w_iclr, tril, same, bd):
    def shifted(x, halo, mu):
        row0 = jnp.where(tile0, 0.0, halo[7:8, :])
        prev = shift_rows_down(x, row0)
        return x + (prev - x) * mu, prev

    ur, pr = shifted(r0, hr, mu_r)
    uk, pk = shifted(k0, hk, mu_k)
    uv, pv = shifted(v0, hv, mu_v)
    ul, plo = shifted(l0, hl, mu_l)
    th = jnp.tanh(ul)
    sg = sigmoid(w0 + mm(th, w_dec))
    lw = -DECAY_SCALE * sg
    a_ic = sigmoid(a0 + mm(ul, w_iclr))
    kkraw = uk * k_k
    nrm_raw = jnp.sqrt(head_sum(kkraw * kkraw, bd))
    nrm = jnp.maximum(nrm_raw, 1e-12)
    kk = kkraw / nrm
    k2 = uk * (1.0 + (a_ic - 1.0) * k_a)
    lc = ones_dot(tril, lw, 3)
    lcl = ones_dot(same, lw, 3)
    return dict(ur=ur, uk=uk, uv=uv, ul=ul, pr=pr, pk=pk, pv=pv, pl=plo, th=th, sg=sg, lw=lw, a_ic=a_ic,
                kkraw=kkraw, nrm_raw=nrm_raw, nrm=nrm, kk=kk, k2=k2, lc=lc, lcl=lcl)


def rwkv_prep_tile(step0, tile0, r0, k0, v0, l0, hr, hk, hv, hl, *consts):
    f = rwkv_prep_core(tile0, r0, k0, v0, l0, hr, hk, hv, hl, *consts)
    lc, lw = f["lc"], f["lw"]
    e_neg = jnp.exp(-lc)
    rt = f["ur"] * jnp.exp(lc)
    at = -f["kk"] * jnp.exp(lc - lw)
    bt = f["kk"] * f["a_ic"] * e_neg
    kt = f["k2"] * e_neg
    return rt, at, bt, kt, jnp.exp(f["lcl"]), f["uv"], f["ur"], f["k2"]


def wkv_masks():
    lane = lane_iota((1, PAIR))
    m_lo = (lane < HEAD).astype(F32)
    ri = row_iota((CHUNK, CHUNK))
    ci = lane_iota((CHUNK, CHUNK))
    r2 = row_iota((PAIR, PAIR))
    c2 = lane_iota((PAIR, PAIR))
    bd = ((r2 < HEAD) == (c2 < HEAD)).astype(F32)
    eye2 = (r2 == c2).astype(F32)
    return (m_lo, 1.0 - m_lo), ri > ci, ri >= ci, (ri == ci).astype(F32), bd, eye2


def wkv_chunks_pre(chunks, masks):
    ms, strict, incl, eye, bd, eye2 = masks
    items = [(c, m) for c in range(len(chunks)) for m in ms]
    at, bt, kt, rt, v, cl = (list(t) for t in zip(*chunks))
    atm = [at[c] * m for c, m in items]
    rtm = [rt[c] * m for c, m in items]
    aab = [jnp.where(strict, mm_nt(x, bt[c]), 0.0) for x, (c, _) in zip(atm, items)]
    aak = [jnp.where(strict, mm_nt(x, kt[c]), 0.0) for x, (c, _) in zip(atm, items)]
    prb = [jnp.where(incl, mm_nt(x, bt[c]), 0.0) for x, (c, _) in zip(rtm, items)]
    prk = [jnp.where(incl, mm_nt(x, kt[c]), 0.0) for x, (c, _) in zip(rtm, items)]
    tinv = [eye + a for a in aab]
    power = aab
    for _ in range(5):
        power = [mm(p, p) for p in power]
        tinv = [t + mm(t, p) for t, p in zip(tinv, power)]

    def by_chunk(parts):
        return [parts[2 * c] + parts[2 * c + 1] for c in range(len(chunks))]

    w = by_chunk([mm(a, v[c] * m) for a, (c, m) in zip(aak, items)])
    ah = by_chunk([mm(t, x) for t, x in zip(tinv, atm)])
    wh = by_chunk([mm(t, w[c] * m) for t, (c, m) in zip(tinv, items)])
    rh = [r + d for r, d in zip(rt, by_chunk([mm(p, ah[c] * m) for p, (c, m) in zip(prb, items)]))]
    yh = by_chunk([mm(p, wh[c] * m) + mm(q, v[c] * m) for p, q, (c, m) in zip(prb, prk, items)])
    bc = [b * c_ for b, c_ in zip(bt, cl)]
    kc = [k * c_ for k, c_ in zip(kt, cl)]
    g = [eye2 * c_ + bd * mm_tn(b, a) for c_, b, a in zip(cl, bc, ah)]
    h = [bd * (mm_tn(b, w_) + mm_tn(k, v_)) for b, w_, k, v_ in zip(bc, wh, kc, v)]
    side = lambda parts: [jnp.concatenate([parts[2 * c], parts[2 * c + 1]], axis=1).astype(BF16)
                          for c in range(len(chunks))]
    saved = (side(tinv), side(aak), side(prb), side(prk), [a.astype(BF16) for a in ah], wh)
    return g, h, rh, yh, saved


def wkv_chunks_grad(chunks, saved, m0, dy, dm1, masks):
    ms, strict, incl, eye, bd, eye2 = masks
    n = len(chunks)
    at, bt, kt, rt, v, cl = (list(t) for t in zip(*chunks))
    items = [(c, m) for c in range(n) for m in ms]
    atm = [at[c] * m for c, m in items]
    rtm = [rt[c] * m for c, m in items]
    halves = lambda pairs: [x for pr in pairs for x in (pr[:, :CHUNK], pr[:, CHUNK:])]
    tinv, aak, prb, prk = (halves(s) for s in zip(*[(a, b, c_, d) for a, b, c_, d, _, _ in saved]))
    ah = [s[4] for s in saved]
    wh = [s[5] for s in saved]
    bc = [b * c_ for b, c_ in zip(bt, cl)]
    kc = [k * c_ for k, c_ in zip(kt, cl)]

    def by_chunk(parts):
        return [parts[2 * c] + parts[2 * c + 1] for c in range(n)]

    u = [mm(a, m) + w for a, m, w in zip(ah, m0, wh)]
    dm1 = [d * bd for d in dm1]
    dym = [dy[c] * m for c, m in items]
    du = [mm(b, d) + e for b, d, e in zip(bc, dm1, by_chunk([mm_tn(p, x) for p, x in zip(prb, dym)]))]
    dv = [mm(k, d) + e for k, d, e in zip(kc, dm1, by_chunk([mm_tn(p, x) for p, x in zip(prk, dym)]))]
    dz = by_chunk([mm_tn(t, du[c] * m) for t, (c, m) in zip(tinv, items)])
    dzm = [dz[c] * m for c, m in items]
    dv = [a + b for a, b in zip(dv, by_chunk([mm_tn(a_, x) for a_, x in zip(aak, dzm)]))]
    drt = [mm_nt(d, m) for d, m in zip(dy, m0)]
    dat = [mm_nt(d, m) for d, m in zip(dz, m0)]
    udm = [mm_nt(x, d) for x, d in zip(u, dm1)]
    vdm = [mm_nt(x, d) for x, d in zip(v, dm1)]
    daab = [jnp.where(strict, mm_nt(x, u[c]), 0.0) for x, (c, _) in zip(dzm, items)]
    daak = [jnp.where(strict, mm_nt(x, v[c]), 0.0) for x, (c, _) in zip(dzm, items)]
    dprb = [jnp.where(incl, mm_nt(x, u[c]), 0.0) for x, (c, _) in zip(dym, items)]
    dprk = [jnp.where(incl, mm_nt(x, v[c]), 0.0) for x, (c, _) in zip(dym, items)]
    drt2 = by_chunk([(mm(p, bt[c]) + mm(q, kt[c])) * m for p, q, (c, m) in zip(dprb, dprk, items)])
    dat2 = by_chunk([(mm(p, bt[c]) + mm(q, kt[c])) * m for p, q, (c, m) in zip(daab, daak, items)])
    dbt2 = by_chunk([mm_tn(p, r) + mm_tn(a_, x) for p, r, a_, x in zip(dprb, rtm, daab, atm)])
    dkt2 = by_chunk([mm_tn(p, r) + mm_tn(a_, x) for p, r, a_, x in zip(dprk, rtm, daak, atm)])
    ones = jnp.ones((8, PAIR), F32)
    upper = (lane_iota((CHUNK, CHUNK)) >= row_iota((CHUNK, CHUNK))).astype(F32)
    out = []
    for c in range(n):
        drt_c = drt[c] + drt2[c]
        dat_c = dat[c] + dat2[c]
        dbt_c = udm[c] * cl[c] + dbt2[c]
        dkt_c = vdm[c] * cl[c] + dkt2[c]
        dlcl = hdot_nt(ones, dm1[c] * m0[c])[0:1, :] * cl[c] + colsum(bc[c] * udm[c] + kc[c] * vdm[c])
        g = drt_c * rt[c] - dbt_c * bt[c] - dkt_c * kt[c] + dat_c * at[c]
        dlw = hdot(upper, g) - dat_c * at[c] + dlcl
        out.append((dat_c, dbt_c, dkt_c, drt_c, dv[c], dlw))
    return out


def wkv_forward(at, bt, kt, rt, v, clf):
    n_rows = at.shape[0]
    cps = WKV_CHUNKS_PER_STEP
    rb = cps * CHUNK
    n_steps = n_rows // rb

    def body(a_ref, b_ref, k_ref, r_ref, v_ref, c_ref, y_ref, m0_ref, g_ref, rh_ref, *rest):
        saved_refs, m_scr = rest[:6], rest[6]

        @pl.when(pl.program_id(1) == 0)
        def _():
            m_scr[...] = jnp.zeros_like(m_scr)

        masks = wkv_masks()
        chunks = []
        for cc in range(cps):
            sl = slice(cc * CHUNK, (cc + 1) * CHUNK)
            chunks.append((a_ref[sl, :], b_ref[sl, :], k_ref[sl, :], r_ref[sl, :], v_ref[sl, :],
                           c_ref[cc * CHUNK:cc * CHUNK + 1, :]))
        gs, hs, rhs, yhs, saved = wkv_chunks_pre(chunks, masks)
        for ref, per_chunk in zip(saved_refs, saved):
            for cc, val in enumerate(per_chunk):
                ref[cc * CHUNK:(cc + 1) * CHUNK, :] = val
        m = m_scr[...]
        for cc, (g, h, rh, yh) in enumerate(zip(gs, hs, rhs, yhs)):
            sl = slice(cc * CHUNK, (cc + 1) * CHUNK)
            m0_ref[0, cc] = m
            g_ref[0, cc] = g
            rh_ref[sl, :] = rh
            y_ref[sl, :] = hdot(rh, m) + yh
            m = hdot(g, m) + h
        m_scr[...] = m

    blk = pl.BlockSpec((rb, PAIR), lambda p, s: (s, p))
    state_blk = pl.BlockSpec((1, cps, PAIR, PAIR), lambda p, s: (p, s, 0, 0))
    state_shape = jax.ShapeDtypeStruct((WIDTH // PAIR, n_rows // CHUNK, PAIR, PAIR), F32)
    rows_f32 = jax.ShapeDtypeStruct((n_rows, WIDTH), F32)
    rows_bf16 = jax.ShapeDtypeStruct((n_rows, WIDTH), BF16)
    return pl.pallas_call(
        body, name="wkv_forward", grid=(WIDTH // PAIR, n_steps),
        in_specs=[blk] * 6,
        out_specs=[blk, state_blk, state_blk, blk] + [blk] * 6,
        out_shape=[rows_f32, state_shape, state_shape, rows_f32] + [rows_bf16] * 5 + [rows_f32],
        scratch_shapes=[pltpu.VMEM((PAIR, PAIR), F32)],
        compiler_params=pltpu.CompilerParams(dimension_semantics=("arbitrary", "arbitrary"),
                                             vmem_limit_bytes=VMEM_LIMIT),
    )(at, bt, kt, rt, v, clf)


def wkv_backward(at, bt, kt, rt, v, clf, m0s, gs, rh, saved, dy):
    n_rows = at.shape[0]
    cps = WKV_CHUNKS_PER_STEP
    rb = cps * CHUNK
    n_steps = n_rows // rb

    def body(a_ref, b_ref, k_ref, r_ref, v_ref, c_ref, m0_ref, g_ref, rh_ref, *rest):
        saved_refs, dy_ref = rest[:6], rest[6]
        da_ref, db_ref, dk_ref, dr_ref, dv_ref, dlw_ref, dm_scr = rest[7:]

        @pl.when(pl.program_id(1) == 0)
        def _():
            dm_scr[...] = jnp.zeros_like(dm_scr)

        masks = wkv_masks()
        bd = masks[4]
        dm = dm_scr[...]
        dm1 = [None] * cps
        for cc in reversed(range(cps)):
            sl = slice(cc * CHUNK, (cc + 1) * CHUNK)
            dm1[cc] = dm
            dm = bd * (hdot_tn(g_ref[0, cc], dm) + hdot_tn(rh_ref[sl, :], dy_ref[sl, :]))
        dm_scr[...] = dm
        chunks, kept, m0, dys = [], [], [], []
        for cc in range(cps):
            sl = slice(cc * CHUNK, (cc + 1) * CHUNK)
            chunks.append((a_ref[sl, :], b_ref[sl, :], k_ref[sl, :], r_ref[sl, :], v_ref[sl, :],
                           c_ref[cc * CHUNK:cc * CHUNK + 1, :]))
            kept.append(tuple(ref[sl, :] for ref in saved_refs))
            m0.append(m0_ref[0, cc])
            dys.append(dy_ref[sl, :])
        grads = wkv_chunks_grad(chunks, kept, m0, dys, dm1, masks)
        for cc, (dat, dbt, dkt, drt, dv, dlw) in enumerate(grads):
            sl = slice(cc * CHUNK, (cc + 1) * CHUNK)
            da_ref[sl, :] = dat
            db_ref[sl, :] = dbt
            dk_ref[sl, :] = dkt
            dr_ref[sl, :] = drt
            dv_ref[sl, :] = dv
            dlw_ref[sl, :] = dlw

    blk = pl.BlockSpec((rb, PAIR), lambda p, s: (n_steps - 1 - s, p))
    state_blk = pl.BlockSpec((1, cps, PAIR, PAIR), lambda p, s: (p, n_steps - 1 - s, 0, 0))
    return pl.pallas_call(
        body, name="wkv_backward", grid=(WIDTH // PAIR, n_steps),
        in_specs=[blk] * 6 + [state_blk, state_blk, blk] + [blk] * 6 + [blk],
        out_specs=[blk] * 6,
        out_shape=[jax.ShapeDtypeStruct((n_rows, WIDTH), F32)] * 6,
        scratch_shapes=[pltpu.VMEM((PAIR, PAIR), F32)],
        compiler_params=pltpu.CompilerParams(dimension_semantics=("arbitrary", "arbitrary"),
                                             vmem_limit_bytes=VMEM_LIMIT),
    )(at, bt, kt, rt, v, clf, m0s, gs, rh, *saved, dy)


def visible(q_row0, k_row0, shape):
    qc = (q_row0 + row_iota(shape)) // CHUNK
    kc = (k_row0 + lane_iota(shape)) // CHUNK
    return kc <= qc


def attention_forward(q, k, v):
    n_rows = q.shape[0]
    tq, tk = ATTN_FWD_TILES
    n_q = n_rows // tq
    n_masked = max(1, tq // tk)

    def body(q_ref, k_ref, v_ref, o_ref, lse_ref):
        i = pl.program_id(1)
        lane = lane_iota((tq, LANE))
        heads = [slice(0, LANE), slice(LANE, 2 * LANE)]
        qs = [q_ref[:, cols] for cols in heads]

        def step(j, carry, masked):
            rows = pl.ds(pl.multiple_of(j * tk, tk), tk)
            ss = [mm_nt(qh, k_ref[rows, cols]) for qh, cols in zip(qs, heads)]
            if masked:
                vis = visible(i * tq, j * tk, ss[0].shape)
                ss = [jnp.where(vis, s, -jnp.inf) for s in ss]
            ps, stats = [], []
            for s, (m, l, _) in zip(ss, carry):
                m_new = jnp.maximum(m, jnp.max(s, axis=-1, keepdims=True))
                p = jnp.exp2(s - m_new)
                alpha = jnp.exp2(m - m_new)
                ps.append(p)
                stats.append((m_new, alpha, alpha * l + jnp.sum(p, axis=-1, keepdims=True)))
            pvs = [mm(p, v_ref[rows, cols]) for p, cols in zip(ps, heads)]
            return tuple((m_new, l, alpha * acc + pv)
                         for (m_new, alpha, l), (_, _, acc), pv in zip(stats, carry, pvs))

        carry = tuple((jnp.full((tq, 1), -jnp.inf, F32), jnp.zeros((tq, 1), F32), jnp.zeros((tq, LANE), F32))
                      for _ in heads)
        n_full = (i * tq) // tk
        carry = lax.fori_loop(0, n_full, functools.partial(step, masked=False), carry)
        for extra in range(n_masked):
            carry = step(n_full + extra, carry, masked=True)
        (m0, l0, acc0), (m1, l1, acc1) = carry
        o_ref[...] = acc0 / l0 + acc1 / l1
        lse_ref[...] = jnp.where(lane >= HEAD, m1 + jnp.log2(l1), m0 + jnp.log2(l0))

    return pl.pallas_call(
        body, name="attention_forward", grid=(HEADS // 2, n_q),
        in_specs=[pl.BlockSpec((tq, 2 * LANE), lambda p, i: (i, p)),
                  pl.BlockSpec((n_rows, 2 * LANE), lambda p, i: (0, p)),
                  pl.BlockSpec((n_rows, 2 * LANE), lambda p, i: (0, p))],
        out_specs=[pl.BlockSpec((tq, LANE), lambda p, i: (i, p))] * 2,
        out_shape=[jax.ShapeDtypeStruct((n_rows, WIDTH), F32)] * 2,
        compiler_params=pltpu.CompilerParams(dimension_semantics=("arbitrary", "arbitrary"),
                                             vmem_limit_bytes=VMEM_LIMIT),
    )(q, k, v)


def attention_backward(q, k, v, o, do, lse):
    n_rows = q.shape[0]
    tq, tk = ATTN_BWD_TILES
    n_q = n_rows // tq
    n_masked = max(1, tk // tq)

    def body(q_ref, k_ref, v_ref, o_ref, do_ref, lse_ref, dq_ref, dk_ref, dv_ref):
        j = pl.program_id(1)

        @pl.when(j == 0)
        def _():
            dq_ref[...] = jnp.zeros_like(dq_ref)

        lane = lane_iota((tq, LANE))
        heads = [slice(0, LANE), slice(LANE, 2 * LANE)]
        ks = [k_ref[:, cols] for cols in heads]
        vs = [v_ref[:, cols] for cols in heads]
        head_lanes = [(lane < HEAD).astype(F32), (lane >= HEAD).astype(F32)]

        def step(i, carry, masked):
            rows = pl.ds(pl.multiple_of(i * tq, tq), tq)
            qs = [q_ref[rows, cols] for cols in heads]
            dout = do_ref[rows, :]
            dout_o = dout * o_ref[rows, :]
            lse_t = lse_ref[rows, :]
            ss = [mm_nt(qh, kh) for qh, kh in zip(qs, ks)]
            dps = [mm_nt(dout, vh) for vh in vs]
            ps, dss = [], []
            for hh in range(2):
                delta = jnp.sum(dout_o * head_lanes[hh], axis=-1, keepdims=True)
                lse_h = jnp.sum(jnp.where(lane == hh * HEAD, lse_t, 0.0), axis=-1, keepdims=True)
                p = jnp.exp2(ss[hh] - lse_h)
                if masked:
                    p = jnp.where(visible(i * tq, j * tk, p.shape), p, 0.0)
                ps.append(p)
                dss.append(p * (dps[hh] - delta))
            dvs = [mm_tn(p, dout) for p in ps]
            dqs = [mm(ds, kh) for ds, kh in zip(dss, ks)]
            dks = [mm_tn(ds, qh) for ds, qh in zip(dss, qs)]
            for cols, dq in zip(heads, dqs):
                dq_ref[rows, cols] += dq * ATTN_SCALE
            return tuple((dk + a, dv + b) for (dk, dv), a, b in zip(carry, dks, dvs))

        carry = tuple((jnp.zeros((tk, LANE), F32), jnp.zeros((tk, LANE), F32)) for _ in heads)
        i_first = (j * tk) // tq
        for extra in range(n_masked):
            carry = step(i_first + extra, carry, masked=True)
        carry = lax.fori_loop(i_first + n_masked, n_q, functools.partial(step, masked=False), carry)
        for cols, (dk, dv) in zip(heads, carry):
            dk_ref[:, cols] = dk * (1.0 / LOG2_E)
            dv_ref[:, cols] = dv

    full = lambda w: pl.BlockSpec((n_rows, w), lambda p, j: (0, p))
    blk = pl.BlockSpec((tk, 2 * LANE), lambda p, j: (j, p))
    return pl.pallas_call(
        body, name="attention_backward", grid=(HEADS // 2, n_rows // tk),
        in_specs=[full(2 * LANE), blk, blk, full(LANE), full(LANE), full(LANE)],
        out_specs=[full(2 * LANE), blk, blk],
        out_shape=[jax.ShapeDtypeStruct((n_rows, HEADS * LANE), F32)] * 3,
        compiler_params=pltpu.CompilerParams(dimension_semantics=("arbitrary", "arbitrary"),
                                             vmem_limit_bytes=VMEM_LIMIT),
    )(q, k, v, o, do, lse)


def tail_tile(step0, tile0, x, tgt, ma, mb, gpa, gpb, ya, y, ur, k2, uv,
              mod, wpa, wpb, wout, gn_g, gn_b, r_k, post_g, post_b, bd):
    gate = mod[2:3]
    inv = 1.0 / HEAD
    yc = y - head_sum(y, bd) * inv
    rs = lax.rsqrt(head_sum(yc * yc, bd) * inv + GN_EPS)
    yn = yc * rs
    yb = yn * gn_g + gn_b + head_sum(ur * k2 * r_k, bd) * uv
    sga, sgb = sigmoid(gpa), sigmoid(gpb)
    sila, silb = gpa * sga, gpb * sgb
    ga, gb = ya * sila, yb * silb
    pa, pb = mm(ga, wpa), mm(gb, wpb)
    sa, sb = sigmoid(ma), sigmoid(mb)
    merged = sa * pa + sb * pb
    sub = mm(merged, wout)
    z = ALPHA * x + (1.0 + gate) * sub
    zhat, rstd = layer_norm_stats(z)
    err = zhat * post_g + post_b - tgt
    loss = 0.5 * jnp.sum(rowmean(err * err), axis=0, keepdims=True) + jnp.zeros((1, LANE), F32)
    dout = err * (1.0 / D_MODEL)
    dpost_g = colsum(dout * zhat)
    dpost_b = colsum(dout)
    dz = layer_norm_bwd(dout * post_g, zhat, rstd)
    dgate = colsum(dz * sub)
    dsub = dz * (1.0 + gate)
    dwout = mm_tn(merged, dsub)
    dmerged = mm_nt(dsub, wout)
    dpa, dpb = dmerged * sa, dmerged * sb
    dma = dmerged * pa * sa * (1.0 - sa)
    dmb = dmerged * pb * sb * (1.0 - sb)
    dwpa = mm_tn(ga, dpa)
    dwpb = mm_tn(gb, dpb)
    dga = mm_nt(dpa, wpa)
    dgb = mm_nt(dpb, wpb)
    dya = dga * sila
    dgpa = dga * ya * (sga * (1.0 + gpa * (1.0 - sga)))
    dyb = dgb * silb
    dgpb = dgb * yb * (sgb * (1.0 + gpb * (1.0 - sgb)))
    dgn_g = colsum(dyb * yn)
    dgn_b = colsum(dyb)
    dyn = dyb * gn_g
    dy = rs * (dyn - head_sum(dyn, bd) * inv - yn * head_sum(dyn * yn, bd) * inv)
    return (dz, dma, dmb, dgpa, dgpb, dya, dy, dyb,
            loss, dwout, dwpa, dwpb, dgn_g, dgn_b, dpost_g, dpost_b, dgate)


def mla_prep_bwd_tile(step0, tile0, q_c, kv_c, cos, sin, dq, dk, dv, gq, gkv, wq, wqr, wkn, wv):
    qn, qh, rq = rms_norm_fwd(q_c, gq)
    kvn, kvh, rkv = rms_norm_fwd(kv_c, gkv)
    dqc = dq * tile_lanes(cos, HEADS)
    dqs = dq * tile_lanes(sin, HEADS)
    dqn = mm_nt(dqc, wq) + mm_nt(dqs, wqr)
    dkvn = mm_nt(dk, wkn) + mm_nt(dv, wv)
    dkpe = dk[:, 0:LANE]
    for h in range(1, HEADS):
        dkpe = dkpe + dk[:, h * LANE:(h + 1) * LANE]
    dkr = dkpe * (cos * key_rope_mask(cos.shape))
    dkrr = dkpe * sin

    def rms_bwd(dyv, xh, r, g):
        dyg = dyv * g
        return r * (dyg - xh * rowmean(dyg * xh)), colsum(dyv * xh)

    dq_c, dgq = rms_bwd(dqn, qh, rq, gq)
    dkv_c, dgkv = rms_bwd(dkvn, kvh, rkv, gkv)
    return (dq_c, dkv_c, dkr, dkrr,
            mm_tn(qn, dqc), mm_tn(qn, dqs), mm_tn(kvn, dk), mm_tn(kvn, dv), dgq, dgkv)


def rwkv_prep_bwd_tile(step0, tile0, r0, k0, v0, l0, drt, dat, dbt, dkt, dvv, dlw, dyb, hr, hk, hv, hl,
                       mu_r, mu_k, mu_v, mu_l, w0, a0, k_k, k_a, w_dec, w_iclr, tril, same, bd, r_k,
                       cr, ck, cv, cl_):
    f = rwkv_prep_core(tile0, r0, k0, v0, l0, hr, hk, hv, hl, mu_r, mu_k, mu_v, mu_l, w0, a0, k_k, k_a,
                       w_dec, w_iclr, tril, same, bd)
    ur, uk, uv, ul, kk, k2, a_ic, sg, th = (f[n] for n in ("ur", "uk", "uv", "ul", "kk", "k2", "a_ic", "sg", "th"))
    lc, lw = f["lc"], f["lw"]
    e_neg = jnp.exp(-lc)
    dur = drt * jnp.exp(lc)
    da = dat * jnp.exp(lc - lw)
    db = dbt * e_neg
    dk2 = dkt * e_neg
    s = head_sum(ur * k2 * r_k, bd)
    duv = dvv + dyb * s
    ds = head_sum(dyb * uv, bd)
    dur = dur + ds * k2 * r_k
    dk2 = dk2 + ds * ur * r_k
    dr_k = colsum(ds * ur * k2)
    dkk = db * a_ic - da
    da_ic = db * kk + dk2 * uk * k_a
    duk = dk2 * (1.0 + (a_ic - 1.0) * k_a)
    dk_a = colsum(dk2 * uk * (a_ic - 1.0))
    dkkraw = jnp.where(f["nrm_raw"] > 1e-12, (dkk - kk * head_sum(dkk * kk, bd)) / f["nrm"], dkk * 1e12)
    duk = duk + dkkraw * k_k
    dk_k = colsum(dkkraw * uk)
    dai = da_ic * a_ic * (1.0 - a_ic)
    dd = dlw * (-DECAY_SCALE) * sg * (1.0 - sg)
    dul = mm_nt(dai, w_iclr) + mm_nt(dd, w_dec) * (1.0 - th * th)

    def unshift(du, x, prev, mu, carry_row):
        nxt = shift_rows_up(du, carry_row)
        return du * (1.0 - mu) + nxt * mu, colsum(du * (prev - x)), du[0:1, :]

    dr0, dmu_r, ncr = unshift(dur, r0, f["pr"], mu_r, cr)
    dk0, dmu_k, nck = unshift(duk, k0, f["pk"], mu_k, ck)
    dv0, dmu_v, ncv = unshift(duv, v0, f["pv"], mu_v, cv)
    dl0, dmu_l, ncl = unshift(dul, l0, f["pl"], mu_l, cl_)
    return (dr0, dk0, dv0, dl0,
            dmu_r, dmu_k, dmu_v, dmu_l, colsum(dd), colsum(dai), dk_k, dk_a, dr_k, mm_tn(th, dd), mm_tn(ul, dai),
            ncr, nck, ncv, ncl)


def in_backward(x, dz, pieces, mod, w_in_p, unrot):
    n_rows = x.shape[0]
    ts = ROW_TILE
    n_p = len(pieces)
    shard_cols = IN_WIDTH // N_DEV

    def body(*refs):
        x_ref, dz_ref = refs[:2]
        p_refs = refs[2:2 + n_p]
        mod_ref, w_ref, unrot_ref = refs[2 + n_p:5 + n_p]
        dx_ref, ht_ref, blocks_ref, dshift_ref, dscale_ref = refs[5 + n_p:]
        step0 = pl.program_id(0) == 0
        dma, dmb, dr0, dk0, dv0, dgpa, dgpb, dq_c, dkv_c, dkr, dkrr, dl0 = (r[...] for r in p_refs)
        dproj = jnp.concatenate([dma, dmb, dr0, dk0, dv0, dgpa, dgpb, dq_c, dkv_c, dkr, dkrr, dl0], axis=1)
        dh = mm_nt(dproj, w_ref[...])
        xhat, rstd = layer_norm_stats(x_ref[...])
        scale1 = 1.0 + mod_ref[1:2, :]
        dx_ref[...] = layer_norm_bwd(dh * scale1, xhat, rstd) + ALPHA * dz_ref[...]
        ht_ref[...] = jnp.transpose(xhat * scale1 + mod_ref[0:1, :]).astype(BF16)
        dkrope = (dkr.astype(F32) + mm(dkrr, unrot_ref[...]))[:, NOPE:QK_DIM]
        natural = jnp.concatenate(
            [dq_c.astype(F32), dkv_c.astype(F32), dkrope]
            + [p.astype(F32) for p in (dgpa, dr0, dk0, dv0, dl0, dgpb, dma, dmb)], axis=1)
        for j in range(N_DEV):
            blocks_ref[j] = natural[:, j * shard_cols:(j + 1) * shard_cols].astype(BF16)
        for ref, val in ((dshift_ref, colsum(dh)), (dscale_ref, colsum(dh * xhat))):
            @pl.when(step0)
            def _(ref=ref, val=val):
                ref[...] = val

            @pl.when(jnp.logical_not(step0))
            def _(ref=ref, val=val):
                ref[...] += val

    row = lambda w: pl.BlockSpec((ts, w), lambda i: (i, 0))
    const = pl.BlockSpec(memory_space=pltpu.VMEM)
    vec = pl.BlockSpec((1, D_MODEL), lambda i: (0, 0))
    return pl.pallas_call(
        body, name="in_backward", grid=(n_rows // ts,),
        in_specs=[row(D_MODEL), row(D_MODEL)] + [row(p.shape[1]) for p in pieces] + [const] * 3,
        out_specs=[row(D_MODEL), pl.BlockSpec((D_MODEL, ts), lambda i: (0, i)),
                   pl.BlockSpec((N_DEV, ts, shard_cols), lambda i: (0, i, 0)), vec, vec],
        out_shape=[jax.ShapeDtypeStruct((n_rows, D_MODEL), F32), jax.ShapeDtypeStruct((D_MODEL, n_rows), BF16),
                   jax.ShapeDtypeStruct((N_DEV, n_rows, shard_cols), BF16),
                   jax.ShapeDtypeStruct((1, D_MODEL), F32), jax.ShapeDtypeStruct((1, D_MODEL), F32)],
        compiler_params=pltpu.CompilerParams(dimension_semantics=("arbitrary",), vmem_limit_bytes=VMEM_LIMIT),
    )(x, dz, *pieces, mod, w_in_p, unrot)


def in_weight_grad_exchange(h_t, dp_blocks, others, small, order):
    n = len(others)
    n_rows = h_t.shape[1]
    ts = 2 * ROW_TILE
    n_i = n_rows // ts
    shard_cols = dp_blocks.shape[2]
    n_chips = N_DEV // 2
    last = N_DEV - 1

    def body(order_ref, h_ref, dp_ref, *rest):
        g_refs, s_ref = rest[:n], rest[n]
        rwin_ref, rg_refs, rs_ref = rest[n + 1], rest[n + 2:2 * n + 2], rest[2 * n + 2]
        (acc, sendbuf, sib_buf, sib_send, sib_recv, win_send, win_recv,
         o_send, o_recv, local_sems) = rest[2 * n + 3:]
        b, i = pl.program_id(0), pl.program_id(1)
        me = my_position()
        mi = flat_index(me)
        sibling = (me[0], me[1], 1 - me[2])

        def other_copies(k, src_index, dst_index):
            peer = flip(me, k)
            out = [pltpu.make_async_remote_copy(
                src_ref=g_refs[a].at[src_index], dst_ref=rg_refs[a].at[dst_index],
                send_sem=o_send.at[(n + 1) * (k - 1) + a], recv_sem=o_recv.at[(n + 1) * (k - 1) + a],
                device_id=peer, device_id_type=MESH_IDS) for a in range(n)]
            out.append(pltpu.make_async_remote_copy(
                src_ref=s_ref, dst_ref=rs_ref.at[dst_index],
                send_sem=o_send.at[(n + 1) * (k - 1) + n], recv_sem=o_recv.at[(n + 1) * (k - 1) + n],
                device_id=peer, device_id_type=MESH_IDS))
            return out

        def local_copies():
            out = [pltpu.make_async_copy(g_refs[a].at[mi], rg_refs[a].at[mi], local_sems.at[a]) for a in range(n)]
            out.append(pltpu.make_async_copy(s_ref, rs_ref.at[mi], local_sems.at[n]))
            return out

        def to_sibling(t):
            return pltpu.make_async_remote_copy(
                src_ref=sendbuf.at[t], dst_ref=sib_buf.at[t], send_sem=sib_send.at[t], recv_sem=sib_recv.at[t],
                device_id=sibling, device_id_type=MESH_IDS)

        def to_owner(t):
            flip_x = (t < 2) * 1
            flip_y = 1 - (t & 1)
            owner = (me[0] ^ flip_x, me[1] ^ flip_y, me[2])
            return pltpu.make_async_remote_copy(
                src_ref=sendbuf.at[n_chips + t], dst_ref=rwin_ref.at[t], send_sem=win_send.at[t],
                recv_sem=win_recv.at[t], device_id=owner, device_id_type=MESH_IDS)

        own_block = pltpu.make_async_copy(sendbuf.at[last], rwin_ref.at[n_chips - 1], local_sems.at[n + 1])

        @pl.when(jnp.logical_and(b == 0, i == 0))
        def _():
            for cp in local_copies():
                cp.start()
            for k in range(1, N_DEV):
                for cp in other_copies(k, flat_index(flip(me, k)), mi):
                    cp.start()

        contrib = jnp.dot(h_ref[...], dp_ref[...], preferred_element_type=F32)

        @pl.when(i == 0)
        def _():
            acc[...] = contrib

        @pl.when(i > 0)
        def _():
            acc[...] += contrib

        @pl.when(jnp.logical_and(i == n_i - 1, b < n_chips))
        def _():
            sendbuf[b] = acc[...].astype(BF16)
            to_sibling(b).start()

        @pl.when(jnp.logical_and(i == n_i - 1, b >= n_chips))
        def _():
            t = b - n_chips
            to_sibling(t).wait_recv()
            sendbuf[b] = (acc[...] + sib_buf[t].astype(F32)).astype(BF16)

            @pl.when(b < last)
            def _():
                to_owner(t).start()

            @pl.when(b == last)
            def _():
                own_block.start()

        @pl.when(jnp.logical_and(b == last, i == n_i - 1))
        def _():
            for t in range(n_chips - 1):
                to_owner(t).wait_recv()
            for k in range(1, N_DEV):
                pi = flat_index(flip(me, k))
                for cp in other_copies(k, pi, pi):
                    cp.wait_recv()
            for t in range(n_chips):
                to_sibling(t).wait_send()
            for t in range(n_chips - 1):
                to_owner(t).wait_send()
            for k in range(1, N_DEV):
                for cp in other_copies(k, flat_index(flip(me, k)), mi):
                    cp.wait_send()
            for cp in local_copies():
                cp.wait()
            own_block.wait()

    hbm = pl.BlockSpec(memory_space=pl.ANY)
    n_sem = 7 * (n + 1)
    grid_spec = pltpu.PrefetchScalarGridSpec(
        num_scalar_prefetch=1, grid=(N_DEV, n_i),
        in_specs=[pl.BlockSpec((D_MODEL, ts), lambda b, i, order: (0, i)),
                  pl.BlockSpec((None, ts, shard_cols), lambda b, i, order: (order[b], i, 0))] + [hbm] * (n + 1),
        out_specs=[hbm] * (n + 2),
        scratch_shapes=[pltpu.VMEM((D_MODEL, shard_cols), F32), pltpu.VMEM((N_DEV, D_MODEL, shard_cols), BF16),
                        pltpu.VMEM((n_chips, D_MODEL, shard_cols), BF16),
                        pltpu.SemaphoreType.DMA((n_chips,)), pltpu.SemaphoreType.DMA((n_chips,)),
                        pltpu.SemaphoreType.DMA((n_chips - 1,)), pltpu.SemaphoreType.DMA((n_chips - 1,)),
                        pltpu.SemaphoreType.DMA((n_sem,)), pltpu.SemaphoreType.DMA((n_sem,)),
                        pltpu.SemaphoreType.DMA((n + 2,))])
    return pl.pallas_call(
        body, name="in_weight_grad_exchange", grid_spec=grid_spec,
        out_shape=[jax.ShapeDtypeStruct((n_chips, D_MODEL, shard_cols), BF16)]
        + [jax.ShapeDtypeStruct(o.shape, o.dtype) for o in others]
        + [jax.ShapeDtypeStruct((N_DEV,) + small.shape, small.dtype)],
        compiler_params=pltpu.CompilerParams(dimension_semantics=("arbitrary", "arbitrary"),
                                             vmem_limit_bytes=VMEM_LIMIT),
    )(order, h_t, dp_blocks, *others, small)


def ada_weight_grad(c_all, dmod_cols):
    def body(c_ref, d_ref, o_ref):
        cv = c_ref[...]
        o_ref[...] = hdot_tn(cv * sigmoid(cv), d_ref[...])

    return pl.pallas_call(
        body, name="ada_weight_grad",
        out_shape=jax.ShapeDtypeStruct((c_all.shape[1], dmod_cols.shape[1]), F32),
    )(c_all, dmod_cols)


def adamw_update(g, w, m, v):
    nm = ADAM_B1 * m + (1.0 - ADAM_B1) * g
    nv = ADAM_B2 * v + (1.0 - ADAM_B2) * (g * g)
    m_hat = nm / (1.0 - ADAM_B1 ** ADAM_STEP)
    v_hat = nv / (1.0 - ADAM_B2 ** ADAM_STEP)
    return -ADAM_LR * (m_hat / (jnp.sqrt(v_hat) + ADAM_EPS) + ADAM_WD * w), nm, nv


def adamw(parts, w, m, v, name):
    k, rows, cols = parts.shape
    rb = 128 if rows % 128 == 0 else rows

    def body(p_ref, w_ref, m_ref, v_ref, g_ref, d_ref, nm_ref, nv_ref):
        g = p_ref[0].astype(F32)
        for i in range(1, k):
            g = g + p_ref[i].astype(F32)
        g_ref[0] = g
        d_ref[0], nm_ref[0], nv_ref[0] = adamw_update(g, w_ref[0], m_ref[0], v_ref[0])

    blk = pl.BlockSpec((1, rb, cols), lambda i: (0, i, 0))
    return pl.pallas_call(
        body, name=name, grid=(rows // rb,),
        in_specs=[pl.BlockSpec((k, rb, cols), lambda i: (0, i, 0)), blk, blk, blk],
        out_specs=[blk] * 4, out_shape=[jax.ShapeDtypeStruct((1, rows, cols), F32)] * 4,
        compiler_params=pltpu.CompilerParams(dimension_semantics=("arbitrary",), vmem_limit_bytes=VMEM_LIMIT),
    )(parts, w, m, v)


def adamw_small(parts, ws, ms, vs):
    k = parts.shape[0]
    n = len(ws)
    sizes = [w.shape[1] for w in ws]

    def body(p_ref, *refs):
        ins, outs = refs[:3 * n], refs[3 * n:]
        g_all = p_ref[0]
        for i in range(1, k):
            g_all = g_all + p_ref[i]
        off = 0
        for a, size in enumerate(sizes):
            g = g_all[:, off:off + size]
            off += size
            d, nm, nv = adamw_update(g, ins[a][...], ins[n + a][...], ins[2 * n + a][...])
            for kind, val in enumerate((g, d, nm, nv)):
                outs[kind * n + a][...] = val

    return pl.pallas_call(
        body, name="adamw_small",
        out_shape=[jax.ShapeDtypeStruct((1, size), F32) for _ in range(4) for size in sizes],
    )(parts, *ws, *ms, *vs)


def rot_cols(w):
    return jnp.concatenate([-w[:, ROPE // 2:], w[:, :ROPE // 2]], axis=1)


def unrot_cols(dw):
    return jnp.concatenate([dw[:, ROPE // 2:], -dw[:, :ROPE // 2]], axis=1)


def columns_from_shards(g, rows, cols):
    return g.reshape(N_DEV, rows, cols).transpose(1, 0, 2).reshape(rows, N_DEV * cols)


def shards_from_columns(w, rows, cols):
    return w.reshape(rows, N_DEV, cols).transpose(1, 0, 2).reshape(N_DEV, rows * cols)


def permute_w_in(w):
    z = lambda n: jnp.zeros((D_MODEL, n), w.dtype)
    krope = w[:, N_KROPE:N_KROPE + ROPE]
    rw = N_RWKV
    return jnp.concatenate([
        w[:, N_MA:N_MA + 1024], w[:, N_MB:N_MB + 1024],
        w[:, rw:rw + 512], w[:, rw + 512:rw + 1024], w[:, rw + 1024:rw + 1536],
        w[:, N_GPA:N_GPA + 512], w[:, N_GPB:N_GPB + 512],
        w[:, N_QC:N_QC + 256], w[:, N_KVC:N_KVC + 128],
        z(NOPE), krope, z(LANE - QK_DIM), z(NOPE), rot_cols(krope), z(LANE - QK_DIM),
        w[:, rw + 1536:rw + 1664]], axis=1)


def unpermute_w_in_grad(d):
    rw = P_R
    krope = d[:, P_KR + NOPE:P_KR + QK_DIM] + unrot_cols(d[:, P_KRR + NOPE:P_KRR + QK_DIM])
    return jnp.concatenate([
        d[:, P_QC:P_QC + 256], d[:, P_KVC:P_KVC + 128], krope, d[:, P_GPA:P_GPA + 512],
        d[:, rw:rw + 1536], d[:, P_LORA:P_LORA + 128], d[:, P_GPB:P_GPB + 512],
        d[:, P_MA:P_MA + 1024], d[:, P_MB:P_MB + 1024]], axis=1)


def pad_heads_q(w_uq):
    w = w_uq.reshape(Q_RANK, HEADS, QK_DIM)
    zpad = jnp.zeros((Q_RANK, HEADS, LANE - QK_DIM), w.dtype)
    wq = jnp.concatenate([w, zpad], axis=2).reshape(Q_RANK, HEADS * LANE)
    pe = w[:, :, NOPE:]
    rot = jnp.concatenate([-pe[:, :, ROPE // 2:], pe[:, :, :ROPE // 2]], axis=2)
    wqr = jnp.concatenate([jnp.zeros((Q_RANK, HEADS, NOPE), w.dtype), rot, zpad], axis=2).reshape(Q_RANK, HEADS * LANE)
    return wq, wqr


def unpad_heads_q_grad(dwq, dwqr):
    a = dwq.reshape(Q_RANK, HEADS, LANE)
    r = dwqr.reshape(Q_RANK, HEADS, LANE)[:, :, NOPE:QK_DIM]
    pe = a[:, :, NOPE:QK_DIM] + jnp.concatenate([r[:, :, ROPE // 2:], -r[:, :, :ROPE // 2]], axis=2)
    return jnp.concatenate([a[:, :, :NOPE], pe], axis=2).reshape(Q_RANK, HEADS * QK_DIM)


def pad_heads_kv(w_ukv):
    w = w_ukv.reshape(KV_RANK, HEADS, 2 * HEAD)
    z = jnp.zeros((KV_RANK, HEADS, HEAD), w.dtype)
    wkn = jnp.concatenate([w[:, :, :NOPE], z], axis=2).reshape(KV_RANK, HEADS * LANE)
    val = w[:, :, NOPE:]
    odd = (jnp.arange(HEADS) % 2 == 1)[None, :, None]
    wv = jnp.concatenate([jnp.where(odd, 0, val), jnp.where(odd, val, 0)], axis=2).reshape(KV_RANK, HEADS * LANE)
    return wkn, wv


def unpad_heads_kv_grad(dwkn, dwv):
    a = dwkn.reshape(KV_RANK, HEADS, LANE)[:, :, :NOPE]
    b = dwv.reshape(KV_RANK, HEADS, LANE)
    odd = (jnp.arange(HEADS) % 2 == 1)[None, :, None]
    val = jnp.where(odd, b[:, :, HEAD:], b[:, :, :HEAD])
    return jnp.concatenate([a, val], axis=2).reshape(KV_RANK, HEADS * 2 * HEAD)


def kernel(x, c, positions, w_ada, b_ada, w_in, q_norm_g, w_uq, kv_norm_g, w_ukv, mu_rwkv, w0, w_decay_up, a0, w_iclr_up, k_k, k_a, r_k, gn_g, gn_b, w_proj_a, w_proj_b, w_out, post_g, post_b, loss_target, m_w_ada, m_b_ada, m_w_in, m_q_norm_g, m_w_uq, m_kv_norm_g, m_w_ukv, m_mu_rwkv, m_w0, m_w_decay_up, m_a0, m_w_iclr_up, m_k_k, m_k_a, m_r_k, m_gn_g, m_gn_b, m_w_proj_a, m_w_proj_b, m_w_out, m_post_g, m_post_b, v_w_ada, v_b_ada, v_w_in, v_q_norm_g, v_w_uq, v_kv_norm_g, v_w_ukv, v_mu_rwkv, v_w0, v_w_decay_up, v_a0, v_w_iclr_up, v_k_k, v_k_a, v_r_k, v_gn_g, v_gn_b, v_w_proj_a, v_w_proj_b, v_w_out, v_post_g, v_post_b):
    weights = dict(w_ada=w_ada, b_ada=b_ada, w_in=w_in, q_norm_g=q_norm_g, w_uq=w_uq, kv_norm_g=kv_norm_g,
                   w_ukv=w_ukv, mu_rwkv=mu_rwkv, w0=w0, w_decay_up=w_decay_up, a0=a0, w_iclr_up=w_iclr_up,
                   k_k=k_k, k_a=k_a, r_k=r_k, gn_g=gn_g, gn_b=gn_b, w_proj_a=w_proj_a, w_proj_b=w_proj_b,
                   w_out=w_out, post_g=post_g, post_b=post_b)
    mom1 = dict(w_ada=m_w_ada, b_ada=m_b_ada, w_in=m_w_in, q_norm_g=m_q_norm_g, w_uq=m_w_uq, kv_norm_g=m_kv_norm_g,
                w_ukv=m_w_ukv, mu_rwkv=m_mu_rwkv, w0=m_w0, w_decay_up=m_w_decay_up, a0=m_a0, w_iclr_up=m_w_iclr_up,
                k_k=m_k_k, k_a=m_k_a, r_k=m_r_k, gn_g=m_gn_g, gn_b=m_gn_b, w_proj_a=m_w_proj_a, w_proj_b=m_w_proj_b,
                w_out=m_w_out, post_g=m_post_g, post_b=m_post_b)
    mom2 = dict(w_ada=v_w_ada, b_ada=v_b_ada, w_in=v_w_in, q_norm_g=v_q_norm_g, w_uq=v_w_uq, kv_norm_g=v_kv_norm_g,
                w_ukv=v_w_ukv, mu_rwkv=v_mu_rwkv, w0=v_w0, w_decay_up=v_w_decay_up, a0=v_a0, w_iclr_up=v_w_iclr_up,
                k_k=v_k_k, k_a=v_k_a, r_k=v_r_k, gn_g=v_gn_g, gn_b=v_gn_b, w_proj_a=v_w_proj_a, w_proj_b=v_w_proj_b,
                w_out=v_w_out, post_g=v_post_g, post_b=v_post_b)
    names = list(weights)
    n_rows = x.shape[1]
    me = 4 * lax.axis_index("x") + 2 * lax.axis_index("y") + lax.axis_index("c")
    xr = x[0]
    tgt = loss_target[0]
    row = lambda a: a.reshape(1, -1)

    w_in_all, c_all = gather_shards([w_in[0].astype(BF16), c])
    c_all = c_all.reshape(N_DEV, D_MODEL)
    w_in_p = permute_w_in(columns_from_shards(w_in_all, D_MODEL, IN_WIDTH // N_DEV))

    mod_all = ada_modulation(c_all, w_ada[0], b_ada.reshape(N_DEV, -1))
    mod = lax.dynamic_index_in_dim(mod_all, me, axis=1, keepdims=False).reshape(3, D_MODEL)

    proj, *gathered = fwd_in_gather(xr, mod, w_in_p, [weights[n][0].astype(BF16) for n, _, _ in SHARDED[1:]])
    pcol = lambda off_, w: (proj, w, off_ // w)
    full = {}
    for (n, r, cdim), part in zip(SHARDED[1:], gathered):
        full[n] = part.reshape(N_DEV * r, cdim) if n == "w_out" else columns_from_shards(part, r, cdim)
    wq, wqr = pad_heads_q(full["w_uq"])
    wkn, wv = pad_heads_kv(full["w_ukv"])
    zl = jnp.zeros((LORA, WIDTH), BF16)
    w_dec = jnp.concatenate([full["w_decay_up"], zl], axis=0)
    w_iclr = jnp.concatenate([zl, full["w_iclr_up"]], axis=0)
    wpa, wpb, wout = full["w_proj_a"], full["w_proj_b"], full["w_out"]

    inv_freq = ROPE_THETA ** (-jnp.arange(0, ROPE, 2, dtype=F32) / ROPE)
    ang = positions[0].astype(F32)[:, None] * inv_freq
    ones_n, zeros_n, zeros_p = jnp.ones((n_rows, NOPE), F32), jnp.zeros((n_rows, NOPE), F32), jnp.zeros((n_rows, LANE - QK_DIM), F32)
    cos_t = jnp.concatenate([ones_n, jnp.cos(ang), jnp.cos(ang), zeros_p], axis=1)
    sin_t = jnp.concatenate([zeros_n, jnp.sin(ang), jnp.sin(ang), zeros_p], axis=1)

    gq, gkv = q_norm_g, kv_norm_g
    mla_consts = [gq, gkv, wq, wqr, wkn, wv]
    q, k, v = row_call(
        "mla_prep", mla_prep_tile, n_rows,
        [pcol(P_QC, 256), pcol(P_KVC, 128), pcol(P_KR, 128), pcol(P_KRR, 128), (cos_t, LANE, 0), (sin_t, LANE, 0)],
        mla_consts, [(HEADS * LANE, BF16)] * 3)
    ya, lse = attention_forward(q, k, v)

    t_idx = jnp.arange(ROW_TILE)
    same_chunk = (t_idx[:, None] // CHUNK) == (t_idx[None, :] // CHUNK)
    same = same_chunk.astype(F32)
    tril = (same_chunk & (t_idx[:, None] >= t_idx[None, :])).astype(F32)
    l_idx = jnp.arange(LANE)
    bd = ((l_idx[:, None] // HEAD) == (l_idx[None, :] // HEAD)).astype(F32)
    mu = mu_rwkv
    mu_r, mu_k, mu_v, mu_l = mu[:, 0:512], mu[:, 512:1024], mu[:, 1024:1536], mu[:, 1536:1664]
    rk_row = row(r_k)
    rwkv_consts = [mu_r, mu_k, mu_v, mu_l, w0, a0, k_k, k_a, w_dec, w_iclr, tril, same, bd]
    rwkv_rows = [pcol(P_R, 512), pcol(P_K, 512), pcol(P_V, 512), pcol(P_LORA, 128)]
    rt, at, bt, kt, clf, uv, ur, k2 = row_call(
        "rwkv_prep", rwkv_prep_tile, n_rows, rwkv_rows, rwkv_consts, [(WIDTH, F32)] * 8, halo_in=rwkv_rows)
    y, m0s, state_maps, out_maps, *wkv_saved = wkv_forward(at, bt, kt, rt, uv, clf)

    tail = row_call(
        "tail", tail_tile, n_rows,
        [(xr, D_MODEL, 0), (tgt, D_MODEL, 0), pcol(P_MA, 1024), pcol(P_MB, 1024), pcol(P_GPA, 512), pcol(P_GPB, 512),
         (ya, WIDTH, 0), (y, WIDTH, 0), (ur, WIDTH, 0), (k2, WIDTH, 0), (uv, WIDTH, 0)],
        [mod, wpa, wpb, wout, gn_g, gn_b, rk_row, post_g, post_b, bd],
        [(D_MODEL, F32), (1024, BF16), (1024, BF16), (512, BF16), (512, BF16), (WIDTH, F32), (WIDTH, F32), (WIDTH, F32)],
        acc_out=[((1, LANE), F32), ((D_MODEL, D_MODEL), F32), ((WIDTH, D_MODEL), F32), ((WIDTH, D_MODEL), F32),
                 ((1, WIDTH), F32), ((1, WIDTH), F32), ((1, D_MODEL), F32), ((1, D_MODEL), F32), ((1, D_MODEL), F32)])
    (dz, dma, dmb, dgpa, dgpb, dya, dy, dyb,
     loss_row, g_wout, g_wpa, g_wpb, g_gn_g, g_gn_b, g_post_g, g_post_b, dgate) = tail

    dq, dk, dv = attention_backward(q, k, v, ya, dya, lse)
    dq_c, dkv_c, dkr, dkrr, g_wq, g_wqr, g_wkn, g_wv, g_gq, g_gkv = row_call(
        "mla_prep_bwd", mla_prep_bwd_tile, n_rows,
        [pcol(P_QC, 256), pcol(P_KVC, 128), (cos_t, LANE, 0), (sin_t, LANE, 0),
         (dq, HEADS * LANE, 0), (dk, HEADS * LANE, 0), (dv, HEADS * LANE, 0)],
        mla_consts, [(256, BF16), (128, BF16), (128, BF16), (128, BF16)],
        acc_out=[((Q_RANK, HEADS * LANE), F32)] * 2 + [((KV_RANK, HEADS * LANE), F32)] * 2
        + [((1, Q_RANK), F32), ((1, KV_RANK), F32)])

    dat, dbt, dkt, drt, dvv, dlw = wkv_backward(at, bt, kt, rt, uv, clf, m0s, state_maps, out_maps, wkv_saved, dy)
    (dr0, dk0, dv0, dl0, g_mu_r, g_mu_k, g_mu_v, g_mu_l, g_w0, g_a0, g_k_k, g_k_a, g_r_k, g_wdec, g_wiclr) = row_call(
        "rwkv_prep_bwd", rwkv_prep_bwd_tile, n_rows,
        rwkv_rows + [(drt, WIDTH, 0), (dat, WIDTH, 0), (dbt, WIDTH, 0), (dkt, WIDTH, 0), (dvv, WIDTH, 0),
                     (dlw, WIDTH, 0), (dyb, WIDTH, 0)],
        rwkv_consts + [rk_row], [(512, BF16), (512, BF16), (512, BF16), (128, BF16)],
        acc_out=[((1, 512), F32)] * 3 + [((1, 128), F32)] + [((1, 512), F32)] * 5 + [((LANE, WIDTH), F32)] * 2,
        halo_in=rwkv_rows, carry=[512, 512, 512, 128], reverse=True)

    li = jnp.arange(LANE)
    src, dst = li[:, None], li[None, :]
    half = ROPE // 2
    unrot = (jnp.where((dst >= NOPE) & (dst < NOPE + half) & (src == dst + half), 1.0, 0.0)
             - jnp.where((dst >= NOPE + half) & (dst < QK_DIM) & (src == dst - half), 1.0, 0.0)).astype(BF16)
    dx, h_t, dproj_blocks, dshift, dscale = in_backward(
        xr, dz, [dma, dmb, dr0, dk0, dv0, dgpa, dgpb, dq_c, dkv_c, dkr, dkrr, dl0], mod, w_in_p, unrot)

    grads_full = {
        "w_uq": unpad_heads_q_grad(g_wq, g_wqr), "w_ukv": unpad_heads_kv_grad(g_wkn, g_wv),
        "w_decay_up": g_wdec[:LORA], "w_iclr_up": g_wiclr[LORA:],
        "w_proj_a": g_wpa, "w_proj_b": g_wpb, "w_out": g_wout}
    blocks = [(grads_full[n].reshape(N_DEV, r, cdim) if n == "w_out"
               else grads_full[n].reshape(r, N_DEV, cdim).transpose(1, 0, 2)).astype(BF16) for n, r, cdim in SHARDED[1:]]
    dmod = jnp.concatenate([dshift, dscale, dgate], axis=1)
    small = jnp.concatenate([dmod, g_gq, g_gkv, g_mu_r, g_mu_k, g_mu_v, g_mu_l, g_w0, g_a0, g_k_k, g_k_a, g_r_k,
                             g_gn_g, g_gn_b, g_post_g, g_post_b, loss_row], axis=1)
    my_x, my_y, my_c = lax.axis_index("x"), lax.axis_index("y"), lax.axis_index("c")
    chip_order = [4 * (my_x ^ fx) + 2 * (my_y ^ fy) for fx, fy in ((1, 1), (1, 0), (0, 1), (0, 0))]
    order = jnp.stack([ch + (1 - my_c) for ch in chip_order] + [ch + my_c for ch in chip_order]).astype(jnp.int32)
    *got_blocks, got_small = in_weight_grad_exchange(h_t, dproj_blocks, blocks, small, order)
    loss = jnp.sum(got_small[:, 0, SMALL_ELEMS])

    ada_cols = w_ada.shape[2]
    dmod_all = got_small[:, 0, :3 * D_MODEL]
    g_ada = ada_weight_grad(c_all, lax.dynamic_slice_in_dim(dmod_all, me * ada_cols, ada_cols, axis=1))

    outs = [dict() for _ in range(4)]
    res = adamw(g_ada[None], w_ada, m_w_ada, v_w_ada, "adamw_w_ada")
    for kind in range(4):
        outs[kind]["w_ada"] = res[kind]
    for (n, r, cdim), got in zip(SHARDED, got_blocks):
        res = adamw(got, weights[n], mom1[n], mom2[n], "adamw_" + n)
        for kind in range(4):
            outs[kind][n] = res[kind]
    rows_of = lambda tree: [tree[n].reshape(1, -1) for n, _ in SMALL]
    res = adamw_small(got_small, rows_of(weights), rows_of(mom1), rows_of(mom2))
    for kind in range(4):
        for a, (n, _) in enumerate(SMALL):
            outs[kind][n] = res[kind * len(SMALL) + a].reshape(weights[n].shape)
    return (loss, dx[None], *[outs[0][n] for n in names], *[outs[1][n] for n in names],
            *[outs[2][n] for n in names], *[outs[3][n] for n in names])
```

```python
import functools
import math

import jax
import jax.numpy as jnp
from jax import lax
from jax.experimental import pallas as pl
from jax.experimental.pallas import tpu as pltpu

F32 = jnp.float32
BF16 = jnp.bfloat16
HIGHEST = lax.Precision.HIGHEST
MESH_IDS = pl.DeviceIdType.MESH

N_DEV = 8
D_MODEL = 1024
LN_EPS = 1e-5
RMS_EPS = 1e-6
GN_EPS = 64e-5
HEADS = 8
Q_RANK = 256
KV_RANK = 128
ROPE = 32
NOPE = 64
QK_DIM = NOPE + ROPE
WIDTH = 512
HEAD = 64
LORA = 64
CHUNK = 64
DEPTH = 1
ALPHA = (2.0 * DEPTH) ** 0.25
ROPE_THETA = 10000.0
ATTN_SCALE = QK_DIM ** -0.5
DECAY_SCALE = math.exp(-0.5)

ADAM_LR = 0.001
ADAM_B1 = 0.9
ADAM_B2 = 0.999
ADAM_EPS = 1e-08
ADAM_WD = 0.01
ADAM_STEP = 10

LANE = 128
PAIR = 2 * HEAD
ROW_TILE = 256
HALO_ROWS = 16
ATTN_FWD_TILES = (512, 1024)
ATTN_BWD_TILES = (512, 512)
LOG2_E = math.log2(math.e)
Q_PRESCALE = ATTN_SCALE * LOG2_E
WKV_CHUNKS_PER_STEP = 8
WGRAD_SLOTS = (0, 1, 4, 2, 5, 6, 3, 7)
VMEM_LIMIT = 56 * 1024 * 1024

P_MA, P_MB, P_R, P_K, P_V, P_GPA, P_GPB, P_QC, P_KVC, P_KR, P_KRR, P_LORA = (
    0, 1024, 2048, 2560, 3072, 3584, 4096, 4608, 4864, 4992, 5120, 5248)
P_WIDTH = 5376
DW_BLOCK = 768

N_QC, N_KVC, N_KROPE, N_GPA, N_RWKV, N_GPB, N_MA, N_MB = 0, 256, 384, 416, 928, 2592, 3104, 4128
IN_WIDTH = 5152

SHARDED = (("w_in", 1024, 644), ("w_uq", 256, 96), ("w_ukv", 128, 128), ("w_decay_up", 64, 64),
           ("w_iclr_up", 64, 64), ("w_proj_a", 512, 128), ("w_proj_b", 512, 128), ("w_out", 128, 1024))
SHARD_ELEMS = sum(r * c for _, r, c in SHARDED)
SHARD_ROWS = SHARD_ELEMS // LANE
GATHER_ROWS = SHARD_ROWS + 2 * D_MODEL // LANE
SMALL = (("b_ada", 3072), ("q_norm_g", 256), ("kv_norm_g", 128), ("mu_rwkv", 1664), ("w0", 512), ("a0", 512),
         ("k_k", 512), ("k_a", 512), ("r_k", 512), ("gn_g", 512), ("gn_b", 512), ("post_g", 1024), ("post_b", 1024))
SMALL_ELEMS = sum(n for _, n in SMALL)
SMALL_ROWS = SMALL_ELEMS // LANE


def mm(a, b):
    return jnp.dot(a.astype(BF16), b.astype(BF16), preferred_element_type=F32)


def mm_nt(a, b):
    return lax.dot_general(a.astype(BF16), b.astype(BF16), (((1,), (1,)), ((), ())), preferred_element_type=F32)


def mm_tn(a, b):
    return lax.dot_general(a.astype(BF16), b.astype(BF16), (((0,), (0,)), ((), ())), preferred_element_type=F32)


def hdot(a, b):
    return jnp.dot(a, b, precision=HIGHEST, preferred_element_type=F32)


def hdot_nt(a, b):
    return lax.dot_general(a, b, (((1,), (1,)), ((), ())), precision=HIGHEST, preferred_element_type=F32)


def hdot_tn(a, b):
    return lax.dot_general(a, b, (((0,), (0,)), ((), ())), precision=HIGHEST, preferred_element_type=F32)


def sigmoid(x):
    return 1.0 / (1.0 + jnp.exp(-x))


def colsum(x):
    return jnp.sum(x, axis=0, keepdims=True)


def rowmean(x):
    return jnp.mean(x, axis=-1, keepdims=True)


def layer_norm_stats(x):
    xc = x - rowmean(x)
    rstd = lax.rsqrt(rowmean(xc * xc) + LN_EPS)
    return xc * rstd, rstd


def layer_norm_bwd(dy, xhat, rstd):
    return rstd * (dy - rowmean(dy) - xhat * rowmean(dy * xhat))


def bf16_pieces(x, n):
    pieces = []
    for _ in range(n):
        p = x.astype(BF16)
        pieces.append(p)
        x = x - p.astype(F32)
    return pieces


def ones_dot(ones, x, n_pieces):
    ones = ones.astype(BF16)
    return sum(jnp.dot(ones, p, preferred_element_type=F32) for p in bf16_pieces(x, n_pieces))


def head_sum(x, bd):
    bd = bd.astype(BF16)
    out = []
    for p in range(x.shape[1] // LANE):
        hi, lo = bf16_pieces(x[:, p * LANE:(p + 1) * LANE], 2)
        out.append(jnp.dot(hi, bd, preferred_element_type=F32) + jnp.dot(lo, bd, preferred_element_type=F32))
    return jnp.concatenate(out, axis=1)


def tile_lanes(t, n):
    return jnp.concatenate([t] * n, axis=1)


def row_iota(shape):
    return lax.broadcasted_iota(jnp.int32, shape, 0)


def lane_iota(shape):
    return lax.broadcasted_iota(jnp.int32, shape, 1)


def shift_rows_down(x, row0):
    rolled = pltpu.roll(x, 1, axis=0)
    return jnp.where(row_iota(x.shape) == 0, row0, rolled)


def shift_rows_up(x, row_last):
    rolled = pltpu.roll(x, x.shape[0] - 1, axis=0)
    return jnp.where(row_iota(x.shape) == x.shape[0] - 1, row_last, rolled)


def row_call(name, fn, n_rows, row_in, const_in, row_out, acc_out=(), halo_in=(), carry=(), reverse=False):
    ts = ROW_TILE
    n_tiles = n_rows // ts
    n_in = len(row_in) + len(halo_in) + len(const_in)
    n_ro, n_ao = len(row_out), len(acc_out)

    def tile_of(g):
        return (n_tiles - 1 - g) if reverse else g

    def body(*refs):
        ins = refs[:n_in]
        ro = refs[n_in:n_in + n_ro]
        ao = refs[n_in + n_ro:n_in + n_ro + n_ao]
        cr = refs[n_in + n_ro + n_ao:]
        g = pl.program_id(0)
        step0 = g == 0
        tile0 = tile_of(g) == 0
        for r in cr:
            @pl.when(step0)
            def _(r=r):
                r[...] = jnp.zeros_like(r)
        n_tiled = len(row_in) + len(halo_in)
        vals = [r[...].astype(F32) for r in ins[:n_tiled]] + [r[...] for r in ins[n_tiled:]]
        outs = fn(step0, tile0, *vals, *[c[0:1, :] for c in cr])
        for r, v in zip(ro, outs[:n_ro]):
            r[...] = v.astype(r.dtype)
        for r, v in zip(ao, outs[n_ro:n_ro + n_ao]):
            @pl.when(step0)
            def _(r=r, v=v):
                r[...] = v.astype(r.dtype)

            @pl.when(jnp.logical_not(step0))
            def _(r=r, v=v):
                r[...] += v.astype(r.dtype)
        for r, v in zip(cr, outs[n_ro + n_ao:]):
            r[0:1, :] = v

    in_specs = [pl.BlockSpec((ts, w), functools.partial(lambda g, cb: (tile_of(g), cb), cb=cb)) for _, w, cb in row_in]
    in_specs += [pl.BlockSpec((HALO_ROWS, w), functools.partial(
        lambda g, cb: (jnp.maximum(tile_of(g) * (ts // HALO_ROWS) - 1, 0), cb), cb=cb)) for _, w, cb in halo_in]
    in_specs += [pl.BlockSpec(memory_space=pltpu.VMEM) for _ in const_in]
    out_specs = [pl.BlockSpec((ts, w), lambda g: (tile_of(g), 0)) for w, _ in row_out]
    out_specs += [pl.BlockSpec(s, lambda g: (0, 0)) for s, _ in acc_out]
    out_shape = [jax.ShapeDtypeStruct((n_rows, w), d) for w, d in row_out]
    out_shape += [jax.ShapeDtypeStruct(s, d) for s, d in acc_out]
    return pl.pallas_call(
        body, name=name, grid=(n_tiles,), in_specs=in_specs, out_specs=out_specs, out_shape=out_shape,
        scratch_shapes=[pltpu.VMEM((8, w), F32) for w in carry],
        compiler_params=pltpu.CompilerParams(dimension_semantics=("arbitrary",), vmem_limit_bytes=VMEM_LIMIT),
    )(*[a for a, _, _ in row_in], *[a for a, _, _ in halo_in], *const_in)


def my_position():
    return lax.axis_index("x"), lax.axis_index("y"), lax.axis_index("c")


def flip(pos, k):
    x, y, c = pos
    dx, dy, dc = (k >> 2) & 1, (k >> 1) & 1, k & 1
    return (1 - x if dx else x, 1 - y if dy else y, 1 - c if dc else c)


def flat_index(pos):
    return 4 * pos[0] + 2 * pos[1] + pos[2]


def gather_shards(shards):
    n = len(shards)

    def body(*refs):
        x_refs, out_refs = refs[:n], refs[n:2 * n]
        send_sems, recv_sems, local_sems = refs[2 * n:]
        x, y, c = my_position()
        me, sibling = (x, y, c), (x, y, 1 - c)
        chips = [(1 - x, y), (x, 1 - y), (1 - x, 1 - y)]

        def copy(a, k, block, to, from_input=False):
            slot = out_refs[a].at[flat_index(block)]
            return pltpu.make_async_remote_copy(
                src_ref=x_refs[a] if from_input else slot, dst_ref=slot,
                send_sem=send_sems.at[7 * a + k], recv_sem=recv_sems.at[7 * a + k],
                device_id=to, device_id_type=MESH_IDS)

        mine = [pltpu.make_async_copy(x_refs[a], out_refs[a].at[flat_index(me)], local_sems.at[a]) for a in range(n)]
        for cp in mine:
            cp.start()
        first = []
        for a in range(n):
            first.append(copy(a, 0, me, sibling, from_input=True))
            first += [copy(a, 1 + j, me, (*chip, c), from_input=True) for j, chip in enumerate(chips)]
        for cp in first:
            cp.start()
        passed = []
        for j, chip in enumerate(chips):
            for a in range(n):
                copy(a, 1 + j, (*chip, c), me).wait_recv()
                cp = copy(a, 4 + j, (*chip, c), sibling)
                cp.start()
                passed.append(cp)
        for a in range(n):
            copy(a, 0, sibling, me).wait_recv()
            for j, chip in enumerate(chips):
                copy(a, 4 + j, (*chip, 1 - c), me).wait_recv()
        for cp in first + passed:
            cp.wait_send()
        for cp in mine:
            cp.wait()

    return pl.pallas_call(
        body, name="gather_shards",
        out_shape=[jax.ShapeDtypeStruct((N_DEV,) + s.shape, s.dtype) for s in shards],
        in_specs=[pl.BlockSpec(memory_space=pl.ANY)] * n, out_specs=[pl.BlockSpec(memory_space=pl.ANY)] * n,
        scratch_shapes=[pltpu.SemaphoreType.DMA((7 * n,)), pltpu.SemaphoreType.DMA((7 * n,)),
                        pltpu.SemaphoreType.DMA((n,))],
    )(*shards)


def ada_modulation(c_all, w_ada_loc, b_ada_blocks):
    cols = w_ada_loc.shape[1]

    def body(c_ref, w_ref, b_ref, out_ref, send_sems, recv_sems):
        me = my_position()
        mi = flat_index(me)
        cv = c_ref[...]
        res = hdot(cv * sigmoid(cv), w_ref[...]) + b_ref[pl.ds(mi, 1), :]
        out_ref[mi] = res
        sends = []
        for k in range(1, N_DEV):
            cp = pltpu.make_async_remote_copy(
                src_ref=out_ref.at[mi], dst_ref=out_ref.at[mi], send_sem=send_sems.at[k - 1],
                recv_sem=recv_sems.at[k - 1], device_id=flip(me, k), device_id_type=MESH_IDS)
            cp.start()
            sends.append(cp)
        for k in range(1, N_DEV):
            pi = flat_index(flip(me, k))
            pltpu.make_async_remote_copy(
                src_ref=out_ref.at[pi], dst_ref=out_ref.at[pi], send_sem=send_sems.at[k - 1],
                recv_sem=recv_sems.at[k - 1], device_id=flip(me, k), device_id_type=MESH_IDS).wait_recv()
        for cp in sends:
            cp.wait_send()

    return pl.pallas_call(
        body, name="ada_modulation",
        out_shape=jax.ShapeDtypeStruct((N_DEV, N_DEV, cols), F32),
        in_specs=[pl.BlockSpec(memory_space=pltpu.VMEM)] * 3, out_specs=pl.BlockSpec(memory_space=pltpu.VMEM),
        scratch_shapes=[pltpu.SemaphoreType.DMA((7,)), pltpu.SemaphoreType.DMA((7,))],
    )(c_all, w_ada_loc, b_ada_blocks)


def fwd_in_tile(step0, tile0, x, mod, w_in_p):
    xhat, _ = layer_norm_stats(x)
    h = xhat * (1.0 + mod[1:2]) + mod[0:1]
    return (mm(h, w_in_p),)


def fwd_in_gather(x, mod, w_in_p, shards):
    n = len(shards)
    n_rows = x.shape[0]
    ts = ROW_TILE
    n_tiles = n_rows // ts

    def body(x_ref, mod_ref, w_ref, *rest):
        s_refs = rest[:n]
        proj_ref, out_refs = rest[n], rest[n + 1:2 * n + 1]
        send_sems, recv_sems, local_sems = rest[2 * n + 1:]
        g = pl.program_id(0)
        me = my_position()
        mi = flat_index(me)

        def copies(k, slot):
            return [pltpu.make_async_remote_copy(
                src_ref=s_refs[a], dst_ref=out_refs[a].at[slot], send_sem=send_sems.at[7 * a + k - 1],
                recv_sem=recv_sems.at[7 * a + k - 1], device_id=flip(me, k), device_id_type=MESH_IDS)
                for a in range(n)]

        local = [pltpu.make_async_copy(s_refs[a], out_refs[a].at[mi], local_sems.at[a]) for a in range(n)]

        @pl.when(g == 0)
        def _():
            for cp in local:
                cp.start()
            for k in range(1, N_DEV):
                for cp in copies(k, mi):
                    cp.start()

        proj_ref[...] = fwd_in_tile(None, None, x_ref[...], mod_ref[...], w_ref[...])[0].astype(BF16)

        @pl.when(g == n_tiles - 1)
        def _():
            for k in range(1, N_DEV):
                for cp in copies(k, flat_index(flip(me, k))):
                    cp.wait_recv()
            for k in range(1, N_DEV):
                for cp in copies(k, mi):
                    cp.wait_send()
            for cp in local:
                cp.wait()

    hbm = pl.BlockSpec(memory_space=pl.ANY)
    const = pl.BlockSpec(memory_space=pltpu.VMEM)
    return pl.pallas_call(
        body, name="fwd_in_gather", grid=(n_tiles,),
        in_specs=[pl.BlockSpec((ts, D_MODEL), lambda g: (g, 0)), const, const] + [hbm] * n,
        out_specs=[pl.BlockSpec((ts, P_WIDTH), lambda g: (g, 0))] + [hbm] * n,
        out_shape=[jax.ShapeDtypeStruct((n_rows, P_WIDTH), BF16)]
        + [jax.ShapeDtypeStruct((N_DEV,) + s.shape, s.dtype) for s in shards],
        scratch_shapes=[pltpu.SemaphoreType.DMA((7 * n,)), pltpu.SemaphoreType.DMA((7 * n,)),
                        pltpu.SemaphoreType.DMA((n,))],
        compiler_params=pltpu.CompilerParams(dimension_semantics=("arbitrary",), vmem_limit_bytes=VMEM_LIMIT),
    )(x, mod, w_in_p, *shards)


def rms_norm_fwd(x, g):
    r = lax.rsqrt(rowmean(x * x) + RMS_EPS)
    xh = x * r
    return xh * g, xh, r


def key_rope_mask(shape):
    return (lane_iota(shape) >= NOPE).astype(F32)


def mla_prep_tile(step0, tile0, q_c, kv_c, kr, krr, cos, sin, gq, gkv, wq, wqr, wkn, wv):
    qn, _, _ = rms_norm_fwd(q_c, gq)
    kvn, _, _ = rms_norm_fwd(kv_c, gkv)
    q = (mm(qn, wq) * tile_lanes(cos, HEADS) + mm(qn, wqr) * tile_lanes(sin, HEADS)) * Q_PRESCALE
    kpe = kr * (cos * key_rope_mask(cos.shape)) + krr * sin
    k = mm(kvn, wkn) + tile_lanes(kpe, HEADS)
    v = mm(kvn, wv)
    return q, k, v


def rwkv_prep_core(tile0, r0, k0, v0, l0, hr, hk, hv, hl, mu_r, mu_k, mu_v, mu_l, w0, a0, k_k, k_a,
                   w_dec, w_iclr, tril, same, bd):
    def shifted(x, halo, mu):
        row0 = jnp.where(tile0, 0.0, halo[HALO_ROWS - 1:HALO_ROWS, :])
        prev = shift_rows_down(x, row0)
        return x + (prev - x) * mu, prev

    ur, pr = shifted(r0, hr, mu_r)
    uk, pk = shifted(k0, hk, mu_k)
    uv, pv = shifted(v0, hv, mu_v)
    ul, plo = shifted(l0, hl, mu_l)
    th = jnp.tanh(ul)
    sg = sigmoid(w0 + mm(th, w_dec))
    lw = -DECAY_SCALE * sg
    a_ic = sigmoid(a0 + mm(ul, w_iclr))
    kkraw = uk * k_k
    nrm_raw = jnp.sqrt(head_sum(kkraw * kkraw, bd))
    nrm = jnp.maximum(nrm_raw, 1e-12)
    kk = kkraw / nrm
    k2 = uk * (1.0 + (a_ic - 1.0) * k_a)
    lc = ones_dot(tril, lw, 3)
    lcl = ones_dot(same, lw, 3)
    return dict(ur=ur, uk=uk, uv=uv, ul=ul, pr=pr, pk=pk, pv=pv, pl=plo, th=th, sg=sg, lw=lw, a_ic=a_ic,
                kkraw=kkraw, nrm_raw=nrm_raw, nrm=nrm, kk=kk, k2=k2, lc=lc, lcl=lcl)


def rwkv_prep_tile(step0, tile0, r0, k0, v0, l0, hr, hk, hv, hl, *consts):
    f = rwkv_prep_core(tile0, r0, k0, v0, l0, hr, hk, hv, hl, *consts)
    lc, lw = f["lc"], f["lw"]
    e_neg = jnp.exp(-lc)
    rt = f["ur"] * jnp.exp(lc)
    at = -f["kk"] * jnp.exp(lc - lw)
    bt = f["kk"] * f["a_ic"] * e_neg
    kt = f["k2"] * e_neg
    return rt, at, bt, kt, jnp.exp(f["lcl"]), f["uv"], f["ur"], f["k2"]


def wkv_masks():
    lane = lane_iota((1, PAIR))
    m_lo = (lane < HEAD).astype(F32)
    ri = row_iota((CHUNK, CHUNK))
    ci = lane_iota((CHUNK, CHUNK))
    r2 = row_iota((PAIR, PAIR))
    c2 = lane_iota((PAIR, PAIR))
    bd = ((r2 < HEAD) == (c2 < HEAD)).astype(F32)
    eye2 = (r2 == c2).astype(F32)
    return (m_lo, 1.0 - m_lo), ri > ci, ri >= ci, (ri == ci).astype(F32), bd, eye2


def wkv_chunks_pre(chunks, masks):
    ms, strict, incl, eye, bd, eye2 = masks
    items = [(c, m) for c in range(len(chunks)) for m in ms]
    at, bt, kt, rt, v, cl = (list(t) for t in zip(*chunks))
    atm = [at[c] * m for c, m in items]
    rtm = [rt[c] * m for c, m in items]
    aab = [jnp.where(strict, mm_nt(x, bt[c]), 0.0) for x, (c, _) in zip(atm, items)]
    aak = [jnp.where(strict, mm_nt(x, kt[c]), 0.0) for x, (c, _) in zip(atm, items)]
    prb = [jnp.where(incl, mm_nt(x, bt[c]), 0.0) for x, (c, _) in zip(rtm, items)]
    prk = [jnp.where(incl, mm_nt(x, kt[c]), 0.0) for x, (c, _) in zip(rtm, items)]
    tinv = [eye + a for a in aab]
    power = aab
    for _ in range(5):
        power = [mm(p, p) for p in power]
        tinv = [t + mm(t, p) for t, p in zip(tinv, power)]

    def by_chunk(parts):
        return [parts[2 * c] + parts[2 * c + 1] for c in range(len(chunks))]

    w = by_chunk([mm(a, v[c] * m) for a, (c, m) in zip(aak, items)])
    ah = by_chunk([mm(t, x) for t, x in zip(tinv, atm)])
    wh = by_chunk([mm(t, w[c] * m) for t, (c, m) in zip(tinv, items)])
    rh = [r + d for r, d in zip(rt, by_chunk([mm(p, ah[c] * m) for p, (c, m) in zip(prb, items)]))]
    yh = by_chunk([mm(p, wh[c] * m) + mm(q, v[c] * m) for p, q, (c, m) in zip(prb, prk, items)])
    bc = [b * c_ for b, c_ in zip(bt, cl)]
    kc = [k * c_ for k, c_ in zip(kt, cl)]
    g = [eye2 * c_ + bd * mm_tn(b, a) for c_, b, a in zip(cl, bc, ah)]
    h = [bd * (mm_tn(b, w_) + mm_tn(k, v_)) for b, w_, k, v_ in zip(bc, wh, kc, v)]
    side = lambda parts: [jnp.concatenate([parts[2 * c], parts[2 * c + 1]], axis=1).astype(BF16)
                          for c in range(len(chunks))]
    saved = (side(tinv), side(aak), side(prb), side(prk), [a.astype(BF16) for a in ah], wh)
    return g, h, rh, yh, saved


def wkv_chunks_grad(chunks, saved, m0, dy, dm1, masks):
    ms, strict, incl, eye, bd, eye2 = masks
    n = len(chunks)
    at, bt, kt, rt, v, cl = (list(t) for t in zip(*chunks))
    items = [(c, m) for c in range(n) for m in ms]
    atm = [at[c] * m for c, m in items]
    rtm = [rt[c] * m for c, m in items]
    halves = lambda pairs: [x for pr in pairs for x in (pr[:, :CHUNK], pr[:, CHUNK:])]
    tinv, aak, prb, prk = (halves(s) for s in zip(*[(a, b, c_, d) for a, b, c_, d, _, _ in saved]))
    ah = [s[4] for s in saved]
    wh = [s[5] for s in saved]
    bc = [b * c_ for b, c_ in zip(bt, cl)]
    kc = [k * c_ for k, c_ in zip(kt, cl)]

    def by_chunk(parts):
        return [parts[2 * c] + parts[2 * c + 1] for c in range(n)]

    u = [mm(a, m) + w for a, m, w in zip(ah, m0, wh)]
    dm1 = [d * bd for d in dm1]
    dym = [dy[c] * m for c, m in items]
    du = [mm(b, d) + e for b, d, e in zip(bc, dm1, by_chunk([mm_tn(p, x) for p, x in zip(prb, dym)]))]
    dv = [mm(k, d) + e for k, d, e in zip(kc, dm1, by_chunk([mm_tn(p, x) for p, x in zip(prk, dym)]))]
    dz = by_chunk([mm_tn(t, du[c] * m) for t, (c, m) in zip(tinv, items)])
    dzm = [dz[c] * m for c, m in items]
    dv = [a + b for a, b in zip(dv, by_chunk([mm_tn(a_, x) for a_, x in zip(aak, dzm)]))]
    drt = [mm_nt(d, m) for d, m in zip(dy, m0)]
    dat = [mm_nt(d, m) for d, m in zip(dz, m0)]
    udm = [mm_nt(x, d) for x, d in zip(u, dm1)]
    vdm = [mm_nt(x, d) for x, d in zip(v, dm1)]
    daab = [jnp.where(strict, mm_nt(x, u[c]), 0.0) for x, (c, _) in zip(dzm, items)]
    daak = [jnp.where(strict, mm_nt(x, v[c]), 0.0) for x, (c, _) in zip(dzm, items)]
    dprb = [jnp.where(incl, mm_nt(x, u[c]), 0.0) for x, (c, _) in zip(dym, items)]
    dprk = [jnp.where(incl, mm_nt(x, v[c]), 0.0) for x, (c, _) in zip(dym, items)]
    drt2 = by_chunk([(mm(p, bt[c]) + mm(q, kt[c])) * m for p, q, (c, m) in zip(dprb, dprk, items)])
    dat2 = by_chunk([(mm(p, bt[c]) + mm(q, kt[c])) * m for p, q, (c, m) in zip(daab, daak, items)])
    dbt2 = by_chunk([mm_tn(p, r) + mm_tn(a_, x) for p, r, a_, x in zip(dprb, rtm, daab, atm)])
    dkt2 = by_chunk([mm_tn(p, r) + mm_tn(a_, x) for p, r, a_, x in zip(dprk, rtm, daak, atm)])
    ones = jnp.ones((8, PAIR), F32)
    upper = (lane_iota((CHUNK, CHUNK)) >= row_iota((CHUNK, CHUNK))).astype(F32)
    out = []
    for c in range(n):
        drt_c = drt[c] + drt2[c]
        dat_c = dat[c] + dat2[c]
        dbt_c = udm[c] * cl[c] + dbt2[c]
        dkt_c = vdm[c] * cl[c] + dkt2[c]
        dlcl = hdot_nt(ones, dm1[c] * m0[c])[0:1, :] * cl[c] + colsum(bc[c] * udm[c] + kc[c] * vdm[c])
        g = drt_c * rt[c] - dbt_c * bt[c] - dkt_c * kt[c] + dat_c * at[c]
        dlw = hdot(upper, g) - dat_c * at[c] + dlcl
        out.append((dat_c, dbt_c, dkt_c, drt_c, dv[c], dlw))
    return out


def wkv_forward(at, bt, kt, rt, v, clf):
    n_rows = at.shape[0]
    cps = WKV_CHUNKS_PER_STEP
    rb = cps * CHUNK
    n_steps = n_rows // rb

    def body(a_ref, b_ref, k_ref, r_ref, v_ref, c_ref, y_ref, m0_ref, g_ref, rh_ref, *rest):
        saved_refs, m_scr = rest[:6], rest[6]

        @pl.when(pl.program_id(1) == 0)
        def _():
            m_scr[...] = jnp.zeros_like(m_scr)

        masks = wkv_masks()
        chunks = []
        for cc in range(cps):
            sl = slice(cc * CHUNK, (cc + 1) * CHUNK)
            chunks.append((a_ref[sl, :], b_ref[sl, :], k_ref[sl, :], r_ref[sl, :], v_ref[sl, :],
                           c_ref[cc * CHUNK:cc * CHUNK + 1, :]))
        gs, hs, rhs, yhs, saved = wkv_chunks_pre(chunks, masks)
        for ref, per_chunk in zip(saved_refs, saved):
            for cc, val in enumerate(per_chunk):
                ref[cc * CHUNK:(cc + 1) * CHUNK, :] = val
        m = m_scr[...]
        for cc, (g, h, rh, yh) in enumerate(zip(gs, hs, rhs, yhs)):
            sl = slice(cc * CHUNK, (cc + 1) * CHUNK)
            m0_ref[0, cc] = m
            g_ref[0, cc] = g
            rh_ref[sl, :] = rh
            y_ref[sl, :] = hdot(rh, m) + yh
            m = hdot(g, m) + h
        m_scr[...] = m

    blk = pl.BlockSpec((rb, PAIR), lambda p, s: (s, p))
    state_blk = pl.BlockSpec((1, cps, PAIR, PAIR), lambda p, s: (p, s, 0, 0))
    state_shape = jax.ShapeDtypeStruct((WIDTH // PAIR, n_rows // CHUNK, PAIR, PAIR), F32)
    rows_f32 = jax.ShapeDtypeStruct((n_rows, WIDTH), F32)
    rows_bf16 = jax.ShapeDtypeStruct((n_rows, WIDTH), BF16)
    return pl.pallas_call(
        body, name="wkv_forward", grid=(WIDTH // PAIR, n_steps),
        in_specs=[blk] * 6,
        out_specs=[blk, state_blk, state_blk, blk] + [blk] * 6,
        out_shape=[rows_f32, state_shape, state_shape, rows_f32] + [rows_bf16] * 5 + [rows_f32],
        scratch_shapes=[pltpu.VMEM((PAIR, PAIR), F32)],
        compiler_params=pltpu.CompilerParams(dimension_semantics=("arbitrary", "arbitrary"),
                                             vmem_limit_bytes=VMEM_LIMIT),
    )(at, bt, kt, rt, v, clf)


def wkv_backward(at, bt, kt, rt, v, clf, m0s, gs, rh, saved, dy):
    n_rows = at.shape[0]
    cps = WKV_CHUNKS_PER_STEP
    rb = cps * CHUNK
    n_steps = n_rows // rb

    def body(a_ref, b_ref, k_ref, r_ref, v_ref, c_ref, m0_ref, g_ref, rh_ref, *rest):
        saved_refs, dy_ref = rest[:6], rest[6]
        da_ref, db_ref, dk_ref, dr_ref, dv_ref, dlw_ref, dm_scr = rest[7:]

        @pl.when(pl.program_id(1) == 0)
        def _():
            dm_scr[...] = jnp.zeros_like(dm_scr)

        masks = wkv_masks()
        bd = masks[4]
        dm = dm_scr[...]
        dm1 = [None] * cps
        for cc in reversed(range(cps)):
            sl = slice(cc * CHUNK, (cc + 1) * CHUNK)
            dm1[cc] = dm
            dm = bd * (hdot_tn(g_ref[0, cc], dm) + hdot_tn(rh_ref[sl, :], dy_ref[sl, :]))
        dm_scr[...] = dm
        chunks, kept, m0, dys = [], [], [], []
        for cc in range(cps):
            sl = slice(cc * CHUNK, (cc + 1) * CHUNK)
            chunks.append((a_ref[sl, :], b_ref[sl, :], k_ref[sl, :], r_ref[sl, :], v_ref[sl, :],
                           c_ref[cc * CHUNK:cc * CHUNK + 1, :]))
            kept.append(tuple(ref[sl, :] for ref in saved_refs))
            m0.append(m0_ref[0, cc])
            dys.append(dy_ref[sl, :])
        grads = wkv_chunks_grad(chunks, kept, m0, dys, dm1, masks)
        for cc, (dat, dbt, dkt, drt, dv, dlw) in enumerate(grads):
            sl = slice(cc * CHUNK, (cc + 1) * CHUNK)
            da_ref[sl, :] = dat
            db_ref[sl, :] = dbt
            dk_ref[sl, :] = dkt
            dr_ref[sl, :] = drt
            dv_ref[sl, :] = dv
            dlw_ref[sl, :] = dlw

    blk = pl.BlockSpec((rb, PAIR), lambda p, s: (n_steps - 1 - s, p))
    state_blk = pl.BlockSpec((1, cps, PAIR, PAIR), lambda p, s: (p, n_steps - 1 - s, 0, 0))
    return pl.pallas_call(
        body, name="wkv_backward", grid=(WIDTH // PAIR, n_steps),
        in_specs=[blk] * 6 + [state_blk, state_blk, blk] + [blk] * 6 + [blk],
        out_specs=[blk] * 6,
        out_shape=[jax.ShapeDtypeStruct((n_rows, WIDTH), F32)] * 6,
        scratch_shapes=[pltpu.VMEM((PAIR, PAIR), F32)],
        compiler_params=pltpu.CompilerParams(dimension_semantics=("arbitrary", "arbitrary"),
                                             vmem_limit_bytes=VMEM_LIMIT),
    )(at, bt, kt, rt, v, clf, m0s, gs, rh, *saved, dy)


def visible(q_row0, k_row0, shape):
    qc = (q_row0 + row_iota(shape)) // CHUNK
    kc = (k_row0 + lane_iota(shape)) // CHUNK
    return kc <= qc


def attention_forward(q, k, v):
    n_rows = q.shape[0]
    tq, tk = ATTN_FWD_TILES
    n_q = n_rows // tq
    assert tk % tq == 0

    def body(q_ref, k_ref, v_ref, o_ref, lse_ref):
        i = pl.program_id(1)
        lane = lane_iota((tq, LANE))
        heads = [slice(0, LANE), slice(LANE, 2 * LANE)]
        qs = [q_ref[:, cols] for cols in heads]

        def step(j, carry, size, masked):
            rows = pl.ds(pl.multiple_of(j * size, size), size)
            ss = [mm_nt(qh, k_ref[rows, cols]) for qh, cols in zip(qs, heads)]
            if masked:
                vis = visible(i * tq, j * size, ss[0].shape)
                ss = [jnp.where(vis, s, -jnp.inf) for s in ss]
            ps, stats = [], []
            for s, (m, l, _) in zip(ss, carry):
                m_new = jnp.maximum(m, jnp.max(s, axis=-1, keepdims=True))
                p = jnp.exp2(s - m_new)
                alpha = jnp.exp2(m - m_new)
                ps.append(p)
                stats.append((m_new, alpha, alpha * l + jnp.sum(p, axis=-1, keepdims=True)))
            pvs = [mm(p, v_ref[rows, cols]) for p, cols in zip(ps, heads)]
            return tuple((m_new, l, alpha * acc + pv)
                         for (m_new, alpha, l), (_, _, acc), pv in zip(stats, carry, pvs))

        carry = tuple((jnp.full((tq, 1), -jnp.inf, F32), jnp.zeros((tq, 1), F32), jnp.zeros((tq, LANE), F32))
                      for _ in heads)
        n_full = (i * tq) // tk
        carry = lax.fori_loop(0, n_full, functools.partial(step, size=tk, masked=False), carry)
        (m0, l0, acc0), (m1, l1, acc1) = step(n_full, carry, size=tk, masked=True)
        o_ref[...] = acc0 / l0 + acc1 / l1
        lse_ref[...] = jnp.where(lane >= HEAD, m1 + jnp.log2(l1), m0 + jnp.log2(l0))

    return pl.pallas_call(
        body, name="attention_forward", grid=(HEADS // 2, n_q),
        in_specs=[pl.BlockSpec((tq, 2 * LANE), lambda p, i: (i, p)),
                  pl.BlockSpec((n_rows, 2 * LANE), lambda p, i: (0, p)),
                  pl.BlockSpec((n_rows, 2 * LANE), lambda p, i: (0, p))],
        out_specs=[pl.BlockSpec((tq, LANE), lambda p, i: (i, p))] * 2,
        out_shape=[jax.ShapeDtypeStruct((n_rows, WIDTH), F32)] * 2,
        compiler_params=pltpu.CompilerParams(dimension_semantics=("arbitrary", "arbitrary"),
                                             vmem_limit_bytes=VMEM_LIMIT),
    )(q, k, v)


def attention_backward(q, k, v, o, do, lse):
    n_rows = q.shape[0]
    tq, tk = ATTN_BWD_TILES
    n_q = n_rows // tq
    n_masked = max(1, tk // tq)

    def body(q_ref, k_ref, v_ref, o_ref, do_ref, lse_ref, dq_ref, dk_ref, dv_ref):
        j = pl.program_id(1)

        @pl.when(j == 0)
        def _():
            dq_ref[...] = jnp.zeros_like(dq_ref)

        lane = lane_iota((tq, LANE))
        heads = [slice(0, LANE), slice(LANE, 2 * LANE)]
        ks = [k_ref[:, cols] for cols in heads]
        vs = [v_ref[:, cols] for cols in heads]
        head_lanes = [(lane < HEAD).astype(F32), (lane >= HEAD).astype(F32)]

        def step(i, carry, masked):
            rows = pl.ds(pl.multiple_of(i * tq, tq), tq)
            qs = [q_ref[rows, cols] for cols in heads]
            dout = do_ref[rows, :]
            dout_o = dout * o_ref[rows, :]
            lse_t = lse_ref[rows, :]
            ss = [mm_nt(qh, kh) for qh, kh in zip(qs, ks)]
            dps = [mm_nt(dout, vh) for vh in vs]
            ps, dss = [], []
            for hh in range(2):
                delta = jnp.sum(dout_o * head_lanes[hh], axis=-1, keepdims=True)
                lse_h = jnp.sum(jnp.where(lane == hh * HEAD, lse_t, 0.0), axis=-1, keepdims=True)
                p = jnp.exp2(ss[hh] - lse_h)
                if masked:
                    p = jnp.where(visible(i * tq, j * tk, p.shape), p, 0.0)
                ps.append(p)
                dss.append(p * (dps[hh] - delta))
            dvs = [mm_tn(p, dout) for p in ps]
            dqs = [mm(ds, kh) for ds, kh in zip(dss, ks)]
            dks = [mm_tn(ds, qh) for ds, qh in zip(dss, qs)]
            for cols, dq in zip(heads, dqs):
                dq_ref[rows, cols] += dq * ATTN_SCALE
            return tuple((dk + a, dv + b) for (dk, dv), a, b in zip(carry, dks, dvs))

        carry = tuple((jnp.zeros((tk, LANE), F32), jnp.zeros((tk, LANE), F32)) for _ in heads)
        i_first = (j * tk) // tq
        for extra in range(n_masked):
            carry = step(i_first + extra, carry, masked=True)
        carry = lax.fori_loop(i_first + n_masked, n_q, functools.partial(step, masked=False), carry)
        for cols, (dk, dv) in zip(heads, carry):
            dk_ref[:, cols] = dk * (1.0 / LOG2_E)
            dv_ref[:, cols] = dv

    full = lambda w: pl.BlockSpec((n_rows, w), lambda p, j: (0, p))
    blk = pl.BlockSpec((tk, 2 * LANE), lambda p, j: (j, p))
    return pl.pallas_call(
        body, name="attention_backward", grid=(HEADS // 2, n_rows // tk),
        in_specs=[full(2 * LANE), blk, blk, full(LANE), full(LANE), full(LANE)],
        out_specs=[full(2 * LANE), blk, blk],
        out_shape=[jax.ShapeDtypeStruct((n_rows, HEADS * LANE), F32)] * 3,
        compiler_params=pltpu.CompilerParams(dimension_semantics=("arbitrary", "arbitrary"),
                                             vmem_limit_bytes=VMEM_LIMIT),
    )(q, k, v, o, do, lse)


def tail_tile(step0, tile0, x, tgt, ma, mb, gpa, gpb, ya, y, ur, k2, uv,
              mod, wpa, wpb, wout, gn_g, gn_b, r_k, post_g, post_b, bd):
    gate = mod[2:3]
    inv = 1.0 / HEAD
    yc = y - head_sum(y, bd) * inv
    rs = lax.rsqrt(head_sum(yc * yc, bd) * inv + GN_EPS)
    yn = yc * rs
    yb = yn * gn_g + gn_b + head_sum(ur * k2 * r_k, bd) * uv
    sga, sgb = sigmoid(gpa), sigmoid(gpb)
    sila, silb = gpa * sga, gpb * sgb
    ga, gb = ya * sila, yb * silb
    pa, pb = mm(ga, wpa), mm(gb, wpb)
    sa, sb = sigmoid(ma), sigmoid(mb)
    merged = sa * pa + sb * pb
    sub = mm(merged, wout)
    z = ALPHA * x + (1.0 + gate) * sub
    zhat, rstd = layer_norm_stats(z)
    err = zhat * post_g + post_b - tgt
    loss = 0.5 * jnp.sum(rowmean(err * err), axis=0, keepdims=True) + jnp.zeros((1, LANE), F32)
    dout = err * (1.0 / D_MODEL)
    dpost_g = colsum(dout * zhat)
    dpost_b = colsum(dout)
    dz = layer_norm_bwd(dout * post_g, zhat, rstd)
    dgate = colsum(dz * sub)
    dsub = dz * (1.0 + gate)
    dwout = mm_tn(merged, dsub)
    dmerged = mm_nt(dsub, wout)
    dpa, dpb = dmerged * sa, dmerged * sb
    dma = dmerged * pa * sa * (1.0 - sa)
    dmb = dmerged * pb * sb * (1.0 - sb)
    dwpa = mm_tn(ga, dpa)
    dwpb = mm_tn(gb, dpb)
    dga = mm_nt(dpa, wpa)
    dgb = mm_nt(dpb, wpb)
    dya = dga * sila
    dgpa = dga * ya * (sga * (1.0 + gpa * (1.0 - sga)))
    dyb = dgb * silb
    dgpb = dgb * yb * (sgb * (1.0 + gpb * (1.0 - sgb)))
    dgn_g = colsum(dyb * yn)
    dgn_b = colsum(dyb)
    dyn = dyb * gn_g
    dy = rs * (dyn - head_sum(dyn, bd) * inv - yn * head_sum(dyn * yn, bd) * inv)
    return (dz, dma, dmb, dgpa, dgpb, dya, dy, dyb,
            loss, dwout, dwpa, dwpb, dgn_g, dgn_b, dpost_g, dpost_b, dgate)


def mla_prep_bwd_tile(step0, tile0, q_c, kv_c, cos, sin, dq, dk, dv, gq, gkv, wq, wqr, wkn, wv):
    qn, qh, rq = rms_norm_fwd(q_c, gq)
    kvn, kvh, rkv = rms_norm_fwd(kv_c, gkv)
    dqc = dq * tile_lanes(cos, HEADS)
    dqs = dq * tile_lanes(sin, HEADS)
    dqn = mm_nt(dqc, wq) + mm_nt(dqs, wqr)
    dkvn = mm_nt(dk, wkn) + mm_nt(dv, wv)
    dkpe = dk[:, 0:LANE]
    for h in range(1, HEADS):
        dkpe = dkpe + dk[:, h * LANE:(h + 1) * LANE]
    dkr = dkpe * (cos * key_rope_mask(cos.shape))
    dkrr = dkpe * sin

    def rms_bwd(dyv, xh, r, g):
        dyg = dyv * g
        return r * (dyg - xh * rowmean(dyg * xh)), colsum(dyv * xh)

    dq_c, dgq = rms_bwd(dqn, qh, rq, gq)
    dkv_c, dgkv = rms_bwd(dkvn, kvh, rkv, gkv)
    return (dq_c, dkv_c, dkr, dkrr,
            mm_tn(qn, dqc), mm_tn(qn, dqs), mm_tn(kvn, dk), mm_tn(kvn, dv), dgq, dgkv)


def rwkv_prep_bwd_tile(step0, tile0, r0, k0, v0, l0, drt, dat, dbt, dkt, dvv, dlw, dyb, hr, hk, hv, hl,
                       mu_r, mu_k, mu_v, mu_l, w0, a0, k_k, k_a, w_dec, w_iclr, tril, same, bd, r_k,
                       cr, ck, cv, cl_):
    f = rwkv_prep_core(tile0, r0, k0, v0, l0, hr, hk, hv, hl, mu_r, mu_k, mu_v, mu_l, w0, a0, k_k, k_a,
                       w_dec, w_iclr, tril, same, bd)
    ur, uk, uv, ul, kk, k2, a_ic, sg, th = (f[n] for n in ("ur", "uk", "uv", "ul", "kk", "k2", "a_ic", "sg", "th"))
    lc, lw = f["lc"], f["lw"]
    e_neg = jnp.exp(-lc)
    dur = drt * jnp.exp(lc)
    da = dat * jnp.exp(lc - lw)
    db = dbt * e_neg
    dk2 = dkt * e_neg
    s = head_sum(ur * k2 * r_k, bd)
    duv = dvv + dyb * s
    ds = head_sum(dyb * uv, bd)
    dur = dur + ds * k2 * r_k
    dk2 = dk2 + ds * ur * r_k
    dr_k = colsum(ds * ur * k2)
    dkk = db * a_ic - da
    da_ic = db * kk + dk2 * uk * k_a
    duk = dk2 * (1.0 + (a_ic - 1.0) * k_a)
    dk_a = colsum(dk2 * uk * (a_ic - 1.0))
    dkkraw = jnp.where(f["nrm_raw"] > 1e-12, (dkk - kk * head_sum(dkk * kk, bd)) / f["nrm"], dkk * 1e12)
    duk = duk + dkkraw * k_k
    dk_k = colsum(dkkraw * uk)
    dai = da_ic * a_ic * (1.0 - a_ic)
    dd = dlw * (-DECAY_SCALE) * sg * (1.0 - sg)
    dul = mm_nt(dai, w_iclr) + mm_nt(dd, w_dec) * (1.0 - th * th)

    def unshift(du, x, prev, mu, carry_row):
        nxt = shift_rows_up(du, carry_row)
        return du * (1.0 - mu) + nxt * mu, colsum(du * (prev - x)), du[0:1, :]

    dr0, dmu_r, ncr = unshift(dur, r0, f["pr"], mu_r, cr)
    dk0, dmu_k, nck = unshift(duk, k0, f["pk"], mu_k, ck)
    dv0, dmu_v, ncv = unshift(duv, v0, f["pv"], mu_v, cv)
    dl0, dmu_l, ncl = unshift(dul, l0, f["pl"], mu_l, cl_)
    return (dr0, dk0, dv0, dl0,
            dmu_r, dmu_k, dmu_v, dmu_l, colsum(dd), colsum(dai), dk_k, dk_a, dr_k, mm_tn(th, dd), mm_tn(ul, dai),
            ncr, nck, ncv, ncl)


def in_backward(x, dz, pieces, mod, w_in_p, unrot):
    n_rows = x.shape[0]
    ts = ROW_TILE
    n_p = len(pieces)
    shard_cols = IN_WIDTH // N_DEV

    def body(*refs):
        x_ref, dz_ref = refs[:2]
        p_refs = refs[2:2 + n_p]
        mod_ref, w_ref, unrot_ref = refs[2 + n_p:5 + n_p]
        dx_ref, ht_ref, blocks_ref, dshift_ref, dscale_ref = refs[5 + n_p:]
        step0 = pl.program_id(0) == 0
        dma, dmb, dr0, dk0, dv0, dgpa, dgpb, dq_c, dkv_c, dkr, dkrr, dl0 = (r[...] for r in p_refs)
        dproj = jnp.concatenate([dma, dmb, dr0, dk0, dv0, dgpa, dgpb, dq_c, dkv_c, dkr, dkrr, dl0], axis=1)
        dh = mm_nt(dproj, w_ref[...])
        xhat, rstd = layer_norm_stats(x_ref[...])
        scale1 = 1.0 + mod_ref[1:2, :]
        dx_ref[...] = layer_norm_bwd(dh * scale1, xhat, rstd) + ALPHA * dz_ref[...]
        ht_ref[...] = jnp.transpose(xhat * scale1 + mod_ref[0:1, :]).astype(BF16)
        dkrope = (dkr.astype(F32) + mm(dkrr, unrot_ref[...]))[:, NOPE:QK_DIM]
        natural = jnp.concatenate(
            [dq_c.astype(F32), dkv_c.astype(F32), dkrope]
            + [p.astype(F32) for p in (dgpa, dr0, dk0, dv0, dl0, dgpb, dma, dmb)], axis=1)
        for j in range(N_DEV):
            blocks_ref[j] = natural[:, j * shard_cols:(j + 1) * shard_cols].astype(BF16)
        for ref, val in ((dshift_ref, colsum(dh)), (dscale_ref, colsum(dh * xhat))):
            @pl.when(step0)
            def _(ref=ref, val=val):
                ref[...] = val

            @pl.when(jnp.logical_not(step0))
            def _(ref=ref, val=val):
                ref[...] += val

    row = lambda w: pl.BlockSpec((ts, w), lambda i: (i, 0))
    const = pl.BlockSpec(memory_space=pltpu.VMEM)
    vec = pl.BlockSpec((1, D_MODEL), lambda i: (0, 0))
    return pl.pallas_call(
        body, name="in_backward", grid=(n_rows // ts,),
        in_specs=[row(D_MODEL), row(D_MODEL)] + [row(p.shape[1]) for p in pieces] + [const] * 3,
        out_specs=[row(D_MODEL), pl.BlockSpec((D_MODEL, ts), lambda i: (0, i)),
                   pl.BlockSpec((N_DEV, ts, shard_cols), lambda i: (0, i, 0)), vec, vec],
        out_shape=[jax.ShapeDtypeStruct((n_rows, D_MODEL), F32), jax.ShapeDtypeStruct((D_MODEL, n_rows), BF16),
                   jax.ShapeDtypeStruct((N_DEV, n_rows, shard_cols), BF16),
                   jax.ShapeDtypeStruct((1, D_MODEL), F32), jax.ShapeDtypeStruct((1, D_MODEL), F32)],
        compiler_params=pltpu.CompilerParams(dimension_semantics=("arbitrary",), vmem_limit_bytes=VMEM_LIMIT),
    )(x, dz, *pieces, mod, w_in_p, unrot)


def in_weight_grad_exchange(h_t, dp_blocks, others, small, order):
    n = len(others)
    n_rows = h_t.shape[1]
    ts = 2 * ROW_TILE
    n_i = n_rows // ts
    shard_cols = dp_blocks.shape[2]
    n_chips = N_DEV // 2
    last = N_DEV - 1

    def body(order_ref, h_ref, dp_ref, *rest):
        g_refs, s_ref = rest[:n], rest[n]
        rwin_ref, rg_refs, rs_ref = rest[n + 1], rest[n + 2:2 * n + 2], rest[2 * n + 2]
        (acc, sendbuf, sib_buf, sib_send, sib_recv, win_send, win_recv,
         o_send, o_recv, local_sems) = rest[2 * n + 3:]
        b, i = pl.program_id(0), pl.program_id(1)
        me = my_position()
        mi = flat_index(me)
        sibling = (me[0], me[1], 1 - me[2])

        def other_copies(k, src_index, dst_index):
            peer = flip(me, k)
            out = [pltpu.make_async_remote_copy(
                src_ref=g_refs[a].at[src_index], dst_ref=rg_refs[a].at[dst_index],
                send_sem=o_send.at[(n + 1) * (k - 1) + a], recv_sem=o_recv.at[(n + 1) * (k - 1) + a],
                device_id=peer, device_id_type=MESH_IDS) for a in range(n)]
            out.append(pltpu.make_async_remote_copy(
                src_ref=s_ref, dst_ref=rs_ref.at[dst_index],
                send_sem=o_send.at[(n + 1) * (k - 1) + n], recv_sem=o_recv.at[(n + 1) * (k - 1) + n],
                device_id=peer, device_id_type=MESH_IDS))
            return out

        def local_copies():
            out = [pltpu.make_async_copy(g_refs[a].at[mi], rg_refs[a].at[mi], local_sems.at[a]) for a in range(n)]
            out.append(pltpu.make_async_copy(s_ref, rs_ref.at[mi], local_sems.at[n]))
            return out

        def to_sibling(t):
            return pltpu.make_async_remote_copy(
                src_ref=sendbuf.at[t], dst_ref=sib_buf.at[t], send_sem=sib_send.at[t], recv_sem=sib_recv.at[t],
                device_id=sibling, device_id_type=MESH_IDS)

        def to_owner(t):
            flip_x = (t < 2) * 1
            flip_y = 1 - (t & 1)
            owner = (me[0] ^ flip_x, me[1] ^ flip_y, me[2])
            return pltpu.make_async_remote_copy(
                src_ref=sendbuf.at[n_chips + t], dst_ref=rwin_ref.at[t], send_sem=win_send.at[t],
                recv_sem=win_recv.at[t], device_id=owner, device_id_type=MESH_IDS)

        own_block = pltpu.make_async_copy(sendbuf.at[last], rwin_ref.at[n_chips - 1], local_sems.at[n + 1])

        @pl.when(jnp.logical_and(b == 0, i == 0))
        def _():
            for cp in local_copies():
                cp.start()
            for k in range(1, N_DEV):
                for cp in other_copies(k, flat_index(flip(me, k)), mi):
                    cp.start()

        contrib = jnp.dot(h_ref[...], dp_ref[...], preferred_element_type=F32)

        @pl.when(i == 0)
        def _():
            acc[...] = contrib

        @pl.when(i > 0)
        def _():
            acc[...] += contrib

        slot = order_ref[N_DEV + b]
        t = slot & (n_chips - 1)

        @pl.when(jnp.logical_and(i == n_i - 1, slot < n_chips))
        def _():
            sendbuf[slot] = acc[...].astype(BF16)
            to_sibling(t).start()

        @pl.when(jnp.logical_and(i == n_i - 1, slot >= n_chips))
        def _():
            to_sibling(t).wait_recv()
            sendbuf[slot] = (acc[...] + sib_buf[t].astype(F32)).astype(BF16)

            @pl.when(slot < last)
            def _():
                to_owner(t).start()

            @pl.when(slot == last)
            def _():
                own_block.start()

        @pl.when(jnp.logical_and(b == last, i == n_i - 1))
        def _():
            for t in range(n_chips - 1):
                to_owner(t).wait_recv()
            for k in range(1, N_DEV):
                pi = flat_index(flip(me, k))
                for cp in other_copies(k, pi, pi):
                    cp.wait_recv()
            for t in range(n_chips):
                to_sibling(t).wait_send()
            for t in range(n_chips - 1):
                to_owner(t).wait_send()
            for k in range(1, N_DEV):
                for cp in other_copies(k, flat_index(flip(me, k)), mi):
                    cp.wait_send()
            for cp in local_copies():
                cp.wait()
            own_block.wait()

    hbm = pl.BlockSpec(memory_space=pl.ANY)
    n_sem = 7 * (n + 1)
    grid_spec = pltpu.PrefetchScalarGridSpec(
        num_scalar_prefetch=1, grid=(N_DEV, n_i),
        in_specs=[pl.BlockSpec((D_MODEL, ts), lambda b, i, order: (0, i)),
                  pl.BlockSpec((None, ts, shard_cols), lambda b, i, order: (order[b], i, 0))] + [hbm] * (n + 1),
        out_specs=[hbm] * (n + 2),
        scratch_shapes=[pltpu.VMEM((D_MODEL, shard_cols), F32), pltpu.VMEM((N_DEV, D_MODEL, shard_cols), BF16),
                        pltpu.VMEM((n_chips, D_MODEL, shard_cols), BF16),
                        pltpu.SemaphoreType.DMA((n_chips,)), pltpu.SemaphoreType.DMA((n_chips,)),
                        pltpu.SemaphoreType.DMA((n_chips - 1,)), pltpu.SemaphoreType.DMA((n_chips - 1,)),
                        pltpu.SemaphoreType.DMA((n_sem,)), pltpu.SemaphoreType.DMA((n_sem,)),
                        pltpu.SemaphoreType.DMA((n + 2,))])
    return pl.pallas_call(
        body, name="in_weight_grad_exchange", grid_spec=grid_spec,
        out_shape=[jax.ShapeDtypeStruct((n_chips, D_MODEL, shard_cols), BF16)]
        + [jax.ShapeDtypeStruct(o.shape, o.dtype) for o in others]
        + [jax.ShapeDtypeStruct((N_DEV,) + small.shape, small.dtype)],
        compiler_params=pltpu.CompilerParams(dimension_semantics=("arbitrary", "arbitrary"),
                                             vmem_limit_bytes=VMEM_LIMIT),
    )(order, h_t, dp_blocks, *others, small)


def ada_weight_grad(c_all, dmod_cols):
    def body(c_ref, d_ref, o_ref):
        cv = c_ref[...]
        o_ref[...] = hdot_tn(cv * sigmoid(cv), d_ref[...])

    return pl.pallas_call(
        body, name="ada_weight_grad",
        out_shape=jax.ShapeDtypeStruct((c_all.shape[1], dmod_cols.shape[1]), F32),
    )(c_all, dmod_cols)


def adamw_update(g, w, m, v):
    nm = ADAM_B1 * m + (1.0 - ADAM_B1) * g
    nv = ADAM_B2 * v + (1.0 - ADAM_B2) * (g * g)
    m_hat = nm / (1.0 - ADAM_B1 ** ADAM_STEP)
    v_hat = nv / (1.0 - ADAM_B2 ** ADAM_STEP)
    return -ADAM_LR * (m_hat / (jnp.sqrt(v_hat) + ADAM_EPS) + ADAM_WD * w), nm, nv


def adamw(parts, w, m, v, name):
    k, rows, cols = parts.shape
    rb = 128 if rows % 128 == 0 else rows

    def body(p_ref, w_ref, m_ref, v_ref, g_ref, d_ref, nm_ref, nv_ref):
        g = p_ref[0].astype(F32)
        for i in range(1, k):
            g = g + p_ref[i].astype(F32)
        g_ref[0] = g
        d_ref[0], nm_ref[0], nv_ref[0] = adamw_update(g, w_ref[0], m_ref[0], v_ref[0])

    blk = pl.BlockSpec((1, rb, cols), lambda i: (0, i, 0))
    return pl.pallas_call(
        body, name=name, grid=(rows // rb,),
        in_specs=[pl.BlockSpec((k, rb, cols), lambda i: (0, i, 0)), blk, blk, blk],
        out_specs=[blk] * 4, out_shape=[jax.ShapeDtypeStruct((1, rows, cols), F32)] * 4,
        compiler_params=pltpu.CompilerParams(dimension_semantics=("arbitrary",), vmem_limit_bytes=VMEM_LIMIT),
    )(parts, w, m, v)


def adamw_small(parts, ws, ms, vs):
    k = parts.shape[0]
    n = len(ws)
    sizes = [w.shape[1] for w in ws]

    def body(p_ref, *refs):
        ins, outs = refs[:3 * n], refs[3 * n:]
        g_all = p_ref[0]
        for i in range(1, k):
            g_all = g_all + p_ref[i]
        off = 0
        for a, size in enumerate(sizes):
            g = g_all[:, off:off + size]
            off += size
            d, nm, nv = adamw_update(g, ins[a][...], ins[n + a][...], ins[2 * n + a][...])
            for kind, val in enumerate((g, d, nm, nv)):
                outs[kind * n + a][...] = val

    return pl.pallas_call(
        body, name="adamw_small",
        out_shape=[jax.ShapeDtypeStruct((1, size), F32) for _ in range(4) for size in sizes],
    )(parts, *ws, *ms, *vs)


def rot_cols(w):
    return jnp.concatenate([-w[:, ROPE // 2:], w[:, :ROPE // 2]], axis=1)


def unrot_cols(dw):
    return jnp.concatenate([dw[:, ROPE // 2:], -dw[:, :ROPE // 2]], axis=1)


def columns_from_shards(g, rows, cols):
    return g.reshape(N_DEV, rows, cols).transpose(1, 0, 2).reshape(rows, N_DEV * cols)


def shards_from_columns(w, rows, cols):
    return w.reshape(rows, N_DEV, cols).transpose(1, 0, 2).reshape(N_DEV, rows * cols)


def permute_w_in(w):
    z = lambda n: jnp.zeros((D_MODEL, n), w.dtype)
    krope = w[:, N_KROPE:N_KROPE + ROPE]
    rw = N_RWKV
    return jnp.concatenate([
        w[:, N_MA:N_MA + 1024], w[:, N_MB:N_MB + 1024],
        w[:, rw:rw + 512], w[:, rw + 512:rw + 1024], w[:, rw + 1024:rw + 1536],
        w[:, N_GPA:N_GPA + 512], w[:, N_GPB:N_GPB + 512],
        w[:, N_QC:N_QC + 256], w[:, N_KVC:N_KVC + 128],
        z(NOPE), krope, z(LANE - QK_DIM), z(NOPE), rot_cols(krope), z(LANE - QK_DIM),
        w[:, rw + 1536:rw + 1664]], axis=1)


def unpermute_w_in_grad(d):
    rw = P_R
    krope = d[:, P_KR + NOPE:P_KR + QK_DIM] + unrot_cols(d[:, P_KRR + NOPE:P_KRR + QK_DIM])
    return jnp.concatenate([
        d[:, P_QC:P_QC + 256], d[:, P_KVC:P_KVC + 128], krope, d[:, P_GPA:P_GPA + 512],
        d[:, rw:rw + 1536], d[:, P_LORA:P_LORA + 128], d[:, P_GPB:P_GPB + 512],
        d[:, P_MA:P_MA + 1024], d[:, P_MB:P_MB + 1024]], axis=1)


def pad_heads_q(w_uq):
    w = w_uq.reshape(Q_RANK, HEADS, QK_DIM)
    zpad = jnp.zeros((Q_RANK, HEADS, LANE - QK_DIM), w.dtype)
    wq = jnp.concatenate([w, zpad], axis=2).reshape(Q_RANK, HEADS * LANE)
    pe = w[:, :, NOPE:]
    rot = jnp.concatenate([-pe[:, :, ROPE // 2:], pe[:, :, :ROPE // 2]], axis=2)
    wqr = jnp.concatenate([jnp.zeros((Q_RANK, HEADS, NOPE), w.dtype), rot, zpad], axis=2).reshape(Q_RANK, HEADS * LANE)
    return wq, wqr


def unpad_heads_q_grad(dwq, dwqr):
    a = dwq.reshape(Q_RANK, HEADS, LANE)
    r = dwqr.reshape(Q_RANK, HEADS, LANE)[:, :, NOPE:QK_DIM]
    pe = a[:, :, NOPE:QK_DIM] + jnp.concatenate([r[:, :, ROPE // 2:], -r[:, :, :ROPE // 2]], axis=2)
    return jnp.concatenate([a[:, :, :NOPE], pe], axis=2).reshape(Q_RANK, HEADS * QK_DIM)


def pad_heads_kv(w_ukv):
    w = w_ukv.reshape(KV_RANK, HEADS, 2 * HEAD)
    z = jnp.zeros((KV_RANK, HEADS, HEAD), w.dtype)
    wkn = jnp.concatenate([w[:, :, :NOPE], z], axis=2).reshape(KV_RANK, HEADS * LANE)
    val = w[:, :, NOPE:]
    odd = (jnp.arange(HEADS) % 2 == 1)[None, :, None]
    wv = jnp.concatenate([jnp.where(odd, 0, val), jnp.where(odd, val, 0)], axis=2).reshape(KV_RANK, HEADS * LANE)
    return wkn, wv


def unpad_heads_kv_grad(dwkn, dwv):
    a = dwkn.reshape(KV_RANK, HEADS, LANE)[:, :, :NOPE]
    b = dwv.reshape(KV_RANK, HEADS, LANE)
    odd = (jnp.arange(HEADS) % 2 == 1)[None, :, None]
    val = jnp.where(odd, b[:, :, HEAD:], b[:, :, :HEAD])
    return jnp.concatenate([a, val], axis=2).reshape(KV_RANK, HEADS * 2 * HEAD)


def kernel(x, c, positions, w_ada, b_ada, w_in, q_norm_g, w_uq, kv_norm_g, w_ukv, mu_rwkv, w0, w_decay_up, a0, w_iclr_up, k_k, k_a, r_k, gn_g, gn_b, w_proj_a, w_proj_b, w_out, post_g, post_b, loss_target, m_w_ada, m_b_ada, m_w_in, m_q_norm_g, m_w_uq, m_kv_norm_g, m_w_ukv, m_mu_rwkv, m_w0, m_w_decay_up, m_a0, m_w_iclr_up, m_k_k, m_k_a, m_r_k, m_gn_g, m_gn_b, m_w_proj_a, m_w_proj_b, m_w_out, m_post_g, m_post_b, v_w_ada, v_b_ada, v_w_in, v_q_norm_g, v_w_uq, v_kv_norm_g, v_w_ukv, v_mu_rwkv, v_w0, v_w_decay_up, v_a0, v_w_iclr_up, v_k_k, v_k_a, v_r_k, v_gn_g, v_gn_b, v_w_proj_a, v_w_proj_b, v_w_out, v_post_g, v_post_b):
    weights = dict(w_ada=w_ada, b_ada=b_ada, w_in=w_in, q_norm_g=q_norm_g, w_uq=w_uq, kv_norm_g=kv_norm_g,
                   w_ukv=w_ukv, mu_rwkv=mu_rwkv, w0=w0, w_decay_up=w_decay_up, a0=a0, w_iclr_up=w_iclr_up,
                   k_k=k_k, k_a=k_a, r_k=r_k, gn_g=gn_g, gn_b=gn_b, w_proj_a=w_proj_a, w_proj_b=w_proj_b,
                   w_out=w_out, post_g=post_g, post_b=post_b)
    mom1 = dict(w_ada=m_w_ada, b_ada=m_b_ada, w_in=m_w_in, q_norm_g=m_q_norm_g, w_uq=m_w_uq, kv_norm_g=m_kv_norm_g,
                w_ukv=m_w_ukv, mu_rwkv=m_mu_rwkv, w0=m_w0, w_decay_up=m_w_decay_up, a0=m_a0, w_iclr_up=m_w_iclr_up,
                k_k=m_k_k, k_a=m_k_a, r_k=m_r_k, gn_g=m_gn_g, gn_b=m_gn_b, w_proj_a=m_w_proj_a, w_proj_b=m_w_proj_b,
                w_out=m_w_out, post_g=m_post_g, post_b=m_post_b)
    mom2 = dict(w_ada=v_w_ada, b_ada=v_b_ada, w_in=v_w_in, q_norm_g=v_q_norm_g, w_uq=v_w_uq, kv_norm_g=v_kv_norm_g,
                w_ukv=v_w_ukv, mu_rwkv=v_mu_rwkv, w0=v_w0, w_decay_up=v_w_decay_up, a0=v_a0, w_iclr_up=v_w_iclr_up,
                k_k=v_k_k, k_a=v_k_a, r_k=v_r_k, gn_g=v_gn_g, gn_b=v_gn_b, w_proj_a=v_w_proj_a, w_proj_b=v_w_proj_b,
                w_out=v_w_out, post_g=v_post_g, post_b=v_post_b)
    names = list(weights)
    n_rows = x.shape[1]
    me = 4 * lax.axis_index("x") + 2 * lax.axis_index("y") + lax.axis_index("c")
    xr = x[0]
    tgt = loss_target[0]
    row = lambda a: a.reshape(1, -1)

    w_in_all, c_all = gather_shards([w_in[0].astype(BF16), c])
    c_all = c_all.reshape(N_DEV, D_MODEL)
    w_in_p = permute_w_in(columns_from_shards(w_in_all, D_MODEL, IN_WIDTH // N_DEV))

    mod_all = ada_modulation(c_all, w_ada[0], b_ada.reshape(N_DEV, -1))
    mod = lax.dynamic_index_in_dim(mod_all, me, axis=1, keepdims=False).reshape(3, D_MODEL)

    proj, *gathered = fwd_in_gather(xr, mod, w_in_p, [weights[n][0].astype(BF16) for n, _, _ in SHARDED[1:]])
    pcol = lambda off_, w: (proj, w, off_ // w)
    full = {}
    for (n, r, cdim), part in zip(SHARDED[1:], gathered):
        full[n] = part.reshape(N_DEV * r, cdim) if n == "w_out" else columns_from_shards(part, r, cdim)
    wq, wqr = pad_heads_q(full["w_uq"])
    wkn, wv = pad_heads_kv(full["w_ukv"])
    zl = jnp.zeros((LORA, WIDTH), BF16)
    w_dec = jnp.concatenate([full["w_decay_up"], zl], axis=0)
    w_iclr = jnp.concatenate([zl, full["w_iclr_up"]], axis=0)
    wpa, wpb, wout = full["w_proj_a"], full["w_proj_b"], full["w_out"]

    inv_freq = ROPE_THETA ** (-jnp.arange(0, ROPE, 2, dtype=F32) / ROPE)
    ang = positions[0].astype(F32)[:, None] * inv_freq
    ones_n, zeros_n, zeros_p = jnp.ones((n_rows, NOPE), F32), jnp.zeros((n_rows, NOPE), F32), jnp.zeros((n_rows, LANE - QK_DIM), F32)
    cos_t = jnp.concatenate([ones_n, jnp.cos(ang), jnp.cos(ang), zeros_p], axis=1)
    sin_t = jnp.concatenate([zeros_n, jnp.sin(ang), jnp.sin(ang), zeros_p], axis=1)

    gq, gkv = q_norm_g, kv_norm_g
    mla_consts = [gq, gkv, wq, wqr, wkn, wv]
    q, k, v = row_call(
        "mla_prep", mla_prep_tile, n_rows,
        [pcol(P_QC, 256), pcol(P_KVC, 128), pcol(P_KR, 128), pcol(P_KRR, 128), (cos_t, LANE, 0), (sin_t, LANE, 0)],
        mla_consts, [(HEADS * LANE, BF16)] * 3)
    ya, lse = attention_forward(q, k, v)

    t_idx = jnp.arange(ROW_TILE)
    same_chunk = (t_idx[:, None] // CHUNK) == (t_idx[None, :] // CHUNK)
    same = same_chunk.astype(F32)
    tril = (same_chunk & (t_idx[:, None] >= t_idx[None, :])).astype(F32)
    l_idx = jnp.arange(LANE)
    bd = ((l_idx[:, None] // HEAD) == (l_idx[None, :] // HEAD)).astype(F32)
    mu = mu_rwkv
    mu_r, mu_k, mu_v, mu_l = mu[:, 0:512], mu[:, 512:1024], mu[:, 1024:1536], mu[:, 1536:1664]
    rk_row = row(r_k)
    rwkv_consts = [mu_r, mu_k, mu_v, mu_l, w0, a0, k_k, k_a, w_dec, w_iclr, tril, same, bd]
    rwkv_rows = [pcol(P_R, 512), pcol(P_K, 512), pcol(P_V, 512), pcol(P_LORA, 128)]
    rt, at, bt, kt, clf, uv, ur, k2 = row_call(
        "rwkv_prep", rwkv_prep_tile, n_rows, rwkv_rows, rwkv_consts, [(WIDTH, F32)] * 8, halo_in=rwkv_rows)
    y, m0s, state_maps, out_maps, *wkv_saved = wkv_forward(at, bt, kt, rt, uv, clf)

    tail = row_call(
        "tail", tail_tile, n_rows,
        [(xr, D_MODEL, 0), (tgt, D_MODEL, 0), pcol(P_MA, 1024), pcol(P_MB, 1024), pcol(P_GPA, 512), pcol(P_GPB, 512),
         (ya, WIDTH, 0), (y, WIDTH, 0), (ur, WIDTH, 0), (k2, WIDTH, 0), (uv, WIDTH, 0)],
        [mod, wpa, wpb, wout, gn_g, gn_b, rk_row, post_g, post_b, bd],
        [(D_MODEL, F32), (1024, BF16), (1024, BF16), (512, BF16), (512, BF16), (WIDTH, F32), (WIDTH, F32), (WIDTH, F32)],
        acc_out=[((1, LANE), F32), ((D_MODEL, D_MODEL), F32), ((WIDTH, D_MODEL), F32), ((WIDTH, D_MODEL), F32),
                 ((1, WIDTH), F32), ((1, WIDTH), F32), ((1, D_MODEL), F32), ((1, D_MODEL), F32), ((1, D_MODEL), F32)])
    (dz, dma, dmb, dgpa, dgpb, dya, dy, dyb,
     loss_row, g_wout, g_wpa, g_wpb, g_gn_g, g_gn_b, g_post_g, g_post_b, dgate) = tail

    dq, dk, dv = attention_backward(q, k, v, ya, dya, lse)
    dq_c, dkv_c, dkr, dkrr, g_wq, g_wqr, g_wkn, g_wv, g_gq, g_gkv = row_call(
        "mla_prep_bwd", mla_prep_bwd_tile, n_rows,
        [pcol(P_QC, 256), pcol(P_KVC, 128), (cos_t, LANE, 0), (sin_t, LANE, 0),
         (dq, HEADS * LANE, 0), (dk, HEADS * LANE, 0), (dv, HEADS * LANE, 0)],
        mla_consts, [(256, BF16), (128, BF16), (128, BF16), (128, BF16)],
        acc_out=[((Q_RANK, HEADS * LANE), F32)] * 2 + [((KV_RANK, HEADS * LANE), F32)] * 2
        + [((1, Q_RANK), F32), ((1, KV_RANK), F32)])

    dat, dbt, dkt, drt, dvv, dlw = wkv_backward(at, bt, kt, rt, uv, clf, m0s, state_maps, out_maps, wkv_saved, dy)
    (dr0, dk0, dv0, dl0, g_mu_r, g_mu_k, g_mu_v, g_mu_l, g_w0, g_a0, g_k_k, g_k_a, g_r_k, g_wdec, g_wiclr) = row_call(
        "rwkv_prep_bwd", rwkv_prep_bwd_tile, n_rows,
        rwkv_rows + [(drt, WIDTH, 0), (dat, WIDTH, 0), (dbt, WIDTH, 0), (dkt, WIDTH, 0), (dvv, WIDTH, 0),
                     (dlw, WIDTH, 0), (dyb, WIDTH, 0)],
        rwkv_consts + [rk_row], [(512, BF16), (512, BF16), (512, BF16), (128, BF16)],
        acc_out=[((1, 512), F32)] * 3 + [((1, 128), F32)] + [((1, 512), F32)] * 5 + [((LANE, WIDTH), F32)] * 2,
        halo_in=rwkv_rows, carry=[512, 512, 512, 128], reverse=True)

    li = jnp.arange(LANE)
    src, dst = li[:, None], li[None, :]
    half = ROPE // 2
    unrot = (jnp.where((dst >= NOPE) & (dst < NOPE + half) & (src == dst + half), 1.0, 0.0)
             - jnp.where((dst >= NOPE + half) & (dst < QK_DIM) & (src == dst - half), 1.0, 0.0)).astype(BF16)
    dx, h_t, dproj_blocks, dshift, dscale = in_backward(
        xr, dz, [dma, dmb, dr0, dk0, dv0, dgpa, dgpb, dq_c, dkv_c, dkr, dkrr, dl0], mod, w_in_p, unrot)

    grads_full = {
        "w_uq": unpad_heads_q_grad(g_wq, g_wqr), "w_ukv": unpad_heads_kv_grad(g_wkn, g_wv),
        "w_decay_up": g_wdec[:LORA], "w_iclr_up": g_wiclr[LORA:],
        "w_proj_a": g_wpa, "w_proj_b": g_wpb, "w_out": g_wout}
    blocks = [(grads_full[n].reshape(N_DEV, r, cdim) if n == "w_out"
               else grads_full[n].reshape(r, N_DEV, cdim).transpose(1, 0, 2)).astype(BF16) for n, r, cdim in SHARDED[1:]]
    dmod = jnp.concatenate([dshift, dscale, dgate], axis=1)
    small = jnp.concatenate([dmod, g_gq, g_gkv, g_mu_r, g_mu_k, g_mu_v, g_mu_l, g_w0, g_a0, g_k_k, g_k_a, g_r_k,
                             g_gn_g, g_gn_b, g_post_g, g_post_b, loss_row], axis=1)
    my_x, my_y, my_c = lax.axis_index("x"), lax.axis_index("y"), lax.axis_index("c")
    chip_order = [4 * (my_x ^ fx) + 2 * (my_y ^ fy) for fx, fy in ((1, 1), (1, 0), (0, 1), (0, 0))]
    owners = [chip_order[s % 4] + (my_c if s >= 4 else 1 - my_c) for s in WGRAD_SLOTS]
    order = jnp.stack(owners + [jnp.int32(s) for s in WGRAD_SLOTS]).astype(jnp.int32)
    *got_blocks, got_small = in_weight_grad_exchange(h_t, dproj_blocks, blocks, small, order)
    loss = jnp.sum(got_small[:, 0, SMALL_ELEMS])

    ada_cols = w_ada.shape[2]
    dmod_all = got_small[:, 0, :3 * D_MODEL]
    g_ada = ada_weight_grad(c_all, lax.dynamic_slice_in_dim(dmod_all, me * ada_cols, ada_cols, axis=1))

    outs = [dict() for _ in range(4)]
    res = adamw(g_ada[None], w_ada, m_w_ada, v_w_ada, "adamw_w_ada")
    for kind in range(4):
        outs[kind]["w_ada"] = res[kind]
    for (n, r, cdim), got in zip(SHARDED, got_blocks):
        res = adamw(got, weights[n], mom1[n], mom2[n], "adamw_" + n)
        for kind in range(4):
            outs[kind][n] = res[kind]
    rows_of = lambda tree: [tree[n].reshape(1, -1) for n, _ in SMALL]
    res = adamw_small(got_small, rows_of(weights), rows_of(mom1), rows_of(mom2))
    for kind in range(4):
        for a, (n, _) in enumerate(SMALL):
            outs[kind][n] = res[kind * len(SMALL) + a].reshape(weights[n].shape)
    return (loss, dx[None], *[outs[0][n] for n in names], *[outs[1][n] for n in names],
            *[outs[2][n] for n in names], *[outs[3][n] for n in names])
```

```python
import functools
import math

import jax
import jax.numpy as jnp
from jax import lax
from jax.experimental import pallas as pl
from jax.experimental.pallas import tpu as pltpu

F32 = jnp.float32
BF16 = jnp.bfloat16
HIGHEST = lax.Precision.HIGHEST
MESH_IDS = pl.DeviceIdType.MESH

N_DEV = 8
D_MODEL = 1024
LN_EPS = 1e-5
RMS_EPS = 1e-6
GN_EPS = 64e-5
HEADS = 8
Q_RANK = 256
KV_RANK = 128
ROPE = 32
NOPE = 64
QK_DIM = NOPE + ROPE
WIDTH = 512
HEAD = 64
LORA = 64
CHUNK = 64
DEPTH = 1
ALPHA = (2.0 * DEPTH) ** 0.25
ROPE_THETA = 10000.0
ATTN_SCALE = QK_DIM ** -0.5
DECAY_SCALE = math.exp(-0.5)

ADAM_LR = 0.001
ADAM_B1 = 0.9
ADAM_B2 = 0.999
ADAM_EPS = 1e-08
ADAM_WD = 0.01
ADAM_STEP = 10

LANE = 128
PAIR = 2 * HEAD
ROW_TILE = 256
HALO_ROWS = 16
ATTN_FWD_TILES = (512, 1024)
ATTN_BWD_TILES = (512, 512)
LOG2_E = math.log2(math.e)
Q_PRESCALE = ATTN_SCALE * LOG2_E
WKV_CHUNKS_PER_STEP = 8
WGRAD_SLOTS = (0, 1, 4, 2, 5, 6, 3, 7)
VMEM_LIMIT = 56 * 1024 * 1024

P_MA, P_MB, P_R, P_K, P_V, P_GPA, P_GPB, P_QC, P_KVC, P_KR, P_KRR, P_LORA = (
    0, 1024, 2048, 2560, 3072, 3584, 4096, 4608, 4864, 4992, 5120, 5248)
P_WIDTH = 5376
DW_BLOCK = 768

N_QC, N_KVC, N_KROPE, N_GPA, N_RWKV, N_GPB, N_MA, N_MB = 0, 256, 384, 416, 928, 2592, 3104, 4128
IN_WIDTH = 5152

SHARDED = (("w_in", 1024, 644), ("w_uq", 256, 96), ("w_ukv", 128, 128), ("w_decay_up", 64, 64),
           ("w_iclr_up", 64, 64), ("w_proj_a", 512, 128), ("w_proj_b", 512, 128), ("w_out", 128, 1024))
SHARD_ELEMS = sum(r * c for _, r, c in SHARDED)
SHARD_ROWS = SHARD_ELEMS // LANE
GATHER_ROWS = SHARD_ROWS + 2 * D_MODEL // LANE
SMALL = (("b_ada", 3072), ("q_norm_g", 256), ("kv_norm_g", 128), ("mu_rwkv", 1664), ("w0", 512), ("a0", 512),
         ("k_k", 512), ("k_a", 512), ("r_k", 512), ("gn_g", 512), ("gn_b", 512), ("post_g", 1024), ("post_b", 1024))
SMALL_ELEMS = sum(n for _, n in SMALL)
SMALL_ROWS = SMALL_ELEMS // LANE


def mm(a, b):
    return jnp.dot(a.astype(BF16), b.astype(BF16), preferred_element_type=F32)


def mm_nt(a, b):
    return lax.dot_general(a.astype(BF16), b.astype(BF16), (((1,), (1,)), ((), ())), preferred_element_type=F32)


def mm_tn(a, b):
    return lax.dot_general(a.astype(BF16), b.astype(BF16), (((0,), (0,)), ((), ())), preferred_element_type=F32)


def hdot(a, b):
    return jnp.dot(a, b, precision=HIGHEST, preferred_element_type=F32)


def hdot_nt(a, b):
    return lax.dot_general(a, b, (((1,), (1,)), ((), ())), precision=HIGHEST, preferred_element_type=F32)


def hdot_tn(a, b):
    return lax.dot_general(a, b, (((0,), (0,)), ((), ())), precision=HIGHEST, preferred_element_type=F32)


def sigmoid(x):
    return 1.0 / (1.0 + jnp.exp(-x))


def colsum(x):
    return jnp.sum(x, axis=0, keepdims=True)


def rowmean(x):
    return jnp.mean(x, axis=-1, keepdims=True)


def layer_norm_stats(x):
    xc = x - rowmean(x)
    rstd = lax.rsqrt(rowmean(xc * xc) + LN_EPS)
    return xc * rstd, rstd


def layer_norm_bwd(dy, xhat, rstd):
    return rstd * (dy - rowmean(dy) - xhat * rowmean(dy * xhat))


def bf16_pieces(x, n):
    pieces = []
    for _ in range(n):
        p = x.astype(BF16)
        pieces.append(p)
        x = x - p.astype(F32)
    return pieces


def ones_dot(ones, x, n_pieces):
    ones = ones.astype(BF16)
    return sum(jnp.dot(ones, p, preferred_element_type=F32) for p in bf16_pieces(x, n_pieces))


def ones_dot_nt(ones, x, n_pieces):
    ones = ones.astype(BF16)
    return sum(lax.dot_general(ones, p, (((1,), (1,)), ((), ())), preferred_element_type=F32)
               for p in bf16_pieces(x, n_pieces))


def head_sum(x, bd):
    bd = bd.astype(BF16)
    out = []
    for p in range(x.shape[1] // LANE):
        hi, lo = bf16_pieces(x[:, p * LANE:(p + 1) * LANE], 2)
        out.append(jnp.dot(hi, bd, preferred_element_type=F32) + jnp.dot(lo, bd, preferred_element_type=F32))
    return jnp.concatenate(out, axis=1)


def tile_lanes(t, n):
    return jnp.concatenate([t] * n, axis=1)


def row_iota(shape):
    return lax.broadcasted_iota(jnp.int32, shape, 0)


def lane_iota(shape):
    return lax.broadcasted_iota(jnp.int32, shape, 1)


def shift_rows_down(x, row0):
    rolled = pltpu.roll(x, 1, axis=0)
    return jnp.where(row_iota(x.shape) == 0, row0, rolled)


def shift_rows_up(x, row_last):
    rolled = pltpu.roll(x, x.shape[0] - 1, axis=0)
    return jnp.where(row_iota(x.shape) == x.shape[0] - 1, row_last, rolled)


def row_call(name, fn, n_rows, row_in, const_in, row_out, acc_out=(), halo_in=(), carry=(), reverse=False):
    ts = ROW_TILE
    n_tiles = n_rows // ts
    n_in = len(row_in) + len(halo_in) + len(const_in)
    n_ro, n_ao = len(row_out), len(acc_out)

    def tile_of(g):
        return (n_tiles - 1 - g) if reverse else g

    def body(*refs):
        ins = refs[:n_in]
        ro = refs[n_in:n_in + n_ro]
        ao = refs[n_in + n_ro:n_in + n_ro + n_ao]
        cr = refs[n_in + n_ro + n_ao:]
        g = pl.program_id(0)
        step0 = g == 0
        tile0 = tile_of(g) == 0
        for r in cr:
            @pl.when(step0)
            def _(r=r):
                r[...] = jnp.zeros_like(r)
        n_tiled = len(row_in) + len(halo_in)
        vals = [r[...].astype(F32) for r in ins[:n_tiled]] + [r[...] for r in ins[n_tiled:]]
        outs = fn(step0, tile0, *vals, *[c[0:1, :] for c in cr])
        for r, v in zip(ro, outs[:n_ro]):
            r[...] = v.astype(r.dtype)
        for r, v in zip(ao, outs[n_ro:n_ro + n_ao]):
            @pl.when(step0)
            def _(r=r, v=v):
                r[...] = v.astype(r.dtype)

            @pl.when(jnp.logical_not(step0))
            def _(r=r, v=v):
                r[...] += v.astype(r.dtype)
        for r, v in zip(cr, outs[n_ro + n_ao:]):
            r[0:1, :] = v

    in_specs = [pl.BlockSpec((ts, w), functools.partial(lambda g, cb: (tile_of(g), cb), cb=cb)) for _, w, cb in row_in]
    in_specs += [pl.BlockSpec((HALO_ROWS, w), functools.partial(
        lambda g, cb: (jnp.maximum(tile_of(g) * (ts // HALO_ROWS) - 1, 0), cb), cb=cb)) for _, w, cb in halo_in]
    in_specs += [pl.BlockSpec(memory_space=pltpu.VMEM) for _ in const_in]
    out_specs = [pl.BlockSpec((ts, w), lambda g: (tile_of(g), 0)) for w, _ in row_out]
    out_specs += [pl.BlockSpec(s, lambda g: (0, 0)) for s, _ in acc_out]
    out_shape = [jax.ShapeDtypeStruct((n_rows, w), d) for w, d in row_out]
    out_shape += [jax.ShapeDtypeStruct(s, d) for s, d in acc_out]
    return pl.pallas_call(
        body, name=name, grid=(n_tiles,), in_specs=in_specs, out_specs=out_specs, out_shape=out_shape,
        scratch_shapes=[pltpu.VMEM((8, w), F32) for w in carry],
        compiler_params=pltpu.CompilerParams(dimension_semantics=("arbitrary",), vmem_limit_bytes=VMEM_LIMIT),
    )(*[a for a, _, _ in row_in], *[a for a, _, _ in halo_in], *const_in)


def my_position():
    return lax.axis_index("x"), lax.axis_index("y"), lax.axis_index("c")


def flip(pos, k):
    x, y, c = pos
    dx, dy, dc = (k >> 2) & 1, (k >> 1) & 1, k & 1
    return (1 - x if dx else x, 1 - y if dy else y, 1 - c if dc else c)


def flat_index(pos):
    return 4 * pos[0] + 2 * pos[1] + pos[2]


def gather_shards(shards):
    n = len(shards)

    def body(*refs):
        x_refs, out_refs = refs[:n], refs[n:2 * n]
        send_sems, recv_sems, local_sems = refs[2 * n:]
        x, y, c = my_position()
        me, sibling = (x, y, c), (x, y, 1 - c)
        chips = [(1 - x, y), (x, 1 - y), (1 - x, 1 - y)]

        def copy(a, k, block, to, from_input=False):
            slot = out_refs[a].at[flat_index(block)]
            return pltpu.make_async_remote_copy(
                src_ref=x_refs[a] if from_input else slot, dst_ref=slot,
                send_sem=send_sems.at[7 * a + k], recv_sem=recv_sems.at[7 * a + k],
                device_id=to, device_id_type=MESH_IDS)

        mine = [pltpu.make_async_copy(x_refs[a], out_refs[a].at[flat_index(me)], local_sems.at[a]) for a in range(n)]
        for cp in mine:
            cp.start()
        first = []
        for a in range(n):
            first.append(copy(a, 0, me, sibling, from_input=True))
            first += [copy(a, 1 + j, me, (*chip, c), from_input=True) for j, chip in enumerate(chips)]
        for cp in first:
            cp.start()
        passed = []
        for j, chip in enumerate(chips):
            for a in range(n):
                copy(a, 1 + j, (*chip, c), me).wait_recv()
                cp = copy(a, 4 + j, (*chip, c), sibling)
                cp.start()
                passed.append(cp)
        for a in range(n):
            copy(a, 0, sibling, me).wait_recv()
            for j, chip in enumerate(chips):
                copy(a, 4 + j, (*chip, 1 - c), me).wait_recv()
        for cp in first + passed:
            cp.wait_send()
        for cp in mine:
            cp.wait()

    return pl.pallas_call(
        body, name="gather_shards",
        out_shape=[jax.ShapeDtypeStruct((N_DEV,) + s.shape, s.dtype) for s in shards],
        in_specs=[pl.BlockSpec(memory_space=pl.ANY)] * n, out_specs=[pl.BlockSpec(memory_space=pl.ANY)] * n,
        scratch_shapes=[pltpu.SemaphoreType.DMA((7 * n,)), pltpu.SemaphoreType.DMA((7 * n,)),
                        pltpu.SemaphoreType.DMA((n,))],
    )(*shards)


def ada_modulation(c_all, w_ada_loc, b_ada_blocks):
    cols = w_ada_loc.shape[1]

    def body(c_ref, w_ref, b_ref, out_ref, send_sems, recv_sems):
        me = my_position()
        mi = flat_index(me)
        cv = c_ref[...]
        res = hdot(cv * sigmoid(cv), w_ref[...]) + b_ref[pl.ds(mi, 1), :]
        out_ref[mi] = res
        sends = []
        for k in range(1, N_DEV):
            cp = pltpu.make_async_remote_copy(
                src_ref=out_ref.at[mi], dst_ref=out_ref.at[mi], send_sem=send_sems.at[k - 1],
                recv_sem=recv_sems.at[k - 1], device_id=flip(me, k), device_id_type=MESH_IDS)
            cp.start()
            sends.append(cp)
        for k in range(1, N_DEV):
            pi = flat_index(flip(me, k))
            pltpu.make_async_remote_copy(
                src_ref=out_ref.at[pi], dst_ref=out_ref.at[pi], send_sem=send_sems.at[k - 1],
                recv_sem=recv_sems.at[k - 1], device_id=flip(me, k), device_id_type=MESH_IDS).wait_recv()
        for cp in sends:
            cp.wait_send()

    return pl.pallas_call(
        body, name="ada_modulation",
        out_shape=jax.ShapeDtypeStruct((N_DEV, N_DEV, cols), F32),
        in_specs=[pl.BlockSpec(memory_space=pltpu.VMEM)] * 3, out_specs=pl.BlockSpec(memory_space=pltpu.VMEM),
        scratch_shapes=[pltpu.SemaphoreType.DMA((7,)), pltpu.SemaphoreType.DMA((7,))],
    )(c_all, w_ada_loc, b_ada_blocks)


def fwd_in_tile(step0, tile0, x, mod, w_in_p):
    xhat, _ = layer_norm_stats(x)
    h = xhat * (1.0 + mod[1:2]) + mod[0:1]
    return (mm(h, w_in_p),)


def fwd_in_gather(x, mod, w_in_p, shards):
    n = len(shards)
    n_rows = x.shape[0]
    ts = ROW_TILE
    n_tiles = n_rows // ts

    def body(x_ref, mod_ref, w_ref, *rest):
        s_refs = rest[:n]
        proj_ref, out_refs = rest[n], rest[n + 1:2 * n + 1]
        send_sems, recv_sems, local_sems = rest[2 * n + 1:]
        g = pl.program_id(0)
        me = my_position()
        mi = flat_index(me)

        def copies(k, slot):
            return [pltpu.make_async_remote_copy(
                src_ref=s_refs[a], dst_ref=out_refs[a].at[slot], send_sem=send_sems.at[7 * a + k - 1],
                recv_sem=recv_sems.at[7 * a + k - 1], device_id=flip(me, k), device_id_type=MESH_IDS)
                for a in range(n)]

        local = [pltpu.make_async_copy(s_refs[a], out_refs[a].at[mi], local_sems.at[a]) for a in range(n)]

        @pl.when(g == 0)
        def _():
            for cp in local:
                cp.start()
            for k in range(1, N_DEV):
                for cp in copies(k, mi):
                    cp.start()

        proj_ref[...] = fwd_in_tile(None, None, x_ref[...], mod_ref[...], w_ref[...])[0].astype(BF16)

        @pl.when(g == n_tiles - 1)
        def _():
            for k in range(1, N_DEV):
                for cp in copies(k, flat_index(flip(me, k))):
                    cp.wait_recv()
            for k in range(1, N_DEV):
                for cp in copies(k, mi):
                    cp.wait_send()
            for cp in local:
                cp.wait()

    hbm = pl.BlockSpec(memory_space=pl.ANY)
    const = pl.BlockSpec(memory_space=pltpu.VMEM)
    return pl.pallas_call(
        body, name="fwd_in_gather", grid=(n_tiles,),
        in_specs=[pl.BlockSpec((ts, D_MODEL), lambda g: (g, 0)), const, const] + [hbm] * n,
        out_specs=[pl.BlockSpec((ts, P_WIDTH), lambda g: (g, 0))] + [hbm] * n,
        out_shape=[jax.ShapeDtypeStruct((n_rows, P_WIDTH), BF16)]
        + [jax.ShapeDtypeStruct((N_DEV,) + s.shape, s.dtype) for s in shards],
        scratch_shapes=[pltpu.SemaphoreType.DMA((7 * n,)), pltpu.SemaphoreType.DMA((7 * n,)),
                        pltpu.SemaphoreType.DMA((n,))],
        compiler_params=pltpu.CompilerParams(dimension_semantics=("arbitrary",), vmem_limit_bytes=VMEM_LIMIT),
    )(x, mod, w_in_p, *shards)


def rms_norm_fwd(x, g):
    r = lax.rsqrt(rowmean(x * x) + RMS_EPS)
    xh = x * r
    return xh * g, xh, r


def key_rope_mask(shape):
    return (lane_iota(shape) >= NOPE).astype(F32)


def mla_prep_tile(step0, tile0, q_c, kv_c, kr, krr, cos, sin, gq, gkv, wq, wqr, wkn, wv):
    qn, _, _ = rms_norm_fwd(q_c, gq)
    kvn, _, _ = rms_norm_fwd(kv_c, gkv)
    q = (mm(qn, wq) * tile_lanes(cos, HEADS) + mm(qn, wqr) * tile_lanes(sin, HEADS)) * Q_PRESCALE
    kpe = kr * (cos * key_rope_mask(cos.shape)) + krr * sin
    k = mm(kvn, wkn) + tile_lanes(kpe, HEADS)
    v = mm(kvn, wv)
    return q, k, v


def rwkv_prep_core(tile0, r0, k0, v0, l0, hr, hk, hv, hl, mu_r, mu_k, mu_v, mu_l, w0, a0, k_k, k_a,
                   w_dec, w_iclr, tril, same, bd):
    def shifted(x, halo, mu):
        row0 = jnp.where(tile0, 0.0, halo[HALO_ROWS - 1:HALO_ROWS, :])
        prev = shift_rows_down(x, row0)
        return x + (prev - x) * mu, prev

    ur, pr = shifted(r0, hr, mu_r)
    uk, pk = shifted(k0, hk, mu_k)
    uv, pv = shifted(v0, hv, mu_v)
    ul, plo = shifted(l0, hl, mu_l)
    th = jnp.tanh(ul)
    sg = sigmoid(w0 + mm(th, w_dec))
    lw = -DECAY_SCALE * sg
    a_ic = sigmoid(a0 + mm(ul, w_iclr))
    kkraw = uk * k_k
    nrm_raw = jnp.sqrt(head_sum(kkraw * kkraw, bd))
    nrm = jnp.maximum(nrm_raw, 1e-12)
    kk = kkraw / nrm
    k2 = uk * (1.0 + (a_ic - 1.0) * k_a)
    lc = ones_dot(tril, lw, 3)
    lcl = ones_dot(same, lw, 3)
    return dict(ur=ur, uk=uk, uv=uv, ul=ul, pr=pr, pk=pk, pv=pv, pl=plo, th=th, sg=sg, lw=lw, a_ic=a_ic,
                kkraw=kkraw, nrm_raw=nrm_raw, nrm=nrm, kk=kk, k2=k2, lc=lc, lcl=lcl)


def rwkv_prep_tile(step0, tile0, r0, k0, v0, l0, hr, hk, hv, hl, *consts):
    f = rwkv_prep_core(tile0, r0, k0, v0, l0, hr, hk, hv, hl, *consts)
    lc, lw = f["lc"], f["lw"]
    e_neg = jnp.exp(-lc)
    rt = f["ur"] * jnp.exp(lc)
    at = -f["kk"] * jnp.exp(lc - lw)
    bt = f["kk"] * f["a_ic"] * e_neg
    kt = f["k2"] * e_neg
    return rt, at, bt, kt, jnp.exp(f["lcl"]), f["uv"], f["ur"], f["k2"]


def wkv_masks():
    lane = lane_iota((1, PAIR))
    m_lo = (lane < HEAD).astype(F32)
    r2 = row_iota((PAIR, PAIR))
    c2 = lane_iota((PAIR, PAIR))
    bd = ((r2 < HEAD) == (c2 < HEAD)).astype(F32)
    eye2 = (r2 == c2).astype(F32)
    eye = (row_iota((CHUNK, CHUNK)) == lane_iota((CHUNK, CHUNK))).astype(F32)
    t_idx = row_iota((4 * CHUNK, PAIR)) % CHUNK
    s_idx = lane_iota((4 * CHUNK, PAIR)) % CHUNK
    keep = s_idx < t_idx + (row_iota((4 * CHUNK, PAIR)) >= 2 * CHUNK).astype(jnp.int32)
    return (m_lo, 1.0 - m_lo), keep, eye, bd, eye2


def rows(*parts):
    return jnp.concatenate(parts, axis=0)


def lanes(*parts):
    return jnp.concatenate(parts, axis=1)


def head_rows(x, ms):
    return rows(x * ms[0], x * ms[1])


def wkv_score_stack(at, rt, ms):
    return rows(head_rows(at, ms), head_rows(rt, ms))


def wkv_chunks_pre(chunks, masks):
    ms, keep, eye, bd, eye2 = masks
    n = len(chunks)
    at, bt, kt, rt, v, cl = (list(t) for t in zip(*chunks))
    scores = [jnp.where(keep, mm_nt(wkv_score_stack(a, r, ms), rows(b, k)), 0.0)
              for a, r, b, k in zip(at, rt, bt, kt)]
    q = CHUNK
    aab = [s[h * q:(h + 1) * q, :q] for s in scores for h in range(2)]
    tinv = [eye + a for a in aab]
    power = [mm(a, a) for a in aab]
    for _ in range(5):
        both = [mm(rows(t, p), p) for t, p in zip(tinv, power)]
        tinv = [t + x[:q] for t, x in zip(tinv, both)]
        power = [x[q:] for x in both]
    pair = lambda c, row0, col0: lanes(scores[c][row0:row0 + q, col0:col0 + q],
                                       scores[c][row0 + q:row0 + 2 * q, col0:col0 + q])
    tinv_p = [lanes(tinv[2 * c], tinv[2 * c + 1]) for c in range(n)]
    aak_p = [pair(c, 0, q) for c in range(n)]
    prb_p = [pair(c, 2 * q, 0) for c in range(n)]
    prk_p = [pair(c, 2 * q, q) for c in range(n)]
    v_rows = [head_rows(x, ms) for x in v]
    wy = [mm(rows(a, p), x) for a, p, x in zip(aak_p, prk_p, v_rows)]
    w = [x[:q] for x in wy]
    yh2 = [x[q:] for x in wy]
    aw = [mm(t, lanes(head_rows(a, ms), head_rows(w_, ms))) for t, a, w_ in zip(tinv_p, at, w)]
    ah = [x[:, :PAIR] for x in aw]
    wh = [x[:, PAIR:] for x in aw]
    ry = [mm(p, lanes(head_rows(a, ms), head_rows(w_, ms))) for p, a, w_ in zip(prb_p, ah, wh)]
    rh = [r + x[:, :PAIR] for r, x in zip(rt, ry)]
    yh = [x[:, PAIR:] + y for x, y in zip(ry, yh2)]
    bc = [b * c_ for b, c_ in zip(bt, cl)]
    kc = [k * c_ for k, c_ in zip(kt, cl)]
    gh = [mm_tn(b, lanes(a, w_)) for b, a, w_ in zip(bc, ah, wh)]
    g = [eye2 * c_ + bd * x[:, :PAIR] for c_, x in zip(cl, gh)]
    h = [bd * (x[:, PAIR:] + mm_tn(k, v_)) for x, k, v_ in zip(gh, kc, v)]
    as_bf16 = lambda xs: [x.astype(BF16) for x in xs]
    saved = (as_bf16(tinv_p), as_bf16(aak_p), as_bf16(prb_p), as_bf16(prk_p), as_bf16(ah), wh)
    return g, h, rh, yh, saved


def wkv_chunks_grad(chunks, saved, m0, dy, dm1, masks):
    ms, keep, eye, bd, eye2 = masks
    n = len(chunks)
    q = CHUNK
    at, bt, kt, rt, v, cl = (list(t) for t in zip(*chunks))
    tinv_p, aak_p, prb_p, prk_p, ah, wh = (list(t) for t in zip(*saved))
    head_stack = lambda p: rows(p[:, :q], p[:, q:])
    bc = [b * c_ for b, c_ in zip(bt, cl)]
    kc = [k * c_ for k, c_ in zip(kt, cl)]
    u = [mm(a, m) + w for a, m, w in zip(ah, m0, wh)]
    dm1 = [d * bd for d in dm1]
    from_state = [mm(rows(b, k), d) for b, k, d in zip(bc, kc, dm1)]
    dy_rows = [head_rows(d, ms) for d in dy]
    from_out = [mm_tn(lanes(head_stack(pb), head_stack(pk)), d) for pb, pk, d in zip(prb_p, prk_p, dy_rows)]
    du = [a[:q] + b[:q] for a, b in zip(from_state, from_out)]
    dv = [a[q:] + b[q:] for a, b in zip(from_state, from_out)]
    dz = [mm_tn(head_stack(t), head_rows(d, ms)) for t, d in zip(tinv_p, du)]
    dz_rows = [head_rows(d, ms) for d in dz]
    dv = [a + mm_tn(head_stack(k), d) for a, k, d in zip(dv, aak_p, dz_rows)]
    by_m0 = [mm_nt(rows(d, z), m) for d, z, m in zip(dy, dz, m0)]
    uv = [rows(x, y) for x, y in zip(u, v)]
    by_dm1 = [mm_nt(x, d) for x, d in zip(uv, dm1)]
    udm = [x[:q] for x in by_dm1]
    vdm = [x[q:] for x in by_dm1]
    dscores = [jnp.where(keep, mm_nt(rows(z, d), x), 0.0) for z, d, x in zip(dz_rows, dy_rows, uv)]
    to_ar = [mm(d, rows(b, k)) for d, b, k in zip(dscores, bt, kt)]
    to_bk = [mm_tn(d, wkv_score_stack(a, r, ms)) for d, a, r in zip(dscores, at, rt)]
    ones = jnp.ones((8, PAIR), F32)
    upper = (lane_iota((CHUNK, CHUNK)) >= row_iota((CHUNK, CHUNK))).astype(F32)
    out = []
    for c in range(n):
        e = to_ar[c]
        dat_c = by_m0[c][q:] + e[:q] * ms[0] + e[q:2 * q] * ms[1]
        drt_c = by_m0[c][:q] + e[2 * q:3 * q] * ms[0] + e[3 * q:] * ms[1]
        dbt_c = udm[c] * cl[c] + to_bk[c][:q]
        dkt_c = vdm[c] * cl[c] + to_bk[c][q:]
        dlcl = ones_dot_nt(ones, dm1[c] * m0[c], 3)[0:1, :] * cl[c] + colsum(bc[c] * udm[c] + kc[c] * vdm[c])
        g = drt_c * rt[c] - dbt_c * bt[c] - dkt_c * kt[c] + dat_c * at[c]
        dlw = ones_dot(upper, g, 3) - dat_c * at[c] + dlcl
        out.append((dat_c, dbt_c, dkt_c, drt_c, dv[c], dlw))
    return out


def wkv_forward(at, bt, kt, rt, v, clf):
    n_rows = at.shape[0]
    cps = WKV_CHUNKS_PER_STEP
    rb = cps * CHUNK
    n_steps = n_rows // rb

    def body(a_ref, b_ref, k_ref, r_ref, v_ref, c_ref, y_ref, m0_ref, g_ref, rh_ref, *rest):
        saved_refs, m_scr = rest[:6], rest[6]

        @pl.when(pl.program_id(1) == 0)
        def _():
            m_scr[...] = jnp.zeros_like(m_scr)

        masks = wkv_masks()
        chunks = []
        for cc in range(cps):
            sl = slice(cc * CHUNK, (cc + 1) * CHUNK)
            chunks.append((a_ref[sl, :], b_ref[sl, :], k_ref[sl, :], r_ref[sl, :], v_ref[sl, :],
                           c_ref[cc * CHUNK:cc * CHUNK + 1, :]))
        gs, hs, rhs, yhs, saved = wkv_chunks_pre(chunks, masks)
        for ref, per_chunk in zip(saved_refs, saved):
            for cc, val in enumerate(per_chunk):
                ref[cc * CHUNK:(cc + 1) * CHUNK, :] = val
        m = m_scr[...]
        for cc, (g, h, rh, yh) in enumerate(zip(gs, hs, rhs, yhs)):
            sl = slice(cc * CHUNK, (cc + 1) * CHUNK)
            m0_ref[0, cc] = m
            g_ref[0, cc] = g
            rh_ref[sl, :] = rh
            y_ref[sl, :] = hdot(rh, m) + yh
            m = hdot(g, m) + h
        m_scr[...] = m

    blk = pl.BlockSpec((rb, PAIR), lambda p, s: (s, p))
    state_blk = pl.BlockSpec((1, cps, PAIR, PAIR), lambda p, s: (p, s, 0, 0))
    state_shape = jax.ShapeDtypeStruct((WIDTH // PAIR, n_rows // CHUNK, PAIR, PAIR), F32)
    rows_f32 = jax.ShapeDtypeStruct((n_rows, WIDTH), F32)
    rows_bf16 = jax.ShapeDtypeStruct((n_rows, WIDTH), BF16)
    return pl.pallas_call(
        body, name="wkv_forward", grid=(WIDTH // PAIR, n_steps),
        in_specs=[blk] * 6,
        out_specs=[blk, state_blk, state_blk, blk] + [blk] * 6,
        out_shape=[rows_f32, state_shape, state_shape, rows_f32] + [rows_bf16] * 5 + [rows_f32],
        scratch_shapes=[pltpu.VMEM((PAIR, PAIR), F32)],
        compiler_params=pltpu.CompilerParams(dimension_semantics=("arbitrary", "arbitrary"),
                                             vmem_limit_bytes=VMEM_LIMIT),
    )(at, bt, kt, rt, v, clf)


def wkv_backward(at, bt, kt, rt, v, clf, m0s, gs, rh, saved, dy):
    n_rows = at.shape[0]
    cps = WKV_CHUNKS_PER_STEP
    rb = cps * CHUNK
    n_steps = n_rows // rb

    def body(a_ref, b_ref, k_ref, r_ref, v_ref, c_ref, m0_ref, g_ref, rh_ref, *rest):
        saved_refs, dy_ref = rest[:6], rest[6]
        da_ref, db_ref, dk_ref, dr_ref, dv_ref, dlw_ref, dm_scr = rest[7:]

        @pl.when(pl.program_id(1) == 0)
        def _():
            dm_scr[...] = jnp.zeros_like(dm_scr)

        masks = wkv_masks()
        bd = masks[3]
        dm = dm_scr[...]
        dm1 = [None] * cps
        for cc in reversed(range(cps)):
            sl = slice(cc * CHUNK, (cc + 1) * CHUNK)
            dm1[cc] = dm
            dm = bd * (hdot_tn(g_ref[0, cc], dm) + hdot_tn(rh_ref[sl, :], dy_ref[sl, :]))
        dm_scr[...] = dm
        chunks, kept, m0, dys = [], [], [], []
        for cc in range(cps):
            sl = slice(cc * CHUNK, (cc + 1) * CHUNK)
            chunks.append((a_ref[sl, :], b_ref[sl, :], k_ref[sl, :], r_ref[sl, :], v_ref[sl, :],
                           c_ref[cc * CHUNK:cc * CHUNK + 1, :]))
            kept.append(tuple(ref[sl, :] for ref in saved_refs))
            m0.append(m0_ref[0, cc])
            dys.append(dy_ref[sl, :])
        grads = wkv_chunks_grad(chunks, kept, m0, dys, dm1, masks)
        for cc, (dat, dbt, dkt, drt, dv, dlw) in enumerate(grads):
            sl = slice(cc * CHUNK, (cc + 1) * CHUNK)
            da_ref[sl, :] = dat
            db_ref[sl, :] = dbt
            dk_ref[sl, :] = dkt
            dr_ref[sl, :] = drt
            dv_ref[sl, :] = dv
            dlw_ref[sl, :] = dlw

    blk = pl.BlockSpec((rb, PAIR), lambda p, s: (n_steps - 1 - s, p))
    state_blk = pl.BlockSpec((1, cps, PAIR, PAIR), lambda p, s: (p, n_steps - 1 - s, 0, 0))
    return pl.pallas_call(
        body, name="wkv_backward", grid=(WIDTH // PAIR, n_steps),
        in_specs=[blk] * 6 + [state_blk, state_blk, blk] + [blk] * 6 + [blk],
        out_specs=[blk] * 6,
        out_shape=[jax.ShapeDtypeStruct((n_rows, WIDTH), F32)] * 6,
        scratch_shapes=[pltpu.VMEM((PAIR, PAIR), F32)],
        compiler_params=pltpu.CompilerParams(dimension_semantics=("arbitrary", "arbitrary"),
                                             vmem_limit_bytes=VMEM_LIMIT),
    )(at, bt, kt, rt, v, clf, m0s, gs, rh, *saved, dy)


def visible(q_row0, k_row0, shape):
    qc = (q_row0 + row_iota(shape)) // CHUNK
    kc = (k_row0 + lane_iota(shape)) // CHUNK
    return kc <= qc


def attention_forward(q, k, v):
    n_rows = q.shape[0]
    tq, tk = ATTN_FWD_TILES
    n_q = n_rows // tq
    assert tk % tq == 0

    def body(q_ref, k_ref, v_ref, o_ref, lse_ref):
        i = pl.program_id(1)
        lane = lane_iota((tq, LANE))
        heads = [slice(0, LANE), slice(LANE, 2 * LANE)]
        qs = [q_ref[:, cols] for cols in heads]

        def step(j, carry, size, masked):
            rows = pl.ds(pl.multiple_of(j * size, size), size)
            ss = [mm_nt(qh, k_ref[rows, cols]) for qh, cols in zip(qs, heads)]
            if masked:
                vis = visible(i * tq, j * size, ss[0].shape)
                ss = [jnp.where(vis, s, -jnp.inf) for s in ss]
            ps, stats = [], []
            for s, (m, l, _) in zip(ss, carry):
                m_new = jnp.maximum(m, jnp.max(s, axis=-1, keepdims=True))
                p = jnp.exp2(s - m_new)
                alpha = jnp.exp2(m - m_new)
                ps.append(p)
                stats.append((m_new, alpha, alpha * l + jnp.sum(p, axis=-1, keepdims=True)))
            pvs = [mm(p, v_ref[rows, cols]) for p, cols in zip(ps, heads)]
            return tuple((m_new, l, alpha * acc + pv)
                         for (m_new, alpha, l), (_, _, acc), pv in zip(stats, carry, pvs))

        carry = tuple((jnp.full((tq, 1), -jnp.inf, F32), jnp.zeros((tq, 1), F32), jnp.zeros((tq, LANE), F32))
                      for _ in heads)
        n_full = (i * tq) // tk
        carry = lax.fori_loop(0, n_full, functools.partial(step, size=tk, masked=False), carry)
        (m0, l0, acc0), (m1, l1, acc1) = step(n_full, carry, size=tk, masked=True)
        o_ref[...] = acc0 / l0 + acc1 / l1
        lse_ref[...] = jnp.where(lane >= HEAD, m1 + jnp.log2(l1), m0 + jnp.log2(l0))

    return pl.pallas_call(
        body, name="attention_forward", grid=(HEADS // 2, n_q),
        in_specs=[pl.BlockSpec((tq, 2 * LANE), lambda p, i: (i, p)),
                  pl.BlockSpec((n_rows, 2 * LANE), lambda p, i: (0, p)),
                  pl.BlockSpec((n_rows, 2 * LANE), lambda p, i: (0, p))],
        out_specs=[pl.BlockSpec((tq, LANE), lambda p, i: (i, p))] * 2,
        out_shape=[jax.ShapeDtypeStruct((n_rows, WIDTH), F32)] * 2,
        compiler_params=pltpu.CompilerParams(dimension_semantics=("arbitrary", "arbitrary"),
                                             vmem_limit_bytes=VMEM_LIMIT),
    )(q, k, v)


def attention_backward(q, k, v, o, do, lse):
    n_rows = q.shape[0]
    tq, tk = ATTN_BWD_TILES
    n_q = n_rows // tq
    n_masked = max(1, tk // tq)

    def body(q_ref, k_ref, v_ref, o_ref, do_ref, lse_ref, dq_ref, dk_ref, dv_ref):
        j = pl.program_id(1)

        @pl.when(j == 0)
        def _():
            dq_ref[...] = jnp.zeros_like(dq_ref)

        lane = lane_iota((tq, LANE))
        heads = [slice(0, LANE), slice(LANE, 2 * LANE)]
        ks = [k_ref[:, cols] for cols in heads]
        vs = [v_ref[:, cols] for cols in heads]
        head_lanes = [(lane < HEAD).astype(F32), (lane >= HEAD).astype(F32)]

        def step(i, carry, masked):
            rows = pl.ds(pl.multiple_of(i * tq, tq), tq)
            qs = [q_ref[rows, cols] for cols in heads]
            dout = do_ref[rows, :]
            dout_o = dout * o_ref[rows, :]
            lse_t = lse_ref[rows, :]
            ss = [mm_nt(qh, kh) for qh, kh in zip(qs, ks)]
            dps = [mm_nt(dout, vh) for vh in vs]
            ps, dss = [], []
            for hh in range(2):
                delta = jnp.sum(dout_o * head_lanes[hh], axis=-1, keepdims=True)
                lse_h = jnp.sum(jnp.where(lane == hh * HEAD, lse_t, 0.0), axis=-1, keepdims=True)
                p = jnp.exp2(ss[hh] - lse_h)
                if masked:
                    p = jnp.where(visible(i * tq, j * tk, p.shape), p, 0.0)
                ps.append(p)
                dss.append(p * (dps[hh] - delta))
            dvs = [mm_tn(p, dout) for p in ps]
            dqs = [mm(ds, kh) for ds, kh in zip(dss, ks)]
            dks = [mm_tn(ds, qh) for ds, qh in zip(dss, qs)]
            for cols, dq in zip(heads, dqs):
                dq_ref[rows, cols] += dq * ATTN_SCALE
            return tuple((dk + a, dv + b) for (dk, dv), a, b in zip(carry, dks, dvs))

        carry = tuple((jnp.zeros((tk, LANE), F32), jnp.zeros((tk, LANE), F32)) for _ in heads)
        i_first = (j * tk) // tq
        for extra in range(n_masked):
            carry = step(i_first + extra, carry, masked=True)
        carry = lax.fori_loop(i_first + n_masked, n_q, functools.partial(step, masked=False), carry)
        for cols, (dk, dv) in zip(heads, carry):
            dk_ref[:, cols] = dk * (1.0 / LOG2_E)
            dv_ref[:, cols] = dv

    full = lambda w: pl.BlockSpec((n_rows, w), lambda p, j: (0, p))
    blk = pl.BlockSpec((tk, 2 * LANE), lambda p, j: (j, p))
    return pl.pallas_call(
        body, name="attention_backward", grid=(HEADS // 2, n_rows // tk),
        in_specs=[full(2 * LANE), blk, blk, full(LANE), full(LANE), full(LANE)],
        out_specs=[full(2 * LANE), blk, blk],
        out_shape=[jax.ShapeDtypeStruct((n_rows, HEADS * LANE), F32)] * 3,
        compiler_params=pltpu.CompilerParams(dimension_semantics=("arbitrary", "arbitrary"),
                                             vmem_limit_bytes=VMEM_LIMIT),
    )(q, k, v, o, do, lse)


def tail_tile(step0, tile0, x, tgt, ma, mb, gpa, gpb, ya, y, ur, k2, uv,
              mod, wpa, wpb, wout, gn_g, gn_b, r_k, post_g, post_b, bd):
    gate = mod[2:3]
    inv = 1.0 / HEAD
    yc = y - head_sum(y, bd) * inv
    rs = lax.rsqrt(head_sum(yc * yc, bd) * inv + GN_EPS)
    yn = yc * rs
    yb = yn * gn_g + gn_b + head_sum(ur * k2 * r_k, bd) * uv
    sga, sgb = sigmoid(gpa), sigmoid(gpb)
    sila, silb = gpa * sga, gpb * sgb
    ga, gb = ya * sila, yb * silb
    pa, pb = mm(ga, wpa), mm(gb, wpb)
    sa, sb = sigmoid(ma), sigmoid(mb)
    merged = sa * pa + sb * pb
    sub = mm(merged, wout)
    z = ALPHA * x + (1.0 + gate) * sub
    zhat, rstd = layer_norm_stats(z)
    err = zhat * post_g + post_b - tgt
    loss = 0.5 * jnp.sum(rowmean(err * err), axis=0, keepdims=True) + jnp.zeros((1, LANE), F32)
    dout = err * (1.0 / D_MODEL)
    dpost_g = colsum(dout * zhat)
    dpost_b = colsum(dout)
    dz = layer_norm_bwd(dout * post_g, zhat, rstd)
    dgate = colsum(dz * sub)
    dsub = dz * (1.0 + gate)
    dwout = mm_tn(merged, dsub)
    dmerged = mm_nt(dsub, wout)
    dpa, dpb = dmerged * sa, dmerged * sb
    dma = dmerged * pa * sa * (1.0 - sa)
    dmb = dmerged * pb * sb * (1.0 - sb)
    dwpa = mm_tn(ga, dpa)
    dwpb = mm_tn(gb, dpb)
    dga = mm_nt(dpa, wpa)
    dgb = mm_nt(dpb, wpb)
    dya = dga * sila
    dgpa = dga * ya * (sga * (1.0 + gpa * (1.0 - sga)))
    dyb = dgb * silb
    dgpb = dgb * yb * (sgb * (1.0 + gpb * (1.0 - sgb)))
    dgn_g = colsum(dyb * yn)
    dgn_b = colsum(dyb)
    dyn = dyb * gn_g
    dy = rs * (dyn - head_sum(dyn, bd) * inv - yn * head_sum(dyn * yn, bd) * inv)
    return (dz, dma, dmb, dgpa, dgpb, dya, dy, dyb,
            loss, dwout, dwpa, dwpb, dgn_g, dgn_b, dpost_g, dpost_b, dgate)


def mla_prep_bwd_tile(step0, tile0, q_c, kv_c, cos, sin, dq, dk, dv, gq, gkv, wq, wqr, wkn, wv):
    qn, qh, rq = rms_norm_fwd(q_c, gq)
    kvn, kvh, rkv = rms_norm_fwd(kv_c, gkv)
    dqc = dq * tile_lanes(cos, HEADS)
    dqs = dq * tile_lanes(sin, HEADS)
    dqn = mm_nt(dqc, wq) + mm_nt(dqs, wqr)
    dkvn = mm_nt(dk, wkn) + mm_nt(dv, wv)
    dkpe = dk[:, 0:LANE]
    for h in range(1, HEADS):
        dkpe = dkpe + dk[:, h * LANE:(h + 1) * LANE]
    dkr = dkpe * (cos * key_rope_mask(cos.shape))
    dkrr = dkpe * sin

    def rms_bwd(dyv, xh, r, g):
        dyg = dyv * g
        return r * (dyg - xh * rowmean(dyg * xh)), colsum(dyv * xh)

    dq_c, dgq = rms_bwd(dqn, qh, rq, gq)
    dkv_c, dgkv = rms_bwd(dkvn, kvh, rkv, gkv)
    return (dq_c, dkv_c, dkr, dkrr,
            mm_tn(qn, dqc), mm_tn(qn, dqs), mm_tn(kvn, dk), mm_tn(kvn, dv), dgq, dgkv)


def rwkv_prep_bwd_tile(step0, tile0, r0, k0, v0, l0, drt, dat, dbt, dkt, dvv, dlw, dyb, hr, hk, hv, hl,
                       mu_r, mu_k, mu_v, mu_l, w0, a0, k_k, k_a, w_dec, w_iclr, tril, same, bd, r_k,
                       cr, ck, cv, cl_):
    f = rwkv_prep_core(tile0, r0, k0, v0, l0, hr, hk, hv, hl, mu_r, mu_k, mu_v, mu_l, w0, a0, k_k, k_a,
                       w_dec, w_iclr, tril, same, bd)
    ur, uk, uv, ul, kk, k2, a_ic, sg, th = (f[n] for n in ("ur", "uk", "uv", "ul", "kk", "k2", "a_ic", "sg", "th"))
    lc, lw = f["lc"], f["lw"]
    e_neg = jnp.exp(-lc)
    dur = drt * jnp.exp(lc)
    da = dat * jnp.exp(lc - lw)
    db = dbt * e_neg
    dk2 = dkt * e_neg
    s = head_sum(ur * k2 * r_k, bd)
    duv = dvv + dyb * s
    ds = head_sum(dyb * uv, bd)
    dur = dur + ds * k2 * r_k
    dk2 = dk2 + ds * ur * r_k
    dr_k = colsum(ds * ur * k2)
    dkk = db * a_ic - da
    da_ic = db * kk + dk2 * uk * k_a
    duk = dk2 * (1.0 + (a_ic - 1.0) * k_a)
    dk_a = colsum(dk2 * uk * (a_ic - 1.0))
    dkkraw = jnp.where(f["nrm_raw"] > 1e-12, (dkk - kk * head_sum(dkk * kk, bd)) / f["nrm"], dkk * 1e12)
    duk = duk + dkkraw * k_k
    dk_k = colsum(dkkraw * uk)
    dai = da_ic * a_ic * (1.0 - a_ic)
    dd = dlw * (-DECAY_SCALE) * sg * (1.0 - sg)
    dul = mm_nt(dai, w_iclr) + mm_nt(dd, w_dec) * (1.0 - th * th)

    def unshift(du, x, prev, mu, carry_row):
        nxt = shift_rows_up(du, carry_row)
        return du * (1.0 - mu) + nxt * mu, colsum(du * (prev - x)), du[0:1, :]

    dr0, dmu_r, ncr = unshift(dur, r0, f["pr"], mu_r, cr)
    dk0, dmu_k, nck = unshift(duk, k0, f["pk"], mu_k, ck)
    dv0, dmu_v, ncv = unshift(duv, v0, f["pv"], mu_v, cv)
    dl0, dmu_l, ncl = unshift(dul, l0, f["pl"], mu_l, cl_)
    return (dr0, dk0, dv0, dl0,
            dmu_r, dmu_k, dmu_v, dmu_l, colsum(dd), colsum(dai), dk_k, dk_a, dr_k, mm_tn(th, dd), mm_tn(ul, dai),
            ncr, nck, ncv, ncl)


def in_backward(x, dz, pieces, mod, w_in_p, unrot):
    n_rows = x.shape[0]
    ts = ROW_TILE
    n_p = len(pieces)
    shard_cols = IN_WIDTH // N_DEV

    def body(*refs):
        x_ref, dz_ref = refs[:2]
        p_refs = refs[2:2 + n_p]
        mod_ref, w_ref, unrot_ref = refs[2 + n_p:5 + n_p]
        dx_ref, ht_ref, blocks_ref, dshift_ref, dscale_ref = refs[5 + n_p:]
        step0 = pl.program_id(0) == 0
        dma, dmb, dr0, dk0, dv0, dgpa, dgpb, dq_c, dkv_c, dkr, dkrr, dl0 = (r[...] for r in p_refs)
        dproj = jnp.concatenate([dma, dmb, dr0, dk0, dv0, dgpa, dgpb, dq_c, dkv_c, dkr, dkrr, dl0], axis=1)
        dh = mm_nt(dproj, w_ref[...])
        xhat, rstd = layer_norm_stats(x_ref[...])
        scale1 = 1.0 + mod_ref[1:2, :]
        dx_ref[...] = layer_norm_bwd(dh * scale1, xhat, rstd) + ALPHA * dz_ref[...]
        ht_ref[...] = jnp.transpose(xhat * scale1 + mod_ref[0:1, :]).astype(BF16)
        dkrope = (dkr.astype(F32) + mm(dkrr, unrot_ref[...]))[:, NOPE:QK_DIM]
        natural = jnp.concatenate(
            [dq_c.astype(F32), dkv_c.astype(F32), dkrope]
            + [p.astype(F32) for p in (dgpa, dr0, dk0, dv0, dl0, dgpb, dma, dmb)], axis=1)
        for j in range(N_DEV):
            blocks_ref[j] = natural[:, j * shard_cols:(j + 1) * shard_cols].astype(BF16)
        for ref, val in ((dshift_ref, colsum(dh)), (dscale_ref, colsum(dh * xhat))):
            @pl.when(step0)
            def _(ref=ref, val=val):
                ref[...] = val

            @pl.when(jnp.logical_not(step0))
            def _(ref=ref, val=val):
                ref[...] += val

    row = lambda w: pl.BlockSpec((ts, w), lambda i: (i, 0))
    const = pl.BlockSpec(memory_space=pltpu.VMEM)
    vec = pl.BlockSpec((1, D_MODEL), lambda i: (0, 0))
    return pl.pallas_call(
        body, name="in_backward", grid=(n_rows // ts,),
        in_specs=[row(D_MODEL), row(D_MODEL)] + [row(p.shape[1]) for p in pieces] + [const] * 3,
        out_specs=[row(D_MODEL), pl.BlockSpec((D_MODEL, ts), lambda i: (0, i)),
                   pl.BlockSpec((N_DEV, ts, shard_cols), lambda i: (0, i, 0)), vec, vec],
        out_shape=[jax.ShapeDtypeStruct((n_rows, D_MODEL), F32), jax.ShapeDtypeStruct((D_MODEL, n_rows), BF16),
                   jax.ShapeDtypeStruct((N_DEV, n_rows, shard_cols), BF16),
                   jax.ShapeDtypeStruct((1, D_MODEL), F32), jax.ShapeDtypeStruct((1, D_MODEL), F32)],
        compiler_params=pltpu.CompilerParams(dimension_semantics=("arbitrary",), vmem_limit_bytes=VMEM_LIMIT),
    )(x, dz, *pieces, mod, w_in_p, unrot)


def in_weight_grad_exchange(h_t, dp_blocks, others, small, order):
    n = len(others)
    n_rows = h_t.shape[1]
    ts = 2 * ROW_TILE
    n_i = n_rows // ts
    shard_cols = dp_blocks.shape[2]
    n_chips = N_DEV // 2
    last = N_DEV - 1

    def body(order_ref, h_ref, dp_ref, *rest):
        g_refs, s_ref = rest[:n], rest[n]
        rwin_ref, rg_refs, rs_ref = rest[n + 1], rest[n + 2:2 * n + 2], rest[2 * n + 2]
        (acc, sendbuf, sib_buf, sib_send, sib_recv, win_send, win_recv,
         o_send, o_recv, local_sems) = rest[2 * n + 3:]
        b, i = pl.program_id(0), pl.program_id(1)
        me = my_position()
        mi = flat_index(me)
        sibling = (me[0], me[1], 1 - me[2])

        def other_copies(k, src_index, dst_index):
            peer = flip(me, k)
            out = [pltpu.make_async_remote_copy(
                src_ref=g_refs[a].at[src_index], dst_ref=rg_refs[a].at[dst_index],
                send_sem=o_send.at[(n + 1) * (k - 1) + a], recv_sem=o_recv.at[(n + 1) * (k - 1) + a],
                device_id=peer, device_id_type=MESH_IDS) for a in range(n)]
            out.append(pltpu.make_async_remote_copy(
                src_ref=s_ref, dst_ref=rs_ref.at[dst_index],
                send_sem=o_send.at[(n + 1) * (k - 1) + n], recv_sem=o_recv.at[(n + 1) * (k - 1) + n],
                device_id=peer, device_id_type=MESH_IDS))
            return out

        def local_copies():
            out = [pltpu.make_async_copy(g_refs[a].at[mi], rg_refs[a].at[mi], local_sems.at[a]) for a in range(n)]
            out.append(pltpu.make_async_copy(s_ref, rs_ref.at[mi], local_sems.at[n]))
            return out

        def to_sibling(t):
            return pltpu.make_async_remote_copy(
                src_ref=sendbuf.at[t], dst_ref=sib_buf.at[t], send_sem=sib_send.at[t], recv_sem=sib_recv.at[t],
                device_id=sibling, device_id_type=MESH_IDS)

        def to_owner(t):
            flip_x = (t < 2) * 1
            flip_y = 1 - (t & 1)
            owner = (me[0] ^ flip_x, me[1] ^ flip_y, me[2])
            return pltpu.make_async_remote_copy(
                src_ref=sendbuf.at[n_chips + t], dst_ref=rwin_ref.at[t], send_sem=win_send.at[t],
                recv_sem=win_recv.at[t], device_id=owner, device_id_type=MESH_IDS)

        own_block = pltpu.make_async_copy(sendbuf.at[last], rwin_ref.at[n_chips - 1], local_sems.at[n + 1])

        @pl.when(jnp.logical_and(b == 0, i == 0))
        def _():
            for cp in local_copies():
                cp.start()
            for k in range(1, N_DEV):
                for cp in other_copies(k, flat_index(flip(me, k)), mi):
                    cp.start()

        contrib = jnp.dot(h_ref[...], dp_ref[...], preferred_element_type=F32)

        @pl.when(i == 0)
        def _():
            acc[...] = contrib

        @pl.when(i > 0)
        def _():
            acc[...] += contrib

        slot = order_ref[N_DEV + b]
        t = slot & (n_chips - 1)

        @pl.when(jnp.logical_and(i == n_i - 1, slot < n_chips))
        def _():
            sendbuf[slot] = acc[...].astype(BF16)
            to_sibling(t).start()

        @pl.when(jnp.logical_and(i == n_i - 1, slot >= n_chips))
        def _():
            to_sibling(t).wait_recv()
            sendbuf[slot] = (acc[...] + sib_buf[t].astype(F32)).astype(BF16)

            @pl.when(slot < last)
            def _():
                to_owner(t).start()

            @pl.when(slot == last)
            def _():
                own_block.start()

        @pl.when(jnp.logical_and(b == last, i == n_i - 1))
        def _():
            for t in range(n_chips - 1):
                to_owner(t).wait_recv()
            for k in range(1, N_DEV):
                pi = flat_index(flip(me, k))
                for cp in other_copies(k, pi, pi):
                    cp.wait_recv()
            for t in range(n_chips):
                to_sibling(t).wait_send()
            for t in range(n_chips - 1):
                to_owner(t).wait_send()
            for k in range(1, N_DEV):
                for cp in other_copies(k, flat_index(flip(me, k)), mi):
                    cp.wait_send()
            for cp in local_copies():
                cp.wait()
            own_block.wait()

    hbm = pl.BlockSpec(memory_space=pl.ANY)
    n_sem = 7 * (n + 1)
    grid_spec = pltpu.PrefetchScalarGridSpec(
        num_scalar_prefetch=1, grid=(N_DEV, n_i),
        in_specs=[pl.BlockSpec((D_MODEL, ts), lambda b, i, order: (0, i)),
                  pl.BlockSpec((None, ts, shard_cols), lambda b, i, order: (order[b], i, 0))] + [hbm] * (n + 1),
        out_specs=[hbm] * (n + 2),
        scratch_shapes=[pltpu.VMEM((D_MODEL, shard_cols), F32), pltpu.VMEM((N_DEV, D_MODEL, shard_cols), BF16),
                        pltpu.VMEM((n_chips, D_MODEL, shard_cols), BF16),
                        pltpu.SemaphoreType.DMA((n_chips,)), pltpu.SemaphoreType.DMA((n_chips,)),
                        pltpu.SemaphoreType.DMA((n_chips - 1,)), pltpu.SemaphoreType.DMA((n_chips - 1,)),
                        pltpu.SemaphoreType.DMA((n_sem,)), pltpu.SemaphoreType.DMA((n_sem,)),
                        pltpu.SemaphoreType.DMA((n + 2,))])
    return pl.pallas_call(
        body, name="in_weight_grad_exchange", grid_spec=grid_spec,
        out_shape=[jax.ShapeDtypeStruct((n_chips, D_MODEL, shard_cols), BF16)]
        + [jax.ShapeDtypeStruct(o.shape, o.dtype) for o in others]
        + [jax.ShapeDtypeStruct((N_DEV,) + small.shape, small.dtype)],
        compiler_params=pltpu.CompilerParams(dimension_semantics=("arbitrary", "arbitrary"),
                                             vmem_limit_bytes=VMEM_LIMIT),
    )(order, h_t, dp_blocks, *others, small)


def ada_weight_grad(c_all, dmod_cols):
    def body(c_ref, d_ref, o_ref):
        cv = c_ref[...]
        o_ref[...] = hdot_tn(cv * sigmoid(cv), d_ref[...])

    return pl.pallas_call(
        body, name="ada_weight_grad",
        out_shape=jax.ShapeDtypeStruct((c_all.shape[1], dmod_cols.shape[1]), F32),
    )(c_all, dmod_cols)


def adamw_update(g, w, m, v):
    nm = ADAM_B1 * m + (1.0 - ADAM_B1) * g
    nv = ADAM_B2 * v + (1.0 - ADAM_B2) * (g * g)
    m_hat = nm / (1.0 - ADAM_B1 ** ADAM_STEP)
    v_hat = nv / (1.0 - ADAM_B2 ** ADAM_STEP)
    return -ADAM_LR * (m_hat / (jnp.sqrt(v_hat) + ADAM_EPS) + ADAM_WD * w), nm, nv


def adamw(parts, w, m, v, name):
    k, rows, cols = parts.shape
    rb = 128 if rows % 128 == 0 else rows

    def body(p_ref, w_ref, m_ref, v_ref, g_ref, d_ref, nm_ref, nv_ref):
        g = p_ref[0].astype(F32)
        for i in range(1, k):
            g = g + p_ref[i].astype(F32)
        g_ref[0] = g
        d_ref[0], nm_ref[0], nv_ref[0] = adamw_update(g, w_ref[0], m_ref[0], v_ref[0])

    blk = pl.BlockSpec((1, rb, cols), lambda i: (0, i, 0))
    return pl.pallas_call(
        body, name=name, grid=(rows // rb,),
        in_specs=[pl.BlockSpec((k, rb, cols), lambda i: (0, i, 0)), blk, blk, blk],
        out_specs=[blk] * 4, out_shape=[jax.ShapeDtypeStruct((1, rows, cols), F32)] * 4,
        compiler_params=pltpu.CompilerParams(dimension_semantics=("arbitrary",), vmem_limit_bytes=VMEM_LIMIT),
    )(parts, w, m, v)


def adamw_small(parts, ws, ms, vs):
    k = parts.shape[0]
    n = len(ws)
    sizes = [w.shape[1] for w in ws]

    def body(p_ref, *refs):
        ins, outs = refs[:3 * n], refs[3 * n:]
        g_all = p_ref[0]
        for i in range(1, k):
            g_all = g_all + p_ref[i]
        off = 0
        for a, size in enumerate(sizes):
            g = g_all[:, off:off + size]
            off += size
            d, nm, nv = adamw_update(g, ins[a][...], ins[n + a][...], ins[2 * n + a][...])
            for kind, val in enumerate((g, d, nm, nv)):
                outs[kind * n + a][...] = val

    return pl.pallas_call(
        body, name="adamw_small",
        out_shape=[jax.ShapeDtypeStruct((1, size), F32) for _ in range(4) for size in sizes],
    )(parts, *ws, *ms, *vs)


def rot_cols(w):
    return jnp.concatenate([-w[:, ROPE // 2:], w[:, :ROPE // 2]], axis=1)


def unrot_cols(dw):
    return jnp.concatenate([dw[:, ROPE // 2:], -dw[:, :ROPE // 2]], axis=1)


def columns_from_shards(g, rows, cols):
    return g.reshape(N_DEV, rows, cols).transpose(1, 0, 2).reshape(rows, N_DEV * cols)


def shards_from_columns(w, rows, cols):
    return w.reshape(rows, N_DEV, cols).transpose(1, 0, 2).reshape(N_DEV, rows * cols)


def permute_w_in(w):
    z = lambda n: jnp.zeros((D_MODEL, n), w.dtype)
    krope = w[:, N_KROPE:N_KROPE + ROPE]
    rw = N_RWKV
    return jnp.concatenate([
        w[:, N_MA:N_MA + 1024], w[:, N_MB:N_MB + 1024],
        w[:, rw:rw + 512], w[:, rw + 512:rw + 1024], w[:, rw + 1024:rw + 1536],
        w[:, N_GPA:N_GPA + 512], w[:, N_GPB:N_GPB + 512],
        w[:, N_QC:N_QC + 256], w[:, N_KVC:N_KVC + 128],
        z(NOPE), krope, z(LANE - QK_DIM), z(NOPE), rot_cols(krope), z(LANE - QK_DIM),
        w[:, rw + 1536:rw + 1664]], axis=1)


def unpermute_w_in_grad(d):
    rw = P_R
    krope = d[:, P_KR + NOPE:P_KR + QK_DIM] + unrot_cols(d[:, P_KRR + NOPE:P_KRR + QK_DIM])
    return jnp.concatenate([
        d[:, P_QC:P_QC + 256], d[:, P_KVC:P_KVC + 128], krope, d[:, P_GPA:P_GPA + 512],
        d[:, rw:rw + 1536], d[:, P_LORA:P_LORA + 128], d[:, P_GPB:P_GPB + 512],
        d[:, P_MA:P_MA + 1024], d[:, P_MB:P_MB + 1024]], axis=1)


def pad_heads_q(w_uq):
    w = w_uq.reshape(Q_RANK, HEADS, QK_DIM)
    zpad = jnp.zeros((Q_RANK, HEADS, LANE - QK_DIM), w.dtype)
    wq = jnp.concatenate([w, zpad], axis=2).reshape(Q_RANK, HEADS * LANE)
    pe = w[:, :, NOPE:]
    rot = jnp.concatenate([-pe[:, :, ROPE // 2:], pe[:, :, :ROPE // 2]], axis=2)
    wqr = jnp.concatenate([jnp.zeros((Q_RANK, HEADS, NOPE), w.dtype), rot, zpad], axis=2).reshape(Q_RANK, HEADS * LANE)
    return wq, wqr


def unpad_heads_q_grad(dwq, dwqr):
    a = dwq.reshape(Q_RANK, HEADS, LANE)
    r = dwqr.reshape(Q_RANK, HEADS, LANE)[:, :, NOPE:QK_DIM]
    pe = a[:, :, NOPE:QK_DIM] + jnp.concatenate([r[:, :, ROPE // 2:], -r[:, :, :ROPE // 2]], axis=2)
    return jnp.concatenate([a[:, :, :NOPE], pe], axis=2).reshape(Q_RANK, HEADS * QK_DIM)


def pad_heads_kv(w_ukv):
    w = w_ukv.reshape(KV_RANK, HEADS, 2 * HEAD)
    z = jnp.zeros((KV_RANK, HEADS, HEAD), w.dtype)
    wkn = jnp.concatenate([w[:, :, :NOPE], z], axis=2).reshape(KV_RANK, HEADS * LANE)
    val = w[:, :, NOPE:]
    odd = (jnp.arange(HEADS) % 2 == 1)[None, :, None]
    wv = jnp.concatenate([jnp.where(odd, 0, val), jnp.where(odd, val, 0)], axis=2).reshape(KV_RANK, HEADS * LANE)
    return wkn, wv


def unpad_heads_kv_grad(dwkn, dwv):
    a = dwkn.reshape(KV_RANK, HEADS, LANE)[:, :, :NOPE]
    b = dwv.reshape(KV_RANK, HEADS, LANE)
    odd = (jnp.arange(HEADS) % 2 == 1)[None, :, None]
    val = jnp.where(odd, b[:, :, HEAD:], b[:, :, :HEAD])
    return jnp.concatenate([a, val], axis=2).reshape(KV_RANK, HEADS * 2 * HEAD)


def kernel(x, c, positions, w_ada, b_ada, w_in, q_norm_g, w_uq, kv_norm_g, w_ukv, mu_rwkv, w0, w_decay_up, a0, w_iclr_up, k_k, k_a, r_k, gn_g, gn_b, w_proj_a, w_proj_b, w_out, post_g, post_b, loss_target, m_w_ada, m_b_ada, m_w_in, m_q_norm_g, m_w_uq, m_kv_norm_g, m_w_ukv, m_mu_rwkv, m_w0, m_w_decay_up, m_a0, m_w_iclr_up, m_k_k, m_k_a, m_r_k, m_gn_g, m_gn_b, m_w_proj_a, m_w_proj_b, m_w_out, m_post_g, m_post_b, v_w_ada, v_b_ada, v_w_in, v_q_norm_g, v_w_uq, v_kv_norm_g, v_w_ukv, v_mu_rwkv, v_w0, v_w_decay_up, v_a0, v_w_iclr_up, v_k_k, v_k_a, v_r_k, v_gn_g, v_gn_b, v_w_proj_a, v_w_proj_b, v_w_out, v_post_g, v_post_b):
    weights = dict(w_ada=w_ada, b_ada=b_ada, w_in=w_in, q_norm_g=q_norm_g, w_uq=w_uq, kv_norm_g=kv_norm_g,
                   w_ukv=w_ukv, mu_rwkv=mu_rwkv, w0=w0, w_decay_up=w_decay_up, a0=a0, w_iclr_up=w_iclr_up,
                   k_k=k_k, k_a=k_a, r_k=r_k, gn_g=gn_g, gn_b=gn_b, w_proj_a=w_proj_a, w_proj_b=w_proj_b,
                   w_out=w_out, post_g=post_g, post_b=post_b)
    mom1 = dict(w_ada=m_w_ada, b_ada=m_b_ada, w_in=m_w_in, q_norm_g=m_q_norm_g, w_uq=m_w_uq, kv_norm_g=m_kv_norm_g,
                w_ukv=m_w_ukv, mu_rwkv=m_mu_rwkv, w0=m_w0, w_decay_up=m_w_decay_up, a0=m_a0, w_iclr_up=m_w_iclr_up,
                k_k=m_k_k, k_a=m_k_a, r_k=m_r_k, gn_g=m_gn_g, gn_b=m_gn_b, w_proj_a=m_w_proj_a, w_proj_b=m_w_proj_b,
                w_out=m_w_out, post_g=m_post_g, post_b=m_post_b)
    mom2 = dict(w_ada=v_w_ada, b_ada=v_b_ada, w_in=v_w_in, q_norm_g=v_q_norm_g, w_uq=v_w_uq, kv_norm_g=v_kv_norm_g,
                w_ukv=v_w_ukv, mu_rwkv=v_mu_rwkv, w0=v_w0, w_decay_up=v_w_decay_up, a0=v_a0, w_iclr_up=v_w_iclr_up,
                k_k=v_k_k, k_a=v_k_a, r_k=v_r_k, gn_g=v_gn_g, gn_b=v_gn_b, w_proj_a=v_w_proj_a, w_proj_b=v_w_proj_b,
                w_out=v_w_out, post_g=v_post_g, post_b=v_post_b)
    names = list(weights)
    n_rows = x.shape[1]
    me = 4 * lax.axis_index("x") + 2 * lax.axis_index("y") + lax.axis_index("c")
    xr = x[0]
    tgt = loss_target[0]
    row = lambda a: a.reshape(1, -1)

    w_in_all, c_all = gather_shards([w_in[0].astype(BF16), c])
    c_all = c_all.reshape(N_DEV, D_MODEL)
    w_in_p = permute_w_in(columns_from_shards(w_in_all, D_MODEL, IN_WIDTH // N_DEV))

    mod_all = ada_modulation(c_all, w_ada[0], b_ada.reshape(N_DEV, -1))
    mod = lax.dynamic_index_in_dim(mod_all, me, axis=1, keepdims=False).reshape(3, D_MODEL)

    proj, *gathered = fwd_in_gather(xr, mod, w_in_p, [weights[n][0].astype(BF16) for n, _, _ in SHARDED[1:]])
    pcol = lambda off_, w: (proj, w, off_ // w)
    full = {}
    for (n, r, cdim), part in zip(SHARDED[1:], gathered):
        full[n] = part.reshape(N_DEV * r, cdim) if n == "w_out" else columns_from_shards(part, r, cdim)
    wq, wqr = pad_heads_q(full["w_uq"])
    wkn, wv = pad_heads_kv(full["w_ukv"])
    zl = jnp.zeros((LORA, WIDTH), BF16)
    w_dec = jnp.concatenate([full["w_decay_up"], zl], axis=0)
    w_iclr = jnp.concatenate([zl, full["w_iclr_up"]], axis=0)
    wpa, wpb, wout = full["w_proj_a"], full["w_proj_b"], full["w_out"]

    inv_freq = ROPE_THETA ** (-jnp.arange(0, ROPE, 2, dtype=F32) / ROPE)
    ang = positions[0].astype(F32)[:, None] * inv_freq
    ones_n, zeros_n, zeros_p = jnp.ones((n_rows, NOPE), F32), jnp.zeros((n_rows, NOPE), F32), jnp.zeros((n_rows, LANE - QK_DIM), F32)
    cos_t = jnp.concatenate([ones_n, jnp.cos(ang), jnp.cos(ang), zeros_p], axis=1)
    sin_t = jnp.concatenate([zeros_n, jnp.sin(ang), jnp.sin(ang), zeros_p], axis=1)

    gq, gkv = q_norm_g, kv_norm_g
    mla_consts = [gq, gkv, wq, wqr, wkn, wv]
    q, k, v = row_call(
        "mla_prep", mla_prep_tile, n_rows,
        [pcol(P_QC, 256), pcol(P_KVC, 128), pcol(P_KR, 128), pcol(P_KRR, 128), (cos_t, LANE, 0), (sin_t, LANE, 0)],
        mla_consts, [(HEADS * LANE, BF16)] * 3)
    ya, lse = attention_forward(q, k, v)

    t_idx = jnp.arange(ROW_TILE)
    same_chunk = (t_idx[:, None] // CHUNK) == (t_idx[None, :] // CHUNK)
    same = same_chunk.astype(F32)
    tril = (same_chunk & (t_idx[:, None] >= t_idx[None, :])).astype(F32)
    l_idx = jnp.arange(LANE)
    bd = ((l_idx[:, None] // HEAD) == (l_idx[None, :] // HEAD)).astype(F32)
    mu = mu_rwkv
    mu_r, mu_k, mu_v, mu_l = mu[:, 0:512], mu[:, 512:1024], mu[:, 1024:1536], mu[:, 1536:1664]
    rk_row = row(r_k)
    rwkv_consts = [mu_r, mu_k, mu_v, mu_l, w0, a0, k_k, k_a, w_dec, w_iclr, tril, same, bd]
    rwkv_rows = [pcol(P_R, 512), pcol(P_K, 512), pcol(P_V, 512), pcol(P_LORA, 128)]
    rt, at, bt, kt, clf, uv, ur, k2 = row_call(
        "rwkv_prep", rwkv_prep_tile, n_rows, rwkv_rows, rwkv_consts, [(WIDTH, F32)] * 8, halo_in=rwkv_rows)
    y, m0s, state_maps, out_maps, *wkv_saved = wkv_forward(at, bt, kt, rt, uv, clf)

    tail = row_call(
        "tail", tail_tile, n_rows,
        [(xr, D_MODEL, 0), (tgt, D_MODEL, 0), pcol(P_MA, 1024), pcol(P_MB, 1024), pcol(P_GPA, 512), pcol(P_GPB, 512),
         (ya, WIDTH, 0), (y, WIDTH, 0), (ur, WIDTH, 0), (k2, WIDTH, 0), (uv, WIDTH, 0)],
        [mod, wpa, wpb, wout, gn_g, gn_b, rk_row, post_g, post_b, bd],
        [(D_MODEL, F32), (1024, BF16), (1024, BF16), (512, BF16), (512, BF16), (WIDTH, F32), (WIDTH, F32), (WIDTH, F32)],
        acc_out=[((1, LANE), F32), ((D_MODEL, D_MODEL), F32), ((WIDTH, D_MODEL), F32), ((WIDTH, D_MODEL), F32),
                 ((1, WIDTH), F32), ((1, WIDTH), F32), ((1, D_MODEL), F32), ((1, D_MODEL), F32), ((1, D_MODEL), F32)])
    (dz, dma, dmb, dgpa, dgpb, dya, dy, dyb,
     loss_row, g_wout, g_wpa, g_wpb, g_gn_g, g_gn_b, g_post_g, g_post_b, dgate) = tail

    dq, dk, dv = attention_backward(q, k, v, ya, dya, lse)
    dq_c, dkv_c, dkr, dkrr, g_wq, g_wqr, g_wkn, g_wv, g_gq, g_gkv = row_call(
        "mla_prep_bwd", mla_prep_bwd_tile, n_rows,
        [pcol(P_QC, 256), pcol(P_KVC, 128), (cos_t, LANE, 0), (sin_t, LANE, 0),
         (dq, HEADS * LANE, 0), (dk, HEADS * LANE, 0), (dv, HEADS * LANE, 0)],
        mla_consts, [(256, BF16), (128, BF16), (128, BF16), (128, BF16)],
        acc_out=[((Q_RANK, HEADS * LANE), F32)] * 2 + [((KV_RANK, HEADS * LANE), F32)] * 2
        + [((1, Q_RANK), F32), ((1, KV_RANK), F32)])

    dat, dbt, dkt, drt, dvv, dlw = wkv_backward(at, bt, kt, rt, uv, clf, m0s, state_maps, out_maps, wkv_saved, dy)
    (dr0, dk0, dv0, dl0, g_mu_r, g_mu_k, g_mu_v, g_mu_l, g_w0, g_a0, g_k_k, g_k_a, g_r_k, g_wdec, g_wiclr) = row_call(
        "rwkv_prep_bwd", rwkv_prep_bwd_tile, n_rows,
        rwkv_rows + [(drt, WIDTH, 0), (dat, WIDTH, 0), (dbt, WIDTH, 0), (dkt, WIDTH, 0), (dvv, WIDTH, 0),
                     (dlw, WIDTH, 0), (dyb, WIDTH, 0)],
        rwkv_consts + [rk_row], [(512, BF16), (512, BF16), (512, BF16), (128, BF16)],
        acc_out=[((1, 512), F32)] * 3 + [((1, 128), F32)] + [((1, 512), F32)] * 5 + [((LANE, WIDTH), F32)] * 2,
        halo_in=rwkv_rows, carry=[512, 512, 512, 128], reverse=True)

    li = jnp.arange(LANE)
    src, dst = li[:, None], li[None, :]
    half = ROPE // 2
    unrot = (jnp.where((dst >= NOPE) & (dst < NOPE + half) & (src == dst + half), 1.0, 0.0)
             - jnp.where((dst >= NOPE + half) & (dst < QK_DIM) & (src == dst - half), 1.0, 0.0)).astype(BF16)
    dx, h_t, dproj_blocks, dshift, dscale = in_backward(
        xr, dz, [dma, dmb, dr0, dk0, dv0, dgpa, dgpb, dq_c, dkv_c, dkr, dkrr, dl0], mod, w_in_p, unrot)

    grads_full = {
        "w_uq": unpad_heads_q_grad(g_wq, g_wqr), "w_ukv": unpad_heads_kv_grad(g_wkn, g_wv),
        "w_decay_up": g_wdec[:LORA], "w_iclr_up": g_wiclr[LORA:],
        "w_proj_a": g_wpa, "w_proj_b": g_wpb, "w_out": g_wout}
    blocks = [(grads_full[n].reshape(N_DEV, r, cdim) if n == "w_out"
               else grads_full[n].reshape(r, N_DEV, cdim).transpose(1, 0, 2)).astype(BF16) for n, r, cdim in SHARDED[1:]]
    dmod = jnp.concatenate([dshift, dscale, dgate], axis=1)
    small = jnp.concatenate([dmod, g_gq, g_gkv, g_mu_r, g_mu_k, g_mu_v, g_mu_l, g_w0, g_a0, g_k_k, g_k_a, g_r_k,
                             g_gn_g, g_gn_b, g_post_g, g_post_b, loss_row], axis=1)
    my_x, my_y, my_c = lax.axis_index("x"), lax.axis_index("y"), lax.axis_index("c")
    chip_order = [4 * (my_x ^ fx) + 2 * (my_y ^ fy) for fx, fy in ((1, 1), (1, 0), (0, 1), (0, 0))]
    owners = [chip_order[s % 4] + (my_c if s >= 4 else 1 - my_c) for s in WGRAD_SLOTS]
    order = jnp.stack(owners + [jnp.int32(s) for s in WGRAD_SLOTS]).astype(jnp.int32)
    *got_blocks, got_small = in_weight_grad_exchange(h_t, dproj_blocks, blocks, small, order)
    loss = jnp.sum(got_small[:, 0, SMALL_ELEMS])

    ada_cols = w_ada.shape[2]
    dmod_all = got_small[:, 0, :3 * D_MODEL]
    g_ada = ada_weight_grad(c_all, lax.dynamic_slice_in_dim(dmod_all, me * ada_cols, ada_cols, axis=1))

    outs = [dict() for _ in range(4)]
    res = adamw(g_ada[None], w_ada, m_w_ada, v_w_ada, "adamw_w_ada")
    for kind in range(4):
        outs[kind]["w_ada"] = res[kind]
    for (n, r, cdim), got in zip(SHARDED, got_blocks):
        res = adamw(got, weights[n], mom1[n], mom2[n], "adamw_" + n)
        for kind in range(4):
            outs[kind][n] = res[kind]
    rows_of = lambda tree: [tree[n].reshape(1, -1) for n, _ in SMALL]
    res = adamw_small(got_small, rows_of(weights), rows_of(mom1), rows_of(mom2))
    for kind in range(4):
        for a, (n, _) in enumerate(SMALL):
            outs[kind][n] = res[kind * len(SMALL) + a].reshape(weights[n].shape)
    return (loss, dx[None], *[outs[0][n] for n in names], *[outs[1][n] for n in names],
            *[outs[2][n] for n in names], *[outs[3][n] for n in names])
```

```python
import functools
import math

import jax
import jax.numpy as jnp
from jax import lax
from jax.experimental import pallas as pl
from jax.experimental.pallas import tpu as pltpu

F32 = jnp.float32
BF16 = jnp.bfloat16
HIGHEST = lax.Precision.HIGHEST
MESH_IDS = pl.DeviceIdType.MESH

N_DEV = 8
D_MODEL = 1024
LN_EPS = 1e-5
RMS_EPS = 1e-6
GN_EPS = 64e-5
HEADS = 8
Q_RANK = 256
KV_RANK = 128
ROPE = 32
NOPE = 64
QK_DIM = NOPE + ROPE
WIDTH = 512
HEAD = 64
LORA = 64
CHUNK = 64
DEPTH = 1
ALPHA = (2.0 * DEPTH) ** 0.25
ROPE_THETA = 10000.0
ATTN_SCALE = QK_DIM ** -0.5
DECAY_SCALE = math.exp(-0.5)

ADAM_LR = 0.001
ADAM_B1 = 0.9
ADAM_B2 = 0.999
ADAM_EPS = 1e-08
ADAM_WD = 0.01
ADAM_STEP = 10

LANE = 128
PAIR = 2 * HEAD
ROW_TILE = 256
HALO_ROWS = 16
ATTN_FWD_TILES = (512, 1024)
ATTN_BWD_TILES = (512, 512)
LOG2_E = math.log2(math.e)
Q_PRESCALE = ATTN_SCALE * LOG2_E
WKV_CHUNKS_PER_STEP = 8
WGRAD_SLOTS = (0, 1, 4, 2, 5, 6, 3, 7)
VMEM_LIMIT = 56 * 1024 * 1024

P_MA, P_MB, P_R, P_K, P_V, P_GPA, P_GPB, P_QC, P_KVC, P_KR, P_KRR, P_LORA = (
    0, 1024, 2048, 2560, 3072, 3584, 4096, 4608, 4864, 4992, 5120, 5248)
P_WIDTH = 5376
DW_BLOCK = 768

N_QC, N_KVC, N_KROPE, N_GPA, N_RWKV, N_GPB, N_MA, N_MB = 0, 256, 384, 416, 928, 2592, 3104, 4128
IN_WIDTH = 5152

SHARDED = (("w_in", 1024, 644), ("w_uq", 256, 96), ("w_ukv", 128, 128), ("w_decay_up", 64, 64),
           ("w_iclr_up", 64, 64), ("w_proj_a", 512, 128), ("w_proj_b", 512, 128), ("w_out", 128, 1024))
SHARD_ELEMS = sum(r * c for _, r, c in SHARDED)
SHARD_ROWS = SHARD_ELEMS // LANE
GATHER_ROWS = SHARD_ROWS + 2 * D_MODEL // LANE
SMALL = (("b_ada", 3072), ("q_norm_g", 256), ("kv_norm_g", 128), ("mu_rwkv", 1664), ("w0", 512), ("a0", 512),
         ("k_k", 512), ("k_a", 512), ("r_k", 512), ("gn_g", 512), ("gn_b", 512), ("post_g", 1024), ("post_b", 1024))
SMALL_ELEMS = sum(n for _, n in SMALL)
SMALL_ROWS = SMALL_ELEMS // LANE


def mm(a, b):
    return jnp.dot(a.astype(BF16), b.astype(BF16), preferred_element_type=F32)


def mm_nt(a, b):
    return lax.dot_general(a.astype(BF16), b.astype(BF16), (((1,), (1,)), ((), ())), preferred_element_type=F32)


def mm_tn(a, b):
    return lax.dot_general(a.astype(BF16), b.astype(BF16), (((0,), (0,)), ((), ())), preferred_element_type=F32)


def hdot(a, b):
    return jnp.dot(a, b, precision=HIGHEST, preferred_element_type=F32)


def hdot_nt(a, b):
    return lax.dot_general(a, b, (((1,), (1,)), ((), ())), precision=HIGHEST, preferred_element_type=F32)


def hdot_tn(a, b):
    return lax.dot_general(a, b, (((0,), (0,)), ((), ())), precision=HIGHEST, preferred_element_type=F32)


def sigmoid(x):
    return 1.0 / (1.0 + jnp.exp(-x))


def colsum(x):
    return jnp.sum(x, axis=0, keepdims=True)


def rowmean(x):
    return jnp.mean(x, axis=-1, keepdims=True)


def layer_norm_stats(x):
    xc = x - rowmean(x)
    rstd = lax.rsqrt(rowmean(xc * xc) + LN_EPS)
    return xc * rstd, rstd


def layer_norm_bwd(dy, xhat, rstd):
    return rstd * (dy - rowmean(dy) - xhat * rowmean(dy * xhat))


def bf16_pieces(x, n):
    pieces = []
    for _ in range(n):
        p = x.astype(BF16)
        pieces.append(p)
        x = x - p.astype(F32)
    return pieces


def ones_dot(ones, x, n_pieces):
    ones = ones.astype(BF16)
    return sum(jnp.dot(ones, p, preferred_element_type=F32) for p in bf16_pieces(x, n_pieces))


def ones_dot_nt(ones, x, n_pieces):
    ones = ones.astype(BF16)
    return sum(lax.dot_general(ones, p, (((1,), (1,)), ((), ())), preferred_element_type=F32)
               for p in bf16_pieces(x, n_pieces))


def head_sum(x, bd):
    bd = bd.astype(BF16)
    out = []
    for p in range(x.shape[1] // LANE):
        hi, lo = bf16_pieces(x[:, p * LANE:(p + 1) * LANE], 2)
        out.append(jnp.dot(hi, bd, preferred_element_type=F32) + jnp.dot(lo, bd, preferred_element_type=F32))
    return jnp.concatenate(out, axis=1)


def tile_lanes(t, n):
    return jnp.concatenate([t] * n, axis=1)


def row_iota(shape):
    return lax.broadcasted_iota(jnp.int32, shape, 0)


def lane_iota(shape):
    return lax.broadcasted_iota(jnp.int32, shape, 1)


def shift_rows_down(x, row0):
    rolled = pltpu.roll(x, 1, axis=0)
    return jnp.where(row_iota(x.shape) == 0, row0, rolled)


def shift_rows_up(x, row_last):
    rolled = pltpu.roll(x, x.shape[0] - 1, axis=0)
    return jnp.where(row_iota(x.shape) == x.shape[0] - 1, row_last, rolled)


def row_call(name, fn, n_rows, row_in, const_in, row_out, acc_out=(), halo_in=(), carry=(), reverse=False):
    ts = ROW_TILE
    n_tiles = n_rows // ts
    n_in = len(row_in) + len(halo_in) + len(const_in)
    n_ro, n_ao = len(row_out), len(acc_out)

    def tile_of(g):
        return (n_tiles - 1 - g) if reverse else g

    def body(*refs):
        ins = refs[:n_in]
        ro = refs[n_in:n_in + n_ro]
        ao = refs[n_in + n_ro:n_in + n_ro + n_ao]
        cr = refs[n_in + n_ro + n_ao:]
        g = pl.program_id(0)
        step0 = g == 0
        tile0 = tile_of(g) == 0
        for r in cr:
            @pl.when(step0)
            def _(r=r):
                r[...] = jnp.zeros_like(r)
        n_tiled = len(row_in) + len(halo_in)
        vals = [r[...].astype(F32) for r in ins[:n_tiled]] + [r[...] for r in ins[n_tiled:]]
        outs = fn(step0, tile0, *vals, *[c[0:1, :] for c in cr])
        for r, v in zip(ro, outs[:n_ro]):
            r[...] = v.astype(r.dtype)
        for r, v in zip(ao, outs[n_ro:n_ro + n_ao]):
            @pl.when(step0)
            def _(r=r, v=v):
                r[...] = v.astype(r.dtype)

            @pl.when(jnp.logical_not(step0))
            def _(r=r, v=v):
                r[...] += v.astype(r.dtype)
        for r, v in zip(cr, outs[n_ro + n_ao:]):
            r[0:1, :] = v

    in_specs = [pl.BlockSpec((ts, w), functools.partial(lambda g, cb: (tile_of(g), cb), cb=cb)) for _, w, cb in row_in]
    in_specs += [pl.BlockSpec((HALO_ROWS, w), functools.partial(
        lambda g, cb: (jnp.maximum(tile_of(g) * (ts // HALO_ROWS) - 1, 0), cb), cb=cb)) for _, w, cb in halo_in]
    in_specs += [pl.BlockSpec(memory_space=pltpu.VMEM) for _ in const_in]
    out_specs = [pl.BlockSpec((ts, w), lambda g: (tile_of(g), 0)) for w, _ in row_out]
    out_specs += [pl.BlockSpec(s, lambda g: (0, 0)) for s, _ in acc_out]
    out_shape = [jax.ShapeDtypeStruct((n_rows, w), d) for w, d in row_out]
    out_shape += [jax.ShapeDtypeStruct(s, d) for s, d in acc_out]
    return pl.pallas_call(
        body, name=name, grid=(n_tiles,), in_specs=in_specs, out_specs=out_specs, out_shape=out_shape,
        scratch_shapes=[pltpu.VMEM((8, w), F32) for w in carry],
        compiler_params=pltpu.CompilerParams(dimension_semantics=("arbitrary",), vmem_limit_bytes=VMEM_LIMIT),
    )(*[a for a, _, _ in row_in], *[a for a, _, _ in halo_in], *const_in)


def my_position():
    return lax.axis_index("x"), lax.axis_index("y"), lax.axis_index("c")


def flip(pos, k):
    x, y, c = pos
    dx, dy, dc = (k >> 2) & 1, (k >> 1) & 1, k & 1
    return (1 - x if dx else x, 1 - y if dy else y, 1 - c if dc else c)


def flat_index(pos):
    return 4 * pos[0] + 2 * pos[1] + pos[2]


def gather_shards(shards):
    n = len(shards)

    def body(*refs):
        x_refs, out_refs = refs[:n], refs[n:2 * n]
        send_sems, recv_sems, local_sems = refs[2 * n:]
        x, y, c = my_position()
        me, sibling = (x, y, c), (x, y, 1 - c)
        chips = [(1 - x, y), (x, 1 - y), (1 - x, 1 - y)]

        def copy(a, k, block, to, from_input=False):
            slot = out_refs[a].at[flat_index(block)]
            return pltpu.make_async_remote_copy(
                src_ref=x_refs[a] if from_input else slot, dst_ref=slot,
                send_sem=send_sems.at[7 * a + k], recv_sem=recv_sems.at[7 * a + k],
                device_id=to, device_id_type=MESH_IDS)

        mine = [pltpu.make_async_copy(x_refs[a], out_refs[a].at[flat_index(me)], local_sems.at[a]) for a in range(n)]
        for cp in mine:
            cp.start()
        first = []
        for a in range(n):
            first.append(copy(a, 0, me, sibling, from_input=True))
            first += [copy(a, 1 + j, me, (*chip, c), from_input=True) for j, chip in enumerate(chips)]
        for cp in first:
            cp.start()
        passed = []
        for j, chip in enumerate(chips):
            for a in range(n):
                copy(a, 1 + j, (*chip, c), me).wait_recv()
                cp = copy(a, 4 + j, (*chip, c), sibling)
                cp.start()
                passed.append(cp)
        for a in range(n):
            copy(a, 0, sibling, me).wait_recv()
            for j, chip in enumerate(chips):
                copy(a, 4 + j, (*chip, 1 - c), me).wait_recv()
        for cp in first + passed:
            cp.wait_send()
        for cp in mine:
            cp.wait()

    return pl.pallas_call(
        body, name="gather_shards",
        out_shape=[jax.ShapeDtypeStruct((N_DEV,) + s.shape, s.dtype) for s in shards],
        in_specs=[pl.BlockSpec(memory_space=pl.ANY)] * n, out_specs=[pl.BlockSpec(memory_space=pl.ANY)] * n,
        scratch_shapes=[pltpu.SemaphoreType.DMA((7 * n,)), pltpu.SemaphoreType.DMA((7 * n,)),
                        pltpu.SemaphoreType.DMA((n,))],
    )(*shards)


def ada_modulation(c_all, w_ada_loc, b_ada_blocks):
    cols = w_ada_loc.shape[1]

    def body(c_ref, w_ref, b_ref, out_ref, send_sems, recv_sems):
        me = my_position()
        mi = flat_index(me)
        cv = c_ref[...]
        res = hdot(cv * sigmoid(cv), w_ref[...]) + b_ref[pl.ds(mi, 1), :]
        out_ref[mi] = res
        sends = []
        for k in range(1, N_DEV):
            cp = pltpu.make_async_remote_copy(
                src_ref=out_ref.at[mi], dst_ref=out_ref.at[mi], send_sem=send_sems.at[k - 1],
                recv_sem=recv_sems.at[k - 1], device_id=flip(me, k), device_id_type=MESH_IDS)
            cp.start()
            sends.append(cp)
        for k in range(1, N_DEV):
            pi = flat_index(flip(me, k))
            pltpu.make_async_remote_copy(
                src_ref=out_ref.at[pi], dst_ref=out_ref.at[pi], send_sem=send_sems.at[k - 1],
                recv_sem=recv_sems.at[k - 1], device_id=flip(me, k), device_id_type=MESH_IDS).wait_recv()
        for cp in sends:
            cp.wait_send()

    return pl.pallas_call(
        body, name="ada_modulation",
        out_shape=jax.ShapeDtypeStruct((N_DEV, N_DEV, cols), F32),
        in_specs=[pl.BlockSpec(memory_space=pltpu.VMEM)] * 3, out_specs=pl.BlockSpec(memory_space=pltpu.VMEM),
        scratch_shapes=[pltpu.SemaphoreType.DMA((7,)), pltpu.SemaphoreType.DMA((7,))],
    )(c_all, w_ada_loc, b_ada_blocks)


def fwd_in_tile(step0, tile0, x, mod, w_in_p):
    xhat, _ = layer_norm_stats(x)
    h = xhat * (1.0 + mod[1:2]) + mod[0:1]
    return (mm(h, w_in_p),)


def fwd_in_gather(x, mod, w_in_p, shards):
    n = len(shards)
    n_rows = x.shape[0]
    ts = ROW_TILE
    n_tiles = n_rows // ts

    def body(x_ref, mod_ref, w_ref, *rest):
        s_refs = rest[:n]
        proj_ref, out_refs = rest[n], rest[n + 1:2 * n + 1]
        send_sems, recv_sems, local_sems = rest[2 * n + 1:]
        g = pl.program_id(0)
        me = my_position()
        mi = flat_index(me)

        def copies(k, slot):
            return [pltpu.make_async_remote_copy(
                src_ref=s_refs[a], dst_ref=out_refs[a].at[slot], send_sem=send_sems.at[7 * a + k - 1],
                recv_sem=recv_sems.at[7 * a + k - 1], device_id=flip(me, k), device_id_type=MESH_IDS)
                for a in range(n)]

        local = [pltpu.make_async_copy(s_refs[a], out_refs[a].at[mi], local_sems.at[a]) for a in range(n)]

        @pl.when(g == 0)
        def _():
            for cp in local:
                cp.start()
            for k in range(1, N_DEV):
                for cp in copies(k, mi):
                    cp.start()

        proj_ref[...] = fwd_in_tile(None, None, x_ref[...], mod_ref[...], w_ref[...])[0].astype(BF16)

        @pl.when(g == n_tiles - 1)
        def _():
            for k in range(1, N_DEV):
                for cp in copies(k, flat_index(flip(me, k))):
                    cp.wait_recv()
            for k in range(1, N_DEV):
                for cp in copies(k, mi):
                    cp.wait_send()
            for cp in local:
                cp.wait()

    hbm = pl.BlockSpec(memory_space=pl.ANY)
    const = pl.BlockSpec(memory_space=pltpu.VMEM)
    return pl.pallas_call(
        body, name="fwd_in_gather", grid=(n_tiles,),
        in_specs=[pl.BlockSpec((ts, D_MODEL), lambda g: (g, 0)), const, const] + [hbm] * n,
        out_specs=[pl.BlockSpec((ts, P_WIDTH), lambda g: (g, 0))] + [hbm] * n,
        out_shape=[jax.ShapeDtypeStruct((n_rows, P_WIDTH), BF16)]
        + [jax.ShapeDtypeStruct((N_DEV,) + s.shape, s.dtype) for s in shards],
        scratch_shapes=[pltpu.SemaphoreType.DMA((7 * n,)), pltpu.SemaphoreType.DMA((7 * n,)),
                        pltpu.SemaphoreType.DMA((n,))],
        compiler_params=pltpu.CompilerParams(dimension_semantics=("arbitrary",), vmem_limit_bytes=VMEM_LIMIT),
    )(x, mod, w_in_p, *shards)


def rms_norm_fwd(x, g):
    r = lax.rsqrt(rowmean(x * x) + RMS_EPS)
    xh = x * r
    return xh * g, xh, r


def key_rope_mask(shape):
    return (lane_iota(shape) >= NOPE).astype(F32)


def mla_prep_tile(step0, tile0, q_c, kv_c, kr, krr, cos, sin, gq, gkv, wq, wqr, wkn, wv):
    qn, _, _ = rms_norm_fwd(q_c, gq)
    kvn, _, _ = rms_norm_fwd(kv_c, gkv)
    q = (mm(qn, wq) * tile_lanes(cos, HEADS) + mm(qn, wqr) * tile_lanes(sin, HEADS)) * Q_PRESCALE
    kpe = kr * (cos * key_rope_mask(cos.shape)) + krr * sin
    k = mm(kvn, wkn) + tile_lanes(kpe, HEADS)
    v = mm(kvn, wv)
    return q, k, v


def rwkv_prep_core(tile0, r0, k0, v0, l0, hr, hk, hv, hl, mu_r, mu_k, mu_v, mu_l, w0, a0, k_k, k_a,
                   w_dec, w_iclr, tril, same, bd):
    def shifted(x, halo, mu):
        row0 = jnp.where(tile0, 0.0, halo[HALO_ROWS - 1:HALO_ROWS, :])
        prev = shift_rows_down(x, row0)
        return x + (prev - x) * mu, prev

    ur, pr = shifted(r0, hr, mu_r)
    uk, pk = shifted(k0, hk, mu_k)
    uv, pv = shifted(v0, hv, mu_v)
    ul, plo = shifted(l0, hl, mu_l)
    th = jnp.tanh(ul)
    sg = sigmoid(w0 + mm(th, w_dec))
    lw = -DECAY_SCALE * sg
    a_ic = sigmoid(a0 + mm(ul, w_iclr))
    kkraw = uk * k_k
    nrm_raw = jnp.sqrt(head_sum(kkraw * kkraw, bd))
    nrm = jnp.maximum(nrm_raw, 1e-12)
    kk = kkraw / nrm
    k2 = uk * (1.0 + (a_ic - 1.0) * k_a)
    lc = ones_dot(tril, lw, 3)
    lcl = ones_dot(same, lw, 3)
    return dict(ur=ur, uk=uk, uv=uv, ul=ul, pr=pr, pk=pk, pv=pv, pl=plo, th=th, sg=sg, lw=lw, a_ic=a_ic,
                kkraw=kkraw, nrm_raw=nrm_raw, nrm=nrm, kk=kk, k2=k2, lc=lc, lcl=lcl)


def rwkv_prep_tile(step0, tile0, r0, k0, v0, l0, hr, hk, hv, hl, *consts):
    f = rwkv_prep_core(tile0, r0, k0, v0, l0, hr, hk, hv, hl, *consts)
    lc, lw = f["lc"], f["lw"]
    e_neg = jnp.exp(-lc)
    rt = f["ur"] * jnp.exp(lc)
    at = -f["kk"] * jnp.exp(lc - lw)
    bt = f["kk"] * f["a_ic"] * e_neg
    kt = f["k2"] * e_neg
    return rt, at, bt, kt, jnp.exp(f["lcl"]), f["uv"], f["ur"], f["k2"]


def wkv_masks():
    lane = lane_iota((1, PAIR))
    m_lo = (lane < HEAD).astype(F32)
    r2 = row_iota((PAIR, PAIR))
    c2 = lane_iota((PAIR, PAIR))
    bd = ((r2 < HEAD) == (c2 < HEAD)).astype(F32)
    eye2 = (r2 == c2).astype(F32)
    eye = (row_iota((CHUNK, CHUNK)) == lane_iota((CHUNK, CHUNK))).astype(F32)
    t_idx = row_iota((4 * CHUNK, PAIR)) % CHUNK
    s_idx = lane_iota((4 * CHUNK, PAIR)) % CHUNK
    keep = s_idx < t_idx + (row_iota((4 * CHUNK, PAIR)) >= 2 * CHUNK).astype(jnp.int32)
    return (m_lo, 1.0 - m_lo), keep, eye, bd, eye2


def rows(*parts):
    return jnp.concatenate(parts, axis=0)


def lanes(*parts):
    return jnp.concatenate(parts, axis=1)


def head_rows(x, ms):
    return rows(x * ms[0], x * ms[1])


def wkv_score_stack(at, rt, ms):
    return rows(head_rows(at, ms), head_rows(rt, ms))


def wkv_chunks_pre(chunks, masks):
    ms, keep, eye, bd, eye2 = masks
    n = len(chunks)
    at, bt, kt, rt, v, cl = (list(t) for t in zip(*chunks))
    scores = [jnp.where(keep, mm_nt(wkv_score_stack(a, r, ms), rows(b, k)), 0.0)
              for a, r, b, k in zip(at, rt, bt, kt)]
    q = CHUNK
    aab = [s[h * q:(h + 1) * q, :q] for s in scores for h in range(2)]
    tinv = [eye + a for a in aab]
    power = [mm(a, a) for a in aab]
    for _ in range(5):
        both = [mm(rows(t, p), p) for t, p in zip(tinv, power)]
        tinv = [t + x[:q] for t, x in zip(tinv, both)]
        power = [x[q:] for x in both]
    pair = lambda c, row0, col0: lanes(scores[c][row0:row0 + q, col0:col0 + q],
                                       scores[c][row0 + q:row0 + 2 * q, col0:col0 + q])
    tinv_p = [lanes(tinv[2 * c], tinv[2 * c + 1]) for c in range(n)]
    aak_p = [pair(c, 0, q) for c in range(n)]
    prb_p = [pair(c, 2 * q, 0) for c in range(n)]
    prk_p = [pair(c, 2 * q, q) for c in range(n)]
    v_rows = [head_rows(x, ms) for x in v]
    wy = [mm(rows(a, p), x) for a, p, x in zip(aak_p, prk_p, v_rows)]
    w = [x[:q] for x in wy]
    yh2 = [x[q:] for x in wy]
    aw = [mm(t, lanes(head_rows(a, ms), head_rows(w_, ms))) for t, a, w_ in zip(tinv_p, at, w)]
    ah = [x[:, :PAIR] for x in aw]
    wh = [x[:, PAIR:] for x in aw]
    ry = [mm(p, lanes(head_rows(a, ms), head_rows(w_, ms))) for p, a, w_ in zip(prb_p, ah, wh)]
    rh = [r + x[:, :PAIR] for r, x in zip(rt, ry)]
    yh = [x[:, PAIR:] + y for x, y in zip(ry, yh2)]
    bc = [b * c_ for b, c_ in zip(bt, cl)]
    kc = [k * c_ for k, c_ in zip(kt, cl)]
    gh = [mm_tn(b, lanes(a, w_)) for b, a, w_ in zip(bc, ah, wh)]
    g = [eye2 * c_ + bd * x[:, :PAIR] for c_, x in zip(cl, gh)]
    h = [bd * (x[:, PAIR:] + mm_tn(k, v_)) for x, k, v_ in zip(gh, kc, v)]
    as_bf16 = lambda xs: [x.astype(BF16) for x in xs]
    saved = (as_bf16(tinv_p), as_bf16(aak_p), as_bf16(prb_p), as_bf16(prk_p), as_bf16(ah), wh)
    return g, h, rh, yh, saved


def wkv_chunks_grad(chunks, saved, m0, dy, dm1, masks):
    ms, keep, eye, bd, eye2 = masks
    n = len(chunks)
    q = CHUNK
    at, bt, kt, rt, v, cl = (list(t) for t in zip(*chunks))
    tinv_p, aak_p, prb_p, prk_p, ah, wh = (list(t) for t in zip(*saved))
    head_stack = lambda p: rows(p[:, :q], p[:, q:])
    bc = [b * c_ for b, c_ in zip(bt, cl)]
    kc = [k * c_ for k, c_ in zip(kt, cl)]
    u = [mm(a, m) + w for a, m, w in zip(ah, m0, wh)]
    dm1 = [d * bd for d in dm1]
    from_state = [mm(rows(b, k), d) for b, k, d in zip(bc, kc, dm1)]
    dy_rows = [head_rows(d, ms) for d in dy]
    from_out = [mm_tn(lanes(head_stack(pb), head_stack(pk)), d) for pb, pk, d in zip(prb_p, prk_p, dy_rows)]
    du = [a[:q] + b[:q] for a, b in zip(from_state, from_out)]
    dv = [a[q:] + b[q:] for a, b in zip(from_state, from_out)]
    dz = [mm_tn(head_stack(t), head_rows(d, ms)) for t, d in zip(tinv_p, du)]
    dz_rows = [head_rows(d, ms) for d in dz]
    dv = [a + mm_tn(head_stack(k), d) for a, k, d in zip(dv, aak_p, dz_rows)]
    by_m0 = [mm_nt(rows(d, z), m) for d, z, m in zip(dy, dz, m0)]
    uv = [rows(x, y) for x, y in zip(u, v)]
    by_dm1 = [mm_nt(x, d) for x, d in zip(uv, dm1)]
    udm = [x[:q] for x in by_dm1]
    vdm = [x[q:] for x in by_dm1]
    dscores = [jnp.where(keep, mm_nt(rows(z, d), x), 0.0) for z, d, x in zip(dz_rows, dy_rows, uv)]
    to_ar = [mm(d, rows(b, k)) for d, b, k in zip(dscores, bt, kt)]
    to_bk = [mm_tn(d, wkv_score_stack(a, r, ms)) for d, a, r in zip(dscores, at, rt)]
    ones = jnp.ones((8, PAIR), F32)
    upper = (lane_iota((CHUNK, CHUNK)) >= row_iota((CHUNK, CHUNK))).astype(F32)
    out = []
    for c in range(n):
        e = to_ar[c]
        dat_c = by_m0[c][q:] + e[:q] * ms[0] + e[q:2 * q] * ms[1]
        drt_c = by_m0[c][:q] + e[2 * q:3 * q] * ms[0] + e[3 * q:] * ms[1]
        dbt_c = udm[c] * cl[c] + to_bk[c][:q]
        dkt_c = vdm[c] * cl[c] + to_bk[c][q:]
        dlcl = ones_dot_nt(ones, dm1[c] * m0[c], 3)[0:1, :] * cl[c] + colsum(bc[c] * udm[c] + kc[c] * vdm[c])
        g = drt_c * rt[c] - dbt_c * bt[c] - dkt_c * kt[c] + dat_c * at[c]
        dlw = ones_dot(upper, g, 3) - dat_c * at[c] + dlcl
        out.append((dat_c, dbt_c, dkt_c, drt_c, dv[c], dlw))
    return out


def wkv_forward(at, bt, kt, rt, v, clf):
    n_rows = at.shape[0]
    cps = WKV_CHUNKS_PER_STEP
    rb = cps * CHUNK
    n_steps = n_rows // rb

    def body(a_ref, b_ref, k_ref, r_ref, v_ref, c_ref, y_ref, m0_ref, g_ref, rh_ref, *rest):
        saved_refs, m_scr = rest[:6], rest[6]

        @pl.when(pl.program_id(1) == 0)
        def _():
            m_scr[...] = jnp.zeros_like(m_scr)

        masks = wkv_masks()
        chunks = []
        for cc in range(cps):
            sl = slice(cc * CHUNK, (cc + 1) * CHUNK)
            chunks.append((a_ref[sl, :], b_ref[sl, :], k_ref[sl, :], r_ref[sl, :], v_ref[sl, :],
                           c_ref[cc * CHUNK:cc * CHUNK + 1, :]))
        gs, hs, rhs, yhs, saved = wkv_chunks_pre(chunks, masks)
        for ref, per_chunk in zip(saved_refs, saved):
            for cc, val in enumerate(per_chunk):
                ref[cc * CHUNK:(cc + 1) * CHUNK, :] = val
        m = m_scr[...]
        for cc, (g, h, rh, yh) in enumerate(zip(gs, hs, rhs, yhs)):
            sl = slice(cc * CHUNK, (cc + 1) * CHUNK)
            m0_ref[0, cc] = m
            g_ref[0, cc] = g
            rh_ref[sl, :] = rh
            y_ref[sl, :] = hdot(rh, m) + yh
            m = hdot(g, m) + h
        m_scr[...] = m

    blk = pl.BlockSpec((rb, PAIR), lambda p, s: (s, p))
    state_blk = pl.BlockSpec((1, cps, PAIR, PAIR), lambda p, s: (p, s, 0, 0))
    state_shape = jax.ShapeDtypeStruct((WIDTH // PAIR, n_rows // CHUNK, PAIR, PAIR), F32)
    rows_f32 = jax.ShapeDtypeStruct((n_rows, WIDTH), F32)
    rows_bf16 = jax.ShapeDtypeStruct((n_rows, WIDTH), BF16)
    return pl.pallas_call(
        body, name="wkv_forward", grid=(WIDTH // PAIR, n_steps),
        in_specs=[blk] * 6,
        out_specs=[blk, state_blk, state_blk, blk] + [blk] * 6,
        out_shape=[rows_f32, state_shape, state_shape, rows_f32] + [rows_bf16] * 5 + [rows_f32],
        scratch_shapes=[pltpu.VMEM((PAIR, PAIR), F32)],
        compiler_params=pltpu.CompilerParams(dimension_semantics=("arbitrary", "arbitrary"),
                                             vmem_limit_bytes=VMEM_LIMIT),
    )(at, bt, kt, rt, v, clf)


def wkv_backward(at, bt, kt, rt, v, clf, m0s, gs, rh, saved, dy):
    n_rows = at.shape[0]
    cps = WKV_CHUNKS_PER_STEP
    rb = cps * CHUNK
    n_steps = n_rows // rb

    def body(a_ref, b_ref, k_ref, r_ref, v_ref, c_ref, m0_ref, g_ref, rh_ref, *rest):
        saved_refs, dy_ref = rest[:6], rest[6]
        da_ref, db_ref, dk_ref, dr_ref, dv_ref, dlw_ref, dm_scr = rest[7:]

        @pl.when(pl.program_id(1) == 0)
        def _():
            dm_scr[...] = jnp.zeros_like(dm_scr)

        masks = wkv_masks()
        bd = masks[3]
        dm = dm_scr[...]
        dm1 = [None] * cps
        for cc in reversed(range(cps)):
            sl = slice(cc * CHUNK, (cc + 1) * CHUNK)
            dm1[cc] = dm
            dm = bd * (hdot_tn(g_ref[0, cc], dm) + hdot_tn(rh_ref[sl, :], dy_ref[sl, :]))
        dm_scr[...] = dm
        chunks, kept, m0, dys = [], [], [], []
        for cc in range(cps):
            sl = slice(cc * CHUNK, (cc + 1) * CHUNK)
            chunks.append((a_ref[sl, :], b_ref[sl, :], k_ref[sl, :], r_ref[sl, :], v_ref[sl, :],
                           c_ref[cc * CHUNK:cc * CHUNK + 1, :]))
            kept.append(tuple(ref[sl, :] for ref in saved_refs))
            m0.append(m0_ref[0, cc])
            dys.append(dy_ref[sl, :])
        grads = wkv_chunks_grad(chunks, kept, m0, dys, dm1, masks)
        for cc, (dat, dbt, dkt, drt, dv, dlw) in enumerate(grads):
            sl = slice(cc * CHUNK, (cc + 1) * CHUNK)
            da_ref[sl, :] = dat
            db_ref[sl, :] = dbt
            dk_ref[sl, :] = dkt
            dr_ref[sl, :] = drt
            dv_ref[sl, :] = dv
            dlw_ref[sl, :] = dlw

    blk = pl.BlockSpec((rb, PAIR), lambda p, s: (n_steps - 1 - s, p))
    state_blk = pl.BlockSpec((1, cps, PAIR, PAIR), lambda p, s: (p, n_steps - 1 - s, 0, 0))
    return pl.pallas_call(
        body, name="wkv_backward", grid=(WIDTH // PAIR, n_steps),
        in_specs=[blk] * 6 + [state_blk, state_blk, blk] + [blk] * 6 + [blk],
        out_specs=[blk] * 6,
        out_shape=[jax.ShapeDtypeStruct((n_rows, WIDTH), F32)] * 6,
        scratch_shapes=[pltpu.VMEM((PAIR, PAIR), F32)],
        compiler_params=pltpu.CompilerParams(dimension_semantics=("arbitrary", "arbitrary"),
                                             vmem_limit_bytes=VMEM_LIMIT),
    )(at, bt, kt, rt, v, clf, m0s, gs, rh, *saved, dy)


def visible(q_row0, k_row0, shape):
    qc = (q_row0 + row_iota(shape)) // CHUNK
    kc = (k_row0 + lane_iota(shape)) // CHUNK
    return kc <= qc


def attention_forward(q, k, v):
    n_rows = q.shape[0]
    tq, tk = ATTN_FWD_TILES
    n_q = n_rows // tq
    assert tk % tq == 0

    def body(q_ref, k_ref, v_ref, o_ref, lse_ref):
        i = pl.program_id(1)
        lane = lane_iota((tq, LANE))
        heads = [slice(0, LANE), slice(LANE, 2 * LANE)]
        qs = [q_ref[:, cols] for cols in heads]

        def step(j, carry, size, masked):
            rows = pl.ds(pl.multiple_of(j * size, size), size)
            ss = [mm_nt(qh, k_ref[rows, cols]) for qh, cols in zip(qs, heads)]
            if masked:
                vis = visible(i * tq, j * size, ss[0].shape)
                ss = [jnp.where(vis, s, -jnp.inf) for s in ss]
            ps, stats = [], []
            for s, (m, l, _) in zip(ss, carry):
                m_new = jnp.maximum(m, jnp.max(s, axis=-1, keepdims=True))
                p = jnp.exp2(s - m_new)
                alpha = jnp.exp2(m - m_new)
                ps.append(p)
                stats.append((m_new, alpha, alpha * l + jnp.sum(p, axis=-1, keepdims=True)))
            pvs = [mm(p, v_ref[rows, cols]) for p, cols in zip(ps, heads)]
            return tuple((m_new, l, alpha * acc + pv)
                         for (m_new, alpha, l), (_, _, acc), pv in zip(stats, carry, pvs))

        carry = tuple((jnp.full((tq, 1), -jnp.inf, F32), jnp.zeros((tq, 1), F32), jnp.zeros((tq, LANE), F32))
                      for _ in heads)
        n_full = (i * tq) // tk
        carry = lax.fori_loop(0, n_full, functools.partial(step, size=tk, masked=False), carry)
        (m0, l0, acc0), (m1, l1, acc1) = step(n_full, carry, size=tk, masked=True)
        o_ref[...] = acc0 / l0 + acc1 / l1
        lse_ref[...] = jnp.where(lane >= HEAD, m1 + jnp.log2(l1), m0 + jnp.log2(l0))

    return pl.pallas_call(
        body, name="attention_forward", grid=(HEADS // 2, n_q),
        in_specs=[pl.BlockSpec((tq, 2 * LANE), lambda p, i: (i, p)),
                  pl.BlockSpec((n_rows, 2 * LANE), lambda p, i: (0, p)),
                  pl.BlockSpec((n_rows, 2 * LANE), lambda p, i: (0, p))],
        out_specs=[pl.BlockSpec((tq, LANE), lambda p, i: (i, p))] * 2,
        out_shape=[jax.ShapeDtypeStruct((n_rows, WIDTH), F32)] * 2,
        compiler_params=pltpu.CompilerParams(dimension_semantics=("arbitrary", "arbitrary"),
                                             vmem_limit_bytes=VMEM_LIMIT),
    )(q, k, v)


def attention_backward(q, k, v, o, do, lse):
    n_rows = q.shape[0]
    tq, tk = ATTN_BWD_TILES
    n_q = n_rows // tq
    n_masked = max(1, tk // tq)

    def body(q_ref, k_ref, v_ref, o_ref, do_ref, lse_ref, dq_ref, dk_ref, dv_ref):
        j = pl.program_id(1)

        @pl.when(j == 0)
        def _():
            dq_ref[...] = jnp.zeros_like(dq_ref)

        lane = lane_iota((tq, LANE))
        heads = [slice(0, LANE), slice(LANE, 2 * LANE)]
        ks = [k_ref[:, cols] for cols in heads]
        vs = [v_ref[:, cols] for cols in heads]
        head_lanes = [(lane < HEAD).astype(F32), (lane >= HEAD).astype(F32)]

        def step(i, carry, masked):
            rows = pl.ds(pl.multiple_of(i * tq, tq), tq)
            qs = [q_ref[rows, cols] for cols in heads]
            dout = do_ref[rows, :]
            dout_o = dout * o_ref[rows, :]
            lse_t = lse_ref[rows, :]
            ss = [mm_nt(qh, kh) for qh, kh in zip(qs, ks)]
            dps = [mm_nt(dout, vh) for vh in vs]
            ps, dss = [], []
            for hh in range(2):
                delta = jnp.sum(dout_o * head_lanes[hh], axis=-1, keepdims=True)
                lse_h = jnp.sum(jnp.where(lane == hh * HEAD, lse_t, 0.0), axis=-1, keepdims=True)
                p = jnp.exp2(ss[hh] - lse_h)
                if masked:
                    p = jnp.where(visible(i * tq, j * tk, p.shape), p, 0.0)
                ps.append(p)
                dss.append(p * (dps[hh] - delta))
            dvs = [mm_tn(p, dout) for p in ps]
            dqs = [mm(ds, kh) for ds, kh in zip(dss, ks)]
            dks = [mm_tn(ds, qh) for ds, qh in zip(dss, qs)]
            for cols, dq in zip(heads, dqs):
                dq_ref[rows, cols] += dq * ATTN_SCALE
            return tuple((dk + a, dv + b) for (dk, dv), a, b in zip(carry, dks, dvs))

        carry = tuple((jnp.zeros((tk, LANE), F32), jnp.zeros((tk, LANE), F32)) for _ in heads)
        i_first = (j * tk) // tq
        for extra in range(n_masked):
            carry = step(i_first + extra, carry, masked=True)
        carry = lax.fori_loop(i_first + n_masked, n_q, functools.partial(step, masked=False), carry)
        for cols, (dk, dv) in zip(heads, carry):
            dk_ref[:, cols] = dk * (1.0 / LOG2_E)
            dv_ref[:, cols] = dv

    full = lambda w: pl.BlockSpec((n_rows, w), lambda p, j: (0, p))
    blk = pl.BlockSpec((tk, 2 * LANE), lambda p, j: (j, p))
    return pl.pallas_call(
        body, name="attention_backward", grid=(HEADS // 2, n_rows // tk),
        in_specs=[full(2 * LANE), blk, blk, full(LANE), full(LANE), full(LANE)],
        out_specs=[full(2 * LANE), blk, blk],
        out_shape=[jax.ShapeDtypeStruct((n_rows, HEADS * LANE), F32)] * 3,
        compiler_params=pltpu.CompilerParams(dimension_semantics=("arbitrary", "arbitrary"),
                                             vmem_limit_bytes=VMEM_LIMIT),
    )(q, k, v, o, do, lse)


def tail_tile(step0, tile0, x, tgt, ma, mb, gpa, gpb, ya, y, ur, k2, uv,
              mod, wpa, wpb, wout, gn_g, gn_b, r_k, post_g, post_b, bd):
    gate = mod[2:3]
    inv = 1.0 / HEAD
    yc = y - head_sum(y, bd) * inv
    rs = lax.rsqrt(head_sum(yc * yc, bd) * inv + GN_EPS)
    yn = yc * rs
    yb = yn * gn_g + gn_b + head_sum(ur * k2 * r_k, bd) * uv
    sga, sgb = sigmoid(gpa), sigmoid(gpb)
    sila, silb = gpa * sga, gpb * sgb
    ga, gb = ya * sila, yb * silb
    pa, pb = mm(ga, wpa), mm(gb, wpb)
    sa, sb = sigmoid(ma), sigmoid(mb)
    merged = sa * pa + sb * pb
    sub = mm(merged, wout)
    z = ALPHA * x + (1.0 + gate) * sub
    zhat, rstd = layer_norm_stats(z)
    err = zhat * post_g + post_b - tgt
    loss = 0.5 * jnp.sum(rowmean(err * err), axis=0, keepdims=True) + jnp.zeros((1, LANE), F32)
    dout = err * (1.0 / D_MODEL)
    dpost_g = colsum(dout * zhat)
    dpost_b = colsum(dout)
    dz = layer_norm_bwd(dout * post_g, zhat, rstd)
    dgate = colsum(dz * sub)
    dsub = dz * (1.0 + gate)
    dwout = mm_tn(merged, dsub)
    dmerged = mm_nt(dsub, wout)
    dpa, dpb = dmerged * sa, dmerged * sb
    dma = dmerged * pa * sa * (1.0 - sa)
    dmb = dmerged * pb * sb * (1.0 - sb)
    dwpa = mm_tn(ga, dpa)
    dwpb = mm_tn(gb, dpb)
    dga = mm_nt(dpa, wpa)
    dgb = mm_nt(dpb, wpb)
    dya = dga * sila
    dgpa = dga * ya * (sga * (1.0 + gpa * (1.0 - sga)))
    dyb = dgb * silb
    dgpb = dgb * yb * (sgb * (1.0 + gpb * (1.0 - sgb)))
    dgn_g = colsum(dyb * yn)
    dgn_b = colsum(dyb)
    dyn = dyb * gn_g
    dy = rs * (dyn - head_sum(dyn, bd) * inv - yn * head_sum(dyn * yn, bd) * inv)
    return (dz, dma, dmb, dgpa, dgpb, dya, dy, dyb,
            loss, dwout, dwpa, dwpb, dgn_g, dgn_b, dpost_g, dpost_b, dgate)


def mla_prep_bwd_tile(step0, tile0, q_c, kv_c, cos, sin, dq, dk, dv, gq, gkv, wq, wqr, wkn, wv):
    qn, qh, rq = rms_norm_fwd(q_c, gq)
    kvn, kvh, rkv = rms_norm_fwd(kv_c, gkv)
    dqc = dq * tile_lanes(cos, HEADS)
    dqs = dq * tile_lanes(sin, HEADS)
    dqn = mm_nt(dqc, wq) + mm_nt(dqs, wqr)
    dkvn = mm_nt(dk, wkn) + mm_nt(dv, wv)
    dkpe = dk[:, 0:LANE]
    for h in range(1, HEADS):
        dkpe = dkpe + dk[:, h * LANE:(h + 1) * LANE]
    dkr = dkpe * (cos * key_rope_mask(cos.shape))
    dkrr = dkpe * sin

    def rms_bwd(dyv, xh, r, g):
        dyg = dyv * g
        return r * (dyg - xh * rowmean(dyg * xh)), colsum(dyv * xh)

    dq_c, dgq = rms_bwd(dqn, qh, rq, gq)
    dkv_c, dgkv = rms_bwd(dkvn, kvh, rkv, gkv)
    return (dq_c, dkv_c, dkr, dkrr,
            mm_tn(qn, dqc), mm_tn(qn, dqs), mm_tn(kvn, dk), mm_tn(kvn, dv), dgq, dgkv)


def rwkv_prep_bwd_tile(step0, tile0, r0, k0, v0, l0, drt, dat, dbt, dkt, dvv, dlw, dyb, hr, hk, hv, hl,
                       mu_r, mu_k, mu_v, mu_l, w0, a0, k_k, k_a, w_dec, w_iclr, tril, same, bd, r_k,
                       cr, ck, cv, cl_):
    f = rwkv_prep_core(tile0, r0, k0, v0, l0, hr, hk, hv, hl, mu_r, mu_k, mu_v, mu_l, w0, a0, k_k, k_a,
                       w_dec, w_iclr, tril, same, bd)
    ur, uk, uv, ul, kk, k2, a_ic, sg, th = (f[n] for n in ("ur", "uk", "uv", "ul", "kk", "k2", "a_ic", "sg", "th"))
    lc, lw = f["lc"], f["lw"]
    e_neg = jnp.exp(-lc)
    dur = drt * jnp.exp(lc)
    da = dat * jnp.exp(lc - lw)
    db = dbt * e_neg
    dk2 = dkt * e_neg
    s = head_sum(ur * k2 * r_k, bd)
    duv = dvv + dyb * s
    ds = head_sum(dyb * uv, bd)
    dur = dur + ds * k2 * r_k
    dk2 = dk2 + ds * ur * r_k
    dr_k = colsum(ds * ur * k2)
    dkk = db * a_ic - da
    da_ic = db * kk + dk2 * uk * k_a
    duk = dk2 * (1.0 + (a_ic - 1.0) * k_a)
    dk_a = colsum(dk2 * uk * (a_ic - 1.0))
    dkkraw = jnp.where(f["nrm_raw"] > 1e-12, (dkk - kk * head_sum(dkk * kk, bd)) / f["nrm"], dkk * 1e12)
    duk = duk + dkkraw * k_k
    dk_k = colsum(dkkraw * uk)
    dai = da_ic * a_ic * (1.0 - a_ic)
    dd = dlw * (-DECAY_SCALE) * sg * (1.0 - sg)
    dul = mm_nt(dai, w_iclr) + mm_nt(dd, w_dec) * (1.0 - th * th)

    def unshift(du, x, prev, mu, carry_row):
        nxt = shift_rows_up(du, carry_row)
        return du * (1.0 - mu) + nxt * mu, colsum(du * (prev - x)), du[0:1, :]

    dr0, dmu_r, ncr = unshift(dur, r0, f["pr"], mu_r, cr)
    dk0, dmu_k, nck = unshift(duk, k0, f["pk"], mu_k, ck)
    dv0, dmu_v, ncv = unshift(duv, v0, f["pv"], mu_v, cv)
    dl0, dmu_l, ncl = unshift(dul, l0, f["pl"], mu_l, cl_)
    return (dr0, dk0, dv0, dl0,
            dmu_r, dmu_k, dmu_v, dmu_l, colsum(dd), colsum(dai), dk_k, dk_a, dr_k, mm_tn(th, dd), mm_tn(ul, dai),
            ncr, nck, ncv, ncl)


def in_backward(x, dz, pieces, mod, w_in_p, unrot, others):
    n_rows = x.shape[0]
    ts = ROW_TILE
    n_tiles = n_rows // ts
    n_p = len(pieces)
    n_o = len(others)
    shard_cols = IN_WIDTH // N_DEV

    def body(*refs):
        x_ref, dz_ref = refs[:2]
        p_refs = refs[2:2 + n_p]
        mod_ref, w_ref, unrot_ref = refs[2 + n_p:5 + n_p]
        g_refs = refs[5 + n_p:5 + n_p + n_o]
        dx_ref, ht_ref, blocks_ref, dshift_ref, dscale_ref = refs[5 + n_p + n_o:10 + n_p + n_o]
        rg_refs = refs[10 + n_p + n_o:10 + n_p + 2 * n_o]
        send_sems, recv_sems, local_sems = refs[10 + n_p + 2 * n_o:]
        step0 = pl.program_id(0) == 0
        me = my_position()
        mi = flat_index(me)

        def copies(k, src_index, dst_index):
            return [pltpu.make_async_remote_copy(
                src_ref=g_refs[a].at[src_index], dst_ref=rg_refs[a].at[dst_index],
                send_sem=send_sems.at[7 * a + k - 1], recv_sem=recv_sems.at[7 * a + k - 1],
                device_id=flip(me, k), device_id_type=MESH_IDS) for a in range(n_o)]

        local = [pltpu.make_async_copy(g_refs[a].at[mi], rg_refs[a].at[mi], local_sems.at[a]) for a in range(n_o)]

        @pl.when(step0)
        def _():
            for cp in local:
                cp.start()
            for k in range(1, N_DEV):
                for cp in copies(k, flat_index(flip(me, k)), mi):
                    cp.start()

        dma, dmb, dr0, dk0, dv0, dgpa, dgpb, dq_c, dkv_c, dkr, dkrr, dl0 = (r[...] for r in p_refs)
        dproj = jnp.concatenate([dma, dmb, dr0, dk0, dv0, dgpa, dgpb, dq_c, dkv_c, dkr, dkrr, dl0], axis=1)
        dh = mm_nt(dproj, w_ref[...])
        xhat, rstd = layer_norm_stats(x_ref[...])
        scale1 = 1.0 + mod_ref[1:2, :]
        dx_ref[...] = layer_norm_bwd(dh * scale1, xhat, rstd) + ALPHA * dz_ref[...]
        ht_ref[...] = jnp.transpose(xhat * scale1 + mod_ref[0:1, :]).astype(BF16)
        dkrope = (dkr.astype(F32) + mm(dkrr, unrot_ref[...]))[:, NOPE:QK_DIM]
        natural = jnp.concatenate(
            [dq_c.astype(F32), dkv_c.astype(F32), dkrope]
            + [p.astype(F32) for p in (dgpa, dr0, dk0, dv0, dl0, dgpb, dma, dmb)], axis=1)
        for j in range(N_DEV):
            blocks_ref[j] = natural[:, j * shard_cols:(j + 1) * shard_cols].astype(BF16)
        for ref, val in ((dshift_ref, colsum(dh)), (dscale_ref, colsum(dh * xhat))):
            @pl.when(step0)
            def _(ref=ref, val=val):
                ref[...] = val

            @pl.when(jnp.logical_not(step0))
            def _(ref=ref, val=val):
                ref[...] += val

        @pl.when(pl.program_id(0) == n_tiles - 1)
        def _():
            for k in range(1, N_DEV):
                pi = flat_index(flip(me, k))
                for cp in copies(k, pi, pi):
                    cp.wait_recv()
            for k in range(1, N_DEV):
                for cp in copies(k, flat_index(flip(me, k)), mi):
                    cp.wait_send()
            for cp in local:
                cp.wait()

    row = lambda w: pl.BlockSpec((ts, w), lambda i: (i, 0))
    const = pl.BlockSpec(memory_space=pltpu.VMEM)
    hbm = pl.BlockSpec(memory_space=pl.ANY)
    vec = pl.BlockSpec((1, D_MODEL), lambda i: (0, 0))
    return pl.pallas_call(
        body, name="in_backward", grid=(n_tiles,),
        in_specs=[row(D_MODEL), row(D_MODEL)] + [row(p.shape[1]) for p in pieces] + [const] * 3 + [hbm] * n_o,
        out_specs=[row(D_MODEL), pl.BlockSpec((D_MODEL, ts), lambda i: (0, i)),
                   pl.BlockSpec((N_DEV, ts, shard_cols), lambda i: (0, i, 0)), vec, vec] + [hbm] * n_o,
        out_shape=[jax.ShapeDtypeStruct((n_rows, D_MODEL), F32), jax.ShapeDtypeStruct((D_MODEL, n_rows), BF16),
                   jax.ShapeDtypeStruct((N_DEV, n_rows, shard_cols), BF16),
                   jax.ShapeDtypeStruct((1, D_MODEL), F32), jax.ShapeDtypeStruct((1, D_MODEL), F32)]
        + [jax.ShapeDtypeStruct(o.shape, o.dtype) for o in others],
        scratch_shapes=[pltpu.SemaphoreType.DMA((7 * n_o,)), pltpu.SemaphoreType.DMA((7 * n_o,)),
                        pltpu.SemaphoreType.DMA((n_o,))],
        compiler_params=pltpu.CompilerParams(dimension_semantics=("arbitrary",), vmem_limit_bytes=VMEM_LIMIT),
    )(x, dz, *pieces, mod, w_in_p, unrot, *others)


def in_weight_grad_exchange(h_t, dp_blocks, others, small, order):
    n = len(others)
    n_rows = h_t.shape[1]
    ts = 2 * ROW_TILE
    n_i = n_rows // ts
    shard_cols = dp_blocks.shape[2]
    n_chips = N_DEV // 2
    last = N_DEV - 1

    def body(order_ref, h_ref, dp_ref, *rest):
        g_refs, s_ref = rest[:n], rest[n]
        rwin_ref, rg_refs, rs_ref = rest[n + 1], rest[n + 2:2 * n + 2], rest[2 * n + 2]
        (acc, sendbuf, sib_buf, sib_send, sib_recv, win_send, win_recv,
         o_send, o_recv, local_sems) = rest[2 * n + 3:]
        b, i = pl.program_id(0), pl.program_id(1)
        me = my_position()
        mi = flat_index(me)
        sibling = (me[0], me[1], 1 - me[2])

        def other_copies(k, src_index, dst_index):
            peer = flip(me, k)
            out = [pltpu.make_async_remote_copy(
                src_ref=g_refs[a].at[src_index], dst_ref=rg_refs[a].at[dst_index],
                send_sem=o_send.at[(n + 1) * (k - 1) + a], recv_sem=o_recv.at[(n + 1) * (k - 1) + a],
                device_id=peer, device_id_type=MESH_IDS) for a in range(n)]
            out.append(pltpu.make_async_remote_copy(
                src_ref=s_ref, dst_ref=rs_ref.at[dst_index],
                send_sem=o_send.at[(n + 1) * (k - 1) + n], recv_sem=o_recv.at[(n + 1) * (k - 1) + n],
                device_id=peer, device_id_type=MESH_IDS))
            return out

        def local_copies():
            out = [pltpu.make_async_copy(g_refs[a].at[mi], rg_refs[a].at[mi], local_sems.at[a]) for a in range(n)]
            out.append(pltpu.make_async_copy(s_ref, rs_ref.at[mi], local_sems.at[n]))
            return out

        def to_sibling(t):
            return pltpu.make_async_remote_copy(
                src_ref=sendbuf.at[t], dst_ref=sib_buf.at[t], send_sem=sib_send.at[t], recv_sem=sib_recv.at[t],
                device_id=sibling, device_id_type=MESH_IDS)

        def to_owner(t):
            flip_x = (t < 2) * 1
            flip_y = 1 - (t & 1)
            owner = (me[0] ^ flip_x, me[1] ^ flip_y, me[2])
            return pltpu.make_async_remote_copy(
                src_ref=sendbuf.at[n_chips + t], dst_ref=rwin_ref.at[t], send_sem=win_send.at[t],
                recv_sem=win_recv.at[t], device_id=owner, device_id_type=MESH_IDS)

        own_block = pltpu.make_async_copy(sendbuf.at[last], rwin_ref.at[n_chips - 1], local_sems.at[n + 1])

        @pl.when(jnp.logical_and(b == 0, i == 0))
        def _():
            for cp in local_copies():
                cp.start()
            for k in range(1, N_DEV):
                for cp in other_copies(k, flat_index(flip(me, k)), mi):
                    cp.start()

        contrib = jnp.dot(h_ref[...], dp_ref[...], preferred_element_type=F32)

        @pl.when(i == 0)
        def _():
            acc[...] = contrib

        @pl.when(i > 0)
        def _():
            acc[...] += contrib

        slot = order_ref[N_DEV + b]
        t = slot & (n_chips - 1)

        @pl.when(jnp.logical_and(i == n_i - 1, slot < n_chips))
        def _():
            sendbuf[slot] = acc[...].astype(BF16)
            to_sibling(t).start()

        @pl.when(jnp.logical_and(i == n_i - 1, slot >= n_chips))
        def _():
            to_sibling(t).wait_recv()
            sendbuf[slot] = (acc[...] + sib_buf[t].astype(F32)).astype(BF16)

            @pl.when(slot < last)
            def _():
                to_owner(t).start()

            @pl.when(slot == last)
            def _():
                own_block.start()

        @pl.when(jnp.logical_and(b == last, i == n_i - 1))
        def _():
            for t in range(n_chips - 1):
                to_owner(t).wait_recv()
            for k in range(1, N_DEV):
                pi = flat_index(flip(me, k))
                for cp in other_copies(k, pi, pi):
                    cp.wait_recv()
            for t in range(n_chips):
                to_sibling(t).wait_send()
            for t in range(n_chips - 1):
                to_owner(t).wait_send()
            for k in range(1, N_DEV):
                for cp in other_copies(k, flat_index(flip(me, k)), mi):
                    cp.wait_send()
            for cp in local_copies():
                cp.wait()
            own_block.wait()

    hbm = pl.BlockSpec(memory_space=pl.ANY)
    n_sem = 7 * (n + 1)
    grid_spec = pltpu.PrefetchScalarGridSpec(
        num_scalar_prefetch=1, grid=(N_DEV, n_i),
        in_specs=[pl.BlockSpec((D_MODEL, ts), lambda b, i, order: (0, i)),
                  pl.BlockSpec((None, ts, shard_cols), lambda b, i, order: (order[b], i, 0))] + [hbm] * (n + 1),
        out_specs=[hbm] * (n + 2),
        scratch_shapes=[pltpu.VMEM((D_MODEL, shard_cols), F32), pltpu.VMEM((N_DEV, D_MODEL, shard_cols), BF16),
                        pltpu.VMEM((n_chips, D_MODEL, shard_cols), BF16),
                        pltpu.SemaphoreType.DMA((n_chips,)), pltpu.SemaphoreType.DMA((n_chips,)),
                        pltpu.SemaphoreType.DMA((n_chips - 1,)), pltpu.SemaphoreType.DMA((n_chips - 1,)),
                        pltpu.SemaphoreType.DMA((n_sem,)), pltpu.SemaphoreType.DMA((n_sem,)),
                        pltpu.SemaphoreType.DMA((n + 2,))])
    return pl.pallas_call(
        body, name="in_weight_grad_exchange", grid_spec=grid_spec,
        out_shape=[jax.ShapeDtypeStruct((n_chips, D_MODEL, shard_cols), BF16)]
        + [jax.ShapeDtypeStruct(o.shape, o.dtype) for o in others]
        + [jax.ShapeDtypeStruct((N_DEV,) + small.shape, small.dtype)],
        compiler_params=pltpu.CompilerParams(dimension_semantics=("arbitrary", "arbitrary"),
                                             vmem_limit_bytes=VMEM_LIMIT),
    )(order, h_t, dp_blocks, *others, small)


def ada_weight_grad(c_all, dmod_cols):
    def body(c_ref, d_ref, o_ref):
        cv = c_ref[...]
        o_ref[...] = hdot_tn(cv * sigmoid(cv), d_ref[...])

    return pl.pallas_call(
        body, name="ada_weight_grad",
        out_shape=jax.ShapeDtypeStruct((c_all.shape[1], dmod_cols.shape[1]), F32),
    )(c_all, dmod_cols)


def adamw_update(g, w, m, v):
    nm = ADAM_B1 * m + (1.0 - ADAM_B1) * g
    nv = ADAM_B2 * v + (1.0 - ADAM_B2) * (g * g)
    m_hat = nm / (1.0 - ADAM_B1 ** ADAM_STEP)
    v_hat = nv / (1.0 - ADAM_B2 ** ADAM_STEP)
    return -ADAM_LR * (m_hat / (jnp.sqrt(v_hat) + ADAM_EPS) + ADAM_WD * w), nm, nv


def adamw(parts, w, m, v, name):
    k, rows, cols = parts.shape
    rb = 128 if rows % 128 == 0 else rows

    def body(p_ref, w_ref, m_ref, v_ref, g_ref, d_ref, nm_ref, nv_ref):
        g = p_ref[0].astype(F32)
        for i in range(1, k):
            g = g + p_ref[i].astype(F32)
        g_ref[0] = g
        d_ref[0], nm_ref[0], nv_ref[0] = adamw_update(g, w_ref[0], m_ref[0], v_ref[0])

    blk = pl.BlockSpec((1, rb, cols), lambda i: (0, i, 0))
    return pl.pallas_call(
        body, name=name, grid=(rows // rb,),
        in_specs=[pl.BlockSpec((k, rb, cols), lambda i: (0, i, 0)), blk, blk, blk],
        out_specs=[blk] * 4, out_shape=[jax.ShapeDtypeStruct((1, rows, cols), F32)] * 4,
        compiler_params=pltpu.CompilerParams(dimension_semantics=("arbitrary",), vmem_limit_bytes=VMEM_LIMIT),
    )(parts, w, m, v)


def adamw_small(parts, ws, ms, vs):
    k = parts.shape[0]
    n = len(ws)
    sizes = [w.shape[1] for w in ws]

    def body(p_ref, *refs):
        ins, outs = refs[:3 * n], refs[3 * n:]
        g_all = p_ref[0]
        for i in range(1, k):
            g_all = g_all + p_ref[i]
        off = 0
        for a, size in enumerate(sizes):
            g = g_all[:, off:off + size]
            off += size
            d, nm, nv = adamw_update(g, ins[a][...], ins[n + a][...], ins[2 * n + a][...])
            for kind, val in enumerate((g, d, nm, nv)):
                outs[kind * n + a][...] = val

    return pl.pallas_call(
        body, name="adamw_small",
        out_shape=[jax.ShapeDtypeStruct((1, size), F32) for _ in range(4) for size in sizes],
    )(parts, *ws, *ms, *vs)


def rot_cols(w):
    return jnp.concatenate([-w[:, ROPE // 2:], w[:, :ROPE // 2]], axis=1)


def unrot_cols(dw):
    return jnp.concatenate([dw[:, ROPE // 2:], -dw[:, :ROPE // 2]], axis=1)


def columns_from_shards(g, rows, cols):
    return g.reshape(N_DEV, rows, cols).transpose(1, 0, 2).reshape(rows, N_DEV * cols)


def shards_from_columns(w, rows, cols):
    return w.reshape(rows, N_DEV, cols).transpose(1, 0, 2).reshape(N_DEV, rows * cols)


def permute_w_in(w):
    z = lambda n: jnp.zeros((D_MODEL, n), w.dtype)
    krope = w[:, N_KROPE:N_KROPE + ROPE]
    rw = N_RWKV
    return jnp.concatenate([
        w[:, N_MA:N_MA + 1024], w[:, N_MB:N_MB + 1024],
        w[:, rw:rw + 512], w[:, rw + 512:rw + 1024], w[:, rw + 1024:rw + 1536],
        w[:, N_GPA:N_GPA + 512], w[:, N_GPB:N_GPB + 512],
        w[:, N_QC:N_QC + 256], w[:, N_KVC:N_KVC + 128],
        z(NOPE), krope, z(LANE - QK_DIM), z(NOPE), rot_cols(krope), z(LANE - QK_DIM),
        w[:, rw + 1536:rw + 1664]], axis=1)


def unpermute_w_in_grad(d):
    rw = P_R
    krope = d[:, P_KR + NOPE:P_KR + QK_DIM] + unrot_cols(d[:, P_KRR + NOPE:P_KRR + QK_DIM])
    return jnp.concatenate([
        d[:, P_QC:P_QC + 256], d[:, P_KVC:P_KVC + 128], krope, d[:, P_GPA:P_GPA + 512],
        d[:, rw:rw + 1536], d[:, P_LORA:P_LORA + 128], d[:, P_GPB:P_GPB + 512],
        d[:, P_MA:P_MA + 1024], d[:, P_MB:P_MB + 1024]], axis=1)


def pad_heads_q(w_uq):
    w = w_uq.reshape(Q_RANK, HEADS, QK_DIM)
    zpad = jnp.zeros((Q_RANK, HEADS, LANE - QK_DIM), w.dtype)
    wq = jnp.concatenate([w, zpad], axis=2).reshape(Q_RANK, HEADS * LANE)
    pe = w[:, :, NOPE:]
    rot = jnp.concatenate([-pe[:, :, ROPE // 2:], pe[:, :, :ROPE // 2]], axis=2)
    wqr = jnp.concatenate([jnp.zeros((Q_RANK, HEADS, NOPE), w.dtype), rot, zpad], axis=2).reshape(Q_RANK, HEADS * LANE)
    return wq, wqr


def unpad_heads_q_grad(dwq, dwqr):
    a = dwq.reshape(Q_RANK, HEADS, LANE)
    r = dwqr.reshape(Q_RANK, HEADS, LANE)[:, :, NOPE:QK_DIM]
    pe = a[:, :, NOPE:QK_DIM] + jnp.concatenate([r[:, :, ROPE // 2:], -r[:, :, :ROPE // 2]], axis=2)
    return jnp.concatenate([a[:, :, :NOPE], pe], axis=2).reshape(Q_RANK, HEADS * QK_DIM)


def pad_heads_kv(w_ukv):
    w = w_ukv.reshape(KV_RANK, HEADS, 2 * HEAD)
    z = jnp.zeros((KV_RANK, HEADS, HEAD), w.dtype)
    wkn = jnp.concatenate([w[:, :, :NOPE], z], axis=2).reshape(KV_RANK, HEADS * LANE)
    val = w[:, :, NOPE:]
    odd = (jnp.arange(HEADS) % 2 == 1)[None, :, None]
    wv = jnp.concatenate([jnp.where(odd, 0, val), jnp.where(odd, val, 0)], axis=2).reshape(KV_RANK, HEADS * LANE)
    return wkn, wv


def unpad_heads_kv_grad(dwkn, dwv):
    a = dwkn.reshape(KV_RANK, HEADS, LANE)[:, :, :NOPE]
    b = dwv.reshape(KV_RANK, HEADS, LANE)
    odd = (jnp.arange(HEADS) % 2 == 1)[None, :, None]
    val = jnp.where(odd, b[:, :, HEAD:], b[:, :, :HEAD])
    return jnp.concatenate([a, val], axis=2).reshape(KV_RANK, HEADS * 2 * HEAD)


def kernel(x, c, positions, w_ada, b_ada, w_in, q_norm_g, w_uq, kv_norm_g, w_ukv, mu_rwkv, w0, w_decay_up, a0, w_iclr_up, k_k, k_a, r_k, gn_g, gn_b, w_proj_a, w_proj_b, w_out, post_g, post_b, loss_target, m_w_ada, m_b_ada, m_w_in, m_q_norm_g, m_w_uq, m_kv_norm_g, m_w_ukv, m_mu_rwkv, m_w0, m_w_decay_up, m_a0, m_w_iclr_up, m_k_k, m_k_a, m_r_k, m_gn_g, m_gn_b, m_w_proj_a, m_w_proj_b, m_w_out, m_post_g, m_post_b, v_w_ada, v_b_ada, v_w_in, v_q_norm_g, v_w_uq, v_kv_norm_g, v_w_ukv, v_mu_rwkv, v_w0, v_w_decay_up, v_a0, v_w_iclr_up, v_k_k, v_k_a, v_r_k, v_gn_g, v_gn_b, v_w_proj_a, v_w_proj_b, v_w_out, v_post_g, v_post_b):
    weights = dict(w_ada=w_ada, b_ada=b_ada, w_in=w_in, q_norm_g=q_norm_g, w_uq=w_uq, kv_norm_g=kv_norm_g,
                   w_ukv=w_ukv, mu_rwkv=mu_rwkv, w0=w0, w_decay_up=w_decay_up, a0=a0, w_iclr_up=w_iclr_up,
                   k_k=k_k, k_a=k_a, r_k=r_k, gn_g=gn_g, gn_b=gn_b, w_proj_a=w_proj_a, w_proj_b=w_proj_b,
                   w_out=w_out, post_g=post_g, post_b=post_b)
    mom1 = dict(w_ada=m_w_ada, b_ada=m_b_ada, w_in=m_w_in, q_norm_g=m_q_norm_g, w_uq=m_w_uq, kv_norm_g=m_kv_norm_g,
                w_ukv=m_w_ukv, mu_rwkv=m_mu_rwkv, w0=m_w0, w_decay_up=m_w_decay_up, a0=m_a0, w_iclr_up=m_w_iclr_up,
                k_k=m_k_k, k_a=m_k_a, r_k=m_r_k, gn_g=m_gn_g, gn_b=m_gn_b, w_proj_a=m_w_proj_a, w_proj_b=m_w_proj_b,
                w_out=m_w_out, post_g=m_post_g, post_b=m_post_b)
    mom2 = dict(w_ada=v_w_ada, b_ada=v_b_ada, w_in=v_w_in, q_norm_g=v_q_norm_g, w_uq=v_w_uq, kv_norm_g=v_kv_norm_g,
                w_ukv=v_w_ukv, mu_rwkv=v_mu_rwkv, w0=v_w0, w_decay_up=v_w_decay_up, a0=v_a0, w_iclr_up=v_w_iclr_up,
                k_k=v_k_k, k_a=v_k_a, r_k=v_r_k, gn_g=v_gn_g, gn_b=v_gn_b, w_proj_a=v_w_proj_a, w_proj_b=v_w_proj_b,
                w_out=v_w_out, post_g=v_post_g, post_b=v_post_b)
    names = list(weights)
    n_rows = x.shape[1]
    me = 4 * lax.axis_index("x") + 2 * lax.axis_index("y") + lax.axis_index("c")
    xr = x[0]
    tgt = loss_target[0]
    row = lambda a: a.reshape(1, -1)

    w_in_all, c_all = gather_shards([w_in[0].astype(BF16), c])
    c_all = c_all.reshape(N_DEV, D_MODEL)
    w_in_p = permute_w_in(columns_from_shards(w_in_all, D_MODEL, IN_WIDTH // N_DEV))

    mod_all = ada_modulation(c_all, w_ada[0], b_ada.reshape(N_DEV, -1))
    mod = lax.dynamic_index_in_dim(mod_all, me, axis=1, keepdims=False).reshape(3, D_MODEL)

    proj, *gathered = fwd_in_gather(xr, mod, w_in_p, [weights[n][0].astype(BF16) for n, _, _ in SHARDED[1:]])
    pcol = lambda off_, w: (proj, w, off_ // w)
    full = {}
    for (n, r, cdim), part in zip(SHARDED[1:], gathered):
        full[n] = part.reshape(N_DEV * r, cdim) if n == "w_out" else columns_from_shards(part, r, cdim)
    wq, wqr = pad_heads_q(full["w_uq"])
    wkn, wv = pad_heads_kv(full["w_ukv"])
    zl = jnp.zeros((LORA, WIDTH), BF16)
    w_dec = jnp.concatenate([full["w_decay_up"], zl], axis=0)
    w_iclr = jnp.concatenate([zl, full["w_iclr_up"]], axis=0)
    wpa, wpb, wout = full["w_proj_a"], full["w_proj_b"], full["w_out"]

    inv_freq = ROPE_THETA ** (-jnp.arange(0, ROPE, 2, dtype=F32) / ROPE)
    ang = positions[0].astype(F32)[:, None] * inv_freq
    ones_n, zeros_n, zeros_p = jnp.ones((n_rows, NOPE), F32), jnp.zeros((n_rows, NOPE), F32), jnp.zeros((n_rows, LANE - QK_DIM), F32)
    cos_t = jnp.concatenate([ones_n, jnp.cos(ang), jnp.cos(ang), zeros_p], axis=1)
    sin_t = jnp.concatenate([zeros_n, jnp.sin(ang), jnp.sin(ang), zeros_p], axis=1)

    gq, gkv = q_norm_g, kv_norm_g
    mla_consts = [gq, gkv, wq, wqr, wkn, wv]
    q, k, v = row_call(
        "mla_prep", mla_prep_tile, n_rows,
        [pcol(P_QC, 256), pcol(P_KVC, 128), pcol(P_KR, 128), pcol(P_KRR, 128), (cos_t, LANE, 0), (sin_t, LANE, 0)],
        mla_consts, [(HEADS * LANE, BF16)] * 3)
    ya, lse = attention_forward(q, k, v)

    t_idx = jnp.arange(ROW_TILE)
    same_chunk = (t_idx[:, None] // CHUNK) == (t_idx[None, :] // CHUNK)
    same = same_chunk.astype(F32)
    tril = (same_chunk & (t_idx[:, None] >= t_idx[None, :])).astype(F32)
    l_idx = jnp.arange(LANE)
    bd = ((l_idx[:, None] // HEAD) == (l_idx[None, :] // HEAD)).astype(F32)
    mu = mu_rwkv
    mu_r, mu_k, mu_v, mu_l = mu[:, 0:512], mu[:, 512:1024], mu[:, 1024:1536], mu[:, 1536:1664]
    rk_row = row(r_k)
    rwkv_consts = [mu_r, mu_k, mu_v, mu_l, w0, a0, k_k, k_a, w_dec, w_iclr, tril, same, bd]
    rwkv_rows = [pcol(P_R, 512), pcol(P_K, 512), pcol(P_V, 512), pcol(P_LORA, 128)]
    rt, at, bt, kt, clf, uv, ur, k2 = row_call(
        "rwkv_prep", rwkv_prep_tile, n_rows, rwkv_rows, rwkv_consts, [(WIDTH, F32)] * 8, halo_in=rwkv_rows)
    y, m0s, state_maps, out_maps, *wkv_saved = wkv_forward(at, bt, kt, rt, uv, clf)

    tail = row_call(
        "tail", tail_tile, n_rows,
        [(xr, D_MODEL, 0), (tgt, D_MODEL, 0), pcol(P_MA, 1024), pcol(P_MB, 1024), pcol(P_GPA, 512), pcol(P_GPB, 512),
         (ya, WIDTH, 0), (y, WIDTH, 0), (ur, WIDTH, 0), (k2, WIDTH, 0), (uv, WIDTH, 0)],
        [mod, wpa, wpb, wout, gn_g, gn_b, rk_row, post_g, post_b, bd],
        [(D_MODEL, F32), (1024, BF16), (1024, BF16), (512, BF16), (512, BF16), (WIDTH, F32), (WIDTH, F32), (WIDTH, F32)],
        acc_out=[((1, LANE), F32), ((D_MODEL, D_MODEL), F32), ((WIDTH, D_MODEL), F32), ((WIDTH, D_MODEL), F32),
                 ((1, WIDTH), F32), ((1, WIDTH), F32), ((1, D_MODEL), F32), ((1, D_MODEL), F32), ((1, D_MODEL), F32)])
    (dz, dma, dmb, dgpa, dgpb, dya, dy, dyb,
     loss_row, g_wout, g_wpa, g_wpb, g_gn_g, g_gn_b, g_post_g, g_post_b, dgate) = tail

    dq, dk, dv = attention_backward(q, k, v, ya, dya, lse)
    dq_c, dkv_c, dkr, dkrr, g_wq, g_wqr, g_wkn, g_wv, g_gq, g_gkv = row_call(
        "mla_prep_bwd", mla_prep_bwd_tile, n_rows,
        [pcol(P_QC, 256), pcol(P_KVC, 128), (cos_t, LANE, 0), (sin_t, LANE, 0),
         (dq, HEADS * LANE, 0), (dk, HEADS * LANE, 0), (dv, HEADS * LANE, 0)],
        mla_consts, [(256, BF16), (128, BF16), (128, BF16), (128, BF16)],
        acc_out=[((Q_RANK, HEADS * LANE), F32)] * 2 + [((KV_RANK, HEADS * LANE), F32)] * 2
        + [((1, Q_RANK), F32), ((1, KV_RANK), F32)])

    dat, dbt, dkt, drt, dvv, dlw = wkv_backward(at, bt, kt, rt, uv, clf, m0s, state_maps, out_maps, wkv_saved, dy)
    (dr0, dk0, dv0, dl0, g_mu_r, g_mu_k, g_mu_v, g_mu_l, g_w0, g_a0, g_k_k, g_k_a, g_r_k, g_wdec, g_wiclr) = row_call(
        "rwkv_prep_bwd", rwkv_prep_bwd_tile, n_rows,
        rwkv_rows + [(drt, WIDTH, 0), (dat, WIDTH, 0), (dbt, WIDTH, 0), (dkt, WIDTH, 0), (dvv, WIDTH, 0),
                     (dlw, WIDTH, 0), (dyb, WIDTH, 0)],
        rwkv_consts + [rk_row], [(512, BF16), (512, BF16), (512, BF16), (128, BF16)],
        acc_out=[((1, 512), F32)] * 3 + [((1, 128), F32)] + [((1, 512), F32)] * 5 + [((LANE, WIDTH), F32)] * 2,
        halo_in=rwkv_rows, carry=[512, 512, 512, 128], reverse=True)

    li = jnp.arange(LANE)
    src, dst = li[:, None], li[None, :]
    half = ROPE // 2
    unrot = (jnp.where((dst >= NOPE) & (dst < NOPE + half) & (src == dst + half), 1.0, 0.0)
             - jnp.where((dst >= NOPE + half) & (dst < QK_DIM) & (src == dst - half), 1.0, 0.0)).astype(BF16)
    grads_full = {
        "w_uq": unpad_heads_q_grad(g_wq, g_wqr), "w_ukv": unpad_heads_kv_grad(g_wkn, g_wv),
        "w_decay_up": g_wdec[:LORA], "w_iclr_up": g_wiclr[LORA:],
        "w_proj_a": g_wpa, "w_proj_b": g_wpb, "w_out": g_wout}
    blocks = [(grads_full[n].reshape(N_DEV, r, cdim) if n == "w_out"
               else grads_full[n].reshape(r, N_DEV, cdim).transpose(1, 0, 2)).astype(BF16) for n, r, cdim in SHARDED[1:]]
    dx, h_t, dproj_blocks, dshift, dscale, *got_others = in_backward(
        xr, dz, [dma, dmb, dr0, dk0, dv0, dgpa, dgpb, dq_c, dkv_c, dkr, dkrr, dl0], mod, w_in_p, unrot, blocks)

    dmod = jnp.concatenate([dshift, dscale, dgate], axis=1)
    small = jnp.concatenate([dmod, g_gq, g_gkv, g_mu_r, g_mu_k, g_mu_v, g_mu_l, g_w0, g_a0, g_k_k, g_k_a, g_r_k,
                             g_gn_g, g_gn_b, g_post_g, g_post_b, loss_row], axis=1)
    my_x, my_y, my_c = lax.axis_index("x"), lax.axis_index("y"), lax.axis_index("c")
    chip_order = [4 * (my_x ^ fx) + 2 * (my_y ^ fy) for fx, fy in ((1, 1), (1, 0), (0, 1), (0, 0))]
    owners = [chip_order[s % 4] + (my_c if s >= 4 else 1 - my_c) for s in WGRAD_SLOTS]
    order = jnp.stack(owners + [jnp.int32(s) for s in WGRAD_SLOTS]).astype(jnp.int32)
    got_w_in, got_small = in_weight_grad_exchange(h_t, dproj_blocks, [], small, order)
    got_blocks = [got_w_in] + got_others
    loss = jnp.sum(got_small[:, 0, SMALL_ELEMS])

    ada_cols = w_ada.shape[2]
    dmod_all = got_small[:, 0, :3 * D_MODEL]
    g_ada = ada_weight_grad(c_all, lax.dynamic_slice_in_dim(dmod_all, me * ada_cols, ada_cols, axis=1))

    outs = [dict() for _ in range(4)]
    res = adamw(g_ada[None], w_ada, m_w_ada, v_w_ada, "adamw_w_ada")
    for kind in range(4):
        outs[kind]["w_ada"] = res[kind]
    for (n, r, cdim), got in zip(SHARDED, got_blocks):
        res = adamw(got, weights[n], mom1[n], mom2[n], "adamw_" + n)
        for kind in range(4):
            outs[kind][n] = res[kind]
    rows_of = lambda tree: [tree[n].reshape(1, -1) for n, _ in SMALL]
    res = adamw_small(got_small, rows_of(weights), rows_of(mom1), rows_of(mom2))
    for kind in range(4):
        for a, (n, _) in enumerate(SMALL):
            outs[kind][n] = res[kind * len(SMALL) + a].reshape(weights[n].shape)
    return (loss, dx[None], *[outs[0][n] for n in names], *[outs[1][n] for n in names],
            *[outs[2][n] for n in names], *[outs[3][n] for n in names])
```

```python
import functools
import math

import jax
import jax.numpy as jnp
from jax import lax
from jax.experimental import pallas as pl
from jax.experimental.pallas import tpu as pltpu

F32 = jnp.float32
BF16 = jnp.bfloat16
HIGHEST = lax.Precision.HIGHEST
MESH_IDS = pl.DeviceIdType.MESH

N_DEV = 8
D_MODEL = 1024
LN_EPS = 1e-5
RMS_EPS = 1e-6
GN_EPS = 64e-5
HEADS = 8
Q_RANK = 256
KV_RANK = 128
ROPE = 32
NOPE = 64
QK_DIM = NOPE + ROPE
WIDTH = 512
HEAD = 64
LORA = 64
CHUNK = 64
DEPTH = 1
ALPHA = (2.0 * DEPTH) ** 0.25
ROPE_THETA = 10000.0
ATTN_SCALE = QK_DIM ** -0.5
DECAY_SCALE = math.exp(-0.5)

ADAM_LR = 0.001
ADAM_B1 = 0.9
ADAM_B2 = 0.999
ADAM_EPS = 1e-08
ADAM_WD = 0.01
ADAM_STEP = 10

LANE = 128
PAIR = 2 * HEAD
ROW_TILE = 256
HALO_ROWS = 16
ATTN_FWD_TILES = (512, 1024)
ATTN_BWD_TILES = (512, 512)
LOG2_E = math.log2(math.e)
Q_PRESCALE = ATTN_SCALE * LOG2_E
WKV_CHUNKS_PER_STEP = 8
WGRAD_SLOTS = (0, 1, 4, 2, 5, 6, 3, 7)
VMEM_LIMIT = 56 * 1024 * 1024

P_MA, P_MB, P_R, P_K, P_V, P_GPA, P_GPB, P_QC, P_KVC, P_KR, P_KRR, P_LORA = (
    0, 1024, 2048, 2560, 3072, 3584, 4096, 4608, 4864, 4992, 5120, 5248)
P_WIDTH = 5376
DW_BLOCK = 768

N_QC, N_KVC, N_KROPE, N_GPA, N_RWKV, N_GPB, N_MA, N_MB = 0, 256, 384, 416, 928, 2592, 3104, 4128
IN_WIDTH = 5152

SHARDED = (("w_in", 1024, 644), ("w_uq", 256, 96), ("w_ukv", 128, 128), ("w_decay_up", 64, 64),
           ("w_iclr_up", 64, 64), ("w_proj_a", 512, 128), ("w_proj_b", 512, 128), ("w_out", 128, 1024))
SHARD_ELEMS = sum(r * c for _, r, c in SHARDED)
SHARD_ROWS = SHARD_ELEMS // LANE
GATHER_ROWS = SHARD_ROWS + 2 * D_MODEL // LANE
SMALL = (("b_ada", 3072), ("q_norm_g", 256), ("kv_norm_g", 128), ("mu_rwkv", 1664), ("w0", 512), ("a0", 512),
         ("k_k", 512), ("k_a", 512), ("r_k", 512), ("gn_g", 512), ("gn_b", 512), ("post_g", 1024), ("post_b", 1024))
SMALL_ELEMS = sum(n for _, n in SMALL)
SMALL_ROWS = SMALL_ELEMS // LANE


def mm(a, b):
    return jnp.dot(a.astype(BF16), b.astype(BF16), preferred_element_type=F32)


def mm_nt(a, b):
    return lax.dot_general(a.astype(BF16), b.astype(BF16), (((1,), (1,)), ((), ())), preferred_element_type=F32)


def mm_tn(a, b):
    return lax.dot_general(a.astype(BF16), b.astype(BF16), (((0,), (0,)), ((), ())), preferred_element_type=F32)


def hdot(a, b):
    return jnp.dot(a, b, precision=HIGHEST, preferred_element_type=F32)


def hdot_nt(a, b):
    return lax.dot_general(a, b, (((1,), (1,)), ((), ())), precision=HIGHEST, preferred_element_type=F32)


def hdot_tn(a, b):
    return lax.dot_general(a, b, (((0,), (0,)), ((), ())), precision=HIGHEST, preferred_element_type=F32)


def sigmoid(x):
    return 1.0 / (1.0 + jnp.exp(-x))


def colsum(x):
    return jnp.sum(x, axis=0, keepdims=True)


def rowmean(x):
    return jnp.mean(x, axis=-1, keepdims=True)


def layer_norm_stats(x):
    xc = x - rowmean(x)
    rstd = lax.rsqrt(rowmean(xc * xc) + LN_EPS)
    return xc * rstd, rstd


def layer_norm_bwd(dy, xhat, rstd):
    return rstd * (dy - rowmean(dy) - xhat * rowmean(dy * xhat))


def bf16_pieces(x, n):
    pieces = []
    for _ in range(n):
        p = x.astype(BF16)
        pieces.append(p)
        x = x - p.astype(F32)
    return pieces


def ones_dot(ones, x, n_pieces):
    ones = ones.astype(BF16)
    return sum(jnp.dot(ones, p, preferred_element_type=F32) for p in bf16_pieces(x, n_pieces))


def ones_dot_nt(ones, x, n_pieces):
    ones = ones.astype(BF16)
    return sum(lax.dot_general(ones, p, (((1,), (1,)), ((), ())), preferred_element_type=F32)
               for p in bf16_pieces(x, n_pieces))


def head_sum(x, bd):
    return jnp.concatenate([mm(x[:, p * LANE:(p + 1) * LANE], bd) for p in range(x.shape[1] // LANE)], axis=1)


def tile_lanes(t, n):
    return jnp.concatenate([t] * n, axis=1)


def row_iota(shape):
    return lax.broadcasted_iota(jnp.int32, shape, 0)


def lane_iota(shape):
    return lax.broadcasted_iota(jnp.int32, shape, 1)


def shift_rows_down(x, row0):
    rolled = pltpu.roll(x, 1, axis=0)
    return jnp.where(row_iota(x.shape) == 0, row0, rolled)


def shift_rows_up(x, row_last):
    rolled = pltpu.roll(x, x.shape[0] - 1, axis=0)
    return jnp.where(row_iota(x.shape) == x.shape[0] - 1, row_last, rolled)


def row_call(name, fn, n_rows, row_in, const_in, row_out, acc_out=(), halo_in=(), carry=(), reverse=False):
    ts = ROW_TILE
    n_tiles = n_rows // ts
    n_in = len(row_in) + len(halo_in) + len(const_in)
    n_ro, n_ao = len(row_out), len(acc_out)

    def tile_of(g):
        return (n_tiles - 1 - g) if reverse else g

    def body(*refs):
        ins = refs[:n_in]
        ro = refs[n_in:n_in + n_ro]
        ao = refs[n_in + n_ro:n_in + n_ro + n_ao]
        cr = refs[n_in + n_ro + n_ao:]
        g = pl.program_id(0)
        step0 = g == 0
        tile0 = tile_of(g) == 0
        for r in cr:
            @pl.when(step0)
            def _(r=r):
                r[...] = jnp.zeros_like(r)
        n_tiled = len(row_in) + len(halo_in)
        vals = [r[...].astype(F32) for r in ins[:n_tiled]] + [r[...] for r in ins[n_tiled:]]
        outs = fn(step0, tile0, *vals, *[c[0:1, :] for c in cr])
        for r, v in zip(ro, outs[:n_ro]):
            r[...] = v.astype(r.dtype)
        for r, v in zip(ao, outs[n_ro:n_ro + n_ao]):
            @pl.when(step0)
            def _(r=r, v=v):
                r[...] = v.astype(r.dtype)

            @pl.when(jnp.logical_not(step0))
            def _(r=r, v=v):
                r[...] += v.astype(r.dtype)
        for r, v in zip(cr, outs[n_ro + n_ao:]):
            r[0:1, :] = v

    in_specs = [pl.BlockSpec((ts, w), functools.partial(lambda g, cb: (tile_of(g), cb), cb=cb)) for _, w, cb in row_in]
    in_specs += [pl.BlockSpec((HALO_ROWS, w), functools.partial(
        lambda g, cb: (jnp.maximum(tile_of(g) * (ts // HALO_ROWS) - 1, 0), cb), cb=cb)) for _, w, cb in halo_in]
    in_specs += [pl.BlockSpec(memory_space=pltpu.VMEM) for _ in const_in]
    out_specs = [pl.BlockSpec((ts, w), lambda g: (tile_of(g), 0)) for w, _ in row_out]
    out_specs += [pl.BlockSpec(s, lambda g: (0, 0)) for s, _ in acc_out]
    out_shape = [jax.ShapeDtypeStruct((n_rows, w), d) for w, d in row_out]
    out_shape += [jax.ShapeDtypeStruct(s, d) for s, d in acc_out]
    return pl.pallas_call(
        body, name=name, grid=(n_tiles,), in_specs=in_specs, out_specs=out_specs, out_shape=out_shape,
        scratch_shapes=[pltpu.VMEM((8, w), F32) for w in carry],
        compiler_params=pltpu.CompilerParams(dimension_semantics=("arbitrary",), vmem_limit_bytes=VMEM_LIMIT),
    )(*[a for a, _, _ in row_in], *[a for a, _, _ in halo_in], *const_in)


def my_position():
    return lax.axis_index("x"), lax.axis_index("y"), lax.axis_index("c")


def flip(pos, k):
    x, y, c = pos
    dx, dy, dc = (k >> 2) & 1, (k >> 1) & 1, k & 1
    return (1 - x if dx else x, 1 - y if dy else y, 1 - c if dc else c)


def flat_index(pos):
    return 4 * pos[0] + 2 * pos[1] + pos[2]


def gather_shards(shards):
    n = len(shards)

    def body(*refs):
        x_refs, out_refs = refs[:n], refs[n:2 * n]
        send_sems, recv_sems, local_sems = refs[2 * n:]
        x, y, c = my_position()
        me, sibling = (x, y, c), (x, y, 1 - c)
        chips = [(1 - x, y), (x, 1 - y), (1 - x, 1 - y)]

        def copy(a, k, block, to, from_input=False):
            slot = out_refs[a].at[flat_index(block)]
            return pltpu.make_async_remote_copy(
                src_ref=x_refs[a] if from_input else slot, dst_ref=slot,
                send_sem=send_sems.at[7 * a + k], recv_sem=recv_sems.at[7 * a + k],
                device_id=to, device_id_type=MESH_IDS)

        mine = [pltpu.make_async_copy(x_refs[a], out_refs[a].at[flat_index(me)], local_sems.at[a]) for a in range(n)]
        for cp in mine:
            cp.start()
        first = []
        for a in range(n):
            first.append(copy(a, 0, me, sibling, from_input=True))
            first += [copy(a, 1 + j, me, (*chip, c), from_input=True) for j, chip in enumerate(chips)]
        for cp in first:
            cp.start()
        passed = []
        for j, chip in enumerate(chips):
            for a in range(n):
                copy(a, 1 + j, (*chip, c), me).wait_recv()
                cp = copy(a, 4 + j, (*chip, c), sibling)
                cp.start()
                passed.append(cp)
        for a in range(n):
            copy(a, 0, sibling, me).wait_recv()
            for j, chip in enumerate(chips):
                copy(a, 4 + j, (*chip, 1 - c), me).wait_recv()
        for cp in first + passed:
            cp.wait_send()
        for cp in mine:
            cp.wait()

    return pl.pallas_call(
        body, name="gather_shards",
        out_shape=[jax.ShapeDtypeStruct((N_DEV,) + s.shape, s.dtype) for s in shards],
        in_specs=[pl.BlockSpec(memory_space=pl.ANY)] * n, out_specs=[pl.BlockSpec(memory_space=pl.ANY)] * n,
        scratch_shapes=[pltpu.SemaphoreType.DMA((7 * n,)), pltpu.SemaphoreType.DMA((7 * n,)),
                        pltpu.SemaphoreType.DMA((n,))],
    )(*shards)


def ada_modulation(c_all, w_ada_loc, b_ada_blocks):
    cols = w_ada_loc.shape[1]

    def body(c_ref, w_ref, b_ref, out_ref, send_sems, recv_sems):
        me = my_position()
        mi = flat_index(me)
        cv = c_ref[...]
        res = hdot(cv * sigmoid(cv), w_ref[...]) + b_ref[pl.ds(mi, 1), :]
        out_ref[mi] = res
        sends = []
        for k in range(1, N_DEV):
            cp = pltpu.make_async_remote_copy(
                src_ref=out_ref.at[mi], dst_ref=out_ref.at[mi], send_sem=send_sems.at[k - 1],
                recv_sem=recv_sems.at[k - 1], device_id=flip(me, k), device_id_type=MESH_IDS)
            cp.start()
            sends.append(cp)
        for k in range(1, N_DEV):
            pi = flat_index(flip(me, k))
            pltpu.make_async_remote_copy(
                src_ref=out_ref.at[pi], dst_ref=out_ref.at[pi], send_sem=send_sems.at[k - 1],
                recv_sem=recv_sems.at[k - 1], device_id=flip(me, k), device_id_type=MESH_IDS).wait_recv()
        for cp in sends:
            cp.wait_send()

    return pl.pallas_call(
        body, name="ada_modulation",
        out_shape=jax.ShapeDtypeStruct((N_DEV, N_DEV, cols), F32),
        in_specs=[pl.BlockSpec(memory_space=pltpu.VMEM)] * 3, out_specs=pl.BlockSpec(memory_space=pltpu.VMEM),
        scratch_shapes=[pltpu.SemaphoreType.DMA((7,)), pltpu.SemaphoreType.DMA((7,))],
    )(c_all, w_ada_loc, b_ada_blocks)


def fwd_in_tile(step0, tile0, x, mod, w_in_ptt):
    xhat, _ = layer_norm_stats(x)
    h = xhat * (1.0 + mod[1:2]) + mod[0:1]
    return (mm_nt(h, w_in_ptt),)


def fwd_in_gather(x, mod, w_in_pt, shards):
    n = len(shards)
    n_rows = x.shape[0]
    ts = ROW_TILE
    n_tiles = n_rows // ts

    def body(x_ref, mod_ref, w_ref, *rest):
        s_refs = rest[:n]
        proj_ref, out_refs = rest[n], rest[n + 1:2 * n + 1]
        send_sems, recv_sems, local_sems = rest[2 * n + 1:]
        g = pl.program_id(0)
        me = my_position()
        mi = flat_index(me)

        def copies(k, slot):
            return [pltpu.make_async_remote_copy(
                src_ref=s_refs[a], dst_ref=out_refs[a].at[slot], send_sem=send_sems.at[7 * a + k - 1],
                recv_sem=recv_sems.at[7 * a + k - 1], device_id=flip(me, k), device_id_type=MESH_IDS)
                for a in range(n)]

        local = [pltpu.make_async_copy(s_refs[a], out_refs[a].at[mi], local_sems.at[a]) for a in range(n)]

        @pl.when(g == 0)
        def _():
            for cp in local:
                cp.start()
            for k in range(1, N_DEV):
                for cp in copies(k, mi):
                    cp.start()

        proj_ref[...] = fwd_in_tile(None, None, x_ref[...], mod_ref[...], w_ref[...])[0].astype(BF16)

        @pl.when(g == n_tiles - 1)
        def _():
            for k in range(1, N_DEV):
                for cp in copies(k, flat_index(flip(me, k))):
                    cp.wait_recv()
            for k in range(1, N_DEV):
                for cp in copies(k, mi):
                    cp.wait_send()
            for cp in local:
                cp.wait()

    hbm = pl.BlockSpec(memory_space=pl.ANY)
    const = pl.BlockSpec(memory_space=pltpu.VMEM)
    return pl.pallas_call(
        body, name="fwd_in_gather", grid=(n_tiles,),
        in_specs=[pl.BlockSpec((ts, D_MODEL), lambda g: (g, 0)), const, const] + [hbm] * n,
        out_specs=[pl.BlockSpec((ts, P_WIDTH), lambda g: (g, 0))] + [hbm] * n,
        out_shape=[jax.ShapeDtypeStruct((n_rows, P_WIDTH), BF16)]
        + [jax.ShapeDtypeStruct((N_DEV,) + s.shape, s.dtype) for s in shards],
        scratch_shapes=[pltpu.SemaphoreType.DMA((7 * n,)), pltpu.SemaphoreType.DMA((7 * n,)),
                        pltpu.SemaphoreType.DMA((n,))],
        compiler_params=pltpu.CompilerParams(dimension_semantics=("arbitrary",), vmem_limit_bytes=VMEM_LIMIT),
    )(x, mod, w_in_pt, *shards)


def rms_norm_fwd(x, g):
    r = lax.rsqrt(rowmean(x * x) + RMS_EPS)
    xh = x * r
    return xh * g, xh, r


def key_rope_mask(shape):
    return (lane_iota(shape) >= NOPE).astype(F32)


def mla_prep_tile(step0, tile0, q_c, kv_c, kr, krr, cos, sin, gq, gkv, wq, wqr, wkn, wv):
    qn, _, _ = rms_norm_fwd(q_c, gq)
    kvn, _, _ = rms_norm_fwd(kv_c, gkv)
    q = (mm(qn, wq) * tile_lanes(cos, HEADS) + mm(qn, wqr) * tile_lanes(sin, HEADS)) * Q_PRESCALE
    kpe = kr * (cos * key_rope_mask(cos.shape)) + krr * sin
    k = mm(kvn, wkn) + tile_lanes(kpe, HEADS)
    v = mm(kvn, wv)
    return q, k, v


def rwkv_prep_core(tile0, r0, k0, v0, l0, hr, hk, hv, hl, mu_r, mu_k, mu_v, mu_l, w0, a0, k_k, k_a,
                   w_dec, w_iclr, tril, same, bd):
    def shifted(x, halo, mu):
        row0 = jnp.where(tile0, 0.0, halo[HALO_ROWS - 1:HALO_ROWS, :])
        prev = shift_rows_down(x, row0)
        return x + (prev - x) * mu, prev

    ur, pr = shifted(r0, hr, mu_r)
    uk, pk = shifted(k0, hk, mu_k)
    uv, pv = shifted(v0, hv, mu_v)
    ul, plo = shifted(l0, hl, mu_l)
    th = jnp.tanh(ul)
    sg = sigmoid(w0 + mm(th, w_dec))
    lw = -DECAY_SCALE * sg
    a_ic = sigmoid(a0 + mm(ul, w_iclr))
    kkraw = uk * k_k
    nrm_raw = jnp.sqrt(head_sum(kkraw * kkraw, bd))
    nrm = jnp.maximum(nrm_raw, 1e-12)
    kk = kkraw / nrm
    k2 = uk * (1.0 + (a_ic - 1.0) * k_a)
    lc = ones_dot(tril, lw, 3)
    lcl = ones_dot(same, lw, 3)
    return dict(ur=ur, uk=uk, uv=uv, ul=ul, pr=pr, pk=pk, pv=pv, pl=plo, th=th, sg=sg, lw=lw, a_ic=a_ic,
                kkraw=kkraw, nrm_raw=nrm_raw, nrm=nrm, kk=kk, k2=k2, lc=lc, lcl=lcl)


def rwkv_prep_tile(step0, tile0, r0, k0, v0, l0, hr, hk, hv, hl, *consts):
    f = rwkv_prep_core(tile0, r0, k0, v0, l0, hr, hk, hv, hl, *consts)
    lc, lw = f["lc"], f["lw"]
    e_neg = jnp.exp(-lc)
    rt = f["ur"] * jnp.exp(lc)
    at = -f["kk"] * jnp.exp(lc - lw)
    bt = f["kk"] * f["a_ic"] * e_neg
    kt = f["k2"] * e_neg
    return rt, at, bt, kt, jnp.exp(f["lcl"]), f["uv"], f["ur"], f["k2"]


def wkv_masks():
    lane = lane_iota((1, PAIR))
    m_lo = (lane < HEAD).astype(F32)
    r2 = row_iota((PAIR, PAIR))
    c2 = lane_iota((PAIR, PAIR))
    bd = ((r2 < HEAD) == (c2 < HEAD)).astype(F32)
    eye2 = (r2 == c2).astype(F32)
    eye = (row_iota((CHUNK, CHUNK)) == lane_iota((CHUNK, CHUNK))).astype(F32)
    t_idx = row_iota((4 * CHUNK, PAIR)) % CHUNK
    s_idx = lane_iota((4 * CHUNK, PAIR)) % CHUNK
    keep = s_idx < t_idx + (row_iota((4 * CHUNK, PAIR)) >= 2 * CHUNK).astype(jnp.int32)
    return (m_lo, 1.0 - m_lo), keep, eye, bd, eye2


def rows(*parts):
    return jnp.concatenate(parts, axis=0)


def lanes(*parts):
    return jnp.concatenate(parts, axis=1)


def head_rows(x, ms):
    return rows(x * ms[0], x * ms[1])


def wkv_score_stack(at, rt, ms):
    return rows(head_rows(at, ms), head_rows(rt, ms))


def wkv_chunks_pre(chunks, masks):
    ms, keep, eye, bd, eye2 = masks
    n = len(chunks)
    at, bt, kt, rt, v, cl = (list(t) for t in zip(*chunks))
    scores = [jnp.where(keep, mm_nt(wkv_score_stack(a, r, ms), rows(b, k)), 0.0)
              for a, r, b, k in zip(at, rt, bt, kt)]
    q = CHUNK
    aab = [s[h * q:(h + 1) * q, :q] for s in scores for h in range(2)]
    tinv = [eye + a for a in aab]
    power = [mm(a, a) for a in aab]
    for _ in range(5):
        both = [mm(rows(t, p), p) for t, p in zip(tinv, power)]
        tinv = [t + x[:q] for t, x in zip(tinv, both)]
        power = [x[q:] for x in both]
    pair = lambda c, row0, col0: lanes(scores[c][row0:row0 + q, col0:col0 + q],
                                       scores[c][row0 + q:row0 + 2 * q, col0:col0 + q])
    tinv_p = [lanes(tinv[2 * c], tinv[2 * c + 1]) for c in range(n)]
    aak_p = [pair(c, 0, q) for c in range(n)]
    prb_p = [pair(c, 2 * q, 0) for c in range(n)]
    prk_p = [pair(c, 2 * q, q) for c in range(n)]
    v_rows = [head_rows(x, ms) for x in v]
    wy = [mm(rows(a, p), x) for a, p, x in zip(aak_p, prk_p, v_rows)]
    w = [x[:q] for x in wy]
    yh2 = [x[q:] for x in wy]
    aw = [mm(t, lanes(head_rows(a, ms), head_rows(w_, ms))) for t, a, w_ in zip(tinv_p, at, w)]
    ah = [x[:, :PAIR] for x in aw]
    wh = [x[:, PAIR:] for x in aw]
    ry = [mm(p, lanes(head_rows(a, ms), head_rows(w_, ms))) for p, a, w_ in zip(prb_p, ah, wh)]
    rh = [r + x[:, :PAIR] for r, x in zip(rt, ry)]
    yh = [x[:, PAIR:] + y for x, y in zip(ry, yh2)]
    bc = [b * c_ for b, c_ in zip(bt, cl)]
    kc = [k * c_ for k, c_ in zip(kt, cl)]
    gh = [mm_tn(b, lanes(a, w_)) for b, a, w_ in zip(bc, ah, wh)]
    g = [eye2 * c_ + bd * x[:, :PAIR] for c_, x in zip(cl, gh)]
    h = [bd * (x[:, PAIR:] + mm_tn(k, v_)) for x, k, v_ in zip(gh, kc, v)]
    as_bf16 = lambda xs: [x.astype(BF16) for x in xs]
    saved = (as_bf16(tinv_p), as_bf16(aak_p), as_bf16(prb_p), as_bf16(prk_p), as_bf16(ah), wh)
    return g, h, rh, yh, saved


def wkv_chunks_grad(chunks, saved, m0, dy, dm1, masks):
    ms, keep, eye, bd, eye2 = masks
    n = len(chunks)
    q = CHUNK
    at, bt, kt, rt, v, cl = (list(t) for t in zip(*chunks))
    tinv_p, aak_p, prb_p, prk_p, ah, wh = (list(t) for t in zip(*saved))
    head_stack = lambda p: rows(p[:, :q], p[:, q:])
    bc = [b * c_ for b, c_ in zip(bt, cl)]
    kc = [k * c_ for k, c_ in zip(kt, cl)]
    u = [mm(a, m) + w for a, m, w in zip(ah, m0, wh)]
    dm1 = [d * bd for d in dm1]
    from_state = [mm(rows(b, k), d) for b, k, d in zip(bc, kc, dm1)]
    dy_rows = [head_rows(d, ms) for d in dy]
    from_out = [mm_tn(lanes(head_stack(pb), head_stack(pk)), d) for pb, pk, d in zip(prb_p, prk_p, dy_rows)]
    du = [a[:q] + b[:q] for a, b in zip(from_state, from_out)]
    dv = [a[q:] + b[q:] for a, b in zip(from_state, from_out)]
    dz = [mm_tn(head_stack(t), head_rows(d, ms)) for t, d in zip(tinv_p, du)]
    dz_rows = [head_rows(d, ms) for d in dz]
    dv = [a + mm_tn(head_stack(k), d) for a, k, d in zip(dv, aak_p, dz_rows)]
    by_m0 = [mm_nt(rows(d, z), m) for d, z, m in zip(dy, dz, m0)]
    uv = [rows(x, y) for x, y in zip(u, v)]
    by_dm1 = [mm_nt(x, d) for x, d in zip(uv, dm1)]
    udm = [x[:q] for x in by_dm1]
    vdm = [x[q:] for x in by_dm1]
    dscores = [jnp.where(keep, mm_nt(rows(z, d), x), 0.0) for z, d, x in zip(dz_rows, dy_rows, uv)]
    to_ar = [mm(d, rows(b, k)) for d, b, k in zip(dscores, bt, kt)]
    to_bk = [mm_tn(d, wkv_score_stack(a, r, ms)) for d, a, r in zip(dscores, at, rt)]
    ones = jnp.ones((8, PAIR), F32)
    upper = (lane_iota((CHUNK, CHUNK)) >= row_iota((CHUNK, CHUNK))).astype(F32)
    out = []
    for c in range(n):
        e = to_ar[c]
        dat_c = by_m0[c][q:] + e[:q] * ms[0] + e[q:2 * q] * ms[1]
        drt_c = by_m0[c][:q] + e[2 * q:3 * q] * ms[0] + e[3 * q:] * ms[1]
        dbt_c = udm[c] * cl[c] + to_bk[c][:q]
        dkt_c = vdm[c] * cl[c] + to_bk[c][q:]
        dlcl = ones_dot_nt(ones, dm1[c] * m0[c], 3)[0:1, :] * cl[c] + colsum(bc[c] * udm[c] + kc[c] * vdm[c])
        g = drt_c * rt[c] - dbt_c * bt[c] - dkt_c * kt[c] + dat_c * at[c]
        dlw = ones_dot(upper, g, 3) - dat_c * at[c] + dlcl
        out.append((dat_c, dbt_c, dkt_c, drt_c, dv[c], dlw))
    return out


def wkv_forward(at, bt, kt, rt, v, clf):
    n_rows = at.shape[0]
    cps = WKV_CHUNKS_PER_STEP
    rb = cps * CHUNK
    n_steps = n_rows // rb

    def body(a_ref, b_ref, k_ref, r_ref, v_ref, c_ref, y_ref, m0_ref, g_ref, rh_ref, *rest):
        saved_refs, m_scr = rest[:6], rest[6]

        @pl.when(pl.program_id(1) == 0)
        def _():
            m_scr[...] = jnp.zeros_like(m_scr)

        masks = wkv_masks()
        chunks = []
        for cc in range(cps):
            sl = slice(cc * CHUNK, (cc + 1) * CHUNK)
            chunks.append((a_ref[sl, :], b_ref[sl, :], k_ref[sl, :], r_ref[sl, :], v_ref[sl, :],
                           c_ref[cc * CHUNK:cc * CHUNK + 1, :]))
        gs, hs, rhs, yhs, saved = wkv_chunks_pre(chunks, masks)
        for ref, per_chunk in zip(saved_refs, saved):
            for cc, val in enumerate(per_chunk):
                ref[cc * CHUNK:(cc + 1) * CHUNK, :] = val
        m = m_scr[...]
        for cc, (g, h, rh, yh) in enumerate(zip(gs, hs, rhs, yhs)):
            sl = slice(cc * CHUNK, (cc + 1) * CHUNK)
            m0_ref[0, cc] = m
            g_ref[0, cc] = g
            rh_ref[sl, :] = rh
            y_ref[sl, :] = hdot(rh, m) + yh
            m = hdot(g, m) + h
        m_scr[...] = m

    blk = pl.BlockSpec((rb, PAIR), lambda p, s: (s, p))
    state_blk = pl.BlockSpec((1, cps, PAIR, PAIR), lambda p, s: (p, s, 0, 0))
    state_shape = jax.ShapeDtypeStruct((WIDTH // PAIR, n_rows // CHUNK, PAIR, PAIR), F32)
    rows_f32 = jax.ShapeDtypeStruct((n_rows, WIDTH), F32)
    rows_bf16 = jax.ShapeDtypeStruct((n_rows, WIDTH), BF16)
    return pl.pallas_call(
        body, name="wkv_forward", grid=(WIDTH // PAIR, n_steps),
        in_specs=[blk] * 6,
        out_specs=[blk, state_blk, state_blk, blk] + [blk] * 6,
        out_shape=[rows_f32, state_shape, state_shape, rows_f32] + [rows_bf16] * 5 + [rows_f32],
        scratch_shapes=[pltpu.VMEM((PAIR, PAIR), F32)],
        compiler_params=pltpu.CompilerParams(dimension_semantics=("arbitrary", "arbitrary"),
                                             vmem_limit_bytes=VMEM_LIMIT),
    )(at, bt, kt, rt, v, clf)


def wkv_backward(at, bt, kt, rt, v, clf, m0s, gs, rh, saved, dy):
    n_rows = at.shape[0]
    cps = WKV_CHUNKS_PER_STEP
    rb = cps * CHUNK
    n_steps = n_rows // rb

    def body(a_ref, b_ref, k_ref, r_ref, v_ref, c_ref, m0_ref, g_ref, rh_ref, *rest):
        saved_refs, dy_ref = rest[:6], rest[6]
        da_ref, db_ref, dk_ref, dr_ref, dv_ref, dlw_ref, dm_scr = rest[7:]

        @pl.when(pl.program_id(1) == 0)
        def _():
            dm_scr[...] = jnp.zeros_like(dm_scr)

        masks = wkv_masks()
        bd = masks[3]
        dm = dm_scr[...]
        dm1 = [None] * cps
        for cc in reversed(range(cps)):
            sl = slice(cc * CHUNK, (cc + 1) * CHUNK)
            dm1[cc] = dm
            dm = bd * (hdot_tn(g_ref[0, cc], dm) + hdot_tn(rh_ref[sl, :], dy_ref[sl, :]))
        dm_scr[...] = dm
        chunks, kept, m0, dys = [], [], [], []
        for cc in range(cps):
            sl = slice(cc * CHUNK, (cc + 1) * CHUNK)
            chunks.append((a_ref[sl, :], b_ref[sl, :], k_ref[sl, :], r_ref[sl, :], v_ref[sl, :],
                           c_ref[cc * CHUNK:cc * CHUNK + 1, :]))
            kept.append(tuple(ref[sl, :] for ref in saved_refs))
            m0.append(m0_ref[0, cc])
            dys.append(dy_ref[sl, :])
        grads = wkv_chunks_grad(chunks, kept, m0, dys, dm1, masks)
        for cc, (dat, dbt, dkt, drt, dv, dlw) in enumerate(grads):
            sl = slice(cc * CHUNK, (cc + 1) * CHUNK)
            da_ref[sl, :] = dat
            db_ref[sl, :] = dbt
            dk_ref[sl, :] = dkt
            dr_ref[sl, :] = drt
            dv_ref[sl, :] = dv
            dlw_ref[sl, :] = dlw

    blk = pl.BlockSpec((rb, PAIR), lambda p, s: (n_steps - 1 - s, p))
    state_blk = pl.BlockSpec((1, cps, PAIR, PAIR), lambda p, s: (p, n_steps - 1 - s, 0, 0))
    return pl.pallas_call(
        body, name="wkv_backward", grid=(WIDTH // PAIR, n_steps),
        in_specs=[blk] * 6 + [state_blk, state_blk, blk] + [blk] * 6 + [blk],
        out_specs=[blk] * 6,
        out_shape=[jax.ShapeDtypeStruct((n_rows, WIDTH), F32)] * 6,
        scratch_shapes=[pltpu.VMEM((PAIR, PAIR), F32)],
        compiler_params=pltpu.CompilerParams(dimension_semantics=("arbitrary", "arbitrary"),
                                             vmem_limit_bytes=VMEM_LIMIT),
    )(at, bt, kt, rt, v, clf, m0s, gs, rh, *saved, dy)


def visible(q_row0, k_row0, shape):
    qc = (q_row0 + row_iota(shape)) // CHUNK
    kc = (k_row0 + lane_iota(shape)) // CHUNK
    return kc <= qc


def attention_forward(q, k, v):
    n_rows = q.shape[0]
    tq, tk = ATTN_FWD_TILES
    n_q = n_rows // tq
    assert tk % tq == 0

    def body(q_ref, k_ref, v_ref, o_ref, lse_ref):
        i = pl.program_id(1)
        lane = lane_iota((tq, LANE))
        heads = [slice(0, LANE), slice(LANE, 2 * LANE)]
        qs = [q_ref[:, cols] for cols in heads]

        def step(j, carry, size, masked):
            rows = pl.ds(pl.multiple_of(j * size, size), size)
            ss = [mm_nt(qh, k_ref[rows, cols]) for qh, cols in zip(qs, heads)]
            if masked:
                vis = visible(i * tq, j * size, ss[0].shape)
                ss = [jnp.where(vis, s, -jnp.inf) for s in ss]
            ps, stats = [], []
            for s, (m, l, _) in zip(ss, carry):
                m_new = jnp.maximum(m, jnp.max(s, axis=-1, keepdims=True))
                p = jnp.exp2(s - m_new)
                alpha = jnp.exp2(m - m_new)
                ps.append(p)
                stats.append((m_new, alpha, alpha * l + jnp.sum(p, axis=-1, keepdims=True)))
            pvs = [mm(p, v_ref[rows, cols]) for p, cols in zip(ps, heads)]
            return tuple((m_new, l, alpha * acc + pv)
                         for (m_new, alpha, l), (_, _, acc), pv in zip(stats, carry, pvs))

        carry = tuple((jnp.full((tq, 1), -jnp.inf, F32), jnp.zeros((tq, 1), F32), jnp.zeros((tq, LANE), F32))
                      for _ in heads)
        n_full = (i * tq) // tk
        carry = lax.fori_loop(0, n_full, functools.partial(step, size=tk, masked=False), carry)
        (m0, l0, acc0), (m1, l1, acc1) = step(n_full, carry, size=tk, masked=True)
        o_ref[...] = acc0 / l0 + acc1 / l1
        lse_ref[...] = jnp.where(lane >= HEAD, m1 + jnp.log2(l1), m0 + jnp.log2(l0))

    return pl.pallas_call(
        body, name="attention_forward", grid=(HEADS // 2, n_q),
        in_specs=[pl.BlockSpec((tq, 2 * LANE), lambda p, i: (i, p)),
                  pl.BlockSpec((n_rows, 2 * LANE), lambda p, i: (0, p)),
                  pl.BlockSpec((n_rows, 2 * LANE), lambda p, i: (0, p))],
        out_specs=[pl.BlockSpec((tq, LANE), lambda p, i: (i, p))] * 2,
        out_shape=[jax.ShapeDtypeStruct((n_rows, WIDTH), F32)] * 2,
        compiler_params=pltpu.CompilerParams(dimension_semantics=("arbitrary", "arbitrary"),
                                             vmem_limit_bytes=VMEM_LIMIT),
    )(q, k, v)


def attention_backward(q, k, v, o, do, lse):
    n_rows = q.shape[0]
    tq, tk = ATTN_BWD_TILES
    n_q = n_rows // tq
    n_masked = max(1, tk // tq)

    def body(q_ref, k_ref, v_ref, o_ref, do_ref, lse_ref, dq_ref, dk_ref, dv_ref):
        j = pl.program_id(1)

        @pl.when(j == 0)
        def _():
            dq_ref[...] = jnp.zeros_like(dq_ref)

        lane = lane_iota((tq, LANE))
        heads = [slice(0, LANE), slice(LANE, 2 * LANE)]
        ks = [k_ref[:, cols] for cols in heads]
        vs = [v_ref[:, cols] for cols in heads]
        head_lanes = [(lane < HEAD).astype(F32), (lane >= HEAD).astype(F32)]

        def step(i, carry, masked):
            rows = pl.ds(pl.multiple_of(i * tq, tq), tq)
            qs = [q_ref[rows, cols] for cols in heads]
            dout = do_ref[rows, :]
            dout_o = dout * o_ref[rows, :]
            lse_t = lse_ref[rows, :]
            ss = [mm_nt(qh, kh) for qh, kh in zip(qs, ks)]
            dps = [mm_nt(dout, vh) for vh in vs]
            ps, dss = [], []
            for hh in range(2):
                delta = jnp.sum(dout_o * head_lanes[hh], axis=-1, keepdims=True)
                lse_h = jnp.sum(jnp.where(lane == hh * HEAD, lse_t, 0.0), axis=-1, keepdims=True)
                p = jnp.exp2(ss[hh] - lse_h)
                if masked:
                    p = jnp.where(visible(i * tq, j * tk, p.shape), p, 0.0)
                ps.append(p)
                dss.append(p * (dps[hh] - delta))
            dvs = [mm_tn(p, dout) for p in ps]
            dqs = [mm(ds, kh) for ds, kh in zip(dss, ks)]
            dks = [mm_tn(ds, qh) for ds, qh in zip(dss, qs)]
            for cols, dq in zip(heads, dqs):
                dq_ref[rows, cols] += dq * ATTN_SCALE
            return tuple((dk + a, dv + b) for (dk, dv), a, b in zip(carry, dks, dvs))

        carry = tuple((jnp.zeros((tk, LANE), F32), jnp.zeros((tk, LANE), F32)) for _ in heads)
        i_first = (j * tk) // tq
        for extra in range(n_masked):
            carry = step(i_first + extra, carry, masked=True)
        carry = lax.fori_loop(i_first + n_masked, n_q, functools.partial(step, masked=False), carry)
        for cols, (dk, dv) in zip(heads, carry):
            dk_ref[:, cols] = dk * (1.0 / LOG2_E)
            dv_ref[:, cols] = dv

    full = lambda w: pl.BlockSpec((n_rows, w), lambda p, j: (0, p))
    blk = pl.BlockSpec((tk, 2 * LANE), lambda p, j: (j, p))
    return pl.pallas_call(
        body, name="attention_backward", grid=(HEADS // 2, n_rows // tk),
        in_specs=[full(2 * LANE), blk, blk, full(LANE), full(LANE), full(LANE)],
        out_specs=[full(2 * LANE), blk, blk],
        out_shape=[jax.ShapeDtypeStruct((n_rows, HEADS * LANE), F32)] * 3,
        compiler_params=pltpu.CompilerParams(dimension_semantics=("arbitrary", "arbitrary"),
                                             vmem_limit_bytes=VMEM_LIMIT),
    )(q, k, v, o, do, lse)


def tail_tile(step0, tile0, x, tgt, ma, mb, gpa, gpb, ya, y, ur, k2, uv,
              mod, wpa, wpb, wout, gn_g, gn_b, r_k, post_g, post_b, bd):
    gate = mod[2:3]
    inv = 1.0 / HEAD
    yc = y - head_sum(y, bd) * inv
    rs = lax.rsqrt(head_sum(yc * yc, bd) * inv + GN_EPS)
    yn = yc * rs
    yb = yn * gn_g + gn_b + head_sum(ur * k2 * r_k, bd) * uv
    sga, sgb = sigmoid(gpa), sigmoid(gpb)
    sila, silb = gpa * sga, gpb * sgb
    ga, gb = ya * sila, yb * silb
    pa, pb = mm(ga, wpa), mm(gb, wpb)
    sa, sb = sigmoid(ma), sigmoid(mb)
    merged = sa * pa + sb * pb
    sub = mm(merged, wout)
    z = ALPHA * x + (1.0 + gate) * sub
    zhat, rstd = layer_norm_stats(z)
    err = zhat * post_g + post_b - tgt
    loss = 0.5 * jnp.sum(rowmean(err * err), axis=0, keepdims=True) + jnp.zeros((1, LANE), F32)
    dout = err * (1.0 / D_MODEL)
    dpost_g = colsum(dout * zhat)
    dpost_b = colsum(dout)
    dz = layer_norm_bwd(dout * post_g, zhat, rstd)
    dgate = colsum(dz * sub)
    dsub = dz * (1.0 + gate)
    dwout = mm_tn(merged, dsub)
    dmerged = mm_nt(dsub, wout)
    dpa, dpb = dmerged * sa, dmerged * sb
    dma = dmerged * pa * sa * (1.0 - sa)
    dmb = dmerged * pb * sb * (1.0 - sb)
    dwpa = mm_tn(ga, dpa)
    dwpb = mm_tn(gb, dpb)
    dga = mm_nt(dpa, wpa)
    dgb = mm_nt(dpb, wpb)
    dya = dga * sila
    dgpa = dga * ya * (sga * (1.0 + gpa * (1.0 - sga)))
    dyb = dgb * silb
    dgpb = dgb * yb * (sgb * (1.0 + gpb * (1.0 - sgb)))
    dgn_g = colsum(dyb * yn)
    dgn_b = colsum(dyb)
    dyn = dyb * gn_g
    dy = rs * (dyn - head_sum(dyn, bd) * inv - yn * head_sum(dyn * yn, bd) * inv)
    return (dz, dma, dmb, dgpa, dgpb, dya, dy, dyb,
            loss, dwout, dwpa, dwpb, dgn_g, dgn_b, dpost_g, dpost_b, dgate)


def mla_prep_bwd_tile(step0, tile0, q_c, kv_c, cos, sin, dq, dk, dv, gq, gkv, wq, wqr, wkn, wv):
    qn, qh, rq = rms_norm_fwd(q_c, gq)
    kvn, kvh, rkv = rms_norm_fwd(kv_c, gkv)
    dqc = dq * tile_lanes(cos, HEADS)
    dqs = dq * tile_lanes(sin, HEADS)
    dqn = mm_nt(dqc, wq) + mm_nt(dqs, wqr)
    dkvn = mm_nt(dk, wkn) + mm_nt(dv, wv)
    dkpe = dk[:, 0:LANE]
    for h in range(1, HEADS):
        dkpe = dkpe + dk[:, h * LANE:(h + 1) * LANE]
    dkr = dkpe * (cos * key_rope_mask(cos.shape))
    dkrr = dkpe * sin

    def rms_bwd(dyv, xh, r, g):
        dyg = dyv * g
        return r * (dyg - xh * rowmean(dyg * xh)), colsum(dyv * xh)

    dq_c, dgq = rms_bwd(dqn, qh, rq, gq)
    dkv_c, dgkv = rms_bwd(dkvn, kvh, rkv, gkv)
    return (dq_c, dkv_c, dkr, dkrr,
            mm_tn(qn, dqc), mm_tn(qn, dqs), mm_tn(kvn, dk), mm_tn(kvn, dv), dgq, dgkv)


def rwkv_prep_bwd_tile(step0, tile0, r0, k0, v0, l0, drt, dat, dbt, dkt, dvv, dlw, dyb, hr, hk, hv, hl,
                       mu_r, mu_k, mu_v, mu_l, w0, a0, k_k, k_a, w_dec, w_iclr, tril, same, bd, r_k,
                       cr, ck, cv, cl_):
    f = rwkv_prep_core(tile0, r0, k0, v0, l0, hr, hk, hv, hl, mu_r, mu_k, mu_v, mu_l, w0, a0, k_k, k_a,
                       w_dec, w_iclr, tril, same, bd)
    ur, uk, uv, ul, kk, k2, a_ic, sg, th = (f[n] for n in ("ur", "uk", "uv", "ul", "kk", "k2", "a_ic", "sg", "th"))
    lc, lw = f["lc"], f["lw"]
    e_neg = jnp.exp(-lc)
    dur = drt * jnp.exp(lc)
    da = dat * jnp.exp(lc - lw)
    db = dbt * e_neg
    dk2 = dkt * e_neg
    s = head_sum(ur * k2 * r_k, bd)
    duv = dvv + dyb * s
    ds = head_sum(dyb * uv, bd)
    dur = dur + ds * k2 * r_k
    dk2 = dk2 + ds * ur * r_k
    dr_k = colsum(ds * ur * k2)
    dkk = db * a_ic - da
    da_ic = db * kk + dk2 * uk * k_a
    duk = dk2 * (1.0 + (a_ic - 1.0) * k_a)
    dk_a = colsum(dk2 * uk * (a_ic - 1.0))
    dkkraw = jnp.where(f["nrm_raw"] > 1e-12, (dkk - kk * head_sum(dkk * kk, bd)) / f["nrm"], dkk * 1e12)
    duk = duk + dkkraw * k_k
    dk_k = colsum(dkkraw * uk)
    dai = da_ic * a_ic * (1.0 - a_ic)
    dd = dlw * (-DECAY_SCALE) * sg * (1.0 - sg)
    dul = mm_nt(dai, w_iclr) + mm_nt(dd, w_dec) * (1.0 - th * th)

    def unshift(du, x, prev, mu, carry_row):
        nxt = shift_rows_up(du, carry_row)
        return du * (1.0 - mu) + nxt * mu, colsum(du * (prev - x)), du[0:1, :]

    dr0, dmu_r, ncr = unshift(dur, r0, f["pr"], mu_r, cr)
    dk0, dmu_k, nck = unshift(duk, k0, f["pk"], mu_k, ck)
    dv0, dmu_v, ncv = unshift(duv, v0, f["pv"], mu_v, cv)
    dl0, dmu_l, ncl = unshift(dul, l0, f["pl"], mu_l, cl_)
    return (dr0, dk0, dv0, dl0,
            dmu_r, dmu_k, dmu_v, dmu_l, colsum(dd), colsum(dai), dk_k, dk_a, dr_k, mm_tn(th, dd), mm_tn(ul, dai),
            ncr, nck, ncv, ncl)


def in_backward(x, dz, pieces, mod, w_in_pt, unrot):
    n_rows = x.shape[0]
    ts = ROW_TILE
    n_p = len(pieces)
    shard_cols = IN_WIDTH // N_DEV

    def body(*refs):
        x_ref, dz_ref = refs[:2]
        p_refs = refs[2:2 + n_p]
        mod_ref, w_ref, unrot_ref = refs[2 + n_p:5 + n_p]
        dx_ref, ht_ref, blocks_ref, dshift_ref, dscale_ref = refs[5 + n_p:]
        step0 = pl.program_id(0) == 0
        dma, dmb, dr0, dk0, dv0, dgpa, dgpb, dq_c, dkv_c, dkr, dkrr, dl0 = (r[...] for r in p_refs)
        dproj = jnp.concatenate([dma, dmb, dr0, dk0, dv0, dgpa, dgpb, dq_c, dkv_c, dkr, dkrr, dl0], axis=1)
        dh = mm(dproj, w_ref[...])
        xhat, rstd = layer_norm_stats(x_ref[...])
        scale1 = 1.0 + mod_ref[1:2, :]
        dx_ref[...] = layer_norm_bwd(dh * scale1, xhat, rstd) + ALPHA * dz_ref[...]
        ht_ref[...] = jnp.transpose(xhat * scale1 + mod_ref[0:1, :]).astype(BF16)
        dkrope = (dkr.astype(F32) + mm(dkrr, unrot_ref[...]))[:, NOPE:QK_DIM]
        natural = jnp.concatenate(
            [dq_c.astype(F32), dkv_c.astype(F32), dkrope]
            + [p.astype(F32) for p in (dgpa, dr0, dk0, dv0, dl0, dgpb, dma, dmb)], axis=1)
        for j in range(N_DEV):
            blocks_ref[j] = natural[:, j * shard_cols:(j + 1) * shard_cols].astype(BF16)
        for ref, val in ((dshift_ref, colsum(dh)), (dscale_ref, colsum(dh * xhat))):
            @pl.when(step0)
            def _(ref=ref, val=val):
                ref[...] = val

            @pl.when(jnp.logical_not(step0))
            def _(ref=ref, val=val):
                ref[...] += val

    row = lambda w: pl.BlockSpec((ts, w), lambda i: (i, 0))
    const = pl.BlockSpec(memory_space=pltpu.VMEM)
    vec = pl.BlockSpec((1, D_MODEL), lambda i: (0, 0))
    return pl.pallas_call(
        body, name="in_backward", grid=(n_rows // ts,),
        in_specs=[row(D_MODEL), row(D_MODEL)] + [row(p.shape[1]) for p in pieces] + [const] * 3,
        out_specs=[row(D_MODEL), pl.BlockSpec((D_MODEL, ts), lambda i: (0, i)),
                   pl.BlockSpec((N_DEV, ts, shard_cols), lambda i: (0, i, 0)), vec, vec],
        out_shape=[jax.ShapeDtypeStruct((n_rows, D_MODEL), F32), jax.ShapeDtypeStruct((D_MODEL, n_rows), BF16),
                   jax.ShapeDtypeStruct((N_DEV, n_rows, shard_cols), BF16),
                   jax.ShapeDtypeStruct((1, D_MODEL), F32), jax.ShapeDtypeStruct((1, D_MODEL), F32)],
        compiler_params=pltpu.CompilerParams(dimension_semantics=("arbitrary",), vmem_limit_bytes=VMEM_LIMIT),
    )(x, dz, *pieces, mod, w_in_pt, unrot)


def in_weight_grad_exchange(h_t, dp_blocks, others, small, order):
    n = len(others)
    n_rows = h_t.shape[1]
    ts = 2 * ROW_TILE
    n_i = n_rows // ts
    shard_cols = dp_blocks.shape[2]
    n_chips = N_DEV // 2
    last = N_DEV - 1

    def body(order_ref, h_ref, dp_ref, *rest):
        g_refs, s_ref = rest[:n], rest[n]
        rwin_ref, rg_refs, rs_ref = rest[n + 1], rest[n + 2:2 * n + 2], rest[2 * n + 2]
        (acc, sendbuf, sib_buf, sib_send, sib_recv, win_send, win_recv,
         o_send, o_recv, local_sems) = rest[2 * n + 3:]
        b, i = pl.program_id(0), pl.program_id(1)
        me = my_position()
        mi = flat_index(me)
        sibling = (me[0], me[1], 1 - me[2])

        def other_copies(k, src_index, dst_index):
            peer = flip(me, k)
            out = [pltpu.make_async_remote_copy(
                src_ref=g_refs[a].at[src_index], dst_ref=rg_refs[a].at[dst_index],
                send_sem=o_send.at[(n + 1) * (k - 1) + a], recv_sem=o_recv.at[(n + 1) * (k - 1) + a],
                device_id=peer, device_id_type=MESH_IDS) for a in range(n)]
            out.append(pltpu.make_async_remote_copy(
                src_ref=s_ref, dst_ref=rs_ref.at[dst_index],
                send_sem=o_send.at[(n + 1) * (k - 1) + n], recv_sem=o_recv.at[(n + 1) * (k - 1) + n],
                device_id=peer, device_id_type=MESH_IDS))
            return out

        def local_copies():
            out = [pltpu.make_async_copy(g_refs[a].at[mi], rg_refs[a].at[mi], local_sems.at[a]) for a in range(n)]
            out.append(pltpu.make_async_copy(s_ref, rs_ref.at[mi], local_sems.at[n]))
            return out

        def to_sibling(t):
            return pltpu.make_async_remote_copy(
                src_ref=sendbuf.at[t], dst_ref=sib_buf.at[t], send_sem=sib_send.at[t], recv_sem=sib_recv.at[t],
                device_id=sibling, device_id_type=MESH_IDS)

        def to_owner(t):
            flip_x = (t < 2) * 1
            flip_y = 1 - (t & 1)
            owner = (me[0] ^ flip_x, me[1] ^ flip_y, me[2])
            return pltpu.make_async_remote_copy(
                src_ref=sendbuf.at[n_chips + t], dst_ref=rwin_ref.at[t], send_sem=win_send.at[t],
                recv_sem=win_recv.at[t], device_id=owner, device_id_type=MESH_IDS)

        own_block = pltpu.make_async_copy(sendbuf.at[last], rwin_ref.at[n_chips - 1], local_sems.at[n + 1])

        @pl.when(jnp.logical_and(b == 0, i == 0))
        def _():
            for cp in local_copies():
                cp.start()
            for k in range(1, N_DEV):
                for cp in other_copies(k, flat_index(flip(me, k)), mi):
                    cp.start()

        contrib = jnp.dot(h_ref[...], dp_ref[...], preferred_element_type=F32)

        @pl.when(i == 0)
        def _():
            acc[...] = contrib

        @pl.when(i > 0)
        def _():
            acc[...] += contrib

        slot = order_ref[N_DEV + b]
        t = slot & (n_chips - 1)

        @pl.when(jnp.logical_and(i == n_i - 1, slot < n_chips))
        def _():
            sendbuf[slot] = acc[...].astype(BF16)
            to_sibling(t).start()

        @pl.when(jnp.logical_and(i == n_i - 1, slot >= n_chips))
        def _():
            to_sibling(t).wait_recv()
            sendbuf[slot] = (acc[...] + sib_buf[t].astype(F32)).astype(BF16)

            @pl.when(slot < last)
            def _():
                to_owner(t).start()

            @pl.when(slot == last)
            def _():
                own_block.start()

        @pl.when(jnp.logical_and(b == last, i == n_i - 1))
        def _():
            for t in range(n_chips - 1):
                to_owner(t).wait_recv()
            for k in range(1, N_DEV):
                pi = flat_index(flip(me, k))
                for cp in other_copies(k, pi, pi):
                    cp.wait_recv()
            for t in range(n_chips):
                to_sibling(t).wait_send()
            for t in range(n_chips - 1):
                to_owner(t).wait_send()
            for k in range(1, N_DEV):
                for cp in other_copies(k, flat_index(flip(me, k)), mi):
                    cp.wait_send()
            for cp in local_copies():
                cp.wait()
            own_block.wait()

    hbm = pl.BlockSpec(memory_space=pl.ANY)
    n_sem = 7 * (n + 1)
    grid_spec = pltpu.PrefetchScalarGridSpec(
        num_scalar_prefetch=1, grid=(N_DEV, n_i),
        in_specs=[pl.BlockSpec((D_MODEL, ts), lambda b, i, order: (0, i)),
                  pl.BlockSpec((None, ts, shard_cols), lambda b, i, order: (order[b], i, 0))] + [hbm] * (n + 1),
        out_specs=[hbm] * (n + 2),
        scratch_shapes=[pltpu.VMEM((D_MODEL, shard_cols), F32), pltpu.VMEM((N_DEV, D_MODEL, shard_cols), BF16),
                        pltpu.VMEM((n_chips, D_MODEL, shard_cols), BF16),
                        pltpu.SemaphoreType.DMA((n_chips,)), pltpu.SemaphoreType.DMA((n_chips,)),
                        pltpu.SemaphoreType.DMA((n_chips - 1,)), pltpu.SemaphoreType.DMA((n_chips - 1,)),
                        pltpu.SemaphoreType.DMA((n_sem,)), pltpu.SemaphoreType.DMA((n_sem,)),
                        pltpu.SemaphoreType.DMA((n + 2,))])
    return pl.pallas_call(
        body, name="in_weight_grad_exchange", grid_spec=grid_spec,
        out_shape=[jax.ShapeDtypeStruct((n_chips, D_MODEL, shard_cols), BF16)]
        + [jax.ShapeDtypeStruct(o.shape, o.dtype) for o in others]
        + [jax.ShapeDtypeStruct((N_DEV,) + small.shape, small.dtype)],
        compiler_params=pltpu.CompilerParams(dimension_semantics=("arbitrary", "arbitrary"),
                                             vmem_limit_bytes=VMEM_LIMIT),
    )(order, h_t, dp_blocks, *others, small)


def ada_weight_grad(c_all, dmod_cols):
    def body(c_ref, d_ref, o_ref):
        cv = c_ref[...]
        o_ref[...] = hdot_tn(cv * sigmoid(cv), d_ref[...])

    return pl.pallas_call(
        body, name="ada_weight_grad",
        out_shape=jax.ShapeDtypeStruct((c_all.shape[1], dmod_cols.shape[1]), F32),
    )(c_all, dmod_cols)


def adamw_update(g, w, m, v):
    nm = ADAM_B1 * m + (1.0 - ADAM_B1) * g
    nv = ADAM_B2 * v + (1.0 - ADAM_B2) * (g * g)
    m_hat = nm / (1.0 - ADAM_B1 ** ADAM_STEP)
    v_hat = nv / (1.0 - ADAM_B2 ** ADAM_STEP)
    return -ADAM_LR * (m_hat / (jnp.sqrt(v_hat) + ADAM_EPS) + ADAM_WD * w), nm, nv


def adamw(parts, w, m, v, name):
    k, rows, cols = parts.shape
    rb = 128 if rows % 128 == 0 else rows

    def body(p_ref, w_ref, m_ref, v_ref, g_ref, d_ref, nm_ref, nv_ref):
        g = p_ref[0].astype(F32)
        for i in range(1, k):
            g = g + p_ref[i].astype(F32)
        g_ref[0] = g
        d_ref[0], nm_ref[0], nv_ref[0] = adamw_update(g, w_ref[0], m_ref[0], v_ref[0])

    blk = pl.BlockSpec((1, rb, cols), lambda i: (0, i, 0))
    return pl.pallas_call(
        body, name=name, grid=(rows // rb,),
        in_specs=[pl.BlockSpec((k, rb, cols), lambda i: (0, i, 0)), blk, blk, blk],
        out_specs=[blk] * 4, out_shape=[jax.ShapeDtypeStruct((1, rows, cols), F32)] * 4,
        compiler_params=pltpu.CompilerParams(dimension_semantics=("arbitrary",), vmem_limit_bytes=VMEM_LIMIT),
    )(parts, w, m, v)


def adamw_small(parts, ws, ms, vs):
    k = parts.shape[0]
    n = len(ws)
    sizes = [w.shape[1] for w in ws]

    def body(p_ref, *refs):
        ins, outs = refs[:3 * n], refs[3 * n:]
        g_all = p_ref[0]
        for i in range(1, k):
            g_all = g_all + p_ref[i]
        off = 0
        for a, size in enumerate(sizes):
            g = g_all[:, off:off + size]
            off += size
            d, nm, nv = adamw_update(g, ins[a][...], ins[n + a][...], ins[2 * n + a][...])
            for kind, val in enumerate((g, d, nm, nv)):
                outs[kind * n + a][...] = val

    return pl.pallas_call(
        body, name="adamw_small",
        out_shape=[jax.ShapeDtypeStruct((1, size), F32) for _ in range(4) for size in sizes],
    )(parts, *ws, *ms, *vs)


def rot_cols(w):
    return jnp.concatenate([-w[:, ROPE // 2:], w[:, :ROPE // 2]], axis=1)


def unrot_cols(dw):
    return jnp.concatenate([dw[:, ROPE // 2:], -dw[:, :ROPE // 2]], axis=1)


def columns_from_shards(g, rows, cols):
    return g.reshape(N_DEV, rows, cols).transpose(1, 0, 2).reshape(rows, N_DEV * cols)


def shards_from_columns(w, rows, cols):
    return w.reshape(rows, N_DEV, cols).transpose(1, 0, 2).reshape(N_DEV, rows * cols)


def permute_w_in_t(wt):
    z = lambda n: jnp.zeros((n, D_MODEL), wt.dtype)
    krope = wt[N_KROPE:N_KROPE + ROPE]
    krope_rot = jnp.concatenate([-krope[ROPE // 2:], krope[:ROPE // 2]], axis=0)
    rw = N_RWKV
    return jnp.concatenate([
        wt[N_MA:N_MA + 1024], wt[N_MB:N_MB + 1024],
        wt[rw:rw + 512], wt[rw + 512:rw + 1024], wt[rw + 1024:rw + 1536],
        wt[N_GPA:N_GPA + 512], wt[N_GPB:N_GPB + 512],
        wt[N_QC:N_QC + 256], wt[N_KVC:N_KVC + 128],
        z(NOPE), krope, z(LANE - QK_DIM), z(NOPE), krope_rot, z(LANE - QK_DIM),
        wt[rw + 1536:rw + 1664]], axis=0)


def unpermute_w_in_grad(d):
    rw = P_R
    krope = d[:, P_KR + NOPE:P_KR + QK_DIM] + unrot_cols(d[:, P_KRR + NOPE:P_KRR + QK_DIM])
    return jnp.concatenate([
        d[:, P_QC:P_QC + 256], d[:, P_KVC:P_KVC + 128], krope, d[:, P_GPA:P_GPA + 512],
        d[:, rw:rw + 1536], d[:, P_LORA:P_LORA + 128], d[:, P_GPB:P_GPB + 512],
        d[:, P_MA:P_MA + 1024], d[:, P_MB:P_MB + 1024]], axis=1)


def pad_heads_q(w_uq):
    w = w_uq.reshape(Q_RANK, HEADS, QK_DIM)
    zpad = jnp.zeros((Q_RANK, HEADS, LANE - QK_DIM), w.dtype)
    wq = jnp.concatenate([w, zpad], axis=2).reshape(Q_RANK, HEADS * LANE)
    pe = w[:, :, NOPE:]
    rot = jnp.concatenate([-pe[:, :, ROPE // 2:], pe[:, :, :ROPE // 2]], axis=2)
    wqr = jnp.concatenate([jnp.zeros((Q_RANK, HEADS, NOPE), w.dtype), rot, zpad], axis=2).reshape(Q_RANK, HEADS * LANE)
    return wq, wqr


def unpad_heads_q_grad(dwq, dwqr):
    a = dwq.reshape(Q_RANK, HEADS, LANE)
    r = dwqr.reshape(Q_RANK, HEADS, LANE)[:, :, NOPE:QK_DIM]
    pe = a[:, :, NOPE:QK_DIM] + jnp.concatenate([r[:, :, ROPE // 2:], -r[:, :, :ROPE // 2]], axis=2)
    return jnp.concatenate([a[:, :, :NOPE], pe], axis=2).reshape(Q_RANK, HEADS * QK_DIM)


def pad_heads_kv(w_ukv):
    w = w_ukv.reshape(KV_RANK, HEADS, 2 * HEAD)
    z = jnp.zeros((KV_RANK, HEADS, HEAD), w.dtype)
    wkn = jnp.concatenate([w[:, :, :NOPE], z], axis=2).reshape(KV_RANK, HEADS * LANE)
    val = w[:, :, NOPE:]
    odd = (jnp.arange(HEADS) % 2 == 1)[None, :, None]
    wv = jnp.concatenate([jnp.where(odd, 0, val), jnp.where(odd, val, 0)], axis=2).reshape(KV_RANK, HEADS * LANE)
    return wkn, wv


def unpad_heads_kv_grad(dwkn, dwv):
    a = dwkn.reshape(KV_RANK, HEADS, LANE)[:, :, :NOPE]
    b = dwv.reshape(KV_RANK, HEADS, LANE)
    odd = (jnp.arange(HEADS) % 2 == 1)[None, :, None]
    val = jnp.where(odd, b[:, :, HEAD:], b[:, :, :HEAD])
    return jnp.concatenate([a, val], axis=2).reshape(KV_RANK, HEADS * 2 * HEAD)


def kernel(x, c, positions, w_ada, b_ada, w_in, q_norm_g, w_uq, kv_norm_g, w_ukv, mu_rwkv, w0, w_decay_up, a0, w_iclr_up, k_k, k_a, r_k, gn_g, gn_b, w_proj_a, w_proj_b, w_out, post_g, post_b, loss_target, m_w_ada, m_b_ada, m_w_in, m_q_norm_g, m_w_uq, m_kv_norm_g, m_w_ukv, m_mu_rwkv, m_w0, m_w_decay_up, m_a0, m_w_iclr_up, m_k_k, m_k_a, m_r_k, m_gn_g, m_gn_b, m_w_proj_a, m_w_proj_b, m_w_out, m_post_g, m_post_b, v_w_ada, v_b_ada, v_w_in, v_q_norm_g, v_w_uq, v_kv_norm_g, v_w_ukv, v_mu_rwkv, v_w0, v_w_decay_up, v_a0, v_w_iclr_up, v_k_k, v_k_a, v_r_k, v_gn_g, v_gn_b, v_w_proj_a, v_w_proj_b, v_w_out, v_post_g, v_post_b):
    weights = dict(w_ada=w_ada, b_ada=b_ada, w_in=w_in, q_norm_g=q_norm_g, w_uq=w_uq, kv_norm_g=kv_norm_g,
                   w_ukv=w_ukv, mu_rwkv=mu_rwkv, w0=w0, w_decay_up=w_decay_up, a0=a0, w_iclr_up=w_iclr_up,
                   k_k=k_k, k_a=k_a, r_k=r_k, gn_g=gn_g, gn_b=gn_b, w_proj_a=w_proj_a, w_proj_b=w_proj_b,
                   w_out=w_out, post_g=post_g, post_b=post_b)
    mom1 = dict(w_ada=m_w_ada, b_ada=m_b_ada, w_in=m_w_in, q_norm_g=m_q_norm_g, w_uq=m_w_uq, kv_norm_g=m_kv_norm_g,
                w_ukv=m_w_ukv, mu_rwkv=m_mu_rwkv, w0=m_w0, w_decay_up=m_w_decay_up, a0=m_a0, w_iclr_up=m_w_iclr_up,
                k_k=m_k_k, k_a=m_k_a, r_k=m_r_k, gn_g=m_gn_g, gn_b=m_gn_b, w_proj_a=m_w_proj_a, w_proj_b=m_w_proj_b,
                w_out=m_w_out, post_g=m_post_g, post_b=m_post_b)
    mom2 = dict(w_ada=v_w_ada, b_ada=v_b_ada, w_in=v_w_in, q_norm_g=v_q_norm_g, w_uq=v_w_uq, kv_norm_g=v_kv_norm_g,
                w_ukv=v_w_ukv, mu_rwkv=v_mu_rwkv, w0=v_w0, w_decay_up=v_w_decay_up, a0=v_a0, w_iclr_up=v_w_iclr_up,
                k_k=v_k_k, k_a=v_k_a, r_k=v_r_k, gn_g=v_gn_g, gn_b=v_gn_b, w_proj_a=v_w_proj_a, w_proj_b=v_w_proj_b,
                w_out=v_w_out, post_g=v_post_g, post_b=v_post_b)
    names = list(weights)
    n_rows = x.shape[1]
    me = 4 * lax.axis_index("x") + 2 * lax.axis_index("y") + lax.axis_index("c")
    xr = x[0]
    tgt = loss_target[0]
    row = lambda a: a.reshape(1, -1)

    w_in_all, c_all = gather_shards([w_in[0].T.astype(BF16), c])
    c_all = c_all.reshape(N_DEV, D_MODEL)
    w_in_pt = permute_w_in_t(w_in_all.reshape(IN_WIDTH, D_MODEL))

    mod_all = ada_modulation(c_all, w_ada[0], b_ada.reshape(N_DEV, -1))
    mod = lax.dynamic_index_in_dim(mod_all, me, axis=1, keepdims=False).reshape(3, D_MODEL)

    proj, *gathered = fwd_in_gather(xr, mod, w_in_pt, [weights[n][0].astype(BF16) for n, _, _ in SHARDED[1:]])
    pcol = lambda off_, w: (proj, w, off_ // w)
    full = {}
    for (n, r, cdim), part in zip(SHARDED[1:], gathered):
        full[n] = part.reshape(N_DEV * r, cdim) if n == "w_out" else columns_from_shards(part, r, cdim)
    wq, wqr = pad_heads_q(full["w_uq"])
    wkn, wv = pad_heads_kv(full["w_ukv"])
    zl = jnp.zeros((LORA, WIDTH), BF16)
    w_dec = jnp.concatenate([full["w_decay_up"], zl], axis=0)
    w_iclr = jnp.concatenate([zl, full["w_iclr_up"]], axis=0)
    wpa, wpb, wout = full["w_proj_a"], full["w_proj_b"], full["w_out"]

    inv_freq = ROPE_THETA ** (-jnp.arange(0, ROPE, 2, dtype=F32) / ROPE)
    ang = positions[0].astype(F32)[:, None] * inv_freq
    ones_n, zeros_n, zeros_p = jnp.ones((n_rows, NOPE), F32), jnp.zeros((n_rows, NOPE), F32), jnp.zeros((n_rows, LANE - QK_DIM), F32)
    cos_t = jnp.concatenate([ones_n, jnp.cos(ang), jnp.cos(ang), zeros_p], axis=1)
    sin_t = jnp.concatenate([zeros_n, jnp.sin(ang), jnp.sin(ang), zeros_p], axis=1)

    gq, gkv = q_norm_g, kv_norm_g
    mla_consts = [gq, gkv, wq, wqr, wkn, wv]
    q, k, v = row_call(
        "mla_prep", mla_prep_tile, n_rows,
        [pcol(P_QC, 256), pcol(P_KVC, 128), pcol(P_KR, 128), pcol(P_KRR, 128), (cos_t, LANE, 0), (sin_t, LANE, 0)],
        mla_consts, [(HEADS * LANE, BF16)] * 3)
    ya, lse = attention_forward(q, k, v)

    t_idx = jnp.arange(ROW_TILE)
    same_chunk = (t_idx[:, None] // CHUNK) == (t_idx[None, :] // CHUNK)
    same = same_chunk.astype(F32)
    tril = (same_chunk & (t_idx[:, None] >= t_idx[None, :])).astype(F32)
    l_idx = jnp.arange(LANE)
    bd = ((l_idx[:, None] // HEAD) == (l_idx[None, :] // HEAD)).astype(F32)
    mu = mu_rwkv
    mu_r, mu_k, mu_v, mu_l = mu[:, 0:512], mu[:, 512:1024], mu[:, 1024:1536], mu[:, 1536:1664]
    rk_row = row(r_k)
    rwkv_consts = [mu_r, mu_k, mu_v, mu_l, w0, a0, k_k, k_a, w_dec, w_iclr, tril, same, bd]
    rwkv_rows = [pcol(P_R, 512), pcol(P_K, 512), pcol(P_V, 512), pcol(P_LORA, 128)]
    rt, at, bt, kt, clf, uv, ur, k2 = row_call(
        "rwkv_prep", rwkv_prep_tile, n_rows, rwkv_rows, rwkv_consts, [(WIDTH, F32)] * 8, halo_in=rwkv_rows)
    y, m0s, state_maps, out_maps, *wkv_saved = wkv_forward(at, bt, kt, rt, uv, clf)

    tail = row_call(
        "tail", tail_tile, n_rows,
        [(xr, D_MODEL, 0), (tgt, D_MODEL, 0), pcol(P_MA, 1024), pcol(P_MB, 1024), pcol(P_GPA, 512), pcol(P_GPB, 512),
         (ya, WIDTH, 0), (y, WIDTH, 0), (ur, WIDTH, 0), (k2, WIDTH, 0), (uv, WIDTH, 0)],
        [mod, wpa, wpb, wout, gn_g, gn_b, rk_row, post_g, post_b, bd],
        [(D_MODEL, F32), (1024, BF16), (1024, BF16), (512, BF16), (512, BF16), (WIDTH, F32), (WIDTH, F32), (WIDTH, F32)],
        acc_out=[((1, LANE), F32), ((D_MODEL, D_MODEL), F32), ((WIDTH, D_MODEL), F32), ((WIDTH, D_MODEL), F32),
                 ((1, WIDTH), F32), ((1, WIDTH), F32), ((1, D_MODEL), F32), ((1, D_MODEL), F32), ((1, D_MODEL), F32)])
    (dz, dma, dmb, dgpa, dgpb, dya, dy, dyb,
     loss_row, g_wout, g_wpa, g_wpb, g_gn_g, g_gn_b, g_post_g, g_post_b, dgate) = tail

    dq, dk, dv = attention_backward(q, k, v, ya, dya, lse)
    dq_c, dkv_c, dkr, dkrr, g_wq, g_wqr, g_wkn, g_wv, g_gq, g_gkv = row_call(
        "mla_prep_bwd", mla_prep_bwd_tile, n_rows,
        [pcol(P_QC, 256), pcol(P_KVC, 128), (cos_t, LANE, 0), (sin_t, LANE, 0),
         (dq, HEADS * LANE, 0), (dk, HEADS * LANE, 0), (dv, HEADS * LANE, 0)],
        mla_consts, [(256, BF16), (128, BF16), (128, BF16), (128, BF16)],
        acc_out=[((Q_RANK, HEADS * LANE), F32)] * 2 + [((KV_RANK, HEADS * LANE), F32)] * 2
        + [((1, Q_RANK), F32), ((1, KV_RANK), F32)])

    dat, dbt, dkt, drt, dvv, dlw = wkv_backward(at, bt, kt, rt, uv, clf, m0s, state_maps, out_maps, wkv_saved, dy)
    (dr0, dk0, dv0, dl0, g_mu_r, g_mu_k, g_mu_v, g_mu_l, g_w0, g_a0, g_k_k, g_k_a, g_r_k, g_wdec, g_wiclr) = row_call(
        "rwkv_prep_bwd", rwkv_prep_bwd_tile, n_rows,
        rwkv_rows + [(drt, WIDTH, 0), (dat, WIDTH, 0), (dbt, WIDTH, 0), (dkt, WIDTH, 0), (dvv, WIDTH, 0),
                     (dlw, WIDTH, 0), (dyb, WIDTH, 0)],
        rwkv_consts + [rk_row], [(512, BF16), (512, BF16), (512, BF16), (128, BF16)],
        acc_out=[((1, 512), F32)] * 3 + [((1, 128), F32)] + [((1, 512), F32)] * 5 + [((LANE, WIDTH), F32)] * 2,
        halo_in=rwkv_rows, carry=[512, 512, 512, 128], reverse=True)

    li = jnp.arange(LANE)
    src, dst = li[:, None], li[None, :]
    half = ROPE // 2
    unrot = (jnp.where((dst >= NOPE) & (dst < NOPE + half) & (src == dst + half), 1.0, 0.0)
             - jnp.where((dst >= NOPE + half) & (dst < QK_DIM) & (src == dst - half), 1.0, 0.0)).astype(BF16)
    dx, h_t, dproj_blocks, dshift, dscale = in_backward(
        xr, dz, [dma, dmb, dr0, dk0, dv0, dgpa, dgpb, dq_c, dkv_c, dkr, dkrr, dl0], mod, w_in_pt, unrot)

    grads_full = {
        "w_uq": unpad_heads_q_grad(g_wq, g_wqr), "w_ukv": unpad_heads_kv_grad(g_wkn, g_wv),
        "w_decay_up": g_wdec[:LORA], "w_iclr_up": g_wiclr[LORA:],
        "w_proj_a": g_wpa, "w_proj_b": g_wpb, "w_out": g_wout}
    blocks = [(grads_full[n].reshape(N_DEV, r, cdim) if n == "w_out"
               else grads_full[n].reshape(r, N_DEV, cdim).transpose(1, 0, 2)).astype(BF16) for n, r, cdim in SHARDED[1:]]
    dmod = jnp.concatenate([dshift, dscale, dgate], axis=1)
    small = jnp.concatenate([dmod, g_gq, g_gkv, g_mu_r, g_mu_k, g_mu_v, g_mu_l, g_w0, g_a0, g_k_k, g_k_a, g_r_k,
                             g_gn_g, g_gn_b, g_post_g, g_post_b, loss_row], axis=1)
    my_x, my_y, my_c = lax.axis_index("x"), lax.axis_index("y"), lax.axis_index("c")
    chip_order = [4 * (my_x ^ fx) + 2 * (my_y ^ fy) for fx, fy in ((1, 1), (1, 0), (0, 1), (0, 0))]
    owners = [chip_order[s % 4] + (my_c if s >= 4 else 1 - my_c) for s in WGRAD_SLOTS]
    order = jnp.stack(owners + [jnp.int32(s) for s in WGRAD_SLOTS]).astype(jnp.int32)
    *got_blocks, got_small = in_weight_grad_exchange(h_t, dproj_blocks, blocks, small, order)
    loss = jnp.sum(got_small[:, 0, SMALL_ELEMS])

    ada_cols = w_ada.shape[2]
    dmod_all = got_small[:, 0, :3 * D_MODEL]
    g_ada = ada_weight_grad(c_all, lax.dynamic_slice_in_dim(dmod_all, me * ada_cols, ada_cols, axis=1))

    outs = [dict() for _ in range(4)]
    res = adamw(g_ada[None], w_ada, m_w_ada, v_w_ada, "adamw_w_ada")
    for kind in range(4):
        outs[kind]["w_ada"] = res[kind]
    for (n, r, cdim), got in zip(SHARDED, got_blocks):
        res = adamw(got, weights[n], mom1[n], mom2[n], "adamw_" + n)
        for kind in range(4):
            outs[kind][n] = res[kind]
    rows_of = lambda tree: [tree[n].reshape(1, -1) for n, _ in SMALL]
    res = adamw_small(got_small, rows_of(weights), rows_of(mom1), rows_of(mom2))
    for kind in range(4):
        for a, (n, _) in enumerate(SMALL):
            outs[kind][n] = res[kind * len(SMALL) + a].reshape(weights[n].shape)
    return (loss, dx[None], *[outs[0][n] for n in names], *[outs[1][n] for n in names],
            *[outs[2][n] for n in names], *[outs[3][n] for n in names])
```

```python
import functools
import math

import jax
import jax.numpy as jnp
from jax import lax
from jax.experimental import pallas as pl
from jax.experimental.pallas import tpu as pltpu

F32 = jnp.float32
BF16 = jnp.bfloat16
HIGHEST = lax.Precision.HIGHEST
MESH_IDS = pl.DeviceIdType.MESH

N_DEV = 8
D_MODEL = 1024
LN_EPS = 1e-5
RMS_EPS = 1e-6
GN_EPS = 64e-5
HEADS = 8
Q_RANK = 256
KV_RANK = 128
ROPE = 32
NOPE = 64
QK_DIM = NOPE + ROPE
WIDTH = 512
HEAD = 64
LORA = 64
CHUNK = 64
DEPTH = 1
ALPHA = (2.0 * DEPTH) ** 0.25
ROPE_THETA = 10000.0
ATTN_SCALE = QK_DIM ** -0.5
DECAY_SCALE = math.exp(-0.5)

ADAM_LR = 0.001
ADAM_B1 = 0.9
ADAM_B2 = 0.999
ADAM_EPS = 1e-08
ADAM_WD = 0.01
ADAM_STEP = 10

LANE = 128
PAIR = 2 * HEAD
ROW_TILE = 256
HALO_ROWS = 16
ATTN_FWD_TILES = (512, 1024)
ATTN_BWD_TILES = (512, 512)
LOG2_E = math.log2(math.e)
Q_PRESCALE = ATTN_SCALE * LOG2_E
WKV_CHUNKS_PER_STEP = 8
WGRAD_SLOTS = (0, 1, 4, 2, 5, 6, 3, 7)
VMEM_LIMIT = 56 * 1024 * 1024

P_MA, P_MB, P_R, P_K, P_V, P_GPA, P_GPB, P_QC, P_KVC, P_KR, P_KRR, P_LORA = (
    0, 1024, 2048, 2560, 3072, 3584, 4096, 4608, 4864, 4992, 5120, 5248)
P_WIDTH = 5376

N_QC, N_KVC, N_KROPE, N_GPA, N_RWKV, N_GPB, N_MA, N_MB = 0, 256, 384, 416, 928, 2592, 3104, 4128
IN_WIDTH = 5152

SHARDED = (("w_in", 1024, 644), ("w_uq", 256, 96), ("w_ukv", 128, 128), ("w_decay_up", 64, 64),
           ("w_iclr_up", 64, 64), ("w_proj_a", 512, 128), ("w_proj_b", 512, 128), ("w_out", 128, 1024))
SMALL = (("b_ada", 3072), ("q_norm_g", 256), ("kv_norm_g", 128), ("mu_rwkv", 1664), ("w0", 512), ("a0", 512),
         ("k_k", 512), ("k_a", 512), ("r_k", 512), ("gn_g", 512), ("gn_b", 512), ("post_g", 1024), ("post_b", 1024))
SMALL_ELEMS = sum(n for _, n in SMALL)


def mm(a, b):
    return jnp.dot(a.astype(BF16), b.astype(BF16), preferred_element_type=F32)


def mm_nt(a, b):
    return lax.dot_general(a.astype(BF16), b.astype(BF16), (((1,), (1,)), ((), ())), preferred_element_type=F32)


def mm_tn(a, b):
    return lax.dot_general(a.astype(BF16), b.astype(BF16), (((0,), (0,)), ((), ())), preferred_element_type=F32)


def hdot(a, b):
    return jnp.dot(a, b, precision=HIGHEST, preferred_element_type=F32)


def hdot_tn(a, b):
    return lax.dot_general(a, b, (((0,), (0,)), ((), ())), precision=HIGHEST, preferred_element_type=F32)


def sigmoid(x):
    return 1.0 / (1.0 + jnp.exp(-x))


def colsum(x):
    return jnp.sum(x, axis=0, keepdims=True)


def rowmean(x):
    return jnp.mean(x, axis=-1, keepdims=True)


def layer_norm_stats(x):
    xc = x - rowmean(x)
    rstd = lax.rsqrt(rowmean(xc * xc) + LN_EPS)
    return xc * rstd, rstd


def layer_norm_bwd(dy, xhat, rstd):
    return rstd * (dy - rowmean(dy) - xhat * rowmean(dy * xhat))


def bf16_pieces(x, n):
    pieces = []
    for _ in range(n):
        p = x.astype(BF16)
        pieces.append(p)
        x = x - p.astype(F32)
    return pieces


def ones_dot(ones, x, n_pieces):
    ones = ones.astype(BF16)
    return sum(jnp.dot(ones, p, preferred_element_type=F32) for p in bf16_pieces(x, n_pieces))


def ones_dot_nt(ones, x, n_pieces):
    ones = ones.astype(BF16)
    return sum(lax.dot_general(ones, p, (((1,), (1,)), ((), ())), preferred_element_type=F32)
               for p in bf16_pieces(x, n_pieces))


def head_sum(x, bd):
    return jnp.concatenate([mm(x[:, p * LANE:(p + 1) * LANE], bd) for p in range(x.shape[1] // LANE)], axis=1)


def tile_lanes(t, n):
    return jnp.concatenate([t] * n, axis=1)


def row_iota(shape):
    return lax.broadcasted_iota(jnp.int32, shape, 0)


def lane_iota(shape):
    return lax.broadcasted_iota(jnp.int32, shape, 1)


def shift_rows_down(x, row0):
    rolled = pltpu.roll(x, 1, axis=0)
    return jnp.where(row_iota(x.shape) == 0, row0, rolled)


def shift_rows_up(x, row_last):
    rolled = pltpu.roll(x, x.shape[0] - 1, axis=0)
    return jnp.where(row_iota(x.shape) == x.shape[0] - 1, row_last, rolled)


def row_call(name, fn, n_rows, row_in, const_in, row_out, acc_out=(), halo_in=(), carry=(), reverse=False):
    ts = ROW_TILE
    n_tiles = n_rows // ts
    n_in = len(row_in) + len(halo_in) + len(const_in)
    n_ro, n_ao = len(row_out), len(acc_out)

    def tile_of(g):
        return (n_tiles - 1 - g) if reverse else g

    def body(*refs):
        ins = refs[:n_in]
        ro = refs[n_in:n_in + n_ro]
        ao = refs[n_in + n_ro:n_in + n_ro + n_ao]
        cr = refs[n_in + n_ro + n_ao:]
        g = pl.program_id(0)
        step0 = g == 0
        tile0 = tile_of(g) == 0
        for r in cr:
            @pl.when(step0)
            def _(r=r):
                r[...] = jnp.zeros_like(r)
        n_tiled = len(row_in) + len(halo_in)
        vals = [r[...].astype(F32) for r in ins[:n_tiled]] + [r[...] for r in ins[n_tiled:]]
        outs = fn(step0, tile0, *vals, *[c[0:1, :] for c in cr])
        for r, v in zip(ro, outs[:n_ro]):
            r[...] = v.astype(r.dtype)
        for r, v in zip(ao, outs[n_ro:n_ro + n_ao]):
            @pl.when(step0)
            def _(r=r, v=v):
                r[...] = v.astype(r.dtype)

            @pl.when(jnp.logical_not(step0))
            def _(r=r, v=v):
                r[...] += v.astype(r.dtype)
        for r, v in zip(cr, outs[n_ro + n_ao:]):
            r[0:1, :] = v

    in_specs = [pl.BlockSpec((ts, w), functools.partial(lambda g, cb: (tile_of(g), cb), cb=cb)) for _, w, cb in row_in]
    in_specs += [pl.BlockSpec((HALO_ROWS, w), functools.partial(
        lambda g, cb: (jnp.maximum(tile_of(g) * (ts // HALO_ROWS) - 1, 0), cb), cb=cb)) for _, w, cb in halo_in]
    in_specs += [pl.BlockSpec(memory_space=pltpu.VMEM) for _ in const_in]
    out_specs = [pl.BlockSpec((ts, w), lambda g: (tile_of(g), 0)) for w, _ in row_out]
    out_specs += [pl.BlockSpec(s, lambda g: (0, 0)) for s, _ in acc_out]
    out_shape = [jax.ShapeDtypeStruct((n_rows, w), d) for w, d in row_out]
    out_shape += [jax.ShapeDtypeStruct(s, d) for s, d in acc_out]
    return pl.pallas_call(
        body, name=name, grid=(n_tiles,), in_specs=in_specs, out_specs=out_specs, out_shape=out_shape,
        scratch_shapes=[pltpu.VMEM((8, w), F32) for w in carry],
        compiler_params=pltpu.CompilerParams(dimension_semantics=("arbitrary",), vmem_limit_bytes=VMEM_LIMIT),
    )(*[a for a, _, _ in row_in], *[a for a, _, _ in halo_in], *const_in)


def my_position():
    return lax.axis_index("x"), lax.axis_index("y"), lax.axis_index("c")


def flip(pos, k):
    x, y, c = pos
    dx, dy, dc = (k >> 2) & 1, (k >> 1) & 1, k & 1
    return (1 - x if dx else x, 1 - y if dy else y, 1 - c if dc else c)


def flat_index(pos):
    return 4 * pos[0] + 2 * pos[1] + pos[2]


def gather_shards(shards):
    n = len(shards)

    def body(*refs):
        x_refs, out_refs = refs[:n], refs[n:2 * n]
        send_sems, recv_sems, local_sems = refs[2 * n:]
        x, y, c = my_position()
        me, sibling = (x, y, c), (x, y, 1 - c)
        chips = [(1 - x, y), (x, 1 - y), (1 - x, 1 - y)]

        def copy(a, k, block, to, from_input=False):
            slot = out_refs[a].at[flat_index(block)]
            return pltpu.make_async_remote_copy(
                src_ref=x_refs[a] if from_input else slot, dst_ref=slot,
                send_sem=send_sems.at[7 * a + k], recv_sem=recv_sems.at[7 * a + k],
                device_id=to, device_id_type=MESH_IDS)

        mine = [pltpu.make_async_copy(x_refs[a], out_refs[a].at[flat_index(me)], local_sems.at[a]) for a in range(n)]
        for cp in mine:
            cp.start()
        first = []
        for a in range(n):
            first.append(copy(a, 0, me, sibling, from_input=True))
            first += [copy(a, 1 + j, me, (*chip, c), from_input=True) for j, chip in enumerate(chips)]
        for cp in first:
            cp.start()
        passed = []
        for j, chip in enumerate(chips):
            for a in range(n):
                copy(a, 1 + j, (*chip, c), me).wait_recv()
                cp = copy(a, 4 + j, (*chip, c), sibling)
                cp.start()
                passed.append(cp)
        for a in range(n):
            copy(a, 0, sibling, me).wait_recv()
            for j, chip in enumerate(chips):
                copy(a, 4 + j, (*chip, 1 - c), me).wait_recv()
        for cp in first + passed:
            cp.wait_send()
        for cp in mine:
            cp.wait()

    return pl.pallas_call(
        body, name="gather_shards",
        out_shape=[jax.ShapeDtypeStruct((N_DEV,) + s.shape, s.dtype) for s in shards],
        in_specs=[pl.BlockSpec(memory_space=pl.ANY)] * n, out_specs=[pl.BlockSpec(memory_space=pl.ANY)] * n,
        scratch_shapes=[pltpu.SemaphoreType.DMA((7 * n,)), pltpu.SemaphoreType.DMA((7 * n,)),
                        pltpu.SemaphoreType.DMA((n,))],
    )(*shards)


def ada_modulation(c_all, w_ada_loc, b_ada_blocks):
    cols = w_ada_loc.shape[1]

    def body(c_ref, w_ref, b_ref, out_ref, send_sems, recv_sems):
        me = my_position()
        mi = flat_index(me)
        cv = c_ref[...]
        res = hdot(cv * sigmoid(cv), w_ref[...]) + b_ref[pl.ds(mi, 1), :]
        out_ref[mi] = res
        sends = []
        for k in range(1, N_DEV):
            cp = pltpu.make_async_remote_copy(
                src_ref=out_ref.at[mi], dst_ref=out_ref.at[mi], send_sem=send_sems.at[k - 1],
                recv_sem=recv_sems.at[k - 1], device_id=flip(me, k), device_id_type=MESH_IDS)
            cp.start()
            sends.append(cp)
        for k in range(1, N_DEV):
            pi = flat_index(flip(me, k))
            pltpu.make_async_remote_copy(
                src_ref=out_ref.at[pi], dst_ref=out_ref.at[pi], send_sem=send_sems.at[k - 1],
                recv_sem=recv_sems.at[k - 1], device_id=flip(me, k), device_id_type=MESH_IDS).wait_recv()
        for cp in sends:
            cp.wait_send()

    return pl.pallas_call(
        body, name="ada_modulation",
        out_shape=jax.ShapeDtypeStruct((N_DEV, N_DEV, cols), F32),
        in_specs=[pl.BlockSpec(memory_space=pltpu.VMEM)] * 3, out_specs=pl.BlockSpec(memory_space=pltpu.VMEM),
        scratch_shapes=[pltpu.SemaphoreType.DMA((7,)), pltpu.SemaphoreType.DMA((7,))],
    )(c_all, w_ada_loc, b_ada_blocks)


def fwd_in_tile(step0, tile0, x, mod, w_in_ptt):
    xhat, _ = layer_norm_stats(x)
    h = xhat * (1.0 + mod[1:2]) + mod[0:1]
    return (mm_nt(h, w_in_ptt),)


def fwd_in_gather(x, mod, w_in_pt, shards):
    n = len(shards)
    n_rows = x.shape[0]
    ts = ROW_TILE
    n_tiles = n_rows // ts

    def body(x_ref, mod_ref, w_ref, *rest):
        s_refs = rest[:n]
        proj_ref, out_refs = rest[n], rest[n + 1:2 * n + 1]
        send_sems, recv_sems, local_sems = rest[2 * n + 1:]
        g = pl.program_id(0)
        me = my_position()
        mi = flat_index(me)

        def copies(k, slot):
            return [pltpu.make_async_remote_copy(
                src_ref=s_refs[a], dst_ref=out_refs[a].at[slot], send_sem=send_sems.at[7 * a + k - 1],
                recv_sem=recv_sems.at[7 * a + k - 1], device_id=flip(me, k), device_id_type=MESH_IDS)
                for a in range(n)]

        local = [pltpu.make_async_copy(s_refs[a], out_refs[a].at[mi], local_sems.at[a]) for a in range(n)]

        @pl.when(g == 0)
        def _():
            for cp in local:
                cp.start()
            for k in range(1, N_DEV):
                for cp in copies(k, mi):
                    cp.start()

        proj_ref[...] = fwd_in_tile(None, None, x_ref[...], mod_ref[...], w_ref[...])[0].astype(BF16)

        @pl.when(g == n_tiles - 1)
        def _():
            for k in range(1, N_DEV):
                for cp in copies(k, flat_index(flip(me, k))):
                    cp.wait_recv()
            for k in range(1, N_DEV):
                for cp in copies(k, mi):
                    cp.wait_send()
            for cp in local:
                cp.wait()

    hbm = pl.BlockSpec(memory_space=pl.ANY)
    const = pl.BlockSpec(memory_space=pltpu.VMEM)
    return pl.pallas_call(
        body, name="fwd_in_gather", grid=(n_tiles,),
        in_specs=[pl.BlockSpec((ts, D_MODEL), lambda g: (g, 0)), const, const] + [hbm] * n,
        out_specs=[pl.BlockSpec((ts, P_WIDTH), lambda g: (g, 0))] + [hbm] * n,
        out_shape=[jax.ShapeDtypeStruct((n_rows, P_WIDTH), BF16)]
        + [jax.ShapeDtypeStruct((N_DEV,) + s.shape, s.dtype) for s in shards],
        scratch_shapes=[pltpu.SemaphoreType.DMA((7 * n,)), pltpu.SemaphoreType.DMA((7 * n,)),
                        pltpu.SemaphoreType.DMA((n,))],
        compiler_params=pltpu.CompilerParams(dimension_semantics=("arbitrary",), vmem_limit_bytes=VMEM_LIMIT),
    )(x, mod, w_in_pt, *shards)


def rms_norm_fwd(x, g):
    r = lax.rsqrt(rowmean(x * x) + RMS_EPS)
    xh = x * r
    return xh * g, xh, r


def key_rope_mask(shape):
    return (lane_iota(shape) >= NOPE).astype(F32)


def mla_prep_tile(step0, tile0, q_c, kv_c, kr, krr, cos, sin, gq, gkv, wq, wqr, wkn, wv):
    qn, _, _ = rms_norm_fwd(q_c, gq)
    kvn, _, _ = rms_norm_fwd(kv_c, gkv)
    q = (mm(qn, wq) * tile_lanes(cos, HEADS) + mm(qn, wqr) * tile_lanes(sin, HEADS)) * Q_PRESCALE
    kpe = kr * (cos * key_rope_mask(cos.shape)) + krr * sin
    k = mm(kvn, wkn) + tile_lanes(kpe, HEADS)
    v = mm(kvn, wv)
    return q, k, v


def rwkv_prep_core(tile0, r0, k0, v0, l0, hr, hk, hv, hl, mu_r, mu_k, mu_v, mu_l, w0, a0, k_k, k_a,
                   w_dec, w_iclr, tril, same, bd):
    def shifted(x, halo, mu):
        row0 = jnp.where(tile0, 0.0, halo[HALO_ROWS - 1:HALO_ROWS, :])
        prev = shift_rows_down(x, row0)
        return x + (prev - x) * mu, prev

    ur, pr = shifted(r0, hr, mu_r)
    uk, pk = shifted(k0, hk, mu_k)
    uv, pv = shifted(v0, hv, mu_v)
    ul, plo = shifted(l0, hl, mu_l)
    th = jnp.tanh(ul)
    sg = sigmoid(w0 + mm(th, w_dec))
    lw = -DECAY_SCALE * sg
    a_ic = sigmoid(a0 + mm(ul, w_iclr))
    kkraw = uk * k_k
    nrm_raw = jnp.sqrt(head_sum(kkraw * kkraw, bd))
    nrm = jnp.maximum(nrm_raw, 1e-12)
    kk = kkraw / nrm
    k2 = uk * (1.0 + (a_ic - 1.0) * k_a)
    lc = ones_dot(tril, lw, 3)
    lcl = ones_dot(same, lw, 3)
    return dict(ur=ur, uk=uk, uv=uv, ul=ul, pr=pr, pk=pk, pv=pv, pl=plo, th=th, sg=sg, lw=lw, a_ic=a_ic,
                kkraw=kkraw, nrm_raw=nrm_raw, nrm=nrm, kk=kk, k2=k2, lc=lc, lcl=lcl)


def rwkv_prep_tile(step0, tile0, r0, k0, v0, l0, hr, hk, hv, hl, *consts):
    f = rwkv_prep_core(tile0, r0, k0, v0, l0, hr, hk, hv, hl, *consts)
    lc, lw = f["lc"], f["lw"]
    e_neg = jnp.exp(-lc)
    rt = f["ur"] * jnp.exp(lc)
    at = -f["kk"] * jnp.exp(lc - lw)
    bt = f["kk"] * f["a_ic"] * e_neg
    kt = f["k2"] * e_neg
    return rt, at, bt, kt, jnp.exp(f["lcl"]), f["uv"], f["ur"], f["k2"]


def wkv_masks():
    lane = lane_iota((1, PAIR))
    m_lo = (lane < HEAD).astype(F32)
    r2 = row_iota((PAIR, PAIR))
    c2 = lane_iota((PAIR, PAIR))
    bd = ((r2 < HEAD) == (c2 < HEAD)).astype(F32)
    eye2 = (r2 == c2).astype(F32)
    eye = (row_iota((CHUNK, CHUNK)) == lane_iota((CHUNK, CHUNK))).astype(F32)
    t_idx = row_iota((4 * CHUNK, PAIR)) % CHUNK
    s_idx = lane_iota((4 * CHUNK, PAIR)) % CHUNK
    keep = s_idx < t_idx + (row_iota((4 * CHUNK, PAIR)) >= 2 * CHUNK).astype(jnp.int32)
    return (m_lo, 1.0 - m_lo), keep, eye, bd, eye2


def rows(*parts):
    return jnp.concatenate(parts, axis=0)


def lanes(*parts):
    return jnp.concatenate(parts, axis=1)


def head_rows(x, ms):
    return rows(x * ms[0], x * ms[1])


def wkv_score_stack(at, rt, ms):
    return rows(head_rows(at, ms), head_rows(rt, ms))


def wkv_chunks_pre(chunks, masks):
    ms, keep, eye, bd, eye2 = masks
    n = len(chunks)
    at, bt, kt, rt, v, cl = (list(t) for t in zip(*chunks))
    scores = [jnp.where(keep, mm_nt(wkv_score_stack(a, r, ms), rows(b, k)), 0.0)
              for a, r, b, k in zip(at, rt, bt, kt)]
    q = CHUNK
    aab = [s[h * q:(h + 1) * q, :q] for s in scores for h in range(2)]
    tinv = [eye + a for a in aab]
    power = [mm(a, a) for a in aab]
    for _ in range(5):
        both = [mm(rows(t, p), p) for t, p in zip(tinv, power)]
        tinv = [t + x[:q] for t, x in zip(tinv, both)]
        power = [x[q:] for x in both]
    pair = lambda c, row0, col0: lanes(scores[c][row0:row0 + q, col0:col0 + q],
                                       scores[c][row0 + q:row0 + 2 * q, col0:col0 + q])
    tinv_p = [lanes(tinv[2 * c], tinv[2 * c + 1]) for c in range(n)]
    aak_p = [pair(c, 0, q) for c in range(n)]
    prb_p = [pair(c, 2 * q, 0) for c in range(n)]
    prk_p = [pair(c, 2 * q, q) for c in range(n)]
    v_rows = [head_rows(x, ms) for x in v]
    wy = [mm(rows(a, p), x) for a, p, x in zip(aak_p, prk_p, v_rows)]
    w = [x[:q] for x in wy]
    yh2 = [x[q:] for x in wy]
    aw = [mm(t, lanes(head_rows(a, ms), head_rows(w_, ms))) for t, a, w_ in zip(tinv_p, at, w)]
    ah = [x[:, :PAIR] for x in aw]
    wh = [x[:, PAIR:] for x in aw]
    ry = [mm(p, lanes(head_rows(a, ms), head_rows(w_, ms))) for p, a, w_ in zip(prb_p, ah, wh)]
    rh = [r + x[:, :PAIR] for r, x in zip(rt, ry)]
    yh = [x[:, PAIR:] + y for x, y in zip(ry, yh2)]
    bc = [b * c_ for b, c_ in zip(bt, cl)]
    kc = [k * c_ for k, c_ in zip(kt, cl)]
    gh = [mm_tn(b, lanes(a, w_)) for b, a, w_ in zip(bc, ah, wh)]
    g = [eye2 * c_ + bd * x[:, :PAIR] for c_, x in zip(cl, gh)]
    h = [bd * (x[:, PAIR:] + mm_tn(k, v_)) for x, k, v_ in zip(gh, kc, v)]
    as_bf16 = lambda xs: [x.astype(BF16) for x in xs]
    saved = (as_bf16(tinv_p), as_bf16(aak_p), as_bf16(prb_p), as_bf16(prk_p), as_bf16(ah), wh)
    return g, h, rh, yh, saved


def wkv_chunks_grad(chunks, saved, m0, dy, dm1, masks):
    ms, keep, eye, bd, eye2 = masks
    n = len(chunks)
    q = CHUNK
    at, bt, kt, rt, v, cl = (list(t) for t in zip(*chunks))
    tinv_p, aak_p, prb_p, prk_p, ah, wh = (list(t) for t in zip(*saved))
    head_stack = lambda p: rows(p[:, :q], p[:, q:])
    bc = [b * c_ for b, c_ in zip(bt, cl)]
    kc = [k * c_ for k, c_ in zip(kt, cl)]
    u = [mm(a, m) + w for a, m, w in zip(ah, m0, wh)]
    dm1 = [d * bd for d in dm1]
    from_state = [mm(rows(b, k), d) for b, k, d in zip(bc, kc, dm1)]
    dy_rows = [head_rows(d, ms) for d in dy]
    from_out = [mm_tn(lanes(head_stack(pb), head_stack(pk)), d) for pb, pk, d in zip(prb_p, prk_p, dy_rows)]
    du = [a[:q] + b[:q] for a, b in zip(from_state, from_out)]
    dv = [a[q:] + b[q:] for a, b in zip(from_state, from_out)]
    dz = [mm_tn(head_stack(t), head_rows(d, ms)) for t, d in zip(tinv_p, du)]
    dz_rows = [head_rows(d, ms) for d in dz]
    dv = [a + mm_tn(head_stack(k), d) for a, k, d in zip(dv, aak_p, dz_rows)]
    by_m0 = [mm_nt(rows(d, z), m) for d, z, m in zip(dy, dz, m0)]
    uv = [rows(x, y) for x, y in zip(u, v)]
    by_dm1 = [mm_nt(x, d) for x, d in zip(uv, dm1)]
    udm = [x[:q] for x in by_dm1]
    vdm = [x[q:] for x in by_dm1]
    dscores = [jnp.where(keep, mm_nt(rows(z, d), x), 0.0) for z, d, x in zip(dz_rows, dy_rows, uv)]
    to_ar = [mm(d, rows(b, k)) for d, b, k in zip(dscores, bt, kt)]
    to_bk = [mm_tn(d, wkv_score_stack(a, r, ms)) for d, a, r in zip(dscores, at, rt)]
    ones = jnp.ones((8, PAIR), F32)
    upper = (lane_iota((CHUNK, CHUNK)) >= row_iota((CHUNK, CHUNK))).astype(F32)
    out = []
    for c in range(n):
        e = to_ar[c]
        dat_c = by_m0[c][q:] + e[:q] * ms[0] + e[q:2 * q] * ms[1]
        drt_c = by_m0[c][:q] + e[2 * q:3 * q] * ms[0] + e[3 * q:] * ms[1]
        dbt_c = udm[c] * cl[c] + to_bk[c][:q]
        dkt_c = vdm[c] * cl[c] + to_bk[c][q:]
        dlcl = ones_dot_nt(ones, dm1[c] * m0[c], 3)[0:1, :] * cl[c] + colsum(bc[c] * udm[c] + kc[c] * vdm[c])
        g = drt_c * rt[c] - dbt_c * bt[c] - dkt_c * kt[c] + dat_c * at[c]
        dlw = ones_dot(upper, g, 3) - dat_c * at[c] + dlcl
        out.append((dat_c, dbt_c, dkt_c, drt_c, dv[c], dlw))
    return out


def wkv_forward(at, bt, kt, rt, v, clf):
    n_rows = at.shape[0]
    cps = WKV_CHUNKS_PER_STEP
    rb = cps * CHUNK
    n_steps = n_rows // rb

    def body(a_ref, b_ref, k_ref, r_ref, v_ref, c_ref, y_ref, m0_ref, g_ref, rh_ref, *rest):
        saved_refs, m_scr = rest[:6], rest[6]

        @pl.when(pl.program_id(1) == 0)
        def _():
            m_scr[...] = jnp.zeros_like(m_scr)

        masks = wkv_masks()
        chunks = []
        for cc in range(cps):
            sl = slice(cc * CHUNK, (cc + 1) * CHUNK)
            chunks.append((a_ref[sl, :], b_ref[sl, :], k_ref[sl, :], r_ref[sl, :], v_ref[sl, :],
                           c_ref[cc * CHUNK:cc * CHUNK + 1, :]))
        gs, hs, rhs, yhs, saved = wkv_chunks_pre(chunks, masks)
        for ref, per_chunk in zip(saved_refs, saved):
            for cc, val in enumerate(per_chunk):
                ref[cc * CHUNK:(cc + 1) * CHUNK, :] = val
        m = m_scr[...]
        for cc, (g, h, rh, yh) in enumerate(zip(gs, hs, rhs, yhs)):
            sl = slice(cc * CHUNK, (cc + 1) * CHUNK)
            m0_ref[0, cc] = m
            g_ref[0, cc] = g
            rh_ref[sl, :] = rh
            y_ref[sl, :] = hdot(rh, m) + yh
            m = hdot(g, m) + h
        m_scr[...] = m

    blk = pl.BlockSpec((rb, PAIR), lambda p, s: (s, p))
    state_blk = pl.BlockSpec((1, cps, PAIR, PAIR), lambda p, s: (p, s, 0, 0))
    state_shape = jax.ShapeDtypeStruct((WIDTH // PAIR, n_rows // CHUNK, PAIR, PAIR), F32)
    rows_f32 = jax.ShapeDtypeStruct((n_rows, WIDTH), F32)
    rows_bf16 = jax.ShapeDtypeStruct((n_rows, WIDTH), BF16)
    return pl.pallas_call(
        body, name="wkv_forward", grid=(WIDTH // PAIR, n_steps),
        in_specs=[blk] * 6,
        out_specs=[blk, state_blk, state_blk, blk] + [blk] * 6,
        out_shape=[rows_f32, state_shape, state_shape, rows_f32] + [rows_bf16] * 5 + [rows_f32],
        scratch_shapes=[pltpu.VMEM((PAIR, PAIR), F32)],
        compiler_params=pltpu.CompilerParams(dimension_semantics=("arbitrary", "arbitrary"),
                                             vmem_limit_bytes=VMEM_LIMIT),
    )(at, bt, kt, rt, v, clf)


def wkv_backward(at, bt, kt, rt, v, clf, m0s, gs, rh, saved, dy):
    n_rows = at.shape[0]
    cps = WKV_CHUNKS_PER_STEP
    rb = cps * CHUNK
    n_steps = n_rows // rb

    def body(a_ref, b_ref, k_ref, r_ref, v_ref, c_ref, m0_ref, g_ref, rh_ref, *rest):
        saved_refs, dy_ref = rest[:6], rest[6]
        da_ref, db_ref, dk_ref, dr_ref, dv_ref, dlw_ref, dm_scr = rest[7:]

        @pl.when(pl.program_id(1) == 0)
        def _():
            dm_scr[...] = jnp.zeros_like(dm_scr)

        masks = wkv_masks()
        bd = masks[3]
        dm = dm_scr[...]
        dm1 = [None] * cps
        for cc in reversed(range(cps)):
            sl = slice(cc * CHUNK, (cc + 1) * CHUNK)
            dm1[cc] = dm
            dm = bd * (hdot_tn(g_ref[0, cc], dm) + hdot_tn(rh_ref[sl, :], dy_ref[sl, :]))
        dm_scr[...] = dm
        chunks, kept, m0, dys = [], [], [], []
        for cc in range(cps):
            sl = slice(cc * CHUNK, (cc + 1) * CHUNK)
            chunks.append((a_ref[sl, :], b_ref[sl, :], k_ref[sl, :], r_ref[sl, :], v_ref[sl, :],
                           c_ref[cc * CHUNK:cc * CHUNK + 1, :]))
            kept.append(tuple(ref[sl, :] for ref in saved_refs))
            m0.append(m0_ref[0, cc])
            dys.append(dy_ref[sl, :])
        grads = wkv_chunks_grad(chunks, kept, m0, dys, dm1, masks)
        for cc, (dat, dbt, dkt, drt, dv, dlw) in enumerate(grads):
            sl = slice(cc * CHUNK, (cc + 1) * CHUNK)
            da_ref[sl, :] = dat
            db_ref[sl, :] = dbt
            dk_ref[sl, :] = dkt
            dr_ref[sl, :] = drt
            dv_ref[sl, :] = dv
            dlw_ref[sl, :] = dlw

    blk = pl.BlockSpec((rb, PAIR), lambda p, s: (n_steps - 1 - s, p))
    state_blk = pl.BlockSpec((1, cps, PAIR, PAIR), lambda p, s: (p, n_steps - 1 - s, 0, 0))
    return pl.pallas_call(
        body, name="wkv_backward", grid=(WIDTH // PAIR, n_steps),
        in_specs=[blk] * 6 + [state_blk, state_blk, blk] + [blk] * 6 + [blk],
        out_specs=[blk] * 6,
        out_shape=[jax.ShapeDtypeStruct((n_rows, WIDTH), F32)] * 6,
        scratch_shapes=[pltpu.VMEM((PAIR, PAIR), F32)],
        compiler_params=pltpu.CompilerParams(dimension_semantics=("arbitrary", "arbitrary"),
                                             vmem_limit_bytes=VMEM_LIMIT),
    )(at, bt, kt, rt, v, clf, m0s, gs, rh, *saved, dy)


def visible(q_row0, k_row0, shape):
    qc = (q_row0 + row_iota(shape)) // CHUNK
    kc = (k_row0 + lane_iota(shape)) // CHUNK
    return kc <= qc


def attention_forward(q, k, v):
    n_rows = q.shape[0]
    tq, tk = ATTN_FWD_TILES
    n_q = n_rows // tq
    assert tk % tq == 0

    def body(q_ref, k_ref, v_ref, o_ref, lse_ref):
        i = pl.program_id(1)
        lane = lane_iota((tq, LANE))
        heads = [slice(0, LANE), slice(LANE, 2 * LANE)]
        qs = [q_ref[:, cols] for cols in heads]

        def step(j, carry, size, masked):
            rows = pl.ds(pl.multiple_of(j * size, size), size)
            ss = [mm_nt(qh, k_ref[rows, cols]) for qh, cols in zip(qs, heads)]
            if masked:
                vis = visible(i * tq, j * size, ss[0].shape)
                ss = [jnp.where(vis, s, -jnp.inf) for s in ss]
            ps, stats = [], []
            for s, (m, l, _) in zip(ss, carry):
                m_new = jnp.maximum(m, jnp.max(s, axis=-1, keepdims=True))
                p = jnp.exp2(s - m_new)
                alpha = jnp.exp2(m - m_new)
                ps.append(p)
                stats.append((m_new, alpha, alpha * l + jnp.sum(p, axis=-1, keepdims=True)))
            pvs = [mm(p, v_ref[rows, cols]) for p, cols in zip(ps, heads)]
            return tuple((m_new, l, alpha * acc + pv)
                         for (m_new, alpha, l), (_, _, acc), pv in zip(stats, carry, pvs))

        carry = tuple((jnp.full((tq, 1), -jnp.inf, F32), jnp.zeros((tq, 1), F32), jnp.zeros((tq, LANE), F32))
                      for _ in heads)
        n_full = (i * tq) // tk
        carry = lax.fori_loop(0, n_full, functools.partial(step, size=tk, masked=False), carry)
        (m0, l0, acc0), (m1, l1, acc1) = step(n_full, carry, size=tk, masked=True)
        o_ref[...] = acc0 / l0 + acc1 / l1
        lse_ref[...] = jnp.where(lane >= HEAD, m1 + jnp.log2(l1), m0 + jnp.log2(l0))

    return pl.pallas_call(
        body, name="attention_forward", grid=(HEADS // 2, n_q),
        in_specs=[pl.BlockSpec((tq, 2 * LANE), lambda p, i: (i, p)),
                  pl.BlockSpec((n_rows, 2 * LANE), lambda p, i: (0, p)),
                  pl.BlockSpec((n_rows, 2 * LANE), lambda p, i: (0, p))],
        out_specs=[pl.BlockSpec((tq, LANE), lambda p, i: (i, p))] * 2,
        out_shape=[jax.ShapeDtypeStruct((n_rows, WIDTH), F32)] * 2,
        compiler_params=pltpu.CompilerParams(dimension_semantics=("arbitrary", "arbitrary"),
                                             vmem_limit_bytes=VMEM_LIMIT),
    )(q, k, v)


def attention_backward(q, k, v, o, do, lse):
    n_rows = q.shape[0]
    tq, tk = ATTN_BWD_TILES
    n_q = n_rows // tq
    n_masked = max(1, tk // tq)

    def body(q_ref, k_ref, v_ref, o_ref, do_ref, lse_ref, dq_ref, dk_ref, dv_ref):
        j = pl.program_id(1)

        @pl.when(j == 0)
        def _():
            dq_ref[...] = jnp.zeros_like(dq_ref)

        lane = lane_iota((tq, LANE))
        heads = [slice(0, LANE), slice(LANE, 2 * LANE)]
        ks = [k_ref[:, cols] for cols in heads]
        vs = [v_ref[:, cols] for cols in heads]
        head_lanes = [(lane < HEAD).astype(F32), (lane >= HEAD).astype(F32)]

        def step(i, carry, masked):
            rows = pl.ds(pl.multiple_of(i * tq, tq), tq)
            qs = [q_ref[rows, cols] for cols in heads]
            dout = do_ref[rows, :]
            dout_o = dout * o_ref[rows, :]
            lse_t = lse_ref[rows, :]
            ss = [mm_nt(qh, kh) for qh, kh in zip(qs, ks)]
            dps = [mm_nt(dout, vh) for vh in vs]
            ps, dss = [], []
            for hh in range(2):
                delta = jnp.sum(dout_o * head_lanes[hh], axis=-1, keepdims=True)
                lse_h = jnp.sum(jnp.where(lane == hh * HEAD, lse_t, 0.0), axis=-1, keepdims=True)
                p = jnp.exp2(ss[hh] - lse_h)
                if masked:
                    p = jnp.where(visible(i * tq, j * tk, p.shape), p, 0.0)
                ps.append(p)
                dss.append(p * (dps[hh] - delta))
            dvs = [mm_tn(p, dout) for p in ps]
            dqs = [mm(ds, kh) for ds, kh in zip(dss, ks)]
            dks = [mm_tn(ds, qh) for ds, qh in zip(dss, qs)]
            for cols, dq in zip(heads, dqs):
                dq_ref[rows, cols] += dq * ATTN_SCALE
            return tuple((dk + a, dv + b) for (dk, dv), a, b in zip(carry, dks, dvs))

        carry = tuple((jnp.zeros((tk, LANE), F32), jnp.zeros((tk, LANE), F32)) for _ in heads)
        i_first = (j * tk) // tq
        for extra in range(n_masked):
            carry = step(i_first + extra, carry, masked=True)
        carry = lax.fori_loop(i_first + n_masked, n_q, functools.partial(step, masked=False), carry)
        for cols, (dk, dv) in zip(heads, carry):
            dk_ref[:, cols] = dk * (1.0 / LOG2_E)
            dv_ref[:, cols] = dv

    full = lambda w: pl.BlockSpec((n_rows, w), lambda p, j: (0, p))
    blk = pl.BlockSpec((tk, 2 * LANE), lambda p, j: (j, p))
    return pl.pallas_call(
        body, name="attention_backward", grid=(HEADS // 2, n_rows // tk),
        in_specs=[full(2 * LANE), blk, blk, full(LANE), full(LANE), full(LANE)],
        out_specs=[full(2 * LANE), blk, blk],
        out_shape=[jax.ShapeDtypeStruct((n_rows, HEADS * LANE), F32)] * 3,
        compiler_params=pltpu.CompilerParams(dimension_semantics=("arbitrary", "arbitrary"),
                                             vmem_limit_bytes=VMEM_LIMIT),
    )(q, k, v, o, do, lse)


def tail_tile(step0, tile0, x, tgt, ma, mb, gpa, gpb, ya, y, ur, k2, uv,
              mod, wpa, wpb, wout, gn_g, gn_b, r_k, post_g, post_b, bd):
    gate = mod[2:3]
    inv = 1.0 / HEAD
    yc = y - head_sum(y, bd) * inv
    rs = lax.rsqrt(head_sum(yc * yc, bd) * inv + GN_EPS)
    yn = yc * rs
    yb = yn * gn_g + gn_b + head_sum(ur * k2 * r_k, bd) * uv
    sga, sgb = sigmoid(gpa), sigmoid(gpb)
    sila, silb = gpa * sga, gpb * sgb
    ga, gb = ya * sila, yb * silb
    pa, pb = mm(ga, wpa), mm(gb, wpb)
    sa, sb = sigmoid(ma), sigmoid(mb)
    merged = sa * pa + sb * pb
    sub = mm(merged, wout)
    z = ALPHA * x + (1.0 + gate) * sub
    zhat, rstd = layer_norm_stats(z)
    err = zhat * post_g + post_b - tgt
    loss = 0.5 * jnp.sum(rowmean(err * err), axis=0, keepdims=True) + jnp.zeros((1, LANE), F32)
    dout = err * (1.0 / D_MODEL)
    dpost_g = colsum(dout * zhat)
    dpost_b = colsum(dout)
    dz = layer_norm_bwd(dout * post_g, zhat, rstd)
    dgate = colsum(dz * sub)
    dsub = dz * (1.0 + gate)
    dwout = mm_tn(merged, dsub)
    dmerged = mm_nt(dsub, wout)
    dpa, dpb = dmerged * sa, dmerged * sb
    dma = dmerged * pa * sa * (1.0 - sa)
    dmb = dmerged * pb * sb * (1.0 - sb)
    dwpa = mm_tn(ga, dpa)
    dwpb = mm_tn(gb, dpb)
    dga = mm_nt(dpa, wpa)
    dgb = mm_nt(dpb, wpb)
    dya = dga * sila
    dgpa = dga * ya * (sga * (1.0 + gpa * (1.0 - sga)))
    dyb = dgb * silb
    dgpb = dgb * yb * (sgb * (1.0 + gpb * (1.0 - sgb)))
    dgn_g = colsum(dyb * yn)
    dgn_b = colsum(dyb)
    dyn = dyb * gn_g
    dy = rs * (dyn - head_sum(dyn, bd) * inv - yn * head_sum(dyn * yn, bd) * inv)
    return (dz, dma, dmb, dgpa, dgpb, dya, dy, dyb,
            loss, dwout, dwpa, dwpb, dgn_g, dgn_b, dpost_g, dpost_b, dgate)


def mla_prep_bwd_tile(step0, tile0, q_c, kv_c, cos, sin, dq, dk, dv, gq, gkv, wq, wqr, wkn, wv):
    qn, qh, rq = rms_norm_fwd(q_c, gq)
    kvn, kvh, rkv = rms_norm_fwd(kv_c, gkv)
    dqc = dq * tile_lanes(cos, HEADS)
    dqs = dq * tile_lanes(sin, HEADS)
    dqn = mm_nt(dqc, wq) + mm_nt(dqs, wqr)
    dkvn = mm_nt(dk, wkn) + mm_nt(dv, wv)
    dkpe = dk[:, 0:LANE]
    for h in range(1, HEADS):
        dkpe = dkpe + dk[:, h * LANE:(h + 1) * LANE]
    dkr = dkpe * (cos * key_rope_mask(cos.shape))
    dkrr = dkpe * sin

    def rms_bwd(dyv, xh, r, g):
        dyg = dyv * g
        return r * (dyg - xh * rowmean(dyg * xh)), colsum(dyv * xh)

    dq_c, dgq = rms_bwd(dqn, qh, rq, gq)
    dkv_c, dgkv = rms_bwd(dkvn, kvh, rkv, gkv)
    return (dq_c, dkv_c, dkr, dkrr,
            mm_tn(qn, dqc), mm_tn(qn, dqs), mm_tn(kvn, dk), mm_tn(kvn, dv), dgq, dgkv)


def rwkv_prep_bwd_tile(step0, tile0, r0, k0, v0, l0, drt, dat, dbt, dkt, dvv, dlw, dyb, hr, hk, hv, hl,
                       mu_r, mu_k, mu_v, mu_l, w0, a0, k_k, k_a, w_dec, w_iclr, tril, same, bd, r_k,
                       cr, ck, cv, cl_):
    f = rwkv_prep_core(tile0, r0, k0, v0, l0, hr, hk, hv, hl, mu_r, mu_k, mu_v, mu_l, w0, a0, k_k, k_a,
                       w_dec, w_iclr, tril, same, bd)
    ur, uk, uv, ul, kk, k2, a_ic, sg, th = (f[n] for n in ("ur", "uk", "uv", "ul", "kk", "k2", "a_ic", "sg", "th"))
    lc, lw = f["lc"], f["lw"]
    e_neg = jnp.exp(-lc)
    dur = drt * jnp.exp(lc)
    da = dat * jnp.exp(lc - lw)
    db = dbt * e_neg
    dk2 = dkt * e_neg
    s = head_sum(ur * k2 * r_k, bd)
    duv = dvv + dyb * s
    ds = head_sum(dyb * uv, bd)
    dur = dur + ds * k2 * r_k
    dk2 = dk2 + ds * ur * r_k
    dr_k = colsum(ds * ur * k2)
    dkk = db * a_ic - da
    da_ic = db * kk + dk2 * uk * k_a
    duk = dk2 * (1.0 + (a_ic - 1.0) * k_a)
    dk_a = colsum(dk2 * uk * (a_ic - 1.0))
    dkkraw = jnp.where(f["nrm_raw"] > 1e-12, (dkk - kk * head_sum(dkk * kk, bd)) / f["nrm"], dkk * 1e12)
    duk = duk + dkkraw * k_k
    dk_k = colsum(dkkraw * uk)
    dai = da_ic * a_ic * (1.0 - a_ic)
    dd = dlw * (-DECAY_SCALE) * sg * (1.0 - sg)
    dul = mm_nt(dai, w_iclr) + mm_nt(dd, w_dec) * (1.0 - th * th)

    def unshift(du, x, prev, mu, carry_row):
        nxt = shift_rows_up(du, carry_row)
        return du * (1.0 - mu) + nxt * mu, colsum(du * (prev - x)), du[0:1, :]

    dr0, dmu_r, ncr = unshift(dur, r0, f["pr"], mu_r, cr)
    dk0, dmu_k, nck = unshift(duk, k0, f["pk"], mu_k, ck)
    dv0, dmu_v, ncv = unshift(duv, v0, f["pv"], mu_v, cv)
    dl0, dmu_l, ncl = unshift(dul, l0, f["pl"], mu_l, cl_)
    return (dr0, dk0, dv0, dl0,
            dmu_r, dmu_k, dmu_v, dmu_l, colsum(dd), colsum(dai), dk_k, dk_a, dr_k, mm_tn(th, dd), mm_tn(ul, dai),
            ncr, nck, ncv, ncl)


def in_backward(x, dz, pieces, mod, w_in_pt, unrot):
    n_rows = x.shape[0]
    ts = ROW_TILE
    n_p = len(pieces)
    shard_cols = IN_WIDTH // N_DEV

    def body(*refs):
        x_ref, dz_ref = refs[:2]
        p_refs = refs[2:2 + n_p]
        mod_ref, w_ref, unrot_ref = refs[2 + n_p:5 + n_p]
        dx_ref, ht_ref, blocks_ref, dshift_ref, dscale_ref = refs[5 + n_p:]
        step0 = pl.program_id(0) == 0
        dma, dmb, dr0, dk0, dv0, dgpa, dgpb, dq_c, dkv_c, dkr, dkrr, dl0 = (r[...] for r in p_refs)
        dproj = jnp.concatenate([dma, dmb, dr0, dk0, dv0, dgpa, dgpb, dq_c, dkv_c, dkr, dkrr, dl0], axis=1)
        dh = mm(dproj, w_ref[...])
        xhat, rstd = layer_norm_stats(x_ref[...])
        scale1 = 1.0 + mod_ref[1:2, :]
        dx_ref[...] = layer_norm_bwd(dh * scale1, xhat, rstd) + ALPHA * dz_ref[...]
        ht_ref[...] = jnp.transpose(xhat * scale1 + mod_ref[0:1, :]).astype(BF16)
        dkrope = (dkr.astype(F32) + mm(dkrr, unrot_ref[...]))[:, NOPE:QK_DIM]
        natural = jnp.concatenate(
            [dq_c.astype(F32), dkv_c.astype(F32), dkrope]
            + [p.astype(F32) for p in (dgpa, dr0, dk0, dv0, dl0, dgpb, dma, dmb)], axis=1)
        for j in range(N_DEV):
            blocks_ref[j] = natural[:, j * shard_cols:(j + 1) * shard_cols].astype(BF16)
        for ref, val in ((dshift_ref, colsum(dh)), (dscale_ref, colsum(dh * xhat))):
            @pl.when(step0)
            def _(ref=ref, val=val):
                ref[...] = val

            @pl.when(jnp.logical_not(step0))
            def _(ref=ref, val=val):
                ref[...] += val

    row = lambda w: pl.BlockSpec((ts, w), lambda i: (i, 0))
    const = pl.BlockSpec(memory_space=pltpu.VMEM)
    vec = pl.BlockSpec((1, D_MODEL), lambda i: (0, 0))
    return pl.pallas_call(
        body, name="in_backward", grid=(n_rows // ts,),
        in_specs=[row(D_MODEL), row(D_MODEL)] + [row(p.shape[1]) for p in pieces] + [const] * 3,
        out_specs=[row(D_MODEL), pl.BlockSpec((D_MODEL, ts), lambda i: (0, i)),
                   pl.BlockSpec((N_DEV, ts, shard_cols), lambda i: (0, i, 0)), vec, vec],
        out_shape=[jax.ShapeDtypeStruct((n_rows, D_MODEL), F32), jax.ShapeDtypeStruct((D_MODEL, n_rows), BF16),
                   jax.ShapeDtypeStruct((N_DEV, n_rows, shard_cols), BF16),
                   jax.ShapeDtypeStruct((1, D_MODEL), F32), jax.ShapeDtypeStruct((1, D_MODEL), F32)],
        compiler_params=pltpu.CompilerParams(dimension_semantics=("arbitrary",), vmem_limit_bytes=VMEM_LIMIT),
    )(x, dz, *pieces, mod, w_in_pt, unrot)


def in_weight_grad_exchange(h_t, dp_blocks, others, small, order):
    n = len(others)
    n_rows = h_t.shape[1]
    ts = 4 * ROW_TILE
    n_i = n_rows // ts
    shard_cols = dp_blocks.shape[2]
    n_chips = N_DEV // 2
    last = N_DEV - 1

    def body(order_ref, h_ref, dp_ref, *rest):
        g_refs, s_ref = rest[:n], rest[n]
        rwin_ref, rg_refs, rs_ref = rest[n + 1], rest[n + 2:2 * n + 2], rest[2 * n + 2]
        (acc, sendbuf, sib_buf, sib_send, sib_recv, win_send, win_recv,
         o_send, o_recv, local_sems) = rest[2 * n + 3:]
        b, i = pl.program_id(0), pl.program_id(1)
        me = my_position()
        mi = flat_index(me)
        sibling = (me[0], me[1], 1 - me[2])

        def other_copies(k, src_index, dst_index):
            peer = flip(me, k)
            out = [pltpu.make_async_remote_copy(
                src_ref=g_refs[a].at[src_index], dst_ref=rg_refs[a].at[dst_index],
                send_sem=o_send.at[(n + 1) * (k - 1) + a], recv_sem=o_recv.at[(n + 1) * (k - 1) + a],
                device_id=peer, device_id_type=MESH_IDS) for a in range(n)]
            out.append(pltpu.make_async_remote_copy(
                src_ref=s_ref, dst_ref=rs_ref.at[dst_index],
                send_sem=o_send.at[(n + 1) * (k - 1) + n], recv_sem=o_recv.at[(n + 1) * (k - 1) + n],
                device_id=peer, device_id_type=MESH_IDS))
            return out

        def local_copies():
            out = [pltpu.make_async_copy(g_refs[a].at[mi], rg_refs[a].at[mi], local_sems.at[a]) for a in range(n)]
            out.append(pltpu.make_async_copy(s_ref, rs_ref.at[mi], local_sems.at[n]))
            return out

        def to_sibling(t):
            return pltpu.make_async_remote_copy(
                src_ref=sendbuf.at[t], dst_ref=sib_buf.at[t], send_sem=sib_send.at[t], recv_sem=sib_recv.at[t],
                device_id=sibling, device_id_type=MESH_IDS)

        def to_owner(t):
            flip_x = (t < 2) * 1
            flip_y = 1 - (t & 1)
            owner = (me[0] ^ flip_x, me[1] ^ flip_y, me[2])
            return pltpu.make_async_remote_copy(
                src_ref=sendbuf.at[n_chips + t], dst_ref=rwin_ref.at[t], send_sem=win_send.at[t],
                recv_sem=win_recv.at[t], device_id=owner, device_id_type=MESH_IDS)

        own_block = pltpu.make_async_copy(sendbuf.at[last], rwin_ref.at[n_chips - 1], local_sems.at[n + 1])

        @pl.when(jnp.logical_and(b == 0, i == 0))
        def _():
            for cp in local_copies():
                cp.start()
            for k in range(1, N_DEV):
                for cp in other_copies(k, flat_index(flip(me, k)), mi):
                    cp.start()

        contrib = jnp.dot(h_ref[...], dp_ref[...], preferred_element_type=F32)

        @pl.when(i == 0)
        def _():
            acc[...] = contrib

        @pl.when(i > 0)
        def _():
            acc[...] += contrib

        slot = order_ref[N_DEV + b]
        t = slot & (n_chips - 1)

        @pl.when(jnp.logical_and(i == n_i - 1, slot < n_chips))
        def _():
            sendbuf[slot] = acc[...].astype(BF16)
            to_sibling(t).start()

        @pl.when(jnp.logical_and(i == n_i - 1, slot >= n_chips))
        def _():
            to_sibling(t).wait_recv()
            sendbuf[slot] = (acc[...] + sib_buf[t].astype(F32)).astype(BF16)

            @pl.when(slot < last)
            def _():
                to_owner(t).start()

            @pl.when(slot == last)
            def _():
                own_block.start()

        @pl.when(jnp.logical_and(b == last, i == n_i - 1))
        def _():
            for t in range(n_chips - 1):
                to_owner(t).wait_recv()
            for k in range(1, N_DEV):
                pi = flat_index(flip(me, k))
                for cp in other_copies(k, pi, pi):
                    cp.wait_recv()
            for t in range(n_chips):
                to_sibling(t).wait_send()
            for t in range(n_chips - 1):
                to_owner(t).wait_send()
            for k in range(1, N_DEV):
                for cp in other_copies(k, flat_index(flip(me, k)), mi):
                    cp.wait_send()
            for cp in local_copies():
                cp.wait()
            own_block.wait()

    hbm = pl.BlockSpec(memory_space=pl.ANY)
    n_sem = 7 * (n + 1)
    grid_spec = pltpu.PrefetchScalarGridSpec(
        num_scalar_prefetch=1, grid=(N_DEV, n_i),
        in_specs=[pl.BlockSpec((D_MODEL, ts), lambda b, i, order: (0, i)),
                  pl.BlockSpec((None, ts, shard_cols), lambda b, i, order: (order[b], i, 0))] + [hbm] * (n + 1),
        out_specs=[hbm] * (n + 2),
        scratch_shapes=[pltpu.VMEM((D_MODEL, shard_cols), F32), pltpu.VMEM((N_DEV, D_MODEL, shard_cols), BF16),
                        pltpu.VMEM((n_chips, D_MODEL, shard_cols), BF16),
                        pltpu.SemaphoreType.DMA((n_chips,)), pltpu.SemaphoreType.DMA((n_chips,)),
                        pltpu.SemaphoreType.DMA((n_chips - 1,)), pltpu.SemaphoreType.DMA((n_chips - 1,)),
                        pltpu.SemaphoreType.DMA((n_sem,)), pltpu.SemaphoreType.DMA((n_sem,)),
                        pltpu.SemaphoreType.DMA((n + 2,))])
    return pl.pallas_call(
        body, name="in_weight_grad_exchange", grid_spec=grid_spec,
        out_shape=[jax.ShapeDtypeStruct((n_chips, D_MODEL, shard_cols), BF16)]
        + [jax.ShapeDtypeStruct(o.shape, o.dtype) for o in others]
        + [jax.ShapeDtypeStruct((N_DEV,) + small.shape, small.dtype)],
        compiler_params=pltpu.CompilerParams(dimension_semantics=("arbitrary", "arbitrary"),
                                             vmem_limit_bytes=VMEM_LIMIT),
    )(order, h_t, dp_blocks, *others, small)


def ada_weight_grad(c_all, dmod_cols):
    def body(c_ref, d_ref, o_ref):
        cv = c_ref[...]
        o_ref[...] = hdot_tn(cv * sigmoid(cv), d_ref[...])

    return pl.pallas_call(
        body, name="ada_weight_grad",
        out_shape=jax.ShapeDtypeStruct((c_all.shape[1], dmod_cols.shape[1]), F32),
    )(c_all, dmod_cols)


def adamw_update(g, w, m, v):
    nm = ADAM_B1 * m + (1.0 - ADAM_B1) * g
    nv = ADAM_B2 * v + (1.0 - ADAM_B2) * (g * g)
    m_hat = nm / (1.0 - ADAM_B1 ** ADAM_STEP)
    v_hat = nv / (1.0 - ADAM_B2 ** ADAM_STEP)
    return -ADAM_LR * (m_hat / (jnp.sqrt(v_hat) + ADAM_EPS) + ADAM_WD * w), nm, nv


def adamw(parts, w, m, v, name):
    k, rows, cols = parts.shape
    rb = 128 if rows % 128 == 0 else rows

    def body(p_ref, w_ref, m_ref, v_ref, g_ref, d_ref, nm_ref, nv_ref):
        g = p_ref[0].astype(F32)
        for i in range(1, k):
            g = g + p_ref[i].astype(F32)
        g_ref[0] = g
        d_ref[0], nm_ref[0], nv_ref[0] = adamw_update(g, w_ref[0], m_ref[0], v_ref[0])

    blk = pl.BlockSpec((1, rb, cols), lambda i: (0, i, 0))
    return pl.pallas_call(
        body, name=name, grid=(rows // rb,),
        in_specs=[pl.BlockSpec((k, rb, cols), lambda i: (0, i, 0)), blk, blk, blk],
        out_specs=[blk] * 4, out_shape=[jax.ShapeDtypeStruct((1, rows, cols), F32)] * 4,
        compiler_params=pltpu.CompilerParams(dimension_semantics=("arbitrary",), vmem_limit_bytes=VMEM_LIMIT),
    )(parts, w, m, v)


def adamw_small(parts, ws, ms, vs):
    k = parts.shape[0]
    n = len(ws)
    sizes = [w.shape[1] for w in ws]

    def body(p_ref, *refs):
        ins, outs = refs[:3 * n], refs[3 * n:]
        g_all = p_ref[0]
        for i in range(1, k):
            g_all = g_all + p_ref[i]
        off = 0
        for a, size in enumerate(sizes):
            g = g_all[:, off:off + size]
            off += size
            d, nm, nv = adamw_update(g, ins[a][...], ins[n + a][...], ins[2 * n + a][...])
            for kind, val in enumerate((g, d, nm, nv)):
                outs[kind * n + a][...] = val

    return pl.pallas_call(
        body, name="adamw_small",
        out_shape=[jax.ShapeDtypeStruct((1, size), F32) for _ in range(4) for size in sizes],
    )(parts, *ws, *ms, *vs)


def columns_from_shards(g, rows, cols):
    return g.reshape(N_DEV, rows, cols).transpose(1, 0, 2).reshape(rows, N_DEV * cols)


def permute_w_in_t(wt):
    z = lambda n: jnp.zeros((n, D_MODEL), wt.dtype)
    krope = wt[N_KROPE:N_KROPE + ROPE]
    krope_rot = jnp.concatenate([-krope[ROPE // 2:], krope[:ROPE // 2]], axis=0)
    rw = N_RWKV
    return jnp.concatenate([
        wt[N_MA:N_MA + 1024], wt[N_MB:N_MB + 1024],
        wt[rw:rw + 512], wt[rw + 512:rw + 1024], wt[rw + 1024:rw + 1536],
        wt[N_GPA:N_GPA + 512], wt[N_GPB:N_GPB + 512],
        wt[N_QC:N_QC + 256], wt[N_KVC:N_KVC + 128],
        z(NOPE), krope, z(LANE - QK_DIM), z(NOPE), krope_rot, z(LANE - QK_DIM),
        wt[rw + 1536:rw + 1664]], axis=0)


def pad_heads_q(w_uq):
    w = w_uq.reshape(Q_RANK, HEADS, QK_DIM)
    zpad = jnp.zeros((Q_RANK, HEADS, LANE - QK_DIM), w.dtype)
    wq = jnp.concatenate([w, zpad], axis=2).reshape(Q_RANK, HEADS * LANE)
    pe = w[:, :, NOPE:]
    rot = jnp.concatenate([-pe[:, :, ROPE // 2:], pe[:, :, :ROPE // 2]], axis=2)
    wqr = jnp.concatenate([jnp.zeros((Q_RANK, HEADS, NOPE), w.dtype), rot, zpad], axis=2).reshape(Q_RANK, HEADS * LANE)
    return wq, wqr


def unpad_heads_q_grad(dwq, dwqr):
    a = dwq.reshape(Q_RANK, HEADS, LANE)
    r = dwqr.reshape(Q_RANK, HEADS, LANE)[:, :, NOPE:QK_DIM]
    pe = a[:, :, NOPE:QK_DIM] + jnp.concatenate([r[:, :, ROPE // 2:], -r[:, :, :ROPE // 2]], axis=2)
    return jnp.concatenate([a[:, :, :NOPE], pe], axis=2).reshape(Q_RANK, HEADS * QK_DIM)


def pad_heads_kv(w_ukv):
    w = w_ukv.reshape(KV_RANK, HEADS, 2 * HEAD)
    z = jnp.zeros((KV_RANK, HEADS, HEAD), w.dtype)
    wkn = jnp.concatenate([w[:, :, :NOPE], z], axis=2).reshape(KV_RANK, HEADS * LANE)
    val = w[:, :, NOPE:]
    odd = (jnp.arange(HEADS) % 2 == 1)[None, :, None]
    wv = jnp.concatenate([jnp.where(odd, 0, val), jnp.where(odd, val, 0)], axis=2).reshape(KV_RANK, HEADS * LANE)
    return wkn, wv


def unpad_heads_kv_grad(dwkn, dwv):
    a = dwkn.reshape(KV_RANK, HEADS, LANE)[:, :, :NOPE]
    b = dwv.reshape(KV_RANK, HEADS, LANE)
    odd = (jnp.arange(HEADS) % 2 == 1)[None, :, None]
    val = jnp.where(odd, b[:, :, HEAD:], b[:, :, :HEAD])
    return jnp.concatenate([a, val], axis=2).reshape(KV_RANK, HEADS * 2 * HEAD)


def kernel(x, c, positions, w_ada, b_ada, w_in, q_norm_g, w_uq, kv_norm_g, w_ukv, mu_rwkv, w0, w_decay_up, a0, w_iclr_up, k_k, k_a, r_k, gn_g, gn_b, w_proj_a, w_proj_b, w_out, post_g, post_b, loss_target, m_w_ada, m_b_ada, m_w_in, m_q_norm_g, m_w_uq, m_kv_norm_g, m_w_ukv, m_mu_rwkv, m_w0, m_w_decay_up, m_a0, m_w_iclr_up, m_k_k, m_k_a, m_r_k, m_gn_g, m_gn_b, m_w_proj_a, m_w_proj_b, m_w_out, m_post_g, m_post_b, v_w_ada, v_b_ada, v_w_in, v_q_norm_g, v_w_uq, v_kv_norm_g, v_w_ukv, v_mu_rwkv, v_w0, v_w_decay_up, v_a0, v_w_iclr_up, v_k_k, v_k_a, v_r_k, v_gn_g, v_gn_b, v_w_proj_a, v_w_proj_b, v_w_out, v_post_g, v_post_b):
    weights = dict(w_ada=w_ada, b_ada=b_ada, w_in=w_in, q_norm_g=q_norm_g, w_uq=w_uq, kv_norm_g=kv_norm_g,
                   w_ukv=w_ukv, mu_rwkv=mu_rwkv, w0=w0, w_decay_up=w_decay_up, a0=a0, w_iclr_up=w_iclr_up,
                   k_k=k_k, k_a=k_a, r_k=r_k, gn_g=gn_g, gn_b=gn_b, w_proj_a=w_proj_a, w_proj_b=w_proj_b,
                   w_out=w_out, post_g=post_g, post_b=post_b)
    mom1 = dict(w_ada=m_w_ada, b_ada=m_b_ada, w_in=m_w_in, q_norm_g=m_q_norm_g, w_uq=m_w_uq, kv_norm_g=m_kv_norm_g,
                w_ukv=m_w_ukv, mu_rwkv=m_mu_rwkv, w0=m_w0, w_decay_up=m_w_decay_up, a0=m_a0, w_iclr_up=m_w_iclr_up,
                k_k=m_k_k, k_a=m_k_a, r_k=m_r_k, gn_g=m_gn_g, gn_b=m_gn_b, w_proj_a=m_w_proj_a, w_proj_b=m_w_proj_b,
                w_out=m_w_out, post_g=m_post_g, post_b=m_post_b)
    mom2 = dict(w_ada=v_w_ada, b_ada=v_b_ada, w_in=v_w_in, q_norm_g=v_q_norm_g, w_uq=v_w_uq, kv_norm_g=v_kv_norm_g,
                w_ukv=v_w_ukv, mu_rwkv=v_mu_rwkv, w0=v_w0, w_decay_up=v_w_decay_up, a0=v_a0, w_iclr_up=v_w_iclr_up,
                k_k=v_k_k, k_a=v_k_a, r_k=v_r_k, gn_g=v_gn_g, gn_b=v_gn_b, w_proj_a=v_w_proj_a, w_proj_b=v_w_proj_b,
                w_out=v_w_out, post_g=v_post_g, post_b=v_post_b)
    names = list(weights)
    n_rows = x.shape[1]
    me = 4 * lax.axis_index("x") + 2 * lax.axis_index("y") + lax.axis_index("c")
    xr = x[0]
    tgt = loss_target[0]
    row = lambda a: a.reshape(1, -1)

    w_in_all, c_all = gather_shards([w_in[0].T.astype(BF16), c])
    c_all = c_all.reshape(N_DEV, D_MODEL)
    w_in_pt = permute_w_in_t(w_in_all.reshape(IN_WIDTH, D_MODEL))

    mod_all = ada_modulation(c_all, w_ada[0], b_ada.reshape(N_DEV, -1))
    mod = lax.dynamic_index_in_dim(mod_all, me, axis=1, keepdims=False).reshape(3, D_MODEL)

    proj, *gathered = fwd_in_gather(xr, mod, w_in_pt, [weights[n][0].astype(BF16) for n, _, _ in SHARDED[1:]])
    pcol = lambda off_, w: (proj, w, off_ // w)
    full = {}
    for (n, r, cdim), part in zip(SHARDED[1:], gathered):
        full[n] = part.reshape(N_DEV * r, cdim) if n == "w_out" else columns_from_shards(part, r, cdim)
    wq, wqr = pad_heads_q(full["w_uq"])
    wkn, wv = pad_heads_kv(full["w_ukv"])
    zl = jnp.zeros((LORA, WIDTH), BF16)
    w_dec = jnp.concatenate([full["w_decay_up"], zl], axis=0)
    w_iclr = jnp.concatenate([zl, full["w_iclr_up"]], axis=0)
    wpa, wpb, wout = full["w_proj_a"], full["w_proj_b"], full["w_out"]

    inv_freq = ROPE_THETA ** (-jnp.arange(0, ROPE, 2, dtype=F32) / ROPE)
    ang = positions[0].astype(F32)[:, None] * inv_freq
    ones_n, zeros_n, zeros_p = jnp.ones((n_rows, NOPE), F32), jnp.zeros((n_rows, NOPE), F32), jnp.zeros((n_rows, LANE - QK_DIM), F32)
    cos_t = jnp.concatenate([ones_n, jnp.cos(ang), jnp.cos(ang), zeros_p], axis=1)
    sin_t = jnp.concatenate([zeros_n, jnp.sin(ang), jnp.sin(ang), zeros_p], axis=1)

    gq, gkv = q_norm_g, kv_norm_g
    mla_consts = [gq, gkv, wq, wqr, wkn, wv]
    q, k, v = row_call(
        "mla_prep", mla_prep_tile, n_rows,
        [pcol(P_QC, 256), pcol(P_KVC, 128), pcol(P_KR, 128), pcol(P_KRR, 128), (cos_t, LANE, 0), (sin_t, LANE, 0)],
        mla_consts, [(HEADS * LANE, BF16)] * 3)
    ya, lse = attention_forward(q, k, v)

    t_idx = jnp.arange(ROW_TILE)
    same_chunk = (t_idx[:, None] // CHUNK) == (t_idx[None, :] // CHUNK)
    same = same_chunk.astype(F32)
    tril = (same_chunk & (t_idx[:, None] >= t_idx[None, :])).astype(F32)
    l_idx = jnp.arange(LANE)
    bd = ((l_idx[:, None] // HEAD) == (l_idx[None, :] // HEAD)).astype(F32)
    mu = mu_rwkv
    mu_r, mu_k, mu_v, mu_l = mu[:, 0:512], mu[:, 512:1024], mu[:, 1024:1536], mu[:, 1536:1664]
    rk_row = row(r_k)
    rwkv_consts = [mu_r, mu_k, mu_v, mu_l, w0, a0, k_k, k_a, w_dec, w_iclr, tril, same, bd]
    rwkv_rows = [pcol(P_R, 512), pcol(P_K, 512), pcol(P_V, 512), pcol(P_LORA, 128)]
    rt, at, bt, kt, clf, uv, ur, k2 = row_call(
        "rwkv_prep", rwkv_prep_tile, n_rows, rwkv_rows, rwkv_consts, [(WIDTH, F32)] * 8, halo_in=rwkv_rows)
    y, m0s, state_maps, out_maps, *wkv_saved = wkv_forward(at, bt, kt, rt, uv, clf)

    tail = row_call(
        "tail", tail_tile, n_rows,
        [(xr, D_MODEL, 0), (tgt, D_MODEL, 0), pcol(P_MA, 1024), pcol(P_MB, 1024), pcol(P_GPA, 512), pcol(P_GPB, 512),
         (ya, WIDTH, 0), (y, WIDTH, 0), (ur, WIDTH, 0), (k2, WIDTH, 0), (uv, WIDTH, 0)],
        [mod, wpa, wpb, wout, gn_g, gn_b, rk_row, post_g, post_b, bd],
        [(D_MODEL, F32), (1024, BF16), (1024, BF16), (512, BF16), (512, BF16), (WIDTH, F32), (WIDTH, F32), (WIDTH, F32)],
        acc_out=[((1, LANE), F32), ((D_MODEL, D_MODEL), F32), ((WIDTH, D_MODEL), F32), ((WIDTH, D_MODEL), F32),
                 ((1, WIDTH), F32), ((1, WIDTH), F32), ((1, D_MODEL), F32), ((1, D_MODEL), F32), ((1, D_MODEL), F32)])
    (dz, dma, dmb, dgpa, dgpb, dya, dy, dyb,
     loss_row, g_wout, g_wpa, g_wpb, g_gn_g, g_gn_b, g_post_g, g_post_b, dgate) = tail

    dq, dk, dv = attention_backward(q, k, v, ya, dya, lse)
    dq_c, dkv_c, dkr, dkrr, g_wq, g_wqr, g_wkn, g_wv, g_gq, g_gkv = row_call(
        "mla_prep_bwd", mla_prep_bwd_tile, n_rows,
        [pcol(P_QC, 256), pcol(P_KVC, 128), (cos_t, LANE, 0), (sin_t, LANE, 0),
         (dq, HEADS * LANE, 0), (dk, HEADS * LANE, 0), (dv, HEADS * LANE, 0)],
        mla_consts, [(256, BF16), (128, BF16), (128, BF16), (128, BF16)],
        acc_out=[((Q_RANK, HEADS * LANE), F32)] * 2 + [((KV_RANK, HEADS * LANE), F32)] * 2
        + [((1, Q_RANK), F32), ((1, KV_RANK), F32)])

    dat, dbt, dkt, drt, dvv, dlw = wkv_backward(at, bt, kt, rt, uv, clf, m0s, state_maps, out_maps, wkv_saved, dy)
    (dr0, dk0, dv0, dl0, g_mu_r, g_mu_k, g_mu_v, g_mu_l, g_w0, g_a0, g_k_k, g_k_a, g_r_k, g_wdec, g_wiclr) = row_call(
        "rwkv_prep_bwd", rwkv_prep_bwd_tile, n_rows,
        rwkv_rows + [(drt, WIDTH, 0), (dat, WIDTH, 0), (dbt, WIDTH, 0), (dkt, WIDTH, 0), (dvv, WIDTH, 0),
                     (dlw, WIDTH, 0), (dyb, WIDTH, 0)],
        rwkv_consts + [rk_row], [(512, BF16), (512, BF16), (512, BF16), (128, BF16)],
        acc_out=[((1, 512), F32)] * 3 + [((1, 128), F32)] + [((1, 512), F32)] * 5 + [((LANE, WIDTH), F32)] * 2,
        halo_in=rwkv_rows, carry=[512, 512, 512, 128], reverse=True)

    li = jnp.arange(LANE)
    src, dst = li[:, None], li[None, :]
    half = ROPE // 2
    unrot = (jnp.where((dst >= NOPE) & (dst < NOPE + half) & (src == dst + half), 1.0, 0.0)
             - jnp.where((dst >= NOPE + half) & (dst < QK_DIM) & (src == dst - half), 1.0, 0.0)).astype(BF16)
    dx, h_t, dproj_blocks, dshift, dscale = in_backward(
        xr, dz, [dma, dmb, dr0, dk0, dv0, dgpa, dgpb, dq_c, dkv_c, dkr, dkrr, dl0], mod, w_in_pt, unrot)

    grads_full = {
        "w_uq": unpad_heads_q_grad(g_wq, g_wqr), "w_ukv": unpad_heads_kv_grad(g_wkn, g_wv),
        "w_decay_up": g_wdec[:LORA], "w_iclr_up": g_wiclr[LORA:],
        "w_proj_a": g_wpa, "w_proj_b": g_wpb, "w_out": g_wout}
    blocks = [(grads_full[n].reshape(N_DEV, r, cdim) if n == "w_out"
               else grads_full[n].reshape(r, N_DEV, cdim).transpose(1, 0, 2)).astype(BF16) for n, r, cdim in SHARDED[1:]]
    dmod = jnp.concatenate([dshift, dscale, dgate], axis=1)
    small = jnp.concatenate([dmod, g_gq, g_gkv, g_mu_r, g_mu_k, g_mu_v, g_mu_l, g_w0, g_a0, g_k_k, g_k_a, g_r_k,
                             g_gn_g, g_gn_b, g_post_g, g_post_b, loss_row], axis=1)
    my_x, my_y, my_c = lax.axis_index("x"), lax.axis_index("y"), lax.axis_index("c")
    chip_order = [4 * (my_x ^ fx) + 2 * (my_y ^ fy) for fx, fy in ((1, 1), (1, 0), (0, 1), (0, 0))]
    owners = [chip_order[s % 4] + (my_c if s >= 4 else 1 - my_c) for s in WGRAD_SLOTS]
    order = jnp.stack(owners + [jnp.int32(s) for s in WGRAD_SLOTS]).astype(jnp.int32)
    *got_blocks, got_small = in_weight_grad_exchange(h_t, dproj_blocks, blocks, small, order)
    loss = jnp.sum(got_small[:, 0, SMALL_ELEMS])

    ada_cols = w_ada.shape[2]
    dmod_all = got_small[:, 0, :3 * D_MODEL]
    g_ada = ada_weight_grad(c_all, lax.dynamic_slice_in_dim(dmod_all, me * ada_cols, ada_cols, axis=1))

    outs = [dict() for _ in range(4)]
    res = adamw(g_ada[None], w_ada, m_w_ada, v_w_ada, "adamw_w_ada")
    for kind in range(4):
        outs[kind]["w_ada"] = res[kind]
    for (n, r, cdim), got in zip(SHARDED, got_blocks):
        res = adamw(got, weights[n], mom1[n], mom2[n], "adamw_" + n)
        for kind in range(4):
            outs[kind][n] = res[kind]
    rows_of = lambda tree: [tree[n].reshape(1, -1) for n, _ in SMALL]
    res = adamw_small(got_small, rows_of(weights), rows_of(mom1), rows_of(mom2))
    for kind in range(4):
        for a, (n, _) in enumerate(SMALL):
            outs[kind][n] = res[kind * len(SMALL) + a].reshape(weights[n].shape)
    return (loss, dx[None], *[outs[0][n] for n in names], *[outs[1][n] for n in names],
            *[outs[2][n] for n in names], *[outs[3][n] for n in names])
```

```python
import functools
import math

import jax
import jax.numpy as jnp
from jax import lax
from jax.experimental import pallas as pl
from jax.experimental.pallas import tpu as pltpu

F32 = jnp.float32
BF16 = jnp.bfloat16
HIGHEST = lax.Precision.HIGHEST
MESH_IDS = pl.DeviceIdType.MESH

N_DEV = 8
D_MODEL = 1024
LN_EPS = 1e-5
RMS_EPS = 1e-6
GN_EPS = 64e-5
HEADS = 8
Q_RANK = 256
KV_RANK = 128
ROPE = 32
NOPE = 64
QK_DIM = NOPE + ROPE
WIDTH = 512
HEAD = 64
LORA = 64
CHUNK = 64
DEPTH = 1
ALPHA = (2.0 * DEPTH) ** 0.25
ROPE_THETA = 10000.0
ATTN_SCALE = QK_DIM ** -0.5
DECAY_SCALE = math.exp(-0.5)

ADAM_LR = 0.001
ADAM_B1 = 0.9
ADAM_B2 = 0.999
ADAM_EPS = 1e-08
ADAM_WD = 0.01
ADAM_STEP = 10

LANE = 128
PAIR = 2 * HEAD
ROW_TILE = 256
PREP_TILE = 512
HALO_ROWS = 16
ATTN_FWD_TILES = (512, 1024)
ATTN_BWD_TILES = (512, 512)
LOG2_E = math.log2(math.e)
Q_PRESCALE = ATTN_SCALE * LOG2_E
WKV_CHUNKS_PER_STEP = 8
WGRAD_SLOTS = (0, 1, 4, 2, 5, 6, 3, 7)
VMEM_LIMIT = 56 * 1024 * 1024

P_MA, P_MB, P_R, P_K, P_V, P_GPA, P_GPB, P_QC, P_KVC, P_KR, P_KRR, P_LORA = (
    0, 1024, 2048, 2560, 3072, 3584, 4096, 4608, 4864, 4992, 5120, 5248)
P_WIDTH = 5376

N_QC, N_KVC, N_KROPE, N_GPA, N_RWKV, N_GPB, N_MA, N_MB = 0, 256, 384, 416, 928, 2592, 3104, 4128
IN_WIDTH = 5152

SHARDED = (("w_in", 1024, 644), ("w_uq", 256, 96), ("w_ukv", 128, 128), ("w_decay_up", 64, 64),
           ("w_iclr_up", 64, 64), ("w_proj_a", 512, 128), ("w_proj_b", 512, 128), ("w_out", 128, 1024))
SMALL = (("b_ada", 3072), ("q_norm_g", 256), ("kv_norm_g", 128), ("mu_rwkv", 1664), ("w0", 512), ("a0", 512),
         ("k_k", 512), ("k_a", 512), ("r_k", 512), ("gn_g", 512), ("gn_b", 512), ("post_g", 1024), ("post_b", 1024))
SMALL_ELEMS = sum(n for _, n in SMALL)


def mm(a, b):
    return jnp.dot(a.astype(BF16), b.astype(BF16), preferred_element_type=F32)


def mm_nt(a, b):
    return lax.dot_general(a.astype(BF16), b.astype(BF16), (((1,), (1,)), ((), ())), preferred_element_type=F32)


def mm_tn(a, b):
    return lax.dot_general(a.astype(BF16), b.astype(BF16), (((0,), (0,)), ((), ())), preferred_element_type=F32)


def hdot(a, b):
    return jnp.dot(a, b, precision=HIGHEST, preferred_element_type=F32)


def hdot_tn(a, b):
    return lax.dot_general(a, b, (((0,), (0,)), ((), ())), precision=HIGHEST, preferred_element_type=F32)


def sigmoid(x):
    return 1.0 / (1.0 + jnp.exp(-x))


def colsum(x):
    return jnp.sum(x, axis=0, keepdims=True)


def rowmean(x):
    return jnp.mean(x, axis=-1, keepdims=True)


def layer_norm_stats(x):
    xc = x - rowmean(x)
    rstd = lax.rsqrt(rowmean(xc * xc) + LN_EPS)
    return xc * rstd, rstd


def layer_norm_bwd(dy, xhat, rstd):
    return rstd * (dy - rowmean(dy) - xhat * rowmean(dy * xhat))


def bf16_pieces(x, n):
    pieces = []
    for _ in range(n):
        p = x.astype(BF16)
        pieces.append(p)
        x = x - p.astype(F32)
    return pieces


def ones_dot(ones, x, n_pieces):
    ones = ones.astype(BF16)
    return sum(jnp.dot(ones, p, preferred_element_type=F32) for p in bf16_pieces(x, n_pieces))


def ones_dot_nt(ones, x, n_pieces):
    ones = ones.astype(BF16)
    return sum(lax.dot_general(ones, p, (((1,), (1,)), ((), ())), preferred_element_type=F32)
               for p in bf16_pieces(x, n_pieces))


def head_sum(x, bd):
    return jnp.concatenate([mm(x[:, p * LANE:(p + 1) * LANE], bd) for p in range(x.shape[1] // LANE)], axis=1)


def tile_lanes(t, n):
    return jnp.concatenate([t] * n, axis=1)


def row_iota(shape):
    return lax.broadcasted_iota(jnp.int32, shape, 0)


def lane_iota(shape):
    return lax.broadcasted_iota(jnp.int32, shape, 1)


def shift_rows_down(x, row0):
    rolled = pltpu.roll(x, 1, axis=0)
    return jnp.where(row_iota(x.shape) == 0, row0, rolled)


def shift_rows_up(x, row_last):
    rolled = pltpu.roll(x, x.shape[0] - 1, axis=0)
    return jnp.where(row_iota(x.shape) == x.shape[0] - 1, row_last, rolled)


def row_call(name, fn, n_rows, row_in, const_in, row_out, acc_out=(), halo_in=(), carry=(), reverse=False,
             tile_rows=ROW_TILE):
    ts = tile_rows
    n_tiles = n_rows // ts
    n_in = len(row_in) + len(halo_in) + len(const_in)
    n_ro, n_ao = len(row_out), len(acc_out)

    def tile_of(g):
        return (n_tiles - 1 - g) if reverse else g

    def body(*refs):
        ins = refs[:n_in]
        ro = refs[n_in:n_in + n_ro]
        ao = refs[n_in + n_ro:n_in + n_ro + n_ao]
        cr = refs[n_in + n_ro + n_ao:]
        g = pl.program_id(0)
        step0 = g == 0
        tile0 = tile_of(g) == 0
        for r in cr:
            @pl.when(step0)
            def _(r=r):
                r[...] = jnp.zeros_like(r)
        n_tiled = len(row_in) + len(halo_in)
        vals = [r[...].astype(F32) for r in ins[:n_tiled]] + [r[...] for r in ins[n_tiled:]]
        outs = fn(step0, tile0, *vals, *[c[0:1, :] for c in cr])
        for r, v in zip(ro, outs[:n_ro]):
            r[...] = v.astype(r.dtype)
        for r, v in zip(ao, outs[n_ro:n_ro + n_ao]):
            @pl.when(step0)
            def _(r=r, v=v):
                r[...] = v.astype(r.dtype)

            @pl.when(jnp.logical_not(step0))
            def _(r=r, v=v):
                r[...] += v.astype(r.dtype)
        for r, v in zip(cr, outs[n_ro + n_ao:]):
            r[0:1, :] = v

    in_specs = [pl.BlockSpec((ts, w), functools.partial(lambda g, cb: (tile_of(g), cb), cb=cb)) for _, w, cb in row_in]
    in_specs += [pl.BlockSpec((HALO_ROWS, w), functools.partial(
        lambda g, cb: (jnp.maximum(tile_of(g) * (ts // HALO_ROWS) - 1, 0), cb), cb=cb)) for _, w, cb in halo_in]
    in_specs += [pl.BlockSpec(memory_space=pltpu.VMEM) for _ in const_in]
    out_specs = [pl.BlockSpec((ts, w), lambda g: (tile_of(g), 0)) for w, _ in row_out]
    out_specs += [pl.BlockSpec(s, lambda g: (0, 0)) for s, _ in acc_out]
    out_shape = [jax.ShapeDtypeStruct((n_rows, w), d) for w, d in row_out]
    out_shape += [jax.ShapeDtypeStruct(s, d) for s, d in acc_out]
    return pl.pallas_call(
        body, name=name, grid=(n_tiles,), in_specs=in_specs, out_specs=out_specs, out_shape=out_shape,
        scratch_shapes=[pltpu.VMEM((8, w), F32) for w in carry],
        compiler_params=pltpu.CompilerParams(dimension_semantics=("arbitrary",), vmem_limit_bytes=VMEM_LIMIT),
    )(*[a for a, _, _ in row_in], *[a for a, _, _ in halo_in], *const_in)


def my_position():
    return lax.axis_index("x"), lax.axis_index("y"), lax.axis_index("c")


def flip(pos, k):
    x, y, c = pos
    dx, dy, dc = (k >> 2) & 1, (k >> 1) & 1, k & 1
    return (1 - x if dx else x, 1 - y if dy else y, 1 - c if dc else c)


def flat_index(pos):
    return 4 * pos[0] + 2 * pos[1] + pos[2]


def gather_shards(shards):
    n = len(shards)

    def body(*refs):
        x_refs, out_refs = refs[:n], refs[n:2 * n]
        send_sems, recv_sems, local_sems = refs[2 * n:]
        x, y, c = my_position()
        me, sibling = (x, y, c), (x, y, 1 - c)
        chips = [(1 - x, y), (x, 1 - y), (1 - x, 1 - y)]

        def copy(a, k, block, to, from_input=False):
            slot = out_refs[a].at[flat_index(block)]
            return pltpu.make_async_remote_copy(
                src_ref=x_refs[a] if from_input else slot, dst_ref=slot,
                send_sem=send_sems.at[7 * a + k], recv_sem=recv_sems.at[7 * a + k],
                device_id=to, device_id_type=MESH_IDS)

        mine = [pltpu.make_async_copy(x_refs[a], out_refs[a].at[flat_index(me)], local_sems.at[a]) for a in range(n)]
        for cp in mine:
            cp.start()
        first = []
        for a in range(n):
            first.append(copy(a, 0, me, sibling, from_input=True))
            first += [copy(a, 1 + j, me, (*chip, c), from_input=True) for j, chip in enumerate(chips)]
        for cp in first:
            cp.start()
        passed = []
        for j, chip in enumerate(chips):
            for a in range(n):
                copy(a, 1 + j, (*chip, c), me).wait_recv()
                cp = copy(a, 4 + j, (*chip, c), sibling)
                cp.start()
                passed.append(cp)
        for a in range(n):
            copy(a, 0, sibling, me).wait_recv()
            for j, chip in enumerate(chips):
                copy(a, 4 + j, (*chip, 1 - c), me).wait_recv()
        for cp in first + passed:
            cp.wait_send()
        for cp in mine:
            cp.wait()

    return pl.pallas_call(
        body, name="gather_shards",
        out_shape=[jax.ShapeDtypeStruct((N_DEV,) + s.shape, s.dtype) for s in shards],
        in_specs=[pl.BlockSpec(memory_space=pl.ANY)] * n, out_specs=[pl.BlockSpec(memory_space=pl.ANY)] * n,
        scratch_shapes=[pltpu.SemaphoreType.DMA((7 * n,)), pltpu.SemaphoreType.DMA((7 * n,)),
                        pltpu.SemaphoreType.DMA((n,))],
    )(*shards)


def ada_modulation(c_all, w_ada_loc, b_ada_blocks):
    cols = w_ada_loc.shape[1]

    def body(c_ref, w_ref, b_ref, out_ref, send_sems, recv_sems):
        me = my_position()
        mi = flat_index(me)
        cv = c_ref[...]
        res = hdot(cv * sigmoid(cv), w_ref[...]) + b_ref[pl.ds(mi, 1), :]
        out_ref[mi] = res
        sends = []
        for k in range(1, N_DEV):
            cp = pltpu.make_async_remote_copy(
                src_ref=out_ref.at[mi], dst_ref=out_ref.at[mi], send_sem=send_sems.at[k - 1],
                recv_sem=recv_sems.at[k - 1], device_id=flip(me, k), device_id_type=MESH_IDS)
            cp.start()
            sends.append(cp)
        for k in range(1, N_DEV):
            pi = flat_index(flip(me, k))
            pltpu.make_async_remote_copy(
                src_ref=out_ref.at[pi], dst_ref=out_ref.at[pi], send_sem=send_sems.at[k - 1],
                recv_sem=recv_sems.at[k - 1], device_id=flip(me, k), device_id_type=MESH_IDS).wait_recv()
        for cp in sends:
            cp.wait_send()

    return pl.pallas_call(
        body, name="ada_modulation",
        out_shape=jax.ShapeDtypeStruct((N_DEV, N_DEV, cols), F32),
        in_specs=[pl.BlockSpec(memory_space=pltpu.VMEM)] * 3, out_specs=pl.BlockSpec(memory_space=pltpu.VMEM),
        scratch_shapes=[pltpu.SemaphoreType.DMA((7,)), pltpu.SemaphoreType.DMA((7,))],
    )(c_all, w_ada_loc, b_ada_blocks)


def fwd_in_tile(step0, tile0, x, mod, w_in_ptt):
    xhat, _ = layer_norm_stats(x)
    h = xhat * (1.0 + mod[1:2]) + mod[0:1]
    return (mm_nt(h, w_in_ptt),)


def fwd_in_gather(x, mod, w_in_pt, shards):
    n = len(shards)
    n_rows = x.shape[0]
    ts = ROW_TILE
    n_tiles = n_rows // ts

    def body(x_ref, mod_ref, w_ref, *rest):
        s_refs = rest[:n]
        proj_ref, out_refs = rest[n], rest[n + 1:2 * n + 1]
        send_sems, recv_sems, local_sems = rest[2 * n + 1:]
        g = pl.program_id(0)
        me = my_position()
        mi = flat_index(me)

        def copies(k, slot):
            return [pltpu.make_async_remote_copy(
                src_ref=s_refs[a], dst_ref=out_refs[a].at[slot], send_sem=send_sems.at[7 * a + k - 1],
                recv_sem=recv_sems.at[7 * a + k - 1], device_id=flip(me, k), device_id_type=MESH_IDS)
                for a in range(n)]

        local = [pltpu.make_async_copy(s_refs[a], out_refs[a].at[mi], local_sems.at[a]) for a in range(n)]

        @pl.when(g == 0)
        def _():
            for cp in local:
                cp.start()
            for k in range(1, N_DEV):
                for cp in copies(k, mi):
                    cp.start()

        proj_ref[...] = fwd_in_tile(None, None, x_ref[...], mod_ref[...], w_ref[...])[0].astype(BF16)

        @pl.when(g == n_tiles - 1)
        def _():
            for k in range(1, N_DEV):
                for cp in copies(k, flat_index(flip(me, k))):
                    cp.wait_recv()
            for k in range(1, N_DEV):
                for cp in copies(k, mi):
                    cp.wait_send()
            for cp in local:
                cp.wait()

    hbm = pl.BlockSpec(memory_space=pl.ANY)
    const = pl.BlockSpec(memory_space=pltpu.VMEM)
    return pl.pallas_call(
        body, name="fwd_in_gather", grid=(n_tiles,),
        in_specs=[pl.BlockSpec((ts, D_MODEL), lambda g: (g, 0)), const, const] + [hbm] * n,
        out_specs=[pl.BlockSpec((ts, P_WIDTH), lambda g: (g, 0))] + [hbm] * n,
        out_shape=[jax.ShapeDtypeStruct((n_rows, P_WIDTH), BF16)]
        + [jax.ShapeDtypeStruct((N_DEV,) + s.shape, s.dtype) for s in shards],
        scratch_shapes=[pltpu.SemaphoreType.DMA((7 * n,)), pltpu.SemaphoreType.DMA((7 * n,)),
                        pltpu.SemaphoreType.DMA((n,))],
        compiler_params=pltpu.CompilerParams(dimension_semantics=("arbitrary",), vmem_limit_bytes=VMEM_LIMIT),
    )(x, mod, w_in_pt, *shards)


def rms_norm_fwd(x, g):
    r = lax.rsqrt(rowmean(x * x) + RMS_EPS)
    xh = x * r
    return xh * g, xh, r


def key_rope_mask(shape):
    return (lane_iota(shape) >= NOPE).astype(F32)


def mla_prep_tile(step0, tile0, q_c, kv_c, kr, krr, cos, sin, gq, gkv, wq, wqr, wkn, wv):
    qn, _, _ = rms_norm_fwd(q_c, gq)
    kvn, _, _ = rms_norm_fwd(kv_c, gkv)
    q = (mm(qn, wq) * tile_lanes(cos, HEADS) + mm(qn, wqr) * tile_lanes(sin, HEADS)) * Q_PRESCALE
    kpe = kr * (cos * key_rope_mask(cos.shape)) + krr * sin
    k = mm(kvn, wkn) + tile_lanes(kpe, HEADS)
    v = mm(kvn, wv)
    return q, k, v


def rwkv_prep_core(tile0, r0, k0, v0, l0, hr, hk, hv, hl, mu_r, mu_k, mu_v, mu_l, w0, a0, k_k, k_a,
                   w_dec, w_iclr, tril, same, bd):
    def shifted(x, halo, mu):
        row0 = jnp.where(tile0, 0.0, halo[HALO_ROWS - 1:HALO_ROWS, :])
        prev = shift_rows_down(x, row0)
        return x + (prev - x) * mu, prev

    ur, pr = shifted(r0, hr, mu_r)
    uk, pk = shifted(k0, hk, mu_k)
    uv, pv = shifted(v0, hv, mu_v)
    ul, plo = shifted(l0, hl, mu_l)
    th = jnp.tanh(ul)
    sg = sigmoid(w0 + mm(th, w_dec))
    lw = -DECAY_SCALE * sg
    a_ic = sigmoid(a0 + mm(ul, w_iclr))
    kkraw = uk * k_k
    nrm_raw = jnp.sqrt(head_sum(kkraw * kkraw, bd))
    nrm = jnp.maximum(nrm_raw, 1e-12)
    kk = kkraw / nrm
    k2 = uk * (1.0 + (a_ic - 1.0) * k_a)
    lc = ones_dot(tril, lw, 3)
    lcl = ones_dot(same, lw, 3)
    return dict(ur=ur, uk=uk, uv=uv, ul=ul, pr=pr, pk=pk, pv=pv, pl=plo, th=th, sg=sg, lw=lw, a_ic=a_ic,
                kkraw=kkraw, nrm_raw=nrm_raw, nrm=nrm, kk=kk, k2=k2, lc=lc, lcl=lcl)


def rwkv_prep_tile(step0, tile0, r0, k0, v0, l0, hr, hk, hv, hl, *consts):
    f = rwkv_prep_core(tile0, r0, k0, v0, l0, hr, hk, hv, hl, *consts)
    lc, lw = f["lc"], f["lw"]
    e_neg = jnp.exp(-lc)
    rt = f["ur"] * jnp.exp(lc)
    at = -f["kk"] * jnp.exp(lc - lw)
    bt = f["kk"] * f["a_ic"] * e_neg
    kt = f["k2"] * e_neg
    return rt, at, bt, kt, jnp.exp(f["lcl"]), f["uv"], f["ur"], f["k2"]


def wkv_masks():
    lane = lane_iota((1, PAIR))
    m_lo = (lane < HEAD).astype(F32)
    r2 = row_iota((PAIR, PAIR))
    c2 = lane_iota((PAIR, PAIR))
    bd = ((r2 < HEAD) == (c2 < HEAD)).astype(F32)
    eye2 = (r2 == c2).astype(F32)
    eye = (row_iota((CHUNK, CHUNK)) == lane_iota((CHUNK, CHUNK))).astype(F32)
    t_idx = row_iota((4 * CHUNK, PAIR)) % CHUNK
    s_idx = lane_iota((4 * CHUNK, PAIR)) % CHUNK
    keep = s_idx < t_idx + (row_iota((4 * CHUNK, PAIR)) >= 2 * CHUNK).astype(jnp.int32)
    return (m_lo, 1.0 - m_lo), keep, eye, bd, eye2


def rows(*parts):
    return jnp.concatenate(parts, axis=0)


def lanes(*parts):
    return jnp.concatenate(parts, axis=1)


def head_rows(x, ms):
    return rows(x * ms[0], x * ms[1])


def wkv_score_stack(at, rt, ms):
    return rows(head_rows(at, ms), head_rows(rt, ms))


def wkv_chunks_pre(chunks, masks):
    ms, keep, eye, bd, eye2 = masks
    n = len(chunks)
    at, bt, kt, rt, v, cl = (list(t) for t in zip(*chunks))
    scores = [jnp.where(keep, mm_nt(wkv_score_stack(a, r, ms), rows(b, k)), 0.0)
              for a, r, b, k in zip(at, rt, bt, kt)]
    q = CHUNK
    aab = [s[h * q:(h + 1) * q, :q] for s in scores for h in range(2)]
    tinv = [eye + a for a in aab]
    power = [mm(a, a) for a in aab]
    for _ in range(5):
        both = [mm(rows(t, p), p) for t, p in zip(tinv, power)]
        tinv = [t + x[:q] for t, x in zip(tinv, both)]
        power = [x[q:] for x in both]
    pair = lambda c, row0, col0: lanes(scores[c][row0:row0 + q, col0:col0 + q],
                                       scores[c][row0 + q:row0 + 2 * q, col0:col0 + q])
    tinv_p = [lanes(tinv[2 * c], tinv[2 * c + 1]) for c in range(n)]
    aak_p = [pair(c, 0, q) for c in range(n)]
    prb_p = [pair(c, 2 * q, 0) for c in range(n)]
    prk_p = [pair(c, 2 * q, q) for c in range(n)]
    v_rows = [head_rows(x, ms) for x in v]
    wy = [mm(rows(a, p), x) for a, p, x in zip(aak_p, prk_p, v_rows)]
    w = [x[:q] for x in wy]
    yh2 = [x[q:] for x in wy]
    aw = [mm(t, lanes(head_rows(a, ms), head_rows(w_, ms))) for t, a, w_ in zip(tinv_p, at, w)]
    ah = [x[:, :PAIR] for x in aw]
    wh = [x[:, PAIR:] for x in aw]
    ry = [mm(p, lanes(head_rows(a, ms), head_rows(w_, ms))) for p, a, w_ in zip(prb_p, ah, wh)]
    rh = [r + x[:, :PAIR] for r, x in zip(rt, ry)]
    yh = [x[:, PAIR:] + y for x, y in zip(ry, yh2)]
    bc = [b * c_ for b, c_ in zip(bt, cl)]
    kc = [k * c_ for k, c_ in zip(kt, cl)]
    gh = [mm_tn(b, lanes(a, w_)) for b, a, w_ in zip(bc, ah, wh)]
    g = [eye2 * c_ + bd * x[:, :PAIR] for c_, x in zip(cl, gh)]
    h = [bd * (x[:, PAIR:] + mm_tn(k, v_)) for x, k, v_ in zip(gh, kc, v)]
    as_bf16 = lambda xs: [x.astype(BF16) for x in xs]
    saved = (as_bf16(tinv_p), as_bf16(aak_p), as_bf16(prb_p), as_bf16(prk_p), as_bf16(ah), wh)
    return g, h, rh, yh, saved


def wkv_chunks_grad(chunks, saved, m0, dy, dm1, masks):
    ms, keep, eye, bd, eye2 = masks
    n = len(chunks)
    q = CHUNK
    at, bt, kt, rt, v, cl = (list(t) for t in zip(*chunks))
    tinv_p, aak_p, prb_p, prk_p, ah, wh = (list(t) for t in zip(*saved))
    head_stack = lambda p: rows(p[:, :q], p[:, q:])
    bc = [b * c_ for b, c_ in zip(bt, cl)]
    kc = [k * c_ for k, c_ in zip(kt, cl)]
    u = [mm(a, m) + w for a, m, w in zip(ah, m0, wh)]
    dm1 = [d * bd for d in dm1]
    from_state = [mm(rows(b, k), d) for b, k, d in zip(bc, kc, dm1)]
    dy_rows = [head_rows(d, ms) for d in dy]
    from_out = [mm_tn(lanes(head_stack(pb), head_stack(pk)), d) for pb, pk, d in zip(prb_p, prk_p, dy_rows)]
    du = [a[:q] + b[:q] for a, b in zip(from_state, from_out)]
    dv = [a[q:] + b[q:] for a, b in zip(from_state, from_out)]
    dz = [mm_tn(head_stack(t), head_rows(d, ms)) for t, d in zip(tinv_p, du)]
    dz_rows = [head_rows(d, ms) for d in dz]
    dv = [a + mm_tn(head_stack(k), d) for a, k, d in zip(dv, aak_p, dz_rows)]
    by_m0 = [mm_nt(rows(d, z), m) for d, z, m in zip(dy, dz, m0)]
    uv = [rows(x, y) for x, y in zip(u, v)]
    by_dm1 = [mm_nt(x, d) for x, d in zip(uv, dm1)]
    udm = [x[:q] for x in by_dm1]
    vdm = [x[q:] for x in by_dm1]
    dscores = [jnp.where(keep, mm_nt(rows(z, d), x), 0.0) for z, d, x in zip(dz_rows, dy_rows, uv)]
    to_ar = [mm(d, rows(b, k)) for d, b, k in zip(dscores, bt, kt)]
    to_bk = [mm_tn(d, wkv_score_stack(a, r, ms)) for d, a, r in zip(dscores, at, rt)]
    ones = jnp.ones((8, PAIR), F32)
    upper = (lane_iota((CHUNK, CHUNK)) >= row_iota((CHUNK, CHUNK))).astype(F32)
    out = []
    for c in range(n):
        e = to_ar[c]
        dat_c = by_m0[c][q:] + e[:q] * ms[0] + e[q:2 * q] * ms[1]
        drt_c = by_m0[c][:q] + e[2 * q:3 * q] * ms[0] + e[3 * q:] * ms[1]
        dbt_c = udm[c] * cl[c] + to_bk[c][:q]
        dkt_c = vdm[c] * cl[c] + to_bk[c][q:]
        dlcl = ones_dot_nt(ones, dm1[c] * m0[c], 3)[0:1, :] * cl[c] + colsum(bc[c] * udm[c] + kc[c] * vdm[c])
        g = drt_c * rt[c] - dbt_c * bt[c] - dkt_c * kt[c] + dat_c * at[c]
        dlw = ones_dot(upper, g, 3) - dat_c * at[c] + dlcl
        out.append((dat_c, dbt_c, dkt_c, drt_c, dv[c], dlw))
    return out


def wkv_forward(at, bt, kt, rt, v, clf):
    n_rows = at.shape[0]
    cps = WKV_CHUNKS_PER_STEP
    rb = cps * CHUNK
    n_steps = n_rows // rb

    def body(a_ref, b_ref, k_ref, r_ref, v_ref, c_ref, y_ref, m0_ref, g_ref, rh_ref, *rest):
        saved_refs, m_scr = rest[:6], rest[6]

        @pl.when(pl.program_id(1) == 0)
        def _():
            m_scr[...] = jnp.zeros_like(m_scr)

        masks = wkv_masks()
        chunks = []
        for cc in range(cps):
            sl = slice(cc * CHUNK, (cc + 1) * CHUNK)
            chunks.append((a_ref[sl, :], b_ref[sl, :], k_ref[sl, :], r_ref[sl, :], v_ref[sl, :],
                           c_ref[cc * CHUNK:cc * CHUNK + 1, :]))
        gs, hs, rhs, yhs, saved = wkv_chunks_pre(chunks, masks)
        for ref, per_chunk in zip(saved_refs, saved):
            for cc, val in enumerate(per_chunk):
                ref[cc * CHUNK:(cc + 1) * CHUNK, :] = val
        m = m_scr[...]
        for cc, (g, h, rh, yh) in enumerate(zip(gs, hs, rhs, yhs)):
            sl = slice(cc * CHUNK, (cc + 1) * CHUNK)
            m0_ref[0, cc] = m
            g_ref[0, cc] = g
            rh_ref[sl, :] = rh
            y_ref[sl, :] = hdot(rh, m) + yh
            m = hdot(g, m) + h
        m_scr[...] = m

    blk = pl.BlockSpec((rb, PAIR), lambda p, s: (s, p))
    state_blk = pl.BlockSpec((1, cps, PAIR, PAIR), lambda p, s: (p, s, 0, 0))
    state_shape = jax.ShapeDtypeStruct((WIDTH // PAIR, n_rows // CHUNK, PAIR, PAIR), F32)
    rows_f32 = jax.ShapeDtypeStruct((n_rows, WIDTH), F32)
    rows_bf16 = jax.ShapeDtypeStruct((n_rows, WIDTH), BF16)
    return pl.pallas_call(
        body, name="wkv_forward", grid=(WIDTH // PAIR, n_steps),
        in_specs=[blk] * 6,
        out_specs=[blk, state_blk, state_blk, blk] + [blk] * 6,
        out_shape=[rows_f32, state_shape, state_shape, rows_f32] + [rows_bf16] * 5 + [rows_f32],
        scratch_shapes=[pltpu.VMEM((PAIR, PAIR), F32)],
        compiler_params=pltpu.CompilerParams(dimension_semantics=("arbitrary", "arbitrary"),
                                             vmem_limit_bytes=VMEM_LIMIT),
    )(at, bt, kt, rt, v, clf)


def wkv_backward(at, bt, kt, rt, v, clf, m0s, gs, rh, saved, dy):
    n_rows = at.shape[0]
    cps = WKV_CHUNKS_PER_STEP
    rb = cps * CHUNK
    n_steps = n_rows // rb

    def body(a_ref, b_ref, k_ref, r_ref, v_ref, c_ref, m0_ref, g_ref, rh_ref, *rest):
        saved_refs, dy_ref = rest[:6], rest[6]
        da_ref, db_ref, dk_ref, dr_ref, dv_ref, dlw_ref, dm_scr = rest[7:]

        @pl.when(pl.program_id(1) == 0)
        def _():
            dm_scr[...] = jnp.zeros_like(dm_scr)

        masks = wkv_masks()
        bd = masks[3]
        dm = dm_scr[...]
        dm1 = [None] * cps
        for cc in reversed(range(cps)):
            sl = slice(cc * CHUNK, (cc + 1) * CHUNK)
            dm1[cc] = dm
            dm = bd * (hdot_tn(g_ref[0, cc], dm) + hdot_tn(rh_ref[sl, :], dy_ref[sl, :]))
        dm_scr[...] = dm
        chunks, kept, m0, dys = [], [], [], []
        for cc in range(cps):
            sl = slice(cc * CHUNK, (cc + 1) * CHUNK)
            chunks.append((a_ref[sl, :], b_ref[sl, :], k_ref[sl, :], r_ref[sl, :], v_ref[sl, :],
                           c_ref[cc * CHUNK:cc * CHUNK + 1, :]))
            kept.append(tuple(ref[sl, :] for ref in saved_refs))
            m0.append(m0_ref[0, cc])
            dys.append(dy_ref[sl, :])
        grads = wkv_chunks_grad(chunks, kept, m0, dys, dm1, masks)
        for cc, (dat, dbt, dkt, drt, dv, dlw) in enumerate(grads):
            sl = slice(cc * CHUNK, (cc + 1) * CHUNK)
            da_ref[sl, :] = dat
            db_ref[sl, :] = dbt
            dk_ref[sl, :] = dkt
            dr_ref[sl, :] = drt
            dv_ref[sl, :] = dv
            dlw_ref[sl, :] = dlw

    blk = pl.BlockSpec((rb, PAIR), lambda p, s: (n_steps - 1 - s, p))
    state_blk = pl.BlockSpec((1, cps, PAIR, PAIR), lambda p, s: (p, n_steps - 1 - s, 0, 0))
    return pl.pallas_call(
        body, name="wkv_backward", grid=(WIDTH // PAIR, n_steps),
        in_specs=[blk] * 6 + [state_blk, state_blk, blk] + [blk] * 6 + [blk],
        out_specs=[blk] * 6,
        out_shape=[jax.ShapeDtypeStruct((n_rows, WIDTH), F32)] * 6,
        scratch_shapes=[pltpu.VMEM((PAIR, PAIR), F32)],
        compiler_params=pltpu.CompilerParams(dimension_semantics=("arbitrary", "arbitrary"),
                                             vmem_limit_bytes=VMEM_LIMIT),
    )(at, bt, kt, rt, v, clf, m0s, gs, rh, *saved, dy)


def visible(q_row0, k_row0, shape):
    qc = (q_row0 + row_iota(shape)) // CHUNK
    kc = (k_row0 + lane_iota(shape)) // CHUNK
    return kc <= qc


def attention_forward(q, k, v):
    n_rows = q.shape[0]
    tq, tk = ATTN_FWD_TILES
    n_q = n_rows // tq
    assert tk % tq == 0

    def body(q_ref, k_ref, v_ref, o_ref, lse_ref):
        i = pl.program_id(1)
        lane = lane_iota((tq, LANE))
        heads = [slice(0, LANE), slice(LANE, 2 * LANE)]
        qs = [q_ref[:, cols] for cols in heads]

        def step(j, carry, size, masked):
            rows = pl.ds(pl.multiple_of(j * size, size), size)
            ss = [mm_nt(qh, k_ref[rows, cols]) for qh, cols in zip(qs, heads)]
            if masked:
                vis = visible(i * tq, j * size, ss[0].shape)
                ss = [jnp.where(vis, s, -jnp.inf) for s in ss]
            ps, stats = [], []
            for s, (m, l, _) in zip(ss, carry):
                m_new = jnp.maximum(m, jnp.max(s, axis=-1, keepdims=True))
                p = jnp.exp2(s - m_new)
                alpha = jnp.exp2(m - m_new)
                ps.append(p)
                stats.append((m_new, alpha, alpha * l + jnp.sum(p, axis=-1, keepdims=True)))
            pvs = [mm(p, v_ref[rows, cols]) for p, cols in zip(ps, heads)]
            return tuple((m_new, l, alpha * acc + pv)
                         for (m_new, alpha, l), (_, _, acc), pv in zip(stats, carry, pvs))

        carry = tuple((jnp.full((tq, 1), -jnp.inf, F32), jnp.zeros((tq, 1), F32), jnp.zeros((tq, LANE), F32))
                      for _ in heads)
        n_full = (i * tq) // tk
        carry = lax.fori_loop(0, n_full, functools.partial(step, size=tk, masked=False), carry)
        (m0, l0, acc0), (m1, l1, acc1) = step(n_full, carry, size=tk, masked=True)
        o_ref[...] = acc0 / l0 + acc1 / l1
        lse_ref[...] = jnp.where(lane >= HEAD, m1 + jnp.log2(l1), m0 + jnp.log2(l0))

    return pl.pallas_call(
        body, name="attention_forward", grid=(HEADS // 2, n_q),
        in_specs=[pl.BlockSpec((tq, 2 * LANE), lambda p, i: (i, p)),
                  pl.BlockSpec((n_rows, 2 * LANE), lambda p, i: (0, p)),
                  pl.BlockSpec((n_rows, 2 * LANE), lambda p, i: (0, p))],
        out_specs=[pl.BlockSpec((tq, LANE), lambda p, i: (i, p))] * 2,
        out_shape=[jax.ShapeDtypeStruct((n_rows, WIDTH), F32)] * 2,
        compiler_params=pltpu.CompilerParams(dimension_semantics=("arbitrary", "arbitrary"),
                                             vmem_limit_bytes=VMEM_LIMIT),
    )(q, k, v)


def attention_backward(q, k, v, o, do, lse):
    n_rows = q.shape[0]
    tq, tk = ATTN_BWD_TILES
    n_q = n_rows // tq
    n_masked = max(1, tk // tq)

    def body(q_ref, k_ref, v_ref, o_ref, do_ref, lse_ref, dq_ref, dk_ref, dv_ref):
        j = pl.program_id(1)

        @pl.when(j == 0)
        def _():
            dq_ref[...] = jnp.zeros_like(dq_ref)

        lane = lane_iota((tq, LANE))
        heads = [slice(0, LANE), slice(LANE, 2 * LANE)]
        ks = [k_ref[:, cols] for cols in heads]
        vs = [v_ref[:, cols] for cols in heads]
        head_lanes = [(lane < HEAD).astype(F32), (lane >= HEAD).astype(F32)]

        def step(i, carry, masked):
            rows = pl.ds(pl.multiple_of(i * tq, tq), tq)
            qs = [q_ref[rows, cols] for cols in heads]
            dout = do_ref[rows, :]
            dout_o = dout * o_ref[rows, :]
            lse_t = lse_ref[rows, :]
            ss = [mm_nt(qh, kh) for qh, kh in zip(qs, ks)]
            dps = [mm_nt(dout, vh) for vh in vs]
            ps, dss = [], []
            for hh in range(2):
                delta = jnp.sum(dout_o * head_lanes[hh], axis=-1, keepdims=True)
                lse_h = jnp.sum(jnp.where(lane == hh * HEAD, lse_t, 0.0), axis=-1, keepdims=True)
                p = jnp.exp2(ss[hh] - lse_h)
                if masked:
                    p = jnp.where(visible(i * tq, j * tk, p.shape), p, 0.0)
                ps.append(p)
                dss.append(p * (dps[hh] - delta))
            dvs = [mm_tn(p, dout) for p in ps]
            dqs = [mm(ds, kh) for ds, kh in zip(dss, ks)]
            dks = [mm_tn(ds, qh) for ds, qh in zip(dss, qs)]
            for cols, dq in zip(heads, dqs):
                dq_ref[rows, cols] += dq * ATTN_SCALE
            return tuple((dk + a, dv + b) for (dk, dv), a, b in zip(carry, dks, dvs))

        carry = tuple((jnp.zeros((tk, LANE), F32), jnp.zeros((tk, LANE), F32)) for _ in heads)
        i_first = (j * tk) // tq
        for extra in range(n_masked):
            carry = step(i_first + extra, carry, masked=True)
        carry = lax.fori_loop(i_first + n_masked, n_q, functools.partial(step, masked=False), carry)
        for cols, (dk, dv) in zip(heads, carry):
            dk_ref[:, cols] = dk * (1.0 / LOG2_E)
            dv_ref[:, cols] = dv

    full = lambda w: pl.BlockSpec((n_rows, w), lambda p, j: (0, p))
    blk = pl.BlockSpec((tk, 2 * LANE), lambda p, j: (j, p))
    return pl.pallas_call(
        body, name="attention_backward", grid=(HEADS // 2, n_rows // tk),
        in_specs=[full(2 * LANE), blk, blk, full(LANE), full(LANE), full(LANE)],
        out_specs=[full(2 * LANE), blk, blk],
        out_shape=[jax.ShapeDtypeStruct((n_rows, HEADS * LANE), F32)] * 3,
        compiler_params=pltpu.CompilerParams(dimension_semantics=("arbitrary", "arbitrary"),
                                             vmem_limit_bytes=VMEM_LIMIT),
    )(q, k, v, o, do, lse)


def tail_tile(step0, tile0, x, tgt, ma, mb, gpa, gpb, ya, y, ur, k2, uv,
              mod, wpa, wpb, wout, gn_g, gn_b, r_k, post_g, post_b, bd):
    gate = mod[2:3]
    inv = 1.0 / HEAD
    yc = y - head_sum(y, bd) * inv
    rs = lax.rsqrt(head_sum(yc * yc, bd) * inv + GN_EPS)
    yn = yc * rs
    yb = yn * gn_g + gn_b + head_sum(ur * k2 * r_k, bd) * uv
    sga, sgb = sigmoid(gpa), sigmoid(gpb)
    sila, silb = gpa * sga, gpb * sgb
    ga, gb = ya * sila, yb * silb
    pa, pb = mm(ga, wpa), mm(gb, wpb)
    sa, sb = sigmoid(ma), sigmoid(mb)
    merged = sa * pa + sb * pb
    sub = mm(merged, wout)
    z = ALPHA * x + (1.0 + gate) * sub
    zhat, rstd = layer_norm_stats(z)
    err = zhat * post_g + post_b - tgt
    loss = 0.5 * jnp.sum(rowmean(err * err), axis=0, keepdims=True) + jnp.zeros((1, LANE), F32)
    dout = err * (1.0 / D_MODEL)
    dpost_g = colsum(dout * zhat)
    dpost_b = colsum(dout)
    dz = layer_norm_bwd(dout * post_g, zhat, rstd)
    dgate = colsum(dz * sub)
    dsub = dz * (1.0 + gate)
    dwout = mm_tn(merged, dsub)
    dmerged = mm_nt(dsub, wout)
    dpa, dpb = dmerged * sa, dmerged * sb
    dma = dmerged * pa * sa * (1.0 - sa)
    dmb = dmerged * pb * sb * (1.0 - sb)
    dwpa = mm_tn(ga, dpa)
    dwpb = mm_tn(gb, dpb)
    dga = mm_nt(dpa, wpa)
    dgb = mm_nt(dpb, wpb)
    dya = dga * sila
    dgpa = dga * ya * (sga * (1.0 + gpa * (1.0 - sga)))
    dyb = dgb * silb
    dgpb = dgb * yb * (sgb * (1.0 + gpb * (1.0 - sgb)))
    dgn_g = colsum(dyb * yn)
    dgn_b = colsum(dyb)
    dyn = dyb * gn_g
    dy = rs * (dyn - head_sum(dyn, bd) * inv - yn * head_sum(dyn * yn, bd) * inv)
    return (dz, dma, dmb, dgpa, dgpb, dya, dy, dyb,
            loss, dwout, dwpa, dwpb, dgn_g, dgn_b, dpost_g, dpost_b, dgate)


def mla_prep_bwd_tile(step0, tile0, q_c, kv_c, cos, sin, dq, dk, dv, gq, gkv, wq, wqr, wkn, wv):
    qn, qh, rq = rms_norm_fwd(q_c, gq)
    kvn, kvh, rkv = rms_norm_fwd(kv_c, gkv)
    dqc = dq * tile_lanes(cos, HEADS)
    dqs = dq * tile_lanes(sin, HEADS)
    dqn = mm_nt(dqc, wq) + mm_nt(dqs, wqr)
    dkvn = mm_nt(dk, wkn) + mm_nt(dv, wv)
    dkpe = dk[:, 0:LANE]
    for h in range(1, HEADS):
        dkpe = dkpe + dk[:, h * LANE:(h + 1) * LANE]
    dkr = dkpe * (cos * key_rope_mask(cos.shape))
    dkrr = dkpe * sin

    def rms_bwd(dyv, xh, r, g):
        dyg = dyv * g
        return r * (dyg - xh * rowmean(dyg * xh)), colsum(dyv * xh)

    dq_c, dgq = rms_bwd(dqn, qh, rq, gq)
    dkv_c, dgkv = rms_bwd(dkvn, kvh, rkv, gkv)
    return (dq_c, dkv_c, dkr, dkrr,
            mm_tn(qn, dqc), mm_tn(qn, dqs), mm_tn(kvn, dk), mm_tn(kvn, dv), dgq, dgkv)


def rwkv_prep_bwd_tile(step0, tile0, r0, k0, v0, l0, drt, dat, dbt, dkt, dvv, dlw, dyb, hr, hk, hv, hl,
                       mu_r, mu_k, mu_v, mu_l, w0, a0, k_k, k_a, w_dec, w_iclr, tril, same, bd, r_k,
                       cr, ck, cv, cl_):
    f = rwkv_prep_core(tile0, r0, k0, v0, l0, hr, hk, hv, hl, mu_r, mu_k, mu_v, mu_l, w0, a0, k_k, k_a,
                       w_dec, w_iclr, tril, same, bd)
    ur, uk, uv, ul, kk, k2, a_ic, sg, th = (f[n] for n in ("ur", "uk", "uv", "ul", "kk", "k2", "a_ic", "sg", "th"))
    lc, lw = f["lc"], f["lw"]
    e_neg = jnp.exp(-lc)
    dur = drt * jnp.exp(lc)
    da = dat * jnp.exp(lc - lw)
    db = dbt * e_neg
    dk2 = dkt * e_neg
    s = head_sum(ur * k2 * r_k, bd)
    duv = dvv + dyb * s
    ds = head_sum(dyb * uv, bd)
    dur = dur + ds * k2 * r_k
    dk2 = dk2 + ds * ur * r_k
    dr_k = colsum(ds * ur * k2)
    dkk = db * a_ic - da
    da_ic = db * kk + dk2 * uk * k_a
    duk = dk2 * (1.0 + (a_ic - 1.0) * k_a)
    dk_a = colsum(dk2 * uk * (a_ic - 1.0))
    dkkraw = jnp.where(f["nrm_raw"] > 1e-12, (dkk - kk * head_sum(dkk * kk, bd)) / f["nrm"], dkk * 1e12)
    duk = duk + dkkraw * k_k
    dk_k = colsum(dkkraw * uk)
    dai = da_ic * a_ic * (1.0 - a_ic)
    dd = dlw * (-DECAY_SCALE) * sg * (1.0 - sg)
    dul = mm_nt(dai, w_iclr) + mm_nt(dd, w_dec) * (1.0 - th * th)

    def unshift(du, x, prev, mu, carry_row):
        nxt = shift_rows_up(du, carry_row)
        return du * (1.0 - mu) + nxt * mu, colsum(du * (prev - x)), du[0:1, :]

    dr0, dmu_r, ncr = unshift(dur, r0, f["pr"], mu_r, cr)
    dk0, dmu_k, nck = unshift(duk, k0, f["pk"], mu_k, ck)
    dv0, dmu_v, ncv = unshift(duv, v0, f["pv"], mu_v, cv)
    dl0, dmu_l, ncl = unshift(dul, l0, f["pl"], mu_l, cl_)
    return (dr0, dk0, dv0, dl0,
            dmu_r, dmu_k, dmu_v, dmu_l, colsum(dd), colsum(dai), dk_k, dk_a, dr_k, mm_tn(th, dd), mm_tn(ul, dai),
            ncr, nck, ncv, ncl)


def in_backward(x, dz, pieces, mod, w_in_pt, unrot):
    n_rows = x.shape[0]
    ts = ROW_TILE
    n_p = len(pieces)
    shard_cols = IN_WIDTH // N_DEV

    def body(*refs):
        x_ref, dz_ref = refs[:2]
        p_refs = refs[2:2 + n_p]
        mod_ref, w_ref, unrot_ref = refs[2 + n_p:5 + n_p]
        dx_ref, ht_ref, blocks_ref, dshift_ref, dscale_ref = refs[5 + n_p:]
        step0 = pl.program_id(0) == 0
        dma, dmb, dr0, dk0, dv0, dgpa, dgpb, dq_c, dkv_c, dkr, dkrr, dl0 = (r[...] for r in p_refs)
        dproj = jnp.concatenate([dma, dmb, dr0, dk0, dv0, dgpa, dgpb, dq_c, dkv_c, dkr, dkrr, dl0], axis=1)
        dh = mm(dproj, w_ref[...])
        xhat, rstd = layer_norm_stats(x_ref[...])
        scale1 = 1.0 + mod_ref[1:2, :]
        dx_ref[...] = layer_norm_bwd(dh * scale1, xhat, rstd) + ALPHA * dz_ref[...]
        ht_ref[...] = jnp.transpose(xhat * scale1 + mod_ref[0:1, :]).astype(BF16)
        dkrope = (dkr.astype(F32) + mm(dkrr, unrot_ref[...]))[:, NOPE:QK_DIM]
        natural = jnp.concatenate(
            [dq_c.astype(F32), dkv_c.astype(F32), dkrope]
            + [p.astype(F32) for p in (dgpa, dr0, dk0, dv0, dl0, dgpb, dma, dmb)], axis=1)
        for j in range(N_DEV):
            blocks_ref[j] = natural[:, j * shard_cols:(j + 1) * shard_cols].astype(BF16)
        for ref, val in ((dshift_ref, colsum(dh)), (dscale_ref, colsum(dh * xhat))):
            @pl.when(step0)
            def _(ref=ref, val=val):
                ref[...] = val

            @pl.when(jnp.logical_not(step0))
            def _(ref=ref, val=val):
                ref[...] += val

    row = lambda w: pl.BlockSpec((ts, w), lambda i: (i, 0))
    const = pl.BlockSpec(memory_space=pltpu.VMEM)
    vec = pl.BlockSpec((1, D_MODEL), lambda i: (0, 0))
    return pl.pallas_call(
        body, name="in_backward", grid=(n_rows // ts,),
        in_specs=[row(D_MODEL), row(D_MODEL)] + [row(p.shape[1]) for p in pieces] + [const] * 3,
        out_specs=[row(D_MODEL), pl.BlockSpec((D_MODEL, ts), lambda i: (0, i)),
                   pl.BlockSpec((N_DEV, ts, shard_cols), lambda i: (0, i, 0)), vec, vec],
        out_shape=[jax.ShapeDtypeStruct((n_rows, D_MODEL), F32), jax.ShapeDtypeStruct((D_MODEL, n_rows), BF16),
                   jax.ShapeDtypeStruct((N_DEV, n_rows, shard_cols), BF16),
                   jax.ShapeDtypeStruct((1, D_MODEL), F32), jax.ShapeDtypeStruct((1, D_MODEL), F32)],
        compiler_params=pltpu.CompilerParams(dimension_semantics=("arbitrary",), vmem_limit_bytes=VMEM_LIMIT),
    )(x, dz, *pieces, mod, w_in_pt, unrot)


def in_weight_grad_exchange(h_t, dp_blocks, others, small, order):
    n = len(others)
    n_rows = h_t.shape[1]
    ts = 4 * ROW_TILE
    n_i = n_rows // ts
    shard_cols = dp_blocks.shape[2]
    n_chips = N_DEV // 2
    last = N_DEV - 1

    def body(order_ref, h_ref, dp_ref, *rest):
        g_refs, s_ref = rest[:n], rest[n]
        rwin_ref, rg_refs, rs_ref = rest[n + 1], rest[n + 2:2 * n + 2], rest[2 * n + 2]
        (acc, sendbuf, sib_buf, sib_send, sib_recv, win_send, win_recv,
         o_send, o_recv, local_sems) = rest[2 * n + 3:]
        b, i = pl.program_id(0), pl.program_id(1)
        me = my_position()
        mi = flat_index(me)
        sibling = (me[0], me[1], 1 - me[2])

        def other_copies(k, src_index, dst_index):
            peer = flip(me, k)
            out = [pltpu.make_async_remote_copy(
                src_ref=g_refs[a].at[src_index], dst_ref=rg_refs[a].at[dst_index],
                send_sem=o_send.at[(n + 1) * (k - 1) + a], recv_sem=o_recv.at[(n + 1) * (k - 1) + a],
                device_id=peer, device_id_type=MESH_IDS) for a in range(n)]
            out.append(pltpu.make_async_remote_copy(
                src_ref=s_ref, dst_ref=rs_ref.at[dst_index],
                send_sem=o_send.at[(n + 1) * (k - 1) + n], recv_sem=o_recv.at[(n + 1) * (k - 1) + n],
                device_id=peer, device_id_type=MESH_IDS))
            return out

        def local_copies():
            out = [pltpu.make_async_copy(g_refs[a].at[mi], rg_refs[a].at[mi], local_sems.at[a]) for a in range(n)]
            out.append(pltpu.make_async_copy(s_ref, rs_ref.at[mi], local_sems.at[n]))
            return out

        def to_sibling(t):
            return pltpu.make_async_remote_copy(
                src_ref=sendbuf.at[t], dst_ref=sib_buf.at[t], send_sem=sib_send.at[t], recv_sem=sib_recv.at[t],
                device_id=sibling, device_id_type=MESH_IDS)

        def to_owner(t):
            flip_x = (t < 2) * 1
            flip_y = 1 - (t & 1)
            owner = (me[0] ^ flip_x, me[1] ^ flip_y, me[2])
            return pltpu.make_async_remote_copy(
                src_ref=sendbuf.at[n_chips + t], dst_ref=rwin_ref.at[t], send_sem=win_send.at[t],
                recv_sem=win_recv.at[t], device_id=owner, device_id_type=MESH_IDS)

        own_block = pltpu.make_async_copy(sendbuf.at[last], rwin_ref.at[n_chips - 1], local_sems.at[n + 1])

        @pl.when(jnp.logical_and(b == 0, i == 0))
        def _():
            for cp in local_copies():
                cp.start()
            for k in range(1, N_DEV):
                for cp in other_copies(k, flat_index(flip(me, k)), mi):
                    cp.start()

        contrib = jnp.dot(h_ref[...], dp_ref[...], preferred_element_type=F32)

        @pl.when(i == 0)
        def _():
            acc[...] = contrib

        @pl.when(i > 0)
        def _():
            acc[...] += contrib

        slot = order_ref[N_DEV + b]
        t = slot & (n_chips - 1)

        @pl.when(jnp.logical_and(i == n_i - 1, slot < n_chips))
        def _():
            sendbuf[slot] = acc[...].astype(BF16)
            to_sibling(t).start()

        @pl.when(jnp.logical_and(i == n_i - 1, slot >= n_chips))
        def _():
            to_sibling(t).wait_recv()
            sendbuf[slot] = (acc[...] + sib_buf[t].astype(F32)).astype(BF16)

            @pl.when(slot < last)
            def _():
                to_owner(t).start()

            @pl.when(slot == last)
            def _():
                own_block.start()

        @pl.when(jnp.logical_and(b == last, i == n_i - 1))
        def _():
            for t in range(n_chips - 1):
                to_owner(t).wait_recv()
            for k in range(1, N_DEV):
                pi = flat_index(flip(me, k))
                for cp in other_copies(k, pi, pi):
                    cp.wait_recv()
            for t in range(n_chips):
                to_sibling(t).wait_send()
            for t in range(n_chips - 1):
                to_owner(t).wait_send()
            for k in range(1, N_DEV):
                for cp in other_copies(k, flat_index(flip(me, k)), mi):
                    cp.wait_send()
            for cp in local_copies():
                cp.wait()
            own_block.wait()

    hbm = pl.BlockSpec(memory_space=pl.ANY)
    n_sem = 7 * (n + 1)
    grid_spec = pltpu.PrefetchScalarGridSpec(
        num_scalar_prefetch=1, grid=(N_DEV, n_i),
        in_specs=[pl.BlockSpec((D_MODEL, ts), lambda b, i, order: (0, i)),
                  pl.BlockSpec((None, ts, shard_cols), lambda b, i, order: (order[b], i, 0))] + [hbm] * (n + 1),
        out_specs=[hbm] * (n + 2),
        scratch_shapes=[pltpu.VMEM((D_MODEL, shard_cols), F32), pltpu.VMEM((N_DEV, D_MODEL, shard_cols), BF16),
                        pltpu.VMEM((n_chips, D_MODEL, shard_cols), BF16),
                        pltpu.SemaphoreType.DMA((n_chips,)), pltpu.SemaphoreType.DMA((n_chips,)),
                        pltpu.SemaphoreType.DMA((n_chips - 1,)), pltpu.SemaphoreType.DMA((n_chips - 1,)),
                        pltpu.SemaphoreType.DMA((n_sem,)), pltpu.SemaphoreType.DMA((n_sem,)),
                        pltpu.SemaphoreType.DMA((n + 2,))])
    return pl.pallas_call(
        body, name="in_weight_grad_exchange", grid_spec=grid_spec,
        out_shape=[jax.ShapeDtypeStruct((n_chips, D_MODEL, shard_cols), BF16)]
        + [jax.ShapeDtypeStruct(o.shape, o.dtype) for o in others]
        + [jax.ShapeDtypeStruct((N_DEV,) + small.shape, small.dtype)],
        compiler_params=pltpu.CompilerParams(dimension_semantics=("arbitrary", "arbitrary"),
                                             vmem_limit_bytes=VMEM_LIMIT),
    )(order, h_t, dp_blocks, *others, small)


def ada_weight_grad(c_all, dmod_cols):
    def body(c_ref, d_ref, o_ref):
        cv = c_ref[...]
        o_ref[...] = hdot_tn(cv * sigmoid(cv), d_ref[...])

    return pl.pallas_call(
        body, name="ada_weight_grad",
        out_shape=jax.ShapeDtypeStruct((c_all.shape[1], dmod_cols.shape[1]), F32),
    )(c_all, dmod_cols)


def adamw_update(g, w, m, v):
    nm = ADAM_B1 * m + (1.0 - ADAM_B1) * g
    nv = ADAM_B2 * v + (1.0 - ADAM_B2) * (g * g)
    m_hat = nm / (1.0 - ADAM_B1 ** ADAM_STEP)
    v_hat = nv / (1.0 - ADAM_B2 ** ADAM_STEP)
    return -ADAM_LR * (m_hat / (jnp.sqrt(v_hat) + ADAM_EPS) + ADAM_WD * w), nm, nv


def adamw(parts, w, m, v, name):
    k, rows, cols = parts.shape
    rb = 128 if rows % 128 == 0 else rows

    def body(p_ref, w_ref, m_ref, v_ref, g_ref, d_ref, nm_ref, nv_ref):
        g = p_ref[0].astype(F32)
        for i in range(1, k):
            g = g + p_ref[i].astype(F32)
        g_ref[0] = g
        d_ref[0], nm_ref[0], nv_ref[0] = adamw_update(g, w_ref[0], m_ref[0], v_ref[0])

    blk = pl.BlockSpec((1, rb, cols), lambda i: (0, i, 0))
    return pl.pallas_call(
        body, name=name, grid=(rows // rb,),
        in_specs=[pl.BlockSpec((k, rb, cols), lambda i: (0, i, 0)), blk, blk, blk],
        out_specs=[blk] * 4, out_shape=[jax.ShapeDtypeStruct((1, rows, cols), F32)] * 4,
        compiler_params=pltpu.CompilerParams(dimension_semantics=("arbitrary",), vmem_limit_bytes=VMEM_LIMIT),
    )(parts, w, m, v)


def adamw_small(parts, ws, ms, vs):
    k = parts.shape[0]
    n = len(ws)
    sizes = [w.shape[1] for w in ws]

    def body(p_ref, *refs):
        ins, outs = refs[:3 * n], refs[3 * n:]
        g_all = p_ref[0]
        for i in range(1, k):
            g_all = g_all + p_ref[i]
        off = 0
        for a, size in enumerate(sizes):
            g = g_all[:, off:off + size]
            off += size
            d, nm, nv = adamw_update(g, ins[a][...], ins[n + a][...], ins[2 * n + a][...])
            for kind, val in enumerate((g, d, nm, nv)):
                outs[kind * n + a][...] = val

    return pl.pallas_call(
        body, name="adamw_small",
        out_shape=[jax.ShapeDtypeStruct((1, size), F32) for _ in range(4) for size in sizes],
    )(parts, *ws, *ms, *vs)


def columns_from_shards(g, rows, cols):
    return g.reshape(N_DEV, rows, cols).transpose(1, 0, 2).reshape(rows, N_DEV * cols)


def permute_w_in_t(wt):
    z = lambda n: jnp.zeros((n, D_MODEL), wt.dtype)
    krope = wt[N_KROPE:N_KROPE + ROPE]
    krope_rot = jnp.concatenate([-krope[ROPE // 2:], krope[:ROPE // 2]], axis=0)
    rw = N_RWKV
    return jnp.concatenate([
        wt[N_MA:N_MA + 1024], wt[N_MB:N_MB + 1024],
        wt[rw:rw + 512], wt[rw + 512:rw + 1024], wt[rw + 1024:rw + 1536],
        wt[N_GPA:N_GPA + 512], wt[N_GPB:N_GPB + 512],
        wt[N_QC:N_QC + 256], wt[N_KVC:N_KVC + 128],
        z(NOPE), krope, z(LANE - QK_DIM), z(NOPE), krope_rot, z(LANE - QK_DIM),
        wt[rw + 1536:rw + 1664]], axis=0)


def pad_heads_q(w_uq):
    w = w_uq.reshape(Q_RANK, HEADS, QK_DIM)
    zpad = jnp.zeros((Q_RANK, HEADS, LANE - QK_DIM), w.dtype)
    wq = jnp.concatenate([w, zpad], axis=2).reshape(Q_RANK, HEADS * LANE)
    pe = w[:, :, NOPE:]
    rot = jnp.concatenate([-pe[:, :, ROPE // 2:], pe[:, :, :ROPE // 2]], axis=2)
    wqr = jnp.concatenate([jnp.zeros((Q_RANK, HEADS, NOPE), w.dtype), rot, zpad], axis=2).reshape(Q_RANK, HEADS * LANE)
    return wq, wqr


def unpad_heads_q_grad(dwq, dwqr):
    a = dwq.reshape(Q_RANK, HEADS, LANE)
    r = dwqr.reshape(Q_RANK, HEADS, LANE)[:, :, NOPE:QK_DIM]
    pe = a[:, :, NOPE:QK_DIM] + jnp.concatenate([r[:, :, ROPE // 2:], -r[:, :, :ROPE // 2]], axis=2)
    return jnp.concatenate([a[:, :, :NOPE], pe], axis=2).reshape(Q_RANK, HEADS * QK_DIM)


def pad_heads_kv(w_ukv):
    w = w_ukv.reshape(KV_RANK, HEADS, 2 * HEAD)
    z = jnp.zeros((KV_RANK, HEADS, HEAD), w.dtype)
    wkn = jnp.concatenate([w[:, :, :NOPE], z], axis=2).reshape(KV_RANK, HEADS * LANE)
    val = w[:, :, NOPE:]
    odd = (jnp.arange(HEADS) % 2 == 1)[None, :, None]
    wv = jnp.concatenate([jnp.where(odd, 0, val), jnp.where(odd, val, 0)], axis=2).reshape(KV_RANK, HEADS * LANE)
    return wkn, wv


def unpad_heads_kv_grad(dwkn, dwv):
    a = dwkn.reshape(KV_RANK, HEADS, LANE)[:, :, :NOPE]
    b = dwv.reshape(KV_RANK, HEADS, LANE)
    odd = (jnp.arange(HEADS) % 2 == 1)[None, :, None]
    val = jnp.where(odd, b[:, :, HEAD:], b[:, :, :HEAD])
    return jnp.concatenate([a, val], axis=2).reshape(KV_RANK, HEADS * 2 * HEAD)


def kernel(x, c, positions, w_ada, b_ada, w_in, q_norm_g, w_uq, kv_norm_g, w_ukv, mu_rwkv, w0, w_decay_up, a0, w_iclr_up, k_k, k_a, r_k, gn_g, gn_b, w_proj_a, w_proj_b, w_out, post_g, post_b, loss_target, m_w_ada, m_b_ada, m_w_in, m_q_norm_g, m_w_uq, m_kv_norm_g, m_w_ukv, m_mu_rwkv, m_w0, m_w_decay_up, m_a0, m_w_iclr_up, m_k_k, m_k_a, m_r_k, m_gn_g, m_gn_b, m_w_proj_a, m_w_proj_b, m_w_out, m_post_g, m_post_b, v_w_ada, v_b_ada, v_w_in, v_q_norm_g, v_w_uq, v_kv_norm_g, v_w_ukv, v_mu_rwkv, v_w0, v_w_decay_up, v_a0, v_w_iclr_up, v_k_k, v_k_a, v_r_k, v_gn_g, v_gn_b, v_w_proj_a, v_w_proj_b, v_w_out, v_post_g, v_post_b):
    weights = dict(w_ada=w_ada, b_ada=b_ada, w_in=w_in, q_norm_g=q_norm_g, w_uq=w_uq, kv_norm_g=kv_norm_g,
                   w_ukv=w_ukv, mu_rwkv=mu_rwkv, w0=w0, w_decay_up=w_decay_up, a0=a0, w_iclr_up=w_iclr_up,
                   k_k=k_k, k_a=k_a, r_k=r_k, gn_g=gn_g, gn_b=gn_b, w_proj_a=w_proj_a, w_proj_b=w_proj_b,
                   w_out=w_out, post_g=post_g, post_b=post_b)
    mom1 = dict(w_ada=m_w_ada, b_ada=m_b_ada, w_in=m_w_in, q_norm_g=m_q_norm_g, w_uq=m_w_uq, kv_norm_g=m_kv_norm_g,
                w_ukv=m_w_ukv, mu_rwkv=m_mu_rwkv, w0=m_w0, w_decay_up=m_w_decay_up, a0=m_a0, w_iclr_up=m_w_iclr_up,
                k_k=m_k_k, k_a=m_k_a, r_k=m_r_k, gn_g=m_gn_g, gn_b=m_gn_b, w_proj_a=m_w_proj_a, w_proj_b=m_w_proj_b,
                w_out=m_w_out, post_g=m_post_g, post_b=m_post_b)
    mom2 = dict(w_ada=v_w_ada, b_ada=v_b_ada, w_in=v_w_in, q_norm_g=v_q_norm_g, w_uq=v_w_uq, kv_norm_g=v_kv_norm_g,
                w_ukv=v_w_ukv, mu_rwkv=v_mu_rwkv, w0=v_w0, w_decay_up=v_w_decay_up, a0=v_a0, w_iclr_up=v_w_iclr_up,
                k_k=v_k_k, k_a=v_k_a, r_k=v_r_k, gn_g=v_gn_g, gn_b=v_gn_b, w_proj_a=v_w_proj_a, w_proj_b=v_w_proj_b,
                w_out=v_w_out, post_g=v_post_g, post_b=v_post_b)
    names = list(weights)
    n_rows = x.shape[1]
    me = 4 * lax.axis_index("x") + 2 * lax.axis_index("y") + lax.axis_index("c")
    xr = x[0]
    tgt = loss_target[0]
    row = lambda a: a.reshape(1, -1)

    w_in_all, c_all = gather_shards([w_in[0].T.astype(BF16), c])
    c_all = c_all.reshape(N_DEV, D_MODEL)
    w_in_pt = permute_w_in_t(w_in_all.reshape(IN_WIDTH, D_MODEL))

    mod_all = ada_modulation(c_all, w_ada[0], b_ada.reshape(N_DEV, -1))
    mod = lax.dynamic_index_in_dim(mod_all, me, axis=1, keepdims=False).reshape(3, D_MODEL)

    proj, *gathered = fwd_in_gather(xr, mod, w_in_pt, [weights[n][0].astype(BF16) for n, _, _ in SHARDED[1:]])
    pcol = lambda off_, w: (proj, w, off_ // w)
    full = {}
    for (n, r, cdim), part in zip(SHARDED[1:], gathered):
        full[n] = part.reshape(N_DEV * r, cdim) if n == "w_out" else columns_from_shards(part, r, cdim)
    wq, wqr = pad_heads_q(full["w_uq"])
    wkn, wv = pad_heads_kv(full["w_ukv"])
    zl = jnp.zeros((LORA, WIDTH), BF16)
    w_dec = jnp.concatenate([full["w_decay_up"], zl], axis=0)
    w_iclr = jnp.concatenate([zl, full["w_iclr_up"]], axis=0)
    wpa, wpb, wout = full["w_proj_a"], full["w_proj_b"], full["w_out"]

    inv_freq = ROPE_THETA ** (-jnp.arange(0, ROPE, 2, dtype=F32) / ROPE)
    ang = positions[0].astype(F32)[:, None] * inv_freq
    ones_n, zeros_n, zeros_p = jnp.ones((n_rows, NOPE), F32), jnp.zeros((n_rows, NOPE), F32), jnp.zeros((n_rows, LANE - QK_DIM), F32)
    cos_t = jnp.concatenate([ones_n, jnp.cos(ang), jnp.cos(ang), zeros_p], axis=1)
    sin_t = jnp.concatenate([zeros_n, jnp.sin(ang), jnp.sin(ang), zeros_p], axis=1)

    gq, gkv = q_norm_g, kv_norm_g
    mla_consts = [gq, gkv, wq, wqr, wkn, wv]
    q, k, v = row_call(
        "mla_prep", mla_prep_tile, n_rows,
        [pcol(P_QC, 256), pcol(P_KVC, 128), pcol(P_KR, 128), pcol(P_KRR, 128), (cos_t, LANE, 0), (sin_t, LANE, 0)],
        mla_consts, [(HEADS * LANE, BF16)] * 3, tile_rows=PREP_TILE)
    ya, lse = attention_forward(q, k, v)

    def chunk_sum_matrices(n):
        t_idx = jnp.arange(n)
        same_chunk = (t_idx[:, None] // CHUNK) == (t_idx[None, :] // CHUNK)
        return (same_chunk & (t_idx[:, None] >= t_idx[None, :])).astype(F32), same_chunk.astype(F32)

    l_idx = jnp.arange(LANE)
    bd = ((l_idx[:, None] // HEAD) == (l_idx[None, :] // HEAD)).astype(F32)
    mu = mu_rwkv
    mu_r, mu_k, mu_v, mu_l = mu[:, 0:512], mu[:, 512:1024], mu[:, 1024:1536], mu[:, 1536:1664]
    rk_row = row(r_k)
    rwkv_consts = lambda n: [mu_r, mu_k, mu_v, mu_l, w0, a0, k_k, k_a, w_dec, w_iclr, *chunk_sum_matrices(n), bd]
    rwkv_rows = [pcol(P_R, 512), pcol(P_K, 512), pcol(P_V, 512), pcol(P_LORA, 128)]
    rt, at, bt, kt, clf, uv, ur, k2 = row_call(
        "rwkv_prep", rwkv_prep_tile, n_rows, rwkv_rows, rwkv_consts(ROW_TILE), [(WIDTH, F32)] * 8, halo_in=rwkv_rows)
    y, m0s, state_maps, out_maps, *wkv_saved = wkv_forward(at, bt, kt, rt, uv, clf)

    tail = row_call(
        "tail", tail_tile, n_rows,
        [(xr, D_MODEL, 0), (tgt, D_MODEL, 0), pcol(P_MA, 1024), pcol(P_MB, 1024), pcol(P_GPA, 512), pcol(P_GPB, 512),
         (ya, WIDTH, 0), (y, WIDTH, 0), (ur, WIDTH, 0), (k2, WIDTH, 0), (uv, WIDTH, 0)],
        [mod, wpa, wpb, wout, gn_g, gn_b, rk_row, post_g, post_b, bd],
        [(D_MODEL, F32), (1024, BF16), (1024, BF16), (512, BF16), (512, BF16), (WIDTH, F32), (WIDTH, F32), (WIDTH, F32)],
        acc_out=[((1, LANE), F32), ((D_MODEL, D_MODEL), F32), ((WIDTH, D_MODEL), F32), ((WIDTH, D_MODEL), F32),
                 ((1, WIDTH), F32), ((1, WIDTH), F32), ((1, D_MODEL), F32), ((1, D_MODEL), F32), ((1, D_MODEL), F32)])
    (dz, dma, dmb, dgpa, dgpb, dya, dy, dyb,
     loss_row, g_wout, g_wpa, g_wpb, g_gn_g, g_gn_b, g_post_g, g_post_b, dgate) = tail

    dq, dk, dv = attention_backward(q, k, v, ya, dya, lse)
    dq_c, dkv_c, dkr, dkrr, g_wq, g_wqr, g_wkn, g_wv, g_gq, g_gkv = row_call(
        "mla_prep_bwd", mla_prep_bwd_tile, n_rows,
        [pcol(P_QC, 256), pcol(P_KVC, 128), (cos_t, LANE, 0), (sin_t, LANE, 0),
         (dq, HEADS * LANE, 0), (dk, HEADS * LANE, 0), (dv, HEADS * LANE, 0)],
        mla_consts, [(256, BF16), (128, BF16), (128, BF16), (128, BF16)],
        acc_out=[((Q_RANK, HEADS * LANE), F32)] * 2 + [((KV_RANK, HEADS * LANE), F32)] * 2
        + [((1, Q_RANK), F32), ((1, KV_RANK), F32)], tile_rows=PREP_TILE)

    dat, dbt, dkt, drt, dvv, dlw = wkv_backward(at, bt, kt, rt, uv, clf, m0s, state_maps, out_maps, wkv_saved, dy)
    (dr0, dk0, dv0, dl0, g_mu_r, g_mu_k, g_mu_v, g_mu_l, g_w0, g_a0, g_k_k, g_k_a, g_r_k, g_wdec, g_wiclr) = row_call(
        "rwkv_prep_bwd", rwkv_prep_bwd_tile, n_rows,
        rwkv_rows + [(drt, WIDTH, 0), (dat, WIDTH, 0), (dbt, WIDTH, 0), (dkt, WIDTH, 0), (dvv, WIDTH, 0),
                     (dlw, WIDTH, 0), (dyb, WIDTH, 0)],
        rwkv_consts(PREP_TILE) + [rk_row], [(512, BF16), (512, BF16), (512, BF16), (128, BF16)],
        acc_out=[((1, 512), F32)] * 3 + [((1, 128), F32)] + [((1, 512), F32)] * 5 + [((LANE, WIDTH), F32)] * 2,
        halo_in=rwkv_rows, carry=[512, 512, 512, 128], reverse=True, tile_rows=PREP_TILE)

    li = jnp.arange(LANE)
    src, dst = li[:, None], li[None, :]
    half = ROPE // 2
    unrot = (jnp.where((dst >= NOPE) & (dst < NOPE + half) & (src == dst + half), 1.0, 0.0)
             - jnp.where((dst >= NOPE + half) & (dst < QK_DIM) & (src == dst - half), 1.0, 0.0)).astype(BF16)
    dx, h_t, dproj_blocks, dshift, dscale = in_backward(
        xr, dz, [dma, dmb, dr0, dk0, dv0, dgpa, dgpb, dq_c, dkv_c, dkr, dkrr, dl0], mod, w_in_pt, unrot)

    grads_full = {
        "w_uq": unpad_heads_q_grad(g_wq, g_wqr), "w_ukv": unpad_heads_kv_grad(g_wkn, g_wv),
        "w_decay_up": g_wdec[:LORA], "w_iclr_up": g_wiclr[LORA:],
        "w_proj_a": g_wpa, "w_proj_b": g_wpb, "w_out": g_wout}
    blocks = [(grads_full[n].reshape(N_DEV, r, cdim) if n == "w_out"
               else grads_full[n].reshape(r, N_DEV, cdim).transpose(1, 0, 2)).astype(BF16) for n, r, cdim in SHARDED[1:]]
    dmod = jnp.concatenate([dshift, dscale, dgate], axis=1)
    small = jnp.concatenate([dmod, g_gq, g_gkv, g_mu_r, g_mu_k, g_mu_v, g_mu_l, g_w0, g_a0, g_k_k, g_k_a, g_r_k,
                             g_gn_g, g_gn_b, g_post_g, g_post_b, loss_row], axis=1)
    my_x, my_y, my_c = lax.axis_index("x"), lax.axis_index("y"), lax.axis_index("c")
    chip_order = [4 * (my_x ^ fx) + 2 * (my_y ^ fy) for fx, fy in ((1, 1), (1, 0), (0, 1), (0, 0))]
    owners = [chip_order[s % 4] + (my_c if s >= 4 else 1 - my_c) for s in WGRAD_SLOTS]
    order = jnp.stack(owners + [jnp.int32(s) for s in WGRAD_SLOTS]).astype(jnp.int32)
    *got_blocks, got_small = in_weight_grad_exchange(h_t, dproj_blocks, blocks, small, order)
    loss = jnp.sum(got_small[:, 0, SMALL_ELEMS])

    ada_cols = w_ada.shape[2]
    dmod_all = got_small[:, 0, :3 * D_MODEL]
    g_ada = ada_weight_grad(c_all, lax.dynamic_slice_in_dim(dmod_all, me * ada_cols, ada_cols, axis=1))

    outs = [dict() for _ in range(4)]
    res = adamw(g_ada[None], w_ada, m_w_ada, v_w_ada, "adamw_w_ada")
    for kind in range(4):
        outs[kind]["w_ada"] = res[kind]
    for (n, r, cdim), got in zip(SHARDED, got_blocks):
        res = adamw(got, weights[n], mom1[n], mom2[n], "adamw_" + n)
        for kind in range(4):
            outs[kind][n] = res[kind]
    rows_of = lambda tree: [tree[n].reshape(1, -1) for n, _ in SMALL]
    res = adamw_small(got_small, rows_of(weights), rows_of(mom1), rows_of(mom2))
    for kind in range(4):
        for a, (n, _) in enumerate(SMALL):
            outs[kind][n] = res[kind * len(SMALL) + a].reshape(weights[n].shape)
    return (loss, dx[None], *[outs[0][n] for n in names], *[outs[1][n] for n in names],
            *[outs[2][n] for n in names], *[outs[3][n] for n in names])
```

```python
import functools
import math

import jax
import jax.numpy as jnp
from jax import lax
from jax.experimental import pallas as pl
from jax.experimental.pallas import tpu as pltpu

F32 = jnp.float32
BF16 = jnp.bfloat16
HIGHEST = lax.Precision.HIGHEST
MESH_IDS = pl.DeviceIdType.MESH

N_DEV = 8
D_MODEL = 1024
LN_EPS = 1e-5
RMS_EPS = 1e-6
GN_EPS = 64e-5
HEADS = 8
Q_RANK = 256
KV_RANK = 128
ROPE = 32
NOPE = 64
QK_DIM = NOPE + ROPE
WIDTH = 512
HEAD = 64
LORA = 64
CHUNK = 64
DEPTH = 1
ALPHA = (2.0 * DEPTH) ** 0.25
ROPE_THETA = 10000.0
ATTN_SCALE = QK_DIM ** -0.5
DECAY_SCALE = math.exp(-0.5)

ADAM_LR = 0.001
ADAM_B1 = 0.9
ADAM_B2 = 0.999
ADAM_EPS = 1e-08
ADAM_WD = 0.01
ADAM_STEP = 10

LANE = 128
PAIR = 2 * HEAD
ROW_TILE = 256
PREP_TILE = 512
HALO_ROWS = 16
ATTN_FWD_TILES = (512, 1024)
ATTN_BWD_TILES = (512, 512)
LOG2_E = math.log2(math.e)
Q_PRESCALE = ATTN_SCALE * LOG2_E
WKV_CHUNKS_PER_STEP = 8
WGRAD_SLOTS = (0, 1, 4, 2, 5, 6, 3, 7)
VMEM_LIMIT = 56 * 1024 * 1024

P_MA, P_MB, P_R, P_K, P_V, P_GPA, P_GPB, P_QC, P_KVC, P_KR, P_KRR, P_LORA = (
    0, 1024, 2048, 2560, 3072, 3584, 4096, 4608, 4864, 4992, 5120, 5248)
P_WIDTH = 5376

N_QC, N_KVC, N_KROPE, N_GPA, N_RWKV, N_GPB, N_MA, N_MB = 0, 256, 384, 416, 928, 2592, 3104, 4128
IN_WIDTH = 5152

SHARDED = (("w_in", 1024, 644), ("w_uq", 256, 96), ("w_ukv", 128, 128), ("w_decay_up", 64, 64),
           ("w_iclr_up", 64, 64), ("w_proj_a", 512, 128), ("w_proj_b", 512, 128), ("w_out", 128, 1024))
SMALL = (("b_ada", 3072), ("q_norm_g", 256), ("kv_norm_g", 128), ("mu_rwkv", 1664), ("w0", 512), ("a0", 512),
         ("k_k", 512), ("k_a", 512), ("r_k", 512), ("gn_g", 512), ("gn_b", 512), ("post_g", 1024), ("post_b", 1024))
SMALL_ELEMS = sum(n for _, n in SMALL)


def mm(a, b):
    return jnp.dot(a.astype(BF16), b.astype(BF16), preferred_element_type=F32)


def mm_nt(a, b):
    return lax.dot_general(a.astype(BF16), b.astype(BF16), (((1,), (1,)), ((), ())), preferred_element_type=F32)


def mm_tn(a, b):
    return lax.dot_general(a.astype(BF16), b.astype(BF16), (((0,), (0,)), ((), ())), preferred_element_type=F32)


def hdot(a, b):
    return jnp.dot(a, b, precision=HIGHEST, preferred_element_type=F32)


def hdot_tn(a, b):
    return lax.dot_general(a, b, (((0,), (0,)), ((), ())), precision=HIGHEST, preferred_element_type=F32)


def sigmoid(x):
    return 1.0 / (1.0 + jnp.exp(-x))


def colsum(x):
    return jnp.sum(x, axis=0, keepdims=True)


def rowmean(x):
    return jnp.mean(x, axis=-1, keepdims=True)


def layer_norm_stats(x):
    xc = x - rowmean(x)
    rstd = lax.rsqrt(rowmean(xc * xc) + LN_EPS)
    return xc * rstd, rstd


def layer_norm_bwd(dy, xhat, rstd):
    return rstd * (dy - rowmean(dy) - xhat * rowmean(dy * xhat))


def bf16_pieces(x, n):
    pieces = []
    for _ in range(n):
        p = x.astype(BF16)
        pieces.append(p)
        x = x - p.astype(F32)
    return pieces


def ones_dot(ones, x, n_pieces):
    ones = ones.astype(BF16)
    return sum(jnp.dot(ones, p, preferred_element_type=F32) for p in bf16_pieces(x, n_pieces))


def ones_dot_nt(ones, x, n_pieces):
    ones = ones.astype(BF16)
    return sum(lax.dot_general(ones, p, (((1,), (1,)), ((), ())), preferred_element_type=F32)
               for p in bf16_pieces(x, n_pieces))


def head_sum(x, bd):
    return jnp.concatenate([mm(x[:, p * LANE:(p + 1) * LANE], bd) for p in range(x.shape[1] // LANE)], axis=1)


def tile_lanes(t, n):
    return jnp.concatenate([t] * n, axis=1)


def row_iota(shape):
    return lax.broadcasted_iota(jnp.int32, shape, 0)


def lane_iota(shape):
    return lax.broadcasted_iota(jnp.int32, shape, 1)


def shift_rows_down(x, row0):
    rolled = pltpu.roll(x, 1, axis=0)
    return jnp.where(row_iota(x.shape) == 0, row0, rolled)


def shift_rows_up(x, row_last):
    rolled = pltpu.roll(x, x.shape[0] - 1, axis=0)
    return jnp.where(row_iota(x.shape) == x.shape[0] - 1, row_last, rolled)


def row_call(name, fn, n_rows, row_in, const_in, row_out, acc_out=(), halo_in=(), carry=(), reverse=False,
             tile_rows=ROW_TILE):
    ts = tile_rows
    n_tiles = n_rows // ts
    n_in = len(row_in) + len(halo_in) + len(const_in)
    n_ro, n_ao = len(row_out), len(acc_out)

    def tile_of(g):
        return (n_tiles - 1 - g) if reverse else g

    def body(*refs):
        ins = refs[:n_in]
        ro = refs[n_in:n_in + n_ro]
        ao = refs[n_in + n_ro:n_in + n_ro + n_ao]
        cr = refs[n_in + n_ro + n_ao:]
        g = pl.program_id(0)
        step0 = g == 0
        tile0 = tile_of(g) == 0
        for r in cr:
            @pl.when(step0)
            def _(r=r):
                r[...] = jnp.zeros_like(r)
        n_tiled = len(row_in) + len(halo_in)
        vals = [r[...].astype(F32) for r in ins[:n_tiled]] + [r[...] for r in ins[n_tiled:]]
        outs = fn(step0, tile0, *vals, *[c[0:1, :] for c in cr])
        for r, v in zip(ro, outs[:n_ro]):
            r[...] = v.astype(r.dtype)
        for r, v in zip(ao, outs[n_ro:n_ro + n_ao]):
            @pl.when(step0)
            def _(r=r, v=v):
                r[...] = v.astype(r.dtype)

            @pl.when(jnp.logical_not(step0))
            def _(r=r, v=v):
                r[...] += v.astype(r.dtype)
        for r, v in zip(cr, outs[n_ro + n_ao:]):
            r[0:1, :] = v

    in_specs = [pl.BlockSpec((ts, w), functools.partial(lambda g, cb: (tile_of(g), cb), cb=cb)) for _, w, cb in row_in]
    in_specs += [pl.BlockSpec((HALO_ROWS, w), functools.partial(
        lambda g, cb: (jnp.maximum(tile_of(g) * (ts // HALO_ROWS) - 1, 0), cb), cb=cb)) for _, w, cb in halo_in]
    in_specs += [pl.BlockSpec(memory_space=pltpu.VMEM) for _ in const_in]
    out_specs = [pl.BlockSpec((ts, w), lambda g: (tile_of(g), 0)) for w, _ in row_out]
    out_specs += [pl.BlockSpec(s, lambda g: (0, 0)) for s, _ in acc_out]
    out_shape = [jax.ShapeDtypeStruct((n_rows, w), d) for w, d in row_out]
    out_shape += [jax.ShapeDtypeStruct(s, d) for s, d in acc_out]
    return pl.pallas_call(
        body, name=name, grid=(n_tiles,), in_specs=in_specs, out_specs=out_specs, out_shape=out_shape,
        scratch_shapes=[pltpu.VMEM((8, w), F32) for w in carry],
        compiler_params=pltpu.CompilerParams(dimension_semantics=("arbitrary",), vmem_limit_bytes=VMEM_LIMIT),
    )(*[a for a, _, _ in row_in], *[a for a, _, _ in halo_in], *const_in)


def my_position():
    return lax.axis_index("x"), lax.axis_index("y"), lax.axis_index("c")


def flip(pos, k):
    x, y, c = pos
    dx, dy, dc = (k >> 2) & 1, (k >> 1) & 1, k & 1
    return (1 - x if dx else x, 1 - y if dy else y, 1 - c if dc else c)


def flat_index(pos):
    return 4 * pos[0] + 2 * pos[1] + pos[2]


def gather_shards(shards):
    n = len(shards)

    def body(*refs):
        x_refs, out_refs = refs[:n], refs[n:2 * n]
        send_sems, recv_sems, local_sems = refs[2 * n:]
        x, y, c = my_position()
        me, sibling = (x, y, c), (x, y, 1 - c)
        chips = [(1 - x, y), (x, 1 - y), (1 - x, 1 - y)]

        def copy(a, k, block, to, from_input=False):
            slot = out_refs[a].at[flat_index(block)]
            return pltpu.make_async_remote_copy(
                src_ref=x_refs[a] if from_input else slot, dst_ref=slot,
                send_sem=send_sems.at[7 * a + k], recv_sem=recv_sems.at[7 * a + k],
                device_id=to, device_id_type=MESH_IDS)

        mine = [pltpu.make_async_copy(x_refs[a], out_refs[a].at[flat_index(me)], local_sems.at[a]) for a in range(n)]
        for cp in mine:
            cp.start()
        first = []
        for a in range(n):
            first.append(copy(a, 0, me, sibling, from_input=True))
            first += [copy(a, 1 + j, me, (*chip, c), from_input=True) for j, chip in enumerate(chips)]
        for cp in first:
            cp.start()
        passed = []
        for j, chip in enumerate(chips):
            for a in range(n):
                copy(a, 1 + j, (*chip, c), me).wait_recv()
                cp = copy(a, 4 + j, (*chip, c), sibling)
                cp.start()
                passed.append(cp)
        for a in range(n):
            copy(a, 0, sibling, me).wait_recv()
            for j, chip in enumerate(chips):
                copy(a, 4 + j, (*chip, 1 - c), me).wait_recv()
        for cp in first + passed:
            cp.wait_send()
        for cp in mine:
            cp.wait()

    return pl.pallas_call(
        body, name="gather_shards",
        out_shape=[jax.ShapeDtypeStruct((N_DEV,) + s.shape, s.dtype) for s in shards],
        in_specs=[pl.BlockSpec(memory_space=pl.ANY)] * n, out_specs=[pl.BlockSpec(memory_space=pl.ANY)] * n,
        scratch_shapes=[pltpu.SemaphoreType.DMA((7 * n,)), pltpu.SemaphoreType.DMA((7 * n,)),
                        pltpu.SemaphoreType.DMA((n,))],
    )(*shards)


def ada_modulation(c_all, w_ada_loc, b_ada_blocks):
    cols = w_ada_loc.shape[1]

    def body(c_ref, w_ref, b_ref, out_ref, send_sems, recv_sems):
        me = my_position()
        mi = flat_index(me)
        cv = c_ref[...]
        res = hdot(cv * sigmoid(cv), w_ref[...]) + b_ref[pl.ds(mi, 1), :]
        out_ref[mi] = res
        sends = []
        for k in range(1, N_DEV):
            cp = pltpu.make_async_remote_copy(
                src_ref=out_ref.at[mi], dst_ref=out_ref.at[mi], send_sem=send_sems.at[k - 1],
                recv_sem=recv_sems.at[k - 1], device_id=flip(me, k), device_id_type=MESH_IDS)
            cp.start()
            sends.append(cp)
        for k in range(1, N_DEV):
            pi = flat_index(flip(me, k))
            pltpu.make_async_remote_copy(
                src_ref=out_ref.at[pi], dst_ref=out_ref.at[pi], send_sem=send_sems.at[k - 1],
                recv_sem=recv_sems.at[k - 1], device_id=flip(me, k), device_id_type=MESH_IDS).wait_recv()
        for cp in sends:
            cp.wait_send()

    return pl.pallas_call(
        body, name="ada_modulation",
        out_shape=jax.ShapeDtypeStruct((N_DEV, N_DEV, cols), F32),
        in_specs=[pl.BlockSpec(memory_space=pltpu.VMEM)] * 3, out_specs=pl.BlockSpec(memory_space=pltpu.VMEM),
        scratch_shapes=[pltpu.SemaphoreType.DMA((7,)), pltpu.SemaphoreType.DMA((7,))],
    )(c_all, w_ada_loc, b_ada_blocks)


def fwd_in_tile(step0, tile0, x, mod, w_in_ptt):
    xhat, _ = layer_norm_stats(x)
    h = xhat * (1.0 + mod[1:2]) + mod[0:1]
    return (mm_nt(h, w_in_ptt),)


def fwd_in_gather(x, mod, w_in_pt, shards):
    n = len(shards)
    n_rows = x.shape[0]
    ts = ROW_TILE
    n_tiles = n_rows // ts

    def body(x_ref, mod_ref, w_ref, *rest):
        s_refs = rest[:n]
        proj_ref, out_refs = rest[n], rest[n + 1:2 * n + 1]
        send_sems, recv_sems, local_sems = rest[2 * n + 1:]
        g = pl.program_id(0)
        me = my_position()
        mi = flat_index(me)

        def copies(k, slot):
            return [pltpu.make_async_remote_copy(
                src_ref=s_refs[a], dst_ref=out_refs[a].at[slot], send_sem=send_sems.at[7 * a + k - 1],
                recv_sem=recv_sems.at[7 * a + k - 1], device_id=flip(me, k), device_id_type=MESH_IDS)
                for a in range(n)]

        local = [pltpu.make_async_copy(s_refs[a], out_refs[a].at[mi], local_sems.at[a]) for a in range(n)]

        @pl.when(g == 0)
        def _():
            for cp in local:
                cp.start()
            for k in range(1, N_DEV):
                for cp in copies(k, mi):
                    cp.start()

        proj_ref[...] = fwd_in_tile(None, None, x_ref[...], mod_ref[...], w_ref[...])[0].astype(BF16)

        @pl.when(g == n_tiles - 1)
        def _():
            for k in range(1, N_DEV):
                for cp in copies(k, flat_index(flip(me, k))):
                    cp.wait_recv()
            for k in range(1, N_DEV):
                for cp in copies(k, mi):
                    cp.wait_send()
            for cp in local:
                cp.wait()

    hbm = pl.BlockSpec(memory_space=pl.ANY)
    const = pl.BlockSpec(memory_space=pltpu.VMEM)
    return pl.pallas_call(
        body, name="fwd_in_gather", grid=(n_tiles,),
        in_specs=[pl.BlockSpec((ts, D_MODEL), lambda g: (g, 0)), const, const] + [hbm] * n,
        out_specs=[pl.BlockSpec((ts, P_WIDTH), lambda g: (g, 0))] + [hbm] * n,
        out_shape=[jax.ShapeDtypeStruct((n_rows, P_WIDTH), BF16)]
        + [jax.ShapeDtypeStruct((N_DEV,) + s.shape, s.dtype) for s in shards],
        scratch_shapes=[pltpu.SemaphoreType.DMA((7 * n,)), pltpu.SemaphoreType.DMA((7 * n,)),
                        pltpu.SemaphoreType.DMA((n,))],
        compiler_params=pltpu.CompilerParams(dimension_semantics=("arbitrary",), vmem_limit_bytes=VMEM_LIMIT),
    )(x, mod, w_in_pt, *shards)


def rms_norm_fwd(x, g):
    r = lax.rsqrt(rowmean(x * x) + RMS_EPS)
    xh = x * r
    return xh * g, xh, r


def key_rope_mask(shape):
    return (lane_iota(shape) >= NOPE).astype(F32)


def mla_prep_tile(step0, tile0, q_c, kv_c, kr, krr, cos, sin, gq, gkv, wq, wqr, wkn, wv):
    qn, _, _ = rms_norm_fwd(q_c, gq)
    kvn, _, _ = rms_norm_fwd(kv_c, gkv)
    q = (mm(qn, wq) * tile_lanes(cos, HEADS) + mm(qn, wqr) * tile_lanes(sin, HEADS)) * Q_PRESCALE
    kpe = kr * (cos * key_rope_mask(cos.shape)) + krr * sin
    k = mm(kvn, wkn) + tile_lanes(kpe, HEADS)
    v = mm(kvn, wv)
    return q, k, v


def rwkv_prep_core(tile0, r0, k0, v0, l0, hr, hk, hv, hl, mu_r, mu_k, mu_v, mu_l, w0, a0, k_k, k_a,
                   w_dec, w_iclr, tril, same, bd):
    def shifted(x, halo, mu):
        row0 = jnp.where(tile0, 0.0, halo[HALO_ROWS - 1:HALO_ROWS, :])
        prev = shift_rows_down(x, row0)
        return x + (prev - x) * mu, prev

    ur, pr = shifted(r0, hr, mu_r)
    uk, pk = shifted(k0, hk, mu_k)
    uv, pv = shifted(v0, hv, mu_v)
    ul, plo = shifted(l0, hl, mu_l)
    th = jnp.tanh(ul)
    sg = sigmoid(w0 + mm(th, w_dec))
    lw = -DECAY_SCALE * sg
    a_ic = sigmoid(a0 + mm(ul, w_iclr))
    kkraw = uk * k_k
    nrm_raw = jnp.sqrt(head_sum(kkraw * kkraw, bd))
    nrm = jnp.maximum(nrm_raw, 1e-12)
    kk = kkraw / nrm
    k2 = uk * (1.0 + (a_ic - 1.0) * k_a)
    lc = ones_dot(tril, lw, 3)
    lcl = ones_dot(same, lw, 3)
    return dict(ur=ur, uk=uk, uv=uv, ul=ul, pr=pr, pk=pk, pv=pv, pl=plo, th=th, sg=sg, lw=lw, a_ic=a_ic,
                kkraw=kkraw, nrm_raw=nrm_raw, nrm=nrm, kk=kk, k2=k2, lc=lc, lcl=lcl)


def rwkv_prep_tile(step0, tile0, r0, k0, v0, l0, hr, hk, hv, hl, *consts):
    f = rwkv_prep_core(tile0, r0, k0, v0, l0, hr, hk, hv, hl, *consts)
    lc, lw = f["lc"], f["lw"]
    e_neg = jnp.exp(-lc)
    rt = f["ur"] * jnp.exp(lc)
    at = -f["kk"] * jnp.exp(lc - lw)
    bt = f["kk"] * f["a_ic"] * e_neg
    kt = f["k2"] * e_neg
    return rt, at, bt, kt, jnp.exp(f["lcl"]), f["uv"], f["ur"], f["k2"]


def wkv_masks():
    lane = lane_iota((1, PAIR))
    m_lo = (lane < HEAD).astype(F32)
    r2 = row_iota((PAIR, PAIR))
    c2 = lane_iota((PAIR, PAIR))
    bd = ((r2 < HEAD) == (c2 < HEAD)).astype(F32)
    eye2 = (r2 == c2).astype(F32)
    eye = (row_iota((CHUNK, CHUNK)) == lane_iota((CHUNK, CHUNK))).astype(F32)
    t_idx = row_iota((4 * CHUNK, PAIR)) % CHUNK
    s_idx = lane_iota((4 * CHUNK, PAIR)) % CHUNK
    keep = s_idx < t_idx + (row_iota((4 * CHUNK, PAIR)) >= 2 * CHUNK).astype(jnp.int32)
    return (m_lo, 1.0 - m_lo), keep, eye, bd, eye2


def rows(*parts):
    return jnp.concatenate(parts, axis=0)


def lanes(*parts):
    return jnp.concatenate(parts, axis=1)


def head_rows(x, ms):
    return rows(x * ms[0], x * ms[1])


def wkv_score_stack(at, rt, ms):
    return rows(head_rows(at, ms), head_rows(rt, ms))


def wkv_chunks_pre(chunks, masks):
    ms, keep, eye, bd, eye2 = masks
    n = len(chunks)
    at, bt, kt, rt, v, cl = (list(t) for t in zip(*chunks))
    scores = [jnp.where(keep, mm_nt(wkv_score_stack(a, r, ms), rows(b, k)), 0.0)
              for a, r, b, k in zip(at, rt, bt, kt)]
    q = CHUNK
    aab = [s[h * q:(h + 1) * q, :q] for s in scores for h in range(2)]
    tinv = [eye + a for a in aab]
    power = [mm(a, a) for a in aab]
    for _ in range(5):
        both = [mm(rows(t, p), p) for t, p in zip(tinv, power)]
        tinv = [t + x[:q] for t, x in zip(tinv, both)]
        power = [x[q:] for x in both]
    pair = lambda c, row0, col0: lanes(scores[c][row0:row0 + q, col0:col0 + q],
                                       scores[c][row0 + q:row0 + 2 * q, col0:col0 + q])
    tinv_p = [lanes(tinv[2 * c], tinv[2 * c + 1]) for c in range(n)]
    aak_p = [pair(c, 0, q) for c in range(n)]
    prb_p = [pair(c, 2 * q, 0) for c in range(n)]
    prk_p = [pair(c, 2 * q, q) for c in range(n)]
    v_rows = [head_rows(x, ms) for x in v]
    wy = [mm(rows(a, p), x) for a, p, x in zip(aak_p, prk_p, v_rows)]
    w = [x[:q] for x in wy]
    yh2 = [x[q:] for x in wy]
    aw = [mm(t, lanes(head_rows(a, ms), head_rows(w_, ms))) for t, a, w_ in zip(tinv_p, at, w)]
    ah = [x[:, :PAIR] for x in aw]
    wh = [x[:, PAIR:] for x in aw]
    ry = [mm(p, lanes(head_rows(a, ms), head_rows(w_, ms))) for p, a, w_ in zip(prb_p, ah, wh)]
    rh = [r + x[:, :PAIR] for r, x in zip(rt, ry)]
    yh = [x[:, PAIR:] + y for x, y in zip(ry, yh2)]
    bc = [b * c_ for b, c_ in zip(bt, cl)]
    kc = [k * c_ for k, c_ in zip(kt, cl)]
    gh = [mm_tn(b, lanes(a, w_)) for b, a, w_ in zip(bc, ah, wh)]
    g = [eye2 * c_ + bd * x[:, :PAIR] for c_, x in zip(cl, gh)]
    h = [bd * (x[:, PAIR:] + mm_tn(k, v_)) for x, k, v_ in zip(gh, kc, v)]
    as_bf16 = lambda xs: [x.astype(BF16) for x in xs]
    saved = (as_bf16(tinv_p), as_bf16(aak_p), as_bf16(prb_p), as_bf16(prk_p), as_bf16(ah), wh)
    return g, h, rh, yh, saved


def wkv_chunks_grad(chunks, saved, m0, dy, dm1, masks):
    ms, keep, eye, bd, eye2 = masks
    n = len(chunks)
    q = CHUNK
    at, bt, kt, rt, v, cl = (list(t) for t in zip(*chunks))
    tinv_p, aak_p, prb_p, prk_p, ah, wh = (list(t) for t in zip(*saved))
    head_stack = lambda p: rows(p[:, :q], p[:, q:])
    bc = [b * c_ for b, c_ in zip(bt, cl)]
    kc = [k * c_ for k, c_ in zip(kt, cl)]
    u = [mm(a, m) + w for a, m, w in zip(ah, m0, wh)]
    dm1 = [d * bd for d in dm1]
    from_state = [mm(rows(b, k), d) for b, k, d in zip(bc, kc, dm1)]
    dy_rows = [head_rows(d, ms) for d in dy]
    from_out = [mm_tn(lanes(head_stack(pb), head_stack(pk)), d) for pb, pk, d in zip(prb_p, prk_p, dy_rows)]
    du = [a[:q] + b[:q] for a, b in zip(from_state, from_out)]
    dv = [a[q:] + b[q:] for a, b in zip(from_state, from_out)]
    dz = [mm_tn(head_stack(t), head_rows(d, ms)) for t, d in zip(tinv_p, du)]
    dz_rows = [head_rows(d, ms) for d in dz]
    dv = [a + mm_tn(head_stack(k), d) for a, k, d in zip(dv, aak_p, dz_rows)]
    by_m0 = [mm_nt(rows(d, z), m) for d, z, m in zip(dy, dz, m0)]
    uv = [rows(x, y) for x, y in zip(u, v)]
    by_dm1 = [mm_nt(x, d) for x, d in zip(uv, dm1)]
    udm = [x[:q] for x in by_dm1]
    vdm = [x[q:] for x in by_dm1]
    dscores = [jnp.where(keep, mm_nt(rows(z, d), x), 0.0) for z, d, x in zip(dz_rows, dy_rows, uv)]
    to_ar = [mm(d, rows(b, k)) for d, b, k in zip(dscores, bt, kt)]
    to_bk = [mm_tn(d, wkv_score_stack(a, r, ms)) for d, a, r in zip(dscores, at, rt)]
    ones = jnp.ones((8, PAIR), F32)
    upper = (lane_iota((CHUNK, CHUNK)) >= row_iota((CHUNK, CHUNK))).astype(F32)
    out = []
    for c in range(n):
        e = to_ar[c]
        dat_c = by_m0[c][q:] + e[:q] * ms[0] + e[q:2 * q] * ms[1]
        drt_c = by_m0[c][:q] + e[2 * q:3 * q] * ms[0] + e[3 * q:] * ms[1]
        dbt_c = udm[c] * cl[c] + to_bk[c][:q]
        dkt_c = vdm[c] * cl[c] + to_bk[c][q:]
        dlcl = ones_dot_nt(ones, dm1[c] * m0[c], 3)[0:1, :] * cl[c] + colsum(bc[c] * udm[c] + kc[c] * vdm[c])
        g = drt_c * rt[c] - dbt_c * bt[c] - dkt_c * kt[c] + dat_c * at[c]
        dlw = ones_dot(upper, g, 3) - dat_c * at[c] + dlcl
        out.append((dat_c, dbt_c, dkt_c, drt_c, dv[c], dlw))
    return out


def wkv_forward(at, bt, kt, rt, v, clf):
    n_rows = at.shape[0]
    cps = WKV_CHUNKS_PER_STEP
    rb = cps * CHUNK
    n_steps = n_rows // rb

    def body(a_ref, b_ref, k_ref, r_ref, v_ref, c_ref, y_ref, m0_ref, g_ref, rh_ref, *rest):
        saved_refs, m_scr = rest[:6], rest[6]

        @pl.when(pl.program_id(1) == 0)
        def _():
            m_scr[...] = jnp.zeros_like(m_scr)

        masks = wkv_masks()
        chunks = []
        for cc in range(cps):
            sl = slice(cc * CHUNK, (cc + 1) * CHUNK)
            chunks.append((a_ref[sl, :], b_ref[sl, :], k_ref[sl, :], r_ref[sl, :], v_ref[sl, :],
                           c_ref[cc * CHUNK:cc * CHUNK + 1, :]))
        gs, hs, rhs, yhs, saved = wkv_chunks_pre(chunks, masks)
        for ref, per_chunk in zip(saved_refs, saved):
            for cc, val in enumerate(per_chunk):
                ref[cc * CHUNK:(cc + 1) * CHUNK, :] = val
        m = m_scr[...]
        for cc, (g, h, rh, yh) in enumerate(zip(gs, hs, rhs, yhs)):
            sl = slice(cc * CHUNK, (cc + 1) * CHUNK)
            m0_ref[0, cc] = m
            g_ref[0, cc] = g
            rh_ref[sl, :] = rh
            y_ref[sl, :] = hdot(rh, m) + yh
            m = hdot(g, m) + h
        m_scr[...] = m

    blk = pl.BlockSpec((rb, PAIR), lambda p, s: (s, p))
    state_blk = pl.BlockSpec((1, cps, PAIR, PAIR), lambda p, s: (p, s, 0, 0))
    state_shape = jax.ShapeDtypeStruct((WIDTH // PAIR, n_rows // CHUNK, PAIR, PAIR), F32)
    rows_f32 = jax.ShapeDtypeStruct((n_rows, WIDTH), F32)
    rows_bf16 = jax.ShapeDtypeStruct((n_rows, WIDTH), BF16)
    return pl.pallas_call(
        body, name="wkv_forward", grid=(WIDTH // PAIR, n_steps),
        in_specs=[blk] * 6,
        out_specs=[blk, state_blk, state_blk, blk] + [blk] * 6,
        out_shape=[rows_f32, state_shape, state_shape, rows_f32] + [rows_bf16] * 5 + [rows_f32],
        scratch_shapes=[pltpu.VMEM((PAIR, PAIR), F32)],
        compiler_params=pltpu.CompilerParams(dimension_semantics=("arbitrary", "arbitrary"),
                                             vmem_limit_bytes=VMEM_LIMIT),
    )(at, bt, kt, rt, v, clf)


def wkv_backward(at, bt, kt, rt, v, clf, m0s, gs, rh, saved, dy):
    n_rows = at.shape[0]
    cps = WKV_CHUNKS_PER_STEP
    rb = cps * CHUNK
    n_steps = n_rows // rb

    def body(a_ref, b_ref, k_ref, r_ref, v_ref, c_ref, m0_ref, g_ref, rh_ref, *rest):
        saved_refs, dy_ref = rest[:6], rest[6]
        da_ref, db_ref, dk_ref, dr_ref, dv_ref, dlw_ref, dm_scr = rest[7:]

        @pl.when(pl.program_id(1) == 0)
        def _():
            dm_scr[...] = jnp.zeros_like(dm_scr)

        masks = wkv_masks()
        bd = masks[3]
        dm = dm_scr[...]
        dm1 = [None] * cps
        for cc in reversed(range(cps)):
            sl = slice(cc * CHUNK, (cc + 1) * CHUNK)
            dm1[cc] = dm
            dm = bd * (hdot_tn(g_ref[0, cc], dm) + hdot_tn(rh_ref[sl, :], dy_ref[sl, :]))
        dm_scr[...] = dm
        chunks, kept, m0, dys = [], [], [], []
        for cc in range(cps):
            sl = slice(cc * CHUNK, (cc + 1) * CHUNK)
            chunks.append((a_ref[sl, :], b_ref[sl, :], k_ref[sl, :], r_ref[sl, :], v_ref[sl, :],
                           c_ref[cc * CHUNK:cc * CHUNK + 1, :]))
            kept.append(tuple(ref[sl, :] for ref in saved_refs))
            m0.append(m0_ref[0, cc])
            dys.append(dy_ref[sl, :])
        grads = wkv_chunks_grad(chunks, kept, m0, dys, dm1, masks)
        for cc, (dat, dbt, dkt, drt, dv, dlw) in enumerate(grads):
            sl = slice(cc * CHUNK, (cc + 1) * CHUNK)
            da_ref[sl, :] = dat
            db_ref[sl, :] = dbt
            dk_ref[sl, :] = dkt
            dr_ref[sl, :] = drt
            dv_ref[sl, :] = dv
            dlw_ref[sl, :] = dlw

    blk = pl.BlockSpec((rb, PAIR), lambda p, s: (n_steps - 1 - s, p))
    state_blk = pl.BlockSpec((1, cps, PAIR, PAIR), lambda p, s: (p, n_steps - 1 - s, 0, 0))
    return pl.pallas_call(
        body, name="wkv_backward", grid=(WIDTH // PAIR, n_steps),
        in_specs=[blk] * 6 + [state_blk, state_blk, blk] + [blk] * 6 + [blk],
        out_specs=[blk] * 6,
        out_shape=[jax.ShapeDtypeStruct((n_rows, WIDTH), F32)] * 6,
        scratch_shapes=[pltpu.VMEM((PAIR, PAIR), F32)],
        compiler_params=pltpu.CompilerParams(dimension_semantics=("arbitrary", "arbitrary"),
                                             vmem_limit_bytes=VMEM_LIMIT),
    )(at, bt, kt, rt, v, clf, m0s, gs, rh, *saved, dy)


def visible(q_row0, k_row0, shape):
    qc = (q_row0 + row_iota(shape)) // CHUNK
    kc = (k_row0 + lane_iota(shape)) // CHUNK
    return kc <= qc


def attention_forward(q, k, v):
    n_rows = q.shape[0]
    tq, tk = ATTN_FWD_TILES
    n_q = n_rows // tq
    assert tk % tq == 0

    def body(q_ref, k_ref, v_ref, o_ref, lse_ref):
        i = pl.program_id(1)
        lane = lane_iota((tq, LANE))
        heads = [slice(0, LANE), slice(LANE, 2 * LANE)]
        qs = [q_ref[:, cols] for cols in heads]

        def step(j, carry, size, masked):
            rows = pl.ds(pl.multiple_of(j * size, size), size)
            ss = [mm_nt(qh, k_ref[rows, cols]) for qh, cols in zip(qs, heads)]
            if masked:
                vis = visible(i * tq, j * size, ss[0].shape)
                ss = [jnp.where(vis, s, -jnp.inf) for s in ss]
            ps, stats = [], []
            for s, (m, l, _) in zip(ss, carry):
                m_new = jnp.maximum(m, jnp.max(s, axis=-1, keepdims=True))
                p = jnp.exp2(s - m_new)
                alpha = jnp.exp2(m - m_new)
                ps.append(p)
                stats.append((m_new, alpha, alpha * l + jnp.sum(p, axis=-1, keepdims=True)))
            pvs = [mm(p, v_ref[rows, cols]) for p, cols in zip(ps, heads)]
            return tuple((m_new, l, alpha * acc + pv)
                         for (m_new, alpha, l), (_, _, acc), pv in zip(stats, carry, pvs))

        carry = tuple((jnp.full((tq, 1), -jnp.inf, F32), jnp.zeros((tq, 1), F32), jnp.zeros((tq, LANE), F32))
                      for _ in heads)
        n_full = (i * tq) // tk
        carry = lax.fori_loop(0, n_full, functools.partial(step, size=tk, masked=False), carry)
        (m0, l0, acc0), (m1, l1, acc1) = step(n_full, carry, size=tk, masked=True)
        o_ref[...] = acc0 / l0 + acc1 / l1
        lse_ref[...] = jnp.where(lane >= HEAD, m1 + jnp.log2(l1), m0 + jnp.log2(l0))

    return pl.pallas_call(
        body, name="attention_forward", grid=(HEADS // 2, n_q),
        in_specs=[pl.BlockSpec((tq, 2 * LANE), lambda p, i: (i, p)),
                  pl.BlockSpec((n_rows, 2 * LANE), lambda p, i: (0, p)),
                  pl.BlockSpec((n_rows, 2 * LANE), lambda p, i: (0, p))],
        out_specs=[pl.BlockSpec((tq, LANE), lambda p, i: (i, p))] * 2,
        out_shape=[jax.ShapeDtypeStruct((n_rows, WIDTH), F32)] * 2,
        compiler_params=pltpu.CompilerParams(dimension_semantics=("arbitrary", "arbitrary"),
                                             vmem_limit_bytes=VMEM_LIMIT),
    )(q, k, v)


def block_exchange(g_refs, rg_refs, send_sems, recv_sems, local_sems):
    n = len(g_refs)
    me = my_position()
    mi = flat_index(me)

    def copies(k, src_index, dst_index):
        return [pltpu.make_async_remote_copy(
            src_ref=g_refs[a].at[src_index], dst_ref=rg_refs[a].at[dst_index],
            send_sem=send_sems.at[7 * a + k - 1], recv_sem=recv_sems.at[7 * a + k - 1],
            device_id=flip(me, k), device_id_type=MESH_IDS) for a in range(n)]

    local = [pltpu.make_async_copy(g_refs[a].at[mi], rg_refs[a].at[mi], local_sems.at[a]) for a in range(n)]

    def start():
        for cp in local:
            cp.start()
        for k in range(1, N_DEV):
            for cp in copies(k, flat_index(flip(me, k)), mi):
                cp.start()

    def wait():
        for k in range(1, N_DEV):
            pi = flat_index(flip(me, k))
            for cp in copies(k, pi, pi):
                cp.wait_recv()
        for k in range(1, N_DEV):
            for cp in copies(k, flat_index(flip(me, k)), mi):
                cp.wait_send()
        for cp in local:
            cp.wait()

    return start, wait


def attention_backward(q, k, v, o, do, lse, riders):
    n_rows = q.shape[0]
    tq, tk = ATTN_BWD_TILES
    n_q = n_rows // tq
    n_k = n_rows // tk
    n_masked = max(1, tk // tq)
    n_r = len(riders)

    def body(q_ref, k_ref, v_ref, o_ref, do_ref, lse_ref, *rest):
        g_refs = rest[:n_r]
        dq_ref, dk_ref, dv_ref = rest[n_r:n_r + 3]
        rg_refs = rest[n_r + 3:2 * n_r + 3]
        start_riders, wait_riders = block_exchange(g_refs, rg_refs, *rest[2 * n_r + 3:])
        j = pl.program_id(1)

        @pl.when(jnp.logical_and(pl.program_id(0) == 0, j == 0))
        def _():
            start_riders()

        @pl.when(j == 0)
        def _():
            dq_ref[...] = jnp.zeros_like(dq_ref)

        lane = lane_iota((tq, LANE))
        heads = [slice(0, LANE), slice(LANE, 2 * LANE)]
        ks = [k_ref[:, cols] for cols in heads]
        vs = [v_ref[:, cols] for cols in heads]
        head_lanes = [(lane < HEAD).astype(F32), (lane >= HEAD).astype(F32)]

        def step(i, carry, masked):
            rows = pl.ds(pl.multiple_of(i * tq, tq), tq)
            qs = [q_ref[rows, cols] for cols in heads]
            dout = do_ref[rows, :]
            dout_o = dout * o_ref[rows, :]
            lse_t = lse_ref[rows, :]
            ss = [mm_nt(qh, kh) for qh, kh in zip(qs, ks)]
            dps = [mm_nt(dout, vh) for vh in vs]
            ps, dss = [], []
            for hh in range(2):
                delta = jnp.sum(dout_o * head_lanes[hh], axis=-1, keepdims=True)
                lse_h = jnp.sum(jnp.where(lane == hh * HEAD, lse_t, 0.0), axis=-1, keepdims=True)
                p = jnp.exp2(ss[hh] - lse_h)
                if masked:
                    p = jnp.where(visible(i * tq, j * tk, p.shape), p, 0.0)
                ps.append(p)
                dss.append(p * (dps[hh] - delta))
            dvs = [mm_tn(p, dout) for p in ps]
            dqs = [mm(ds, kh) for ds, kh in zip(dss, ks)]
            dks = [mm_tn(ds, qh) for ds, qh in zip(dss, qs)]
            for cols, dq in zip(heads, dqs):
                dq_ref[rows, cols] += dq * ATTN_SCALE
            return tuple((dk + a, dv + b) for (dk, dv), a, b in zip(carry, dks, dvs))

        carry = tuple((jnp.zeros((tk, LANE), F32), jnp.zeros((tk, LANE), F32)) for _ in heads)
        i_first = (j * tk) // tq
        for extra in range(n_masked):
            carry = step(i_first + extra, carry, masked=True)
        carry = lax.fori_loop(i_first + n_masked, n_q, functools.partial(step, masked=False), carry)
        for cols, (dk, dv) in zip(heads, carry):
            dk_ref[:, cols] = dk * (1.0 / LOG2_E)
            dv_ref[:, cols] = dv

        @pl.when(jnp.logical_and(pl.program_id(0) == HEADS // 2 - 1, j == n_k - 1))
        def _():
            wait_riders()

    full = lambda w: pl.BlockSpec((n_rows, w), lambda p, j: (0, p))
    blk = pl.BlockSpec((tk, 2 * LANE), lambda p, j: (j, p))
    hbm = pl.BlockSpec(memory_space=pl.ANY)
    return pl.pallas_call(
        body, name="attention_backward", grid=(HEADS // 2, n_k),
        in_specs=[full(2 * LANE), blk, blk, full(LANE), full(LANE), full(LANE)] + [hbm] * n_r,
        out_specs=[full(2 * LANE), blk, blk] + [hbm] * n_r,
        out_shape=[jax.ShapeDtypeStruct((n_rows, HEADS * LANE), F32)] * 3
        + [jax.ShapeDtypeStruct(r.shape, r.dtype) for r in riders],
        scratch_shapes=[pltpu.SemaphoreType.DMA((7 * n_r,)), pltpu.SemaphoreType.DMA((7 * n_r,)),
                        pltpu.SemaphoreType.DMA((n_r,))],
        compiler_params=pltpu.CompilerParams(dimension_semantics=("arbitrary", "arbitrary"),
                                             vmem_limit_bytes=VMEM_LIMIT),
    )(q, k, v, o, do, lse, *riders)


def tail_tile(step0, tile0, x, tgt, ma, mb, gpa, gpb, ya, y, ur, k2, uv,
              mod, wpa, wpb, wout, gn_g, gn_b, r_k, post_g, post_b, bd):
    gate = mod[2:3]
    inv = 1.0 / HEAD
    yc = y - head_sum(y, bd) * inv
    rs = lax.rsqrt(head_sum(yc * yc, bd) * inv + GN_EPS)
    yn = yc * rs
    yb = yn * gn_g + gn_b + head_sum(ur * k2 * r_k, bd) * uv
    sga, sgb = sigmoid(gpa), sigmoid(gpb)
    sila, silb = gpa * sga, gpb * sgb
    ga, gb = ya * sila, yb * silb
    pa, pb = mm(ga, wpa), mm(gb, wpb)
    sa, sb = sigmoid(ma), sigmoid(mb)
    merged = sa * pa + sb * pb
    sub = mm(merged, wout)
    z = ALPHA * x + (1.0 + gate) * sub
    zhat, rstd = layer_norm_stats(z)
    err = zhat * post_g + post_b - tgt
    loss = 0.5 * jnp.sum(rowmean(err * err), axis=0, keepdims=True) + jnp.zeros((1, LANE), F32)
    dout = err * (1.0 / D_MODEL)
    dpost_g = colsum(dout * zhat)
    dpost_b = colsum(dout)
    dz = layer_norm_bwd(dout * post_g, zhat, rstd)
    dgate = colsum(dz * sub)
    dsub = dz * (1.0 + gate)
    dwout = mm_tn(merged, dsub)
    dmerged = mm_nt(dsub, wout)
    dpa, dpb = dmerged * sa, dmerged * sb
    dma = dmerged * pa * sa * (1.0 - sa)
    dmb = dmerged * pb * sb * (1.0 - sb)
    dwpa = mm_tn(ga, dpa)
    dwpb = mm_tn(gb, dpb)
    dga = mm_nt(dpa, wpa)
    dgb = mm_nt(dpb, wpb)
    dya = dga * sila
    dgpa = dga * ya * (sga * (1.0 + gpa * (1.0 - sga)))
    dyb = dgb * silb
    dgpb = dgb * yb * (sgb * (1.0 + gpb * (1.0 - sgb)))
    dgn_g = colsum(dyb * yn)
    dgn_b = colsum(dyb)
    dyn = dyb * gn_g
    dy = rs * (dyn - head_sum(dyn, bd) * inv - yn * head_sum(dyn * yn, bd) * inv)
    return (dz, dma, dmb, dgpa, dgpb, dya, dy, dyb,
            loss, dwout, dwpa, dwpb, dgn_g, dgn_b, dpost_g, dpost_b, dgate)


def mla_prep_bwd_tile(step0, tile0, q_c, kv_c, cos, sin, dq, dk, dv, gq, gkv, wq, wqr, wkn, wv):
    qn, qh, rq = rms_norm_fwd(q_c, gq)
    kvn, kvh, rkv = rms_norm_fwd(kv_c, gkv)
    dqc = dq * tile_lanes(cos, HEADS)
    dqs = dq * tile_lanes(sin, HEADS)
    dqn = mm_nt(dqc, wq) + mm_nt(dqs, wqr)
    dkvn = mm_nt(dk, wkn) + mm_nt(dv, wv)
    dkpe = dk[:, 0:LANE]
    for h in range(1, HEADS):
        dkpe = dkpe + dk[:, h * LANE:(h + 1) * LANE]
    dkr = dkpe * (cos * key_rope_mask(cos.shape))
    dkrr = dkpe * sin

    def rms_bwd(dyv, xh, r, g):
        dyg = dyv * g
        return r * (dyg - xh * rowmean(dyg * xh)), colsum(dyv * xh)

    dq_c, dgq = rms_bwd(dqn, qh, rq, gq)
    dkv_c, dgkv = rms_bwd(dkvn, kvh, rkv, gkv)
    return (dq_c, dkv_c, dkr, dkrr,
            mm_tn(qn, dqc), mm_tn(qn, dqs), mm_tn(kvn, dk), mm_tn(kvn, dv), dgq, dgkv)


def rwkv_prep_bwd_tile(step0, tile0, r0, k0, v0, l0, drt, dat, dbt, dkt, dvv, dlw, dyb, hr, hk, hv, hl,
                       mu_r, mu_k, mu_v, mu_l, w0, a0, k_k, k_a, w_dec, w_iclr, tril, same, bd, r_k,
                       cr, ck, cv, cl_):
    f = rwkv_prep_core(tile0, r0, k0, v0, l0, hr, hk, hv, hl, mu_r, mu_k, mu_v, mu_l, w0, a0, k_k, k_a,
                       w_dec, w_iclr, tril, same, bd)
    ur, uk, uv, ul, kk, k2, a_ic, sg, th = (f[n] for n in ("ur", "uk", "uv", "ul", "kk", "k2", "a_ic", "sg", "th"))
    lc, lw = f["lc"], f["lw"]
    e_neg = jnp.exp(-lc)
    dur = drt * jnp.exp(lc)
    da = dat * jnp.exp(lc - lw)
    db = dbt * e_neg
    dk2 = dkt * e_neg
    s = head_sum(ur * k2 * r_k, bd)
    duv = dvv + dyb * s
    ds = head_sum(dyb * uv, bd)
    dur = dur + ds * k2 * r_k
    dk2 = dk2 + ds * ur * r_k
    dr_k = colsum(ds * ur * k2)
    dkk = db * a_ic - da
    da_ic = db * kk + dk2 * uk * k_a
    duk = dk2 * (1.0 + (a_ic - 1.0) * k_a)
    dk_a = colsum(dk2 * uk * (a_ic - 1.0))
    dkkraw = jnp.where(f["nrm_raw"] > 1e-12, (dkk - kk * head_sum(dkk * kk, bd)) / f["nrm"], dkk * 1e12)
    duk = duk + dkkraw * k_k
    dk_k = colsum(dkkraw * uk)
    dai = da_ic * a_ic * (1.0 - a_ic)
    dd = dlw * (-DECAY_SCALE) * sg * (1.0 - sg)
    dul = mm_nt(dai, w_iclr) + mm_nt(dd, w_dec) * (1.0 - th * th)

    def unshift(du, x, prev, mu, carry_row):
        nxt = shift_rows_up(du, carry_row)
        return du * (1.0 - mu) + nxt * mu, colsum(du * (prev - x)), du[0:1, :]

    dr0, dmu_r, ncr = unshift(dur, r0, f["pr"], mu_r, cr)
    dk0, dmu_k, nck = unshift(duk, k0, f["pk"], mu_k, ck)
    dv0, dmu_v, ncv = unshift(duv, v0, f["pv"], mu_v, cv)
    dl0, dmu_l, ncl = unshift(dul, l0, f["pl"], mu_l, cl_)
    return (dr0, dk0, dv0, dl0,
            dmu_r, dmu_k, dmu_v, dmu_l, colsum(dd), colsum(dai), dk_k, dk_a, dr_k, mm_tn(th, dd), mm_tn(ul, dai),
            ncr, nck, ncv, ncl)


def in_backward(x, dz, pieces, mod, w_in_pt, unrot):
    n_rows = x.shape[0]
    ts = ROW_TILE
    n_p = len(pieces)
    shard_cols = IN_WIDTH // N_DEV

    def body(*refs):
        x_ref, dz_ref = refs[:2]
        p_refs = refs[2:2 + n_p]
        mod_ref, w_ref, unrot_ref = refs[2 + n_p:5 + n_p]
        dx_ref, ht_ref, blocks_ref, dshift_ref, dscale_ref = refs[5 + n_p:]
        step0 = pl.program_id(0) == 0
        dma, dmb, dr0, dk0, dv0, dgpa, dgpb, dq_c, dkv_c, dkr, dkrr, dl0 = (r[...] for r in p_refs)
        dproj = jnp.concatenate([dma, dmb, dr0, dk0, dv0, dgpa, dgpb, dq_c, dkv_c, dkr, dkrr, dl0], axis=1)
        dh = mm(dproj, w_ref[...])
        xhat, rstd = layer_norm_stats(x_ref[...])
        scale1 = 1.0 + mod_ref[1:2, :]
        dx_ref[...] = layer_norm_bwd(dh * scale1, xhat, rstd) + ALPHA * dz_ref[...]
        ht_ref[...] = jnp.transpose(xhat * scale1 + mod_ref[0:1, :]).astype(BF16)
        dkrope = (dkr.astype(F32) + mm(dkrr, unrot_ref[...]))[:, NOPE:QK_DIM]
        natural = jnp.concatenate(
            [dq_c.astype(F32), dkv_c.astype(F32), dkrope]
            + [p.astype(F32) for p in (dgpa, dr0, dk0, dv0, dl0, dgpb, dma, dmb)], axis=1)
        for j in range(N_DEV):
            blocks_ref[j] = natural[:, j * shard_cols:(j + 1) * shard_cols].astype(BF16)
        for ref, val in ((dshift_ref, colsum(dh)), (dscale_ref, colsum(dh * xhat))):
            @pl.when(step0)
            def _(ref=ref, val=val):
                ref[...] = val

            @pl.when(jnp.logical_not(step0))
            def _(ref=ref, val=val):
                ref[...] += val

    row = lambda w: pl.BlockSpec((ts, w), lambda i: (i, 0))
    const = pl.BlockSpec(memory_space=pltpu.VMEM)
    vec = pl.BlockSpec((1, D_MODEL), lambda i: (0, 0))
    return pl.pallas_call(
        body, name="in_backward", grid=(n_rows // ts,),
        in_specs=[row(D_MODEL), row(D_MODEL)] + [row(p.shape[1]) for p in pieces] + [const] * 3,
        out_specs=[row(D_MODEL), pl.BlockSpec((D_MODEL, ts), lambda i: (0, i)),
                   pl.BlockSpec((N_DEV, ts, shard_cols), lambda i: (0, i, 0)), vec, vec],
        out_shape=[jax.ShapeDtypeStruct((n_rows, D_MODEL), F32), jax.ShapeDtypeStruct((D_MODEL, n_rows), BF16),
                   jax.ShapeDtypeStruct((N_DEV, n_rows, shard_cols), BF16),
                   jax.ShapeDtypeStruct((1, D_MODEL), F32), jax.ShapeDtypeStruct((1, D_MODEL), F32)],
        compiler_params=pltpu.CompilerParams(dimension_semantics=("arbitrary",), vmem_limit_bytes=VMEM_LIMIT),
    )(x, dz, *pieces, mod, w_in_pt, unrot)


def in_weight_grad_exchange(h_t, dp_blocks, others, small, order):
    n = len(others)
    n_rows = h_t.shape[1]
    ts = 4 * ROW_TILE
    n_i = n_rows // ts
    shard_cols = dp_blocks.shape[2]
    n_chips = N_DEV // 2
    last = N_DEV - 1

    def body(order_ref, h_ref, dp_ref, *rest):
        g_refs, s_ref = rest[:n], rest[n]
        rwin_ref, rg_refs, rs_ref = rest[n + 1], rest[n + 2:2 * n + 2], rest[2 * n + 2]
        (acc, sendbuf, sib_buf, sib_send, sib_recv, win_send, win_recv,
         o_send, o_recv, local_sems) = rest[2 * n + 3:]
        b, i = pl.program_id(0), pl.program_id(1)
        me = my_position()
        mi = flat_index(me)
        sibling = (me[0], me[1], 1 - me[2])

        def other_copies(k, src_index, dst_index):
            peer = flip(me, k)
            out = [pltpu.make_async_remote_copy(
                src_ref=g_refs[a].at[src_index], dst_ref=rg_refs[a].at[dst_index],
                send_sem=o_send.at[(n + 1) * (k - 1) + a], recv_sem=o_recv.at[(n + 1) * (k - 1) + a],
                device_id=peer, device_id_type=MESH_IDS) for a in range(n)]
            out.append(pltpu.make_async_remote_copy(
                src_ref=s_ref, dst_ref=rs_ref.at[dst_index],
                send_sem=o_send.at[(n + 1) * (k - 1) + n], recv_sem=o_recv.at[(n + 1) * (k - 1) + n],
                device_id=peer, device_id_type=MESH_IDS))
            return out

        def local_copies():
            out = [pltpu.make_async_copy(g_refs[a].at[mi], rg_refs[a].at[mi], local_sems.at[a]) for a in range(n)]
            out.append(pltpu.make_async_copy(s_ref, rs_ref.at[mi], local_sems.at[n]))
            return out

        def to_sibling(t):
            return pltpu.make_async_remote_copy(
                src_ref=sendbuf.at[t], dst_ref=sib_buf.at[t], send_sem=sib_send.at[t], recv_sem=sib_recv.at[t],
                device_id=sibling, device_id_type=MESH_IDS)

        def to_owner(t):
            flip_x = (t < 2) * 1
            flip_y = 1 - (t & 1)
            owner = (me[0] ^ flip_x, me[1] ^ flip_y, me[2])
            return pltpu.make_async_remote_copy(
                src_ref=sendbuf.at[n_chips + t], dst_ref=rwin_ref.at[t], send_sem=win_send.at[t],
                recv_sem=win_recv.at[t], device_id=owner, device_id_type=MESH_IDS)

        own_block = pltpu.make_async_copy(sendbuf.at[last], rwin_ref.at[n_chips - 1], local_sems.at[n + 1])

        @pl.when(jnp.logical_and(b == 0, i == 0))
        def _():
            for cp in local_copies():
                cp.start()
            for k in range(1, N_DEV):
                for cp in other_copies(k, flat_index(flip(me, k)), mi):
                    cp.start()

        contrib = jnp.dot(h_ref[...], dp_ref[...], preferred_element_type=F32)

        @pl.when(i == 0)
        def _():
            acc[...] = contrib

        @pl.when(i > 0)
        def _():
            acc[...] += contrib

        slot = order_ref[N_DEV + b]
        t = slot & (n_chips - 1)

        @pl.when(jnp.logical_and(i == n_i - 1, slot < n_chips))
        def _():
            sendbuf[slot] = acc[...].astype(BF16)
            to_sibling(t).start()

        @pl.when(jnp.logical_and(i == n_i - 1, slot >= n_chips))
        def _():
            to_sibling(t).wait_recv()
            sendbuf[slot] = (acc[...] + sib_buf[t].astype(F32)).astype(BF16)

            @pl.when(slot < last)
            def _():
                to_owner(t).start()

            @pl.when(slot == last)
            def _():
                own_block.start()

        @pl.when(jnp.logical_and(b == last, i == n_i - 1))
        def _():
            for t in range(n_chips - 1):
                to_owner(t).wait_recv()
            for k in range(1, N_DEV):
                pi = flat_index(flip(me, k))
                for cp in other_copies(k, pi, pi):
                    cp.wait_recv()
            for t in range(n_chips):
                to_sibling(t).wait_send()
            for t in range(n_chips - 1):
                to_owner(t).wait_send()
            for k in range(1, N_DEV):
                for cp in other_copies(k, flat_index(flip(me, k)), mi):
                    cp.wait_send()
            for cp in local_copies():
                cp.wait()
            own_block.wait()

    hbm = pl.BlockSpec(memory_space=pl.ANY)
    n_sem = 7 * (n + 1)
    grid_spec = pltpu.PrefetchScalarGridSpec(
        num_scalar_prefetch=1, grid=(N_DEV, n_i),
        in_specs=[pl.BlockSpec((D_MODEL, ts), lambda b, i, order: (0, i)),
                  pl.BlockSpec((None, ts, shard_cols), lambda b, i, order: (order[b], i, 0))] + [hbm] * (n + 1),
        out_specs=[hbm] * (n + 2),
        scratch_shapes=[pltpu.VMEM((D_MODEL, shard_cols), F32), pltpu.VMEM((N_DEV, D_MODEL, shard_cols), BF16),
                        pltpu.VMEM((n_chips, D_MODEL, shard_cols), BF16),
                        pltpu.SemaphoreType.DMA((n_chips,)), pltpu.SemaphoreType.DMA((n_chips,)),
                        pltpu.SemaphoreType.DMA((n_chips - 1,)), pltpu.SemaphoreType.DMA((n_chips - 1,)),
                        pltpu.SemaphoreType.DMA((n_sem,)), pltpu.SemaphoreType.DMA((n_sem,)),
                        pltpu.SemaphoreType.DMA((n + 2,))])
    return pl.pallas_call(
        body, name="in_weight_grad_exchange", grid_spec=grid_spec,
        out_shape=[jax.ShapeDtypeStruct((n_chips, D_MODEL, shard_cols), BF16)]
        + [jax.ShapeDtypeStruct(o.shape, o.dtype) for o in others]
        + [jax.ShapeDtypeStruct((N_DEV,) + small.shape, small.dtype)],
        compiler_params=pltpu.CompilerParams(dimension_semantics=("arbitrary", "arbitrary"),
                                             vmem_limit_bytes=VMEM_LIMIT),
    )(order, h_t, dp_blocks, *others, small)


def ada_weight_grad(c_all, dmod_cols):
    def body(c_ref, d_ref, o_ref):
        cv = c_ref[...]
        o_ref[...] = hdot_tn(cv * sigmoid(cv), d_ref[...])

    return pl.pallas_call(
        body, name="ada_weight_grad",
        out_shape=jax.ShapeDtypeStruct((c_all.shape[1], dmod_cols.shape[1]), F32),
    )(c_all, dmod_cols)


def adamw_update(g, w, m, v):
    nm = ADAM_B1 * m + (1.0 - ADAM_B1) * g
    nv = ADAM_B2 * v + (1.0 - ADAM_B2) * (g * g)
    m_hat = nm / (1.0 - ADAM_B1 ** ADAM_STEP)
    v_hat = nv / (1.0 - ADAM_B2 ** ADAM_STEP)
    return -ADAM_LR * (m_hat / (jnp.sqrt(v_hat) + ADAM_EPS) + ADAM_WD * w), nm, nv


def adamw(parts, w, m, v, name):
    k, rows, cols = parts.shape
    rb = 128 if rows % 128 == 0 else rows

    def body(p_ref, w_ref, m_ref, v_ref, g_ref, d_ref, nm_ref, nv_ref):
        g = p_ref[0].astype(F32)
        for i in range(1, k):
            g = g + p_ref[i].astype(F32)
        g_ref[0] = g
        d_ref[0], nm_ref[0], nv_ref[0] = adamw_update(g, w_ref[0], m_ref[0], v_ref[0])

    blk = pl.BlockSpec((1, rb, cols), lambda i: (0, i, 0))
    return pl.pallas_call(
        body, name=name, grid=(rows // rb,),
        in_specs=[pl.BlockSpec((k, rb, cols), lambda i: (0, i, 0)), blk, blk, blk],
        out_specs=[blk] * 4, out_shape=[jax.ShapeDtypeStruct((1, rows, cols), F32)] * 4,
        compiler_params=pltpu.CompilerParams(dimension_semantics=("arbitrary",), vmem_limit_bytes=VMEM_LIMIT),
    )(parts, w, m, v)


def adamw_small(parts, ws, ms, vs):
    k = parts.shape[0]
    n = len(ws)
    sizes = [w.shape[1] for w in ws]

    def body(p_ref, *refs):
        ins, outs = refs[:3 * n], refs[3 * n:]
        g_all = p_ref[0]
        for i in range(1, k):
            g_all = g_all + p_ref[i]
        off = 0
        for a, size in enumerate(sizes):
            g = g_all[:, off:off + size]
            off += size
            d, nm, nv = adamw_update(g, ins[a][...], ins[n + a][...], ins[2 * n + a][...])
            for kind, val in enumerate((g, d, nm, nv)):
                outs[kind * n + a][...] = val

    return pl.pallas_call(
        body, name="adamw_small",
        out_shape=[jax.ShapeDtypeStruct((1, size), F32) for _ in range(4) for size in sizes],
    )(parts, *ws, *ms, *vs)


def columns_from_shards(g, rows, cols):
    return g.reshape(N_DEV, rows, cols).transpose(1, 0, 2).reshape(rows, N_DEV * cols)


def permute_w_in_t(wt):
    z = lambda n: jnp.zeros((n, D_MODEL), wt.dtype)
    krope = wt[N_KROPE:N_KROPE + ROPE]
    krope_rot = jnp.concatenate([-krope[ROPE // 2:], krope[:ROPE // 2]], axis=0)
    rw = N_RWKV
    return jnp.concatenate([
        wt[N_MA:N_MA + 1024], wt[N_MB:N_MB + 1024],
        wt[rw:rw + 512], wt[rw + 512:rw + 1024], wt[rw + 1024:rw + 1536],
        wt[N_GPA:N_GPA + 512], wt[N_GPB:N_GPB + 512],
        wt[N_QC:N_QC + 256], wt[N_KVC:N_KVC + 128],
        z(NOPE), krope, z(LANE - QK_DIM), z(NOPE), krope_rot, z(LANE - QK_DIM),
        wt[rw + 1536:rw + 1664]], axis=0)


def pad_heads_q(w_uq):
    w = w_uq.reshape(Q_RANK, HEADS, QK_DIM)
    zpad = jnp.zeros((Q_RANK, HEADS, LANE - QK_DIM), w.dtype)
    wq = jnp.concatenate([w, zpad], axis=2).reshape(Q_RANK, HEADS * LANE)
    pe = w[:, :, NOPE:]
    rot = jnp.concatenate([-pe[:, :, ROPE // 2:], pe[:, :, :ROPE // 2]], axis=2)
    wqr = jnp.concatenate([jnp.zeros((Q_RANK, HEADS, NOPE), w.dtype), rot, zpad], axis=2).reshape(Q_RANK, HEADS * LANE)
    return wq, wqr


def unpad_heads_q_grad(dwq, dwqr):
    a = dwq.reshape(Q_RANK, HEADS, LANE)
    r = dwqr.reshape(Q_RANK, HEADS, LANE)[:, :, NOPE:QK_DIM]
    pe = a[:, :, NOPE:QK_DIM] + jnp.concatenate([r[:, :, ROPE // 2:], -r[:, :, :ROPE // 2]], axis=2)
    return jnp.concatenate([a[:, :, :NOPE], pe], axis=2).reshape(Q_RANK, HEADS * QK_DIM)


def pad_heads_kv(w_ukv):
    w = w_ukv.reshape(KV_RANK, HEADS, 2 * HEAD)
    z = jnp.zeros((KV_RANK, HEADS, HEAD), w.dtype)
    wkn = jnp.concatenate([w[:, :, :NOPE], z], axis=2).reshape(KV_RANK, HEADS * LANE)
    val = w[:, :, NOPE:]
    odd = (jnp.arange(HEADS) % 2 == 1)[None, :, None]
    wv = jnp.concatenate([jnp.where(odd, 0, val), jnp.where(odd, val, 0)], axis=2).reshape(KV_RANK, HEADS * LANE)
    return wkn, wv


def unpad_heads_kv_grad(dwkn, dwv):
    a = dwkn.reshape(KV_RANK, HEADS, LANE)[:, :, :NOPE]
    b = dwv.reshape(KV_RANK, HEADS, LANE)
    odd = (jnp.arange(HEADS) % 2 == 1)[None, :, None]
    val = jnp.where(odd, b[:, :, HEAD:], b[:, :, :HEAD])
    return jnp.concatenate([a, val], axis=2).reshape(KV_RANK, HEADS * 2 * HEAD)


def kernel(x, c, positions, w_ada, b_ada, w_in, q_norm_g, w_uq, kv_norm_g, w_ukv, mu_rwkv, w0, w_decay_up, a0, w_iclr_up, k_k, k_a, r_k, gn_g, gn_b, w_proj_a, w_proj_b, w_out, post_g, post_b, loss_target, m_w_ada, m_b_ada, m_w_in, m_q_norm_g, m_w_uq, m_kv_norm_g, m_w_ukv, m_mu_rwkv, m_w0, m_w_decay_up, m_a0, m_w_iclr_up, m_k_k, m_k_a, m_r_k, m_gn_g, m_gn_b, m_w_proj_a, m_w_proj_b, m_w_out, m_post_g, m_post_b, v_w_ada, v_b_ada, v_w_in, v_q_norm_g, v_w_uq, v_kv_norm_g, v_w_ukv, v_mu_rwkv, v_w0, v_w_decay_up, v_a0, v_w_iclr_up, v_k_k, v_k_a, v_r_k, v_gn_g, v_gn_b, v_w_proj_a, v_w_proj_b, v_w_out, v_post_g, v_post_b):
    weights = dict(w_ada=w_ada, b_ada=b_ada, w_in=w_in, q_norm_g=q_norm_g, w_uq=w_uq, kv_norm_g=kv_norm_g,
                   w_ukv=w_ukv, mu_rwkv=mu_rwkv, w0=w0, w_decay_up=w_decay_up, a0=a0, w_iclr_up=w_iclr_up,
                   k_k=k_k, k_a=k_a, r_k=r_k, gn_g=gn_g, gn_b=gn_b, w_proj_a=w_proj_a, w_proj_b=w_proj_b,
                   w_out=w_out, post_g=post_g, post_b=post_b)
    mom1 = dict(w_ada=m_w_ada, b_ada=m_b_ada, w_in=m_w_in, q_norm_g=m_q_norm_g, w_uq=m_w_uq, kv_norm_g=m_kv_norm_g,
                w_ukv=m_w_ukv, mu_rwkv=m_mu_rwkv, w0=m_w0, w_decay_up=m_w_decay_up, a0=m_a0, w_iclr_up=m_w_iclr_up,
                k_k=m_k_k, k_a=m_k_a, r_k=m_r_k, gn_g=m_gn_g, gn_b=m_gn_b, w_proj_a=m_w_proj_a, w_proj_b=m_w_proj_b,
                w_out=m_w_out, post_g=m_post_g, post_b=m_post_b)
    mom2 = dict(w_ada=v_w_ada, b_ada=v_b_ada, w_in=v_w_in, q_norm_g=v_q_norm_g, w_uq=v_w_uq, kv_norm_g=v_kv_norm_g,
                w_ukv=v_w_ukv, mu_rwkv=v_mu_rwkv, w0=v_w0, w_decay_up=v_w_decay_up, a0=v_a0, w_iclr_up=v_w_iclr_up,
                k_k=v_k_k, k_a=v_k_a, r_k=v_r_k, gn_g=v_gn_g, gn_b=v_gn_b, w_proj_a=v_w_proj_a, w_proj_b=v_w_proj_b,
                w_out=v_w_out, post_g=v_post_g, post_b=v_post_b)
    names = list(weights)
    n_rows = x.shape[1]
    me = 4 * lax.axis_index("x") + 2 * lax.axis_index("y") + lax.axis_index("c")
    xr = x[0]
    tgt = loss_target[0]
    row = lambda a: a.reshape(1, -1)

    w_in_all, c_all = gather_shards([w_in[0].T.astype(BF16), c])
    c_all = c_all.reshape(N_DEV, D_MODEL)
    w_in_pt = permute_w_in_t(w_in_all.reshape(IN_WIDTH, D_MODEL))

    mod_all = ada_modulation(c_all, w_ada[0], b_ada.reshape(N_DEV, -1))
    mod = lax.dynamic_index_in_dim(mod_all, me, axis=1, keepdims=False).reshape(3, D_MODEL)

    proj, *gathered = fwd_in_gather(xr, mod, w_in_pt, [weights[n][0].astype(BF16) for n, _, _ in SHARDED[1:]])
    pcol = lambda off_, w: (proj, w, off_ // w)
    full = {}
    for (n, r, cdim), part in zip(SHARDED[1:], gathered):
        full[n] = part.reshape(N_DEV * r, cdim) if n == "w_out" else columns_from_shards(part, r, cdim)
    wq, wqr = pad_heads_q(full["w_uq"])
    wkn, wv = pad_heads_kv(full["w_ukv"])
    zl = jnp.zeros((LORA, WIDTH), BF16)
    w_dec = jnp.concatenate([full["w_decay_up"], zl], axis=0)
    w_iclr = jnp.concatenate([zl, full["w_iclr_up"]], axis=0)
    wpa, wpb, wout = full["w_proj_a"], full["w_proj_b"], full["w_out"]

    inv_freq = ROPE_THETA ** (-jnp.arange(0, ROPE, 2, dtype=F32) / ROPE)
    ang = positions[0].astype(F32)[:, None] * inv_freq
    ones_n, zeros_n, zeros_p = jnp.ones((n_rows, NOPE), F32), jnp.zeros((n_rows, NOPE), F32), jnp.zeros((n_rows, LANE - QK_DIM), F32)
    cos_t = jnp.concatenate([ones_n, jnp.cos(ang), jnp.cos(ang), zeros_p], axis=1)
    sin_t = jnp.concatenate([zeros_n, jnp.sin(ang), jnp.sin(ang), zeros_p], axis=1)

    gq, gkv = q_norm_g, kv_norm_g
    mla_consts = [gq, gkv, wq, wqr, wkn, wv]
    q, k, v = row_call(
        "mla_prep", mla_prep_tile, n_rows,
        [pcol(P_QC, 256), pcol(P_KVC, 128), pcol(P_KR, 128), pcol(P_KRR, 128), (cos_t, LANE, 0), (sin_t, LANE, 0)],
        mla_consts, [(HEADS * LANE, BF16)] * 3, tile_rows=PREP_TILE)
    ya, lse = attention_forward(q, k, v)

    def chunk_sum_matrices(n):
        t_idx = jnp.arange(n)
        same_chunk = (t_idx[:, None] // CHUNK) == (t_idx[None, :] // CHUNK)
        return (same_chunk & (t_idx[:, None] >= t_idx[None, :])).astype(F32), same_chunk.astype(F32)

    l_idx = jnp.arange(LANE)
    bd = ((l_idx[:, None] // HEAD) == (l_idx[None, :] // HEAD)).astype(F32)
    mu = mu_rwkv
    mu_r, mu_k, mu_v, mu_l = mu[:, 0:512], mu[:, 512:1024], mu[:, 1024:1536], mu[:, 1536:1664]
    rk_row = row(r_k)
    rwkv_consts = lambda n: [mu_r, mu_k, mu_v, mu_l, w0, a0, k_k, k_a, w_dec, w_iclr, *chunk_sum_matrices(n), bd]
    rwkv_rows = [pcol(P_R, 512), pcol(P_K, 512), pcol(P_V, 512), pcol(P_LORA, 128)]
    rt, at, bt, kt, clf, uv, ur, k2 = row_call(
        "rwkv_prep", rwkv_prep_tile, n_rows, rwkv_rows, rwkv_consts(ROW_TILE), [(WIDTH, F32)] * 8, halo_in=rwkv_rows)
    y, m0s, state_maps, out_maps, *wkv_saved = wkv_forward(at, bt, kt, rt, uv, clf)

    tail = row_call(
        "tail", tail_tile, n_rows,
        [(xr, D_MODEL, 0), (tgt, D_MODEL, 0), pcol(P_MA, 1024), pcol(P_MB, 1024), pcol(P_GPA, 512), pcol(P_GPB, 512),
         (ya, WIDTH, 0), (y, WIDTH, 0), (ur, WIDTH, 0), (k2, WIDTH, 0), (uv, WIDTH, 0)],
        [mod, wpa, wpb, wout, gn_g, gn_b, rk_row, post_g, post_b, bd],
        [(D_MODEL, F32), (1024, BF16), (1024, BF16), (512, BF16), (512, BF16), (WIDTH, F32), (WIDTH, F32), (WIDTH, F32)],
        acc_out=[((1, LANE), F32), ((D_MODEL, D_MODEL), F32), ((WIDTH, D_MODEL), F32), ((WIDTH, D_MODEL), F32),
                 ((1, WIDTH), F32), ((1, WIDTH), F32), ((1, D_MODEL), F32), ((1, D_MODEL), F32), ((1, D_MODEL), F32)])
    (dz, dma, dmb, dgpa, dgpb, dya, dy, dyb,
     loss_row, g_wout, g_wpa, g_wpb, g_gn_g, g_gn_b, g_post_g, g_post_b, dgate) = tail

    def owner_blocks(g, n):
        r, cdim = next((r, cdim) for name, r, cdim in SHARDED if name == n)
        return (g.reshape(N_DEV, r, cdim) if n == "w_out" else g.reshape(r, N_DEV, cdim).transpose(1, 0, 2)).astype(BF16)

    early = ("w_proj_a", "w_proj_b", "w_out")
    dq, dk, dv, *got_early = attention_backward(
        q, k, v, ya, dya, lse, [owner_blocks(g, n) for g, n in zip((g_wpa, g_wpb, g_wout), early)])
    dq_c, dkv_c, dkr, dkrr, g_wq, g_wqr, g_wkn, g_wv, g_gq, g_gkv = row_call(
        "mla_prep_bwd", mla_prep_bwd_tile, n_rows,
        [pcol(P_QC, 256), pcol(P_KVC, 128), (cos_t, LANE, 0), (sin_t, LANE, 0),
         (dq, HEADS * LANE, 0), (dk, HEADS * LANE, 0), (dv, HEADS * LANE, 0)],
        mla_consts, [(256, BF16), (128, BF16), (128, BF16), (128, BF16)],
        acc_out=[((Q_RANK, HEADS * LANE), F32)] * 2 + [((KV_RANK, HEADS * LANE), F32)] * 2
        + [((1, Q_RANK), F32), ((1, KV_RANK), F32)], tile_rows=PREP_TILE)

    dat, dbt, dkt, drt, dvv, dlw = wkv_backward(at, bt, kt, rt, uv, clf, m0s, state_maps, out_maps, wkv_saved, dy)
    (dr0, dk0, dv0, dl0, g_mu_r, g_mu_k, g_mu_v, g_mu_l, g_w0, g_a0, g_k_k, g_k_a, g_r_k, g_wdec, g_wiclr) = row_call(
        "rwkv_prep_bwd", rwkv_prep_bwd_tile, n_rows,
        rwkv_rows + [(drt, WIDTH, 0), (dat, WIDTH, 0), (dbt, WIDTH, 0), (dkt, WIDTH, 0), (dvv, WIDTH, 0),
                     (dlw, WIDTH, 0), (dyb, WIDTH, 0)],
        rwkv_consts(PREP_TILE) + [rk_row], [(512, BF16), (512, BF16), (512, BF16), (128, BF16)],
        acc_out=[((1, 512), F32)] * 3 + [((1, 128), F32)] + [((1, 512), F32)] * 5 + [((LANE, WIDTH), F32)] * 2,
        halo_in=rwkv_rows, carry=[512, 512, 512, 128], reverse=True, tile_rows=PREP_TILE)

    li = jnp.arange(LANE)
    src, dst = li[:, None], li[None, :]
    half = ROPE // 2
    unrot = (jnp.where((dst >= NOPE) & (dst < NOPE + half) & (src == dst + half), 1.0, 0.0)
             - jnp.where((dst >= NOPE + half) & (dst < QK_DIM) & (src == dst - half), 1.0, 0.0)).astype(BF16)
    dx, h_t, dproj_blocks, dshift, dscale = in_backward(
        xr, dz, [dma, dmb, dr0, dk0, dv0, dgpa, dgpb, dq_c, dkv_c, dkr, dkrr, dl0], mod, w_in_pt, unrot)

    late = ("w_uq", "w_ukv", "w_decay_up", "w_iclr_up")
    late_grads = (unpad_heads_q_grad(g_wq, g_wqr), unpad_heads_kv_grad(g_wkn, g_wv), g_wdec[:LORA], g_wiclr[LORA:])
    blocks = [owner_blocks(g, n) for g, n in zip(late_grads, late)]
    dmod = jnp.concatenate([dshift, dscale, dgate], axis=1)
    small = jnp.concatenate([dmod, g_gq, g_gkv, g_mu_r, g_mu_k, g_mu_v, g_mu_l, g_w0, g_a0, g_k_k, g_k_a, g_r_k,
                             g_gn_g, g_gn_b, g_post_g, g_post_b, loss_row], axis=1)
    my_x, my_y, my_c = lax.axis_index("x"), lax.axis_index("y"), lax.axis_index("c")
    chip_order = [4 * (my_x ^ fx) + 2 * (my_y ^ fy) for fx, fy in ((1, 1), (1, 0), (0, 1), (0, 0))]
    owners = [chip_order[s % 4] + (my_c if s >= 4 else 1 - my_c) for s in WGRAD_SLOTS]
    order = jnp.stack(owners + [jnp.int32(s) for s in WGRAD_SLOTS]).astype(jnp.int32)
    got_w_in, *got_late, got_small = in_weight_grad_exchange(h_t, dproj_blocks, blocks, small, order)
    got = {"w_in": got_w_in, **dict(zip(late, got_late)), **dict(zip(early, got_early))}
    loss = jnp.sum(got_small[:, 0, SMALL_ELEMS])

    ada_cols = w_ada.shape[2]
    dmod_all = got_small[:, 0, :3 * D_MODEL]
    g_ada = ada_weight_grad(c_all, lax.dynamic_slice_in_dim(dmod_all, me * ada_cols, ada_cols, axis=1))

    outs = [dict() for _ in range(4)]
    res = adamw(g_ada[None], w_ada, m_w_ada, v_w_ada, "adamw_w_ada")
    for kind in range(4):
        outs[kind]["w_ada"] = res[kind]
    for n, _, _ in SHARDED:
        res = adamw(got[n], weights[n], mom1[n], mom2[n], "adamw_" + n)
        for kind in range(4):
            outs[kind][n] = res[kind]
    rows_of = lambda tree: [tree[n].reshape(1, -1) for n, _ in SMALL]
    res = adamw_small(got_small, rows_of(weights), rows_of(mom1), rows_of(mom2))
    for kind in range(4):
        for a, (n, _) in enumerate(SMALL):
            outs[kind][n] = res[kind * len(SMALL) + a].reshape(weights[n].shape)
    return (loss, dx[None], *[outs[0][n] for n in names], *[outs[1][n] for n in names],
            *[outs[2][n] for n in names], *[outs[3][n] for n in names])
```

```python
import functools
import math

import jax
import jax.numpy as jnp
from jax import lax
from jax.experimental import pallas as pl
from jax.experimental.pallas import tpu as pltpu

F32 = jnp.float32
BF16 = jnp.bfloat16
HIGHEST = lax.Precision.HIGHEST
MESH_IDS = pl.DeviceIdType.MESH

N_DEV = 8
D_MODEL = 1024
LN_EPS = 1e-5
RMS_EPS = 1e-6
GN_EPS = 64e-5
HEADS = 8
Q_RANK = 256
KV_RANK = 128
ROPE = 32
NOPE = 64
QK_DIM = NOPE + ROPE
WIDTH = 512
HEAD = 64
LORA = 64
CHUNK = 64
DEPTH = 1
ALPHA = (2.0 * DEPTH) ** 0.25
ROPE_THETA = 10000.0
ATTN_SCALE = QK_DIM ** -0.5
DECAY_SCALE = math.exp(-0.5)

ADAM_LR = 0.001
ADAM_B1 = 0.9
ADAM_B2 = 0.999
ADAM_EPS = 1e-08
ADAM_WD = 0.01
ADAM_STEP = 10

LANE = 128
PAIR = 2 * HEAD
ROW_TILE = 256
PREP_TILE = 512
HALO_ROWS = 16
ATTN_FWD_TILES = (512, 1024)
ATTN_BWD_TILES = (512, 512)
LOG2_E = math.log2(math.e)
Q_PRESCALE = ATTN_SCALE * LOG2_E
WKV_CHUNKS_PER_STEP = 16
WGRAD_SLOTS = (0, 1, 4, 2, 5, 6, 3, 7)
VMEM_LIMIT = 56 * 1024 * 1024

P_MA, P_MB, P_R, P_K, P_V, P_GPA, P_GPB, P_QC, P_KVC, P_KR, P_KRR, P_LORA = (
    0, 1024, 2048, 2560, 3072, 3584, 4096, 4608, 4864, 4992, 5120, 5248)
P_WIDTH = 5376

N_QC, N_KVC, N_KROPE, N_GPA, N_RWKV, N_GPB, N_MA, N_MB = 0, 256, 384, 416, 928, 2592, 3104, 4128
IN_WIDTH = 5152

SHARDED = (("w_in", 1024, 644), ("w_uq", 256, 96), ("w_ukv", 128, 128), ("w_decay_up", 64, 64),
           ("w_iclr_up", 64, 64), ("w_proj_a", 512, 128), ("w_proj_b", 512, 128), ("w_out", 128, 1024))
SMALL = (("b_ada", 3072), ("q_norm_g", 256), ("kv_norm_g", 128), ("mu_rwkv", 1664), ("w0", 512), ("a0", 512),
         ("k_k", 512), ("k_a", 512), ("r_k", 512), ("gn_g", 512), ("gn_b", 512), ("post_g", 1024), ("post_b", 1024))
SMALL_ELEMS = sum(n for _, n in SMALL)


def mm(a, b):
    return jnp.dot(a.astype(BF16), b.astype(BF16), preferred_element_type=F32)


def mm_nt(a, b):
    return lax.dot_general(a.astype(BF16), b.astype(BF16), (((1,), (1,)), ((), ())), preferred_element_type=F32)


def mm_tn(a, b):
    return lax.dot_general(a.astype(BF16), b.astype(BF16), (((0,), (0,)), ((), ())), preferred_element_type=F32)


def hdot(a, b):
    return jnp.dot(a, b, precision=HIGHEST, preferred_element_type=F32)


def hdot_tn(a, b):
    return lax.dot_general(a, b, (((0,), (0,)), ((), ())), precision=HIGHEST, preferred_element_type=F32)


def sigmoid(x):
    return 1.0 / (1.0 + jnp.exp(-x))


def colsum(x):
    return jnp.sum(x, axis=0, keepdims=True)


def rowmean(x):
    return jnp.mean(x, axis=-1, keepdims=True)


def layer_norm_stats(x):
    xc = x - rowmean(x)
    rstd = lax.rsqrt(rowmean(xc * xc) + LN_EPS)
    return xc * rstd, rstd


def layer_norm_bwd(dy, xhat, rstd):
    return rstd * (dy - rowmean(dy) - xhat * rowmean(dy * xhat))


def bf16_pieces(x, n):
    pieces = []
    for _ in range(n):
        p = x.astype(BF16)
        pieces.append(p)
        x = x - p.astype(F32)
    return pieces


def ones_dot(ones, x, n_pieces):
    ones = ones.astype(BF16)
    return sum(jnp.dot(ones, p, preferred_element_type=F32) for p in bf16_pieces(x, n_pieces))


def ones_dot_nt(ones, x, n_pieces):
    ones = ones.astype(BF16)
    return sum(lax.dot_general(ones, p, (((1,), (1,)), ((), ())), preferred_element_type=F32)
               for p in bf16_pieces(x, n_pieces))


def head_sum(x, bd):
    return jnp.concatenate([mm(x[:, p * LANE:(p + 1) * LANE], bd) for p in range(x.shape[1] // LANE)], axis=1)


def tile_lanes(t, n):
    return jnp.concatenate([t] * n, axis=1)


def row_iota(shape):
    return lax.broadcasted_iota(jnp.int32, shape, 0)


def lane_iota(shape):
    return lax.broadcasted_iota(jnp.int32, shape, 1)


def shift_rows_down(x, row0):
    rolled = pltpu.roll(x, 1, axis=0)
    return jnp.where(row_iota(x.shape) == 0, row0, rolled)


def shift_rows_up(x, row_last):
    rolled = pltpu.roll(x, x.shape[0] - 1, axis=0)
    return jnp.where(row_iota(x.shape) == x.shape[0] - 1, row_last, rolled)


def row_call(name, fn, n_rows, row_in, const_in, row_out, acc_out=(), halo_in=(), carry=(), reverse=False,
             tile_rows=ROW_TILE):
    ts = tile_rows
    n_tiles = n_rows // ts
    n_in = len(row_in) + len(halo_in) + len(const_in)
    n_ro, n_ao = len(row_out), len(acc_out)

    def tile_of(g):
        return (n_tiles - 1 - g) if reverse else g

    def body(*refs):
        ins = refs[:n_in]
        ro = refs[n_in:n_in + n_ro]
        ao = refs[n_in + n_ro:n_in + n_ro + n_ao]
        cr = refs[n_in + n_ro + n_ao:]
        g = pl.program_id(0)
        step0 = g == 0
        tile0 = tile_of(g) == 0
        for r in cr:
            @pl.when(step0)
            def _(r=r):
                r[...] = jnp.zeros_like(r)
        n_tiled = len(row_in) + len(halo_in)
        vals = [r[...].astype(F32) for r in ins[:n_tiled]] + [r[...] for r in ins[n_tiled:]]
        outs = fn(step0, tile0, *vals, *[c[0:1, :] for c in cr])
        for r, v in zip(ro, outs[:n_ro]):
            r[...] = v.astype(r.dtype)
        for r, v in zip(ao, outs[n_ro:n_ro + n_ao]):
            @pl.when(step0)
            def _(r=r, v=v):
                r[...] = v.astype(r.dtype)

            @pl.when(jnp.logical_not(step0))
            def _(r=r, v=v):
                r[...] += v.astype(r.dtype)
        for r, v in zip(cr, outs[n_ro + n_ao:]):
            r[0:1, :] = v

    in_specs = [pl.BlockSpec((ts, w), functools.partial(lambda g, cb: (tile_of(g), cb), cb=cb)) for _, w, cb in row_in]
    in_specs += [pl.BlockSpec((HALO_ROWS, w), functools.partial(
        lambda g, cb: (jnp.maximum(tile_of(g) * (ts // HALO_ROWS) - 1, 0), cb), cb=cb)) for _, w, cb in halo_in]
    in_specs += [pl.BlockSpec(memory_space=pltpu.VMEM) for _ in const_in]
    out_specs = [pl.BlockSpec((ts, w), lambda g: (tile_of(g), 0)) for w, _ in row_out]
    out_specs += [pl.BlockSpec(s, lambda g: (0, 0)) for s, _ in acc_out]
    out_shape = [jax.ShapeDtypeStruct((n_rows, w), d) for w, d in row_out]
    out_shape += [jax.ShapeDtypeStruct(s, d) for s, d in acc_out]
    return pl.pallas_call(
        body, name=name, grid=(n_tiles,), in_specs=in_specs, out_specs=out_specs, out_shape=out_shape,
        scratch_shapes=[pltpu.VMEM((8, w), F32) for w in carry],
        compiler_params=pltpu.CompilerParams(dimension_semantics=("arbitrary",), vmem_limit_bytes=VMEM_LIMIT),
    )(*[a for a, _, _ in row_in], *[a for a, _, _ in halo_in], *const_in)


def my_position():
    return lax.axis_index("x"), lax.axis_index("y"), lax.axis_index("c")


def flip(pos, k):
    x, y, c = pos
    dx, dy, dc = (k >> 2) & 1, (k >> 1) & 1, k & 1
    return (1 - x if dx else x, 1 - y if dy else y, 1 - c if dc else c)


def flat_index(pos):
    return 4 * pos[0] + 2 * pos[1] + pos[2]


def gather_shards(shards):
    n = len(shards)

    def body(*refs):
        x_refs, out_refs = refs[:n], refs[n:2 * n]
        send_sems, recv_sems, local_sems = refs[2 * n:]
        x, y, c = my_position()
        me, sibling = (x, y, c), (x, y, 1 - c)
        chips = [(1 - x, y), (x, 1 - y), (1 - x, 1 - y)]

        def copy(a, k, block, to, from_input=False):
            slot = out_refs[a].at[flat_index(block)]
            return pltpu.make_async_remote_copy(
                src_ref=x_refs[a] if from_input else slot, dst_ref=slot,
                send_sem=send_sems.at[7 * a + k], recv_sem=recv_sems.at[7 * a + k],
                device_id=to, device_id_type=MESH_IDS)

        mine = [pltpu.make_async_copy(x_refs[a], out_refs[a].at[flat_index(me)], local_sems.at[a]) for a in range(n)]
        for cp in mine:
            cp.start()
        first = []
        for a in range(n):
            first.append(copy(a, 0, me, sibling, from_input=True))
            first += [copy(a, 1 + j, me, (*chip, c), from_input=True) for j, chip in enumerate(chips)]
        for cp in first:
            cp.start()
        passed = []
        for j, chip in enumerate(chips):
            for a in range(n):
                copy(a, 1 + j, (*chip, c), me).wait_recv()
                cp = copy(a, 4 + j, (*chip, c), sibling)
                cp.start()
                passed.append(cp)
        for a in range(n):
            copy(a, 0, sibling, me).wait_recv()
            for j, chip in enumerate(chips):
                copy(a, 4 + j, (*chip, 1 - c), me).wait_recv()
        for cp in first + passed:
            cp.wait_send()
        for cp in mine:
            cp.wait()

    return pl.pallas_call(
        body, name="gather_shards",
        out_shape=[jax.ShapeDtypeStruct((N_DEV,) + s.shape, s.dtype) for s in shards],
        in_specs=[pl.BlockSpec(memory_space=pl.ANY)] * n, out_specs=[pl.BlockSpec(memory_space=pl.ANY)] * n,
        scratch_shapes=[pltpu.SemaphoreType.DMA((7 * n,)), pltpu.SemaphoreType.DMA((7 * n,)),
                        pltpu.SemaphoreType.DMA((n,))],
    )(*shards)


def ada_modulation(c_all, w_ada_loc, b_ada_blocks):
    cols = w_ada_loc.shape[1]

    def body(c_ref, w_ref, b_ref, out_ref, send_sems, recv_sems):
        me = my_position()
        mi = flat_index(me)
        cv = c_ref[...]
        res = hdot(cv * sigmoid(cv), w_ref[...]) + b_ref[pl.ds(mi, 1), :]
        out_ref[mi] = res
        sends = []
        for k in range(1, N_DEV):
            cp = pltpu.make_async_remote_copy(
                src_ref=out_ref.at[mi], dst_ref=out_ref.at[mi], send_sem=send_sems.at[k - 1],
                recv_sem=recv_sems.at[k - 1], device_id=flip(me, k), device_id_type=MESH_IDS)
            cp.start()
            sends.append(cp)
        for k in range(1, N_DEV):
            pi = flat_index(flip(me, k))
            pltpu.make_async_remote_copy(
                src_ref=out_ref.at[pi], dst_ref=out_ref.at[pi], send_sem=send_sems.at[k - 1],
                recv_sem=recv_sems.at[k - 1], device_id=flip(me, k), device_id_type=MESH_IDS).wait_recv()
        for cp in sends:
            cp.wait_send()

    return pl.pallas_call(
        body, name="ada_modulation",
        out_shape=jax.ShapeDtypeStruct((N_DEV, N_DEV, cols), F32),
        in_specs=[pl.BlockSpec(memory_space=pltpu.VMEM)] * 3, out_specs=pl.BlockSpec(memory_space=pltpu.VMEM),
        scratch_shapes=[pltpu.SemaphoreType.DMA((7,)), pltpu.SemaphoreType.DMA((7,))],
    )(c_all, w_ada_loc, b_ada_blocks)


def fwd_in_tile(step0, tile0, x, mod, w_in_ptt):
    xhat, _ = layer_norm_stats(x)
    h = xhat * (1.0 + mod[1:2]) + mod[0:1]
    return (mm_nt(h, w_in_ptt),)


def fwd_in_gather(x, mod, w_in_pt, shards):
    n = len(shards)
    n_rows = x.shape[0]
    ts = ROW_TILE
    n_tiles = n_rows // ts

    def body(x_ref, mod_ref, w_ref, *rest):
        s_refs = rest[:n]
        proj_ref, out_refs = rest[n], rest[n + 1:2 * n + 1]
        send_sems, recv_sems, local_sems = rest[2 * n + 1:]
        g = pl.program_id(0)
        me = my_position()
        mi = flat_index(me)

        def copies(k, slot):
            return [pltpu.make_async_remote_copy(
                src_ref=s_refs[a], dst_ref=out_refs[a].at[slot], send_sem=send_sems.at[7 * a + k - 1],
                recv_sem=recv_sems.at[7 * a + k - 1], device_id=flip(me, k), device_id_type=MESH_IDS)
                for a in range(n)]

        local = [pltpu.make_async_copy(s_refs[a], out_refs[a].at[mi], local_sems.at[a]) for a in range(n)]

        @pl.when(g == 0)
        def _():
            for cp in local:
                cp.start()
            for k in range(1, N_DEV):
                for cp in copies(k, mi):
                    cp.start()

        proj_ref[...] = fwd_in_tile(None, None, x_ref[...], mod_ref[...], w_ref[...])[0].astype(BF16)

        @pl.when(g == n_tiles - 1)
        def _():
            for k in range(1, N_DEV):
                for cp in copies(k, flat_index(flip(me, k))):
                    cp.wait_recv()
            for k in range(1, N_DEV):
                for cp in copies(k, mi):
                    cp.wait_send()
            for cp in local:
                cp.wait()

    hbm = pl.BlockSpec(memory_space=pl.ANY)
    const = pl.BlockSpec(memory_space=pltpu.VMEM)
    return pl.pallas_call(
        body, name="fwd_in_gather", grid=(n_tiles,),
        in_specs=[pl.BlockSpec((ts, D_MODEL), lambda g: (g, 0)), const, const] + [hbm] * n,
        out_specs=[pl.BlockSpec((ts, P_WIDTH), lambda g: (g, 0))] + [hbm] * n,
        out_shape=[jax.ShapeDtypeStruct((n_rows, P_WIDTH), BF16)]
        + [jax.ShapeDtypeStruct((N_DEV,) + s.shape, s.dtype) for s in shards],
        scratch_shapes=[pltpu.SemaphoreType.DMA((7 * n,)), pltpu.SemaphoreType.DMA((7 * n,)),
                        pltpu.SemaphoreType.DMA((n,))],
        compiler_params=pltpu.CompilerParams(dimension_semantics=("arbitrary",), vmem_limit_bytes=VMEM_LIMIT),
    )(x, mod, w_in_pt, *shards)


def rms_norm_fwd(x, g):
    r = lax.rsqrt(rowmean(x * x) + RMS_EPS)
    xh = x * r
    return xh * g, xh, r


def key_rope_mask(shape):
    return (lane_iota(shape) >= NOPE).astype(F32)


def mla_prep_tile(step0, tile0, q_c, kv_c, kr, krr, cos, sin, gq, gkv, wq, wqr, wkn, wv):
    qn, _, _ = rms_norm_fwd(q_c, gq)
    kvn, _, _ = rms_norm_fwd(kv_c, gkv)
    q = (mm(qn, wq) * tile_lanes(cos, HEADS) + mm(qn, wqr) * tile_lanes(sin, HEADS)) * Q_PRESCALE
    kpe = kr * (cos * key_rope_mask(cos.shape)) + krr * sin
    k = mm(kvn, wkn) + tile_lanes(kpe, HEADS)
    v = mm(kvn, wv)
    return q, k, v


def rwkv_prep_core(tile0, r0, k0, v0, l0, hr, hk, hv, hl, mu_r, mu_k, mu_v, mu_l, w0, a0, k_k, k_a,
                   w_dec, w_iclr, tril, same, bd):
    def shifted(x, halo, mu):
        row0 = jnp.where(tile0, 0.0, halo[HALO_ROWS - 1:HALO_ROWS, :])
        prev = shift_rows_down(x, row0)
        return x + (prev - x) * mu, prev

    ur, pr = shifted(r0, hr, mu_r)
    uk, pk = shifted(k0, hk, mu_k)
    uv, pv = shifted(v0, hv, mu_v)
    ul, plo = shifted(l0, hl, mu_l)
    th = jnp.tanh(ul)
    sg = sigmoid(w0 + mm(th, w_dec))
    lw = -DECAY_SCALE * sg
    a_ic = sigmoid(a0 + mm(ul, w_iclr))
    kkraw = uk * k_k
    nrm_raw = jnp.sqrt(head_sum(kkraw * kkraw, bd))
    nrm = jnp.maximum(nrm_raw, 1e-12)
    kk = kkraw / nrm
    k2 = uk * (1.0 + (a_ic - 1.0) * k_a)
    lc = ones_dot(tril, lw, 3)
    lcl = ones_dot(same, lw, 3)
    return dict(ur=ur, uk=uk, uv=uv, ul=ul, pr=pr, pk=pk, pv=pv, pl=plo, th=th, sg=sg, lw=lw, a_ic=a_ic,
                kkraw=kkraw, nrm_raw=nrm_raw, nrm=nrm, kk=kk, k2=k2, lc=lc, lcl=lcl)


def rwkv_prep_tile(step0, tile0, r0, k0, v0, l0, hr, hk, hv, hl, *consts):
    f = rwkv_prep_core(tile0, r0, k0, v0, l0, hr, hk, hv, hl, *consts)
    lc, lw = f["lc"], f["lw"]
    e_neg = jnp.exp(-lc)
    rt = f["ur"] * jnp.exp(lc)
    at = -f["kk"] * jnp.exp(lc - lw)
    bt = f["kk"] * f["a_ic"] * e_neg
    kt = f["k2"] * e_neg
    return rt, at, bt, kt, jnp.exp(f["lcl"]), f["uv"], f["ur"], f["k2"]


def wkv_masks():
    lane = lane_iota((1, PAIR))
    m_lo = (lane < HEAD).astype(F32)
    r2 = row_iota((PAIR, PAIR))
    c2 = lane_iota((PAIR, PAIR))
    bd = ((r2 < HEAD) == (c2 < HEAD)).astype(F32)
    eye2 = (r2 == c2).astype(F32)
    eye = (row_iota((CHUNK, CHUNK)) == lane_iota((CHUNK, CHUNK))).astype(F32)
    t_idx = row_iota((4 * CHUNK, PAIR)) % CHUNK
    s_idx = lane_iota((4 * CHUNK, PAIR)) % CHUNK
    keep = s_idx < t_idx + (row_iota((4 * CHUNK, PAIR)) >= 2 * CHUNK).astype(jnp.int32)
    return (m_lo, 1.0 - m_lo), keep, eye, bd, eye2


def rows(*parts):
    return jnp.concatenate(parts, axis=0)


def lanes(*parts):
    return jnp.concatenate(parts, axis=1)


def head_rows(x, ms):
    return rows(x * ms[0], x * ms[1])


def wkv_score_stack(at, rt, ms):
    return rows(head_rows(at, ms), head_rows(rt, ms))


def wkv_chunks_pre(chunks, masks):
    ms, keep, eye, bd, eye2 = masks
    n = len(chunks)
    at, bt, kt, rt, v, cl = (list(t) for t in zip(*chunks))
    scores = [jnp.where(keep, mm_nt(wkv_score_stack(a, r, ms), rows(b, k)), 0.0)
              for a, r, b, k in zip(at, rt, bt, kt)]
    q = CHUNK
    aab = [s[h * q:(h + 1) * q, :q] for s in scores for h in range(2)]
    tinv = [eye + a for a in aab]
    power = [mm(a, a) for a in aab]
    for _ in range(5):
        both = [mm(rows(t, p), p) for t, p in zip(tinv, power)]
        tinv = [t + x[:q] for t, x in zip(tinv, both)]
        power = [x[q:] for x in both]
    pair = lambda c, row0, col0: lanes(scores[c][row0:row0 + q, col0:col0 + q],
                                       scores[c][row0 + q:row0 + 2 * q, col0:col0 + q])
    tinv_p = [lanes(tinv[2 * c], tinv[2 * c + 1]) for c in range(n)]
    aak_p = [pair(c, 0, q) for c in range(n)]
    prb_p = [pair(c, 2 * q, 0) for c in range(n)]
    prk_p = [pair(c, 2 * q, q) for c in range(n)]
    v_rows = [head_rows(x, ms) for x in v]
    wy = [mm(rows(a, p), x) for a, p, x in zip(aak_p, prk_p, v_rows)]
    w = [x[:q] for x in wy]
    yh2 = [x[q:] for x in wy]
    aw = [mm(t, lanes(head_rows(a, ms), head_rows(w_, ms))) for t, a, w_ in zip(tinv_p, at, w)]
    ah = [x[:, :PAIR] for x in aw]
    wh = [x[:, PAIR:] for x in aw]
    ry = [mm(p, lanes(head_rows(a, ms), head_rows(w_, ms))) for p, a, w_ in zip(prb_p, ah, wh)]
    rh = [r + x[:, :PAIR] for r, x in zip(rt, ry)]
    yh = [x[:, PAIR:] + y for x, y in zip(ry, yh2)]
    bc = [b * c_ for b, c_ in zip(bt, cl)]
    kc = [k * c_ for k, c_ in zip(kt, cl)]
    gh = [mm_tn(b, lanes(a, w_)) for b, a, w_ in zip(bc, ah, wh)]
    g = [eye2 * c_ + bd * x[:, :PAIR] for c_, x in zip(cl, gh)]
    h = [bd * (x[:, PAIR:] + mm_tn(k, v_)) for x, k, v_ in zip(gh, kc, v)]
    as_bf16 = lambda xs: [x.astype(BF16) for x in xs]
    saved = (as_bf16(tinv_p), as_bf16(aak_p), as_bf16(prb_p), as_bf16(prk_p), as_bf16(ah), wh)
    return g, h, rh, yh, saved


def wkv_chunks_grad(chunks, saved, m0, dy, dm1, masks):
    ms, keep, eye, bd, eye2 = masks
    n = len(chunks)
    q = CHUNK
    at, bt, kt, rt, v, cl = (list(t) for t in zip(*chunks))
    tinv_p, aak_p, prb_p, prk_p, ah, wh = (list(t) for t in zip(*saved))
    head_stack = lambda p: rows(p[:, :q], p[:, q:])
    bc = [b * c_ for b, c_ in zip(bt, cl)]
    kc = [k * c_ for k, c_ in zip(kt, cl)]
    u = [mm(a, m) + w for a, m, w in zip(ah, m0, wh)]
    dm1 = [d * bd for d in dm1]
    from_state = [mm(rows(b, k), d) for b, k, d in zip(bc, kc, dm1)]
    dy_rows = [head_rows(d, ms) for d in dy]
    from_out = [mm_tn(lanes(head_stack(pb), head_stack(pk)), d) for pb, pk, d in zip(prb_p, prk_p, dy_rows)]
    du = [a[:q] + b[:q] for a, b in zip(from_state, from_out)]
    dv = [a[q:] + b[q:] for a, b in zip(from_state, from_out)]
    dz = [mm_tn(head_stack(t), head_rows(d, ms)) for t, d in zip(tinv_p, du)]
    dz_rows = [head_rows(d, ms) for d in dz]
    dv = [a + mm_tn(head_stack(k), d) for a, k, d in zip(dv, aak_p, dz_rows)]
    by_m0 = [mm_nt(rows(d, z), m) for d, z, m in zip(dy, dz, m0)]
    uv = [rows(x, y) for x, y in zip(u, v)]
    by_dm1 = [mm_nt(x, d) for x, d in zip(uv, dm1)]
    udm = [x[:q] for x in by_dm1]
    vdm = [x[q:] for x in by_dm1]
    dscores = [jnp.where(keep, mm_nt(rows(z, d), x), 0.0) for z, d, x in zip(dz_rows, dy_rows, uv)]
    to_ar = [mm(d, rows(b, k)) for d, b, k in zip(dscores, bt, kt)]
    to_bk = [mm_tn(d, wkv_score_stack(a, r, ms)) for d, a, r in zip(dscores, at, rt)]
    ones = jnp.ones((8, PAIR), F32)
    upper = (lane_iota((CHUNK, CHUNK)) >= row_iota((CHUNK, CHUNK))).astype(F32)
    out = []
    for c in range(n):
        e = to_ar[c]
        dat_c = by_m0[c][q:] + e[:q] * ms[0] + e[q:2 * q] * ms[1]
        drt_c = by_m0[c][:q] + e[2 * q:3 * q] * ms[0] + e[3 * q:] * ms[1]
        dbt_c = udm[c] * cl[c] + to_bk[c][:q]
        dkt_c = vdm[c] * cl[c] + to_bk[c][q:]
        dlcl = ones_dot_nt(ones, dm1[c] * m0[c], 3)[0:1, :] * cl[c] + colsum(bc[c] * udm[c] + kc[c] * vdm[c])
        g = drt_c * rt[c] - dbt_c * bt[c] - dkt_c * kt[c] + dat_c * at[c]
        dlw = ones_dot(upper, g, 3) - dat_c * at[c] + dlcl
        out.append((dat_c, dbt_c, dkt_c, drt_c, dv[c], dlw))
    return out


def wkv_forward(at, bt, kt, rt, v, clf):
    n_rows = at.shape[0]
    cps = WKV_CHUNKS_PER_STEP
    rb = cps * CHUNK
    n_steps = n_rows // rb

    def body(a_ref, b_ref, k_ref, r_ref, v_ref, c_ref, y_ref, m0_ref, g_ref, rh_ref, *rest):
        saved_refs, m_scr = rest[:6], rest[6]

        @pl.when(pl.program_id(1) == 0)
        def _():
            m_scr[...] = jnp.zeros_like(m_scr)

        masks = wkv_masks()
        chunks = []
        for cc in range(cps):
            sl = slice(cc * CHUNK, (cc + 1) * CHUNK)
            chunks.append((a_ref[sl, :], b_ref[sl, :], k_ref[sl, :], r_ref[sl, :], v_ref[sl, :],
                           c_ref[cc * CHUNK:cc * CHUNK + 1, :]))
        gs, hs, rhs, yhs, saved = wkv_chunks_pre(chunks, masks)
        for ref, per_chunk in zip(saved_refs, saved):
            for cc, val in enumerate(per_chunk):
                ref[cc * CHUNK:(cc + 1) * CHUNK, :] = val
        m = m_scr[...]
        for cc, (g, h, rh, yh) in enumerate(zip(gs, hs, rhs, yhs)):
            sl = slice(cc * CHUNK, (cc + 1) * CHUNK)
            m0_ref[0, cc] = m
            g_ref[0, cc] = g
            rh_ref[sl, :] = rh
            y_ref[sl, :] = hdot(rh, m) + yh
            m = hdot(g, m) + h
        m_scr[...] = m

    blk = pl.BlockSpec((rb, PAIR), lambda p, s: (s, p))
    state_blk = pl.BlockSpec((1, cps, PAIR, PAIR), lambda p, s: (p, s, 0, 0))
    state_shape = jax.ShapeDtypeStruct((WIDTH // PAIR, n_rows // CHUNK, PAIR, PAIR), F32)
    rows_f32 = jax.ShapeDtypeStruct((n_rows, WIDTH), F32)
    rows_bf16 = jax.ShapeDtypeStruct((n_rows, WIDTH), BF16)
    return pl.pallas_call(
        body, name="wkv_forward", grid=(WIDTH // PAIR, n_steps),
        in_specs=[blk] * 6,
        out_specs=[blk, state_blk, state_blk, blk] + [blk] * 6,
        out_shape=[rows_f32, state_shape, state_shape, rows_f32] + [rows_bf16] * 5 + [rows_f32],
        scratch_shapes=[pltpu.VMEM((PAIR, PAIR), F32)],
        compiler_params=pltpu.CompilerParams(dimension_semantics=("arbitrary", "arbitrary"),
                                             vmem_limit_bytes=VMEM_LIMIT),
    )(at, bt, kt, rt, v, clf)


def wkv_backward(at, bt, kt, rt, v, clf, m0s, gs, rh, saved, dy):
    n_rows = at.shape[0]
    cps = WKV_CHUNKS_PER_STEP
    rb = cps * CHUNK
    n_steps = n_rows // rb

    def body(a_ref, b_ref, k_ref, r_ref, v_ref, c_ref, m0_ref, g_ref, rh_ref, *rest):
        saved_refs, dy_ref = rest[:6], rest[6]
        da_ref, db_ref, dk_ref, dr_ref, dv_ref, dlw_ref, dm_scr = rest[7:]

        @pl.when(pl.program_id(1) == 0)
        def _():
            dm_scr[...] = jnp.zeros_like(dm_scr)

        masks = wkv_masks()
        bd = masks[3]
        dm = dm_scr[...]
        dm1 = [None] * cps
        for cc in reversed(range(cps)):
            sl = slice(cc * CHUNK, (cc + 1) * CHUNK)
            dm1[cc] = dm
            dm = bd * (hdot_tn(g_ref[0, cc], dm) + hdot_tn(rh_ref[sl, :], dy_ref[sl, :]))
        dm_scr[...] = dm
        chunks, kept, m0, dys = [], [], [], []
        for cc in range(cps):
            sl = slice(cc * CHUNK, (cc + 1) * CHUNK)
            chunks.append((a_ref[sl, :], b_ref[sl, :], k_ref[sl, :], r_ref[sl, :], v_ref[sl, :],
                           c_ref[cc * CHUNK:cc * CHUNK + 1, :]))
            kept.append(tuple(ref[sl, :] for ref in saved_refs))
            m0.append(m0_ref[0, cc])
            dys.append(dy_ref[sl, :])
        grads = wkv_chunks_grad(chunks, kept, m0, dys, dm1, masks)
        for cc, (dat, dbt, dkt, drt, dv, dlw) in enumerate(grads):
            sl = slice(cc * CHUNK, (cc + 1) * CHUNK)
            da_ref[sl, :] = dat
            db_ref[sl, :] = dbt
            dk_ref[sl, :] = dkt
            dr_ref[sl, :] = drt
            dv_ref[sl, :] = dv
            dlw_ref[sl, :] = dlw

    blk = pl.BlockSpec((rb, PAIR), lambda p, s: (n_steps - 1 - s, p))
    state_blk = pl.BlockSpec((1, cps, PAIR, PAIR), lambda p, s: (p, n_steps - 1 - s, 0, 0))
    return pl.pallas_call(
        body, name="wkv_backward", grid=(WIDTH // PAIR, n_steps),
        in_specs=[blk] * 6 + [state_blk, state_blk, blk] + [blk] * 6 + [blk],
        out_specs=[blk] * 6,
        out_shape=[jax.ShapeDtypeStruct((n_rows, WIDTH), F32)] * 6,
        scratch_shapes=[pltpu.VMEM((PAIR, PAIR), F32)],
        compiler_params=pltpu.CompilerParams(dimension_semantics=("arbitrary", "arbitrary"),
                                             vmem_limit_bytes=VMEM_LIMIT),
    )(at, bt, kt, rt, v, clf, m0s, gs, rh, *saved, dy)


def visible(q_row0, k_row0, shape):
    qc = (q_row0 + row_iota(shape)) // CHUNK
    kc = (k_row0 + lane_iota(shape)) // CHUNK
    return kc <= qc


def attention_forward(q, k, v):
    n_rows = q.shape[0]
    tq, tk = ATTN_FWD_TILES
    n_q = n_rows // tq
    assert tk % tq == 0

    def body(q_ref, k_ref, v_ref, o_ref, lse_ref):
        i = pl.program_id(1)
        lane = lane_iota((tq, LANE))
        heads = [slice(0, LANE), slice(LANE, 2 * LANE)]
        qs = [q_ref[:, cols] for cols in heads]

        def step(j, carry, size, masked):
            rows = pl.ds(pl.multiple_of(j * size, size), size)
            ss = [mm_nt(qh, k_ref[rows, cols]) for qh, cols in zip(qs, heads)]
            if masked:
                vis = visible(i * tq, j * size, ss[0].shape)
                ss = [jnp.where(vis, s, -jnp.inf) for s in ss]
            ps, stats = [], []
            for s, (m, l, _) in zip(ss, carry):
                m_new = jnp.maximum(m, jnp.max(s, axis=-1, keepdims=True))
                p = jnp.exp2(s - m_new)
                alpha = jnp.exp2(m - m_new)
                ps.append(p)
                stats.append((m_new, alpha, alpha * l + jnp.sum(p, axis=-1, keepdims=True)))
            pvs = [mm(p, v_ref[rows, cols]) for p, cols in zip(ps, heads)]
            return tuple((m_new, l, alpha * acc + pv)
                         for (m_new, alpha, l), (_, _, acc), pv in zip(stats, carry, pvs))

        carry = tuple((jnp.full((tq, 1), -jnp.inf, F32), jnp.zeros((tq, 1), F32), jnp.zeros((tq, LANE), F32))
                      for _ in heads)
        n_full = (i * tq) // tk
        carry = lax.fori_loop(0, n_full, functools.partial(step, size=tk, masked=False), carry)
        (m0, l0, acc0), (m1, l1, acc1) = step(n_full, carry, size=tk, masked=True)
        o_ref[...] = acc0 / l0 + acc1 / l1
        lse_ref[...] = jnp.where(lane >= HEAD, m1 + jnp.log2(l1), m0 + jnp.log2(l0))

    return pl.pallas_call(
        body, name="attention_forward", grid=(HEADS // 2, n_q),
        in_specs=[pl.BlockSpec((tq, 2 * LANE), lambda p, i: (i, p)),
                  pl.BlockSpec((n_rows, 2 * LANE), lambda p, i: (0, p)),
                  pl.BlockSpec((n_rows, 2 * LANE), lambda p, i: (0, p))],
        out_specs=[pl.BlockSpec((tq, LANE), lambda p, i: (i, p))] * 2,
        out_shape=[jax.ShapeDtypeStruct((n_rows, WIDTH), F32)] * 2,
        compiler_params=pltpu.CompilerParams(dimension_semantics=("arbitrary", "arbitrary"),
                                             vmem_limit_bytes=VMEM_LIMIT),
    )(q, k, v)


def block_exchange(g_refs, rg_refs, send_sems, recv_sems, local_sems):
    n = len(g_refs)
    me = my_position()
    mi = flat_index(me)

    def copies(k, src_index, dst_index):
        return [pltpu.make_async_remote_copy(
            src_ref=g_refs[a].at[src_index], dst_ref=rg_refs[a].at[dst_index],
            send_sem=send_sems.at[7 * a + k - 1], recv_sem=recv_sems.at[7 * a + k - 1],
            device_id=flip(me, k), device_id_type=MESH_IDS) for a in range(n)]

    local = [pltpu.make_async_copy(g_refs[a].at[mi], rg_refs[a].at[mi], local_sems.at[a]) for a in range(n)]

    def start():
        for cp in local:
            cp.start()
        for k in range(1, N_DEV):
            for cp in copies(k, flat_index(flip(me, k)), mi):
                cp.start()

    def wait():
        for k in range(1, N_DEV):
            pi = flat_index(flip(me, k))
            for cp in copies(k, pi, pi):
                cp.wait_recv()
        for k in range(1, N_DEV):
            for cp in copies(k, flat_index(flip(me, k)), mi):
                cp.wait_send()
        for cp in local:
            cp.wait()

    return start, wait


def attention_backward(q, k, v, o, do, lse, riders):
    n_rows = q.shape[0]
    tq, tk = ATTN_BWD_TILES
    n_q = n_rows // tq
    n_k = n_rows // tk
    n_masked = max(1, tk // tq)
    n_r = len(riders)

    def body(q_ref, k_ref, v_ref, o_ref, do_ref, lse_ref, *rest):
        g_refs = rest[:n_r]
        dq_ref, dk_ref, dv_ref = rest[n_r:n_r + 3]
        rg_refs = rest[n_r + 3:2 * n_r + 3]
        start_riders, wait_riders = block_exchange(g_refs, rg_refs, *rest[2 * n_r + 3:])
        j = pl.program_id(1)

        @pl.when(jnp.logical_and(pl.program_id(0) == 0, j == 0))
        def _():
            start_riders()

        @pl.when(j == 0)
        def _():
            dq_ref[...] = jnp.zeros_like(dq_ref)

        lane = lane_iota((tq, LANE))
        heads = [slice(0, LANE), slice(LANE, 2 * LANE)]
        ks = [k_ref[:, cols] for cols in heads]
        vs = [v_ref[:, cols] for cols in heads]
        head_lanes = [(lane < HEAD).astype(F32), (lane >= HEAD).astype(F32)]

        def step(i, carry, masked):
            rows = pl.ds(pl.multiple_of(i * tq, tq), tq)
            qs = [q_ref[rows, cols] for cols in heads]
            dout = do_ref[rows, :]
            dout_o = dout * o_ref[rows, :]
            lse_t = lse_ref[rows, :]
            ss = [mm_nt(qh, kh) for qh, kh in zip(qs, ks)]
            dps = [mm_nt(dout, vh) for vh in vs]
            ps, dss = [], []
            for hh in range(2):
                delta = jnp.sum(dout_o * head_lanes[hh], axis=-1, keepdims=True)
                lse_h = jnp.sum(jnp.where(lane == hh * HEAD, lse_t, 0.0), axis=-1, keepdims=True)
                p = jnp.exp2(ss[hh] - lse_h)
                if masked:
                    p = jnp.where(visible(i * tq, j * tk, p.shape), p, 0.0)
                ps.append(p)
                dss.append(p * (dps[hh] - delta))
            dvs = [mm_tn(p, dout) for p in ps]
            dqs = [mm(ds, kh) for ds, kh in zip(dss, ks)]
            dks = [mm_tn(ds, qh) for ds, qh in zip(dss, qs)]
            for cols, dq in zip(heads, dqs):
                dq_ref[rows, cols] += dq * ATTN_SCALE
            return tuple((dk + a, dv + b) for (dk, dv), a, b in zip(carry, dks, dvs))

        carry = tuple((jnp.zeros((tk, LANE), F32), jnp.zeros((tk, LANE), F32)) for _ in heads)
        i_first = (j * tk) // tq
        for extra in range(n_masked):
            carry = step(i_first + extra, carry, masked=True)
        carry = lax.fori_loop(i_first + n_masked, n_q, functools.partial(step, masked=False), carry)
        for cols, (dk, dv) in zip(heads, carry):
            dk_ref[:, cols] = dk * (1.0 / LOG2_E)
            dv_ref[:, cols] = dv

        @pl.when(jnp.logical_and(pl.program_id(0) == HEADS // 2 - 1, j == n_k - 1))
        def _():
            wait_riders()

    full = lambda w: pl.BlockSpec((n_rows, w), lambda p, j: (0, p))
    blk = pl.BlockSpec((tk, 2 * LANE), lambda p, j: (j, p))
    hbm = pl.BlockSpec(memory_space=pl.ANY)
    return pl.pallas_call(
        body, name="attention_backward", grid=(HEADS // 2, n_k),
        in_specs=[full(2 * LANE), blk, blk, full(LANE), full(LANE), full(LANE)] + [hbm] * n_r,
        out_specs=[full(2 * LANE), blk, blk] + [hbm] * n_r,
        out_shape=[jax.ShapeDtypeStruct((n_rows, HEADS * LANE), F32)] * 3
        + [jax.ShapeDtypeStruct(r.shape, r.dtype) for r in riders],
        scratch_shapes=[pltpu.SemaphoreType.DMA((7 * n_r,)), pltpu.SemaphoreType.DMA((7 * n_r,)),
                        pltpu.SemaphoreType.DMA((n_r,))],
        compiler_params=pltpu.CompilerParams(dimension_semantics=("arbitrary", "arbitrary"),
                                             vmem_limit_bytes=VMEM_LIMIT),
    )(q, k, v, o, do, lse, *riders)


def tail_tile(step0, tile0, x, tgt, ma, mb, gpa, gpb, ya, y, ur, k2, uv,
              mod, wpa, wpb, wout, gn_g, gn_b, r_k, post_g, post_b, bd):
    gate = mod[2:3]
    inv = 1.0 / HEAD
    yc = y - head_sum(y, bd) * inv
    rs = lax.rsqrt(head_sum(yc * yc, bd) * inv + GN_EPS)
    yn = yc * rs
    yb = yn * gn_g + gn_b + head_sum(ur * k2 * r_k, bd) * uv
    sga, sgb = sigmoid(gpa), sigmoid(gpb)
    sila, silb = gpa * sga, gpb * sgb
    ga, gb = ya * sila, yb * silb
    pa, pb = mm(ga, wpa), mm(gb, wpb)
    sa, sb = sigmoid(ma), sigmoid(mb)
    merged = sa * pa + sb * pb
    sub = mm(merged, wout)
    z = ALPHA * x + (1.0 + gate) * sub
    zhat, rstd = layer_norm_stats(z)
    err = zhat * post_g + post_b - tgt
    loss = 0.5 * jnp.sum(rowmean(err * err), axis=0, keepdims=True) + jnp.zeros((1, LANE), F32)
    dout = err * (1.0 / D_MODEL)
    dpost_g = colsum(dout * zhat)
    dpost_b = colsum(dout)
    dz = layer_norm_bwd(dout * post_g, zhat, rstd)
    dgate = colsum(dz * sub)
    dsub = dz * (1.0 + gate)
    dwout = mm_tn(merged, dsub)
    dmerged = mm_nt(dsub, wout)
    dpa, dpb = dmerged * sa, dmerged * sb
    dma = dmerged * pa * sa * (1.0 - sa)
    dmb = dmerged * pb * sb * (1.0 - sb)
    dwpa = mm_tn(ga, dpa)
    dwpb = mm_tn(gb, dpb)
    dga = mm_nt(dpa, wpa)
    dgb = mm_nt(dpb, wpb)
    dya = dga * sila
    dgpa = dga * ya * (sga * (1.0 + gpa * (1.0 - sga)))
    dyb = dgb * silb
    dgpb = dgb * yb * (sgb * (1.0 + gpb * (1.0 - sgb)))
    dgn_g = colsum(dyb * yn)
    dgn_b = colsum(dyb)
    dyn = dyb * gn_g
    dy = rs * (dyn - head_sum(dyn, bd) * inv - yn * head_sum(dyn * yn, bd) * inv)
    return (dz, dma, dmb, dgpa, dgpb, dya, dy, dyb,
            loss, dwout, dwpa, dwpb, dgn_g, dgn_b, dpost_g, dpost_b, dgate)


def mla_prep_bwd_tile(step0, tile0, q_c, kv_c, cos, sin, dq, dk, dv, gq, gkv, wq, wqr, wkn, wv):
    qn, qh, rq = rms_norm_fwd(q_c, gq)
    kvn, kvh, rkv = rms_norm_fwd(kv_c, gkv)
    dqc = dq * tile_lanes(cos, HEADS)
    dqs = dq * tile_lanes(sin, HEADS)
    dqn = mm_nt(dqc, wq) + mm_nt(dqs, wqr)
    dkvn = mm_nt(dk, wkn) + mm_nt(dv, wv)
    dkpe = dk[:, 0:LANE]
    for h in range(1, HEADS):
        dkpe = dkpe + dk[:, h * LANE:(h + 1) * LANE]
    dkr = dkpe * (cos * key_rope_mask(cos.shape))
    dkrr = dkpe * sin

    def rms_bwd(dyv, xh, r, g):
        dyg = dyv * g
        return r * (dyg - xh * rowmean(dyg * xh)), colsum(dyv * xh)

    dq_c, dgq = rms_bwd(dqn, qh, rq, gq)
    dkv_c, dgkv = rms_bwd(dkvn, kvh, rkv, gkv)
    return (dq_c, dkv_c, dkr, dkrr,
            mm_tn(qn, dqc), mm_tn(qn, dqs), mm_tn(kvn, dk), mm_tn(kvn, dv), dgq, dgkv)


def rwkv_prep_bwd_tile(step0, tile0, r0, k0, v0, l0, drt, dat, dbt, dkt, dvv, dlw, dyb, hr, hk, hv, hl,
                       mu_r, mu_k, mu_v, mu_l, w0, a0, k_k, k_a, w_dec, w_iclr, tril, same, bd, r_k,
                       cr, ck, cv, cl_):
    f = rwkv_prep_core(tile0, r0, k0, v0, l0, hr, hk, hv, hl, mu_r, mu_k, mu_v, mu_l, w0, a0, k_k, k_a,
                       w_dec, w_iclr, tril, same, bd)
    ur, uk, uv, ul, kk, k2, a_ic, sg, th = (f[n] for n in ("ur", "uk", "uv", "ul", "kk", "k2", "a_ic", "sg", "th"))
    lc, lw = f["lc"], f["lw"]
    e_neg = jnp.exp(-lc)
    dur = drt * jnp.exp(lc)
    da = dat * jnp.exp(lc - lw)
    db = dbt * e_neg
    dk2 = dkt * e_neg
    s = head_sum(ur * k2 * r_k, bd)
    duv = dvv + dyb * s
    ds = head_sum(dyb * uv, bd)
    dur = dur + ds * k2 * r_k
    dk2 = dk2 + ds * ur * r_k
    dr_k = colsum(ds * ur * k2)
    dkk = db * a_ic - da
    da_ic = db * kk + dk2 * uk * k_a
    duk = dk2 * (1.0 + (a_ic - 1.0) * k_a)
    dk_a = colsum(dk2 * uk * (a_ic - 1.0))
    dkkraw = jnp.where(f["nrm_raw"] > 1e-12, (dkk - kk * head_sum(dkk * kk, bd)) / f["nrm"], dkk * 1e12)
    duk = duk + dkkraw * k_k
    dk_k = colsum(dkkraw * uk)
    dai = da_ic * a_ic * (1.0 - a_ic)
    dd = dlw * (-DECAY_SCALE) * sg * (1.0 - sg)
    dul = mm_nt(dai, w_iclr) + mm_nt(dd, w_dec) * (1.0 - th * th)

    def unshift(du, x, prev, mu, carry_row):
        nxt = shift_rows_up(du, carry_row)
        return du * (1.0 - mu) + nxt * mu, colsum(du * (prev - x)), du[0:1, :]

    dr0, dmu_r, ncr = unshift(dur, r0, f["pr"], mu_r, cr)
    dk0, dmu_k, nck = unshift(duk, k0, f["pk"], mu_k, ck)
    dv0, dmu_v, ncv = unshift(duv, v0, f["pv"], mu_v, cv)
    dl0, dmu_l, ncl = unshift(dul, l0, f["pl"], mu_l, cl_)
    return (dr0, dk0, dv0, dl0,
            dmu_r, dmu_k, dmu_v, dmu_l, colsum(dd), colsum(dai), dk_k, dk_a, dr_k, mm_tn(th, dd), mm_tn(ul, dai),
            ncr, nck, ncv, ncl)


def in_backward(x, dz, pieces, mod, w_in_pt, unrot):
    n_rows = x.shape[0]
    ts = ROW_TILE
    n_p = len(pieces)
    shard_cols = IN_WIDTH // N_DEV

    def body(*refs):
        x_ref, dz_ref = refs[:2]
        p_refs = refs[2:2 + n_p]
        mod_ref, w_ref, unrot_ref = refs[2 + n_p:5 + n_p]
        dx_ref, ht_ref, blocks_ref, dshift_ref, dscale_ref = refs[5 + n_p:]
        step0 = pl.program_id(0) == 0
        dma, dmb, dr0, dk0, dv0, dgpa, dgpb, dq_c, dkv_c, dkr, dkrr, dl0 = (r[...] for r in p_refs)
        dproj = jnp.concatenate([dma, dmb, dr0, dk0, dv0, dgpa, dgpb, dq_c, dkv_c, dkr, dkrr, dl0], axis=1)
        dh = mm(dproj, w_ref[...])
        xhat, rstd = layer_norm_stats(x_ref[...])
        scale1 = 1.0 + mod_ref[1:2, :]
        dx_ref[...] = layer_norm_bwd(dh * scale1, xhat, rstd) + ALPHA * dz_ref[...]
        ht_ref[...] = jnp.transpose(xhat * scale1 + mod_ref[0:1, :]).astype(BF16)
        dkrope = (dkr.astype(F32) + mm(dkrr, unrot_ref[...]))[:, NOPE:QK_DIM]
        natural = jnp.concatenate(
            [dq_c.astype(F32), dkv_c.astype(F32), dkrope]
            + [p.astype(F32) for p in (dgpa, dr0, dk0, dv0, dl0, dgpb, dma, dmb)], axis=1)
        for j in range(N_DEV):
            blocks_ref[j] = natural[:, j * shard_cols:(j + 1) * shard_cols].astype(BF16)
        for ref, val in ((dshift_ref, colsum(dh)), (dscale_ref, colsum(dh * xhat))):
            @pl.when(step0)
            def _(ref=ref, val=val):
                ref[...] = val

            @pl.when(jnp.logical_not(step0))
            def _(ref=ref, val=val):
                ref[...] += val

    row = lambda w: pl.BlockSpec((ts, w), lambda i: (i, 0))
    const = pl.BlockSpec(memory_space=pltpu.VMEM)
    vec = pl.BlockSpec((1, D_MODEL), lambda i: (0, 0))
    return pl.pallas_call(
        body, name="in_backward", grid=(n_rows // ts,),
        in_specs=[row(D_MODEL), row(D_MODEL)] + [row(p.shape[1]) for p in pieces] + [const] * 3,
        out_specs=[row(D_MODEL), pl.BlockSpec((D_MODEL, ts), lambda i: (0, i)),
                   pl.BlockSpec((N_DEV, ts, shard_cols), lambda i: (0, i, 0)), vec, vec],
        out_shape=[jax.ShapeDtypeStruct((n_rows, D_MODEL), F32), jax.ShapeDtypeStruct((D_MODEL, n_rows), BF16),
                   jax.ShapeDtypeStruct((N_DEV, n_rows, shard_cols), BF16),
                   jax.ShapeDtypeStruct((1, D_MODEL), F32), jax.ShapeDtypeStruct((1, D_MODEL), F32)],
        compiler_params=pltpu.CompilerParams(dimension_semantics=("arbitrary",), vmem_limit_bytes=VMEM_LIMIT),
    )(x, dz, *pieces, mod, w_in_pt, unrot)


def in_weight_grad_exchange(h_t, dp_blocks, others, small, order):
    n = len(others)
    n_rows = h_t.shape[1]
    ts = 4 * ROW_TILE
    n_i = n_rows // ts
    shard_cols = dp_blocks.shape[2]
    n_chips = N_DEV // 2
    last = N_DEV - 1

    def body(order_ref, h_ref, dp_ref, *rest):
        g_refs, s_ref = rest[:n], rest[n]
        rwin_ref, rg_refs, rs_ref = rest[n + 1], rest[n + 2:2 * n + 2], rest[2 * n + 2]
        (acc, sendbuf, sib_buf, sib_send, sib_recv, win_send, win_recv,
         o_send, o_recv, local_sems) = rest[2 * n + 3:]
        b, i = pl.program_id(0), pl.program_id(1)
        me = my_position()
        mi = flat_index(me)
        sibling = (me[0], me[1], 1 - me[2])

        def other_copies(k, src_index, dst_index):
            peer = flip(me, k)
            out = [pltpu.make_async_remote_copy(
                src_ref=g_refs[a].at[src_index], dst_ref=rg_refs[a].at[dst_index],
                send_sem=o_send.at[(n + 1) * (k - 1) + a], recv_sem=o_recv.at[(n + 1) * (k - 1) + a],
                device_id=peer, device_id_type=MESH_IDS) for a in range(n)]
            out.append(pltpu.make_async_remote_copy(
                src_ref=s_ref, dst_ref=rs_ref.at[dst_index],
                send_sem=o_send.at[(n + 1) * (k - 1) + n], recv_sem=o_recv.at[(n + 1) * (k - 1) + n],
                device_id=peer, device_id_type=MESH_IDS))
            return out

        def local_copies():
            out = [pltpu.make_async_copy(g_refs[a].at[mi], rg_refs[a].at[mi], local_sems.at[a]) for a in range(n)]
            out.append(pltpu.make_async_copy(s_ref, rs_ref.at[mi], local_sems.at[n]))
            return out

        def to_sibling(t):
            return pltpu.make_async_remote_copy(
                src_ref=sendbuf.at[t], dst_ref=sib_buf.at[t], send_sem=sib_send.at[t], recv_sem=sib_recv.at[t],
                device_id=sibling, device_id_type=MESH_IDS)

        def to_owner(t):
            flip_x = (t < 2) * 1
            flip_y = 1 - (t & 1)
            owner = (me[0] ^ flip_x, me[1] ^ flip_y, me[2])
            return pltpu.make_async_remote_copy(
                src_ref=sendbuf.at[n_chips + t], dst_ref=rwin_ref.at[t], send_sem=win_send.at[t],
                recv_sem=win_recv.at[t], device_id=owner, device_id_type=MESH_IDS)

        own_block = pltpu.make_async_copy(sendbuf.at[last], rwin_ref.at[n_chips - 1], local_sems.at[n + 1])

        @pl.when(jnp.logical_and(b == 0, i == 0))
        def _():
            for cp in local_copies():
                cp.start()
            for k in range(1, N_DEV):
                for cp in other_copies(k, flat_index(flip(me, k)), mi):
                    cp.start()

        contrib = jnp.dot(h_ref[...], dp_ref[...], preferred_element_type=F32)

        @pl.when(i == 0)
        def _():
            acc[...] = contrib

        @pl.when(i > 0)
        def _():
            acc[...] += contrib

        slot = order_ref[N_DEV + b]
        t = slot & (n_chips - 1)

        @pl.when(jnp.logical_and(i == n_i - 1, slot < n_chips))
        def _():
            sendbuf[slot] = acc[...].astype(BF16)
            to_sibling(t).start()

        @pl.when(jnp.logical_and(i == n_i - 1, slot >= n_chips))
        def _():
            to_sibling(t).wait_recv()
            sendbuf[slot] = (acc[...] + sib_buf[t].astype(F32)).astype(BF16)

            @pl.when(slot < last)
            def _():
                to_owner(t).start()

            @pl.when(slot == last)
            def _():
                own_block.start()

        @pl.when(jnp.logical_and(b == last, i == n_i - 1))
        def _():
            for t in range(n_chips - 1):
                to_owner(t).wait_recv()
            for k in range(1, N_DEV):
                pi = flat_index(flip(me, k))
                for cp in other_copies(k, pi, pi):
                    cp.wait_recv()
            for t in range(n_chips):
                to_sibling(t).wait_send()
            for t in range(n_chips - 1):
                to_owner(t).wait_send()
            for k in range(1, N_DEV):
                for cp in other_copies(k, flat_index(flip(me, k)), mi):
                    cp.wait_send()
            for cp in local_copies():
                cp.wait()
            own_block.wait()

    hbm = pl.BlockSpec(memory_space=pl.ANY)
    n_sem = 7 * (n + 1)
    grid_spec = pltpu.PrefetchScalarGridSpec(
        num_scalar_prefetch=1, grid=(N_DEV, n_i),
        in_specs=[pl.BlockSpec((D_MODEL, ts), lambda b, i, order: (0, i)),
                  pl.BlockSpec((None, ts, shard_cols), lambda b, i, order: (order[b], i, 0))] + [hbm] * (n + 1),
        out_specs=[hbm] * (n + 2),
        scratch_shapes=[pltpu.VMEM((D_MODEL, shard_cols), F32), pltpu.VMEM((N_DEV, D_MODEL, shard_cols), BF16),
                        pltpu.VMEM((n_chips, D_MODEL, shard_cols), BF16),
                        pltpu.SemaphoreType.DMA((n_chips,)), pltpu.SemaphoreType.DMA((n_chips,)),
                        pltpu.SemaphoreType.DMA((n_chips - 1,)), pltpu.SemaphoreType.DMA((n_chips - 1,)),
                        pltpu.SemaphoreType.DMA((n_sem,)), pltpu.SemaphoreType.DMA((n_sem,)),
                        pltpu.SemaphoreType.DMA((n + 2,))])
    return pl.pallas_call(
        body, name="in_weight_grad_exchange", grid_spec=grid_spec,
        out_shape=[jax.ShapeDtypeStruct((n_chips, D_MODEL, shard_cols), BF16)]
        + [jax.ShapeDtypeStruct(o.shape, o.dtype) for o in others]
        + [jax.ShapeDtypeStruct((N_DEV,) + small.shape, small.dtype)],
        compiler_params=pltpu.CompilerParams(dimension_semantics=("arbitrary", "arbitrary"),
                                             vmem_limit_bytes=VMEM_LIMIT),
    )(order, h_t, dp_blocks, *others, small)


def ada_weight_grad(c_all, dmod_cols):
    def body(c_ref, d_ref, o_ref):
        cv = c_ref[...]
        o_ref[...] = hdot_tn(cv * sigmoid(cv), d_ref[...])

    return pl.pallas_call(
        body, name="ada_weight_grad",
        out_shape=jax.ShapeDtypeStruct((c_all.shape[1], dmod_cols.shape[1]), F32),
    )(c_all, dmod_cols)


def adamw_update(g, w, m, v):
    nm = ADAM_B1 * m + (1.0 - ADAM_B1) * g
    nv = ADAM_B2 * v + (1.0 - ADAM_B2) * (g * g)
    m_hat = nm / (1.0 - ADAM_B1 ** ADAM_STEP)
    v_hat = nv / (1.0 - ADAM_B2 ** ADAM_STEP)
    return -ADAM_LR * (m_hat / (jnp.sqrt(v_hat) + ADAM_EPS) + ADAM_WD * w), nm, nv


def adamw(parts, w, m, v, name):
    k, rows, cols = parts.shape

    def body(p_ref, w_hbm, m_hbm, v_hbm, g_ref, d_ref, nm_ref, nv_ref, w_buf, m_buf, v_buf, sems):
        loads = [pltpu.make_async_copy(src, dst, sems.at[i])
                 for i, (src, dst) in enumerate(((w_hbm, w_buf), (m_hbm, m_buf), (v_hbm, v_buf)))]
        for cp in loads:
            cp.start()
        g = p_ref[0].astype(F32)
        for i in range(1, k):
            g = g + p_ref[i].astype(F32)
        g_ref[0] = g
        for cp in loads:
            cp.wait()
        d_ref[0], nm_ref[0], nv_ref[0] = adamw_update(g, w_buf[0], m_buf[0], v_buf[0])

    hbm = pl.BlockSpec(memory_space=pl.ANY)
    whole = pl.BlockSpec(memory_space=pltpu.VMEM)
    return pl.pallas_call(
        body, name=name,
        in_specs=[whole, hbm, hbm, hbm], out_specs=[whole] * 4,
        out_shape=[jax.ShapeDtypeStruct((1, rows, cols), F32)] * 4,
        scratch_shapes=[pltpu.VMEM((1, rows, cols), F32)] * 3 + [pltpu.SemaphoreType.DMA((3,))],
        compiler_params=pltpu.CompilerParams(vmem_limit_bytes=VMEM_LIMIT),
    )(parts, w, m, v)


def adamw_small(parts, ws, ms, vs):
    k = parts.shape[0]
    n = len(ws)
    sizes = [w.shape[1] for w in ws]

    def body(p_ref, *refs):
        ins, outs = refs[:3 * n], refs[3 * n:]
        g_all = p_ref[0]
        for i in range(1, k):
            g_all = g_all + p_ref[i]
        off = 0
        for a, size in enumerate(sizes):
            g = g_all[:, off:off + size]
            off += size
            d, nm, nv = adamw_update(g, ins[a][...], ins[n + a][...], ins[2 * n + a][...])
            for kind, val in enumerate((g, d, nm, nv)):
                outs[kind * n + a][...] = val

    return pl.pallas_call(
        body, name="adamw_small",
        out_shape=[jax.ShapeDtypeStruct((1, size), F32) for _ in range(4) for size in sizes],
    )(parts, *ws, *ms, *vs)


def columns_from_shards(g, rows, cols):
    return g.reshape(N_DEV, rows, cols).transpose(1, 0, 2).reshape(rows, N_DEV * cols)


def permute_w_in_t(wt):
    z = lambda n: jnp.zeros((n, D_MODEL), wt.dtype)
    krope = wt[N_KROPE:N_KROPE + ROPE]
    krope_rot = jnp.concatenate([-krope[ROPE // 2:], krope[:ROPE // 2]], axis=0)
    rw = N_RWKV
    return jnp.concatenate([
        wt[N_MA:N_MA + 1024], wt[N_MB:N_MB + 1024],
        wt[rw:rw + 512], wt[rw + 512:rw + 1024], wt[rw + 1024:rw + 1536],
        wt[N_GPA:N_GPA + 512], wt[N_GPB:N_GPB + 512],
        wt[N_QC:N_QC + 256], wt[N_KVC:N_KVC + 128],
        z(NOPE), krope, z(LANE - QK_DIM), z(NOPE), krope_rot, z(LANE - QK_DIM),
        wt[rw + 1536:rw + 1664]], axis=0)


def pad_heads_q(w_uq):
    w = w_uq.reshape(Q_RANK, HEADS, QK_DIM)
    zpad = jnp.zeros((Q_RANK, HEADS, LANE - QK_DIM), w.dtype)
    wq = jnp.concatenate([w, zpad], axis=2).reshape(Q_RANK, HEADS * LANE)
    pe = w[:, :, NOPE:]
    rot = jnp.concatenate([-pe[:, :, ROPE // 2:], pe[:, :, :ROPE // 2]], axis=2)
    wqr = jnp.concatenate([jnp.zeros((Q_RANK, HEADS, NOPE), w.dtype), rot, zpad], axis=2).reshape(Q_RANK, HEADS * LANE)
    return wq, wqr


def unpad_heads_q_grad(dwq, dwqr):
    a = dwq.reshape(Q_RANK, HEADS, LANE)
    r = dwqr.reshape(Q_RANK, HEADS, LANE)[:, :, NOPE:QK_DIM]
    pe = a[:, :, NOPE:QK_DIM] + jnp.concatenate([r[:, :, ROPE // 2:], -r[:, :, :ROPE // 2]], axis=2)
    return jnp.concatenate([a[:, :, :NOPE], pe], axis=2).reshape(Q_RANK, HEADS * QK_DIM)


def pad_heads_kv(w_ukv):
    w = w_ukv.reshape(KV_RANK, HEADS, 2 * HEAD)
    z = jnp.zeros((KV_RANK, HEADS, HEAD), w.dtype)
    wkn = jnp.concatenate([w[:, :, :NOPE], z], axis=2).reshape(KV_RANK, HEADS * LANE)
    val = w[:, :, NOPE:]
    odd = (jnp.arange(HEADS) % 2 == 1)[None, :, None]
    wv = jnp.concatenate([jnp.where(odd, 0, val), jnp.where(odd, val, 0)], axis=2).reshape(KV_RANK, HEADS * LANE)
    return wkn, wv


def unpad_heads_kv_grad(dwkn, dwv):
    a = dwkn.reshape(KV_RANK, HEADS, LANE)[:, :, :NOPE]
    b = dwv.reshape(KV_RANK, HEADS, LANE)
    odd = (jnp.arange(HEADS) % 2 == 1)[None, :, None]
    val = jnp.where(odd, b[:, :, HEAD:], b[:, :, :HEAD])
    return jnp.concatenate([a, val], axis=2).reshape(KV_RANK, HEADS * 2 * HEAD)


def kernel(x, c, positions, w_ada, b_ada, w_in, q_norm_g, w_uq, kv_norm_g, w_ukv, mu_rwkv, w0, w_decay_up, a0, w_iclr_up, k_k, k_a, r_k, gn_g, gn_b, w_proj_a, w_proj_b, w_out, post_g, post_b, loss_target, m_w_ada, m_b_ada, m_w_in, m_q_norm_g, m_w_uq, m_kv_norm_g, m_w_ukv, m_mu_rwkv, m_w0, m_w_decay_up, m_a0, m_w_iclr_up, m_k_k, m_k_a, m_r_k, m_gn_g, m_gn_b, m_w_proj_a, m_w_proj_b, m_w_out, m_post_g, m_post_b, v_w_ada, v_b_ada, v_w_in, v_q_norm_g, v_w_uq, v_kv_norm_g, v_w_ukv, v_mu_rwkv, v_w0, v_w_decay_up, v_a0, v_w_iclr_up, v_k_k, v_k_a, v_r_k, v_gn_g, v_gn_b, v_w_proj_a, v_w_proj_b, v_w_out, v_post_g, v_post_b):
    weights = dict(w_ada=w_ada, b_ada=b_ada, w_in=w_in, q_norm_g=q_norm_g, w_uq=w_uq, kv_norm_g=kv_norm_g,
                   w_ukv=w_ukv, mu_rwkv=mu_rwkv, w0=w0, w_decay_up=w_decay_up, a0=a0, w_iclr_up=w_iclr_up,
                   k_k=k_k, k_a=k_a, r_k=r_k, gn_g=gn_g, gn_b=gn_b, w_proj_a=w_proj_a, w_proj_b=w_proj_b,
                   w_out=w_out, post_g=post_g, post_b=post_b)
    mom1 = dict(w_ada=m_w_ada, b_ada=m_b_ada, w_in=m_w_in, q_norm_g=m_q_norm_g, w_uq=m_w_uq, kv_norm_g=m_kv_norm_g,
                w_ukv=m_w_ukv, mu_rwkv=m_mu_rwkv, w0=m_w0, w_decay_up=m_w_decay_up, a0=m_a0, w_iclr_up=m_w_iclr_up,
                k_k=m_k_k, k_a=m_k_a, r_k=m_r_k, gn_g=m_gn_g, gn_b=m_gn_b, w_proj_a=m_w_proj_a, w_proj_b=m_w_proj_b,
                w_out=m_w_out, post_g=m_post_g, post_b=m_post_b)
    mom2 = dict(w_ada=v_w_ada, b_ada=v_b_ada, w_in=v_w_in, q_norm_g=v_q_norm_g, w_uq=v_w_uq, kv_norm_g=v_kv_norm_g,
                w_ukv=v_w_ukv, mu_rwkv=v_mu_rwkv, w0=v_w0, w_decay_up=v_w_decay_up, a0=v_a0, w_iclr_up=v_w_iclr_up,
                k_k=v_k_k, k_a=v_k_a, r_k=v_r_k, gn_g=v_gn_g, gn_b=v_gn_b, w_proj_a=v_w_proj_a, w_proj_b=v_w_proj_b,
                w_out=v_w_out, post_g=v_post_g, post_b=v_post_b)
    names = list(weights)
    n_rows = x.shape[1]
    me = 4 * lax.axis_index("x") + 2 * lax.axis_index("y") + lax.axis_index("c")
    xr = x[0]
    tgt = loss_target[0]
    row = lambda a: a.reshape(1, -1)

    w_in_all, c_all = gather_shards([w_in[0].T.astype(BF16), c])
    c_all = c_all.reshape(N_DEV, D_MODEL)
    w_in_pt = permute_w_in_t(w_in_all.reshape(IN_WIDTH, D_MODEL))

    mod_all = ada_modulation(c_all, w_ada[0], b_ada.reshape(N_DEV, -1))
    mod = lax.dynamic_index_in_dim(mod_all, me, axis=1, keepdims=False).reshape(3, D_MODEL)

    proj, *gathered = fwd_in_gather(xr, mod, w_in_pt, [weights[n][0].astype(BF16) for n, _, _ in SHARDED[1:]])
    pcol = lambda off_, w: (proj, w, off_ // w)
    full = {}
    for (n, r, cdim), part in zip(SHARDED[1:], gathered):
        full[n] = part.reshape(N_DEV * r, cdim) if n == "w_out" else columns_from_shards(part, r, cdim)
    wq, wqr = pad_heads_q(full["w_uq"])
    wkn, wv = pad_heads_kv(full["w_ukv"])
    zl = jnp.zeros((LORA, WIDTH), BF16)
    w_dec = jnp.concatenate([full["w_decay_up"], zl], axis=0)
    w_iclr = jnp.concatenate([zl, full["w_iclr_up"]], axis=0)
    wpa, wpb, wout = full["w_proj_a"], full["w_proj_b"], full["w_out"]

    inv_freq = ROPE_THETA ** (-jnp.arange(0, ROPE, 2, dtype=F32) / ROPE)
    ang = positions[0].astype(F32)[:, None] * inv_freq
    ones_n, zeros_n, zeros_p = jnp.ones((n_rows, NOPE), F32), jnp.zeros((n_rows, NOPE), F32), jnp.zeros((n_rows, LANE - QK_DIM), F32)
    cos_t = jnp.concatenate([ones_n, jnp.cos(ang), jnp.cos(ang), zeros_p], axis=1)
    sin_t = jnp.concatenate([zeros_n, jnp.sin(ang), jnp.sin(ang), zeros_p], axis=1)

    gq, gkv = q_norm_g, kv_norm_g
    mla_consts = [gq, gkv, wq, wqr, wkn, wv]
    q, k, v = row_call(
        "mla_prep", mla_prep_tile, n_rows,
        [pcol(P_QC, 256), pcol(P_KVC, 128), pcol(P_KR, 128), pcol(P_KRR, 128), (cos_t, LANE, 0), (sin_t, LANE, 0)],
        mla_consts, [(HEADS * LANE, BF16)] * 3, tile_rows=PREP_TILE)
    ya, lse = attention_forward(q, k, v)

    def chunk_sum_matrices(n):
        t_idx = jnp.arange(n)
        same_chunk = (t_idx[:, None] // CHUNK) == (t_idx[None, :] // CHUNK)
        return (same_chunk & (t_idx[:, None] >= t_idx[None, :])).astype(F32), same_chunk.astype(F32)

    l_idx = jnp.arange(LANE)
    bd = ((l_idx[:, None] // HEAD) == (l_idx[None, :] // HEAD)).astype(F32)
    mu = mu_rwkv
    mu_r, mu_k, mu_v, mu_l = mu[:, 0:512], mu[:, 512:1024], mu[:, 1024:1536], mu[:, 1536:1664]
    rk_row = row(r_k)
    rwkv_consts = lambda n: [mu_r, mu_k, mu_v, mu_l, w0, a0, k_k, k_a, w_dec, w_iclr, *chunk_sum_matrices(n), bd]
    rwkv_rows = [pcol(P_R, 512), pcol(P_K, 512), pcol(P_V, 512), pcol(P_LORA, 128)]
    rt, at, bt, kt, clf, uv, ur, k2 = row_call(
        "rwkv_prep", rwkv_prep_tile, n_rows, rwkv_rows, rwkv_consts(ROW_TILE), [(WIDTH, F32)] * 8, halo_in=rwkv_rows)
    y, m0s, state_maps, out_maps, *wkv_saved = wkv_forward(at, bt, kt, rt, uv, clf)

    tail = row_call(
        "tail", tail_tile, n_rows,
        [(xr, D_MODEL, 0), (tgt, D_MODEL, 0), pcol(P_MA, 1024), pcol(P_MB, 1024), pcol(P_GPA, 512), pcol(P_GPB, 512),
         (ya, WIDTH, 0), (y, WIDTH, 0), (ur, WIDTH, 0), (k2, WIDTH, 0), (uv, WIDTH, 0)],
        [mod, wpa, wpb, wout, gn_g, gn_b, rk_row, post_g, post_b, bd],
        [(D_MODEL, F32), (1024, BF16), (1024, BF16), (512, BF16), (512, BF16), (WIDTH, F32), (WIDTH, F32), (WIDTH, F32)],
        acc_out=[((1, LANE), F32), ((D_MODEL, D_MODEL), F32), ((WIDTH, D_MODEL), F32), ((WIDTH, D_MODEL), F32),
                 ((1, WIDTH), F32), ((1, WIDTH), F32), ((1, D_MODEL), F32), ((1, D_MODEL), F32), ((1, D_MODEL), F32)])
    (dz, dma, dmb, dgpa, dgpb, dya, dy, dyb,
     loss_row, g_wout, g_wpa, g_wpb, g_gn_g, g_gn_b, g_post_g, g_post_b, dgate) = tail

    def owner_blocks(g, n):
        r, cdim = next((r, cdim) for name, r, cdim in SHARDED if name == n)
        return (g.reshape(N_DEV, r, cdim) if n == "w_out" else g.reshape(r, N_DEV, cdim).transpose(1, 0, 2)).astype(BF16)

    early = ("w_proj_a", "w_proj_b", "w_out")
    dq, dk, dv, *got_early = attention_backward(
        q, k, v, ya, dya, lse, [owner_blocks(g, n) for g, n in zip((g_wpa, g_wpb, g_wout), early)])
    dq_c, dkv_c, dkr, dkrr, g_wq, g_wqr, g_wkn, g_wv, g_gq, g_gkv = row_call(
        "mla_prep_bwd", mla_prep_bwd_tile, n_rows,
        [pcol(P_QC, 256), pcol(P_KVC, 128), (cos_t, LANE, 0), (sin_t, LANE, 0),
         (dq, HEADS * LANE, 0), (dk, HEADS * LANE, 0), (dv, HEADS * LANE, 0)],
        mla_consts, [(256, BF16), (128, BF16), (128, BF16), (128, BF16)],
        acc_out=[((Q_RANK, HEADS * LANE), F32)] * 2 + [((KV_RANK, HEADS * LANE), F32)] * 2
        + [((1, Q_RANK), F32), ((1, KV_RANK), F32)], tile_rows=PREP_TILE)

    dat, dbt, dkt, drt, dvv, dlw = wkv_backward(at, bt, kt, rt, uv, clf, m0s, state_maps, out_maps, wkv_saved, dy)
    (dr0, dk0, dv0, dl0, g_mu_r, g_mu_k, g_mu_v, g_mu_l, g_w0, g_a0, g_k_k, g_k_a, g_r_k, g_wdec, g_wiclr) = row_call(
        "rwkv_prep_bwd", rwkv_prep_bwd_tile, n_rows,
        rwkv_rows + [(drt, WIDTH, 0), (dat, WIDTH, 0), (dbt, WIDTH, 0), (dkt, WIDTH, 0), (dvv, WIDTH, 0),
                     (dlw, WIDTH, 0), (dyb, WIDTH, 0)],
        rwkv_consts(PREP_TILE) + [rk_row], [(512, BF16), (512, BF16), (512, BF16), (128, BF16)],
        acc_out=[((1, 512), F32)] * 3 + [((1, 128), F32)] + [((1, 512), F32)] * 5 + [((LANE, WIDTH), F32)] * 2,
        halo_in=rwkv_rows, carry=[512, 512, 512, 128], reverse=True, tile_rows=PREP_TILE)

    li = jnp.arange(LANE)
    src, dst = li[:, None], li[None, :]
    half = ROPE // 2
    unrot = (jnp.where((dst >= NOPE) & (dst < NOPE + half) & (src == dst + half), 1.0, 0.0)
             - jnp.where((dst >= NOPE + half) & (dst < QK_DIM) & (src == dst - half), 1.0, 0.0)).astype(BF16)
    dx, h_t, dproj_blocks, dshift, dscale = in_backward(
        xr, dz, [dma, dmb, dr0, dk0, dv0, dgpa, dgpb, dq_c, dkv_c, dkr, dkrr, dl0], mod, w_in_pt, unrot)

    late = ("w_uq", "w_ukv", "w_decay_up", "w_iclr_up")
    late_grads = (unpad_heads_q_grad(g_wq, g_wqr), unpad_heads_kv_grad(g_wkn, g_wv), g_wdec[:LORA], g_wiclr[LORA:])
    blocks = [owner_blocks(g, n) for g, n in zip(late_grads, late)]
    dmod = jnp.concatenate([dshift, dscale, dgate], axis=1)
    small = jnp.concatenate([dmod, g_gq, g_gkv, g_mu_r, g_mu_k, g_mu_v, g_mu_l, g_w0, g_a0, g_k_k, g_k_a, g_r_k,
                             g_gn_g, g_gn_b, g_post_g, g_post_b, loss_row], axis=1)
    my_x, my_y, my_c = lax.axis_index("x"), lax.axis_index("y"), lax.axis_index("c")
    chip_order = [4 * (my_x ^ fx) + 2 * (my_y ^ fy) for fx, fy in ((1, 1), (1, 0), (0, 1), (0, 0))]
    owners = [chip_order[s % 4] + (my_c if s >= 4 else 1 - my_c) for s in WGRAD_SLOTS]
    order = jnp.stack(owners + [jnp.int32(s) for s in WGRAD_SLOTS]).astype(jnp.int32)
    got_w_in, *got_late, got_small = in_weight_grad_exchange(h_t, dproj_blocks, blocks, small, order)
    got = {"w_in": got_w_in, **dict(zip(late, got_late)), **dict(zip(early, got_early))}
    loss = jnp.sum(got_small[:, 0, SMALL_ELEMS])

    ada_cols = w_ada.shape[2]
    dmod_all = got_small[:, 0, :3 * D_MODEL]
    g_ada = ada_weight_grad(c_all, lax.dynamic_slice_in_dim(dmod_all, me * ada_cols, ada_cols, axis=1))

    outs = [dict() for _ in range(4)]
    res = adamw(g_ada[None], w_ada, m_w_ada, v_w_ada, "adamw_w_ada")
    for kind in range(4):
        outs[kind]["w_ada"] = res[kind]
    for n, _, _ in SHARDED:
        res = adamw(got[n], weights[n], mom1[n], mom2[n], "adamw_" + n)
        for kind in range(4):
            outs[kind][n] = res[kind]
    rows_of = lambda tree: [tree[n].reshape(1, -1) for n, _ in SMALL]
    res = adamw_small(got_small, rows_of(weights), rows_of(mom1), rows_of(mom2))
    for kind in range(4):
        for a, (n, _) in enumerate(SMALL):
            outs[kind][n] = res[kind * len(SMALL) + a].reshape(weights[n].shape)
    return (loss, dx[None], *[outs[0][n] for n in names], *[outs[1][n] for n in names],
            *[outs[2][n] for n in names], *[outs[3][n] for n in names])
```

```python
import functools
import math

import jax
import jax.numpy as jnp
from jax import lax
from jax.experimental import pallas as pl
from jax.experimental.pallas import tpu as pltpu

F32 = jnp.float32
BF16 = jnp.bfloat16
HIGHEST = lax.Precision.HIGHEST
MESH_IDS = pl.DeviceIdType.MESH

N_DEV = 8
D_MODEL = 1024
LN_EPS = 1e-5
RMS_EPS = 1e-6
GN_EPS = 64e-5
HEADS = 8
Q_RANK = 256
KV_RANK = 128
ROPE = 32
NOPE = 64
QK_DIM = NOPE + ROPE
WIDTH = 512
HEAD = 64
LORA = 64
CHUNK = 64
DEPTH = 1
ALPHA = (2.0 * DEPTH) ** 0.25
ROPE_THETA = 10000.0
ATTN_SCALE = QK_DIM ** -0.5
DECAY_SCALE = math.exp(-0.5)

ADAM_LR = 0.001
ADAM_B1 = 0.9
ADAM_B2 = 0.999
ADAM_EPS = 1e-08
ADAM_WD = 0.01
ADAM_STEP = 10

LANE = 128
PAIR = 2 * HEAD
ROW_TILE = 256
PREP_TILE = 512
HALO_ROWS = 16
ATTN_FWD_TILES = (512, 1024)
ATTN_BWD_TILES = (512, 512)
LOG2_E = math.log2(math.e)
Q_PRESCALE = ATTN_SCALE * LOG2_E
WKV_CHUNKS_PER_STEP = 16
WKV_CHAIN_GROUPS = 2
WGRAD_SLOTS = (0, 1, 4, 2, 5, 6, 3, 7)
VMEM_LIMIT = 56 * 1024 * 1024

P_MA, P_MB, P_R, P_K, P_V, P_GPA, P_GPB, P_QC, P_KVC, P_KR, P_KRR, P_LORA = (
    0, 1024, 2048, 2560, 3072, 3584, 4096, 4608, 4864, 4992, 5120, 5248)
P_WIDTH = 5376

N_QC, N_KVC, N_KROPE, N_GPA, N_RWKV, N_GPB, N_MA, N_MB = 0, 256, 384, 416, 928, 2592, 3104, 4128
IN_WIDTH = 5152

SHARDED = (("w_in", 1024, 644), ("w_uq", 256, 96), ("w_ukv", 128, 128), ("w_decay_up", 64, 64),
           ("w_iclr_up", 64, 64), ("w_proj_a", 512, 128), ("w_proj_b", 512, 128), ("w_out", 128, 1024))
SMALL = (("b_ada", 3072), ("q_norm_g", 256), ("kv_norm_g", 128), ("mu_rwkv", 1664), ("w0", 512), ("a0", 512),
         ("k_k", 512), ("k_a", 512), ("r_k", 512), ("gn_g", 512), ("gn_b", 512), ("post_g", 1024), ("post_b", 1024))
SMALL_ELEMS = sum(n for _, n in SMALL)


def mm(a, b):
    return jnp.dot(a.astype(BF16), b.astype(BF16), preferred_element_type=F32)


def mm_nt(a, b):
    return lax.dot_general(a.astype(BF16), b.astype(BF16), (((1,), (1,)), ((), ())), preferred_element_type=F32)


def mm_tn(a, b):
    return lax.dot_general(a.astype(BF16), b.astype(BF16), (((0,), (0,)), ((), ())), preferred_element_type=F32)


def hdot(a, b):
    return jnp.dot(a, b, precision=HIGHEST, preferred_element_type=F32)


def hdot_tn(a, b):
    return lax.dot_general(a, b, (((0,), (0,)), ((), ())), precision=HIGHEST, preferred_element_type=F32)


def sigmoid(x):
    return 1.0 / (1.0 + jnp.exp(-x))


def colsum(x):
    return jnp.sum(x, axis=0, keepdims=True)


def rowmean(x):
    return jnp.mean(x, axis=-1, keepdims=True)


def layer_norm_stats(x):
    xc = x - rowmean(x)
    rstd = lax.rsqrt(rowmean(xc * xc) + LN_EPS)
    return xc * rstd, rstd


def layer_norm_bwd(dy, xhat, rstd):
    return rstd * (dy - rowmean(dy) - xhat * rowmean(dy * xhat))


def bf16_pieces(x, n):
    pieces = []
    for _ in range(n):
        p = x.astype(BF16)
        pieces.append(p)
        x = x - p.astype(F32)
    return pieces


def ones_dot(ones, x, n_pieces):
    ones = ones.astype(BF16)
    return sum(jnp.dot(ones, p, preferred_element_type=F32) for p in bf16_pieces(x, n_pieces))


def ones_dot_nt(ones, x, n_pieces):
    ones = ones.astype(BF16)
    return sum(lax.dot_general(ones, p, (((1,), (1,)), ((), ())), preferred_element_type=F32)
               for p in bf16_pieces(x, n_pieces))


def head_sum(x, bd):
    return jnp.concatenate([mm(x[:, p * LANE:(p + 1) * LANE], bd) for p in range(x.shape[1] // LANE)], axis=1)


def tile_lanes(t, n):
    return jnp.concatenate([t] * n, axis=1)


def row_iota(shape):
    return lax.broadcasted_iota(jnp.int32, shape, 0)


def lane_iota(shape):
    return lax.broadcasted_iota(jnp.int32, shape, 1)


def shift_rows_down(x, row0):
    rolled = pltpu.roll(x, 1, axis=0)
    return jnp.where(row_iota(x.shape) == 0, row0, rolled)


def shift_rows_up(x, row_last):
    rolled = pltpu.roll(x, x.shape[0] - 1, axis=0)
    return jnp.where(row_iota(x.shape) == x.shape[0] - 1, row_last, rolled)


def row_call(name, fn, n_rows, row_in, const_in, row_out, acc_out=(), halo_in=(), carry=(), reverse=False,
             tile_rows=ROW_TILE):
    ts = tile_rows
    n_tiles = n_rows // ts
    n_in = len(row_in) + len(halo_in) + len(const_in)
    n_ro, n_ao = len(row_out), len(acc_out)

    def tile_of(g):
        return (n_tiles - 1 - g) if reverse else g

    def body(*refs):
        ins = refs[:n_in]
        ro = refs[n_in:n_in + n_ro]
        ao = refs[n_in + n_ro:n_in + n_ro + n_ao]
        cr = refs[n_in + n_ro + n_ao:]
        g = pl.program_id(0)
        step0 = g == 0
        tile0 = tile_of(g) == 0
        for r in cr:
            @pl.when(step0)
            def _(r=r):
                r[...] = jnp.zeros_like(r)
        n_tiled = len(row_in) + len(halo_in)
        vals = [r[...].astype(F32) for r in ins[:n_tiled]] + [r[...] for r in ins[n_tiled:]]
        outs = fn(step0, tile0, *vals, *[c[0:1, :] for c in cr])
        for r, v in zip(ro, outs[:n_ro]):
            r[...] = v.astype(r.dtype)
        for r, v in zip(ao, outs[n_ro:n_ro + n_ao]):
            @pl.when(step0)
            def _(r=r, v=v):
                r[...] = v.astype(r.dtype)

            @pl.when(jnp.logical_not(step0))
            def _(r=r, v=v):
                r[...] += v.astype(r.dtype)
        for r, v in zip(cr, outs[n_ro + n_ao:]):
            r[0:1, :] = v

    in_specs = [pl.BlockSpec((ts, w), functools.partial(lambda g, cb: (tile_of(g), cb), cb=cb)) for _, w, cb in row_in]
    in_specs += [pl.BlockSpec((HALO_ROWS, w), functools.partial(
        lambda g, cb: (jnp.maximum(tile_of(g) * (ts // HALO_ROWS) - 1, 0), cb), cb=cb)) for _, w, cb in halo_in]
    in_specs += [pl.BlockSpec(memory_space=pltpu.VMEM) for _ in const_in]
    out_specs = [pl.BlockSpec((ts, w), lambda g: (tile_of(g), 0)) for w, _ in row_out]
    out_specs += [pl.BlockSpec(s, lambda g: (0, 0)) for s, _ in acc_out]
    out_shape = [jax.ShapeDtypeStruct((n_rows, w), d) for w, d in row_out]
    out_shape += [jax.ShapeDtypeStruct(s, d) for s, d in acc_out]
    return pl.pallas_call(
        body, name=name, grid=(n_tiles,), in_specs=in_specs, out_specs=out_specs, out_shape=out_shape,
        scratch_shapes=[pltpu.VMEM((8, w), F32) for w in carry],
        compiler_params=pltpu.CompilerParams(dimension_semantics=("arbitrary",), vmem_limit_bytes=VMEM_LIMIT),
    )(*[a for a, _, _ in row_in], *[a for a, _, _ in halo_in], *const_in)


def my_position():
    return lax.axis_index("x"), lax.axis_index("y"), lax.axis_index("c")


def flip(pos, k):
    x, y, c = pos
    dx, dy, dc = (k >> 2) & 1, (k >> 1) & 1, k & 1
    return (1 - x if dx else x, 1 - y if dy else y, 1 - c if dc else c)


def flat_index(pos):
    return 4 * pos[0] + 2 * pos[1] + pos[2]


def gather_shards(shards):
    n = len(shards)

    def body(*refs):
        x_refs, out_refs = refs[:n], refs[n:2 * n]
        send_sems, recv_sems, local_sems = refs[2 * n:]
        x, y, c = my_position()
        me, sibling = (x, y, c), (x, y, 1 - c)
        chips = [(1 - x, y), (x, 1 - y), (1 - x, 1 - y)]

        def copy(a, k, block, to, from_input=False):
            slot = out_refs[a].at[flat_index(block)]
            return pltpu.make_async_remote_copy(
                src_ref=x_refs[a] if from_input else slot, dst_ref=slot,
                send_sem=send_sems.at[7 * a + k], recv_sem=recv_sems.at[7 * a + k],
                device_id=to, device_id_type=MESH_IDS)

        mine = [pltpu.make_async_copy(x_refs[a], out_refs[a].at[flat_index(me)], local_sems.at[a]) for a in range(n)]
        for cp in mine:
            cp.start()
        first = []
        for a in range(n):
            first.append(copy(a, 0, me, sibling, from_input=True))
            first += [copy(a, 1 + j, me, (*chip, c), from_input=True) for j, chip in enumerate(chips)]
        for cp in first:
            cp.start()
        passed = []
        for j, chip in enumerate(chips):
            for a in range(n):
                copy(a, 1 + j, (*chip, c), me).wait_recv()
                cp = copy(a, 4 + j, (*chip, c), sibling)
                cp.start()
                passed.append(cp)
        for a in range(n):
            copy(a, 0, sibling, me).wait_recv()
            for j, chip in enumerate(chips):
                copy(a, 4 + j, (*chip, 1 - c), me).wait_recv()
        for cp in first + passed:
            cp.wait_send()
        for cp in mine:
            cp.wait()

    return pl.pallas_call(
        body, name="gather_shards",
        out_shape=[jax.ShapeDtypeStruct((N_DEV,) + s.shape, s.dtype) for s in shards],
        in_specs=[pl.BlockSpec(memory_space=pl.ANY)] * n, out_specs=[pl.BlockSpec(memory_space=pl.ANY)] * n,
        scratch_shapes=[pltpu.SemaphoreType.DMA((7 * n,)), pltpu.SemaphoreType.DMA((7 * n,)),
                        pltpu.SemaphoreType.DMA((n,))],
    )(*shards)


def ada_modulation(c_all, w_ada_loc, b_ada_blocks):
    cols = w_ada_loc.shape[1]

    def body(c_ref, w_ref, b_ref, out_ref, send_sems, recv_sems):
        me = my_position()
        mi = flat_index(me)
        cv = c_ref[...]
        res = hdot(cv * sigmoid(cv), w_ref[...]) + b_ref[pl.ds(mi, 1), :]
        out_ref[mi] = res
        sends = []
        for k in range(1, N_DEV):
            cp = pltpu.make_async_remote_copy(
                src_ref=out_ref.at[mi], dst_ref=out_ref.at[mi], send_sem=send_sems.at[k - 1],
                recv_sem=recv_sems.at[k - 1], device_id=flip(me, k), device_id_type=MESH_IDS)
            cp.start()
            sends.append(cp)
        for k in range(1, N_DEV):
            pi = flat_index(flip(me, k))
            pltpu.make_async_remote_copy(
                src_ref=out_ref.at[pi], dst_ref=out_ref.at[pi], send_sem=send_sems.at[k - 1],
                recv_sem=recv_sems.at[k - 1], device_id=flip(me, k), device_id_type=MESH_IDS).wait_recv()
        for cp in sends:
            cp.wait_send()

    return pl.pallas_call(
        body, name="ada_modulation",
        out_shape=jax.ShapeDtypeStruct((N_DEV, N_DEV, cols), F32),
        in_specs=[pl.BlockSpec(memory_space=pltpu.VMEM)] * 3, out_specs=pl.BlockSpec(memory_space=pltpu.VMEM),
        scratch_shapes=[pltpu.SemaphoreType.DMA((7,)), pltpu.SemaphoreType.DMA((7,))],
    )(c_all, w_ada_loc, b_ada_blocks)


def fwd_in_tile(step0, tile0, x, mod, w_in_ptt):
    xhat, _ = layer_norm_stats(x)
    h = xhat * (1.0 + mod[1:2]) + mod[0:1]
    return (mm_nt(h, w_in_ptt),)


def fwd_in_gather(x, mod, w_in_pt, shards):
    n = len(shards)
    n_rows = x.shape[0]
    ts = ROW_TILE
    n_tiles = n_rows // ts

    def body(x_ref, mod_ref, w_ref, *rest):
        s_refs = rest[:n]
        proj_ref, out_refs = rest[n], rest[n + 1:2 * n + 1]
        send_sems, recv_sems, local_sems = rest[2 * n + 1:]
        g = pl.program_id(0)
        me = my_position()
        mi = flat_index(me)

        def copies(k, slot):
            return [pltpu.make_async_remote_copy(
                src_ref=s_refs[a], dst_ref=out_refs[a].at[slot], send_sem=send_sems.at[7 * a + k - 1],
                recv_sem=recv_sems.at[7 * a + k - 1], device_id=flip(me, k), device_id_type=MESH_IDS)
                for a in range(n)]

        local = [pltpu.make_async_copy(s_refs[a], out_refs[a].at[mi], local_sems.at[a]) for a in range(n)]

        @pl.when(g == 0)
        def _():
            for cp in local:
                cp.start()
            for k in range(1, N_DEV):
                for cp in copies(k, mi):
                    cp.start()

        proj_ref[...] = fwd_in_tile(None, None, x_ref[...], mod_ref[...], w_ref[...])[0].astype(BF16)

        @pl.when(g == n_tiles - 1)
        def _():
            for k in range(1, N_DEV):
                for cp in copies(k, flat_index(flip(me, k))):
                    cp.wait_recv()
            for k in range(1, N_DEV):
                for cp in copies(k, mi):
                    cp.wait_send()
            for cp in local:
                cp.wait()

    hbm = pl.BlockSpec(memory_space=pl.ANY)
    const = pl.BlockSpec(memory_space=pltpu.VMEM)
    return pl.pallas_call(
        body, name="fwd_in_gather", grid=(n_tiles,),
        in_specs=[pl.BlockSpec((ts, D_MODEL), lambda g: (g, 0)), const, const] + [hbm] * n,
        out_specs=[pl.BlockSpec((ts, P_WIDTH), lambda g: (g, 0))] + [hbm] * n,
        out_shape=[jax.ShapeDtypeStruct((n_rows, P_WIDTH), BF16)]
        + [jax.ShapeDtypeStruct((N_DEV,) + s.shape, s.dtype) for s in shards],
        scratch_shapes=[pltpu.SemaphoreType.DMA((7 * n,)), pltpu.SemaphoreType.DMA((7 * n,)),
                        pltpu.SemaphoreType.DMA((n,))],
        compiler_params=pltpu.CompilerParams(dimension_semantics=("arbitrary",), vmem_limit_bytes=VMEM_LIMIT),
    )(x, mod, w_in_pt, *shards)


def rms_norm_fwd(x, g):
    r = lax.rsqrt(rowmean(x * x) + RMS_EPS)
    xh = x * r
    return xh * g, xh, r


def key_rope_mask(shape):
    return (lane_iota(shape) >= NOPE).astype(F32)


def mla_prep_tile(step0, tile0, q_c, kv_c, kr, krr, cos, sin, gq, gkv, wq, wqr, wkn, wv):
    qn, _, _ = rms_norm_fwd(q_c, gq)
    kvn, _, _ = rms_norm_fwd(kv_c, gkv)
    q = (mm(qn, wq) * tile_lanes(cos, HEADS) + mm(qn, wqr) * tile_lanes(sin, HEADS)) * Q_PRESCALE
    kpe = kr * (cos * key_rope_mask(cos.shape)) + krr * sin
    k = mm(kvn, wkn) + tile_lanes(kpe, HEADS)
    v = mm(kvn, wv)
    return q, k, v


def rwkv_prep_core(tile0, r0, k0, v0, l0, hr, hk, hv, hl, mu_r, mu_k, mu_v, mu_l, w0, a0, k_k, k_a,
                   w_dec, w_iclr, tril, same, bd):
    def shifted(x, halo, mu):
        row0 = jnp.where(tile0, 0.0, halo[HALO_ROWS - 1:HALO_ROWS, :])
        prev = shift_rows_down(x, row0)
        return x + (prev - x) * mu, prev

    ur, pr = shifted(r0, hr, mu_r)
    uk, pk = shifted(k0, hk, mu_k)
    uv, pv = shifted(v0, hv, mu_v)
    ul, plo = shifted(l0, hl, mu_l)
    th = jnp.tanh(ul)
    sg = sigmoid(w0 + mm(th, w_dec))
    lw = -DECAY_SCALE * sg
    a_ic = sigmoid(a0 + mm(ul, w_iclr))
    kkraw = uk * k_k
    nrm_raw = jnp.sqrt(head_sum(kkraw * kkraw, bd))
    nrm = jnp.maximum(nrm_raw, 1e-12)
    kk = kkraw / nrm
    k2 = uk * (1.0 + (a_ic - 1.0) * k_a)
    lc = ones_dot(tril, lw, 3)
    lcl = ones_dot(same, lw, 3)
    return dict(ur=ur, uk=uk, uv=uv, ul=ul, pr=pr, pk=pk, pv=pv, pl=plo, th=th, sg=sg, lw=lw, a_ic=a_ic,
                kkraw=kkraw, nrm_raw=nrm_raw, nrm=nrm, kk=kk, k2=k2, lc=lc, lcl=lcl)


def rwkv_prep_tile(step0, tile0, r0, k0, v0, l0, hr, hk, hv, hl, *consts):
    f = rwkv_prep_core(tile0, r0, k0, v0, l0, hr, hk, hv, hl, *consts)
    lc, lw = f["lc"], f["lw"]
    e_neg = jnp.exp(-lc)
    rt = f["ur"] * jnp.exp(lc)
    at = -f["kk"] * jnp.exp(lc - lw)
    bt = f["kk"] * f["a_ic"] * e_neg
    kt = f["k2"] * e_neg
    return rt, at, bt, kt, jnp.exp(f["lcl"]), f["uv"], f["ur"], f["k2"]


def wkv_masks():
    lane = lane_iota((1, PAIR))
    m_lo = (lane < HEAD).astype(F32)
    r2 = row_iota((PAIR, PAIR))
    c2 = lane_iota((PAIR, PAIR))
    bd = ((r2 < HEAD) == (c2 < HEAD)).astype(F32)
    eye2 = (r2 == c2).astype(F32)
    eye = (row_iota((CHUNK, CHUNK)) == lane_iota((CHUNK, CHUNK))).astype(F32)
    t_idx = row_iota((4 * CHUNK, PAIR)) % CHUNK
    s_idx = lane_iota((4 * CHUNK, PAIR)) % CHUNK
    keep = s_idx < t_idx + (row_iota((4 * CHUNK, PAIR)) >= 2 * CHUNK).astype(jnp.int32)
    return (m_lo, 1.0 - m_lo), keep, eye, bd, eye2


def rows(*parts):
    return jnp.concatenate(parts, axis=0)


def lanes(*parts):
    return jnp.concatenate(parts, axis=1)


def head_rows(x, ms):
    return rows(x * ms[0], x * ms[1])


def wkv_score_stack(at, rt, ms):
    return rows(head_rows(at, ms), head_rows(rt, ms))


def wkv_chunks_pre(chunks, masks, between_stages=lambda: None):
    ms, keep, eye, bd, eye2 = masks
    n = len(chunks)
    at, bt, kt, rt, v, cl = (list(t) for t in zip(*chunks))
    scores = [jnp.where(keep, mm_nt(wkv_score_stack(a, r, ms), rows(b, k)), 0.0)
              for a, r, b, k in zip(at, rt, bt, kt)]
    between_stages()
    q = CHUNK
    aab = [s[h * q:(h + 1) * q, :q] for s in scores for h in range(2)]
    tinv = [eye + a for a in aab]
    power = [mm(a, a) for a in aab]
    between_stages()
    for _ in range(5):
        both = [mm(rows(t, p), p) for t, p in zip(tinv, power)]
        tinv = [t + x[:q] for t, x in zip(tinv, both)]
        power = [x[q:] for x in both]
        between_stages()
    pair = lambda c, row0, col0: lanes(scores[c][row0:row0 + q, col0:col0 + q],
                                       scores[c][row0 + q:row0 + 2 * q, col0:col0 + q])
    tinv_p = [lanes(tinv[2 * c], tinv[2 * c + 1]) for c in range(n)]
    aak_p = [pair(c, 0, q) for c in range(n)]
    prb_p = [pair(c, 2 * q, 0) for c in range(n)]
    prk_p = [pair(c, 2 * q, q) for c in range(n)]
    v_rows = [head_rows(x, ms) for x in v]
    wy = [mm(rows(a, p), x) for a, p, x in zip(aak_p, prk_p, v_rows)]
    between_stages()
    w = [x[:q] for x in wy]
    yh2 = [x[q:] for x in wy]
    aw = [mm(t, lanes(head_rows(a, ms), head_rows(w_, ms))) for t, a, w_ in zip(tinv_p, at, w)]
    between_stages()
    ah = [x[:, :PAIR] for x in aw]
    wh = [x[:, PAIR:] for x in aw]
    ry = [mm(p, lanes(head_rows(a, ms), head_rows(w_, ms))) for p, a, w_ in zip(prb_p, ah, wh)]
    between_stages()
    rh = [r + x[:, :PAIR] for r, x in zip(rt, ry)]
    yh = [x[:, PAIR:] + y for x, y in zip(ry, yh2)]
    bc = [b * c_ for b, c_ in zip(bt, cl)]
    kc = [k * c_ for k, c_ in zip(kt, cl)]
    gh = [mm_tn(b, lanes(a, w_)) for b, a, w_ in zip(bc, ah, wh)]
    g = [eye2 * c_ + bd * x[:, :PAIR] for c_, x in zip(cl, gh)]
    h = [bd * (x[:, PAIR:] + mm_tn(k, v_)) for x, k, v_ in zip(gh, kc, v)]
    as_bf16 = lambda xs: [x.astype(BF16) for x in xs]
    saved = (as_bf16(tinv_p), as_bf16(aak_p), as_bf16(prb_p), as_bf16(prk_p), as_bf16(ah), wh)
    return g, h, rh, yh, saved


def wkv_chunks_grad(chunks, saved, m0, dy, dm1, masks, between_stages=lambda: None):
    ms, keep, eye, bd, eye2 = masks
    n = len(chunks)
    q = CHUNK
    at, bt, kt, rt, v, cl = (list(t) for t in zip(*chunks))
    tinv_p, aak_p, prb_p, prk_p, ah, wh = (list(t) for t in zip(*saved))
    head_stack = lambda p: rows(p[:, :q], p[:, q:])
    bc = [b * c_ for b, c_ in zip(bt, cl)]
    kc = [k * c_ for k, c_ in zip(kt, cl)]
    u = [mm(a, m) + w for a, m, w in zip(ah, m0, wh)]
    between_stages()
    dm1 = [d * bd for d in dm1]
    from_state = [mm(rows(b, k), d) for b, k, d in zip(bc, kc, dm1)]
    between_stages()
    dy_rows = [head_rows(d, ms) for d in dy]
    from_out = [mm_tn(lanes(head_stack(pb), head_stack(pk)), d) for pb, pk, d in zip(prb_p, prk_p, dy_rows)]
    between_stages()
    du = [a[:q] + b[:q] for a, b in zip(from_state, from_out)]
    dv = [a[q:] + b[q:] for a, b in zip(from_state, from_out)]
    dz = [mm_tn(head_stack(t), head_rows(d, ms)) for t, d in zip(tinv_p, du)]
    between_stages()
    dz_rows = [head_rows(d, ms) for d in dz]
    dv = [a + mm_tn(head_stack(k), d) for a, k, d in zip(dv, aak_p, dz_rows)]
    between_stages()
    by_m0 = [mm_nt(rows(d, z), m) for d, z, m in zip(dy, dz, m0)]
    between_stages()
    uv = [rows(x, y) for x, y in zip(u, v)]
    by_dm1 = [mm_nt(x, d) for x, d in zip(uv, dm1)]
    between_stages()
    udm = [x[:q] for x in by_dm1]
    vdm = [x[q:] for x in by_dm1]
    dscores = [jnp.where(keep, mm_nt(rows(z, d), x), 0.0) for z, d, x in zip(dz_rows, dy_rows, uv)]
    between_stages()
    to_ar = [mm(d, rows(b, k)) for d, b, k in zip(dscores, bt, kt)]
    to_bk = [mm_tn(d, wkv_score_stack(a, r, ms)) for d, a, r in zip(dscores, at, rt)]
    ones = jnp.ones((8, PAIR), F32)
    upper = (lane_iota((CHUNK, CHUNK)) >= row_iota((CHUNK, CHUNK))).astype(F32)
    out = []
    for c in range(n):
        e = to_ar[c]
        dat_c = by_m0[c][q:] + e[:q] * ms[0] + e[q:2 * q] * ms[1]
        drt_c = by_m0[c][:q] + e[2 * q:3 * q] * ms[0] + e[3 * q:] * ms[1]
        dbt_c = udm[c] * cl[c] + to_bk[c][:q]
        dkt_c = vdm[c] * cl[c] + to_bk[c][q:]
        dlcl = ones_dot_nt(ones, dm1[c] * m0[c], 3)[0:1, :] * cl[c] + colsum(bc[c] * udm[c] + kc[c] * vdm[c])
        g = drt_c * rt[c] - dbt_c * bt[c] - dkt_c * kt[c] + dat_c * at[c]
        dlw = ones_dot(upper, g, 3) - dat_c * at[c] + dlcl
        out.append((dat_c, dbt_c, dkt_c, drt_c, dv[c], dlw))
    return out


def wkv_forward(at, bt, kt, rt, v, clf):
    n_rows = at.shape[0]
    cps = WKV_CHUNKS_PER_STEP
    rb = cps * CHUNK
    n_steps = n_rows // rb

    def body(a_ref, b_ref, k_ref, r_ref, v_ref, c_ref, y_ref, m0_ref, g_ref, rh_ref, *rest):
        saved_refs, m_scr = rest[:6], rest[6]

        @pl.when(pl.program_id(1) == 0)
        def _():
            m_scr[...] = jnp.zeros_like(m_scr)

        masks = wkv_masks()
        chunks = []
        for cc in range(cps):
            sl = slice(cc * CHUNK, (cc + 1) * CHUNK)
            chunks.append((a_ref[sl, :], b_ref[sl, :], k_ref[sl, :], r_ref[sl, :], v_ref[sl, :],
                           c_ref[cc * CHUNK:cc * CHUNK + 1, :]))
        state = [m_scr[...]]
        pending = []

        def chain_step():
            if not pending:
                return
            cc, g, h, rh, yh = pending.pop(0)
            sl = slice(cc * CHUNK, (cc + 1) * CHUNK)
            m = state[0]
            m0_ref[0, cc] = m
            g_ref[0, cc] = g
            rh_ref[sl, :] = rh
            y_ref[sl, :] = hdot(rh, m) + yh
            state[0] = hdot(g, m) + h

        def prepare(first, last, between_stages):
            gs, hs, rhs, yhs, saved = wkv_chunks_pre(chunks[first:last], masks, between_stages)
            for ref, per_chunk in zip(saved_refs, saved):
                for cc, val in enumerate(per_chunk, start=first):
                    ref[cc * CHUNK:(cc + 1) * CHUNK, :] = val
            pending.extend(zip(range(first, last), gs, hs, rhs, yhs))

        group = cps // WKV_CHAIN_GROUPS
        for first in range(0, cps, group):
            prepare(first, first + group, chain_step)
        while pending:
            chain_step()
        m_scr[...] = state[0]

    blk = pl.BlockSpec((rb, PAIR), lambda p, s: (s, p))
    state_blk = pl.BlockSpec((1, cps, PAIR, PAIR), lambda p, s: (p, s, 0, 0))
    state_shape = jax.ShapeDtypeStruct((WIDTH // PAIR, n_rows // CHUNK, PAIR, PAIR), F32)
    rows_f32 = jax.ShapeDtypeStruct((n_rows, WIDTH), F32)
    rows_bf16 = jax.ShapeDtypeStruct((n_rows, WIDTH), BF16)
    return pl.pallas_call(
        body, name="wkv_forward", grid=(WIDTH // PAIR, n_steps),
        in_specs=[blk] * 6,
        out_specs=[blk, state_blk, state_blk, blk] + [blk] * 6,
        out_shape=[rows_f32, state_shape, state_shape, rows_f32] + [rows_bf16] * 5 + [rows_f32],
        scratch_shapes=[pltpu.VMEM((PAIR, PAIR), F32)],
        compiler_params=pltpu.CompilerParams(dimension_semantics=("arbitrary", "arbitrary"),
                                             vmem_limit_bytes=VMEM_LIMIT),
    )(at, bt, kt, rt, v, clf)


def wkv_backward(at, bt, kt, rt, v, clf, m0s, gs, rh, saved, dy):
    n_rows = at.shape[0]
    cps = WKV_CHUNKS_PER_STEP
    rb = cps * CHUNK
    n_steps = n_rows // rb

    def body(a_ref, b_ref, k_ref, r_ref, v_ref, c_ref, m0_ref, g_ref, rh_ref, *rest):
        saved_refs, dy_ref = rest[:6], rest[6]
        da_ref, db_ref, dk_ref, dr_ref, dv_ref, dlw_ref, dm_scr = rest[7:]

        @pl.when(pl.program_id(1) == 0)
        def _():
            dm_scr[...] = jnp.zeros_like(dm_scr)

        masks = wkv_masks()
        bd = masks[3]
        state = [dm_scr[...]]
        dm1 = [None] * cps
        todo = list(reversed(range(cps)))

        def chain_step():
            if not todo:
                return
            cc = todo.pop(0)
            sl = slice(cc * CHUNK, (cc + 1) * CHUNK)
            dm1[cc] = state[0]
            state[0] = bd * (hdot_tn(g_ref[0, cc], state[0]) + hdot_tn(rh_ref[sl, :], dy_ref[sl, :]))

        def gradients(first, last, between_stages):
            chunks, kept, m0, dys = [], [], [], []
            for cc in range(first, last):
                sl = slice(cc * CHUNK, (cc + 1) * CHUNK)
                chunks.append((a_ref[sl, :], b_ref[sl, :], k_ref[sl, :], r_ref[sl, :], v_ref[sl, :],
                               c_ref[cc * CHUNK:cc * CHUNK + 1, :]))
                kept.append(tuple(ref[sl, :] for ref in saved_refs))
                m0.append(m0_ref[0, cc])
                dys.append(dy_ref[sl, :])
            grads = wkv_chunks_grad(chunks, kept, m0, dys, dm1[first:last], masks, between_stages)
            for cc, (dat, dbt, dkt, drt, dv, dlw) in enumerate(grads, start=first):
                sl = slice(cc * CHUNK, (cc + 1) * CHUNK)
                da_ref[sl, :] = dat
                db_ref[sl, :] = dbt
                dk_ref[sl, :] = dkt
                dr_ref[sl, :] = drt
                dv_ref[sl, :] = dv
                dlw_ref[sl, :] = dlw

        group = cps // WKV_CHAIN_GROUPS
        for first in reversed(range(0, cps, group)):
            while todo and todo[0] >= first:
                chain_step()
            gradients(first, first + group, chain_step)
        dm_scr[...] = state[0]

    blk = pl.BlockSpec((rb, PAIR), lambda p, s: (n_steps - 1 - s, p))
    state_blk = pl.BlockSpec((1, cps, PAIR, PAIR), lambda p, s: (p, n_steps - 1 - s, 0, 0))
    return pl.pallas_call(
        body, name="wkv_backward", grid=(WIDTH // PAIR, n_steps),
        in_specs=[blk] * 6 + [state_blk, state_blk, blk] + [blk] * 6 + [blk],
        out_specs=[blk] * 6,
        out_shape=[jax.ShapeDtypeStruct((n_rows, WIDTH), F32)] * 6,
        scratch_shapes=[pltpu.VMEM((PAIR, PAIR), F32)],
        compiler_params=pltpu.CompilerParams(dimension_semantics=("arbitrary", "arbitrary"),
                                             vmem_limit_bytes=VMEM_LIMIT),
    )(at, bt, kt, rt, v, clf, m0s, gs, rh, *saved, dy)


def visible(q_row0, k_row0, shape):
    qc = (q_row0 + row_iota(shape)) // CHUNK
    kc = (k_row0 + lane_iota(shape)) // CHUNK
    return kc <= qc


def attention_forward(q, k, v):
    n_rows = q.shape[0]
    tq, tk = ATTN_FWD_TILES
    n_q = n_rows // tq
    assert tk % tq == 0

    def body(q_ref, k_ref, v_ref, o_ref, lse_ref):
        i = pl.program_id(1)
        lane = lane_iota((tq, LANE))
        heads = [slice(0, LANE), slice(LANE, 2 * LANE)]
        qs = [q_ref[:, cols] for cols in heads]

        def step(j, carry, size, masked):
            rows = pl.ds(pl.multiple_of(j * size, size), size)
            ss = [mm_nt(qh, k_ref[rows, cols]) for qh, cols in zip(qs, heads)]
            if masked:
                vis = visible(i * tq, j * size, ss[0].shape)
                ss = [jnp.where(vis, s, -jnp.inf) for s in ss]
            ps, stats = [], []
            for s, (m, l, _) in zip(ss, carry):
                m_new = jnp.maximum(m, jnp.max(s, axis=-1, keepdims=True))
                p = jnp.exp2(s - m_new)
                alpha = jnp.exp2(m - m_new)
                ps.append(p)
                stats.append((m_new, alpha, alpha * l + jnp.sum(p, axis=-1, keepdims=True)))
            pvs = [mm(p, v_ref[rows, cols]) for p, cols in zip(ps, heads)]
            return tuple((m_new, l, alpha * acc + pv)
                         for (m_new, alpha, l), (_, _, acc), pv in zip(stats, carry, pvs))

        carry = tuple((jnp.full((tq, 1), -jnp.inf, F32), jnp.zeros((tq, 1), F32), jnp.zeros((tq, LANE), F32))
                      for _ in heads)
        n_full = (i * tq) // tk
        carry = lax.fori_loop(0, n_full, functools.partial(step, size=tk, masked=False), carry)
        (m0, l0, acc0), (m1, l1, acc1) = step(n_full, carry, size=tk, masked=True)
        o_ref[...] = acc0 / l0 + acc1 / l1
        lse_ref[...] = jnp.where(lane >= HEAD, m1 + jnp.log2(l1), m0 + jnp.log2(l0))

    return pl.pallas_call(
        body, name="attention_forward", grid=(HEADS // 2, n_q),
        in_specs=[pl.BlockSpec((tq, 2 * LANE), lambda p, i: (i, p)),
                  pl.BlockSpec((n_rows, 2 * LANE), lambda p, i: (0, p)),
                  pl.BlockSpec((n_rows, 2 * LANE), lambda p, i: (0, p))],
        out_specs=[pl.BlockSpec((tq, LANE), lambda p, i: (i, p))] * 2,
        out_shape=[jax.ShapeDtypeStruct((n_rows, WIDTH), F32)] * 2,
        compiler_params=pltpu.CompilerParams(dimension_semantics=("arbitrary", "arbitrary"),
                                             vmem_limit_bytes=VMEM_LIMIT),
    )(q, k, v)


def block_exchange(g_refs, rg_refs, send_sems, recv_sems, local_sems):
    n = len(g_refs)
    me = my_position()
    mi = flat_index(me)

    def copies(k, src_index, dst_index):
        return [pltpu.make_async_remote_copy(
            src_ref=g_refs[a].at[src_index], dst_ref=rg_refs[a].at[dst_index],
            send_sem=send_sems.at[7 * a + k - 1], recv_sem=recv_sems.at[7 * a + k - 1],
            device_id=flip(me, k), device_id_type=MESH_IDS) for a in range(n)]

    local = [pltpu.make_async_copy(g_refs[a].at[mi], rg_refs[a].at[mi], local_sems.at[a]) for a in range(n)]

    def start():
        for cp in local:
            cp.start()
        for k in range(1, N_DEV):
            for cp in copies(k, flat_index(flip(me, k)), mi):
                cp.start()

    def wait():
        for k in range(1, N_DEV):
            pi = flat_index(flip(me, k))
            for cp in copies(k, pi, pi):
                cp.wait_recv()
        for k in range(1, N_DEV):
            for cp in copies(k, flat_index(flip(me, k)), mi):
                cp.wait_send()
        for cp in local:
            cp.wait()

    return start, wait


def attention_backward(q, k, v, o, do, lse, riders):
    n_rows = q.shape[0]
    tq, tk = ATTN_BWD_TILES
    n_q = n_rows // tq
    n_k = n_rows // tk
    n_masked = max(1, tk // tq)
    n_r = len(riders)

    def body(q_ref, k_ref, v_ref, o_ref, do_ref, lse_ref, *rest):
        g_refs = rest[:n_r]
        dq_ref, dk_ref, dv_ref = rest[n_r:n_r + 3]
        rg_refs = rest[n_r + 3:2 * n_r + 3]
        start_riders, wait_riders = block_exchange(g_refs, rg_refs, *rest[2 * n_r + 3:])
        j = pl.program_id(1)

        @pl.when(jnp.logical_and(pl.program_id(0) == 0, j == 0))
        def _():
            start_riders()

        @pl.when(j == 0)
        def _():
            dq_ref[...] = jnp.zeros_like(dq_ref)

        lane = lane_iota((tq, LANE))
        heads = [slice(0, LANE), slice(LANE, 2 * LANE)]
        ks = [k_ref[:, cols] for cols in heads]
        vs = [v_ref[:, cols] for cols in heads]
        head_lanes = [(lane < HEAD).astype(F32), (lane >= HEAD).astype(F32)]

        def step(i, carry, masked):
            rows = pl.ds(pl.multiple_of(i * tq, tq), tq)
            qs = [q_ref[rows, cols] for cols in heads]
            dout = do_ref[rows, :]
            dout_o = dout * o_ref[rows, :]
            lse_t = lse_ref[rows, :]
            ss = [mm_nt(qh, kh) for qh, kh in zip(qs, ks)]
            dps = [mm_nt(dout, vh) for vh in vs]
            ps, dss = [], []
            for hh in range(2):
                delta = jnp.sum(dout_o * head_lanes[hh], axis=-1, keepdims=True)
                lse_h = jnp.sum(jnp.where(lane == hh * HEAD, lse_t, 0.0), axis=-1, keepdims=True)
                p = jnp.exp2(ss[hh] - lse_h)
                if masked:
                    p = jnp.where(visible(i * tq, j * tk, p.shape), p, 0.0)
                ps.append(p)
                dss.append(p * (dps[hh] - delta))
            dvs = [mm_tn(p, dout) for p in ps]
            dqs = [mm(ds, kh) for ds, kh in zip(dss, ks)]
            dks = [mm_tn(ds, qh) for ds, qh in zip(dss, qs)]
            for cols, dq in zip(heads, dqs):
                dq_ref[rows, cols] += dq * ATTN_SCALE
            return tuple((dk + a, dv + b) for (dk, dv), a, b in zip(carry, dks, dvs))

        carry = tuple((jnp.zeros((tk, LANE), F32), jnp.zeros((tk, LANE), F32)) for _ in heads)
        i_first = (j * tk) // tq
        for extra in range(n_masked):
            carry = step(i_first + extra, carry, masked=True)
        carry = lax.fori_loop(i_first + n_masked, n_q, functools.partial(step, masked=False), carry)
        for cols, (dk, dv) in zip(heads, carry):
            dk_ref[:, cols] = dk * (1.0 / LOG2_E)
            dv_ref[:, cols] = dv

        @pl.when(jnp.logical_and(pl.program_id(0) == HEADS // 2 - 1, j == n_k - 1))
        def _():
            wait_riders()

    full = lambda w: pl.BlockSpec((n_rows, w), lambda p, j: (0, p))
    blk = pl.BlockSpec((tk, 2 * LANE), lambda p, j: (j, p))
    hbm = pl.BlockSpec(memory_space=pl.ANY)
    return pl.pallas_call(
        body, name="attention_backward", grid=(HEADS // 2, n_k),
        in_specs=[full(2 * LANE), blk, blk, full(LANE), full(LANE), full(LANE)] + [hbm] * n_r,
        out_specs=[full(2 * LANE), blk, blk] + [hbm] * n_r,
        out_shape=[jax.ShapeDtypeStruct((n_rows, HEADS * LANE), F32)] * 3
        + [jax.ShapeDtypeStruct(r.shape, r.dtype) for r in riders],
        scratch_shapes=[pltpu.SemaphoreType.DMA((7 * n_r,)), pltpu.SemaphoreType.DMA((7 * n_r,)),
                        pltpu.SemaphoreType.DMA((n_r,))],
        compiler_params=pltpu.CompilerParams(dimension_semantics=("arbitrary", "arbitrary"),
                                             vmem_limit_bytes=VMEM_LIMIT),
    )(q, k, v, o, do, lse, *riders)


def tail_tile(step0, tile0, x, tgt, ma, mb, gpa, gpb, ya, y, ur, k2, uv,
              mod, wpa, wpb, wout, gn_g, gn_b, r_k, post_g, post_b, bd):
    gate = mod[2:3]
    inv = 1.0 / HEAD
    yc = y - head_sum(y, bd) * inv
    rs = lax.rsqrt(head_sum(yc * yc, bd) * inv + GN_EPS)
    yn = yc * rs
    yb = yn * gn_g + gn_b + head_sum(ur * k2 * r_k, bd) * uv
    sga, sgb = sigmoid(gpa), sigmoid(gpb)
    sila, silb = gpa * sga, gpb * sgb
    ga, gb = ya * sila, yb * silb
    pa, pb = mm(ga, wpa), mm(gb, wpb)
    sa, sb = sigmoid(ma), sigmoid(mb)
    merged = sa * pa + sb * pb
    sub = mm(merged, wout)
    z = ALPHA * x + (1.0 + gate) * sub
    zhat, rstd = layer_norm_stats(z)
    err = zhat * post_g + post_b - tgt
    loss = 0.5 * jnp.sum(rowmean(err * err), axis=0, keepdims=True) + jnp.zeros((1, LANE), F32)
    dout = err * (1.0 / D_MODEL)
    dpost_g = colsum(dout * zhat)
    dpost_b = colsum(dout)
    dz = layer_norm_bwd(dout * post_g, zhat, rstd)
    dgate = colsum(dz * sub)
    dsub = dz * (1.0 + gate)
    dwout = mm_tn(merged, dsub)
    dmerged = mm_nt(dsub, wout)
    dpa, dpb = dmerged * sa, dmerged * sb
    dma = dmerged * pa * sa * (1.0 - sa)
    dmb = dmerged * pb * sb * (1.0 - sb)
    dwpa = mm_tn(ga, dpa)
    dwpb = mm_tn(gb, dpb)
    dga = mm_nt(dpa, wpa)
    dgb = mm_nt(dpb, wpb)
    dya = dga * sila
    dgpa = dga * ya * (sga * (1.0 + gpa * (1.0 - sga)))
    dyb = dgb * silb
    dgpb = dgb * yb * (sgb * (1.0 + gpb * (1.0 - sgb)))
    dgn_g = colsum(dyb * yn)
    dgn_b = colsum(dyb)
    dyn = dyb * gn_g
    dy = rs * (dyn - head_sum(dyn, bd) * inv - yn * head_sum(dyn * yn, bd) * inv)
    return (dz, dma, dmb, dgpa, dgpb, dya, dy, dyb,
            loss, dwout, dwpa, dwpb, dgn_g, dgn_b, dpost_g, dpost_b, dgate)


def mla_prep_bwd_tile(step0, tile0, q_c, kv_c, cos, sin, dq, dk, dv, gq, gkv, wq, wqr, wkn, wv):
    qn, qh, rq = rms_norm_fwd(q_c, gq)
    kvn, kvh, rkv = rms_norm_fwd(kv_c, gkv)
    dqc = dq * tile_lanes(cos, HEADS)
    dqs = dq * tile_lanes(sin, HEADS)
    dqn = mm_nt(dqc, wq) + mm_nt(dqs, wqr)
    dkvn = mm_nt(dk, wkn) + mm_nt(dv, wv)
    dkpe = dk[:, 0:LANE]
    for h in range(1, HEADS):
        dkpe = dkpe + dk[:, h * LANE:(h + 1) * LANE]
    dkr = dkpe * (cos * key_rope_mask(cos.shape))
    dkrr = dkpe * sin

    def rms_bwd(dyv, xh, r, g):
        dyg = dyv * g
        return r * (dyg - xh * rowmean(dyg * xh)), colsum(dyv * xh)

    dq_c, dgq = rms_bwd(dqn, qh, rq, gq)
    dkv_c, dgkv = rms_bwd(dkvn, kvh, rkv, gkv)
    return (dq_c, dkv_c, dkr, dkrr,
            mm_tn(qn, dqc), mm_tn(qn, dqs), mm_tn(kvn, dk), mm_tn(kvn, dv), dgq, dgkv)


def rwkv_prep_bwd_tile(step0, tile0, r0, k0, v0, l0, drt, dat, dbt, dkt, dvv, dlw, dyb, hr, hk, hv, hl,
                       mu_r, mu_k, mu_v, mu_l, w0, a0, k_k, k_a, w_dec, w_iclr, tril, same, bd, r_k,
                       cr, ck, cv, cl_):
    f = rwkv_prep_core(tile0, r0, k0, v0, l0, hr, hk, hv, hl, mu_r, mu_k, mu_v, mu_l, w0, a0, k_k, k_a,
                       w_dec, w_iclr, tril, same, bd)
    ur, uk, uv, ul, kk, k2, a_ic, sg, th = (f[n] for n in ("ur", "uk", "uv", "ul", "kk", "k2", "a_ic", "sg", "th"))
    lc, lw = f["lc"], f["lw"]
    e_neg = jnp.exp(-lc)
    dur = drt * jnp.exp(lc)
    da = dat * jnp.exp(lc - lw)
    db = dbt * e_neg
    dk2 = dkt * e_neg
    s = head_sum(ur * k2 * r_k, bd)
    duv = dvv + dyb * s
    ds = head_sum(dyb * uv, bd)
    dur = dur + ds * k2 * r_k
    dk2 = dk2 + ds * ur * r_k
    dr_k = colsum(ds * ur * k2)
    dkk = db * a_ic - da
    da_ic = db * kk + dk2 * uk * k_a
    duk = dk2 * (1.0 + (a_ic - 1.0) * k_a)
    dk_a = colsum(dk2 * uk * (a_ic - 1.0))
    dkkraw = jnp.where(f["nrm_raw"] > 1e-12, (dkk - kk * head_sum(dkk * kk, bd)) / f["nrm"], dkk * 1e12)
    duk = duk + dkkraw * k_k
    dk_k = colsum(dkkraw * uk)
    dai = da_ic * a_ic * (1.0 - a_ic)
    dd = dlw * (-DECAY_SCALE) * sg * (1.0 - sg)
    dul = mm_nt(dai, w_iclr) + mm_nt(dd, w_dec) * (1.0 - th * th)

    def unshift(du, x, prev, mu, carry_row):
        nxt = shift_rows_up(du, carry_row)
        return du * (1.0 - mu) + nxt * mu, colsum(du * (prev - x)), du[0:1, :]

    dr0, dmu_r, ncr = unshift(dur, r0, f["pr"], mu_r, cr)
    dk0, dmu_k, nck = unshift(duk, k0, f["pk"], mu_k, ck)
    dv0, dmu_v, ncv = unshift(duv, v0, f["pv"], mu_v, cv)
    dl0, dmu_l, ncl = unshift(dul, l0, f["pl"], mu_l, cl_)
    return (dr0, dk0, dv0, dl0,
            dmu_r, dmu_k, dmu_v, dmu_l, colsum(dd), colsum(dai), dk_k, dk_a, dr_k, mm_tn(th, dd), mm_tn(ul, dai),
            ncr, nck, ncv, ncl)


def in_backward(x, dz, pieces, mod, w_in_pt, unrot):
    n_rows = x.shape[0]
    ts = ROW_TILE
    n_p = len(pieces)
    shard_cols = IN_WIDTH // N_DEV

    def body(*refs):
        x_ref, dz_ref = refs[:2]
        p_refs = refs[2:2 + n_p]
        mod_ref, w_ref, unrot_ref = refs[2 + n_p:5 + n_p]
        dx_ref, ht_ref, blocks_ref, dshift_ref, dscale_ref = refs[5 + n_p:]
        step0 = pl.program_id(0) == 0
        dma, dmb, dr0, dk0, dv0, dgpa, dgpb, dq_c, dkv_c, dkr, dkrr, dl0 = (r[...] for r in p_refs)
        dproj = jnp.concatenate([dma, dmb, dr0, dk0, dv0, dgpa, dgpb, dq_c, dkv_c, dkr, dkrr, dl0], axis=1)
        dh = mm(dproj, w_ref[...])
        xhat, rstd = layer_norm_stats(x_ref[...])
        scale1 = 1.0 + mod_ref[1:2, :]
        dx_ref[...] = layer_norm_bwd(dh * scale1, xhat, rstd) + ALPHA * dz_ref[...]
        ht_ref[...] = jnp.transpose(xhat * scale1 + mod_ref[0:1, :]).astype(BF16)
        dkrope = (dkr.astype(F32) + mm(dkrr, unrot_ref[...]))[:, NOPE:QK_DIM]
        natural = jnp.concatenate(
            [dq_c.astype(F32), dkv_c.astype(F32), dkrope]
            + [p.astype(F32) for p in (dgpa, dr0, dk0, dv0, dl0, dgpb, dma, dmb)], axis=1)
        for j in range(N_DEV):
            blocks_ref[j] = natural[:, j * shard_cols:(j + 1) * shard_cols].astype(BF16)
        for ref, val in ((dshift_ref, colsum(dh)), (dscale_ref, colsum(dh * xhat))):
            @pl.when(step0)
            def _(ref=ref, val=val):
                ref[...] = val

            @pl.when(jnp.logical_not(step0))
            def _(ref=ref, val=val):
                ref[...] += val

    row = lambda w: pl.BlockSpec((ts, w), lambda i: (i, 0))
    const = pl.BlockSpec(memory_space=pltpu.VMEM)
    vec = pl.BlockSpec((1, D_MODEL), lambda i: (0, 0))
    return pl.pallas_call(
        body, name="in_backward", grid=(n_rows // ts,),
        in_specs=[row(D_MODEL), row(D_MODEL)] + [row(p.shape[1]) for p in pieces] + [const] * 3,
        out_specs=[row(D_MODEL), pl.BlockSpec((D_MODEL, ts), lambda i: (0, i)),
                   pl.BlockSpec((N_DEV, ts, shard_cols), lambda i: (0, i, 0)), vec, vec],
        out_shape=[jax.ShapeDtypeStruct((n_rows, D_MODEL), F32), jax.ShapeDtypeStruct((D_MODEL, n_rows), BF16),
                   jax.ShapeDtypeStruct((N_DEV, n_rows, shard_cols), BF16),
                   jax.ShapeDtypeStruct((1, D_MODEL), F32), jax.ShapeDtypeStruct((1, D_MODEL), F32)],
        compiler_params=pltpu.CompilerParams(dimension_semantics=("arbitrary",), vmem_limit_bytes=VMEM_LIMIT),
    )(x, dz, *pieces, mod, w_in_pt, unrot)


def in_weight_grad_exchange(h_t, dp_blocks, others, small, order):
    n = len(others)
    n_rows = h_t.shape[1]
    ts = 4 * ROW_TILE
    n_i = n_rows // ts
    shard_cols = dp_blocks.shape[2]
    n_chips = N_DEV // 2
    last = N_DEV - 1

    def body(order_ref, h_ref, dp_ref, *rest):
        g_refs, s_ref = rest[:n], rest[n]
        rwin_ref, rg_refs, rs_ref = rest[n + 1], rest[n + 2:2 * n + 2], rest[2 * n + 2]
        (acc, sendbuf, sib_buf, sib_send, sib_recv, win_send, win_recv,
         o_send, o_recv, local_sems) = rest[2 * n + 3:]
        b, i = pl.program_id(0), pl.program_id(1)
        me = my_position()
        mi = flat_index(me)
        sibling = (me[0], me[1], 1 - me[2])

        def other_copies(k, src_index, dst_index):
            peer = flip(me, k)
            out = [pltpu.make_async_remote_copy(
                src_ref=g_refs[a].at[src_index], dst_ref=rg_refs[a].at[dst_index],
                send_sem=o_send.at[(n + 1) * (k - 1) + a], recv_sem=o_recv.at[(n + 1) * (k - 1) + a],
                device_id=peer, device_id_type=MESH_IDS) for a in range(n)]
            out.append(pltpu.make_async_remote_copy(
                src_ref=s_ref, dst_ref=rs_ref.at[dst_index],
                send_sem=o_send.at[(n + 1) * (k - 1) + n], recv_sem=o_recv.at[(n + 1) * (k - 1) + n],
                device_id=peer, device_id_type=MESH_IDS))
            return out

        def local_copies():
            out = [pltpu.make_async_copy(g_refs[a].at[mi], rg_refs[a].at[mi], local_sems.at[a]) for a in range(n)]
            out.append(pltpu.make_async_copy(s_ref, rs_ref.at[mi], local_sems.at[n]))
            return out

        def to_sibling(t):
            return pltpu.make_async_remote_copy(
                src_ref=sendbuf.at[t], dst_ref=sib_buf.at[t], send_sem=sib_send.at[t], recv_sem=sib_recv.at[t],
                device_id=sibling, device_id_type=MESH_IDS)

        def to_owner(t):
            flip_x = (t < 2) * 1
            flip_y = 1 - (t & 1)
            owner = (me[0] ^ flip_x, me[1] ^ flip_y, me[2])
            return pltpu.make_async_remote_copy(
                src_ref=sendbuf.at[n_chips + t], dst_ref=rwin_ref.at[t], send_sem=win_send.at[t],
                recv_sem=win_recv.at[t], device_id=owner, device_id_type=MESH_IDS)

        own_block = pltpu.make_async_copy(sendbuf.at[last], rwin_ref.at[n_chips - 1], local_sems.at[n + 1])

        @pl.when(jnp.logical_and(b == 0, i == 0))
        def _():
            for cp in local_copies():
                cp.start()
            for k in range(1, N_DEV):
                for cp in other_copies(k, flat_index(flip(me, k)), mi):
                    cp.start()

        contrib = jnp.dot(h_ref[...], dp_ref[...], preferred_element_type=F32)

        @pl.when(i == 0)
        def _():
            acc[...] = contrib

        @pl.when(i > 0)
        def _():
            acc[...] += contrib

        slot = order_ref[N_DEV + b]
        t = slot & (n_chips - 1)

        @pl.when(jnp.logical_and(i == n_i - 1, slot < n_chips))
        def _():
            sendbuf[slot] = acc[...].astype(BF16)
            to_sibling(t).start()

        @pl.when(jnp.logical_and(i == n_i - 1, slot >= n_chips))
        def _():
            to_sibling(t).wait_recv()
            sendbuf[slot] = (acc[...] + sib_buf[t].astype(F32)).astype(BF16)

            @pl.when(slot < last)
            def _():
                to_owner(t).start()

            @pl.when(slot == last)
            def _():
                own_block.start()

        @pl.when(jnp.logical_and(b == last, i == n_i - 1))
        def _():
            for t in range(n_chips - 1):
                to_owner(t).wait_recv()
            for k in range(1, N_DEV):
                pi = flat_index(flip(me, k))
                for cp in other_copies(k, pi, pi):
                    cp.wait_recv()
            for t in range(n_chips):
                to_sibling(t).wait_send()
            for t in range(n_chips - 1):
                to_owner(t).wait_send()
            for k in range(1, N_DEV):
                for cp in other_copies(k, flat_index(flip(me, k)), mi):
                    cp.wait_send()
            for cp in local_copies():
                cp.wait()
            own_block.wait()

    hbm = pl.BlockSpec(memory_space=pl.ANY)
    n_sem = 7 * (n + 1)
    grid_spec = pltpu.PrefetchScalarGridSpec(
        num_scalar_prefetch=1, grid=(N_DEV, n_i),
        in_specs=[pl.BlockSpec((D_MODEL, ts), lambda b, i, order: (0, i)),
                  pl.BlockSpec((None, ts, shard_cols), lambda b, i, order: (order[b], i, 0))] + [hbm] * (n + 1),
        out_specs=[hbm] * (n + 2),
        scratch_shapes=[pltpu.VMEM((D_MODEL, shard_cols), F32), pltpu.VMEM((N_DEV, D_MODEL, shard_cols), BF16),
                        pltpu.VMEM((n_chips, D_MODEL, shard_cols), BF16),
                        pltpu.SemaphoreType.DMA((n_chips,)), pltpu.SemaphoreType.DMA((n_chips,)),
                        pltpu.SemaphoreType.DMA((n_chips - 1,)), pltpu.SemaphoreType.DMA((n_chips - 1,)),
                        pltpu.SemaphoreType.DMA((n_sem,)), pltpu.SemaphoreType.DMA((n_sem,)),
                        pltpu.SemaphoreType.DMA((n + 2,))])
    return pl.pallas_call(
        body, name="in_weight_grad_exchange", grid_spec=grid_spec,
        out_shape=[jax.ShapeDtypeStruct((n_chips, D_MODEL, shard_cols), BF16)]
        + [jax.ShapeDtypeStruct(o.shape, o.dtype) for o in others]
        + [jax.ShapeDtypeStruct((N_DEV,) + small.shape, small.dtype)],
        compiler_params=pltpu.CompilerParams(dimension_semantics=("arbitrary", "arbitrary"),
                                             vmem_limit_bytes=VMEM_LIMIT),
    )(order, h_t, dp_blocks, *others, small)


def ada_weight_grad(c_all, dmod_cols):
    def body(c_ref, d_ref, o_ref):
        cv = c_ref[...]
        o_ref[...] = hdot_tn(cv * sigmoid(cv), d_ref[...])

    return pl.pallas_call(
        body, name="ada_weight_grad",
        out_shape=jax.ShapeDtypeStruct((c_all.shape[1], dmod_cols.shape[1]), F32),
    )(c_all, dmod_cols)


def adamw_update(g, w, m, v):
    nm = ADAM_B1 * m + (1.0 - ADAM_B1) * g
    nv = ADAM_B2 * v + (1.0 - ADAM_B2) * (g * g)
    m_hat = nm / (1.0 - ADAM_B1 ** ADAM_STEP)
    v_hat = nv / (1.0 - ADAM_B2 ** ADAM_STEP)
    return -ADAM_LR * (m_hat / (jnp.sqrt(v_hat) + ADAM_EPS) + ADAM_WD * w), nm, nv


def adamw(parts, w, m, v, name):
    k, rows, cols = parts.shape

    def body(p_ref, w_hbm, m_hbm, v_hbm, g_ref, d_ref, nm_ref, nv_ref, w_buf, m_buf, v_buf, sems):
        loads = [pltpu.make_async_copy(src, dst, sems.at[i])
                 for i, (src, dst) in enumerate(((w_hbm, w_buf), (m_hbm, m_buf), (v_hbm, v_buf)))]
        for cp in loads:
            cp.start()
        g = p_ref[0].astype(F32)
        for i in range(1, k):
            g = g + p_ref[i].astype(F32)
        g_ref[0] = g
        for cp in loads:
            cp.wait()
        d_ref[0], nm_ref[0], nv_ref[0] = adamw_update(g, w_buf[0], m_buf[0], v_buf[0])

    hbm = pl.BlockSpec(memory_space=pl.ANY)
    whole = pl.BlockSpec(memory_space=pltpu.VMEM)
    return pl.pallas_call(
        body, name=name,
        in_specs=[whole, hbm, hbm, hbm], out_specs=[whole] * 4,
        out_shape=[jax.ShapeDtypeStruct((1, rows, cols), F32)] * 4,
        scratch_shapes=[pltpu.VMEM((1, rows, cols), F32)] * 3 + [pltpu.SemaphoreType.DMA((3,))],
        compiler_params=pltpu.CompilerParams(vmem_limit_bytes=VMEM_LIMIT),
    )(parts, w, m, v)


def adamw_small(parts, ws, ms, vs):
    k = parts.shape[0]
    n = len(ws)
    sizes = [w.shape[1] for w in ws]

    def body(p_ref, *refs):
        ins, outs = refs[:3 * n], refs[3 * n:]
        g_all = p_ref[0]
        for i in range(1, k):
            g_all = g_all + p_ref[i]
        off = 0
        for a, size in enumerate(sizes):
            g = g_all[:, off:off + size]
            off += size
            d, nm, nv = adamw_update(g, ins[a][...], ins[n + a][...], ins[2 * n + a][...])
            for kind, val in enumerate((g, d, nm, nv)):
                outs[kind * n + a][...] = val

    return pl.pallas_call(
        body, name="adamw_small",
        out_shape=[jax.ShapeDtypeStruct((1, size), F32) for _ in range(4) for size in sizes],
    )(parts, *ws, *ms, *vs)


def columns_from_shards(g, rows, cols):
    return g.reshape(N_DEV, rows, cols).transpose(1, 0, 2).reshape(rows, N_DEV * cols)


def permute_w_in_t(wt):
    z = lambda n: jnp.zeros((n, D_MODEL), wt.dtype)
    krope = wt[N_KROPE:N_KROPE + ROPE]
    krope_rot = jnp.concatenate([-krope[ROPE // 2:], krope[:ROPE // 2]], axis=0)
    rw = N_RWKV
    return jnp.concatenate([
        wt[N_MA:N_MA + 1024], wt[N_MB:N_MB + 1024],
        wt[rw:rw + 512], wt[rw + 512:rw + 1024], wt[rw + 1024:rw + 1536],
        wt[N_GPA:N_GPA + 512], wt[N_GPB:N_GPB + 512],
        wt[N_QC:N_QC + 256], wt[N_KVC:N_KVC + 128],
        z(NOPE), krope, z(LANE - QK_DIM), z(NOPE), krope_rot, z(LANE - QK_DIM),
        wt[rw + 1536:rw + 1664]], axis=0)


def pad_heads_q(w_uq):
    w = w_uq.reshape(Q_RANK, HEADS, QK_DIM)
    zpad = jnp.zeros((Q_RANK, HEADS, LANE - QK_DIM), w.dtype)
    wq = jnp.concatenate([w, zpad], axis=2).reshape(Q_RANK, HEADS * LANE)
    pe = w[:, :, NOPE:]
    rot = jnp.concatenate([-pe[:, :, ROPE // 2:], pe[:, :, :ROPE // 2]], axis=2)
    wqr = jnp.concatenate([jnp.zeros((Q_RANK, HEADS, NOPE), w.dtype), rot, zpad], axis=2).reshape(Q_RANK, HEADS * LANE)
    return wq, wqr


def unpad_heads_q_grad(dwq, dwqr):
    a = dwq.reshape(Q_RANK, HEADS, LANE)
    r = dwqr.reshape(Q_RANK, HEADS, LANE)[:, :, NOPE:QK_DIM]
    pe = a[:, :, NOPE:QK_DIM] + jnp.concatenate([r[:, :, ROPE // 2:], -r[:, :, :ROPE // 2]], axis=2)
    return jnp.concatenate([a[:, :, :NOPE], pe], axis=2).reshape(Q_RANK, HEADS * QK_DIM)


def pad_heads_kv(w_ukv):
    w = w_ukv.reshape(KV_RANK, HEADS, 2 * HEAD)
    z = jnp.zeros((KV_RANK, HEADS, HEAD), w.dtype)
    wkn = jnp.concatenate([w[:, :, :NOPE], z], axis=2).reshape(KV_RANK, HEADS * LANE)
    val = w[:, :, NOPE:]
    odd = (jnp.arange(HEADS) % 2 == 1)[None, :, None]
    wv = jnp.concatenate([jnp.where(odd, 0, val), jnp.where(odd, val, 0)], axis=2).reshape(KV_RANK, HEADS * LANE)
    return wkn, wv


def unpad_heads_kv_grad(dwkn, dwv):
    a = dwkn.reshape(KV_RANK, HEADS, LANE)[:, :, :NOPE]
    b = dwv.reshape(KV_RANK, HEADS, LANE)
    odd = (jnp.arange(HEADS) % 2 == 1)[None, :, None]
    val = jnp.where(odd, b[:, :, HEAD:], b[:, :, :HEAD])
    return jnp.concatenate([a, val], axis=2).reshape(KV_RANK, HEADS * 2 * HEAD)


def kernel(x, c, positions, w_ada, b_ada, w_in, q_norm_g, w_uq, kv_norm_g, w_ukv, mu_rwkv, w0, w_decay_up, a0, w_iclr_up, k_k, k_a, r_k, gn_g, gn_b, w_proj_a, w_proj_b, w_out, post_g, post_b, loss_target, m_w_ada, m_b_ada, m_w_in, m_q_norm_g, m_w_uq, m_kv_norm_g, m_w_ukv, m_mu_rwkv, m_w0, m_w_decay_up, m_a0, m_w_iclr_up, m_k_k, m_k_a, m_r_k, m_gn_g, m_gn_b, m_w_proj_a, m_w_proj_b, m_w_out, m_post_g, m_post_b, v_w_ada, v_b_ada, v_w_in, v_q_norm_g, v_w_uq, v_kv_norm_g, v_w_ukv, v_mu_rwkv, v_w0, v_w_decay_up, v_a0, v_w_iclr_up, v_k_k, v_k_a, v_r_k, v_gn_g, v_gn_b, v_w_proj_a, v_w_proj_b, v_w_out, v_post_g, v_post_b):
    weights = dict(w_ada=w_ada, b_ada=b_ada, w_in=w_in, q_norm_g=q_norm_g, w_uq=w_uq, kv_norm_g=kv_norm_g,
                   w_ukv=w_ukv, mu_rwkv=mu_rwkv, w0=w0, w_decay_up=w_decay_up, a0=a0, w_iclr_up=w_iclr_up,
                   k_k=k_k, k_a=k_a, r_k=r_k, gn_g=gn_g, gn_b=gn_b, w_proj_a=w_proj_a, w_proj_b=w_proj_b,
                   w_out=w_out, post_g=post_g, post_b=post_b)
    mom1 = dict(w_ada=m_w_ada, b_ada=m_b_ada, w_in=m_w_in, q_norm_g=m_q_norm_g, w_uq=m_w_uq, kv_norm_g=m_kv_norm_g,
                w_ukv=m_w_ukv, mu_rwkv=m_mu_rwkv, w0=m_w0, w_decay_up=m_w_decay_up, a0=m_a0, w_iclr_up=m_w_iclr_up,
                k_k=m_k_k, k_a=m_k_a, r_k=m_r_k, gn_g=m_gn_g, gn_b=m_gn_b, w_proj_a=m_w_proj_a, w_proj_b=m_w_proj_b,
                w_out=m_w_out, post_g=m_post_g, post_b=m_post_b)
    mom2 = dict(w_ada=v_w_ada, b_ada=v_b_ada, w_in=v_w_in, q_norm_g=v_q_norm_g, w_uq=v_w_uq, kv_norm_g=v_kv_norm_g,
                w_ukv=v_w_ukv, mu_rwkv=v_mu_rwkv, w0=v_w0, w_decay_up=v_w_decay_up, a0=v_a0, w_iclr_up=v_w_iclr_up,
                k_k=v_k_k, k_a=v_k_a, r_k=v_r_k, gn_g=v_gn_g, gn_b=v_gn_b, w_proj_a=v_w_proj_a, w_proj_b=v_w_proj_b,
                w_out=v_w_out, post_g=v_post_g, post_b=v_post_b)
    names = list(weights)
    n_rows = x.shape[1]
    me = 4 * lax.axis_index("x") + 2 * lax.axis_index("y") + lax.axis_index("c")
    xr = x[0]
    tgt = loss_target[0]
    row = lambda a: a.reshape(1, -1)

    w_in_all, c_all = gather_shards([w_in[0].T.astype(BF16), c])
    c_all = c_all.reshape(N_DEV, D_MODEL)
    w_in_pt = permute_w_in_t(w_in_all.reshape(IN_WIDTH, D_MODEL))

    mod_all = ada_modulation(c_all, w_ada[0], b_ada.reshape(N_DEV, -1))
    mod = lax.dynamic_index_in_dim(mod_all, me, axis=1, keepdims=False).reshape(3, D_MODEL)

    proj, *gathered = fwd_in_gather(xr, mod, w_in_pt, [weights[n][0].astype(BF16) for n, _, _ in SHARDED[1:]])
    pcol = lambda off_, w: (proj, w, off_ // w)
    full = {}
    for (n, r, cdim), part in zip(SHARDED[1:], gathered):
        full[n] = part.reshape(N_DEV * r, cdim) if n == "w_out" else columns_from_shards(part, r, cdim)
    wq, wqr = pad_heads_q(full["w_uq"])
    wkn, wv = pad_heads_kv(full["w_ukv"])
    zl = jnp.zeros((LORA, WIDTH), BF16)
    w_dec = jnp.concatenate([full["w_decay_up"], zl], axis=0)
    w_iclr = jnp.concatenate([zl, full["w_iclr_up"]], axis=0)
    wpa, wpb, wout = full["w_proj_a"], full["w_proj_b"], full["w_out"]

    inv_freq = ROPE_THETA ** (-jnp.arange(0, ROPE, 2, dtype=F32) / ROPE)
    ang = positions[0].astype(F32)[:, None] * inv_freq
    ones_n, zeros_n, zeros_p = jnp.ones((n_rows, NOPE), F32), jnp.zeros((n_rows, NOPE), F32), jnp.zeros((n_rows, LANE - QK_DIM), F32)
    cos_t = jnp.concatenate([ones_n, jnp.cos(ang), jnp.cos(ang), zeros_p], axis=1)
    sin_t = jnp.concatenate([zeros_n, jnp.sin(ang), jnp.sin(ang), zeros_p], axis=1)

    gq, gkv = q_norm_g, kv_norm_g
    mla_consts = [gq, gkv, wq, wqr, wkn, wv]
    q, k, v = row_call(
        "mla_prep", mla_prep_tile, n_rows,
        [pcol(P_QC, 256), pcol(P_KVC, 128), pcol(P_KR, 128), pcol(P_KRR, 128), (cos_t, LANE, 0), (sin_t, LANE, 0)],
        mla_consts, [(HEADS * LANE, BF16)] * 3, tile_rows=PREP_TILE)
    ya, lse = attention_forward(q, k, v)

    def chunk_sum_matrices(n):
        t_idx = jnp.arange(n)
        same_chunk = (t_idx[:, None] // CHUNK) == (t_idx[None, :] // CHUNK)
        return (same_chunk & (t_idx[:, None] >= t_idx[None, :])).astype(F32), same_chunk.astype(F32)

    l_idx = jnp.arange(LANE)
    bd = ((l_idx[:, None] // HEAD) == (l_idx[None, :] // HEAD)).astype(F32)
    mu = mu_rwkv
    mu_r, mu_k, mu_v, mu_l = mu[:, 0:512], mu[:, 512:1024], mu[:, 1024:1536], mu[:, 1536:1664]
    rk_row = row(r_k)
    rwkv_consts = lambda n: [mu_r, mu_k, mu_v, mu_l, w0, a0, k_k, k_a, w_dec, w_iclr, *chunk_sum_matrices(n), bd]
    rwkv_rows = [pcol(P_R, 512), pcol(P_K, 512), pcol(P_V, 512), pcol(P_LORA, 128)]
    rt, at, bt, kt, clf, uv, ur, k2 = row_call(
        "rwkv_prep", rwkv_prep_tile, n_rows, rwkv_rows, rwkv_consts(ROW_TILE), [(WIDTH, F32)] * 8, halo_in=rwkv_rows)
    y, m0s, state_maps, out_maps, *wkv_saved = wkv_forward(at, bt, kt, rt, uv, clf)

    tail = row_call(
        "tail", tail_tile, n_rows,
        [(xr, D_MODEL, 0), (tgt, D_MODEL, 0), pcol(P_MA, 1024), pcol(P_MB, 1024), pcol(P_GPA, 512), pcol(P_GPB, 512),
         (ya, WIDTH, 0), (y, WIDTH, 0), (ur, WIDTH, 0), (k2, WIDTH, 0), (uv, WIDTH, 0)],
        [mod, wpa, wpb, wout, gn_g, gn_b, rk_row, post_g, post_b, bd],
        [(D_MODEL, F32), (1024, BF16), (1024, BF16), (512, BF16), (512, BF16), (WIDTH, F32), (WIDTH, F32), (WIDTH, F32)],
        acc_out=[((1, LANE), F32), ((D_MODEL, D_MODEL), F32), ((WIDTH, D_MODEL), F32), ((WIDTH, D_MODEL), F32),
                 ((1, WIDTH), F32), ((1, WIDTH), F32), ((1, D_MODEL), F32), ((1, D_MODEL), F32), ((1, D_MODEL), F32)])
    (dz, dma, dmb, dgpa, dgpb, dya, dy, dyb,
     loss_row, g_wout, g_wpa, g_wpb, g_gn_g, g_gn_b, g_post_g, g_post_b, dgate) = tail

    def owner_blocks(g, n):
        r, cdim = next((r, cdim) for name, r, cdim in SHARDED if name == n)
        return (g.reshape(N_DEV, r, cdim) if n == "w_out" else g.reshape(r, N_DEV, cdim).transpose(1, 0, 2)).astype(BF16)

    early = ("w_proj_a", "w_proj_b", "w_out")
    dq, dk, dv, *got_early = attention_backward(
        q, k, v, ya, dya, lse, [owner_blocks(g, n) for g, n in zip((g_wpa, g_wpb, g_wout), early)])
    dq_c, dkv_c, dkr, dkrr, g_wq, g_wqr, g_wkn, g_wv, g_gq, g_gkv = row_call(
        "mla_prep_bwd", mla_prep_bwd_tile, n_rows,
        [pcol(P_QC, 256), pcol(P_KVC, 128), (cos_t, LANE, 0), (sin_t, LANE, 0),
         (dq, HEADS * LANE, 0), (dk, HEADS * LANE, 0), (dv, HEADS * LANE, 0)],
        mla_consts, [(256, BF16), (128, BF16), (128, BF16), (128, BF16)],
        acc_out=[((Q_RANK, HEADS * LANE), F32)] * 2 + [((KV_RANK, HEADS * LANE), F32)] * 2
        + [((1, Q_RANK), F32), ((1, KV_RANK), F32)], tile_rows=PREP_TILE)

    dat, dbt, dkt, drt, dvv, dlw = wkv_backward(at, bt, kt, rt, uv, clf, m0s, state_maps, out_maps, wkv_saved, dy)
    (dr0, dk0, dv0, dl0, g_mu_r, g_mu_k, g_mu_v, g_mu_l, g_w0, g_a0, g_k_k, g_k_a, g_r_k, g_wdec, g_wiclr) = row_call(
        "rwkv_prep_bwd", rwkv_prep_bwd_tile, n_rows,
        rwkv_rows + [(drt, WIDTH, 0), (dat, WIDTH, 0), (dbt, WIDTH, 0), (dkt, WIDTH, 0), (dvv, WIDTH, 0),
                     (dlw, WIDTH, 0), (dyb, WIDTH, 0)],
        rwkv_consts(PREP_TILE) + [rk_row], [(512, BF16), (512, BF16), (512, BF16), (128, BF16)],
        acc_out=[((1, 512), F32)] * 3 + [((1, 128), F32)] + [((1, 512), F32)] * 5 + [((LANE, WIDTH), F32)] * 2,
        halo_in=rwkv_rows, carry=[512, 512, 512, 128], reverse=True, tile_rows=PREP_TILE)

    li = jnp.arange(LANE)
    src, dst = li[:, None], li[None, :]
    half = ROPE // 2
    unrot = (jnp.where((dst >= NOPE) & (dst < NOPE + half) & (src == dst + half), 1.0, 0.0)
             - jnp.where((dst >= NOPE + half) & (dst < QK_DIM) & (src == dst - half), 1.0, 0.0)).astype(BF16)
    dx, h_t, dproj_blocks, dshift, dscale = in_backward(
        xr, dz, [dma, dmb, dr0, dk0, dv0, dgpa, dgpb, dq_c, dkv_c, dkr, dkrr, dl0], mod, w_in_pt, unrot)

    late = ("w_uq", "w_ukv", "w_decay_up", "w_iclr_up")
    late_grads = (unpad_heads_q_grad(g_wq, g_wqr), unpad_heads_kv_grad(g_wkn, g_wv), g_wdec[:LORA], g_wiclr[LORA:])
    blocks = [owner_blocks(g, n) for g, n in zip(late_grads, late)]
    dmod = jnp.concatenate([dshift, dscale, dgate], axis=1)
    small = jnp.concatenate([dmod, g_gq, g_gkv, g_mu_r, g_mu_k, g_mu_v, g_mu_l, g_w0, g_a0, g_k_k, g_k_a, g_r_k,
                             g_gn_g, g_gn_b, g_post_g, g_post_b, loss_row], axis=1)
    my_x, my_y, my_c = lax.axis_index("x"), lax.axis_index("y"), lax.axis_index("c")
    chip_order = [4 * (my_x ^ fx) + 2 * (my_y ^ fy) for fx, fy in ((1, 1), (1, 0), (0, 1), (0, 0))]
    owners = [chip_order[s % 4] + (my_c if s >= 4 else 1 - my_c) for s in WGRAD_SLOTS]
    order = jnp.stack(owners + [jnp.int32(s) for s in WGRAD_SLOTS]).astype(jnp.int32)
    got_w_in, *got_late, got_small = in_weight_grad_exchange(h_t, dproj_blocks, blocks, small, order)
    got = {"w_in": got_w_in, **dict(zip(late, got_late)), **dict(zip(early, got_early))}
    loss = jnp.sum(got_small[:, 0, SMALL_ELEMS])

    ada_cols = w_ada.shape[2]
    dmod_all = got_small[:, 0, :3 * D_MODEL]
    g_ada = ada_weight_grad(c_all, lax.dynamic_slice_in_dim(dmod_all, me * ada_cols, ada_cols, axis=1))

    outs = [dict() for _ in range(4)]
    res = adamw(g_ada[None], w_ada, m_w_ada, v_w_ada, "adamw_w_ada")
    for kind in range(4):
        outs[kind]["w_ada"] = res[kind]
    for n, _, _ in SHARDED:
        res = adamw(got[n], weights[n], mom1[n], mom2[n], "adamw_" + n)
        for kind in range(4):
            outs[kind][n] = res[kind]
    rows_of = lambda tree: [tree[n].reshape(1, -1) for n, _ in SMALL]
    res = adamw_small(got_small, rows_of(weights), rows_of(mom1), rows_of(mom2))
    for kind in range(4):
        for a, (n, _) in enumerate(SMALL):
            outs[kind][n] = res[kind * len(SMALL) + a].reshape(weights[n].shape)
    return (loss, dx[None], *[outs[0][n] for n in names], *[outs[1][n] for n in names],
            *[outs[2][n] for n in names], *[outs[3][n] for n in names])
```

```python
import functools
import math

import jax
import jax.numpy as jnp
from jax import lax
from jax.experimental import pallas as pl
from jax.experimental.pallas import tpu as pltpu

F32 = jnp.float32
BF16 = jnp.bfloat16
HIGHEST = lax.Precision.HIGHEST
MESH_IDS = pl.DeviceIdType.MESH

N_DEV = 8
D_MODEL = 1024
LN_EPS = 1e-5
RMS_EPS = 1e-6
GN_EPS = 64e-5
HEADS = 8
Q_RANK = 256
KV_RANK = 128
ROPE = 32
NOPE = 64
QK_DIM = NOPE + ROPE
WIDTH = 512
HEAD = 64
LORA = 64
CHUNK = 64
DEPTH = 1
ALPHA = (2.0 * DEPTH) ** 0.25
ROPE_THETA = 10000.0
ATTN_SCALE = QK_DIM ** -0.5
DECAY_SCALE = math.exp(-0.5)

ADAM_LR = 0.001
ADAM_B1 = 0.9
ADAM_B2 = 0.999
ADAM_EPS = 1e-08
ADAM_WD = 0.01
ADAM_STEP = 10

LANE = 128
PAIR = 2 * HEAD
ROW_TILE = 256
PREP_TILE = 512
HALO_ROWS = 16
ATTN_FWD_TILES = (512, 1024)
ATTN_BWD_TILES = (512, 512)
LOG2_E = math.log2(math.e)
Q_PRESCALE = ATTN_SCALE * LOG2_E
WKV_CHUNKS_PER_STEP = 16
WKV_CHAIN_GROUPS = 2
WGRAD_SLOTS = (0, 1, 4, 2, 5, 6, 3, 7)
VMEM_LIMIT = 56 * 1024 * 1024

P_MA, P_MB, P_R, P_K, P_V, P_GPA, P_GPB, P_QC, P_KVC, P_KR, P_KRR, P_LORA = (
    0, 1024, 2048, 2560, 3072, 3584, 4096, 4608, 4864, 4992, 5120, 5248)
P_WIDTH = 5376

N_QC, N_KVC, N_KROPE, N_GPA, N_RWKV, N_GPB, N_MA, N_MB = 0, 256, 384, 416, 928, 2592, 3104, 4128
IN_WIDTH = 5152

SHARDED = (("w_in", 1024, 644), ("w_uq", 256, 96), ("w_ukv", 128, 128), ("w_decay_up", 64, 64),
           ("w_iclr_up", 64, 64), ("w_proj_a", 512, 128), ("w_proj_b", 512, 128), ("w_out", 128, 1024))
SMALL = (("b_ada", 3072), ("q_norm_g", 256), ("kv_norm_g", 128), ("mu_rwkv", 1664), ("w0", 512), ("a0", 512),
         ("k_k", 512), ("k_a", 512), ("r_k", 512), ("gn_g", 512), ("gn_b", 512), ("post_g", 1024), ("post_b", 1024))
SMALL_ELEMS = sum(n for _, n in SMALL)


def mm(a, b):
    return jnp.dot(a.astype(BF16), b.astype(BF16), preferred_element_type=F32)


def mm_nt(a, b):
    return lax.dot_general(a.astype(BF16), b.astype(BF16), (((1,), (1,)), ((), ())), preferred_element_type=F32)


def mm_tn(a, b):
    return lax.dot_general(a.astype(BF16), b.astype(BF16), (((0,), (0,)), ((), ())), preferred_element_type=F32)


def hdot(a, b):
    return jnp.dot(a, b, precision=HIGHEST, preferred_element_type=F32)


def hdot_tn(a, b):
    return lax.dot_general(a, b, (((0,), (0,)), ((), ())), precision=HIGHEST, preferred_element_type=F32)


def sigmoid(x):
    return 1.0 / (1.0 + jnp.exp(-x))


def colsum(x):
    return jnp.sum(x, axis=0, keepdims=True)


def rowmean(x):
    return jnp.mean(x, axis=-1, keepdims=True)


def layer_norm_stats(x):
    xc = x - rowmean(x)
    rstd = lax.rsqrt(rowmean(xc * xc) + LN_EPS)
    return xc * rstd, rstd


def layer_norm_bwd(dy, xhat, rstd):
    return rstd * (dy - rowmean(dy) - xhat * rowmean(dy * xhat))


def bf16_pieces(x, n):
    pieces = []
    for _ in range(n):
        p = x.astype(BF16)
        pieces.append(p)
        x = x - p.astype(F32)
    return pieces


def ones_dot(ones, x, n_pieces):
    ones = ones.astype(BF16)
    return sum(jnp.dot(ones, p, preferred_element_type=F32) for p in bf16_pieces(x, n_pieces))


def ones_dot_nt(ones, x, n_pieces):
    ones = ones.astype(BF16)
    return sum(lax.dot_general(ones, p, (((1,), (1,)), ((), ())), preferred_element_type=F32)
               for p in bf16_pieces(x, n_pieces))


def head_sum(x, bd):
    return jnp.concatenate([mm(x[:, p * LANE:(p + 1) * LANE], bd) for p in range(x.shape[1] // LANE)], axis=1)


def tile_lanes(t, n):
    return jnp.concatenate([t] * n, axis=1)


def row_iota(shape):
    return lax.broadcasted_iota(jnp.int32, shape, 0)


def lane_iota(shape):
    return lax.broadcasted_iota(jnp.int32, shape, 1)


def shift_rows_down(x, row0):
    rolled = pltpu.roll(x, 1, axis=0)
    return jnp.where(row_iota(x.shape) == 0, row0, rolled)


def shift_rows_up(x, row_last):
    rolled = pltpu.roll(x, x.shape[0] - 1, axis=0)
    return jnp.where(row_iota(x.shape) == x.shape[0] - 1, row_last, rolled)


def row_call(name, fn, n_rows, row_in, const_in, row_out, acc_out=(), halo_in=(), carry=(), reverse=False,
             tile_rows=ROW_TILE):
    ts = tile_rows
    n_tiles = n_rows // ts
    n_in = len(row_in) + len(halo_in) + len(const_in)
    n_ro, n_ao = len(row_out), len(acc_out)

    def tile_of(g):
        return (n_tiles - 1 - g) if reverse else g

    def body(*refs):
        ins = refs[:n_in]
        ro = refs[n_in:n_in + n_ro]
        ao = refs[n_in + n_ro:n_in + n_ro + n_ao]
        cr = refs[n_in + n_ro + n_ao:]
        g = pl.program_id(0)
        step0 = g == 0
        tile0 = tile_of(g) == 0
        for r in cr:
            @pl.when(step0)
            def _(r=r):
                r[...] = jnp.zeros_like(r)
        n_tiled = len(row_in) + len(halo_in)
        vals = [r[...].astype(F32) for r in ins[:n_tiled]] + [r[...] for r in ins[n_tiled:]]
        outs = fn(step0, tile0, *vals, *[c[0:1, :] for c in cr])
        for r, v in zip(ro, outs[:n_ro]):
            r[...] = v.astype(r.dtype)
        for r, v in zip(ao, outs[n_ro:n_ro + n_ao]):
            @pl.when(step0)
            def _(r=r, v=v):
                r[...] = v.astype(r.dtype)

            @pl.when(jnp.logical_not(step0))
            def _(r=r, v=v):
                r[...] += v.astype(r.dtype)
        for r, v in zip(cr, outs[n_ro + n_ao:]):
            r[0:1, :] = v

    in_specs = [pl.BlockSpec((ts, w), functools.partial(lambda g, cb: (tile_of(g), cb), cb=cb)) for _, w, cb in row_in]
    in_specs += [pl.BlockSpec((HALO_ROWS, w), functools.partial(
        lambda g, cb: (jnp.maximum(tile_of(g) * (ts // HALO_ROWS) - 1, 0), cb), cb=cb)) for _, w, cb in halo_in]
    in_specs += [pl.BlockSpec(memory_space=pltpu.VMEM) for _ in const_in]
    out_specs = [pl.BlockSpec((ts, w), lambda g: (tile_of(g), 0)) for w, _ in row_out]
    out_specs += [pl.BlockSpec(s, lambda g: (0, 0)) for s, _ in acc_out]
    out_shape = [jax.ShapeDtypeStruct((n_rows, w), d) for w, d in row_out]
    out_shape += [jax.ShapeDtypeStruct(s, d) for s, d in acc_out]
    return pl.pallas_call(
        body, name=name, grid=(n_tiles,), in_specs=in_specs, out_specs=out_specs, out_shape=out_shape,
        scratch_shapes=[pltpu.VMEM((8, w), F32) for w in carry],
        compiler_params=pltpu.CompilerParams(dimension_semantics=("arbitrary",), vmem_limit_bytes=VMEM_LIMIT),
    )(*[a for a, _, _ in row_in], *[a for a, _, _ in halo_in], *const_in)


def my_position():
    return lax.axis_index("x"), lax.axis_index("y"), lax.axis_index("c")


def flip(pos, k):
    x, y, c = pos
    dx, dy, dc = (k >> 2) & 1, (k >> 1) & 1, k & 1
    return (1 - x if dx else x, 1 - y if dy else y, 1 - c if dc else c)


def flat_index(pos):
    return 4 * pos[0] + 2 * pos[1] + pos[2]


def gather_shards(shards):
    n = len(shards)

    def body(*refs):
        x_refs, out_refs = refs[:n], refs[n:2 * n]
        send_sems, recv_sems, local_sems = refs[2 * n:]
        x, y, c = my_position()
        me, sibling = (x, y, c), (x, y, 1 - c)
        chips = [(1 - x, y), (x, 1 - y), (1 - x, 1 - y)]
        relay_from = (x ^ (1 - c), y ^ c, c)
        relay_to = (x ^ c, y ^ (1 - c), c)

        def copy(a, k, block, to, from_input=False):
            slot = out_refs[a].at[flat_index(block)]
            return pltpu.make_async_remote_copy(
                src_ref=x_refs[a] if from_input else slot, dst_ref=slot,
                send_sem=send_sems.at[7 * a + k], recv_sem=recv_sems.at[7 * a + k],
                device_id=to, device_id_type=MESH_IDS)

        mine = [pltpu.make_async_copy(x_refs[a], out_refs[a].at[flat_index(me)], local_sems.at[a]) for a in range(n)]
        for cp in mine:
            cp.start()
        first = []
        for a in range(n):
            first.append(copy(a, 0, me, sibling, from_input=True))
            first += [copy(a, 1 + j, me, (*chip, c), from_input=True) for j, chip in enumerate(chips[:2])]
        for cp in first:
            cp.start()
        relayed = [copy(a, 3, relay_from, relay_to) for a in range(n)]
        passed = []
        for j, chip in enumerate(chips):
            for a in range(n):
                copy(a, 1 + j, (*chip, c), me).wait_recv()
                cp = copy(a, 4 + j, (*chip, c), sibling)
                cp.start()
                passed.append(cp)
                if j < 2:
                    @pl.when(c == j)
                    def _(a=a):
                        relayed[a].start()
        for a in range(n):
            copy(a, 0, sibling, me).wait_recv()
            for j, chip in enumerate(chips):
                copy(a, 4 + j, (*chip, 1 - c), me).wait_recv()
        for cp in first + passed + relayed:
            cp.wait_send()
        for cp in mine:
            cp.wait()

    return pl.pallas_call(
        body, name="gather_shards",
        out_shape=[jax.ShapeDtypeStruct((N_DEV,) + s.shape, s.dtype) for s in shards],
        in_specs=[pl.BlockSpec(memory_space=pl.ANY)] * n, out_specs=[pl.BlockSpec(memory_space=pl.ANY)] * n,
        scratch_shapes=[pltpu.SemaphoreType.DMA((7 * n,)), pltpu.SemaphoreType.DMA((7 * n,)),
                        pltpu.SemaphoreType.DMA((n,))],
    )(*shards)


def ada_modulation(c_all, w_ada_loc, b_ada_blocks):
    cols = w_ada_loc.shape[1]

    def body(c_ref, w_ref, b_ref, out_ref, send_sems, recv_sems):
        me = my_position()
        mi = flat_index(me)
        cv = c_ref[...]
        res = hdot(cv * sigmoid(cv), w_ref[...]) + b_ref[pl.ds(mi, 1), :]
        out_ref[mi] = res
        sends = []
        for k in range(1, N_DEV):
            cp = pltpu.make_async_remote_copy(
                src_ref=out_ref.at[mi], dst_ref=out_ref.at[mi], send_sem=send_sems.at[k - 1],
                recv_sem=recv_sems.at[k - 1], device_id=flip(me, k), device_id_type=MESH_IDS)
            cp.start()
            sends.append(cp)
        for k in range(1, N_DEV):
            pi = flat_index(flip(me, k))
            pltpu.make_async_remote_copy(
                src_ref=out_ref.at[pi], dst_ref=out_ref.at[pi], send_sem=send_sems.at[k - 1],
                recv_sem=recv_sems.at[k - 1], device_id=flip(me, k), device_id_type=MESH_IDS).wait_recv()
        for cp in sends:
            cp.wait_send()

    return pl.pallas_call(
        body, name="ada_modulation",
        out_shape=jax.ShapeDtypeStruct((N_DEV, N_DEV, cols), F32),
        in_specs=[pl.BlockSpec(memory_space=pltpu.VMEM)] * 3, out_specs=pl.BlockSpec(memory_space=pltpu.VMEM),
        scratch_shapes=[pltpu.SemaphoreType.DMA((7,)), pltpu.SemaphoreType.DMA((7,))],
    )(c_all, w_ada_loc, b_ada_blocks)


def fwd_in_tile(step0, tile0, x, mod, w_in_pt):
    xhat, _ = layer_norm_stats(x)
    h = xhat * (1.0 + mod[1:2]) + mod[0:1]
    return (mm_nt(h, w_in_pt),)


def fwd_in_gather(x, mod, w_in_pt, shards):
    n = len(shards)
    n_rows = x.shape[0]
    ts = ROW_TILE
    n_tiles = n_rows // ts

    def body(x_ref, mod_ref, w_ref, *rest):
        s_refs = rest[:n]
        proj_ref, out_refs = rest[n], rest[n + 1:2 * n + 1]
        send_sems, recv_sems, local_sems = rest[2 * n + 1:]
        g = pl.program_id(0)
        me = my_position()
        mi = flat_index(me)

        def copies(k, slot):
            return [pltpu.make_async_remote_copy(
                src_ref=s_refs[a], dst_ref=out_refs[a].at[slot], send_sem=send_sems.at[7 * a + k - 1],
                recv_sem=recv_sems.at[7 * a + k - 1], device_id=flip(me, k), device_id_type=MESH_IDS)
                for a in range(n)]

        local = [pltpu.make_async_copy(s_refs[a], out_refs[a].at[mi], local_sems.at[a]) for a in range(n)]

        @pl.when(g == 0)
        def _():
            for cp in local:
                cp.start()
            for k in range(1, N_DEV):
                for cp in copies(k, mi):
                    cp.start()

        proj_ref[...] = fwd_in_tile(None, None, x_ref[...], mod_ref[...], w_ref[...])[0].astype(BF16)

        @pl.when(g == n_tiles - 1)
        def _():
            for k in range(1, N_DEV):
                for cp in copies(k, flat_index(flip(me, k))):
                    cp.wait_recv()
            for k in range(1, N_DEV):
                for cp in copies(k, mi):
                    cp.wait_send()
            for cp in local:
                cp.wait()

    hbm = pl.BlockSpec(memory_space=pl.ANY)
    const = pl.BlockSpec(memory_space=pltpu.VMEM)
    return pl.pallas_call(
        body, name="fwd_in_gather", grid=(n_tiles,),
        in_specs=[pl.BlockSpec((ts, D_MODEL), lambda g: (g, 0)), const, const] + [hbm] * n,
        out_specs=[pl.BlockSpec((ts, P_WIDTH), lambda g: (g, 0))] + [hbm] * n,
        out_shape=[jax.ShapeDtypeStruct((n_rows, P_WIDTH), BF16)]
        + [jax.ShapeDtypeStruct((N_DEV,) + s.shape, s.dtype) for s in shards],
        scratch_shapes=[pltpu.SemaphoreType.DMA((7 * n,)), pltpu.SemaphoreType.DMA((7 * n,)),
                        pltpu.SemaphoreType.DMA((n,))],
        compiler_params=pltpu.CompilerParams(dimension_semantics=("arbitrary",), vmem_limit_bytes=VMEM_LIMIT),
    )(x, mod, w_in_pt, *shards)


def rms_norm_fwd(x, g):
    r = lax.rsqrt(rowmean(x * x) + RMS_EPS)
    xh = x * r
    return xh * g, xh, r


def key_rope_mask(shape):
    return (lane_iota(shape) >= NOPE).astype(F32)


def mla_prep_tile(step0, tile0, q_c, kv_c, kr, krr, cos, sin, gq, gkv, wq, wqr, wkn, wv):
    qn, _, _ = rms_norm_fwd(q_c, gq)
    kvn, _, _ = rms_norm_fwd(kv_c, gkv)
    q = (mm(qn, wq) * tile_lanes(cos, HEADS) + mm(qn, wqr) * tile_lanes(sin, HEADS)) * Q_PRESCALE
    kpe = kr * (cos * key_rope_mask(cos.shape)) + krr * sin
    k = mm(kvn, wkn) + tile_lanes(kpe, HEADS)
    v = mm(kvn, wv)
    return q, k, v


def rwkv_prep_core(tile0, r0, k0, v0, l0, hr, hk, hv, hl, mu_r, mu_k, mu_v, mu_l, w0, a0, k_k, k_a,
                   w_dec, w_iclr, tril, same, bd):
    def shifted(x, halo, mu):
        row0 = jnp.where(tile0, 0.0, halo[HALO_ROWS - 1:HALO_ROWS, :])
        prev = shift_rows_down(x, row0)
        return x + (prev - x) * mu, prev

    ur, pr = shifted(r0, hr, mu_r)
    uk, pk = shifted(k0, hk, mu_k)
    uv, pv = shifted(v0, hv, mu_v)
    ul, plo = shifted(l0, hl, mu_l)
    th = jnp.tanh(ul)
    sg = sigmoid(w0 + mm(th, w_dec))
    lw = -DECAY_SCALE * sg
    a_ic = sigmoid(a0 + mm(ul, w_iclr))
    kkraw = uk * k_k
    nrm_raw = jnp.sqrt(head_sum(kkraw * kkraw, bd))
    nrm = jnp.maximum(nrm_raw, 1e-12)
    kk = kkraw / nrm
    k2 = uk * (1.0 + (a_ic - 1.0) * k_a)
    lc = ones_dot(tril, lw, 3)
    lcl = ones_dot(same, lw, 3)
    return dict(ur=ur, uk=uk, uv=uv, ul=ul, pr=pr, pk=pk, pv=pv, pl=plo, th=th, sg=sg, lw=lw, a_ic=a_ic,
                kkraw=kkraw, nrm_raw=nrm_raw, nrm=nrm, kk=kk, k2=k2, lc=lc, lcl=lcl)


def rwkv_prep_tile(step0, tile0, r0, k0, v0, l0, hr, hk, hv, hl, *consts):
    f = rwkv_prep_core(tile0, r0, k0, v0, l0, hr, hk, hv, hl, *consts)
    lc, lw = f["lc"], f["lw"]
    e_neg = jnp.exp(-lc)
    rt = f["ur"] * jnp.exp(lc)
    at = -f["kk"] * jnp.exp(lc - lw)
    bt = f["kk"] * f["a_ic"] * e_neg
    kt = f["k2"] * e_neg
    return rt, at, bt, kt, jnp.exp(f["lcl"]), f["uv"], f["ur"], f["k2"]


def wkv_masks():
    lane = lane_iota((1, PAIR))
    m_lo = (lane < HEAD).astype(F32)
    r2 = row_iota((PAIR, PAIR))
    c2 = lane_iota((PAIR, PAIR))
    bd = ((r2 < HEAD) == (c2 < HEAD)).astype(F32)
    eye2 = (r2 == c2).astype(F32)
    eye = (row_iota((CHUNK, CHUNK)) == lane_iota((CHUNK, CHUNK))).astype(F32)
    t_idx = row_iota((4 * CHUNK, PAIR)) % CHUNK
    s_idx = lane_iota((4 * CHUNK, PAIR)) % CHUNK
    keep = s_idx < t_idx + (row_iota((4 * CHUNK, PAIR)) >= 2 * CHUNK).astype(jnp.int32)
    return (m_lo, 1.0 - m_lo), keep, eye, bd, eye2


def rows(*parts):
    return jnp.concatenate(parts, axis=0)


def lanes(*parts):
    return jnp.concatenate(parts, axis=1)


def head_rows(x, ms):
    return rows(x * ms[0], x * ms[1])


def wkv_score_stack(at, rt, ms):
    return rows(head_rows(at, ms), head_rows(rt, ms))


def wkv_chunks_pre(chunks, masks, between_stages=lambda: None):
    ms, keep, eye, bd, eye2 = masks
    n = len(chunks)
    at, bt, kt, rt, v, cl = (list(t) for t in zip(*chunks))
    scores = [jnp.where(keep, mm_nt(wkv_score_stack(a, r, ms), rows(b, k)), 0.0)
              for a, r, b, k in zip(at, rt, bt, kt)]
    between_stages()
    q = CHUNK
    aab = [s[h * q:(h + 1) * q, :q] for s in scores for h in range(2)]
    tinv = [eye + a for a in aab]
    power = [mm(a, a) for a in aab]
    between_stages()
    for _ in range(5):
        both = [mm(rows(t, p), p) for t, p in zip(tinv, power)]
        tinv = [t + x[:q] for t, x in zip(tinv, both)]
        power = [x[q:] for x in both]
        between_stages()
    pair = lambda c, row0, col0: lanes(scores[c][row0:row0 + q, col0:col0 + q],
                                       scores[c][row0 + q:row0 + 2 * q, col0:col0 + q])
    tinv_p = [lanes(tinv[2 * c], tinv[2 * c + 1]) for c in range(n)]
    aak_p = [pair(c, 0, q) for c in range(n)]
    prb_p = [pair(c, 2 * q, 0) for c in range(n)]
    prk_p = [pair(c, 2 * q, q) for c in range(n)]
    v_rows = [head_rows(x, ms) for x in v]
    wy = [mm(rows(a, p), x) for a, p, x in zip(aak_p, prk_p, v_rows)]
    between_stages()
    w = [x[:q] for x in wy]
    yh2 = [x[q:] for x in wy]
    aw = [mm(t, lanes(head_rows(a, ms), head_rows(w_, ms))) for t, a, w_ in zip(tinv_p, at, w)]
    between_stages()
    ah = [x[:, :PAIR] for x in aw]
    wh = [x[:, PAIR:] for x in aw]
    ry = [mm(p, lanes(head_rows(a, ms), head_rows(w_, ms))) for p, a, w_ in zip(prb_p, ah, wh)]
    between_stages()
    rh = [r + x[:, :PAIR] for r, x in zip(rt, ry)]
    yh = [x[:, PAIR:] + y for x, y in zip(ry, yh2)]
    bc = [b * c_ for b, c_ in zip(bt, cl)]
    kc = [k * c_ for k, c_ in zip(kt, cl)]
    gh = [mm_tn(b, lanes(a, w_)) for b, a, w_ in zip(bc, ah, wh)]
    g = [eye2 * c_ + bd * x[:, :PAIR] for c_, x in zip(cl, gh)]
    h = [bd * (x[:, PAIR:] + mm_tn(k, v_)) for x, k, v_ in zip(gh, kc, v)]
    as_bf16 = lambda xs: [x.astype(BF16) for x in xs]
    saved = (as_bf16(tinv_p), as_bf16(aak_p), as_bf16(prb_p), as_bf16(prk_p), as_bf16(ah), wh)
    return g, h, rh, yh, saved


def wkv_chunks_grad(chunks, saved, m0, dy, dm1, masks, between_stages=lambda: None):
    ms, keep, eye, bd, eye2 = masks
    n = len(chunks)
    q = CHUNK
    at, bt, kt, rt, v, cl = (list(t) for t in zip(*chunks))
    tinv_p, aak_p, prb_p, prk_p, ah, wh = (list(t) for t in zip(*saved))
    head_stack = lambda p: rows(p[:, :q], p[:, q:])
    bc = [b * c_ for b, c_ in zip(bt, cl)]
    kc = [k * c_ for k, c_ in zip(kt, cl)]
    u = [mm(a, m) + w for a, m, w in zip(ah, m0, wh)]
    between_stages()
    dm1 = [d * bd for d in dm1]
    from_state = [mm(rows(b, k), d) for b, k, d in zip(bc, kc, dm1)]
    between_stages()
    dy_rows = [head_rows(d, ms) for d in dy]
    from_out = [mm_tn(lanes(head_stack(pb), head_stack(pk)), d) for pb, pk, d in zip(prb_p, prk_p, dy_rows)]
    between_stages()
    du = [a[:q] + b[:q] for a, b in zip(from_state, from_out)]
    dv = [a[q:] + b[q:] for a, b in zip(from_state, from_out)]
    dz = [mm_tn(head_stack(t), head_rows(d, ms)) for t, d in zip(tinv_p, du)]
    between_stages()
    dz_rows = [head_rows(d, ms) for d in dz]
    dv = [a + mm_tn(head_stack(k), d) for a, k, d in zip(dv, aak_p, dz_rows)]
    between_stages()
    by_m0 = [mm_nt(rows(d, z), m) for d, z, m in zip(dy, dz, m0)]
    between_stages()
    uv = [rows(x, y) for x, y in zip(u, v)]
    by_dm1 = [mm_nt(x, d) for x, d in zip(uv, dm1)]
    between_stages()
    udm = [x[:q] for x in by_dm1]
    vdm = [x[q:] for x in by_dm1]
    dscores = [jnp.where(keep, mm_nt(rows(z, d), x), 0.0) for z, d, x in zip(dz_rows, dy_rows, uv)]
    between_stages()
    to_ar = [mm(d, rows(b, k)) for d, b, k in zip(dscores, bt, kt)]
    to_bk = [mm_tn(d, wkv_score_stack(a, r, ms)) for d, a, r in zip(dscores, at, rt)]
    ones = jnp.ones((8, PAIR), F32)
    upper = (lane_iota((CHUNK, CHUNK)) >= row_iota((CHUNK, CHUNK))).astype(F32)
    out = []
    for c in range(n):
        e = to_ar[c]
        dat_c = by_m0[c][q:] + e[:q] * ms[0] + e[q:2 * q] * ms[1]
        drt_c = by_m0[c][:q] + e[2 * q:3 * q] * ms[0] + e[3 * q:] * ms[1]
        dbt_c = udm[c] * cl[c] + to_bk[c][:q]
        dkt_c = vdm[c] * cl[c] + to_bk[c][q:]
        dlcl = ones_dot_nt(ones, dm1[c] * m0[c], 3)[0:1, :] * cl[c] + colsum(bc[c] * udm[c] + kc[c] * vdm[c])
        g = drt_c * rt[c] - dbt_c * bt[c] - dkt_c * kt[c] + dat_c * at[c]
        dlw = ones_dot(upper, g, 3) - dat_c * at[c] + dlcl
        out.append((dat_c, dbt_c, dkt_c, drt_c, dv[c], dlw))
    return out


def wkv_forward(at, bt, kt, rt, v, clf):
    n_rows = at.shape[0]
    cps = WKV_CHUNKS_PER_STEP
    rb = cps * CHUNK
    n_steps = n_rows // rb

    def body(a_ref, b_ref, k_ref, r_ref, v_ref, c_ref, y_ref, m0_ref, g_ref, rh_ref, *rest):
        saved_refs, m_scr = rest[:6], rest[6]

        @pl.when(pl.program_id(1) == 0)
        def _():
            m_scr[...] = jnp.zeros_like(m_scr)

        masks = wkv_masks()
        chunks = []
        for cc in range(cps):
            sl = slice(cc * CHUNK, (cc + 1) * CHUNK)
            chunks.append((a_ref[sl, :], b_ref[sl, :], k_ref[sl, :], r_ref[sl, :], v_ref[sl, :],
                           c_ref[cc * CHUNK:cc * CHUNK + 1, :]))
        state = [m_scr[...]]
        pending = []

        def chain_step():
            if not pending:
                return
            cc, g, h, rh, yh = pending.pop(0)
            sl = slice(cc * CHUNK, (cc + 1) * CHUNK)
            m = state[0]
            m0_ref[0, cc] = m
            g_ref[0, cc] = g
            rh_ref[sl, :] = rh
            y_ref[sl, :] = hdot(rh, m) + yh
            state[0] = hdot(g, m) + h

        def prepare(first, last, between_stages):
            gs, hs, rhs, yhs, saved = wkv_chunks_pre(chunks[first:last], masks, between_stages)
            for ref, per_chunk in zip(saved_refs, saved):
                for cc, val in enumerate(per_chunk, start=first):
                    ref[cc * CHUNK:(cc + 1) * CHUNK, :] = val
            pending.extend(zip(range(first, last), gs, hs, rhs, yhs))

        group = cps // WKV_CHAIN_GROUPS
        for first in range(0, cps, group):
            prepare(first, first + group, chain_step)
        while pending:
            chain_step()
        m_scr[...] = state[0]

    blk = pl.BlockSpec((rb, PAIR), lambda p, s: (s, p))
    state_blk = pl.BlockSpec((1, cps, PAIR, PAIR), lambda p, s: (p, s, 0, 0))
    state_shape = jax.ShapeDtypeStruct((WIDTH // PAIR, n_rows // CHUNK, PAIR, PAIR), F32)
    rows_f32 = jax.ShapeDtypeStruct((n_rows, WIDTH), F32)
    rows_bf16 = jax.ShapeDtypeStruct((n_rows, WIDTH), BF16)
    return pl.pallas_call(
        body, name="wkv_forward", grid=(WIDTH // PAIR, n_steps),
        in_specs=[blk] * 6,
        out_specs=[blk, state_blk, state_blk, blk] + [blk] * 6,
        out_shape=[rows_f32, state_shape, state_shape, rows_f32] + [rows_bf16] * 5 + [rows_f32],
        scratch_shapes=[pltpu.VMEM((PAIR, PAIR), F32)],
        compiler_params=pltpu.CompilerParams(dimension_semantics=("arbitrary", "arbitrary"),
                                             vmem_limit_bytes=VMEM_LIMIT),
    )(at, bt, kt, rt, v, clf)


def wkv_backward(at, bt, kt, rt, v, clf, m0s, gs, rh, saved, dy):
    n_rows = at.shape[0]
    cps = WKV_CHUNKS_PER_STEP
    rb = cps * CHUNK
    n_steps = n_rows // rb

    def body(a_ref, b_ref, k_ref, r_ref, v_ref, c_ref, m0_ref, g_ref, rh_ref, *rest):
        saved_refs, dy_ref = rest[:6], rest[6]
        da_ref, db_ref, dk_ref, dr_ref, dv_ref, dlw_ref, dm_scr = rest[7:]

        @pl.when(pl.program_id(1) == 0)
        def _():
            dm_scr[...] = jnp.zeros_like(dm_scr)

        masks = wkv_masks()
        bd = masks[3]
        state = [dm_scr[...]]
        dm1 = [None] * cps
        todo = list(reversed(range(cps)))

        def chain_step():
            if not todo:
                return
            cc = todo.pop(0)
            sl = slice(cc * CHUNK, (cc + 1) * CHUNK)
            dm1[cc] = state[0]
            state[0] = bd * (hdot_tn(g_ref[0, cc], state[0]) + hdot_tn(rh_ref[sl, :], dy_ref[sl, :]))

        def gradients(first, last, between_stages):
            chunks, kept, m0, dys = [], [], [], []
            for cc in range(first, last):
                sl = slice(cc * CHUNK, (cc + 1) * CHUNK)
                chunks.append((a_ref[sl, :], b_ref[sl, :], k_ref[sl, :], r_ref[sl, :], v_ref[sl, :],
                               c_ref[cc * CHUNK:cc * CHUNK + 1, :]))
                kept.append(tuple(ref[sl, :] for ref in saved_refs))
                m0.append(m0_ref[0, cc])
                dys.append(dy_ref[sl, :])
            grads = wkv_chunks_grad(chunks, kept, m0, dys, dm1[first:last], masks, between_stages)
            for cc, (dat, dbt, dkt, drt, dv, dlw) in enumerate(grads, start=first):
                sl = slice(cc * CHUNK, (cc + 1) * CHUNK)
                da_ref[sl, :] = dat
                db_ref[sl, :] = dbt
                dk_ref[sl, :] = dkt
                dr_ref[sl, :] = drt
                dv_ref[sl, :] = dv
                dlw_ref[sl, :] = dlw

        group = cps // WKV_CHAIN_GROUPS
        for first in reversed(range(0, cps, group)):
            while todo and todo[0] >= first:
                chain_step()
            gradients(first, first + group, chain_step)
        dm_scr[...] = state[0]

    blk = pl.BlockSpec((rb, PAIR), lambda p, s: (n_steps - 1 - s, p))
    state_blk = pl.BlockSpec((1, cps, PAIR, PAIR), lambda p, s: (p, n_steps - 1 - s, 0, 0))
    return pl.pallas_call(
        body, name="wkv_backward", grid=(WIDTH // PAIR, n_steps),
        in_specs=[blk] * 6 + [state_blk, state_blk, blk] + [blk] * 6 + [blk],
        out_specs=[blk] * 6,
        out_shape=[jax.ShapeDtypeStruct((n_rows, WIDTH), F32)] * 6,
        scratch_shapes=[pltpu.VMEM((PAIR, PAIR), F32)],
        compiler_params=pltpu.CompilerParams(dimension_semantics=("arbitrary", "arbitrary"),
                                             vmem_limit_bytes=VMEM_LIMIT),
    )(at, bt, kt, rt, v, clf, m0s, gs, rh, *saved, dy)


def visible(q_row0, k_row0, shape):
    qc = (q_row0 + row_iota(shape)) // CHUNK
    kc = (k_row0 + lane_iota(shape)) // CHUNK
    return kc <= qc


def attention_forward(q, k, v):
    n_rows = q.shape[0]
    tq, tk = ATTN_FWD_TILES
    n_q = n_rows // tq
    assert tk % tq == 0

    def body(q_ref, k_ref, v_ref, o_ref, lse_ref):
        i = pl.program_id(1)
        lane = lane_iota((tq, LANE))
        heads = [slice(0, LANE), slice(LANE, 2 * LANE)]
        qs = [q_ref[:, cols] for cols in heads]

        def step(j, carry, size, masked):
            rows = pl.ds(pl.multiple_of(j * size, size), size)
            ss = [mm_nt(qh, k_ref[rows, cols]) for qh, cols in zip(qs, heads)]
            if masked:
                vis = visible(i * tq, j * size, ss[0].shape)
                ss = [jnp.where(vis, s, -jnp.inf) for s in ss]
            ps, stats = [], []
            for s, (m, l, _) in zip(ss, carry):
                m_new = jnp.maximum(m, jnp.max(s, axis=-1, keepdims=True))
                p = jnp.exp2(s - m_new)
                alpha = jnp.exp2(m - m_new)
                ps.append(p)
                stats.append((m_new, alpha, alpha * l + jnp.sum(p, axis=-1, keepdims=True)))
            pvs = [mm(p, v_ref[rows, cols]) for p, cols in zip(ps, heads)]
            return tuple((m_new, l, alpha * acc + pv)
                         for (m_new, alpha, l), (_, _, acc), pv in zip(stats, carry, pvs))

        carry = tuple((jnp.full((tq, 1), -jnp.inf, F32), jnp.zeros((tq, 1), F32), jnp.zeros((tq, LANE), F32))
                      for _ in heads)
        n_full = (i * tq) // tk
        carry = lax.fori_loop(0, n_full, functools.partial(step, size=tk, masked=False), carry)
        (m0, l0, acc0), (m1, l1, acc1) = step(n_full, carry, size=tk, masked=True)
        o_ref[...] = acc0 / l0 + acc1 / l1
        lse_ref[...] = jnp.where(lane >= HEAD, m1 + jnp.log2(l1), m0 + jnp.log2(l0))

    return pl.pallas_call(
        body, name="attention_forward", grid=(HEADS // 2, n_q),
        in_specs=[pl.BlockSpec((tq, 2 * LANE), lambda p, i: (i, p)),
                  pl.BlockSpec((n_rows, 2 * LANE), lambda p, i: (0, p)),
                  pl.BlockSpec((n_rows, 2 * LANE), lambda p, i: (0, p))],
        out_specs=[pl.BlockSpec((tq, LANE), lambda p, i: (i, p))] * 2,
        out_shape=[jax.ShapeDtypeStruct((n_rows, WIDTH), F32)] * 2,
        compiler_params=pltpu.CompilerParams(dimension_semantics=("arbitrary", "arbitrary"),
                                             vmem_limit_bytes=VMEM_LIMIT),
    )(q, k, v)


def block_exchange(g_refs, rg_refs, send_sems, recv_sems, local_sems):
    n = len(g_refs)
    me = my_position()
    mi = flat_index(me)

    def copies(k, src_index, dst_index):
        return [pltpu.make_async_remote_copy(
            src_ref=g_refs[a].at[src_index], dst_ref=rg_refs[a].at[dst_index],
            send_sem=send_sems.at[7 * a + k - 1], recv_sem=recv_sems.at[7 * a + k - 1],
            device_id=flip(me, k), device_id_type=MESH_IDS) for a in range(n)]

    local = [pltpu.make_async_copy(g_refs[a].at[mi], rg_refs[a].at[mi], local_sems.at[a]) for a in range(n)]

    def start():
        for cp in local:
            cp.start()
        for k in range(1, N_DEV):
            for cp in copies(k, flat_index(flip(me, k)), mi):
                cp.start()

    def wait():
        for k in range(1, N_DEV):
            pi = flat_index(flip(me, k))
            for cp in copies(k, pi, pi):
                cp.wait_recv()
        for k in range(1, N_DEV):
            for cp in copies(k, flat_index(flip(me, k)), mi):
                cp.wait_send()
        for cp in local:
            cp.wait()

    return start, wait


def attention_backward(q, k, v, o, do, lse, riders):
    n_rows = q.shape[0]
    tq, tk = ATTN_BWD_TILES
    n_q = n_rows // tq
    n_k = n_rows // tk
    n_masked = max(1, tk // tq)
    n_r = len(riders)

    def body(q_ref, k_ref, v_ref, o_ref, do_ref, lse_ref, *rest):
        g_refs = rest[:n_r]
        dq_ref, dk_ref, dv_ref = rest[n_r:n_r + 3]
        rg_refs = rest[n_r + 3:2 * n_r + 3]
        start_riders, wait_riders = block_exchange(g_refs, rg_refs, *rest[2 * n_r + 3:])
        j = pl.program_id(1)

        @pl.when(jnp.logical_and(pl.program_id(0) == 0, j == 0))
        def _():
            start_riders()

        @pl.when(j == 0)
        def _():
            dq_ref[...] = jnp.zeros_like(dq_ref)

        lane = lane_iota((tq, LANE))
        heads = [slice(0, LANE), slice(LANE, 2 * LANE)]
        ks = [k_ref[:, cols] for cols in heads]
        vs = [v_ref[:, cols] for cols in heads]
        head_lanes = [(lane < HEAD).astype(F32), (lane >= HEAD).astype(F32)]

        def step(i, carry, masked):
            rows = pl.ds(pl.multiple_of(i * tq, tq), tq)
            qs = [q_ref[rows, cols] for cols in heads]
            dout = do_ref[rows, :]
            dout_o = dout * o_ref[rows, :]
            lse_t = lse_ref[rows, :]
            ss = [mm_nt(qh, kh) for qh, kh in zip(qs, ks)]
            dps = [mm_nt(dout, vh) for vh in vs]
            ps, dss = [], []
            for hh in range(2):
                delta = jnp.sum(dout_o * head_lanes[hh], axis=-1, keepdims=True)
                lse_h = jnp.sum(jnp.where(lane == hh * HEAD, lse_t, 0.0), axis=-1, keepdims=True)
                p = jnp.exp2(ss[hh] - lse_h)
                if masked:
                    p = jnp.where(visible(i * tq, j * tk, p.shape), p, 0.0)
                ps.append(p)
                dss.append(p * (dps[hh] - delta))
            dvs = [mm_tn(p, dout) for p in ps]
            dqs = [mm(ds, kh) for ds, kh in zip(dss, ks)]
            dks = [mm_tn(ds, qh) for ds, qh in zip(dss, qs)]
            for cols, dq in zip(heads, dqs):
                dq_ref[rows, cols] += dq * ATTN_SCALE
            return tuple((dk + a, dv + b) for (dk, dv), a, b in zip(carry, dks, dvs))

        carry = tuple((jnp.zeros((tk, LANE), F32), jnp.zeros((tk, LANE), F32)) for _ in heads)
        i_first = (j * tk) // tq
        for extra in range(n_masked):
            carry = step(i_first + extra, carry, masked=True)
        carry = lax.fori_loop(i_first + n_masked, n_q, functools.partial(step, masked=False), carry)
        for cols, (dk, dv) in zip(heads, carry):
            dk_ref[:, cols] = dk * (1.0 / LOG2_E)
            dv_ref[:, cols] = dv

        @pl.when(jnp.logical_and(pl.program_id(0) == HEADS // 2 - 1, j == n_k - 1))
        def _():
            wait_riders()

    full = lambda w: pl.BlockSpec((n_rows, w), lambda p, j: (0, p))
    blk = pl.BlockSpec((tk, 2 * LANE), lambda p, j: (j, p))
    hbm = pl.BlockSpec(memory_space=pl.ANY)
    return pl.pallas_call(
        body, name="attention_backward", grid=(HEADS // 2, n_k),
        in_specs=[full(2 * LANE), blk, blk, full(LANE), full(LANE), full(LANE)] + [hbm] * n_r,
        out_specs=[full(2 * LANE), blk, blk] + [hbm] * n_r,
        out_shape=[jax.ShapeDtypeStruct((n_rows, HEADS * LANE), F32)] * 3
        + [jax.ShapeDtypeStruct(r.shape, r.dtype) for r in riders],
        scratch_shapes=[pltpu.SemaphoreType.DMA((7 * n_r,)), pltpu.SemaphoreType.DMA((7 * n_r,)),
                        pltpu.SemaphoreType.DMA((n_r,))],
        compiler_params=pltpu.CompilerParams(dimension_semantics=("arbitrary", "arbitrary"),
                                             vmem_limit_bytes=VMEM_LIMIT),
    )(q, k, v, o, do, lse, *riders)


def tail_tile(step0, tile0, x, tgt, ma, mb, gpa, gpb, ya, y, ur, k2, uv,
              mod, wpa, wpb, wout, gn_g, gn_b, r_k, post_g, post_b, bd):
    gate = mod[2:3]
    inv = 1.0 / HEAD
    yc = y - head_sum(y, bd) * inv
    rs = lax.rsqrt(head_sum(yc * yc, bd) * inv + GN_EPS)
    yn = yc * rs
    yb = yn * gn_g + gn_b + head_sum(ur * k2 * r_k, bd) * uv
    sga, sgb = sigmoid(gpa), sigmoid(gpb)
    sila, silb = gpa * sga, gpb * sgb
    ga, gb = ya * sila, yb * silb
    pa, pb = mm(ga, wpa), mm(gb, wpb)
    sa, sb = sigmoid(ma), sigmoid(mb)
    merged = sa * pa + sb * pb
    sub = mm(merged, wout)
    z = ALPHA * x + (1.0 + gate) * sub
    zhat, rstd = layer_norm_stats(z)
    err = zhat * post_g + post_b - tgt
    loss = 0.5 * jnp.sum(rowmean(err * err), axis=0, keepdims=True) + jnp.zeros((1, LANE), F32)
    dout = err * (1.0 / D_MODEL)
    dpost_g = colsum(dout * zhat)
    dpost_b = colsum(dout)
    dz = layer_norm_bwd(dout * post_g, zhat, rstd)
    dgate = colsum(dz * sub)
    dsub = dz * (1.0 + gate)
    dwout = mm_tn(merged, dsub)
    dmerged = mm_nt(dsub, wout)
    dpa, dpb = dmerged * sa, dmerged * sb
    dma = dmerged * pa * sa * (1.0 - sa)
    dmb = dmerged * pb * sb * (1.0 - sb)
    dwpa = mm_tn(ga, dpa)
    dwpb = mm_tn(gb, dpb)
    dga = mm_nt(dpa, wpa)
    dgb = mm_nt(dpb, wpb)
    dya = dga * sila
    dgpa = dga * ya * (sga * (1.0 + gpa * (1.0 - sga)))
    dyb = dgb * silb
    dgpb = dgb * yb * (sgb * (1.0 + gpb * (1.0 - sgb)))
    dgn_g = colsum(dyb * yn)
    dgn_b = colsum(dyb)
    dyn = dyb * gn_g
    dy = rs * (dyn - head_sum(dyn, bd) * inv - yn * head_sum(dyn * yn, bd) * inv)
    return (dz, dma, dmb, dgpa, dgpb, dya, dy, dyb,
            loss, dwout, dwpa, dwpb, dgn_g, dgn_b, dpost_g, dpost_b, dgate)


def mla_prep_bwd_tile(step0, tile0, q_c, kv_c, cos, sin, dq, dk, dv, gq, gkv, wq, wqr, wkn, wv):
    qn, qh, rq = rms_norm_fwd(q_c, gq)
    kvn, kvh, rkv = rms_norm_fwd(kv_c, gkv)
    dqc = dq * tile_lanes(cos, HEADS)
    dqs = dq * tile_lanes(sin, HEADS)
    dqn = mm_nt(dqc, wq) + mm_nt(dqs, wqr)
    dkvn = mm_nt(dk, wkn) + mm_nt(dv, wv)
    dkpe = dk[:, 0:LANE]
    for h in range(1, HEADS):
        dkpe = dkpe + dk[:, h * LANE:(h + 1) * LANE]
    dkr = dkpe * (cos * key_rope_mask(cos.shape))
    dkrr = dkpe * sin

    def rms_bwd(dyv, xh, r, g):
        dyg = dyv * g
        return r * (dyg - xh * rowmean(dyg * xh)), colsum(dyv * xh)

    dq_c, dgq = rms_bwd(dqn, qh, rq, gq)
    dkv_c, dgkv = rms_bwd(dkvn, kvh, rkv, gkv)
    return (dq_c, dkv_c, dkr, dkrr,
            mm_tn(qn, dqc), mm_tn(qn, dqs), mm_tn(kvn, dk), mm_tn(kvn, dv), dgq, dgkv)


def rwkv_prep_bwd_tile(step0, tile0, r0, k0, v0, l0, drt, dat, dbt, dkt, dvv, dlw, dyb, hr, hk, hv, hl,
                       mu_r, mu_k, mu_v, mu_l, w0, a0, k_k, k_a, w_dec, w_iclr, tril, same, bd, r_k,
                       cr, ck, cv, cl_):
    f = rwkv_prep_core(tile0, r0, k0, v0, l0, hr, hk, hv, hl, mu_r, mu_k, mu_v, mu_l, w0, a0, k_k, k_a,
                       w_dec, w_iclr, tril, same, bd)
    ur, uk, uv, ul, kk, k2, a_ic, sg, th = (f[n] for n in ("ur", "uk", "uv", "ul", "kk", "k2", "a_ic", "sg", "th"))
    lc, lw = f["lc"], f["lw"]
    e_neg = jnp.exp(-lc)
    dur = drt * jnp.exp(lc)
    da = dat * jnp.exp(lc - lw)
    db = dbt * e_neg
    dk2 = dkt * e_neg
    s = head_sum(ur * k2 * r_k, bd)
    duv = dvv + dyb * s
    ds = head_sum(dyb * uv, bd)
    dur = dur + ds * k2 * r_k
    dk2 = dk2 + ds * ur * r_k
    dr_k = colsum(ds * ur * k2)
    dkk = db * a_ic - da
    da_ic = db * kk + dk2 * uk * k_a
    duk = dk2 * (1.0 + (a_ic - 1.0) * k_a)
    dk_a = colsum(dk2 * uk * (a_ic - 1.0))
    dkkraw = jnp.where(f["nrm_raw"] > 1e-12, (dkk - kk * head_sum(dkk * kk, bd)) / f["nrm"], dkk * 1e12)
    duk = duk + dkkraw * k_k
    dk_k = colsum(dkkraw * uk)
    dai = da_ic * a_ic * (1.0 - a_ic)
    dd = dlw * (-DECAY_SCALE) * sg * (1.0 - sg)
    dul = mm_nt(dai, w_iclr) + mm_nt(dd, w_dec) * (1.0 - th * th)

    def unshift(du, x, prev, mu, carry_row):
        nxt = shift_rows_up(du, carry_row)
        return du * (1.0 - mu) + nxt * mu, colsum(du * (prev - x)), du[0:1, :]

    dr0, dmu_r, ncr = unshift(dur, r0, f["pr"], mu_r, cr)
    dk0, dmu_k, nck = unshift(duk, k0, f["pk"], mu_k, ck)
    dv0, dmu_v, ncv = unshift(duv, v0, f["pv"], mu_v, cv)
    dl0, dmu_l, ncl = unshift(dul, l0, f["pl"], mu_l, cl_)
    return (dr0, dk0, dv0, dl0,
            dmu_r, dmu_k, dmu_v, dmu_l, colsum(dd), colsum(dai), dk_k, dk_a, dr_k, mm_tn(th, dd), mm_tn(ul, dai),
            ncr, nck, ncv, ncl)


def in_backward(x, dz, pieces, mod, w_in_pt, unrot):
    n_rows = x.shape[0]
    ts = ROW_TILE
    n_p = len(pieces)
    shard_cols = IN_WIDTH // N_DEV

    def body(*refs):
        x_ref, dz_ref = refs[:2]
        p_refs = refs[2:2 + n_p]
        mod_ref, w_ref, unrot_ref = refs[2 + n_p:5 + n_p]
        dx_ref, ht_ref, blocks_ref, dshift_ref, dscale_ref = refs[5 + n_p:]
        step0 = pl.program_id(0) == 0
        dma, dmb, dr0, dk0, dv0, dgpa, dgpb, dq_c, dkv_c, dkr, dkrr, dl0 = (r[...] for r in p_refs)
        dproj = jnp.concatenate([dma, dmb, dr0, dk0, dv0, dgpa, dgpb, dq_c, dkv_c, dkr, dkrr, dl0], axis=1)
        dh = mm(dproj, w_ref[...])
        xhat, rstd = layer_norm_stats(x_ref[...])
        scale1 = 1.0 + mod_ref[1:2, :]
        dx_ref[...] = layer_norm_bwd(dh * scale1, xhat, rstd) + ALPHA * dz_ref[...]
        ht_ref[...] = jnp.transpose(xhat * scale1 + mod_ref[0:1, :]).astype(BF16)
        dkrope = (dkr.astype(F32) + mm(dkrr, unrot_ref[...]))[:, NOPE:QK_DIM]
        natural = jnp.concatenate(
            [dq_c.astype(F32), dkv_c.astype(F32), dkrope]
            + [p.astype(F32) for p in (dgpa, dr0, dk0, dv0, dl0, dgpb, dma, dmb)], axis=1)
        for j in range(N_DEV):
            blocks_ref[j] = natural[:, j * shard_cols:(j + 1) * shard_cols].astype(BF16)
        for ref, val in ((dshift_ref, colsum(dh)), (dscale_ref, colsum(dh * xhat))):
            @pl.when(step0)
            def _(ref=ref, val=val):
                ref[...] = val

            @pl.when(jnp.logical_not(step0))
            def _(ref=ref, val=val):
                ref[...] += val

    row = lambda w: pl.BlockSpec((ts, w), lambda i: (i, 0))
    const = pl.BlockSpec(memory_space=pltpu.VMEM)
    vec = pl.BlockSpec((1, D_MODEL), lambda i: (0, 0))
    return pl.pallas_call(
        body, name="in_backward", grid=(n_rows // ts,),
        in_specs=[row(D_MODEL), row(D_MODEL)] + [row(p.shape[1]) for p in pieces] + [const] * 3,
        out_specs=[row(D_MODEL), pl.BlockSpec((D_MODEL, ts), lambda i: (0, i)),
                   pl.BlockSpec((N_DEV, ts, shard_cols), lambda i: (0, i, 0)), vec, vec],
        out_shape=[jax.ShapeDtypeStruct((n_rows, D_MODEL), F32), jax.ShapeDtypeStruct((D_MODEL, n_rows), BF16),
                   jax.ShapeDtypeStruct((N_DEV, n_rows, shard_cols), BF16),
                   jax.ShapeDtypeStruct((1, D_MODEL), F32), jax.ShapeDtypeStruct((1, D_MODEL), F32)],
        compiler_params=pltpu.CompilerParams(dimension_semantics=("arbitrary",), vmem_limit_bytes=VMEM_LIMIT),
    )(x, dz, *pieces, mod, w_in_pt, unrot)


def in_weight_grad_exchange(h_t, dp_blocks, others, small, order):
    n = len(others)
    n_rows = h_t.shape[1]
    ts = 4 * ROW_TILE
    n_i = n_rows // ts
    shard_cols = dp_blocks.shape[2]
    n_chips = N_DEV // 2
    last = N_DEV - 1

    def body(order_ref, h_ref, dp_ref, *rest):
        g_refs, s_ref = rest[:n], rest[n]
        rwin_ref, rg_refs, rs_ref = rest[n + 1], rest[n + 2:2 * n + 2], rest[2 * n + 2]
        (acc, sendbuf, sib_buf, sib_send, sib_recv, win_send, win_recv,
         o_send, o_recv, local_sems) = rest[2 * n + 3:]
        b, i = pl.program_id(0), pl.program_id(1)
        me = my_position()
        mi = flat_index(me)
        sibling = (me[0], me[1], 1 - me[2])

        def other_copies(k, src_index, dst_index):
            peer = flip(me, k)
            out = [pltpu.make_async_remote_copy(
                src_ref=g_refs[a].at[src_index], dst_ref=rg_refs[a].at[dst_index],
                send_sem=o_send.at[(n + 1) * (k - 1) + a], recv_sem=o_recv.at[(n + 1) * (k - 1) + a],
                device_id=peer, device_id_type=MESH_IDS) for a in range(n)]
            out.append(pltpu.make_async_remote_copy(
                src_ref=s_ref, dst_ref=rs_ref.at[dst_index],
                send_sem=o_send.at[(n + 1) * (k - 1) + n], recv_sem=o_recv.at[(n + 1) * (k - 1) + n],
                device_id=peer, device_id_type=MESH_IDS))
            return out

        def local_copies():
            out = [pltpu.make_async_copy(g_refs[a].at[mi], rg_refs[a].at[mi], local_sems.at[a]) for a in range(n)]
            out.append(pltpu.make_async_copy(s_ref, rs_ref.at[mi], local_sems.at[n]))
            return out

        def to_sibling(t):
            return pltpu.make_async_remote_copy(
                src_ref=sendbuf.at[t], dst_ref=sib_buf.at[t], send_sem=sib_send.at[t], recv_sem=sib_recv.at[t],
                device_id=sibling, device_id_type=MESH_IDS)

        def to_owner(t):
            flip_x = (t < 2) * 1
            flip_y = 1 - (t & 1)
            owner = (me[0] ^ flip_x, me[1] ^ flip_y, me[2])
            return pltpu.make_async_remote_copy(
                src_ref=sendbuf.at[n_chips + t], dst_ref=rwin_ref.at[t], send_sem=win_send.at[t],
                recv_sem=win_recv.at[t], device_id=owner, device_id_type=MESH_IDS)

        own_block = pltpu.make_async_copy(sendbuf.at[last], rwin_ref.at[n_chips - 1], local_sems.at[n + 1])

        @pl.when(jnp.logical_and(b == 0, i == 0))
        def _():
            for cp in local_copies():
                cp.start()
            for k in range(1, N_DEV):
                for cp in other_copies(k, flat_index(flip(me, k)), mi):
                    cp.start()

        contrib = jnp.dot(h_ref[...], dp_ref[...], preferred_element_type=F32)

        @pl.when(i == 0)
        def _():
            acc[...] = contrib

        @pl.when(i > 0)
        def _():
            acc[...] += contrib

        slot = order_ref[N_DEV + b]
        t = slot & (n_chips - 1)

        @pl.when(jnp.logical_and(i == n_i - 1, slot < n_chips))
        def _():
            sendbuf[slot] = acc[...].astype(BF16)
            to_sibling(t).start()

        @pl.when(jnp.logical_and(i == n_i - 1, slot >= n_chips))
        def _():
            to_sibling(t).wait_recv()
            sendbuf[slot] = (acc[...] + sib_buf[t].astype(F32)).astype(BF16)

            @pl.when(slot < last)
            def _():
                to_owner(t).start()

            @pl.when(slot == last)
            def _():
                own_block.start()

        @pl.when(jnp.logical_and(b == last, i == n_i - 1))
        def _():
            for t in range(n_chips - 1):
                to_owner(t).wait_recv()
            for k in range(1, N_DEV):
                pi = flat_index(flip(me, k))
                for cp in other_copies(k, pi, pi):
                    cp.wait_recv()
            for t in range(n_chips):
                to_sibling(t).wait_send()
            for t in range(n_chips - 1):
                to_owner(t).wait_send()
            for k in range(1, N_DEV):
                for cp in other_copies(k, flat_index(flip(me, k)), mi):
                    cp.wait_send()
            for cp in local_copies():
                cp.wait()
            own_block.wait()

    hbm = pl.BlockSpec(memory_space=pl.ANY)
    n_sem = 7 * (n + 1)
    grid_spec = pltpu.PrefetchScalarGridSpec(
        num_scalar_prefetch=1, grid=(N_DEV, n_i),
        in_specs=[pl.BlockSpec((D_MODEL, ts), lambda b, i, order: (0, i)),
                  pl.BlockSpec((None, ts, shard_cols), lambda b, i, order: (order[b], i, 0))] + [hbm] * (n + 1),
        out_specs=[hbm] * (n + 2),
        scratch_shapes=[pltpu.VMEM((D_MODEL, shard_cols), F32), pltpu.VMEM((N_DEV, D_MODEL, shard_cols), BF16),
                        pltpu.VMEM((n_chips, D_MODEL, shard_cols), BF16),
                        pltpu.SemaphoreType.DMA((n_chips,)), pltpu.SemaphoreType.DMA((n_chips,)),
                        pltpu.SemaphoreType.DMA((n_chips - 1,)), pltpu.SemaphoreType.DMA((n_chips - 1,)),
                        pltpu.SemaphoreType.DMA((n_sem,)), pltpu.SemaphoreType.DMA((n_sem,)),
                        pltpu.SemaphoreType.DMA((n + 2,))])
    return pl.pallas_call(
        body, name="in_weight_grad_exchange", grid_spec=grid_spec,
        out_shape=[jax.ShapeDtypeStruct((n_chips, D_MODEL, shard_cols), BF16)]
        + [jax.ShapeDtypeStruct(o.shape, o.dtype) for o in others]
        + [jax.ShapeDtypeStruct((N_DEV,) + small.shape, small.dtype)],
        compiler_params=pltpu.CompilerParams(dimension_semantics=("arbitrary", "arbitrary"),
                                             vmem_limit_bytes=VMEM_LIMIT),
    )(order, h_t, dp_blocks, *others, small)


def ada_weight_grad(c_all, dmod_cols):
    def body(c_ref, d_ref, o_ref):
        cv = c_ref[...]
        o_ref[...] = hdot_tn(cv * sigmoid(cv), d_ref[...])

    return pl.pallas_call(
        body, name="ada_weight_grad",
        out_shape=jax.ShapeDtypeStruct((c_all.shape[1], dmod_cols.shape[1]), F32),
    )(c_all, dmod_cols)


def adamw_update(g, w, m, v):
    nm = ADAM_B1 * m + (1.0 - ADAM_B1) * g
    nv = ADAM_B2 * v + (1.0 - ADAM_B2) * (g * g)
    m_hat = nm / (1.0 - ADAM_B1 ** ADAM_STEP)
    v_hat = nv / (1.0 - ADAM_B2 ** ADAM_STEP)
    return -ADAM_LR * (m_hat / (jnp.sqrt(v_hat) + ADAM_EPS) + ADAM_WD * w), nm, nv


def adamw(parts, w, m, v, name):
    k, rows, cols = parts.shape

    def body(p_ref, w_hbm, m_hbm, v_hbm, g_ref, d_ref, nm_ref, nv_ref, w_buf, m_buf, v_buf, sems):
        loads = [pltpu.make_async_copy(src, dst, sems.at[i])
                 for i, (src, dst) in enumerate(((w_hbm, w_buf), (m_hbm, m_buf), (v_hbm, v_buf)))]
        for cp in loads:
            cp.start()
        g = p_ref[0].astype(F32)
        for i in range(1, k):
            g = g + p_ref[i].astype(F32)
        g_ref[0] = g
        for cp in loads:
            cp.wait()
        d_ref[0], nm_ref[0], nv_ref[0] = adamw_update(g, w_buf[0], m_buf[0], v_buf[0])

    hbm = pl.BlockSpec(memory_space=pl.ANY)
    whole = pl.BlockSpec(memory_space=pltpu.VMEM)
    return pl.pallas_call(
        body, name=name,
        in_specs=[whole, hbm, hbm, hbm], out_specs=[whole] * 4,
        out_shape=[jax.ShapeDtypeStruct((1, rows, cols), F32)] * 4,
        scratch_shapes=[pltpu.VMEM((1, rows, cols), F32)] * 3 + [pltpu.SemaphoreType.DMA((3,))],
        compiler_params=pltpu.CompilerParams(vmem_limit_bytes=VMEM_LIMIT),
    )(parts, w, m, v)


def adamw_small(parts, ws, ms, vs):
    k = parts.shape[0]
    n = len(ws)
    sizes = [w.shape[1] for w in ws]

    def body(p_ref, *refs):
        ins, outs = refs[:3 * n], refs[3 * n:]
        g_all = p_ref[0]
        for i in range(1, k):
            g_all = g_all + p_ref[i]
        off = 0
        for a, size in enumerate(sizes):
            g = g_all[:, off:off + size]
            off += size
            d, nm, nv = adamw_update(g, ins[a][...], ins[n + a][...], ins[2 * n + a][...])
            for kind, val in enumerate((g, d, nm, nv)):
                outs[kind * n + a][...] = val

    return pl.pallas_call(
        body, name="adamw_small",
        out_shape=[jax.ShapeDtypeStruct((1, size), F32) for _ in range(4) for size in sizes],
    )(parts, *ws, *ms, *vs)


def columns_from_shards(g, rows, cols):
    return g.reshape(N_DEV, rows, cols).transpose(1, 0, 2).reshape(rows, N_DEV * cols)


def permute_w_in_t(wt):
    z = lambda n: jnp.zeros((n, D_MODEL), wt.dtype)
    krope = wt[N_KROPE:N_KROPE + ROPE]
    krope_rot = jnp.concatenate([-krope[ROPE // 2:], krope[:ROPE // 2]], axis=0)
    rw = N_RWKV
    return jnp.concatenate([
        wt[N_MA:N_MA + 1024], wt[N_MB:N_MB + 1024],
        wt[rw:rw + 512], wt[rw + 512:rw + 1024], wt[rw + 1024:rw + 1536],
        wt[N_GPA:N_GPA + 512], wt[N_GPB:N_GPB + 512],
        wt[N_QC:N_QC + 256], wt[N_KVC:N_KVC + 128],
        z(NOPE), krope, z(LANE - QK_DIM), z(NOPE), krope_rot, z(LANE - QK_DIM),
        wt[rw + 1536:rw + 1664]], axis=0)


def pad_heads_q(w_uq):
    w = w_uq.reshape(Q_RANK, HEADS, QK_DIM)
    zpad = jnp.zeros((Q_RANK, HEADS, LANE - QK_DIM), w.dtype)
    wq = jnp.concatenate([w, zpad], axis=2).reshape(Q_RANK, HEADS * LANE)
    pe = w[:, :, NOPE:]
    rot = jnp.concatenate([-pe[:, :, ROPE // 2:], pe[:, :, :ROPE // 2]], axis=2)
    wqr = jnp.concatenate([jnp.zeros((Q_RANK, HEADS, NOPE), w.dtype), rot, zpad], axis=2).reshape(Q_RANK, HEADS * LANE)
    return wq, wqr


def unpad_heads_q_grad(dwq, dwqr):
    a = dwq.reshape(Q_RANK, HEADS, LANE)
    r = dwqr.reshape(Q_RANK, HEADS, LANE)[:, :, NOPE:QK_DIM]
    pe = a[:, :, NOPE:QK_DIM] + jnp.concatenate([r[:, :, ROPE // 2:], -r[:, :, :ROPE // 2]], axis=2)
    return jnp.concatenate([a[:, :, :NOPE], pe], axis=2).reshape(Q_RANK, HEADS * QK_DIM)


def pad_heads_kv(w_ukv):
    w = w_ukv.reshape(KV_RANK, HEADS, 2 * HEAD)
    z = jnp.zeros((KV_RANK, HEADS, HEAD), w.dtype)
    wkn = jnp.concatenate([w[:, :, :NOPE], z], axis=2).reshape(KV_RANK, HEADS * LANE)
    val = w[:, :, NOPE:]
    odd = (jnp.arange(HEADS) % 2 == 1)[None, :, None]
    wv = jnp.concatenate([jnp.where(odd, 0, val), jnp.where(odd, val, 0)], axis=2).reshape(KV_RANK, HEADS * LANE)
    return wkn, wv


def unpad_heads_kv_grad(dwkn, dwv):
    a = dwkn.reshape(KV_RANK, HEADS, LANE)[:, :, :NOPE]
    b = dwv.reshape(KV_RANK, HEADS, LANE)
    odd = (jnp.arange(HEADS) % 2 == 1)[None, :, None]
    val = jnp.where(odd, b[:, :, HEAD:], b[:, :, :HEAD])
    return jnp.concatenate([a, val], axis=2).reshape(KV_RANK, HEADS * 2 * HEAD)


def kernel(x, c, positions, w_ada, b_ada, w_in, q_norm_g, w_uq, kv_norm_g, w_ukv, mu_rwkv, w0, w_decay_up, a0, w_iclr_up, k_k, k_a, r_k, gn_g, gn_b, w_proj_a, w_proj_b, w_out, post_g, post_b, loss_target, m_w_ada, m_b_ada, m_w_in, m_q_norm_g, m_w_uq, m_kv_norm_g, m_w_ukv, m_mu_rwkv, m_w0, m_w_decay_up, m_a0, m_w_iclr_up, m_k_k, m_k_a, m_r_k, m_gn_g, m_gn_b, m_w_proj_a, m_w_proj_b, m_w_out, m_post_g, m_post_b, v_w_ada, v_b_ada, v_w_in, v_q_norm_g, v_w_uq, v_kv_norm_g, v_w_ukv, v_mu_rwkv, v_w0, v_w_decay_up, v_a0, v_w_iclr_up, v_k_k, v_k_a, v_r_k, v_gn_g, v_gn_b, v_w_proj_a, v_w_proj_b, v_w_out, v_post_g, v_post_b):
    weights = dict(w_ada=w_ada, b_ada=b_ada, w_in=w_in, q_norm_g=q_norm_g, w_uq=w_uq, kv_norm_g=kv_norm_g,
                   w_ukv=w_ukv, mu_rwkv=mu_rwkv, w0=w0, w_decay_up=w_decay_up, a0=a0, w_iclr_up=w_iclr_up,
                   k_k=k_k, k_a=k_a, r_k=r_k, gn_g=gn_g, gn_b=gn_b, w_proj_a=w_proj_a, w_proj_b=w_proj_b,
                   w_out=w_out, post_g=post_g, post_b=post_b)
    mom1 = dict(w_ada=m_w_ada, b_ada=m_b_ada, w_in=m_w_in, q_norm_g=m_q_norm_g, w_uq=m_w_uq, kv_norm_g=m_kv_norm_g,
                w_ukv=m_w_ukv, mu_rwkv=m_mu_rwkv, w0=m_w0, w_decay_up=m_w_decay_up, a0=m_a0, w_iclr_up=m_w_iclr_up,
                k_k=m_k_k, k_a=m_k_a, r_k=m_r_k, gn_g=m_gn_g, gn_b=m_gn_b, w_proj_a=m_w_proj_a, w_proj_b=m_w_proj_b,
                w_out=m_w_out, post_g=m_post_g, post_b=m_post_b)
    mom2 = dict(w_ada=v_w_ada, b_ada=v_b_ada, w_in=v_w_in, q_norm_g=v_q_norm_g, w_uq=v_w_uq, kv_norm_g=v_kv_norm_g,
                w_ukv=v_w_ukv, mu_rwkv=v_mu_rwkv, w0=v_w0, w_decay_up=v_w_decay_up, a0=v_a0, w_iclr_up=v_w_iclr_up,
                k_k=v_k_k, k_a=v_k_a, r_k=v_r_k, gn_g=v_gn_g, gn_b=v_gn_b, w_proj_a=v_w_proj_a, w_proj_b=v_w_proj_b,
                w_out=v_w_out, post_g=v_post_g, post_b=v_post_b)
    names = list(weights)
    n_rows = x.shape[1]
    me = 4 * lax.axis_index("x") + 2 * lax.axis_index("y") + lax.axis_index("c")
    xr = x[0]
    tgt = loss_target[0]
    row = lambda a: a.reshape(1, -1)

    w_in_all, c_all = gather_shards([w_in[0].T.astype(BF16), c])
    c_all = c_all.reshape(N_DEV, D_MODEL)
    w_in_pt = permute_w_in_t(w_in_all.reshape(IN_WIDTH, D_MODEL))

    mod_all = ada_modulation(c_all, w_ada[0], b_ada.reshape(N_DEV, -1))
    mod = lax.dynamic_index_in_dim(mod_all, me, axis=1, keepdims=False).reshape(3, D_MODEL)

    proj, *gathered = fwd_in_gather(xr, mod, w_in_pt, [weights[n][0].astype(BF16) for n, _, _ in SHARDED[1:]])
    pcol = lambda off_, w: (proj, w, off_ // w)
    full = {}
    for (n, r, cdim), part in zip(SHARDED[1:], gathered):
        full[n] = part.reshape(N_DEV * r, cdim) if n == "w_out" else columns_from_shards(part, r, cdim)
    wq, wqr = pad_heads_q(full["w_uq"])
    wkn, wv = pad_heads_kv(full["w_ukv"])
    zl = jnp.zeros((LORA, WIDTH), BF16)
    w_dec = jnp.concatenate([full["w_decay_up"], zl], axis=0)
    w_iclr = jnp.concatenate([zl, full["w_iclr_up"]], axis=0)
    wpa, wpb, wout = full["w_proj_a"], full["w_proj_b"], full["w_out"]

    inv_freq = ROPE_THETA ** (-jnp.arange(0, ROPE, 2, dtype=F32) / ROPE)
    ang = positions[0].astype(F32)[:, None] * inv_freq
    ones_n, zeros_n, zeros_p = jnp.ones((n_rows, NOPE), F32), jnp.zeros((n_rows, NOPE), F32), jnp.zeros((n_rows, LANE - QK_DIM), F32)
    cos_t = jnp.concatenate([ones_n, jnp.cos(ang), jnp.cos(ang), zeros_p], axis=1)
    sin_t = jnp.concatenate([zeros_n, jnp.sin(ang), jnp.sin(ang), zeros_p], axis=1)

    gq, gkv = q_norm_g, kv_norm_g
    mla_consts = [gq, gkv, wq, wqr, wkn, wv]
    q, k, v = row_call(
        "mla_prep", mla_prep_tile, n_rows,
        [pcol(P_QC, 256), pcol(P_KVC, 128), pcol(P_KR, 128), pcol(P_KRR, 128), (cos_t, LANE, 0), (sin_t, LANE, 0)],
        mla_consts, [(HEADS * LANE, BF16)] * 3, tile_rows=PREP_TILE)
    ya, lse = attention_forward(q, k, v)

    def chunk_sum_matrices(n):
        t_idx = jnp.arange(n)
        same_chunk = (t_idx[:, None] // CHUNK) == (t_idx[None, :] // CHUNK)
        return (same_chunk & (t_idx[:, None] >= t_idx[None, :])).astype(F32), same_chunk.astype(F32)

    l_idx = jnp.arange(LANE)
    bd = ((l_idx[:, None] // HEAD) == (l_idx[None, :] // HEAD)).astype(F32)
    mu = mu_rwkv
    mu_r, mu_k, mu_v, mu_l = mu[:, 0:512], mu[:, 512:1024], mu[:, 1024:1536], mu[:, 1536:1664]
    rk_row = row(r_k)
    rwkv_consts = lambda n: [mu_r, mu_k, mu_v, mu_l, w0, a0, k_k, k_a, w_dec, w_iclr, *chunk_sum_matrices(n), bd]
    rwkv_rows = [pcol(P_R, 512), pcol(P_K, 512), pcol(P_V, 512), pcol(P_LORA, 128)]
    rt, at, bt, kt, clf, uv, ur, k2 = row_call(
        "rwkv_prep", rwkv_prep_tile, n_rows, rwkv_rows, rwkv_consts(ROW_TILE), [(WIDTH, F32)] * 8, halo_in=rwkv_rows)
    y, m0s, state_maps, out_maps, *wkv_saved = wkv_forward(at, bt, kt, rt, uv, clf)

    tail = row_call(
        "tail", tail_tile, n_rows,
        [(xr, D_MODEL, 0), (tgt, D_MODEL, 0), pcol(P_MA, 1024), pcol(P_MB, 1024), pcol(P_GPA, 512), pcol(P_GPB, 512),
         (ya, WIDTH, 0), (y, WIDTH, 0), (ur, WIDTH, 0), (k2, WIDTH, 0), (uv, WIDTH, 0)],
        [mod, wpa, wpb, wout, gn_g, gn_b, rk_row, post_g, post_b, bd],
        [(D_MODEL, F32), (1024, BF16), (1024, BF16), (512, BF16), (512, BF16), (WIDTH, F32), (WIDTH, F32), (WIDTH, F32)],
        acc_out=[((1, LANE), F32), ((D_MODEL, D_MODEL), F32), ((WIDTH, D_MODEL), F32), ((WIDTH, D_MODEL), F32),
                 ((1, WIDTH), F32), ((1, WIDTH), F32), ((1, D_MODEL), F32), ((1, D_MODEL), F32), ((1, D_MODEL), F32)])
    (dz, dma, dmb, dgpa, dgpb, dya, dy, dyb,
     loss_row, g_wout, g_wpa, g_wpb, g_gn_g, g_gn_b, g_post_g, g_post_b, dgate) = tail

    def owner_blocks(g, n):
        r, cdim = next((r, cdim) for name, r, cdim in SHARDED if name == n)
        return (g.reshape(N_DEV, r, cdim) if n == "w_out" else g.reshape(r, N_DEV, cdim).transpose(1, 0, 2)).astype(BF16)

    early = ("w_proj_a", "w_proj_b", "w_out")
    dq, dk, dv, *got_early = attention_backward(
        q, k, v, ya, dya, lse, [owner_blocks(g, n) for g, n in zip((g_wpa, g_wpb, g_wout), early)])
    dq_c, dkv_c, dkr, dkrr, g_wq, g_wqr, g_wkn, g_wv, g_gq, g_gkv = row_call(
        "mla_prep_bwd", mla_prep_bwd_tile, n_rows,
        [pcol(P_QC, 256), pcol(P_KVC, 128), (cos_t, LANE, 0), (sin_t, LANE, 0),
         (dq, HEADS * LANE, 0), (dk, HEADS * LANE, 0), (dv, HEADS * LANE, 0)],
        mla_consts, [(256, BF16), (128, BF16), (128, BF16), (128, BF16)],
        acc_out=[((Q_RANK, HEADS * LANE), F32)] * 2 + [((KV_RANK, HEADS * LANE), F32)] * 2
        + [((1, Q_RANK), F32), ((1, KV_RANK), F32)], tile_rows=PREP_TILE)

    dat, dbt, dkt, drt, dvv, dlw = wkv_backward(at, bt, kt, rt, uv, clf, m0s, state_maps, out_maps, wkv_saved, dy)
    (dr0, dk0, dv0, dl0, g_mu_r, g_mu_k, g_mu_v, g_mu_l, g_w0, g_a0, g_k_k, g_k_a, g_r_k, g_wdec, g_wiclr) = row_call(
        "rwkv_prep_bwd", rwkv_prep_bwd_tile, n_rows,
        rwkv_rows + [(drt, WIDTH, 0), (dat, WIDTH, 0), (dbt, WIDTH, 0), (dkt, WIDTH, 0), (dvv, WIDTH, 0),
                     (dlw, WIDTH, 0), (dyb, WIDTH, 0)],
        rwkv_consts(PREP_TILE) + [rk_row], [(512, BF16), (512, BF16), (512, BF16), (128, BF16)],
        acc_out=[((1, 512), F32)] * 3 + [((1, 128), F32)] + [((1, 512), F32)] * 5 + [((LANE, WIDTH), F32)] * 2,
        halo_in=rwkv_rows, carry=[512, 512, 512, 128], reverse=True, tile_rows=PREP_TILE)

    li = jnp.arange(LANE)
    src, dst = li[:, None], li[None, :]
    half = ROPE // 2
    unrot = (jnp.where((dst >= NOPE) & (dst < NOPE + half) & (src == dst + half), 1.0, 0.0)
             - jnp.where((dst >= NOPE + half) & (dst < QK_DIM) & (src == dst - half), 1.0, 0.0)).astype(BF16)
    dx, h_t, dproj_blocks, dshift, dscale = in_backward(
        xr, dz, [dma, dmb, dr0, dk0, dv0, dgpa, dgpb, dq_c, dkv_c, dkr, dkrr, dl0], mod, w_in_pt, unrot)

    late = ("w_uq", "w_ukv", "w_decay_up", "w_iclr_up")
    late_grads = (unpad_heads_q_grad(g_wq, g_wqr), unpad_heads_kv_grad(g_wkn, g_wv), g_wdec[:LORA], g_wiclr[LORA:])
    blocks = [owner_blocks(g, n) for g, n in zip(late_grads, late)]
    dmod = jnp.concatenate([dshift, dscale, dgate], axis=1)
    small = jnp.concatenate([dmod, g_gq, g_gkv, g_mu_r, g_mu_k, g_mu_v, g_mu_l, g_w0, g_a0, g_k_k, g_k_a, g_r_k,
                             g_gn_g, g_gn_b, g_post_g, g_post_b, loss_row], axis=1)
    my_x, my_y, my_c = lax.axis_index("x"), lax.axis_index("y"), lax.axis_index("c")
    chip_order = [4 * (my_x ^ fx) + 2 * (my_y ^ fy) for fx, fy in ((1, 1), (1, 0), (0, 1), (0, 0))]
    owners = [chip_order[s % 4] + (my_c if s >= 4 else 1 - my_c) for s in WGRAD_SLOTS]
    order = jnp.stack(owners + [jnp.int32(s) for s in WGRAD_SLOTS]).astype(jnp.int32)
    got_w_in, *got_late, got_small = in_weight_grad_exchange(h_t, dproj_blocks, blocks, small, order)
    got = {"w_in": got_w_in, **dict(zip(late, got_late)), **dict(zip(early, got_early))}
    loss = jnp.sum(got_small[:, 0, SMALL_ELEMS])

    ada_cols = w_ada.shape[2]
    dmod_all = got_small[:, 0, :3 * D_MODEL]
    g_ada = ada_weight_grad(c_all, lax.dynamic_slice_in_dim(dmod_all, me * ada_cols, ada_cols, axis=1))

    outs = [dict() for _ in range(4)]
    res = adamw(g_ada[None], w_ada, m_w_ada, v_w_ada, "adamw_w_ada")
    for kind in range(4):
        outs[kind]["w_ada"] = res[kind]
    for n, _, _ in SHARDED:
        res = adamw(got[n], weights[n], mom1[n], mom2[n], "adamw_" + n)
        for kind in range(4):
            outs[kind][n] = res[kind]
    rows_of = lambda tree: [tree[n].reshape(1, -1) for n, _ in SMALL]
    res = adamw_small(got_small, rows_of(weights), rows_of(mom1), rows_of(mom2))
    for kind in range(4):
        for a, (n, _) in enumerate(SMALL):
            outs[kind][n] = res[kind * len(SMALL) + a].reshape(weights[n].shape)
    return (loss, dx[None], *[outs[0][n] for n in names], *[outs[1][n] for n in names],
            *[outs[2][n] for n in names], *[outs[3][n] for n in names])
```

```python
import functools
import math

import jax
import jax.numpy as jnp
from jax import lax
from jax.experimental import pallas as pl
from jax.experimental.pallas import tpu as pltpu

F32 = jnp.float32
BF16 = jnp.bfloat16
HIGHEST = lax.Precision.HIGHEST
MESH_IDS = pl.DeviceIdType.MESH

N_DEV = 8
D_MODEL = 1024
LN_EPS = 1e-5
RMS_EPS = 1e-6
GN_EPS = 64e-5
HEADS = 8
Q_RANK = 256
KV_RANK = 128
ROPE = 32
NOPE = 64
QK_DIM = NOPE + ROPE
WIDTH = 512
HEAD = 64
LORA = 64
CHUNK = 64
DEPTH = 1
ALPHA = (2.0 * DEPTH) ** 0.25
ROPE_THETA = 10000.0
ATTN_SCALE = QK_DIM ** -0.5
DECAY_SCALE = math.exp(-0.5)

ADAM_LR = 0.001
ADAM_B1 = 0.9
ADAM_B2 = 0.999
ADAM_EPS = 1e-08
ADAM_WD = 0.01
ADAM_STEP = 10

LANE = 128
PAIR = 2 * HEAD
ROW_TILE = 256
PREP_TILE = 512
HALO_ROWS = 16
ATTN_FWD_TILES = (512, 1024)
ATTN_BWD_TILES = (512, 512)
LOG2_E = math.log2(math.e)
Q_PRESCALE = ATTN_SCALE * LOG2_E
WKV_CHUNKS_PER_STEP = 16
WKV_CHAIN_GROUPS = 2
WGRAD_SLOTS = (0, 1, 4, 2, 5, 6, 3, 7)
RELAY_AFTER = 1
VMEM_LIMIT = 56 * 1024 * 1024

P_MA, P_MB, P_R, P_K, P_V, P_GPA, P_GPB, P_QC, P_KVC, P_KR, P_KRR, P_LORA = (
    0, 1024, 2048, 2560, 3072, 3584, 4096, 4608, 4864, 4992, 5120, 5248)
P_WIDTH = 5376

N_QC, N_KVC, N_KROPE, N_GPA, N_RWKV, N_GPB, N_MA, N_MB = 0, 256, 384, 416, 928, 2592, 3104, 4128
IN_WIDTH = 5152

SHARDED = (("w_in", 1024, 644), ("w_uq", 256, 96), ("w_ukv", 128, 128), ("w_decay_up", 64, 64),
           ("w_iclr_up", 64, 64), ("w_proj_a", 512, 128), ("w_proj_b", 512, 128), ("w_out", 128, 1024))
SMALL = (("b_ada", 3072), ("q_norm_g", 256), ("kv_norm_g", 128), ("mu_rwkv", 1664), ("w0", 512), ("a0", 512),
         ("k_k", 512), ("k_a", 512), ("r_k", 512), ("gn_g", 512), ("gn_b", 512), ("post_g", 1024), ("post_b", 1024))
SMALL_ELEMS = sum(n for _, n in SMALL)


def mm(a, b):
    return jnp.dot(a.astype(BF16), b.astype(BF16), preferred_element_type=F32)


def mm_nt(a, b):
    return lax.dot_general(a.astype(BF16), b.astype(BF16), (((1,), (1,)), ((), ())), preferred_element_type=F32)


def mm_tn(a, b):
    return lax.dot_general(a.astype(BF16), b.astype(BF16), (((0,), (0,)), ((), ())), preferred_element_type=F32)


def hdot(a, b):
    return jnp.dot(a, b, precision=HIGHEST, preferred_element_type=F32)


def hdot_tn(a, b):
    return lax.dot_general(a, b, (((0,), (0,)), ((), ())), precision=HIGHEST, preferred_element_type=F32)


def sigmoid(x):
    return 1.0 / (1.0 + jnp.exp(-x))


def colsum(x):
    return jnp.sum(x, axis=0, keepdims=True)


def rowmean(x):
    return jnp.mean(x, axis=-1, keepdims=True)


def layer_norm_stats(x):
    xc = x - rowmean(x)
    rstd = lax.rsqrt(rowmean(xc * xc) + LN_EPS)
    return xc * rstd, rstd


def layer_norm_bwd(dy, xhat, rstd):
    return rstd * (dy - rowmean(dy) - xhat * rowmean(dy * xhat))


def bf16_pieces(x, n):
    pieces = []
    for _ in range(n):
        p = x.astype(BF16)
        pieces.append(p)
        x = x - p.astype(F32)
    return pieces


def ones_dot(ones, x, n_pieces):
    ones = ones.astype(BF16)
    return sum(jnp.dot(ones, p, preferred_element_type=F32) for p in bf16_pieces(x, n_pieces))


def ones_dot_nt(ones, x, n_pieces):
    ones = ones.astype(BF16)
    return sum(lax.dot_general(ones, p, (((1,), (1,)), ((), ())), preferred_element_type=F32)
               for p in bf16_pieces(x, n_pieces))


def head_sum(x, bd):
    return jnp.concatenate([mm(x[:, p * LANE:(p + 1) * LANE], bd) for p in range(x.shape[1] // LANE)], axis=1)


def tile_lanes(t, n):
    return jnp.concatenate([t] * n, axis=1)


def row_iota(shape):
    return lax.broadcasted_iota(jnp.int32, shape, 0)


def lane_iota(shape):
    return lax.broadcasted_iota(jnp.int32, shape, 1)


def shift_rows_down(x, row0):
    rolled = pltpu.roll(x, 1, axis=0)
    return jnp.where(row_iota(x.shape) == 0, row0, rolled)


def shift_rows_up(x, row_last):
    rolled = pltpu.roll(x, x.shape[0] - 1, axis=0)
    return jnp.where(row_iota(x.shape) == x.shape[0] - 1, row_last, rolled)


def row_call(name, fn, n_rows, row_in, const_in, row_out, acc_out=(), halo_in=(), carry=(), reverse=False,
             tile_rows=ROW_TILE):
    ts = tile_rows
    n_tiles = n_rows // ts
    n_in = len(row_in) + len(halo_in) + len(const_in)
    n_ro, n_ao = len(row_out), len(acc_out)

    def tile_of(g):
        return (n_tiles - 1 - g) if reverse else g

    def body(*refs):
        ins = refs[:n_in]
        ro = refs[n_in:n_in + n_ro]
        ao = refs[n_in + n_ro:n_in + n_ro + n_ao]
        cr = refs[n_in + n_ro + n_ao:]
        g = pl.program_id(0)
        step0 = g == 0
        tile0 = tile_of(g) == 0
        for r in cr:
            @pl.when(step0)
            def _(r=r):
                r[...] = jnp.zeros_like(r)
        n_tiled = len(row_in) + len(halo_in)
        vals = [r[...].astype(F32) for r in ins[:n_tiled]] + [r[...] for r in ins[n_tiled:]]
        outs = fn(step0, tile0, *vals, *[c[0:1, :] for c in cr])
        for r, v in zip(ro, outs[:n_ro]):
            r[...] = v.astype(r.dtype)
        for r, v in zip(ao, outs[n_ro:n_ro + n_ao]):
            @pl.when(step0)
            def _(r=r, v=v):
                r[...] = v.astype(r.dtype)

            @pl.when(jnp.logical_not(step0))
            def _(r=r, v=v):
                r[...] += v.astype(r.dtype)
        for r, v in zip(cr, outs[n_ro + n_ao:]):
            r[0:1, :] = v

    in_specs = [pl.BlockSpec((ts, w), functools.partial(lambda g, cb: (tile_of(g), cb), cb=cb)) for _, w, cb in row_in]
    in_specs += [pl.BlockSpec((HALO_ROWS, w), functools.partial(
        lambda g, cb: (jnp.maximum(tile_of(g) * (ts // HALO_ROWS) - 1, 0), cb), cb=cb)) for _, w, cb in halo_in]
    in_specs += [pl.BlockSpec(memory_space=pltpu.VMEM) for _ in const_in]
    out_specs = [pl.BlockSpec((ts, w), lambda g: (tile_of(g), 0)) for w, _ in row_out]
    out_specs += [pl.BlockSpec(s, lambda g: (0, 0)) for s, _ in acc_out]
    out_shape = [jax.ShapeDtypeStruct((n_rows, w), d) for w, d in row_out]
    out_shape += [jax.ShapeDtypeStruct(s, d) for s, d in acc_out]
    return pl.pallas_call(
        body, name=name, grid=(n_tiles,), in_specs=in_specs, out_specs=out_specs, out_shape=out_shape,
        scratch_shapes=[pltpu.VMEM((8, w), F32) for w in carry],
        compiler_params=pltpu.CompilerParams(dimension_semantics=("arbitrary",), vmem_limit_bytes=VMEM_LIMIT),
    )(*[a for a, _, _ in row_in], *[a for a, _, _ in halo_in], *const_in)


def my_position():
    return lax.axis_index("x"), lax.axis_index("y"), lax.axis_index("c")


def flip(pos, k):
    x, y, c = pos
    dx, dy, dc = (k >> 2) & 1, (k >> 1) & 1, k & 1
    return (1 - x if dx else x, 1 - y if dy else y, 1 - c if dc else c)


def flat_index(pos):
    return 4 * pos[0] + 2 * pos[1] + pos[2]


def gather_shards(shards):
    n = len(shards)

    def body(*refs):
        x_refs, out_refs = refs[:n], refs[n:2 * n]
        send_sems, recv_sems, local_sems = refs[2 * n:]
        x, y, c = my_position()
        me, sibling = (x, y, c), (x, y, 1 - c)
        chips = [(1 - x, y), (x, 1 - y), (1 - x, 1 - y)]
        relay_from = (x ^ (1 - c), y ^ c, c)
        relay_to = (x ^ c, y ^ (1 - c), c)

        def copy(a, k, block, to, from_input=False):
            slot = out_refs[a].at[flat_index(block)]
            return pltpu.make_async_remote_copy(
                src_ref=x_refs[a] if from_input else slot, dst_ref=slot,
                send_sem=send_sems.at[7 * a + k], recv_sem=recv_sems.at[7 * a + k],
                device_id=to, device_id_type=MESH_IDS)

        mine = [pltpu.make_async_copy(x_refs[a], out_refs[a].at[flat_index(me)], local_sems.at[a]) for a in range(n)]
        for cp in mine:
            cp.start()
        first = []
        for a in range(n):
            first.append(copy(a, 0, me, sibling, from_input=True))
            first += [copy(a, 1 + j, me, (*chip, c), from_input=True) for j, chip in enumerate(chips[:2])]
        for cp in first:
            cp.start()
        relayed = [copy(a, 3, relay_from, relay_to) for a in range(n)]
        passed = []
        for j, chip in enumerate(chips):
            for a in range(n):
                copy(a, 1 + j, (*chip, c), me).wait_recv()
                cp = copy(a, 4 + j, (*chip, c), sibling)
                cp.start()
                passed.append(cp)
                if j < 2:
                    @pl.when(c == j)
                    def _(a=a):
                        relayed[a].start()
        for a in range(n):
            copy(a, 0, sibling, me).wait_recv()
            for j, chip in enumerate(chips):
                copy(a, 4 + j, (*chip, 1 - c), me).wait_recv()
        for cp in first + passed + relayed:
            cp.wait_send()
        for cp in mine:
            cp.wait()

    return pl.pallas_call(
        body, name="gather_shards",
        out_shape=[jax.ShapeDtypeStruct((N_DEV,) + s.shape, s.dtype) for s in shards],
        in_specs=[pl.BlockSpec(memory_space=pl.ANY)] * n, out_specs=[pl.BlockSpec(memory_space=pl.ANY)] * n,
        scratch_shapes=[pltpu.SemaphoreType.DMA((7 * n,)), pltpu.SemaphoreType.DMA((7 * n,)),
                        pltpu.SemaphoreType.DMA((n,))],
    )(*shards)


def ada_modulation(c_all, w_ada_loc, b_ada_blocks):
    cols = w_ada_loc.shape[1]

    def body(c_ref, w_ref, b_ref, out_ref, send_sems, recv_sems):
        me = my_position()
        mi = flat_index(me)
        cv = c_ref[...]
        res = hdot(cv * sigmoid(cv), w_ref[...]) + b_ref[pl.ds(mi, 1), :]
        out_ref[mi] = res
        sends = []
        for k in range(1, N_DEV):
            cp = pltpu.make_async_remote_copy(
                src_ref=out_ref.at[mi], dst_ref=out_ref.at[mi], send_sem=send_sems.at[k - 1],
                recv_sem=recv_sems.at[k - 1], device_id=flip(me, k), device_id_type=MESH_IDS)
            cp.start()
            sends.append(cp)
        for k in range(1, N_DEV):
            pi = flat_index(flip(me, k))
            pltpu.make_async_remote_copy(
                src_ref=out_ref.at[pi], dst_ref=out_ref.at[pi], send_sem=send_sems.at[k - 1],
                recv_sem=recv_sems.at[k - 1], device_id=flip(me, k), device_id_type=MESH_IDS).wait_recv()
        for cp in sends:
            cp.wait_send()

    return pl.pallas_call(
        body, name="ada_modulation",
        out_shape=jax.ShapeDtypeStruct((N_DEV, N_DEV, cols), F32),
        in_specs=[pl.BlockSpec(memory_space=pltpu.VMEM)] * 3, out_specs=pl.BlockSpec(memory_space=pltpu.VMEM),
        scratch_shapes=[pltpu.SemaphoreType.DMA((7,)), pltpu.SemaphoreType.DMA((7,))],
    )(c_all, w_ada_loc, b_ada_blocks)


def fwd_in_tile(step0, tile0, x, mod, w_in_pt):
    xhat, _ = layer_norm_stats(x)
    h = xhat * (1.0 + mod[1:2]) + mod[0:1]
    return (mm_nt(h, w_in_pt),)


def fwd_in_gather(x, mod, w_in_pt, shards):
    n = len(shards)
    n_rows = x.shape[0]
    ts = ROW_TILE
    n_tiles = n_rows // ts

    def body(x_ref, mod_ref, w_ref, *rest):
        s_refs = rest[:n]
        proj_ref, out_refs = rest[n], rest[n + 1:2 * n + 1]
        send_sems, recv_sems, local_sems = rest[2 * n + 1:]
        g = pl.program_id(0)
        me = my_position()
        mi = flat_index(me)

        def copies(k, slot):
            return [pltpu.make_async_remote_copy(
                src_ref=s_refs[a], dst_ref=out_refs[a].at[slot], send_sem=send_sems.at[7 * a + k - 1],
                recv_sem=recv_sems.at[7 * a + k - 1], device_id=flip(me, k), device_id_type=MESH_IDS)
                for a in range(n)]

        local = [pltpu.make_async_copy(s_refs[a], out_refs[a].at[mi], local_sems.at[a]) for a in range(n)]

        @pl.when(g == 0)
        def _():
            for cp in local:
                cp.start()
            for k in range(1, N_DEV):
                for cp in copies(k, mi):
                    cp.start()

        proj_ref[...] = fwd_in_tile(None, None, x_ref[...], mod_ref[...], w_ref[...])[0].astype(BF16)

        @pl.when(g == n_tiles - 1)
        def _():
            for k in range(1, N_DEV):
                for cp in copies(k, flat_index(flip(me, k))):
                    cp.wait_recv()
            for k in range(1, N_DEV):
                for cp in copies(k, mi):
                    cp.wait_send()
            for cp in local:
                cp.wait()

    hbm = pl.BlockSpec(memory_space=pl.ANY)
    const = pl.BlockSpec(memory_space=pltpu.VMEM)
    return pl.pallas_call(
        body, name="fwd_in_gather", grid=(n_tiles,),
        in_specs=[pl.BlockSpec((ts, D_MODEL), lambda g: (g, 0)), const, const] + [hbm] * n,
        out_specs=[pl.BlockSpec((ts, P_WIDTH), lambda g: (g, 0))] + [hbm] * n,
        out_shape=[jax.ShapeDtypeStruct((n_rows, P_WIDTH), BF16)]
        + [jax.ShapeDtypeStruct((N_DEV,) + s.shape, s.dtype) for s in shards],
        scratch_shapes=[pltpu.SemaphoreType.DMA((7 * n,)), pltpu.SemaphoreType.DMA((7 * n,)),
                        pltpu.SemaphoreType.DMA((n,))],
        compiler_params=pltpu.CompilerParams(dimension_semantics=("arbitrary",), vmem_limit_bytes=VMEM_LIMIT),
    )(x, mod, w_in_pt, *shards)


def rms_norm_fwd(x, g):
    r = lax.rsqrt(rowmean(x * x) + RMS_EPS)
    xh = x * r
    return xh * g, xh, r


def key_rope_mask(shape):
    return (lane_iota(shape) >= NOPE).astype(F32)


def mla_prep_tile(step0, tile0, q_c, kv_c, kr, krr, cos, sin, gq, gkv, wq, wqr, wkn, wv):
    qn, _, _ = rms_norm_fwd(q_c, gq)
    kvn, _, _ = rms_norm_fwd(kv_c, gkv)
    q = (mm(qn, wq) * tile_lanes(cos, HEADS) + mm(qn, wqr) * tile_lanes(sin, HEADS)) * Q_PRESCALE
    kpe = kr * (cos * key_rope_mask(cos.shape)) + krr * sin
    k = mm(kvn, wkn) + tile_lanes(kpe, HEADS)
    v = mm(kvn, wv)
    return q, k, v


def rwkv_prep_core(tile0, r0, k0, v0, l0, hr, hk, hv, hl, mu_r, mu_k, mu_v, mu_l, w0, a0, k_k, k_a,
                   w_dec, w_iclr, tril, same, bd):
    def shifted(x, halo, mu):
        row0 = jnp.where(tile0, 0.0, halo[HALO_ROWS - 1:HALO_ROWS, :])
        prev = shift_rows_down(x, row0)
        return x + (prev - x) * mu, prev

    ur, pr = shifted(r0, hr, mu_r)
    uk, pk = shifted(k0, hk, mu_k)
    uv, pv = shifted(v0, hv, mu_v)
    ul, plo = shifted(l0, hl, mu_l)
    th = jnp.tanh(ul)
    sg = sigmoid(w0 + mm(th, w_dec))
    lw = -DECAY_SCALE * sg
    a_ic = sigmoid(a0 + mm(ul, w_iclr))
    kkraw = uk * k_k
    nrm_raw = jnp.sqrt(head_sum(kkraw * kkraw, bd))
    nrm = jnp.maximum(nrm_raw, 1e-12)
    kk = kkraw / nrm
    k2 = uk * (1.0 + (a_ic - 1.0) * k_a)
    lc = ones_dot(tril, lw, 3)
    lcl = ones_dot(same, lw, 3)
    return dict(ur=ur, uk=uk, uv=uv, ul=ul, pr=pr, pk=pk, pv=pv, pl=plo, th=th, sg=sg, lw=lw, a_ic=a_ic,
                kkraw=kkraw, nrm_raw=nrm_raw, nrm=nrm, kk=kk, k2=k2, lc=lc, lcl=lcl)


def rwkv_prep_tile(step0, tile0, r0, k0, v0, l0, hr, hk, hv, hl, *consts):
    f = rwkv_prep_core(tile0, r0, k0, v0, l0, hr, hk, hv, hl, *consts)
    lc, lw = f["lc"], f["lw"]
    e_neg = jnp.exp(-lc)
    rt = f["ur"] * jnp.exp(lc)
    at = -f["kk"] * jnp.exp(lc - lw)
    bt = f["kk"] * f["a_ic"] * e_neg
    kt = f["k2"] * e_neg
    return rt, at, bt, kt, jnp.exp(f["lcl"]), f["uv"], f["ur"], f["k2"]


def wkv_masks():
    lane = lane_iota((1, PAIR))
    m_lo = (lane < HEAD).astype(F32)
    r2 = row_iota((PAIR, PAIR))
    c2 = lane_iota((PAIR, PAIR))
    bd = ((r2 < HEAD) == (c2 < HEAD)).astype(F32)
    eye2 = (r2 == c2).astype(F32)
    eye = (row_iota((CHUNK, CHUNK)) == lane_iota((CHUNK, CHUNK))).astype(F32)
    t_idx = row_iota((4 * CHUNK, PAIR)) % CHUNK
    s_idx = lane_iota((4 * CHUNK, PAIR)) % CHUNK
    keep = s_idx < t_idx + (row_iota((4 * CHUNK, PAIR)) >= 2 * CHUNK).astype(jnp.int32)
    return (m_lo, 1.0 - m_lo), keep, eye, bd, eye2


def rows(*parts):
    return jnp.concatenate(parts, axis=0)


def lanes(*parts):
    return jnp.concatenate(parts, axis=1)


def head_rows(x, ms):
    return rows(x * ms[0], x * ms[1])


def wkv_score_stack(at, rt, ms):
    return rows(head_rows(at, ms), head_rows(rt, ms))


def wkv_chunks_pre(chunks, masks, between_stages=lambda: None):
    ms, keep, eye, bd, eye2 = masks
    n = len(chunks)
    at, bt, kt, rt, v, cl = (list(t) for t in zip(*chunks))
    scores = [jnp.where(keep, mm_nt(wkv_score_stack(a, r, ms), rows(b, k)), 0.0)
              for a, r, b, k in zip(at, rt, bt, kt)]
    between_stages()
    q = CHUNK
    aab = [s[h * q:(h + 1) * q, :q] for s in scores for h in range(2)]
    tinv = [eye + a for a in aab]
    power = [mm(a, a) for a in aab]
    between_stages()
    for _ in range(5):
        both = [mm(rows(t, p), p) for t, p in zip(tinv, power)]
        tinv = [t + x[:q] for t, x in zip(tinv, both)]
        power = [x[q:] for x in both]
        between_stages()
    pair = lambda c, row0, col0: lanes(scores[c][row0:row0 + q, col0:col0 + q],
                                       scores[c][row0 + q:row0 + 2 * q, col0:col0 + q])
    tinv_p = [lanes(tinv[2 * c], tinv[2 * c + 1]) for c in range(n)]
    aak_p = [pair(c, 0, q) for c in range(n)]
    prb_p = [pair(c, 2 * q, 0) for c in range(n)]
    prk_p = [pair(c, 2 * q, q) for c in range(n)]
    v_rows = [head_rows(x, ms) for x in v]
    wy = [mm(rows(a, p), x) for a, p, x in zip(aak_p, prk_p, v_rows)]
    between_stages()
    w = [x[:q] for x in wy]
    yh2 = [x[q:] for x in wy]
    aw = [mm(t, lanes(head_rows(a, ms), head_rows(w_, ms))) for t, a, w_ in zip(tinv_p, at, w)]
    between_stages()
    ah = [x[:, :PAIR] for x in aw]
    wh = [x[:, PAIR:] for x in aw]
    ry = [mm(p, lanes(head_rows(a, ms), head_rows(w_, ms))) for p, a, w_ in zip(prb_p, ah, wh)]
    between_stages()
    rh = [r + x[:, :PAIR] for r, x in zip(rt, ry)]
    yh = [x[:, PAIR:] + y for x, y in zip(ry, yh2)]
    bc = [b * c_ for b, c_ in zip(bt, cl)]
    kc = [k * c_ for k, c_ in zip(kt, cl)]
    gh = [mm_tn(b, lanes(a, w_)) for b, a, w_ in zip(bc, ah, wh)]
    g = [eye2 * c_ + bd * x[:, :PAIR] for c_, x in zip(cl, gh)]
    h = [bd * (x[:, PAIR:] + mm_tn(k, v_)) for x, k, v_ in zip(gh, kc, v)]
    as_bf16 = lambda xs: [x.astype(BF16) for x in xs]
    saved = (as_bf16(tinv_p), as_bf16(aak_p), as_bf16(prb_p), as_bf16(prk_p), as_bf16(ah), wh)
    return g, h, rh, yh, saved


def wkv_chunks_grad(chunks, saved, m0, dy, dm1, masks, between_stages=lambda: None):
    ms, keep, eye, bd, eye2 = masks
    n = len(chunks)
    q = CHUNK
    at, bt, kt, rt, v, cl = (list(t) for t in zip(*chunks))
    tinv_p, aak_p, prb_p, prk_p, ah, wh = (list(t) for t in zip(*saved))
    head_stack = lambda p: rows(p[:, :q], p[:, q:])
    bc = [b * c_ for b, c_ in zip(bt, cl)]
    kc = [k * c_ for k, c_ in zip(kt, cl)]
    u = [mm(a, m) + w for a, m, w in zip(ah, m0, wh)]
    between_stages()
    dm1 = [d * bd for d in dm1]
    from_state = [mm(rows(b, k), d) for b, k, d in zip(bc, kc, dm1)]
    between_stages()
    dy_rows = [head_rows(d, ms) for d in dy]
    from_out = [mm_tn(lanes(head_stack(pb), head_stack(pk)), d) for pb, pk, d in zip(prb_p, prk_p, dy_rows)]
    between_stages()
    du = [a[:q] + b[:q] for a, b in zip(from_state, from_out)]
    dv = [a[q:] + b[q:] for a, b in zip(from_state, from_out)]
    dz = [mm_tn(head_stack(t), head_rows(d, ms)) for t, d in zip(tinv_p, du)]
    between_stages()
    dz_rows = [head_rows(d, ms) for d in dz]
    dv = [a + mm_tn(head_stack(k), d) for a, k, d in zip(dv, aak_p, dz_rows)]
    between_stages()
    by_m0 = [mm_nt(rows(d, z), m) for d, z, m in zip(dy, dz, m0)]
    between_stages()
    uv = [rows(x, y) for x, y in zip(u, v)]
    by_dm1 = [mm_nt(x, d) for x, d in zip(uv, dm1)]
    between_stages()
    udm = [x[:q] for x in by_dm1]
    vdm = [x[q:] for x in by_dm1]
    dscores = [jnp.where(keep, mm_nt(rows(z, d), x), 0.0) for z, d, x in zip(dz_rows, dy_rows, uv)]
    between_stages()
    to_ar = [mm(d, rows(b, k)) for d, b, k in zip(dscores, bt, kt)]
    to_bk = [mm_tn(d, wkv_score_stack(a, r, ms)) for d, a, r in zip(dscores, at, rt)]
    ones = jnp.ones((8, PAIR), F32)
    upper = (lane_iota((CHUNK, CHUNK)) >= row_iota((CHUNK, CHUNK))).astype(F32)
    out = []
    for c in range(n):
        e = to_ar[c]
        dat_c = by_m0[c][q:] + e[:q] * ms[0] + e[q:2 * q] * ms[1]
        drt_c = by_m0[c][:q] + e[2 * q:3 * q] * ms[0] + e[3 * q:] * ms[1]
        dbt_c = udm[c] * cl[c] + to_bk[c][:q]
        dkt_c = vdm[c] * cl[c] + to_bk[c][q:]
        dlcl = ones_dot_nt(ones, dm1[c] * m0[c], 3)[0:1, :] * cl[c] + colsum(bc[c] * udm[c] + kc[c] * vdm[c])
        g = drt_c * rt[c] - dbt_c * bt[c] - dkt_c * kt[c] + dat_c * at[c]
        dlw = ones_dot(upper, g, 3) - dat_c * at[c] + dlcl
        out.append((dat_c, dbt_c, dkt_c, drt_c, dv[c], dlw))
    return out


def wkv_forward(at, bt, kt, rt, v, clf):
    n_rows = at.shape[0]
    cps = WKV_CHUNKS_PER_STEP
    rb = cps * CHUNK
    n_steps = n_rows // rb

    def body(a_ref, b_ref, k_ref, r_ref, v_ref, c_ref, y_ref, m0_ref, g_ref, rh_ref, *rest):
        saved_refs, m_scr = rest[:6], rest[6]

        @pl.when(pl.program_id(1) == 0)
        def _():
            m_scr[...] = jnp.zeros_like(m_scr)

        masks = wkv_masks()
        chunks = []
        for cc in range(cps):
            sl = slice(cc * CHUNK, (cc + 1) * CHUNK)
            chunks.append((a_ref[sl, :], b_ref[sl, :], k_ref[sl, :], r_ref[sl, :], v_ref[sl, :],
                           c_ref[cc * CHUNK:cc * CHUNK + 1, :]))
        state = [m_scr[...]]
        pending = []

        def chain_step():
            if not pending:
                return
            cc, g, h, rh, yh = pending.pop(0)
            sl = slice(cc * CHUNK, (cc + 1) * CHUNK)
            m = state[0]
            m0_ref[0, cc] = m
            g_ref[0, cc] = g
            rh_ref[sl, :] = rh
            y_ref[sl, :] = hdot(rh, m) + yh
            state[0] = hdot(g, m) + h

        def prepare(first, last, between_stages):
            gs, hs, rhs, yhs, saved = wkv_chunks_pre(chunks[first:last], masks, between_stages)
            for ref, per_chunk in zip(saved_refs, saved):
                for cc, val in enumerate(per_chunk, start=first):
                    ref[cc * CHUNK:(cc + 1) * CHUNK, :] = val
            pending.extend(zip(range(first, last), gs, hs, rhs, yhs))

        group = cps // WKV_CHAIN_GROUPS
        for first in range(0, cps, group):
            prepare(first, first + group, chain_step)
        while pending:
            chain_step()
        m_scr[...] = state[0]

    blk = pl.BlockSpec((rb, PAIR), lambda p, s: (s, p))
    state_blk = pl.BlockSpec((1, cps, PAIR, PAIR), lambda p, s: (p, s, 0, 0))
    state_shape = jax.ShapeDtypeStruct((WIDTH // PAIR, n_rows // CHUNK, PAIR, PAIR), F32)
    rows_f32 = jax.ShapeDtypeStruct((n_rows, WIDTH), F32)
    rows_bf16 = jax.ShapeDtypeStruct((n_rows, WIDTH), BF16)
    return pl.pallas_call(
        body, name="wkv_forward", grid=(WIDTH // PAIR, n_steps),
        in_specs=[blk] * 6,
        out_specs=[blk, state_blk, state_blk, blk] + [blk] * 6,
        out_shape=[rows_f32, state_shape, state_shape, rows_f32] + [rows_bf16] * 5 + [rows_f32],
        scratch_shapes=[pltpu.VMEM((PAIR, PAIR), F32)],
        compiler_params=pltpu.CompilerParams(dimension_semantics=("arbitrary", "arbitrary"),
                                             vmem_limit_bytes=VMEM_LIMIT),
    )(at, bt, kt, rt, v, clf)


def wkv_backward(at, bt, kt, rt, v, clf, m0s, gs, rh, saved, dy):
    n_rows = at.shape[0]
    cps = WKV_CHUNKS_PER_STEP
    rb = cps * CHUNK
    n_steps = n_rows // rb

    def body(a_ref, b_ref, k_ref, r_ref, v_ref, c_ref, m0_ref, g_ref, rh_ref, *rest):
        saved_refs, dy_ref = rest[:6], rest[6]
        da_ref, db_ref, dk_ref, dr_ref, dv_ref, dlw_ref, dm_scr = rest[7:]

        @pl.when(pl.program_id(1) == 0)
        def _():
            dm_scr[...] = jnp.zeros_like(dm_scr)

        masks = wkv_masks()
        bd = masks[3]
        state = [dm_scr[...]]
        dm1 = [None] * cps
        todo = list(reversed(range(cps)))

        def chain_step():
            if not todo:
                return
            cc = todo.pop(0)
            sl = slice(cc * CHUNK, (cc + 1) * CHUNK)
            dm1[cc] = state[0]
            state[0] = bd * (hdot_tn(g_ref[0, cc], state[0]) + hdot_tn(rh_ref[sl, :], dy_ref[sl, :]))

        def gradients(first, last, between_stages):
            chunks, kept, m0, dys = [], [], [], []
            for cc in range(first, last):
                sl = slice(cc * CHUNK, (cc + 1) * CHUNK)
                chunks.append((a_ref[sl, :], b_ref[sl, :], k_ref[sl, :], r_ref[sl, :], v_ref[sl, :],
                               c_ref[cc * CHUNK:cc * CHUNK + 1, :]))
                kept.append(tuple(ref[sl, :] for ref in saved_refs))
                m0.append(m0_ref[0, cc])
                dys.append(dy_ref[sl, :])
            grads = wkv_chunks_grad(chunks, kept, m0, dys, dm1[first:last], masks, between_stages)
            for cc, (dat, dbt, dkt, drt, dv, dlw) in enumerate(grads, start=first):
                sl = slice(cc * CHUNK, (cc + 1) * CHUNK)
                da_ref[sl, :] = dat
                db_ref[sl, :] = dbt
                dk_ref[sl, :] = dkt
                dr_ref[sl, :] = drt
                dv_ref[sl, :] = dv
                dlw_ref[sl, :] = dlw

        group = cps // WKV_CHAIN_GROUPS
        for first in reversed(range(0, cps, group)):
            while todo and todo[0] >= first:
                chain_step()
            gradients(first, first + group, chain_step)
        dm_scr[...] = state[0]

    blk = pl.BlockSpec((rb, PAIR), lambda p, s: (n_steps - 1 - s, p))
    state_blk = pl.BlockSpec((1, cps, PAIR, PAIR), lambda p, s: (p, n_steps - 1 - s, 0, 0))
    return pl.pallas_call(
        body, name="wkv_backward", grid=(WIDTH // PAIR, n_steps),
        in_specs=[blk] * 6 + [state_blk, state_blk, blk] + [blk] * 6 + [blk],
        out_specs=[blk] * 6,
        out_shape=[jax.ShapeDtypeStruct((n_rows, WIDTH), F32)] * 6,
        scratch_shapes=[pltpu.VMEM((PAIR, PAIR), F32)],
        compiler_params=pltpu.CompilerParams(dimension_semantics=("arbitrary", "arbitrary"),
                                             vmem_limit_bytes=VMEM_LIMIT),
    )(at, bt, kt, rt, v, clf, m0s, gs, rh, *saved, dy)


def visible(q_row0, k_row0, shape):
    qc = (q_row0 + row_iota(shape)) // CHUNK
    kc = (k_row0 + lane_iota(shape)) // CHUNK
    return kc <= qc


def attention_forward(q, k, v):
    n_rows = q.shape[0]
    tq, tk = ATTN_FWD_TILES
    n_q = n_rows // tq
    assert tk % tq == 0

    def body(q_ref, k_ref, v_ref, o_ref, lse_ref):
        i = pl.program_id(1)
        lane = lane_iota((tq, LANE))
        heads = [slice(0, LANE), slice(LANE, 2 * LANE)]
        qs = [q_ref[:, cols] for cols in heads]

        def step(j, carry, size, masked):
            rows = pl.ds(pl.multiple_of(j * size, size), size)
            ss = [mm_nt(qh, k_ref[rows, cols]) for qh, cols in zip(qs, heads)]
            if masked:
                vis = visible(i * tq, j * size, ss[0].shape)
                ss = [jnp.where(vis, s, -jnp.inf) for s in ss]
            ps, stats = [], []
            for s, (m, l, _) in zip(ss, carry):
                m_new = jnp.maximum(m, jnp.max(s, axis=-1, keepdims=True))
                p = jnp.exp2(s - m_new)
                alpha = jnp.exp2(m - m_new)
                ps.append(p)
                stats.append((m_new, alpha, alpha * l + jnp.sum(p, axis=-1, keepdims=True)))
            pvs = [mm(p, v_ref[rows, cols]) for p, cols in zip(ps, heads)]
            return tuple((m_new, l, alpha * acc + pv)
                         for (m_new, alpha, l), (_, _, acc), pv in zip(stats, carry, pvs))

        carry = tuple((jnp.full((tq, 1), -jnp.inf, F32), jnp.zeros((tq, 1), F32), jnp.zeros((tq, LANE), F32))
                      for _ in heads)
        n_full = (i * tq) // tk
        carry = lax.fori_loop(0, n_full, functools.partial(step, size=tk, masked=False), carry)
        (m0, l0, acc0), (m1, l1, acc1) = step(n_full, carry, size=tk, masked=True)
        o_ref[...] = acc0 / l0 + acc1 / l1
        lse_ref[...] = jnp.where(lane >= HEAD, m1 + jnp.log2(l1), m0 + jnp.log2(l0))

    return pl.pallas_call(
        body, name="attention_forward", grid=(HEADS // 2, n_q),
        in_specs=[pl.BlockSpec((tq, 2 * LANE), lambda p, i: (i, p)),
                  pl.BlockSpec((n_rows, 2 * LANE), lambda p, i: (0, p)),
                  pl.BlockSpec((n_rows, 2 * LANE), lambda p, i: (0, p))],
        out_specs=[pl.BlockSpec((tq, LANE), lambda p, i: (i, p))] * 2,
        out_shape=[jax.ShapeDtypeStruct((n_rows, WIDTH), F32)] * 2,
        compiler_params=pltpu.CompilerParams(dimension_semantics=("arbitrary", "arbitrary"),
                                             vmem_limit_bytes=VMEM_LIMIT),
    )(q, k, v)


def block_exchange(g_refs, rg_refs, send_sems, recv_sems, local_sems):
    n = len(g_refs)
    me = my_position()
    mi = flat_index(me)

    def copies(k, src_index, dst_index):
        return [pltpu.make_async_remote_copy(
            src_ref=g_refs[a].at[src_index], dst_ref=rg_refs[a].at[dst_index],
            send_sem=send_sems.at[7 * a + k - 1], recv_sem=recv_sems.at[7 * a + k - 1],
            device_id=flip(me, k), device_id_type=MESH_IDS) for a in range(n)]

    local = [pltpu.make_async_copy(g_refs[a].at[mi], rg_refs[a].at[mi], local_sems.at[a]) for a in range(n)]

    def start():
        for cp in local:
            cp.start()
        for k in range(1, N_DEV):
            for cp in copies(k, flat_index(flip(me, k)), mi):
                cp.start()

    def wait():
        for k in range(1, N_DEV):
            pi = flat_index(flip(me, k))
            for cp in copies(k, pi, pi):
                cp.wait_recv()
        for k in range(1, N_DEV):
            for cp in copies(k, flat_index(flip(me, k)), mi):
                cp.wait_send()
        for cp in local:
            cp.wait()

    return start, wait


def attention_backward(q, k, v, o, do, lse, riders):
    n_rows = q.shape[0]
    tq, tk = ATTN_BWD_TILES
    n_q = n_rows // tq
    n_k = n_rows // tk
    n_masked = max(1, tk // tq)
    n_r = len(riders)

    def body(q_ref, k_ref, v_ref, o_ref, do_ref, lse_ref, *rest):
        g_refs = rest[:n_r]
        dq_ref, dk_ref, dv_ref = rest[n_r:n_r + 3]
        rg_refs = rest[n_r + 3:2 * n_r + 3]
        start_riders, wait_riders = block_exchange(g_refs, rg_refs, *rest[2 * n_r + 3:])
        j = pl.program_id(1)

        @pl.when(jnp.logical_and(pl.program_id(0) == 0, j == 0))
        def _():
            start_riders()

        @pl.when(j == 0)
        def _():
            dq_ref[...] = jnp.zeros_like(dq_ref)

        lane = lane_iota((tq, LANE))
        heads = [slice(0, LANE), slice(LANE, 2 * LANE)]
        ks = [k_ref[:, cols] for cols in heads]
        vs = [v_ref[:, cols] for cols in heads]
        head_lanes = [(lane < HEAD).astype(F32), (lane >= HEAD).astype(F32)]

        def step(i, carry, masked):
            rows = pl.ds(pl.multiple_of(i * tq, tq), tq)
            qs = [q_ref[rows, cols] for cols in heads]
            dout = do_ref[rows, :]
            dout_o = dout * o_ref[rows, :]
            lse_t = lse_ref[rows, :]
            ss = [mm_nt(qh, kh) for qh, kh in zip(qs, ks)]
            dps = [mm_nt(dout, vh) for vh in vs]
            ps, dss = [], []
            for hh in range(2):
                delta = jnp.sum(dout_o * head_lanes[hh], axis=-1, keepdims=True)
                lse_h = jnp.sum(jnp.where(lane == hh * HEAD, lse_t, 0.0), axis=-1, keepdims=True)
                p = jnp.exp2(ss[hh] - lse_h)
                if masked:
                    p = jnp.where(visible(i * tq, j * tk, p.shape), p, 0.0)
                ps.append(p)
                dss.append(p * (dps[hh] - delta))
            dvs = [mm_tn(p, dout) for p in ps]
            dqs = [mm(ds, kh) for ds, kh in zip(dss, ks)]
            dks = [mm_tn(ds, qh) for ds, qh in zip(dss, qs)]
            for cols, dq in zip(heads, dqs):
                dq_ref[rows, cols] += dq * ATTN_SCALE
            return tuple((dk + a, dv + b) for (dk, dv), a, b in zip(carry, dks, dvs))

        carry = tuple((jnp.zeros((tk, LANE), F32), jnp.zeros((tk, LANE), F32)) for _ in heads)
        i_first = (j * tk) // tq
        for extra in range(n_masked):
            carry = step(i_first + extra, carry, masked=True)
        carry = lax.fori_loop(i_first + n_masked, n_q, functools.partial(step, masked=False), carry)
        for cols, (dk, dv) in zip(heads, carry):
            dk_ref[:, cols] = dk * (1.0 / LOG2_E)
            dv_ref[:, cols] = dv

        @pl.when(jnp.logical_and(pl.program_id(0) == HEADS // 2 - 1, j == n_k - 1))
        def _():
            wait_riders()

    full = lambda w: pl.BlockSpec((n_rows, w), lambda p, j: (0, p))
    blk = pl.BlockSpec((tk, 2 * LANE), lambda p, j: (j, p))
    hbm = pl.BlockSpec(memory_space=pl.ANY)
    return pl.pallas_call(
        body, name="attention_backward", grid=(HEADS // 2, n_k),
        in_specs=[full(2 * LANE), blk, blk, full(LANE), full(LANE), full(LANE)] + [hbm] * n_r,
        out_specs=[full(2 * LANE), blk, blk] + [hbm] * n_r,
        out_shape=[jax.ShapeDtypeStruct((n_rows, HEADS * LANE), F32)] * 3
        + [jax.ShapeDtypeStruct(r.shape, r.dtype) for r in riders],
        scratch_shapes=[pltpu.SemaphoreType.DMA((7 * n_r,)), pltpu.SemaphoreType.DMA((7 * n_r,)),
                        pltpu.SemaphoreType.DMA((n_r,))],
        compiler_params=pltpu.CompilerParams(dimension_semantics=("arbitrary", "arbitrary"),
                                             vmem_limit_bytes=VMEM_LIMIT),
    )(q, k, v, o, do, lse, *riders)


def tail_tile(step0, tile0, x, tgt, ma, mb, gpa, gpb, ya, y, ur, k2, uv,
              mod, wpa, wpb, wout, gn_g, gn_b, r_k, post_g, post_b, bd):
    gate = mod[2:3]
    inv = 1.0 / HEAD
    yc = y - head_sum(y, bd) * inv
    rs = lax.rsqrt(head_sum(yc * yc, bd) * inv + GN_EPS)
    yn = yc * rs
    yb = yn * gn_g + gn_b + head_sum(ur * k2 * r_k, bd) * uv
    sga, sgb = sigmoid(gpa), sigmoid(gpb)
    sila, silb = gpa * sga, gpb * sgb
    ga, gb = ya * sila, yb * silb
    pa, pb = mm(ga, wpa), mm(gb, wpb)
    sa, sb = sigmoid(ma), sigmoid(mb)
    merged = sa * pa + sb * pb
    sub = mm(merged, wout)
    z = ALPHA * x + (1.0 + gate) * sub
    zhat, rstd = layer_norm_stats(z)
    err = zhat * post_g + post_b - tgt
    loss = 0.5 * jnp.sum(rowmean(err * err), axis=0, keepdims=True) + jnp.zeros((1, LANE), F32)
    dout = err * (1.0 / D_MODEL)
    dpost_g = colsum(dout * zhat)
    dpost_b = colsum(dout)
    dz = layer_norm_bwd(dout * post_g, zhat, rstd)
    dgate = colsum(dz * sub)
    dsub = dz * (1.0 + gate)
    dwout = mm_tn(merged, dsub)
    dmerged = mm_nt(dsub, wout)
    dpa, dpb = dmerged * sa, dmerged * sb
    dma = dmerged * pa * sa * (1.0 - sa)
    dmb = dmerged * pb * sb * (1.0 - sb)
    dwpa = mm_tn(ga, dpa)
    dwpb = mm_tn(gb, dpb)
    dga = mm_nt(dpa, wpa)
    dgb = mm_nt(dpb, wpb)
    dya = dga * sila
    dgpa = dga * ya * (sga * (1.0 + gpa * (1.0 - sga)))
    dyb = dgb * silb
    dgpb = dgb * yb * (sgb * (1.0 + gpb * (1.0 - sgb)))
    dgn_g = colsum(dyb * yn)
    dgn_b = colsum(dyb)
    dyn = dyb * gn_g
    dy = rs * (dyn - head_sum(dyn, bd) * inv - yn * head_sum(dyn * yn, bd) * inv)
    return (dz, dma, dmb, dgpa, dgpb, dya, dy, dyb,
            loss, dwout, dwpa, dwpb, dgn_g, dgn_b, dpost_g, dpost_b, dgate)


def mla_prep_bwd_tile(step0, tile0, q_c, kv_c, cos, sin, dq, dk, dv, gq, gkv, wq, wqr, wkn, wv):
    qn, qh, rq = rms_norm_fwd(q_c, gq)
    kvn, kvh, rkv = rms_norm_fwd(kv_c, gkv)
    dqc = dq * tile_lanes(cos, HEADS)
    dqs = dq * tile_lanes(sin, HEADS)
    dqn = mm_nt(dqc, wq) + mm_nt(dqs, wqr)
    dkvn = mm_nt(dk, wkn) + mm_nt(dv, wv)
    dkpe = dk[:, 0:LANE]
    for h in range(1, HEADS):
        dkpe = dkpe + dk[:, h * LANE:(h + 1) * LANE]
    dkr = dkpe * (cos * key_rope_mask(cos.shape))
    dkrr = dkpe * sin

    def rms_bwd(dyv, xh, r, g):
        dyg = dyv * g
        return r * (dyg - xh * rowmean(dyg * xh)), colsum(dyv * xh)

    dq_c, dgq = rms_bwd(dqn, qh, rq, gq)
    dkv_c, dgkv = rms_bwd(dkvn, kvh, rkv, gkv)
    return (dq_c, dkv_c, dkr, dkrr,
            mm_tn(qn, dqc), mm_tn(qn, dqs), mm_tn(kvn, dk), mm_tn(kvn, dv), dgq, dgkv)


def rwkv_prep_bwd_tile(step0, tile0, r0, k0, v0, l0, drt, dat, dbt, dkt, dvv, dlw, dyb, hr, hk, hv, hl,
                       mu_r, mu_k, mu_v, mu_l, w0, a0, k_k, k_a, w_dec, w_iclr, tril, same, bd, r_k,
                       cr, ck, cv, cl_):
    f = rwkv_prep_core(tile0, r0, k0, v0, l0, hr, hk, hv, hl, mu_r, mu_k, mu_v, mu_l, w0, a0, k_k, k_a,
                       w_dec, w_iclr, tril, same, bd)
    ur, uk, uv, ul, kk, k2, a_ic, sg, th = (f[n] for n in ("ur", "uk", "uv", "ul", "kk", "k2", "a_ic", "sg", "th"))
    lc, lw = f["lc"], f["lw"]
    e_neg = jnp.exp(-lc)
    dur = drt * jnp.exp(lc)
    da = dat * jnp.exp(lc - lw)
    db = dbt * e_neg
    dk2 = dkt * e_neg
    s = head_sum(ur * k2 * r_k, bd)
    duv = dvv + dyb * s
    ds = head_sum(dyb * uv, bd)
    dur = dur + ds * k2 * r_k
    dk2 = dk2 + ds * ur * r_k
    dr_k = colsum(ds * ur * k2)
    dkk = db * a_ic - da
    da_ic = db * kk + dk2 * uk * k_a
    duk = dk2 * (1.0 + (a_ic - 1.0) * k_a)
    dk_a = colsum(dk2 * uk * (a_ic - 1.0))
    dkkraw = jnp.where(f["nrm_raw"] > 1e-12, (dkk - kk * head_sum(dkk * kk, bd)) / f["nrm"], dkk * 1e12)
    duk = duk + dkkraw * k_k
    dk_k = colsum(dkkraw * uk)
    dai = da_ic * a_ic * (1.0 - a_ic)
    dd = dlw * (-DECAY_SCALE) * sg * (1.0 - sg)
    dul = mm_nt(dai, w_iclr) + mm_nt(dd, w_dec) * (1.0 - th * th)

    def unshift(du, x, prev, mu, carry_row):
        nxt = shift_rows_up(du, carry_row)
        return du * (1.0 - mu) + nxt * mu, colsum(du * (prev - x)), du[0:1, :]

    dr0, dmu_r, ncr = unshift(dur, r0, f["pr"], mu_r, cr)
    dk0, dmu_k, nck = unshift(duk, k0, f["pk"], mu_k, ck)
    dv0, dmu_v, ncv = unshift(duv, v0, f["pv"], mu_v, cv)
    dl0, dmu_l, ncl = unshift(dul, l0, f["pl"], mu_l, cl_)
    return (dr0, dk0, dv0, dl0,
            dmu_r, dmu_k, dmu_v, dmu_l, colsum(dd), colsum(dai), dk_k, dk_a, dr_k, mm_tn(th, dd), mm_tn(ul, dai),
            ncr, nck, ncv, ncl)


def in_backward(x, dz, pieces, mod, w_in_pt, unrot):
    n_rows = x.shape[0]
    ts = ROW_TILE
    n_p = len(pieces)
    shard_cols = IN_WIDTH // N_DEV

    def body(*refs):
        x_ref, dz_ref = refs[:2]
        p_refs = refs[2:2 + n_p]
        mod_ref, w_ref, unrot_ref = refs[2 + n_p:5 + n_p]
        dx_ref, ht_ref, blocks_ref, dshift_ref, dscale_ref = refs[5 + n_p:]
        step0 = pl.program_id(0) == 0
        dma, dmb, dr0, dk0, dv0, dgpa, dgpb, dq_c, dkv_c, dkr, dkrr, dl0 = (r[...] for r in p_refs)
        dproj = jnp.concatenate([dma, dmb, dr0, dk0, dv0, dgpa, dgpb, dq_c, dkv_c, dkr, dkrr, dl0], axis=1)
        dh = mm(dproj, w_ref[...])
        xhat, rstd = layer_norm_stats(x_ref[...])
        scale1 = 1.0 + mod_ref[1:2, :]
        dx_ref[...] = layer_norm_bwd(dh * scale1, xhat, rstd) + ALPHA * dz_ref[...]
        ht_ref[...] = jnp.transpose(xhat * scale1 + mod_ref[0:1, :]).astype(BF16)
        dkrope = (dkr.astype(F32) + mm(dkrr, unrot_ref[...]))[:, NOPE:QK_DIM]
        natural = jnp.concatenate(
            [dq_c.astype(F32), dkv_c.astype(F32), dkrope]
            + [p.astype(F32) for p in (dgpa, dr0, dk0, dv0, dl0, dgpb, dma, dmb)], axis=1)
        for j in range(N_DEV):
            blocks_ref[j] = natural[:, j * shard_cols:(j + 1) * shard_cols].astype(BF16)
        for ref, val in ((dshift_ref, colsum(dh)), (dscale_ref, colsum(dh * xhat))):
            @pl.when(step0)
            def _(ref=ref, val=val):
                ref[...] = val

            @pl.when(jnp.logical_not(step0))
            def _(ref=ref, val=val):
                ref[...] += val

    row = lambda w: pl.BlockSpec((ts, w), lambda i: (i, 0))
    const = pl.BlockSpec(memory_space=pltpu.VMEM)
    vec = pl.BlockSpec((1, D_MODEL), lambda i: (0, 0))
    return pl.pallas_call(
        body, name="in_backward", grid=(n_rows // ts,),
        in_specs=[row(D_MODEL), row(D_MODEL)] + [row(p.shape[1]) for p in pieces] + [const] * 3,
        out_specs=[row(D_MODEL), pl.BlockSpec((D_MODEL, ts), lambda i: (0, i)),
                   pl.BlockSpec((N_DEV, ts, shard_cols), lambda i: (0, i, 0)), vec, vec],
        out_shape=[jax.ShapeDtypeStruct((n_rows, D_MODEL), F32), jax.ShapeDtypeStruct((D_MODEL, n_rows), BF16),
                   jax.ShapeDtypeStruct((N_DEV, n_rows, shard_cols), BF16),
                   jax.ShapeDtypeStruct((1, D_MODEL), F32), jax.ShapeDtypeStruct((1, D_MODEL), F32)],
        compiler_params=pltpu.CompilerParams(dimension_semantics=("arbitrary",), vmem_limit_bytes=VMEM_LIMIT),
    )(x, dz, *pieces, mod, w_in_pt, unrot)


def in_weight_grad_exchange(h_t, dp_blocks, others, small, order):
    n = len(others)
    n_rows = h_t.shape[1]
    ts = 4 * ROW_TILE
    n_i = n_rows // ts
    shard_cols = dp_blocks.shape[2]
    n_chips = N_DEV // 2
    last = N_DEV - 1

    def body(order_ref, h_ref, dp_ref, *rest):
        g_refs, s_ref = rest[:n], rest[n]
        rwin_ref, rg_refs, rs_ref = rest[n + 1], rest[n + 2:2 * n + 2], rest[2 * n + 2]
        (acc, sendbuf, sib_buf, relay_buf, sib_send, sib_recv, win_send, win_recv, relay_sems,
         o_send, o_recv, local_sems) = rest[2 * n + 3:]
        b, i = pl.program_id(0), pl.program_id(1)
        me = my_position()
        mi = flat_index(me)
        sibling = (me[0], me[1], 1 - me[2])

        def other_copies(k, src_index, dst_index):
            peer = flip(me, k)
            out = [pltpu.make_async_remote_copy(
                src_ref=g_refs[a].at[src_index], dst_ref=rg_refs[a].at[dst_index],
                send_sem=o_send.at[(n + 1) * (k - 1) + a], recv_sem=o_recv.at[(n + 1) * (k - 1) + a],
                device_id=peer, device_id_type=MESH_IDS) for a in range(n)]
            out.append(pltpu.make_async_remote_copy(
                src_ref=s_ref, dst_ref=rs_ref.at[dst_index],
                send_sem=o_send.at[(n + 1) * (k - 1) + n], recv_sem=o_recv.at[(n + 1) * (k - 1) + n],
                device_id=peer, device_id_type=MESH_IDS))
            return out

        def local_copies():
            out = [pltpu.make_async_copy(g_refs[a].at[mi], rg_refs[a].at[mi], local_sems.at[a]) for a in range(n)]
            out.append(pltpu.make_async_copy(s_ref, rs_ref.at[mi], local_sems.at[n]))
            return out

        def to_sibling(t):
            return pltpu.make_async_remote_copy(
                src_ref=sendbuf.at[t], dst_ref=sib_buf.at[t], send_sem=sib_send.at[t], recv_sem=sib_recv.at[t],
                device_id=sibling, device_id_type=MESH_IDS)

        def to_owner(t):
            flip_x = (t < 2) * 1
            flip_y = 1 - (t & 1)
            owner = (me[0] ^ flip_x, me[1] ^ flip_y, me[2])
            return pltpu.make_async_remote_copy(
                src_ref=sendbuf.at[n_chips + t], dst_ref=rwin_ref.at[t], send_sem=win_send.at[t],
                recv_sem=win_recv.at[t], device_id=owner, device_id_type=MESH_IDS)

        to_relay = pltpu.make_async_remote_copy(
            src_ref=sendbuf.at[n_chips], dst_ref=relay_buf, send_sem=relay_sems.at[0], recv_sem=relay_sems.at[1],
            device_id=(me[0] ^ (1 - me[2]), me[1] ^ me[2], me[2]), device_id_type=MESH_IDS)
        relay_on = pltpu.make_async_remote_copy(
            src_ref=relay_buf, dst_ref=rwin_ref.at[0], send_sem=win_send.at[0], recv_sem=win_recv.at[0],
            device_id=(me[0] ^ me[2], me[1] ^ (1 - me[2]), me[2]), device_id_type=MESH_IDS)

        own_block = pltpu.make_async_copy(sendbuf.at[last], rwin_ref.at[n_chips - 1], local_sems.at[n + 1])

        @pl.when(jnp.logical_and(b == 0, i == 0))
        def _():
            for cp in local_copies():
                cp.start()
            for k in range(1, N_DEV):
                for cp in other_copies(k, flat_index(flip(me, k)), mi):
                    cp.start()

        contrib = jnp.dot(h_ref[...], dp_ref[...], preferred_element_type=F32)

        @pl.when(i == 0)
        def _():
            acc[...] = contrib

        @pl.when(i > 0)
        def _():
            acc[...] += contrib

        slot = order_ref[N_DEV + b]
        t = slot & (n_chips - 1)

        @pl.when(jnp.logical_and(i == n_i - 1, slot < n_chips))
        def _():
            sendbuf[slot] = acc[...].astype(BF16)
            to_sibling(t).start()

        @pl.when(jnp.logical_and(i == n_i - 1, slot >= n_chips))
        def _():
            to_sibling(t).wait_recv()
            sendbuf[slot] = (acc[...] + sib_buf[t].astype(F32)).astype(BF16)

            @pl.when(slot == n_chips)
            def _():
                to_relay.start()

            @pl.when(jnp.logical_and(slot > n_chips, slot < last))
            def _():
                to_owner(t).start()

            @pl.when(slot == n_chips + RELAY_AFTER)
            def _():
                to_relay.wait_recv()
                relay_on.start()

            @pl.when(slot == last)
            def _():
                own_block.start()

        @pl.when(jnp.logical_and(b == last, i == n_i - 1))
        def _():
            for t in range(n_chips - 1):
                to_owner(t).wait_recv()
            to_relay.wait_send()
            relay_on.wait_send()
            for k in range(1, N_DEV):
                pi = flat_index(flip(me, k))
                for cp in other_copies(k, pi, pi):
                    cp.wait_recv()
            for t in range(n_chips):
                to_sibling(t).wait_send()
            for t in range(1, n_chips - 1):
                to_owner(t).wait_send()
            for k in range(1, N_DEV):
                for cp in other_copies(k, flat_index(flip(me, k)), mi):
                    cp.wait_send()
            for cp in local_copies():
                cp.wait()
            own_block.wait()

    hbm = pl.BlockSpec(memory_space=pl.ANY)
    n_sem = 7 * (n + 1)
    grid_spec = pltpu.PrefetchScalarGridSpec(
        num_scalar_prefetch=1, grid=(N_DEV, n_i),
        in_specs=[pl.BlockSpec((D_MODEL, ts), lambda b, i, order: (0, i)),
                  pl.BlockSpec((None, ts, shard_cols), lambda b, i, order: (order[b], i, 0))] + [hbm] * (n + 1),
        out_specs=[hbm] * (n + 2),
        scratch_shapes=[pltpu.VMEM((D_MODEL, shard_cols), F32), pltpu.VMEM((N_DEV, D_MODEL, shard_cols), BF16),
                        pltpu.VMEM((n_chips, D_MODEL, shard_cols), BF16), pltpu.VMEM((D_MODEL, shard_cols), BF16),
                        pltpu.SemaphoreType.DMA((n_chips,)), pltpu.SemaphoreType.DMA((n_chips,)),
                        pltpu.SemaphoreType.DMA((n_chips - 1,)), pltpu.SemaphoreType.DMA((n_chips - 1,)),
                        pltpu.SemaphoreType.DMA((2,)),
                        pltpu.SemaphoreType.DMA((n_sem,)), pltpu.SemaphoreType.DMA((n_sem,)),
                        pltpu.SemaphoreType.DMA((n + 2,))])
    return pl.pallas_call(
        body, name="in_weight_grad_exchange", grid_spec=grid_spec,
        out_shape=[jax.ShapeDtypeStruct((n_chips, D_MODEL, shard_cols), BF16)]
        + [jax.ShapeDtypeStruct(o.shape, o.dtype) for o in others]
        + [jax.ShapeDtypeStruct((N_DEV,) + small.shape, small.dtype)],
        compiler_params=pltpu.CompilerParams(dimension_semantics=("arbitrary", "arbitrary"),
                                             vmem_limit_bytes=VMEM_LIMIT),
    )(order, h_t, dp_blocks, *others, small)


def ada_weight_grad(c_all, dmod_cols):
    def body(c_ref, d_ref, o_ref):
        cv = c_ref[...]
        o_ref[...] = hdot_tn(cv * sigmoid(cv), d_ref[...])

    return pl.pallas_call(
        body, name="ada_weight_grad",
        out_shape=jax.ShapeDtypeStruct((c_all.shape[1], dmod_cols.shape[1]), F32),
    )(c_all, dmod_cols)


def adamw_update(g, w, m, v):
    nm = ADAM_B1 * m + (1.0 - ADAM_B1) * g
    nv = ADAM_B2 * v + (1.0 - ADAM_B2) * (g * g)
    m_hat = nm / (1.0 - ADAM_B1 ** ADAM_STEP)
    v_hat = nv / (1.0 - ADAM_B2 ** ADAM_STEP)
    return -ADAM_LR * (m_hat / (jnp.sqrt(v_hat) + ADAM_EPS) + ADAM_WD * w), nm, nv


def adamw(parts, w, m, v, name):
    k, rows, cols = parts.shape

    def body(p_ref, w_hbm, m_hbm, v_hbm, g_ref, d_ref, nm_ref, nv_ref, w_buf, m_buf, v_buf, sems):
        loads = [pltpu.make_async_copy(src, dst, sems.at[i])
                 for i, (src, dst) in enumerate(((w_hbm, w_buf), (m_hbm, m_buf), (v_hbm, v_buf)))]
        for cp in loads:
            cp.start()
        g = p_ref[0].astype(F32)
        for i in range(1, k):
            g = g + p_ref[i].astype(F32)
        g_ref[0] = g
        for cp in loads:
            cp.wait()
        d_ref[0], nm_ref[0], nv_ref[0] = adamw_update(g, w_buf[0], m_buf[0], v_buf[0])

    hbm = pl.BlockSpec(memory_space=pl.ANY)
    whole = pl.BlockSpec(memory_space=pltpu.VMEM)
    return pl.pallas_call(
        body, name=name,
        in_specs=[whole, hbm, hbm, hbm], out_specs=[whole] * 4,
        out_shape=[jax.ShapeDtypeStruct((1, rows, cols), F32)] * 4,
        scratch_shapes=[pltpu.VMEM((1, rows, cols), F32)] * 3 + [pltpu.SemaphoreType.DMA((3,))],
        compiler_params=pltpu.CompilerParams(vmem_limit_bytes=VMEM_LIMIT),
    )(parts, w, m, v)


def adamw_small(parts, ws, ms, vs):
    k = parts.shape[0]
    n = len(ws)
    sizes = [w.shape[1] for w in ws]

    def body(p_ref, *refs):
        ins, outs = refs[:3 * n], refs[3 * n:]
        g_all = p_ref[0]
        for i in range(1, k):
            g_all = g_all + p_ref[i]
        off = 0
        for a, size in enumerate(sizes):
            g = g_all[:, off:off + size]
            off += size
            d, nm, nv = adamw_update(g, ins[a][...], ins[n + a][...], ins[2 * n + a][...])
            for kind, val in enumerate((g, d, nm, nv)):
                outs[kind * n + a][...] = val

    return pl.pallas_call(
        body, name="adamw_small",
        out_shape=[jax.ShapeDtypeStruct((1, size), F32) for _ in range(4) for size in sizes],
    )(parts, *ws, *ms, *vs)


def columns_from_shards(g, rows, cols):
    return g.reshape(N_DEV, rows, cols).transpose(1, 0, 2).reshape(rows, N_DEV * cols)


def permute_w_in_t(wt):
    z = lambda n: jnp.zeros((n, D_MODEL), wt.dtype)
    krope = wt[N_KROPE:N_KROPE + ROPE]
    krope_rot = jnp.concatenate([-krope[ROPE // 2:], krope[:ROPE // 2]], axis=0)
    rw = N_RWKV
    return jnp.concatenate([
        wt[N_MA:N_MA + 1024], wt[N_MB:N_MB + 1024],
        wt[rw:rw + 512], wt[rw + 512:rw + 1024], wt[rw + 1024:rw + 1536],
        wt[N_GPA:N_GPA + 512], wt[N_GPB:N_GPB + 512],
        wt[N_QC:N_QC + 256], wt[N_KVC:N_KVC + 128],
        z(NOPE), krope, z(LANE - QK_DIM), z(NOPE), krope_rot, z(LANE - QK_DIM),
        wt[rw + 1536:rw + 1664]], axis=0)


def pad_heads_q(w_uq):
    w = w_uq.reshape(Q_RANK, HEADS, QK_DIM)
    zpad = jnp.zeros((Q_RANK, HEADS, LANE - QK_DIM), w.dtype)
    wq = jnp.concatenate([w, zpad], axis=2).reshape(Q_RANK, HEADS * LANE)
    pe = w[:, :, NOPE:]
    rot = jnp.concatenate([-pe[:, :, ROPE // 2:], pe[:, :, :ROPE // 2]], axis=2)
    wqr = jnp.concatenate([jnp.zeros((Q_RANK, HEADS, NOPE), w.dtype), rot, zpad], axis=2).reshape(Q_RANK, HEADS * LANE)
    return wq, wqr


def unpad_heads_q_grad(dwq, dwqr):
    a = dwq.reshape(Q_RANK, HEADS, LANE)
    r = dwqr.reshape(Q_RANK, HEADS, LANE)[:, :, NOPE:QK_DIM]
    pe = a[:, :, NOPE:QK_DIM] + jnp.concatenate([r[:, :, ROPE // 2:], -r[:, :, :ROPE // 2]], axis=2)
    return jnp.concatenate([a[:, :, :NOPE], pe], axis=2).reshape(Q_RANK, HEADS * QK_DIM)


def pad_heads_kv(w_ukv):
    w = w_ukv.reshape(KV_RANK, HEADS, 2 * HEAD)
    z = jnp.zeros((KV_RANK, HEADS, HEAD), w.dtype)
    wkn = jnp.concatenate([w[:, :, :NOPE], z], axis=2).reshape(KV_RANK, HEADS * LANE)
    val = w[:, :, NOPE:]
    odd = (jnp.arange(HEADS) % 2 == 1)[None, :, None]
    wv = jnp.concatenate([jnp.where(odd, 0, val), jnp.where(odd, val, 0)], axis=2).reshape(KV_RANK, HEADS * LANE)
    return wkn, wv


def unpad_heads_kv_grad(dwkn, dwv):
    a = dwkn.reshape(KV_RANK, HEADS, LANE)[:, :, :NOPE]
    b = dwv.reshape(KV_RANK, HEADS, LANE)
    odd = (jnp.arange(HEADS) % 2 == 1)[None, :, None]
    val = jnp.where(odd, b[:, :, HEAD:], b[:, :, :HEAD])
    return jnp.concatenate([a, val], axis=2).reshape(KV_RANK, HEADS * 2 * HEAD)


def kernel(x, c, positions, w_ada, b_ada, w_in, q_norm_g, w_uq, kv_norm_g, w_ukv, mu_rwkv, w0, w_decay_up, a0, w_iclr_up, k_k, k_a, r_k, gn_g, gn_b, w_proj_a, w_proj_b, w_out, post_g, post_b, loss_target, m_w_ada, m_b_ada, m_w_in, m_q_norm_g, m_w_uq, m_kv_norm_g, m_w_ukv, m_mu_rwkv, m_w0, m_w_decay_up, m_a0, m_w_iclr_up, m_k_k, m_k_a, m_r_k, m_gn_g, m_gn_b, m_w_proj_a, m_w_proj_b, m_w_out, m_post_g, m_post_b, v_w_ada, v_b_ada, v_w_in, v_q_norm_g, v_w_uq, v_kv_norm_g, v_w_ukv, v_mu_rwkv, v_w0, v_w_decay_up, v_a0, v_w_iclr_up, v_k_k, v_k_a, v_r_k, v_gn_g, v_gn_b, v_w_proj_a, v_w_proj_b, v_w_out, v_post_g, v_post_b):
    weights = dict(w_ada=w_ada, b_ada=b_ada, w_in=w_in, q_norm_g=q_norm_g, w_uq=w_uq, kv_norm_g=kv_norm_g,
                   w_ukv=w_ukv, mu_rwkv=mu_rwkv, w0=w0, w_decay_up=w_decay_up, a0=a0, w_iclr_up=w_iclr_up,
                   k_k=k_k, k_a=k_a, r_k=r_k, gn_g=gn_g, gn_b=gn_b, w_proj_a=w_proj_a, w_proj_b=w_proj_b,
                   w_out=w_out, post_g=post_g, post_b=post_b)
    mom1 = dict(w_ada=m_w_ada, b_ada=m_b_ada, w_in=m_w_in, q_norm_g=m_q_norm_g, w_uq=m_w_uq, kv_norm_g=m_kv_norm_g,
                w_ukv=m_w_ukv, mu_rwkv=m_mu_rwkv, w0=m_w0, w_decay_up=m_w_decay_up, a0=m_a0, w_iclr_up=m_w_iclr_up,
                k_k=m_k_k, k_a=m_k_a, r_k=m_r_k, gn_g=m_gn_g, gn_b=m_gn_b, w_proj_a=m_w_proj_a, w_proj_b=m_w_proj_b,
                w_out=m_w_out, post_g=m_post_g, post_b=m_post_b)
    mom2 = dict(w_ada=v_w_ada, b_ada=v_b_ada, w_in=v_w_in, q_norm_g=v_q_norm_g, w_uq=v_w_uq, kv_norm_g=v_kv_norm_g,
                w_ukv=v_w_ukv, mu_rwkv=v_mu_rwkv, w0=v_w0, w_decay_up=v_w_decay_up, a0=v_a0, w_iclr_up=v_w_iclr_up,
                k_k=v_k_k, k_a=v_k_a, r_k=v_r_k, gn_g=v_gn_g, gn_b=v_gn_b, w_proj_a=v_w_proj_a, w_proj_b=v_w_proj_b,
                w_out=v_w_out, post_g=v_post_g, post_b=v_post_b)
    names = list(weights)
    n_rows = x.shape[1]
    me = 4 * lax.axis_index("x") + 2 * lax.axis_index("y") + lax.axis_index("c")
    xr = x[0]
    tgt = loss_target[0]
    row = lambda a: a.reshape(1, -1)

    w_in_all, c_all = gather_shards([w_in[0].T.astype(BF16), c])
    c_all = c_all.reshape(N_DEV, D_MODEL)
    w_in_pt = permute_w_in_t(w_in_all.reshape(IN_WIDTH, D_MODEL))

    mod_all = ada_modulation(c_all, w_ada[0], b_ada.reshape(N_DEV, -1))
    mod = lax.dynamic_index_in_dim(mod_all, me, axis=1, keepdims=False).reshape(3, D_MODEL)

    proj, *gathered = fwd_in_gather(xr, mod, w_in_pt, [weights[n][0].astype(BF16) for n, _, _ in SHARDED[1:]])
    pcol = lambda off_, w: (proj, w, off_ // w)
    full = {}
    for (n, r, cdim), part in zip(SHARDED[1:], gathered):
        full[n] = part.reshape(N_DEV * r, cdim) if n == "w_out" else columns_from_shards(part, r, cdim)
    wq, wqr = pad_heads_q(full["w_uq"])
    wkn, wv = pad_heads_kv(full["w_ukv"])
    zl = jnp.zeros((LORA, WIDTH), BF16)
    w_dec = jnp.concatenate([full["w_decay_up"], zl], axis=0)
    w_iclr = jnp.concatenate([zl, full["w_iclr_up"]], axis=0)
    wpa, wpb, wout = full["w_proj_a"], full["w_proj_b"], full["w_out"]

    inv_freq = ROPE_THETA ** (-jnp.arange(0, ROPE, 2, dtype=F32) / ROPE)
    ang = positions[0].astype(F32)[:, None] * inv_freq
    ones_n, zeros_n, zeros_p = jnp.ones((n_rows, NOPE), F32), jnp.zeros((n_rows, NOPE), F32), jnp.zeros((n_rows, LANE - QK_DIM), F32)
    cos_t = jnp.concatenate([ones_n, jnp.cos(ang), jnp.cos(ang), zeros_p], axis=1)
    sin_t = jnp.concatenate([zeros_n, jnp.sin(ang), jnp.sin(ang), zeros_p], axis=1)

    gq, gkv = q_norm_g, kv_norm_g
    mla_consts = [gq, gkv, wq, wqr, wkn, wv]
    q, k, v = row_call(
        "mla_prep", mla_prep_tile, n_rows,
        [pcol(P_QC, 256), pcol(P_KVC, 128), pcol(P_KR, 128), pcol(P_KRR, 128), (cos_t, LANE, 0), (sin_t, LANE, 0)],
        mla_consts, [(HEADS * LANE, BF16)] * 3, tile_rows=PREP_TILE)
    ya, lse = attention_forward(q, k, v)

    def chunk_sum_matrices(n):
        t_idx = jnp.arange(n)
        same_chunk = (t_idx[:, None] // CHUNK) == (t_idx[None, :] // CHUNK)
        return (same_chunk & (t_idx[:, None] >= t_idx[None, :])).astype(F32), same_chunk.astype(F32)

    l_idx = jnp.arange(LANE)
    bd = ((l_idx[:, None] // HEAD) == (l_idx[None, :] // HEAD)).astype(F32)
    mu = mu_rwkv
    mu_r, mu_k, mu_v, mu_l = mu[:, 0:512], mu[:, 512:1024], mu[:, 1024:1536], mu[:, 1536:1664]
    rk_row = row(r_k)
    rwkv_consts = lambda n: [mu_r, mu_k, mu_v, mu_l, w0, a0, k_k, k_a, w_dec, w_iclr, *chunk_sum_matrices(n), bd]
    rwkv_rows = [pcol(P_R, 512), pcol(P_K, 512), pcol(P_V, 512), pcol(P_LORA, 128)]
    rt, at, bt, kt, clf, uv, ur, k2 = row_call(
        "rwkv_prep", rwkv_prep_tile, n_rows, rwkv_rows, rwkv_consts(ROW_TILE), [(WIDTH, F32)] * 8, halo_in=rwkv_rows)
    y, m0s, state_maps, out_maps, *wkv_saved = wkv_forward(at, bt, kt, rt, uv, clf)

    tail = row_call(
        "tail", tail_tile, n_rows,
        [(xr, D_MODEL, 0), (tgt, D_MODEL, 0), pcol(P_MA, 1024), pcol(P_MB, 1024), pcol(P_GPA, 512), pcol(P_GPB, 512),
         (ya, WIDTH, 0), (y, WIDTH, 0), (ur, WIDTH, 0), (k2, WIDTH, 0), (uv, WIDTH, 0)],
        [mod, wpa, wpb, wout, gn_g, gn_b, rk_row, post_g, post_b, bd],
        [(D_MODEL, F32), (1024, BF16), (1024, BF16), (512, BF16), (512, BF16), (WIDTH, F32), (WIDTH, F32), (WIDTH, F32)],
        acc_out=[((1, LANE), F32), ((D_MODEL, D_MODEL), F32), ((WIDTH, D_MODEL), F32), ((WIDTH, D_MODEL), F32),
                 ((1, WIDTH), F32), ((1, WIDTH), F32), ((1, D_MODEL), F32), ((1, D_MODEL), F32), ((1, D_MODEL), F32)])
    (dz, dma, dmb, dgpa, dgpb, dya, dy, dyb,
     loss_row, g_wout, g_wpa, g_wpb, g_gn_g, g_gn_b, g_post_g, g_post_b, dgate) = tail

    def owner_blocks(g, n):
        r, cdim = next((r, cdim) for name, r, cdim in SHARDED if name == n)
        return (g.reshape(N_DEV, r, cdim) if n == "w_out" else g.reshape(r, N_DEV, cdim).transpose(1, 0, 2)).astype(BF16)

    early = ("w_proj_a", "w_proj_b", "w_out")
    dq, dk, dv, *got_early = attention_backward(
        q, k, v, ya, dya, lse, [owner_blocks(g, n) for g, n in zip((g_wpa, g_wpb, g_wout), early)])
    dq_c, dkv_c, dkr, dkrr, g_wq, g_wqr, g_wkn, g_wv, g_gq, g_gkv = row_call(
        "mla_prep_bwd", mla_prep_bwd_tile, n_rows,
        [pcol(P_QC, 256), pcol(P_KVC, 128), (cos_t, LANE, 0), (sin_t, LANE, 0),
         (dq, HEADS * LANE, 0), (dk, HEADS * LANE, 0), (dv, HEADS * LANE, 0)],
        mla_consts, [(256, BF16), (128, BF16), (128, BF16), (128, BF16)],
        acc_out=[((Q_RANK, HEADS * LANE), F32)] * 2 + [((KV_RANK, HEADS * LANE), F32)] * 2
        + [((1, Q_RANK), F32), ((1, KV_RANK), F32)], tile_rows=PREP_TILE)

    dat, dbt, dkt, drt, dvv, dlw = wkv_backward(at, bt, kt, rt, uv, clf, m0s, state_maps, out_maps, wkv_saved, dy)
    (dr0, dk0, dv0, dl0, g_mu_r, g_mu_k, g_mu_v, g_mu_l, g_w0, g_a0, g_k_k, g_k_a, g_r_k, g_wdec, g_wiclr) = row_call(
        "rwkv_prep_bwd", rwkv_prep_bwd_tile, n_rows,
        rwkv_rows + [(drt, WIDTH, 0), (dat, WIDTH, 0), (dbt, WIDTH, 0), (dkt, WIDTH, 0), (dvv, WIDTH, 0),
                     (dlw, WIDTH, 0), (dyb, WIDTH, 0)],
        rwkv_consts(PREP_TILE) + [rk_row], [(512, BF16), (512, BF16), (512, BF16), (128, BF16)],
        acc_out=[((1, 512), F32)] * 3 + [((1, 128), F32)] + [((1, 512), F32)] * 5 + [((LANE, WIDTH), F32)] * 2,
        halo_in=rwkv_rows, carry=[512, 512, 512, 128], reverse=True, tile_rows=PREP_TILE)

    li = jnp.arange(LANE)
    src, dst = li[:, None], li[None, :]
    half = ROPE // 2
    unrot = (jnp.where((dst >= NOPE) & (dst < NOPE + half) & (src == dst + half), 1.0, 0.0)
             - jnp.where((dst >= NOPE + half) & (dst < QK_DIM) & (src == dst - half), 1.0, 0.0)).astype(BF16)
    dx, h_t, dproj_blocks, dshift, dscale = in_backward(
        xr, dz, [dma, dmb, dr0, dk0, dv0, dgpa, dgpb, dq_c, dkv_c, dkr, dkrr, dl0], mod, w_in_pt, unrot)

    late = ("w_uq", "w_ukv", "w_decay_up", "w_iclr_up")
    late_grads = (unpad_heads_q_grad(g_wq, g_wqr), unpad_heads_kv_grad(g_wkn, g_wv), g_wdec[:LORA], g_wiclr[LORA:])
    blocks = [owner_blocks(g, n) for g, n in zip(late_grads, late)]
    dmod = jnp.concatenate([dshift, dscale, dgate], axis=1)
    small = jnp.concatenate([dmod, g_gq, g_gkv, g_mu_r, g_mu_k, g_mu_v, g_mu_l, g_w0, g_a0, g_k_k, g_k_a, g_r_k,
                             g_gn_g, g_gn_b, g_post_g, g_post_b, loss_row], axis=1)
    my_x, my_y, my_c = lax.axis_index("x"), lax.axis_index("y"), lax.axis_index("c")
    chip_order = [4 * (my_x ^ fx) + 2 * (my_y ^ fy) for fx, fy in ((1, 1), (1, 0), (0, 1), (0, 0))]
    owners = [chip_order[s % 4] + (my_c if s >= 4 else 1 - my_c) for s in WGRAD_SLOTS]
    order = jnp.stack(owners + [jnp.int32(s) for s in WGRAD_SLOTS]).astype(jnp.int32)
    got_w_in, *got_late, got_small = in_weight_grad_exchange(h_t, dproj_blocks, blocks, small, order)
    got = {"w_in": got_w_in, **dict(zip(late, got_late)), **dict(zip(early, got_early))}
    loss = jnp.sum(got_small[:, 0, SMALL_ELEMS])

    ada_cols = w_ada.shape[2]
    dmod_all = got_small[:, 0, :3 * D_MODEL]
    g_ada = ada_weight_grad(c_all, lax.dynamic_slice_in_dim(dmod_all, me * ada_cols, ada_cols, axis=1))

    outs = [dict() for _ in range(4)]
    res = adamw(g_ada[None], w_ada, m_w_ada, v_w_ada, "adamw_w_ada")
    for kind in range(4):
        outs[kind]["w_ada"] = res[kind]
    for n, _, _ in SHARDED:
        res = adamw(got[n], weights[n], mom1[n], mom2[n], "adamw_" + n)
        for kind in range(4):
            outs[kind][n] = res[kind]
    rows_of = lambda tree: [tree[n].reshape(1, -1) for n, _ in SMALL]
    res = adamw_small(got_small, rows_of(weights), rows_of(mom1), rows_of(mom2))
    for kind in range(4):
        for a, (n, _) in enumerate(SMALL):
            outs[kind][n] = res[kind * len(SMALL) + a].reshape(weights[n].shape)
    return (loss, dx[None], *[outs[0][n] for n in names], *[outs[1][n] for n in names],
            *[outs[2][n] for n in names], *[outs[3][n] for n in names])
```

```python
import functools
import math

import jax
import jax.numpy as jnp
from jax import lax
from jax.experimental import pallas as pl
from jax.experimental.pallas import tpu as pltpu

F32 = jnp.float32
BF16 = jnp.bfloat16
HIGHEST = lax.Precision.HIGHEST
MESH_IDS = pl.DeviceIdType.MESH

N_DEV = 8
D_MODEL = 1024
LN_EPS = 1e-5
RMS_EPS = 1e-6
GN_EPS = 64e-5
HEADS = 8
Q_RANK = 256
KV_RANK = 128
ROPE = 32
NOPE = 64
QK_DIM = NOPE + ROPE
WIDTH = 512
HEAD = 64
LORA = 64
CHUNK = 64
DEPTH = 1
ALPHA = (2.0 * DEPTH) ** 0.25
ROPE_THETA = 10000.0
ATTN_SCALE = QK_DIM ** -0.5
DECAY_SCALE = math.exp(-0.5)

ADAM_LR = 0.001
ADAM_B1 = 0.9
ADAM_B2 = 0.999
ADAM_EPS = 1e-08
ADAM_WD = 0.01
ADAM_STEP = 10

LANE = 128
PAIR = 2 * HEAD
ROW_TILE = 256
PREP_TILE = 512
HALO_ROWS = 16
ATTN_FWD_TILES = (512, 1024)
ATTN_BWD_TILES = (512, 512)
LOG2_E = math.log2(math.e)
Q_PRESCALE = ATTN_SCALE * LOG2_E
WKV_CHUNKS_PER_STEP = 16
WKV_CHAIN_GROUPS = 2
WGRAD_SLOTS = (0, 1, 4, 2, 5, 6, 3, 7)
VMEM_LIMIT = 56 * 1024 * 1024

P_MA, P_MB, P_R, P_K, P_V, P_GPA, P_GPB, P_QC, P_KVC, P_KR, P_KRR, P_LORA = (
    0, 1024, 2048, 2560, 3072, 3584, 4096, 4608, 4864, 4992, 5120, 5248)
P_WIDTH = 5376

N_QC, N_KVC, N_KROPE, N_GPA, N_RWKV, N_GPB, N_MA, N_MB = 0, 256, 384, 416, 928, 2592, 3104, 4128
IN_WIDTH = 5152

SHARDED = (("w_in", 1024, 644), ("w_uq", 256, 96), ("w_ukv", 128, 128), ("w_decay_up", 64, 64),
           ("w_iclr_up", 64, 64), ("w_proj_a", 512, 128), ("w_proj_b", 512, 128), ("w_out", 128, 1024))
SMALL = (("b_ada", 3072), ("q_norm_g", 256), ("kv_norm_g", 128), ("mu_rwkv", 1664), ("w0", 512), ("a0", 512),
         ("k_k", 512), ("k_a", 512), ("r_k", 512), ("gn_g", 512), ("gn_b", 512), ("post_g", 1024), ("post_b", 1024))
SMALL_ELEMS = sum(n for _, n in SMALL)


def mm(a, b):
    return jnp.dot(a.astype(BF16), b.astype(BF16), preferred_element_type=F32)


def mm_nt(a, b):
    return lax.dot_general(a.astype(BF16), b.astype(BF16), (((1,), (1,)), ((), ())), preferred_element_type=F32)


def mm_tn(a, b):
    return lax.dot_general(a.astype(BF16), b.astype(BF16), (((0,), (0,)), ((), ())), preferred_element_type=F32)


def hdot(a, b):
    return jnp.dot(a, b, precision=HIGHEST, preferred_element_type=F32)


def hdot_tn(a, b):
    return lax.dot_general(a, b, (((0,), (0,)), ((), ())), precision=HIGHEST, preferred_element_type=F32)


def sigmoid(x):
    return 1.0 / (1.0 + jnp.exp(-x))


def colsum(x):
    return jnp.sum(x, axis=0, keepdims=True)


def rowmean(x):
    return jnp.mean(x, axis=-1, keepdims=True)


def layer_norm_stats(x):
    xc = x - rowmean(x)
    rstd = lax.rsqrt(rowmean(xc * xc) + LN_EPS)
    return xc * rstd, rstd


def layer_norm_bwd(dy, xhat, rstd):
    return rstd * (dy - rowmean(dy) - xhat * rowmean(dy * xhat))


def bf16_pieces(x, n):
    pieces = []
    for _ in range(n):
        p = x.astype(BF16)
        pieces.append(p)
        x = x - p.astype(F32)
    return pieces


def ones_dot(ones, x, n_pieces):
    ones = ones.astype(BF16)
    return sum(jnp.dot(ones, p, preferred_element_type=F32) for p in bf16_pieces(x, n_pieces))


def ones_dot_nt(ones, x, n_pieces):
    ones = ones.astype(BF16)
    return sum(lax.dot_general(ones, p, (((1,), (1,)), ((), ())), preferred_element_type=F32)
               for p in bf16_pieces(x, n_pieces))


def head_sum(x, bd):
    return jnp.concatenate([mm(x[:, p * LANE:(p + 1) * LANE], bd) for p in range(x.shape[1] // LANE)], axis=1)


def tile_lanes(t, n):
    return jnp.concatenate([t] * n, axis=1)


def row_iota(shape):
    return lax.broadcasted_iota(jnp.int32, shape, 0)


def lane_iota(shape):
    return lax.broadcasted_iota(jnp.int32, shape, 1)


def shift_rows_down(x, row0):
    rolled = pltpu.roll(x, 1, axis=0)
    return jnp.where(row_iota(x.shape) == 0, row0, rolled)


def shift_rows_up(x, row_last):
    rolled = pltpu.roll(x, x.shape[0] - 1, axis=0)
    return jnp.where(row_iota(x.shape) == x.shape[0] - 1, row_last, rolled)


def row_call(name, fn, n_rows, row_in, const_in, row_out, acc_out=(), halo_in=(), carry=(), reverse=False,
             tile_rows=ROW_TILE):
    ts = tile_rows
    n_tiles = n_rows // ts
    n_in = len(row_in) + len(halo_in) + len(const_in)
    n_ro, n_ao = len(row_out), len(acc_out)

    def tile_of(g):
        return (n_tiles - 1 - g) if reverse else g

    def body(*refs):
        ins = refs[:n_in]
        ro = refs[n_in:n_in + n_ro]
        ao = refs[n_in + n_ro:n_in + n_ro + n_ao]
        cr = refs[n_in + n_ro + n_ao:]
        g = pl.program_id(0)
        step0 = g == 0
        tile0 = tile_of(g) == 0
        for r in cr:
            @pl.when(step0)
            def _(r=r):
                r[...] = jnp.zeros_like(r)
        n_tiled = len(row_in) + len(halo_in)
        vals = [r[...].astype(F32) for r in ins[:n_tiled]] + [r[...] for r in ins[n_tiled:]]
        outs = fn(step0, tile0, *vals, *[c[0:1, :] for c in cr])
        for r, v in zip(ro, outs[:n_ro]):
            r[...] = v.astype(r.dtype)
        for r, v in zip(ao, outs[n_ro:n_ro + n_ao]):
            @pl.when(step0)
            def _(r=r, v=v):
                r[...] = v.astype(r.dtype)

            @pl.when(jnp.logical_not(step0))
            def _(r=r, v=v):
                r[...] += v.astype(r.dtype)
        for r, v in zip(cr, outs[n_ro + n_ao:]):
            r[0:1, :] = v

    in_specs = [pl.BlockSpec((ts, w), functools.partial(lambda g, cb: (tile_of(g), cb), cb=cb)) for _, w, cb in row_in]
    in_specs += [pl.BlockSpec((HALO_ROWS, w), functools.partial(
        lambda g, cb: (jnp.maximum(tile_of(g) * (ts // HALO_ROWS) - 1, 0), cb), cb=cb)) for _, w, cb in halo_in]
    in_specs += [pl.BlockSpec(memory_space=pltpu.VMEM) for _ in const_in]
    out_specs = [pl.BlockSpec((ts, w), lambda g: (tile_of(g), 0)) for w, _ in row_out]
    out_specs += [pl.BlockSpec(s, lambda g: (0, 0)) for s, _ in acc_out]
    out_shape = [jax.ShapeDtypeStruct((n_rows, w), d) for w, d in row_out]
    out_shape += [jax.ShapeDtypeStruct(s, d) for s, d in acc_out]
    return pl.pallas_call(
        body, name=name, grid=(n_tiles,), in_specs=in_specs, out_specs=out_specs, out_shape=out_shape,
        scratch_shapes=[pltpu.VMEM((8, w), F32) for w in carry],
        compiler_params=pltpu.CompilerParams(dimension_semantics=("arbitrary",), vmem_limit_bytes=VMEM_LIMIT),
    )(*[a for a, _, _ in row_in], *[a for a, _, _ in halo_in], *const_in)


def my_position():
    return lax.axis_index("x"), lax.axis_index("y"), lax.axis_index("c")


def flip(pos, k):
    x, y, c = pos
    dx, dy, dc = (k >> 2) & 1, (k >> 1) & 1, k & 1
    return (1 - x if dx else x, 1 - y if dy else y, 1 - c if dc else c)


def flat_index(pos):
    return 4 * pos[0] + 2 * pos[1] + pos[2]


def gather_shards(shards):
    n = len(shards)

    def body(*refs):
        x_refs, out_refs = refs[:n], refs[n:2 * n]
        send_sems, recv_sems, local_sems = refs[2 * n:]
        x, y, c = my_position()
        me, sibling = (x, y, c), (x, y, 1 - c)
        chips = [(1 - x, y), (x, 1 - y), (1 - x, 1 - y)]
        relay_from = (x ^ (1 - c), y ^ c, c)
        relay_to = (x ^ c, y ^ (1 - c), c)

        def copy(a, k, block, to, from_input=False):
            slot = out_refs[a].at[flat_index(block)]
            return pltpu.make_async_remote_copy(
                src_ref=x_refs[a] if from_input else slot, dst_ref=slot,
                send_sem=send_sems.at[7 * a + k], recv_sem=recv_sems.at[7 * a + k],
                device_id=to, device_id_type=MESH_IDS)

        mine = [pltpu.make_async_copy(x_refs[a], out_refs[a].at[flat_index(me)], local_sems.at[a]) for a in range(n)]
        for cp in mine:
            cp.start()
        first = []
        for a in range(n):
            first.append(copy(a, 0, me, sibling, from_input=True))
            first += [copy(a, 1 + j, me, (*chip, c), from_input=True) for j, chip in enumerate(chips[:2])]
        for cp in first:
            cp.start()
        relayed = [copy(a, 3, relay_from, relay_to) for a in range(n)]
        passed = []
        for j, chip in enumerate(chips):
            for a in range(n):
                copy(a, 1 + j, (*chip, c), me).wait_recv()
                cp = copy(a, 4 + j, (*chip, c), sibling)
                cp.start()
                passed.append(cp)
                if j < 2:
                    @pl.when(c == j)
                    def _(a=a):
                        relayed[a].start()
        for a in range(n):
            copy(a, 0, sibling, me).wait_recv()
            for j, chip in enumerate(chips):
                copy(a, 4 + j, (*chip, 1 - c), me).wait_recv()
        for cp in first + passed + relayed:
            cp.wait_send()
        for cp in mine:
            cp.wait()

    return pl.pallas_call(
        body, name="gather_shards",
        out_shape=[jax.ShapeDtypeStruct((N_DEV,) + s.shape, s.dtype) for s in shards],
        in_specs=[pl.BlockSpec(memory_space=pl.ANY)] * n, out_specs=[pl.BlockSpec(memory_space=pl.ANY)] * n,
        scratch_shapes=[pltpu.SemaphoreType.DMA((7 * n,)), pltpu.SemaphoreType.DMA((7 * n,)),
                        pltpu.SemaphoreType.DMA((n,))],
    )(*shards)


def ada_modulation(c_all, w_ada_loc, b_ada_blocks):
    cols = w_ada_loc.shape[1]

    def body(c_ref, w_ref, b_ref, out_ref, send_sems, recv_sems):
        me = my_position()
        mi = flat_index(me)
        cv = c_ref[...]
        res = hdot(cv * sigmoid(cv), w_ref[...]) + b_ref[pl.ds(mi, 1), :]
        out_ref[mi] = res
        sends = []
        for k in range(1, N_DEV):
            cp = pltpu.make_async_remote_copy(
                src_ref=out_ref.at[mi], dst_ref=out_ref.at[mi], send_sem=send_sems.at[k - 1],
                recv_sem=recv_sems.at[k - 1], device_id=flip(me, k), device_id_type=MESH_IDS)
            cp.start()
            sends.append(cp)
        for k in range(1, N_DEV):
            pi = flat_index(flip(me, k))
            pltpu.make_async_remote_copy(
                src_ref=out_ref.at[pi], dst_ref=out_ref.at[pi], send_sem=send_sems.at[k - 1],
                recv_sem=recv_sems.at[k - 1], device_id=flip(me, k), device_id_type=MESH_IDS).wait_recv()
        for cp in sends:
            cp.wait_send()

    return pl.pallas_call(
        body, name="ada_modulation",
        out_shape=jax.ShapeDtypeStruct((N_DEV, N_DEV, cols), F32),
        in_specs=[pl.BlockSpec(memory_space=pltpu.VMEM)] * 3, out_specs=pl.BlockSpec(memory_space=pltpu.VMEM),
        scratch_shapes=[pltpu.SemaphoreType.DMA((7,)), pltpu.SemaphoreType.DMA((7,))],
    )(c_all, w_ada_loc, b_ada_blocks)


def fwd_in_tile(step0, tile0, x, mod, w_in_pt):
    xhat, _ = layer_norm_stats(x)
    h = xhat * (1.0 + mod[1:2]) + mod[0:1]
    return (mm_nt(h, w_in_pt),)


def fwd_in_gather(x, mod, w_in_pt, shards):
    n = len(shards)
    n_rows = x.shape[0]
    ts = PREP_TILE
    n_tiles = n_rows // ts

    def body(x_ref, mod_ref, w_ref, *rest):
        s_refs = rest[:n]
        proj_ref, out_refs = rest[n], rest[n + 1:2 * n + 1]
        send_sems, recv_sems, local_sems = rest[2 * n + 1:]
        g = pl.program_id(0)
        me = my_position()
        mi = flat_index(me)

        def copies(k, slot):
            return [pltpu.make_async_remote_copy(
                src_ref=s_refs[a], dst_ref=out_refs[a].at[slot], send_sem=send_sems.at[7 * a + k - 1],
                recv_sem=recv_sems.at[7 * a + k - 1], device_id=flip(me, k), device_id_type=MESH_IDS)
                for a in range(n)]

        local = [pltpu.make_async_copy(s_refs[a], out_refs[a].at[mi], local_sems.at[a]) for a in range(n)]

        @pl.when(g == 0)
        def _():
            for cp in local:
                cp.start()
            for k in range(1, N_DEV):
                for cp in copies(k, mi):
                    cp.start()

        proj_ref[...] = fwd_in_tile(None, None, x_ref[...], mod_ref[...], w_ref[...])[0].astype(BF16)

        @pl.when(g == n_tiles - 1)
        def _():
            for k in range(1, N_DEV):
                for cp in copies(k, flat_index(flip(me, k))):
                    cp.wait_recv()
            for k in range(1, N_DEV):
                for cp in copies(k, mi):
                    cp.wait_send()
            for cp in local:
                cp.wait()

    hbm = pl.BlockSpec(memory_space=pl.ANY)
    const = pl.BlockSpec(memory_space=pltpu.VMEM)
    return pl.pallas_call(
        body, name="fwd_in_gather", grid=(n_tiles,),
        in_specs=[pl.BlockSpec((ts, D_MODEL), lambda g: (g, 0)), const, const] + [hbm] * n,
        out_specs=[pl.BlockSpec((ts, P_WIDTH), lambda g: (g, 0))] + [hbm] * n,
        out_shape=[jax.ShapeDtypeStruct((n_rows, P_WIDTH), BF16)]
        + [jax.ShapeDtypeStruct((N_DEV,) + s.shape, s.dtype) for s in shards],
        scratch_shapes=[pltpu.SemaphoreType.DMA((7 * n,)), pltpu.SemaphoreType.DMA((7 * n,)),
                        pltpu.SemaphoreType.DMA((n,))],
        compiler_params=pltpu.CompilerParams(dimension_semantics=("arbitrary",), vmem_limit_bytes=VMEM_LIMIT),
    )(x, mod, w_in_pt, *shards)


def rms_norm_fwd(x, g):
    r = lax.rsqrt(rowmean(x * x) + RMS_EPS)
    xh = x * r
    return xh * g, xh, r


def key_rope_mask(shape):
    return (lane_iota(shape) >= NOPE).astype(F32)


def mla_prep_tile(step0, tile0, q_c, kv_c, kr, krr, cos, sin, gq, gkv, wq, wqr, wkn, wv):
    qn, _, _ = rms_norm_fwd(q_c, gq)
    kvn, _, _ = rms_norm_fwd(kv_c, gkv)
    q = (mm(qn, wq) * tile_lanes(cos, HEADS) + mm(qn, wqr) * tile_lanes(sin, HEADS)) * Q_PRESCALE
    kpe = kr * (cos * key_rope_mask(cos.shape)) + krr * sin
    k = mm(kvn, wkn) + tile_lanes(kpe, HEADS)
    v = mm(kvn, wv)
    return q, k, v


def rwkv_prep_core(tile0, r0, k0, v0, l0, hr, hk, hv, hl, mu_r, mu_k, mu_v, mu_l, w0, a0, k_k, k_a,
                   w_dec, w_iclr, tril, same, bd):
    def shifted(x, halo, mu):
        row0 = jnp.where(tile0, 0.0, halo[HALO_ROWS - 1:HALO_ROWS, :])
        prev = shift_rows_down(x, row0)
        return x + (prev - x) * mu, prev

    ur, pr = shifted(r0, hr, mu_r)
    uk, pk = shifted(k0, hk, mu_k)
    uv, pv = shifted(v0, hv, mu_v)
    ul, plo = shifted(l0, hl, mu_l)
    th = jnp.tanh(ul)
    sg = sigmoid(w0 + mm(th, w_dec))
    lw = -DECAY_SCALE * sg
    a_ic = sigmoid(a0 + mm(ul, w_iclr))
    kkraw = uk * k_k
    nrm_raw = jnp.sqrt(head_sum(kkraw * kkraw, bd))
    nrm = jnp.maximum(nrm_raw, 1e-12)
    kk = kkraw / nrm
    k2 = uk * (1.0 + (a_ic - 1.0) * k_a)
    lc = ones_dot(tril, lw, 3)
    lcl = ones_dot(same, lw, 3)
    return dict(ur=ur, uk=uk, uv=uv, ul=ul, pr=pr, pk=pk, pv=pv, pl=plo, th=th, sg=sg, lw=lw, a_ic=a_ic,
                kkraw=kkraw, nrm_raw=nrm_raw, nrm=nrm, kk=kk, k2=k2, lc=lc, lcl=lcl)


def rwkv_prep_tile(step0, tile0, r0, k0, v0, l0, hr, hk, hv, hl, *consts):
    f = rwkv_prep_core(tile0, r0, k0, v0, l0, hr, hk, hv, hl, *consts)
    lc, lw = f["lc"], f["lw"]
    e_neg = jnp.exp(-lc)
    rt = f["ur"] * jnp.exp(lc)
    at = -f["kk"] * jnp.exp(lc - lw)
    bt = f["kk"] * f["a_ic"] * e_neg
    kt = f["k2"] * e_neg
    return rt, at, bt, kt, jnp.exp(f["lcl"]), f["uv"], f["ur"], f["k2"]


def wkv_masks():
    lane = lane_iota((1, PAIR))
    m_lo = (lane < HEAD).astype(F32)
    r2 = row_iota((PAIR, PAIR))
    c2 = lane_iota((PAIR, PAIR))
    bd = ((r2 < HEAD) == (c2 < HEAD)).astype(F32)
    eye2 = (r2 == c2).astype(F32)
    eye = (row_iota((CHUNK, CHUNK)) == lane_iota((CHUNK, CHUNK))).astype(F32)
    t_idx = row_iota((4 * CHUNK, PAIR)) % CHUNK
    s_idx = lane_iota((4 * CHUNK, PAIR)) % CHUNK
    keep = s_idx < t_idx + (row_iota((4 * CHUNK, PAIR)) >= 2 * CHUNK).astype(jnp.int32)
    return (m_lo, 1.0 - m_lo), keep, eye, bd, eye2


def rows(*parts):
    return jnp.concatenate(parts, axis=0)


def lanes(*parts):
    return jnp.concatenate(parts, axis=1)


def head_rows(x, ms):
    return rows(x * ms[0], x * ms[1])


def wkv_score_stack(at, rt, ms):
    return rows(head_rows(at, ms), head_rows(rt, ms))


def wkv_chunks_pre(chunks, masks, between_stages=lambda: None):
    ms, keep, eye, bd, eye2 = masks
    n = len(chunks)
    at, bt, kt, rt, v, cl = (list(t) for t in zip(*chunks))
    scores = [jnp.where(keep, mm_nt(wkv_score_stack(a, r, ms), rows(b, k)), 0.0)
              for a, r, b, k in zip(at, rt, bt, kt)]
    between_stages()
    q = CHUNK
    aab = [s[h * q:(h + 1) * q, :q] for s in scores for h in range(2)]
    tinv = [eye + a for a in aab]
    power = [mm(a, a) for a in aab]
    between_stages()
    for _ in range(5):
        both = [mm(rows(t, p), p) for t, p in zip(tinv, power)]
        tinv = [t + x[:q] for t, x in zip(tinv, both)]
        power = [x[q:] for x in both]
        between_stages()
    pair = lambda c, row0, col0: lanes(scores[c][row0:row0 + q, col0:col0 + q],
                                       scores[c][row0 + q:row0 + 2 * q, col0:col0 + q])
    tinv_p = [lanes(tinv[2 * c], tinv[2 * c + 1]) for c in range(n)]
    aak_p = [pair(c, 0, q) for c in range(n)]
    prb_p = [pair(c, 2 * q, 0) for c in range(n)]
    prk_p = [pair(c, 2 * q, q) for c in range(n)]
    v_rows = [head_rows(x, ms) for x in v]
    wy = [mm(rows(a, p), x) for a, p, x in zip(aak_p, prk_p, v_rows)]
    between_stages()
    w = [x[:q] for x in wy]
    yh2 = [x[q:] for x in wy]
    aw = [mm(t, lanes(head_rows(a, ms), head_rows(w_, ms))) for t, a, w_ in zip(tinv_p, at, w)]
    between_stages()
    ah = [x[:, :PAIR] for x in aw]
    wh = [x[:, PAIR:] for x in aw]
    ry = [mm(p, lanes(head_rows(a, ms), head_rows(w_, ms))) for p, a, w_ in zip(prb_p, ah, wh)]
    between_stages()
    rh = [r + x[:, :PAIR] for r, x in zip(rt, ry)]
    yh = [x[:, PAIR:] + y for x, y in zip(ry, yh2)]
    bc = [b * c_ for b, c_ in zip(bt, cl)]
    kc = [k * c_ for k, c_ in zip(kt, cl)]
    gh = [mm_tn(b, lanes(a, w_)) for b, a, w_ in zip(bc, ah, wh)]
    g = [eye2 * c_ + bd * x[:, :PAIR] for c_, x in zip(cl, gh)]
    h = [bd * (x[:, PAIR:] + mm_tn(k, v_)) for x, k, v_ in zip(gh, kc, v)]
    as_bf16 = lambda xs: [x.astype(BF16) for x in xs]
    saved = (as_bf16(tinv_p), as_bf16(aak_p), as_bf16(prb_p), as_bf16(prk_p), as_bf16(ah), wh)
    return g, h, rh, yh, saved


def wkv_chunks_grad(chunks, saved, m0, dy, dm1, masks, between_stages=lambda: None):
    ms, keep, eye, bd, eye2 = masks
    n = len(chunks)
    q = CHUNK
    at, bt, kt, rt, v, cl = (list(t) for t in zip(*chunks))
    tinv_p, aak_p, prb_p, prk_p, ah, wh = (list(t) for t in zip(*saved))
    head_stack = lambda p: rows(p[:, :q], p[:, q:])
    bc = [b * c_ for b, c_ in zip(bt, cl)]
    kc = [k * c_ for k, c_ in zip(kt, cl)]
    u = [mm(a, m) + w for a, m, w in zip(ah, m0, wh)]
    between_stages()
    dm1 = [d * bd for d in dm1]
    from_state = [mm(rows(b, k), d) for b, k, d in zip(bc, kc, dm1)]
    between_stages()
    dy_rows = [head_rows(d, ms) for d in dy]
    from_out = [mm_tn(lanes(head_stack(pb), head_stack(pk)), d) for pb, pk, d in zip(prb_p, prk_p, dy_rows)]
    between_stages()
    du = [a[:q] + b[:q] for a, b in zip(from_state, from_out)]
    dv = [a[q:] + b[q:] for a, b in zip(from_state, from_out)]
    dz = [mm_tn(head_stack(t), head_rows(d, ms)) for t, d in zip(tinv_p, du)]
    between_stages()
    dz_rows = [head_rows(d, ms) for d in dz]
    dv = [a + mm_tn(head_stack(k), d) for a, k, d in zip(dv, aak_p, dz_rows)]
    between_stages()
    by_m0 = [mm_nt(rows(d, z), m) for d, z, m in zip(dy, dz, m0)]
    between_stages()
    uv = [rows(x, y) for x, y in zip(u, v)]
    by_dm1 = [mm_nt(x, d) for x, d in zip(uv, dm1)]
    between_stages()
    udm = [x[:q] for x in by_dm1]
    vdm = [x[q:] for x in by_dm1]
    dscores = [jnp.where(keep, mm_nt(rows(z, d), x), 0.0) for z, d, x in zip(dz_rows, dy_rows, uv)]
    between_stages()
    to_ar = [mm(d, rows(b, k)) for d, b, k in zip(dscores, bt, kt)]
    to_bk = [mm_tn(d, wkv_score_stack(a, r, ms)) for d, a, r in zip(dscores, at, rt)]
    ones = jnp.ones((8, PAIR), F32)
    upper = (lane_iota((CHUNK, CHUNK)) >= row_iota((CHUNK, CHUNK))).astype(F32)
    out = []
    for c in range(n):
        e = to_ar[c]
        dat_c = by_m0[c][q:] + e[:q] * ms[0] + e[q:2 * q] * ms[1]
        drt_c = by_m0[c][:q] + e[2 * q:3 * q] * ms[0] + e[3 * q:] * ms[1]
        dbt_c = udm[c] * cl[c] + to_bk[c][:q]
        dkt_c = vdm[c] * cl[c] + to_bk[c][q:]
        dlcl = ones_dot_nt(ones, dm1[c] * m0[c], 3)[0:1, :] * cl[c] + colsum(bc[c] * udm[c] + kc[c] * vdm[c])
        g = drt_c * rt[c] - dbt_c * bt[c] - dkt_c * kt[c] + dat_c * at[c]
        dlw = ones_dot(upper, g, 3) - dat_c * at[c] + dlcl
        out.append((dat_c, dbt_c, dkt_c, drt_c, dv[c], dlw))
    return out


def wkv_forward(at, bt, kt, rt, v, clf):
    n_rows = at.shape[0]
    cps = WKV_CHUNKS_PER_STEP
    rb = cps * CHUNK
    n_steps = n_rows // rb

    def body(a_ref, b_ref, k_ref, r_ref, v_ref, c_ref, y_ref, m0_ref, g_ref, rh_ref, *rest):
        saved_refs, m_scr = rest[:6], rest[6]

        @pl.when(pl.program_id(1) == 0)
        def _():
            m_scr[...] = jnp.zeros_like(m_scr)

        masks = wkv_masks()
        chunks = []
        for cc in range(cps):
            sl = slice(cc * CHUNK, (cc + 1) * CHUNK)
            chunks.append((a_ref[sl, :], b_ref[sl, :], k_ref[sl, :], r_ref[sl, :], v_ref[sl, :],
                           c_ref[cc * CHUNK:cc * CHUNK + 1, :]))
        state = [m_scr[...]]
        pending = []

        def chain_step():
            if not pending:
                return
            cc, g, h, rh, yh = pending.pop(0)
            sl = slice(cc * CHUNK, (cc + 1) * CHUNK)
            m = state[0]
            m0_ref[0, cc] = m
            g_ref[0, cc] = g
            rh_ref[sl, :] = rh
            y_ref[sl, :] = hdot(rh, m) + yh
            state[0] = hdot(g, m) + h

        def prepare(first, last, between_stages):
            gs, hs, rhs, yhs, saved = wkv_chunks_pre(chunks[first:last], masks, between_stages)
            for ref, per_chunk in zip(saved_refs, saved):
                for cc, val in enumerate(per_chunk, start=first):
                    ref[cc * CHUNK:(cc + 1) * CHUNK, :] = val
            pending.extend(zip(range(first, last), gs, hs, rhs, yhs))

        group = cps // WKV_CHAIN_GROUPS
        for first in range(0, cps, group):
            prepare(first, first + group, chain_step)
        while pending:
            chain_step()
        m_scr[...] = state[0]

    blk = pl.BlockSpec((rb, PAIR), lambda p, s: (s, p))
    state_blk = pl.BlockSpec((1, cps, PAIR, PAIR), lambda p, s: (p, s, 0, 0))
    state_shape = jax.ShapeDtypeStruct((WIDTH // PAIR, n_rows // CHUNK, PAIR, PAIR), F32)
    rows_f32 = jax.ShapeDtypeStruct((n_rows, WIDTH), F32)
    rows_bf16 = jax.ShapeDtypeStruct((n_rows, WIDTH), BF16)
    return pl.pallas_call(
        body, name="wkv_forward", grid=(WIDTH // PAIR, n_steps),
        in_specs=[blk] * 6,
        out_specs=[blk, state_blk, state_blk, blk] + [blk] * 6,
        out_shape=[rows_f32, state_shape, state_shape, rows_f32] + [rows_bf16] * 5 + [rows_f32],
        scratch_shapes=[pltpu.VMEM((PAIR, PAIR), F32)],
        compiler_params=pltpu.CompilerParams(dimension_semantics=("arbitrary", "arbitrary"),
                                             vmem_limit_bytes=VMEM_LIMIT),
    )(at, bt, kt, rt, v, clf)


def wkv_backward(at, bt, kt, rt, v, clf, m0s, gs, rh, saved, dy):
    n_rows = at.shape[0]
    cps = WKV_CHUNKS_PER_STEP
    rb = cps * CHUNK
    n_steps = n_rows // rb

    def body(a_ref, b_ref, k_ref, r_ref, v_ref, c_ref, m0_ref, g_ref, rh_ref, *rest):
        saved_refs, dy_ref = rest[:6], rest[6]
        da_ref, db_ref, dk_ref, dr_ref, dv_ref, dlw_ref, dm_scr = rest[7:]

        @pl.when(pl.program_id(1) == 0)
        def _():
            dm_scr[...] = jnp.zeros_like(dm_scr)

        masks = wkv_masks()
        bd = masks[3]
        state = [dm_scr[...]]
        dm1 = [None] * cps
        todo = list(reversed(range(cps)))

        def chain_step():
            if not todo:
                return
            cc = todo.pop(0)
            sl = slice(cc * CHUNK, (cc + 1) * CHUNK)
            dm1[cc] = state[0]
            state[0] = bd * (hdot_tn(g_ref[0, cc], state[0]) + hdot_tn(rh_ref[sl, :], dy_ref[sl, :]))

        def gradients(first, last, between_stages):
            chunks, kept, m0, dys = [], [], [], []
            for cc in range(first, last):
                sl = slice(cc * CHUNK, (cc + 1) * CHUNK)
                chunks.append((a_ref[sl, :], b_ref[sl, :], k_ref[sl, :], r_ref[sl, :], v_ref[sl, :],
                               c_ref[cc * CHUNK:cc * CHUNK + 1, :]))
                kept.append(tuple(ref[sl, :] for ref in saved_refs))
                m0.append(m0_ref[0, cc])
                dys.append(dy_ref[sl, :])
            grads = wkv_chunks_grad(chunks, kept, m0, dys, dm1[first:last], masks, between_stages)
            for cc, (dat, dbt, dkt, drt, dv, dlw) in enumerate(grads, start=first):
                sl = slice(cc * CHUNK, (cc + 1) * CHUNK)
                da_ref[sl, :] = dat
                db_ref[sl, :] = dbt
                dk_ref[sl, :] = dkt
                dr_ref[sl, :] = drt
                dv_ref[sl, :] = dv
                dlw_ref[sl, :] = dlw

        group = cps // WKV_CHAIN_GROUPS
        for first in reversed(range(0, cps, group)):
            while todo and todo[0] >= first:
                chain_step()
            gradients(first, first + group, chain_step)
        dm_scr[...] = state[0]

    blk = pl.BlockSpec((rb, PAIR), lambda p, s: (n_steps - 1 - s, p))
    state_blk = pl.BlockSpec((1, cps, PAIR, PAIR), lambda p, s: (p, n_steps - 1 - s, 0, 0))
    return pl.pallas_call(
        body, name="wkv_backward", grid=(WIDTH // PAIR, n_steps),
        in_specs=[blk] * 6 + [state_blk, state_blk, blk] + [blk] * 6 + [blk],
        out_specs=[blk] * 6,
        out_shape=[jax.ShapeDtypeStruct((n_rows, WIDTH), F32)] * 6,
        scratch_shapes=[pltpu.VMEM((PAIR, PAIR), F32)],
        compiler_params=pltpu.CompilerParams(dimension_semantics=("arbitrary", "arbitrary"),
                                             vmem_limit_bytes=VMEM_LIMIT),
    )(at, bt, kt, rt, v, clf, m0s, gs, rh, *saved, dy)


def visible(q_row0, k_row0, shape):
    qc = (q_row0 + row_iota(shape)) // CHUNK
    kc = (k_row0 + lane_iota(shape)) // CHUNK
    return kc <= qc


def attention_forward(q, k, v):
    n_rows = q.shape[0]
    tq, tk = ATTN_FWD_TILES
    n_q = n_rows // tq
    assert tk % tq == 0

    def body(q_ref, k_ref, v_ref, o_ref, lse_ref):
        i = pl.program_id(1)
        lane = lane_iota((tq, LANE))
        heads = [slice(0, LANE), slice(LANE, 2 * LANE)]
        qs = [q_ref[:, cols] for cols in heads]

        def step(j, carry, size, masked):
            rows = pl.ds(pl.multiple_of(j * size, size), size)
            ss = [mm_nt(qh, k_ref[rows, cols]) for qh, cols in zip(qs, heads)]
            if masked:
                vis = visible(i * tq, j * size, ss[0].shape)
                ss = [jnp.where(vis, s, -jnp.inf) for s in ss]
            ps, stats = [], []
            for s, (m, l, _) in zip(ss, carry):
                m_new = jnp.maximum(m, jnp.max(s, axis=-1, keepdims=True))
                p = jnp.exp2(s - m_new)
                alpha = jnp.exp2(m - m_new)
                ps.append(p)
                stats.append((m_new, alpha, alpha * l + jnp.sum(p, axis=-1, keepdims=True)))
            pvs = [mm(p, v_ref[rows, cols]) for p, cols in zip(ps, heads)]
            return tuple((m_new, l, alpha * acc + pv)
                         for (m_new, alpha, l), (_, _, acc), pv in zip(stats, carry, pvs))

        carry = tuple((jnp.full((tq, 1), -jnp.inf, F32), jnp.zeros((tq, 1), F32), jnp.zeros((tq, LANE), F32))
                      for _ in heads)
        n_full = (i * tq) // tk
        carry = lax.fori_loop(0, n_full, functools.partial(step, size=tk, masked=False), carry)
        (m0, l0, acc0), (m1, l1, acc1) = step(n_full, carry, size=tk, masked=True)
        o_ref[...] = acc0 / l0 + acc1 / l1
        lse_ref[...] = jnp.where(lane >= HEAD, m1 + jnp.log2(l1), m0 + jnp.log2(l0))

    return pl.pallas_call(
        body, name="attention_forward", grid=(HEADS // 2, n_q),
        in_specs=[pl.BlockSpec((tq, 2 * LANE), lambda p, i: (i, p)),
                  pl.BlockSpec((n_rows, 2 * LANE), lambda p, i: (0, p)),
                  pl.BlockSpec((n_rows, 2 * LANE), lambda p, i: (0, p))],
        out_specs=[pl.BlockSpec((tq, LANE), lambda p, i: (i, p))] * 2,
        out_shape=[jax.ShapeDtypeStruct((n_rows, WIDTH), F32)] * 2,
        compiler_params=pltpu.CompilerParams(dimension_semantics=("arbitrary", "arbitrary"),
                                             vmem_limit_bytes=VMEM_LIMIT),
    )(q, k, v)


def block_exchange(g_refs, rg_refs, send_sems, recv_sems, local_sems):
    n = len(g_refs)
    me = my_position()
    mi = flat_index(me)

    def copies(k, src_index, dst_index):
        return [pltpu.make_async_remote_copy(
            src_ref=g_refs[a].at[src_index], dst_ref=rg_refs[a].at[dst_index],
            send_sem=send_sems.at[7 * a + k - 1], recv_sem=recv_sems.at[7 * a + k - 1],
            device_id=flip(me, k), device_id_type=MESH_IDS) for a in range(n)]

    local = [pltpu.make_async_copy(g_refs[a].at[mi], rg_refs[a].at[mi], local_sems.at[a]) for a in range(n)]

    def start():
        for cp in local:
            cp.start()
        for k in range(1, N_DEV):
            for cp in copies(k, flat_index(flip(me, k)), mi):
                cp.start()

    def wait():
        for k in range(1, N_DEV):
            pi = flat_index(flip(me, k))
            for cp in copies(k, pi, pi):
                cp.wait_recv()
        for k in range(1, N_DEV):
            for cp in copies(k, flat_index(flip(me, k)), mi):
                cp.wait_send()
        for cp in local:
            cp.wait()

    return start, wait


def attention_backward(q, k, v, o, do, lse, riders):
    n_rows = q.shape[0]
    tq, tk = ATTN_BWD_TILES
    n_q = n_rows // tq
    n_k = n_rows // tk
    n_masked = max(1, tk // tq)
    n_r = len(riders)

    def body(q_ref, k_ref, v_ref, o_ref, do_ref, lse_ref, *rest):
        g_refs = rest[:n_r]
        dq_ref, dk_ref, dv_ref = rest[n_r:n_r + 3]
        rg_refs = rest[n_r + 3:2 * n_r + 3]
        start_riders, wait_riders = block_exchange(g_refs, rg_refs, *rest[2 * n_r + 3:])
        j = pl.program_id(1)

        @pl.when(jnp.logical_and(pl.program_id(0) == 0, j == 0))
        def _():
            start_riders()

        @pl.when(j == 0)
        def _():
            dq_ref[...] = jnp.zeros_like(dq_ref)

        lane = lane_iota((tq, LANE))
        heads = [slice(0, LANE), slice(LANE, 2 * LANE)]
        ks = [k_ref[:, cols] for cols in heads]
        vs = [v_ref[:, cols] for cols in heads]
        head_lanes = [(lane < HEAD).astype(F32), (lane >= HEAD).astype(F32)]

        def step(i, carry, masked):
            rows = pl.ds(pl.multiple_of(i * tq, tq), tq)
            qs = [q_ref[rows, cols] for cols in heads]
            dout = do_ref[rows, :]
            dout_o = dout * o_ref[rows, :]
            lse_t = lse_ref[rows, :]
            ss = [mm_nt(qh, kh) for qh, kh in zip(qs, ks)]
            dps = [mm_nt(dout, vh) for vh in vs]
            ps, dss = [], []
            for hh in range(2):
                delta = jnp.sum(dout_o * head_lanes[hh], axis=-1, keepdims=True)
                lse_h = jnp.sum(jnp.where(lane == hh * HEAD, lse_t, 0.0), axis=-1, keepdims=True)
                p = jnp.exp2(ss[hh] - lse_h)
                if masked:
                    p = jnp.where(visible(i * tq, j * tk, p.shape), p, 0.0)
                ps.append(p)
                dss.append(p * (dps[hh] - delta))
            dvs = [mm_tn(p, dout) for p in ps]
            dqs = [mm(ds, kh) for ds, kh in zip(dss, ks)]
            dks = [mm_tn(ds, qh) for ds, qh in zip(dss, qs)]
            for cols, dq in zip(heads, dqs):
                dq_ref[rows, cols] += dq * ATTN_SCALE
            return tuple((dk + a, dv + b) for (dk, dv), a, b in zip(carry, dks, dvs))

        carry = tuple((jnp.zeros((tk, LANE), F32), jnp.zeros((tk, LANE), F32)) for _ in heads)
        i_first = (j * tk) // tq
        for extra in range(n_masked):
            carry = step(i_first + extra, carry, masked=True)
        carry = lax.fori_loop(i_first + n_masked, n_q, functools.partial(step, masked=False), carry)
        for cols, (dk, dv) in zip(heads, carry):
            dk_ref[:, cols] = dk * (1.0 / LOG2_E)
            dv_ref[:, cols] = dv

        @pl.when(jnp.logical_and(pl.program_id(0) == HEADS // 2 - 1, j == n_k - 1))
        def _():
            wait_riders()

    full = lambda w: pl.BlockSpec((n_rows, w), lambda p, j: (0, p))
    blk = pl.BlockSpec((tk, 2 * LANE), lambda p, j: (j, p))
    hbm = pl.BlockSpec(memory_space=pl.ANY)
    return pl.pallas_call(
        body, name="attention_backward", grid=(HEADS // 2, n_k),
        in_specs=[full(2 * LANE), blk, blk, full(LANE), full(LANE), full(LANE)] + [hbm] * n_r,
        out_specs=[full(2 * LANE), blk, blk] + [hbm] * n_r,
        out_shape=[jax.ShapeDtypeStruct((n_rows, HEADS * LANE), F32)] * 3
        + [jax.ShapeDtypeStruct(r.shape, r.dtype) for r in riders],
        scratch_shapes=[pltpu.SemaphoreType.DMA((7 * n_r,)), pltpu.SemaphoreType.DMA((7 * n_r,)),
                        pltpu.SemaphoreType.DMA((n_r,))],
        compiler_params=pltpu.CompilerParams(dimension_semantics=("arbitrary", "arbitrary"),
                                             vmem_limit_bytes=VMEM_LIMIT),
    )(q, k, v, o, do, lse, *riders)


def tail_tile(step0, tile0, x, tgt, ma, mb, gpa, gpb, ya, y, ur, k2, uv,
              mod, wpa, wpb, wout, gn_g, gn_b, r_k, post_g, post_b, bd):
    gate = mod[2:3]
    inv = 1.0 / HEAD
    yc = y - head_sum(y, bd) * inv
    rs = lax.rsqrt(head_sum(yc * yc, bd) * inv + GN_EPS)
    yn = yc * rs
    yb = yn * gn_g + gn_b + head_sum(ur * k2 * r_k, bd) * uv
    sga, sgb = sigmoid(gpa), sigmoid(gpb)
    sila, silb = gpa * sga, gpb * sgb
    ga, gb = ya * sila, yb * silb
    pa, pb = mm(ga, wpa), mm(gb, wpb)
    sa, sb = sigmoid(ma), sigmoid(mb)
    merged = sa * pa + sb * pb
    sub = mm(merged, wout)
    z = ALPHA * x + (1.0 + gate) * sub
    zhat, rstd = layer_norm_stats(z)
    err = zhat * post_g + post_b - tgt
    loss = 0.5 * jnp.sum(rowmean(err * err), axis=0, keepdims=True) + jnp.zeros((1, LANE), F32)
    dout = err * (1.0 / D_MODEL)
    dpost_g = colsum(dout * zhat)
    dpost_b = colsum(dout)
    dz = layer_norm_bwd(dout * post_g, zhat, rstd)
    dgate = colsum(dz * sub)
    dsub = dz * (1.0 + gate)
    dwout = mm_tn(merged, dsub)
    dmerged = mm_nt(dsub, wout)
    dpa, dpb = dmerged * sa, dmerged * sb
    dma = dmerged * pa * sa * (1.0 - sa)
    dmb = dmerged * pb * sb * (1.0 - sb)
    dwpa = mm_tn(ga, dpa)
    dwpb = mm_tn(gb, dpb)
    dga = mm_nt(dpa, wpa)
    dgb = mm_nt(dpb, wpb)
    dya = dga * sila
    dgpa = dga * ya * (sga * (1.0 + gpa * (1.0 - sga)))
    dyb = dgb * silb
    dgpb = dgb * yb * (sgb * (1.0 + gpb * (1.0 - sgb)))
    dgn_g = colsum(dyb * yn)
    dgn_b = colsum(dyb)
    dyn = dyb * gn_g
    dy = rs * (dyn - head_sum(dyn, bd) * inv - yn * head_sum(dyn * yn, bd) * inv)
    return (dz, dma, dmb, dgpa, dgpb, dya, dy, dyb,
            loss, dwout, dwpa, dwpb, dgn_g, dgn_b, dpost_g, dpost_b, dgate)


def mla_prep_bwd_tile(step0, tile0, q_c, kv_c, cos, sin, dq, dk, dv, gq, gkv, wq, wqr, wkn, wv):
    qn, qh, rq = rms_norm_fwd(q_c, gq)
    kvn, kvh, rkv = rms_norm_fwd(kv_c, gkv)
    dqc = dq * tile_lanes(cos, HEADS)
    dqs = dq * tile_lanes(sin, HEADS)
    dqn = mm_nt(dqc, wq) + mm_nt(dqs, wqr)
    dkvn = mm_nt(dk, wkn) + mm_nt(dv, wv)
    dkpe = dk[:, 0:LANE]
    for h in range(1, HEADS):
        dkpe = dkpe + dk[:, h * LANE:(h + 1) * LANE]
    dkr = dkpe * (cos * key_rope_mask(cos.shape))
    dkrr = dkpe * sin

    def rms_bwd(dyv, xh, r, g):
        dyg = dyv * g
        return r * (dyg - xh * rowmean(dyg * xh)), colsum(dyv * xh)

    dq_c, dgq = rms_bwd(dqn, qh, rq, gq)
    dkv_c, dgkv = rms_bwd(dkvn, kvh, rkv, gkv)
    return (dq_c, dkv_c, dkr, dkrr,
            mm_tn(qn, dqc), mm_tn(qn, dqs), mm_tn(kvn, dk), mm_tn(kvn, dv), dgq, dgkv)


def rwkv_prep_bwd_tile(step0, tile0, r0, k0, v0, l0, drt, dat, dbt, dkt, dvv, dlw, dyb, hr, hk, hv, hl,
                       mu_r, mu_k, mu_v, mu_l, w0, a0, k_k, k_a, w_dec, w_iclr, tril, same, bd, r_k,
                       cr, ck, cv, cl_):
    f = rwkv_prep_core(tile0, r0, k0, v0, l0, hr, hk, hv, hl, mu_r, mu_k, mu_v, mu_l, w0, a0, k_k, k_a,
                       w_dec, w_iclr, tril, same, bd)
    ur, uk, uv, ul, kk, k2, a_ic, sg, th = (f[n] for n in ("ur", "uk", "uv", "ul", "kk", "k2", "a_ic", "sg", "th"))
    lc, lw = f["lc"], f["lw"]
    e_neg = jnp.exp(-lc)
    dur = drt * jnp.exp(lc)
    da = dat * jnp.exp(lc - lw)
    db = dbt * e_neg
    dk2 = dkt * e_neg
    s = head_sum(ur * k2 * r_k, bd)
    duv = dvv + dyb * s
    ds = head_sum(dyb * uv, bd)
    dur = dur + ds * k2 * r_k
    dk2 = dk2 + ds * ur * r_k
    dr_k = colsum(ds * ur * k2)
    dkk = db * a_ic - da
    da_ic = db * kk + dk2 * uk * k_a
    duk = dk2 * (1.0 + (a_ic - 1.0) * k_a)
    dk_a = colsum(dk2 * uk * (a_ic - 1.0))
    dkkraw = jnp.where(f["nrm_raw"] > 1e-12, (dkk - kk * head_sum(dkk * kk, bd)) / f["nrm"], dkk * 1e12)
    duk = duk + dkkraw * k_k
    dk_k = colsum(dkkraw * uk)
    dai = da_ic * a_ic * (1.0 - a_ic)
    dd = dlw * (-DECAY_SCALE) * sg * (1.0 - sg)
    dul = mm_nt(dai, w_iclr) + mm_nt(dd, w_dec) * (1.0 - th * th)

    def unshift(du, x, prev, mu, carry_row):
        nxt = shift_rows_up(du, carry_row)
        return du * (1.0 - mu) + nxt * mu, colsum(du * (prev - x)), du[0:1, :]

    dr0, dmu_r, ncr = unshift(dur, r0, f["pr"], mu_r, cr)
    dk0, dmu_k, nck = unshift(duk, k0, f["pk"], mu_k, ck)
    dv0, dmu_v, ncv = unshift(duv, v0, f["pv"], mu_v, cv)
    dl0, dmu_l, ncl = unshift(dul, l0, f["pl"], mu_l, cl_)
    return (dr0, dk0, dv0, dl0,
            dmu_r, dmu_k, dmu_v, dmu_l, colsum(dd), colsum(dai), dk_k, dk_a, dr_k, mm_tn(th, dd), mm_tn(ul, dai),
            ncr, nck, ncv, ncl)


def in_backward(x, dz, pieces, mod, w_in_pt, unrot):
    n_rows = x.shape[0]
    ts = ROW_TILE
    n_p = len(pieces)
    shard_cols = IN_WIDTH // N_DEV

    def body(*refs):
        x_ref, dz_ref = refs[:2]
        p_refs = refs[2:2 + n_p]
        mod_ref, w_ref, unrot_ref = refs[2 + n_p:5 + n_p]
        dx_ref, ht_ref, blocks_ref, dshift_ref, dscale_ref = refs[5 + n_p:]
        step0 = pl.program_id(0) == 0
        dma, dmb, dr0, dk0, dv0, dgpa, dgpb, dq_c, dkv_c, dkr, dkrr, dl0 = (r[...] for r in p_refs)
        dproj = jnp.concatenate([dma, dmb, dr0, dk0, dv0, dgpa, dgpb, dq_c, dkv_c, dkr, dkrr, dl0], axis=1)
        dh = mm(dproj, w_ref[...])
        xhat, rstd = layer_norm_stats(x_ref[...])
        scale1 = 1.0 + mod_ref[1:2, :]
        dx_ref[...] = layer_norm_bwd(dh * scale1, xhat, rstd) + ALPHA * dz_ref[...]
        ht_ref[...] = jnp.transpose(xhat * scale1 + mod_ref[0:1, :]).astype(BF16)
        dkrope = (dkr.astype(F32) + mm(dkrr, unrot_ref[...]))[:, NOPE:QK_DIM]
        natural = jnp.concatenate(
            [dq_c.astype(F32), dkv_c.astype(F32), dkrope]
            + [p.astype(F32) for p in (dgpa, dr0, dk0, dv0, dl0, dgpb, dma, dmb)], axis=1)
        for j in range(N_DEV):
            blocks_ref[j] = natural[:, j * shard_cols:(j + 1) * shard_cols].astype(BF16)
        for ref, val in ((dshift_ref, colsum(dh)), (dscale_ref, colsum(dh * xhat))):
            @pl.when(step0)
            def _(ref=ref, val=val):
                ref[...] = val

            @pl.when(jnp.logical_not(step0))
            def _(ref=ref, val=val):
                ref[...] += val

    row = lambda w: pl.BlockSpec((ts, w), lambda i: (i, 0))
    const = pl.BlockSpec(memory_space=pltpu.VMEM)
    vec = pl.BlockSpec((1, D_MODEL), lambda i: (0, 0))
    return pl.pallas_call(
        body, name="in_backward", grid=(n_rows // ts,),
        in_specs=[row(D_MODEL), row(D_MODEL)] + [row(p.shape[1]) for p in pieces] + [const] * 3,
        out_specs=[row(D_MODEL), pl.BlockSpec((D_MODEL, ts), lambda i: (0, i)),
                   pl.BlockSpec((N_DEV, ts, shard_cols), lambda i: (0, i, 0)), vec, vec],
        out_shape=[jax.ShapeDtypeStruct((n_rows, D_MODEL), F32), jax.ShapeDtypeStruct((D_MODEL, n_rows), BF16),
                   jax.ShapeDtypeStruct((N_DEV, n_rows, shard_cols), BF16),
                   jax.ShapeDtypeStruct((1, D_MODEL), F32), jax.ShapeDtypeStruct((1, D_MODEL), F32)],
        compiler_params=pltpu.CompilerParams(dimension_semantics=("arbitrary",), vmem_limit_bytes=VMEM_LIMIT),
    )(x, dz, *pieces, mod, w_in_pt, unrot)


def in_weight_grad_exchange(h_t, dp_blocks, others, small, order):
    n = len(others)
    n_rows = h_t.shape[1]
    ts = 4 * ROW_TILE
    n_i = n_rows // ts
    shard_cols = dp_blocks.shape[2]
    n_chips = N_DEV // 2
    last = N_DEV - 1

    def body(order_ref, h_ref, dp_ref, *rest):
        g_refs, s_ref = rest[:n], rest[n]
        rwin_ref, rg_refs, rs_ref = rest[n + 1], rest[n + 2:2 * n + 2], rest[2 * n + 2]
        (acc, sendbuf, sib_buf, sib_send, sib_recv, win_send, win_recv,
         o_send, o_recv, local_sems) = rest[2 * n + 3:]
        b, i = pl.program_id(0), pl.program_id(1)
        me = my_position()
        mi = flat_index(me)
        sibling = (me[0], me[1], 1 - me[2])

        def other_copies(k, src_index, dst_index):
            peer = flip(me, k)
            out = [pltpu.make_async_remote_copy(
                src_ref=g_refs[a].at[src_index], dst_ref=rg_refs[a].at[dst_index],
                send_sem=o_send.at[(n + 1) * (k - 1) + a], recv_sem=o_recv.at[(n + 1) * (k - 1) + a],
                device_id=peer, device_id_type=MESH_IDS) for a in range(n)]
            out.append(pltpu.make_async_remote_copy(
                src_ref=s_ref, dst_ref=rs_ref.at[dst_index],
                send_sem=o_send.at[(n + 1) * (k - 1) + n], recv_sem=o_recv.at[(n + 1) * (k - 1) + n],
                device_id=peer, device_id_type=MESH_IDS))
            return out

        def local_copies():
            out = [pltpu.make_async_copy(g_refs[a].at[mi], rg_refs[a].at[mi], local_sems.at[a]) for a in range(n)]
            out.append(pltpu.make_async_copy(s_ref, rs_ref.at[mi], local_sems.at[n]))
            return out

        def to_sibling(t):
            return pltpu.make_async_remote_copy(
                src_ref=sendbuf.at[t], dst_ref=sib_buf.at[t], send_sem=sib_send.at[t], recv_sem=sib_recv.at[t],
                device_id=sibling, device_id_type=MESH_IDS)

        def to_owner(t):
            flip_x = (t < 2) * 1
            flip_y = 1 - (t & 1)
            owner = (me[0] ^ flip_x, me[1] ^ flip_y, me[2])
            return pltpu.make_async_remote_copy(
                src_ref=sendbuf.at[n_chips + t], dst_ref=rwin_ref.at[t], send_sem=win_send.at[t],
                recv_sem=win_recv.at[t], device_id=owner, device_id_type=MESH_IDS)

        own_block = pltpu.make_async_copy(sendbuf.at[last], rwin_ref.at[n_chips - 1], local_sems.at[n + 1])

        @pl.when(jnp.logical_and(b == 0, i == 0))
        def _():
            for cp in local_copies():
                cp.start()
            for k in range(1, N_DEV):
                for cp in other_copies(k, flat_index(flip(me, k)), mi):
                    cp.start()

        contrib = jnp.dot(h_ref[...], dp_ref[...], preferred_element_type=F32)

        @pl.when(i == 0)
        def _():
            acc[...] = contrib

        @pl.when(i > 0)
        def _():
            acc[...] += contrib

        slot = order_ref[N_DEV + b]
        t = slot & (n_chips - 1)

        @pl.when(jnp.logical_and(i == n_i - 1, slot < n_chips))
        def _():
            sendbuf[slot] = acc[...].astype(BF16)
            to_sibling(t).start()

        @pl.when(jnp.logical_and(i == n_i - 1, slot >= n_chips))
        def _():
            to_sibling(t).wait_recv()
            sendbuf[slot] = (acc[...] + sib_buf[t].astype(F32)).astype(BF16)

            @pl.when(slot < last)
            def _():
                to_owner(t).start()

            @pl.when(slot == last)
            def _():
                own_block.start()

        @pl.when(jnp.logical_and(b == last, i == n_i - 1))
        def _():
            for t in range(n_chips - 1):
                to_owner(t).wait_recv()
            for k in range(1, N_DEV):
                pi = flat_index(flip(me, k))
                for cp in other_copies(k, pi, pi):
                    cp.wait_recv()
            for t in range(n_chips):
                to_sibling(t).wait_send()
            for t in range(n_chips - 1):
                to_owner(t).wait_send()
            for k in range(1, N_DEV):
                for cp in other_copies(k, flat_index(flip(me, k)), mi):
                    cp.wait_send()
            for cp in local_copies():
                cp.wait()
            own_block.wait()

    hbm = pl.BlockSpec(memory_space=pl.ANY)
    n_sem = 7 * (n + 1)
    grid_spec = pltpu.PrefetchScalarGridSpec(
        num_scalar_prefetch=1, grid=(N_DEV, n_i),
        in_specs=[pl.BlockSpec((D_MODEL, ts), lambda b, i, order: (0, i)),
                  pl.BlockSpec((None, ts, shard_cols), lambda b, i, order: (order[b], i, 0))] + [hbm] * (n + 1),
        out_specs=[hbm] * (n + 2),
        scratch_shapes=[pltpu.VMEM((D_MODEL, shard_cols), F32), pltpu.VMEM((N_DEV, D_MODEL, shard_cols), BF16),
                        pltpu.VMEM((n_chips, D_MODEL, shard_cols), BF16),
                        pltpu.SemaphoreType.DMA((n_chips,)), pltpu.SemaphoreType.DMA((n_chips,)),
                        pltpu.SemaphoreType.DMA((n_chips - 1,)), pltpu.SemaphoreType.DMA((n_chips - 1,)),
                        pltpu.SemaphoreType.DMA((n_sem,)), pltpu.SemaphoreType.DMA((n_sem,)),
                        pltpu.SemaphoreType.DMA((n + 2,))])
    return pl.pallas_call(
        body, name="in_weight_grad_exchange", grid_spec=grid_spec,
        out_shape=[jax.ShapeDtypeStruct((n_chips, D_MODEL, shard_cols), BF16)]
        + [jax.ShapeDtypeStruct(o.shape, o.dtype) for o in others]
        + [jax.ShapeDtypeStruct((N_DEV,) + small.shape, small.dtype)],
        compiler_params=pltpu.CompilerParams(dimension_semantics=("arbitrary", "arbitrary"),
                                             vmem_limit_bytes=VMEM_LIMIT),
    )(order, h_t, dp_blocks, *others, small)


def ada_weight_grad(c_all, dmod_cols):
    def body(c_ref, d_ref, o_ref):
        cv = c_ref[...]
        o_ref[...] = hdot_tn(cv * sigmoid(cv), d_ref[...])

    return pl.pallas_call(
        body, name="ada_weight_grad",
        out_shape=jax.ShapeDtypeStruct((c_all.shape[1], dmod_cols.shape[1]), F32),
    )(c_all, dmod_cols)


def adamw_update(g, w, m, v):
    nm = ADAM_B1 * m + (1.0 - ADAM_B1) * g
    nv = ADAM_B2 * v + (1.0 - ADAM_B2) * (g * g)
    m_hat = nm / (1.0 - ADAM_B1 ** ADAM_STEP)
    v_hat = nv / (1.0 - ADAM_B2 ** ADAM_STEP)
    return -ADAM_LR * (m_hat / (jnp.sqrt(v_hat) + ADAM_EPS) + ADAM_WD * w), nm, nv


def adamw(parts, w, m, v, name):
    k, rows, cols = parts.shape

    def body(p_ref, w_hbm, m_hbm, v_hbm, g_ref, d_ref, nm_ref, nv_ref, w_buf, m_buf, v_buf, sems):
        loads = [pltpu.make_async_copy(src, dst, sems.at[i])
                 for i, (src, dst) in enumerate(((w_hbm, w_buf), (m_hbm, m_buf), (v_hbm, v_buf)))]
        for cp in loads:
            cp.start()
        g = p_ref[0].astype(F32)
        for i in range(1, k):
            g = g + p_ref[i].astype(F32)
        g_ref[0] = g
        for cp in loads:
            cp.wait()
        d_ref[0], nm_ref[0], nv_ref[0] = adamw_update(g, w_buf[0], m_buf[0], v_buf[0])

    hbm = pl.BlockSpec(memory_space=pl.ANY)
    whole = pl.BlockSpec(memory_space=pltpu.VMEM)
    return pl.pallas_call(
        body, name=name,
        in_specs=[whole, hbm, hbm, hbm], out_specs=[whole] * 4,
        out_shape=[jax.ShapeDtypeStruct((1, rows, cols), F32)] * 4,
        scratch_shapes=[pltpu.VMEM((1, rows, cols), F32)] * 3 + [pltpu.SemaphoreType.DMA((3,))],
        compiler_params=pltpu.CompilerParams(vmem_limit_bytes=VMEM_LIMIT),
    )(parts, w, m, v)


def adamw_small(parts, ws, ms, vs):
    k = parts.shape[0]
    n = len(ws)
    sizes = [w.shape[1] for w in ws]

    def body(p_ref, *refs):
        ins, outs = refs[:3 * n], refs[3 * n:]
        g_all = p_ref[0]
        for i in range(1, k):
            g_all = g_all + p_ref[i]
        off = 0
        for a, size in enumerate(sizes):
            g = g_all[:, off:off + size]
            off += size
            d, nm, nv = adamw_update(g, ins[a][...], ins[n + a][...], ins[2 * n + a][...])
            for kind, val in enumerate((g, d, nm, nv)):
                outs[kind * n + a][...] = val

    return pl.pallas_call(
        body, name="adamw_small",
        out_shape=[jax.ShapeDtypeStruct((1, size), F32) for _ in range(4) for size in sizes],
    )(parts, *ws, *ms, *vs)


def columns_from_shards(g, rows, cols):
    return g.reshape(N_DEV, rows, cols).transpose(1, 0, 2).reshape(rows, N_DEV * cols)


def permute_w_in_t(wt):
    z = lambda n: jnp.zeros((n, D_MODEL), wt.dtype)
    krope = wt[N_KROPE:N_KROPE + ROPE]
    krope_rot = jnp.concatenate([-krope[ROPE // 2:], krope[:ROPE // 2]], axis=0)
    rw = N_RWKV
    return jnp.concatenate([
        wt[N_MA:N_MA + 1024], wt[N_MB:N_MB + 1024],
        wt[rw:rw + 512], wt[rw + 512:rw + 1024], wt[rw + 1024:rw + 1536],
        wt[N_GPA:N_GPA + 512], wt[N_GPB:N_GPB + 512],
        wt[N_QC:N_QC + 256], wt[N_KVC:N_KVC + 128],
        z(NOPE), krope, z(LANE - QK_DIM), z(NOPE), krope_rot, z(LANE - QK_DIM),
        wt[rw + 1536:rw + 1664]], axis=0)


def pad_heads_q(w_uq):
    w = w_uq.reshape(Q_RANK, HEADS, QK_DIM)
    zpad = jnp.zeros((Q_RANK, HEADS, LANE - QK_DIM), w.dtype)
    wq = jnp.concatenate([w, zpad], axis=2).reshape(Q_RANK, HEADS * LANE)
    pe = w[:, :, NOPE:]
    rot = jnp.concatenate([-pe[:, :, ROPE // 2:], pe[:, :, :ROPE // 2]], axis=2)
    wqr = jnp.concatenate([jnp.zeros((Q_RANK, HEADS, NOPE), w.dtype), rot, zpad], axis=2).reshape(Q_RANK, HEADS * LANE)
    return wq, wqr


def unpad_heads_q_grad(dwq, dwqr):
    a = dwq.reshape(Q_RANK, HEADS, LANE)
    r = dwqr.reshape(Q_RANK, HEADS, LANE)[:, :, NOPE:QK_DIM]
    pe = a[:, :, NOPE:QK_DIM] + jnp.concatenate([r[:, :, ROPE // 2:], -r[:, :, :ROPE // 2]], axis=2)
    return jnp.concatenate([a[:, :, :NOPE], pe], axis=2).reshape(Q_RANK, HEADS * QK_DIM)


def pad_heads_kv(w_ukv):
    w = w_ukv.reshape(KV_RANK, HEADS, 2 * HEAD)
    z = jnp.zeros((KV_RANK, HEADS, HEAD), w.dtype)
    wkn = jnp.concatenate([w[:, :, :NOPE], z], axis=2).reshape(KV_RANK, HEADS * LANE)
    val = w[:, :, NOPE:]
    odd = (jnp.arange(HEADS) % 2 == 1)[None, :, None]
    wv = jnp.concatenate([jnp.where(odd, 0, val), jnp.where(odd, val, 0)], axis=2).reshape(KV_RANK, HEADS * LANE)
    return wkn, wv


def unpad_heads_kv_grad(dwkn, dwv):
    a = dwkn.reshape(KV_RANK, HEADS, LANE)[:, :, :NOPE]
    b = dwv.reshape(KV_RANK, HEADS, LANE)
    odd = (jnp.arange(HEADS) % 2 == 1)[None, :, None]
    val = jnp.where(odd, b[:, :, HEAD:], b[:, :, :HEAD])
    return jnp.concatenate([a, val], axis=2).reshape(KV_RANK, HEADS * 2 * HEAD)


def kernel(x, c, positions, w_ada, b_ada, w_in, q_norm_g, w_uq, kv_norm_g, w_ukv, mu_rwkv, w0, w_decay_up, a0, w_iclr_up, k_k, k_a, r_k, gn_g, gn_b, w_proj_a, w_proj_b, w_out, post_g, post_b, loss_target, m_w_ada, m_b_ada, m_w_in, m_q_norm_g, m_w_uq, m_kv_norm_g, m_w_ukv, m_mu_rwkv, m_w0, m_w_decay_up, m_a0, m_w_iclr_up, m_k_k, m_k_a, m_r_k, m_gn_g, m_gn_b, m_w_proj_a, m_w_proj_b, m_w_out, m_post_g, m_post_b, v_w_ada, v_b_ada, v_w_in, v_q_norm_g, v_w_uq, v_kv_norm_g, v_w_ukv, v_mu_rwkv, v_w0, v_w_decay_up, v_a0, v_w_iclr_up, v_k_k, v_k_a, v_r_k, v_gn_g, v_gn_b, v_w_proj_a, v_w_proj_b, v_w_out, v_post_g, v_post_b):
    weights = dict(w_ada=w_ada, b_ada=b_ada, w_in=w_in, q_norm_g=q_norm_g, w_uq=w_uq, kv_norm_g=kv_norm_g,
                   w_ukv=w_ukv, mu_rwkv=mu_rwkv, w0=w0, w_decay_up=w_decay_up, a0=a0, w_iclr_up=w_iclr_up,
                   k_k=k_k, k_a=k_a, r_k=r_k, gn_g=gn_g, gn_b=gn_b, w_proj_a=w_proj_a, w_proj_b=w_proj_b,
                   w_out=w_out, post_g=post_g, post_b=post_b)
    mom1 = dict(w_ada=m_w_ada, b_ada=m_b_ada, w_in=m_w_in, q_norm_g=m_q_norm_g, w_uq=m_w_uq, kv_norm_g=m_kv_norm_g,
                w_ukv=m_w_ukv, mu_rwkv=m_mu_rwkv, w0=m_w0, w_decay_up=m_w_decay_up, a0=m_a0, w_iclr_up=m_w_iclr_up,
                k_k=m_k_k, k_a=m_k_a, r_k=m_r_k, gn_g=m_gn_g, gn_b=m_gn_b, w_proj_a=m_w_proj_a, w_proj_b=m_w_proj_b,
                w_out=m_w_out, post_g=m_post_g, post_b=m_post_b)
    mom2 = dict(w_ada=v_w_ada, b_ada=v_b_ada, w_in=v_w_in, q_norm_g=v_q_norm_g, w_uq=v_w_uq, kv_norm_g=v_kv_norm_g,
                w_ukv=v_w_ukv, mu_rwkv=v_mu_rwkv, w0=v_w0, w_decay_up=v_w_decay_up, a0=v_a0, w_iclr_up=v_w_iclr_up,
                k_k=v_k_k, k_a=v_k_a, r_k=v_r_k, gn_g=v_gn_g, gn_b=v_gn_b, w_proj_a=v_w_proj_a, w_proj_b=v_w_proj_b,
                w_out=v_w_out, post_g=v_post_g, post_b=v_post_b)
    names = list(weights)
    n_rows = x.shape[1]
    me = 4 * lax.axis_index("x") + 2 * lax.axis_index("y") + lax.axis_index("c")
    xr = x[0]
    tgt = loss_target[0]
    row = lambda a: a.reshape(1, -1)

    w_in_all, c_all = gather_shards([w_in[0].T.astype(BF16), c])
    c_all = c_all.reshape(N_DEV, D_MODEL)
    w_in_pt = permute_w_in_t(w_in_all.reshape(IN_WIDTH, D_MODEL))

    mod_all = ada_modulation(c_all, w_ada[0], b_ada.reshape(N_DEV, -1))
    mod = lax.dynamic_index_in_dim(mod_all, me, axis=1, keepdims=False).reshape(3, D_MODEL)

    proj, *gathered = fwd_in_gather(xr, mod, w_in_pt, [weights[n][0].astype(BF16) for n, _, _ in SHARDED[1:]])
    pcol = lambda off_, w: (proj, w, off_ // w)
    full = {}
    for (n, r, cdim), part in zip(SHARDED[1:], gathered):
        full[n] = part.reshape(N_DEV * r, cdim) if n == "w_out" else columns_from_shards(part, r, cdim)
    wq, wqr = pad_heads_q(full["w_uq"])
    wkn, wv = pad_heads_kv(full["w_ukv"])
    zl = jnp.zeros((LORA, WIDTH), BF16)
    w_dec = jnp.concatenate([full["w_decay_up"], zl], axis=0)
    w_iclr = jnp.concatenate([zl, full["w_iclr_up"]], axis=0)
    wpa, wpb, wout = full["w_proj_a"], full["w_proj_b"], full["w_out"]

    inv_freq = ROPE_THETA ** (-jnp.arange(0, ROPE, 2, dtype=F32) / ROPE)
    ang = positions[0].astype(F32)[:, None] * inv_freq
    ones_n, zeros_n, zeros_p = jnp.ones((n_rows, NOPE), F32), jnp.zeros((n_rows, NOPE), F32), jnp.zeros((n_rows, LANE - QK_DIM), F32)
    cos_t = jnp.concatenate([ones_n, jnp.cos(ang), jnp.cos(ang), zeros_p], axis=1)
    sin_t = jnp.concatenate([zeros_n, jnp.sin(ang), jnp.sin(ang), zeros_p], axis=1)

    gq, gkv = q_norm_g, kv_norm_g
    mla_consts = [gq, gkv, wq, wqr, wkn, wv]
    q, k, v = row_call(
        "mla_prep", mla_prep_tile, n_rows,
        [pcol(P_QC, 256), pcol(P_KVC, 128), pcol(P_KR, 128), pcol(P_KRR, 128), (cos_t, LANE, 0), (sin_t, LANE, 0)],
        mla_consts, [(HEADS * LANE, BF16)] * 3, tile_rows=PREP_TILE)
    ya, lse = attention_forward(q, k, v)

    def chunk_sum_matrices(n):
        t_idx = jnp.arange(n)
        same_chunk = (t_idx[:, None] // CHUNK) == (t_idx[None, :] // CHUNK)
        return (same_chunk & (t_idx[:, None] >= t_idx[None, :])).astype(F32), same_chunk.astype(F32)

    l_idx = jnp.arange(LANE)
    bd = ((l_idx[:, None] // HEAD) == (l_idx[None, :] // HEAD)).astype(F32)
    mu = mu_rwkv
    mu_r, mu_k, mu_v, mu_l = mu[:, 0:512], mu[:, 512:1024], mu[:, 1024:1536], mu[:, 1536:1664]
    rk_row = row(r_k)
    rwkv_consts = lambda n: [mu_r, mu_k, mu_v, mu_l, w0, a0, k_k, k_a, w_dec, w_iclr, *chunk_sum_matrices(n), bd]
    rwkv_rows = [pcol(P_R, 512), pcol(P_K, 512), pcol(P_V, 512), pcol(P_LORA, 128)]
    rt, at, bt, kt, clf, uv, ur, k2 = row_call(
        "rwkv_prep", rwkv_prep_tile, n_rows, rwkv_rows, rwkv_consts(ROW_TILE), [(WIDTH, F32)] * 8, halo_in=rwkv_rows)
    y, m0s, state_maps, out_maps, *wkv_saved = wkv_forward(at, bt, kt, rt, uv, clf)

    tail = row_call(
        "tail", tail_tile, n_rows,
        [(xr, D_MODEL, 0), (tgt, D_MODEL, 0), pcol(P_MA, 1024), pcol(P_MB, 1024), pcol(P_GPA, 512), pcol(P_GPB, 512),
         (ya, WIDTH, 0), (y, WIDTH, 0), (ur, WIDTH, 0), (k2, WIDTH, 0), (uv, WIDTH, 0)],
        [mod, wpa, wpb, wout, gn_g, gn_b, rk_row, post_g, post_b, bd],
        [(D_MODEL, F32), (1024, BF16), (1024, BF16), (512, BF16), (512, BF16), (WIDTH, F32), (WIDTH, F32), (WIDTH, F32)],
        acc_out=[((1, LANE), F32), ((D_MODEL, D_MODEL), F32), ((WIDTH, D_MODEL), F32), ((WIDTH, D_MODEL), F32),
                 ((1, WIDTH), F32), ((1, WIDTH), F32), ((1, D_MODEL), F32), ((1, D_MODEL), F32), ((1, D_MODEL), F32)])
    (dz, dma, dmb, dgpa, dgpb, dya, dy, dyb,
     loss_row, g_wout, g_wpa, g_wpb, g_gn_g, g_gn_b, g_post_g, g_post_b, dgate) = tail

    def owner_blocks(g, n):
        r, cdim = next((r, cdim) for name, r, cdim in SHARDED if name == n)
        return (g.reshape(N_DEV, r, cdim) if n == "w_out" else g.reshape(r, N_DEV, cdim).transpose(1, 0, 2)).astype(BF16)

    early = ("w_proj_a", "w_proj_b", "w_out")
    dq, dk, dv, *got_early = attention_backward(
        q, k, v, ya, dya, lse, [owner_blocks(g, n) for g, n in zip((g_wpa, g_wpb, g_wout), early)])
    dq_c, dkv_c, dkr, dkrr, g_wq, g_wqr, g_wkn, g_wv, g_gq, g_gkv = row_call(
        "mla_prep_bwd", mla_prep_bwd_tile, n_rows,
        [pcol(P_QC, 256), pcol(P_KVC, 128), (cos_t, LANE, 0), (sin_t, LANE, 0),
         (dq, HEADS * LANE, 0), (dk, HEADS * LANE, 0), (dv, HEADS * LANE, 0)],
        mla_consts, [(256, BF16), (128, BF16), (128, BF16), (128, BF16)],
        acc_out=[((Q_RANK, HEADS * LANE), F32)] * 2 + [((KV_RANK, HEADS * LANE), F32)] * 2
        + [((1, Q_RANK), F32), ((1, KV_RANK), F32)], tile_rows=PREP_TILE)

    dat, dbt, dkt, drt, dvv, dlw = wkv_backward(at, bt, kt, rt, uv, clf, m0s, state_maps, out_maps, wkv_saved, dy)
    (dr0, dk0, dv0, dl0, g_mu_r, g_mu_k, g_mu_v, g_mu_l, g_w0, g_a0, g_k_k, g_k_a, g_r_k, g_wdec, g_wiclr) = row_call(
        "rwkv_prep_bwd", rwkv_prep_bwd_tile, n_rows,
        rwkv_rows + [(drt, WIDTH, 0), (dat, WIDTH, 0), (dbt, WIDTH, 0), (dkt, WIDTH, 0), (dvv, WIDTH, 0),
                     (dlw, WIDTH, 0), (dyb, WIDTH, 0)],
        rwkv_consts(PREP_TILE) + [rk_row], [(512, BF16), (512, BF16), (512, BF16), (128, BF16)],
        acc_out=[((1, 512), F32)] * 3 + [((1, 128), F32)] + [((1, 512), F32)] * 5 + [((LANE, WIDTH), F32)] * 2,
        halo_in=rwkv_rows, carry=[512, 512, 512, 128], reverse=True, tile_rows=PREP_TILE)

    li = jnp.arange(LANE)
    src, dst = li[:, None], li[None, :]
    half = ROPE // 2
    unrot = (jnp.where((dst >= NOPE) & (dst < NOPE + half) & (src == dst + half), 1.0, 0.0)
             - jnp.where((dst >= NOPE + half) & (dst < QK_DIM) & (src == dst - half), 1.0, 0.0)).astype(BF16)
    dx, h_t, dproj_blocks, dshift, dscale = in_backward(
        xr, dz, [dma, dmb, dr0, dk0, dv0, dgpa, dgpb, dq_c, dkv_c, dkr, dkrr, dl0], mod, w_in_pt, unrot)

    late = ("w_uq", "w_ukv", "w_decay_up", "w_iclr_up")
    late_grads = (unpad_heads_q_grad(g_wq, g_wqr), unpad_heads_kv_grad(g_wkn, g_wv), g_wdec[:LORA], g_wiclr[LORA:])
    blocks = [owner_blocks(g, n) for g, n in zip(late_grads, late)]
    dmod = jnp.concatenate([dshift, dscale, dgate], axis=1)
    small = jnp.concatenate([dmod, g_gq, g_gkv, g_mu_r, g_mu_k, g_mu_v, g_mu_l, g_w0, g_a0, g_k_k, g_k_a, g_r_k,
                             g_gn_g, g_gn_b, g_post_g, g_post_b, loss_row], axis=1)
    my_x, my_y, my_c = lax.axis_index("x"), lax.axis_index("y"), lax.axis_index("c")
    chip_order = [4 * (my_x ^ fx) + 2 * (my_y ^ fy) for fx, fy in ((1, 1), (1, 0), (0, 1), (0, 0))]
    owners = [chip_order[s % 4] + (my_c if s >= 4 else 1 - my_c) for s in WGRAD_SLOTS]
    order = jnp.stack(owners + [jnp.int32(s) for s in WGRAD_SLOTS]).astype(jnp.int32)
    got_w_in, *got_late, got_small = in_weight_grad_exchange(h_t, dproj_blocks, blocks, small, order)
    got = {"w_in": got_w_in, **dict(zip(late, got_late)), **dict(zip(early, got_early))}
    loss = jnp.sum(got_small[:, 0, SMALL_ELEMS])

    ada_cols = w_ada.shape[2]
    dmod_all = got_small[:, 0, :3 * D_MODEL]
    g_ada = ada_weight_grad(c_all, lax.dynamic_slice_in_dim(dmod_all, me * ada_cols, ada_cols, axis=1))

    outs = [dict() for _ in range(4)]
    res = adamw(g_ada[None], w_ada, m_w_ada, v_w_ada, "adamw_w_ada")
    for kind in range(4):
        outs[kind]["w_ada"] = res[kind]
    for n, _, _ in SHARDED:
        res = adamw(got[n], weights[n], mom1[n], mom2[n], "adamw_" + n)
        for kind in range(4):
            outs[kind][n] = res[kind]
    rows_of = lambda tree: [tree[n].reshape(1, -1) for n, _ in SMALL]
    res = adamw_small(got_small, rows_of(weights), rows_of(mom1), rows_of(mom2))
    for kind in range(4):
        for a, (n, _) in enumerate(SMALL):
            outs[kind][n] = res[kind * len(SMALL) + a].reshape(weights[n].shape)
    return (loss, dx[None], *[outs[0][n] for n in names], *[outs[1][n] for n in names],
            *[outs[2][n] for n in names], *[outs[3][n] for n in names])
```

```python
import functools
import math

import jax
import jax.numpy as jnp
from jax import lax
from jax.experimental import pallas as pl
from jax.experimental.pallas import tpu as pltpu

F32 = jnp.float32
BF16 = jnp.bfloat16
HIGHEST = lax.Precision.HIGHEST
MESH_IDS = pl.DeviceIdType.MESH

N_DEV = 8
D_MODEL = 1024
LN_EPS = 1e-5
RMS_EPS = 1e-6
GN_EPS = 64e-5
HEADS = 8
Q_RANK = 256
KV_RANK = 128
ROPE = 32
NOPE = 64
QK_DIM = NOPE + ROPE
WIDTH = 512
HEAD = 64
LORA = 64
CHUNK = 64
DEPTH = 1
ALPHA = (2.0 * DEPTH) ** 0.25
ROPE_THETA = 10000.0
ATTN_SCALE = QK_DIM ** -0.5
DECAY_SCALE = math.exp(-0.5)

ADAM_LR = 0.001
ADAM_B1 = 0.9
ADAM_B2 = 0.999
ADAM_EPS = 1e-08
ADAM_WD = 0.01
ADAM_STEP = 10

LANE = 128
PAIR = 2 * HEAD
ROW_TILE = 256
PREP_TILE = 512
HALO_ROWS = 16
ATTN_FWD_TILES = (512, 1024)
ATTN_BWD_TILES = (512, 512)
LOG2_E = math.log2(math.e)
Q_PRESCALE = ATTN_SCALE * LOG2_E
WKV_CHUNKS_PER_STEP = 16
WKV_CHAIN_GROUPS = 2
WGRAD_SLOTS = (0, 1, 4, 2, 5, 6, 3, 7)
VMEM_LIMIT = 56 * 1024 * 1024

P_MA, P_MB, P_R, P_K, P_V, P_GPA, P_GPB, P_QC, P_KVC, P_KR, P_KRR, P_LORA = (
    0, 1024, 2048, 2560, 3072, 3584, 4096, 4608, 4864, 4992, 5120, 5248)
P_WIDTH = 5376

N_QC, N_KVC, N_KROPE, N_GPA, N_RWKV, N_GPB, N_MA, N_MB = 0, 256, 384, 416, 928, 2592, 3104, 4128
IN_WIDTH = 5152

SHARDED = (("w_in", 1024, 644), ("w_uq", 256, 96), ("w_ukv", 128, 128), ("w_decay_up", 64, 64),
           ("w_iclr_up", 64, 64), ("w_proj_a", 512, 128), ("w_proj_b", 512, 128), ("w_out", 128, 1024))
SMALL = (("b_ada", 3072), ("q_norm_g", 256), ("kv_norm_g", 128), ("mu_rwkv", 1664), ("w0", 512), ("a0", 512),
         ("k_k", 512), ("k_a", 512), ("r_k", 512), ("gn_g", 512), ("gn_b", 512), ("post_g", 1024), ("post_b", 1024))
SMALL_ELEMS = sum(n for _, n in SMALL)


def mm(a, b):
    return jnp.dot(a.astype(BF16), b.astype(BF16), preferred_element_type=F32)


def mm_nt(a, b):
    return lax.dot_general(a.astype(BF16), b.astype(BF16), (((1,), (1,)), ((), ())), preferred_element_type=F32)


def mm_tn(a, b):
    return lax.dot_general(a.astype(BF16), b.astype(BF16), (((0,), (0,)), ((), ())), preferred_element_type=F32)


def hdot(a, b):
    return jnp.dot(a, b, precision=HIGHEST, preferred_element_type=F32)


def hdot_tn(a, b):
    return lax.dot_general(a, b, (((0,), (0,)), ((), ())), precision=HIGHEST, preferred_element_type=F32)


def sigmoid(x):
    return 1.0 / (1.0 + jnp.exp(-x))


def colsum(x):
    return jnp.sum(x, axis=0, keepdims=True)


def rowmean(x):
    return jnp.mean(x, axis=-1, keepdims=True)


def layer_norm_stats(x):
    xc = x - rowmean(x)
    rstd = lax.rsqrt(rowmean(xc * xc) + LN_EPS)
    return xc * rstd, rstd


def layer_norm_bwd(dy, xhat, rstd):
    return rstd * (dy - rowmean(dy) - xhat * rowmean(dy * xhat))


def bf16_pieces(x, n):
    pieces = []
    for _ in range(n):
        p = x.astype(BF16)
        pieces.append(p)
        x = x - p.astype(F32)
    return pieces


def ones_dot(ones, x, n_pieces):
    ones = ones.astype(BF16)
    return sum(jnp.dot(ones, p, preferred_element_type=F32) for p in bf16_pieces(x, n_pieces))


def ones_dot_nt(ones, x, n_pieces):
    ones = ones.astype(BF16)
    return sum(lax.dot_general(ones, p, (((1,), (1,)), ((), ())), preferred_element_type=F32)
               for p in bf16_pieces(x, n_pieces))


def head_sum(x, bd):
    return jnp.concatenate([mm(x[:, p * LANE:(p + 1) * LANE], bd) for p in range(x.shape[1] // LANE)], axis=1)


def tile_lanes(t, n):
    return jnp.concatenate([t] * n, axis=1)


def row_iota(shape):
    return lax.broadcasted_iota(jnp.int32, shape, 0)


def lane_iota(shape):
    return lax.broadcasted_iota(jnp.int32, shape, 1)


def shift_rows_down(x, row0):
    rolled = pltpu.roll(x, 1, axis=0)
    return jnp.where(row_iota(x.shape) == 0, row0, rolled)


def shift_rows_up(x, row_last):
    rolled = pltpu.roll(x, x.shape[0] - 1, axis=0)
    return jnp.where(row_iota(x.shape) == x.shape[0] - 1, row_last, rolled)


def row_call(name, fn, n_rows, row_in, const_in, row_out, acc_out=(), halo_in=(), carry=(), reverse=False,
             tile_rows=ROW_TILE):
    ts = tile_rows
    n_tiles = n_rows // ts
    n_in = len(row_in) + len(halo_in) + len(const_in)
    n_ro, n_ao = len(row_out), len(acc_out)

    def tile_of(g):
        return (n_tiles - 1 - g) if reverse else g

    def body(*refs):
        ins = refs[:n_in]
        ro = refs[n_in:n_in + n_ro]
        ao = refs[n_in + n_ro:n_in + n_ro + n_ao]
        cr = refs[n_in + n_ro + n_ao:]
        g = pl.program_id(0)
        step0 = g == 0
        tile0 = tile_of(g) == 0
        for r in cr:
            @pl.when(step0)
            def _(r=r):
                r[...] = jnp.zeros_like(r)
        n_tiled = len(row_in) + len(halo_in)
        vals = [r[...].astype(F32) for r in ins[:n_tiled]] + [r[...] for r in ins[n_tiled:]]
        outs = fn(step0, tile0, *vals, *[c[0:1, :] for c in cr])
        for r, v in zip(ro, outs[:n_ro]):
            r[...] = v.astype(r.dtype)
        for r, v in zip(ao, outs[n_ro:n_ro + n_ao]):
            @pl.when(step0)
            def _(r=r, v=v):
                r[...] = v.astype(r.dtype)

            @pl.when(jnp.logical_not(step0))
            def _(r=r, v=v):
                r[...] += v.astype(r.dtype)
        for r, v in zip(cr, outs[n_ro + n_ao:]):
            r[0:1, :] = v

    in_specs = [pl.BlockSpec((ts, w), functools.partial(lambda g, cb: (tile_of(g), cb), cb=cb)) for _, w, cb in row_in]
    in_specs += [pl.BlockSpec((HALO_ROWS, w), functools.partial(
        lambda g, cb: (jnp.maximum(tile_of(g) * (ts // HALO_ROWS) - 1, 0), cb), cb=cb)) for _, w, cb in halo_in]
    in_specs += [pl.BlockSpec(memory_space=pltpu.VMEM) for _ in const_in]
    out_specs = [pl.BlockSpec((ts, w), lambda g: (tile_of(g), 0)) for w, _ in row_out]
    out_specs += [pl.BlockSpec(s, lambda g: (0, 0)) for s, _ in acc_out]
    out_shape = [jax.ShapeDtypeStruct((n_rows, w), d) for w, d in row_out]
    out_shape += [jax.ShapeDtypeStruct(s, d) for s, d in acc_out]
    return pl.pallas_call(
        body, name=name, grid=(n_tiles,), in_specs=in_specs, out_specs=out_specs, out_shape=out_shape,
        scratch_shapes=[pltpu.VMEM((8, w), F32) for w in carry],
        compiler_params=pltpu.CompilerParams(dimension_semantics=("arbitrary",), vmem_limit_bytes=VMEM_LIMIT),
    )(*[a for a, _, _ in row_in], *[a for a, _, _ in halo_in], *const_in)


def my_position():
    return lax.axis_index("x"), lax.axis_index("y"), lax.axis_index("c")


def flip(pos, k):
    x, y, c = pos
    dx, dy, dc = (k >> 2) & 1, (k >> 1) & 1, k & 1
    return (1 - x if dx else x, 1 - y if dy else y, 1 - c if dc else c)


def flat_index(pos):
    return 4 * pos[0] + 2 * pos[1] + pos[2]


def shard_allgather(s_refs, out_refs, send_sems, recv_sems, local_sems):
    n = len(s_refs)
    me = my_position()
    mi = flat_index(me)

    def copies(k, slot):
        return [pltpu.make_async_remote_copy(
            src_ref=s_refs[a], dst_ref=out_refs[a].at[slot], send_sem=send_sems.at[7 * a + k - 1],
            recv_sem=recv_sems.at[7 * a + k - 1], device_id=flip(me, k), device_id_type=MESH_IDS)
            for a in range(n)]

    local = [pltpu.make_async_copy(s_refs[a], out_refs[a].at[mi], local_sems.at[a]) for a in range(n)]

    def start():
        for cp in local:
            cp.start()
        for k in range(1, N_DEV):
            for cp in copies(k, mi):
                cp.start()

    def wait():
        for k in range(1, N_DEV):
            for cp in copies(k, flat_index(flip(me, k))):
                cp.wait_recv()
        for k in range(1, N_DEV):
            for cp in copies(k, mi):
                cp.wait_send()
        for cp in local:
            cp.wait()

    return start, wait


def gather_shards(shards):
    n = len(shards)

    def body(*refs):
        x_refs, out_refs = refs[:n], refs[n:2 * n]
        send_sems, recv_sems, local_sems = refs[2 * n:]
        x, y, c = my_position()
        me, sibling = (x, y, c), (x, y, 1 - c)
        chips = [(1 - x, y), (x, 1 - y), (1 - x, 1 - y)]
        relay_from = (x ^ (1 - c), y ^ c, c)
        relay_to = (x ^ c, y ^ (1 - c), c)

        def copy(a, k, block, to, from_input=False):
            slot = out_refs[a].at[flat_index(block)]
            return pltpu.make_async_remote_copy(
                src_ref=x_refs[a] if from_input else slot, dst_ref=slot,
                send_sem=send_sems.at[7 * a + k], recv_sem=recv_sems.at[7 * a + k],
                device_id=to, device_id_type=MESH_IDS)

        mine = [pltpu.make_async_copy(x_refs[a], out_refs[a].at[flat_index(me)], local_sems.at[a]) for a in range(n)]
        for cp in mine:
            cp.start()
        first = []
        for a in range(n):
            first.append(copy(a, 0, me, sibling, from_input=True))
            first += [copy(a, 1 + j, me, (*chip, c), from_input=True) for j, chip in enumerate(chips[:2])]
        for cp in first:
            cp.start()
        relayed = [copy(a, 3, relay_from, relay_to) for a in range(n)]
        passed = []
        for j, chip in enumerate(chips):
            for a in range(n):
                copy(a, 1 + j, (*chip, c), me).wait_recv()
                cp = copy(a, 4 + j, (*chip, c), sibling)
                cp.start()
                passed.append(cp)
                if j < 2:
                    @pl.when(c == j)
                    def _(a=a):
                        relayed[a].start()
        for a in range(n):
            copy(a, 0, sibling, me).wait_recv()
            for j, chip in enumerate(chips):
                copy(a, 4 + j, (*chip, 1 - c), me).wait_recv()
        for cp in first + passed + relayed:
            cp.wait_send()
        for cp in mine:
            cp.wait()

    return pl.pallas_call(
        body, name="gather_shards",
        out_shape=[jax.ShapeDtypeStruct((N_DEV,) + s.shape, s.dtype) for s in shards],
        in_specs=[pl.BlockSpec(memory_space=pl.ANY)] * n, out_specs=[pl.BlockSpec(memory_space=pl.ANY)] * n,
        scratch_shapes=[pltpu.SemaphoreType.DMA((7 * n,)), pltpu.SemaphoreType.DMA((7 * n,)),
                        pltpu.SemaphoreType.DMA((n,))],
    )(*shards)


def ada_modulation(c_all, w_ada_loc, b_ada_blocks):
    cols = w_ada_loc.shape[1]

    def body(c_ref, w_ref, b_ref, out_ref, send_sems, recv_sems):
        me = my_position()
        mi = flat_index(me)
        cv = c_ref[...]
        res = hdot(cv * sigmoid(cv), w_ref[...]) + b_ref[pl.ds(mi, 1), :]
        out_ref[mi] = res
        sends = []
        for k in range(1, N_DEV):
            cp = pltpu.make_async_remote_copy(
                src_ref=out_ref.at[mi], dst_ref=out_ref.at[mi], send_sem=send_sems.at[k - 1],
                recv_sem=recv_sems.at[k - 1], device_id=flip(me, k), device_id_type=MESH_IDS)
            cp.start()
            sends.append(cp)
        for k in range(1, N_DEV):
            pi = flat_index(flip(me, k))
            pltpu.make_async_remote_copy(
                src_ref=out_ref.at[pi], dst_ref=out_ref.at[pi], send_sem=send_sems.at[k - 1],
                recv_sem=recv_sems.at[k - 1], device_id=flip(me, k), device_id_type=MESH_IDS).wait_recv()
        for cp in sends:
            cp.wait_send()

    return pl.pallas_call(
        body, name="ada_modulation",
        out_shape=jax.ShapeDtypeStruct((N_DEV, N_DEV, cols), F32),
        in_specs=[pl.BlockSpec(memory_space=pltpu.VMEM)] * 3, out_specs=pl.BlockSpec(memory_space=pltpu.VMEM),
        scratch_shapes=[pltpu.SemaphoreType.DMA((7,)), pltpu.SemaphoreType.DMA((7,))],
    )(c_all, w_ada_loc, b_ada_blocks)


def fwd_in_tile(step0, tile0, x, mod, w_in_pt):
    xhat, _ = layer_norm_stats(x)
    h = xhat * (1.0 + mod[1:2]) + mod[0:1]
    return (mm_nt(h, w_in_pt),)


def fwd_in_gather(x, mod, w_in_pt, shards):
    n = len(shards)
    n_rows = x.shape[0]
    ts = PREP_TILE
    n_tiles = n_rows // ts

    def body(x_ref, mod_ref, w_ref, *rest):
        s_refs = rest[:n]
        proj_ref, out_refs = rest[n], rest[n + 1:2 * n + 1]
        start_riders, wait_riders = shard_allgather(s_refs, out_refs, *rest[2 * n + 1:])
        g = pl.program_id(0)

        @pl.when(g == 0)
        def _():
            start_riders()

        proj_ref[...] = fwd_in_tile(None, None, x_ref[...], mod_ref[...], w_ref[...])[0].astype(BF16)

        @pl.when(g == n_tiles - 1)
        def _():
            wait_riders()

    hbm = pl.BlockSpec(memory_space=pl.ANY)
    const = pl.BlockSpec(memory_space=pltpu.VMEM)
    return pl.pallas_call(
        body, name="fwd_in_gather", grid=(n_tiles,),
        in_specs=[pl.BlockSpec((ts, D_MODEL), lambda g: (g, 0)), const, const] + [hbm] * n,
        out_specs=[pl.BlockSpec((ts, P_WIDTH), lambda g: (g, 0))] + [hbm] * n,
        out_shape=[jax.ShapeDtypeStruct((n_rows, P_WIDTH), BF16)]
        + [jax.ShapeDtypeStruct((N_DEV,) + s.shape, s.dtype) for s in shards],
        scratch_shapes=[pltpu.SemaphoreType.DMA((7 * n,)), pltpu.SemaphoreType.DMA((7 * n,)),
                        pltpu.SemaphoreType.DMA((n,))],
        compiler_params=pltpu.CompilerParams(dimension_semantics=("arbitrary",), vmem_limit_bytes=VMEM_LIMIT),
    )(x, mod, w_in_pt, *shards)


def rms_norm_fwd(x, g):
    r = lax.rsqrt(rowmean(x * x) + RMS_EPS)
    xh = x * r
    return xh * g, xh, r


def key_rope_mask(shape):
    return (lane_iota(shape) >= NOPE).astype(F32)


def mla_prep_tile(step0, tile0, q_c, kv_c, kr, krr, cos, sin, gq, gkv, wq, wqr, wkn, wv):
    qn, _, _ = rms_norm_fwd(q_c, gq)
    kvn, _, _ = rms_norm_fwd(kv_c, gkv)
    q = (mm(qn, wq) * tile_lanes(cos, HEADS) + mm(qn, wqr) * tile_lanes(sin, HEADS)) * Q_PRESCALE
    kpe = kr * (cos * key_rope_mask(cos.shape)) + krr * sin
    k = mm(kvn, wkn) + tile_lanes(kpe, HEADS)
    v = mm(kvn, wv)
    return q, k, v


def rwkv_prep_core(tile0, r0, k0, v0, l0, hr, hk, hv, hl, mu_r, mu_k, mu_v, mu_l, w0, a0, k_k, k_a,
                   w_dec, w_iclr, tril, same, bd):
    def shifted(x, halo, mu):
        row0 = jnp.where(tile0, 0.0, halo[HALO_ROWS - 1:HALO_ROWS, :])
        prev = shift_rows_down(x, row0)
        return x + (prev - x) * mu, prev

    ur, pr = shifted(r0, hr, mu_r)
    uk, pk = shifted(k0, hk, mu_k)
    uv, pv = shifted(v0, hv, mu_v)
    ul, plo = shifted(l0, hl, mu_l)
    th = jnp.tanh(ul)
    sg = sigmoid(w0 + mm(th, w_dec))
    lw = -DECAY_SCALE * sg
    a_ic = sigmoid(a0 + mm(ul, w_iclr))
    kkraw = uk * k_k
    nrm_raw = jnp.sqrt(head_sum(kkraw * kkraw, bd))
    nrm = jnp.maximum(nrm_raw, 1e-12)
    kk = kkraw / nrm
    k2 = uk * (1.0 + (a_ic - 1.0) * k_a)
    lc = ones_dot(tril, lw, 3)
    lcl = ones_dot(same, lw, 3)
    return dict(ur=ur, uk=uk, uv=uv, ul=ul, pr=pr, pk=pk, pv=pv, pl=plo, th=th, sg=sg, lw=lw, a_ic=a_ic,
                kkraw=kkraw, nrm_raw=nrm_raw, nrm=nrm, kk=kk, k2=k2, lc=lc, lcl=lcl)


def rwkv_prep_tile(step0, tile0, r0, k0, v0, l0, hr, hk, hv, hl, *consts):
    f = rwkv_prep_core(tile0, r0, k0, v0, l0, hr, hk, hv, hl, *consts)
    lc, lw = f["lc"], f["lw"]
    e_neg = jnp.exp(-lc)
    rt = f["ur"] * jnp.exp(lc)
    at = -f["kk"] * jnp.exp(lc - lw)
    bt = f["kk"] * f["a_ic"] * e_neg
    kt = f["k2"] * e_neg
    return rt, at, bt, kt, jnp.exp(f["lcl"]), f["uv"], f["ur"], f["k2"]


def wkv_masks():
    lane = lane_iota((1, PAIR))
    m_lo = (lane < HEAD).astype(F32)
    r2 = row_iota((PAIR, PAIR))
    c2 = lane_iota((PAIR, PAIR))
    bd = ((r2 < HEAD) == (c2 < HEAD)).astype(F32)
    eye2 = (r2 == c2).astype(F32)
    eye = (row_iota((CHUNK, CHUNK)) == lane_iota((CHUNK, CHUNK))).astype(F32)
    t_idx = row_iota((4 * CHUNK, PAIR)) % CHUNK
    s_idx = lane_iota((4 * CHUNK, PAIR)) % CHUNK
    keep = s_idx < t_idx + (row_iota((4 * CHUNK, PAIR)) >= 2 * CHUNK).astype(jnp.int32)
    return (m_lo, 1.0 - m_lo), keep, eye, bd, eye2


def rows(*parts):
    return jnp.concatenate(parts, axis=0)


def lanes(*parts):
    return jnp.concatenate(parts, axis=1)


def head_rows(x, ms):
    return rows(x * ms[0], x * ms[1])


def wkv_score_stack(at, rt, ms):
    return rows(head_rows(at, ms), head_rows(rt, ms))


def wkv_chunks_pre(chunks, masks, between_stages=lambda: None):
    ms, keep, eye, bd, eye2 = masks
    n = len(chunks)
    at, bt, kt, rt, v, cl = (list(t) for t in zip(*chunks))
    scores = [jnp.where(keep, mm_nt(wkv_score_stack(a, r, ms), rows(b, k)), 0.0)
              for a, r, b, k in zip(at, rt, bt, kt)]
    between_stages()
    q = CHUNK
    aab = [s[h * q:(h + 1) * q, :q] for s in scores for h in range(2)]
    tinv = [eye + a for a in aab]
    power = [mm(a, a) for a in aab]
    between_stages()
    for _ in range(5):
        both = [mm(rows(t, p), p) for t, p in zip(tinv, power)]
        tinv = [t + x[:q] for t, x in zip(tinv, both)]
        power = [x[q:] for x in both]
        between_stages()
    pair = lambda c, row0, col0: lanes(scores[c][row0:row0 + q, col0:col0 + q],
                                       scores[c][row0 + q:row0 + 2 * q, col0:col0 + q])
    tinv_p = [lanes(tinv[2 * c], tinv[2 * c + 1]) for c in range(n)]
    aak_p = [pair(c, 0, q) for c in range(n)]
    prb_p = [pair(c, 2 * q, 0) for c in range(n)]
    prk_p = [pair(c, 2 * q, q) for c in range(n)]
    v_rows = [head_rows(x, ms) for x in v]
    wy = [mm(rows(a, p), x) for a, p, x in zip(aak_p, prk_p, v_rows)]
    between_stages()
    w = [x[:q] for x in wy]
    yh2 = [x[q:] for x in wy]
    aw = [mm(t, lanes(head_rows(a, ms), head_rows(w_, ms))) for t, a, w_ in zip(tinv_p, at, w)]
    between_stages()
    ah = [x[:, :PAIR] for x in aw]
    wh = [x[:, PAIR:] for x in aw]
    ry = [mm(p, lanes(head_rows(a, ms), head_rows(w_, ms))) for p, a, w_ in zip(prb_p, ah, wh)]
    between_stages()
    rh = [r + x[:, :PAIR] for r, x in zip(rt, ry)]
    yh = [x[:, PAIR:] + y for x, y in zip(ry, yh2)]
    bc = [b * c_ for b, c_ in zip(bt, cl)]
    kc = [k * c_ for k, c_ in zip(kt, cl)]
    gh = [mm_tn(b, lanes(a, w_)) for b, a, w_ in zip(bc, ah, wh)]
    g = [eye2 * c_ + bd * x[:, :PAIR] for c_, x in zip(cl, gh)]
    h = [bd * (x[:, PAIR:] + mm_tn(k, v_)) for x, k, v_ in zip(gh, kc, v)]
    as_bf16 = lambda xs: [x.astype(BF16) for x in xs]
    saved = (as_bf16(tinv_p), as_bf16(aak_p), as_bf16(prb_p), as_bf16(prk_p), as_bf16(ah), wh)
    return g, h, rh, yh, saved


def wkv_chunks_grad(chunks, saved, m0, dy, dm1, masks, between_stages=lambda: None):
    ms, keep, eye, bd, eye2 = masks
    n = len(chunks)
    q = CHUNK
    at, bt, kt, rt, v, cl = (list(t) for t in zip(*chunks))
    tinv_p, aak_p, prb_p, prk_p, ah, wh = (list(t) for t in zip(*saved))
    head_stack = lambda p: rows(p[:, :q], p[:, q:])
    bc = [b * c_ for b, c_ in zip(bt, cl)]
    kc = [k * c_ for k, c_ in zip(kt, cl)]
    u = [mm(a, m) + w for a, m, w in zip(ah, m0, wh)]
    between_stages()
    dm1 = [d * bd for d in dm1]
    from_state = [mm(rows(b, k), d) for b, k, d in zip(bc, kc, dm1)]
    between_stages()
    dy_rows = [head_rows(d, ms) for d in dy]
    from_out = [mm_tn(lanes(head_stack(pb), head_stack(pk)), d) for pb, pk, d in zip(prb_p, prk_p, dy_rows)]
    between_stages()
    du = [a[:q] + b[:q] for a, b in zip(from_state, from_out)]
    dv = [a[q:] + b[q:] for a, b in zip(from_state, from_out)]
    dz = [mm_tn(head_stack(t), head_rows(d, ms)) for t, d in zip(tinv_p, du)]
    between_stages()
    dz_rows = [head_rows(d, ms) for d in dz]
    dv = [a + mm_tn(head_stack(k), d) for a, k, d in zip(dv, aak_p, dz_rows)]
    between_stages()
    by_m0 = [mm_nt(rows(d, z), m) for d, z, m in zip(dy, dz, m0)]
    between_stages()
    uv = [rows(x, y) for x, y in zip(u, v)]
    by_dm1 = [mm_nt(x, d) for x, d in zip(uv, dm1)]
    between_stages()
    udm = [x[:q] for x in by_dm1]
    vdm = [x[q:] for x in by_dm1]
    dscores = [jnp.where(keep, mm_nt(rows(z, d), x), 0.0) for z, d, x in zip(dz_rows, dy_rows, uv)]
    between_stages()
    to_ar = [mm(d, rows(b, k)) for d, b, k in zip(dscores, bt, kt)]
    to_bk = [mm_tn(d, wkv_score_stack(a, r, ms)) for d, a, r in zip(dscores, at, rt)]
    ones = jnp.ones((8, PAIR), F32)
    upper = (lane_iota((CHUNK, CHUNK)) >= row_iota((CHUNK, CHUNK))).astype(F32)
    out = []
    for c in range(n):
        e = to_ar[c]
        dat_c = by_m0[c][q:] + e[:q] * ms[0] + e[q:2 * q] * ms[1]
        drt_c = by_m0[c][:q] + e[2 * q:3 * q] * ms[0] + e[3 * q:] * ms[1]
        dbt_c = udm[c] * cl[c] + to_bk[c][:q]
        dkt_c = vdm[c] * cl[c] + to_bk[c][q:]
        dlcl = ones_dot_nt(ones, dm1[c] * m0[c], 3)[0:1, :] * cl[c] + colsum(bc[c] * udm[c] + kc[c] * vdm[c])
        g = drt_c * rt[c] - dbt_c * bt[c] - dkt_c * kt[c] + dat_c * at[c]
        dlw = ones_dot(upper, g, 3) - dat_c * at[c] + dlcl
        out.append((dat_c, dbt_c, dkt_c, drt_c, dv[c], dlw))
    return out


def wkv_forward(at, bt, kt, rt, v, clf):
    n_rows = at.shape[0]
    cps = WKV_CHUNKS_PER_STEP
    rb = cps * CHUNK
    n_steps = n_rows // rb

    def body(a_ref, b_ref, k_ref, r_ref, v_ref, c_ref, y_ref, m0_ref, g_ref, rh_ref, *rest):
        saved_refs, m_scr = rest[:6], rest[6]

        @pl.when(pl.program_id(1) == 0)
        def _():
            m_scr[...] = jnp.zeros_like(m_scr)

        masks = wkv_masks()
        chunks = []
        for cc in range(cps):
            sl = slice(cc * CHUNK, (cc + 1) * CHUNK)
            chunks.append((a_ref[sl, :], b_ref[sl, :], k_ref[sl, :], r_ref[sl, :], v_ref[sl, :],
                           c_ref[cc * CHUNK:cc * CHUNK + 1, :]))
        state = [m_scr[...]]
        pending = []

        def chain_step():
            if not pending:
                return
            cc, g, h, rh, yh = pending.pop(0)
            sl = slice(cc * CHUNK, (cc + 1) * CHUNK)
            m = state[0]
            m0_ref[0, cc] = m
            g_ref[0, cc] = g
            rh_ref[sl, :] = rh
            y_ref[sl, :] = hdot(rh, m) + yh
            state[0] = hdot(g, m) + h

        def prepare(first, last, between_stages):
            gs, hs, rhs, yhs, saved = wkv_chunks_pre(chunks[first:last], masks, between_stages)
            for ref, per_chunk in zip(saved_refs, saved):
                for cc, val in enumerate(per_chunk, start=first):
                    ref[cc * CHUNK:(cc + 1) * CHUNK, :] = val
            pending.extend(zip(range(first, last), gs, hs, rhs, yhs))

        group = cps // WKV_CHAIN_GROUPS
        for first in range(0, cps, group):
            prepare(first, first + group, chain_step)
        while pending:
            chain_step()
        m_scr[...] = state[0]

    blk = pl.BlockSpec((rb, PAIR), lambda p, s: (s, p))
    state_blk = pl.BlockSpec((1, cps, PAIR, PAIR), lambda p, s: (p, s, 0, 0))
    state_shape = jax.ShapeDtypeStruct((WIDTH // PAIR, n_rows // CHUNK, PAIR, PAIR), F32)
    rows_f32 = jax.ShapeDtypeStruct((n_rows, WIDTH), F32)
    rows_bf16 = jax.ShapeDtypeStruct((n_rows, WIDTH), BF16)
    return pl.pallas_call(
        body, name="wkv_forward", grid=(WIDTH // PAIR, n_steps),
        in_specs=[blk] * 6,
        out_specs=[blk, state_blk, state_blk, blk] + [blk] * 6,
        out_shape=[rows_f32, state_shape, state_shape, rows_f32] + [rows_bf16] * 5 + [rows_f32],
        scratch_shapes=[pltpu.VMEM((PAIR, PAIR), F32)],
        compiler_params=pltpu.CompilerParams(dimension_semantics=("arbitrary", "arbitrary"),
                                             vmem_limit_bytes=VMEM_LIMIT),
    )(at, bt, kt, rt, v, clf)


def wkv_backward(at, bt, kt, rt, v, clf, m0s, gs, rh, saved, dy):
    n_rows = at.shape[0]
    cps = WKV_CHUNKS_PER_STEP
    rb = cps * CHUNK
    n_steps = n_rows // rb

    def body(a_ref, b_ref, k_ref, r_ref, v_ref, c_ref, m0_ref, g_ref, rh_ref, *rest):
        saved_refs, dy_ref = rest[:6], rest[6]
        da_ref, db_ref, dk_ref, dr_ref, dv_ref, dlw_ref, dm_scr = rest[7:]

        @pl.when(pl.program_id(1) == 0)
        def _():
            dm_scr[...] = jnp.zeros_like(dm_scr)

        masks = wkv_masks()
        bd = masks[3]
        state = [dm_scr[...]]
        dm1 = [None] * cps
        todo = list(reversed(range(cps)))

        def chain_step():
            if not todo:
                return
            cc = todo.pop(0)
            sl = slice(cc * CHUNK, (cc + 1) * CHUNK)
            dm1[cc] = state[0]
            state[0] = bd * (hdot_tn(g_ref[0, cc], state[0]) + hdot_tn(rh_ref[sl, :], dy_ref[sl, :]))

        def gradients(first, last, between_stages):
            chunks, kept, m0, dys = [], [], [], []
            for cc in range(first, last):
                sl = slice(cc * CHUNK, (cc + 1) * CHUNK)
                chunks.append((a_ref[sl, :], b_ref[sl, :], k_ref[sl, :], r_ref[sl, :], v_ref[sl, :],
                               c_ref[cc * CHUNK:cc * CHUNK + 1, :]))
                kept.append(tuple(ref[sl, :] for ref in saved_refs))
                m0.append(m0_ref[0, cc])
                dys.append(dy_ref[sl, :])
            grads = wkv_chunks_grad(chunks, kept, m0, dys, dm1[first:last], masks, between_stages)
            for cc, (dat, dbt, dkt, drt, dv, dlw) in enumerate(grads, start=first):
                sl = slice(cc * CHUNK, (cc + 1) * CHUNK)
                da_ref[sl, :] = dat
                db_ref[sl, :] = dbt
                dk_ref[sl, :] = dkt
                dr_ref[sl, :] = drt
                dv_ref[sl, :] = dv
                dlw_ref[sl, :] = dlw

        group = cps // WKV_CHAIN_GROUPS
        for first in reversed(range(0, cps, group)):
            while todo and todo[0] >= first:
                chain_step()
            gradients(first, first + group, chain_step)
        dm_scr[...] = state[0]

    blk = pl.BlockSpec((rb, PAIR), lambda p, s: (n_steps - 1 - s, p))
    state_blk = pl.BlockSpec((1, cps, PAIR, PAIR), lambda p, s: (p, n_steps - 1 - s, 0, 0))
    return pl.pallas_call(
        body, name="wkv_backward", grid=(WIDTH // PAIR, n_steps),
        in_specs=[blk] * 6 + [state_blk, state_blk, blk] + [blk] * 6 + [blk],
        out_specs=[blk] * 6,
        out_shape=[jax.ShapeDtypeStruct((n_rows, WIDTH), F32)] * 6,
        scratch_shapes=[pltpu.VMEM((PAIR, PAIR), F32)],
        compiler_params=pltpu.CompilerParams(dimension_semantics=("arbitrary", "arbitrary"),
                                             vmem_limit_bytes=VMEM_LIMIT),
    )(at, bt, kt, rt, v, clf, m0s, gs, rh, *saved, dy)


def visible(q_row0, k_row0, shape):
    qc = (q_row0 + row_iota(shape)) // CHUNK
    kc = (k_row0 + lane_iota(shape)) // CHUNK
    return kc <= qc


def attention_forward(q, k, v, riders):
    n_rows = q.shape[0]
    tq, tk = ATTN_FWD_TILES
    n_q = n_rows // tq
    n_r = len(riders)
    assert tk % tq == 0

    def body(q_ref, k_ref, v_ref, *rest):
        s_refs = rest[:n_r]
        o_ref, lse_ref = rest[n_r:n_r + 2]
        start_riders, wait_riders = shard_allgather(s_refs, rest[n_r + 2:2 * n_r + 2], *rest[2 * n_r + 2:])
        i = pl.program_id(1)

        @pl.when(jnp.logical_and(pl.program_id(0) == 0, i == 0))
        def _():
            start_riders()

        @pl.when(jnp.logical_and(pl.program_id(0) == HEADS // 2 - 1, i == n_q - 1))
        def _():
            wait_riders()

        lane = lane_iota((tq, LANE))
        heads = [slice(0, LANE), slice(LANE, 2 * LANE)]
        qs = [q_ref[:, cols] for cols in heads]

        def step(j, carry, size, masked):
            rows = pl.ds(pl.multiple_of(j * size, size), size)
            ss = [mm_nt(qh, k_ref[rows, cols]) for qh, cols in zip(qs, heads)]
            if masked:
                vis = visible(i * tq, j * size, ss[0].shape)
                ss = [jnp.where(vis, s, -jnp.inf) for s in ss]
            ps, stats = [], []
            for s, (m, l, _) in zip(ss, carry):
                m_new = jnp.maximum(m, jnp.max(s, axis=-1, keepdims=True))
                p = jnp.exp2(s - m_new)
                alpha = jnp.exp2(m - m_new)
                ps.append(p)
                stats.append((m_new, alpha, alpha * l + jnp.sum(p, axis=-1, keepdims=True)))
            pvs = [mm(p, v_ref[rows, cols]) for p, cols in zip(ps, heads)]
            return tuple((m_new, l, alpha * acc + pv)
                         for (m_new, alpha, l), (_, _, acc), pv in zip(stats, carry, pvs))

        carry = tuple((jnp.full((tq, 1), -jnp.inf, F32), jnp.zeros((tq, 1), F32), jnp.zeros((tq, LANE), F32))
                      for _ in heads)
        n_full = (i * tq) // tk
        carry = lax.fori_loop(0, n_full, functools.partial(step, size=tk, masked=False), carry)
        (m0, l0, acc0), (m1, l1, acc1) = step(n_full, carry, size=tk, masked=True)
        o_ref[...] = acc0 / l0 + acc1 / l1
        lse_ref[...] = jnp.where(lane >= HEAD, m1 + jnp.log2(l1), m0 + jnp.log2(l0))

    return pl.pallas_call(
        body, name="attention_forward", grid=(HEADS // 2, n_q),
        in_specs=[pl.BlockSpec((tq, 2 * LANE), lambda p, i: (i, p)),
                  pl.BlockSpec((n_rows, 2 * LANE), lambda p, i: (0, p)),
                  pl.BlockSpec((n_rows, 2 * LANE), lambda p, i: (0, p))] + [pl.BlockSpec(memory_space=pl.ANY)] * n_r,
        out_specs=[pl.BlockSpec((tq, LANE), lambda p, i: (i, p))] * 2 + [pl.BlockSpec(memory_space=pl.ANY)] * n_r,
        out_shape=[jax.ShapeDtypeStruct((n_rows, WIDTH), F32)] * 2
        + [jax.ShapeDtypeStruct((N_DEV,) + r.shape, r.dtype) for r in riders],
        scratch_shapes=[pltpu.SemaphoreType.DMA((7 * n_r,)), pltpu.SemaphoreType.DMA((7 * n_r,)),
                        pltpu.SemaphoreType.DMA((n_r,))],
        compiler_params=pltpu.CompilerParams(dimension_semantics=("arbitrary", "arbitrary"),
                                             vmem_limit_bytes=VMEM_LIMIT),
    )(q, k, v, *riders)


def block_exchange(g_refs, rg_refs, send_sems, recv_sems, local_sems):
    n = len(g_refs)
    me = my_position()
    mi = flat_index(me)

    def copies(k, src_index, dst_index):
        return [pltpu.make_async_remote_copy(
            src_ref=g_refs[a].at[src_index], dst_ref=rg_refs[a].at[dst_index],
            send_sem=send_sems.at[7 * a + k - 1], recv_sem=recv_sems.at[7 * a + k - 1],
            device_id=flip(me, k), device_id_type=MESH_IDS) for a in range(n)]

    local = [pltpu.make_async_copy(g_refs[a].at[mi], rg_refs[a].at[mi], local_sems.at[a]) for a in range(n)]

    def start():
        for cp in local:
            cp.start()
        for k in range(1, N_DEV):
            for cp in copies(k, flat_index(flip(me, k)), mi):
                cp.start()

    def wait():
        for k in range(1, N_DEV):
            pi = flat_index(flip(me, k))
            for cp in copies(k, pi, pi):
                cp.wait_recv()
        for k in range(1, N_DEV):
            for cp in copies(k, flat_index(flip(me, k)), mi):
                cp.wait_send()
        for cp in local:
            cp.wait()

    return start, wait


def attention_backward(q, k, v, o, do, lse, riders):
    n_rows = q.shape[0]
    tq, tk = ATTN_BWD_TILES
    n_q = n_rows // tq
    n_k = n_rows // tk
    n_masked = max(1, tk // tq)
    n_r = len(riders)

    def body(q_ref, k_ref, v_ref, o_ref, do_ref, lse_ref, *rest):
        g_refs = rest[:n_r]
        dq_ref, dk_ref, dv_ref = rest[n_r:n_r + 3]
        rg_refs = rest[n_r + 3:2 * n_r + 3]
        start_riders, wait_riders = block_exchange(g_refs, rg_refs, *rest[2 * n_r + 3:])
        j = pl.program_id(1)

        @pl.when(jnp.logical_and(pl.program_id(0) == 0, j == 0))
        def _():
            start_riders()

        @pl.when(j == 0)
        def _():
            dq_ref[...] = jnp.zeros_like(dq_ref)

        lane = lane_iota((tq, LANE))
        heads = [slice(0, LANE), slice(LANE, 2 * LANE)]
        ks = [k_ref[:, cols] for cols in heads]
        vs = [v_ref[:, cols] for cols in heads]
        head_lanes = [(lane < HEAD).astype(F32), (lane >= HEAD).astype(F32)]

        def step(i, carry, masked):
            rows = pl.ds(pl.multiple_of(i * tq, tq), tq)
            qs = [q_ref[rows, cols] for cols in heads]
            dout = do_ref[rows, :]
            dout_o = dout * o_ref[rows, :]
            lse_t = lse_ref[rows, :]
            ss = [mm_nt(qh, kh) for qh, kh in zip(qs, ks)]
            dps = [mm_nt(dout, vh) for vh in vs]
            ps, dss = [], []
            for hh in range(2):
                delta = jnp.sum(dout_o * head_lanes[hh], axis=-1, keepdims=True)
                lse_h = jnp.sum(jnp.where(lane == hh * HEAD, lse_t, 0.0), axis=-1, keepdims=True)
                p = jnp.exp2(ss[hh] - lse_h)
                if masked:
                    p = jnp.where(visible(i * tq, j * tk, p.shape), p, 0.0)
                ps.append(p)
                dss.append(p * (dps[hh] - delta))
            dvs = [mm_tn(p, dout) for p in ps]
            dqs = [mm(ds, kh) for ds, kh in zip(dss, ks)]
            dks = [mm_tn(ds, qh) for ds, qh in zip(dss, qs)]
            for cols, dq in zip(heads, dqs):
                dq_ref[rows, cols] += dq * ATTN_SCALE
            return tuple((dk + a, dv + b) for (dk, dv), a, b in zip(carry, dks, dvs))

        carry = tuple((jnp.zeros((tk, LANE), F32), jnp.zeros((tk, LANE), F32)) for _ in heads)
        i_first = (j * tk) // tq
        for extra in range(n_masked):
            carry = step(i_first + extra, carry, masked=True)
        carry = lax.fori_loop(i_first + n_masked, n_q, functools.partial(step, masked=False), carry)
        for cols, (dk, dv) in zip(heads, carry):
            dk_ref[:, cols] = dk * (1.0 / LOG2_E)
            dv_ref[:, cols] = dv

        @pl.when(jnp.logical_and(pl.program_id(0) == HEADS // 2 - 1, j == n_k - 1))
        def _():
            wait_riders()

    full = lambda w: pl.BlockSpec((n_rows, w), lambda p, j: (0, p))
    blk = pl.BlockSpec((tk, 2 * LANE), lambda p, j: (j, p))
    hbm = pl.BlockSpec(memory_space=pl.ANY)
    return pl.pallas_call(
        body, name="attention_backward", grid=(HEADS // 2, n_k),
        in_specs=[full(2 * LANE), blk, blk, full(LANE), full(LANE), full(LANE)] + [hbm] * n_r,
        out_specs=[full(2 * LANE), blk, blk] + [hbm] * n_r,
        out_shape=[jax.ShapeDtypeStruct((n_rows, HEADS * LANE), F32)] * 3
        + [jax.ShapeDtypeStruct(r.shape, r.dtype) for r in riders],
        scratch_shapes=[pltpu.SemaphoreType.DMA((7 * n_r,)), pltpu.SemaphoreType.DMA((7 * n_r,)),
                        pltpu.SemaphoreType.DMA((n_r,))],
        compiler_params=pltpu.CompilerParams(dimension_semantics=("arbitrary", "arbitrary"),
                                             vmem_limit_bytes=VMEM_LIMIT),
    )(q, k, v, o, do, lse, *riders)


def tail_tile(step0, tile0, x, tgt, ma, mb, gpa, gpb, ya, y, ur, k2, uv,
              mod, wpa, wpb, wout, gn_g, gn_b, r_k, post_g, post_b, bd):
    gate = mod[2:3]
    inv = 1.0 / HEAD
    yc = y - head_sum(y, bd) * inv
    rs = lax.rsqrt(head_sum(yc * yc, bd) * inv + GN_EPS)
    yn = yc * rs
    yb = yn * gn_g + gn_b + head_sum(ur * k2 * r_k, bd) * uv
    sga, sgb = sigmoid(gpa), sigmoid(gpb)
    sila, silb = gpa * sga, gpb * sgb
    ga, gb = ya * sila, yb * silb
    pa, pb = mm(ga, wpa), mm(gb, wpb)
    sa, sb = sigmoid(ma), sigmoid(mb)
    merged = sa * pa + sb * pb
    sub = mm(merged, wout)
    z = ALPHA * x + (1.0 + gate) * sub
    zhat, rstd = layer_norm_stats(z)
    err = zhat * post_g + post_b - tgt
    loss = 0.5 * jnp.sum(rowmean(err * err), axis=0, keepdims=True) + jnp.zeros((1, LANE), F32)
    dout = err * (1.0 / D_MODEL)
    dpost_g = colsum(dout * zhat)
    dpost_b = colsum(dout)
    dz = layer_norm_bwd(dout * post_g, zhat, rstd)
    dgate = colsum(dz * sub)
    dsub = dz * (1.0 + gate)
    dwout = mm_tn(merged, dsub)
    dmerged = mm_nt(dsub, wout)
    dpa, dpb = dmerged * sa, dmerged * sb
    dma = dmerged * pa * sa * (1.0 - sa)
    dmb = dmerged * pb * sb * (1.0 - sb)
    dwpa = mm_tn(ga, dpa)
    dwpb = mm_tn(gb, dpb)
    dga = mm_nt(dpa, wpa)
    dgb = mm_nt(dpb, wpb)
    dya = dga * sila
    dgpa = dga * ya * (sga * (1.0 + gpa * (1.0 - sga)))
    dyb = dgb * silb
    dgpb = dgb * yb * (sgb * (1.0 + gpb * (1.0 - sgb)))
    dgn_g = colsum(dyb * yn)
    dgn_b = colsum(dyb)
    dyn = dyb * gn_g
    dy = rs * (dyn - head_sum(dyn, bd) * inv - yn * head_sum(dyn * yn, bd) * inv)
    return (dz, dma, dmb, dgpa, dgpb, dya, dy, dyb,
            loss, dwout, dwpa, dwpb, dgn_g, dgn_b, dpost_g, dpost_b, dgate)


def mla_prep_bwd_tile(step0, tile0, q_c, kv_c, cos, sin, dq, dk, dv, gq, gkv, wq, wqr, wkn, wv):
    qn, qh, rq = rms_norm_fwd(q_c, gq)
    kvn, kvh, rkv = rms_norm_fwd(kv_c, gkv)
    dqc = dq * tile_lanes(cos, HEADS)
    dqs = dq * tile_lanes(sin, HEADS)
    dqn = mm_nt(dqc, wq) + mm_nt(dqs, wqr)
    dkvn = mm_nt(dk, wkn) + mm_nt(dv, wv)
    dkpe = dk[:, 0:LANE]
    for h in range(1, HEADS):
        dkpe = dkpe + dk[:, h * LANE:(h + 1) * LANE]
    dkr = dkpe * (cos * key_rope_mask(cos.shape))
    dkrr = dkpe * sin

    def rms_bwd(dyv, xh, r, g):
        dyg = dyv * g
        return r * (dyg - xh * rowmean(dyg * xh)), colsum(dyv * xh)

    dq_c, dgq = rms_bwd(dqn, qh, rq, gq)
    dkv_c, dgkv = rms_bwd(dkvn, kvh, rkv, gkv)
    return (dq_c, dkv_c, dkr, dkrr,
            mm_tn(qn, dqc), mm_tn(qn, dqs), mm_tn(kvn, dk), mm_tn(kvn, dv), dgq, dgkv)


def rwkv_prep_bwd_tile(step0, tile0, r0, k0, v0, l0, drt, dat, dbt, dkt, dvv, dlw, dyb, hr, hk, hv, hl,
                       mu_r, mu_k, mu_v, mu_l, w0, a0, k_k, k_a, w_dec, w_iclr, tril, same, bd, r_k,
                       cr, ck, cv, cl_):
    f = rwkv_prep_core(tile0, r0, k0, v0, l0, hr, hk, hv, hl, mu_r, mu_k, mu_v, mu_l, w0, a0, k_k, k_a,
                       w_dec, w_iclr, tril, same, bd)
    ur, uk, uv, ul, kk, k2, a_ic, sg, th = (f[n] for n in ("ur", "uk", "uv", "ul", "kk", "k2", "a_ic", "sg", "th"))
    lc, lw = f["lc"], f["lw"]
    e_neg = jnp.exp(-lc)
    dur = drt * jnp.exp(lc)
    da = dat * jnp.exp(lc - lw)
    db = dbt * e_neg
    dk2 = dkt * e_neg
    s = head_sum(ur * k2 * r_k, bd)
    duv = dvv + dyb * s
    ds = head_sum(dyb * uv, bd)
    dur = dur + ds * k2 * r_k
    dk2 = dk2 + ds * ur * r_k
    dr_k = colsum(ds * ur * k2)
    dkk = db * a_ic - da
    da_ic = db * kk + dk2 * uk * k_a
    duk = dk2 * (1.0 + (a_ic - 1.0) * k_a)
    dk_a = colsum(dk2 * uk * (a_ic - 1.0))
    dkkraw = jnp.where(f["nrm_raw"] > 1e-12, (dkk - kk * head_sum(dkk * kk, bd)) / f["nrm"], dkk * 1e12)
    duk = duk + dkkraw * k_k
    dk_k = colsum(dkkraw * uk)
    dai = da_ic * a_ic * (1.0 - a_ic)
    dd = dlw * (-DECAY_SCALE) * sg * (1.0 - sg)
    dul = mm_nt(dai, w_iclr) + mm_nt(dd, w_dec) * (1.0 - th * th)

    def unshift(du, x, prev, mu, carry_row):
        nxt = shift_rows_up(du, carry_row)
        return du * (1.0 - mu) + nxt * mu, colsum(du * (prev - x)), du[0:1, :]

    dr0, dmu_r, ncr = unshift(dur, r0, f["pr"], mu_r, cr)
    dk0, dmu_k, nck = unshift(duk, k0, f["pk"], mu_k, ck)
    dv0, dmu_v, ncv = unshift(duv, v0, f["pv"], mu_v, cv)
    dl0, dmu_l, ncl = unshift(dul, l0, f["pl"], mu_l, cl_)
    return (dr0, dk0, dv0, dl0,
            dmu_r, dmu_k, dmu_v, dmu_l, colsum(dd), colsum(dai), dk_k, dk_a, dr_k, mm_tn(th, dd), mm_tn(ul, dai),
            ncr, nck, ncv, ncl)


def in_backward(x, dz, pieces, mod, w_in_pt, unrot):
    n_rows = x.shape[0]
    ts = ROW_TILE
    n_p = len(pieces)
    shard_cols = IN_WIDTH // N_DEV

    def body(*refs):
        x_ref, dz_ref = refs[:2]
        p_refs = refs[2:2 + n_p]
        mod_ref, w_ref, unrot_ref = refs[2 + n_p:5 + n_p]
        dx_ref, ht_ref, blocks_ref, dshift_ref, dscale_ref = refs[5 + n_p:]
        step0 = pl.program_id(0) == 0
        dma, dmb, dr0, dk0, dv0, dgpa, dgpb, dq_c, dkv_c, dkr, dkrr, dl0 = (r[...] for r in p_refs)
        dproj = jnp.concatenate([dma, dmb, dr0, dk0, dv0, dgpa, dgpb, dq_c, dkv_c, dkr, dkrr, dl0], axis=1)
        dh = mm(dproj, w_ref[...])
        xhat, rstd = layer_norm_stats(x_ref[...])
        scale1 = 1.0 + mod_ref[1:2, :]
        dx_ref[...] = layer_norm_bwd(dh * scale1, xhat, rstd) + ALPHA * dz_ref[...]
        ht_ref[...] = jnp.transpose(xhat * scale1 + mod_ref[0:1, :]).astype(BF16)
        dkrope = (dkr.astype(F32) + mm(dkrr, unrot_ref[...]))[:, NOPE:QK_DIM]
        natural = jnp.concatenate(
            [dq_c.astype(F32), dkv_c.astype(F32), dkrope]
            + [p.astype(F32) for p in (dgpa, dr0, dk0, dv0, dl0, dgpb, dma, dmb)], axis=1)
        for j in range(N_DEV):
            blocks_ref[j] = natural[:, j * shard_cols:(j + 1) * shard_cols].astype(BF16)
        for ref, val in ((dshift_ref, colsum(dh)), (dscale_ref, colsum(dh * xhat))):
            @pl.when(step0)
            def _(ref=ref, val=val):
                ref[...] = val

            @pl.when(jnp.logical_not(step0))
            def _(ref=ref, val=val):
                ref[...] += val

    row = lambda w: pl.BlockSpec((ts, w), lambda i: (i, 0))
    const = pl.BlockSpec(memory_space=pltpu.VMEM)
    vec = pl.BlockSpec((1, D_MODEL), lambda i: (0, 0))
    return pl.pallas_call(
        body, name="in_backward", grid=(n_rows // ts,),
        in_specs=[row(D_MODEL), row(D_MODEL)] + [row(p.shape[1]) for p in pieces] + [const] * 3,
        out_specs=[row(D_MODEL), pl.BlockSpec((D_MODEL, ts), lambda i: (0, i)),
                   pl.BlockSpec((N_DEV, ts, shard_cols), lambda i: (0, i, 0)), vec, vec],
        out_shape=[jax.ShapeDtypeStruct((n_rows, D_MODEL), F32), jax.ShapeDtypeStruct((D_MODEL, n_rows), BF16),
                   jax.ShapeDtypeStruct((N_DEV, n_rows, shard_cols), BF16),
                   jax.ShapeDtypeStruct((1, D_MODEL), F32), jax.ShapeDtypeStruct((1, D_MODEL), F32)],
        compiler_params=pltpu.CompilerParams(dimension_semantics=("arbitrary",), vmem_limit_bytes=VMEM_LIMIT),
    )(x, dz, *pieces, mod, w_in_pt, unrot)


def in_weight_grad_exchange(h_t, dp_blocks, others, small, order):
    n = len(others)
    n_rows = h_t.shape[1]
    ts = 4 * ROW_TILE
    n_i = n_rows // ts
    shard_cols = dp_blocks.shape[2]
    n_chips = N_DEV // 2
    last = N_DEV - 1

    def body(order_ref, h_ref, dp_ref, *rest):
        g_refs, s_ref = rest[:n], rest[n]
        rwin_ref, rg_refs, rs_ref = rest[n + 1], rest[n + 2:2 * n + 2], rest[2 * n + 2]
        (acc, sendbuf, sib_buf, sib_send, sib_recv, win_send, win_recv,
         o_send, o_recv, local_sems) = rest[2 * n + 3:]
        b, i = pl.program_id(0), pl.program_id(1)
        me = my_position()
        mi = flat_index(me)
        sibling = (me[0], me[1], 1 - me[2])

        def other_copies(k, src_index, dst_index):
            peer = flip(me, k)
            out = [pltpu.make_async_remote_copy(
                src_ref=g_refs[a].at[src_index], dst_ref=rg_refs[a].at[dst_index],
                send_sem=o_send.at[(n + 1) * (k - 1) + a], recv_sem=o_recv.at[(n + 1) * (k - 1) + a],
                device_id=peer, device_id_type=MESH_IDS) for a in range(n)]
            out.append(pltpu.make_async_remote_copy(
                src_ref=s_ref, dst_ref=rs_ref.at[dst_index],
                send_sem=o_send.at[(n + 1) * (k - 1) + n], recv_sem=o_recv.at[(n + 1) * (k - 1) + n],
                device_id=peer, device_id_type=MESH_IDS))
            return out

        def local_copies():
            out = [pltpu.make_async_copy(g_refs[a].at[mi], rg_refs[a].at[mi], local_sems.at[a]) for a in range(n)]
            out.append(pltpu.make_async_copy(s_ref, rs_ref.at[mi], local_sems.at[n]))
            return out

        def to_sibling(t):
            return pltpu.make_async_remote_copy(
                src_ref=sendbuf.at[t], dst_ref=sib_buf.at[t], send_sem=sib_send.at[t], recv_sem=sib_recv.at[t],
                device_id=sibling, device_id_type=MESH_IDS)

        def to_owner(t):
            flip_x = (t < 2) * 1
            flip_y = 1 - (t & 1)
            owner = (me[0] ^ flip_x, me[1] ^ flip_y, me[2])
            return pltpu.make_async_remote_copy(
                src_ref=sendbuf.at[n_chips + t], dst_ref=rwin_ref.at[t], send_sem=win_send.at[t],
                recv_sem=win_recv.at[t], device_id=owner, device_id_type=MESH_IDS)

        own_block = pltpu.make_async_copy(sendbuf.at[last], rwin_ref.at[n_chips - 1], local_sems.at[n + 1])

        @pl.when(jnp.logical_and(b == 0, i == 0))
        def _():
            for cp in local_copies():
                cp.start()
            for k in range(1, N_DEV):
                for cp in other_copies(k, flat_index(flip(me, k)), mi):
                    cp.start()

        contrib = jnp.dot(h_ref[...], dp_ref[...], preferred_element_type=F32)

        @pl.when(i == 0)
        def _():
            acc[...] = contrib

        @pl.when(i > 0)
        def _():
            acc[...] += contrib

        slot = order_ref[N_DEV + b]
        t = slot & (n_chips - 1)

        @pl.when(jnp.logical_and(i == n_i - 1, slot < n_chips))
        def _():
            sendbuf[slot] = acc[...].astype(BF16)
            to_sibling(t).start()

        @pl.when(jnp.logical_and(i == n_i - 1, slot >= n_chips))
        def _():
            to_sibling(t).wait_recv()
            sendbuf[slot] = (acc[...] + sib_buf[t].astype(F32)).astype(BF16)

            @pl.when(slot < last)
            def _():
                to_owner(t).start()

            @pl.when(slot == last)
            def _():
                own_block.start()

        @pl.when(jnp.logical_and(b == last, i == n_i - 1))
        def _():
            for t in range(n_chips - 1):
                to_owner(t).wait_recv()
            for k in range(1, N_DEV):
                pi = flat_index(flip(me, k))
                for cp in other_copies(k, pi, pi):
                    cp.wait_recv()
            for t in range(n_chips):
                to_sibling(t).wait_send()
            for t in range(n_chips - 1):
                to_owner(t).wait_send()
            for k in range(1, N_DEV):
                for cp in other_copies(k, flat_index(flip(me, k)), mi):
                    cp.wait_send()
            for cp in local_copies():
                cp.wait()
            own_block.wait()

    hbm = pl.BlockSpec(memory_space=pl.ANY)
    n_sem = 7 * (n + 1)
    grid_spec = pltpu.PrefetchScalarGridSpec(
        num_scalar_prefetch=1, grid=(N_DEV, n_i),
        in_specs=[pl.BlockSpec((D_MODEL, ts), lambda b, i, order: (0, i)),
                  pl.BlockSpec((None, ts, shard_cols), lambda b, i, order: (order[b], i, 0))] + [hbm] * (n + 1),
        out_specs=[hbm] * (n + 2),
        scratch_shapes=[pltpu.VMEM((D_MODEL, shard_cols), F32), pltpu.VMEM((N_DEV, D_MODEL, shard_cols), BF16),
                        pltpu.VMEM((n_chips, D_MODEL, shard_cols), BF16),
                        pltpu.SemaphoreType.DMA((n_chips,)), pltpu.SemaphoreType.DMA((n_chips,)),
                        pltpu.SemaphoreType.DMA((n_chips - 1,)), pltpu.SemaphoreType.DMA((n_chips - 1,)),
                        pltpu.SemaphoreType.DMA((n_sem,)), pltpu.SemaphoreType.DMA((n_sem,)),
                        pltpu.SemaphoreType.DMA((n + 2,))])
    return pl.pallas_call(
        body, name="in_weight_grad_exchange", grid_spec=grid_spec,
        out_shape=[jax.ShapeDtypeStruct((n_chips, D_MODEL, shard_cols), BF16)]
        + [jax.ShapeDtypeStruct(o.shape, o.dtype) for o in others]
        + [jax.ShapeDtypeStruct((N_DEV,) + small.shape, small.dtype)],
        compiler_params=pltpu.CompilerParams(dimension_semantics=("arbitrary", "arbitrary"),
                                             vmem_limit_bytes=VMEM_LIMIT),
    )(order, h_t, dp_blocks, *others, small)


def ada_weight_grad(c_all, dmod_cols):
    def body(c_ref, d_ref, o_ref):
        cv = c_ref[...]
        o_ref[...] = hdot_tn(cv * sigmoid(cv), d_ref[...])

    return pl.pallas_call(
        body, name="ada_weight_grad",
        out_shape=jax.ShapeDtypeStruct((c_all.shape[1], dmod_cols.shape[1]), F32),
    )(c_all, dmod_cols)


def adamw_update(g, w, m, v):
    nm = ADAM_B1 * m + (1.0 - ADAM_B1) * g
    nv = ADAM_B2 * v + (1.0 - ADAM_B2) * (g * g)
    m_hat = nm / (1.0 - ADAM_B1 ** ADAM_STEP)
    v_hat = nv / (1.0 - ADAM_B2 ** ADAM_STEP)
    return -ADAM_LR * (m_hat / (jnp.sqrt(v_hat) + ADAM_EPS) + ADAM_WD * w), nm, nv


def adamw(parts, w, m, v, name):
    k, rows, cols = parts.shape

    def body(p_ref, w_hbm, m_hbm, v_hbm, g_ref, d_ref, nm_ref, nv_ref, w_buf, m_buf, v_buf, sems):
        loads = [pltpu.make_async_copy(src, dst, sems.at[i])
                 for i, (src, dst) in enumerate(((w_hbm, w_buf), (m_hbm, m_buf), (v_hbm, v_buf)))]
        for cp in loads:
            cp.start()
        g = p_ref[0].astype(F32)
        for i in range(1, k):
            g = g + p_ref[i].astype(F32)
        g_ref[0] = g
        for cp in loads:
            cp.wait()
        d_ref[0], nm_ref[0], nv_ref[0] = adamw_update(g, w_buf[0], m_buf[0], v_buf[0])

    hbm = pl.BlockSpec(memory_space=pl.ANY)
    whole = pl.BlockSpec(memory_space=pltpu.VMEM)
    return pl.pallas_call(
        body, name=name,
        in_specs=[whole, hbm, hbm, hbm], out_specs=[whole] * 4,
        out_shape=[jax.ShapeDtypeStruct((1, rows, cols), F32)] * 4,
        scratch_shapes=[pltpu.VMEM((1, rows, cols), F32)] * 3 + [pltpu.SemaphoreType.DMA((3,))],
        compiler_params=pltpu.CompilerParams(vmem_limit_bytes=VMEM_LIMIT),
    )(parts, w, m, v)


def adamw_small(parts, ws, ms, vs):
    k = parts.shape[0]
    n = len(ws)
    sizes = [w.shape[1] for w in ws]

    def body(p_ref, *refs):
        ins, outs = refs[:3 * n], refs[3 * n:]
        g_all = p_ref[0]
        for i in range(1, k):
            g_all = g_all + p_ref[i]
        off = 0
        for a, size in enumerate(sizes):
            g = g_all[:, off:off + size]
            off += size
            d, nm, nv = adamw_update(g, ins[a][...], ins[n + a][...], ins[2 * n + a][...])
            for kind, val in enumerate((g, d, nm, nv)):
                outs[kind * n + a][...] = val

    return pl.pallas_call(
        body, name="adamw_small",
        out_shape=[jax.ShapeDtypeStruct((1, size), F32) for _ in range(4) for size in sizes],
    )(parts, *ws, *ms, *vs)


def columns_from_shards(g, rows, cols):
    return g.reshape(N_DEV, rows, cols).transpose(1, 0, 2).reshape(rows, N_DEV * cols)


def permute_w_in_t(wt):
    z = lambda n: jnp.zeros((n, D_MODEL), wt.dtype)
    krope = wt[N_KROPE:N_KROPE + ROPE]
    krope_rot = jnp.concatenate([-krope[ROPE // 2:], krope[:ROPE // 2]], axis=0)
    rw = N_RWKV
    return jnp.concatenate([
        wt[N_MA:N_MA + 1024], wt[N_MB:N_MB + 1024],
        wt[rw:rw + 512], wt[rw + 512:rw + 1024], wt[rw + 1024:rw + 1536],
        wt[N_GPA:N_GPA + 512], wt[N_GPB:N_GPB + 512],
        wt[N_QC:N_QC + 256], wt[N_KVC:N_KVC + 128],
        z(NOPE), krope, z(LANE - QK_DIM), z(NOPE), krope_rot, z(LANE - QK_DIM),
        wt[rw + 1536:rw + 1664]], axis=0)


def pad_heads_q(w_uq):
    w = w_uq.reshape(Q_RANK, HEADS, QK_DIM)
    zpad = jnp.zeros((Q_RANK, HEADS, LANE - QK_DIM), w.dtype)
    wq = jnp.concatenate([w, zpad], axis=2).reshape(Q_RANK, HEADS * LANE)
    pe = w[:, :, NOPE:]
    rot = jnp.concatenate([-pe[:, :, ROPE // 2:], pe[:, :, :ROPE // 2]], axis=2)
    wqr = jnp.concatenate([jnp.zeros((Q_RANK, HEADS, NOPE), w.dtype), rot, zpad], axis=2).reshape(Q_RANK, HEADS * LANE)
    return wq, wqr


def unpad_heads_q_grad(dwq, dwqr):
    a = dwq.reshape(Q_RANK, HEADS, LANE)
    r = dwqr.reshape(Q_RANK, HEADS, LANE)[:, :, NOPE:QK_DIM]
    pe = a[:, :, NOPE:QK_DIM] + jnp.concatenate([r[:, :, ROPE // 2:], -r[:, :, :ROPE // 2]], axis=2)
    return jnp.concatenate([a[:, :, :NOPE], pe], axis=2).reshape(Q_RANK, HEADS * QK_DIM)


def pad_heads_kv(w_ukv):
    w = w_ukv.reshape(KV_RANK, HEADS, 2 * HEAD)
    z = jnp.zeros((KV_RANK, HEADS, HEAD), w.dtype)
    wkn = jnp.concatenate([w[:, :, :NOPE], z], axis=2).reshape(KV_RANK, HEADS * LANE)
    val = w[:, :, NOPE:]
    odd = (jnp.arange(HEADS) % 2 == 1)[None, :, None]
    wv = jnp.concatenate([jnp.where(odd, 0, val), jnp.where(odd, val, 0)], axis=2).reshape(KV_RANK, HEADS * LANE)
    return wkn, wv


def unpad_heads_kv_grad(dwkn, dwv):
    a = dwkn.reshape(KV_RANK, HEADS, LANE)[:, :, :NOPE]
    b = dwv.reshape(KV_RANK, HEADS, LANE)
    odd = (jnp.arange(HEADS) % 2 == 1)[None, :, None]
    val = jnp.where(odd, b[:, :, HEAD:], b[:, :, :HEAD])
    return jnp.concatenate([a, val], axis=2).reshape(KV_RANK, HEADS * 2 * HEAD)


def kernel(x, c, positions, w_ada, b_ada, w_in, q_norm_g, w_uq, kv_norm_g, w_ukv, mu_rwkv, w0, w_decay_up, a0, w_iclr_up, k_k, k_a, r_k, gn_g, gn_b, w_proj_a, w_proj_b, w_out, post_g, post_b, loss_target, m_w_ada, m_b_ada, m_w_in, m_q_norm_g, m_w_uq, m_kv_norm_g, m_w_ukv, m_mu_rwkv, m_w0, m_w_decay_up, m_a0, m_w_iclr_up, m_k_k, m_k_a, m_r_k, m_gn_g, m_gn_b, m_w_proj_a, m_w_proj_b, m_w_out, m_post_g, m_post_b, v_w_ada, v_b_ada, v_w_in, v_q_norm_g, v_w_uq, v_kv_norm_g, v_w_ukv, v_mu_rwkv, v_w0, v_w_decay_up, v_a0, v_w_iclr_up, v_k_k, v_k_a, v_r_k, v_gn_g, v_gn_b, v_w_proj_a, v_w_proj_b, v_w_out, v_post_g, v_post_b):
    weights = dict(w_ada=w_ada, b_ada=b_ada, w_in=w_in, q_norm_g=q_norm_g, w_uq=w_uq, kv_norm_g=kv_norm_g,
                   w_ukv=w_ukv, mu_rwkv=mu_rwkv, w0=w0, w_decay_up=w_decay_up, a0=a0, w_iclr_up=w_iclr_up,
                   k_k=k_k, k_a=k_a, r_k=r_k, gn_g=gn_g, gn_b=gn_b, w_proj_a=w_proj_a, w_proj_b=w_proj_b,
                   w_out=w_out, post_g=post_g, post_b=post_b)
    mom1 = dict(w_ada=m_w_ada, b_ada=m_b_ada, w_in=m_w_in, q_norm_g=m_q_norm_g, w_uq=m_w_uq, kv_norm_g=m_kv_norm_g,
                w_ukv=m_w_ukv, mu_rwkv=m_mu_rwkv, w0=m_w0, w_decay_up=m_w_decay_up, a0=m_a0, w_iclr_up=m_w_iclr_up,
                k_k=m_k_k, k_a=m_k_a, r_k=m_r_k, gn_g=m_gn_g, gn_b=m_gn_b, w_proj_a=m_w_proj_a, w_proj_b=m_w_proj_b,
                w_out=m_w_out, post_g=m_post_g, post_b=m_post_b)
    mom2 = dict(w_ada=v_w_ada, b_ada=v_b_ada, w_in=v_w_in, q_norm_g=v_q_norm_g, w_uq=v_w_uq, kv_norm_g=v_kv_norm_g,
                w_ukv=v_w_ukv, mu_rwkv=v_mu_rwkv, w0=v_w0, w_decay_up=v_w_decay_up, a0=v_a0, w_iclr_up=v_w_iclr_up,
                k_k=v_k_k, k_a=v_k_a, r_k=v_r_k, gn_g=v_gn_g, gn_b=v_gn_b, w_proj_a=v_w_proj_a, w_proj_b=v_w_proj_b,
                w_out=v_w_out, post_g=v_post_g, post_b=v_post_b)
    names = list(weights)
    n_rows = x.shape[1]
    me = 4 * lax.axis_index("x") + 2 * lax.axis_index("y") + lax.axis_index("c")
    xr = x[0]
    tgt = loss_target[0]
    row = lambda a: a.reshape(1, -1)

    w_in_all, c_all = gather_shards([w_in[0].T.astype(BF16), c])
    c_all = c_all.reshape(N_DEV, D_MODEL)
    w_in_pt = permute_w_in_t(w_in_all.reshape(IN_WIDTH, D_MODEL))

    mod_all = ada_modulation(c_all, w_ada[0], b_ada.reshape(N_DEV, -1))
    mod = lax.dynamic_index_in_dim(mod_all, me, axis=1, keepdims=False).reshape(3, D_MODEL)

    def full_weights(entries, gathered):
        return {n: part.reshape(N_DEV * r, cdim) if n == "w_out" else columns_from_shards(part, r, cdim)
                for (n, r, cdim), part in zip(entries, gathered)}

    prep_shards, tail_shards = SHARDED[1:5], SHARDED[5:]
    proj, *gathered = fwd_in_gather(xr, mod, w_in_pt, [weights[n][0].astype(BF16) for n, _, _ in prep_shards])
    pcol = lambda off_, w: (proj, w, off_ // w)
    full = full_weights(prep_shards, gathered)
    wq, wqr = pad_heads_q(full["w_uq"])
    wkn, wv = pad_heads_kv(full["w_ukv"])
    zl = jnp.zeros((LORA, WIDTH), BF16)
    w_dec = jnp.concatenate([full["w_decay_up"], zl], axis=0)
    w_iclr = jnp.concatenate([zl, full["w_iclr_up"]], axis=0)

    inv_freq = ROPE_THETA ** (-jnp.arange(0, ROPE, 2, dtype=F32) / ROPE)
    ang = positions[0].astype(F32)[:, None] * inv_freq
    ones_n, zeros_n, zeros_p = jnp.ones((n_rows, NOPE), F32), jnp.zeros((n_rows, NOPE), F32), jnp.zeros((n_rows, LANE - QK_DIM), F32)
    cos_t = jnp.concatenate([ones_n, jnp.cos(ang), jnp.cos(ang), zeros_p], axis=1)
    sin_t = jnp.concatenate([zeros_n, jnp.sin(ang), jnp.sin(ang), zeros_p], axis=1)

    gq, gkv = q_norm_g, kv_norm_g
    mla_consts = [gq, gkv, wq, wqr, wkn, wv]
    q, k, v = row_call(
        "mla_prep", mla_prep_tile, n_rows,
        [pcol(P_QC, 256), pcol(P_KVC, 128), pcol(P_KR, 128), pcol(P_KRR, 128), (cos_t, LANE, 0), (sin_t, LANE, 0)],
        mla_consts, [(HEADS * LANE, BF16)] * 3, tile_rows=PREP_TILE)
    ya, lse, *gathered = attention_forward(q, k, v, [weights[n][0].astype(BF16) for n, _, _ in tail_shards])
    full = full_weights(tail_shards, gathered)
    wpa, wpb, wout = full["w_proj_a"], full["w_proj_b"], full["w_out"]

    def chunk_sum_matrices(n):
        t_idx = jnp.arange(n)
        same_chunk = (t_idx[:, None] // CHUNK) == (t_idx[None, :] // CHUNK)
        return (same_chunk & (t_idx[:, None] >= t_idx[None, :])).astype(F32), same_chunk.astype(F32)

    l_idx = jnp.arange(LANE)
    bd = ((l_idx[:, None] // HEAD) == (l_idx[None, :] // HEAD)).astype(F32)
    mu = mu_rwkv
    mu_r, mu_k, mu_v, mu_l = mu[:, 0:512], mu[:, 512:1024], mu[:, 1024:1536], mu[:, 1536:1664]
    rk_row = row(r_k)
    rwkv_consts = lambda n: [mu_r, mu_k, mu_v, mu_l, w0, a0, k_k, k_a, w_dec, w_iclr, *chunk_sum_matrices(n), bd]
    rwkv_rows = [pcol(P_R, 512), pcol(P_K, 512), pcol(P_V, 512), pcol(P_LORA, 128)]
    rt, at, bt, kt, clf, uv, ur, k2 = row_call(
        "rwkv_prep", rwkv_prep_tile, n_rows, rwkv_rows, rwkv_consts(ROW_TILE), [(WIDTH, F32)] * 8, halo_in=rwkv_rows)
    y, m0s, state_maps, out_maps, *wkv_saved = wkv_forward(at, bt, kt, rt, uv, clf)

    tail = row_call(
        "tail", tail_tile, n_rows,
        [(xr, D_MODEL, 0), (tgt, D_MODEL, 0), pcol(P_MA, 1024), pcol(P_MB, 1024), pcol(P_GPA, 512), pcol(P_GPB, 512),
         (ya, WIDTH, 0), (y, WIDTH, 0), (ur, WIDTH, 0), (k2, WIDTH, 0), (uv, WIDTH, 0)],
        [mod, wpa, wpb, wout, gn_g, gn_b, rk_row, post_g, post_b, bd],
        [(D_MODEL, F32), (1024, BF16), (1024, BF16), (512, BF16), (512, BF16), (WIDTH, F32), (WIDTH, F32), (WIDTH, F32)],
        acc_out=[((1, LANE), F32), ((D_MODEL, D_MODEL), F32), ((WIDTH, D_MODEL), F32), ((WIDTH, D_MODEL), F32),
                 ((1, WIDTH), F32), ((1, WIDTH), F32), ((1, D_MODEL), F32), ((1, D_MODEL), F32), ((1, D_MODEL), F32)])
    (dz, dma, dmb, dgpa, dgpb, dya, dy, dyb,
     loss_row, g_wout, g_wpa, g_wpb, g_gn_g, g_gn_b, g_post_g, g_post_b, dgate) = tail

    def owner_blocks(g, n):
        r, cdim = next((r, cdim) for name, r, cdim in SHARDED if name == n)
        return (g.reshape(N_DEV, r, cdim) if n == "w_out" else g.reshape(r, N_DEV, cdim).transpose(1, 0, 2)).astype(BF16)

    early = ("w_proj_a", "w_proj_b", "w_out")
    dq, dk, dv, *got_early = attention_backward(
        q, k, v, ya, dya, lse, [owner_blocks(g, n) for g, n in zip((g_wpa, g_wpb, g_wout), early)])
    dq_c, dkv_c, dkr, dkrr, g_wq, g_wqr, g_wkn, g_wv, g_gq, g_gkv = row_call(
        "mla_prep_bwd", mla_prep_bwd_tile, n_rows,
        [pcol(P_QC, 256), pcol(P_KVC, 128), (cos_t, LANE, 0), (sin_t, LANE, 0),
         (dq, HEADS * LANE, 0), (dk, HEADS * LANE, 0), (dv, HEADS * LANE, 0)],
        mla_consts, [(256, BF16), (128, BF16), (128, BF16), (128, BF16)],
        acc_out=[((Q_RANK, HEADS * LANE), F32)] * 2 + [((KV_RANK, HEADS * LANE), F32)] * 2
        + [((1, Q_RANK), F32), ((1, KV_RANK), F32)], tile_rows=PREP_TILE)

    dat, dbt, dkt, drt, dvv, dlw = wkv_backward(at, bt, kt, rt, uv, clf, m0s, state_maps, out_maps, wkv_saved, dy)
    (dr0, dk0, dv0, dl0, g_mu_r, g_mu_k, g_mu_v, g_mu_l, g_w0, g_a0, g_k_k, g_k_a, g_r_k, g_wdec, g_wiclr) = row_call(
        "rwkv_prep_bwd", rwkv_prep_bwd_tile, n_rows,
        rwkv_rows + [(drt, WIDTH, 0), (dat, WIDTH, 0), (dbt, WIDTH, 0), (dkt, WIDTH, 0), (dvv, WIDTH, 0),
                     (dlw, WIDTH, 0), (dyb, WIDTH, 0)],
        rwkv_consts(PREP_TILE) + [rk_row], [(512, BF16), (512, BF16), (512, BF16), (128, BF16)],
        acc_out=[((1, 512), F32)] * 3 + [((1, 128), F32)] + [((1, 512), F32)] * 5 + [((LANE, WIDTH), F32)] * 2,
        halo_in=rwkv_rows, carry=[512, 512, 512, 128], reverse=True, tile_rows=PREP_TILE)

    li = jnp.arange(LANE)
    src, dst = li[:, None], li[None, :]
    half = ROPE // 2
    unrot = (jnp.where((dst >= NOPE) & (dst < NOPE + half) & (src == dst + half), 1.0, 0.0)
             - jnp.where((dst >= NOPE + half) & (dst < QK_DIM) & (src == dst - half), 1.0, 0.0)).astype(BF16)
    dx, h_t, dproj_blocks, dshift, dscale = in_backward(
        xr, dz, [dma, dmb, dr0, dk0, dv0, dgpa, dgpb, dq_c, dkv_c, dkr, dkrr, dl0], mod, w_in_pt, unrot)

    late = ("w_uq", "w_ukv", "w_decay_up", "w_iclr_up")
    late_grads = (unpad_heads_q_grad(g_wq, g_wqr), unpad_heads_kv_grad(g_wkn, g_wv), g_wdec[:LORA], g_wiclr[LORA:])
    blocks = [owner_blocks(g, n) for g, n in zip(late_grads, late)]
    dmod = jnp.concatenate([dshift, dscale, dgate], axis=1)
    small = jnp.concatenate([dmod, g_gq, g_gkv, g_mu_r, g_mu_k, g_mu_v, g_mu_l, g_w0, g_a0, g_k_k, g_k_a, g_r_k,
                             g_gn_g, g_gn_b, g_post_g, g_post_b, loss_row], axis=1)
    my_x, my_y, my_c = lax.axis_index("x"), lax.axis_index("y"), lax.axis_index("c")
    chip_order = [4 * (my_x ^ fx) + 2 * (my_y ^ fy) for fx, fy in ((1, 1), (1, 0), (0, 1), (0, 0))]
    owners = [chip_order[s % 4] + (my_c if s >= 4 else 1 - my_c) for s in WGRAD_SLOTS]
    order = jnp.stack(owners + [jnp.int32(s) for s in WGRAD_SLOTS]).astype(jnp.int32)
    got_w_in, *got_late, got_small = in_weight_grad_exchange(h_t, dproj_blocks, blocks, small, order)
    got = {"w_in": got_w_in, **dict(zip(late, got_late)), **dict(zip(early, got_early))}
    loss = jnp.sum(got_small[:, 0, SMALL_ELEMS])

    ada_cols = w_ada.shape[2]
    dmod_all = got_small[:, 0, :3 * D_MODEL]
    g_ada = ada_weight_grad(c_all, lax.dynamic_slice_in_dim(dmod_all, me * ada_cols, ada_cols, axis=1))

    outs = [dict() for _ in range(4)]
    res = adamw(g_ada[None], w_ada, m_w_ada, v_w_ada, "adamw_w_ada")
    for kind in range(4):
        outs[kind]["w_ada"] = res[kind]
    for n, _, _ in SHARDED:
        res = adamw(got[n], weights[n], mom1[n], mom2[n], "adamw_" + n)
        for kind in range(4):
            outs[kind][n] = res[kind]
    rows_of = lambda tree: [tree[n].reshape(1, -1) for n, _ in SMALL]
    res = adamw_small(got_small, rows_of(weights), rows_of(mom1), rows_of(mom2))
    for kind in range(4):
        for a, (n, _) in enumerate(SMALL):
            outs[kind][n] = res[kind * len(SMALL) + a].reshape(weights[n].shape)
    return (loss, dx[None], *[outs[0][n] for n in names], *[outs[1][n] for n in names],
            *[outs[2][n] for n in names], *[outs[3][n] for n in names])
```
